```python
import jax
import jax.numpy as jnp
from jax import lax
import numpy as np

D_MODEL = 1024
BATCH = 8
SEQ = 8192
DEPTH = 2

HEAD_DIM = 64
ROT_DIM = HEAD_DIM // 4
ROPE_THETA = 500000.0
BLOCK = 128
RMS_EPS = 1e-6
LN_EPS = 1e-5

A_HEADS = (D_MODEL // 2) // HEAD_DIM
A_KV_HEADS = 2
A_WINDOW = 128
CONV_CH = D_MODEL // 2
CONV_WIDTH = 31
C_HEADS = (D_MODEL // 2) // HEAD_DIM
DILATED_PAIRS = ((128, 1), (512, 4), (2048, 16))
D_CH = D_MODEL // 2
D_GROUPS = D_CH // HEAD_DIM
CHUNK = 128
D_FF = ((8 * D_MODEL // 3 + 255) // 256) * 256

A_Q = A_HEADS * HEAD_DIM
A_KV = A_KV_HEADS * HEAD_DIM
EVEN_IN = A_Q + 2 * A_KV + 2 * CONV_CH
EVEN_OUT = A_Q + CONV_CH
C_W = C_HEADS * HEAD_DIM
ODD_IN = 3 * C_W + 2 * D_CH
ODD_OUT = C_W + D_CH
N_EVEN = (DEPTH + 1) // 2
N_ODD = DEPTH // 2

kernel_name = 'hybrid_swa_sink_conformer_dilated_gmlp'


def rms_norm(x, g):
    xf = x.astype(jnp.float32)
    y = xf * lax.rsqrt(jnp.mean(xf * xf, axis=-1, keepdims=True) + RMS_EPS)
    return (y * g.astype(jnp.float32)).astype(x.dtype)


def layer_norm(x, g, b):
    xf = x.astype(jnp.float32)
    mu = jnp.mean(xf, axis=-1, keepdims=True)
    xc = xf - mu
    var = jnp.mean(xc * xc, axis=-1, keepdims=True)
    y = xc * lax.rsqrt(var + LN_EPS) * g.astype(jnp.float32) + b.astype(jnp.float32)
    return y.astype(x.dtype)


def rotary(x, pos):
    half = ROT_DIM // 2
    inv_freq = ROPE_THETA ** (-jnp.arange(half, dtype=jnp.float32) * (2.0 / ROT_DIM))
    ang = pos.astype(jnp.float32)[:, None] * inv_freq[None, :]
    cos = jnp.cos(ang)[None, :, None, :]
    sin = jnp.sin(ang)[None, :, None, :]
    xr = x[..., :ROT_DIM].astype(jnp.float32)
    x1, x2 = xr[..., :half], xr[..., half:]
    rot = jnp.concatenate([x1 * cos - x2 * sin, x2 * cos + x1 * sin], axis=-1).astype(x.dtype)
    return jnp.concatenate([rot, x[..., ROT_DIM:]], axis=-1)


def band_attention(q, k, v, max_dist, sink=None):
    B, S, Hq, hd = q.shape
    Hkv = k.shape[2]
    G = Hq // Hkv
    n = S // BLOCK
    qb = q.reshape(B, n, BLOCK, Hkv, G, hd)
    kb = k.reshape(B, n, BLOCK, Hkv, hd)
    vb = v.reshape(B, n, BLOCK, Hkv, hd)
    prev = lambda t: jnp.pad(t, ((0, 0), (1, 0), (0, 0), (0, 0), (0, 0)))[:, :-1]
    kk = jnp.concatenate([prev(kb), kb], axis=2)
    vv = jnp.concatenate([prev(vb), vb], axis=2)
    s = jnp.einsum('bnqhgd,bnjhd->bnhgqj', qb, kk,
                   preferred_element_type=jnp.float32) * (hd ** -0.5)
    qi = jnp.arange(BLOCK)[:, None]
    kj = jnp.arange(2 * BLOCK)[None, :]
    dist = qi + BLOCK - kj
    key_pos = jnp.arange(n)[:, None, None] * BLOCK + kj[None] - BLOCK
    valid = (dist >= 0)[None] & (dist <= max_dist)[None] & (key_pos >= 0)
    s = jnp.where(valid[None, :, None, None], s, -jnp.inf)
    m = jnp.max(s, axis=-1)
    if sink is not None:
        sink_b = sink.astype(jnp.float32).reshape(Hkv, G)[None, None, :, :, None]
        m = jnp.maximum(m, sink_b)
    p = jnp.exp(s - m[..., None])
    l = jnp.sum(p, axis=-1)
    if sink is not None:
        l = l + jnp.exp(sink_b - m)
    o = jnp.einsum('bnhgqj,bnjhd->bnqhgd', p.astype(v.dtype), vv,
                   preferred_element_type=jnp.float32)
    o = o / jnp.transpose(l, (0, 1, 4, 2, 3))[..., None]
    lse = jnp.transpose(m + jnp.log(l), (0, 1, 4, 2, 3)).reshape(B, S, Hq)
    return o.reshape(B, S, Hq, hd).astype(q.dtype), lse


def dilated_window_attention(q, k, v, window, dilation):
    B, S, H, hd = q.shape
    span = dilation * BLOCK
    s_pad = -(-S // span) * span
    sub = s_pad // dilation

    def fold(t):
        t = jnp.pad(t, ((0, 0), (0, s_pad - S), (0, 0), (0, 0)))
        t = t.reshape(B, sub, dilation, t.shape[2], hd)
        return jnp.transpose(t, (0, 2, 1, 3, 4)).reshape(B * dilation, sub, t.shape[3], hd)

    o, lse = band_attention(fold(q), fold(k), fold(v), window // dilation)
    o = jnp.transpose(o.reshape(B, dilation, sub, H, hd), (0, 2, 1, 3, 4)).reshape(B, s_pad, H, hd)
    lse = jnp.transpose(lse.reshape(B, dilation, sub, H), (0, 2, 1, 3)).reshape(B, s_pad, H)
    return o[:, :S], lse[:, :S]


def causal_depthwise_conv(x, w, b):
    C = x.shape[-1]
    y = lax.conv_general_dilated(
        x, w[:, None, :].astype(x.dtype), window_strides=(1,),
        padding=[(CONV_WIDTH - 1, 0)], dimension_numbers=('NWC', 'WIO', 'NWC'),
        feature_group_count=C)
    return y + b.astype(x.dtype)


def even_mixer(h, w_in, sinks, conv_w, conv_b, ln_g, ln_b, w_out, pos):
    B, S, _ = h.shape
    proj = jnp.einsum('bsd,de->bse', h, w_in)
    q, k, v, glu = jnp.split(proj, [A_Q, A_Q + A_KV, A_Q + 2 * A_KV], axis=-1)
    q = rotary(q.reshape(B, S, A_HEADS, HEAD_DIM), pos)
    k = rotary(k.reshape(B, S, A_KV_HEADS, HEAD_DIM), pos)
    v = v.reshape(B, S, A_KV_HEADS, HEAD_DIM)
    a, _ = band_attention(q, k, v, A_WINDOW - 1, sinks)
    a = a.reshape(B, S, A_Q)
    g_a, g_b = jnp.split(glu, 2, axis=-1)
    c = g_a * jax.nn.sigmoid(g_b)
    c = causal_depthwise_conv(c, conv_w, conv_b)
    c = jax.nn.silu(layer_norm(c, ln_g, ln_b))
    return jnp.einsum('bse,ed->bsd', jnp.concatenate([a, c], axis=-1), w_out)


def odd_mixer(h, w_in, sgu_ln_g, sgu_ln_b, spatial_w, spatial_b, w_out, pos):
    B, S, _ = h.shape
    proj = jnp.einsum('bsd,de->bse', h, w_in)
    q, k, v, z = jnp.split(proj, [C_W, 2 * C_W, 3 * C_W], axis=-1)
    q = rotary(q.reshape(B, S, C_HEADS, HEAD_DIM), pos)
    k = rotary(k.reshape(B, S, C_HEADS, HEAD_DIM), pos)
    v = v.reshape(B, S, C_HEADS, HEAD_DIM)
    outs, lses = [], []
    for window, dilation in DILATED_PAIRS:
        o_r, lse_r = dilated_window_attention(q, k, v, window, dilation)
        outs.append(o_r)
        lses.append(lse_r)
    alpha = jax.nn.softmax(jnp.stack(lses, axis=0), axis=0)
    c_out = jnp.einsum('rbsh,rbshd->bshd', alpha, jnp.stack(outs, axis=0).astype(jnp.float32))
    c_out = c_out.astype(h.dtype).reshape(B, S, C_W)
    z = jax.nn.gelu(z)
    u, g = jnp.split(z, 2, axis=-1)
    g = layer_norm(g, sgu_ln_g, sgu_ln_b).reshape(B, S // CHUNK, CHUNK, D_GROUPS, HEAD_DIM)
    causal = jnp.tril(jnp.ones((CHUNK, CHUNK), dtype=bool))
    w_s = jnp.where(causal[None], spatial_w, 0).astype(g.dtype)
    mixed = jnp.einsum('gts,bcsgd->bctgd', w_s, g) + spatial_b.T.astype(g.dtype)[None, None, :, :, None]
    d_out = u * mixed.reshape(B, S, D_CH)
    return jnp.einsum('bse,ed->bsd', jnp.concatenate([c_out, d_out], axis=-1), w_out)


def swiglu(h, w_gate, w_up, w_down):
    gate = jnp.einsum('bsd,df->bsf', h, w_gate)
    up = jnp.einsum('bsd,df->bsf', h, w_up)
    return jnp.einsum('bsf,fd->bsd', jax.nn.silu(gate) * up, w_down)


def _fwd_setup_inputs(seed: int = 0) -> dict:
    key = jax.random.key(seed)
    ks = jax.random.split(key, 21)
    f32 = jnp.float32

    def nrm(k, shape, scale):
        return jax.random.normal(k, shape, f32) * scale

    return {
        'x': nrm(ks[0], (BATCH, SEQ, D_MODEL), 1.0),
        'ev_norm_g': 1.0 + nrm(ks[1], (N_EVEN, D_MODEL), 0.02),
        'ev_w_in': nrm(ks[2], (N_EVEN, D_MODEL, EVEN_IN), D_MODEL ** -0.5),
        'ev_sinks': nrm(ks[3], (N_EVEN, A_HEADS), 0.5),
        'ev_conv_w': nrm(ks[4], (N_EVEN, CONV_WIDTH, CONV_CH), CONV_WIDTH ** -0.5),
        'ev_conv_b': nrm(ks[5], (N_EVEN, CONV_CH), 0.02),
        'ev_conv_ln_g': 1.0 + nrm(ks[6], (N_EVEN, CONV_CH), 0.02),
        'ev_conv_ln_b': nrm(ks[7], (N_EVEN, CONV_CH), 0.02),
        'ev_w_out': nrm(ks[8], (N_EVEN, EVEN_OUT, D_MODEL), EVEN_OUT ** -0.5),
        'od_norm_g': 1.0 + nrm(ks[9], (N_ODD, D_MODEL), 0.02),
        'od_w_in': nrm(ks[10], (N_ODD, D_MODEL, ODD_IN), D_MODEL ** -0.5),
        'od_sgu_ln_g': 1.0 + nrm(ks[11], (N_ODD, D_CH), 0.02),
        'od_sgu_ln_b': nrm(ks[12], (N_ODD, D_CH), 0.02),
        'od_spatial_w': nrm(ks[13], (N_ODD, D_GROUPS, CHUNK, CHUNK), CHUNK ** -0.5),
        'od_spatial_b': 1.0 + nrm(ks[14], (N_ODD, D_GROUPS, CHUNK), 0.02),
        'od_w_out': nrm(ks[15], (N_ODD, ODD_OUT, D_MODEL), ODD_OUT ** -0.5),
        'ffn_norm_g': 1.0 + nrm(ks[16], (DEPTH, D_MODEL), 0.02),
        'ffn_w_gate': nrm(ks[17], (DEPTH, D_MODEL, D_FF), D_MODEL ** -0.5),
        'ffn_w_up': nrm(ks[18], (DEPTH, D_MODEL, D_FF), D_MODEL ** -0.5),
        'ffn_w_down': nrm(ks[19], (DEPTH, D_FF, D_MODEL), D_FF ** -0.5),
        'final_norm_g': 1.0 + nrm(ks[20], (D_MODEL,), 0.02),
    }


def _fwd_reference(x, ev_norm_g, ev_w_in, ev_sinks, ev_conv_w, ev_conv_b, ev_conv_ln_g,
              ev_conv_ln_b, ev_w_out, od_norm_g, od_w_in, od_sgu_ln_g, od_sgu_ln_b,
              od_spatial_w, od_spatial_b, od_w_out, ffn_norm_g, ffn_w_gate, ffn_w_up,
              ffn_w_down, final_norm_g):
    pos = jnp.arange(x.shape[1], dtype=jnp.int32)
    h = x
    for layer in range(DEPTH):
        i = layer // 2
        if layer % 2 == 0:
            h = h + even_mixer(rms_norm(h, ev_norm_g[i]), ev_w_in[i], ev_sinks[i],
                               ev_conv_w[i], ev_conv_b[i], ev_conv_ln_g[i],
                               ev_conv_ln_b[i], ev_w_out[i], pos)
        else:
            h = h + odd_mixer(rms_norm(h, od_norm_g[i]), od_w_in[i], od_sgu_ln_g[i],
                              od_sgu_ln_b[i], od_spatial_w[i], od_spatial_b[i],
                              od_w_out[i], pos)
        h = h + swiglu(rms_norm(h, ffn_norm_g[layer]), ffn_w_gate[layer],
                       ffn_w_up[layer], ffn_w_down[layer])
    return rms_norm(h, final_norm_g)


import jax as _jax
import jax.numpy as _jnp

TWIN_FORMAT = 'train_step'
FWD_PARAMS = ['x', 'ev_norm_g', 'ev_w_in', 'ev_sinks', 'ev_conv_w', 'ev_conv_b', 'ev_conv_ln_g', 'ev_conv_ln_b', 'ev_w_out', 'od_norm_g', 'od_w_in', 'od_sgu_ln_g', 'od_sgu_ln_b', 'od_spatial_w', 'od_spatial_b', 'od_w_out', 'ffn_norm_g', 'ffn_w_gate', 'ffn_w_up', 'ffn_w_down', 'final_norm_g']
TWIN_WEIGHTS = ['ev_norm_g', 'ev_w_in', 'ev_sinks', 'ev_conv_w', 'ev_conv_b', 'ev_conv_ln_g', 'ev_conv_ln_b', 'ev_w_out', 'od_norm_g', 'od_w_in', 'od_sgu_ln_g', 'od_sgu_ln_b', 'od_spatial_w', 'od_spatial_b', 'od_w_out', 'ffn_norm_g', 'ffn_w_gate', 'ffn_w_up', 'ffn_w_down', 'final_norm_g']
TWIN_DIFF_INPUT = 'x'
TWIN_INPUTS = ['x', 'ev_norm_g', 'ev_w_in', 'ev_sinks', 'ev_conv_w', 'ev_conv_b', 'ev_conv_ln_g', 'ev_conv_ln_b', 'ev_w_out', 'od_norm_g', 'od_w_in', 'od_sgu_ln_g', 'od_sgu_ln_b', 'od_spatial_w', 'od_spatial_b', 'od_w_out', 'ffn_norm_g', 'ffn_w_gate', 'ffn_w_up', 'ffn_w_down', 'final_norm_g', 'loss_target', 'm_ev_norm_g', 'm_ev_w_in', 'm_ev_sinks', 'm_ev_conv_w', 'm_ev_conv_b', 'm_ev_conv_ln_g', 'm_ev_conv_ln_b', 'm_ev_w_out', 'm_od_norm_g', 'm_od_w_in', 'm_od_sgu_ln_g', 'm_od_sgu_ln_b', 'm_od_spatial_w', 'm_od_spatial_b', 'm_od_w_out', 'm_ffn_norm_g', 'm_ffn_w_gate', 'm_ffn_w_up', 'm_ffn_w_down', 'm_final_norm_g', 'v_ev_norm_g', 'v_ev_w_in', 'v_ev_sinks', 'v_ev_conv_w', 'v_ev_conv_b', 'v_ev_conv_ln_g', 'v_ev_conv_ln_b', 'v_ev_w_out', 'v_od_norm_g', 'v_od_w_in', 'v_od_sgu_ln_g', 'v_od_sgu_ln_b', 'v_od_spatial_w', 'v_od_spatial_b', 'v_od_w_out', 'v_ffn_norm_g', 'v_ffn_w_gate', 'v_ffn_w_up', 'v_ffn_w_down', 'v_final_norm_g']
TWIN_OUTPUTS = ['loss', 'grad_x', 'grad_ev_norm_g', 'grad_ev_w_in', 'grad_ev_sinks', 'grad_ev_conv_w', 'grad_ev_conv_b', 'grad_ev_conv_ln_g', 'grad_ev_conv_ln_b', 'grad_ev_w_out', 'grad_od_norm_g', 'grad_od_w_in', 'grad_od_sgu_ln_g', 'grad_od_sgu_ln_b', 'grad_od_spatial_w', 'grad_od_spatial_b', 'grad_od_w_out', 'grad_ffn_norm_g', 'grad_ffn_w_gate', 'grad_ffn_w_up', 'grad_ffn_w_down', 'grad_final_norm_g', 'delta_ev_norm_g', 'delta_ev_w_in', 'delta_ev_sinks', 'delta_ev_conv_w', 'delta_ev_conv_b', 'delta_ev_conv_ln_g', 'delta_ev_conv_ln_b', 'delta_ev_w_out', 'delta_od_norm_g', 'delta_od_w_in', 'delta_od_sgu_ln_g', 'delta_od_sgu_ln_b', 'delta_od_spatial_w', 'delta_od_spatial_b', 'delta_od_w_out', 'delta_ffn_norm_g', 'delta_ffn_w_gate', 'delta_ffn_w_up', 'delta_ffn_w_down', 'delta_final_norm_g', 'new_m_ev_norm_g', 'new_m_ev_w_in', 'new_m_ev_sinks', 'new_m_ev_conv_w', 'new_m_ev_conv_b', 'new_m_ev_conv_ln_g', 'new_m_ev_conv_ln_b', 'new_m_ev_w_out', 'new_m_od_norm_g', 'new_m_od_w_in', 'new_m_od_sgu_ln_g', 'new_m_od_sgu_ln_b', 'new_m_od_spatial_w', 'new_m_od_spatial_b', 'new_m_od_w_out', 'new_m_ffn_norm_g', 'new_m_ffn_w_gate', 'new_m_ffn_w_up', 'new_m_ffn_w_down', 'new_m_final_norm_g', 'new_v_ev_norm_g', 'new_v_ev_w_in', 'new_v_ev_sinks', 'new_v_ev_conv_w', 'new_v_ev_conv_b', 'new_v_ev_conv_ln_g', 'new_v_ev_conv_ln_b', 'new_v_ev_w_out', 'new_v_od_norm_g', 'new_v_od_w_in', 'new_v_od_sgu_ln_g', 'new_v_od_sgu_ln_b', 'new_v_od_spatial_w', 'new_v_od_spatial_b', 'new_v_od_w_out', 'new_v_ffn_norm_g', 'new_v_ffn_w_gate', 'new_v_ffn_w_up', 'new_v_ffn_w_down', 'new_v_final_norm_g']
TWIN_LEAF_KINDS = {'loss': 'loss', 'grad_x': 'grad_x', 'grad_ev_norm_g': 'grad_w', 'grad_ev_w_in': 'grad_w', 'grad_ev_sinks': 'grad_w', 'grad_ev_conv_w': 'grad_w', 'grad_ev_conv_b': 'grad_w', 'grad_ev_conv_ln_g': 'grad_w', 'grad_ev_conv_ln_b': 'grad_w', 'grad_ev_w_out': 'grad_w', 'grad_od_norm_g': 'grad_w', 'grad_od_w_in': 'grad_w', 'grad_od_sgu_ln_g': 'grad_w', 'grad_od_sgu_ln_b': 'grad_w', 'grad_od_spatial_w': 'grad_w', 'grad_od_spatial_b': 'grad_w', 'grad_od_w_out': 'grad_w', 'grad_ffn_norm_g': 'grad_w', 'grad_ffn_w_gate': 'grad_w', 'grad_ffn_w_up': 'grad_w', 'grad_ffn_w_down': 'grad_w', 'grad_final_norm_g': 'grad_w', 'delta_ev_norm_g': 'delta_w', 'delta_ev_w_in': 'delta_w', 'delta_ev_sinks': 'delta_w', 'delta_ev_conv_w': 'delta_w', 'delta_ev_conv_b': 'delta_w', 'delta_ev_conv_ln_g': 'delta_w', 'delta_ev_conv_ln_b': 'delta_w', 'delta_ev_w_out': 'delta_w', 'delta_od_norm_g': 'delta_w', 'delta_od_w_in': 'delta_w', 'delta_od_sgu_ln_g': 'delta_w', 'delta_od_sgu_ln_b': 'delta_w', 'delta_od_spatial_w': 'delta_w', 'delta_od_spatial_b': 'delta_w', 'delta_od_w_out': 'delta_w', 'delta_ffn_norm_g': 'delta_w', 'delta_ffn_w_gate': 'delta_w', 'delta_ffn_w_up': 'delta_w', 'delta_ffn_w_down': 'delta_w', 'delta_final_norm_g': 'delta_w', 'new_m_ev_norm_g': 'new_m', 'new_m_ev_w_in': 'new_m', 'new_m_ev_sinks': 'new_m', 'new_m_ev_conv_w': 'new_m', 'new_m_ev_conv_b': 'new_m', 'new_m_ev_conv_ln_g': 'new_m', 'new_m_ev_conv_ln_b': 'new_m', 'new_m_ev_w_out': 'new_m', 'new_m_od_norm_g': 'new_m', 'new_m_od_w_in': 'new_m', 'new_m_od_sgu_ln_g': 'new_m', 'new_m_od_sgu_ln_b': 'new_m', 'new_m_od_spatial_w': 'new_m', 'new_m_od_spatial_b': 'new_m', 'new_m_od_w_out': 'new_m', 'new_m_ffn_norm_g': 'new_m', 'new_m_ffn_w_gate': 'new_m', 'new_m_ffn_w_up': 'new_m', 'new_m_ffn_w_down': 'new_m', 'new_m_final_norm_g': 'new_m', 'new_v_ev_norm_g': 'new_v', 'new_v_ev_w_in': 'new_v', 'new_v_ev_sinks': 'new_v', 'new_v_ev_conv_w': 'new_v', 'new_v_ev_conv_b': 'new_v', 'new_v_ev_conv_ln_g': 'new_v', 'new_v_ev_conv_ln_b': 'new_v', 'new_v_ev_w_out': 'new_v', 'new_v_od_norm_g': 'new_v', 'new_v_od_w_in': 'new_v', 'new_v_od_sgu_ln_g': 'new_v', 'new_v_od_sgu_ln_b': 'new_v', 'new_v_od_spatial_w': 'new_v', 'new_v_od_spatial_b': 'new_v', 'new_v_od_w_out': 'new_v', 'new_v_ffn_norm_g': 'new_v', 'new_v_ffn_w_gate': 'new_v', 'new_v_ffn_w_up': 'new_v', 'new_v_ffn_w_down': 'new_v', 'new_v_final_norm_g': 'new_v'}


def _forward(args):
    return _fwd_reference(*[args[k] for k in FWD_PARAMS])


def _output_shape():
    def fwd():
        inp = _fwd_setup_inputs(0)
        return _fwd_reference(*[inp[k] for k in FWD_PARAMS])
    out = _jax.eval_shape(fwd)
    return out.shape, out.dtype

N_MICROBATCH = 1
ADAM_LR = 0.001
ADAM_B1 = 0.9
ADAM_B2 = 0.999
ADAM_EPS = 1e-08
ADAM_WD = 0.01
ADAM_STEP = 10
PER_EXAMPLE_BATCH_AXIS = {'x': 0, 'loss_target': 0}
SHARED_INPUTS = []
_WEIGHT_DTYPES = {'ev_norm_g': _jnp.float32, 'ev_w_in': _jnp.float32, 'ev_sinks': _jnp.float32, 'ev_conv_w': _jnp.float32, 'ev_conv_b': _jnp.float32, 'ev_conv_ln_g': _jnp.float32, 'ev_conv_ln_b': _jnp.float32, 'ev_w_out': _jnp.float32, 'od_norm_g': _jnp.float32, 'od_w_in': _jnp.float32, 'od_sgu_ln_g': _jnp.float32, 'od_sgu_ln_b': _jnp.float32, 'od_spatial_w': _jnp.float32, 'od_spatial_b': _jnp.float32, 'od_w_out': _jnp.float32, 'ffn_norm_g': _jnp.float32, 'ffn_w_gate': _jnp.float32, 'ffn_w_up': _jnp.float32, 'ffn_w_down': _jnp.float32, 'final_norm_g': _jnp.float32}
MOMENT_SCALE = {'ev_norm_g': 1.516009e-01, 'ev_w_in': 1.148949e-01, 'ev_sinks': 3.711381e-02, 'ev_conv_w': 1.776385e-01, 'ev_conv_b': 4.173747e-01, 'ev_conv_ln_g': 2.578778e-01, 'ev_conv_ln_b': 2.349359e-01, 'ev_w_out': 1.313231e-01, 'od_norm_g': 1.495576e-01, 'od_w_in': 9.205174e-02, 'od_sgu_ln_g': 9.453450e-02, 'od_sgu_ln_b': 9.092047e-02, 'od_spatial_w': 6.477145e-02, 'od_spatial_b': 9.507488e-02, 'od_w_out': 1.186162e-01, 'ffn_norm_g': 1.771225e-01, 'ffn_w_gate': 7.446656e-02, 'ffn_w_up': 7.261280e-02, 'ffn_w_down': 1.200256e-01, 'final_norm_g': 6.410986e+01}


def _to_microbatches(a, axis):
    t = _jnp.moveaxis(a, axis, 0)
    t = t.reshape((N_MICROBATCH, t.shape[0] // N_MICROBATCH) + t.shape[1:])
    return _jnp.moveaxis(t, 1, axis + 1)


def setup_inputs(seed: int = 0) -> dict:
    inp = _fwd_setup_inputs(seed)
    key = _jax.random.fold_in(_jax.random.key(seed), 7919)
    shape, _ = _output_shape()
    out = dict(inp)
    out["loss_target"] = _jax.random.normal(_jax.random.fold_in(key, 0), shape, _jnp.float32)
    for i, name in enumerate(TWIN_WEIGHTS):
        w = inp[name].astype(_jnp.float32)
        if MOMENT_SCALE is None:
            s = _jnp.sqrt(_jnp.mean(_jnp.square(w)) + 1e-30)
        else:
            s = MOMENT_SCALE[name]
        km, kv = _jax.random.split(_jax.random.fold_in(key, i + 1))
        out[name] = w
        out["m_" + name] = s * _jax.random.normal(km, w.shape, _jnp.float32)
        out["v_" + name] = (s * s) * _jax.random.uniform(kv, w.shape, _jnp.float32, 0.5, 1.5)
    if N_MICROBATCH > 1:
        for name, axis in PER_EXAMPLE_BATCH_AXIS.items():
            out[name] = _to_microbatches(out[name], axis)
    return {'x': out['x'], 'ev_norm_g': out['ev_norm_g'], 'ev_w_in': out['ev_w_in'], 'ev_sinks': out['ev_sinks'], 'ev_conv_w': out['ev_conv_w'], 'ev_conv_b': out['ev_conv_b'], 'ev_conv_ln_g': out['ev_conv_ln_g'], 'ev_conv_ln_b': out['ev_conv_ln_b'], 'ev_w_out': out['ev_w_out'], 'od_norm_g': out['od_norm_g'], 'od_w_in': out['od_w_in'], 'od_sgu_ln_g': out['od_sgu_ln_g'], 'od_sgu_ln_b': out['od_sgu_ln_b'], 'od_spatial_w': out['od_spatial_w'], 'od_spatial_b': out['od_spatial_b'], 'od_w_out': out['od_w_out'], 'ffn_norm_g': out['ffn_norm_g'], 'ffn_w_gate': out['ffn_w_gate'], 'ffn_w_up': out['ffn_w_up'], 'ffn_w_down': out['ffn_w_down'], 'final_norm_g': out['final_norm_g'], 'loss_target': out['loss_target'], 'm_ev_norm_g': out['m_ev_norm_g'], 'm_ev_w_in': out['m_ev_w_in'], 'm_ev_sinks': out['m_ev_sinks'], 'm_ev_conv_w': out['m_ev_conv_w'], 'm_ev_conv_b': out['m_ev_conv_b'], 'm_ev_conv_ln_g': out['m_ev_conv_ln_g'], 'm_ev_conv_ln_b': out['m_ev_conv_ln_b'], 'm_ev_w_out': out['m_ev_w_out'], 'm_od_norm_g': out['m_od_norm_g'], 'm_od_w_in': out['m_od_w_in'], 'm_od_sgu_ln_g': out['m_od_sgu_ln_g'], 'm_od_sgu_ln_b': out['m_od_sgu_ln_b'], 'm_od_spatial_w': out['m_od_spatial_w'], 'm_od_spatial_b': out['m_od_spatial_b'], 'm_od_w_out': out['m_od_w_out'], 'm_ffn_norm_g': out['m_ffn_norm_g'], 'm_ffn_w_gate': out['m_ffn_w_gate'], 'm_ffn_w_up': out['m_ffn_w_up'], 'm_ffn_w_down': out['m_ffn_w_down'], 'm_final_norm_g': out['m_final_norm_g'], 'v_ev_norm_g': out['v_ev_norm_g'], 'v_ev_w_in': out['v_ev_w_in'], 'v_ev_sinks': out['v_ev_sinks'], 'v_ev_conv_w': out['v_ev_conv_w'], 'v_ev_conv_b': out['v_ev_conv_b'], 'v_ev_conv_ln_g': out['v_ev_conv_ln_g'], 'v_ev_conv_ln_b': out['v_ev_conv_ln_b'], 'v_ev_w_out': out['v_ev_w_out'], 'v_od_norm_g': out['v_od_norm_g'], 'v_od_w_in': out['v_od_w_in'], 'v_od_sgu_ln_g': out['v_od_sgu_ln_g'], 'v_od_sgu_ln_b': out['v_od_sgu_ln_b'], 'v_od_spatial_w': out['v_od_spatial_w'], 'v_od_spatial_b': out['v_od_spatial_b'], 'v_od_w_out': out['v_od_w_out'], 'v_ffn_norm_g': out['v_ffn_norm_g'], 'v_ffn_w_gate': out['v_ffn_w_gate'], 'v_ffn_w_up': out['v_ffn_w_up'], 'v_ffn_w_down': out['v_ffn_w_down'], 'v_final_norm_g': out['v_final_norm_g']}


def _loss(weights, diff, rest, loss_target):
    with _jax.named_scope("forward"):
        args = {**rest, TWIN_DIFF_INPUT: diff, **{k: w.astype(_WEIGHT_DTYPES[k]) for k, w in weights.items()}}
        y = _forward(args)
    with _jax.named_scope("loss_head"):
        err = _jnp.square(y.astype(_jnp.float32) - loss_target)
        return 0.5 * _jnp.sum(_jnp.mean(err, axis=-1)) if err.ndim else 0.5 * err


def _adamw(w, g, m, v):
    m = ADAM_B1 * m + (1.0 - ADAM_B1) * g
    v = ADAM_B2 * v + (1.0 - ADAM_B2) * _jnp.square(g)
    m_hat = m / (1.0 - ADAM_B1 ** ADAM_STEP)
    v_hat = v / (1.0 - ADAM_B2 ** ADAM_STEP)
    delta = -ADAM_LR * (m_hat / (_jnp.sqrt(v_hat) + ADAM_EPS) + ADAM_WD * w)
    return delta, m, v


def reference(x, ev_norm_g, ev_w_in, ev_sinks, ev_conv_w, ev_conv_b, ev_conv_ln_g, ev_conv_ln_b, ev_w_out, od_norm_g, od_w_in, od_sgu_ln_g, od_sgu_ln_b, od_spatial_w, od_spatial_b, od_w_out, ffn_norm_g, ffn_w_gate, ffn_w_up, ffn_w_down, final_norm_g, loss_target, m_ev_norm_g, m_ev_w_in, m_ev_sinks, m_ev_conv_w, m_ev_conv_b, m_ev_conv_ln_g, m_ev_conv_ln_b, m_ev_w_out, m_od_norm_g, m_od_w_in, m_od_sgu_ln_g, m_od_sgu_ln_b, m_od_spatial_w, m_od_spatial_b, m_od_w_out, m_ffn_norm_g, m_ffn_w_gate, m_ffn_w_up, m_ffn_w_down, m_final_norm_g, v_ev_norm_g, v_ev_w_in, v_ev_sinks, v_ev_conv_w, v_ev_conv_b, v_ev_conv_ln_g, v_ev_conv_ln_b, v_ev_w_out, v_od_norm_g, v_od_w_in, v_od_sgu_ln_g, v_od_sgu_ln_b, v_od_spatial_w, v_od_spatial_b, v_od_w_out, v_ffn_norm_g, v_ffn_w_gate, v_ffn_w_up, v_ffn_w_down, v_final_norm_g):
    given = dict(x=x, ev_norm_g=ev_norm_g, ev_w_in=ev_w_in, ev_sinks=ev_sinks, ev_conv_w=ev_conv_w, ev_conv_b=ev_conv_b, ev_conv_ln_g=ev_conv_ln_g, ev_conv_ln_b=ev_conv_ln_b, ev_w_out=ev_w_out, od_norm_g=od_norm_g, od_w_in=od_w_in, od_sgu_ln_g=od_sgu_ln_g, od_sgu_ln_b=od_sgu_ln_b, od_spatial_w=od_spatial_w, od_spatial_b=od_spatial_b, od_w_out=od_w_out, ffn_norm_g=ffn_norm_g, ffn_w_gate=ffn_w_gate, ffn_w_up=ffn_w_up, ffn_w_down=ffn_w_down, final_norm_g=final_norm_g, loss_target=loss_target, m_ev_norm_g=m_ev_norm_g, m_ev_w_in=m_ev_w_in, m_ev_sinks=m_ev_sinks, m_ev_conv_w=m_ev_conv_w, m_ev_conv_b=m_ev_conv_b, m_ev_conv_ln_g=m_ev_conv_ln_g, m_ev_conv_ln_b=m_ev_conv_ln_b, m_ev_w_out=m_ev_w_out, m_od_norm_g=m_od_norm_g, m_od_w_in=m_od_w_in, m_od_sgu_ln_g=m_od_sgu_ln_g, m_od_sgu_ln_b=m_od_sgu_ln_b, m_od_spatial_w=m_od_spatial_w, m_od_spatial_b=m_od_spatial_b, m_od_w_out=m_od_w_out, m_ffn_norm_g=m_ffn_norm_g, m_ffn_w_gate=m_ffn_w_gate, m_ffn_w_up=m_ffn_w_up, m_ffn_w_down=m_ffn_w_down, m_final_norm_g=m_final_norm_g, v_ev_norm_g=v_ev_norm_g, v_ev_w_in=v_ev_w_in, v_ev_sinks=v_ev_sinks, v_ev_conv_w=v_ev_conv_w, v_ev_conv_b=v_ev_conv_b, v_ev_conv_ln_g=v_ev_conv_ln_g, v_ev_conv_ln_b=v_ev_conv_ln_b, v_ev_w_out=v_ev_w_out, v_od_norm_g=v_od_norm_g, v_od_w_in=v_od_w_in, v_od_sgu_ln_g=v_od_sgu_ln_g, v_od_sgu_ln_b=v_od_sgu_ln_b, v_od_spatial_w=v_od_spatial_w, v_od_spatial_b=v_od_spatial_b, v_od_w_out=v_od_w_out, v_ffn_norm_g=v_ffn_norm_g, v_ffn_w_gate=v_ffn_w_gate, v_ffn_w_up=v_ffn_w_up, v_ffn_w_down=v_ffn_w_down, v_final_norm_g=v_final_norm_g)
    weights = {n: given[n] for n in TWIN_WEIGHTS}
    shared = {n: given[n] for n in SHARED_INPUTS}
    per_example = {n: given[n] for n in ['x']}
    grad_fn = _jax.value_and_grad(_loss, argnums=(0, 1))

    def one_microbatch(ex, loss_target):
        ex = dict(ex)
        diff = ex.pop(TWIN_DIFF_INPUT)
        return grad_fn(weights, diff, {**shared, **ex}, loss_target)

    if N_MICROBATCH == 1:
        loss, (grad_w, grad_x) = one_microbatch(per_example, given["loss_target"])
    else:
        def body(carry, xs):
            loss_sum, grad_sum = carry
            l_k, (gw_k, gx_k) = one_microbatch(xs[0], xs[1])
            with _jax.named_scope("update"):
                return (loss_sum + l_k, _jax.tree.map(_jnp.add, grad_sum, gw_k)), gx_k

        init = (_jnp.zeros((), _jnp.float32), _jax.tree.map(_jnp.zeros_like, weights))
        (loss, grad_w), grad_x = _jax.lax.scan(body, init, (per_example, given["loss_target"]))
    with _jax.named_scope("update"):
        delta_w, new_m, new_v = {}, {}, {}
        for n in TWIN_WEIGHTS:
            delta_w[n], new_m[n], new_v[n] = _adamw(weights[n], grad_w[n], given["m_" + n], given["v_" + n])
    return (loss, grad_x, *[grad_w[n] for n in TWIN_WEIGHTS], *[delta_w[n] for n in TWIN_WEIGHTS],
            *[new_m[n] for n in TWIN_WEIGHTS], *[new_v[n] for n in TWIN_WEIGHTS])
```

```python
import math

import jax
import jax.numpy as jnp
from jax import lax
from jax.experimental import pallas as pl
from jax.experimental.pallas import tpu as pltpu

F32 = jnp.float32
BF16 = jnp.bfloat16

D_MODEL = 1024
HEAD_DIM = 64
ROT_DIM = 16
ROPE_THETA = 500000.0
RMS_EPS = 1e-6
LN_EPS = 1e-5
BLOCK = 128
CONV_WIDTH = 31
CONV_HALO = 32
D_FF = 2816
N_GROUPS = 8
ATTN_SCALE = HEAD_DIM ** -0.5
NEG = -1e30
DILATIONS = (1, 4, 16)

ADAM_LR = 0.001
ADAM_B1 = 0.9
ADAM_B2 = 0.999
ADAM_EPS = 1e-08
ADAM_WD = 0.01
ADAM_STEP = 10

LANES = 128
VMEM_LIMIT = 56 * 1024 * 1024
MESH = pl.DeviceIdType.MESH
N_CHIPS = 4
N_DEV = 8

WEIGHTS = ['ev_norm_g', 'ev_w_in', 'ev_sinks', 'ev_conv_w', 'ev_conv_b', 'ev_conv_ln_g', 'ev_conv_ln_b', 'ev_w_out',
           'od_norm_g', 'od_w_in', 'od_sgu_ln_g', 'od_sgu_ln_b', 'od_spatial_w', 'od_spatial_b', 'od_w_out',
           'ffn_norm_g', 'ffn_w_gate', 'ffn_w_up', 'ffn_w_down', 'final_norm_g']
BIG = {'ev_w_in': 2, 'ev_w_out': 1, 'od_w_in': 2, 'od_w_out': 1, 'ffn_w_gate': 2, 'ffn_w_up': 2, 'ffn_w_down': 1}
SMALL_SHARDED = ['ev_conv_w', 'od_norm_g', 'od_sgu_ln_g', 'od_sgu_ln_b']
SMALL_REPL = ['ev_norm_g', 'ev_sinks', 'ev_conv_b', 'ev_conv_ln_g', 'ev_conv_ln_b', 'od_spatial_w', 'od_spatial_b',
              'ffn_norm_g', 'final_norm_g']


def _tile(n, cap, mult=LANES):
    best = None
    for t in range(mult, min(n, cap) + 1, mult):
        if n % t == 0:
            best = t
    assert best is not None, (n, cap)
    return best


def _params(*sem):
    return pltpu.CompilerParams(dimension_semantics=sem, vmem_limit_bytes=VMEM_LIMIT)


def _sigmoid(x):
    return 1.0 / (1.0 + jnp.exp(-x))


def _matmul(a, b, *, name, trans_a=False, trans_b=False, add=None, out_dtype=F32):
    if trans_a:
        k, m = a.shape
    else:
        m, k = a.shape
    if trans_b:
        n, k2 = b.shape
    else:
        k2, n = b.shape
    assert k == k2 and a.dtype == BF16 and b.dtype == BF16
    tm = _tile(m, 512)
    tn = _tile(n, D_FF // 2)
    tk = k if k <= D_FF else _tile(k, 1024)
    nk = k // tk
    dims = (((0 if trans_a else 1,), (1 if trans_b else 0,)), ((), ()))
    has_add = add is not None

    def body(*refs):
        if has_add:
            a_ref, b_ref, add_ref, o_ref = refs[:4]
        else:
            a_ref, b_ref, o_ref = refs[:3]
            add_ref = None
        part = lax.dot_general(a_ref[...], b_ref[...], dims, preferred_element_type=F32)
        if nk == 1:
            if has_add:
                part = part + add_ref[...]
            o_ref[...] = part.astype(o_ref.dtype)
            return
        acc_ref = refs[-1]
        kk = pl.program_id(2)

        @pl.when(kk == 0)
        def _():
            acc_ref[...] = part

        @pl.when(kk > 0)
        def _():
            acc_ref[...] += part

        @pl.when(kk == nk - 1)
        def _():
            res = acc_ref[...]
            if has_add:
                res = res + add_ref[...]
            o_ref[...] = res.astype(o_ref.dtype)

    a_spec = pl.BlockSpec((tk, tm), lambda i, j, kk: (kk, i)) if trans_a else pl.BlockSpec((tm, tk), lambda i, j, kk: (i, kk))
    b_spec = pl.BlockSpec((tn, tk), lambda i, j, kk: (j, kk)) if trans_b else pl.BlockSpec((tk, tn), lambda i, j, kk: (kk, j))
    o_spec = pl.BlockSpec((tm, tn), lambda i, j, kk: (i, j))
    in_specs = [a_spec, b_spec] + ([o_spec] if has_add else [])
    operands = [a, b] + ([add] if has_add else [])
    return pl.pallas_call(
        body, name=name, grid=(m // tm, n // tn, nk), in_specs=in_specs, out_specs=o_spec,
        out_shape=jax.ShapeDtypeStruct((m, n), out_dtype),
        scratch_shapes=[pltpu.VMEM((tm, tn), F32)] if nk > 1 else [],
        compiler_params=_params("parallel", "parallel", "arbitrary"),
    )(*operands)


def _rows(body, name, tm, tiled, consts, outs, accs=()):
    s = tiled[0].shape[0]
    assert s % tm == 0
    in_specs = [pl.BlockSpec((tm, a.shape[1]), lambda i: (i, 0)) for a in tiled]
    in_specs += [pl.BlockSpec(a.shape, lambda i, nd=a.ndim: (0,) * nd) for a in consts]
    out_shape = [jax.ShapeDtypeStruct((s, c), dt) for c, dt in outs]
    out_shape += [jax.ShapeDtypeStruct(sh, dt) for sh, dt in accs]
    out_specs = [pl.BlockSpec((tm, c), lambda i: (i, 0)) for c, _ in outs]
    out_specs += [pl.BlockSpec(sh, lambda i, nd=len(sh): (0,) * nd) for sh, _ in accs]
    return pl.pallas_call(
        body, name=name, grid=(s // tm,), in_specs=in_specs, out_specs=out_specs, out_shape=out_shape,
        compiler_params=_params("arbitrary"),
    )(*tiled, *consts)


def _first_step():
    return pl.program_id(0) == 0


def _rms_fwd(h, g, name):
    def body(h_ref, g_ref, n_ref):
        x = h_ref[...]
        r = lax.rsqrt(jnp.mean(x * x, axis=-1, keepdims=True) + RMS_EPS)
        n_ref[...] = (x * r * g_ref[...]).astype(BF16)

    return _rows(body, name, 512, [h], [g], [(D_MODEL, BF16)])[0]


def _rms_bwd(dn, h, g, dres, name):
    def body(dn_ref, h_ref, dres_ref, g_ref, dh_ref, dhb_ref, dg_ref):
        @pl.when(_first_step())
        def _():
            dg_ref[...] = jnp.zeros_like(dg_ref)

        x = h_ref[...]
        r = lax.rsqrt(jnp.mean(x * x, axis=-1, keepdims=True) + RMS_EPS)
        xh = x * r
        dy = dn_ref[...]
        dg_ref[...] += jnp.sum(dy * xh, axis=0, keepdims=True)
        dxh = dy * g_ref[...]
        tot = dres_ref[...] + r * (dxh - xh * jnp.mean(dxh * xh, axis=-1, keepdims=True))
        dh_ref[...] = tot
        dhb_ref[...] = tot.astype(BF16)

    return _rows(body, name, 512, [dn, h, dres], [g], [(D_MODEL, F32), (D_MODEL, BF16)], [((1, D_MODEL), F32)])


def _final_loss(h, g, tgt, name):
    def body(h_ref, t_ref, g_ref, dh_ref, dhb_ref, dg_ref, loss_ref):
        @pl.when(_first_step())
        def _():
            dg_ref[...] = jnp.zeros_like(dg_ref)
            loss_ref[...] = jnp.zeros_like(loss_ref)

        x = h_ref[...]
        r = lax.rsqrt(jnp.mean(x * x, axis=-1, keepdims=True) + RMS_EPS)
        xh = x * r
        gg = g_ref[...]
        e = xh * gg - t_ref[...]
        loss_ref[...] += (0.5 / D_MODEL) * jnp.sum(jnp.sum(e * e, axis=-1, keepdims=True), axis=0, keepdims=True)
        dy = e * (1.0 / D_MODEL)
        dg_ref[...] += jnp.sum(dy * xh, axis=0, keepdims=True)
        dxh = dy * gg
        dx = r * (dxh - xh * jnp.mean(dxh * xh, axis=-1, keepdims=True))
        dh_ref[...] = dx
        dhb_ref[...] = dx.astype(BF16)

    return _rows(body, name, 512, [h, tgt], [g], [(D_MODEL, F32), (D_MODEL, BF16)],
                 [((1, D_MODEL), F32), ((1, LANES), F32)])


def _swiglu_fwd(gate, up, name):
    def body(g_ref, u_ref, a_ref):
        g = g_ref[...]
        a_ref[...] = (g * _sigmoid(g) * u_ref[...]).astype(BF16)

    return _rows(body, name, 256, [gate, up], [], [(D_FF, BF16)])[0]


def _swiglu_bwd(dact, gate, up, name):
    def body(d_ref, g_ref, u_ref, dg_ref, du_ref):
        g = g_ref[...]
        d = d_ref[...]
        sg = _sigmoid(g)
        dg_ref[...] = (d * u_ref[...] * sg * (1.0 + g * (1.0 - sg))).astype(BF16)
        du_ref[...] = (d * g * sg).astype(BF16)

    return _rows(body, name, 256, [dact, gate, up], [], [(D_FF, BF16), (D_FF, BF16)])


def _rope_tables(s):
    half = ROT_DIM // 2
    inv_freq = ROPE_THETA ** (-jnp.arange(half, dtype=F32) * (2.0 / ROT_DIM))
    ang = jnp.arange(s, dtype=F32)[:, None] * inv_freq[None, :]
    cos, sin = jnp.cos(ang), jnp.sin(ang)
    rest = HEAD_DIM - ROT_DIM
    ones = jnp.ones((s, rest), F32)
    zeros = jnp.zeros((s, rest), F32)
    zh = jnp.zeros((s, half), F32)
    c_t = jnp.concatenate([cos, cos, ones], axis=1)
    a_t = jnp.concatenate([-sin, zh, zeros], axis=1)
    b_t = jnp.concatenate([zh, sin, zeros], axis=1)
    return tuple(jnp.tile(t, (1, LANES // HEAD_DIM)) for t in (c_t, a_t, b_t))


def _rot(x, c, a, b):
    w = x.shape[1]
    half = ROT_DIM // 2
    return x * c + pltpu.roll(x, w - half, 1) * a + pltpu.roll(x, half, 1) * b


def _wide(t, w):
    return t if w == LANES else jnp.tile(t, (1, w // LANES))


def _qkv_prep(proj, tabs, wq, wk, name):
    def body(p_ref, c_ref, a_ref, b_ref, q_ref, k_ref, v_ref):
        c, a, b = c_ref[...], a_ref[...], b_ref[...]
        q_ref[...] = _rot(p_ref[:, 0:wq], _wide(c, wq), _wide(a, wq), _wide(b, wq)).astype(BF16)
        k_ref[...] = _rot(p_ref[:, wq:wq + wk], _wide(c, wk), _wide(a, wk), _wide(b, wk)).astype(BF16)
        v_ref[...] = p_ref[:, wq + wk:wq + 2 * wk].astype(BF16)

    return _rows(body, name, 512, [proj, *tabs], [], [(wq, BF16), (wk, BF16), (wk, BF16)])


def _qkv_post(dqs, dks, dvs, drest, tabs, name):
    nb = len(dqs)
    wq, wk, wr = dqs[0].shape[1], dks[0].shape[1], drest.shape[1]

    def body(*refs):
        dq_refs, dk_refs, dv_refs = refs[:nb], refs[nb:2 * nb], refs[2 * nb:3 * nb]
        dr_ref, c_ref, a_ref, b_ref, o_ref = refs[3 * nb:]
        c, a, b = c_ref[...], -a_ref[...], -b_ref[...]
        dq = sum(r[...] for r in dq_refs[1:]) + dq_refs[0][...]
        dk = sum(r[...] for r in dk_refs[1:]) + dk_refs[0][...]
        dv = sum(r[...] for r in dv_refs[1:]) + dv_refs[0][...]
        o_ref[:, 0:wq] = _rot(dq, _wide(c, wq), _wide(a, wq), _wide(b, wq)).astype(BF16)
        o_ref[:, wq:wq + wk] = _rot(dk, _wide(c, wk), _wide(a, wk), _wide(b, wk)).astype(BF16)
        o_ref[:, wq + wk:wq + 2 * wk] = dv.astype(BF16)
        o_ref[:, wq + 2 * wk:] = dr_ref[...]

    return _rows(body, name, 256, [*dqs, *dks, *dvs, drest, *tabs], [], [(wq + 2 * wk + wr, BF16)])[0]


def _band_dist(max_dist):
    qi = lax.broadcasted_iota(jnp.int32, (BLOCK, 2 * BLOCK), 0)
    kj = lax.broadcasted_iota(jnp.int32, (BLOCK, 2 * BLOCK), 1)
    dist = qi + BLOCK - kj
    return jnp.where(dist >= 0, dist, max_dist + 1), kj


def _band_scores(s, dist, kj, r0, max_dist):
    dist = jnp.where(kj + r0 >= BLOCK, dist, max_dist + 1)
    return jnp.where(dist <= max_dist, s, NEG)


def _attn_fwd(q, k, v, sinks, *, max_dist, group, name):
    bq, sp, _ = q.shape
    bk = k.shape[0]
    assert bq == bk * group and k.shape[1] == sp + BLOCK
    tq = min(sp, 1024)
    nsub = tq // BLOCK
    has_sink = sinks is not None

    def body(*refs):
        if has_sink:
            sink_ref, q_ref, k_ref, v_ref, o_ref, lse_ref = refs
        else:
            q_ref, k_ref, v_ref, o_ref, lse_ref = refs
        j = pl.program_id(1)
        dist, kj = _band_dist(max_dist)
        if has_sink:
            sk = sink_ref[pl.program_id(0)]
        for i in range(nsub):
            r0 = pl.multiple_of(j * tq + i * BLOCK, BLOCK)
            rows = slice(i * BLOCK, (i + 1) * BLOCK)
            kw = k_ref[0, pl.ds(r0, 2 * BLOCK), :]
            vw = v_ref[0, pl.ds(r0, 2 * BLOCK), :]
            s = lax.dot_general(q_ref[0, rows, :], kw, (((1,), (1,)), ((), ())), preferred_element_type=F32) * ATTN_SCALE
            s = _band_scores(s, dist, kj, r0, max_dist)
            m = jnp.max(s, axis=-1, keepdims=True)
            if has_sink:
                m = jnp.maximum(m, sk)
            p = jnp.exp(s - m)
            l = jnp.sum(p, axis=-1, keepdims=True)
            if has_sink:
                l = l + jnp.exp(sk - m)
            o = jnp.dot(p.astype(BF16), vw, preferred_element_type=F32)
            o_ref[0, rows, :] = o / l
            lse_ref[0, rows, :] = jnp.broadcast_to(m + jnp.log(l), (BLOCK, LANES))

    in_specs = [pl.BlockSpec((1, tq, HEAD_DIM), lambda b, j: (b, j, 0)),
                pl.BlockSpec((1, sp + BLOCK, HEAD_DIM), lambda b, j: (b // group, 0, 0)),
                pl.BlockSpec((1, sp + BLOCK, HEAD_DIM), lambda b, j: (b // group, 0, 0))]
    operands = [q, k, v]
    if has_sink:
        in_specs = [pl.BlockSpec(memory_space=pltpu.SMEM)] + in_specs
        operands = [sinks] + operands
    return pl.pallas_call(
        body, name=name, grid=(bq, sp // tq), in_specs=in_specs,
        out_specs=[pl.BlockSpec((1, tq, HEAD_DIM), lambda b, j: (b, j, 0)),
                   pl.BlockSpec((1, tq, LANES), lambda b, j: (b, j, 0))],
        out_shape=[jax.ShapeDtypeStruct((bq, sp, HEAD_DIM), F32), jax.ShapeDtypeStruct((bq, sp, LANES), F32)],
        compiler_params=_params("parallel", "arbitrary"),
    )(*operands)


def _attn_bwd(q, k, v, do, oo, lse, sinks, *, max_dist, group, name):
    bq, sp, _ = q.shape
    bk = k.shape[0]
    tq = min(sp, 1024)
    nsub = tq // BLOCK
    nj = sp // tq
    has_sink = sinks is not None

    def body(*refs):
        if has_sink:
            sink_ref, q_ref, k_ref, v_ref, do_ref, oo_ref, lse_ref, dq_ref, dk_ref, dv_ref, dsink_ref = refs
        else:
            q_ref, k_ref, v_ref, do_ref, oo_ref, lse_ref, dq_ref, dk_ref, dv_ref = refs
        g = pl.program_id(1)
        j = pl.program_id(2)

        @pl.when((g == 0) & (j == 0))
        def _():
            dk_ref[...] = jnp.zeros_like(dk_ref)
            dv_ref[...] = jnp.zeros_like(dv_ref)

        dist, kj = _band_dist(max_dist)
        if has_sink:
            sk = sink_ref[pl.program_id(0) * group + g]
            sink_acc = jnp.zeros((1, LANES), F32)
        for i in range(nsub):
            r0 = pl.multiple_of(j * tq + i * BLOCK, BLOCK)
            rows = slice(i * BLOCK, (i + 1) * BLOCK)
            win = pl.ds(r0, 2 * BLOCK)
            qi = q_ref[0, rows, :]
            kw = k_ref[0, win, :]
            vw = v_ref[0, win, :]
            doi = do_ref[0, rows, :]
            lse_i = lse_ref[0, rows, :]
            s = lax.dot_general(qi, kw, (((1,), (1,)), ((), ())), preferred_element_type=F32) * ATTN_SCALE
            s = _band_scores(s, dist, kj, r0, max_dist)
            p = jnp.exp(s - jnp.tile(lse_i, (1, 2)))
            delta = jnp.sum(doi * oo_ref[0, rows, :], axis=-1, keepdims=True)
            dob = doi.astype(BF16)
            dv_ref[0, win, :] += lax.dot_general(p.astype(BF16), dob, (((0,), (0,)), ((), ())), preferred_element_type=F32)
            dp = lax.dot_general(dob, vw, (((1,), (1,)), ((), ())), preferred_element_type=F32)
            ds = (p * (dp - delta) * ATTN_SCALE).astype(BF16)
            dq_ref[0, rows, :] = jnp.dot(ds, kw, preferred_element_type=F32)
            dk_ref[0, win, :] += lax.dot_general(ds, qi, (((0,), (0,)), ((), ())), preferred_element_type=F32)
            if has_sink:
                sink_acc = sink_acc - jnp.sum(jnp.exp(sk - lse_i) * delta, axis=0, keepdims=True)
        if has_sink:
            @pl.when(j == 0)
            def _():
                dsink_ref[...] = jnp.zeros_like(dsink_ref)

            dsink_ref[0] += sink_acc

    def qmap(b, g, j):
        return (b * group + g, j, 0)

    def kmap(b, g, j):
        return (b, 0, 0)

    in_specs = [pl.BlockSpec((1, tq, HEAD_DIM), qmap),
                pl.BlockSpec((1, sp + BLOCK, HEAD_DIM), kmap), pl.BlockSpec((1, sp + BLOCK, HEAD_DIM), kmap),
                pl.BlockSpec((1, tq, HEAD_DIM), qmap), pl.BlockSpec((1, tq, HEAD_DIM), qmap),
                pl.BlockSpec((1, tq, LANES), qmap)]
    operands = [q, k, v, do, oo, lse]
    out_specs = [pl.BlockSpec((1, tq, HEAD_DIM), qmap),
                 pl.BlockSpec((1, sp + BLOCK, HEAD_DIM), kmap), pl.BlockSpec((1, sp + BLOCK, HEAD_DIM), kmap)]
    out_shape = [jax.ShapeDtypeStruct((bq, sp, HEAD_DIM), F32),
                 jax.ShapeDtypeStruct((bk, sp + BLOCK, HEAD_DIM), F32),
                 jax.ShapeDtypeStruct((bk, sp + BLOCK, HEAD_DIM), F32)]
    if has_sink:
        in_specs = [pl.BlockSpec(memory_space=pltpu.SMEM)] + in_specs
        operands = [sinks] + operands
        out_specs.append(pl.BlockSpec((1, 1, LANES), lambda b, g, j: (b * group + g, 0, 0)))
        out_shape.append(jax.ShapeDtypeStruct((bq, 1, LANES), F32))
    return pl.pallas_call(
        body, name=name, grid=(bk, group, nj), in_specs=in_specs, out_specs=out_specs, out_shape=out_shape,
        compiler_params=_params("parallel", "arbitrary", "arbitrary"),
    )(*operands)


def _combine(outs, lses, name):
    nh, s, _ = outs[0].shape
    tq = 1024
    nb = len(outs)

    def body(*refs):
        o_refs, l_refs = refs[:nb], refs[nb:2 * nb]
        c_ref, lse_ref = refs[2 * nb:]
        ls = [r[0] for r in l_refs]
        m = ls[0]
        for t in ls[1:]:
            m = jnp.maximum(m, t)
        ws = [jnp.exp(t - m) for t in ls]
        tot = ws[0]
        for t in ws[1:]:
            tot = tot + t
        lse_ref[0] = m + jnp.log(tot)
        acc = jnp.zeros((tq, HEAD_DIM), F32)
        for w, o_ref in zip(ws, o_refs):
            acc = acc + (w / tot)[:, :HEAD_DIM] * o_ref[0]
        c_ref[0] = acc

    o_spec = pl.BlockSpec((1, tq, HEAD_DIM), lambda b, j: (b, j, 0))
    l_spec = pl.BlockSpec((1, tq, LANES), lambda b, j: (b, j, 0))
    return pl.pallas_call(
        body, name=name, grid=(nh, s // tq), in_specs=[o_spec] * nb + [l_spec] * nb, out_specs=[o_spec, l_spec],
        out_shape=[jax.ShapeDtypeStruct((nh, s, HEAD_DIM), F32), jax.ShapeDtypeStruct((nh, s, LANES), F32)],
        compiler_params=_params("parallel", "parallel"),
    )(*outs, *lses)


def _fold(t, heads, d, pad):
    s = t.shape[0]
    t = t.reshape(s // d, d, heads, HEAD_DIM).transpose(2, 1, 0, 3).reshape(heads * d, s // d, HEAD_DIM)
    if pad:
        t = jnp.pad(t, ((0, 0), (BLOCK, 0), (0, 0)))
    return t


def _unfold(t, heads, d):
    sd = t.shape[1]
    return t.reshape(heads, d, sd, HEAD_DIM).transpose(2, 1, 0, 3).reshape(sd * d, heads * HEAD_DIM)


def _refold(t, d):
    heads, s, w = t.shape
    return t.reshape(heads, s // d, d, w).transpose(0, 2, 1, 3).reshape(heads * d, s // d, w)


def _unrefold(t, heads, d):
    _, sd, w = t.shape
    return t.reshape(heads, d, sd, w).transpose(0, 2, 1, 3).reshape(heads, sd * d, w)


GLU_A = slice(768, 1280)
GLU_B = slice(1280, 1792)
EVEN_IN = 1792
CONV_CH = 512


def _conv_fwd(proj, w, b, ln_g, ln_b, name):
    s = proj.shape[0]
    tm = 512
    nh = tm // CONV_HALO
    lead = CONV_HALO - (CONV_WIDTH - 1)

    def body(p_ref, ph_ref, w_ref, b_ref, g_ref, bb_ref, y_ref, o_ref, xf_ref):
        xf_ref[CONV_HALO:, :] = p_ref[:, GLU_A] * _sigmoid(p_ref[:, GLU_B])
        hist = ph_ref[:, GLU_A] * _sigmoid(ph_ref[:, GLU_B])
        xf_ref[0:CONV_HALO, :] = jnp.where(pl.program_id(0) > 0, hist, 0.0)
        acc = jnp.zeros((tm, CONV_CH), F32) + b_ref[...]
        for j in range(CONV_WIDTH):
            acc = acc + xf_ref[pl.ds(lead + j, tm), :] * w_ref[j:j + 1, :]
        y_ref[...] = acc
        mu = jnp.mean(acc, axis=-1, keepdims=True)
        xc = acc - mu
        var = jnp.mean(xc * xc, axis=-1, keepdims=True)
        zz = xc * lax.rsqrt(var + LN_EPS) * g_ref[...] + bb_ref[...]
        o_ref[...] = (zz * _sigmoid(zz)).astype(BF16)

    def const(a):
        return pl.BlockSpec(a.shape, lambda i: (0, 0))

    return pl.pallas_call(
        body, name=name, grid=(s // tm,),
        in_specs=[pl.BlockSpec((tm, EVEN_IN), lambda i: (i, 0)),
                  pl.BlockSpec((CONV_HALO, EVEN_IN), lambda i: (jnp.maximum(i * nh - 1, 0), 0)),
                  const(w), const(b), const(ln_g), const(ln_b)],
        out_specs=[pl.BlockSpec((tm, CONV_CH), lambda i: (i, 0)), pl.BlockSpec((tm, CONV_CH), lambda i: (i, 0))],
        out_shape=[jax.ShapeDtypeStruct((s, CONV_CH), F32), jax.ShapeDtypeStruct((s, CONV_CH), BF16)],
        scratch_shapes=[pltpu.VMEM((tm + CONV_HALO, CONV_CH), F32)],
        compiler_params=_params("arbitrary"),
    )(proj, proj, w, b, ln_g, ln_b)


def _conv_tail_bwd(dmix, yconv, ln_g, ln_b, name):
    def body(d_ref, y_ref, g_ref, b_ref, dy_ref, dg_ref, db_ref, dcb_ref):
        @pl.when(_first_step())
        def _():
            dg_ref[...] = jnp.zeros_like(dg_ref)
            db_ref[...] = jnp.zeros_like(db_ref)
            dcb_ref[...] = jnp.zeros_like(dcb_ref)

        y = y_ref[...]
        g = g_ref[...]
        mu = jnp.mean(y, axis=-1, keepdims=True)
        xc = y - mu
        rstd = lax.rsqrt(jnp.mean(xc * xc, axis=-1, keepdims=True) + LN_EPS)
        xh = xc * rstd
        zz = xh * g + b_ref[...]
        sg = _sigmoid(zz)
        dzz = d_ref[:, CONV_CH:] * sg * (1.0 + zz * (1.0 - sg))
        dg_ref[...] += jnp.sum(dzz * xh, axis=0, keepdims=True)
        db_ref[...] += jnp.sum(dzz, axis=0, keepdims=True)
        dxh = dzz * g
        dy = rstd * (dxh - jnp.mean(dxh, axis=-1, keepdims=True) - xh * jnp.mean(dxh * xh, axis=-1, keepdims=True))
        dcb_ref[...] += jnp.sum(dy, axis=0, keepdims=True)
        dy_ref[...] = dy

    vec = ((1, CONV_CH), F32)
    return _rows(body, name, 512, [dmix, yconv], [ln_g, ln_b], [(CONV_CH, F32)], [vec, vec, vec])


def _conv_bwd(proj, dy, w, name):
    s = proj.shape[0]
    tm = 512
    nh = tm // CONV_HALO
    nsteps = s // tm
    lead = CONV_HALO - (CONV_WIDTH - 1)

    def body(p_ref, ph_ref, dy_ref, dyn_ref, w_ref, dglu_ref, dw_ref, xf_ref, dyf_ref):
        i = pl.program_id(0)

        @pl.when(i == 0)
        def _():
            dw_ref[...] = jnp.zeros_like(dw_ref)

        ga = p_ref[:, GLU_A]
        sgb = _sigmoid(p_ref[:, GLU_B])
        xf_ref[CONV_HALO:, :] = ga * sgb
        hist = ph_ref[:, GLU_A] * _sigmoid(ph_ref[:, GLU_B])
        xf_ref[0:CONV_HALO, :] = jnp.where(i > 0, hist, 0.0)
        dyt = dy_ref[...]
        dyf_ref[0:tm, :] = dyt
        dyf_ref[tm:, :] = jnp.where(i < nsteps - 1, dyn_ref[...], 0.0)
        acc = jnp.zeros((tm, CONV_CH), F32)
        for j in range(CONV_WIDTH):
            acc = acc + dyf_ref[pl.ds(CONV_WIDTH - 1 - j, tm), :] * w_ref[j:j + 1, :]
        for j in range(CONV_WIDTH):
            dw_ref[j:j + 1, :] += jnp.sum(dyt * xf_ref[pl.ds(lead + j, tm), :], axis=0, keepdims=True)
        dglu_ref[:, 0:CONV_CH] = (acc * sgb).astype(BF16)
        dglu_ref[:, CONV_CH:] = (acc * ga * sgb * (1.0 - sgb)).astype(BF16)

    return pl.pallas_call(
        body, name=name, grid=(nsteps,),
        in_specs=[pl.BlockSpec((tm, EVEN_IN), lambda i: (i, 0)),
                  pl.BlockSpec((CONV_HALO, EVEN_IN), lambda i: (jnp.maximum(i * nh - 1, 0), 0)),
                  pl.BlockSpec((tm, CONV_CH), lambda i: (i, 0)),
                  pl.BlockSpec((CONV_HALO, CONV_CH), lambda i: (jnp.minimum((i + 1) * nh, s // CONV_HALO - 1), 0)),
                  pl.BlockSpec(w.shape, lambda i: (0, 0))],
        out_specs=[pl.BlockSpec((tm, 2 * CONV_CH), lambda i: (i, 0)), pl.BlockSpec(w.shape, lambda i: (0, 0))],
        out_shape=[jax.ShapeDtypeStruct((s, 2 * CONV_CH), BF16), jax.ShapeDtypeStruct(w.shape, F32)],
        scratch_shapes=[pltpu.VMEM((tm + CONV_HALO, CONV_CH), F32), pltpu.VMEM((tm + CONV_HALO, CONV_CH), F32)],
        compiler_params=_params("arbitrary"),
    )(proj, proj, dy, dy, w)


GATE_Z = slice(1536, 2560)
D_CH = 512
GELU_C = math.sqrt(2.0 / math.pi)
GELU_K = 0.044715


def _gelu_parts(z):
    t = jnp.tanh(GELU_C * (z + GELU_K * z * z * z))
    return 0.5 * z * (1.0 + t), t


def _lane_group(rows):
    return lax.broadcasted_iota(jnp.int32, (rows, D_CH), 1) // HEAD_DIM


def _tril_mask():
    return lax.broadcasted_iota(jnp.int32, (BLOCK, BLOCK), 0) >= lax.broadcasted_iota(jnp.int32, (BLOCK, BLOCK), 1)


def _layer_norm_parts(x):
    mu = jnp.mean(x, axis=-1, keepdims=True)
    xc = x - mu
    rstd = lax.rsqrt(jnp.mean(xc * xc, axis=-1, keepdims=True) + LN_EPS)
    return xc * rstd, rstd


def _gate_fwd(proj, ln_g, ln_b, w_sp, sb_t, name):
    tm = 512

    def body(p_ref, g_ref, b_ref, w_ref, sb_ref, mixed_ref, out_ref):
        zz, _ = _gelu_parts(p_ref[:, GATE_Z])
        u = zz[:, :D_CH]
        xh, _ = _layer_norm_parts(zz[:, D_CH:])
        gn = (xh * g_ref[...] + b_ref[...]).astype(BF16)
        grp = _lane_group(BLOCK)
        tri = _tril_mask()
        ws = [jnp.where(tri, w_ref[gi], 0.0).astype(BF16) for gi in range(N_GROUPS)]
        bias = jnp.zeros((BLOCK, D_CH), F32)
        for gi in range(N_GROUPS):
            bias = jnp.where(grp == gi, sb_ref[:, gi:gi + 1], bias)
        for ch in range(tm // BLOCK):
            rows = slice(ch * BLOCK, (ch + 1) * BLOCK)
            gc = gn[rows, :]
            mixed = bias
            for gi in range(N_GROUPS):
                r = jnp.dot(ws[gi], gc, preferred_element_type=F32)
                mixed = jnp.where(grp == gi, r + bias, mixed)
            mixed_ref[rows, :] = mixed
            out_ref[rows, :] = (u[rows, :] * mixed).astype(BF16)

    return _rows(body, name, tm, [proj], [ln_g, ln_b, w_sp, sb_t], [(D_CH, F32), (D_CH, BF16)])


def _gate_bwd(dmix, proj, mixed, ln_g, ln_b, w_sp, name):
    tm = 512

    def body(d_ref, p_ref, m_ref, g_ref, b_ref, w_ref, dz_ref, dg_ref, db_ref, dw_ref, dsb_ref, dgn_ref):
        @pl.when(_first_step())
        def _():
            dg_ref[...] = jnp.zeros_like(dg_ref)
            db_ref[...] = jnp.zeros_like(db_ref)
            dw_ref[...] = jnp.zeros_like(dw_ref)
            dsb_ref[...] = jnp.zeros_like(dsb_ref)

        z = p_ref[:, GATE_Z]
        zz, t = _gelu_parts(z)
        u = zz[:, :D_CH]
        xh, rstd = _layer_norm_parts(zz[:, D_CH:])
        g = g_ref[...]
        gn = (xh * g + b_ref[...]).astype(BF16)
        dd = d_ref[:, D_CH:]
        du = dd * m_ref[...]
        dm = dd * u
        grp = _lane_group(BLOCK)
        tri = _tril_mask()
        ws = [jnp.where(tri, w_ref[gi], 0.0).astype(BF16) for gi in range(N_GROUPS)]
        gsel = (lax.broadcasted_iota(jnp.int32, (N_GROUPS, D_CH), 1) // HEAD_DIM
                == lax.broadcasted_iota(jnp.int32, (N_GROUPS, D_CH), 0)).astype(F32)
        for ch in range(tm // BLOCK):
            rows = slice(ch * BLOCK, (ch + 1) * BLOCK)
            dmc = dm[rows, :]
            dmb = dmc.astype(BF16)
            gc = gn[rows, :]
            dgn = jnp.zeros((BLOCK, D_CH), F32)
            for gi in range(N_GROUPS):
                r = lax.dot_general(ws[gi], dmb, (((0,), (0,)), ((), ())), preferred_element_type=F32)
                dgn = jnp.where(grp == gi, r, dgn)
                dmg = jnp.where(grp == gi, dmb, jnp.zeros_like(dmb))
                dwg = lax.dot_general(dmg, gc, (((1,), (1,)), ((), ())), preferred_element_type=F32)
                dw_ref[gi] += jnp.where(tri, dwg, 0.0)
            dsb_ref[...] += lax.dot_general(gsel, dmc, (((1,), (1,)), ((), ())), preferred_element_type=F32,
                                            precision=lax.Precision.HIGHEST)
            dgn_ref[rows, :] = dgn
        dgn = dgn_ref[...]
        db_ref[...] += jnp.sum(dgn, axis=0, keepdims=True)
        dg_ref[...] += jnp.sum(dgn * xh, axis=0, keepdims=True)
        dxh = dgn * g
        dgp = rstd * (dxh - jnp.mean(dxh, axis=-1, keepdims=True) - xh * jnp.mean(dxh * xh, axis=-1, keepdims=True))
        dgelu = 0.5 * (1.0 + t) + 0.5 * z * (1.0 - t * t) * GELU_C * (1.0 + 3.0 * GELU_K * z * z)
        dz_ref[:, 0:D_CH] = (du * dgelu[:, :D_CH]).astype(BF16)
        dz_ref[:, D_CH:] = (dgp * dgelu[:, D_CH:]).astype(BF16)

    s = proj.shape[0]
    tiled = [dmix, proj, mixed]
    consts = [ln_g, ln_b, w_sp]
    in_specs = [pl.BlockSpec((tm, a.shape[1]), lambda i: (i, 0)) for a in tiled]
    in_specs += [pl.BlockSpec(a.shape, lambda i, nd=a.ndim: (0,) * nd) for a in consts]
    vec = (1, D_CH)
    acc_shapes = [vec, vec, w_sp.shape, (N_GROUPS, BLOCK)]
    return pl.pallas_call(
        body, name=name, grid=(s // tm,), in_specs=in_specs,
        out_specs=[pl.BlockSpec((tm, 2 * D_CH), lambda i: (i, 0))]
        + [pl.BlockSpec(sh, lambda i, nd=len(sh): (0,) * nd) for sh in acc_shapes],
        out_shape=[jax.ShapeDtypeStruct((s, 2 * D_CH), BF16)] + [jax.ShapeDtypeStruct(sh, F32) for sh in acc_shapes],
        scratch_shapes=[pltpu.VMEM((tm, D_CH), F32)],
        compiler_params=_params("arbitrary"),
    )(*tiled, *consts)


def _adamw(w, g, m, v, name):
    rows, cols = w.shape
    tm = _tile(rows, 512, 8)

    def body(w_ref, g_ref, m_ref, v_ref, d_ref, nm_ref, nv_ref):
        gg = g_ref[...]
        nm = ADAM_B1 * m_ref[...] + (1.0 - ADAM_B1) * gg
        nv = ADAM_B2 * v_ref[...] + (1.0 - ADAM_B2) * (gg * gg)
        m_hat = nm / (1.0 - ADAM_B1 ** ADAM_STEP)
        v_hat = nv / (1.0 - ADAM_B2 ** ADAM_STEP)
        d_ref[...] = -ADAM_LR * (m_hat / (jnp.sqrt(v_hat) + ADAM_EPS) + ADAM_WD * w_ref[...])
        nm_ref[...] = nm
        nv_ref[...] = nv

    return _rows(body, name, tm, [w, g, m, v], [], [(cols, F32)] * 3)


def _pair_add(a, b, name):
    n, rows, cols = a.shape
    tm = _tile(rows, 512, 16)

    def body(a_ref, b_ref, o_ref):
        o_ref[...] = (a_ref[...] + b_ref[...]).astype(BF16)

    spec = pl.BlockSpec((1, tm, cols), lambda k, i: (k, i, 0))
    return pl.pallas_call(body, name=name, grid=(n, rows // tm), in_specs=[spec, spec], out_specs=spec,
                          out_shape=jax.ShapeDtypeStruct(a.shape, BF16), compiler_params=_params("parallel", "parallel"))(a, b)


def _ordered_sum(parts, name):
    n, rows, cols = parts.shape
    tm = _tile(rows, 512, 16 if parts.dtype == BF16 else 8)

    def body(p_ref, o_ref):
        acc = p_ref[0].astype(F32)
        for k in range(1, n):
            acc = acc + p_ref[k].astype(F32)
        o_ref[...] = acc

    return pl.pallas_call(body, name=name, grid=(rows // tm,),
                          in_specs=[pl.BlockSpec((n, tm, cols), lambda i: (0, i, 0))],
                          out_specs=pl.BlockSpec((tm, cols), lambda i: (i, 0)),
                          out_shape=jax.ShapeDtypeStruct((rows, cols), F32), compiler_params=_params("parallel"))(parts)


ANY = pl.BlockSpec(memory_space=pl.ANY)


def _position():
    x, y, c = lax.axis_index("x"), lax.axis_index("y"), lax.axis_index("c")
    other_chips = [(1 - x, y), (x, 1 - y), (1 - x, 1 - y)]
    return x, y, c, other_chips


def _remote(src, dst, send_sem, recv_sem, to):
    return pltpu.make_async_remote_copy(src_ref=src, dst_ref=dst, send_sem=send_sem, recv_sem=recv_sem,
                                        device_id=to, device_id_type=MESH)


def _gather_chips(shard, name):
    rows, cols = shard.shape
    half = rows // 2

    def body(in_ref, out_ref, send_sems, recv_sems, local_sem):
        x, y, c, chips = _position()
        me = 2 * x + y
        sibling = (x, y, 1 - c)

        def slab(chip, h):
            return out_ref.at[chip, pl.ds(h * half, half), :]

        mine = pltpu.make_async_copy(in_ref, out_ref.at[me], local_sem)
        mine.start()
        first = [_remote(in_ref.at[pl.ds(c * half, half), :], slab(me, c), send_sems.at[j], recv_sems.at[j], (cx, cy, c))
                 for j, (cx, cy) in enumerate(chips)]
        for cp in first:
            cp.start()
        passed = []
        for j, (cx, cy) in enumerate(chips):
            got = slab(2 * cx + cy, c)
            _remote(got, got, send_sems.at[j], recv_sems.at[j], sibling).wait_recv()
            cp = _remote(got, got, send_sems.at[3 + j], recv_sems.at[3 + j], sibling)
            cp.start()
            passed.append(cp)
        for j, (cx, cy) in enumerate(chips):
            got = slab(2 * cx + cy, 1 - c)
            _remote(got, got, send_sems.at[3 + j], recv_sems.at[3 + j], sibling).wait_recv()
        for cp in first + passed:
            cp.wait_send()
        mine.wait()

    return pl.pallas_call(
        body, name=name, in_specs=[ANY], out_specs=ANY,
        out_shape=jax.ShapeDtypeStruct((N_CHIPS, rows, cols), shard.dtype),
        scratch_shapes=[pltpu.SemaphoreType.DMA((6,)), pltpu.SemaphoreType.DMA((6,)), pltpu.SemaphoreType.DMA],
    )(shard)


def _gather_devices(block, name):
    rows, cols = block.shape

    def body(in_ref, out_ref, send_sems, recv_sems, local_sem):
        x, y, c, chips = _position()
        sibling = (x, y, 1 - c)

        def slot(px, py, pc):
            return out_ref.at[4 * px + 2 * py + pc]

        mine = pltpu.make_async_copy(in_ref, slot(x, y, c), local_sem)
        mine.start()
        first = [_remote(in_ref, slot(x, y, c), send_sems.at[0], recv_sems.at[0], sibling)]
        first += [_remote(in_ref, slot(x, y, c), send_sems.at[1 + j], recv_sems.at[1 + j], (cx, cy, c))
                  for j, (cx, cy) in enumerate(chips)]
        for cp in first:
            cp.start()
        passed = []
        for j, (cx, cy) in enumerate(chips):
            got = slot(cx, cy, c)
            _remote(got, got, send_sems.at[1 + j], recv_sems.at[1 + j], sibling).wait_recv()
            cp = _remote(got, got, send_sems.at[4 + j], recv_sems.at[4 + j], sibling)
            cp.start()
            passed.append(cp)
        got = slot(x, y, 1 - c)
        _remote(got, got, send_sems.at[0], recv_sems.at[0], sibling).wait_recv()
        for j, (cx, cy) in enumerate(chips):
            got = slot(cx, cy, 1 - c)
            _remote(got, got, send_sems.at[4 + j], recv_sems.at[4 + j], sibling).wait_recv()
        for cp in first + passed:
            cp.wait_send()
        mine.wait()

    return pl.pallas_call(
        body, name=name, in_specs=[ANY], out_specs=ANY,
        out_shape=jax.ShapeDtypeStruct((N_DEV, rows, cols), block.dtype),
        scratch_shapes=[pltpu.SemaphoreType.DMA((7,)), pltpu.SemaphoreType.DMA((7,)), pltpu.SemaphoreType.DMA],
    )(block)


def _sibling_swap(buf, name):
    def body(in_ref, out_ref, send_sem, recv_sem):
        x, y, c, _ = _position()
        cp = _remote(in_ref, out_ref, send_sem, recv_sem, (x, y, 1 - c))
        cp.start()
        cp.wait()

    return pl.pallas_call(body, name=name, in_specs=[ANY], out_specs=ANY,
                          out_shape=jax.ShapeDtypeStruct(buf.shape, buf.dtype),
                          scratch_shapes=[pltpu.SemaphoreType.DMA, pltpu.SemaphoreType.DMA])(buf)


def _chip_exchange(parts, name):
    def body(in_ref, out_ref, send_sems, recv_sems, local_sem):
        x, y, c, chips = _position()
        me = 2 * x + y
        mine = pltpu.make_async_copy(in_ref.at[me], out_ref.at[me], local_sem)
        mine.start()
        sent = [_remote(in_ref.at[2 * cx + cy], out_ref.at[me], send_sems.at[j], recv_sems.at[j], (cx, cy, c))
                for j, (cx, cy) in enumerate(chips)]
        for cp in sent:
            cp.start()
        for j, (cx, cy) in enumerate(chips):
            got = out_ref.at[2 * cx + cy]
            _remote(got, got, send_sems.at[j], recv_sems.at[j], (cx, cy, c)).wait_recv()
        for cp in sent:
            cp.wait_send()
        mine.wait()

    return pl.pallas_call(
        body, name=name, in_specs=[ANY], out_specs=ANY, out_shape=jax.ShapeDtypeStruct(parts.shape, parts.dtype),
        scratch_shapes=[pltpu.SemaphoreType.DMA((3,)), pltpu.SemaphoreType.DMA((3,)), pltpu.SemaphoreType.DMA],
    )(parts)


def _join_halves(mine, name):
    half, cols = mine.shape

    def body(in_ref, out_ref, send_sem, recv_sem, local_sem):
        x, y, c, _ = _position()
        here = out_ref.at[pl.ds(c * half, half), :]
        there = out_ref.at[pl.ds((1 - c) * half, half), :]
        local = pltpu.make_async_copy(in_ref, here, local_sem)
        local.start()
        cp = _remote(in_ref, here, send_sem, recv_sem, (x, y, 1 - c))
        cp.start()
        _remote(there, there, send_sem, recv_sem, (x, y, 1 - c)).wait_recv()
        cp.wait_send()
        local.wait()

    return pl.pallas_call(body, name=name, in_specs=[ANY], out_specs=ANY,
                          out_shape=jax.ShapeDtypeStruct((2 * half, cols), mine.dtype),
                          scratch_shapes=[pltpu.SemaphoreType.DMA, pltpu.SemaphoreType.DMA, pltpu.SemaphoreType.DMA])(mine)


def _shard_rows(name, shard):
    return shard.reshape(-1, D_MODEL)


def _pack_big(shards):
    return jnp.concatenate([shards[n].reshape(-1, D_MODEL) for n in BIG], axis=0)


def _unpack_big(packed, shard_shapes):
    out, r = {}, 0
    for n in BIG:
        cnt = math.prod(shard_shapes[n]) // D_MODEL
        out[n] = packed[r:r + cnt].reshape(shard_shapes[n])
        r += cnt
    return out


def _join_shards(stack, axis):
    moved = jnp.moveaxis(stack, 0, axis)
    shape = list(stack.shape[1:])
    shape[axis] *= N_CHIPS
    return moved.reshape(shape)


def _split_shards(full, axis):
    shape = list(full.shape)
    shape[axis:axis + 1] = [N_CHIPS, shape[axis] // N_CHIPS]
    return jnp.moveaxis(full.reshape(shape), axis, 0)


def _pad_rows(a, mult):
    extra = (-a.shape[0]) % mult
    return a if extra == 0 else jnp.pad(a, ((0, extra), (0, 0)))


def _pack_small(arrs, mult):
    rows = []
    for a in arrs:
        flat = a.reshape(-1)
        extra = (-flat.shape[0]) % LANES
        if extra:
            flat = jnp.pad(flat, (0, extra))
        rows.append(flat.reshape(-1, LANES))
    return _pad_rows(jnp.concatenate(rows, axis=0), mult)


def _unpack_small(packed, shapes):
    out, r = [], 0
    for sh in shapes:
        n = math.prod(sh)
        cnt = -(-n // LANES)
        out.append(packed[r:r + cnt].reshape(-1)[:n].reshape(sh))
        r += cnt
    return out


def _ffn_fwd(h, g_norm, w_gate, w_up, w_down, tag):
    n = _rms_fwd(h, g_norm, f"{tag}_norm")
    gate = _matmul(n, w_gate, name=f"{tag}_gate")
    up = _matmul(n, w_up, name=f"{tag}_up")
    act = _swiglu_fwd(gate, up, f"{tag}_act")
    out = _matmul(act, w_down, add=h, name=f"{tag}_down")
    return out, (n, gate, up, act)


def _ffn_bwd(dh, dhb, h_in, saved, g_norm, w_gate, w_up, w_down, tag):
    n, gate, up, act = saved
    dact = _matmul(dhb, w_down, trans_b=True, name=f"{tag}_dact")
    dgate, dup = _swiglu_bwd(dact, gate, up, f"{tag}_dswiglu")
    dw_down = _matmul(act, dhb, trans_a=True, name=f"{tag}_dwdown")
    dw_gate = _matmul(n, dgate, trans_a=True, name=f"{tag}_dwgate")
    dw_up = _matmul(n, dup, trans_a=True, name=f"{tag}_dwup")
    dn = _matmul(dgate, w_gate, trans_b=True, name=f"{tag}_dn_gate")
    dn = _matmul(dup, w_up, trans_b=True, add=dn, name=f"{tag}_dn_up")
    dh_in, dh_inb, dg = _rms_bwd(dn, h_in, g_norm, dh, f"{tag}_dnorm")
    return dh_in, dh_inb, dg, dw_gate, dw_up, dw_down


def _local_step(x, tgt, w):
    s = x.shape[0]
    tabs = _rope_tables(s)
    grads = {}

    g_ev = w['ev_norm_g']
    n1 = _rms_fwd(x, g_ev, "ev_norm")
    proj0 = _matmul(n1, w['ev_w_in'][0], name="ev_in")
    q0, k0, v0 = _qkv_prep(proj0, tabs, 512, 128, "ev_qkv")
    q0f, k0f, v0f = _fold(q0, 8, 1, False), _fold(k0, 2, 1, True), _fold(v0, 2, 1, True)
    sinks = w['ev_sinks'].reshape(-1)
    o0, lse0 = _attn_fwd(q0f, k0f, v0f, sinks, max_dist=BLOCK - 1, group=4, name="ev_attn")
    yconv, cout = _conv_fwd(proj0, w['ev_conv_w'][0], w['ev_conv_b'], w['ev_conv_ln_g'], w['ev_conv_ln_b'], "ev_conv")
    mix0 = jnp.concatenate([_unfold(o0, 8, 1).astype(BF16), cout], axis=1)
    h1 = _matmul(mix0, w['ev_w_out'][0], add=x, name="ev_out")

    g_f0 = w['ffn_norm_g'][0:1]
    h2, ffn0 = _ffn_fwd(h1, g_f0, w['ffn_w_gate'][0], w['ffn_w_up'][0], w['ffn_w_down'][0], "ffn0")

    g_od = w['od_norm_g']
    n3 = _rms_fwd(h2, g_od, "od_norm")
    proj1 = _matmul(n3, w['od_w_in'][0], name="od_in")
    q1, k1, v1 = _qkv_prep(proj1, tabs, 512, 512, "od_qkv")
    folded, outs, lses = [], [], []
    for d in DILATIONS:
        qf, kf, vf = _fold(q1, 8, d, False), _fold(k1, 8, d, True), _fold(v1, 8, d, True)
        o_r, lse_r = _attn_fwd(qf, kf, vf, None, max_dist=BLOCK, group=1, name=f"od_attn{d}")
        folded.append((qf, kf, vf))
        outs.append(_unrefold(o_r, 8, d))
        lses.append(_unrefold(lse_r, 8, d))
    c_hm, lse_hm = _combine(outs, lses, "od_combine")
    c_tok = _unfold(c_hm, 8, 1)
    w_sp = w['od_spatial_w'][0]
    sb_t = w['od_spatial_b'][0].T
    mixed, dout = _gate_fwd(proj1, w['od_sgu_ln_g'], w['od_sgu_ln_b'], w_sp, sb_t, "od_gate")
    mix1 = jnp.concatenate([c_tok.astype(BF16), dout], axis=1)
    h3 = _matmul(mix1, w['od_w_out'][0], add=h2, name="od_out")

    g_f1 = w['ffn_norm_g'][1:2]
    h4, ffn1 = _ffn_fwd(h3, g_f1, w['ffn_w_gate'][1], w['ffn_w_up'][1], w['ffn_w_down'][1], "ffn1")

    dh4, dh4b, dg_final, loss_tile = _final_loss(h4, w['final_norm_g'].reshape(1, D_MODEL), tgt, "final")
    grads['final_norm_g'] = dg_final.reshape(D_MODEL)

    dh3, dh3b, dg_f1, dwg1, dwu1, dwd1 = _ffn_bwd(dh4, dh4b, h3, ffn1, g_f1, w['ffn_w_gate'][1], w['ffn_w_up'][1],
                                                   w['ffn_w_down'][1], "ffn1")

    dmix1 = _matmul(dh3b, w['od_w_out'][0], trans_b=True, name="od_dmix")
    grads['od_w_out'] = _matmul(mix1, dh3b, trans_a=True, name="od_dwout")[None]
    do_tok = dmix1[:, :512]
    dqs, dks, dvs = [], [], []
    for d, (qf, kf, vf) in zip(DILATIONS, folded):
        dq_r, dk_r, dv_r = _attn_bwd(qf, kf, vf, _fold(do_tok, 8, d, False), _fold(c_tok, 8, d, False),
                                     _refold(lse_hm, d), None, max_dist=BLOCK, group=1, name=f"od_dattn{d}")
        dqs.append(_unfold(dq_r, 8, d))
        dks.append(_unfold(dk_r[:, BLOCK:], 8, d))
        dvs.append(_unfold(dv_r[:, BLOCK:], 8, d))
    dz, dg_sgu, db_sgu, dw_sp, dsb = _gate_bwd(dmix1, proj1, mixed, w['od_sgu_ln_g'], w['od_sgu_ln_b'], w_sp, "od_dgate")
    grads['od_sgu_ln_g'], grads['od_sgu_ln_b'] = dg_sgu, db_sgu
    grads['od_spatial_w'], grads['od_spatial_b'] = dw_sp[None], dsb[None]
    dproj1 = _qkv_post(dqs, dks, dvs, dz, tabs, "od_dproj")
    grads['od_w_in'] = _matmul(n3, dproj1, trans_a=True, name="od_dwin")[None]
    dn3 = _matmul(dproj1, w['od_w_in'][0], trans_b=True, name="od_dn")
    dh2, dh2b, dg_od = _rms_bwd(dn3, h2, g_od, dh3, "od_dnorm")
    grads['od_norm_g'] = dg_od

    dh1, dh1b, dg_f0, dwg0, dwu0, dwd0 = _ffn_bwd(dh2, dh2b, h1, ffn0, g_f0, w['ffn_w_gate'][0], w['ffn_w_up'][0],
                                                   w['ffn_w_down'][0], "ffn0")
    grads['ffn_norm_g'] = jnp.concatenate([dg_f0, dg_f1], axis=0)
    grads['ffn_w_gate'] = jnp.stack([dwg0, dwg1])
    grads['ffn_w_up'] = jnp.stack([dwu0, dwu1])
    grads['ffn_w_down'] = jnp.stack([dwd0, dwd1])

    dmix0 = _matmul(dh1b, w['ev_w_out'][0], trans_b=True, name="ev_dmix")
    grads['ev_w_out'] = _matmul(mix0, dh1b, trans_a=True, name="ev_dwout")[None]
    dq0, dk0, dv0, dsink = _attn_bwd(q0f, k0f, v0f, _fold(dmix0[:, :512], 8, 1, False), o0, lse0, sinks,
                                     max_dist=BLOCK - 1, group=4, name="ev_dattn")
    grads['ev_sinks'] = dsink[:, 0, 0].reshape(1, 8)
    dyc, dg_cln, db_cln, dcb = _conv_tail_bwd(dmix0, yconv, w['ev_conv_ln_g'], w['ev_conv_ln_b'], "ev_dconv_tail")
    grads['ev_conv_ln_g'], grads['ev_conv_ln_b'], grads['ev_conv_b'] = dg_cln, db_cln, dcb
    dglu, dconv_w = _conv_bwd(proj0, dyc, w['ev_conv_w'][0], "ev_dconv")
    grads['ev_conv_w'] = dconv_w[None]
    dproj0 = _qkv_post([_unfold(dq0, 8, 1)], [_unfold(dk0[:, BLOCK:], 2, 1)], [_unfold(dv0[:, BLOCK:], 2, 1)],
                       dglu, tabs, "ev_dproj")
    grads['ev_w_in'] = _matmul(n1, dproj0, trans_a=True, name="ev_dwin")[None]
    dn1 = _matmul(dproj0, w['ev_w_in'][0], trans_b=True, name="ev_dn")
    dx, _, dg_ev = _rms_bwd(dn1, x, g_ev, dh1, "ev_dnorm")
    grads['ev_norm_g'] = dg_ev
    return loss_tile, dx, grads


def kernel(x, ev_norm_g, ev_w_in, ev_sinks, ev_conv_w, ev_conv_b, ev_conv_ln_g, ev_conv_ln_b, ev_w_out, od_norm_g, od_w_in, od_sgu_ln_g, od_sgu_ln_b, od_spatial_w, od_spatial_b, od_w_out, ffn_norm_g, ffn_w_gate, ffn_w_up, ffn_w_down, final_norm_g, loss_target, m_ev_norm_g, m_ev_w_in, m_ev_sinks, m_ev_conv_w, m_ev_conv_b, m_ev_conv_ln_g, m_ev_conv_ln_b, m_ev_w_out, m_od_norm_g, m_od_w_in, m_od_sgu_ln_g, m_od_sgu_ln_b, m_od_spatial_w, m_od_spatial_b, m_od_w_out, m_ffn_norm_g, m_ffn_w_gate, m_ffn_w_up, m_ffn_w_down, m_final_norm_g, v_ev_norm_g, v_ev_w_in, v_ev_sinks, v_ev_conv_w, v_ev_conv_b, v_ev_conv_ln_g, v_ev_conv_ln_b, v_ev_w_out, v_od_norm_g, v_od_w_in, v_od_sgu_ln_g, v_od_sgu_ln_b, v_od_spatial_w, v_od_spatial_b, v_od_w_out, v_ffn_norm_g, v_ffn_w_gate, v_ffn_w_up, v_ffn_w_down, v_final_norm_g):
    given = dict(locals())
    wts = {n: given[n] for n in WEIGHTS}
    mom = {n: given["m_" + n] for n in WEIGHTS}
    var = {n: given["v_" + n] for n in WEIGHTS}
    chip = 2 * lax.axis_index("x") + lax.axis_index("y")
    core = lax.axis_index("c")

    shard_shapes = {n: wts[n].shape for n in BIG}
    packed_w = _pack_big({n: wts[n].astype(BF16) for n in BIG})
    all_w = _gather_chips(packed_w, "gather_weights")
    full = dict(wts)
    r = 0
    for n, axis in BIG.items():
        cnt = math.prod(shard_shapes[n]) // D_MODEL
        stack = all_w[:, r:r + cnt].reshape((N_CHIPS,) + shard_shapes[n])
        full[n] = _join_shards(stack, axis)
        r += cnt
    small_shards = [wts[n] for n in SMALL_SHARDED]
    small_shapes = [a.shape for a in small_shards]
    all_s = _gather_chips(_pack_small(small_shards, 16), "gather_small_weights")
    per_chip = [_unpack_small(all_s[k], small_shapes) for k in range(N_CHIPS)]
    for i, n in enumerate(SMALL_SHARDED):
        full[n] = jnp.concatenate([per_chip[k][i] for k in range(N_CHIPS)], axis=-1)

    loss_tile, grad_x, grads = _local_step(x[0], loss_target[0], full)
    loss = lax.psum(loss_tile[0, 0], ("x", "y", "c"))

    gpack = jnp.concatenate([_split_shards(grads[n], axis).reshape(N_CHIPS, -1, D_MODEL) for n, axis in BIG.items()],
                            axis=1)
    half = gpack.shape[1] // 2
    keep = lax.dynamic_slice_in_dim(gpack, core * half, half, axis=1)
    give = lax.dynamic_slice_in_dim(gpack, (1 - core) * half, half, axis=1)
    got = _sibling_swap(give, "grad_pair_swap")
    chip_part = _pair_add(keep, got, "grad_pair_add")
    from_chips = _chip_exchange(chip_part, "grad_chip_exchange")
    my_half = _ordered_sum(from_chips, "grad_chip_sum")
    g_big = _unpack_big(_join_halves(my_half, "grad_join_halves"), shard_shapes)

    small_names = SMALL_REPL + SMALL_SHARDED
    small_full_shapes = [grads[n].shape for n in small_names]
    spack = _pack_small([grads[n] for n in small_names], 8)
    s_all = _gather_devices(spack, "grad_small_gather")
    s_sum = _unpack_small(_ordered_sum(s_all, "grad_small_sum"), small_full_shapes)
    g_small = dict(zip(small_names, s_sum))
    for n in SMALL_SHARDED:
        width = wts[n].shape[-1]
        g_small[n] = lax.dynamic_slice_in_dim(g_small[n], chip * width, width, axis=g_small[n].ndim - 1)

    g_all = {**g_big, **g_small}
    d_big, m_big, v_big = _adamw(_pack_big(wts), _pack_big(g_all), _pack_big(mom), _pack_big(var), "adamw_big")
    delta = _unpack_big(d_big, shard_shapes)
    new_m = _unpack_big(m_big, shard_shapes)
    new_v = _unpack_big(v_big, shard_shapes)
    shapes = [wts[n].shape for n in small_names]
    d_s, m_s, v_s = _adamw(*[_pack_small([src[n] for n in small_names], 8) for src in (wts, g_all, mom, var)], "adamw_small")
    for dst, packed in ((delta, d_s), (new_m, m_s), (new_v, v_s)):
        dst.update(zip(small_names, _unpack_small(packed, shapes)))

    return (loss, grad_x[None], *[g_all[n] for n in WEIGHTS], *[delta[n] for n in WEIGHTS],
            *[new_m[n] for n in WEIGHTS], *[new_v[n] for n in WEIGHTS])
```

```python
import math

import jax
import jax.numpy as jnp
from jax import lax
from jax.experimental import pallas as pl
from jax.experimental.pallas import tpu as pltpu

F32 = jnp.float32
BF16 = jnp.bfloat16

D_MODEL = 1024
HEAD_DIM = 64
ROT_DIM = 16
ROPE_THETA = 500000.0
RMS_EPS = 1e-6
LN_EPS = 1e-5
BLOCK = 128
CONV_WIDTH = 31
CONV_HALO = 32
D_FF = 2816
N_GROUPS = 8
ATTN_W = 512
ATTN_SCALE = HEAD_DIM ** -0.5
NEG = -1e30
DILATIONS = (1, 4, 16)

ADAM_LR = 0.001
ADAM_B1 = 0.9
ADAM_B2 = 0.999
ADAM_EPS = 1e-08
ADAM_WD = 0.01
ADAM_STEP = 10

LANES = 128
N_PAIRS = ATTN_W // LANES
VMEM_LIMIT = 56 * 1024 * 1024
MESH = pl.DeviceIdType.MESH
N_CHIPS = 4
N_DEV = 8

WEIGHTS = ['ev_norm_g', 'ev_w_in', 'ev_sinks', 'ev_conv_w', 'ev_conv_b', 'ev_conv_ln_g', 'ev_conv_ln_b', 'ev_w_out',
           'od_norm_g', 'od_w_in', 'od_sgu_ln_g', 'od_sgu_ln_b', 'od_spatial_w', 'od_spatial_b', 'od_w_out',
           'ffn_norm_g', 'ffn_w_gate', 'ffn_w_up', 'ffn_w_down', 'final_norm_g']
BIG = [('ev_w_in', 0, True), ('ev_w_out', 0, False), ('od_w_in', 0, True), ('od_w_out', 0, False),
       ('ffn_w_gate', 0, True), ('ffn_w_gate', 1, True), ('ffn_w_up', 0, True), ('ffn_w_up', 1, True),
       ('ffn_w_down', 0, False), ('ffn_w_down', 1, False)]
BIG_NAMES = ['ev_w_in', 'ev_w_out', 'od_w_in', 'od_w_out', 'ffn_w_gate', 'ffn_w_up', 'ffn_w_down']
SMALL_SHARDED = ['ev_conv_w', 'od_norm_g', 'od_sgu_ln_g', 'od_sgu_ln_b']
SMALL_REPL = ['ev_norm_g', 'ev_sinks', 'ev_conv_b', 'ev_conv_ln_g', 'ev_conv_ln_b', 'od_spatial_w', 'od_spatial_b',
              'ffn_norm_g', 'final_norm_g']


def _tile(n, cap, mult=LANES):
    best = None
    for t in range(mult, min(n, cap) + 1, mult):
        if n % t == 0:
            best = t
    assert best is not None, (n, cap)
    return best


def _params(*sem):
    return pltpu.CompilerParams(dimension_semantics=sem, vmem_limit_bytes=VMEM_LIMIT)


def _sigmoid(x):
    return 1.0 / (1.0 + jnp.exp(-x))


def _pair_block(p):
    return slice(p * LANES, (p + 1) * LANES)


def _matmul(a, b, *, name, trans_a=False, trans_b=False, add=None, out_dtype=F32):
    parts = a if isinstance(a, (tuple, list)) else (a,)
    if trans_a:
        k, m = parts[0].shape
    else:
        m = parts[0].shape[0]
        k = sum(p.shape[1] for p in parts)
    if trans_b:
        n, k2 = b.shape
    else:
        k2, n = b.shape
    assert k == k2 and b.dtype == BF16 and all(p.dtype == BF16 for p in parts)
    tm = _tile(m, 512)
    tn = _tile(n, D_FF // 2)
    tk = k if k <= D_FF else _tile(k, 1024)
    nk = k // tk
    na = len(parts)
    assert na == 1 or (nk == 1 and not trans_a)
    dims = (((0 if trans_a else 1,), (1 if trans_b else 0,)), ((), ()))
    has_add = add is not None

    def body(*refs):
        a_refs, b_ref = refs[:na], refs[na]
        add_ref = refs[na + 1] if has_add else None
        o_ref = refs[na + 1 + has_add]
        a_val = a_refs[0][...] if na == 1 else jnp.concatenate([r[...] for r in a_refs], axis=1)
        part = lax.dot_general(a_val, b_ref[...], dims, preferred_element_type=F32)
        if nk == 1:
            if has_add:
                part = part + add_ref[...]
            o_ref[...] = part.astype(o_ref.dtype)
            return
        acc_ref = refs[-1]
        kk = pl.program_id(2)

        @pl.when(kk == 0)
        def _():
            acc_ref[...] = part

        @pl.when(kk > 0)
        def _():
            acc_ref[...] += part

        @pl.when(kk == nk - 1)
        def _():
            res = acc_ref[...]
            if has_add:
                res = res + add_ref[...]
            o_ref[...] = res.astype(o_ref.dtype)

    if trans_a:
        a_specs = [pl.BlockSpec((tk, tm), lambda i, j, kk: (kk, i))]
    elif na == 1:
        a_specs = [pl.BlockSpec((tm, tk), lambda i, j, kk: (i, kk))]
    else:
        a_specs = [pl.BlockSpec((tm, p.shape[1]), lambda i, j, kk: (i, 0)) for p in parts]
    b_spec = pl.BlockSpec((tn, tk), lambda i, j, kk: (j, kk)) if trans_b else pl.BlockSpec((tk, tn), lambda i, j, kk: (kk, j))
    o_spec = pl.BlockSpec((tm, tn), lambda i, j, kk: (i, j))
    in_specs = a_specs + [b_spec] + ([o_spec] if has_add else [])
    operands = list(parts) + [b] + ([add] if has_add else [])
    return pl.pallas_call(
        body, name=name, grid=(m // tm, n // tn, nk), in_specs=in_specs, out_specs=o_spec,
        out_shape=jax.ShapeDtypeStruct((m, n), out_dtype),
        scratch_shapes=[pltpu.VMEM((tm, tn), F32)] if nk > 1 else [],
        compiler_params=_params("parallel", "parallel", "arbitrary"),
    )(*operands)


def _matmul_tn_pair(a1, a2, b, name):
    kdim, m1 = a1.shape
    m2 = a2.shape[1]
    n = b.shape[1]
    tn = _tile(n, 1024)
    tk = _tile(kdim, 1024)
    nk = kdim // tk
    dims = (((0,), (0,)), ((), ()))

    def body(a1_ref, a2_ref, b_ref, o_ref):
        kk = pl.program_id(1)
        bv = b_ref[...]
        top = lax.dot_general(a1_ref[...], bv, dims, preferred_element_type=F32)
        bot = lax.dot_general(a2_ref[...], bv, dims, preferred_element_type=F32)

        @pl.when(kk == 0)
        def _():
            o_ref[0:m1, :] = top
            o_ref[m1:, :] = bot

        @pl.when(kk > 0)
        def _():
            o_ref[0:m1, :] += top
            o_ref[m1:, :] += bot

    return pl.pallas_call(
        body, name=name, grid=(n // tn, nk),
        in_specs=[pl.BlockSpec((tk, m1), lambda j, kk: (kk, 0)), pl.BlockSpec((tk, m2), lambda j, kk: (kk, 0)),
                  pl.BlockSpec((tk, tn), lambda j, kk: (kk, j))],
        out_specs=pl.BlockSpec((m1 + m2, tn), lambda j, kk: (0, j)),
        out_shape=jax.ShapeDtypeStruct((m1 + m2, n), F32),
        compiler_params=_params("parallel", "arbitrary"),
    )(a1, a2, b)


def _rows(body, name, tm, tiled, consts, outs, accs=()):
    s = tiled[0].shape[0]
    assert s % tm == 0
    in_specs = [pl.BlockSpec((tm, a.shape[1]), lambda i: (i, 0)) for a in tiled]
    in_specs += [pl.BlockSpec(a.shape, lambda i, nd=a.ndim: (0,) * nd) for a in consts]
    out_shape = [jax.ShapeDtypeStruct((s, c), dt) for c, dt in outs]
    out_shape += [jax.ShapeDtypeStruct(sh, dt) for sh, dt in accs]
    out_specs = [pl.BlockSpec((tm, c), lambda i: (i, 0)) for c, _ in outs]
    out_specs += [pl.BlockSpec(sh, lambda i, nd=len(sh): (0,) * nd) for sh, _ in accs]
    return pl.pallas_call(
        body, name=name, grid=(s // tm,), in_specs=in_specs, out_specs=out_specs, out_shape=out_shape,
        compiler_params=_params("arbitrary"),
    )(*tiled, *consts)


def _first_step():
    return pl.program_id(0) == 0


def _rms_fwd(h, g, name):
    def body(h_ref, g_ref, n_ref):
        x = h_ref[...]
        r = lax.rsqrt(jnp.mean(x * x, axis=-1, keepdims=True) + RMS_EPS)
        n_ref[...] = (x * r * g_ref[...]).astype(BF16)

    return _rows(body, name, 512, [h], [g], [(D_MODEL, BF16)])[0]


def _rms_bwd(dn, h, g, dres, name):
    def body(dn_ref, h_ref, dres_ref, g_ref, dh_ref, dhb_ref, dg_ref):
        @pl.when(_first_step())
        def _():
            dg_ref[...] = jnp.zeros_like(dg_ref)

        x = h_ref[...]
        r = lax.rsqrt(jnp.mean(x * x, axis=-1, keepdims=True) + RMS_EPS)
        xh = x * r
        dy = dn_ref[...]
        dg_ref[...] += jnp.sum(dy * xh, axis=0, keepdims=True)
        dxh = dy * g_ref[...]
        tot = dres_ref[...] + r * (dxh - xh * jnp.mean(dxh * xh, axis=-1, keepdims=True))
        dh_ref[...] = tot
        dhb_ref[...] = tot.astype(BF16)

    return _rows(body, name, 512, [dn, h, dres], [g], [(D_MODEL, F32), (D_MODEL, BF16)], [((1, D_MODEL), F32)])


def _final_loss(h, g, tgt, name):
    def body(h_ref, t_ref, g_ref, dh_ref, dhb_ref, dg_ref, loss_ref):
        @pl.when(_first_step())
        def _():
            dg_ref[...] = jnp.zeros_like(dg_ref)
            loss_ref[...] = jnp.zeros_like(loss_ref)

        x = h_ref[...]
        r = lax.rsqrt(jnp.mean(x * x, axis=-1, keepdims=True) + RMS_EPS)
        xh = x * r
        gg = g_ref[...]
        e = xh * gg - t_ref[...]
        loss_ref[...] += (0.5 / D_MODEL) * jnp.sum(jnp.sum(e * e, axis=-1, keepdims=True), axis=0, keepdims=True)
        dy = e * (1.0 / D_MODEL)
        dg_ref[...] += jnp.sum(dy * xh, axis=0, keepdims=True)
        dxh = dy * gg
        dx = r * (dxh - xh * jnp.mean(dxh * xh, axis=-1, keepdims=True))
        dh_ref[...] = dx
        dhb_ref[...] = dx.astype(BF16)

    return _rows(body, name, 512, [h, tgt], [g], [(D_MODEL, F32), (D_MODEL, BF16)],
                 [((1, D_MODEL), F32), ((1, LANES), F32)])


def _swiglu_fwd(gate, up, name):
    def body(g_ref, u_ref, a_ref):
        g = g_ref[...]
        a_ref[...] = (g * _sigmoid(g) * u_ref[...]).astype(BF16)

    return _rows(body, name, 256, [gate, up], [], [(D_FF, BF16)])[0]


def _swiglu_bwd(dact, gate, up, name):
    def body(d_ref, g_ref, u_ref, dg_ref, du_ref):
        g = g_ref[...]
        d = d_ref[...]
        sg = _sigmoid(g)
        dg_ref[...] = (d * u_ref[...] * sg * (1.0 + g * (1.0 - sg))).astype(BF16)
        du_ref[...] = (d * g * sg).astype(BF16)

    return _rows(body, name, 256, [dact, gate, up], [], [(D_FF, BF16), (D_FF, BF16)])


def _rope_tables(s):
    half = ROT_DIM // 2
    inv_freq = ROPE_THETA ** (-jnp.arange(half, dtype=F32) * (2.0 / ROT_DIM))
    ang = jnp.arange(s, dtype=F32)[:, None] * inv_freq[None, :]
    cos, sin = jnp.cos(ang), jnp.sin(ang)
    rest = HEAD_DIM - ROT_DIM
    ones = jnp.ones((s, rest), F32)
    zeros = jnp.zeros((s, rest), F32)
    zh = jnp.zeros((s, half), F32)
    c_t = jnp.concatenate([cos, cos, ones], axis=1)
    a_t = jnp.concatenate([-sin, zh, zeros], axis=1)
    b_t = jnp.concatenate([zh, sin, zeros], axis=1)
    return tuple(jnp.tile(t, (1, LANES // HEAD_DIM)) for t in (c_t, a_t, b_t))


def _rot(x, c, a, b):
    w = x.shape[1]
    half = ROT_DIM // 2
    return x * c + pltpu.roll(x, w - half, 1) * a + pltpu.roll(x, half, 1) * b


def _wide(t, w):
    return t if w == LANES else jnp.tile(t, (1, w // LANES))


def _low_lanes(rows):
    return lax.broadcasted_iota(jnp.int32, (rows, LANES), 1) < HEAD_DIM


def _fold_store(x, sc_ref, out_refs):
    tm = x.shape[0]
    if any(d > 1 for d in out_refs):
        for p in range(N_PAIRS):
            sc_ref[p] = x[:, _pair_block(p)]
    for d, o_ref in out_refs.items():
        if d == 1:
            o_ref[0] = x.astype(o_ref.dtype)
            continue
        for r in range(d):
            for p in range(N_PAIRS):
                o_ref[r, :, _pair_block(p)] = sc_ref[p, pl.ds(r, tm // d, stride=d), :].astype(o_ref.dtype)


def _unfold_load(x_ref, sc_ref, d, add=False):
    n = x_ref.shape[1]
    for r in range(d):
        for p in range(N_PAIRS):
            rows = pl.ds(r, n, stride=d) if d > 1 else slice(None)
            val = x_ref[r, :, _pair_block(p)].astype(F32)
            if add:
                val = val + sc_ref[p, rows, :]
            sc_ref[p, rows, :] = val


def _folded_spec(d, tm, w=ATTN_W):
    return pl.BlockSpec((d, tm // d, w), lambda i: (0, i, 0))


def _folded_shape(s, d, dtype, w=ATTN_W):
    return jax.ShapeDtypeStruct((d, s // d, w), dtype)


def _qkv_prep_even(proj, tabs, name):
    s = proj.shape[0]
    tm = 512

    def body(p_ref, c_ref, a_ref, b_ref, q_ref, k_ref, v_ref):
        c, a, b = c_ref[...], a_ref[...], b_ref[...]
        q_ref[0] = _rot(p_ref[:, 0:ATTN_W], _wide(c, ATTN_W), _wide(a, ATTN_W), _wide(b, ATTN_W)).astype(BF16)
        lo = _low_lanes(tm)
        for src, o_ref in ((_rot(p_ref[:, 512:640], c, a, b), k_ref), (p_ref[:, 640:768], v_ref)):
            swapped = pltpu.roll(src, HEAD_DIM, 1)
            o_ref[0, :, 0:LANES] = jnp.where(lo, src, swapped).astype(BF16)
            o_ref[0, :, LANES:] = jnp.where(lo, swapped, src).astype(BF16)

    row = lambda w: pl.BlockSpec((tm, w), lambda i: (i, 0))
    return pl.pallas_call(
        body, name=name, grid=(s // tm,), in_specs=[row(proj.shape[1]), row(LANES), row(LANES), row(LANES)],
        out_specs=[_folded_spec(1, tm), _folded_spec(1, tm, 2 * LANES), _folded_spec(1, tm, 2 * LANES)],
        out_shape=[_folded_shape(s, 1, BF16), _folded_shape(s, 1, BF16, 2 * LANES), _folded_shape(s, 1, BF16, 2 * LANES)],
        compiler_params=_params("parallel"),
    )(proj, *tabs)


def _qkv_post_even(dq, dk, dv, dglu, tabs, name):
    s = dglu.shape[0]
    tm = 512

    def body(dq_ref, dk_ref, dv_ref, dr_ref, c_ref, a_ref, b_ref, o_ref):
        c, a, b = c_ref[...], -a_ref[...], -b_ref[...]
        o_ref[:, 0:ATTN_W] = _rot(dq_ref[0], _wide(c, ATTN_W), _wide(a, ATTN_W), _wide(b, ATTN_W)).astype(BF16)
        lo = _low_lanes(tm)
        merged = []
        for ref in (dk_ref, dv_ref):
            first, second = ref[0, :, 0:LANES], ref[0, :, LANES:]
            merged.append(jnp.where(lo, first + pltpu.roll(first, HEAD_DIM, 1), second + pltpu.roll(second, HEAD_DIM, 1)))
        o_ref[:, 512:640] = _rot(merged[0], c, a, b).astype(BF16)
        o_ref[:, 640:768] = merged[1].astype(BF16)
        o_ref[:, 768:] = dr_ref[...]

    row = lambda w: pl.BlockSpec((tm, w), lambda i: (i, 0))
    return pl.pallas_call(
        body, name=name, grid=(s // tm,),
        in_specs=[_folded_spec(1, tm), _folded_spec(1, tm, 2 * LANES), _folded_spec(1, tm, 2 * LANES),
                  row(dglu.shape[1]), row(LANES), row(LANES), row(LANES)],
        out_specs=row(EVEN_IN), out_shape=jax.ShapeDtypeStruct((s, EVEN_IN), BF16),
        compiler_params=_params("parallel"),
    )(dq, dk, dv, dglu, *tabs)


def _qkv_prep_odd(proj, tabs, name):
    s = proj.shape[0]
    tm = 512

    def body(p_ref, c_ref, a_ref, b_ref, *rest):
        outs, sc_ref = rest[:-1], rest[-1]
        c, a, b = (_wide(t[...], ATTN_W) for t in (c_ref, a_ref, b_ref))
        for t in range(3):
            x = p_ref[:, t * ATTN_W:(t + 1) * ATTN_W]
            if t < 2:
                x = _rot(x, c, a, b)
            _fold_store(x, sc_ref, {d: outs[t * len(DILATIONS) + i] for i, d in enumerate(DILATIONS)})

    row = lambda w: pl.BlockSpec((tm, w), lambda i: (i, 0))
    return pl.pallas_call(
        body, name=name, grid=(s // tm,), in_specs=[row(proj.shape[1]), row(LANES), row(LANES), row(LANES)],
        out_specs=[_folded_spec(d, tm) for _ in range(3) for d in DILATIONS],
        out_shape=[_folded_shape(s, d, BF16) for _ in range(3) for d in DILATIONS],
        scratch_shapes=[pltpu.VMEM((N_PAIRS, tm, LANES), F32)],
        compiler_params=_params("parallel"),
    )(proj, *tabs)


def _qkv_post_odd(dqs, dks, dvs, dz, tabs, name):
    s = dz.shape[0]
    tm = 256
    nb = len(DILATIONS)

    def body(*refs):
        groups = (refs[:nb], refs[nb:2 * nb], refs[2 * nb:3 * nb])
        dz_ref, c_ref, a_ref, b_ref, o_ref, sc_ref = refs[3 * nb:]
        c, a, b = _wide(c_ref[...], ATTN_W), _wide(-a_ref[...], ATTN_W), _wide(-b_ref[...], ATTN_W)
        for t, group in enumerate(groups):
            for i, d in enumerate(DILATIONS):
                _unfold_load(group[i], sc_ref, d, add=i > 0)
            x = jnp.concatenate([sc_ref[p] for p in range(N_PAIRS)], axis=1)
            if t < 2:
                x = _rot(x, c, a, b)
            o_ref[:, t * ATTN_W:(t + 1) * ATTN_W] = x.astype(BF16)
        o_ref[:, 3 * ATTN_W:] = dz_ref[...]

    row = lambda w: pl.BlockSpec((tm, w), lambda i: (i, 0))
    return pl.pallas_call(
        body, name=name, grid=(s // tm,),
        in_specs=[_folded_spec(d, tm) for _ in range(3) for d in DILATIONS] + [row(dz.shape[1]), row(LANES), row(LANES), row(LANES)],
        out_specs=row(ODD_IN), out_shape=jax.ShapeDtypeStruct((s, ODD_IN), BF16),
        scratch_shapes=[pltpu.VMEM((N_PAIRS, tm, LANES), F32)],
        compiler_params=_params("parallel"),
    )(*dqs, *dks, *dvs, dz, *tabs)


def _fold_dout(dmix, name):
    s = dmix.shape[0]
    tm = 512
    ds = [d for d in DILATIONS if d > 1]

    def body(d_ref, *rest):
        outs, sc_ref = rest[:-1], rest[-1]
        _fold_store(d_ref[...], sc_ref, dict(zip(ds, outs)))

    return pl.pallas_call(
        body, name=name, grid=(s // tm,), in_specs=[pl.BlockSpec((tm, ATTN_W), lambda i: (i, 0))],
        out_specs=[_folded_spec(d, tm) for d in ds], out_shape=[_folded_shape(s, d, BF16) for d in ds],
        scratch_shapes=[pltpu.VMEM((N_PAIRS, tm, LANES), F32)],
        compiler_params=_params("parallel"),
    )(dmix)


def _window(j, i, tq):
    r0 = j * tq + i * BLOCK
    start = pl.multiple_of(jnp.maximum(r0 - BLOCK, 0), BLOCK)
    return pl.ds(start, 2 * BLOCK), r0 - start


def _band_valid(offset, max_dist):
    dist = (lax.broadcasted_iota(jnp.int32, (BLOCK, 2 * BLOCK), 0)
            - lax.broadcasted_iota(jnp.int32, (BLOCK, 2 * BLOCK), 1) + offset)
    return jnp.abs(2 * dist - max_dist) <= max_dist


def _one_head(lo, h, x, other):
    return jnp.where(lo, x, other) if h == 0 else jnp.where(lo, other, x)


NT = (((1,), (1,)), ((), ()))
TN = (((0,), (0,)), ((), ()))


def _attn_fwd(q, k, v, sinks, *, max_dist, name, emit_bf16=False):
    d, sp, wq = q.shape
    nq, nk = wq // LANES, k.shape[2] // LANES
    kdiv = nq // nk
    tq = min(sp, 1024)
    nsub = tq // BLOCK
    has_sink = sinks is not None

    def body(*refs):
        refs = list(refs)
        sink_ref = refs.pop(0) if has_sink else None
        q_ref, k_ref, v_ref, o_ref, lse_ref = refs[:5]
        pair = pl.program_id(1)
        j = pl.program_id(2)
        lo = _low_lanes(BLOCK)
        for i in range(nsub):
            win, offset = _window(j, i, tq)
            valid = _band_valid(offset, max_dist)
            rows = slice(i * BLOCK, (i + 1) * BLOCK)
            q2 = q_ref[0, rows, :]
            kw = k_ref[0, win, :]
            vw = v_ref[0, win, :]
            zero = jnp.zeros_like(q2)
            outs, lses = [], []
            for h in range(2):
                s = lax.dot_general(_one_head(lo, h, q2, zero), kw, NT, preferred_element_type=F32) * ATTN_SCALE
                s = jnp.where(valid, s, NEG)
                m = jnp.max(s, axis=-1, keepdims=True)
                if has_sink:
                    sk = sink_ref[2 * pair + h]
                    m = jnp.maximum(m, sk)
                p = jnp.exp(s - m)
                l = jnp.sum(p, axis=-1, keepdims=True)
                if has_sink:
                    l = l + jnp.exp(sk - m)
                outs.append(jnp.dot(p.astype(BF16), vw, preferred_element_type=F32) / l)
                lses.append(m + jnp.log(l))
            o2 = jnp.where(lo, outs[0], outs[1])
            o_ref[0, rows, :] = o2
            lse_ref[0, rows, :] = jnp.where(lo, lses[0], lses[1])
            if emit_bf16:
                refs[5][0, rows, :] = o2.astype(BF16)

    qspec = pl.BlockSpec((1, tq, LANES), lambda r, p, j: (r, j, p))
    kspec = pl.BlockSpec((1, sp, LANES), lambda r, p, j: (r, 0, p // kdiv))
    in_specs = [qspec, kspec, kspec]
    operands = [q, k, v]
    if has_sink:
        in_specs = [pl.BlockSpec(memory_space=pltpu.SMEM)] + in_specs
        operands = [sinks] + operands
    out_shape = [jax.ShapeDtypeStruct(q.shape, F32), jax.ShapeDtypeStruct(q.shape, F32)]
    if emit_bf16:
        out_shape.append(jax.ShapeDtypeStruct(q.shape, BF16))
    return pl.pallas_call(
        body, name=name, grid=(d, nq, sp // tq), in_specs=in_specs, out_specs=[qspec] * len(out_shape),
        out_shape=out_shape, compiler_params=_params("parallel", "parallel", "arbitrary"),
    )(*operands)


def _attn_bwd(q, k, v, do, oo, lse, sinks, *, max_dist, name):
    d, sp, wq = q.shape
    wk = k.shape[2]
    nq, nk = wq // LANES, wk // LANES
    kdiv = nq // nk
    tq = min(sp, 1024)
    nsub = tq // BLOCK
    has_sink = sinks is not None

    def body(*refs):
        refs = list(refs)
        sink_ref = refs.pop(0) if has_sink else None
        q_ref, k_ref, v_ref, do_ref, oo_ref, lse_ref, dq_ref, dk_ref, dv_ref = refs[:9]
        pk, g, j = pl.program_id(1), pl.program_id(2), pl.program_id(3)

        @pl.when((g == 0) & (j == 0))
        def _():
            dk_ref[...] = jnp.zeros_like(dk_ref)
            dv_ref[...] = jnp.zeros_like(dv_ref)

        lo = _low_lanes(BLOCK)
        sink_acc = [jnp.zeros((1, LANES), F32), jnp.zeros((1, LANES), F32)]
        for i in range(nsub):
            win, offset = _window(j, i, tq)
            valid = _band_valid(offset, max_dist)
            rows = slice(i * BLOCK, (i + 1) * BLOCK)
            q2 = q_ref[0, rows, :]
            kw = k_ref[0, win, :]
            vw = v_ref[0, win, :]
            do2 = do_ref[0, rows, :].astype(F32)
            dob = do2.astype(BF16)
            prod = do2 * oo_ref[0, rows, :]
            lse2 = lse_ref[0, rows, :]
            lse_swapped = pltpu.roll(lse2, HEAD_DIM, 1)
            zero = jnp.zeros_like(q2)
            dqs, dks, dvs = [], [], []
            for h in range(2):
                s = lax.dot_general(_one_head(lo, h, q2, zero), kw, NT, preferred_element_type=F32) * ATTN_SCALE
                s = jnp.where(valid, s, NEG)
                lse_h = _one_head(lo, h, lse2, lse_swapped)
                p = jnp.exp(s - jnp.tile(lse_h, (1, 2)))
                delta = jnp.sum(_one_head(lo, h, prod, 0.0), axis=-1, keepdims=True)
                dvs.append(lax.dot_general(p.astype(BF16), dob, TN, preferred_element_type=F32))
                dp = lax.dot_general(_one_head(lo, h, dob, zero), vw, NT, preferred_element_type=F32)
                ds = (p * (dp - delta) * ATTN_SCALE).astype(BF16)
                dqs.append(jnp.dot(ds, kw, preferred_element_type=F32))
                dks.append(lax.dot_general(ds, q2, TN, preferred_element_type=F32))
                if has_sink:
                    sk = sink_ref[2 * (pk * kdiv + g) + h]
                    sink_acc[h] = sink_acc[h] - jnp.sum(jnp.exp(sk - lse_h) * delta, axis=0, keepdims=True)
            lo2 = lax.broadcasted_iota(jnp.int32, (2 * BLOCK, LANES), 1) < HEAD_DIM
            dq_ref[0, rows, :] = jnp.where(lo, dqs[0], dqs[1])
            dk_ref[0, win, :] += jnp.where(lo2, dks[0], dks[1])
            dv_ref[0, win, :] += jnp.where(lo2, dvs[0], dvs[1])
        if has_sink:
            dsink_ref = refs[9]

            @pl.when(j == 0)
            def _():
                dsink_ref[...] = jnp.zeros_like(dsink_ref)

            dsink_ref[0] += jnp.where(lo[0:1], sink_acc[0], sink_acc[1])

    def qmap(r, pk, g, j):
        return (r, j, pk * kdiv + g)

    def kmap(r, pk, g, j):
        return (r, 0, pk)

    qspec = pl.BlockSpec((1, tq, LANES), qmap)
    kspec = pl.BlockSpec((1, sp, LANES), kmap)
    in_specs = [qspec, kspec, kspec, qspec, qspec, qspec]
    operands = [q, k, v, do, oo, lse]
    out_specs = [qspec, kspec, kspec]
    out_shape = [jax.ShapeDtypeStruct((d, sp, wq), F32), jax.ShapeDtypeStruct((d, sp, wk), F32),
                 jax.ShapeDtypeStruct((d, sp, wk), F32)]
    if has_sink:
        in_specs = [pl.BlockSpec(memory_space=pltpu.SMEM)] + in_specs
        operands = [sinks] + operands
        out_specs.append(pl.BlockSpec((1, 1, LANES), lambda r, pk, g, j: (pk * kdiv + g, 0, 0)))
        out_shape.append(jax.ShapeDtypeStruct((nq, 1, LANES), F32))
    return pl.pallas_call(
        body, name=name, grid=(d, nk, kdiv, sp // tq), in_specs=in_specs, out_specs=out_specs, out_shape=out_shape,
        compiler_params=_params("parallel", "parallel", "arbitrary", "arbitrary"),
    )(*operands)


def _combine(outs, lses, name):
    s = outs[0].shape[1]
    tm = 512
    nb = len(DILATIONS)
    ds = [d for d in DILATIONS if d > 1]

    def body(*refs):
        o_refs, l_refs = refs[:nb], refs[nb:2 * nb]
        cb_ref, c_ref, lse_ref = refs[2 * nb:2 * nb + 3]
        folded = refs[2 * nb + 3:2 * nb + 3 + 2 * len(ds)]
        scratch = refs[2 * nb + 3 + 2 * len(ds):]
        so = {1: None}
        sl = {1: None}
        for i, d in enumerate(ds):
            so[d], sl[d] = scratch[2 * i], scratch[2 * i + 1]
            _unfold_load(o_refs[1 + i], so[d], d)
            _unfold_load(l_refs[1 + i], sl[d], d)
        for p in range(N_PAIRS):
            pb = _pair_block(p)
            ls = [l_refs[0][0, :, pb]] + [sl[d][p] for d in ds]
            os_ = [o_refs[0][0, :, pb]] + [so[d][p] for d in ds]
            m = ls[0]
            for t in ls[1:]:
                m = jnp.maximum(m, t)
            ws = [jnp.exp(t - m) for t in ls]
            tot = ws[0]
            for t in ws[1:]:
                tot = tot + t
            acc = ws[0] * os_[0]
            for w, o in zip(ws[1:], os_[1:]):
                acc = acc + w * o
            cmix = acc / tot
            lse = m + jnp.log(tot)
            cb_ref[:, pb] = cmix.astype(BF16)
            c_ref[0, :, pb] = cmix
            lse_ref[0, :, pb] = lse
            so[ds[0]][p] = cmix
            sl[ds[0]][p] = lse
        for i, d in enumerate(ds):
            for r in range(d):
                for p in range(N_PAIRS):
                    rows = pl.ds(r, tm // d, stride=d)
                    folded[2 * i][r, :, _pair_block(p)] = so[ds[0]][p, rows, :]
                    folded[2 * i + 1][r, :, _pair_block(p)] = sl[ds[0]][p, rows, :]

    in_specs = [_folded_spec(d, tm) for _ in range(2) for d in DILATIONS]
    out_specs = [pl.BlockSpec((tm, ATTN_W), lambda i: (i, 0)), _folded_spec(1, tm), _folded_spec(1, tm)]
    out_shape = [jax.ShapeDtypeStruct((s, ATTN_W), BF16), _folded_shape(s, 1, F32), _folded_shape(s, 1, F32)]
    for d in ds:
        out_specs += [_folded_spec(d, tm)] * 2
        out_shape += [_folded_shape(s, d, F32)] * 2
    return pl.pallas_call(
        body, name=name, grid=(s // tm,), in_specs=in_specs, out_specs=out_specs, out_shape=out_shape,
        scratch_shapes=[pltpu.VMEM((N_PAIRS, tm, LANES), F32)] * (2 * len(ds)),
        compiler_params=_params("parallel"),
    )(*outs, *lses)


GLU_A = slice(768, 1280)
GLU_B = slice(1280, 1792)
EVEN_IN = 1792
ODD_IN = 2560
CONV_CH = 512


def _conv_fwd(proj, w, b, ln_g, ln_b, name):
    s = proj.shape[0]
    tm = 512
    nh = tm // CONV_HALO
    lead = CONV_HALO - (CONV_WIDTH - 1)

    def body(p_ref, ph_ref, w_ref, b_ref, g_ref, bb_ref, y_ref, o_ref, xf_ref):
        xf_ref[CONV_HALO:, :] = p_ref[:, GLU_A] * _sigmoid(p_ref[:, GLU_B])
        hist = ph_ref[:, GLU_A] * _sigmoid(ph_ref[:, GLU_B])
        xf_ref[0:CONV_HALO, :] = jnp.where(pl.program_id(0) > 0, hist, 0.0)
        acc = jnp.zeros((tm, CONV_CH), F32) + b_ref[...]
        for j in range(CONV_WIDTH):
            acc = acc + xf_ref[pl.ds(lead + j, tm), :] * w_ref[j:j + 1, :]
        y_ref[...] = acc
        mu = jnp.mean(acc, axis=-1, keepdims=True)
        xc = acc - mu
        var = jnp.mean(xc * xc, axis=-1, keepdims=True)
        zz = xc * lax.rsqrt(var + LN_EPS) * g_ref[...] + bb_ref[...]
        o_ref[...] = (zz * _sigmoid(zz)).astype(BF16)

    def const(a):
        return pl.BlockSpec(a.shape, lambda i: (0, 0))

    return pl.pallas_call(
        body, name=name, grid=(s // tm,),
        in_specs=[pl.BlockSpec((tm, EVEN_IN), lambda i: (i, 0)),
                  pl.BlockSpec((CONV_HALO, EVEN_IN), lambda i: (jnp.maximum(i * nh - 1, 0), 0)),
                  const(w), const(b), const(ln_g), const(ln_b)],
        out_specs=[pl.BlockSpec((tm, CONV_CH), lambda i: (i, 0)), pl.BlockSpec((tm, CONV_CH), lambda i: (i, 0))],
        out_shape=[jax.ShapeDtypeStruct((s, CONV_CH), F32), jax.ShapeDtypeStruct((s, CONV_CH), BF16)],
        scratch_shapes=[pltpu.VMEM((tm + CONV_HALO, CONV_CH), F32)],
        compiler_params=_params("arbitrary"),
    )(proj, proj, w, b, ln_g, ln_b)


def _conv_tail_bwd(dmix, yconv, ln_g, ln_b, name):
    def body(d_ref, y_ref, g_ref, b_ref, dy_ref, dg_ref, db_ref, dcb_ref):
        @pl.when(_first_step())
        def _():
            dg_ref[...] = jnp.zeros_like(dg_ref)
            db_ref[...] = jnp.zeros_like(db_ref)
            dcb_ref[...] = jnp.zeros_like(dcb_ref)

        y = y_ref[...]
        g = g_ref[...]
        mu = jnp.mean(y, axis=-1, keepdims=True)
        xc = y - mu
        rstd = lax.rsqrt(jnp.mean(xc * xc, axis=-1, keepdims=True) + LN_EPS)
        xh = xc * rstd
        zz = xh * g + b_ref[...]
        sg = _sigmoid(zz)
        dzz = d_ref[:, CONV_CH:] * sg * (1.0 + zz * (1.0 - sg))
        dg_ref[...] += jnp.sum(dzz * xh, axis=0, keepdims=True)
        db_ref[...] += jnp.sum(dzz, axis=0, keepdims=True)
        dxh = dzz * g
        dy = rstd * (dxh - jnp.mean(dxh, axis=-1, keepdims=True) - xh * jnp.mean(dxh * xh, axis=-1, keepdims=True))
        dcb_ref[...] += jnp.sum(dy, axis=0, keepdims=True)
        dy_ref[...] = dy

    vec = ((1, CONV_CH), F32)
    return _rows(body, name, 512, [dmix, yconv], [ln_g, ln_b], [(CONV_CH, F32)], [vec, vec, vec])


def _conv_bwd(proj, dy, w, name):
    s = proj.shape[0]
    tm = 512
    nh = tm // CONV_HALO
    nsteps = s // tm
    lead = CONV_HALO - (CONV_WIDTH - 1)

    def body(p_ref, ph_ref, dy_ref, dyn_ref, w_ref, dglu_ref, dw_ref, xf_ref, dyf_ref):
        i = pl.program_id(0)

        @pl.when(i == 0)
        def _():
            dw_ref[...] = jnp.zeros_like(dw_ref)

        ga = p_ref[:, GLU_A]
        sgb = _sigmoid(p_ref[:, GLU_B])
        xf_ref[CONV_HALO:, :] = ga * sgb
        hist = ph_ref[:, GLU_A] * _sigmoid(ph_ref[:, GLU_B])
        xf_ref[0:CONV_HALO, :] = jnp.where(i > 0, hist, 0.0)
        dyt = dy_ref[...]
        dyf_ref[0:tm, :] = dyt
        dyf_ref[tm:, :] = jnp.where(i < nsteps - 1, dyn_ref[...], 0.0)
        acc = jnp.zeros((tm, CONV_CH), F32)
        for j in range(CONV_WIDTH):
            acc = acc + dyf_ref[pl.ds(CONV_WIDTH - 1 - j, tm), :] * w_ref[j:j + 1, :]
        for j in range(CONV_WIDTH):
            dw_ref[j:j + 1, :] += jnp.sum(dyt * xf_ref[pl.ds(lead + j, tm), :], axis=0, keepdims=True)
        dglu_ref[:, 0:CONV_CH] = (acc * sgb).astype(BF16)
        dglu_ref[:, CONV_CH:] = (acc * ga * sgb * (1.0 - sgb)).astype(BF16)

    return pl.pallas_call(
        body, name=name, grid=(nsteps,),
        in_specs=[pl.BlockSpec((tm, EVEN_IN), lambda i: (i, 0)),
                  pl.BlockSpec((CONV_HALO, EVEN_IN), lambda i: (jnp.maximum(i * nh - 1, 0), 0)),
                  pl.BlockSpec((tm, CONV_CH), lambda i: (i, 0)),
                  pl.BlockSpec((CONV_HALO, CONV_CH), lambda i: (jnp.minimum((i + 1) * nh, s // CONV_HALO - 1), 0)),
                  pl.BlockSpec(w.shape, lambda i: (0, 0))],
        out_specs=[pl.BlockSpec((tm, 2 * CONV_CH), lambda i: (i, 0)), pl.BlockSpec(w.shape, lambda i: (0, 0))],
        out_shape=[jax.ShapeDtypeStruct((s, 2 * CONV_CH), BF16), jax.ShapeDtypeStruct(w.shape, F32)],
        scratch_shapes=[pltpu.VMEM((tm + CONV_HALO, CONV_CH), F32), pltpu.VMEM((tm + CONV_HALO, CONV_CH), F32)],
        compiler_params=_params("arbitrary"),
    )(proj, proj, dy, dy, w)


GATE_Z = slice(1536, 2560)
D_CH = 512
GELU_C = math.sqrt(2.0 / math.pi)
GELU_K = 0.044715


def _gelu_parts(z):
    t = jnp.tanh(GELU_C * (z + GELU_K * z * z * z))
    return 0.5 * z * (1.0 + t), t


def _lane_group(rows):
    return lax.broadcasted_iota(jnp.int32, (rows, D_CH), 1) // HEAD_DIM


def _tril_mask():
    return lax.broadcasted_iota(jnp.int32, (BLOCK, BLOCK), 0) >= lax.broadcasted_iota(jnp.int32, (BLOCK, BLOCK), 1)


def _layer_norm_parts(x):
    mu = jnp.mean(x, axis=-1, keepdims=True)
    xc = x - mu
    rstd = lax.rsqrt(jnp.mean(xc * xc, axis=-1, keepdims=True) + LN_EPS)
    return xc * rstd, rstd


def _gate_fwd(proj, ln_g, ln_b, w_sp, sb_t, name):
    tm = 512

    def body(p_ref, g_ref, b_ref, w_ref, sb_ref, mixed_ref, out_ref):
        zz, _ = _gelu_parts(p_ref[:, GATE_Z])
        u = zz[:, :D_CH]
        xh, _ = _layer_norm_parts(zz[:, D_CH:])
        gn = (xh * g_ref[...] + b_ref[...]).astype(BF16)
        grp = _lane_group(BLOCK)
        tri = _tril_mask()
        ws = [jnp.where(tri, w_ref[gi], 0.0).astype(BF16) for gi in range(N_GROUPS)]
        bias = jnp.zeros((BLOCK, D_CH), F32)
        for gi in range(N_GROUPS):
            bias = jnp.where(grp == gi, sb_ref[:, gi:gi + 1], bias)
        for ch in range(tm // BLOCK):
            rows = slice(ch * BLOCK, (ch + 1) * BLOCK)
            gc = gn[rows, :]
            mixed = bias
            for gi in range(N_GROUPS):
                r = jnp.dot(ws[gi], gc, preferred_element_type=F32)
                mixed = jnp.where(grp == gi, r + bias, mixed)
            mixed_ref[rows, :] = mixed
            out_ref[rows, :] = (u[rows, :] * mixed).astype(BF16)

    return _rows(body, name, tm, [proj], [ln_g, ln_b, w_sp, sb_t], [(D_CH, F32), (D_CH, BF16)])


def _gate_bwd(dmix, proj, mixed, ln_g, ln_b, w_sp, name):
    tm = 512

    def body(d_ref, p_ref, m_ref, g_ref, b_ref, w_ref, dz_ref, dg_ref, db_ref, dw_ref, dsb_ref, dgn_ref):
        @pl.when(_first_step())
        def _():
            dg_ref[...] = jnp.zeros_like(dg_ref)
            db_ref[...] = jnp.zeros_like(db_ref)
            dw_ref[...] = jnp.zeros_like(dw_ref)
            dsb_ref[...] = jnp.zeros_like(dsb_ref)

        z = p_ref[:, GATE_Z]
        zz, t = _gelu_parts(z)
        u = zz[:, :D_CH]
        xh, rstd = _layer_norm_parts(zz[:, D_CH:])
        g = g_ref[...]
        gn = (xh * g + b_ref[...]).astype(BF16)
        dd = d_ref[:, D_CH:]
        du = dd * m_ref[...]
        dm = dd * u
        grp = _lane_group(BLOCK)
        tri = _tril_mask()
        ws = [jnp.where(tri, w_ref[gi], 0.0).astype(BF16) for gi in range(N_GROUPS)]
        gsel = (lax.broadcasted_iota(jnp.int32, (N_GROUPS, D_CH), 1) // HEAD_DIM
                == lax.broadcasted_iota(jnp.int32, (N_GROUPS, D_CH), 0)).astype(F32)
        for ch in range(tm // BLOCK):
            rows = slice(ch * BLOCK, (ch + 1) * BLOCK)
            dmc = dm[rows, :]
            dmb = dmc.astype(BF16)
            gc = gn[rows, :]
            dgn = jnp.zeros((BLOCK, D_CH), F32)
            for gi in range(N_GROUPS):
                r = lax.dot_general(ws[gi], dmb, TN, preferred_element_type=F32)
                dgn = jnp.where(grp == gi, r, dgn)
                dmg = jnp.where(grp == gi, dmb, jnp.zeros_like(dmb))
                dwg = lax.dot_general(dmg, gc, NT, preferred_element_type=F32)
                dw_ref[gi] += jnp.where(tri, dwg, 0.0)
            dsb_ref[...] += lax.dot_general(gsel, dmc, NT, preferred_element_type=F32, precision=lax.Precision.HIGHEST)
            dgn_ref[rows, :] = dgn
        dgn = dgn_ref[...]
        db_ref[...] += jnp.sum(dgn, axis=0, keepdims=True)
        dg_ref[...] += jnp.sum(dgn * xh, axis=0, keepdims=True)
        dxh = dgn * g
        dgp = rstd * (dxh - jnp.mean(dxh, axis=-1, keepdims=True) - xh * jnp.mean(dxh * xh, axis=-1, keepdims=True))
        dgelu = 0.5 * (1.0 + t) + 0.5 * z * (1.0 - t * t) * GELU_C * (1.0 + 3.0 * GELU_K * z * z)
        dz_ref[:, 0:D_CH] = (du * dgelu[:, :D_CH]).astype(BF16)
        dz_ref[:, D_CH:] = (dgp * dgelu[:, D_CH:]).astype(BF16)

    s = proj.shape[0]
    tiled = [dmix, proj, mixed]
    consts = [ln_g, ln_b, w_sp]
    in_specs = [pl.BlockSpec((tm, a.shape[1]), lambda i: (i, 0)) for a in tiled]
    in_specs += [pl.BlockSpec(a.shape, lambda i, nd=a.ndim: (0,) * nd) for a in consts]
    vec = (1, D_CH)
    acc_shapes = [vec, vec, w_sp.shape, (N_GROUPS, BLOCK)]
    return pl.pallas_call(
        body, name=name, grid=(s // tm,), in_specs=in_specs,
        out_specs=[pl.BlockSpec((tm, 2 * D_CH), lambda i: (i, 0))]
        + [pl.BlockSpec(sh, lambda i, nd=len(sh): (0,) * nd) for sh in acc_shapes],
        out_shape=[jax.ShapeDtypeStruct((s, 2 * D_CH), BF16)] + [jax.ShapeDtypeStruct(sh, F32) for sh in acc_shapes],
        scratch_shapes=[pltpu.VMEM((tm, D_CH), F32)],
        compiler_params=_params("arbitrary"),
    )(*tiled, *consts)


def _adam_update(w, g, m, v):
    nm = ADAM_B1 * m + (1.0 - ADAM_B1) * g
    nv = ADAM_B2 * v + (1.0 - ADAM_B2) * (g * g)
    m_hat = nm / (1.0 - ADAM_B1 ** ADAM_STEP)
    v_hat = nv / (1.0 - ADAM_B2 ** ADAM_STEP)
    return -ADAM_LR * (m_hat / (jnp.sqrt(v_hat) + ADAM_EPS) + ADAM_WD * w), nm, nv


def _adamw(w, g, m, v, name):
    rows, cols = w.shape
    tm = _tile(rows, 512, 8)

    def body(w_ref, g_ref, m_ref, v_ref, d_ref, nm_ref, nv_ref):
        d_ref[...], nm_ref[...], nv_ref[...] = _adam_update(w_ref[...], g_ref[...], m_ref[...], v_ref[...])

    return _rows(body, name, tm, [w, g, m, v], [], [(cols, F32)] * 3)


def _adamw_t(w, g_t, m, v, name):
    layers, kdim, n = w.shape
    tr = 256

    def body(w_ref, g_ref, m_ref, v_ref, go_ref, d_ref, nm_ref, nv_ref):
        g = g_ref[0].T
        go_ref[0] = g
        d_ref[0], nm_ref[0], nv_ref[0] = _adam_update(w_ref[0], g, m_ref[0], v_ref[0])

    wspec = pl.BlockSpec((1, tr, n), lambda l, i: (l, i, 0))
    return pl.pallas_call(
        body, name=name, grid=(layers, kdim // tr),
        in_specs=[wspec, pl.BlockSpec((1, n, tr), lambda l, i: (l, 0, i)), wspec, wspec],
        out_specs=[wspec] * 4, out_shape=[jax.ShapeDtypeStruct(w.shape, F32)] * 4,
        compiler_params=_params("parallel", "parallel"),
    )(w, g_t, m, v)


def _pair_add(a, b, name):
    n, rows, cols = a.shape
    tm = _tile(rows, 512, 16)

    def body(a_ref, b_ref, o_ref):
        o_ref[...] = (a_ref[...] + b_ref[...]).astype(BF16)

    spec = pl.BlockSpec((1, tm, cols), lambda k, i: (k, i, 0))
    return pl.pallas_call(body, name=name, grid=(n, rows // tm), in_specs=[spec, spec], out_specs=spec,
                          out_shape=jax.ShapeDtypeStruct(a.shape, BF16), compiler_params=_params("parallel", "parallel"))(a, b)


def _ordered_sum(parts, name):
    n, rows, cols = parts.shape
    tm = _tile(rows, 512, 16 if parts.dtype == BF16 else 8)

    def body(p_ref, o_ref):
        acc = p_ref[0].astype(F32)
        for k in range(1, n):
            acc = acc + p_ref[k].astype(F32)
        o_ref[...] = acc

    return pl.pallas_call(body, name=name, grid=(rows // tm,),
                          in_specs=[pl.BlockSpec((n, tm, cols), lambda i: (0, i, 0))],
                          out_specs=pl.BlockSpec((tm, cols), lambda i: (i, 0)),
                          out_shape=jax.ShapeDtypeStruct((rows, cols), F32), compiler_params=_params("parallel"))(parts)


ANY = pl.BlockSpec(memory_space=pl.ANY)


def _position():
    x, y, c = lax.axis_index("x"), lax.axis_index("y"), lax.axis_index("c")
    other_chips = [(1 - x, y), (x, 1 - y), (1 - x, 1 - y)]
    return x, y, c, other_chips


def _remote(src, dst, send_sem, recv_sem, to):
    return pltpu.make_async_remote_copy(src_ref=src, dst_ref=dst, send_sem=send_sem, recv_sem=recv_sem,
                                        device_id=to, device_id_type=MESH)


def _gather_chips(shard, name):
    rows, cols = shard.shape
    half = rows // 2

    def body(in_ref, out_ref, send_sems, recv_sems, local_sem):
        x, y, c, chips = _position()
        me = 2 * x + y
        sibling = (x, y, 1 - c)

        def slab(chip, h):
            return out_ref.at[chip, pl.ds(h * half, half), :]

        mine = pltpu.make_async_copy(in_ref, out_ref.at[me], local_sem)
        mine.start()
        first = [_remote(in_ref.at[pl.ds(c * half, half), :], slab(me, c), send_sems.at[j], recv_sems.at[j], (cx, cy, c))
                 for j, (cx, cy) in enumerate(chips)]
        for cp in first:
            cp.start()
        passed = []
        for j, (cx, cy) in enumerate(chips):
            got = slab(2 * cx + cy, c)
            _remote(got, got, send_sems.at[j], recv_sems.at[j], sibling).wait_recv()
            cp = _remote(got, got, send_sems.at[3 + j], recv_sems.at[3 + j], sibling)
            cp.start()
            passed.append(cp)
        for j, (cx, cy) in enumerate(chips):
            got = slab(2 * cx + cy, 1 - c)
            _remote(got, got, send_sems.at[3 + j], recv_sems.at[3 + j], sibling).wait_recv()
        for cp in first + passed:
            cp.wait_send()
        mine.wait()

    return pl.pallas_call(
        body, name=name, in_specs=[ANY], out_specs=ANY,
        out_shape=jax.ShapeDtypeStruct((N_CHIPS, rows, cols), shard.dtype),
        scratch_shapes=[pltpu.SemaphoreType.DMA((6,)), pltpu.SemaphoreType.DMA((6,)), pltpu.SemaphoreType.DMA],
    )(shard)


def _gather_devices(block, name):
    rows, cols = block.shape

    def body(in_ref, out_ref, send_sems, recv_sems, local_sem):
        x, y, c, chips = _position()
        sibling = (x, y, 1 - c)

        def slot(px, py, pc):
            return out_ref.at[4 * px + 2 * py + pc]

        mine = pltpu.make_async_copy(in_ref, slot(x, y, c), local_sem)
        mine.start()
        first = [_remote(in_ref, slot(x, y, c), send_sems.at[0], recv_sems.at[0], sibling)]
        first += [_remote(in_ref, slot(x, y, c), send_sems.at[1 + j], recv_sems.at[1 + j], (cx, cy, c))
                  for j, (cx, cy) in enumerate(chips)]
        for cp in first:
            cp.start()
        passed = []
        for j, (cx, cy) in enumerate(chips):
            got = slot(cx, cy, c)
            _remote(got, got, send_sems.at[1 + j], recv_sems.at[1 + j], sibling).wait_recv()
            cp = _remote(got, got, send_sems.at[4 + j], recv_sems.at[4 + j], sibling)
            cp.start()
            passed.append(cp)
        got = slot(x, y, 1 - c)
        _remote(got, got, send_sems.at[0], recv_sems.at[0], sibling).wait_recv()
        for j, (cx, cy) in enumerate(chips):
            got = slot(cx, cy, 1 - c)
            _remote(got, got, send_sems.at[4 + j], recv_sems.at[4 + j], sibling).wait_recv()
        for cp in first + passed:
            cp.wait_send()
        mine.wait()

    return pl.pallas_call(
        body, name=name, in_specs=[ANY], out_specs=ANY,
        out_shape=jax.ShapeDtypeStruct((N_DEV, rows, cols), block.dtype),
        scratch_shapes=[pltpu.SemaphoreType.DMA((7,)), pltpu.SemaphoreType.DMA((7,)), pltpu.SemaphoreType.DMA],
    )(block)


def _pair_swap_pack(grads, name):
    n = len(grads)
    hs = [g.shape[2] for g in grads]
    offs = [sum(hs[:i]) for i in range(n)]
    total = sum(hs)
    cols = grads[0].shape[3]

    def body(*refs):
        g_refs = refs[:n]
        keep_ref, got_ref, send_sems, recv_sems, local_sems = refs[n:]
        x, y, c, _ = _position()
        sibling = (x, y, 1 - c)
        copies = []
        for i, g_ref in enumerate(g_refs):
            rows = pl.ds(offs[i], hs[i])
            local = pltpu.make_async_copy(g_ref.at[:, c], keep_ref.at[:, rows, :], local_sems.at[i])
            local.start()
            cp = _remote(g_ref.at[:, 1 - c], got_ref.at[:, rows, :], send_sems.at[i], recv_sems.at[i], sibling)
            cp.start()
            copies.append((local, cp))
        for local, cp in copies:
            cp.wait()
            local.wait()

    shape = jax.ShapeDtypeStruct((N_CHIPS, total, cols), F32)
    return pl.pallas_call(
        body, name=name, in_specs=[ANY] * n, out_specs=[ANY, ANY], out_shape=[shape, shape],
        scratch_shapes=[pltpu.SemaphoreType.DMA((n,)), pltpu.SemaphoreType.DMA((n,)), pltpu.SemaphoreType.DMA((n,))],
    )(*grads)


def _chip_exchange(parts, name):
    def body(in_ref, out_ref, send_sems, recv_sems, local_sem):
        x, y, c, chips = _position()
        me = 2 * x + y
        mine = pltpu.make_async_copy(in_ref.at[me], out_ref.at[me], local_sem)
        mine.start()
        sent = [_remote(in_ref.at[2 * cx + cy], out_ref.at[me], send_sems.at[j], recv_sems.at[j], (cx, cy, c))
                for j, (cx, cy) in enumerate(chips)]
        for cp in sent:
            cp.start()
        for j, (cx, cy) in enumerate(chips):
            got = out_ref.at[2 * cx + cy]
            _remote(got, got, send_sems.at[j], recv_sems.at[j], (cx, cy, c)).wait_recv()
        for cp in sent:
            cp.wait_send()
        mine.wait()

    return pl.pallas_call(
        body, name=name, in_specs=[ANY], out_specs=ANY, out_shape=jax.ShapeDtypeStruct(parts.shape, parts.dtype),
        scratch_shapes=[pltpu.SemaphoreType.DMA((3,)), pltpu.SemaphoreType.DMA((3,)), pltpu.SemaphoreType.DMA],
    )(parts)


def _join_unpack(mine, hs, name):
    n = len(hs)
    offs = [sum(hs[:i]) for i in range(n)]
    cols = mine.shape[1]

    def body(in_ref, *refs):
        outs = refs[:n]
        send_sems, recv_sems, local_sems = refs[n:]
        x, y, c, _ = _position()
        sibling = (x, y, 1 - c)
        copies = []
        for i, o_ref in enumerate(outs):
            src = in_ref.at[pl.ds(offs[i], hs[i]), :]
            here = o_ref.at[pl.ds(c * hs[i], hs[i]), :]
            local = pltpu.make_async_copy(src, here, local_sems.at[i])
            local.start()
            cp = _remote(src, here, send_sems.at[i], recv_sems.at[i], sibling)
            cp.start()
            copies.append((local, cp, o_ref.at[pl.ds((1 - c) * hs[i], hs[i]), :]))
        for i, (local, cp, there) in enumerate(copies):
            _remote(there, there, send_sems.at[i], recv_sems.at[i], sibling).wait_recv()
            cp.wait_send()
            local.wait()

    return pl.pallas_call(
        body, name=name, in_specs=[ANY], out_specs=[ANY] * n,
        out_shape=[jax.ShapeDtypeStruct((2 * h, cols), F32) for h in hs],
        scratch_shapes=[pltpu.SemaphoreType.DMA((n,)), pltpu.SemaphoreType.DMA((n,)), pltpu.SemaphoreType.DMA((n,))],
    )(mine)


def _pad_rows(a, mult):
    extra = (-a.shape[0]) % mult
    return a if extra == 0 else jnp.pad(a, ((0, extra), (0, 0)))


def _pack_small(arrs, mult):
    rows = []
    for a in arrs:
        flat = a.reshape(-1)
        extra = (-flat.shape[0]) % LANES
        if extra:
            flat = jnp.pad(flat, (0, extra))
        rows.append(flat.reshape(-1, LANES))
    return _pad_rows(jnp.concatenate(rows, axis=0), mult)


def _unpack_small(packed, shapes):
    out, r = [], 0
    for sh in shapes:
        n = math.prod(sh)
        cnt = -(-n // LANES)
        out.append(packed[r:r + cnt].reshape(-1)[:n].reshape(sh))
        r += cnt
    return out


def _ffn_fwd(h, g_norm, w_gate_t, w_up_t, w_down, tag):
    n = _rms_fwd(h, g_norm, f"{tag}_norm")
    gate = _matmul(n, w_gate_t, trans_b=True, name=f"{tag}_gate")
    up = _matmul(n, w_up_t, trans_b=True, name=f"{tag}_up")
    act = _swiglu_fwd(gate, up, f"{tag}_act")
    out = _matmul(act, w_down, add=h, name=f"{tag}_down")
    return out, (n, gate, up, act)


def _ffn_bwd(dh, dhb, h_in, saved, g_norm, w_gate_t, w_up_t, w_down, tag):
    n, gate, up, act = saved
    dact = _matmul(dhb, w_down, trans_b=True, name=f"{tag}_dact")
    dgate, dup = _swiglu_bwd(dact, gate, up, f"{tag}_dswiglu")
    dw_down = _matmul(act, dhb, trans_a=True, name=f"{tag}_dwdown")
    dw_gate_t = _matmul(dgate, n, trans_a=True, name=f"{tag}_dwgate")
    dw_up_t = _matmul(dup, n, trans_a=True, name=f"{tag}_dwup")
    dn = _matmul(dgate, w_gate_t, name=f"{tag}_dn_gate")
    dn = _matmul(dup, w_up_t, add=dn, name=f"{tag}_dn_up")
    dh_in, dh_inb, dg = _rms_bwd(dn, h_in, g_norm, dh, f"{tag}_dnorm")
    return dh_in, dh_inb, dg, dw_gate_t, dw_up_t, dw_down


def _local_step(x, tgt, w, big):
    s = x.shape[0]
    tabs = _rope_tables(s)
    grads, gbig = {}, {}

    g_ev = w['ev_norm_g']
    n1 = _rms_fwd(x, g_ev, "ev_norm")
    proj0 = _matmul(n1, big['ev_w_in', 0], trans_b=True, name="ev_in")
    q0, k0, v0 = _qkv_prep_even(proj0, tabs, "ev_qkv")
    sinks = w['ev_sinks'].reshape(-1)
    o0, lse0, o0b = _attn_fwd(q0, k0, v0, sinks, max_dist=BLOCK - 1, name="ev_attn", emit_bf16=True)
    yconv, cout = _conv_fwd(proj0, w['ev_conv_w'][0], w['ev_conv_b'], w['ev_conv_ln_g'], w['ev_conv_ln_b'], "ev_conv")
    mix0 = (o0b[0], cout)
    h1 = _matmul(mix0, big['ev_w_out', 0], add=x, name="ev_out")

    g_f0 = w['ffn_norm_g'][0:1]
    h2, ffn0 = _ffn_fwd(h1, g_f0, big['ffn_w_gate', 0], big['ffn_w_up', 0], big['ffn_w_down', 0], "ffn0")

    g_od = w['od_norm_g']
    n3 = _rms_fwd(h2, g_od, "od_norm")
    proj1 = _matmul(n3, big['od_w_in', 0], trans_b=True, name="od_in")
    qkv = _qkv_prep_odd(proj1, tabs, "od_qkv")
    nb = len(DILATIONS)
    outs, lses = [], []
    for i, d in enumerate(DILATIONS):
        o_r, lse_r = _attn_fwd(qkv[i], qkv[nb + i], qkv[2 * nb + i], None, max_dist=BLOCK, name=f"od_attn{d}")
        outs.append(o_r)
        lses.append(lse_r)
    comb = _combine(outs, lses, "od_combine")
    c_bf16 = comb[0]
    c_fold = {1: comb[1]}
    lse_fold = {1: comb[2]}
    for i, d in enumerate(DILATIONS[1:]):
        c_fold[d], lse_fold[d] = comb[3 + 2 * i], comb[4 + 2 * i]
    w_sp = w['od_spatial_w'][0]
    sb_t = w['od_spatial_b'][0].T
    mixed, dout = _gate_fwd(proj1, w['od_sgu_ln_g'], w['od_sgu_ln_b'], w_sp, sb_t, "od_gate")
    mix1 = (c_bf16, dout)
    h3 = _matmul(mix1, big['od_w_out', 0], add=h2, name="od_out")

    g_f1 = w['ffn_norm_g'][1:2]
    h4, ffn1 = _ffn_fwd(h3, g_f1, big['ffn_w_gate', 1], big['ffn_w_up', 1], big['ffn_w_down', 1], "ffn1")

    dh4, dh4b, dg_final, loss_tile = _final_loss(h4, w['final_norm_g'].reshape(1, D_MODEL), tgt, "final")
    grads['final_norm_g'] = dg_final.reshape(D_MODEL)

    dh3, dh3b, dg_f1, gbig['ffn_w_gate', 1], gbig['ffn_w_up', 1], gbig['ffn_w_down', 1] = _ffn_bwd(
        dh4, dh4b, h3, ffn1, g_f1, big['ffn_w_gate', 1], big['ffn_w_up', 1], big['ffn_w_down', 1], "ffn1")

    dmix1 = _matmul(dh3b, big['od_w_out', 0], trans_b=True, name="od_dmix")
    gbig['od_w_out', 0] = _matmul_tn_pair(mix1[0], mix1[1], dh3b, "od_dwout")
    do_fold = dict(zip(DILATIONS[1:], _fold_dout(dmix1, "od_fold_dout")))
    do_fold[1] = dmix1[None]
    dqs, dks, dvs = [], [], []
    for i, d in enumerate(DILATIONS):
        dq_r, dk_r, dv_r = _attn_bwd(qkv[i], qkv[nb + i], qkv[2 * nb + i], do_fold[d], c_fold[d], lse_fold[d], None,
                                     max_dist=BLOCK, name=f"od_dattn{d}")
        dqs.append(dq_r)
        dks.append(dk_r)
        dvs.append(dv_r)
    dz, dg_sgu, db_sgu, dw_sp, dsb = _gate_bwd(dmix1, proj1, mixed, w['od_sgu_ln_g'], w['od_sgu_ln_b'], w_sp, "od_dgate")
    grads['od_sgu_ln_g'], grads['od_sgu_ln_b'] = dg_sgu, db_sgu
    grads['od_spatial_w'], grads['od_spatial_b'] = dw_sp[None], dsb[None]
    dproj1 = _qkv_post_odd(dqs, dks, dvs, dz, tabs, "od_dproj")
    gbig['od_w_in', 0] = _matmul(dproj1, n3, trans_a=True, name="od_dwin")
    dn3 = _matmul(dproj1, big['od_w_in', 0], name="od_dn")
    dh2, dh2b, dg_od = _rms_bwd(dn3, h2, g_od, dh3, "od_dnorm")
    grads['od_norm_g'] = dg_od

    dh1, dh1b, dg_f0, gbig['ffn_w_gate', 0], gbig['ffn_w_up', 0], gbig['ffn_w_down', 0] = _ffn_bwd(
        dh2, dh2b, h1, ffn0, g_f0, big['ffn_w_gate', 0], big['ffn_w_up', 0], big['ffn_w_down', 0], "ffn0")
    grads['ffn_norm_g'] = jnp.concatenate([dg_f0, dg_f1], axis=0)

    dmix0 = _matmul(dh1b, big['ev_w_out', 0], trans_b=True, name="ev_dmix")
    gbig['ev_w_out', 0] = _matmul_tn_pair(mix0[0], mix0[1], dh1b, "ev_dwout")
    dq0, dk0, dv0, dsink = _attn_bwd(q0, k0, v0, dmix0[None], o0, lse0, sinks, max_dist=BLOCK - 1, name="ev_dattn")
    grads['ev_sinks'] = dsink[:, 0, :].reshape(N_PAIRS, 2, HEAD_DIM)[:, :, 0].reshape(1, 8)
    dyc, dg_cln, db_cln, dcb = _conv_tail_bwd(dmix0, yconv, w['ev_conv_ln_g'], w['ev_conv_ln_b'], "ev_dconv_tail")
    grads['ev_conv_ln_g'], grads['ev_conv_ln_b'], grads['ev_conv_b'] = dg_cln, db_cln, dcb
    dglu, dconv_w = _conv_bwd(proj0, dyc, w['ev_conv_w'][0], "ev_dconv")
    grads['ev_conv_w'] = dconv_w[None]
    dproj0 = _qkv_post_even(dq0, dk0, dv0, dglu, tabs, "ev_dproj")
    gbig['ev_w_in', 0] = _matmul(dproj0, n1, trans_a=True, name="ev_dwin")
    dn1 = _matmul(dproj0, big['ev_w_in', 0], name="ev_dn")
    dx, _, dg_ev = _rms_bwd(dn1, x, g_ev, dh1, "ev_dnorm")
    grads['ev_norm_g'] = dg_ev
    return loss_tile, dx, grads, gbig


def _shard_rows(w, layer, by_cols):
    return w[layer].T if by_cols else w[layer]


def kernel(x, ev_norm_g, ev_w_in, ev_sinks, ev_conv_w, ev_conv_b, ev_conv_ln_g, ev_conv_ln_b, ev_w_out, od_norm_g, od_w_in, od_sgu_ln_g, od_sgu_ln_b, od_spatial_w, od_spatial_b, od_w_out, ffn_norm_g, ffn_w_gate, ffn_w_up, ffn_w_down, final_norm_g, loss_target, m_ev_norm_g, m_ev_w_in, m_ev_sinks, m_ev_conv_w, m_ev_conv_b, m_ev_conv_ln_g, m_ev_conv_ln_b, m_ev_w_out, m_od_norm_g, m_od_w_in, m_od_sgu_ln_g, m_od_sgu_ln_b, m_od_spatial_w, m_od_spatial_b, m_od_w_out, m_ffn_norm_g, m_ffn_w_gate, m_ffn_w_up, m_ffn_w_down, m_final_norm_g, v_ev_norm_g, v_ev_w_in, v_ev_sinks, v_ev_conv_w, v_ev_conv_b, v_ev_conv_ln_g, v_ev_conv_ln_b, v_ev_w_out, v_od_norm_g, v_od_w_in, v_od_sgu_ln_g, v_od_sgu_ln_b, v_od_spatial_w, v_od_spatial_b, v_od_w_out, v_ffn_norm_g, v_ffn_w_gate, v_ffn_w_up, v_ffn_w_down, v_final_norm_g):
    given = dict(locals())
    wts = {n: given[n] for n in WEIGHTS}
    mom = {n: given["m_" + n] for n in WEIGHTS}
    var = {n: given["v_" + n] for n in WEIGHTS}
    chip = 2 * lax.axis_index("x") + lax.axis_index("y")

    shard_rows = [_shard_rows(wts[n], layer, by_cols).astype(BF16) for n, layer, by_cols in BIG]
    counts = [a.shape[0] for a in shard_rows]
    all_w = _gather_chips(jnp.concatenate(shard_rows, axis=0), "gather_weights")
    big, r = {}, 0
    for (n, layer, _), cnt in zip(BIG, counts):
        big[n, layer] = all_w[:, r:r + cnt].reshape(N_CHIPS * cnt, D_MODEL)
        r += cnt
    full = {n: wts[n] for n in SMALL_REPL}
    small_shards = [wts[n] for n in SMALL_SHARDED]
    small_shapes = [a.shape for a in small_shards]
    all_s = _gather_chips(_pack_small(small_shards, 16), "gather_small_weights")
    per_chip = [_unpack_small(all_s[k], small_shapes) for k in range(N_CHIPS)]
    for i, n in enumerate(SMALL_SHARDED):
        full[n] = jnp.concatenate([per_chip[k][i] for k in range(N_CHIPS)], axis=-1)

    loss_tile, grad_x, grads, gbig = _local_step(x[0], loss_target[0], full, big)
    loss = lax.psum(loss_tile[0, 0], ("x", "y", "c"))

    halves = [cnt // 2 for cnt in counts]
    split = [gbig[n, layer].reshape(N_CHIPS, 2, h, D_MODEL) for (n, layer, _), h in zip(BIG, halves)]
    keep, got = _pair_swap_pack(split, "grad_pair_swap")
    chip_part = _pair_add(keep, got, "grad_pair_add")
    from_chips = _chip_exchange(chip_part, "grad_chip_exchange")
    my_half = _ordered_sum(from_chips, "grad_chip_sum")
    reduced = dict(zip([(n, layer) for n, layer, _ in BIG], _join_unpack(my_half, halves, "grad_join_halves")))

    small_names = SMALL_REPL + SMALL_SHARDED
    small_full_shapes = [grads[n].shape for n in small_names]
    spack = _pack_small([grads[n] for n in small_names], 8)
    s_all = _gather_devices(spack, "grad_small_gather")
    s_sum = _unpack_small(_ordered_sum(s_all, "grad_small_sum"), small_full_shapes)
    g_all = dict(zip(small_names, s_sum))
    for n in SMALL_SHARDED:
        width = wts[n].shape[-1]
        g_all[n] = lax.dynamic_slice_in_dim(g_all[n], chip * width, width, axis=g_all[n].ndim - 1)

    delta, new_m, new_v = {}, {}, {}
    for n in BIG_NAMES:
        layers = [layer for nn, layer, _ in BIG if nn == n]
        by_cols = [bc for nn, _, bc in BIG if nn == n][0]
        g_rows = jnp.stack([reduced[n, layer] for layer in layers])
        if by_cols:
            g_all[n], delta[n], new_m[n], new_v[n] = _adamw_t(wts[n], g_rows, mom[n], var[n], f"adamw_{n}")
        else:
            shape = wts[n].shape
            flat = [a.reshape(-1, D_MODEL) for a in (wts[n], g_rows, mom[n], var[n])]
            g_all[n] = g_rows
            delta[n], new_m[n], new_v[n] = (a.reshape(shape) for a in _adamw(*flat, f"adamw_{n}"))
    shapes = [wts[n].shape for n in small_names]
    d_s, m_s, v_s = _adamw(*[_pack_small([src[n] for n in small_names], 8) for src in (wts, g_all, mom, var)], "adamw_small")
    for dst, packed in ((delta, d_s), (new_m, m_s), (new_v, v_s)):
        dst.update(zip(small_names, _unpack_small(packed, shapes)))

    return (loss, grad_x[None], *[g_all[n] for n in WEIGHTS], *[delta[n] for n in WEIGHTS],
            *[new_m[n] for n in WEIGHTS], *[new_v[n] for n in WEIGHTS])
```

```python
import math

import jax
import jax.numpy as jnp
from jax import lax
from jax.experimental import pallas as pl
from jax.experimental.pallas import tpu as pltpu

F32 = jnp.float32
BF16 = jnp.bfloat16

D_MODEL = 1024
HEAD_DIM = 64
ROT_DIM = 16
ROPE_THETA = 500000.0
RMS_EPS = 1e-6
LN_EPS = 1e-5
BLOCK = 128
CONV_WIDTH = 31
CONV_HALO = 32
D_FF = 2816
N_GROUPS = 8
ATTN_W = 512
ATTN_SCALE = HEAD_DIM ** -0.5
NEG = -1e30
DILATIONS = (1, 4, 16)

ADAM_LR = 0.001
ADAM_B1 = 0.9
ADAM_B2 = 0.999
ADAM_EPS = 1e-08
ADAM_WD = 0.01
ADAM_STEP = 10

LANES = 128
N_PAIRS = ATTN_W // LANES
VMEM_LIMIT = 56 * 1024 * 1024
MESH = pl.DeviceIdType.MESH
N_CHIPS = 4
N_DEV = 8

WEIGHTS = ['ev_norm_g', 'ev_w_in', 'ev_sinks', 'ev_conv_w', 'ev_conv_b', 'ev_conv_ln_g', 'ev_conv_ln_b', 'ev_w_out',
           'od_norm_g', 'od_w_in', 'od_sgu_ln_g', 'od_sgu_ln_b', 'od_spatial_w', 'od_spatial_b', 'od_w_out',
           'ffn_norm_g', 'ffn_w_gate', 'ffn_w_up', 'ffn_w_down', 'final_norm_g']
BIG = [('ev_w_in', 0, True), ('ev_w_out', 0, False), ('od_w_in', 0, True), ('od_w_out', 0, False),
       ('ffn_w_gate', 0, True), ('ffn_w_gate', 1, True), ('ffn_w_up', 0, True), ('ffn_w_up', 1, True),
       ('ffn_w_down', 0, False), ('ffn_w_down', 1, False)]
BIG_NAMES = ['ev_w_in', 'ev_w_out', 'od_w_in', 'od_w_out', 'ffn_w_gate', 'ffn_w_up', 'ffn_w_down']
SMALL_SHARDED = ['ev_conv_w', 'od_norm_g', 'od_sgu_ln_g', 'od_sgu_ln_b']
SMALL_REPL = ['ev_norm_g', 'ev_sinks', 'ev_conv_b', 'ev_conv_ln_g', 'ev_conv_ln_b', 'od_spatial_w', 'od_spatial_b',
              'ffn_norm_g', 'final_norm_g']


def _tile(n, cap, mult=LANES):
    best = None
    for t in range(mult, min(n, cap) + 1, mult):
        if n % t == 0:
            best = t
    assert best is not None, (n, cap)
    return best


def _params(*sem):
    return pltpu.CompilerParams(dimension_semantics=sem, vmem_limit_bytes=VMEM_LIMIT)


def _sigmoid(x):
    return 1.0 / (1.0 + jnp.exp(-x))


def _pair_block(p):
    return slice(p * LANES, (p + 1) * LANES)


def _matmul(a, b, *, name, trans_a=False, trans_b=False, add=None, out_dtype=F32):
    parts = a if isinstance(a, (tuple, list)) else (a,)
    if trans_a:
        k, m = parts[0].shape
    else:
        m = parts[0].shape[0]
        k = sum(p.shape[1] for p in parts)
    if trans_b:
        n, k2 = b.shape
    else:
        k2, n = b.shape
    assert k == k2 and b.dtype == BF16 and all(p.dtype == BF16 for p in parts)
    tm = _tile(m, D_FF // 2 if trans_a else 512)
    tn = _tile(n, D_FF // 2)
    tk = k if k <= D_FF else _tile(k, 1024)
    nk = k // tk
    na = len(parts)
    assert na == 1 or (nk == 1 and not trans_a)
    dims = (((0 if trans_a else 1,), (1 if trans_b else 0,)), ((), ()))
    has_add = add is not None

    def body(*refs):
        a_refs, b_ref = refs[:na], refs[na]
        add_ref = refs[na + 1] if has_add else None
        o_ref = refs[na + 1 + has_add]
        a_val = a_refs[0][...] if na == 1 else jnp.concatenate([r[...] for r in a_refs], axis=1)
        part = lax.dot_general(a_val, b_ref[...], dims, preferred_element_type=F32)
        if nk == 1:
            if has_add:
                part = part + add_ref[...]
            o_ref[...] = part.astype(o_ref.dtype)
            return
        acc_ref = refs[-1]
        kk = pl.program_id(2)

        @pl.when(kk == 0)
        def _():
            acc_ref[...] = part

        @pl.when(kk > 0)
        def _():
            acc_ref[...] += part

        @pl.when(kk == nk - 1)
        def _():
            res = acc_ref[...]
            if has_add:
                res = res + add_ref[...]
            o_ref[...] = res.astype(o_ref.dtype)

    if trans_a:
        a_specs = [pl.BlockSpec((tk, tm), lambda i, j, kk: (kk, i))]
    elif na == 1:
        a_specs = [pl.BlockSpec((tm, tk), lambda i, j, kk: (i, kk))]
    else:
        a_specs = [pl.BlockSpec((tm, p.shape[1]), lambda i, j, kk: (i, 0)) for p in parts]
    b_spec = pl.BlockSpec((tn, tk), lambda i, j, kk: (j, kk)) if trans_b else pl.BlockSpec((tk, tn), lambda i, j, kk: (kk, j))
    o_spec = pl.BlockSpec((tm, tn), lambda i, j, kk: (i, j))
    in_specs = a_specs + [b_spec] + ([o_spec] if has_add else [])
    operands = list(parts) + [b] + ([add] if has_add else [])
    return pl.pallas_call(
        body, name=name, grid=(m // tm, n // tn, nk), in_specs=in_specs, out_specs=o_spec,
        out_shape=jax.ShapeDtypeStruct((m, n), out_dtype),
        scratch_shapes=[pltpu.VMEM((tm, tn), F32)] if nk > 1 else [],
        compiler_params=_params("parallel", "parallel", "arbitrary"),
    )(*operands)


def _matmul_tn_pair(a1, a2, b, name):
    kdim, m1 = a1.shape
    m2 = a2.shape[1]
    n = b.shape[1]
    tn = _tile(n, 1024)
    tk = _tile(kdim, 1024)
    nk = kdim // tk
    dims = (((0,), (0,)), ((), ()))

    def body(a1_ref, a2_ref, b_ref, o_ref):
        kk = pl.program_id(1)
        bv = b_ref[...]
        top = lax.dot_general(a1_ref[...], bv, dims, preferred_element_type=F32)
        bot = lax.dot_general(a2_ref[...], bv, dims, preferred_element_type=F32)

        @pl.when(kk == 0)
        def _():
            o_ref[0:m1, :] = top
            o_ref[m1:, :] = bot

        @pl.when(kk > 0)
        def _():
            o_ref[0:m1, :] += top
            o_ref[m1:, :] += bot

    return pl.pallas_call(
        body, name=name, grid=(n // tn, nk),
        in_specs=[pl.BlockSpec((tk, m1), lambda j, kk: (kk, 0)), pl.BlockSpec((tk, m2), lambda j, kk: (kk, 0)),
                  pl.BlockSpec((tk, tn), lambda j, kk: (kk, j))],
        out_specs=pl.BlockSpec((m1 + m2, tn), lambda j, kk: (0, j)),
        out_shape=jax.ShapeDtypeStruct((m1 + m2, n), F32),
        compiler_params=_params("parallel", "arbitrary"),
    )(a1, a2, b)


def _rows(body, name, tm, tiled, consts, outs, accs=()):
    s = tiled[0].shape[0]
    assert s % tm == 0
    in_specs = [pl.BlockSpec((tm, a.shape[1]), lambda i: (i, 0)) for a in tiled]
    in_specs += [pl.BlockSpec(a.shape, lambda i, nd=a.ndim: (0,) * nd) for a in consts]
    out_shape = [jax.ShapeDtypeStruct((s, c), dt) for c, dt in outs]
    out_shape += [jax.ShapeDtypeStruct(sh, dt) for sh, dt in accs]
    out_specs = [pl.BlockSpec((tm, c), lambda i: (i, 0)) for c, _ in outs]
    out_specs += [pl.BlockSpec(sh, lambda i, nd=len(sh): (0,) * nd) for sh, _ in accs]
    return pl.pallas_call(
        body, name=name, grid=(s // tm,), in_specs=in_specs, out_specs=out_specs, out_shape=out_shape,
        compiler_params=_params("arbitrary"),
    )(*tiled, *consts)


def _first_step():
    return pl.program_id(0) == 0


def _rms_fwd(h, g, name):
    def body(h_ref, g_ref, n_ref):
        x = h_ref[...]
        r = lax.rsqrt(jnp.mean(x * x, axis=-1, keepdims=True) + RMS_EPS)
        n_ref[...] = (x * r * g_ref[...]).astype(BF16)

    return _rows(body, name, 512, [h], [g], [(D_MODEL, BF16)])[0]


def _rms_bwd(dn, h, g, dres, name):
    def body(dn_ref, h_ref, dres_ref, g_ref, dh_ref, dhb_ref, dg_ref):
        @pl.when(_first_step())
        def _():
            dg_ref[...] = jnp.zeros_like(dg_ref)

        x = h_ref[...]
        r = lax.rsqrt(jnp.mean(x * x, axis=-1, keepdims=True) + RMS_EPS)
        xh = x * r
        dy = dn_ref[...]
        dg_ref[...] += jnp.sum(dy * xh, axis=0, keepdims=True)
        dxh = dy * g_ref[...]
        tot = dres_ref[...] + r * (dxh - xh * jnp.mean(dxh * xh, axis=-1, keepdims=True))
        dh_ref[...] = tot
        dhb_ref[...] = tot.astype(BF16)

    return _rows(body, name, 512, [dn, h, dres], [g], [(D_MODEL, F32), (D_MODEL, BF16)], [((1, D_MODEL), F32)])


def _final_loss(h, g, tgt, name):
    def body(h_ref, t_ref, g_ref, dh_ref, dhb_ref, dg_ref, loss_ref):
        @pl.when(_first_step())
        def _():
            dg_ref[...] = jnp.zeros_like(dg_ref)
            loss_ref[...] = jnp.zeros_like(loss_ref)

        x = h_ref[...]
        r = lax.rsqrt(jnp.mean(x * x, axis=-1, keepdims=True) + RMS_EPS)
        xh = x * r
        gg = g_ref[...]
        e = xh * gg - t_ref[...]
        loss_ref[...] += (0.5 / D_MODEL) * jnp.sum(jnp.sum(e * e, axis=-1, keepdims=True), axis=0, keepdims=True)
        dy = e * (1.0 / D_MODEL)
        dg_ref[...] += jnp.sum(dy * xh, axis=0, keepdims=True)
        dxh = dy * gg
        dx = r * (dxh - xh * jnp.mean(dxh * xh, axis=-1, keepdims=True))
        dh_ref[...] = dx
        dhb_ref[...] = dx.astype(BF16)

    return _rows(body, name, 512, [h, tgt], [g], [(D_MODEL, F32), (D_MODEL, BF16)],
                 [((1, D_MODEL), F32), ((1, LANES), F32)])


def _swiglu_fwd(gate, up, name):
    def body(g_ref, u_ref, a_ref):
        g = g_ref[...]
        a_ref[...] = (g * _sigmoid(g) * u_ref[...]).astype(BF16)

    return _rows(body, name, 256, [gate, up], [], [(D_FF, BF16)])[0]


def _swiglu_bwd(dact, gate, up, name):
    def body(d_ref, g_ref, u_ref, dg_ref, du_ref):
        g = g_ref[...]
        d = d_ref[...]
        sg = _sigmoid(g)
        dg_ref[...] = (d * u_ref[...] * sg * (1.0 + g * (1.0 - sg))).astype(BF16)
        du_ref[...] = (d * g * sg).astype(BF16)

    return _rows(body, name, 256, [dact, gate, up], [], [(D_FF, BF16), (D_FF, BF16)])


def _rope_tables(s):
    half = ROT_DIM // 2
    inv_freq = ROPE_THETA ** (-jnp.arange(half, dtype=F32) * (2.0 / ROT_DIM))
    ang = jnp.arange(s, dtype=F32)[:, None] * inv_freq[None, :]
    cos, sin = jnp.cos(ang), jnp.sin(ang)
    rest = HEAD_DIM - ROT_DIM
    ones = jnp.ones((s, rest), F32)
    zeros = jnp.zeros((s, rest), F32)
    zh = jnp.zeros((s, half), F32)
    c_t = jnp.concatenate([cos, cos, ones], axis=1)
    a_t = jnp.concatenate([-sin, zh, zeros], axis=1)
    b_t = jnp.concatenate([zh, sin, zeros], axis=1)
    return tuple(jnp.tile(t, (1, LANES // HEAD_DIM)) for t in (c_t, a_t, b_t))


def _rot(x, c, a, b):
    w = x.shape[1]
    half = ROT_DIM // 2
    return x * c + pltpu.roll(x, w - half, 1) * a + pltpu.roll(x, half, 1) * b


def _wide(t, w):
    return t if w == LANES else jnp.tile(t, (1, w // LANES))


def _low_lanes(rows):
    return lax.broadcasted_iota(jnp.int32, (rows, LANES), 1) < HEAD_DIM


def _fold_store(x, sc_ref, out_refs):
    tm = x.shape[0]
    if any(d > 1 for d in out_refs):
        for p in range(N_PAIRS):
            sc_ref[p] = x[:, _pair_block(p)]
    for d, o_ref in out_refs.items():
        if d == 1:
            o_ref[0] = x.astype(o_ref.dtype)
            continue
        for r in range(d):
            for p in range(N_PAIRS):
                o_ref[r, :, _pair_block(p)] = sc_ref[p, pl.ds(r, tm // d, stride=d), :].astype(o_ref.dtype)


def _unfold_load(x_ref, sc_ref, d, add=False):
    n = x_ref.shape[1]
    for r in range(d):
        for p in range(N_PAIRS):
            rows = pl.ds(r, n, stride=d) if d > 1 else slice(None)
            val = x_ref[r, :, _pair_block(p)].astype(F32)
            if add:
                val = val + sc_ref[p, rows, :]
            sc_ref[p, rows, :] = val


def _folded_spec(d, tm, w=ATTN_W):
    return pl.BlockSpec((d, tm // d, w), lambda i: (0, i, 0))


def _folded_shape(s, d, dtype, w=ATTN_W):
    return jax.ShapeDtypeStruct((d, s // d, w), dtype)


def _qkv_prep_even(proj, tabs, name):
    s = proj.shape[0]
    tm = 512

    def body(p_ref, c_ref, a_ref, b_ref, q_ref, k_ref, v_ref):
        c, a, b = c_ref[...], a_ref[...], b_ref[...]
        q_ref[0] = _rot(p_ref[:, 0:ATTN_W], _wide(c, ATTN_W), _wide(a, ATTN_W), _wide(b, ATTN_W)).astype(BF16)
        lo = _low_lanes(tm)
        for src, o_ref in ((_rot(p_ref[:, 512:640], c, a, b), k_ref), (p_ref[:, 640:768], v_ref)):
            swapped = pltpu.roll(src, HEAD_DIM, 1)
            o_ref[0, :, 0:LANES] = jnp.where(lo, src, swapped).astype(BF16)
            o_ref[0, :, LANES:] = jnp.where(lo, swapped, src).astype(BF16)

    row = lambda w: pl.BlockSpec((tm, w), lambda i: (i, 0))
    return pl.pallas_call(
        body, name=name, grid=(s // tm,), in_specs=[row(proj.shape[1]), row(LANES), row(LANES), row(LANES)],
        out_specs=[_folded_spec(1, tm), _folded_spec(1, tm, 2 * LANES), _folded_spec(1, tm, 2 * LANES)],
        out_shape=[_folded_shape(s, 1, BF16), _folded_shape(s, 1, BF16, 2 * LANES), _folded_shape(s, 1, BF16, 2 * LANES)],
        compiler_params=_params("parallel"),
    )(proj, *tabs)


def _qkv_post_even(dq, dk, dv, dglu, tabs, name):
    s = dglu.shape[0]
    tm = 512

    def body(dq_ref, dk_ref, dv_ref, dr_ref, c_ref, a_ref, b_ref, o_ref):
        c, a, b = c_ref[...], -a_ref[...], -b_ref[...]
        o_ref[:, 0:ATTN_W] = _rot(dq_ref[0], _wide(c, ATTN_W), _wide(a, ATTN_W), _wide(b, ATTN_W)).astype(BF16)
        lo = _low_lanes(tm)
        merged = []
        for ref in (dk_ref, dv_ref):
            first, second = ref[0, :, 0:LANES], ref[0, :, LANES:]
            merged.append(jnp.where(lo, first + pltpu.roll(first, HEAD_DIM, 1), second + pltpu.roll(second, HEAD_DIM, 1)))
        o_ref[:, 512:640] = _rot(merged[0], c, a, b).astype(BF16)
        o_ref[:, 640:768] = merged[1].astype(BF16)
        o_ref[:, 768:] = dr_ref[...]

    row = lambda w: pl.BlockSpec((tm, w), lambda i: (i, 0))
    return pl.pallas_call(
        body, name=name, grid=(s // tm,),
        in_specs=[_folded_spec(1, tm), _folded_spec(1, tm, 2 * LANES), _folded_spec(1, tm, 2 * LANES),
                  row(dglu.shape[1]), row(LANES), row(LANES), row(LANES)],
        out_specs=row(EVEN_IN), out_shape=jax.ShapeDtypeStruct((s, EVEN_IN), BF16),
        compiler_params=_params("parallel"),
    )(dq, dk, dv, dglu, *tabs)


def _qkv_prep_odd(proj, tabs, name):
    s = proj.shape[0]
    tm = 512

    def body(p_ref, c_ref, a_ref, b_ref, *rest):
        outs, sc_ref = rest[:-1], rest[-1]
        c, a, b = (_wide(t[...], ATTN_W) for t in (c_ref, a_ref, b_ref))
        for t in range(3):
            x = p_ref[:, t * ATTN_W:(t + 1) * ATTN_W]
            if t < 2:
                x = _rot(x, c, a, b)
            _fold_store(x, sc_ref, {d: outs[t * len(DILATIONS) + i] for i, d in enumerate(DILATIONS)})

    row = lambda w: pl.BlockSpec((tm, w), lambda i: (i, 0))
    return pl.pallas_call(
        body, name=name, grid=(s // tm,), in_specs=[row(proj.shape[1]), row(LANES), row(LANES), row(LANES)],
        out_specs=[_folded_spec(d, tm) for _ in range(3) for d in DILATIONS],
        out_shape=[_folded_shape(s, d, BF16) for _ in range(3) for d in DILATIONS],
        scratch_shapes=[pltpu.VMEM((N_PAIRS, tm, LANES), F32)],
        compiler_params=_params("parallel"),
    )(proj, *tabs)


def _qkv_post_odd(dqs, dks, dvs, dz, tabs, name):
    s = dz.shape[0]
    tm = 256
    nb = len(DILATIONS)

    def body(*refs):
        groups = (refs[:nb], refs[nb:2 * nb], refs[2 * nb:3 * nb])
        dz_ref, c_ref, a_ref, b_ref, o_ref, sc_ref = refs[3 * nb:]
        c, a, b = _wide(c_ref[...], ATTN_W), _wide(-a_ref[...], ATTN_W), _wide(-b_ref[...], ATTN_W)
        for t, group in enumerate(groups):
            for i, d in enumerate(DILATIONS):
                _unfold_load(group[i], sc_ref, d, add=i > 0)
            x = jnp.concatenate([sc_ref[p] for p in range(N_PAIRS)], axis=1)
            if t < 2:
                x = _rot(x, c, a, b)
            o_ref[:, t * ATTN_W:(t + 1) * ATTN_W] = x.astype(BF16)
        o_ref[:, 3 * ATTN_W:] = dz_ref[...]

    row = lambda w: pl.BlockSpec((tm, w), lambda i: (i, 0))
    return pl.pallas_call(
        body, name=name, grid=(s // tm,),
        in_specs=[_folded_spec(d, tm) for _ in range(3) for d in DILATIONS] + [row(dz.shape[1]), row(LANES), row(LANES), row(LANES)],
        out_specs=row(ODD_IN), out_shape=jax.ShapeDtypeStruct((s, ODD_IN), BF16),
        scratch_shapes=[pltpu.VMEM((N_PAIRS, tm, LANES), F32)],
        compiler_params=_params("parallel"),
    )(*dqs, *dks, *dvs, dz, *tabs)


def _fold_dout(dmix, name):
    s = dmix.shape[0]
    tm = 512
    ds = [d for d in DILATIONS if d > 1]

    def body(d_ref, *rest):
        outs, sc_ref = rest[:-1], rest[-1]
        _fold_store(d_ref[...], sc_ref, dict(zip(ds, outs)))

    return pl.pallas_call(
        body, name=name, grid=(s // tm,), in_specs=[pl.BlockSpec((tm, ATTN_W), lambda i: (i, 0))],
        out_specs=[_folded_spec(d, tm) for d in ds], out_shape=[_folded_shape(s, d, BF16) for d in ds],
        scratch_shapes=[pltpu.VMEM((N_PAIRS, tm, LANES), F32)],
        compiler_params=_params("parallel"),
    )(dmix)


def _window(j, i, tq):
    r0 = j * tq + i * BLOCK
    start = pl.multiple_of(jnp.maximum(r0 - BLOCK, 0), BLOCK)
    return pl.ds(start, 2 * BLOCK), r0 - start


def _band_valid(offset, max_dist):
    dist = (lax.broadcasted_iota(jnp.int32, (BLOCK, 2 * BLOCK), 0)
            - lax.broadcasted_iota(jnp.int32, (BLOCK, 2 * BLOCK), 1) + offset)
    return jnp.abs(2 * dist - max_dist) <= max_dist


def _one_head(lo, h, x, other):
    return jnp.where(lo, x, other) if h == 0 else jnp.where(lo, other, x)


NT = (((1,), (1,)), ((), ()))
TN = (((0,), (0,)), ((), ()))


def _attn_fwd(q, k, v, sinks, *, max_dist, name, emit_bf16=False):
    d, sp, wq = q.shape
    nq, nk = wq // LANES, k.shape[2] // LANES
    kdiv = nq // nk
    tq = min(sp, 1024)
    nsub = tq // BLOCK
    has_sink = sinks is not None

    def body(*refs):
        refs = list(refs)
        sink_ref = refs.pop(0) if has_sink else None
        q_ref, k_ref, v_ref, o_ref, lse_ref = refs[:5]
        pair = pl.program_id(1)
        j = pl.program_id(2)
        lo = _low_lanes(BLOCK)
        for i in range(nsub):
            win, offset = _window(j, i, tq)
            valid = _band_valid(offset, max_dist)
            rows = slice(i * BLOCK, (i + 1) * BLOCK)
            q2 = q_ref[0, rows, :]
            kw = k_ref[0, win, :]
            vw = v_ref[0, win, :]
            zero = jnp.zeros_like(q2)
            outs, lses = [], []
            for h in range(2):
                s = lax.dot_general(_one_head(lo, h, q2, zero), kw, NT, preferred_element_type=F32) * ATTN_SCALE
                s = jnp.where(valid, s, NEG)
                m = jnp.max(s, axis=-1, keepdims=True)
                if has_sink:
                    sk = sink_ref[2 * pair + h]
                    m = jnp.maximum(m, sk)
                p = jnp.exp(s - m)
                l = jnp.sum(p, axis=-1, keepdims=True)
                if has_sink:
                    l = l + jnp.exp(sk - m)
                outs.append(jnp.dot(p.astype(BF16), vw, preferred_element_type=F32) / l)
                lses.append(m + jnp.log(l))
            o2 = jnp.where(lo, outs[0], outs[1])
            o_ref[0, rows, :] = o2
            lse_ref[0, rows, :] = jnp.where(lo, lses[0], lses[1])
            if emit_bf16:
                refs[5][0, rows, :] = o2.astype(BF16)

    qspec = pl.BlockSpec((1, tq, LANES), lambda r, p, j: (r, j, p))
    kspec = pl.BlockSpec((1, sp, LANES), lambda r, p, j: (r, 0, p // kdiv))
    in_specs = [qspec, kspec, kspec]
    operands = [q, k, v]
    if has_sink:
        in_specs = [pl.BlockSpec(memory_space=pltpu.SMEM)] + in_specs
        operands = [sinks] + operands
    out_shape = [jax.ShapeDtypeStruct(q.shape, F32), jax.ShapeDtypeStruct(q.shape, F32)]
    if emit_bf16:
        out_shape.append(jax.ShapeDtypeStruct(q.shape, BF16))
    return pl.pallas_call(
        body, name=name, grid=(d, nq, sp // tq), in_specs=in_specs, out_specs=[qspec] * len(out_shape),
        out_shape=out_shape, compiler_params=_params("parallel", "parallel", "arbitrary"),
    )(*operands)


def _attn_bwd(q, k, v, do, oo, lse, sinks, *, max_dist, name):
    d, sp, wq = q.shape
    wk = k.shape[2]
    nq, nk = wq // LANES, wk // LANES
    kdiv = nq // nk
    tq = min(sp, 1024)
    nsub = tq // BLOCK
    has_sink = sinks is not None

    def body(*refs):
        refs = list(refs)
        sink_ref = refs.pop(0) if has_sink else None
        q_ref, k_ref, v_ref, do_ref, oo_ref, lse_ref, dq_ref, dk_ref, dv_ref = refs[:9]
        pk, g, j = pl.program_id(1), pl.program_id(2), pl.program_id(3)

        @pl.when((g == 0) & (j == 0))
        def _():
            dk_ref[...] = jnp.zeros_like(dk_ref)
            dv_ref[...] = jnp.zeros_like(dv_ref)

        lo = _low_lanes(BLOCK)
        sink_acc = [jnp.zeros((1, LANES), F32), jnp.zeros((1, LANES), F32)]
        for i in range(nsub):
            win, offset = _window(j, i, tq)
            valid = _band_valid(offset, max_dist)
            rows = slice(i * BLOCK, (i + 1) * BLOCK)
            q2 = q_ref[0, rows, :]
            kw = k_ref[0, win, :]
            vw = v_ref[0, win, :]
            do2 = do_ref[0, rows, :].astype(F32)
            dob = do2.astype(BF16)
            prod = do2 * oo_ref[0, rows, :]
            lse2 = lse_ref[0, rows, :]
            lse_swapped = pltpu.roll(lse2, HEAD_DIM, 1)
            zero = jnp.zeros_like(q2)
            dqs, dks, dvs = [], [], []
            for h in range(2):
                s = lax.dot_general(_one_head(lo, h, q2, zero), kw, NT, preferred_element_type=F32) * ATTN_SCALE
                s = jnp.where(valid, s, NEG)
                lse_h = _one_head(lo, h, lse2, lse_swapped)
                p = jnp.exp(s - jnp.tile(lse_h, (1, 2)))
                delta = jnp.sum(_one_head(lo, h, prod, 0.0), axis=-1, keepdims=True)
                dvs.append(lax.dot_general(p.astype(BF16), dob, TN, preferred_element_type=F32))
                dp = lax.dot_general(_one_head(lo, h, dob, zero), vw, NT, preferred_element_type=F32)
                ds = (p * (dp - delta) * ATTN_SCALE).astype(BF16)
                dqs.append(jnp.dot(ds, kw, preferred_element_type=F32))
                dks.append(lax.dot_general(ds, q2, TN, preferred_element_type=F32))
                if has_sink:
                    sk = sink_ref[2 * (pk * kdiv + g) + h]
                    sink_acc[h] = sink_acc[h] - jnp.sum(jnp.exp(sk - lse_h) * delta, axis=0, keepdims=True)
            lo2 = lax.broadcasted_iota(jnp.int32, (2 * BLOCK, LANES), 1) < HEAD_DIM
            dq_ref[0, rows, :] = jnp.where(lo, dqs[0], dqs[1])
            dk_ref[0, win, :] += jnp.where(lo2, dks[0], dks[1])
            dv_ref[0, win, :] += jnp.where(lo2, dvs[0], dvs[1])
        if has_sink:
            dsink_ref = refs[9]

            @pl.when(j == 0)
            def _():
                dsink_ref[...] = jnp.zeros_like(dsink_ref)

            dsink_ref[0] += jnp.where(lo[0:1], sink_acc[0], sink_acc[1])

    def qmap(r, pk, g, j):
        return (r, j, pk * kdiv + g)

    def kmap(r, pk, g, j):
        return (r, 0, pk)

    qspec = pl.BlockSpec((1, tq, LANES), qmap)
    kspec = pl.BlockSpec((1, sp, LANES), kmap)
    in_specs = [qspec, kspec, kspec, qspec, qspec, qspec]
    operands = [q, k, v, do, oo, lse]
    out_specs = [qspec, kspec, kspec]
    out_shape = [jax.ShapeDtypeStruct((d, sp, wq), F32), jax.ShapeDtypeStruct((d, sp, wk), F32),
                 jax.ShapeDtypeStruct((d, sp, wk), F32)]
    if has_sink:
        in_specs = [pl.BlockSpec(memory_space=pltpu.SMEM)] + in_specs
        operands = [sinks] + operands
        out_specs.append(pl.BlockSpec((1, 1, LANES), lambda r, pk, g, j: (pk * kdiv + g, 0, 0)))
        out_shape.append(jax.ShapeDtypeStruct((nq, 1, LANES), F32))
    return pl.pallas_call(
        body, name=name, grid=(d, nk, kdiv, sp // tq), in_specs=in_specs, out_specs=out_specs, out_shape=out_shape,
        compiler_params=_params("parallel", "parallel", "arbitrary", "arbitrary"),
    )(*operands)


def _combine(outs, lses, name):
    s = outs[0].shape[1]
    tm = 512
    nb = len(DILATIONS)
    ds = [d for d in DILATIONS if d > 1]

    def body(*refs):
        o_refs, l_refs = refs[:nb], refs[nb:2 * nb]
        cb_ref, c_ref, lse_ref = refs[2 * nb:2 * nb + 3]
        folded = refs[2 * nb + 3:2 * nb + 3 + 2 * len(ds)]
        scratch = refs[2 * nb + 3 + 2 * len(ds):]
        so = {1: None}
        sl = {1: None}
        for i, d in enumerate(ds):
            so[d], sl[d] = scratch[2 * i], scratch[2 * i + 1]
            _unfold_load(o_refs[1 + i], so[d], d)
            _unfold_load(l_refs[1 + i], sl[d], d)
        for p in range(N_PAIRS):
            pb = _pair_block(p)
            ls = [l_refs[0][0, :, pb]] + [sl[d][p] for d in ds]
            os_ = [o_refs[0][0, :, pb]] + [so[d][p] for d in ds]
            m = ls[0]
            for t in ls[1:]:
                m = jnp.maximum(m, t)
            ws = [jnp.exp(t - m) for t in ls]
            tot = ws[0]
            for t in ws[1:]:
                tot = tot + t
            acc = ws[0] * os_[0]
            for w, o in zip(ws[1:], os_[1:]):
                acc = acc + w * o
            cmix = acc / tot
            lse = m + jnp.log(tot)
            cb_ref[:, pb] = cmix.astype(BF16)
            c_ref[0, :, pb] = cmix
            lse_ref[0, :, pb] = lse
            so[ds[0]][p] = cmix
            sl[ds[0]][p] = lse
        for i, d in enumerate(ds):
            for r in range(d):
                for p in range(N_PAIRS):
                    rows = pl.ds(r, tm // d, stride=d)
                    folded[2 * i][r, :, _pair_block(p)] = so[ds[0]][p, rows, :]
                    folded[2 * i + 1][r, :, _pair_block(p)] = sl[ds[0]][p, rows, :]

    in_specs = [_folded_spec(d, tm) for _ in range(2) for d in DILATIONS]
    out_specs = [pl.BlockSpec((tm, ATTN_W), lambda i: (i, 0)), _folded_spec(1, tm), _folded_spec(1, tm)]
    out_shape = [jax.ShapeDtypeStruct((s, ATTN_W), BF16), _folded_shape(s, 1, F32), _folded_shape(s, 1, F32)]
    for d in ds:
        out_specs += [_folded_spec(d, tm)] * 2
        out_shape += [_folded_shape(s, d, F32)] * 2
    return pl.pallas_call(
        body, name=name, grid=(s // tm,), in_specs=in_specs, out_specs=out_specs, out_shape=out_shape,
        scratch_shapes=[pltpu.VMEM((N_PAIRS, tm, LANES), F32)] * (2 * len(ds)),
        compiler_params=_params("parallel"),
    )(*outs, *lses)


GLU_A = slice(768, 1280)
GLU_B = slice(1280, 1792)
EVEN_IN = 1792
ODD_IN = 2560
CONV_CH = 512


def _conv_fwd(proj, w, b, ln_g, ln_b, name):
    s = proj.shape[0]
    tm = 512
    nh = tm // CONV_HALO
    lead = CONV_HALO - (CONV_WIDTH - 1)

    def body(p_ref, ph_ref, w_ref, b_ref, g_ref, bb_ref, y_ref, o_ref, xf_ref):
        xf_ref[CONV_HALO:, :] = p_ref[:, GLU_A] * _sigmoid(p_ref[:, GLU_B])
        hist = ph_ref[:, GLU_A] * _sigmoid(ph_ref[:, GLU_B])
        xf_ref[0:CONV_HALO, :] = jnp.where(pl.program_id(0) > 0, hist, 0.0)
        acc = jnp.zeros((tm, CONV_CH), F32) + b_ref[...]
        for j in range(CONV_WIDTH):
            acc = acc + xf_ref[pl.ds(lead + j, tm), :] * w_ref[j:j + 1, :]
        y_ref[...] = acc
        mu = jnp.mean(acc, axis=-1, keepdims=True)
        xc = acc - mu
        var = jnp.mean(xc * xc, axis=-1, keepdims=True)
        zz = xc * lax.rsqrt(var + LN_EPS) * g_ref[...] + bb_ref[...]
        o_ref[...] = (zz * _sigmoid(zz)).astype(BF16)

    def const(a):
        return pl.BlockSpec(a.shape, lambda i: (0, 0))

    return pl.pallas_call(
        body, name=name, grid=(s // tm,),
        in_specs=[pl.BlockSpec((tm, EVEN_IN), lambda i: (i, 0)),
                  pl.BlockSpec((CONV_HALO, EVEN_IN), lambda i: (jnp.maximum(i * nh - 1, 0), 0)),
                  const(w), const(b), const(ln_g), const(ln_b)],
        out_specs=[pl.BlockSpec((tm, CONV_CH), lambda i: (i, 0)), pl.BlockSpec((tm, CONV_CH), lambda i: (i, 0))],
        out_shape=[jax.ShapeDtypeStruct((s, CONV_CH), F32), jax.ShapeDtypeStruct((s, CONV_CH), BF16)],
        scratch_shapes=[pltpu.VMEM((tm + CONV_HALO, CONV_CH), F32)],
        compiler_params=_params("arbitrary"),
    )(proj, proj, w, b, ln_g, ln_b)


def _conv_tail_bwd(dmix, yconv, ln_g, ln_b, name):
    def body(d_ref, y_ref, g_ref, b_ref, dy_ref, dg_ref, db_ref, dcb_ref):
        @pl.when(_first_step())
        def _():
            dg_ref[...] = jnp.zeros_like(dg_ref)
            db_ref[...] = jnp.zeros_like(db_ref)
            dcb_ref[...] = jnp.zeros_like(dcb_ref)

        y = y_ref[...]
        g = g_ref[...]
        mu = jnp.mean(y, axis=-1, keepdims=True)
        xc = y - mu
        rstd = lax.rsqrt(jnp.mean(xc * xc, axis=-1, keepdims=True) + LN_EPS)
        xh = xc * rstd
        zz = xh * g + b_ref[...]
        sg = _sigmoid(zz)
        dzz = d_ref[:, CONV_CH:] * sg * (1.0 + zz * (1.0 - sg))
        dg_ref[...] += jnp.sum(dzz * xh, axis=0, keepdims=True)
        db_ref[...] += jnp.sum(dzz, axis=0, keepdims=True)
        dxh = dzz * g
        dy = rstd * (dxh - jnp.mean(dxh, axis=-1, keepdims=True) - xh * jnp.mean(dxh * xh, axis=-1, keepdims=True))
        dcb_ref[...] += jnp.sum(dy, axis=0, keepdims=True)
        dy_ref[...] = dy

    vec = ((1, CONV_CH), F32)
    return _rows(body, name, 512, [dmix, yconv], [ln_g, ln_b], [(CONV_CH, F32)], [vec, vec, vec])


def _conv_bwd(proj, dy, w, name):
    s = proj.shape[0]
    tm = 512
    nh = tm // CONV_HALO
    nsteps = s // tm
    lead = CONV_HALO - (CONV_WIDTH - 1)

    def body(p_ref, ph_ref, dy_ref, dyn_ref, w_ref, dglu_ref, dw_ref, xf_ref, dyf_ref):
        i = pl.program_id(0)

        @pl.when(i == 0)
        def _():
            dw_ref[...] = jnp.zeros_like(dw_ref)

        ga = p_ref[:, GLU_A]
        sgb = _sigmoid(p_ref[:, GLU_B])
        xf_ref[CONV_HALO:, :] = ga * sgb
        hist = ph_ref[:, GLU_A] * _sigmoid(ph_ref[:, GLU_B])
        xf_ref[0:CONV_HALO, :] = jnp.where(i > 0, hist, 0.0)
        dyt = dy_ref[...]
        dyf_ref[0:tm, :] = dyt
        dyf_ref[tm:, :] = jnp.where(i < nsteps - 1, dyn_ref[...], 0.0)
        acc = jnp.zeros((tm, CONV_CH), F32)
        for j in range(CONV_WIDTH):
            acc = acc + dyf_ref[pl.ds(CONV_WIDTH - 1 - j, tm), :] * w_ref[j:j + 1, :]
        for j in range(CONV_WIDTH):
            dw_ref[j:j + 1, :] += jnp.sum(dyt * xf_ref[pl.ds(lead + j, tm), :], axis=0, keepdims=True)
        dglu_ref[:, 0:CONV_CH] = (acc * sgb).astype(BF16)
        dglu_ref[:, CONV_CH:] = (acc * ga * sgb * (1.0 - sgb)).astype(BF16)

    return pl.pallas_call(
        body, name=name, grid=(nsteps,),
        in_specs=[pl.BlockSpec((tm, EVEN_IN), lambda i: (i, 0)),
                  pl.BlockSpec((CONV_HALO, EVEN_IN), lambda i: (jnp.maximum(i * nh - 1, 0), 0)),
                  pl.BlockSpec((tm, CONV_CH), lambda i: (i, 0)),
                  pl.BlockSpec((CONV_HALO, CONV_CH), lambda i: (jnp.minimum((i + 1) * nh, s // CONV_HALO - 1), 0)),
                  pl.BlockSpec(w.shape, lambda i: (0, 0))],
        out_specs=[pl.BlockSpec((tm, 2 * CONV_CH), lambda i: (i, 0)), pl.BlockSpec(w.shape, lambda i: (0, 0))],
        out_shape=[jax.ShapeDtypeStruct((s, 2 * CONV_CH), BF16), jax.ShapeDtypeStruct(w.shape, F32)],
        scratch_shapes=[pltpu.VMEM((tm + CONV_HALO, CONV_CH), F32), pltpu.VMEM((tm + CONV_HALO, CONV_CH), F32)],
        compiler_params=_params("arbitrary"),
    )(proj, proj, dy, dy, w)


GATE_Z = slice(1536, 2560)
D_CH = 512
GELU_C = math.sqrt(2.0 / math.pi)
GELU_K = 0.044715


def _gelu_parts(z):
    t = jnp.tanh(GELU_C * (z + GELU_K * z * z * z))
    return 0.5 * z * (1.0 + t), t


def _lane_group(rows):
    return lax.broadcasted_iota(jnp.int32, (rows, D_CH), 1) // HEAD_DIM


def _tril_mask():
    return lax.broadcasted_iota(jnp.int32, (BLOCK, BLOCK), 0) >= lax.broadcasted_iota(jnp.int32, (BLOCK, BLOCK), 1)


def _layer_norm_parts(x):
    mu = jnp.mean(x, axis=-1, keepdims=True)
    xc = x - mu
    rstd = lax.rsqrt(jnp.mean(xc * xc, axis=-1, keepdims=True) + LN_EPS)
    return xc * rstd, rstd


def _gate_fwd(proj, ln_g, ln_b, w_sp, sb_t, name):
    tm = 512

    def body(p_ref, g_ref, b_ref, w_ref, sb_ref, mixed_ref, out_ref):
        zz, _ = _gelu_parts(p_ref[:, GATE_Z])
        u = zz[:, :D_CH]
        xh, _ = _layer_norm_parts(zz[:, D_CH:])
        gn = (xh * g_ref[...] + b_ref[...]).astype(BF16)
        grp = _lane_group(BLOCK)
        tri = _tril_mask()
        ws = [jnp.where(tri, w_ref[gi], 0.0).astype(BF16) for gi in range(N_GROUPS)]
        bias = jnp.zeros((BLOCK, D_CH), F32)
        for gi in range(N_GROUPS):
            bias = jnp.where(grp == gi, sb_ref[:, gi:gi + 1], bias)
        for ch in range(tm // BLOCK):
            rows = slice(ch * BLOCK, (ch + 1) * BLOCK)
            gc = gn[rows, :]
            mixed = bias
            for gi in range(N_GROUPS):
                r = jnp.dot(ws[gi], gc, preferred_element_type=F32)
                mixed = jnp.where(grp == gi, r + bias, mixed)
            mixed_ref[rows, :] = mixed
            out_ref[rows, :] = (u[rows, :] * mixed).astype(BF16)

    return _rows(body, name, tm, [proj], [ln_g, ln_b, w_sp, sb_t], [(D_CH, F32), (D_CH, BF16)])


def _gate_bwd(dmix, proj, mixed, ln_g, ln_b, w_sp, name):
    tm = 512

    def body(d_ref, p_ref, m_ref, g_ref, b_ref, w_ref, dz_ref, dg_ref, db_ref, dw_ref, dsb_ref, dgn_ref):
        @pl.when(_first_step())
        def _():
            dg_ref[...] = jnp.zeros_like(dg_ref)
            db_ref[...] = jnp.zeros_like(db_ref)
            dw_ref[...] = jnp.zeros_like(dw_ref)
            dsb_ref[...] = jnp.zeros_like(dsb_ref)

        z = p_ref[:, GATE_Z]
        zz, t = _gelu_parts(z)
        u = zz[:, :D_CH]
        xh, rstd = _layer_norm_parts(zz[:, D_CH:])
        g = g_ref[...]
        gn = (xh * g + b_ref[...]).astype(BF16)
        dd = d_ref[:, D_CH:]
        du = dd * m_ref[...]
        dm = dd * u
        grp = _lane_group(BLOCK)
        tri = _tril_mask()
        ws = [jnp.where(tri, w_ref[gi], 0.0).astype(BF16) for gi in range(N_GROUPS)]
        gsel = (lax.broadcasted_iota(jnp.int32, (N_GROUPS, D_CH), 1) // HEAD_DIM
                == lax.broadcasted_iota(jnp.int32, (N_GROUPS, D_CH), 0)).astype(F32)
        for ch in range(tm // BLOCK):
            rows = slice(ch * BLOCK, (ch + 1) * BLOCK)
            dmc = dm[rows, :]
            dmb = dmc.astype(BF16)
            gc = gn[rows, :]
            dgn = jnp.zeros((BLOCK, D_CH), F32)
            for gi in range(N_GROUPS):
                r = lax.dot_general(ws[gi], dmb, TN, preferred_element_type=F32)
                dgn = jnp.where(grp == gi, r, dgn)
                dmg = jnp.where(grp == gi, dmb, jnp.zeros_like(dmb))
                dwg = lax.dot_general(dmg, gc, NT, preferred_element_type=F32)
                dw_ref[gi] += jnp.where(tri, dwg, 0.0)
            dsb_ref[...] += lax.dot_general(gsel, dmc, NT, preferred_element_type=F32, precision=lax.Precision.HIGHEST)
            dgn_ref[rows, :] = dgn
        dgn = dgn_ref[...]
        db_ref[...] += jnp.sum(dgn, axis=0, keepdims=True)
        dg_ref[...] += jnp.sum(dgn * xh, axis=0, keepdims=True)
        dxh = dgn * g
        dgp = rstd * (dxh - jnp.mean(dxh, axis=-1, keepdims=True) - xh * jnp.mean(dxh * xh, axis=-1, keepdims=True))
        dgelu = 0.5 * (1.0 + t) + 0.5 * z * (1.0 - t * t) * GELU_C * (1.0 + 3.0 * GELU_K * z * z)
        dz_ref[:, 0:D_CH] = (du * dgelu[:, :D_CH]).astype(BF16)
        dz_ref[:, D_CH:] = (dgp * dgelu[:, D_CH:]).astype(BF16)

    s = proj.shape[0]
    tiled = [dmix, proj, mixed]
    consts = [ln_g, ln_b, w_sp]
    in_specs = [pl.BlockSpec((tm, a.shape[1]), lambda i: (i, 0)) for a in tiled]
    in_specs += [pl.BlockSpec(a.shape, lambda i, nd=a.ndim: (0,) * nd) for a in consts]
    vec = (1, D_CH)
    acc_shapes = [vec, vec, w_sp.shape, (N_GROUPS, BLOCK)]
    return pl.pallas_call(
        body, name=name, grid=(s // tm,), in_specs=in_specs,
        out_specs=[pl.BlockSpec((tm, 2 * D_CH), lambda i: (i, 0))]
        + [pl.BlockSpec(sh, lambda i, nd=len(sh): (0,) * nd) for sh in acc_shapes],
        out_shape=[jax.ShapeDtypeStruct((s, 2 * D_CH), BF16)] + [jax.ShapeDtypeStruct(sh, F32) for sh in acc_shapes],
        scratch_shapes=[pltpu.VMEM((tm, D_CH), F32)],
        compiler_params=_params("arbitrary"),
    )(*tiled, *consts)


def _adam_update(w, g, m, v):
    nm = ADAM_B1 * m + (1.0 - ADAM_B1) * g
    nv = ADAM_B2 * v + (1.0 - ADAM_B2) * (g * g)
    m_hat = nm / (1.0 - ADAM_B1 ** ADAM_STEP)
    v_hat = nv / (1.0 - ADAM_B2 ** ADAM_STEP)
    return -ADAM_LR * (m_hat / (jnp.sqrt(v_hat) + ADAM_EPS) + ADAM_WD * w), nm, nv


def _adamw(w, g, m, v, name):
    rows, cols = w.shape
    tm = _tile(rows, 512, 8)

    def body(w_ref, g_ref, m_ref, v_ref, d_ref, nm_ref, nv_ref):
        d_ref[...], nm_ref[...], nv_ref[...] = _adam_update(w_ref[...], g_ref[...], m_ref[...], v_ref[...])

    return _rows(body, name, tm, [w, g, m, v], [], [(cols, F32)] * 3)


def _adamw_t(w, g_t, m, v, name):
    layers, kdim, n = w.shape
    tr = 256

    def body(w_ref, g_ref, m_ref, v_ref, go_ref, d_ref, nm_ref, nv_ref):
        g = g_ref[0].T
        go_ref[0] = g
        d_ref[0], nm_ref[0], nv_ref[0] = _adam_update(w_ref[0], g, m_ref[0], v_ref[0])

    wspec = pl.BlockSpec((1, tr, n), lambda l, i: (l, i, 0))
    return pl.pallas_call(
        body, name=name, grid=(layers, kdim // tr),
        in_specs=[wspec, pl.BlockSpec((1, n, tr), lambda l, i: (l, 0, i)), wspec, wspec],
        out_specs=[wspec] * 4, out_shape=[jax.ShapeDtypeStruct(w.shape, F32)] * 4,
        compiler_params=_params("parallel", "parallel"),
    )(w, g_t, m, v)


def _ordered_sum(parts, name):
    n, rows, cols = parts.shape
    tm = _tile(rows, 512, 16 if parts.dtype == BF16 else 8)

    def body(p_ref, o_ref):
        acc = p_ref[0].astype(F32)
        for k in range(1, n):
            acc = acc + p_ref[k].astype(F32)
        o_ref[...] = acc

    return pl.pallas_call(body, name=name, grid=(rows // tm,),
                          in_specs=[pl.BlockSpec((n, tm, cols), lambda i: (0, i, 0))],
                          out_specs=pl.BlockSpec((tm, cols), lambda i: (i, 0)),
                          out_shape=jax.ShapeDtypeStruct((rows, cols), F32), compiler_params=_params("parallel"))(parts)


ANY = pl.BlockSpec(memory_space=pl.ANY)


def _position():
    x, y, c = lax.axis_index("x"), lax.axis_index("y"), lax.axis_index("c")
    other_chips = [(1 - x, y), (x, 1 - y), (1 - x, 1 - y)]
    return x, y, c, other_chips


def _remote(src, dst, send_sem, recv_sem, to):
    return pltpu.make_async_remote_copy(src_ref=src, dst_ref=dst, send_sem=send_sem, recv_sem=recv_sem,
                                        device_id=to, device_id_type=MESH)


STAGE_ROWS = 736


def _staged_copies(copies, buf, in_sems, out_sems):
    n = len(copies)

    def into(u):
        src = copies[u][0]
        return pltpu.make_async_copy(src, buf.at[u % 2, pl.ds(0, src.shape[0]), :], in_sems.at[u % 2])

    def out_of(u):
        dst = copies[u][1]
        return pltpu.make_async_copy(buf.at[u % 2, pl.ds(0, dst.shape[0]), :], dst, out_sems.at[u % 2])

    into(0).start()
    for u in range(n):
        into(u).wait()
        out_of(u).start()
        if u + 1 < n:
            if u >= 1:
                out_of(u - 1).wait()
            into(u + 1).start()
    if n >= 2:
        out_of(n - 2).wait()
    out_of(n - 1).wait()


def _stage_scratch(dtype, cols):
    return [pltpu.VMEM((2, STAGE_ROWS, cols), dtype), pltpu.SemaphoreType.DMA((2,)), pltpu.SemaphoreType.DMA((2,))]


def _row_chunks(rows):
    return [(r, min(STAGE_ROWS, rows - r)) for r in range(0, rows, STAGE_ROWS)]


def _gather_chips(shard, name):
    rows, cols = shard.shape
    half = rows // 2

    def body(in_ref, out_ref, send_sems, recv_sems, buf, in_sems, out_sems):
        x, y, c, chips = _position()
        me = 2 * x + y
        sibling = (x, y, 1 - c)

        def slab(chip, h):
            return out_ref.at[chip, pl.ds(h * half, half), :]

        first = [_remote(in_ref.at[pl.ds(c * half, half), :], slab(me, c), send_sems.at[j], recv_sems.at[j], (cx, cy, c))
                 for j, (cx, cy) in enumerate(chips)]
        for cp in first:
            cp.start()
        _staged_copies([(in_ref.at[pl.ds(r, n), :], out_ref.at[me, pl.ds(r, n), :]) for r, n in _row_chunks(rows)],
                       buf, in_sems, out_sems)
        passed = []
        for j, (cx, cy) in enumerate(chips):
            got = slab(2 * cx + cy, c)
            _remote(got, got, send_sems.at[j], recv_sems.at[j], sibling).wait_recv()
            cp = _remote(got, got, send_sems.at[3 + j], recv_sems.at[3 + j], sibling)
            cp.start()
            passed.append(cp)
        for j, (cx, cy) in enumerate(chips):
            got = slab(2 * cx + cy, 1 - c)
            _remote(got, got, send_sems.at[3 + j], recv_sems.at[3 + j], sibling).wait_recv()
        for cp in first + passed:
            cp.wait_send()

    return pl.pallas_call(
        body, name=name, in_specs=[ANY], out_specs=ANY,
        out_shape=jax.ShapeDtypeStruct((N_CHIPS, rows, cols), shard.dtype),
        scratch_shapes=[pltpu.SemaphoreType.DMA((6,)), pltpu.SemaphoreType.DMA((6,))] + _stage_scratch(shard.dtype, cols),
        compiler_params=pltpu.CompilerParams(vmem_limit_bytes=VMEM_LIMIT),
    )(shard)


def _gather_devices(block, name):
    rows, cols = block.shape

    def body(in_ref, out_ref, send_sems, recv_sems, local_sem):
        x, y, c, chips = _position()
        sibling = (x, y, 1 - c)

        def slot(px, py, pc):
            return out_ref.at[4 * px + 2 * py + pc]

        mine = pltpu.make_async_copy(in_ref, slot(x, y, c), local_sem)
        mine.start()
        first = [_remote(in_ref, slot(x, y, c), send_sems.at[0], recv_sems.at[0], sibling)]
        first += [_remote(in_ref, slot(x, y, c), send_sems.at[1 + j], recv_sems.at[1 + j], (cx, cy, c))
                  for j, (cx, cy) in enumerate(chips)]
        for cp in first:
            cp.start()
        passed = []
        for j, (cx, cy) in enumerate(chips):
            got = slot(cx, cy, c)
            _remote(got, got, send_sems.at[1 + j], recv_sems.at[1 + j], sibling).wait_recv()
            cp = _remote(got, got, send_sems.at[4 + j], recv_sems.at[4 + j], sibling)
            cp.start()
            passed.append(cp)
        got = slot(x, y, 1 - c)
        _remote(got, got, send_sems.at[0], recv_sems.at[0], sibling).wait_recv()
        for j, (cx, cy) in enumerate(chips):
            got = slot(cx, cy, 1 - c)
            _remote(got, got, send_sems.at[4 + j], recv_sems.at[4 + j], sibling).wait_recv()
        for cp in first + passed:
            cp.wait_send()
        mine.wait()

    return pl.pallas_call(
        body, name=name, in_specs=[ANY], out_specs=ANY,
        out_shape=jax.ShapeDtypeStruct((N_DEV, rows, cols), block.dtype),
        scratch_shapes=[pltpu.SemaphoreType.DMA((7,)), pltpu.SemaphoreType.DMA((7,)), pltpu.SemaphoreType.DMA],
    )(block)


def _pair_send(grads, name):
    n = len(grads)
    hs = [g.shape[2] for g in grads]
    offs = [sum(hs[:i]) for i in range(n)]
    cols = grads[0].shape[3]

    def body(*refs):
        g_refs = refs[:n]
        got_ref, send_sems, recv_sems = refs[n:]
        x, y, c, _ = _position()
        copies = [_remote(g_ref.at[:, 1 - c], got_ref.at[:, pl.ds(offs[i], hs[i]), :], send_sems.at[i], recv_sems.at[i],
                          (x, y, 1 - c)) for i, g_ref in enumerate(g_refs)]
        for cp in copies:
            cp.start()
        for cp in copies:
            cp.wait()

    return pl.pallas_call(
        body, name=name, in_specs=[ANY] * n, out_specs=ANY, out_shape=jax.ShapeDtypeStruct((N_CHIPS, sum(hs), cols), F32),
        scratch_shapes=[pltpu.SemaphoreType.DMA((n,)), pltpu.SemaphoreType.DMA((n,))],
    )(*grads)


def _pair_add(grads, got, name):
    n = len(grads)
    hs = [g.shape[2] for g in grads]
    offs = [sum(hs[:i]) for i in range(n)]
    cols = grads[0].shape[3]
    hmax = max(hs)
    units = [(i, k) for k in range(N_CHIPS) for i in range(n)]

    def body(*refs):
        g_refs = refs[:n]
        got_ref, out_ref, a_buf, b_buf, o_buf, a_sems, b_sems, o_sems = refs[n:]
        c = lax.axis_index("c")

        def loads(u):
            i, k = units[u]
            slot, rows = u % 2, pl.ds(0, hs[i])
            return (pltpu.make_async_copy(g_refs[i].at[k, c], a_buf.at[slot, rows, :], a_sems.at[slot]),
                    pltpu.make_async_copy(got_ref.at[k, pl.ds(offs[i], hs[i]), :], b_buf.at[slot, rows, :], b_sems.at[slot]))

        def store(u):
            i, k = units[u]
            return pltpu.make_async_copy(o_buf.at[u % 2, pl.ds(0, hs[i]), :], out_ref.at[k, pl.ds(offs[i], hs[i]), :],
                                         o_sems.at[u % 2])

        for cp in loads(0):
            cp.start()
        for u, (i, k) in enumerate(units):
            if u + 1 < len(units):
                for cp in loads(u + 1):
                    cp.start()
            for cp in loads(u):
                cp.wait()
            if u >= 2:
                store(u - 2).wait()
            rows = pl.ds(0, hs[i])
            o_buf[u % 2, rows, :] = (a_buf[u % 2, rows, :] + b_buf[u % 2, rows, :]).astype(BF16)
            store(u).start()
        store(len(units) - 2).wait()
        store(len(units) - 1).wait()

    return pl.pallas_call(
        body, name=name, in_specs=[ANY] * (n + 1), out_specs=ANY,
        out_shape=jax.ShapeDtypeStruct((N_CHIPS, sum(hs), cols), BF16),
        scratch_shapes=[pltpu.VMEM((2, hmax, cols), F32), pltpu.VMEM((2, hmax, cols), F32), pltpu.VMEM((2, hmax, cols), BF16),
                        pltpu.SemaphoreType.DMA((2,)), pltpu.SemaphoreType.DMA((2,)), pltpu.SemaphoreType.DMA((2,))],
        compiler_params=pltpu.CompilerParams(vmem_limit_bytes=VMEM_LIMIT),
    )(*grads, got)


def _chip_exchange(parts, name):
    _, rows, cols = parts.shape

    def body(in_ref, out_ref, send_sems, recv_sems):
        x, y, c, chips = _position()
        sent = [_remote(in_ref.at[2 * cx + cy], out_ref.at[j], send_sems.at[j], recv_sems.at[j], (cx, cy, c))
                for j, (cx, cy) in enumerate(chips)]
        for cp in sent:
            cp.start()
        for cp in sent:
            cp.wait()

    return pl.pallas_call(
        body, name=name, in_specs=[ANY], out_specs=ANY, out_shape=jax.ShapeDtypeStruct((3, rows, cols), parts.dtype),
        scratch_shapes=[pltpu.SemaphoreType.DMA((3,)), pltpu.SemaphoreType.DMA((3,))],
    )(parts)


def _chip_sum(parts, recv, chip, name):
    _, rows, cols = parts.shape
    tm = _tile(rows, 512, 16)

    def body(chip_ref, own_ref, recv_ref, o_ref):
        acc = own_ref[0].astype(F32)
        for j in range(3):
            acc = acc + recv_ref[j].astype(F32)
        o_ref[...] = acc

    return pl.pallas_call(
        body, name=name,
        grid_spec=pltpu.PrefetchScalarGridSpec(
            num_scalar_prefetch=1, grid=(rows // tm,),
            in_specs=[pl.BlockSpec((1, tm, cols), lambda i, chip_ref: (chip_ref[0], i, 0)),
                      pl.BlockSpec((3, tm, cols), lambda i, chip_ref: (0, i, 0))],
            out_specs=pl.BlockSpec((tm, cols), lambda i, chip_ref: (i, 0))),
        out_shape=jax.ShapeDtypeStruct((rows, cols), F32), compiler_params=_params("parallel"),
    )(chip, parts, recv)


def _join_unpack(mine, hs, name):
    n = len(hs)
    offs = [sum(hs[:i]) for i in range(n)]
    cols = mine.shape[1]

    def body(in_ref, *refs):
        outs = refs[:n]
        send_sems, recv_sems, buf, in_sems, out_sems = refs[n:]
        x, y, c, _ = _position()
        sibling = (x, y, 1 - c)
        sent, local = [], []
        for i, o_ref in enumerate(outs):
            src = in_ref.at[pl.ds(offs[i], hs[i]), :]
            here = o_ref.at[pl.ds(c * hs[i], hs[i]), :]
            cp = _remote(src, here, send_sems.at[i], recv_sems.at[i], sibling)
            cp.start()
            sent.append(cp)
            local.append((src, here))
        _staged_copies(local, buf, in_sems, out_sems)
        for i, (cp, o_ref) in enumerate(zip(sent, outs)):
            there = o_ref.at[pl.ds((1 - c) * hs[i], hs[i]), :]
            _remote(there, there, send_sems.at[i], recv_sems.at[i], sibling).wait_recv()
            cp.wait_send()

    assert max(hs) <= STAGE_ROWS
    return pl.pallas_call(
        body, name=name, in_specs=[ANY], out_specs=[ANY] * n,
        out_shape=[jax.ShapeDtypeStruct((2 * h, cols), F32) for h in hs],
        scratch_shapes=[pltpu.SemaphoreType.DMA((n,)), pltpu.SemaphoreType.DMA((n,))] + _stage_scratch(F32, cols),
        compiler_params=pltpu.CompilerParams(vmem_limit_bytes=VMEM_LIMIT),
    )(mine)


def _pad_rows(a, mult):
    extra = (-a.shape[0]) % mult
    return a if extra == 0 else jnp.pad(a, ((0, extra), (0, 0)))


def _pack_small(arrs, mult):
    rows = []
    for a in arrs:
        flat = a.reshape(-1)
        extra = (-flat.shape[0]) % LANES
        if extra:
            flat = jnp.pad(flat, (0, extra))
        rows.append(flat.reshape(-1, LANES))
    return _pad_rows(jnp.concatenate(rows, axis=0), mult)


def _unpack_small(packed, shapes):
    out, r = [], 0
    for sh in shapes:
        n = math.prod(sh)
        cnt = -(-n // LANES)
        out.append(packed[r:r + cnt].reshape(-1)[:n].reshape(sh))
        r += cnt
    return out


def _ffn_fwd(h, g_norm, w_gate_t, w_up_t, w_down, tag):
    n = _rms_fwd(h, g_norm, f"{tag}_norm")
    gate = _matmul(n, w_gate_t, trans_b=True, name=f"{tag}_gate")
    up = _matmul(n, w_up_t, trans_b=True, name=f"{tag}_up")
    act = _swiglu_fwd(gate, up, f"{tag}_act")
    out = _matmul(act, w_down, add=h, name=f"{tag}_down")
    return out, (n, gate, up, act)


def _ffn_bwd(dh, dhb, h_in, saved, g_norm, w_gate_t, w_up_t, w_down, tag):
    n, gate, up, act = saved
    dact = _matmul(dhb, w_down, trans_b=True, name=f"{tag}_dact")
    dgate, dup = _swiglu_bwd(dact, gate, up, f"{tag}_dswiglu")
    dw_down = _matmul(act, dhb, trans_a=True, name=f"{tag}_dwdown")
    dw_gate_t = _matmul(dgate, n, trans_a=True, name=f"{tag}_dwgate")
    dw_up_t = _matmul(dup, n, trans_a=True, name=f"{tag}_dwup")
    dn = _matmul(dgate, w_gate_t, name=f"{tag}_dn_gate")
    dn = _matmul(dup, w_up_t, add=dn, name=f"{tag}_dn_up")
    dh_in, dh_inb, dg = _rms_bwd(dn, h_in, g_norm, dh, f"{tag}_dnorm")
    return dh_in, dh_inb, dg, dw_gate_t, dw_up_t, dw_down


def _local_step(x, tgt, w, big):
    s = x.shape[0]
    tabs = _rope_tables(s)
    grads, gbig = {}, {}

    g_ev = w['ev_norm_g']
    n1 = _rms_fwd(x, g_ev, "ev_norm")
    proj0 = _matmul(n1, big['ev_w_in', 0], trans_b=True, name="ev_in")
    q0, k0, v0 = _qkv_prep_even(proj0, tabs, "ev_qkv")
    sinks = w['ev_sinks'].reshape(-1)
    o0, lse0, o0b = _attn_fwd(q0, k0, v0, sinks, max_dist=BLOCK - 1, name="ev_attn", emit_bf16=True)
    yconv, cout = _conv_fwd(proj0, w['ev_conv_w'][0], w['ev_conv_b'], w['ev_conv_ln_g'], w['ev_conv_ln_b'], "ev_conv")
    mix0 = (o0b[0], cout)
    h1 = _matmul(mix0, big['ev_w_out', 0], add=x, name="ev_out")

    g_f0 = w['ffn_norm_g'][0:1]
    h2, ffn0 = _ffn_fwd(h1, g_f0, big['ffn_w_gate', 0], big['ffn_w_up', 0], big['ffn_w_down', 0], "ffn0")

    g_od = w['od_norm_g']
    n3 = _rms_fwd(h2, g_od, "od_norm")
    proj1 = _matmul(n3, big['od_w_in', 0], trans_b=True, name="od_in")
    qkv = _qkv_prep_odd(proj1, tabs, "od_qkv")
    nb = len(DILATIONS)
    outs, lses = [], []
    for i, d in enumerate(DILATIONS):
        o_r, lse_r = _attn_fwd(qkv[i], qkv[nb + i], qkv[2 * nb + i], None, max_dist=BLOCK, name=f"od_attn{d}")
        outs.append(o_r)
        lses.append(lse_r)
    comb = _combine(outs, lses, "od_combine")
    c_bf16 = comb[0]
    c_fold = {1: comb[1]}
    lse_fold = {1: comb[2]}
    for i, d in enumerate(DILATIONS[1:]):
        c_fold[d], lse_fold[d] = comb[3 + 2 * i], comb[4 + 2 * i]
    w_sp = w['od_spatial_w'][0]
    sb_t = w['od_spatial_b'][0].T
    mixed, dout = _gate_fwd(proj1, w['od_sgu_ln_g'], w['od_sgu_ln_b'], w_sp, sb_t, "od_gate")
    mix1 = (c_bf16, dout)
    h3 = _matmul(mix1, big['od_w_out', 0], add=h2, name="od_out")

    g_f1 = w['ffn_norm_g'][1:2]
    h4, ffn1 = _ffn_fwd(h3, g_f1, big['ffn_w_gate', 1], big['ffn_w_up', 1], big['ffn_w_down', 1], "ffn1")

    dh4, dh4b, dg_final, loss_tile = _final_loss(h4, w['final_norm_g'].reshape(1, D_MODEL), tgt, "final")
    grads['final_norm_g'] = dg_final.reshape(D_MODEL)

    dh3, dh3b, dg_f1, gbig['ffn_w_gate', 1], gbig['ffn_w_up', 1], gbig['ffn_w_down', 1] = _ffn_bwd(
        dh4, dh4b, h3, ffn1, g_f1, big['ffn_w_gate', 1], big['ffn_w_up', 1], big['ffn_w_down', 1], "ffn1")

    dmix1 = _matmul(dh3b, big['od_w_out', 0], trans_b=True, name="od_dmix")
    gbig['od_w_out', 0] = _matmul_tn_pair(mix1[0], mix1[1], dh3b, "od_dwout")
    do_fold = dict(zip(DILATIONS[1:], _fold_dout(dmix1, "od_fold_dout")))
    do_fold[1] = dmix1[None]
    dqs, dks, dvs = [], [], []
    for i, d in enumerate(DILATIONS):
        dq_r, dk_r, dv_r = _attn_bwd(qkv[i], qkv[nb + i], qkv[2 * nb + i], do_fold[d], c_fold[d], lse_fold[d], None,
                                     max_dist=BLOCK, name=f"od_dattn{d}")
        dqs.append(dq_r)
        dks.append(dk_r)
        dvs.append(dv_r)
    dz, dg_sgu, db_sgu, dw_sp, dsb = _gate_bwd(dmix1, proj1, mixed, w['od_sgu_ln_g'], w['od_sgu_ln_b'], w_sp, "od_dgate")
    grads['od_sgu_ln_g'], grads['od_sgu_ln_b'] = dg_sgu, db_sgu
    grads['od_spatial_w'], grads['od_spatial_b'] = dw_sp[None], dsb[None]
    dproj1 = _qkv_post_odd(dqs, dks, dvs, dz, tabs, "od_dproj")
    gbig['od_w_in', 0] = _matmul(dproj1, n3, trans_a=True, name="od_dwin")
    dn3 = _matmul(dproj1, big['od_w_in', 0], name="od_dn")
    dh2, dh2b, dg_od = _rms_bwd(dn3, h2, g_od, dh3, "od_dnorm")
    grads['od_norm_g'] = dg_od

    dh1, dh1b, dg_f0, gbig['ffn_w_gate', 0], gbig['ffn_w_up', 0], gbig['ffn_w_down', 0] = _ffn_bwd(
        dh2, dh2b, h1, ffn0, g_f0, big['ffn_w_gate', 0], big['ffn_w_up', 0], big['ffn_w_down', 0], "ffn0")
    grads['ffn_norm_g'] = jnp.concatenate([dg_f0, dg_f1], axis=0)

    dmix0 = _matmul(dh1b, big['ev_w_out', 0], trans_b=True, name="ev_dmix")
    gbig['ev_w_out', 0] = _matmul_tn_pair(mix0[0], mix0[1], dh1b, "ev_dwout")
    dq0, dk0, dv0, dsink = _attn_bwd(q0, k0, v0, dmix0[None], o0, lse0, sinks, max_dist=BLOCK - 1, name="ev_dattn")
    grads['ev_sinks'] = dsink[:, 0, :].reshape(N_PAIRS, 2, HEAD_DIM)[:, :, 0].reshape(1, 8)
    dyc, dg_cln, db_cln, dcb = _conv_tail_bwd(dmix0, yconv, w['ev_conv_ln_g'], w['ev_conv_ln_b'], "ev_dconv_tail")
    grads['ev_conv_ln_g'], grads['ev_conv_ln_b'], grads['ev_conv_b'] = dg_cln, db_cln, dcb
    dglu, dconv_w = _conv_bwd(proj0, dyc, w['ev_conv_w'][0], "ev_dconv")
    grads['ev_conv_w'] = dconv_w[None]
    dproj0 = _qkv_post_even(dq0, dk0, dv0, dglu, tabs, "ev_dproj")
    gbig['ev_w_in', 0] = _matmul(dproj0, n1, trans_a=True, name="ev_dwin")
    dn1 = _matmul(dproj0, big['ev_w_in', 0], name="ev_dn")
    dx, _, dg_ev = _rms_bwd(dn1, x, g_ev, dh1, "ev_dnorm")
    grads['ev_norm_g'] = dg_ev
    return loss_tile, dx, grads, gbig


def _shard_rows(w, layer, by_cols):
    return w[layer].T if by_cols else w[layer]


def kernel(x, ev_norm_g, ev_w_in, ev_sinks, ev_conv_w, ev_conv_b, ev_conv_ln_g, ev_conv_ln_b, ev_w_out, od_norm_g, od_w_in, od_sgu_ln_g, od_sgu_ln_b, od_spatial_w, od_spatial_b, od_w_out, ffn_norm_g, ffn_w_gate, ffn_w_up, ffn_w_down, final_norm_g, loss_target, m_ev_norm_g, m_ev_w_in, m_ev_sinks, m_ev_conv_w, m_ev_conv_b, m_ev_conv_ln_g, m_ev_conv_ln_b, m_ev_w_out, m_od_norm_g, m_od_w_in, m_od_sgu_ln_g, m_od_sgu_ln_b, m_od_spatial_w, m_od_spatial_b, m_od_w_out, m_ffn_norm_g, m_ffn_w_gate, m_ffn_w_up, m_ffn_w_down, m_final_norm_g, v_ev_norm_g, v_ev_w_in, v_ev_sinks, v_ev_conv_w, v_ev_conv_b, v_ev_conv_ln_g, v_ev_conv_ln_b, v_ev_w_out, v_od_norm_g, v_od_w_in, v_od_sgu_ln_g, v_od_sgu_ln_b, v_od_spatial_w, v_od_spatial_b, v_od_w_out, v_ffn_norm_g, v_ffn_w_gate, v_ffn_w_up, v_ffn_w_down, v_final_norm_g):
    given = dict(locals())
    wts = {n: given[n] for n in WEIGHTS}
    mom = {n: given["m_" + n] for n in WEIGHTS}
    var = {n: given["v_" + n] for n in WEIGHTS}
    chip = 2 * lax.axis_index("x") + lax.axis_index("y")

    shard_rows = [_shard_rows(wts[n], layer, by_cols).astype(BF16) for n, layer, by_cols in BIG]
    counts = [a.shape[0] for a in shard_rows]
    all_w = _gather_chips(jnp.concatenate(shard_rows, axis=0), "gather_weights")
    big, r = {}, 0
    for (n, layer, _), cnt in zip(BIG, counts):
        big[n, layer] = all_w[:, r:r + cnt].reshape(N_CHIPS * cnt, D_MODEL)
        r += cnt
    full = {n: wts[n] for n in SMALL_REPL}
    small_shards = [wts[n] for n in SMALL_SHARDED]
    small_shapes = [a.shape for a in small_shards]
    all_s = _gather_chips(_pack_small(small_shards, 16), "gather_small_weights")
    per_chip = [_unpack_small(all_s[k], small_shapes) for k in range(N_CHIPS)]
    for i, n in enumerate(SMALL_SHARDED):
        full[n] = jnp.concatenate([per_chip[k][i] for k in range(N_CHIPS)], axis=-1)

    loss_tile, grad_x, grads, gbig = _local_step(x[0], loss_target[0], full, big)
    loss = lax.psum(loss_tile[0, 0], ("x", "y", "c"))

    halves = [cnt // 2 for cnt in counts]
    split = [gbig[n, layer].reshape(N_CHIPS, 2, h, D_MODEL) for (n, layer, _), h in zip(BIG, halves)]
    got = _pair_send(split, "grad_pair_send")
    chip_part = _pair_add(split, got, "grad_pair_add")
    from_chips = _chip_exchange(chip_part, "grad_chip_exchange")
    my_half = _chip_sum(chip_part, from_chips, chip.reshape(1), "grad_chip_sum")
    reduced = dict(zip([(n, layer) for n, layer, _ in BIG], _join_unpack(my_half, halves, "grad_join_halves")))

    small_names = SMALL_REPL + SMALL_SHARDED
    small_full_shapes = [grads[n].shape for n in small_names]
    spack = _pack_small([grads[n] for n in small_names], 8)
    s_all = _gather_devices(spack, "grad_small_gather")
    s_sum = _unpack_small(_ordered_sum(s_all, "grad_small_sum"), small_full_shapes)
    g_all = dict(zip(small_names, s_sum))
    for n in SMALL_SHARDED:
        width = wts[n].shape[-1]
        g_all[n] = lax.dynamic_slice_in_dim(g_all[n], chip * width, width, axis=g_all[n].ndim - 1)

    delta, new_m, new_v = {}, {}, {}
    for n in BIG_NAMES:
        layers = [layer for nn, layer, _ in BIG if nn == n]
        by_cols = [bc for nn, _, bc in BIG if nn == n][0]
        g_rows = jnp.stack([reduced[n, layer] for layer in layers])
        if by_cols:
            g_all[n], delta[n], new_m[n], new_v[n] = _adamw_t(wts[n], g_rows, mom[n], var[n], f"adamw_{n}")
        else:
            shape = wts[n].shape
            flat = [a.reshape(-1, D_MODEL) for a in (wts[n], g_rows, mom[n], var[n])]
            g_all[n] = g_rows
            delta[n], new_m[n], new_v[n] = (a.reshape(shape) for a in _adamw(*flat, f"adamw_{n}"))
    shapes = [wts[n].shape for n in small_names]
    d_s, m_s, v_s = _adamw(*[_pack_small([src[n] for n in small_names], 8) for src in (wts, g_all, mom, var)], "adamw_small")
    for dst, packed in ((delta, d_s), (new_m, m_s), (new_v, v_s)):
        dst.update(zip(small_names, _unpack_small(packed, shapes)))

    return (loss, grad_x[None], *[g_all[n] for n in WEIGHTS], *[delta[n] for n in WEIGHTS],
            *[new_m[n] for n in WEIGHTS], *[new_v[n] for n in WEIGHTS])
```

```python
import math

import jax
import jax.numpy as jnp
from jax import lax
from jax.experimental import pallas as pl
from jax.experimental.pallas import tpu as pltpu

F32 = jnp.float32
BF16 = jnp.bfloat16

D_MODEL = 1024
HEAD_DIM = 64
ROT_DIM = 16
ROPE_THETA = 500000.0
RMS_EPS = 1e-6
LN_EPS = 1e-5
BLOCK = 128
CONV_WIDTH = 31
CONV_HALO = 32
CONV_ROWS = 64
D_FF = 2816
N_GROUPS = 8
ATTN_W = 512
ATTN_SCALE = HEAD_DIM ** -0.5
NEG = -1e30
DILATIONS = (1, 4, 16)

ADAM_LR = 0.001
ADAM_B1 = 0.9
ADAM_B2 = 0.999
ADAM_EPS = 1e-08
ADAM_WD = 0.01
ADAM_STEP = 10

LANES = 128
N_PAIRS = ATTN_W // LANES
VMEM_LIMIT = 56 * 1024 * 1024
MESH = pl.DeviceIdType.MESH
N_CHIPS = 4
N_DEV = 8

WEIGHTS = ['ev_norm_g', 'ev_w_in', 'ev_sinks', 'ev_conv_w', 'ev_conv_b', 'ev_conv_ln_g', 'ev_conv_ln_b', 'ev_w_out',
           'od_norm_g', 'od_w_in', 'od_sgu_ln_g', 'od_sgu_ln_b', 'od_spatial_w', 'od_spatial_b', 'od_w_out',
           'ffn_norm_g', 'ffn_w_gate', 'ffn_w_up', 'ffn_w_down', 'final_norm_g']
BIG = [('ev_w_in', 0, True), ('ev_w_out', 0, False), ('od_w_in', 0, True), ('od_w_out', 0, False),
       ('ffn_w_gate', 0, True), ('ffn_w_gate', 1, True), ('ffn_w_up', 0, True), ('ffn_w_up', 1, True),
       ('ffn_w_down', 0, False), ('ffn_w_down', 1, False)]
BIG_NAMES = ['ev_w_in', 'ev_w_out', 'od_w_in', 'od_w_out', 'ffn_w_gate', 'ffn_w_up', 'ffn_w_down']
SMALL_SHARDED = ['ev_conv_w', 'od_norm_g', 'od_sgu_ln_g', 'od_sgu_ln_b']
SMALL_REPL = ['ev_norm_g', 'ev_sinks', 'ev_conv_b', 'ev_conv_ln_g', 'ev_conv_ln_b', 'od_spatial_w', 'od_spatial_b',
              'ffn_norm_g', 'final_norm_g']


def _tile(n, cap, mult=LANES):
    best = None
    for t in range(mult, min(n, cap) + 1, mult):
        if n % t == 0:
            best = t
    assert best is not None, (n, cap)
    return best


def _params(*sem):
    return pltpu.CompilerParams(dimension_semantics=sem, vmem_limit_bytes=VMEM_LIMIT)


def _sigmoid(x):
    return 1.0 / (1.0 + jnp.exp(-x))


def _pair_block(p):
    return slice(p * LANES, (p + 1) * LANES)


def _matmul(a, b, *, name, trans_a=False, trans_b=False, add=None, out_dtype=F32):
    parts = a if isinstance(a, (tuple, list)) else (a,)
    if trans_a:
        k, m = parts[0].shape
    else:
        m = parts[0].shape[0]
        k = sum(p.shape[1] for p in parts)
    if trans_b:
        n, k2 = b.shape
    else:
        k2, n = b.shape
    assert k == k2 and b.dtype == BF16 and all(p.dtype == BF16 for p in parts)
    tm = _tile(m, D_FF // 2 if trans_a else 512)
    tn = _tile(n, D_FF // 2)
    tk = k if k <= D_FF else _tile(k, 1024)
    nk = k // tk
    na = len(parts)
    assert na == 1 or (nk == 1 and not trans_a)
    dims = (((0 if trans_a else 1,), (1 if trans_b else 0,)), ((), ()))
    has_add = add is not None

    def body(*refs):
        a_refs, b_ref = refs[:na], refs[na]
        add_ref = refs[na + 1] if has_add else None
        o_ref = refs[na + 1 + has_add]
        a_val = a_refs[0][...] if na == 1 else jnp.concatenate([r[...] for r in a_refs], axis=1)
        part = lax.dot_general(a_val, b_ref[...], dims, preferred_element_type=F32)
        if nk == 1:
            if has_add:
                part = part + add_ref[...]
            o_ref[...] = part.astype(o_ref.dtype)
            return
        acc_ref = refs[-1]
        kk = pl.program_id(2)

        @pl.when(kk == 0)
        def _():
            acc_ref[...] = part

        @pl.when(kk > 0)
        def _():
            acc_ref[...] += part

        @pl.when(kk == nk - 1)
        def _():
            res = acc_ref[...]
            if has_add:
                res = res + add_ref[...]
            o_ref[...] = res.astype(o_ref.dtype)

    if trans_a:
        a_specs = [pl.BlockSpec((tk, tm), lambda i, j, kk: (kk, i))]
    elif na == 1:
        a_specs = [pl.BlockSpec((tm, tk), lambda i, j, kk: (i, kk))]
    else:
        a_specs = [pl.BlockSpec((tm, p.shape[1]), lambda i, j, kk: (i, 0)) for p in parts]
    b_spec = pl.BlockSpec((tn, tk), lambda i, j, kk: (j, kk)) if trans_b else pl.BlockSpec((tk, tn), lambda i, j, kk: (kk, j))
    o_spec = pl.BlockSpec((tm, tn), lambda i, j, kk: (i, j))
    in_specs = a_specs + [b_spec] + ([o_spec] if has_add else [])
    operands = list(parts) + [b] + ([add] if has_add else [])
    return pl.pallas_call(
        body, name=name, grid=(m // tm, n // tn, nk), in_specs=in_specs, out_specs=o_spec,
        out_shape=jax.ShapeDtypeStruct((m, n), out_dtype),
        scratch_shapes=[pltpu.VMEM((tm, tn), F32)] if nk > 1 else [],
        compiler_params=_params("parallel", "parallel", "arbitrary"),
    )(*operands)


def _matmul_tn_pair(a1, a2, b, name):
    kdim, m1 = a1.shape
    m2 = a2.shape[1]
    n = b.shape[1]
    tn = _tile(n, 1024)
    tk = _tile(kdim, 1024)
    nk = kdim // tk
    dims = (((0,), (0,)), ((), ()))

    def body(a1_ref, a2_ref, b_ref, o_ref):
        kk = pl.program_id(1)
        bv = b_ref[...]
        top = lax.dot_general(a1_ref[...], bv, dims, preferred_element_type=F32)
        bot = lax.dot_general(a2_ref[...], bv, dims, preferred_element_type=F32)

        @pl.when(kk == 0)
        def _():
            o_ref[0:m1, :] = top
            o_ref[m1:, :] = bot

        @pl.when(kk > 0)
        def _():
            o_ref[0:m1, :] += top
            o_ref[m1:, :] += bot

    return pl.pallas_call(
        body, name=name, grid=(n // tn, nk),
        in_specs=[pl.BlockSpec((tk, m1), lambda j, kk: (kk, 0)), pl.BlockSpec((tk, m2), lambda j, kk: (kk, 0)),
                  pl.BlockSpec((tk, tn), lambda j, kk: (kk, j))],
        out_specs=pl.BlockSpec((m1 + m2, tn), lambda j, kk: (0, j)),
        out_shape=jax.ShapeDtypeStruct((m1 + m2, n), F32),
        compiler_params=_params("parallel", "arbitrary"),
    )(a1, a2, b)


def _ffn_gate_up(n, w_gate_t, w_up_t, name):
    m, k = n.shape
    f = w_gate_t.shape[0]
    tm, tn = _tile(m, 512), _tile(f, D_FF // 2)

    def body(n_ref, wg_ref, wu_ref, act_ref, gate_ref, up_ref):
        a = n_ref[...]
        gate = lax.dot_general(a, wg_ref[...], NT, preferred_element_type=F32)
        up = lax.dot_general(a, wu_ref[...], NT, preferred_element_type=F32)
        act_ref[...] = (gate * _sigmoid(gate) * up).astype(BF16)
        gate_ref[...] = gate.astype(BF16)
        up_ref[...] = up.astype(BF16)

    wspec = pl.BlockSpec((tn, k), lambda i, j: (j, 0))
    ospec = pl.BlockSpec((tm, tn), lambda i, j: (i, j))
    return pl.pallas_call(
        body, name=name, grid=(m // tm, f // tn), in_specs=[pl.BlockSpec((tm, k), lambda i, j: (i, 0)), wspec, wspec],
        out_specs=[ospec] * 3, out_shape=[jax.ShapeDtypeStruct((m, f), BF16)] * 3,
        compiler_params=_params("parallel", "parallel"),
    )(n, w_gate_t, w_up_t)


def _ffn_dact(dhb, w_down, gate, up, name):
    m, k = dhb.shape
    f = w_down.shape[0]
    tm, tn = _tile(m, 512), _tile(f, D_FF // 2)

    def body(d_ref, w_ref, g_ref, u_ref, dg_ref, du_ref):
        dact = lax.dot_general(d_ref[...], w_ref[...], NT, preferred_element_type=F32)
        g = g_ref[...].astype(F32)
        sg = _sigmoid(g)
        dg_ref[...] = (dact * u_ref[...].astype(F32) * sg * (1.0 + g * (1.0 - sg))).astype(BF16)
        du_ref[...] = (dact * g * sg).astype(BF16)

    ospec = pl.BlockSpec((tm, tn), lambda i, j: (i, j))
    return pl.pallas_call(
        body, name=name, grid=(m // tm, f // tn),
        in_specs=[pl.BlockSpec((tm, k), lambda i, j: (i, 0)), pl.BlockSpec((tn, k), lambda i, j: (j, 0)), ospec, ospec],
        out_specs=[ospec] * 2, out_shape=[jax.ShapeDtypeStruct((m, f), BF16)] * 2,
        compiler_params=_params("parallel", "parallel"),
    )(dhb, w_down, gate, up)


def _dn_norm(pairs, h, g, dres, name):
    m = h.shape[0]
    tm = 512
    np_ = len(pairs)

    def body(*refs):
        a_refs, b_refs = refs[:np_], refs[np_:2 * np_]
        h_ref, dres_ref, g_ref, dh_ref, dhb_ref, dg_ref = refs[2 * np_:]

        @pl.when(_first_step())
        def _():
            dg_ref[...] = jnp.zeros_like(dg_ref)

        dy = jnp.dot(a_refs[0][...], b_refs[0][...], preferred_element_type=F32)
        for a_ref, b_ref in zip(a_refs[1:], b_refs[1:]):
            dy = dy + jnp.dot(a_ref[...], b_ref[...], preferred_element_type=F32)
        x = h_ref[...]
        r = lax.rsqrt(jnp.mean(x * x, axis=-1, keepdims=True) + RMS_EPS)
        xh = x * r
        dg_ref[...] += jnp.sum(dy * xh, axis=0, keepdims=True)
        dxh = dy * g_ref[...]
        tot = dres_ref[...] + r * (dxh - xh * jnp.mean(dxh * xh, axis=-1, keepdims=True))
        dh_ref[...] = tot
        dhb_ref[...] = tot.astype(BF16)

    row = lambda w: pl.BlockSpec((tm, w), lambda i: (i, 0))
    whole = lambda a: pl.BlockSpec(a.shape, lambda i: (0, 0))
    a_list, b_list = [a for a, _ in pairs], [b for _, b in pairs]
    return pl.pallas_call(
        body, name=name, grid=(m // tm,),
        in_specs=[row(a.shape[1]) for a in a_list] + [whole(b) for b in b_list] + [row(D_MODEL), row(D_MODEL), whole(g)],
        out_specs=[row(D_MODEL), row(D_MODEL), pl.BlockSpec((1, D_MODEL), lambda i: (0, 0))],
        out_shape=[jax.ShapeDtypeStruct((m, D_MODEL), F32), jax.ShapeDtypeStruct((m, D_MODEL), BF16),
                   jax.ShapeDtypeStruct((1, D_MODEL), F32)],
        compiler_params=_params("arbitrary"),
    )(*a_list, *b_list, h, dres, g)


def _rows(body, name, tm, tiled, consts, outs, accs=()):
    s = tiled[0].shape[0]
    assert s % tm == 0
    in_specs = [pl.BlockSpec((tm, a.shape[1]), lambda i: (i, 0)) for a in tiled]
    in_specs += [pl.BlockSpec(a.shape, lambda i, nd=a.ndim: (0,) * nd) for a in consts]
    out_shape = [jax.ShapeDtypeStruct((s, c), dt) for c, dt in outs]
    out_shape += [jax.ShapeDtypeStruct(sh, dt) for sh, dt in accs]
    out_specs = [pl.BlockSpec((tm, c), lambda i: (i, 0)) for c, _ in outs]
    out_specs += [pl.BlockSpec(sh, lambda i, nd=len(sh): (0,) * nd) for sh, _ in accs]
    return pl.pallas_call(
        body, name=name, grid=(s // tm,), in_specs=in_specs, out_specs=out_specs, out_shape=out_shape,
        compiler_params=_params("arbitrary"),
    )(*tiled, *consts)


def _first_step():
    return pl.program_id(0) == 0


def _rms_fwd(h, g, name):
    def body(h_ref, g_ref, n_ref):
        x = h_ref[...]
        r = lax.rsqrt(jnp.mean(x * x, axis=-1, keepdims=True) + RMS_EPS)
        n_ref[...] = (x * r * g_ref[...]).astype(BF16)

    return _rows(body, name, 512, [h], [g], [(D_MODEL, BF16)])[0]


def _final_loss(h, g, tgt, name):
    def body(h_ref, t_ref, g_ref, dh_ref, dhb_ref, dg_ref, loss_ref):
        @pl.when(_first_step())
        def _():
            dg_ref[...] = jnp.zeros_like(dg_ref)
            loss_ref[...] = jnp.zeros_like(loss_ref)

        x = h_ref[...]
        r = lax.rsqrt(jnp.mean(x * x, axis=-1, keepdims=True) + RMS_EPS)
        xh = x * r
        gg = g_ref[...]
        e = xh * gg - t_ref[...]
        loss_ref[...] += (0.5 / D_MODEL) * jnp.sum(jnp.sum(e * e, axis=-1, keepdims=True), axis=0, keepdims=True)
        dy = e * (1.0 / D_MODEL)
        dg_ref[...] += jnp.sum(dy * xh, axis=0, keepdims=True)
        dxh = dy * gg
        dx = r * (dxh - xh * jnp.mean(dxh * xh, axis=-1, keepdims=True))
        dh_ref[...] = dx
        dhb_ref[...] = dx.astype(BF16)

    return _rows(body, name, 512, [h, tgt], [g], [(D_MODEL, F32), (D_MODEL, BF16)],
                 [((1, D_MODEL), F32), ((1, LANES), F32)])


def _rope_tables(s):
    half = ROT_DIM // 2
    inv_freq = ROPE_THETA ** (-jnp.arange(half, dtype=F32) * (2.0 / ROT_DIM))
    ang = jnp.arange(s, dtype=F32)[:, None] * inv_freq[None, :]
    cos, sin = jnp.cos(ang), jnp.sin(ang)
    rest = HEAD_DIM - ROT_DIM
    ones = jnp.ones((s, rest), F32)
    zeros = jnp.zeros((s, rest), F32)
    zh = jnp.zeros((s, half), F32)
    c_t = jnp.concatenate([cos, cos, ones], axis=1)
    a_t = jnp.concatenate([-sin, zh, zeros], axis=1)
    b_t = jnp.concatenate([zh, sin, zeros], axis=1)
    return tuple(jnp.tile(t, (1, LANES // HEAD_DIM)) for t in (c_t, a_t, b_t))


def _rot(x, c, a, b):
    w = x.shape[1]
    half = ROT_DIM // 2
    return x * c + pltpu.roll(x, w - half, 1) * a + pltpu.roll(x, half, 1) * b


def _wide(t, w):
    return t if w == LANES else jnp.tile(t, (1, w // LANES))


def _low_lanes(rows):
    return lax.broadcasted_iota(jnp.int32, (rows, LANES), 1) < HEAD_DIM


def _fold_store(x, sc_ref, out_refs):
    tm = x.shape[0]
    if any(d > 1 for d in out_refs):
        for p in range(N_PAIRS):
            sc_ref[p] = x[:, _pair_block(p)]
    for d, o_ref in out_refs.items():
        if d == 1:
            o_ref[0] = x.astype(o_ref.dtype)
            continue
        for r in range(d):
            for p in range(N_PAIRS):
                o_ref[r, :, _pair_block(p)] = sc_ref[p, pl.ds(r, tm // d, stride=d), :].astype(o_ref.dtype)


def _unfold_load(x_ref, sc_ref, d, add=False):
    n = x_ref.shape[1]
    for r in range(d):
        for p in range(N_PAIRS):
            rows = pl.ds(r, n, stride=d) if d > 1 else slice(None)
            val = x_ref[r, :, _pair_block(p)].astype(F32)
            if add:
                val = val + sc_ref[p, rows, :]
            sc_ref[p, rows, :] = val


def _folded_spec(d, tm, w=ATTN_W):
    return pl.BlockSpec((d, tm // d, w), lambda i: (0, i, 0))


def _folded_shape(s, d, dtype, w=ATTN_W):
    return jax.ShapeDtypeStruct((d, s // d, w), dtype)


def _qkv_prep_even(proj, tabs, name):
    s = proj.shape[0]
    tm = 512

    def body(p_ref, c_ref, a_ref, b_ref, q_ref, k_ref, v_ref):
        c, a, b = c_ref[...], a_ref[...], b_ref[...]
        q_ref[0] = _rot(p_ref[:, 0:ATTN_W], _wide(c, ATTN_W), _wide(a, ATTN_W), _wide(b, ATTN_W)).astype(BF16)
        lo = _low_lanes(tm)
        for src, o_ref in ((_rot(p_ref[:, 512:640], c, a, b), k_ref), (p_ref[:, 640:768], v_ref)):
            swapped = pltpu.roll(src, HEAD_DIM, 1)
            o_ref[0, :, 0:LANES] = jnp.where(lo, src, swapped).astype(BF16)
            o_ref[0, :, LANES:] = jnp.where(lo, swapped, src).astype(BF16)

    row = lambda w: pl.BlockSpec((tm, w), lambda i: (i, 0))
    return pl.pallas_call(
        body, name=name, grid=(s // tm,), in_specs=[row(proj.shape[1]), row(LANES), row(LANES), row(LANES)],
        out_specs=[_folded_spec(1, tm), _folded_spec(1, tm, 2 * LANES), _folded_spec(1, tm, 2 * LANES)],
        out_shape=[_folded_shape(s, 1, BF16), _folded_shape(s, 1, BF16, 2 * LANES), _folded_shape(s, 1, BF16, 2 * LANES)],
        compiler_params=_params("parallel"),
    )(proj, *tabs)


def _qkv_post_even(dq, dk, dv, dglu, tabs, name):
    s = dglu.shape[0]
    tm = 512

    def body(dq_ref, dk_ref, dv_ref, dr_ref, c_ref, a_ref, b_ref, o_ref):
        c, a, b = c_ref[...], -a_ref[...], -b_ref[...]
        o_ref[:, 0:ATTN_W] = _rot(dq_ref[0], _wide(c, ATTN_W), _wide(a, ATTN_W), _wide(b, ATTN_W)).astype(BF16)
        lo = _low_lanes(tm)
        merged = []
        for ref in (dk_ref, dv_ref):
            first, second = ref[0, :, 0:LANES], ref[0, :, LANES:]
            merged.append(jnp.where(lo, first + pltpu.roll(first, HEAD_DIM, 1), second + pltpu.roll(second, HEAD_DIM, 1)))
        o_ref[:, 512:640] = _rot(merged[0], c, a, b).astype(BF16)
        o_ref[:, 640:768] = merged[1].astype(BF16)
        o_ref[:, 768:] = dr_ref[...]

    row = lambda w: pl.BlockSpec((tm, w), lambda i: (i, 0))
    return pl.pallas_call(
        body, name=name, grid=(s // tm,),
        in_specs=[_folded_spec(1, tm), _folded_spec(1, tm, 2 * LANES), _folded_spec(1, tm, 2 * LANES),
                  row(dglu.shape[1]), row(LANES), row(LANES), row(LANES)],
        out_specs=row(EVEN_IN), out_shape=jax.ShapeDtypeStruct((s, EVEN_IN), BF16),
        compiler_params=_params("parallel"),
    )(dq, dk, dv, dglu, *tabs)


def _qkv_prep_odd(proj, tabs, name):
    s = proj.shape[0]
    tm = 512

    def body(p_ref, c_ref, a_ref, b_ref, *rest):
        outs, sc_ref = rest[:-1], rest[-1]
        c, a, b = (_wide(t[...], ATTN_W) for t in (c_ref, a_ref, b_ref))
        for t in range(3):
            x = p_ref[:, t * ATTN_W:(t + 1) * ATTN_W]
            if t < 2:
                x = _rot(x, c, a, b)
            _fold_store(x, sc_ref, {d: outs[t * len(DILATIONS) + i] for i, d in enumerate(DILATIONS)})

    row = lambda w: pl.BlockSpec((tm, w), lambda i: (i, 0))
    return pl.pallas_call(
        body, name=name, grid=(s // tm,), in_specs=[row(proj.shape[1]), row(LANES), row(LANES), row(LANES)],
        out_specs=[_folded_spec(d, tm) for _ in range(3) for d in DILATIONS],
        out_shape=[_folded_shape(s, d, BF16) for _ in range(3) for d in DILATIONS],
        scratch_shapes=[pltpu.VMEM((N_PAIRS, tm, LANES), F32)],
        compiler_params=_params("parallel"),
    )(proj, *tabs)


def _qkv_post_odd(dqs, dks, dvs, dz, tabs, name):
    s = dz.shape[0]
    tm = 256
    nb = len(DILATIONS)

    def body(*refs):
        groups = (refs[:nb], refs[nb:2 * nb], refs[2 * nb:3 * nb])
        dz_ref, c_ref, a_ref, b_ref, o_ref, sc_ref = refs[3 * nb:]
        c, a, b = _wide(c_ref[...], ATTN_W), _wide(-a_ref[...], ATTN_W), _wide(-b_ref[...], ATTN_W)
        for t, group in enumerate(groups):
            for i, d in enumerate(DILATIONS):
                _unfold_load(group[i], sc_ref, d, add=i > 0)
            x = jnp.concatenate([sc_ref[p] for p in range(N_PAIRS)], axis=1)
            if t < 2:
                x = _rot(x, c, a, b)
            o_ref[:, t * ATTN_W:(t + 1) * ATTN_W] = x.astype(BF16)
        o_ref[:, 3 * ATTN_W:] = dz_ref[...]

    row = lambda w: pl.BlockSpec((tm, w), lambda i: (i, 0))
    return pl.pallas_call(
        body, name=name, grid=(s // tm,),
        in_specs=[_folded_spec(d, tm) for _ in range(3) for d in DILATIONS] + [row(dz.shape[1]), row(LANES), row(LANES), row(LANES)],
        out_specs=row(ODD_IN), out_shape=jax.ShapeDtypeStruct((s, ODD_IN), BF16),
        scratch_shapes=[pltpu.VMEM((N_PAIRS, tm, LANES), F32)],
        compiler_params=_params("parallel"),
    )(*dqs, *dks, *dvs, dz, *tabs)


def _fold_dout(dmix, name):
    s = dmix.shape[0]
    tm = 512
    ds = [d for d in DILATIONS if d > 1]

    def body(d_ref, *rest):
        outs, sc_ref = rest[:-1], rest[-1]
        _fold_store(d_ref[...], sc_ref, dict(zip(ds, outs)))

    return pl.pallas_call(
        body, name=name, grid=(s // tm,), in_specs=[pl.BlockSpec((tm, ATTN_W), lambda i: (i, 0))],
        out_specs=[_folded_spec(d, tm) for d in ds], out_shape=[_folded_shape(s, d, BF16) for d in ds],
        scratch_shapes=[pltpu.VMEM((N_PAIRS, tm, LANES), F32)],
        compiler_params=_params("parallel"),
    )(dmix)


def _window(j, i, tq):
    r0 = j * tq + i * BLOCK
    start = pl.multiple_of(jnp.maximum(r0 - BLOCK, 0), BLOCK)
    return pl.ds(start, 2 * BLOCK), r0 - start


def _band_valid(offset, max_dist):
    shape = (2 * BLOCK, 2 * BLOCK)
    dist = (lax.bitwise_and(lax.broadcasted_iota(jnp.int32, shape, 0), BLOCK - 1)
            - lax.broadcasted_iota(jnp.int32, shape, 1) + offset)
    return jnp.abs(2 * dist - max_dist) <= max_dist


def _stack_heads(lo, x):
    zero = jnp.zeros_like(x)
    return jnp.concatenate([jnp.where(lo, x, zero), jnp.where(lo, zero, x)], axis=0)


def _unstack_heads(lo, x):
    return jnp.where(lo, x[:BLOCK], x[BLOCK:])


NT = (((1,), (1,)), ((), ()))
TN = (((0,), (0,)), ((), ()))


def _attn_fwd(q, k, v, sinks, *, max_dist, name, emit_bf16=False):
    d, sp, wq = q.shape
    nq, nk = wq // LANES, k.shape[2] // LANES
    kdiv = nq // nk
    tq = min(sp, 1024)
    nsub = tq // BLOCK
    has_sink = sinks is not None

    def body(*refs):
        refs = list(refs)
        sink_ref = refs.pop(0) if has_sink else None
        q_ref, k_ref, v_ref, o_ref, lse_ref = refs[:5]
        pair = pl.program_id(1)
        j = pl.program_id(2)
        lo = _low_lanes(BLOCK)
        if has_sink:
            first_head = lax.broadcasted_iota(jnp.int32, (2 * BLOCK, 1), 0) < BLOCK
            sk = jnp.where(first_head, sink_ref[2 * pair], sink_ref[2 * pair + 1])
        for i in range(nsub):
            win, offset = _window(j, i, tq)
            rows = slice(i * BLOCK, (i + 1) * BLOCK)
            kw = k_ref[0, win, :]
            vw = v_ref[0, win, :]
            s = lax.dot_general(_stack_heads(lo, q_ref[0, rows, :]), kw, NT, preferred_element_type=F32) * ATTN_SCALE
            s = jnp.where(_band_valid(offset, max_dist), s, NEG)
            m = jnp.max(s, axis=-1, keepdims=True)
            if has_sink:
                m = jnp.maximum(m, sk)
            p = jnp.exp(s - m)
            l = jnp.sum(p, axis=-1, keepdims=True)
            if has_sink:
                l = l + jnp.exp(sk - m)
            o2 = _unstack_heads(lo, jnp.dot(p.astype(BF16), vw, preferred_element_type=F32) / l)
            o_ref[0, rows, :] = o2
            lse_ref[0, rows, :] = _unstack_heads(lo, m + jnp.log(l))
            if emit_bf16:
                refs[5][0, rows, :] = o2.astype(BF16)

    qspec = pl.BlockSpec((1, tq, LANES), lambda r, p, j: (r, j, p))
    kspec = pl.BlockSpec((1, sp, LANES), lambda r, p, j: (r, 0, p // kdiv))
    in_specs = [qspec, kspec, kspec]
    operands = [q, k, v]
    if has_sink:
        in_specs = [pl.BlockSpec(memory_space=pltpu.SMEM)] + in_specs
        operands = [sinks] + operands
    out_shape = [jax.ShapeDtypeStruct(q.shape, F32), jax.ShapeDtypeStruct(q.shape, F32)]
    if emit_bf16:
        out_shape.append(jax.ShapeDtypeStruct(q.shape, BF16))
    return pl.pallas_call(
        body, name=name, grid=(d, nq, sp // tq), in_specs=in_specs, out_specs=[qspec] * len(out_shape),
        out_shape=out_shape, compiler_params=_params("parallel", "parallel", "arbitrary"),
    )(*operands)


def _attn_bwd(q, k, v, do, oo, lse, sinks, *, max_dist, name):
    d, sp, wq = q.shape
    wk = k.shape[2]
    nq, nk = wq // LANES, wk // LANES
    kdiv = nq // nk
    tq = min(sp, 1024)
    nsub = tq // BLOCK
    has_sink = sinks is not None

    def body(*refs):
        refs = list(refs)
        sink_ref = refs.pop(0) if has_sink else None
        q_ref, k_ref, v_ref, do_ref, oo_ref, lse_ref, dq_ref, dk_ref, dv_ref = refs[:9]
        pk, g, j = pl.program_id(1), pl.program_id(2), pl.program_id(3)

        @pl.when((g == 0) & (j == 0))
        def _():
            dk_ref[...] = jnp.zeros_like(dk_ref)
            dv_ref[...] = jnp.zeros_like(dv_ref)

        lo = _low_lanes(BLOCK)
        if has_sink:
            first_head = lax.broadcasted_iota(jnp.int32, (2 * BLOCK, 1), 0) < BLOCK
            pair = pk * kdiv + g
            sk = jnp.where(first_head, sink_ref[2 * pair], sink_ref[2 * pair + 1])
            sink_acc = jnp.zeros((2 * BLOCK, LANES), F32)
        for i in range(nsub):
            win, offset = _window(j, i, tq)
            rows = slice(i * BLOCK, (i + 1) * BLOCK)
            kw = k_ref[0, win, :]
            vw = v_ref[0, win, :]
            do2 = do_ref[0, rows, :].astype(F32)
            qs = _stack_heads(lo, q_ref[0, rows, :])
            dos = _stack_heads(lo, do2.astype(BF16))
            prod = do2 * oo_ref[0, rows, :]
            delta = jnp.sum(_stack_heads(lo, prod), axis=-1, keepdims=True)
            lse2 = lse_ref[0, rows, :]
            lse_swapped = pltpu.roll(lse2, HEAD_DIM, 1)
            lse_st = jnp.concatenate([jnp.where(lo, lse2, lse_swapped), jnp.where(lo, lse_swapped, lse2)], axis=0)
            s = lax.dot_general(qs, kw, NT, preferred_element_type=F32) * ATTN_SCALE
            s = jnp.where(_band_valid(offset, max_dist), s, NEG)
            p = jnp.exp(s - jnp.tile(lse_st, (1, 2)))
            dv_ref[0, win, :] += lax.dot_general(p.astype(BF16), dos, TN, preferred_element_type=F32)
            dp = lax.dot_general(dos, vw, NT, preferred_element_type=F32)
            ds = (p * (dp - delta) * ATTN_SCALE).astype(BF16)
            dq_ref[0, rows, :] = _unstack_heads(lo, jnp.dot(ds, kw, preferred_element_type=F32))
            dk_ref[0, win, :] += lax.dot_general(ds, qs, TN, preferred_element_type=F32)
            if has_sink:
                sink_acc = sink_acc - jnp.exp(sk - lse_st) * delta
        if has_sink:
            dsink_ref = refs[9]

            @pl.when(j == 0)
            def _():
                dsink_ref[...] = jnp.zeros_like(dsink_ref)

            dsink_ref[0] += jnp.where(lo[0:1], jnp.sum(sink_acc[:BLOCK], axis=0, keepdims=True),
                                      jnp.sum(sink_acc[BLOCK:], axis=0, keepdims=True))

    def qmap(r, pk, g, j):
        return (r, j, pk * kdiv + g)

    def kmap(r, pk, g, j):
        return (r, 0, pk)

    qspec = pl.BlockSpec((1, tq, LANES), qmap)
    kspec = pl.BlockSpec((1, sp, LANES), kmap)
    in_specs = [qspec, kspec, kspec, qspec, qspec, qspec]
    operands = [q, k, v, do, oo, lse]
    out_specs = [qspec, kspec, kspec]
    out_shape = [jax.ShapeDtypeStruct((d, sp, wq), F32), jax.ShapeDtypeStruct((d, sp, wk), F32),
                 jax.ShapeDtypeStruct((d, sp, wk), F32)]
    if has_sink:
        in_specs = [pl.BlockSpec(memory_space=pltpu.SMEM)] + in_specs
        operands = [sinks] + operands
        out_specs.append(pl.BlockSpec((1, 1, LANES), lambda r, pk, g, j: (pk * kdiv + g, 0, 0)))
        out_shape.append(jax.ShapeDtypeStruct((nq, 1, LANES), F32))
    return pl.pallas_call(
        body, name=name, grid=(d, nk, kdiv, sp // tq), in_specs=in_specs, out_specs=out_specs, out_shape=out_shape,
        compiler_params=_params("parallel", "parallel", "arbitrary", "arbitrary"),
    )(*operands)


def _combine(outs, lses, name):
    s = outs[0].shape[1]
    tm = 512
    nb = len(DILATIONS)
    ds = [d for d in DILATIONS if d > 1]

    def body(*refs):
        o_refs, l_refs = refs[:nb], refs[nb:2 * nb]
        cb_ref, c_ref, lse_ref = refs[2 * nb:2 * nb + 3]
        folded = refs[2 * nb + 3:2 * nb + 3 + 2 * len(ds)]
        scratch = refs[2 * nb + 3 + 2 * len(ds):]
        so = {1: None}
        sl = {1: None}
        for i, d in enumerate(ds):
            so[d], sl[d] = scratch[2 * i], scratch[2 * i + 1]
            _unfold_load(o_refs[1 + i], so[d], d)
            _unfold_load(l_refs[1 + i], sl[d], d)
        for p in range(N_PAIRS):
            pb = _pair_block(p)
            ls = [l_refs[0][0, :, pb]] + [sl[d][p] for d in ds]
            os_ = [o_refs[0][0, :, pb]] + [so[d][p] for d in ds]
            m = ls[0]
            for t in ls[1:]:
                m = jnp.maximum(m, t)
            ws = [jnp.exp(t - m) for t in ls]
            tot = ws[0]
            for t in ws[1:]:
                tot = tot + t
            acc = ws[0] * os_[0]
            for w, o in zip(ws[1:], os_[1:]):
                acc = acc + w * o
            cmix = acc / tot
            lse = m + jnp.log(tot)
            cb_ref[:, pb] = cmix.astype(BF16)
            c_ref[0, :, pb] = cmix
            lse_ref[0, :, pb] = lse
            so[ds[0]][p] = cmix
            sl[ds[0]][p] = lse
        for i, d in enumerate(ds):
            for r in range(d):
                for p in range(N_PAIRS):
                    rows = pl.ds(r, tm // d, stride=d)
                    folded[2 * i][r, :, _pair_block(p)] = so[ds[0]][p, rows, :]
                    folded[2 * i + 1][r, :, _pair_block(p)] = sl[ds[0]][p, rows, :]

    in_specs = [_folded_spec(d, tm) for _ in range(2) for d in DILATIONS]
    out_specs = [pl.BlockSpec((tm, ATTN_W), lambda i: (i, 0)), _folded_spec(1, tm), _folded_spec(1, tm)]
    out_shape = [jax.ShapeDtypeStruct((s, ATTN_W), BF16), _folded_shape(s, 1, F32), _folded_shape(s, 1, F32)]
    for d in ds:
        out_specs += [_folded_spec(d, tm)] * 2
        out_shape += [_folded_shape(s, d, F32)] * 2
    return pl.pallas_call(
        body, name=name, grid=(s // tm,), in_specs=in_specs, out_specs=out_specs, out_shape=out_shape,
        scratch_shapes=[pltpu.VMEM((N_PAIRS, tm, LANES), F32)] * (2 * len(ds)),
        compiler_params=_params("parallel"),
    )(*outs, *lses)


GLU_A = slice(768, 1280)
GLU_B = slice(1280, 1792)
EVEN_IN = 1792
ODD_IN = 2560
CONV_CH = 512


def _conv_fwd(proj, w, b, ln_g, ln_b, name):
    s = proj.shape[0]
    tm = 512
    nh = tm // CONV_HALO
    lead = CONV_HALO - (CONV_WIDTH - 1)

    def body(p_ref, ph_ref, w_ref, b_ref, g_ref, bb_ref, y_ref, o_ref, xf_ref):
        xf_ref[CONV_HALO:, :] = p_ref[:, GLU_A] * _sigmoid(p_ref[:, GLU_B])
        hist = ph_ref[:, GLU_A] * _sigmoid(ph_ref[:, GLU_B])
        xf_ref[0:CONV_HALO, :] = jnp.where(pl.program_id(0) > 0, hist, 0.0)
        for c0 in range(0, tm, CONV_ROWS):
            acc = jnp.zeros((CONV_ROWS, CONV_CH), F32) + b_ref[...]
            for j in range(CONV_WIDTH):
                acc = acc + xf_ref[pl.ds(lead + j + c0, CONV_ROWS), :] * w_ref[j:j + 1, :]
            y_ref[c0:c0 + CONV_ROWS, :] = acc
            mu = jnp.mean(acc, axis=-1, keepdims=True)
            xc = acc - mu
            var = jnp.mean(xc * xc, axis=-1, keepdims=True)
            zz = xc * lax.rsqrt(var + LN_EPS) * g_ref[...] + bb_ref[...]
            o_ref[c0:c0 + CONV_ROWS, :] = (zz * _sigmoid(zz)).astype(BF16)

    def const(a):
        return pl.BlockSpec(a.shape, lambda i: (0, 0))

    return pl.pallas_call(
        body, name=name, grid=(s // tm,),
        in_specs=[pl.BlockSpec((tm, EVEN_IN), lambda i: (i, 0)),
                  pl.BlockSpec((CONV_HALO, EVEN_IN), lambda i: (jnp.maximum(i * nh - 1, 0), 0)),
                  const(w), const(b), const(ln_g), const(ln_b)],
        out_specs=[pl.BlockSpec((tm, CONV_CH), lambda i: (i, 0)), pl.BlockSpec((tm, CONV_CH), lambda i: (i, 0))],
        out_shape=[jax.ShapeDtypeStruct((s, CONV_CH), F32), jax.ShapeDtypeStruct((s, CONV_CH), BF16)],
        scratch_shapes=[pltpu.VMEM((tm + CONV_HALO, CONV_CH), F32)],
        compiler_params=_params("arbitrary"),
    )(proj, proj, w, b, ln_g, ln_b)


def _conv_tail_bwd(dmix, yconv, ln_g, ln_b, name):
    def body(d_ref, y_ref, g_ref, b_ref, dy_ref, dg_ref, db_ref, dcb_ref):
        @pl.when(_first_step())
        def _():
            dg_ref[...] = jnp.zeros_like(dg_ref)
            db_ref[...] = jnp.zeros_like(db_ref)
            dcb_ref[...] = jnp.zeros_like(dcb_ref)

        y = y_ref[...]
        g = g_ref[...]
        mu = jnp.mean(y, axis=-1, keepdims=True)
        xc = y - mu
        rstd = lax.rsqrt(jnp.mean(xc * xc, axis=-1, keepdims=True) + LN_EPS)
        xh = xc * rstd
        zz = xh * g + b_ref[...]
        sg = _sigmoid(zz)
        dzz = d_ref[:, CONV_CH:] * sg * (1.0 + zz * (1.0 - sg))
        dg_ref[...] += jnp.sum(dzz * xh, axis=0, keepdims=True)
        db_ref[...] += jnp.sum(dzz, axis=0, keepdims=True)
        dxh = dzz * g
        dy = rstd * (dxh - jnp.mean(dxh, axis=-1, keepdims=True) - xh * jnp.mean(dxh * xh, axis=-1, keepdims=True))
        dcb_ref[...] += jnp.sum(dy, axis=0, keepdims=True)
        dy_ref[...] = dy

    vec = ((1, CONV_CH), F32)
    return _rows(body, name, 512, [dmix, yconv], [ln_g, ln_b], [(CONV_CH, F32)], [vec, vec, vec])


def _conv_bwd(proj, dy, w, name):
    s = proj.shape[0]
    tm = 512
    nh = tm // CONV_HALO
    nsteps = s // tm
    lead = CONV_HALO - (CONV_WIDTH - 1)

    def body(p_ref, ph_ref, dy_ref, dyn_ref, w_ref, dglu_ref, dw_ref, xf_ref, dyf_ref):
        i = pl.program_id(0)

        @pl.when(i == 0)
        def _():
            dw_ref[...] = jnp.zeros_like(dw_ref)

        ga = p_ref[:, GLU_A]
        sgb = _sigmoid(p_ref[:, GLU_B])
        xf_ref[CONV_HALO:, :] = ga * sgb
        hist = ph_ref[:, GLU_A] * _sigmoid(ph_ref[:, GLU_B])
        xf_ref[0:CONV_HALO, :] = jnp.where(i > 0, hist, 0.0)
        dyt = dy_ref[...]
        dyf_ref[0:tm, :] = dyt
        dyf_ref[tm:, :] = jnp.where(i < nsteps - 1, dyn_ref[...], 0.0)
        for c0 in range(0, tm, CONV_ROWS):
            rows = slice(c0, c0 + CONV_ROWS)
            acc = jnp.zeros((CONV_ROWS, CONV_CH), F32)
            for j in range(CONV_WIDTH):
                acc = acc + dyf_ref[pl.ds(CONV_WIDTH - 1 - j + c0, CONV_ROWS), :] * w_ref[j:j + 1, :]
            a_c, s_c = ga[rows, :], sgb[rows, :]
            dglu_ref[rows, 0:CONV_CH] = (acc * s_c).astype(BF16)
            dglu_ref[rows, CONV_CH:] = (acc * a_c * s_c * (1.0 - s_c)).astype(BF16)
        for j in range(CONV_WIDTH):
            part = jnp.zeros((8, CONV_CH), F32)
            for c0 in range(0, tm, CONV_ROWS):
                prod = dy_ref[c0:c0 + CONV_ROWS, :] * xf_ref[pl.ds(lead + j + c0, CONV_ROWS), :]
                part = part + jnp.sum(prod.reshape(CONV_ROWS // 8, 8, CONV_CH), axis=0)
            dw_ref[j:j + 1, :] += jnp.sum(part, axis=0, keepdims=True)

    return pl.pallas_call(
        body, name=name, grid=(nsteps,),
        in_specs=[pl.BlockSpec((tm, EVEN_IN), lambda i: (i, 0)),
                  pl.BlockSpec((CONV_HALO, EVEN_IN), lambda i: (jnp.maximum(i * nh - 1, 0), 0)),
                  pl.BlockSpec((tm, CONV_CH), lambda i: (i, 0)),
                  pl.BlockSpec((CONV_HALO, CONV_CH), lambda i: (jnp.minimum((i + 1) * nh, s // CONV_HALO - 1), 0)),
                  pl.BlockSpec(w.shape, lambda i: (0, 0))],
        out_specs=[pl.BlockSpec((tm, 2 * CONV_CH), lambda i: (i, 0)), pl.BlockSpec(w.shape, lambda i: (0, 0))],
        out_shape=[jax.ShapeDtypeStruct((s, 2 * CONV_CH), BF16), jax.ShapeDtypeStruct(w.shape, F32)],
        scratch_shapes=[pltpu.VMEM((tm + CONV_HALO, CONV_CH), F32), pltpu.VMEM((tm + CONV_HALO, CONV_CH), F32)],
        compiler_params=_params("arbitrary"),
    )(proj, proj, dy, dy, w)


GATE_Z = slice(1536, 2560)
D_CH = 512
GELU_C = math.sqrt(2.0 / math.pi)
GELU_K = 0.044715


def _gelu_parts(z):
    t = jnp.tanh(GELU_C * (z + GELU_K * z * z * z))
    return 0.5 * z * (1.0 + t), t


def _lane_group(rows):
    return lax.broadcasted_iota(jnp.int32, (rows, D_CH), 1) // HEAD_DIM


def _tril_mask():
    return lax.broadcasted_iota(jnp.int32, (BLOCK, BLOCK), 0) >= lax.broadcasted_iota(jnp.int32, (BLOCK, BLOCK), 1)


def _layer_norm_parts(x):
    mu = jnp.mean(x, axis=-1, keepdims=True)
    xc = x - mu
    rstd = lax.rsqrt(jnp.mean(xc * xc, axis=-1, keepdims=True) + LN_EPS)
    return xc * rstd, rstd


def _gate_fwd(proj, ln_g, ln_b, w_sp, sb_t, name):
    tm = 512

    def body(p_ref, g_ref, b_ref, w_ref, sb_ref, mixed_ref, out_ref):
        zz, _ = _gelu_parts(p_ref[:, GATE_Z])
        u = zz[:, :D_CH]
        xh, _ = _layer_norm_parts(zz[:, D_CH:])
        gn = (xh * g_ref[...] + b_ref[...]).astype(BF16)
        grp = _lane_group(BLOCK)
        tri = _tril_mask()
        ws = [jnp.where(tri, w_ref[gi], 0.0).astype(BF16) for gi in range(N_GROUPS)]
        bias = jnp.zeros((BLOCK, D_CH), F32)
        for gi in range(N_GROUPS):
            bias = jnp.where(grp == gi, sb_ref[:, gi:gi + 1], bias)
        for ch in range(tm // BLOCK):
            rows = slice(ch * BLOCK, (ch + 1) * BLOCK)
            gc = gn[rows, :]
            mixed = bias
            for gi in range(N_GROUPS):
                r = jnp.dot(ws[gi], gc, preferred_element_type=F32)
                mixed = jnp.where(grp == gi, r + bias, mixed)
            mixed_ref[rows, :] = mixed
            out_ref[rows, :] = (u[rows, :] * mixed).astype(BF16)

    return _rows(body, name, tm, [proj], [ln_g, ln_b, w_sp, sb_t], [(D_CH, F32), (D_CH, BF16)])


def _gate_bwd(dmix, proj, mixed, ln_g, ln_b, w_sp, name):
    tm = 512

    def body(d_ref, p_ref, m_ref, g_ref, b_ref, w_ref, dz_ref, dg_ref, db_ref, dw_ref, dsb_ref, dgn_ref):
        @pl.when(_first_step())
        def _():
            dg_ref[...] = jnp.zeros_like(dg_ref)
            db_ref[...] = jnp.zeros_like(db_ref)
            dw_ref[...] = jnp.zeros_like(dw_ref)
            dsb_ref[...] = jnp.zeros_like(dsb_ref)

        z = p_ref[:, GATE_Z]
        zz, t = _gelu_parts(z)
        u = zz[:, :D_CH]
        xh, rstd = _layer_norm_parts(zz[:, D_CH:])
        g = g_ref[...]
        gn = (xh * g + b_ref[...]).astype(BF16)
        dd = d_ref[:, D_CH:]
        du = dd * m_ref[...]
        dm = dd * u
        grp = _lane_group(BLOCK)
        tri = _tril_mask()
        ws = [jnp.where(tri, w_ref[gi], 0.0).astype(BF16) for gi in range(N_GROUPS)]
        gsel = (lax.broadcasted_iota(jnp.int32, (N_GROUPS, D_CH), 1) // HEAD_DIM
                == lax.broadcasted_iota(jnp.int32, (N_GROUPS, D_CH), 0)).astype(F32)
        for ch in range(tm // BLOCK):
            rows = slice(ch * BLOCK, (ch + 1) * BLOCK)
            dmc = dm[rows, :]
            dmb = dmc.astype(BF16)
            gc = gn[rows, :]
            dgn = jnp.zeros((BLOCK, D_CH), F32)
            for gi in range(N_GROUPS):
                r = lax.dot_general(ws[gi], dmb, TN, preferred_element_type=F32)
                dgn = jnp.where(grp == gi, r, dgn)
                dmg = jnp.where(grp == gi, dmb, jnp.zeros_like(dmb))
                dwg = lax.dot_general(dmg, gc, NT, preferred_element_type=F32)
                dw_ref[gi] += jnp.where(tri, dwg, 0.0)
            dsb_ref[...] += lax.dot_general(gsel, dmc, NT, preferred_element_type=F32, precision=lax.Precision.HIGHEST)
            dgn_ref[rows, :] = dgn
        dgn = dgn_ref[...]
        db_ref[...] += jnp.sum(dgn, axis=0, keepdims=True)
        dg_ref[...] += jnp.sum(dgn * xh, axis=0, keepdims=True)
        dxh = dgn * g
        dgp = rstd * (dxh - jnp.mean(dxh, axis=-1, keepdims=True) - xh * jnp.mean(dxh * xh, axis=-1, keepdims=True))
        dgelu = 0.5 * (1.0 + t) + 0.5 * z * (1.0 - t * t) * GELU_C * (1.0 + 3.0 * GELU_K * z * z)
        dz_ref[:, 0:D_CH] = (du * dgelu[:, :D_CH]).astype(BF16)
        dz_ref[:, D_CH:] = (dgp * dgelu[:, D_CH:]).astype(BF16)

    s = proj.shape[0]
    tiled = [dmix, proj, mixed]
    consts = [ln_g, ln_b, w_sp]
    in_specs = [pl.BlockSpec((tm, a.shape[1]), lambda i: (i, 0)) for a in tiled]
    in_specs += [pl.BlockSpec(a.shape, lambda i, nd=a.ndim: (0,) * nd) for a in consts]
    vec = (1, D_CH)
    acc_shapes = [vec, vec, w_sp.shape, (N_GROUPS, BLOCK)]
    return pl.pallas_call(
        body, name=name, grid=(s // tm,), in_specs=in_specs,
        out_specs=[pl.BlockSpec((tm, 2 * D_CH), lambda i: (i, 0))]
        + [pl.BlockSpec(sh, lambda i, nd=len(sh): (0,) * nd) for sh in acc_shapes],
        out_shape=[jax.ShapeDtypeStruct((s, 2 * D_CH), BF16)] + [jax.ShapeDtypeStruct(sh, F32) for sh in acc_shapes],
        scratch_shapes=[pltpu.VMEM((tm, D_CH), F32)],
        compiler_params=_params("arbitrary"),
    )(*tiled, *consts)


def _adam_update(w, g, m, v):
    nm = ADAM_B1 * m + (1.0 - ADAM_B1) * g
    nv = ADAM_B2 * v + (1.0 - ADAM_B2) * (g * g)
    m_hat = nm / (1.0 - ADAM_B1 ** ADAM_STEP)
    v_hat = nv / (1.0 - ADAM_B2 ** ADAM_STEP)
    return -ADAM_LR * (m_hat / (jnp.sqrt(v_hat) + ADAM_EPS) + ADAM_WD * w), nm, nv


def _adamw(w, g, m, v, name):
    rows, cols = w.shape
    tm = _tile(rows, 512, 8)

    def body(w_ref, g_ref, m_ref, v_ref, d_ref, nm_ref, nv_ref):
        d_ref[...], nm_ref[...], nv_ref[...] = _adam_update(w_ref[...], g_ref[...], m_ref[...], v_ref[...])

    return _rows(body, name, tm, [w, g, m, v], [], [(cols, F32)] * 3)


def _adamw_t(w, g_t, m, v, name):
    layers, kdim, n = w.shape
    tr = 256

    def body(w_ref, g_ref, m_ref, v_ref, go_ref, d_ref, nm_ref, nv_ref):
        g = g_ref[0].T
        go_ref[0] = g
        d_ref[0], nm_ref[0], nv_ref[0] = _adam_update(w_ref[0], g, m_ref[0], v_ref[0])

    wspec = pl.BlockSpec((1, tr, n), lambda l, i: (l, i, 0))
    return pl.pallas_call(
        body, name=name, grid=(layers, kdim // tr),
        in_specs=[wspec, pl.BlockSpec((1, n, tr), lambda l, i: (l, 0, i)), wspec, wspec],
        out_specs=[wspec] * 4, out_shape=[jax.ShapeDtypeStruct(w.shape, F32)] * 4,
        compiler_params=_params("parallel", "parallel"),
    )(w, g_t, m, v)


def _ordered_sum(parts, name):
    n, rows, cols = parts.shape
    tm = _tile(rows, 512, 16 if parts.dtype == BF16 else 8)

    def body(p_ref, o_ref):
        acc = p_ref[0].astype(F32)
        for k in range(1, n):
            acc = acc + p_ref[k].astype(F32)
        o_ref[...] = acc

    return pl.pallas_call(body, name=name, grid=(rows // tm,),
                          in_specs=[pl.BlockSpec((n, tm, cols), lambda i: (0, i, 0))],
                          out_specs=pl.BlockSpec((tm, cols), lambda i: (i, 0)),
                          out_shape=jax.ShapeDtypeStruct((rows, cols), F32), compiler_params=_params("parallel"))(parts)


ANY = pl.BlockSpec(memory_space=pl.ANY)


def _position():
    x, y, c = lax.axis_index("x"), lax.axis_index("y"), lax.axis_index("c")
    other_chips = [(1 - x, y), (x, 1 - y), (1 - x, 1 - y)]
    return x, y, c, other_chips


def _remote(src, dst, send_sem, recv_sem, to):
    return pltpu.make_async_remote_copy(src_ref=src, dst_ref=dst, send_sem=send_sem, recv_sem=recv_sem,
                                        device_id=to, device_id_type=MESH)


STAGE_ROWS = 736


def _staged_copies(copies, buf, in_sems, out_sems):
    n = len(copies)

    def into(u):
        src = copies[u][0]
        return pltpu.make_async_copy(src, buf.at[u % 2, pl.ds(0, src.shape[0]), :], in_sems.at[u % 2])

    def out_of(u):
        dst = copies[u][1]
        return pltpu.make_async_copy(buf.at[u % 2, pl.ds(0, dst.shape[0]), :], dst, out_sems.at[u % 2])

    into(0).start()
    for u in range(n):
        into(u).wait()
        out_of(u).start()
        if u + 1 < n:
            if u >= 1:
                out_of(u - 1).wait()
            into(u + 1).start()
    if n >= 2:
        out_of(n - 2).wait()
    out_of(n - 1).wait()


def _stage_scratch(dtype, cols):
    return [pltpu.VMEM((2, STAGE_ROWS, cols), dtype), pltpu.SemaphoreType.DMA((2,)), pltpu.SemaphoreType.DMA((2,))]


def _row_chunks(rows):
    return [(r, min(STAGE_ROWS, rows - r)) for r in range(0, rows, STAGE_ROWS)]


def _gather_chips(shard, name):
    rows, cols = shard.shape
    half = rows // 2

    def body(in_ref, out_ref, send_sems, recv_sems, buf, in_sems, out_sems):
        x, y, c, chips = _position()
        me = 2 * x + y
        sibling = (x, y, 1 - c)

        def slab(chip, h):
            return out_ref.at[chip, pl.ds(h * half, half), :]

        first = [_remote(in_ref.at[pl.ds(c * half, half), :], slab(me, c), send_sems.at[j], recv_sems.at[j], (cx, cy, c))
                 for j, (cx, cy) in enumerate(chips)]
        for cp in first:
            cp.start()
        _staged_copies([(in_ref.at[pl.ds(r, n), :], out_ref.at[me, pl.ds(r, n), :]) for r, n in _row_chunks(rows)],
                       buf, in_sems, out_sems)
        passed = []
        for j, (cx, cy) in enumerate(chips):
            got = slab(2 * cx + cy, c)
            _remote(got, got, send_sems.at[j], recv_sems.at[j], sibling).wait_recv()
            cp = _remote(got, got, send_sems.at[3 + j], recv_sems.at[3 + j], sibling)
            cp.start()
            passed.append(cp)
        for j, (cx, cy) in enumerate(chips):
            got = slab(2 * cx + cy, 1 - c)
            _remote(got, got, send_sems.at[3 + j], recv_sems.at[3 + j], sibling).wait_recv()
        for cp in first + passed:
            cp.wait_send()

    return pl.pallas_call(
        body, name=name, in_specs=[ANY], out_specs=ANY,
        out_shape=jax.ShapeDtypeStruct((N_CHIPS, rows, cols), shard.dtype),
        scratch_shapes=[pltpu.SemaphoreType.DMA((6,)), pltpu.SemaphoreType.DMA((6,))] + _stage_scratch(shard.dtype, cols),
        compiler_params=pltpu.CompilerParams(vmem_limit_bytes=VMEM_LIMIT),
    )(shard)


def _gather_devices(block, name):
    rows, cols = block.shape

    def body(in_ref, out_ref, send_sems, recv_sems, local_sem):
        x, y, c, chips = _position()
        sibling = (x, y, 1 - c)

        def slot(px, py, pc):
            return out_ref.at[4 * px + 2 * py + pc]

        mine = pltpu.make_async_copy(in_ref, slot(x, y, c), local_sem)
        mine.start()
        first = [_remote(in_ref, slot(x, y, c), send_sems.at[0], recv_sems.at[0], sibling)]
        first += [_remote(in_ref, slot(x, y, c), send_sems.at[1 + j], recv_sems.at[1 + j], (cx, cy, c))
                  for j, (cx, cy) in enumerate(chips)]
        for cp in first:
            cp.start()
        passed = []
        for j, (cx, cy) in enumerate(chips):
            got = slot(cx, cy, c)
            _remote(got, got, send_sems.at[1 + j], recv_sems.at[1 + j], sibling).wait_recv()
            cp = _remote(got, got, send_sems.at[4 + j], recv_sems.at[4 + j], sibling)
            cp.start()
            passed.append(cp)
        got = slot(x, y, 1 - c)
        _remote(got, got, send_sems.at[0], recv_sems.at[0], sibling).wait_recv()
        for j, (cx, cy) in enumerate(chips):
            got = slot(cx, cy, 1 - c)
            _remote(got, got, send_sems.at[4 + j], recv_sems.at[4 + j], sibling).wait_recv()
        for cp in first + passed:
            cp.wait_send()
        mine.wait()

    return pl.pallas_call(
        body, name=name, in_specs=[ANY], out_specs=ANY,
        out_shape=jax.ShapeDtypeStruct((N_DEV, rows, cols), block.dtype),
        scratch_shapes=[pltpu.SemaphoreType.DMA((7,)), pltpu.SemaphoreType.DMA((7,)), pltpu.SemaphoreType.DMA],
    )(block)


def _pair_send(grads, name):
    n = len(grads)
    hs = [g.shape[2] for g in grads]
    offs = [sum(hs[:i]) for i in range(n)]
    cols = grads[0].shape[3]

    def body(*refs):
        g_refs = refs[:n]
        got_ref, send_sems, recv_sems = refs[n:]
        x, y, c, _ = _position()
        copies = [_remote(g_ref.at[:, 1 - c], got_ref.at[:, pl.ds(offs[i], hs[i]), :], send_sems.at[i], recv_sems.at[i],
                          (x, y, 1 - c)) for i, g_ref in enumerate(g_refs)]
        for cp in copies:
            cp.start()
        for cp in copies:
            cp.wait()

    return pl.pallas_call(
        body, name=name, in_specs=[ANY] * n, out_specs=ANY, out_shape=jax.ShapeDtypeStruct((N_CHIPS, sum(hs), cols), F32),
        scratch_shapes=[pltpu.SemaphoreType.DMA((n,)), pltpu.SemaphoreType.DMA((n,))],
    )(*grads)


def _pair_add(grads, got, name):
    n = len(grads)
    hs = [g.shape[2] for g in grads]
    offs = [sum(hs[:i]) for i in range(n)]
    cols = grads[0].shape[3]
    hmax = max(hs)
    units = [(i, k) for k in range(N_CHIPS) for i in range(n)]

    def body(*refs):
        g_refs = refs[:n]
        got_ref, out_ref, a_buf, b_buf, o_buf, a_sems, b_sems, o_sems = refs[n:]
        c = lax.axis_index("c")

        def loads(u):
            i, k = units[u]
            slot, rows = u % 2, pl.ds(0, hs[i])
            return (pltpu.make_async_copy(g_refs[i].at[k, c], a_buf.at[slot, rows, :], a_sems.at[slot]),
                    pltpu.make_async_copy(got_ref.at[k, pl.ds(offs[i], hs[i]), :], b_buf.at[slot, rows, :], b_sems.at[slot]))

        def store(u):
            i, k = units[u]
            return pltpu.make_async_copy(o_buf.at[u % 2, pl.ds(0, hs[i]), :], out_ref.at[k, pl.ds(offs[i], hs[i]), :],
                                         o_sems.at[u % 2])

        for cp in loads(0):
            cp.start()
        for u, (i, k) in enumerate(units):
            if u + 1 < len(units):
                for cp in loads(u + 1):
                    cp.start()
            for cp in loads(u):
                cp.wait()
            if u >= 2:
                store(u - 2).wait()
            rows = pl.ds(0, hs[i])
            o_buf[u % 2, rows, :] = (a_buf[u % 2, rows, :] + b_buf[u % 2, rows, :]).astype(BF16)
            store(u).start()
        store(len(units) - 2).wait()
        store(len(units) - 1).wait()

    return pl.pallas_call(
        body, name=name, in_specs=[ANY] * (n + 1), out_specs=ANY,
        out_shape=jax.ShapeDtypeStruct((N_CHIPS, sum(hs), cols), BF16),
        scratch_shapes=[pltpu.VMEM((2, hmax, cols), F32), pltpu.VMEM((2, hmax, cols), F32), pltpu.VMEM((2, hmax, cols), BF16),
                        pltpu.SemaphoreType.DMA((2,)), pltpu.SemaphoreType.DMA((2,)), pltpu.SemaphoreType.DMA((2,))],
        compiler_params=pltpu.CompilerParams(vmem_limit_bytes=VMEM_LIMIT),
    )(*grads, got)


def _chip_exchange(parts, name):
    _, rows, cols = parts.shape

    def body(in_ref, out_ref, send_sems, recv_sems):
        x, y, c, chips = _position()
        sent = [_remote(in_ref.at[2 * cx + cy], out_ref.at[j], send_sems.at[j], recv_sems.at[j], (cx, cy, c))
                for j, (cx, cy) in enumerate(chips)]
        for cp in sent:
            cp.start()
        for cp in sent:
            cp.wait()

    return pl.pallas_call(
        body, name=name, in_specs=[ANY], out_specs=ANY, out_shape=jax.ShapeDtypeStruct((3, rows, cols), parts.dtype),
        scratch_shapes=[pltpu.SemaphoreType.DMA((3,)), pltpu.SemaphoreType.DMA((3,))],
    )(parts)


def _chip_sum(parts, recv, chip, name):
    _, rows, cols = parts.shape
    tm = _tile(rows, 512, 16)

    def body(chip_ref, own_ref, recv_ref, o_ref):
        acc = own_ref[0].astype(F32)
        for j in range(3):
            acc = acc + recv_ref[j].astype(F32)
        o_ref[...] = acc

    return pl.pallas_call(
        body, name=name,
        grid_spec=pltpu.PrefetchScalarGridSpec(
            num_scalar_prefetch=1, grid=(rows // tm,),
            in_specs=[pl.BlockSpec((1, tm, cols), lambda i, chip_ref: (chip_ref[0], i, 0)),
                      pl.BlockSpec((3, tm, cols), lambda i, chip_ref: (0, i, 0))],
            out_specs=pl.BlockSpec((tm, cols), lambda i, chip_ref: (i, 0))),
        out_shape=jax.ShapeDtypeStruct((rows, cols), F32), compiler_params=_params("parallel"),
    )(chip, parts, recv)


def _join_unpack(mine, hs, name):
    n = len(hs)
    offs = [sum(hs[:i]) for i in range(n)]
    cols = mine.shape[1]

    def body(in_ref, *refs):
        outs = refs[:n]
        send_sems, recv_sems, buf, in_sems, out_sems = refs[n:]
        x, y, c, _ = _position()
        sibling = (x, y, 1 - c)
        sent, local = [], []
        for i, o_ref in enumerate(outs):
            src = in_ref.at[pl.ds(offs[i], hs[i]), :]
            here = o_ref.at[pl.ds(c * hs[i], hs[i]), :]
            cp = _remote(src, here, send_sems.at[i], recv_sems.at[i], sibling)
            cp.start()
            sent.append(cp)
            local.append((src, here))
        _staged_copies(local, buf, in_sems, out_sems)
        for i, (cp, o_ref) in enumerate(zip(sent, outs)):
            there = o_ref.at[pl.ds((1 - c) * hs[i], hs[i]), :]
            _remote(there, there, send_sems.at[i], recv_sems.at[i], sibling).wait_recv()
            cp.wait_send()

    assert max(hs) <= STAGE_ROWS
    return pl.pallas_call(
        body, name=name, in_specs=[ANY], out_specs=[ANY] * n,
        out_shape=[jax.ShapeDtypeStruct((2 * h, cols), F32) for h in hs],
        scratch_shapes=[pltpu.SemaphoreType.DMA((n,)), pltpu.SemaphoreType.DMA((n,))] + _stage_scratch(F32, cols),
        compiler_params=pltpu.CompilerParams(vmem_limit_bytes=VMEM_LIMIT),
    )(mine)


def _pad_rows(a, mult):
    extra = (-a.shape[0]) % mult
    return a if extra == 0 else jnp.pad(a, ((0, extra), (0, 0)))


def _pack_small(arrs, mult):
    rows = []
    for a in arrs:
        flat = a.reshape(-1)
        extra = (-flat.shape[0]) % LANES
        if extra:
            flat = jnp.pad(flat, (0, extra))
        rows.append(flat.reshape(-1, LANES))
    return _pad_rows(jnp.concatenate(rows, axis=0), mult)


def _unpack_small(packed, shapes):
    out, r = [], 0
    for sh in shapes:
        n = math.prod(sh)
        cnt = -(-n // LANES)
        out.append(packed[r:r + cnt].reshape(-1)[:n].reshape(sh))
        r += cnt
    return out


def _ffn_fwd(h, g_norm, w_gate_t, w_up_t, w_down, tag):
    n = _rms_fwd(h, g_norm, f"{tag}_norm")
    act, gate, up = _ffn_gate_up(n, w_gate_t, w_up_t, f"{tag}_gate_up")
    out = _matmul(act, w_down, add=h, name=f"{tag}_down")
    return out, (n, gate, up, act)


def _ffn_bwd(dh, dhb, h_in, saved, g_norm, w_gate_t, w_up_t, w_down, tag):
    n, gate, up, act = saved
    dgate, dup = _ffn_dact(dhb, w_down, gate, up, f"{tag}_dact")
    dw_down = _matmul(act, dhb, trans_a=True, name=f"{tag}_dwdown")
    dw_gate_t = _matmul(dgate, n, trans_a=True, name=f"{tag}_dwgate")
    dw_up_t = _matmul(dup, n, trans_a=True, name=f"{tag}_dwup")
    dh_in, dh_inb, dg = _dn_norm([(dgate, w_gate_t), (dup, w_up_t)], h_in, g_norm, dh, f"{tag}_dnorm")
    return dh_in, dh_inb, dg, dw_gate_t, dw_up_t, dw_down


def _local_step(x, tgt, w, big):
    s = x.shape[0]
    tabs = _rope_tables(s)
    grads, gbig = {}, {}

    g_ev = w['ev_norm_g']
    n1 = _rms_fwd(x, g_ev, "ev_norm")
    proj0 = _matmul(n1, big['ev_w_in', 0], trans_b=True, name="ev_in")
    q0, k0, v0 = _qkv_prep_even(proj0, tabs, "ev_qkv")
    sinks = w['ev_sinks'].reshape(-1)
    o0, lse0, o0b = _attn_fwd(q0, k0, v0, sinks, max_dist=BLOCK - 1, name="ev_attn", emit_bf16=True)
    yconv, cout = _conv_fwd(proj0, w['ev_conv_w'][0], w['ev_conv_b'], w['ev_conv_ln_g'], w['ev_conv_ln_b'], "ev_conv")
    mix0 = (o0b[0], cout)
    h1 = _matmul(mix0, big['ev_w_out', 0], add=x, name="ev_out")

    g_f0 = w['ffn_norm_g'][0:1]
    h2, ffn0 = _ffn_fwd(h1, g_f0, big['ffn_w_gate', 0], big['ffn_w_up', 0], big['ffn_w_down', 0], "ffn0")

    g_od = w['od_norm_g']
    n3 = _rms_fwd(h2, g_od, "od_norm")
    proj1 = _matmul(n3, big['od_w_in', 0], trans_b=True, name="od_in")
    qkv = _qkv_prep_odd(proj1, tabs, "od_qkv")
    nb = len(DILATIONS)
    outs, lses = [], []
    for i, d in enumerate(DILATIONS):
        o_r, lse_r = _attn_fwd(qkv[i], qkv[nb + i], qkv[2 * nb + i], None, max_dist=BLOCK, name=f"od_attn{d}")
        outs.append(o_r)
        lses.append(lse_r)
    comb = _combine(outs, lses, "od_combine")
    c_bf16 = comb[0]
    c_fold = {1: comb[1]}
    lse_fold = {1: comb[2]}
    for i, d in enumerate(DILATIONS[1:]):
        c_fold[d], lse_fold[d] = comb[3 + 2 * i], comb[4 + 2 * i]
    w_sp = w['od_spatial_w'][0]
    sb_t = w['od_spatial_b'][0].T
    mixed, dout = _gate_fwd(proj1, w['od_sgu_ln_g'], w['od_sgu_ln_b'], w_sp, sb_t, "od_gate")
    mix1 = (c_bf16, dout)
    h3 = _matmul(mix1, big['od_w_out', 0], add=h2, name="od_out")

    g_f1 = w['ffn_norm_g'][1:2]
    h4, ffn1 = _ffn_fwd(h3, g_f1, big['ffn_w_gate', 1], big['ffn_w_up', 1], big['ffn_w_down', 1], "ffn1")

    dh4, dh4b, dg_final, loss_tile = _final_loss(h4, w['final_norm_g'].reshape(1, D_MODEL), tgt, "final")
    grads['final_norm_g'] = dg_final.reshape(D_MODEL)

    dh3, dh3b, dg_f1, gbig['ffn_w_gate', 1], gbig['ffn_w_up', 1], gbig['ffn_w_down', 1] = _ffn_bwd(
        dh4, dh4b, h3, ffn1, g_f1, big['ffn_w_gate', 1], big['ffn_w_up', 1], big['ffn_w_down', 1], "ffn1")

    dmix1 = _matmul(dh3b, big['od_w_out', 0], trans_b=True, name="od_dmix")
    gbig['od_w_out', 0] = _matmul_tn_pair(mix1[0], mix1[1], dh3b, "od_dwout")
    do_fold = dict(zip(DILATIONS[1:], _fold_dout(dmix1, "od_fold_dout")))
    do_fold[1] = dmix1[None]
    dqs, dks, dvs = [], [], []
    for i, d in enumerate(DILATIONS):
        dq_r, dk_r, dv_r = _attn_bwd(qkv[i], qkv[nb + i], qkv[2 * nb + i], do_fold[d], c_fold[d], lse_fold[d], None,
                                     max_dist=BLOCK, name=f"od_dattn{d}")
        dqs.append(dq_r)
        dks.append(dk_r)
        dvs.append(dv_r)
    dz, dg_sgu, db_sgu, dw_sp, dsb = _gate_bwd(dmix1, proj1, mixed, w['od_sgu_ln_g'], w['od_sgu_ln_b'], w_sp, "od_dgate")
    grads['od_sgu_ln_g'], grads['od_sgu_ln_b'] = dg_sgu, db_sgu
    grads['od_spatial_w'], grads['od_spatial_b'] = dw_sp[None], dsb[None]
    dproj1 = _qkv_post_odd(dqs, dks, dvs, dz, tabs, "od_dproj")
    gbig['od_w_in', 0] = _matmul(dproj1, n3, trans_a=True, name="od_dwin")
    dh2, dh2b, dg_od = _dn_norm([(dproj1, big['od_w_in', 0])], h2, g_od, dh3, "od_dnorm")
    grads['od_norm_g'] = dg_od

    dh1, dh1b, dg_f0, gbig['ffn_w_gate', 0], gbig['ffn_w_up', 0], gbig['ffn_w_down', 0] = _ffn_bwd(
        dh2, dh2b, h1, ffn0, g_f0, big['ffn_w_gate', 0], big['ffn_w_up', 0], big['ffn_w_down', 0], "ffn0")
    grads['ffn_norm_g'] = jnp.concatenate([dg_f0, dg_f1], axis=0)

    dmix0 = _matmul(dh1b, big['ev_w_out', 0], trans_b=True, name="ev_dmix")
    gbig['ev_w_out', 0] = _matmul_tn_pair(mix0[0], mix0[1], dh1b, "ev_dwout")
    dq0, dk0, dv0, dsink = _attn_bwd(q0, k0, v0, dmix0[None], o0, lse0, sinks, max_dist=BLOCK - 1, name="ev_dattn")
    grads['ev_sinks'] = dsink[:, 0, :].reshape(N_PAIRS, 2, HEAD_DIM)[:, :, 0].reshape(1, 8)
    dyc, dg_cln, db_cln, dcb = _conv_tail_bwd(dmix0, yconv, w['ev_conv_ln_g'], w['ev_conv_ln_b'], "ev_dconv_tail")
    grads['ev_conv_ln_g'], grads['ev_conv_ln_b'], grads['ev_conv_b'] = dg_cln, db_cln, dcb
    dglu, dconv_w = _conv_bwd(proj0, dyc, w['ev_conv_w'][0], "ev_dconv")
    grads['ev_conv_w'] = dconv_w[None]
    dproj0 = _qkv_post_even(dq0, dk0, dv0, dglu, tabs, "ev_dproj")
    gbig['ev_w_in', 0] = _matmul(dproj0, n1, trans_a=True, name="ev_dwin")
    dx, _, dg_ev = _dn_norm([(dproj0, big['ev_w_in', 0])], x, g_ev, dh1, "ev_dnorm")
    grads['ev_norm_g'] = dg_ev
    return loss_tile, dx, grads, gbig


def _shard_rows(w, layer, by_cols):
    return w[layer].T if by_cols else w[layer]


def kernel(x, ev_norm_g, ev_w_in, ev_sinks, ev_conv_w, ev_conv_b, ev_conv_ln_g, ev_conv_ln_b, ev_w_out, od_norm_g, od_w_in, od_sgu_ln_g, od_sgu_ln_b, od_spatial_w, od_spatial_b, od_w_out, ffn_norm_g, ffn_w_gate, ffn_w_up, ffn_w_down, final_norm_g, loss_target, m_ev_norm_g, m_ev_w_in, m_ev_sinks, m_ev_conv_w, m_ev_conv_b, m_ev_conv_ln_g, m_ev_conv_ln_b, m_ev_w_out, m_od_norm_g, m_od_w_in, m_od_sgu_ln_g, m_od_sgu_ln_b, m_od_spatial_w, m_od_spatial_b, m_od_w_out, m_ffn_norm_g, m_ffn_w_gate, m_ffn_w_up, m_ffn_w_down, m_final_norm_g, v_ev_norm_g, v_ev_w_in, v_ev_sinks, v_ev_conv_w, v_ev_conv_b, v_ev_conv_ln_g, v_ev_conv_ln_b, v_ev_w_out, v_od_norm_g, v_od_w_in, v_od_sgu_ln_g, v_od_sgu_ln_b, v_od_spatial_w, v_od_spatial_b, v_od_w_out, v_ffn_norm_g, v_ffn_w_gate, v_ffn_w_up, v_ffn_w_down, v_final_norm_g):
    given = dict(locals())
    wts = {n: given[n] for n in WEIGHTS}
    mom = {n: given["m_" + n] for n in WEIGHTS}
    var = {n: given["v_" + n] for n in WEIGHTS}
    chip = 2 * lax.axis_index("x") + lax.axis_index("y")

    shard_rows = [_shard_rows(wts[n], layer, by_cols).astype(BF16) for n, layer, by_cols in BIG]
    counts = [a.shape[0] for a in shard_rows]
    all_w = _gather_chips(jnp.concatenate(shard_rows, axis=0), "gather_weights")
    big, r = {}, 0
    for (n, layer, _), cnt in zip(BIG, counts):
        big[n, layer] = all_w[:, r:r + cnt].reshape(N_CHIPS * cnt, D_MODEL)
        r += cnt
    full = {n: wts[n] for n in SMALL_REPL}
    small_shards = [wts[n] for n in SMALL_SHARDED]
    small_shapes = [a.shape for a in small_shards]
    all_s = _gather_chips(_pack_small(small_shards, 16), "gather_small_weights")
    per_chip = [_unpack_small(all_s[k], small_shapes) for k in range(N_CHIPS)]
    for i, n in enumerate(SMALL_SHARDED):
        full[n] = jnp.concatenate([per_chip[k][i] for k in range(N_CHIPS)], axis=-1)

    loss_tile, grad_x, grads, gbig = _local_step(x[0], loss_target[0], full, big)
    loss = lax.psum(loss_tile[0, 0], ("x", "y", "c"))

    halves = [cnt // 2 for cnt in counts]
    split = [gbig[n, layer].reshape(N_CHIPS, 2, h, D_MODEL) for (n, layer, _), h in zip(BIG, halves)]
    got = _pair_send(split, "grad_pair_send")
    chip_part = _pair_add(split, got, "grad_pair_add")
    from_chips = _chip_exchange(chip_part, "grad_chip_exchange")
    my_half = _chip_sum(chip_part, from_chips, chip.reshape(1), "grad_chip_sum")
    reduced = dict(zip([(n, layer) for n, layer, _ in BIG], _join_unpack(my_half, halves, "grad_join_halves")))

    small_names = SMALL_REPL + SMALL_SHARDED
    small_full_shapes = [grads[n].shape for n in small_names]
    spack = _pack_small([grads[n] for n in small_names], 8)
    s_all = _gather_devices(spack, "grad_small_gather")
    s_sum = _unpack_small(_ordered_sum(s_all, "grad_small_sum"), small_full_shapes)
    g_all = dict(zip(small_names, s_sum))
    for n in SMALL_SHARDED:
        width = wts[n].shape[-1]
        g_all[n] = lax.dynamic_slice_in_dim(g_all[n], chip * width, width, axis=g_all[n].ndim - 1)

    delta, new_m, new_v = {}, {}, {}
    for n in BIG_NAMES:
        layers = [layer for nn, layer, _ in BIG if nn == n]
        by_cols = [bc for nn, _, bc in BIG if nn == n][0]
        g_rows = jnp.stack([reduced[n, layer] for layer in layers])
        if by_cols:
            g_all[n], delta[n], new_m[n], new_v[n] = _adamw_t(wts[n], g_rows, mom[n], var[n], f"adamw_{n}")
        else:
            shape = wts[n].shape
            flat = [a.reshape(-1, D_MODEL) for a in (wts[n], g_rows, mom[n], var[n])]
            g_all[n] = g_rows
            delta[n], new_m[n], new_v[n] = (a.reshape(shape) for a in _adamw(*flat, f"adamw_{n}"))
    shapes = [wts[n].shape for n in small_names]
    d_s, m_s, v_s = _adamw(*[_pack_small([src[n] for n in small_names], 8) for src in (wts, g_all, mom, var)], "adamw_small")
    for dst, packed in ((delta, d_s), (new_m, m_s), (new_v, v_s)):
        dst.update(zip(small_names, _unpack_small(packed, shapes)))

    return (loss, grad_x[None], *[g_all[n] for n in WEIGHTS], *[delta[n] for n in WEIGHTS],
            *[new_m[n] for n in WEIGHTS], *[new_v[n] for n in WEIGHTS])
```

```python
import math

import jax
import jax.numpy as jnp
from jax import lax
from jax.experimental import pallas as pl
from jax.experimental.pallas import tpu as pltpu

F32 = jnp.float32
BF16 = jnp.bfloat16

D_MODEL = 1024
HEAD_DIM = 64
ROT_DIM = 16
ROPE_THETA = 500000.0
RMS_EPS = 1e-6
LN_EPS = 1e-5
BLOCK = 128
CONV_WIDTH = 31
CONV_HALO = 32
CONV_ROWS = 64
D_FF = 2816
N_GROUPS = 8
ATTN_W = 512
ATTN_SCALE = HEAD_DIM ** -0.5
NEG = -1e30
DILATIONS = (1, 4, 16)

ADAM_LR = 0.001
ADAM_B1 = 0.9
ADAM_B2 = 0.999
ADAM_EPS = 1e-08
ADAM_WD = 0.01
ADAM_STEP = 10

LANES = 128
N_PAIRS = ATTN_W // LANES
VMEM_LIMIT = 56 * 1024 * 1024
MESH = pl.DeviceIdType.MESH
N_CHIPS = 4
N_DEV = 8

WEIGHTS = ['ev_norm_g', 'ev_w_in', 'ev_sinks', 'ev_conv_w', 'ev_conv_b', 'ev_conv_ln_g', 'ev_conv_ln_b', 'ev_w_out',
           'od_norm_g', 'od_w_in', 'od_sgu_ln_g', 'od_sgu_ln_b', 'od_spatial_w', 'od_spatial_b', 'od_w_out',
           'ffn_norm_g', 'ffn_w_gate', 'ffn_w_up', 'ffn_w_down', 'final_norm_g']
BIG = [('ev_w_in', 0, True), ('ev_w_out', 0, False), ('od_w_in', 0, True), ('od_w_out', 0, False),
       ('ffn_w_gate', 0, True), ('ffn_w_gate', 1, True), ('ffn_w_up', 0, True), ('ffn_w_up', 1, True),
       ('ffn_w_down', 0, False), ('ffn_w_down', 1, False)]
BIG_NAMES = ['ev_w_in', 'ev_w_out', 'od_w_in', 'od_w_out', 'ffn_w_gate', 'ffn_w_up', 'ffn_w_down']
SMALL_SHARDED = ['ev_conv_w', 'od_norm_g', 'od_sgu_ln_g', 'od_sgu_ln_b']
SMALL_REPL = ['ev_norm_g', 'ev_sinks', 'ev_conv_b', 'ev_conv_ln_g', 'ev_conv_ln_b', 'od_spatial_w', 'od_spatial_b',
              'ffn_norm_g', 'final_norm_g']


def _tile(n, cap, mult=LANES):
    best = None
    for t in range(mult, min(n, cap) + 1, mult):
        if n % t == 0:
            best = t
    assert best is not None, (n, cap)
    return best


def _params(*sem):
    return pltpu.CompilerParams(dimension_semantics=sem, vmem_limit_bytes=VMEM_LIMIT)


def _sigmoid(x):
    return 1.0 / (1.0 + jnp.exp(-x))


def _pair_block(p):
    return slice(p * LANES, (p + 1) * LANES)


def _matmul(a, b, *, name, trans_a=False, trans_b=False, add=None, out_dtype=F32):
    parts = a if isinstance(a, (tuple, list)) else (a,)
    if trans_a:
        k, m = parts[0].shape
    else:
        m = parts[0].shape[0]
        k = sum(p.shape[1] for p in parts)
    if trans_b:
        n, k2 = b.shape
    else:
        k2, n = b.shape
    assert k == k2 and b.dtype == BF16 and all(p.dtype == BF16 for p in parts)
    tm = _tile(m, D_FF // 2 if trans_a else 512)
    tn = _tile(n, D_FF // 2)
    tk = k if k <= D_FF else _tile(k, 1024)
    nk = k // tk
    na = len(parts)
    assert na == 1 or (nk == 1 and not trans_a)
    dims = (((0 if trans_a else 1,), (1 if trans_b else 0,)), ((), ()))
    has_add = add is not None

    def body(*refs):
        a_refs, b_ref = refs[:na], refs[na]
        add_ref = refs[na + 1] if has_add else None
        o_ref = refs[na + 1 + has_add]
        a_val = a_refs[0][...] if na == 1 else jnp.concatenate([r[...] for r in a_refs], axis=1)
        part = lax.dot_general(a_val, b_ref[...], dims, preferred_element_type=F32)
        if nk == 1:
            if has_add:
                part = part + add_ref[...]
            o_ref[...] = part.astype(o_ref.dtype)
            return
        acc_ref = refs[-1]
        kk = pl.program_id(2)

        @pl.when(kk == 0)
        def _():
            acc_ref[...] = part

        @pl.when(kk > 0)
        def _():
            acc_ref[...] += part

        @pl.when(kk == nk - 1)
        def _():
            res = acc_ref[...]
            if has_add:
                res = res + add_ref[...]
            o_ref[...] = res.astype(o_ref.dtype)

    if trans_a:
        a_specs = [pl.BlockSpec((tk, tm), lambda i, j, kk: (kk, i))]
    elif na == 1:
        a_specs = [pl.BlockSpec((tm, tk), lambda i, j, kk: (i, kk))]
    else:
        a_specs = [pl.BlockSpec((tm, p.shape[1]), lambda i, j, kk: (i, 0)) for p in parts]
    b_spec = pl.BlockSpec((tn, tk), lambda i, j, kk: (j, kk)) if trans_b else pl.BlockSpec((tk, tn), lambda i, j, kk: (kk, j))
    o_spec = pl.BlockSpec((tm, tn), lambda i, j, kk: (i, j))
    in_specs = a_specs + [b_spec] + ([o_spec] if has_add else [])
    operands = list(parts) + [b] + ([add] if has_add else [])
    return pl.pallas_call(
        body, name=name, grid=(m // tm, n // tn, nk), in_specs=in_specs, out_specs=o_spec,
        out_shape=jax.ShapeDtypeStruct((m, n), out_dtype),
        scratch_shapes=[pltpu.VMEM((tm, tn), F32)] if nk > 1 else [],
        compiler_params=_params("parallel", "parallel", "arbitrary"),
    )(*operands)


def _matmul_tn_pair(a1, a2, b, name):
    kdim, m1 = a1.shape
    m2 = a2.shape[1]
    n = b.shape[1]
    tn = _tile(n, 1024)
    tk = _tile(kdim, 1024)
    nk = kdim // tk
    dims = (((0,), (0,)), ((), ()))

    def body(a1_ref, a2_ref, b_ref, o_ref):
        kk = pl.program_id(1)
        bv = b_ref[...]
        top = lax.dot_general(a1_ref[...], bv, dims, preferred_element_type=F32)
        bot = lax.dot_general(a2_ref[...], bv, dims, preferred_element_type=F32)

        @pl.when(kk == 0)
        def _():
            o_ref[0:m1, :] = top
            o_ref[m1:, :] = bot

        @pl.when(kk > 0)
        def _():
            o_ref[0:m1, :] += top
            o_ref[m1:, :] += bot

    return pl.pallas_call(
        body, name=name, grid=(n // tn, nk),
        in_specs=[pl.BlockSpec((tk, m1), lambda j, kk: (kk, 0)), pl.BlockSpec((tk, m2), lambda j, kk: (kk, 0)),
                  pl.BlockSpec((tk, tn), lambda j, kk: (kk, j))],
        out_specs=pl.BlockSpec((m1 + m2, tn), lambda j, kk: (0, j)),
        out_shape=jax.ShapeDtypeStruct((m1 + m2, n), F32),
        compiler_params=_params("parallel", "arbitrary"),
    )(a1, a2, b)


def _ffn_gate_up(n, w_gate_t, w_up_t, name):
    m, k = n.shape
    f = w_gate_t.shape[0]
    tm, tn = _tile(m, 512), _tile(f, D_FF // 2)

    def body(n_ref, wg_ref, wu_ref, act_ref, gate_ref, up_ref):
        a = n_ref[...]
        gate = lax.dot_general(a, wg_ref[...], NT, preferred_element_type=F32)
        up = lax.dot_general(a, wu_ref[...], NT, preferred_element_type=F32)
        act_ref[...] = (gate * _sigmoid(gate) * up).astype(BF16)
        gate_ref[...] = gate.astype(BF16)
        up_ref[...] = up.astype(BF16)

    wspec = pl.BlockSpec((tn, k), lambda i, j: (j, 0))
    ospec = pl.BlockSpec((tm, tn), lambda i, j: (i, j))
    return pl.pallas_call(
        body, name=name, grid=(m // tm, f // tn), in_specs=[pl.BlockSpec((tm, k), lambda i, j: (i, 0)), wspec, wspec],
        out_specs=[ospec] * 3, out_shape=[jax.ShapeDtypeStruct((m, f), BF16)] * 3,
        compiler_params=_params("parallel", "parallel"),
    )(n, w_gate_t, w_up_t)


def _ffn_dact(dhb, w_down, gate, up, name):
    m, k = dhb.shape
    f = w_down.shape[0]
    tm, tn = _tile(m, 512), _tile(f, D_FF // 2)

    def body(d_ref, w_ref, g_ref, u_ref, dg_ref, du_ref):
        dact = lax.dot_general(d_ref[...], w_ref[...], NT, preferred_element_type=F32)
        g = g_ref[...].astype(F32)
        sg = _sigmoid(g)
        dg_ref[...] = (dact * u_ref[...].astype(F32) * sg * (1.0 + g * (1.0 - sg))).astype(BF16)
        du_ref[...] = (dact * g * sg).astype(BF16)

    ospec = pl.BlockSpec((tm, tn), lambda i, j: (i, j))
    return pl.pallas_call(
        body, name=name, grid=(m // tm, f // tn),
        in_specs=[pl.BlockSpec((tm, k), lambda i, j: (i, 0)), pl.BlockSpec((tn, k), lambda i, j: (j, 0)), ospec, ospec],
        out_specs=[ospec] * 2, out_shape=[jax.ShapeDtypeStruct((m, f), BF16)] * 2,
        compiler_params=_params("parallel", "parallel"),
    )(dhb, w_down, gate, up)


def _dn_norm(pairs, h, g, dres, name):
    m = h.shape[0]
    tm = 512
    np_ = len(pairs)

    def body(*refs):
        a_refs, b_refs = refs[:np_], refs[np_:2 * np_]
        h_ref, dres_ref, g_ref, dh_ref, dhb_ref, dg_ref = refs[2 * np_:]

        @pl.when(_first_step())
        def _():
            dg_ref[...] = jnp.zeros_like(dg_ref)

        dy = jnp.dot(a_refs[0][...], b_refs[0][...], preferred_element_type=F32)
        for a_ref, b_ref in zip(a_refs[1:], b_refs[1:]):
            dy = dy + jnp.dot(a_ref[...], b_ref[...], preferred_element_type=F32)
        x = h_ref[...]
        r = lax.rsqrt(jnp.mean(x * x, axis=-1, keepdims=True) + RMS_EPS)
        xh = x * r
        dg_ref[...] += jnp.sum(dy * xh, axis=0, keepdims=True)
        dxh = dy * g_ref[...]
        tot = dres_ref[...] + r * (dxh - xh * jnp.mean(dxh * xh, axis=-1, keepdims=True))
        dh_ref[...] = tot
        dhb_ref[...] = tot.astype(BF16)

    row = lambda w: pl.BlockSpec((tm, w), lambda i: (i, 0))
    whole = lambda a: pl.BlockSpec(a.shape, lambda i: (0, 0))
    a_list, b_list = [a for a, _ in pairs], [b for _, b in pairs]
    return pl.pallas_call(
        body, name=name, grid=(m // tm,),
        in_specs=[row(a.shape[1]) for a in a_list] + [whole(b) for b in b_list] + [row(D_MODEL), row(D_MODEL), whole(g)],
        out_specs=[row(D_MODEL), row(D_MODEL), pl.BlockSpec((1, D_MODEL), lambda i: (0, 0))],
        out_shape=[jax.ShapeDtypeStruct((m, D_MODEL), F32), jax.ShapeDtypeStruct((m, D_MODEL), BF16),
                   jax.ShapeDtypeStruct((1, D_MODEL), F32)],
        compiler_params=_params("arbitrary"),
    )(*a_list, *b_list, h, dres, g)


def _rows(body, name, tm, tiled, consts, outs, accs=()):
    s = tiled[0].shape[0]
    assert s % tm == 0
    in_specs = [pl.BlockSpec((tm, a.shape[1]), lambda i: (i, 0)) for a in tiled]
    in_specs += [pl.BlockSpec(a.shape, lambda i, nd=a.ndim: (0,) * nd) for a in consts]
    out_shape = [jax.ShapeDtypeStruct((s, c), dt) for c, dt in outs]
    out_shape += [jax.ShapeDtypeStruct(sh, dt) for sh, dt in accs]
    out_specs = [pl.BlockSpec((tm, c), lambda i: (i, 0)) for c, _ in outs]
    out_specs += [pl.BlockSpec(sh, lambda i, nd=len(sh): (0,) * nd) for sh, _ in accs]
    return pl.pallas_call(
        body, name=name, grid=(s // tm,), in_specs=in_specs, out_specs=out_specs, out_shape=out_shape,
        compiler_params=_params("arbitrary"),
    )(*tiled, *consts)


def _first_step():
    return pl.program_id(0) == 0


def _rms_fwd(h, g, name):
    def body(h_ref, g_ref, n_ref):
        x = h_ref[...]
        r = lax.rsqrt(jnp.mean(x * x, axis=-1, keepdims=True) + RMS_EPS)
        n_ref[...] = (x * r * g_ref[...]).astype(BF16)

    return _rows(body, name, 512, [h], [g], [(D_MODEL, BF16)])[0]


def _final_loss(h, g, tgt, name):
    def body(h_ref, t_ref, g_ref, dh_ref, dhb_ref, dg_ref, loss_ref):
        @pl.when(_first_step())
        def _():
            dg_ref[...] = jnp.zeros_like(dg_ref)
            loss_ref[...] = jnp.zeros_like(loss_ref)

        x = h_ref[...]
        r = lax.rsqrt(jnp.mean(x * x, axis=-1, keepdims=True) + RMS_EPS)
        xh = x * r
        gg = g_ref[...]
        e = xh * gg - t_ref[...]
        loss_ref[...] += (0.5 / D_MODEL) * jnp.sum(jnp.sum(e * e, axis=-1, keepdims=True), axis=0, keepdims=True)
        dy = e * (1.0 / D_MODEL)
        dg_ref[...] += jnp.sum(dy * xh, axis=0, keepdims=True)
        dxh = dy * gg
        dx = r * (dxh - xh * jnp.mean(dxh * xh, axis=-1, keepdims=True))
        dh_ref[...] = dx
        dhb_ref[...] = dx.astype(BF16)

    return _rows(body, name, 512, [h, tgt], [g], [(D_MODEL, F32), (D_MODEL, BF16)],
                 [((1, D_MODEL), F32), ((1, LANES), F32)])


def _rope_tables(s):
    half = ROT_DIM // 2
    inv_freq = ROPE_THETA ** (-jnp.arange(half, dtype=F32) * (2.0 / ROT_DIM))
    ang = jnp.arange(s, dtype=F32)[:, None] * inv_freq[None, :]
    cos, sin = jnp.cos(ang), jnp.sin(ang)
    rest = HEAD_DIM - ROT_DIM
    ones = jnp.ones((s, rest), F32)
    zeros = jnp.zeros((s, rest), F32)
    zh = jnp.zeros((s, half), F32)
    c_t = jnp.concatenate([cos, cos, ones], axis=1)
    a_t = jnp.concatenate([-sin, zh, zeros], axis=1)
    b_t = jnp.concatenate([zh, sin, zeros], axis=1)
    return tuple(jnp.tile(t, (1, LANES // HEAD_DIM)) for t in (c_t, a_t, b_t))


def _rot(x, c, a, b):
    w = x.shape[1]
    half = ROT_DIM // 2
    return x * c + pltpu.roll(x, w - half, 1) * a + pltpu.roll(x, half, 1) * b


def _wide(t, w):
    return t if w == LANES else jnp.tile(t, (1, w // LANES))


def _low_lanes(rows):
    return lax.broadcasted_iota(jnp.int32, (rows, LANES), 1) < HEAD_DIM


def _fold_store(x, sc_ref, out_refs):
    tm = x.shape[0]
    if any(d > 1 for d in out_refs):
        for p in range(N_PAIRS):
            sc_ref[p] = x[:, _pair_block(p)]
    for d, o_ref in out_refs.items():
        if d == 1:
            o_ref[0] = x.astype(o_ref.dtype)
            continue
        for r in range(d):
            for p in range(N_PAIRS):
                o_ref[r, :, _pair_block(p)] = sc_ref[p, pl.ds(r, tm // d, stride=d), :].astype(o_ref.dtype)


def _unfold_load(x_ref, sc_ref, d, add=False):
    n = x_ref.shape[1]
    for r in range(d):
        for p in range(N_PAIRS):
            rows = pl.ds(r, n, stride=d) if d > 1 else slice(None)
            val = x_ref[r, :, _pair_block(p)].astype(F32)
            if add:
                val = val + sc_ref[p, rows, :]
            sc_ref[p, rows, :] = val


def _folded_spec(d, tm, w=ATTN_W):
    return pl.BlockSpec((d, tm // d, w), lambda i: (0, i, 0))


def _folded_shape(s, d, dtype, w=ATTN_W):
    return jax.ShapeDtypeStruct((d, s // d, w), dtype)


def _qkv_prep_even(proj, tabs, name):
    s = proj.shape[0]
    tm = 512

    def body(p_ref, c_ref, a_ref, b_ref, q_ref, k_ref, v_ref):
        c, a, b = c_ref[...], a_ref[...], b_ref[...]
        q_ref[0] = _rot(p_ref[:, 0:ATTN_W], _wide(c, ATTN_W), _wide(a, ATTN_W), _wide(b, ATTN_W)).astype(BF16)
        lo = _low_lanes(tm)
        for src, o_ref in ((_rot(p_ref[:, 512:640], c, a, b), k_ref), (p_ref[:, 640:768], v_ref)):
            swapped = pltpu.roll(src, HEAD_DIM, 1)
            o_ref[0, :, 0:LANES] = jnp.where(lo, src, swapped).astype(BF16)
            o_ref[0, :, LANES:] = jnp.where(lo, swapped, src).astype(BF16)

    row = lambda w: pl.BlockSpec((tm, w), lambda i: (i, 0))
    return pl.pallas_call(
        body, name=name, grid=(s // tm,), in_specs=[row(proj.shape[1]), row(LANES), row(LANES), row(LANES)],
        out_specs=[_folded_spec(1, tm), _folded_spec(1, tm, 2 * LANES), _folded_spec(1, tm, 2 * LANES)],
        out_shape=[_folded_shape(s, 1, BF16), _folded_shape(s, 1, BF16, 2 * LANES), _folded_shape(s, 1, BF16, 2 * LANES)],
        compiler_params=_params("parallel"),
    )(proj, *tabs)


def _qkv_post_even(dq, dk, dv, dglu, tabs, name):
    s = dglu.shape[0]
    tm = 512

    def body(dq_ref, dk_ref, dv_ref, dr_ref, c_ref, a_ref, b_ref, o_ref):
        c, a, b = c_ref[...], -a_ref[...], -b_ref[...]
        o_ref[:, 0:ATTN_W] = _rot(dq_ref[0], _wide(c, ATTN_W), _wide(a, ATTN_W), _wide(b, ATTN_W)).astype(BF16)
        lo = _low_lanes(tm)
        merged = []
        for ref in (dk_ref, dv_ref):
            first, second = ref[0, :, 0:LANES], ref[0, :, LANES:]
            merged.append(jnp.where(lo, first + pltpu.roll(first, HEAD_DIM, 1), second + pltpu.roll(second, HEAD_DIM, 1)))
        o_ref[:, 512:640] = _rot(merged[0], c, a, b).astype(BF16)
        o_ref[:, 640:768] = merged[1].astype(BF16)
        o_ref[:, 768:] = dr_ref[...]

    row = lambda w: pl.BlockSpec((tm, w), lambda i: (i, 0))
    return pl.pallas_call(
        body, name=name, grid=(s // tm,),
        in_specs=[_folded_spec(1, tm), _folded_spec(1, tm, 2 * LANES), _folded_spec(1, tm, 2 * LANES),
                  row(dglu.shape[1]), row(LANES), row(LANES), row(LANES)],
        out_specs=row(EVEN_IN), out_shape=jax.ShapeDtypeStruct((s, EVEN_IN), BF16),
        compiler_params=_params("parallel"),
    )(dq, dk, dv, dglu, *tabs)


def _qkv_prep_odd(proj, tabs, name):
    s = proj.shape[0]
    tm = 512

    def body(p_ref, c_ref, a_ref, b_ref, *rest):
        outs, sc_ref = rest[:-1], rest[-1]
        c, a, b = (_wide(t[...], ATTN_W) for t in (c_ref, a_ref, b_ref))
        for t in range(3):
            x = p_ref[:, t * ATTN_W:(t + 1) * ATTN_W]
            if t < 2:
                x = _rot(x, c, a, b)
            _fold_store(x, sc_ref, {d: outs[t * len(DILATIONS) + i] for i, d in enumerate(DILATIONS)})

    row = lambda w: pl.BlockSpec((tm, w), lambda i: (i, 0))
    return pl.pallas_call(
        body, name=name, grid=(s // tm,), in_specs=[row(proj.shape[1]), row(LANES), row(LANES), row(LANES)],
        out_specs=[_folded_spec(d, tm) for _ in range(3) for d in DILATIONS],
        out_shape=[_folded_shape(s, d, BF16) for _ in range(3) for d in DILATIONS],
        scratch_shapes=[pltpu.VMEM((N_PAIRS, tm, LANES), F32)],
        compiler_params=_params("parallel"),
    )(proj, *tabs)


def _qkv_post_odd(dqs, dks, dvs, dz, tabs, name):
    s = dz.shape[0]
    tm = 256
    nb = len(DILATIONS)

    def body(*refs):
        groups = (refs[:nb], refs[nb:2 * nb], refs[2 * nb:3 * nb])
        dz_ref, c_ref, a_ref, b_ref, o_ref, sc_ref = refs[3 * nb:]
        c, a, b = _wide(c_ref[...], ATTN_W), _wide(-a_ref[...], ATTN_W), _wide(-b_ref[...], ATTN_W)
        for t, group in enumerate(groups):
            for i, d in enumerate(DILATIONS):
                _unfold_load(group[i], sc_ref, d, add=i > 0)
            x = jnp.concatenate([sc_ref[p] for p in range(N_PAIRS)], axis=1)
            if t < 2:
                x = _rot(x, c, a, b)
            o_ref[:, t * ATTN_W:(t + 1) * ATTN_W] = x.astype(BF16)
        o_ref[:, 3 * ATTN_W:] = dz_ref[...]

    row = lambda w: pl.BlockSpec((tm, w), lambda i: (i, 0))
    return pl.pallas_call(
        body, name=name, grid=(s // tm,),
        in_specs=[_folded_spec(d, tm) for _ in range(3) for d in DILATIONS] + [row(dz.shape[1]), row(LANES), row(LANES), row(LANES)],
        out_specs=row(ODD_IN), out_shape=jax.ShapeDtypeStruct((s, ODD_IN), BF16),
        scratch_shapes=[pltpu.VMEM((N_PAIRS, tm, LANES), F32)],
        compiler_params=_params("parallel"),
    )(*dqs, *dks, *dvs, dz, *tabs)


def _fold_dout(dmix, name):
    s = dmix.shape[0]
    tm = 512
    ds = [d for d in DILATIONS if d > 1]

    def body(d_ref, *rest):
        outs, sc_ref = rest[:-1], rest[-1]
        _fold_store(d_ref[...], sc_ref, dict(zip(ds, outs)))

    return pl.pallas_call(
        body, name=name, grid=(s // tm,), in_specs=[pl.BlockSpec((tm, ATTN_W), lambda i: (i, 0))],
        out_specs=[_folded_spec(d, tm) for d in ds], out_shape=[_folded_shape(s, d, BF16) for d in ds],
        scratch_shapes=[pltpu.VMEM((N_PAIRS, tm, LANES), F32)],
        compiler_params=_params("parallel"),
    )(dmix)


def _window(j, i, tq):
    r0 = j * tq + i * BLOCK
    start = pl.multiple_of(jnp.maximum(r0 - BLOCK, 0), BLOCK)
    return pl.ds(start, 2 * BLOCK), r0 - start


def _band_valid(offset, max_dist):
    shape = (2 * BLOCK, 2 * BLOCK)
    dist = (lax.bitwise_and(lax.broadcasted_iota(jnp.int32, shape, 0), BLOCK - 1)
            - lax.broadcasted_iota(jnp.int32, shape, 1) + offset)
    return jnp.abs(2 * dist - max_dist) <= max_dist


def _stack_heads(lo, x):
    zero = jnp.zeros_like(x)
    return jnp.concatenate([jnp.where(lo, x, zero), jnp.where(lo, zero, x)], axis=0)


def _unstack_heads(lo, x):
    return jnp.where(lo, x[:BLOCK], x[BLOCK:])


NT = (((1,), (1,)), ((), ()))
TN = (((0,), (0,)), ((), ()))


def _attn_fwd(q, k, v, sinks, *, max_dist, name, emit_bf16=False):
    d, sp, wq = q.shape
    nq, nk = wq // LANES, k.shape[2] // LANES
    kdiv = nq // nk
    tq = min(sp, 1024)
    nsub = tq // BLOCK
    has_sink = sinks is not None

    def body(*refs):
        refs = list(refs)
        sink_ref = refs.pop(0) if has_sink else None
        q_ref, k_ref, v_ref, o_ref, lse_ref = refs[:5]
        pair = pl.program_id(1)
        j = pl.program_id(2)
        lo = _low_lanes(BLOCK)
        if has_sink:
            first_head = lax.broadcasted_iota(jnp.int32, (2 * BLOCK, 1), 0) < BLOCK
            sk = jnp.where(first_head, sink_ref[2 * pair], sink_ref[2 * pair + 1])
        for i in range(nsub):
            win, offset = _window(j, i, tq)
            rows = slice(i * BLOCK, (i + 1) * BLOCK)
            kw = k_ref[0, win, :]
            vw = v_ref[0, win, :]
            s = lax.dot_general(_stack_heads(lo, q_ref[0, rows, :]), kw, NT, preferred_element_type=F32) * ATTN_SCALE
            s = jnp.where(_band_valid(offset, max_dist), s, NEG)
            m = jnp.max(s, axis=-1, keepdims=True)
            if has_sink:
                m = jnp.maximum(m, sk)
            p = jnp.exp(s - m)
            l = jnp.sum(p, axis=-1, keepdims=True)
            if has_sink:
                l = l + jnp.exp(sk - m)
            o2 = _unstack_heads(lo, jnp.dot(p.astype(BF16), vw, preferred_element_type=F32) / l)
            o_ref[0, rows, :] = o2
            lse_ref[0, rows, :] = _unstack_heads(lo, m + jnp.log(l))
            if emit_bf16:
                refs[5][0, rows, :] = o2.astype(BF16)

    qspec = pl.BlockSpec((1, tq, LANES), lambda r, p, j: (r, j, p))
    kspec = pl.BlockSpec((1, sp, LANES), lambda r, p, j: (r, 0, p // kdiv))
    in_specs = [qspec, kspec, kspec]
    operands = [q, k, v]
    if has_sink:
        in_specs = [pl.BlockSpec(memory_space=pltpu.SMEM)] + in_specs
        operands = [sinks] + operands
    out_shape = [jax.ShapeDtypeStruct(q.shape, F32), jax.ShapeDtypeStruct(q.shape, F32)]
    if emit_bf16:
        out_shape.append(jax.ShapeDtypeStruct(q.shape, BF16))
    return pl.pallas_call(
        body, name=name, grid=(d, nq, sp // tq), in_specs=in_specs, out_specs=[qspec] * len(out_shape),
        out_shape=out_shape, compiler_params=_params("parallel", "parallel", "arbitrary"),
    )(*operands)


def _attn_bwd(q, k, v, do, oo, lse, sinks, *, max_dist, name):
    d, sp, wq = q.shape
    wk = k.shape[2]
    nq, nk = wq // LANES, wk // LANES
    kdiv = nq // nk
    tq = min(sp, 1024)
    nsub = tq // BLOCK
    has_sink = sinks is not None

    def body(*refs):
        refs = list(refs)
        sink_ref = refs.pop(0) if has_sink else None
        q_ref, k_ref, v_ref, do_ref, oo_ref, lse_ref, dq_ref, dk_ref, dv_ref = refs[:9]
        pk, g, j = pl.program_id(1), pl.program_id(2), pl.program_id(3)

        @pl.when((g == 0) & (j == 0))
        def _():
            dk_ref[...] = jnp.zeros_like(dk_ref)
            dv_ref[...] = jnp.zeros_like(dv_ref)

        lo = _low_lanes(BLOCK)
        if has_sink:
            first_head = lax.broadcasted_iota(jnp.int32, (2 * BLOCK, 1), 0) < BLOCK
            pair = pk * kdiv + g
            sk = jnp.where(first_head, sink_ref[2 * pair], sink_ref[2 * pair + 1])
            sink_acc = jnp.zeros((2 * BLOCK, LANES), F32)
        for i in range(nsub):
            win, offset = _window(j, i, tq)
            rows = slice(i * BLOCK, (i + 1) * BLOCK)
            kw = k_ref[0, win, :]
            vw = v_ref[0, win, :]
            do2 = do_ref[0, rows, :].astype(F32)
            qs = _stack_heads(lo, q_ref[0, rows, :])
            dos = _stack_heads(lo, do2.astype(BF16))
            prod = do2 * oo_ref[0, rows, :]
            delta = jnp.sum(_stack_heads(lo, prod), axis=-1, keepdims=True)
            lse2 = lse_ref[0, rows, :]
            lse_swapped = pltpu.roll(lse2, HEAD_DIM, 1)
            lse_st = jnp.concatenate([jnp.where(lo, lse2, lse_swapped), jnp.where(lo, lse_swapped, lse2)], axis=0)
            s = lax.dot_general(qs, kw, NT, preferred_element_type=F32) * ATTN_SCALE
            s = jnp.where(_band_valid(offset, max_dist), s, NEG)
            p = jnp.exp(s - jnp.tile(lse_st, (1, 2)))
            dv_ref[0, win, :] += lax.dot_general(p.astype(BF16), dos, TN, preferred_element_type=F32)
            dp = lax.dot_general(dos, vw, NT, preferred_element_type=F32)
            ds = (p * (dp - delta) * ATTN_SCALE).astype(BF16)
            dq_ref[0, rows, :] = _unstack_heads(lo, jnp.dot(ds, kw, preferred_element_type=F32))
            dk_ref[0, win, :] += lax.dot_general(ds, qs, TN, preferred_element_type=F32)
            if has_sink:
                sink_acc = sink_acc - jnp.exp(sk - lse_st) * delta
        if has_sink:
            dsink_ref = refs[9]

            @pl.when(j == 0)
            def _():
                dsink_ref[...] = jnp.zeros_like(dsink_ref)

            dsink_ref[0] += jnp.where(lo[0:1], jnp.sum(sink_acc[:BLOCK], axis=0, keepdims=True),
                                      jnp.sum(sink_acc[BLOCK:], axis=0, keepdims=True))

    def qmap(r, pk, g, j):
        return (r, j, pk * kdiv + g)

    def kmap(r, pk, g, j):
        return (r, 0, pk)

    qspec = pl.BlockSpec((1, tq, LANES), qmap)
    kspec = pl.BlockSpec((1, sp, LANES), kmap)
    in_specs = [qspec, kspec, kspec, qspec, qspec, qspec]
    operands = [q, k, v, do, oo, lse]
    out_specs = [qspec, kspec, kspec]
    out_shape = [jax.ShapeDtypeStruct((d, sp, wq), F32), jax.ShapeDtypeStruct((d, sp, wk), F32),
                 jax.ShapeDtypeStruct((d, sp, wk), F32)]
    if has_sink:
        in_specs = [pl.BlockSpec(memory_space=pltpu.SMEM)] + in_specs
        operands = [sinks] + operands
        out_specs.append(pl.BlockSpec((1, 1, LANES), lambda r, pk, g, j: (pk * kdiv + g, 0, 0)))
        out_shape.append(jax.ShapeDtypeStruct((nq, 1, LANES), F32))
    return pl.pallas_call(
        body, name=name, grid=(d, nk, kdiv, sp // tq), in_specs=in_specs, out_specs=out_specs, out_shape=out_shape,
        compiler_params=_params("parallel", "parallel", "arbitrary", "arbitrary"),
    )(*operands)


def _combine(outs, lses, name):
    s = outs[0].shape[1]
    tm = 512
    nb = len(DILATIONS)
    ds = [d for d in DILATIONS if d > 1]

    def body(*refs):
        o_refs, l_refs = refs[:nb], refs[nb:2 * nb]
        cb_ref, c_ref, lse_ref = refs[2 * nb:2 * nb + 3]
        folded = refs[2 * nb + 3:2 * nb + 3 + 2 * len(ds)]
        scratch = refs[2 * nb + 3 + 2 * len(ds):]
        so = {1: None}
        sl = {1: None}
        for i, d in enumerate(ds):
            so[d], sl[d] = scratch[2 * i], scratch[2 * i + 1]
            _unfold_load(o_refs[1 + i], so[d], d)
            _unfold_load(l_refs[1 + i], sl[d], d)
        for p in range(N_PAIRS):
            pb = _pair_block(p)
            ls = [l_refs[0][0, :, pb]] + [sl[d][p] for d in ds]
            os_ = [o_refs[0][0, :, pb]] + [so[d][p] for d in ds]
            m = ls[0]
            for t in ls[1:]:
                m = jnp.maximum(m, t)
            ws = [jnp.exp(t - m) for t in ls]
            tot = ws[0]
            for t in ws[1:]:
                tot = tot + t
            acc = ws[0] * os_[0]
            for w, o in zip(ws[1:], os_[1:]):
                acc = acc + w * o
            cmix = acc / tot
            lse = m + jnp.log(tot)
            cb_ref[:, pb] = cmix.astype(BF16)
            c_ref[0, :, pb] = cmix
            lse_ref[0, :, pb] = lse
            so[ds[0]][p] = cmix
            sl[ds[0]][p] = lse
        for i, d in enumerate(ds):
            for r in range(d):
                for p in range(N_PAIRS):
                    rows = pl.ds(r, tm // d, stride=d)
                    folded[2 * i][r, :, _pair_block(p)] = so[ds[0]][p, rows, :]
                    folded[2 * i + 1][r, :, _pair_block(p)] = sl[ds[0]][p, rows, :]

    in_specs = [_folded_spec(d, tm) for _ in range(2) for d in DILATIONS]
    out_specs = [pl.BlockSpec((tm, ATTN_W), lambda i: (i, 0)), _folded_spec(1, tm), _folded_spec(1, tm)]
    out_shape = [jax.ShapeDtypeStruct((s, ATTN_W), BF16), _folded_shape(s, 1, F32), _folded_shape(s, 1, F32)]
    for d in ds:
        out_specs += [_folded_spec(d, tm)] * 2
        out_shape += [_folded_shape(s, d, F32)] * 2
    return pl.pallas_call(
        body, name=name, grid=(s // tm,), in_specs=in_specs, out_specs=out_specs, out_shape=out_shape,
        scratch_shapes=[pltpu.VMEM((N_PAIRS, tm, LANES), F32)] * (2 * len(ds)),
        compiler_params=_params("parallel"),
    )(*outs, *lses)


GLU_A = slice(768, 1280)
GLU_B = slice(1280, 1792)
EVEN_IN = 1792
ODD_IN = 2560
CONV_CH = 512


def _conv_fwd(proj, w, b, ln_g, ln_b, name):
    s = proj.shape[0]
    tm = 512
    nh = tm // CONV_HALO
    lead = CONV_HALO - (CONV_WIDTH - 1)

    def body(p_ref, ph_ref, w_ref, b_ref, g_ref, bb_ref, y_ref, o_ref, xf_ref):
        xf_ref[CONV_HALO:, :] = p_ref[:, GLU_A] * _sigmoid(p_ref[:, GLU_B])
        hist = ph_ref[:, GLU_A] * _sigmoid(ph_ref[:, GLU_B])
        xf_ref[0:CONV_HALO, :] = jnp.where(pl.program_id(0) > 0, hist, 0.0)
        for c0 in range(0, tm, CONV_ROWS):
            acc = jnp.zeros((CONV_ROWS, CONV_CH), F32) + b_ref[...]
            for j in range(CONV_WIDTH):
                acc = acc + xf_ref[pl.ds(lead + j + c0, CONV_ROWS), :] * w_ref[j:j + 1, :]
            y_ref[c0:c0 + CONV_ROWS, :] = acc
            mu = jnp.mean(acc, axis=-1, keepdims=True)
            xc = acc - mu
            var = jnp.mean(xc * xc, axis=-1, keepdims=True)
            zz = xc * lax.rsqrt(var + LN_EPS) * g_ref[...] + bb_ref[...]
            o_ref[c0:c0 + CONV_ROWS, :] = (zz * _sigmoid(zz)).astype(BF16)

    def const(a):
        return pl.BlockSpec(a.shape, lambda i: (0, 0))

    return pl.pallas_call(
        body, name=name, grid=(s // tm,),
        in_specs=[pl.BlockSpec((tm, EVEN_IN), lambda i: (i, 0)),
                  pl.BlockSpec((CONV_HALO, EVEN_IN), lambda i: (jnp.maximum(i * nh - 1, 0), 0)),
                  const(w), const(b), const(ln_g), const(ln_b)],
        out_specs=[pl.BlockSpec((tm, CONV_CH), lambda i: (i, 0)), pl.BlockSpec((tm, CONV_CH), lambda i: (i, 0))],
        out_shape=[jax.ShapeDtypeStruct((s, CONV_CH), F32), jax.ShapeDtypeStruct((s, CONV_CH), BF16)],
        scratch_shapes=[pltpu.VMEM((tm + CONV_HALO, CONV_CH), F32)],
        compiler_params=_params("arbitrary"),
    )(proj, proj, w, b, ln_g, ln_b)


def _conv_tail_bwd(dmix, yconv, ln_g, ln_b, name):
    def body(d_ref, y_ref, g_ref, b_ref, dy_ref, dg_ref, db_ref, dcb_ref):
        @pl.when(_first_step())
        def _():
            dg_ref[...] = jnp.zeros_like(dg_ref)
            db_ref[...] = jnp.zeros_like(db_ref)
            dcb_ref[...] = jnp.zeros_like(dcb_ref)

        y = y_ref[...]
        g = g_ref[...]
        mu = jnp.mean(y, axis=-1, keepdims=True)
        xc = y - mu
        rstd = lax.rsqrt(jnp.mean(xc * xc, axis=-1, keepdims=True) + LN_EPS)
        xh = xc * rstd
        zz = xh * g + b_ref[...]
        sg = _sigmoid(zz)
        dzz = d_ref[:, CONV_CH:] * sg * (1.0 + zz * (1.0 - sg))
        dg_ref[...] += jnp.sum(dzz * xh, axis=0, keepdims=True)
        db_ref[...] += jnp.sum(dzz, axis=0, keepdims=True)
        dxh = dzz * g
        dy = rstd * (dxh - jnp.mean(dxh, axis=-1, keepdims=True) - xh * jnp.mean(dxh * xh, axis=-1, keepdims=True))
        dcb_ref[...] += jnp.sum(dy, axis=0, keepdims=True)
        dy_ref[...] = dy

    vec = ((1, CONV_CH), F32)
    return _rows(body, name, 512, [dmix, yconv], [ln_g, ln_b], [(CONV_CH, F32)], [vec, vec, vec])


def _conv_bwd(proj, dy, w, name):
    s = proj.shape[0]
    tm = 512
    nh = tm // CONV_HALO
    nsteps = s // tm
    lead = CONV_HALO - (CONV_WIDTH - 1)

    def body(p_ref, ph_ref, dy_ref, dyn_ref, w_ref, dglu_ref, dw_ref, xf_ref, dyf_ref):
        i = pl.program_id(0)

        @pl.when(i == 0)
        def _():
            dw_ref[...] = jnp.zeros_like(dw_ref)

        ga = p_ref[:, GLU_A]
        sgb = _sigmoid(p_ref[:, GLU_B])
        xf_ref[CONV_HALO:, :] = ga * sgb
        hist = ph_ref[:, GLU_A] * _sigmoid(ph_ref[:, GLU_B])
        xf_ref[0:CONV_HALO, :] = jnp.where(i > 0, hist, 0.0)
        dyt = dy_ref[...]
        dyf_ref[0:tm, :] = dyt
        dyf_ref[tm:, :] = jnp.where(i < nsteps - 1, dyn_ref[...], 0.0)
        for c0 in range(0, tm, CONV_ROWS):
            rows = slice(c0, c0 + CONV_ROWS)
            acc = jnp.zeros((CONV_ROWS, CONV_CH), F32)
            for j in range(CONV_WIDTH):
                acc = acc + dyf_ref[pl.ds(CONV_WIDTH - 1 - j + c0, CONV_ROWS), :] * w_ref[j:j + 1, :]
            a_c, s_c = ga[rows, :], sgb[rows, :]
            dglu_ref[rows, 0:CONV_CH] = (acc * s_c).astype(BF16)
            dglu_ref[rows, CONV_CH:] = (acc * a_c * s_c * (1.0 - s_c)).astype(BF16)
        for j in range(CONV_WIDTH):
            part = jnp.zeros((8, CONV_CH), F32)
            for c0 in range(0, tm, CONV_ROWS):
                prod = dy_ref[c0:c0 + CONV_ROWS, :] * xf_ref[pl.ds(lead + j + c0, CONV_ROWS), :]
                part = part + jnp.sum(prod.reshape(CONV_ROWS // 8, 8, CONV_CH), axis=0)
            dw_ref[j:j + 1, :] += jnp.sum(part, axis=0, keepdims=True)

    return pl.pallas_call(
        body, name=name, grid=(nsteps,),
        in_specs=[pl.BlockSpec((tm, EVEN_IN), lambda i: (i, 0)),
                  pl.BlockSpec((CONV_HALO, EVEN_IN), lambda i: (jnp.maximum(i * nh - 1, 0), 0)),
                  pl.BlockSpec((tm, CONV_CH), lambda i: (i, 0)),
                  pl.BlockSpec((CONV_HALO, CONV_CH), lambda i: (jnp.minimum((i + 1) * nh, s // CONV_HALO - 1), 0)),
                  pl.BlockSpec(w.shape, lambda i: (0, 0))],
        out_specs=[pl.BlockSpec((tm, 2 * CONV_CH), lambda i: (i, 0)), pl.BlockSpec(w.shape, lambda i: (0, 0))],
        out_shape=[jax.ShapeDtypeStruct((s, 2 * CONV_CH), BF16), jax.ShapeDtypeStruct(w.shape, F32)],
        scratch_shapes=[pltpu.VMEM((tm + CONV_HALO, CONV_CH), F32), pltpu.VMEM((tm + CONV_HALO, CONV_CH), F32)],
        compiler_params=_params("arbitrary"),
    )(proj, proj, dy, dy, w)


GATE_Z = slice(1536, 2560)
D_CH = 512
GELU_C = math.sqrt(2.0 / math.pi)
GELU_K = 0.044715


def _gelu_parts(z):
    t = jnp.tanh(GELU_C * (z + GELU_K * z * z * z))
    return 0.5 * z * (1.0 + t), t


def _lane_group(rows):
    return lax.broadcasted_iota(jnp.int32, (rows, D_CH), 1) // HEAD_DIM


def _tril_mask():
    return lax.broadcasted_iota(jnp.int32, (BLOCK, BLOCK), 0) >= lax.broadcasted_iota(jnp.int32, (BLOCK, BLOCK), 1)


def _layer_norm_parts(x):
    mu = jnp.mean(x, axis=-1, keepdims=True)
    xc = x - mu
    rstd = lax.rsqrt(jnp.mean(xc * xc, axis=-1, keepdims=True) + LN_EPS)
    return xc * rstd, rstd


def _gate_fwd(proj, ln_g, ln_b, w_sp, sb_t, name):
    tm = 512

    def body(p_ref, g_ref, b_ref, w_ref, sb_ref, mixed_ref, out_ref):
        zz, _ = _gelu_parts(p_ref[:, GATE_Z])
        u = zz[:, :D_CH]
        xh, _ = _layer_norm_parts(zz[:, D_CH:])
        gn = (xh * g_ref[...] + b_ref[...]).astype(BF16)
        grp = _lane_group(BLOCK)
        tri = _tril_mask()
        ws = [jnp.where(tri, w_ref[gi], 0.0).astype(BF16) for gi in range(N_GROUPS)]
        bias = jnp.zeros((BLOCK, D_CH), F32)
        for gi in range(N_GROUPS):
            bias = jnp.where(grp == gi, sb_ref[:, gi:gi + 1], bias)
        for ch in range(tm // BLOCK):
            rows = slice(ch * BLOCK, (ch + 1) * BLOCK)
            gc = gn[rows, :]
            mixed = bias
            for gi in range(N_GROUPS):
                r = jnp.dot(ws[gi], gc, preferred_element_type=F32)
                mixed = jnp.where(grp == gi, r + bias, mixed)
            mixed_ref[rows, :] = mixed
            out_ref[rows, :] = (u[rows, :] * mixed).astype(BF16)

    return _rows(body, name, tm, [proj], [ln_g, ln_b, w_sp, sb_t], [(D_CH, F32), (D_CH, BF16)])


def _gate_bwd(dmix, proj, mixed, ln_g, ln_b, w_sp, name):
    tm = 512

    def body(d_ref, p_ref, m_ref, g_ref, b_ref, w_ref, dz_ref, dg_ref, db_ref, dw_ref, dsb_ref, dgn_ref):
        @pl.when(_first_step())
        def _():
            dg_ref[...] = jnp.zeros_like(dg_ref)
            db_ref[...] = jnp.zeros_like(db_ref)
            dw_ref[...] = jnp.zeros_like(dw_ref)
            dsb_ref[...] = jnp.zeros_like(dsb_ref)

        z = p_ref[:, GATE_Z]
        zz, t = _gelu_parts(z)
        u = zz[:, :D_CH]
        xh, rstd = _layer_norm_parts(zz[:, D_CH:])
        g = g_ref[...]
        gn = (xh * g + b_ref[...]).astype(BF16)
        dd = d_ref[:, D_CH:]
        du = dd * m_ref[...]
        dm = dd * u
        grp = _lane_group(BLOCK)
        tri = _tril_mask()
        ws = [jnp.where(tri, w_ref[gi], 0.0).astype(BF16) for gi in range(N_GROUPS)]
        gsel = (lax.broadcasted_iota(jnp.int32, (N_GROUPS, D_CH), 1) // HEAD_DIM
                == lax.broadcasted_iota(jnp.int32, (N_GROUPS, D_CH), 0)).astype(F32)
        for ch in range(tm // BLOCK):
            rows = slice(ch * BLOCK, (ch + 1) * BLOCK)
            dmc = dm[rows, :]
            dmb = dmc.astype(BF16)
            gc = gn[rows, :]
            dgn = jnp.zeros((BLOCK, D_CH), F32)
            for gi in range(N_GROUPS):
                r = lax.dot_general(ws[gi], dmb, TN, preferred_element_type=F32)
                dgn = jnp.where(grp == gi, r, dgn)
                dmg = jnp.where(grp == gi, dmb, jnp.zeros_like(dmb))
                dwg = lax.dot_general(dmg, gc, NT, preferred_element_type=F32)
                dw_ref[gi] += jnp.where(tri, dwg, 0.0)
            dsb_ref[...] += lax.dot_general(gsel, dmc, NT, preferred_element_type=F32, precision=lax.Precision.HIGHEST)
            dgn_ref[rows, :] = dgn
        dgn = dgn_ref[...]
        db_ref[...] += jnp.sum(dgn, axis=0, keepdims=True)
        dg_ref[...] += jnp.sum(dgn * xh, axis=0, keepdims=True)
        dxh = dgn * g
        dgp = rstd * (dxh - jnp.mean(dxh, axis=-1, keepdims=True) - xh * jnp.mean(dxh * xh, axis=-1, keepdims=True))
        dgelu = 0.5 * (1.0 + t) + 0.5 * z * (1.0 - t * t) * GELU_C * (1.0 + 3.0 * GELU_K * z * z)
        dz_ref[:, 0:D_CH] = (du * dgelu[:, :D_CH]).astype(BF16)
        dz_ref[:, D_CH:] = (dgp * dgelu[:, D_CH:]).astype(BF16)

    s = proj.shape[0]
    tiled = [dmix, proj, mixed]
    consts = [ln_g, ln_b, w_sp]
    in_specs = [pl.BlockSpec((tm, a.shape[1]), lambda i: (i, 0)) for a in tiled]
    in_specs += [pl.BlockSpec(a.shape, lambda i, nd=a.ndim: (0,) * nd) for a in consts]
    vec = (1, D_CH)
    acc_shapes = [vec, vec, w_sp.shape, (N_GROUPS, BLOCK)]
    return pl.pallas_call(
        body, name=name, grid=(s // tm,), in_specs=in_specs,
        out_specs=[pl.BlockSpec((tm, 2 * D_CH), lambda i: (i, 0))]
        + [pl.BlockSpec(sh, lambda i, nd=len(sh): (0,) * nd) for sh in acc_shapes],
        out_shape=[jax.ShapeDtypeStruct((s, 2 * D_CH), BF16)] + [jax.ShapeDtypeStruct(sh, F32) for sh in acc_shapes],
        scratch_shapes=[pltpu.VMEM((tm, D_CH), F32)],
        compiler_params=_params("arbitrary"),
    )(*tiled, *consts)


def _adam_update(w, g, m, v):
    nm = ADAM_B1 * m + (1.0 - ADAM_B1) * g
    nv = ADAM_B2 * v + (1.0 - ADAM_B2) * (g * g)
    m_hat = nm / (1.0 - ADAM_B1 ** ADAM_STEP)
    v_hat = nv / (1.0 - ADAM_B2 ** ADAM_STEP)
    return -ADAM_LR * (m_hat / (jnp.sqrt(v_hat) + ADAM_EPS) + ADAM_WD * w), nm, nv


def _adamw(w, g, m, v, name):
    rows, cols = w.shape
    tm = _tile(rows, 512, 8)

    def body(w_ref, g_ref, m_ref, v_ref, d_ref, nm_ref, nv_ref):
        d_ref[...], nm_ref[...], nv_ref[...] = _adam_update(w_ref[...], g_ref[...], m_ref[...], v_ref[...])

    return _rows(body, name, tm, [w, g, m, v], [], [(cols, F32)] * 3)


def _adamw_t(w, g_t, m, v, name):
    layers, kdim, n = w.shape
    tr = 256

    def body(w_ref, g_ref, m_ref, v_ref, go_ref, d_ref, nm_ref, nv_ref):
        g = g_ref[0].T
        go_ref[0] = g
        d_ref[0], nm_ref[0], nv_ref[0] = _adam_update(w_ref[0], g, m_ref[0], v_ref[0])

    wspec = pl.BlockSpec((1, tr, n), lambda l, i: (l, i, 0))
    return pl.pallas_call(
        body, name=name, grid=(layers, kdim // tr),
        in_specs=[wspec, pl.BlockSpec((1, n, tr), lambda l, i: (l, 0, i)), wspec, wspec],
        out_specs=[wspec] * 4, out_shape=[jax.ShapeDtypeStruct(w.shape, F32)] * 4,
        compiler_params=_params("parallel", "parallel"),
    )(w, g_t, m, v)


def _ordered_sum(parts, name):
    n, rows, cols = parts.shape
    tm = _tile(rows, 512, 16 if parts.dtype == BF16 else 8)

    def body(p_ref, o_ref):
        acc = p_ref[0].astype(F32)
        for k in range(1, n):
            acc = acc + p_ref[k].astype(F32)
        o_ref[...] = acc

    return pl.pallas_call(body, name=name, grid=(rows // tm,),
                          in_specs=[pl.BlockSpec((n, tm, cols), lambda i: (0, i, 0))],
                          out_specs=pl.BlockSpec((tm, cols), lambda i: (i, 0)),
                          out_shape=jax.ShapeDtypeStruct((rows, cols), F32), compiler_params=_params("parallel"))(parts)


ANY = pl.BlockSpec(memory_space=pl.ANY)


def _position():
    x, y, c = lax.axis_index("x"), lax.axis_index("y"), lax.axis_index("c")
    other_chips = [(1 - x, y), (x, 1 - y), (1 - x, 1 - y)]
    return x, y, c, other_chips


def _remote(src, dst, send_sem, recv_sem, to):
    return pltpu.make_async_remote_copy(src_ref=src, dst_ref=dst, send_sem=send_sem, recv_sem=recv_sem,
                                        device_id=to, device_id_type=MESH)


STAGE_ROWS = 736


def _staged_copies(copies, buf, in_sems, out_sems):
    n = len(copies)

    def into(u):
        src = copies[u][0]
        return pltpu.make_async_copy(src, buf.at[u % 2, pl.ds(0, src.shape[0]), :], in_sems.at[u % 2])

    def out_of(u):
        dst = copies[u][1]
        return pltpu.make_async_copy(buf.at[u % 2, pl.ds(0, dst.shape[0]), :], dst, out_sems.at[u % 2])

    into(0).start()
    for u in range(n):
        into(u).wait()
        out_of(u).start()
        if u + 1 < n:
            if u >= 1:
                out_of(u - 1).wait()
            into(u + 1).start()
    if n >= 2:
        out_of(n - 2).wait()
    out_of(n - 1).wait()


def _stage_scratch(dtype, cols):
    return [pltpu.VMEM((2, STAGE_ROWS, cols), dtype), pltpu.SemaphoreType.DMA((2,)), pltpu.SemaphoreType.DMA((2,))]


def _row_chunks(rows):
    return [(r, min(STAGE_ROWS, rows - r)) for r in range(0, rows, STAGE_ROWS)]


def _gather_chips(shard, name):
    rows, cols = shard.shape
    half = rows // 2

    def body(in_ref, out_ref, send_sems, recv_sems, buf, in_sems, out_sems):
        x, y, c, chips = _position()
        me = 2 * x + y
        sibling = (x, y, 1 - c)

        def slab(chip, h):
            return out_ref.at[chip, pl.ds(h * half, half), :]

        first = [_remote(in_ref.at[pl.ds(c * half, half), :], slab(me, c), send_sems.at[j], recv_sems.at[j], (cx, cy, c))
                 for j, (cx, cy) in enumerate(chips)]
        for cp in first:
            cp.start()
        _staged_copies([(in_ref.at[pl.ds(r, n), :], out_ref.at[me, pl.ds(r, n), :]) for r, n in _row_chunks(rows)],
                       buf, in_sems, out_sems)
        passed = []
        for j, (cx, cy) in enumerate(chips):
            got = slab(2 * cx + cy, c)
            _remote(got, got, send_sems.at[j], recv_sems.at[j], sibling).wait_recv()
            cp = _remote(got, got, send_sems.at[3 + j], recv_sems.at[3 + j], sibling)
            cp.start()
            passed.append(cp)
        for j, (cx, cy) in enumerate(chips):
            got = slab(2 * cx + cy, 1 - c)
            _remote(got, got, send_sems.at[3 + j], recv_sems.at[3 + j], sibling).wait_recv()
        for cp in first + passed:
            cp.wait_send()

    return pl.pallas_call(
        body, name=name, in_specs=[ANY], out_specs=ANY,
        out_shape=jax.ShapeDtypeStruct((N_CHIPS, rows, cols), shard.dtype),
        scratch_shapes=[pltpu.SemaphoreType.DMA((6,)), pltpu.SemaphoreType.DMA((6,))] + _stage_scratch(shard.dtype, cols),
        compiler_params=pltpu.CompilerParams(vmem_limit_bytes=VMEM_LIMIT),
    )(shard)


HBM = pl.BlockSpec(memory_space=pltpu.HBM)
SEM = pl.BlockSpec(memory_space=pltpu.SEMAPHORE)
SIDE_EFFECT = pltpu.SideEffectType.DATAFLOW_SIDE_EFFECTING


def _ici_copies(in_ref, land_ref, send_sems, recv_sems, half):
    x, y, c, chips = _position()
    mine = pl.ds(c * half, half)
    sends = [_remote(in_ref.at[mine, :], land_ref.at[2 * x + y, mine, :], send_sems.at[j], recv_sems.at[j], (cx, cy, c))
             for j, (cx, cy) in enumerate(chips)]
    arrivals = [_remote(in_ref.at[mine, :], land_ref.at[2 * cx + cy, mine, :], send_sems.at[j], recv_sems.at[j], (cx, cy, c))
                for j, (cx, cy) in enumerate(chips)]
    return sends, arrivals


def _gather_start(shard, name):
    rows, cols = shard.shape

    def body(in_ref, land_ref, send_sems, recv_sems, in_thru, land_thru, token):
        sends, _ = _ici_copies(in_ref, land_ref, send_sems, recv_sems, rows // 2)
        for cp in sends:
            cp.start()
        token[...] = jnp.zeros_like(token)

    land = lax.empty((N_CHIPS, rows, cols), shard.dtype)
    return pl.pallas_call(
        body, name=name,
        out_shape=(pltpu.SemaphoreType.DMA((3,)), pltpu.SemaphoreType.DMA((3,)), pltpu.HBM(shard.shape, shard.dtype),
                   pltpu.HBM(land.shape, land.dtype), jax.ShapeDtypeStruct((8, LANES), F32)),
        in_specs=(HBM, HBM), out_specs=(SEM, SEM, HBM, HBM, pl.BlockSpec(memory_space=pltpu.VMEM)),
        input_output_aliases={0: 2, 1: 3},
        compiler_params=pltpu.CompilerParams(has_side_effects=SIDE_EFFECT),
    )(pltpu.with_memory_space_constraint(shard, pltpu.HBM), pltpu.with_memory_space_constraint(land, pltpu.HBM))


def _gather_wait(send_sems, recv_sems, shard, land, after, name):
    rows = shard.shape[0]

    def body(in_ref, land_ref, send_sems, recv_sems, after_ref, in_out, land_out):
        sends, arrivals = _ici_copies(in_ref, land_ref, send_sems, recv_sems, rows // 2)
        for cp in sends:
            cp.wait_send()
        for cp in arrivals:
            cp.wait_recv()

    return pl.pallas_call(
        body, name=name, out_shape=(pltpu.HBM(shard.shape, shard.dtype), pltpu.HBM(land.shape, land.dtype)),
        in_specs=(HBM, HBM, SEM, SEM, ANY), out_specs=(HBM, HBM), input_output_aliases={0: 0, 1: 1},
        compiler_params=pltpu.CompilerParams(has_side_effects=SIDE_EFFECT),
    )(shard, land, send_sems, recv_sems, after)


def _gather_finish(shard, land, name):
    rows, cols = shard.shape
    half = rows // 2

    def body(in_ref, land_ref, out_ref, send_sems, recv_sems, buf, in_sems, out_sems):
        x, y, c, chips = _position()
        me = 2 * x + y
        sibling = (x, y, 1 - c)

        def slab(chip, h):
            return out_ref.at[chip, pl.ds(h * half, half), :]

        passed = [_remote(slab(2 * cx + cy, c), slab(2 * cx + cy, c), send_sems.at[j], recv_sems.at[j], sibling)
                  for j, (cx, cy) in enumerate(chips)]
        for cp in passed:
            cp.start()
        _staged_copies([(in_ref.at[pl.ds(r, n), :], out_ref.at[me, pl.ds(r, n), :]) for r, n in _row_chunks(rows)],
                       buf, in_sems, out_sems)
        for j, (cx, cy) in enumerate(chips):
            got = slab(2 * cx + cy, 1 - c)
            _remote(got, got, send_sems.at[j], recv_sems.at[j], sibling).wait_recv()
        for cp in passed:
            cp.wait_send()

    return pl.pallas_call(
        body, name=name, in_specs=[ANY, ANY], out_specs=ANY, out_shape=jax.ShapeDtypeStruct(land.shape, land.dtype),
        input_output_aliases={1: 0},
        scratch_shapes=[pltpu.SemaphoreType.DMA((3,)), pltpu.SemaphoreType.DMA((3,))] + _stage_scratch(shard.dtype, cols),
        compiler_params=pltpu.CompilerParams(vmem_limit_bytes=VMEM_LIMIT),
    )(shard, land)


def _gather_devices(block, name):
    rows, cols = block.shape

    def body(in_ref, out_ref, send_sems, recv_sems, local_sem):
        x, y, c, chips = _position()
        sibling = (x, y, 1 - c)

        def slot(px, py, pc):
            return out_ref.at[4 * px + 2 * py + pc]

        mine = pltpu.make_async_copy(in_ref, slot(x, y, c), local_sem)
        mine.start()
        first = [_remote(in_ref, slot(x, y, c), send_sems.at[0], recv_sems.at[0], sibling)]
        first += [_remote(in_ref, slot(x, y, c), send_sems.at[1 + j], recv_sems.at[1 + j], (cx, cy, c))
                  for j, (cx, cy) in enumerate(chips)]
        for cp in first:
            cp.start()
        passed = []
        for j, (cx, cy) in enumerate(chips):
            got = slot(cx, cy, c)
            _remote(got, got, send_sems.at[1 + j], recv_sems.at[1 + j], sibling).wait_recv()
            cp = _remote(got, got, send_sems.at[4 + j], recv_sems.at[4 + j], sibling)
            cp.start()
            passed.append(cp)
        got = slot(x, y, 1 - c)
        _remote(got, got, send_sems.at[0], recv_sems.at[0], sibling).wait_recv()
        for j, (cx, cy) in enumerate(chips):
            got = slot(cx, cy, 1 - c)
            _remote(got, got, send_sems.at[4 + j], recv_sems.at[4 + j], sibling).wait_recv()
        for cp in first + passed:
            cp.wait_send()
        mine.wait()

    return pl.pallas_call(
        body, name=name, in_specs=[ANY], out_specs=ANY,
        out_shape=jax.ShapeDtypeStruct((N_DEV, rows, cols), block.dtype),
        scratch_shapes=[pltpu.SemaphoreType.DMA((7,)), pltpu.SemaphoreType.DMA((7,)), pltpu.SemaphoreType.DMA],
    )(block)


def _pair_send(grads, name):
    n = len(grads)
    hs = [g.shape[2] for g in grads]
    offs = [sum(hs[:i]) for i in range(n)]
    cols = grads[0].shape[3]

    def body(*refs):
        g_refs = refs[:n]
        got_ref, send_sems, recv_sems = refs[n:]
        x, y, c, _ = _position()
        copies = [_remote(g_ref.at[:, 1 - c], got_ref.at[:, pl.ds(offs[i], hs[i]), :], send_sems.at[i], recv_sems.at[i],
                          (x, y, 1 - c)) for i, g_ref in enumerate(g_refs)]
        for cp in copies:
            cp.start()
        for cp in copies:
            cp.wait()

    return pl.pallas_call(
        body, name=name, in_specs=[ANY] * n, out_specs=ANY, out_shape=jax.ShapeDtypeStruct((N_CHIPS, sum(hs), cols), F32),
        scratch_shapes=[pltpu.SemaphoreType.DMA((n,)), pltpu.SemaphoreType.DMA((n,))],
    )(*grads)


def _pair_add(grads, got, name):
    n = len(grads)
    hs = [g.shape[2] for g in grads]
    offs = [sum(hs[:i]) for i in range(n)]
    cols = grads[0].shape[3]
    hmax = max(hs)
    units = [(i, k) for k in range(N_CHIPS) for i in range(n)]

    def body(*refs):
        g_refs = refs[:n]
        got_ref, out_ref, a_buf, b_buf, o_buf, a_sems, b_sems, o_sems = refs[n:]
        c = lax.axis_index("c")

        def loads(u):
            i, k = units[u]
            slot, rows = u % 2, pl.ds(0, hs[i])
            return (pltpu.make_async_copy(g_refs[i].at[k, c], a_buf.at[slot, rows, :], a_sems.at[slot]),
                    pltpu.make_async_copy(got_ref.at[k, pl.ds(offs[i], hs[i]), :], b_buf.at[slot, rows, :], b_sems.at[slot]))

        def store(u):
            i, k = units[u]
            return pltpu.make_async_copy(o_buf.at[u % 2, pl.ds(0, hs[i]), :], out_ref.at[k, pl.ds(offs[i], hs[i]), :],
                                         o_sems.at[u % 2])

        for cp in loads(0):
            cp.start()
        for u, (i, k) in enumerate(units):
            if u + 1 < len(units):
                for cp in loads(u + 1):
                    cp.start()
            for cp in loads(u):
                cp.wait()
            if u >= 2:
                store(u - 2).wait()
            rows = pl.ds(0, hs[i])
            o_buf[u % 2, rows, :] = (a_buf[u % 2, rows, :] + b_buf[u % 2, rows, :]).astype(BF16)
            store(u).start()
        store(len(units) - 2).wait()
        store(len(units) - 1).wait()

    return pl.pallas_call(
        body, name=name, in_specs=[ANY] * (n + 1), out_specs=ANY,
        out_shape=jax.ShapeDtypeStruct((N_CHIPS, sum(hs), cols), BF16),
        scratch_shapes=[pltpu.VMEM((2, hmax, cols), F32), pltpu.VMEM((2, hmax, cols), F32), pltpu.VMEM((2, hmax, cols), BF16),
                        pltpu.SemaphoreType.DMA((2,)), pltpu.SemaphoreType.DMA((2,)), pltpu.SemaphoreType.DMA((2,))],
        compiler_params=pltpu.CompilerParams(vmem_limit_bytes=VMEM_LIMIT),
    )(*grads, got)


def _chip_exchange(parts, name):
    _, rows, cols = parts.shape

    def body(in_ref, out_ref, send_sems, recv_sems):
        x, y, c, chips = _position()
        sent = [_remote(in_ref.at[2 * cx + cy], out_ref.at[j], send_sems.at[j], recv_sems.at[j], (cx, cy, c))
                for j, (cx, cy) in enumerate(chips)]
        for cp in sent:
            cp.start()
        for cp in sent:
            cp.wait()

    return pl.pallas_call(
        body, name=name, in_specs=[ANY], out_specs=ANY, out_shape=jax.ShapeDtypeStruct((3, rows, cols), parts.dtype),
        scratch_shapes=[pltpu.SemaphoreType.DMA((3,)), pltpu.SemaphoreType.DMA((3,))],
    )(parts)


def _chip_sum(parts, recv, chip, name):
    _, rows, cols = parts.shape
    tm = _tile(rows, 512, 16)

    def body(chip_ref, own_ref, recv_ref, o_ref):
        acc = own_ref[0].astype(F32)
        for j in range(3):
            acc = acc + recv_ref[j].astype(F32)
        o_ref[...] = acc

    return pl.pallas_call(
        body, name=name,
        grid_spec=pltpu.PrefetchScalarGridSpec(
            num_scalar_prefetch=1, grid=(rows // tm,),
            in_specs=[pl.BlockSpec((1, tm, cols), lambda i, chip_ref: (chip_ref[0], i, 0)),
                      pl.BlockSpec((3, tm, cols), lambda i, chip_ref: (0, i, 0))],
            out_specs=pl.BlockSpec((tm, cols), lambda i, chip_ref: (i, 0))),
        out_shape=jax.ShapeDtypeStruct((rows, cols), F32), compiler_params=_params("parallel"),
    )(chip, parts, recv)


def _join_unpack(mine, hs, name):
    n = len(hs)
    offs = [sum(hs[:i]) for i in range(n)]
    cols = mine.shape[1]

    def body(in_ref, *refs):
        outs = refs[:n]
        send_sems, recv_sems, buf, in_sems, out_sems = refs[n:]
        x, y, c, _ = _position()
        sibling = (x, y, 1 - c)
        sent, local = [], []
        for i, o_ref in enumerate(outs):
            src = in_ref.at[pl.ds(offs[i], hs[i]), :]
            here = o_ref.at[pl.ds(c * hs[i], hs[i]), :]
            cp = _remote(src, here, send_sems.at[i], recv_sems.at[i], sibling)
            cp.start()
            sent.append(cp)
            local.append((src, here))
        _staged_copies(local, buf, in_sems, out_sems)
        for i, (cp, o_ref) in enumerate(zip(sent, outs)):
            there = o_ref.at[pl.ds((1 - c) * hs[i], hs[i]), :]
            _remote(there, there, send_sems.at[i], recv_sems.at[i], sibling).wait_recv()
            cp.wait_send()

    assert max(hs) <= STAGE_ROWS
    return pl.pallas_call(
        body, name=name, in_specs=[ANY], out_specs=[ANY] * n,
        out_shape=[jax.ShapeDtypeStruct((2 * h, cols), F32) for h in hs],
        scratch_shapes=[pltpu.SemaphoreType.DMA((n,)), pltpu.SemaphoreType.DMA((n,))] + _stage_scratch(F32, cols),
        compiler_params=pltpu.CompilerParams(vmem_limit_bytes=VMEM_LIMIT),
    )(mine)


SMALL_ROWS = 16


def _small_rows(n):
    return -(-n // (SMALL_ROWS * LANES)) * SMALL_ROWS


def _pack_small(arrs):
    parts = []
    for a in arrs:
        flat = a.reshape(-1)
        rows = _small_rows(flat.shape[0])
        flat = jnp.pad(flat, (0, rows * LANES - flat.shape[0]))
        parts.append(flat.reshape(rows, LANES))
    return jnp.concatenate(parts, axis=0)


def _unpack_small(packed, shapes):
    out, r = [], 0
    for sh in shapes:
        n = math.prod(sh)
        cnt = _small_rows(n)
        out.append(packed[r:r + cnt].reshape(-1)[:n].reshape(sh))
        r += cnt
    return out


def _ffn_fwd(h, g_norm, w_gate_t, w_up_t, w_down, tag):
    n = _rms_fwd(h, g_norm, f"{tag}_norm")
    act, gate, up = _ffn_gate_up(n, w_gate_t, w_up_t, f"{tag}_gate_up")
    out = _matmul(act, w_down, add=h, name=f"{tag}_down")
    return out, (n, gate, up, act)


def _ffn_bwd(dh, dhb, h_in, saved, g_norm, w_gate_t, w_up_t, w_down, tag):
    n, gate, up, act = saved
    dgate, dup = _ffn_dact(dhb, w_down, gate, up, f"{tag}_dact")
    dw_down = _matmul(act, dhb, trans_a=True, name=f"{tag}_dwdown")
    dw_gate_t = _matmul(dgate, n, trans_a=True, name=f"{tag}_dwgate")
    dw_up_t = _matmul(dup, n, trans_a=True, name=f"{tag}_dwup")
    dh_in, dh_inb, dg = _dn_norm([(dgate, w_gate_t), (dup, w_up_t)], h_in, g_norm, dh, f"{tag}_dnorm")
    return dh_in, dh_inb, dg, dw_gate_t, dw_up_t, dw_down


def _local_step(x, tgt, w, big, late_weights):
    s = x.shape[0]
    tabs = _rope_tables(s)
    grads, gbig = {}, {}

    g_ev = w['ev_norm_g']
    n1 = _rms_fwd(x, g_ev, "ev_norm")
    proj0 = _matmul(n1, big['ev_w_in', 0], trans_b=True, name="ev_in")
    q0, k0, v0 = _qkv_prep_even(proj0, tabs, "ev_qkv")
    sinks = w['ev_sinks'].reshape(-1)
    o0, lse0, o0b = _attn_fwd(q0, k0, v0, sinks, max_dist=BLOCK - 1, name="ev_attn", emit_bf16=True)
    yconv, cout = _conv_fwd(proj0, w['ev_conv_w'][0], w['ev_conv_b'], w['ev_conv_ln_g'], w['ev_conv_ln_b'], "ev_conv")
    mix0 = (o0b[0], cout)
    h1 = _matmul(mix0, big['ev_w_out', 0], add=x, name="ev_out")
    big = {**big, **late_weights(h1)}

    g_f0 = w['ffn_norm_g'][0:1]
    h2, ffn0 = _ffn_fwd(h1, g_f0, big['ffn_w_gate', 0], big['ffn_w_up', 0], big['ffn_w_down', 0], "ffn0")

    g_od = w['od_norm_g']
    n3 = _rms_fwd(h2, g_od, "od_norm")
    proj1 = _matmul(n3, big['od_w_in', 0], trans_b=True, name="od_in")
    qkv = _qkv_prep_odd(proj1, tabs, "od_qkv")
    nb = len(DILATIONS)
    outs, lses = [], []
    for i, d in enumerate(DILATIONS):
        o_r, lse_r = _attn_fwd(qkv[i], qkv[nb + i], qkv[2 * nb + i], None, max_dist=BLOCK, name=f"od_attn{d}")
        outs.append(o_r)
        lses.append(lse_r)
    comb = _combine(outs, lses, "od_combine")
    c_bf16 = comb[0]
    c_fold = {1: comb[1]}
    lse_fold = {1: comb[2]}
    for i, d in enumerate(DILATIONS[1:]):
        c_fold[d], lse_fold[d] = comb[3 + 2 * i], comb[4 + 2 * i]
    w_sp = w['od_spatial_w'][0]
    sb_t = w['od_spatial_b'][0].T
    mixed, dout = _gate_fwd(proj1, w['od_sgu_ln_g'], w['od_sgu_ln_b'], w_sp, sb_t, "od_gate")
    mix1 = (c_bf16, dout)
    h3 = _matmul(mix1, big['od_w_out', 0], add=h2, name="od_out")

    g_f1 = w['ffn_norm_g'][1:2]
    h4, ffn1 = _ffn_fwd(h3, g_f1, big['ffn_w_gate', 1], big['ffn_w_up', 1], big['ffn_w_down', 1], "ffn1")

    dh4, dh4b, dg_final, loss_tile = _final_loss(h4, w['final_norm_g'].reshape(1, D_MODEL), tgt, "final")
    grads['final_norm_g'] = dg_final.reshape(D_MODEL)

    dh3, dh3b, dg_f1, gbig['ffn_w_gate', 1], gbig['ffn_w_up', 1], gbig['ffn_w_down', 1] = _ffn_bwd(
        dh4, dh4b, h3, ffn1, g_f1, big['ffn_w_gate', 1], big['ffn_w_up', 1], big['ffn_w_down', 1], "ffn1")

    dmix1 = _matmul(dh3b, big['od_w_out', 0], trans_b=True, name="od_dmix")
    gbig['od_w_out', 0] = _matmul_tn_pair(mix1[0], mix1[1], dh3b, "od_dwout")
    do_fold = dict(zip(DILATIONS[1:], _fold_dout(dmix1, "od_fold_dout")))
    do_fold[1] = dmix1[None]
    dqs, dks, dvs = [], [], []
    for i, d in enumerate(DILATIONS):
        dq_r, dk_r, dv_r = _attn_bwd(qkv[i], qkv[nb + i], qkv[2 * nb + i], do_fold[d], c_fold[d], lse_fold[d], None,
                                     max_dist=BLOCK, name=f"od_dattn{d}")
        dqs.append(dq_r)
        dks.append(dk_r)
        dvs.append(dv_r)
    dz, dg_sgu, db_sgu, dw_sp, dsb = _gate_bwd(dmix1, proj1, mixed, w['od_sgu_ln_g'], w['od_sgu_ln_b'], w_sp, "od_dgate")
    grads['od_sgu_ln_g'], grads['od_sgu_ln_b'] = dg_sgu, db_sgu
    grads['od_spatial_w'], grads['od_spatial_b'] = dw_sp[None], dsb[None]
    dproj1 = _qkv_post_odd(dqs, dks, dvs, dz, tabs, "od_dproj")
    gbig['od_w_in', 0] = _matmul(dproj1, n3, trans_a=True, name="od_dwin")
    dh2, dh2b, dg_od = _dn_norm([(dproj1, big['od_w_in', 0])], h2, g_od, dh3, "od_dnorm")
    grads['od_norm_g'] = dg_od

    dh1, dh1b, dg_f0, gbig['ffn_w_gate', 0], gbig['ffn_w_up', 0], gbig['ffn_w_down', 0] = _ffn_bwd(
        dh2, dh2b, h1, ffn0, g_f0, big['ffn_w_gate', 0], big['ffn_w_up', 0], big['ffn_w_down', 0], "ffn0")
    grads['ffn_norm_g'] = jnp.concatenate([dg_f0, dg_f1], axis=0)

    dmix0 = _matmul(dh1b, big['ev_w_out', 0], trans_b=True, name="ev_dmix")
    gbig['ev_w_out', 0] = _matmul_tn_pair(mix0[0], mix0[1], dh1b, "ev_dwout")
    dq0, dk0, dv0, dsink = _attn_bwd(q0, k0, v0, dmix0[None], o0, lse0, sinks, max_dist=BLOCK - 1, name="ev_dattn")
    grads['ev_sinks'] = dsink[:, 0, :].reshape(N_PAIRS, 2, HEAD_DIM)[:, :, 0].reshape(1, 8)
    dyc, dg_cln, db_cln, dcb = _conv_tail_bwd(dmix0, yconv, w['ev_conv_ln_g'], w['ev_conv_ln_b'], "ev_dconv_tail")
    grads['ev_conv_ln_g'], grads['ev_conv_ln_b'], grads['ev_conv_b'] = dg_cln, db_cln, dcb
    dglu, dconv_w = _conv_bwd(proj0, dyc, w['ev_conv_w'][0], "ev_dconv")
    grads['ev_conv_w'] = dconv_w[None]
    dproj0 = _qkv_post_even(dq0, dk0, dv0, dglu, tabs, "ev_dproj")
    gbig['ev_w_in', 0] = _matmul(dproj0, n1, trans_a=True, name="ev_dwin")
    dx, _, dg_ev = _dn_norm([(dproj0, big['ev_w_in', 0])], x, g_ev, dh1, "ev_dnorm")
    grads['ev_norm_g'] = dg_ev
    return loss_tile, dx, grads, gbig


def _shard_rows(w, layer, by_cols):
    return w[layer].T if by_cols else w[layer]


def kernel(x, ev_norm_g, ev_w_in, ev_sinks, ev_conv_w, ev_conv_b, ev_conv_ln_g, ev_conv_ln_b, ev_w_out, od_norm_g, od_w_in, od_sgu_ln_g, od_sgu_ln_b, od_spatial_w, od_spatial_b, od_w_out, ffn_norm_g, ffn_w_gate, ffn_w_up, ffn_w_down, final_norm_g, loss_target, m_ev_norm_g, m_ev_w_in, m_ev_sinks, m_ev_conv_w, m_ev_conv_b, m_ev_conv_ln_g, m_ev_conv_ln_b, m_ev_w_out, m_od_norm_g, m_od_w_in, m_od_sgu_ln_g, m_od_sgu_ln_b, m_od_spatial_w, m_od_spatial_b, m_od_w_out, m_ffn_norm_g, m_ffn_w_gate, m_ffn_w_up, m_ffn_w_down, m_final_norm_g, v_ev_norm_g, v_ev_w_in, v_ev_sinks, v_ev_conv_w, v_ev_conv_b, v_ev_conv_ln_g, v_ev_conv_ln_b, v_ev_w_out, v_od_norm_g, v_od_w_in, v_od_sgu_ln_g, v_od_sgu_ln_b, v_od_spatial_w, v_od_spatial_b, v_od_w_out, v_ffn_norm_g, v_ffn_w_gate, v_ffn_w_up, v_ffn_w_down, v_final_norm_g):
    given = dict(locals())
    wts = {n: given[n] for n in WEIGHTS}
    mom = {n: given["m_" + n] for n in WEIGHTS}
    var = {n: given["v_" + n] for n in WEIGHTS}
    chip = 2 * lax.axis_index("x") + lax.axis_index("y")

    shard_rows = [_shard_rows(wts[n], layer, by_cols).astype(BF16) for n, layer, by_cols in BIG]
    counts = [a.shape[0] for a in shard_rows]
    n_first = sum(n.startswith('ev_') for n, _, _ in BIG)

    def unpack(stacked, entries, cnts):
        out, r = {}, 0
        for (n, layer, _), cnt in zip(entries, cnts):
            out[n, layer] = stacked[:, r:r + cnt].reshape(N_CHIPS * cnt, D_MODEL)
            r += cnt
        return out

    big = unpack(_gather_chips(jnp.concatenate(shard_rows[:n_first], axis=0), "gather_weights_ev"),
                 BIG[:n_first], counts[:n_first])
    send_sems, recv_sems, late_shard, late_land, token = _gather_start(jnp.concatenate(shard_rows[n_first:], axis=0),
                                                                      "gather_weights_start")

    def late_weights(after):
        shard, land = _gather_wait(send_sems, recv_sems, late_shard, late_land, after, "gather_weights_wait")
        return unpack(_gather_finish(shard, land, "gather_weights_finish"), BIG[n_first:], counts[n_first:])

    full = {n: wts[n] for n in SMALL_REPL}
    full['ev_norm_g'] = full['ev_norm_g'] + token[0:1, 0:1]
    small_shards = [wts[n] for n in SMALL_SHARDED]
    small_shapes = [a.shape for a in small_shards]
    all_s = _gather_chips(_pack_small(small_shards), "gather_small_weights")
    per_chip = [_unpack_small(all_s[k], small_shapes) for k in range(N_CHIPS)]
    for i, n in enumerate(SMALL_SHARDED):
        full[n] = jnp.concatenate([per_chip[k][i] for k in range(N_CHIPS)], axis=-1)

    loss_tile, grad_x, grads, gbig = _local_step(x[0], loss_target[0], full, big, late_weights)
    loss = lax.psum(loss_tile[0, 0], ("x", "y", "c"))

    halves = [cnt // 2 for cnt in counts]
    split = [gbig[n, layer].reshape(N_CHIPS, 2, h, D_MODEL) for (n, layer, _), h in zip(BIG, halves)]
    got = _pair_send(split, "grad_pair_send")
    chip_part = _pair_add(split, got, "grad_pair_add")
    from_chips = _chip_exchange(chip_part, "grad_chip_exchange")
    my_half = _chip_sum(chip_part, from_chips, chip.reshape(1), "grad_chip_sum")
    reduced = dict(zip([(n, layer) for n, layer, _ in BIG], _join_unpack(my_half, halves, "grad_join_halves")))

    small_names = SMALL_REPL + SMALL_SHARDED
    small_full_shapes = [grads[n].shape for n in small_names]
    spack = _pack_small([grads[n] for n in small_names])
    s_all = _gather_devices(spack, "grad_small_gather")
    s_sum = _unpack_small(_ordered_sum(s_all, "grad_small_sum"), small_full_shapes)
    g_all = dict(zip(small_names, s_sum))
    for n in SMALL_SHARDED:
        width = wts[n].shape[-1]
        g_all[n] = lax.dynamic_slice_in_dim(g_all[n], chip * width, width, axis=g_all[n].ndim - 1)

    delta, new_m, new_v = {}, {}, {}
    for n in BIG_NAMES:
        layers = [layer for nn, layer, _ in BIG if nn == n]
        by_cols = [bc for nn, _, bc in BIG if nn == n][0]
        g_rows = jnp.stack([reduced[n, layer] for layer in layers])
        if by_cols:
            g_all[n], delta[n], new_m[n], new_v[n] = _adamw_t(wts[n], g_rows, mom[n], var[n], f"adamw_{n}")
        else:
            shape = wts[n].shape
            flat = [a.reshape(-1, D_MODEL) for a in (wts[n], g_rows, mom[n], var[n])]
            g_all[n] = g_rows
            delta[n], new_m[n], new_v[n] = (a.reshape(shape) for a in _adamw(*flat, f"adamw_{n}"))
    shapes = [wts[n].shape for n in small_names]
    d_s, m_s, v_s = _adamw(*[_pack_small([src[n] for n in small_names]) for src in (wts, g_all, mom, var)], "adamw_small")
    for dst, packed in ((delta, d_s), (new_m, m_s), (new_v, v_s)):
        dst.update(zip(small_names, _unpack_small(packed, shapes)))

    return (loss, grad_x[None], *[g_all[n] for n in WEIGHTS], *[delta[n] for n in WEIGHTS],
            *[new_m[n] for n in WEIGHTS], *[new_v[n] for n in WEIGHTS])
```

```python
import math

import jax
import jax.numpy as jnp
from jax import lax
from jax.experimental import pallas as pl
from jax.experimental.pallas import tpu as pltpu

F32 = jnp.float32
BF16 = jnp.bfloat16

D_MODEL = 1024
HEAD_DIM = 64
ROT_DIM = 16
ROPE_THETA = 500000.0
RMS_EPS = 1e-6
LN_EPS = 1e-5
BLOCK = 128
CONV_WIDTH = 31
CONV_HALO = 32
CONV_ROWS = 64
D_FF = 2816
N_GROUPS = 8
ATTN_W = 512
ATTN_SCALE = HEAD_DIM ** -0.5
NEG = -1e30
DILATIONS = (1, 4, 16)

ADAM_LR = 0.001
ADAM_B1 = 0.9
ADAM_B2 = 0.999
ADAM_EPS = 1e-08
ADAM_WD = 0.01
ADAM_STEP = 10

LANES = 128
N_PAIRS = ATTN_W // LANES
VMEM_LIMIT = 56 * 1024 * 1024
MESH = pl.DeviceIdType.MESH
N_CHIPS = 4
N_DEV = 8

WEIGHTS = ['ev_norm_g', 'ev_w_in', 'ev_sinks', 'ev_conv_w', 'ev_conv_b', 'ev_conv_ln_g', 'ev_conv_ln_b', 'ev_w_out',
           'od_norm_g', 'od_w_in', 'od_sgu_ln_g', 'od_sgu_ln_b', 'od_spatial_w', 'od_spatial_b', 'od_w_out',
           'ffn_norm_g', 'ffn_w_gate', 'ffn_w_up', 'ffn_w_down', 'final_norm_g']
BIG = [('ev_w_in', 0, True), ('ev_w_out', 0, False), ('od_w_in', 0, True), ('od_w_out', 0, False),
       ('ffn_w_gate', 0, True), ('ffn_w_gate', 1, True), ('ffn_w_up', 0, True), ('ffn_w_up', 1, True),
       ('ffn_w_down', 0, False), ('ffn_w_down', 1, False)]
BIG_NAMES = ['ev_w_in', 'ev_w_out', 'od_w_in', 'od_w_out', 'ffn_w_gate', 'ffn_w_up', 'ffn_w_down']
SMALL_SHARDED = ['ev_conv_w', 'od_norm_g', 'od_sgu_ln_g', 'od_sgu_ln_b']
SMALL_REPL = ['ev_norm_g', 'ev_sinks', 'ev_conv_b', 'ev_conv_ln_g', 'ev_conv_ln_b', 'od_spatial_w', 'od_spatial_b',
              'ffn_norm_g', 'final_norm_g']


def _tile(n, cap, mult=LANES):
    best = None
    for t in range(mult, min(n, cap) + 1, mult):
        if n % t == 0:
            best = t
    assert best is not None, (n, cap)
    return best


def _params(*sem):
    return pltpu.CompilerParams(dimension_semantics=sem, vmem_limit_bytes=VMEM_LIMIT)


def _sigmoid(x):
    return 1.0 / (1.0 + jnp.exp(-x))


def _pair_block(p):
    return slice(p * LANES, (p + 1) * LANES)


def _matmul(a, b, *, name, trans_a=False, trans_b=False, add=None, out_dtype=F32):
    parts = a if isinstance(a, (tuple, list)) else (a,)
    if trans_a:
        k, m = parts[0].shape
    else:
        m = parts[0].shape[0]
        k = sum(p.shape[1] for p in parts)
    if trans_b:
        n, k2 = b.shape
    else:
        k2, n = b.shape
    assert k == k2 and b.dtype == BF16 and all(p.dtype == BF16 for p in parts)
    tm = _tile(m, D_FF // 2 if trans_a else 512)
    tn = _tile(n, D_FF // 2)
    tk = k if k <= D_FF else _tile(k, 1024)
    nk = k // tk
    na = len(parts)
    assert na == 1 or (nk == 1 and not trans_a)
    dims = (((0 if trans_a else 1,), (1 if trans_b else 0,)), ((), ()))
    has_add = add is not None

    def body(*refs):
        a_refs, b_ref = refs[:na], refs[na]
        add_ref = refs[na + 1] if has_add else None
        o_ref = refs[na + 1 + has_add]
        a_val = a_refs[0][...] if na == 1 else jnp.concatenate([r[...] for r in a_refs], axis=1)
        part = lax.dot_general(a_val, b_ref[...], dims, preferred_element_type=F32)
        if nk == 1:
            if has_add:
                part = part + add_ref[...]
            o_ref[...] = part.astype(o_ref.dtype)
            return
        acc_ref = refs[-1]
        kk = pl.program_id(2)

        @pl.when(kk == 0)
        def _():
            acc_ref[...] = part

        @pl.when(kk > 0)
        def _():
            acc_ref[...] += part

        @pl.when(kk == nk - 1)
        def _():
            res = acc_ref[...]
            if has_add:
                res = res + add_ref[...]
            o_ref[...] = res.astype(o_ref.dtype)

    if trans_a:
        a_specs = [pl.BlockSpec((tk, tm), lambda i, j, kk: (kk, i))]
    elif na == 1:
        a_specs = [pl.BlockSpec((tm, tk), lambda i, j, kk: (i, kk))]
    else:
        a_specs = [pl.BlockSpec((tm, p.shape[1]), lambda i, j, kk: (i, 0)) for p in parts]
    b_spec = pl.BlockSpec((tn, tk), lambda i, j, kk: (j, kk)) if trans_b else pl.BlockSpec((tk, tn), lambda i, j, kk: (kk, j))
    o_spec = pl.BlockSpec((tm, tn), lambda i, j, kk: (i, j))
    in_specs = a_specs + [b_spec] + ([o_spec] if has_add else [])
    operands = list(parts) + [b] + ([add] if has_add else [])
    return pl.pallas_call(
        body, name=name, grid=(m // tm, n // tn, nk), in_specs=in_specs, out_specs=o_spec,
        out_shape=jax.ShapeDtypeStruct((m, n), out_dtype),
        scratch_shapes=[pltpu.VMEM((tm, tn), F32)] if nk > 1 else [],
        compiler_params=_params("parallel", "parallel", "arbitrary"),
    )(*operands)


def _matmul_tn_pair(a1, a2, b, name):
    kdim, m1 = a1.shape
    m2 = a2.shape[1]
    n = b.shape[1]
    tn = _tile(n, 1024)
    tk = _tile(kdim, 1024)
    nk = kdim // tk
    dims = (((0,), (0,)), ((), ()))

    def body(a1_ref, a2_ref, b_ref, o_ref):
        kk = pl.program_id(1)
        bv = b_ref[...]
        top = lax.dot_general(a1_ref[...], bv, dims, preferred_element_type=F32)
        bot = lax.dot_general(a2_ref[...], bv, dims, preferred_element_type=F32)

        @pl.when(kk == 0)
        def _():
            o_ref[0:m1, :] = top
            o_ref[m1:, :] = bot

        @pl.when(kk > 0)
        def _():
            o_ref[0:m1, :] += top
            o_ref[m1:, :] += bot

    return pl.pallas_call(
        body, name=name, grid=(n // tn, nk),
        in_specs=[pl.BlockSpec((tk, m1), lambda j, kk: (kk, 0)), pl.BlockSpec((tk, m2), lambda j, kk: (kk, 0)),
                  pl.BlockSpec((tk, tn), lambda j, kk: (kk, j))],
        out_specs=pl.BlockSpec((m1 + m2, tn), lambda j, kk: (0, j)),
        out_shape=jax.ShapeDtypeStruct((m1 + m2, n), F32),
        compiler_params=_params("parallel", "arbitrary"),
    )(a1, a2, b)


def _ffn_gate_up(n, w_gate_t, w_up_t, name):
    m, k = n.shape
    f = w_gate_t.shape[0]
    tm, tn = _tile(m, 512), _tile(f, D_FF // 2)

    def body(n_ref, wg_ref, wu_ref, act_ref, gate_ref, up_ref):
        a = n_ref[...]
        gate = lax.dot_general(a, wg_ref[...], NT, preferred_element_type=F32)
        up = lax.dot_general(a, wu_ref[...], NT, preferred_element_type=F32)
        act_ref[...] = (gate * _sigmoid(gate) * up).astype(BF16)
        gate_ref[...] = gate.astype(BF16)
        up_ref[...] = up.astype(BF16)

    wspec = pl.BlockSpec((tn, k), lambda i, j: (j, 0))
    ospec = pl.BlockSpec((tm, tn), lambda i, j: (i, j))
    return pl.pallas_call(
        body, name=name, grid=(m // tm, f // tn), in_specs=[pl.BlockSpec((tm, k), lambda i, j: (i, 0)), wspec, wspec],
        out_specs=[ospec] * 3, out_shape=[jax.ShapeDtypeStruct((m, f), BF16)] * 3,
        compiler_params=_params("parallel", "parallel"),
    )(n, w_gate_t, w_up_t)


def _ffn_dact(dhb, w_down, gate, up, name):
    m, k = dhb.shape
    f = w_down.shape[0]
    tm, tn = _tile(m, 512), _tile(f, D_FF // 2)

    def body(d_ref, w_ref, g_ref, u_ref, dg_ref, du_ref):
        dact = lax.dot_general(d_ref[...], w_ref[...], NT, preferred_element_type=F32)
        g = g_ref[...].astype(F32)
        sg = _sigmoid(g)
        dg_ref[...] = (dact * u_ref[...].astype(F32) * sg * (1.0 + g * (1.0 - sg))).astype(BF16)
        du_ref[...] = (dact * g * sg).astype(BF16)

    ospec = pl.BlockSpec((tm, tn), lambda i, j: (i, j))
    return pl.pallas_call(
        body, name=name, grid=(m // tm, f // tn),
        in_specs=[pl.BlockSpec((tm, k), lambda i, j: (i, 0)), pl.BlockSpec((tn, k), lambda i, j: (j, 0)), ospec, ospec],
        out_specs=[ospec] * 2, out_shape=[jax.ShapeDtypeStruct((m, f), BF16)] * 2,
        compiler_params=_params("parallel", "parallel"),
    )(dhb, w_down, gate, up)


def _dn_norm(pairs, h, g, dres, name):
    m = h.shape[0]
    tm = 512
    np_ = len(pairs)

    def body(*refs):
        a_refs, b_refs = refs[:np_], refs[np_:2 * np_]
        h_ref, dres_ref, g_ref, dh_ref, dhb_ref, dg_ref = refs[2 * np_:]

        @pl.when(_first_step())
        def _():
            dg_ref[...] = jnp.zeros_like(dg_ref)

        dy = jnp.dot(a_refs[0][...], b_refs[0][...], preferred_element_type=F32)
        for a_ref, b_ref in zip(a_refs[1:], b_refs[1:]):
            dy = dy + jnp.dot(a_ref[...], b_ref[...], preferred_element_type=F32)
        x = h_ref[...]
        r = lax.rsqrt(jnp.mean(x * x, axis=-1, keepdims=True) + RMS_EPS)
        xh = x * r
        dg_ref[...] += jnp.sum(dy * xh, axis=0, keepdims=True)
        dxh = dy * g_ref[...]
        tot = dres_ref[...] + r * (dxh - xh * jnp.mean(dxh * xh, axis=-1, keepdims=True))
        dh_ref[...] = tot
        dhb_ref[...] = tot.astype(BF16)

    row = lambda w: pl.BlockSpec((tm, w), lambda i: (i, 0))
    whole = lambda a: pl.BlockSpec(a.shape, lambda i: (0, 0))
    a_list, b_list = [a for a, _ in pairs], [b for _, b in pairs]
    return pl.pallas_call(
        body, name=name, grid=(m // tm,),
        in_specs=[row(a.shape[1]) for a in a_list] + [whole(b) for b in b_list] + [row(D_MODEL), row(D_MODEL), whole(g)],
        out_specs=[row(D_MODEL), row(D_MODEL), pl.BlockSpec((1, D_MODEL), lambda i: (0, 0))],
        out_shape=[jax.ShapeDtypeStruct((m, D_MODEL), F32), jax.ShapeDtypeStruct((m, D_MODEL), BF16),
                   jax.ShapeDtypeStruct((1, D_MODEL), F32)],
        compiler_params=_params("arbitrary"),
    )(*a_list, *b_list, h, dres, g)


def _rows(body, name, tm, tiled, consts, outs, accs=()):
    s = tiled[0].shape[0]
    assert s % tm == 0
    in_specs = [pl.BlockSpec((tm, a.shape[1]), lambda i: (i, 0)) for a in tiled]
    in_specs += [pl.BlockSpec(a.shape, lambda i, nd=a.ndim: (0,) * nd) for a in consts]
    out_shape = [jax.ShapeDtypeStruct((s, c), dt) for c, dt in outs]
    out_shape += [jax.ShapeDtypeStruct(sh, dt) for sh, dt in accs]
    out_specs = [pl.BlockSpec((tm, c), lambda i: (i, 0)) for c, _ in outs]
    out_specs += [pl.BlockSpec(sh, lambda i, nd=len(sh): (0,) * nd) for sh, _ in accs]
    return pl.pallas_call(
        body, name=name, grid=(s // tm,), in_specs=in_specs, out_specs=out_specs, out_shape=out_shape,
        compiler_params=_params("arbitrary"),
    )(*tiled, *consts)


def _first_step():
    return pl.program_id(0) == 0


def _rms_fwd(h, g, name):
    def body(h_ref, g_ref, n_ref):
        x = h_ref[...]
        r = lax.rsqrt(jnp.mean(x * x, axis=-1, keepdims=True) + RMS_EPS)
        n_ref[...] = (x * r * g_ref[...]).astype(BF16)

    return _rows(body, name, 512, [h], [g], [(D_MODEL, BF16)])[0]


def _final_loss(h, g, tgt, name):
    def body(h_ref, t_ref, g_ref, dh_ref, dhb_ref, dg_ref, loss_ref):
        @pl.when(_first_step())
        def _():
            dg_ref[...] = jnp.zeros_like(dg_ref)
            loss_ref[...] = jnp.zeros_like(loss_ref)

        x = h_ref[...]
        r = lax.rsqrt(jnp.mean(x * x, axis=-1, keepdims=True) + RMS_EPS)
        xh = x * r
        gg = g_ref[...]
        e = xh * gg - t_ref[...]
        loss_ref[...] += (0.5 / D_MODEL) * jnp.sum(jnp.sum(e * e, axis=-1, keepdims=True), axis=0, keepdims=True)
        dy = e * (1.0 / D_MODEL)
        dg_ref[...] += jnp.sum(dy * xh, axis=0, keepdims=True)
        dxh = dy * gg
        dx = r * (dxh - xh * jnp.mean(dxh * xh, axis=-1, keepdims=True))
        dh_ref[...] = dx
        dhb_ref[...] = dx.astype(BF16)

    return _rows(body, name, 512, [h, tgt], [g], [(D_MODEL, F32), (D_MODEL, BF16)],
                 [((1, D_MODEL), F32), ((1, LANES), F32)])


def _rope_tables(s):
    half = ROT_DIM // 2
    inv_freq = ROPE_THETA ** (-jnp.arange(half, dtype=F32) * (2.0 / ROT_DIM))
    ang = jnp.arange(s, dtype=F32)[:, None] * inv_freq[None, :]
    cos, sin = jnp.cos(ang), jnp.sin(ang)
    rest = HEAD_DIM - ROT_DIM
    ones = jnp.ones((s, rest), F32)
    zeros = jnp.zeros((s, rest), F32)
    zh = jnp.zeros((s, half), F32)
    c_t = jnp.concatenate([cos, cos, ones], axis=1)
    a_t = jnp.concatenate([-sin, zh, zeros], axis=1)
    b_t = jnp.concatenate([zh, sin, zeros], axis=1)
    return tuple(jnp.tile(t, (1, LANES // HEAD_DIM)) for t in (c_t, a_t, b_t))


def _rot(x, c, a, b):
    w = x.shape[1]
    half = ROT_DIM // 2
    return x * c + pltpu.roll(x, w - half, 1) * a + pltpu.roll(x, half, 1) * b


def _wide(t, w):
    return t if w == LANES else jnp.tile(t, (1, w // LANES))


def _low_lanes(rows):
    return lax.broadcasted_iota(jnp.int32, (rows, LANES), 1) < HEAD_DIM


def _fold_store(x, sc_ref, out_refs):
    tm = x.shape[0]
    if any(d > 1 for d in out_refs):
        for p in range(N_PAIRS):
            sc_ref[p] = x[:, _pair_block(p)]
    for d, o_ref in out_refs.items():
        if d == 1:
            o_ref[0] = x.astype(o_ref.dtype)
            continue
        for r in range(d):
            for p in range(N_PAIRS):
                o_ref[r, :, _pair_block(p)] = sc_ref[p, pl.ds(r, tm // d, stride=d), :].astype(o_ref.dtype)


def _unfold_load(x_ref, sc_ref, d, add=False):
    n = x_ref.shape[1]
    for r in range(d):
        for p in range(N_PAIRS):
            rows = pl.ds(r, n, stride=d) if d > 1 else slice(None)
            val = x_ref[r, :, _pair_block(p)].astype(F32)
            if add:
                val = val + sc_ref[p, rows, :]
            sc_ref[p, rows, :] = val


def _folded_spec(d, tm, w=ATTN_W):
    return pl.BlockSpec((d, tm // d, w), lambda i: (0, i, 0))


def _folded_shape(s, d, dtype, w=ATTN_W):
    return jax.ShapeDtypeStruct((d, s // d, w), dtype)


def _qkv_prep_even(proj, tabs, name):
    s = proj.shape[0]
    tm = 512

    def body(p_ref, c_ref, a_ref, b_ref, q_ref, k_ref, v_ref):
        c, a, b = c_ref[...], a_ref[...], b_ref[...]
        q_ref[0] = _rot(p_ref[:, 0:ATTN_W], _wide(c, ATTN_W), _wide(a, ATTN_W), _wide(b, ATTN_W)).astype(BF16)
        lo = _low_lanes(tm)
        for src, o_ref in ((_rot(p_ref[:, 512:640], c, a, b), k_ref), (p_ref[:, 640:768], v_ref)):
            swapped = pltpu.roll(src, HEAD_DIM, 1)
            o_ref[0, :, 0:LANES] = jnp.where(lo, src, swapped).astype(BF16)
            o_ref[0, :, LANES:] = jnp.where(lo, swapped, src).astype(BF16)

    row = lambda w: pl.BlockSpec((tm, w), lambda i: (i, 0))
    return pl.pallas_call(
        body, name=name, grid=(s // tm,), in_specs=[row(proj.shape[1]), row(LANES), row(LANES), row(LANES)],
        out_specs=[_folded_spec(1, tm), _folded_spec(1, tm, 2 * LANES), _folded_spec(1, tm, 2 * LANES)],
        out_shape=[_folded_shape(s, 1, BF16), _folded_shape(s, 1, BF16, 2 * LANES), _folded_shape(s, 1, BF16, 2 * LANES)],
        compiler_params=_params("parallel"),
    )(proj, *tabs)


def _qkv_post_even(dq, dk, dv, dglu, tabs, name):
    s = dglu.shape[0]
    tm = 512

    def body(dq_ref, dk_ref, dv_ref, dr_ref, c_ref, a_ref, b_ref, o_ref):
        c, a, b = c_ref[...], -a_ref[...], -b_ref[...]
        o_ref[:, 0:ATTN_W] = _rot(dq_ref[0], _wide(c, ATTN_W), _wide(a, ATTN_W), _wide(b, ATTN_W)).astype(BF16)
        lo = _low_lanes(tm)
        merged = []
        for ref in (dk_ref, dv_ref):
            first, second = ref[0, :, 0:LANES], ref[0, :, LANES:]
            merged.append(jnp.where(lo, first + pltpu.roll(first, HEAD_DIM, 1), second + pltpu.roll(second, HEAD_DIM, 1)))
        o_ref[:, 512:640] = _rot(merged[0], c, a, b).astype(BF16)
        o_ref[:, 640:768] = merged[1].astype(BF16)
        o_ref[:, 768:] = dr_ref[...]

    row = lambda w: pl.BlockSpec((tm, w), lambda i: (i, 0))
    return pl.pallas_call(
        body, name=name, grid=(s // tm,),
        in_specs=[_folded_spec(1, tm), _folded_spec(1, tm, 2 * LANES), _folded_spec(1, tm, 2 * LANES),
                  row(dglu.shape[1]), row(LANES), row(LANES), row(LANES)],
        out_specs=row(EVEN_IN), out_shape=jax.ShapeDtypeStruct((s, EVEN_IN), BF16),
        compiler_params=_params("parallel"),
    )(dq, dk, dv, dglu, *tabs)


def _qkv_prep_odd(proj, tabs, name):
    s = proj.shape[0]
    tm = 512

    def body(p_ref, c_ref, a_ref, b_ref, *rest):
        outs, sc_ref = rest[:-1], rest[-1]
        c, a, b = (_wide(t[...], ATTN_W) for t in (c_ref, a_ref, b_ref))
        for t in range(3):
            x = p_ref[:, t * ATTN_W:(t + 1) * ATTN_W]
            if t < 2:
                x = _rot(x, c, a, b)
            _fold_store(x, sc_ref, {d: outs[t * len(DILATIONS) + i] for i, d in enumerate(DILATIONS)})

    row = lambda w: pl.BlockSpec((tm, w), lambda i: (i, 0))
    return pl.pallas_call(
        body, name=name, grid=(s // tm,), in_specs=[row(proj.shape[1]), row(LANES), row(LANES), row(LANES)],
        out_specs=[_folded_spec(d, tm) for _ in range(3) for d in DILATIONS],
        out_shape=[_folded_shape(s, d, BF16) for _ in range(3) for d in DILATIONS],
        scratch_shapes=[pltpu.VMEM((N_PAIRS, tm, LANES), F32)],
        compiler_params=_params("parallel"),
    )(proj, *tabs)


def _qkv_post_odd(dqs, dks, dvs, dz, tabs, name):
    s = dz.shape[0]
    tm = 256
    nb = len(DILATIONS)

    def body(*refs):
        groups = (refs[:nb], refs[nb:2 * nb], refs[2 * nb:3 * nb])
        dz_ref, c_ref, a_ref, b_ref, o_ref, sc_ref = refs[3 * nb:]
        c, a, b = _wide(c_ref[...], ATTN_W), _wide(-a_ref[...], ATTN_W), _wide(-b_ref[...], ATTN_W)
        for t, group in enumerate(groups):
            for i, d in enumerate(DILATIONS):
                _unfold_load(group[i], sc_ref, d, add=i > 0)
            x = jnp.concatenate([sc_ref[p] for p in range(N_PAIRS)], axis=1)
            if t < 2:
                x = _rot(x, c, a, b)
            o_ref[:, t * ATTN_W:(t + 1) * ATTN_W] = x.astype(BF16)
        o_ref[:, 3 * ATTN_W:] = dz_ref[...]

    row = lambda w: pl.BlockSpec((tm, w), lambda i: (i, 0))
    return pl.pallas_call(
        body, name=name, grid=(s // tm,),
        in_specs=[_folded_spec(d, tm) for _ in range(3) for d in DILATIONS] + [row(dz.shape[1]), row(LANES), row(LANES), row(LANES)],
        out_specs=row(ODD_IN), out_shape=jax.ShapeDtypeStruct((s, ODD_IN), BF16),
        scratch_shapes=[pltpu.VMEM((N_PAIRS, tm, LANES), F32)],
        compiler_params=_params("parallel"),
    )(*dqs, *dks, *dvs, dz, *tabs)


def _fold_dout(dmix, name):
    s = dmix.shape[0]
    tm = 512
    ds = [d for d in DILATIONS if d > 1]

    def body(d_ref, *rest):
        outs, sc_ref = rest[:-1], rest[-1]
        _fold_store(d_ref[...], sc_ref, dict(zip(ds, outs)))

    return pl.pallas_call(
        body, name=name, grid=(s // tm,), in_specs=[pl.BlockSpec((tm, ATTN_W), lambda i: (i, 0))],
        out_specs=[_folded_spec(d, tm) for d in ds], out_shape=[_folded_shape(s, d, BF16) for d in ds],
        scratch_shapes=[pltpu.VMEM((N_PAIRS, tm, LANES), F32)],
        compiler_params=_params("parallel"),
    )(dmix)


def _window(j, i, tq):
    r0 = j * tq + i * BLOCK
    start = pl.multiple_of(jnp.maximum(r0 - BLOCK, 0), BLOCK)
    return pl.ds(start, 2 * BLOCK), r0 - start


def _band_valid(offset, max_dist):
    shape = (2 * BLOCK, 2 * BLOCK)
    dist = (lax.bitwise_and(lax.broadcasted_iota(jnp.int32, shape, 0), BLOCK - 1)
            - lax.broadcasted_iota(jnp.int32, shape, 1) + offset)
    return jnp.abs(2 * dist - max_dist) <= max_dist


def _stack_heads(lo, x):
    zero = jnp.zeros_like(x)
    return jnp.concatenate([jnp.where(lo, x, zero), jnp.where(lo, zero, x)], axis=0)


def _unstack_heads(lo, x):
    return jnp.where(lo, x[:BLOCK], x[BLOCK:])


NT = (((1,), (1,)), ((), ()))
TN = (((0,), (0,)), ((), ()))


def _attn_fwd(q, k, v, sinks, *, max_dist, name, emit_bf16=False):
    d, sp, wq = q.shape
    nq, nk = wq // LANES, k.shape[2] // LANES
    kdiv = nq // nk
    tq = min(sp, 1024)
    nsub = tq // BLOCK
    has_sink = sinks is not None

    def body(*refs):
        refs = list(refs)
        sink_ref = refs.pop(0) if has_sink else None
        q_ref, k_ref, v_ref, o_ref, lse_ref = refs[:5]
        pair = pl.program_id(1)
        j = pl.program_id(2)
        lo = _low_lanes(BLOCK)
        if has_sink:
            first_head = lax.broadcasted_iota(jnp.int32, (2 * BLOCK, 1), 0) < BLOCK
            sk = jnp.where(first_head, sink_ref[2 * pair], sink_ref[2 * pair + 1])
        for i in range(nsub):
            win, offset = _window(j, i, tq)
            rows = slice(i * BLOCK, (i + 1) * BLOCK)
            kw = k_ref[0, win, :]
            vw = v_ref[0, win, :]
            s = lax.dot_general(_stack_heads(lo, q_ref[0, rows, :]), kw, NT, preferred_element_type=F32) * ATTN_SCALE
            s = jnp.where(_band_valid(offset, max_dist), s, NEG)
            m = jnp.max(s, axis=-1, keepdims=True)
            if has_sink:
                m = jnp.maximum(m, sk)
            p = jnp.exp(s - m)
            l = jnp.sum(p, axis=-1, keepdims=True)
            if has_sink:
                l = l + jnp.exp(sk - m)
            o2 = _unstack_heads(lo, jnp.dot(p.astype(BF16), vw, preferred_element_type=F32) / l)
            o_ref[0, rows, :] = o2
            lse_ref[0, rows, :] = _unstack_heads(lo, m + jnp.log(l))
            if emit_bf16:
                refs[5][0, rows, :] = o2.astype(BF16)

    qspec = pl.BlockSpec((1, tq, LANES), lambda r, p, j: (r, j, p))
    kspec = pl.BlockSpec((1, sp, LANES), lambda r, p, j: (r, 0, p // kdiv))
    in_specs = [qspec, kspec, kspec]
    operands = [q, k, v]
    if has_sink:
        in_specs = [pl.BlockSpec(memory_space=pltpu.SMEM)] + in_specs
        operands = [sinks] + operands
    out_shape = [jax.ShapeDtypeStruct(q.shape, F32), jax.ShapeDtypeStruct(q.shape, F32)]
    if emit_bf16:
        out_shape.append(jax.ShapeDtypeStruct(q.shape, BF16))
    return pl.pallas_call(
        body, name=name, grid=(d, nq, sp // tq), in_specs=in_specs, out_specs=[qspec] * len(out_shape),
        out_shape=out_shape, compiler_params=_params("parallel", "parallel", "arbitrary"),
    )(*operands)


def _attn_bwd(q, k, v, do, oo, lse, sinks, *, max_dist, name):
    d, sp, wq = q.shape
    wk = k.shape[2]
    nq, nk = wq // LANES, wk // LANES
    kdiv = nq // nk
    tq = min(sp, 1024)
    nsub = tq // BLOCK
    has_sink = sinks is not None

    def body(*refs):
        refs = list(refs)
        sink_ref = refs.pop(0) if has_sink else None
        q_ref, k_ref, v_ref, do_ref, oo_ref, lse_ref, dq_ref, dk_ref, dv_ref = refs[:9]
        pk, g, j = pl.program_id(1), pl.program_id(2), pl.program_id(3)

        @pl.when((g == 0) & (j == 0))
        def _():
            dk_ref[...] = jnp.zeros_like(dk_ref)
            dv_ref[...] = jnp.zeros_like(dv_ref)

        lo = _low_lanes(BLOCK)
        if has_sink:
            first_head = lax.broadcasted_iota(jnp.int32, (2 * BLOCK, 1), 0) < BLOCK
            pair = pk * kdiv + g
            sk = jnp.where(first_head, sink_ref[2 * pair], sink_ref[2 * pair + 1])
            sink_acc = jnp.zeros((2 * BLOCK, LANES), F32)
        for i in range(nsub):
            win, offset = _window(j, i, tq)
            rows = slice(i * BLOCK, (i + 1) * BLOCK)
            kw = k_ref[0, win, :]
            vw = v_ref[0, win, :]
            do2 = do_ref[0, rows, :].astype(F32)
            qs = _stack_heads(lo, q_ref[0, rows, :])
            dos = _stack_heads(lo, do2.astype(BF16))
            prod = do2 * oo_ref[0, rows, :]
            delta = jnp.sum(_stack_heads(lo, prod), axis=-1, keepdims=True)
            lse2 = lse_ref[0, rows, :]
            lse_swapped = pltpu.roll(lse2, HEAD_DIM, 1)
            lse_st = jnp.concatenate([jnp.where(lo, lse2, lse_swapped), jnp.where(lo, lse_swapped, lse2)], axis=0)
            s = lax.dot_general(qs, kw, NT, preferred_element_type=F32) * ATTN_SCALE
            s = jnp.where(_band_valid(offset, max_dist), s, NEG)
            p = jnp.exp(s - jnp.tile(lse_st, (1, 2)))
            dv_ref[0, win, :] += lax.dot_general(p.astype(BF16), dos, TN, preferred_element_type=F32)
            dp = lax.dot_general(dos, vw, NT, preferred_element_type=F32)
            ds = (p * (dp - delta) * ATTN_SCALE).astype(BF16)
            dq_ref[0, rows, :] = _unstack_heads(lo, jnp.dot(ds, kw, preferred_element_type=F32))
            dk_ref[0, win, :] += lax.dot_general(ds, qs, TN, preferred_element_type=F32)
            if has_sink:
                sink_acc = sink_acc - jnp.exp(sk - lse_st) * delta
        if has_sink:
            dsink_ref = refs[9]

            @pl.when(j == 0)
            def _():
                dsink_ref[...] = jnp.zeros_like(dsink_ref)

            dsink_ref[0] += jnp.where(lo[0:1], jnp.sum(sink_acc[:BLOCK], axis=0, keepdims=True),
                                      jnp.sum(sink_acc[BLOCK:], axis=0, keepdims=True))

    def qmap(r, pk, g, j):
        return (r, j, pk * kdiv + g)

    def kmap(r, pk, g, j):
        return (r, 0, pk)

    qspec = pl.BlockSpec((1, tq, LANES), qmap)
    kspec = pl.BlockSpec((1, sp, LANES), kmap)
    in_specs = [qspec, kspec, kspec, qspec, qspec, qspec]
    operands = [q, k, v, do, oo, lse]
    out_specs = [qspec, kspec, kspec]
    out_shape = [jax.ShapeDtypeStruct((d, sp, wq), F32), jax.ShapeDtypeStruct((d, sp, wk), F32),
                 jax.ShapeDtypeStruct((d, sp, wk), F32)]
    if has_sink:
        in_specs = [pl.BlockSpec(memory_space=pltpu.SMEM)] + in_specs
        operands = [sinks] + operands
        out_specs.append(pl.BlockSpec((1, 1, LANES), lambda r, pk, g, j: (pk * kdiv + g, 0, 0)))
        out_shape.append(jax.ShapeDtypeStruct((nq, 1, LANES), F32))
    return pl.pallas_call(
        body, name=name, grid=(d, nk, kdiv, sp // tq), in_specs=in_specs, out_specs=out_specs, out_shape=out_shape,
        compiler_params=_params("parallel", "parallel", "arbitrary", "arbitrary"),
    )(*operands)


def _combine(outs, lses, name):
    s = outs[0].shape[1]
    tm = 512
    nb = len(DILATIONS)
    ds = [d for d in DILATIONS if d > 1]

    def body(*refs):
        o_refs, l_refs = refs[:nb], refs[nb:2 * nb]
        cb_ref, c_ref, lse_ref = refs[2 * nb:2 * nb + 3]
        folded = refs[2 * nb + 3:2 * nb + 3 + 2 * len(ds)]
        scratch = refs[2 * nb + 3 + 2 * len(ds):]
        so = {1: None}
        sl = {1: None}
        for i, d in enumerate(ds):
            so[d], sl[d] = scratch[2 * i], scratch[2 * i + 1]
            _unfold_load(o_refs[1 + i], so[d], d)
            _unfold_load(l_refs[1 + i], sl[d], d)
        for p in range(N_PAIRS):
            pb = _pair_block(p)
            ls = [l_refs[0][0, :, pb]] + [sl[d][p] for d in ds]
            os_ = [o_refs[0][0, :, pb]] + [so[d][p] for d in ds]
            m = ls[0]
            for t in ls[1:]:
                m = jnp.maximum(m, t)
            ws = [jnp.exp(t - m) for t in ls]
            tot = ws[0]
            for t in ws[1:]:
                tot = tot + t
            acc = ws[0] * os_[0]
            for w, o in zip(ws[1:], os_[1:]):
                acc = acc + w * o
            cmix = acc / tot
            lse = m + jnp.log(tot)
            cb_ref[:, pb] = cmix.astype(BF16)
            c_ref[0, :, pb] = cmix
            lse_ref[0, :, pb] = lse
            so[ds[0]][p] = cmix
            sl[ds[0]][p] = lse
        for i, d in enumerate(ds):
            for r in range(d):
                for p in range(N_PAIRS):
                    rows = pl.ds(r, tm // d, stride=d)
                    folded[2 * i][r, :, _pair_block(p)] = so[ds[0]][p, rows, :]
                    folded[2 * i + 1][r, :, _pair_block(p)] = sl[ds[0]][p, rows, :]

    in_specs = [_folded_spec(d, tm) for _ in range(2) for d in DILATIONS]
    out_specs = [pl.BlockSpec((tm, ATTN_W), lambda i: (i, 0)), _folded_spec(1, tm), _folded_spec(1, tm)]
    out_shape = [jax.ShapeDtypeStruct((s, ATTN_W), BF16), _folded_shape(s, 1, F32), _folded_shape(s, 1, F32)]
    for d in ds:
        out_specs += [_folded_spec(d, tm)] * 2
        out_shape += [_folded_shape(s, d, F32)] * 2
    return pl.pallas_call(
        body, name=name, grid=(s // tm,), in_specs=in_specs, out_specs=out_specs, out_shape=out_shape,
        scratch_shapes=[pltpu.VMEM((N_PAIRS, tm, LANES), F32)] * (2 * len(ds)),
        compiler_params=_params("parallel"),
    )(*outs, *lses)


GLU_A = slice(768, 1280)
GLU_B = slice(1280, 1792)
EVEN_IN = 1792
ODD_IN = 2560
CONV_CH = 512


def _conv_fwd(proj, w, b, ln_g, ln_b, name):
    s = proj.shape[0]
    tm = 512
    nh = tm // CONV_HALO
    lead = CONV_HALO - (CONV_WIDTH - 1)

    def body(p_ref, ph_ref, w_ref, b_ref, g_ref, bb_ref, y_ref, o_ref, xf_ref):
        xf_ref[CONV_HALO:, :] = p_ref[:, GLU_A] * _sigmoid(p_ref[:, GLU_B])
        hist = ph_ref[:, GLU_A] * _sigmoid(ph_ref[:, GLU_B])
        xf_ref[0:CONV_HALO, :] = jnp.where(pl.program_id(0) > 0, hist, 0.0)
        for c0 in range(0, tm, CONV_ROWS):
            acc = jnp.zeros((CONV_ROWS, CONV_CH), F32) + b_ref[...]
            for j in range(CONV_WIDTH):
                acc = acc + xf_ref[pl.ds(lead + j + c0, CONV_ROWS), :] * w_ref[j:j + 1, :]
            y_ref[c0:c0 + CONV_ROWS, :] = acc
            mu = jnp.mean(acc, axis=-1, keepdims=True)
            xc = acc - mu
            var = jnp.mean(xc * xc, axis=-1, keepdims=True)
            zz = xc * lax.rsqrt(var + LN_EPS) * g_ref[...] + bb_ref[...]
            o_ref[c0:c0 + CONV_ROWS, :] = (zz * _sigmoid(zz)).astype(BF16)

    def const(a):
        return pl.BlockSpec(a.shape, lambda i: (0, 0))

    return pl.pallas_call(
        body, name=name, grid=(s // tm,),
        in_specs=[pl.BlockSpec((tm, EVEN_IN), lambda i: (i, 0)),
                  pl.BlockSpec((CONV_HALO, EVEN_IN), lambda i: (jnp.maximum(i * nh - 1, 0), 0)),
                  const(w), const(b), const(ln_g), const(ln_b)],
        out_specs=[pl.BlockSpec((tm, CONV_CH), lambda i: (i, 0)), pl.BlockSpec((tm, CONV_CH), lambda i: (i, 0))],
        out_shape=[jax.ShapeDtypeStruct((s, CONV_CH), F32), jax.ShapeDtypeStruct((s, CONV_CH), BF16)],
        scratch_shapes=[pltpu.VMEM((tm + CONV_HALO, CONV_CH), F32)],
        compiler_params=_params("arbitrary"),
    )(proj, proj, w, b, ln_g, ln_b)


def _conv_tail_bwd(dmix, yconv, ln_g, ln_b, name):
    def body(d_ref, y_ref, g_ref, b_ref, dy_ref, dg_ref, db_ref, dcb_ref):
        @pl.when(_first_step())
        def _():
            dg_ref[...] = jnp.zeros_like(dg_ref)
            db_ref[...] = jnp.zeros_like(db_ref)
            dcb_ref[...] = jnp.zeros_like(dcb_ref)

        y = y_ref[...]
        g = g_ref[...]
        mu = jnp.mean(y, axis=-1, keepdims=True)
        xc = y - mu
        rstd = lax.rsqrt(jnp.mean(xc * xc, axis=-1, keepdims=True) + LN_EPS)
        xh = xc * rstd
        zz = xh * g + b_ref[...]
        sg = _sigmoid(zz)
        dzz = d_ref[:, CONV_CH:] * sg * (1.0 + zz * (1.0 - sg))
        dg_ref[...] += jnp.sum(dzz * xh, axis=0, keepdims=True)
        db_ref[...] += jnp.sum(dzz, axis=0, keepdims=True)
        dxh = dzz * g
        dy = rstd * (dxh - jnp.mean(dxh, axis=-1, keepdims=True) - xh * jnp.mean(dxh * xh, axis=-1, keepdims=True))
        dcb_ref[...] += jnp.sum(dy, axis=0, keepdims=True)
        dy_ref[...] = dy

    vec = ((1, CONV_CH), F32)
    return _rows(body, name, 512, [dmix, yconv], [ln_g, ln_b], [(CONV_CH, F32)], [vec, vec, vec])


def _conv_bwd(proj, dy, w, name):
    s = proj.shape[0]
    tm = 512
    nh = tm // CONV_HALO
    nsteps = s // tm
    lead = CONV_HALO - (CONV_WIDTH - 1)

    def body(p_ref, ph_ref, dy_ref, dyn_ref, w_ref, dglu_ref, dw_ref, xf_ref, dyf_ref):
        i = pl.program_id(0)

        @pl.when(i == 0)
        def _():
            dw_ref[...] = jnp.zeros_like(dw_ref)

        ga = p_ref[:, GLU_A]
        sgb = _sigmoid(p_ref[:, GLU_B])
        xf_ref[CONV_HALO:, :] = ga * sgb
        hist = ph_ref[:, GLU_A] * _sigmoid(ph_ref[:, GLU_B])
        xf_ref[0:CONV_HALO, :] = jnp.where(i > 0, hist, 0.0)
        dyt = dy_ref[...]
        dyf_ref[0:tm, :] = dyt
        dyf_ref[tm:, :] = jnp.where(i < nsteps - 1, dyn_ref[...], 0.0)
        for c0 in range(0, tm, CONV_ROWS):
            rows = slice(c0, c0 + CONV_ROWS)
            acc = jnp.zeros((CONV_ROWS, CONV_CH), F32)
            for j in range(CONV_WIDTH):
                acc = acc + dyf_ref[pl.ds(CONV_WIDTH - 1 - j + c0, CONV_ROWS), :] * w_ref[j:j + 1, :]
            a_c, s_c = ga[rows, :], sgb[rows, :]
            dglu_ref[rows, 0:CONV_CH] = (acc * s_c).astype(BF16)
            dglu_ref[rows, CONV_CH:] = (acc * a_c * s_c * (1.0 - s_c)).astype(BF16)
        for j in range(CONV_WIDTH):
            part = jnp.zeros((8, CONV_CH), F32)
            for c0 in range(0, tm, CONV_ROWS):
                prod = dy_ref[c0:c0 + CONV_ROWS, :] * xf_ref[pl.ds(lead + j + c0, CONV_ROWS), :]
                part = part + jnp.sum(prod.reshape(CONV_ROWS // 8, 8, CONV_CH), axis=0)
            dw_ref[j:j + 1, :] += jnp.sum(part, axis=0, keepdims=True)

    return pl.pallas_call(
        body, name=name, grid=(nsteps,),
        in_specs=[pl.BlockSpec((tm, EVEN_IN), lambda i: (i, 0)),
                  pl.BlockSpec((CONV_HALO, EVEN_IN), lambda i: (jnp.maximum(i * nh - 1, 0), 0)),
                  pl.BlockSpec((tm, CONV_CH), lambda i: (i, 0)),
                  pl.BlockSpec((CONV_HALO, CONV_CH), lambda i: (jnp.minimum((i + 1) * nh, s // CONV_HALO - 1), 0)),
                  pl.BlockSpec(w.shape, lambda i: (0, 0))],
        out_specs=[pl.BlockSpec((tm, 2 * CONV_CH), lambda i: (i, 0)), pl.BlockSpec(w.shape, lambda i: (0, 0))],
        out_shape=[jax.ShapeDtypeStruct((s, 2 * CONV_CH), BF16), jax.ShapeDtypeStruct(w.shape, F32)],
        scratch_shapes=[pltpu.VMEM((tm + CONV_HALO, CONV_CH), F32), pltpu.VMEM((tm + CONV_HALO, CONV_CH), F32)],
        compiler_params=_params("arbitrary"),
    )(proj, proj, dy, dy, w)


GATE_Z = slice(1536, 2560)
D_CH = 512
GELU_C = math.sqrt(2.0 / math.pi)
GELU_K = 0.044715


def _gelu_parts(z):
    t = jnp.tanh(GELU_C * (z + GELU_K * z * z * z))
    return 0.5 * z * (1.0 + t), t


def _lane_group(rows):
    return lax.broadcasted_iota(jnp.int32, (rows, D_CH), 1) // HEAD_DIM


def _tril_mask():
    return lax.broadcasted_iota(jnp.int32, (BLOCK, BLOCK), 0) >= lax.broadcasted_iota(jnp.int32, (BLOCK, BLOCK), 1)


def _layer_norm_parts(x):
    mu = jnp.mean(x, axis=-1, keepdims=True)
    xc = x - mu
    rstd = lax.rsqrt(jnp.mean(xc * xc, axis=-1, keepdims=True) + LN_EPS)
    return xc * rstd, rstd


def _gate_fwd(proj, ln_g, ln_b, w_sp, sb_t, name):
    tm = 512

    def body(p_ref, g_ref, b_ref, w_ref, sb_ref, mixed_ref, out_ref):
        zz, _ = _gelu_parts(p_ref[:, GATE_Z])
        u = zz[:, :D_CH]
        xh, _ = _layer_norm_parts(zz[:, D_CH:])
        gn = (xh * g_ref[...] + b_ref[...]).astype(BF16)
        grp = _lane_group(BLOCK)
        tri = _tril_mask()
        ws = [jnp.where(tri, w_ref[gi], 0.0).astype(BF16) for gi in range(N_GROUPS)]
        bias = jnp.zeros((BLOCK, D_CH), F32)
        for gi in range(N_GROUPS):
            bias = jnp.where(grp == gi, sb_ref[:, gi:gi + 1], bias)
        for ch in range(tm // BLOCK):
            rows = slice(ch * BLOCK, (ch + 1) * BLOCK)
            gc = gn[rows, :]
            mixed = bias
            for gi in range(N_GROUPS):
                r = jnp.dot(ws[gi], gc, preferred_element_type=F32)
                mixed = jnp.where(grp == gi, r + bias, mixed)
            mixed_ref[rows, :] = mixed
            out_ref[rows, :] = (u[rows, :] * mixed).astype(BF16)

    return _rows(body, name, tm, [proj], [ln_g, ln_b, w_sp, sb_t], [(D_CH, F32), (D_CH, BF16)])


def _gate_bwd(dmix, proj, mixed, ln_g, ln_b, w_sp, name):
    tm = 512

    def body(d_ref, p_ref, m_ref, g_ref, b_ref, w_ref, dz_ref, dg_ref, db_ref, dw_ref, dsb_ref, dgn_ref):
        @pl.when(_first_step())
        def _():
            dg_ref[...] = jnp.zeros_like(dg_ref)
            db_ref[...] = jnp.zeros_like(db_ref)
            dw_ref[...] = jnp.zeros_like(dw_ref)
            dsb_ref[...] = jnp.zeros_like(dsb_ref)

        z = p_ref[:, GATE_Z]
        zz, t = _gelu_parts(z)
        u = zz[:, :D_CH]
        xh, rstd = _layer_norm_parts(zz[:, D_CH:])
        g = g_ref[...]
        gn = (xh * g + b_ref[...]).astype(BF16)
        dd = d_ref[:, D_CH:]
        du = dd * m_ref[...]
        dm = dd * u
        grp = _lane_group(BLOCK)
        tri = _tril_mask()
        ws = [jnp.where(tri, w_ref[gi], 0.0).astype(BF16) for gi in range(N_GROUPS)]
        gsel = (lax.broadcasted_iota(jnp.int32, (N_GROUPS, D_CH), 1) // HEAD_DIM
                == lax.broadcasted_iota(jnp.int32, (N_GROUPS, D_CH), 0)).astype(F32)
        for ch in range(tm // BLOCK):
            rows = slice(ch * BLOCK, (ch + 1) * BLOCK)
            dmc = dm[rows, :]
            dmb = dmc.astype(BF16)
            gc = gn[rows, :]
            dgn = jnp.zeros((BLOCK, D_CH), F32)
            for gi in range(N_GROUPS):
                r = lax.dot_general(ws[gi], dmb, TN, preferred_element_type=F32)
                dgn = jnp.where(grp == gi, r, dgn)
                dmg = jnp.where(grp == gi, dmb, jnp.zeros_like(dmb))
                dwg = lax.dot_general(dmg, gc, NT, preferred_element_type=F32)
                dw_ref[gi] += jnp.where(tri, dwg, 0.0)
            dsb_ref[...] += lax.dot_general(gsel, dmc, NT, preferred_element_type=F32, precision=lax.Precision.HIGHEST)
            dgn_ref[rows, :] = dgn
        dgn = dgn_ref[...]
        db_ref[...] += jnp.sum(dgn, axis=0, keepdims=True)
        dg_ref[...] += jnp.sum(dgn * xh, axis=0, keepdims=True)
        dxh = dgn * g
        dgp = rstd * (dxh - jnp.mean(dxh, axis=-1, keepdims=True) - xh * jnp.mean(dxh * xh, axis=-1, keepdims=True))
        dgelu = 0.5 * (1.0 + t) + 0.5 * z * (1.0 - t * t) * GELU_C * (1.0 + 3.0 * GELU_K * z * z)
        dz_ref[:, 0:D_CH] = (du * dgelu[:, :D_CH]).astype(BF16)
        dz_ref[:, D_CH:] = (dgp * dgelu[:, D_CH:]).astype(BF16)

    s = proj.shape[0]
    tiled = [dmix, proj, mixed]
    consts = [ln_g, ln_b, w_sp]
    in_specs = [pl.BlockSpec((tm, a.shape[1]), lambda i: (i, 0)) for a in tiled]
    in_specs += [pl.BlockSpec(a.shape, lambda i, nd=a.ndim: (0,) * nd) for a in consts]
    vec = (1, D_CH)
    acc_shapes = [vec, vec, w_sp.shape, (N_GROUPS, BLOCK)]
    return pl.pallas_call(
        body, name=name, grid=(s // tm,), in_specs=in_specs,
        out_specs=[pl.BlockSpec((tm, 2 * D_CH), lambda i: (i, 0))]
        + [pl.BlockSpec(sh, lambda i, nd=len(sh): (0,) * nd) for sh in acc_shapes],
        out_shape=[jax.ShapeDtypeStruct((s, 2 * D_CH), BF16)] + [jax.ShapeDtypeStruct(sh, F32) for sh in acc_shapes],
        scratch_shapes=[pltpu.VMEM((tm, D_CH), F32)],
        compiler_params=_params("arbitrary"),
    )(*tiled, *consts)


def _adam_update(w, g, m, v):
    nm = ADAM_B1 * m + (1.0 - ADAM_B1) * g
    nv = ADAM_B2 * v + (1.0 - ADAM_B2) * (g * g)
    m_hat = nm / (1.0 - ADAM_B1 ** ADAM_STEP)
    v_hat = nv / (1.0 - ADAM_B2 ** ADAM_STEP)
    return -ADAM_LR * (m_hat / (jnp.sqrt(v_hat) + ADAM_EPS) + ADAM_WD * w), nm, nv


def _adamw(w, g, m, v, name):
    rows, cols = w.shape
    tm = _tile(rows, 512, 8)

    def body(w_ref, g_ref, m_ref, v_ref, d_ref, nm_ref, nv_ref):
        d_ref[...], nm_ref[...], nv_ref[...] = _adam_update(w_ref[...], g_ref[...], m_ref[...], v_ref[...])

    return _rows(body, name, tm, [w, g, m, v], [], [(cols, F32)] * 3)


def _ordered_sum(parts, name):
    n, rows, cols = parts.shape
    tm = _tile(rows, 512, 16 if parts.dtype == BF16 else 8)

    def body(p_ref, o_ref):
        acc = p_ref[0].astype(F32)
        for k in range(1, n):
            acc = acc + p_ref[k].astype(F32)
        o_ref[...] = acc

    return pl.pallas_call(body, name=name, grid=(rows // tm,),
                          in_specs=[pl.BlockSpec((n, tm, cols), lambda i: (0, i, 0))],
                          out_specs=pl.BlockSpec((tm, cols), lambda i: (i, 0)),
                          out_shape=jax.ShapeDtypeStruct((rows, cols), F32), compiler_params=_params("parallel"))(parts)


ANY = pl.BlockSpec(memory_space=pl.ANY)


def _position():
    x, y, c = lax.axis_index("x"), lax.axis_index("y"), lax.axis_index("c")
    other_chips = [(1 - x, y), (x, 1 - y), (1 - x, 1 - y)]
    return x, y, c, other_chips


def _remote(src, dst, send_sem, recv_sem, to):
    return pltpu.make_async_remote_copy(src_ref=src, dst_ref=dst, send_sem=send_sem, recv_sem=recv_sem,
                                        device_id=to, device_id_type=MESH)


STAGE_ROWS = 736


def _staged_copies(copies, buf, in_sems, out_sems):
    n = len(copies)

    def into(u):
        src = copies[u][0]
        return pltpu.make_async_copy(src, buf.at[u % 2, pl.ds(0, src.shape[0]), :], in_sems.at[u % 2])

    def out_of(u):
        dst = copies[u][1]
        return pltpu.make_async_copy(buf.at[u % 2, pl.ds(0, dst.shape[0]), :], dst, out_sems.at[u % 2])

    into(0).start()
    for u in range(n):
        into(u).wait()
        out_of(u).start()
        if u + 1 < n:
            if u >= 1:
                out_of(u - 1).wait()
            into(u + 1).start()
    if n >= 2:
        out_of(n - 2).wait()
    out_of(n - 1).wait()


def _stage_scratch(dtype, cols):
    return [pltpu.VMEM((2, STAGE_ROWS, cols), dtype), pltpu.SemaphoreType.DMA((2,)), pltpu.SemaphoreType.DMA((2,))]


def _row_chunks(rows):
    return [(r, min(STAGE_ROWS, rows - r)) for r in range(0, rows, STAGE_ROWS)]


def _gather_chips(shard, name):
    rows, cols = shard.shape
    half = rows // 2

    def body(in_ref, out_ref, send_sems, recv_sems, buf, in_sems, out_sems):
        x, y, c, chips = _position()
        me = 2 * x + y
        sibling = (x, y, 1 - c)

        def slab(chip, h):
            return out_ref.at[chip, pl.ds(h * half, half), :]

        first = [_remote(in_ref.at[pl.ds(c * half, half), :], slab(me, c), send_sems.at[j], recv_sems.at[j], (cx, cy, c))
                 for j, (cx, cy) in enumerate(chips)]
        for cp in first:
            cp.start()
        _staged_copies([(in_ref.at[pl.ds(r, n), :], out_ref.at[me, pl.ds(r, n), :]) for r, n in _row_chunks(rows)],
                       buf, in_sems, out_sems)
        passed = []
        for j, (cx, cy) in enumerate(chips):
            got = slab(2 * cx + cy, c)
            _remote(got, got, send_sems.at[j], recv_sems.at[j], sibling).wait_recv()
            cp = _remote(got, got, send_sems.at[3 + j], recv_sems.at[3 + j], sibling)
            cp.start()
            passed.append(cp)
        for j, (cx, cy) in enumerate(chips):
            got = slab(2 * cx + cy, 1 - c)
            _remote(got, got, send_sems.at[3 + j], recv_sems.at[3 + j], sibling).wait_recv()
        for cp in first + passed:
            cp.wait_send()

    return pl.pallas_call(
        body, name=name, in_specs=[ANY], out_specs=ANY,
        out_shape=jax.ShapeDtypeStruct((N_CHIPS, rows, cols), shard.dtype),
        scratch_shapes=[pltpu.SemaphoreType.DMA((6,)), pltpu.SemaphoreType.DMA((6,))] + _stage_scratch(shard.dtype, cols),
        compiler_params=pltpu.CompilerParams(vmem_limit_bytes=VMEM_LIMIT),
    )(shard)


HBM = pl.BlockSpec(memory_space=pltpu.HBM)
SEM = pl.BlockSpec(memory_space=pltpu.SEMAPHORE)
SIDE_EFFECT = pltpu.SideEffectType.DATAFLOW_SIDE_EFFECTING


def _ici_copies(in_ref, land_ref, send_sems, recv_sems, half):
    x, y, c, chips = _position()
    mine = pl.ds(c * half, half)
    sends = [_remote(in_ref.at[mine, :], land_ref.at[2 * x + y, mine, :], send_sems.at[j], recv_sems.at[j], (cx, cy, c))
             for j, (cx, cy) in enumerate(chips)]
    arrivals = [_remote(in_ref.at[mine, :], land_ref.at[2 * cx + cy, mine, :], send_sems.at[j], recv_sems.at[j], (cx, cy, c))
                for j, (cx, cy) in enumerate(chips)]
    return sends, arrivals


def _gather_start(shard, after, name):
    rows, cols = shard.shape

    def body(in_ref, land_ref, after_ref, send_sems, recv_sems, in_thru, land_thru, token):
        sends, _ = _ici_copies(in_ref, land_ref, send_sems, recv_sems, rows // 2)
        for cp in sends:
            cp.start()
        token[...] = jnp.zeros_like(token)

    land = lax.empty((N_CHIPS, rows, cols), shard.dtype)
    return pl.pallas_call(
        body, name=name,
        out_shape=(pltpu.SemaphoreType.DMA((3,)), pltpu.SemaphoreType.DMA((3,)), pltpu.HBM(shard.shape, shard.dtype),
                   pltpu.HBM(land.shape, land.dtype), jax.ShapeDtypeStruct((8, LANES), F32)),
        in_specs=(HBM, HBM, ANY), out_specs=(SEM, SEM, HBM, HBM, pl.BlockSpec(memory_space=pltpu.VMEM)),
        input_output_aliases={0: 2, 1: 3},
        compiler_params=pltpu.CompilerParams(has_side_effects=SIDE_EFFECT),
    )(pltpu.with_memory_space_constraint(shard, pltpu.HBM), pltpu.with_memory_space_constraint(land, pltpu.HBM), after)


def _gather_wait(send_sems, recv_sems, shard, land, after, name):
    rows = shard.shape[0]

    def body(in_ref, land_ref, send_sems, recv_sems, after_ref, in_out, land_out):
        sends, arrivals = _ici_copies(in_ref, land_ref, send_sems, recv_sems, rows // 2)
        for cp in sends:
            cp.wait_send()
        for cp in arrivals:
            cp.wait_recv()

    return pl.pallas_call(
        body, name=name, out_shape=(pltpu.HBM(shard.shape, shard.dtype), pltpu.HBM(land.shape, land.dtype)),
        in_specs=(HBM, HBM, SEM, SEM, ANY), out_specs=(HBM, HBM), input_output_aliases={0: 0, 1: 1},
        compiler_params=pltpu.CompilerParams(has_side_effects=SIDE_EFFECT),
    )(shard, land, send_sems, recv_sems, after)


def _gather_finish(shard, land, name):
    rows, cols = shard.shape
    half = rows // 2

    def body(in_ref, land_ref, out_ref, send_sems, recv_sems, buf, in_sems, out_sems):
        x, y, c, chips = _position()
        me = 2 * x + y
        sibling = (x, y, 1 - c)

        def slab(chip, h):
            return out_ref.at[chip, pl.ds(h * half, half), :]

        passed = [_remote(slab(2 * cx + cy, c), slab(2 * cx + cy, c), send_sems.at[j], recv_sems.at[j], sibling)
                  for j, (cx, cy) in enumerate(chips)]
        for cp in passed:
            cp.start()
        _staged_copies([(in_ref.at[pl.ds(r, n), :], out_ref.at[me, pl.ds(r, n), :]) for r, n in _row_chunks(rows)],
                       buf, in_sems, out_sems)
        for j, (cx, cy) in enumerate(chips):
            got = slab(2 * cx + cy, 1 - c)
            _remote(got, got, send_sems.at[j], recv_sems.at[j], sibling).wait_recv()
        for cp in passed:
            cp.wait_send()

    return pl.pallas_call(
        body, name=name, in_specs=[ANY, ANY], out_specs=ANY, out_shape=jax.ShapeDtypeStruct(land.shape, land.dtype),
        input_output_aliases={1: 0},
        scratch_shapes=[pltpu.SemaphoreType.DMA((3,)), pltpu.SemaphoreType.DMA((3,))] + _stage_scratch(shard.dtype, cols),
        compiler_params=pltpu.CompilerParams(vmem_limit_bytes=VMEM_LIMIT),
    )(shard, land)


def _gather_devices(block, name):
    rows, cols = block.shape

    def body(in_ref, out_ref, send_sems, recv_sems, local_sem):
        x, y, c, chips = _position()
        sibling = (x, y, 1 - c)

        def slot(px, py, pc):
            return out_ref.at[4 * px + 2 * py + pc]

        mine = pltpu.make_async_copy(in_ref, slot(x, y, c), local_sem)
        mine.start()
        first = [_remote(in_ref, slot(x, y, c), send_sems.at[0], recv_sems.at[0], sibling)]
        first += [_remote(in_ref, slot(x, y, c), send_sems.at[1 + j], recv_sems.at[1 + j], (cx, cy, c))
                  for j, (cx, cy) in enumerate(chips)]
        for cp in first:
            cp.start()
        passed = []
        for j, (cx, cy) in enumerate(chips):
            got = slot(cx, cy, c)
            _remote(got, got, send_sems.at[1 + j], recv_sems.at[1 + j], sibling).wait_recv()
            cp = _remote(got, got, send_sems.at[4 + j], recv_sems.at[4 + j], sibling)
            cp.start()
            passed.append(cp)
        got = slot(x, y, 1 - c)
        _remote(got, got, send_sems.at[0], recv_sems.at[0], sibling).wait_recv()
        for j, (cx, cy) in enumerate(chips):
            got = slot(cx, cy, 1 - c)
            _remote(got, got, send_sems.at[4 + j], recv_sems.at[4 + j], sibling).wait_recv()
        for cp in first + passed:
            cp.wait_send()
        mine.wait()

    return pl.pallas_call(
        body, name=name, in_specs=[ANY], out_specs=ANY,
        out_shape=jax.ShapeDtypeStruct((N_DEV, rows, cols), block.dtype),
        scratch_shapes=[pltpu.SemaphoreType.DMA((7,)), pltpu.SemaphoreType.DMA((7,)), pltpu.SemaphoreType.DMA],
    )(block)


def _pair_send(grads, name):
    n = len(grads)
    hs = [g.shape[2] for g in grads]
    offs = [sum(hs[:i]) for i in range(n)]
    cols = grads[0].shape[3]

    def body(*refs):
        g_refs = refs[:n]
        got_ref, send_sems, recv_sems = refs[n:]
        x, y, c, _ = _position()
        copies = [_remote(g_ref.at[:, 1 - c], got_ref.at[:, pl.ds(offs[i], hs[i]), :], send_sems.at[i], recv_sems.at[i],
                          (x, y, 1 - c)) for i, g_ref in enumerate(g_refs)]
        for cp in copies:
            cp.start()
        for cp in copies:
            cp.wait()

    return pl.pallas_call(
        body, name=name, in_specs=[ANY] * n, out_specs=ANY, out_shape=jax.ShapeDtypeStruct((N_CHIPS, sum(hs), cols), F32),
        scratch_shapes=[pltpu.SemaphoreType.DMA((n,)), pltpu.SemaphoreType.DMA((n,))],
    )(*grads)


def _pair_add(grads, got, name):
    n = len(grads)
    hs = [g.shape[2] for g in grads]
    offs = [sum(hs[:i]) for i in range(n)]
    cols = grads[0].shape[3]
    hmax = max(hs)
    units = [(i, k) for k in range(N_CHIPS) for i in range(n)]

    def body(*refs):
        g_refs = refs[:n]
        got_ref, out_ref, a_buf, b_buf, o_buf, a_sems, b_sems, o_sems = refs[n:]
        c = lax.axis_index("c")

        def loads(u):
            i, k = units[u]
            slot, rows = u % 2, pl.ds(0, hs[i])
            return (pltpu.make_async_copy(g_refs[i].at[k, c], a_buf.at[slot, rows, :], a_sems.at[slot]),
                    pltpu.make_async_copy(got_ref.at[k, pl.ds(offs[i], hs[i]), :], b_buf.at[slot, rows, :], b_sems.at[slot]))

        def store(u):
            i, k = units[u]
            return pltpu.make_async_copy(o_buf.at[u % 2, pl.ds(0, hs[i]), :], out_ref.at[k, pl.ds(offs[i], hs[i]), :],
                                         o_sems.at[u % 2])

        for cp in loads(0):
            cp.start()
        for u, (i, k) in enumerate(units):
            if u + 1 < len(units):
                for cp in loads(u + 1):
                    cp.start()
            for cp in loads(u):
                cp.wait()
            if u >= 2:
                store(u - 2).wait()
            rows = pl.ds(0, hs[i])
            o_buf[u % 2, rows, :] = (a_buf[u % 2, rows, :] + b_buf[u % 2, rows, :]).astype(BF16)
            store(u).start()
        store(len(units) - 2).wait()
        store(len(units) - 1).wait()

    return pl.pallas_call(
        body, name=name, in_specs=[ANY] * (n + 1), out_specs=ANY,
        out_shape=jax.ShapeDtypeStruct((N_CHIPS, sum(hs), cols), BF16),
        scratch_shapes=[pltpu.VMEM((2, hmax, cols), F32), pltpu.VMEM((2, hmax, cols), F32), pltpu.VMEM((2, hmax, cols), BF16),
                        pltpu.SemaphoreType.DMA((2,)), pltpu.SemaphoreType.DMA((2,)), pltpu.SemaphoreType.DMA((2,))],
        compiler_params=pltpu.CompilerParams(vmem_limit_bytes=VMEM_LIMIT),
    )(*grads, got)


def _chip_exchange(parts, name):
    _, rows, cols = parts.shape

    def body(in_ref, out_ref, send_sems, recv_sems):
        x, y, c, chips = _position()
        sent = [_remote(in_ref.at[2 * cx + cy], out_ref.at[j], send_sems.at[j], recv_sems.at[j], (cx, cy, c))
                for j, (cx, cy) in enumerate(chips)]
        for cp in sent:
            cp.start()
        for cp in sent:
            cp.wait()

    return pl.pallas_call(
        body, name=name, in_specs=[ANY], out_specs=ANY, out_shape=jax.ShapeDtypeStruct((3, rows, cols), parts.dtype),
        scratch_shapes=[pltpu.SemaphoreType.DMA((3,)), pltpu.SemaphoreType.DMA((3,))],
    )(parts)


def _chip_sum(parts, recv, chip, name):
    _, rows, cols = parts.shape
    tm = _tile(rows, 512, 16)

    def body(chip_ref, own_ref, recv_ref, o_ref):
        acc = own_ref[0].astype(F32)
        for j in range(3):
            acc = acc + recv_ref[j].astype(F32)
        o_ref[...] = acc

    return pl.pallas_call(
        body, name=name,
        grid_spec=pltpu.PrefetchScalarGridSpec(
            num_scalar_prefetch=1, grid=(rows // tm,),
            in_specs=[pl.BlockSpec((1, tm, cols), lambda i, chip_ref: (chip_ref[0], i, 0)),
                      pl.BlockSpec((3, tm, cols), lambda i, chip_ref: (0, i, 0))],
            out_specs=pl.BlockSpec((tm, cols), lambda i, chip_ref: (i, 0))),
        out_shape=jax.ShapeDtypeStruct((rows, cols), F32), compiler_params=_params("parallel"),
    )(chip, parts, recv)


def _join_unpack(mine, hs, groups, name):
    n = len(hs)
    offs = [sum(hs[:i]) for i in range(n)]
    cols = mine.shape[1]
    n_out = max(groups) + 1
    base = [2 * sum(h for h, g in zip(hs[:i], groups[:i]) if g == groups[i]) for i in range(n)]
    out_rows = [2 * sum(h for h, g in zip(hs, groups) if g == k) for k in range(n_out)]

    def body(in_ref, *refs):
        outs = refs[:n_out]
        send_sems, recv_sems, buf, in_sems, out_sems = refs[n_out:]
        x, y, c, _ = _position()
        sibling = (x, y, 1 - c)
        sent, local = [], []
        for i in range(n):
            src = in_ref.at[pl.ds(offs[i], hs[i]), :]
            here = outs[groups[i]].at[pl.ds(base[i] + c * hs[i], hs[i]), :]
            cp = _remote(src, here, send_sems.at[i], recv_sems.at[i], sibling)
            cp.start()
            sent.append(cp)
            local.append((src, here))
        _staged_copies(local, buf, in_sems, out_sems)
        for i, cp in enumerate(sent):
            there = outs[groups[i]].at[pl.ds(base[i] + (1 - c) * hs[i], hs[i]), :]
            _remote(there, there, send_sems.at[i], recv_sems.at[i], sibling).wait_recv()
            cp.wait_send()

    assert max(hs) <= STAGE_ROWS
    return pl.pallas_call(
        body, name=name, in_specs=[ANY], out_specs=[ANY] * n_out,
        out_shape=[jax.ShapeDtypeStruct((r, cols), F32) for r in out_rows],
        scratch_shapes=[pltpu.SemaphoreType.DMA((n,)), pltpu.SemaphoreType.DMA((n,))] + _stage_scratch(F32, cols),
        compiler_params=pltpu.CompilerParams(vmem_limit_bytes=VMEM_LIMIT),
    )(mine)


SMALL_ROWS = 16


def _small_rows(n):
    return -(-n // (SMALL_ROWS * LANES)) * SMALL_ROWS


def _pack_small(arrs):
    parts = []
    for a in arrs:
        flat = a.reshape(-1)
        rows = _small_rows(flat.shape[0])
        flat = jnp.pad(flat, (0, rows * LANES - flat.shape[0]))
        parts.append(flat.reshape(rows, LANES))
    return jnp.concatenate(parts, axis=0)


def _unpack_small(packed, shapes):
    out, r = [], 0
    for sh in shapes:
        n = math.prod(sh)
        cnt = _small_rows(n)
        out.append(packed[r:r + cnt].reshape(-1)[:n].reshape(sh))
        r += cnt
    return out


def _ffn_fwd(h, g_norm, w_gate_t, w_up_t, w_down, tag):
    n = _rms_fwd(h, g_norm, f"{tag}_norm")
    act, gate, up = _ffn_gate_up(n, w_gate_t, w_up_t, f"{tag}_gate_up")
    out = _matmul(act, w_down, add=h, name=f"{tag}_down")
    return out, (n, gate, up, act)


def _ffn_bwd(dh, dhb, h_in, saved, g_norm, w_gate_t, w_up_t, w_down, tag):
    n, gate, up, act = saved
    dgate, dup = _ffn_dact(dhb, w_down, gate, up, f"{tag}_dact")
    dw_down = _matmul(act, dhb, trans_a=True, name=f"{tag}_dwdown")
    dw_gate_t = _matmul(dgate, n, trans_a=True, name=f"{tag}_dwgate")
    dw_up_t = _matmul(dup, n, trans_a=True, name=f"{tag}_dwup")
    dh_in, dh_inb, dg = _dn_norm([(dgate, w_gate_t), (dup, w_up_t)], h_in, g_norm, dh, f"{tag}_dnorm")
    return dh_in, dh_inb, dg, dw_gate_t, dw_up_t, dw_down


def _local_step(x, tgt, w, big, late_weights):
    s = x.shape[0]
    tabs = _rope_tables(s)
    grads, gbig = {}, {}

    g_ev = w['ev_norm_g']
    n1 = _rms_fwd(x, g_ev, "ev_norm")
    proj0 = _matmul(n1, big['ev_w_in', 0], trans_b=True, name="ev_in")
    q0, k0, v0 = _qkv_prep_even(proj0, tabs, "ev_qkv")
    sinks = w['ev_sinks'].reshape(-1)
    o0, lse0, o0b = _attn_fwd(q0, k0, v0, sinks, max_dist=BLOCK - 1, name="ev_attn", emit_bf16=True)
    yconv, cout = _conv_fwd(proj0, w['ev_conv_w'][0], w['ev_conv_b'], w['ev_conv_ln_g'], w['ev_conv_ln_b'], "ev_conv")
    mix0 = (o0b[0], cout)
    h1 = _matmul(mix0, big['ev_w_out', 0], add=x, name="ev_out")
    big = {**big, **late_weights(h1)}

    g_f0 = w['ffn_norm_g'][0:1]
    h2, ffn0 = _ffn_fwd(h1, g_f0, big['ffn_w_gate', 0], big['ffn_w_up', 0], big['ffn_w_down', 0], "ffn0")

    g_od = w['od_norm_g']
    n3 = _rms_fwd(h2, g_od, "od_norm")
    proj1 = _matmul(n3, big['od_w_in', 0], trans_b=True, name="od_in")
    qkv = _qkv_prep_odd(proj1, tabs, "od_qkv")
    nb = len(DILATIONS)
    outs, lses = [], []
    for i, d in enumerate(DILATIONS):
        o_r, lse_r = _attn_fwd(qkv[i], qkv[nb + i], qkv[2 * nb + i], None, max_dist=BLOCK, name=f"od_attn{d}")
        outs.append(o_r)
        lses.append(lse_r)
    comb = _combine(outs, lses, "od_combine")
    c_bf16 = comb[0]
    c_fold = {1: comb[1]}
    lse_fold = {1: comb[2]}
    for i, d in enumerate(DILATIONS[1:]):
        c_fold[d], lse_fold[d] = comb[3 + 2 * i], comb[4 + 2 * i]
    w_sp = w['od_spatial_w'][0]
    sb_t = w['od_spatial_b'][0].T
    mixed, dout = _gate_fwd(proj1, w['od_sgu_ln_g'], w['od_sgu_ln_b'], w_sp, sb_t, "od_gate")
    mix1 = (c_bf16, dout)
    h3 = _matmul(mix1, big['od_w_out', 0], add=h2, name="od_out")

    g_f1 = w['ffn_norm_g'][1:2]
    h4, ffn1 = _ffn_fwd(h3, g_f1, big['ffn_w_gate', 1], big['ffn_w_up', 1], big['ffn_w_down', 1], "ffn1")

    dh4, dh4b, dg_final, loss_tile = _final_loss(h4, w['final_norm_g'].reshape(1, D_MODEL), tgt, "final")
    grads['final_norm_g'] = dg_final.reshape(D_MODEL)

    dh3, dh3b, dg_f1, gbig['ffn_w_gate', 1], gbig['ffn_w_up', 1], gbig['ffn_w_down', 1] = _ffn_bwd(
        dh4, dh4b, h3, ffn1, g_f1, big['ffn_w_gate', 1], big['ffn_w_up', 1], big['ffn_w_down', 1], "ffn1")

    dmix1 = _matmul(dh3b, big['od_w_out', 0], trans_b=True, name="od_dmix")
    gbig['od_w_out', 0] = _matmul_tn_pair(mix1[0], mix1[1], dh3b, "od_dwout")
    do_fold = dict(zip(DILATIONS[1:], _fold_dout(dmix1, "od_fold_dout")))
    do_fold[1] = dmix1[None]
    dqs, dks, dvs = [], [], []
    for i, d in enumerate(DILATIONS):
        dq_r, dk_r, dv_r = _attn_bwd(qkv[i], qkv[nb + i], qkv[2 * nb + i], do_fold[d], c_fold[d], lse_fold[d], None,
                                     max_dist=BLOCK, name=f"od_dattn{d}")
        dqs.append(dq_r)
        dks.append(dk_r)
        dvs.append(dv_r)
    dz, dg_sgu, db_sgu, dw_sp, dsb = _gate_bwd(dmix1, proj1, mixed, w['od_sgu_ln_g'], w['od_sgu_ln_b'], w_sp, "od_dgate")
    grads['od_sgu_ln_g'], grads['od_sgu_ln_b'] = dg_sgu, db_sgu
    grads['od_spatial_w'], grads['od_spatial_b'] = dw_sp[None], dsb[None]
    dproj1 = _qkv_post_odd(dqs, dks, dvs, dz, tabs, "od_dproj")
    gbig['od_w_in', 0] = _matmul(dproj1, n3, trans_a=True, name="od_dwin")
    dh2, dh2b, dg_od = _dn_norm([(dproj1, big['od_w_in', 0])], h2, g_od, dh3, "od_dnorm")
    grads['od_norm_g'] = dg_od

    dh1, dh1b, dg_f0, gbig['ffn_w_gate', 0], gbig['ffn_w_up', 0], gbig['ffn_w_down', 0] = _ffn_bwd(
        dh2, dh2b, h1, ffn0, g_f0, big['ffn_w_gate', 0], big['ffn_w_up', 0], big['ffn_w_down', 0], "ffn0")
    grads['ffn_norm_g'] = jnp.concatenate([dg_f0, dg_f1], axis=0)

    dmix0 = _matmul(dh1b, big['ev_w_out', 0], trans_b=True, name="ev_dmix")
    gbig['ev_w_out', 0] = _matmul_tn_pair(mix0[0], mix0[1], dh1b, "ev_dwout")
    dq0, dk0, dv0, dsink = _attn_bwd(q0, k0, v0, dmix0[None], o0, lse0, sinks, max_dist=BLOCK - 1, name="ev_dattn")
    grads['ev_sinks'] = dsink[:, 0, :].reshape(N_PAIRS, 2, HEAD_DIM)[:, :, 0].reshape(1, 8)
    dyc, dg_cln, db_cln, dcb = _conv_tail_bwd(dmix0, yconv, w['ev_conv_ln_g'], w['ev_conv_ln_b'], "ev_dconv_tail")
    grads['ev_conv_ln_g'], grads['ev_conv_ln_b'], grads['ev_conv_b'] = dg_cln, db_cln, dcb
    dglu, dconv_w = _conv_bwd(proj0, dyc, w['ev_conv_w'][0], "ev_dconv")
    grads['ev_conv_w'] = dconv_w[None]
    dproj0 = _qkv_post_even(dq0, dk0, dv0, dglu, tabs, "ev_dproj")
    gbig['ev_w_in', 0] = _matmul(dproj0, n1, trans_a=True, name="ev_dwin")
    dx, _, dg_ev = _dn_norm([(dproj0, big['ev_w_in', 0])], x, g_ev, dh1, "ev_dnorm")
    grads['ev_norm_g'] = dg_ev
    return loss_tile, dx, grads, gbig


def _shard_rows(w, layer, by_cols):
    return w[layer].T if by_cols else w[layer]


def kernel(x, ev_norm_g, ev_w_in, ev_sinks, ev_conv_w, ev_conv_b, ev_conv_ln_g, ev_conv_ln_b, ev_w_out, od_norm_g, od_w_in, od_sgu_ln_g, od_sgu_ln_b, od_spatial_w, od_spatial_b, od_w_out, ffn_norm_g, ffn_w_gate, ffn_w_up, ffn_w_down, final_norm_g, loss_target, m_ev_norm_g, m_ev_w_in, m_ev_sinks, m_ev_conv_w, m_ev_conv_b, m_ev_conv_ln_g, m_ev_conv_ln_b, m_ev_w_out, m_od_norm_g, m_od_w_in, m_od_sgu_ln_g, m_od_sgu_ln_b, m_od_spatial_w, m_od_spatial_b, m_od_w_out, m_ffn_norm_g, m_ffn_w_gate, m_ffn_w_up, m_ffn_w_down, m_final_norm_g, v_ev_norm_g, v_ev_w_in, v_ev_sinks, v_ev_conv_w, v_ev_conv_b, v_ev_conv_ln_g, v_ev_conv_ln_b, v_ev_w_out, v_od_norm_g, v_od_w_in, v_od_sgu_ln_g, v_od_sgu_ln_b, v_od_spatial_w, v_od_spatial_b, v_od_w_out, v_ffn_norm_g, v_ffn_w_gate, v_ffn_w_up, v_ffn_w_down, v_final_norm_g):
    given = dict(locals())
    wts = {n: given[n] for n in WEIGHTS}
    mom = {n: given["m_" + n] for n in WEIGHTS}
    var = {n: given["v_" + n] for n in WEIGHTS}
    chip = 2 * lax.axis_index("x") + lax.axis_index("y")

    shard_rows = [_shard_rows(wts[n], layer, by_cols).astype(BF16) for n, layer, by_cols in BIG]
    counts = [a.shape[0] for a in shard_rows]
    n_first = sum(n.startswith('ev_') for n, _, _ in BIG)

    def unpack(stacked, entries, cnts):
        out, r = {}, 0
        for (n, layer, _), cnt in zip(entries, cnts):
            out[n, layer] = stacked[:, r:r + cnt].reshape(N_CHIPS * cnt, D_MODEL)
            r += cnt
        return out

    first_w = _gather_chips(jnp.concatenate(shard_rows[:n_first], axis=0), "gather_weights_ev")
    big = unpack(first_w, BIG[:n_first], counts[:n_first])
    send_sems, recv_sems, late_shard, late_land, token = _gather_start(jnp.concatenate(shard_rows[n_first:], axis=0),
                                                                      first_w, "gather_weights_start")

    def late_weights(after):
        shard, land = _gather_wait(send_sems, recv_sems, late_shard, late_land, after, "gather_weights_wait")
        return unpack(_gather_finish(shard, land, "gather_weights_finish"), BIG[n_first:], counts[n_first:])

    full = {n: wts[n] for n in SMALL_REPL}
    full['ev_norm_g'] = full['ev_norm_g'] + token[0:1, 0:1]
    small_shards = [wts[n] for n in SMALL_SHARDED]
    small_shapes = [a.shape for a in small_shards]
    all_s = _gather_chips(_pack_small(small_shards), "gather_small_weights")
    per_chip = [_unpack_small(all_s[k], small_shapes) for k in range(N_CHIPS)]
    for i, n in enumerate(SMALL_SHARDED):
        full[n] = jnp.concatenate([per_chip[k][i] for k in range(N_CHIPS)], axis=-1)

    loss_tile, grad_x, grads, gbig = _local_step(x[0], loss_target[0], full, big, late_weights)
    loss = lax.psum(loss_tile[0, 0], ("x", "y", "c"))

    halves = [cnt // 2 for cnt in counts]
    split = [gbig[n, layer].reshape(N_CHIPS, 2, h, D_MODEL) for (n, layer, _), h in zip(BIG, halves)]
    got = _pair_send(split, "grad_pair_send")
    chip_part = _pair_add(split, got, "grad_pair_add")
    from_chips = _chip_exchange(chip_part, "grad_chip_exchange")
    my_half = _chip_sum(chip_part, from_chips, chip.reshape(1), "grad_chip_sum")
    groups = [BIG_NAMES.index(n) for n, _, _ in BIG]
    reduced = dict(zip(BIG_NAMES, _join_unpack(my_half, halves, groups, "grad_join_halves")))

    small_names = SMALL_REPL + SMALL_SHARDED
    small_full_shapes = [grads[n].shape for n in small_names]
    spack = _pack_small([grads[n] for n in small_names])
    s_all = _gather_devices(spack, "grad_small_gather")
    s_sum = _unpack_small(_ordered_sum(s_all, "grad_small_sum"), small_full_shapes)
    g_all = dict(zip(small_names, s_sum))
    for n in SMALL_SHARDED:
        width = wts[n].shape[-1]
        g_all[n] = lax.dynamic_slice_in_dim(g_all[n], chip * width, width, axis=g_all[n].ndim - 1)

    delta, new_m, new_v = {}, {}, {}
    for n in BIG_NAMES:
        by_cols = [bc for nn, _, bc in BIG if nn == n][0]
        layers = wts[n].shape[0]

        def as_rows(a):
            return (jnp.swapaxes(a, 1, 2) if by_cols else a).reshape(-1, D_MODEL)

        def from_rows(a):
            a = a.reshape(layers, -1, D_MODEL)
            return jnp.swapaxes(a, 1, 2) if by_cols else a

        updated = _adamw(as_rows(wts[n]), reduced[n], as_rows(mom[n]), as_rows(var[n]), f"adamw_{n}")
        g_all[n] = from_rows(reduced[n])
        delta[n], new_m[n], new_v[n] = (from_rows(a) for a in updated)
    shapes = [wts[n].shape for n in small_names]
    d_s, m_s, v_s = _adamw(*[_pack_small([src[n] for n in small_names]) for src in (wts, g_all, mom, var)], "adamw_small")
    for dst, packed in ((delta, d_s), (new_m, m_s), (new_v, v_s)):
        dst.update(zip(small_names, _unpack_small(packed, shapes)))

    return (loss, grad_x[None], *[g_all[n] for n in WEIGHTS], *[delta[n] for n in WEIGHTS],
            *[new_m[n] for n in WEIGHTS], *[new_v[n] for n in WEIGHTS])
```

```python
import math

import jax
import jax.numpy as jnp
from jax import lax
from jax.experimental import pallas as pl
from jax.experimental.pallas import tpu as pltpu

F32 = jnp.float32
BF16 = jnp.bfloat16

D_MODEL = 1024
HEAD_DIM = 64
ROT_DIM = 16
ROPE_THETA = 500000.0
RMS_EPS = 1e-6
LN_EPS = 1e-5
BLOCK = 128
CONV_WIDTH = 31
CONV_HALO = 32
CONV_ROWS = 64
D_FF = 2816
N_GROUPS = 8
ATTN_W = 512
ATTN_SCALE = HEAD_DIM ** -0.5
NEG = -1e30
DILATIONS = (1, 4, 16)

ADAM_LR = 0.001
ADAM_B1 = 0.9
ADAM_B2 = 0.999
ADAM_EPS = 1e-08
ADAM_WD = 0.01
ADAM_STEP = 10

LANES = 128
N_PAIRS = ATTN_W // LANES
VMEM_LIMIT = 56 * 1024 * 1024
MESH = pl.DeviceIdType.MESH
N_CHIPS = 4
N_DEV = 8

WEIGHTS = ['ev_norm_g', 'ev_w_in', 'ev_sinks', 'ev_conv_w', 'ev_conv_b', 'ev_conv_ln_g', 'ev_conv_ln_b', 'ev_w_out',
           'od_norm_g', 'od_w_in', 'od_sgu_ln_g', 'od_sgu_ln_b', 'od_spatial_w', 'od_spatial_b', 'od_w_out',
           'ffn_norm_g', 'ffn_w_gate', 'ffn_w_up', 'ffn_w_down', 'final_norm_g']
BIG = [('ev_w_in', 0, True), ('ev_w_out', 0, False), ('od_w_in', 0, True), ('od_w_out', 0, False),
       ('ffn_w_gate', 0, True), ('ffn_w_gate', 1, True), ('ffn_w_up', 0, True), ('ffn_w_up', 1, True),
       ('ffn_w_down', 0, False), ('ffn_w_down', 1, False)]
BIG_NAMES = ['ev_w_in', 'ev_w_out', 'od_w_in', 'od_w_out', 'ffn_w_gate', 'ffn_w_up', 'ffn_w_down']
GRAD_STAGES = ([('od_w_in', 0), ('od_w_out', 0), ('ffn_w_gate', 1), ('ffn_w_up', 1), ('ffn_w_down', 1)],
               [('ffn_w_gate', 0), ('ffn_w_up', 0), ('ffn_w_down', 0)],
               [('ev_w_in', 0), ('ev_w_out', 0)])
SMALL_SHARDED = ['ev_conv_w', 'od_norm_g', 'od_sgu_ln_g', 'od_sgu_ln_b']
SMALL_REPL = ['ev_norm_g', 'ev_sinks', 'ev_conv_b', 'ev_conv_ln_g', 'ev_conv_ln_b', 'od_spatial_w', 'od_spatial_b',
              'ffn_norm_g', 'final_norm_g']


def _tile(n, cap, mult=LANES):
    best = None
    for t in range(mult, min(n, cap) + 1, mult):
        if n % t == 0:
            best = t
    assert best is not None, (n, cap)
    return best


def _params(*sem):
    return pltpu.CompilerParams(dimension_semantics=sem, vmem_limit_bytes=VMEM_LIMIT)


def _sigmoid(x):
    return 1.0 / (1.0 + jnp.exp(-x))


def _pair_block(p):
    return slice(p * LANES, (p + 1) * LANES)


def _matmul(a, b, *, name, trans_a=False, trans_b=False, add=None, out_dtype=F32, after=None):
    parts = a if isinstance(a, (tuple, list)) else (a,)
    if trans_a:
        k, m = parts[0].shape
    else:
        m = parts[0].shape[0]
        k = sum(p.shape[1] for p in parts)
    if trans_b:
        n, k2 = b.shape
    else:
        k2, n = b.shape
    assert k == k2 and b.dtype == BF16 and all(p.dtype == BF16 for p in parts)
    tm = _tile(m, D_FF // 2 if trans_a else 512)
    tn = _tile(n, D_FF // 2)
    tk = k if k <= D_FF else _tile(k, 1024)
    nk = k // tk
    na = len(parts)
    assert na == 1 or (nk == 1 and not trans_a)
    dims = (((0 if trans_a else 1,), (1 if trans_b else 0,)), ((), ()))
    has_add = add is not None

    def body(*refs):
        a_refs, b_ref = refs[:na], refs[na]
        add_ref = refs[na + 1] if has_add else None
        o_ref = refs[na + 1 + has_add + (after is not None)]
        a_val = a_refs[0][...] if na == 1 else jnp.concatenate([r[...] for r in a_refs], axis=1)
        part = lax.dot_general(a_val, b_ref[...], dims, preferred_element_type=F32)
        if nk == 1:
            if has_add:
                part = part + add_ref[...]
            o_ref[...] = part.astype(o_ref.dtype)
            return
        acc_ref = refs[-1]
        kk = pl.program_id(2)

        @pl.when(kk == 0)
        def _():
            acc_ref[...] = part

        @pl.when(kk > 0)
        def _():
            acc_ref[...] += part

        @pl.when(kk == nk - 1)
        def _():
            res = acc_ref[...]
            if has_add:
                res = res + add_ref[...]
            o_ref[...] = res.astype(o_ref.dtype)

    if trans_a:
        a_specs = [pl.BlockSpec((tk, tm), lambda i, j, kk: (kk, i))]
    elif na == 1:
        a_specs = [pl.BlockSpec((tm, tk), lambda i, j, kk: (i, kk))]
    else:
        a_specs = [pl.BlockSpec((tm, p.shape[1]), lambda i, j, kk: (i, 0)) for p in parts]
    b_spec = pl.BlockSpec((tn, tk), lambda i, j, kk: (j, kk)) if trans_b else pl.BlockSpec((tk, tn), lambda i, j, kk: (kk, j))
    o_spec = pl.BlockSpec((tm, tn), lambda i, j, kk: (i, j))
    in_specs = a_specs + [b_spec] + ([o_spec] if has_add else [])
    operands = list(parts) + [b] + ([add] if has_add else [])
    if after is not None:
        in_specs.append(_after_spec(after))
        operands.append(after)
    return pl.pallas_call(
        body, name=name, grid=(m // tm, n // tn, nk), in_specs=in_specs, out_specs=o_spec,
        out_shape=jax.ShapeDtypeStruct((m, n), out_dtype),
        scratch_shapes=[pltpu.VMEM((tm, tn), F32)] if nk > 1 else [],
        compiler_params=_params("parallel", "parallel", "arbitrary"),
    )(*operands)


def _matmul_tn_pair(a1, a2, b, name):
    kdim, m1 = a1.shape
    m2 = a2.shape[1]
    n = b.shape[1]
    tn = _tile(n, 1024)
    tk = _tile(kdim, 1024)
    nk = kdim // tk
    dims = (((0,), (0,)), ((), ()))

    def body(a1_ref, a2_ref, b_ref, o_ref):
        kk = pl.program_id(1)
        bv = b_ref[...]
        top = lax.dot_general(a1_ref[...], bv, dims, preferred_element_type=F32)
        bot = lax.dot_general(a2_ref[...], bv, dims, preferred_element_type=F32)

        @pl.when(kk == 0)
        def _():
            o_ref[0:m1, :] = top
            o_ref[m1:, :] = bot

        @pl.when(kk > 0)
        def _():
            o_ref[0:m1, :] += top
            o_ref[m1:, :] += bot

    return pl.pallas_call(
        body, name=name, grid=(n // tn, nk),
        in_specs=[pl.BlockSpec((tk, m1), lambda j, kk: (kk, 0)), pl.BlockSpec((tk, m2), lambda j, kk: (kk, 0)),
                  pl.BlockSpec((tk, tn), lambda j, kk: (kk, j))],
        out_specs=pl.BlockSpec((m1 + m2, tn), lambda j, kk: (0, j)),
        out_shape=jax.ShapeDtypeStruct((m1 + m2, n), F32),
        compiler_params=_params("parallel", "arbitrary"),
    )(a1, a2, b)


def _ffn_gate_up(n, w_gate_t, w_up_t, name):
    m, k = n.shape
    f = w_gate_t.shape[0]
    tm, tn = _tile(m, 512), _tile(f, D_FF // 2)

    def body(n_ref, wg_ref, wu_ref, act_ref, gate_ref, up_ref):
        a = n_ref[...]
        for cols in _col_chunks(tn):
            gate = lax.dot_general(a, wg_ref[cols, :], NT, preferred_element_type=F32)
            up = lax.dot_general(a, wu_ref[cols, :], NT, preferred_element_type=F32)
            act_ref[:, cols] = (gate * _sigmoid(gate) * up).astype(BF16)
            gate_ref[:, cols] = gate.astype(BF16)
            up_ref[:, cols] = up.astype(BF16)

    wspec = pl.BlockSpec((tn, k), lambda j, i: (j, 0))
    ospec = pl.BlockSpec((tm, tn), lambda j, i: (i, j))
    return pl.pallas_call(
        body, name=name, grid=(f // tn, m // tm), in_specs=[pl.BlockSpec((tm, k), lambda j, i: (i, 0)), wspec, wspec],
        out_specs=[ospec] * 3, out_shape=[jax.ShapeDtypeStruct((m, f), BF16)] * 3,
        compiler_params=_params("parallel", "parallel"),
    )(n, w_gate_t, w_up_t)


def _col_chunks(n, width=384):
    return [slice(c, min(c + width, n)) for c in range(0, n, width)]


def _after_spec(after):
    return pl.BlockSpec(after.shape, lambda *_: (0,) * after.ndim)


def _ffn_dact(dhb, w_down, gate, up, name, after=None):
    m, k = dhb.shape
    f = w_down.shape[0]
    tm, tn = _tile(m, 512), _tile(f, D_FF // 2)

    def body(d_ref, w_ref, g_ref, u_ref, *rest):
        dg_ref, du_ref = rest[-2:]
        d = d_ref[...]
        for cols in _col_chunks(tn):
            dact = lax.dot_general(d, w_ref[cols, :], NT, preferred_element_type=F32)
            g = g_ref[:, cols].astype(F32)
            sg = _sigmoid(g)
            dg_ref[:, cols] = (dact * u_ref[:, cols].astype(F32) * sg * (1.0 + g * (1.0 - sg))).astype(BF16)
            du_ref[:, cols] = (dact * g * sg).astype(BF16)

    ospec = pl.BlockSpec((tm, tn), lambda j, i: (i, j))
    extra = [] if after is None else [after]
    return pl.pallas_call(
        body, name=name, grid=(f // tn, m // tm),
        in_specs=[pl.BlockSpec((tm, k), lambda j, i: (i, 0)), pl.BlockSpec((tn, k), lambda j, i: (j, 0)), ospec, ospec]
        + [_after_spec(a) for a in extra],
        out_specs=[ospec] * 2, out_shape=[jax.ShapeDtypeStruct((m, f), BF16)] * 2,
        compiler_params=_params("parallel", "parallel"),
    )(dhb, w_down, gate, up, *extra)


def _dn_norm(pairs, h, g, dres, name):
    m = h.shape[0]
    tm = 512
    np_ = len(pairs)

    def body(*refs):
        a_refs, b_refs = refs[:np_], refs[np_:2 * np_]
        h_ref, dres_ref, g_ref, dh_ref, dhb_ref, dg_ref = refs[2 * np_:]

        @pl.when(_first_step())
        def _():
            dg_ref[...] = jnp.zeros_like(dg_ref)

        dy = jnp.dot(a_refs[0][...], b_refs[0][...], preferred_element_type=F32)
        for a_ref, b_ref in zip(a_refs[1:], b_refs[1:]):
            dy = dy + jnp.dot(a_ref[...], b_ref[...], preferred_element_type=F32)
        x = h_ref[...]
        r = lax.rsqrt(jnp.mean(x * x, axis=-1, keepdims=True) + RMS_EPS)
        xh = x * r
        dg_ref[...] += jnp.sum(dy * xh, axis=0, keepdims=True)
        dxh = dy * g_ref[...]
        tot = dres_ref[...] + r * (dxh - xh * jnp.mean(dxh * xh, axis=-1, keepdims=True))
        dh_ref[...] = tot
        dhb_ref[...] = tot.astype(BF16)

    row = lambda w: pl.BlockSpec((tm, w), lambda i: (i, 0))
    whole = lambda a: pl.BlockSpec(a.shape, lambda i: (0, 0))
    a_list, b_list = [a for a, _ in pairs], [b for _, b in pairs]
    return pl.pallas_call(
        body, name=name, grid=(m // tm,),
        in_specs=[row(a.shape[1]) for a in a_list] + [whole(b) for b in b_list] + [row(D_MODEL), row(D_MODEL), whole(g)],
        out_specs=[row(D_MODEL), row(D_MODEL), pl.BlockSpec((1, D_MODEL), lambda i: (0, 0))],
        out_shape=[jax.ShapeDtypeStruct((m, D_MODEL), F32), jax.ShapeDtypeStruct((m, D_MODEL), BF16),
                   jax.ShapeDtypeStruct((1, D_MODEL), F32)],
        compiler_params=_params("arbitrary"),
    )(*a_list, *b_list, h, dres, g)


def _rows(body, name, tm, tiled, consts, outs, accs=()):
    s = tiled[0].shape[0]
    assert s % tm == 0
    in_specs = [pl.BlockSpec((tm, a.shape[1]), lambda i: (i, 0)) for a in tiled]
    in_specs += [pl.BlockSpec(a.shape, lambda i, nd=a.ndim: (0,) * nd) for a in consts]
    out_shape = [jax.ShapeDtypeStruct((s, c), dt) for c, dt in outs]
    out_shape += [jax.ShapeDtypeStruct(sh, dt) for sh, dt in accs]
    out_specs = [pl.BlockSpec((tm, c), lambda i: (i, 0)) for c, _ in outs]
    out_specs += [pl.BlockSpec(sh, lambda i, nd=len(sh): (0,) * nd) for sh, _ in accs]
    return pl.pallas_call(
        body, name=name, grid=(s // tm,), in_specs=in_specs, out_specs=out_specs, out_shape=out_shape,
        compiler_params=_params("arbitrary"),
    )(*tiled, *consts)


def _first_step():
    return pl.program_id(0) == 0


def _rms_fwd(h, g, name):
    def body(h_ref, g_ref, n_ref):
        x = h_ref[...]
        r = lax.rsqrt(jnp.mean(x * x, axis=-1, keepdims=True) + RMS_EPS)
        n_ref[...] = (x * r * g_ref[...]).astype(BF16)

    return _rows(body, name, 512, [h], [g], [(D_MODEL, BF16)])[0]


def _final_loss(h, g, tgt, name):
    def body(h_ref, t_ref, g_ref, dh_ref, dhb_ref, dg_ref, loss_ref):
        @pl.when(_first_step())
        def _():
            dg_ref[...] = jnp.zeros_like(dg_ref)
            loss_ref[...] = jnp.zeros_like(loss_ref)

        x = h_ref[...]
        r = lax.rsqrt(jnp.mean(x * x, axis=-1, keepdims=True) + RMS_EPS)
        xh = x * r
        gg = g_ref[...]
        e = xh * gg - t_ref[...]
        loss_ref[...] += (0.5 / D_MODEL) * jnp.sum(jnp.sum(e * e, axis=-1, keepdims=True), axis=0, keepdims=True)
        dy = e * (1.0 / D_MODEL)
        dg_ref[...] += jnp.sum(dy * xh, axis=0, keepdims=True)
        dxh = dy * gg
        dx = r * (dxh - xh * jnp.mean(dxh * xh, axis=-1, keepdims=True))
        dh_ref[...] = dx
        dhb_ref[...] = dx.astype(BF16)

    return _rows(body, name, 512, [h, tgt], [g], [(D_MODEL, F32), (D_MODEL, BF16)],
                 [((1, D_MODEL), F32), ((1, LANES), F32)])


def _rope_tables(s):
    half = ROT_DIM // 2
    inv_freq = ROPE_THETA ** (-jnp.arange(half, dtype=F32) * (2.0 / ROT_DIM))
    ang = jnp.arange(s, dtype=F32)[:, None] * inv_freq[None, :]
    cos, sin = jnp.cos(ang), jnp.sin(ang)
    rest = HEAD_DIM - ROT_DIM
    ones = jnp.ones((s, rest), F32)
    zeros = jnp.zeros((s, rest), F32)
    zh = jnp.zeros((s, half), F32)
    c_t = jnp.concatenate([cos, cos, ones], axis=1)
    a_t = jnp.concatenate([-sin, zh, zeros], axis=1)
    b_t = jnp.concatenate([zh, sin, zeros], axis=1)
    return tuple(jnp.tile(t, (1, LANES // HEAD_DIM)) for t in (c_t, a_t, b_t))


def _rot(x, c, a, b):
    w = x.shape[1]
    half = ROT_DIM // 2
    return x * c + pltpu.roll(x, w - half, 1) * a + pltpu.roll(x, half, 1) * b


def _wide(t, w):
    return t if w == LANES else jnp.tile(t, (1, w // LANES))


def _low_lanes(rows):
    return lax.broadcasted_iota(jnp.int32, (rows, LANES), 1) < HEAD_DIM


def _fold_store(x, sc_ref, out_refs):
    tm = x.shape[0]
    if any(d > 1 for d in out_refs):
        for p in range(N_PAIRS):
            sc_ref[p] = x[:, _pair_block(p)]
    for d, o_ref in out_refs.items():
        if d == 1:
            o_ref[0] = x.astype(o_ref.dtype)
            continue
        for r in range(d):
            for p in range(N_PAIRS):
                o_ref[r, :, _pair_block(p)] = sc_ref[p, pl.ds(r, tm // d, stride=d), :].astype(o_ref.dtype)


def _unfold_load(x_ref, sc_ref, d, add=False):
    n = x_ref.shape[1]
    for r in range(d):
        for p in range(N_PAIRS):
            rows = pl.ds(r, n, stride=d) if d > 1 else slice(None)
            val = x_ref[r, :, _pair_block(p)].astype(F32)
            if add:
                val = val + sc_ref[p, rows, :]
            sc_ref[p, rows, :] = val


def _folded_spec(d, tm, w=ATTN_W):
    return pl.BlockSpec((d, tm // d, w), lambda i: (0, i, 0))


def _folded_shape(s, d, dtype, w=ATTN_W):
    return jax.ShapeDtypeStruct((d, s // d, w), dtype)


def _qkv_prep_even(proj, tabs, name):
    s = proj.shape[0]
    tm = 512

    def body(p_ref, c_ref, a_ref, b_ref, q_ref, k_ref, v_ref):
        c, a, b = c_ref[...], a_ref[...], b_ref[...]
        q_ref[0] = _rot(p_ref[:, 0:ATTN_W], _wide(c, ATTN_W), _wide(a, ATTN_W), _wide(b, ATTN_W)).astype(BF16)
        lo = _low_lanes(tm)
        for src, o_ref in ((_rot(p_ref[:, 512:640], c, a, b), k_ref), (p_ref[:, 640:768], v_ref)):
            swapped = pltpu.roll(src, HEAD_DIM, 1)
            o_ref[0, :, 0:LANES] = jnp.where(lo, src, swapped).astype(BF16)
            o_ref[0, :, LANES:] = jnp.where(lo, swapped, src).astype(BF16)

    row = lambda w: pl.BlockSpec((tm, w), lambda i: (i, 0))
    return pl.pallas_call(
        body, name=name, grid=(s // tm,), in_specs=[row(proj.shape[1]), row(LANES), row(LANES), row(LANES)],
        out_specs=[_folded_spec(1, tm), _folded_spec(1, tm, 2 * LANES), _folded_spec(1, tm, 2 * LANES)],
        out_shape=[_folded_shape(s, 1, BF16), _folded_shape(s, 1, BF16, 2 * LANES), _folded_shape(s, 1, BF16, 2 * LANES)],
        compiler_params=_params("parallel"),
    )(proj, *tabs)


def _qkv_post_even(dq, dk, dv, dglu, tabs, name):
    s = dglu.shape[0]
    tm = 512

    def body(dq_ref, dk_ref, dv_ref, dr_ref, c_ref, a_ref, b_ref, o_ref):
        c, a, b = c_ref[...], -a_ref[...], -b_ref[...]
        o_ref[:, 0:ATTN_W] = _rot(dq_ref[0], _wide(c, ATTN_W), _wide(a, ATTN_W), _wide(b, ATTN_W)).astype(BF16)
        lo = _low_lanes(tm)
        merged = []
        for ref in (dk_ref, dv_ref):
            first, second = ref[0, :, 0:LANES], ref[0, :, LANES:]
            merged.append(jnp.where(lo, first + pltpu.roll(first, HEAD_DIM, 1), second + pltpu.roll(second, HEAD_DIM, 1)))
        o_ref[:, 512:640] = _rot(merged[0], c, a, b).astype(BF16)
        o_ref[:, 640:768] = merged[1].astype(BF16)
        o_ref[:, 768:] = dr_ref[...]

    row = lambda w: pl.BlockSpec((tm, w), lambda i: (i, 0))
    return pl.pallas_call(
        body, name=name, grid=(s // tm,),
        in_specs=[_folded_spec(1, tm), _folded_spec(1, tm, 2 * LANES), _folded_spec(1, tm, 2 * LANES),
                  row(dglu.shape[1]), row(LANES), row(LANES), row(LANES)],
        out_specs=row(EVEN_IN), out_shape=jax.ShapeDtypeStruct((s, EVEN_IN), BF16),
        compiler_params=_params("parallel"),
    )(dq, dk, dv, dglu, *tabs)


def _qkv_prep_odd(proj, tabs, name):
    s = proj.shape[0]
    tm = 512

    def body(p_ref, c_ref, a_ref, b_ref, *rest):
        outs, sc_ref = rest[:-1], rest[-1]
        c, a, b = (_wide(t[...], ATTN_W) for t in (c_ref, a_ref, b_ref))
        for t in range(3):
            x = p_ref[:, t * ATTN_W:(t + 1) * ATTN_W]
            if t < 2:
                x = _rot(x, c, a, b)
            _fold_store(x, sc_ref, {d: outs[t * len(DILATIONS) + i] for i, d in enumerate(DILATIONS)})

    row = lambda w: pl.BlockSpec((tm, w), lambda i: (i, 0))
    return pl.pallas_call(
        body, name=name, grid=(s // tm,), in_specs=[row(proj.shape[1]), row(LANES), row(LANES), row(LANES)],
        out_specs=[_folded_spec(d, tm) for _ in range(3) for d in DILATIONS],
        out_shape=[_folded_shape(s, d, BF16) for _ in range(3) for d in DILATIONS],
        scratch_shapes=[pltpu.VMEM((N_PAIRS, tm, LANES), F32)],
        compiler_params=_params("parallel"),
    )(proj, *tabs)


def _qkv_post_odd(dqs, dks, dvs, dz, tabs, name):
    s = dz.shape[0]
    tm = 256
    nb = len(DILATIONS)

    def body(*refs):
        groups = (refs[:nb], refs[nb:2 * nb], refs[2 * nb:3 * nb])
        dz_ref, c_ref, a_ref, b_ref, o_ref, sc_ref = refs[3 * nb:]
        c, a, b = _wide(c_ref[...], ATTN_W), _wide(-a_ref[...], ATTN_W), _wide(-b_ref[...], ATTN_W)
        for t, group in enumerate(groups):
            for i, d in enumerate(DILATIONS):
                _unfold_load(group[i], sc_ref, d, add=i > 0)
            x = jnp.concatenate([sc_ref[p] for p in range(N_PAIRS)], axis=1)
            if t < 2:
                x = _rot(x, c, a, b)
            o_ref[:, t * ATTN_W:(t + 1) * ATTN_W] = x.astype(BF16)
        o_ref[:, 3 * ATTN_W:] = dz_ref[...]

    row = lambda w: pl.BlockSpec((tm, w), lambda i: (i, 0))
    return pl.pallas_call(
        body, name=name, grid=(s // tm,),
        in_specs=[_folded_spec(d, tm) for _ in range(3) for d in DILATIONS] + [row(dz.shape[1]), row(LANES), row(LANES), row(LANES)],
        out_specs=row(ODD_IN), out_shape=jax.ShapeDtypeStruct((s, ODD_IN), BF16),
        scratch_shapes=[pltpu.VMEM((N_PAIRS, tm, LANES), F32)],
        compiler_params=_params("parallel"),
    )(*dqs, *dks, *dvs, dz, *tabs)


def _fold_dout(dmix, name):
    s = dmix.shape[0]
    tm = 512
    ds = [d for d in DILATIONS if d > 1]

    def body(d_ref, *rest):
        outs, sc_ref = rest[:-1], rest[-1]
        _fold_store(d_ref[...], sc_ref, dict(zip(ds, outs)))

    return pl.pallas_call(
        body, name=name, grid=(s // tm,), in_specs=[pl.BlockSpec((tm, ATTN_W), lambda i: (i, 0))],
        out_specs=[_folded_spec(d, tm) for d in ds], out_shape=[_folded_shape(s, d, BF16) for d in ds],
        scratch_shapes=[pltpu.VMEM((N_PAIRS, tm, LANES), F32)],
        compiler_params=_params("parallel"),
    )(dmix)


def _window(j, i, tq):
    r0 = j * tq + i * BLOCK
    start = pl.multiple_of(jnp.maximum(r0 - BLOCK, 0), BLOCK)
    return pl.ds(start, 2 * BLOCK), r0 - start


def _band_valid(offset, max_dist):
    shape = (2 * BLOCK, 2 * BLOCK)
    dist = (lax.bitwise_and(lax.broadcasted_iota(jnp.int32, shape, 0), BLOCK - 1)
            - lax.broadcasted_iota(jnp.int32, shape, 1) + offset)
    return jnp.abs(2 * dist - max_dist) <= max_dist


def _stack_heads(lo, x):
    zero = jnp.zeros_like(x)
    return jnp.concatenate([jnp.where(lo, x, zero), jnp.where(lo, zero, x)], axis=0)


def _unstack_heads(lo, x):
    return jnp.where(lo, x[:BLOCK], x[BLOCK:])


NT = (((1,), (1,)), ((), ()))
TN = (((0,), (0,)), ((), ()))


def _attn_fwd(q, k, v, sinks, *, max_dist, name, emit_bf16=False):
    d, sp, wq = q.shape
    nq, nk = wq // LANES, k.shape[2] // LANES
    kdiv = nq // nk
    tq = min(sp, 1024)
    nsub = tq // BLOCK
    has_sink = sinks is not None

    def body(*refs):
        refs = list(refs)
        sink_ref = refs.pop(0) if has_sink else None
        q_ref, k_ref, v_ref, o_ref, lse_ref = refs[:5]
        pair = pl.program_id(1)
        j = pl.program_id(2)
        lo = _low_lanes(BLOCK)
        if has_sink:
            first_head = lax.broadcasted_iota(jnp.int32, (2 * BLOCK, 1), 0) < BLOCK
            sk = jnp.where(first_head, sink_ref[2 * pair], sink_ref[2 * pair + 1])
        for i in range(nsub):
            win, offset = _window(j, i, tq)
            rows = slice(i * BLOCK, (i + 1) * BLOCK)
            kw = k_ref[0, win, :]
            vw = v_ref[0, win, :]
            s = lax.dot_general(_stack_heads(lo, q_ref[0, rows, :]), kw, NT, preferred_element_type=F32) * ATTN_SCALE
            s = jnp.where(_band_valid(offset, max_dist), s, NEG)
            m = jnp.max(s, axis=-1, keepdims=True)
            if has_sink:
                m = jnp.maximum(m, sk)
            p = jnp.exp(s - m)
            l = jnp.sum(p, axis=-1, keepdims=True)
            if has_sink:
                l = l + jnp.exp(sk - m)
            o2 = _unstack_heads(lo, jnp.dot(p.astype(BF16), vw, preferred_element_type=F32) / l)
            o_ref[0, rows, :] = o2
            lse_ref[0, rows, :] = _unstack_heads(lo, m + jnp.log(l))
            if emit_bf16:
                refs[5][0, rows, :] = o2.astype(BF16)

    qspec = pl.BlockSpec((1, tq, LANES), lambda r, p, j: (r, j, p))
    kspec = pl.BlockSpec((1, sp, LANES), lambda r, p, j: (r, 0, p // kdiv))
    in_specs = [qspec, kspec, kspec]
    operands = [q, k, v]
    if has_sink:
        in_specs = [pl.BlockSpec(memory_space=pltpu.SMEM)] + in_specs
        operands = [sinks] + operands
    out_shape = [jax.ShapeDtypeStruct(q.shape, F32), jax.ShapeDtypeStruct(q.shape, F32)]
    if emit_bf16:
        out_shape.append(jax.ShapeDtypeStruct(q.shape, BF16))
    return pl.pallas_call(
        body, name=name, grid=(d, nq, sp // tq), in_specs=in_specs, out_specs=[qspec] * len(out_shape),
        out_shape=out_shape, compiler_params=_params("parallel", "parallel", "arbitrary"),
    )(*operands)


def _attn_bwd(q, k, v, do, oo, lse, sinks, *, max_dist, name):
    d, sp, wq = q.shape
    wk = k.shape[2]
    nq, nk = wq // LANES, wk // LANES
    kdiv = nq // nk
    tq = min(sp, 1024)
    nsub = tq // BLOCK
    has_sink = sinks is not None

    def body(*refs):
        refs = list(refs)
        sink_ref = refs.pop(0) if has_sink else None
        q_ref, k_ref, v_ref, do_ref, oo_ref, lse_ref, dq_ref, dk_ref, dv_ref = refs[:9]
        pk, g, j = pl.program_id(1), pl.program_id(2), pl.program_id(3)

        @pl.when((g == 0) & (j == 0))
        def _():
            dk_ref[...] = jnp.zeros_like(dk_ref)
            dv_ref[...] = jnp.zeros_like(dv_ref)

        lo = _low_lanes(BLOCK)
        if has_sink:
            first_head = lax.broadcasted_iota(jnp.int32, (2 * BLOCK, 1), 0) < BLOCK
            pair = pk * kdiv + g
            sk = jnp.where(first_head, sink_ref[2 * pair], sink_ref[2 * pair + 1])
            sink_acc = jnp.zeros((2 * BLOCK, LANES), F32)
        for i in range(nsub):
            win, offset = _window(j, i, tq)
            rows = slice(i * BLOCK, (i + 1) * BLOCK)
            kw = k_ref[0, win, :]
            vw = v_ref[0, win, :]
            do2 = do_ref[0, rows, :].astype(F32)
            qs = _stack_heads(lo, q_ref[0, rows, :])
            dos = _stack_heads(lo, do2.astype(BF16))
            prod = do2 * oo_ref[0, rows, :]
            delta = jnp.sum(_stack_heads(lo, prod), axis=-1, keepdims=True)
            lse2 = lse_ref[0, rows, :]
            lse_swapped = pltpu.roll(lse2, HEAD_DIM, 1)
            lse_st = jnp.concatenate([jnp.where(lo, lse2, lse_swapped), jnp.where(lo, lse_swapped, lse2)], axis=0)
            s = lax.dot_general(qs, kw, NT, preferred_element_type=F32) * ATTN_SCALE
            s = jnp.where(_band_valid(offset, max_dist), s, NEG)
            p = jnp.exp(s - jnp.tile(lse_st, (1, 2)))
            dv_ref[0, win, :] += lax.dot_general(p.astype(BF16), dos, TN, preferred_element_type=F32)
            dp = lax.dot_general(dos, vw, NT, preferred_element_type=F32)
            ds = (p * (dp - delta) * ATTN_SCALE).astype(BF16)
            dq_ref[0, rows, :] = _unstack_heads(lo, jnp.dot(ds, kw, preferred_element_type=F32))
            dk_ref[0, win, :] += lax.dot_general(ds, qs, TN, preferred_element_type=F32)
            if has_sink:
                sink_acc = sink_acc - jnp.exp(sk - lse_st) * delta
        if has_sink:
            dsink_ref = refs[9]

            @pl.when(j == 0)
            def _():
                dsink_ref[...] = jnp.zeros_like(dsink_ref)

            dsink_ref[0] += jnp.where(lo[0:1], jnp.sum(sink_acc[:BLOCK], axis=0, keepdims=True),
                                      jnp.sum(sink_acc[BLOCK:], axis=0, keepdims=True))

    def qmap(r, pk, g, j):
        return (r, j, pk * kdiv + g)

    def kmap(r, pk, g, j):
        return (r, 0, pk)

    qspec = pl.BlockSpec((1, tq, LANES), qmap)
    kspec = pl.BlockSpec((1, sp, LANES), kmap)
    in_specs = [qspec, kspec, kspec, qspec, qspec, qspec]
    operands = [q, k, v, do, oo, lse]
    out_specs = [qspec, kspec, kspec]
    out_shape = [jax.ShapeDtypeStruct((d, sp, wq), F32), jax.ShapeDtypeStruct((d, sp, wk), F32),
                 jax.ShapeDtypeStruct((d, sp, wk), F32)]
    if has_sink:
        in_specs = [pl.BlockSpec(memory_space=pltpu.SMEM)] + in_specs
        operands = [sinks] + operands
        out_specs.append(pl.BlockSpec((1, 1, LANES), lambda r, pk, g, j: (pk * kdiv + g, 0, 0)))
        out_shape.append(jax.ShapeDtypeStruct((nq, 1, LANES), F32))
    return pl.pallas_call(
        body, name=name, grid=(d, nk, kdiv, sp // tq), in_specs=in_specs, out_specs=out_specs, out_shape=out_shape,
        compiler_params=_params("parallel", "parallel", "arbitrary", "arbitrary"),
    )(*operands)


def _combine(outs, lses, name):
    s = outs[0].shape[1]
    tm = 512
    nb = len(DILATIONS)
    ds = [d for d in DILATIONS if d > 1]

    def body(*refs):
        o_refs, l_refs = refs[:nb], refs[nb:2 * nb]
        cb_ref, c_ref, lse_ref = refs[2 * nb:2 * nb + 3]
        folded = refs[2 * nb + 3:2 * nb + 3 + 2 * len(ds)]
        scratch = refs[2 * nb + 3 + 2 * len(ds):]
        so = {1: None}
        sl = {1: None}
        for i, d in enumerate(ds):
            so[d], sl[d] = scratch[2 * i], scratch[2 * i + 1]
            _unfold_load(o_refs[1 + i], so[d], d)
            _unfold_load(l_refs[1 + i], sl[d], d)
        for p in range(N_PAIRS):
            pb = _pair_block(p)
            ls = [l_refs[0][0, :, pb]] + [sl[d][p] for d in ds]
            os_ = [o_refs[0][0, :, pb]] + [so[d][p] for d in ds]
            m = ls[0]
            for t in ls[1:]:
                m = jnp.maximum(m, t)
            ws = [jnp.exp(t - m) for t in ls]
            tot = ws[0]
            for t in ws[1:]:
                tot = tot + t
            acc = ws[0] * os_[0]
            for w, o in zip(ws[1:], os_[1:]):
                acc = acc + w * o
            cmix = acc / tot
            lse = m + jnp.log(tot)
            cb_ref[:, pb] = cmix.astype(BF16)
            c_ref[0, :, pb] = cmix
            lse_ref[0, :, pb] = lse
            so[ds[0]][p] = cmix
            sl[ds[0]][p] = lse
        for i, d in enumerate(ds):
            for r in range(d):
                for p in range(N_PAIRS):
                    rows = pl.ds(r, tm // d, stride=d)
                    folded[2 * i][r, :, _pair_block(p)] = so[ds[0]][p, rows, :]
                    folded[2 * i + 1][r, :, _pair_block(p)] = sl[ds[0]][p, rows, :]

    in_specs = [_folded_spec(d, tm) for _ in range(2) for d in DILATIONS]
    out_specs = [pl.BlockSpec((tm, ATTN_W), lambda i: (i, 0)), _folded_spec(1, tm), _folded_spec(1, tm)]
    out_shape = [jax.ShapeDtypeStruct((s, ATTN_W), BF16), _folded_shape(s, 1, F32), _folded_shape(s, 1, F32)]
    for d in ds:
        out_specs += [_folded_spec(d, tm)] * 2
        out_shape += [_folded_shape(s, d, F32)] * 2
    return pl.pallas_call(
        body, name=name, grid=(s // tm,), in_specs=in_specs, out_specs=out_specs, out_shape=out_shape,
        scratch_shapes=[pltpu.VMEM((N_PAIRS, tm, LANES), F32)] * (2 * len(ds)),
        compiler_params=_params("parallel"),
    )(*outs, *lses)


GLU_A = slice(768, 1280)
GLU_B = slice(1280, 1792)
EVEN_IN = 1792
ODD_IN = 2560
CONV_CH = 512


def _conv_fwd(proj, w, b, ln_g, ln_b, name):
    s = proj.shape[0]
    tm = 512
    nh = tm // CONV_HALO
    lead = CONV_HALO - (CONV_WIDTH - 1)

    def body(p_ref, ph_ref, w_ref, b_ref, g_ref, bb_ref, y_ref, o_ref, xf_ref):
        xf_ref[CONV_HALO:, :] = p_ref[:, GLU_A] * _sigmoid(p_ref[:, GLU_B])
        hist = ph_ref[:, GLU_A] * _sigmoid(ph_ref[:, GLU_B])
        xf_ref[0:CONV_HALO, :] = jnp.where(pl.program_id(0) > 0, hist, 0.0)
        for c0 in range(0, tm, CONV_ROWS):
            acc = jnp.zeros((CONV_ROWS, CONV_CH), F32) + b_ref[...]
            for j in range(CONV_WIDTH):
                acc = acc + xf_ref[pl.ds(lead + j + c0, CONV_ROWS), :] * w_ref[j:j + 1, :]
            y_ref[c0:c0 + CONV_ROWS, :] = acc
            mu = jnp.mean(acc, axis=-1, keepdims=True)
            xc = acc - mu
            var = jnp.mean(xc * xc, axis=-1, keepdims=True)
            zz = xc * lax.rsqrt(var + LN_EPS) * g_ref[...] + bb_ref[...]
            o_ref[c0:c0 + CONV_ROWS, :] = (zz * _sigmoid(zz)).astype(BF16)

    def const(a):
        return pl.BlockSpec(a.shape, lambda i: (0, 0))

    return pl.pallas_call(
        body, name=name, grid=(s // tm,),
        in_specs=[pl.BlockSpec((tm, EVEN_IN), lambda i: (i, 0)),
                  pl.BlockSpec((CONV_HALO, EVEN_IN), lambda i: (jnp.maximum(i * nh - 1, 0), 0)),
                  const(w), const(b), const(ln_g), const(ln_b)],
        out_specs=[pl.BlockSpec((tm, CONV_CH), lambda i: (i, 0)), pl.BlockSpec((tm, CONV_CH), lambda i: (i, 0))],
        out_shape=[jax.ShapeDtypeStruct((s, CONV_CH), F32), jax.ShapeDtypeStruct((s, CONV_CH), BF16)],
        scratch_shapes=[pltpu.VMEM((tm + CONV_HALO, CONV_CH), F32)],
        compiler_params=_params("arbitrary"),
    )(proj, proj, w, b, ln_g, ln_b)


def _conv_tail_bwd(dmix, yconv, ln_g, ln_b, name):
    def body(d_ref, y_ref, g_ref, b_ref, dy_ref, dg_ref, db_ref, dcb_ref):
        @pl.when(_first_step())
        def _():
            dg_ref[...] = jnp.zeros_like(dg_ref)
            db_ref[...] = jnp.zeros_like(db_ref)
            dcb_ref[...] = jnp.zeros_like(dcb_ref)

        y = y_ref[...]
        g = g_ref[...]
        mu = jnp.mean(y, axis=-1, keepdims=True)
        xc = y - mu
        rstd = lax.rsqrt(jnp.mean(xc * xc, axis=-1, keepdims=True) + LN_EPS)
        xh = xc * rstd
        zz = xh * g + b_ref[...]
        sg = _sigmoid(zz)
        dzz = d_ref[:, CONV_CH:] * sg * (1.0 + zz * (1.0 - sg))
        dg_ref[...] += jnp.sum(dzz * xh, axis=0, keepdims=True)
        db_ref[...] += jnp.sum(dzz, axis=0, keepdims=True)
        dxh = dzz * g
        dy = rstd * (dxh - jnp.mean(dxh, axis=-1, keepdims=True) - xh * jnp.mean(dxh * xh, axis=-1, keepdims=True))
        dcb_ref[...] += jnp.sum(dy, axis=0, keepdims=True)
        dy_ref[...] = dy

    vec = ((1, CONV_CH), F32)
    return _rows(body, name, 512, [dmix, yconv], [ln_g, ln_b], [(CONV_CH, F32)], [vec, vec, vec])


def _conv_bwd(proj, dy, w, name):
    s = proj.shape[0]
    tm = 512
    nh = tm // CONV_HALO
    nsteps = s // tm
    lead = CONV_HALO - (CONV_WIDTH - 1)

    def body(p_ref, ph_ref, dy_ref, dyn_ref, w_ref, dglu_ref, dw_ref, xf_ref, dyf_ref):
        i = pl.program_id(0)

        @pl.when(i == 0)
        def _():
            dw_ref[...] = jnp.zeros_like(dw_ref)

        ga = p_ref[:, GLU_A]
        sgb = _sigmoid(p_ref[:, GLU_B])
        xf_ref[CONV_HALO:, :] = ga * sgb
        hist = ph_ref[:, GLU_A] * _sigmoid(ph_ref[:, GLU_B])
        xf_ref[0:CONV_HALO, :] = jnp.where(i > 0, hist, 0.0)
        dyt = dy_ref[...]
        dyf_ref[0:tm, :] = dyt
        dyf_ref[tm:, :] = jnp.where(i < nsteps - 1, dyn_ref[...], 0.0)
        for c0 in range(0, tm, CONV_ROWS):
            rows = slice(c0, c0 + CONV_ROWS)
            acc = jnp.zeros((CONV_ROWS, CONV_CH), F32)
            for j in range(CONV_WIDTH):
                acc = acc + dyf_ref[pl.ds(CONV_WIDTH - 1 - j + c0, CONV_ROWS), :] * w_ref[j:j + 1, :]
            a_c, s_c = ga[rows, :], sgb[rows, :]
            dglu_ref[rows, 0:CONV_CH] = (acc * s_c).astype(BF16)
            dglu_ref[rows, CONV_CH:] = (acc * a_c * s_c * (1.0 - s_c)).astype(BF16)
        for j in range(CONV_WIDTH):
            part = jnp.zeros((8, CONV_CH), F32)
            for c0 in range(0, tm, CONV_ROWS):
                prod = dy_ref[c0:c0 + CONV_ROWS, :] * xf_ref[pl.ds(lead + j + c0, CONV_ROWS), :]
                part = part + jnp.sum(prod.reshape(CONV_ROWS // 8, 8, CONV_CH), axis=0)
            dw_ref[j:j + 1, :] += jnp.sum(part, axis=0, keepdims=True)

    return pl.pallas_call(
        body, name=name, grid=(nsteps,),
        in_specs=[pl.BlockSpec((tm, EVEN_IN), lambda i: (i, 0)),
                  pl.BlockSpec((CONV_HALO, EVEN_IN), lambda i: (jnp.maximum(i * nh - 1, 0), 0)),
                  pl.BlockSpec((tm, CONV_CH), lambda i: (i, 0)),
                  pl.BlockSpec((CONV_HALO, CONV_CH), lambda i: (jnp.minimum((i + 1) * nh, s // CONV_HALO - 1), 0)),
                  pl.BlockSpec(w.shape, lambda i: (0, 0))],
        out_specs=[pl.BlockSpec((tm, 2 * CONV_CH), lambda i: (i, 0)), pl.BlockSpec(w.shape, lambda i: (0, 0))],
        out_shape=[jax.ShapeDtypeStruct((s, 2 * CONV_CH), BF16), jax.ShapeDtypeStruct(w.shape, F32)],
        scratch_shapes=[pltpu.VMEM((tm + CONV_HALO, CONV_CH), F32), pltpu.VMEM((tm + CONV_HALO, CONV_CH), F32)],
        compiler_params=_params("arbitrary"),
    )(proj, proj, dy, dy, w)


GATE_Z = slice(1536, 2560)
D_CH = 512
GELU_C = math.sqrt(2.0 / math.pi)
GELU_K = 0.044715


def _gelu_parts(z):
    t = jnp.tanh(GELU_C * (z + GELU_K * z * z * z))
    return 0.5 * z * (1.0 + t), t


def _lane_group(rows):
    return lax.broadcasted_iota(jnp.int32, (rows, D_CH), 1) // HEAD_DIM


def _tril_mask():
    return lax.broadcasted_iota(jnp.int32, (BLOCK, BLOCK), 0) >= lax.broadcasted_iota(jnp.int32, (BLOCK, BLOCK), 1)


def _layer_norm_parts(x):
    mu = jnp.mean(x, axis=-1, keepdims=True)
    xc = x - mu
    rstd = lax.rsqrt(jnp.mean(xc * xc, axis=-1, keepdims=True) + LN_EPS)
    return xc * rstd, rstd


def _gate_fwd(proj, ln_g, ln_b, w_sp, sb_t, name):
    tm = 512

    def body(p_ref, g_ref, b_ref, w_ref, sb_ref, mixed_ref, out_ref):
        zz, _ = _gelu_parts(p_ref[:, GATE_Z])
        u = zz[:, :D_CH]
        xh, _ = _layer_norm_parts(zz[:, D_CH:])
        gn = (xh * g_ref[...] + b_ref[...]).astype(BF16)
        grp = _lane_group(BLOCK)
        tri = _tril_mask()
        ws = [jnp.where(tri, w_ref[gi], 0.0).astype(BF16) for gi in range(N_GROUPS)]
        bias = jnp.zeros((BLOCK, D_CH), F32)
        for gi in range(N_GROUPS):
            bias = jnp.where(grp == gi, sb_ref[:, gi:gi + 1], bias)
        for ch in range(tm // BLOCK):
            rows = slice(ch * BLOCK, (ch + 1) * BLOCK)
            gc = gn[rows, :]
            mixed = bias
            for gi in range(N_GROUPS):
                r = jnp.dot(ws[gi], gc, preferred_element_type=F32)
                mixed = jnp.where(grp == gi, r + bias, mixed)
            mixed_ref[rows, :] = mixed
            out_ref[rows, :] = (u[rows, :] * mixed).astype(BF16)

    return _rows(body, name, tm, [proj], [ln_g, ln_b, w_sp, sb_t], [(D_CH, F32), (D_CH, BF16)])


def _gate_bwd(dmix, proj, mixed, ln_g, ln_b, w_sp, name):
    tm = 512

    def body(d_ref, p_ref, m_ref, g_ref, b_ref, w_ref, dz_ref, dg_ref, db_ref, dw_ref, dsb_ref, dgn_ref):
        @pl.when(_first_step())
        def _():
            dg_ref[...] = jnp.zeros_like(dg_ref)
            db_ref[...] = jnp.zeros_like(db_ref)
            dw_ref[...] = jnp.zeros_like(dw_ref)
            dsb_ref[...] = jnp.zeros_like(dsb_ref)

        z = p_ref[:, GATE_Z]
        zz, t = _gelu_parts(z)
        u = zz[:, :D_CH]
        xh, rstd = _layer_norm_parts(zz[:, D_CH:])
        g = g_ref[...]
        gn = (xh * g + b_ref[...]).astype(BF16)
        dd = d_ref[:, D_CH:]
        du = dd * m_ref[...]
        dm = dd * u
        grp = _lane_group(BLOCK)
        tri = _tril_mask()
        ws = [jnp.where(tri, w_ref[gi], 0.0).astype(BF16) for gi in range(N_GROUPS)]
        gsel = (lax.broadcasted_iota(jnp.int32, (N_GROUPS, D_CH), 1) // HEAD_DIM
                == lax.broadcasted_iota(jnp.int32, (N_GROUPS, D_CH), 0)).astype(F32)
        for ch in range(tm // BLOCK):
            rows = slice(ch * BLOCK, (ch + 1) * BLOCK)
            dmc = dm[rows, :]
            dmb = dmc.astype(BF16)
            gc = gn[rows, :]
            dgn = jnp.zeros((BLOCK, D_CH), F32)
            for gi in range(N_GROUPS):
                r = lax.dot_general(ws[gi], dmb, TN, preferred_element_type=F32)
                dgn = jnp.where(grp == gi, r, dgn)
                dmg = jnp.where(grp == gi, dmb, jnp.zeros_like(dmb))
                dwg = lax.dot_general(dmg, gc, NT, preferred_element_type=F32)
                dw_ref[gi] += jnp.where(tri, dwg, 0.0)
            dsb_ref[...] += lax.dot_general(gsel, dmc, NT, preferred_element_type=F32, precision=lax.Precision.HIGHEST)
            dgn_ref[rows, :] = dgn
        dgn = dgn_ref[...]
        db_ref[...] += jnp.sum(dgn, axis=0, keepdims=True)
        dg_ref[...] += jnp.sum(dgn * xh, axis=0, keepdims=True)
        dxh = dgn * g
        dgp = rstd * (dxh - jnp.mean(dxh, axis=-1, keepdims=True) - xh * jnp.mean(dxh * xh, axis=-1, keepdims=True))
        dgelu = 0.5 * (1.0 + t) + 0.5 * z * (1.0 - t * t) * GELU_C * (1.0 + 3.0 * GELU_K * z * z)
        dz_ref[:, 0:D_CH] = (du * dgelu[:, :D_CH]).astype(BF16)
        dz_ref[:, D_CH:] = (dgp * dgelu[:, D_CH:]).astype(BF16)

    s = proj.shape[0]
    tiled = [dmix, proj, mixed]
    consts = [ln_g, ln_b, w_sp]
    in_specs = [pl.BlockSpec((tm, a.shape[1]), lambda i: (i, 0)) for a in tiled]
    in_specs += [pl.BlockSpec(a.shape, lambda i, nd=a.ndim: (0,) * nd) for a in consts]
    vec = (1, D_CH)
    acc_shapes = [vec, vec, w_sp.shape, (N_GROUPS, BLOCK)]
    return pl.pallas_call(
        body, name=name, grid=(s // tm,), in_specs=in_specs,
        out_specs=[pl.BlockSpec((tm, 2 * D_CH), lambda i: (i, 0))]
        + [pl.BlockSpec(sh, lambda i, nd=len(sh): (0,) * nd) for sh in acc_shapes],
        out_shape=[jax.ShapeDtypeStruct((s, 2 * D_CH), BF16)] + [jax.ShapeDtypeStruct(sh, F32) for sh in acc_shapes],
        scratch_shapes=[pltpu.VMEM((tm, D_CH), F32)],
        compiler_params=_params("arbitrary"),
    )(*tiled, *consts)


def _adam_update(w, g, m, v):
    nm = ADAM_B1 * m + (1.0 - ADAM_B1) * g
    nv = ADAM_B2 * v + (1.0 - ADAM_B2) * (g * g)
    m_hat = nm / (1.0 - ADAM_B1 ** ADAM_STEP)
    v_hat = nv / (1.0 - ADAM_B2 ** ADAM_STEP)
    return -ADAM_LR * (m_hat / (jnp.sqrt(v_hat) + ADAM_EPS) + ADAM_WD * w), nm, nv


def _adamw(w, g, m, v, name):
    rows, cols = w.shape
    tm = _tile(rows, 512, 8)

    def body(w_ref, g_ref, m_ref, v_ref, d_ref, nm_ref, nv_ref):
        d_ref[...], nm_ref[...], nv_ref[...] = _adam_update(w_ref[...], g_ref[...], m_ref[...], v_ref[...])

    return _rows(body, name, tm, [w, g, m, v], [], [(cols, F32)] * 3)


def _ordered_sum(parts, name):
    n, rows, cols = parts.shape
    tm = _tile(rows, 512, 16 if parts.dtype == BF16 else 8)

    def body(p_ref, o_ref):
        acc = p_ref[0].astype(F32)
        for k in range(1, n):
            acc = acc + p_ref[k].astype(F32)
        o_ref[...] = acc

    return pl.pallas_call(body, name=name, grid=(rows // tm,),
                          in_specs=[pl.BlockSpec((n, tm, cols), lambda i: (0, i, 0))],
                          out_specs=pl.BlockSpec((tm, cols), lambda i: (i, 0)),
                          out_shape=jax.ShapeDtypeStruct((rows, cols), F32), compiler_params=_params("parallel"))(parts)


ANY = pl.BlockSpec(memory_space=pl.ANY)


def _position():
    x, y, c = lax.axis_index("x"), lax.axis_index("y"), lax.axis_index("c")
    other_chips = [(1 - x, y), (x, 1 - y), (1 - x, 1 - y)]
    return x, y, c, other_chips


def _remote(src, dst, send_sem, recv_sem, to):
    return pltpu.make_async_remote_copy(src_ref=src, dst_ref=dst, send_sem=send_sem, recv_sem=recv_sem,
                                        device_id=to, device_id_type=MESH)


STAGE_ROWS = 736


def _staged_copies(copies, buf, in_sems, out_sems):
    n = len(copies)

    def into(u):
        src = copies[u][0]
        return pltpu.make_async_copy(src, buf.at[u % 2, pl.ds(0, src.shape[0]), :], in_sems.at[u % 2])

    def out_of(u):
        dst = copies[u][1]
        return pltpu.make_async_copy(buf.at[u % 2, pl.ds(0, dst.shape[0]), :], dst, out_sems.at[u % 2])

    into(0).start()
    for u in range(n):
        into(u).wait()
        out_of(u).start()
        if u + 1 < n:
            if u >= 1:
                out_of(u - 1).wait()
            into(u + 1).start()
    if n >= 2:
        out_of(n - 2).wait()
    out_of(n - 1).wait()


def _stage_scratch(dtype, cols):
    return [pltpu.VMEM((2, STAGE_ROWS, cols), dtype), pltpu.SemaphoreType.DMA((2,)), pltpu.SemaphoreType.DMA((2,))]


def _row_chunks(rows):
    return [(r, min(STAGE_ROWS, rows - r)) for r in range(0, rows, STAGE_ROWS)]


def _gather_chips(shard, name):
    rows, cols = shard.shape
    half = rows // 2

    def body(in_ref, out_ref, send_sems, recv_sems, buf, in_sems, out_sems):
        x, y, c, chips = _position()
        me = 2 * x + y
        sibling = (x, y, 1 - c)

        def slab(chip, h):
            return out_ref.at[chip, pl.ds(h * half, half), :]

        first = [_remote(in_ref.at[pl.ds(c * half, half), :], slab(me, c), send_sems.at[j], recv_sems.at[j], (cx, cy, c))
                 for j, (cx, cy) in enumerate(chips)]
        for cp in first:
            cp.start()
        _staged_copies([(in_ref.at[pl.ds(r, n), :], out_ref.at[me, pl.ds(r, n), :]) for r, n in _row_chunks(rows)],
                       buf, in_sems, out_sems)
        passed = []
        for j, (cx, cy) in enumerate(chips):
            got = slab(2 * cx + cy, c)
            _remote(got, got, send_sems.at[j], recv_sems.at[j], sibling).wait_recv()
            cp = _remote(got, got, send_sems.at[3 + j], recv_sems.at[3 + j], sibling)
            cp.start()
            passed.append(cp)
        for j, (cx, cy) in enumerate(chips):
            got = slab(2 * cx + cy, 1 - c)
            _remote(got, got, send_sems.at[3 + j], recv_sems.at[3 + j], sibling).wait_recv()
        for cp in first + passed:
            cp.wait_send()

    return pl.pallas_call(
        body, name=name, in_specs=[ANY], out_specs=ANY,
        out_shape=jax.ShapeDtypeStruct((N_CHIPS, rows, cols), shard.dtype),
        scratch_shapes=[pltpu.SemaphoreType.DMA((6,)), pltpu.SemaphoreType.DMA((6,))] + _stage_scratch(shard.dtype, cols),
        compiler_params=pltpu.CompilerParams(vmem_limit_bytes=VMEM_LIMIT),
    )(shard)


HBM = pl.BlockSpec(memory_space=pltpu.HBM)
SEM = pl.BlockSpec(memory_space=pltpu.SEMAPHORE)
SIDE_EFFECT = pltpu.SideEffectType.DATAFLOW_SIDE_EFFECTING


def _ici_copies(in_ref, land_ref, send_sems, recv_sems, half):
    x, y, c, chips = _position()
    mine = pl.ds(c * half, half)
    sends = [_remote(in_ref.at[mine, :], land_ref.at[2 * x + y, mine, :], send_sems.at[j], recv_sems.at[j], (cx, cy, c))
             for j, (cx, cy) in enumerate(chips)]
    arrivals = [_remote(in_ref.at[mine, :], land_ref.at[2 * cx + cy, mine, :], send_sems.at[j], recv_sems.at[j], (cx, cy, c))
                for j, (cx, cy) in enumerate(chips)]
    return sends, arrivals


def _gather_start(shard, after, name):
    rows, cols = shard.shape

    def body(in_ref, land_ref, after_ref, send_sems, recv_sems, in_thru, land_thru, token):
        sends, _ = _ici_copies(in_ref, land_ref, send_sems, recv_sems, rows // 2)
        for cp in sends:
            cp.start()
        token[...] = jnp.zeros_like(token)

    land = lax.empty((N_CHIPS, rows, cols), shard.dtype)
    return pl.pallas_call(
        body, name=name,
        out_shape=(pltpu.SemaphoreType.DMA((3,)), pltpu.SemaphoreType.DMA((3,)), pltpu.HBM(shard.shape, shard.dtype),
                   pltpu.HBM(land.shape, land.dtype), jax.ShapeDtypeStruct((8, LANES), F32)),
        in_specs=(HBM, HBM, ANY), out_specs=(SEM, SEM, HBM, HBM, pl.BlockSpec(memory_space=pltpu.VMEM)),
        input_output_aliases={0: 2, 1: 3},
        compiler_params=pltpu.CompilerParams(has_side_effects=SIDE_EFFECT),
    )(pltpu.with_memory_space_constraint(shard, pltpu.HBM), pltpu.with_memory_space_constraint(land, pltpu.HBM), after)


def _gather_wait(send_sems, recv_sems, shard, land, after, name):
    rows = shard.shape[0]

    def body(in_ref, land_ref, send_sems, recv_sems, after_ref, in_out, land_out):
        sends, arrivals = _ici_copies(in_ref, land_ref, send_sems, recv_sems, rows // 2)
        for cp in sends:
            cp.wait_send()
        for cp in arrivals:
            cp.wait_recv()

    return pl.pallas_call(
        body, name=name, out_shape=(pltpu.HBM(shard.shape, shard.dtype), pltpu.HBM(land.shape, land.dtype)),
        in_specs=(HBM, HBM, SEM, SEM, ANY), out_specs=(HBM, HBM), input_output_aliases={0: 0, 1: 1},
        compiler_params=pltpu.CompilerParams(has_side_effects=SIDE_EFFECT),
    )(shard, land, send_sems, recv_sems, after)


def _gather_finish(shard, land, name):
    rows, cols = shard.shape
    half = rows // 2

    def body(in_ref, land_ref, out_ref, send_sems, recv_sems, buf, in_sems, out_sems):
        x, y, c, chips = _position()
        me = 2 * x + y
        sibling = (x, y, 1 - c)

        def slab(chip, h):
            return out_ref.at[chip, pl.ds(h * half, half), :]

        passed = [_remote(slab(2 * cx + cy, c), slab(2 * cx + cy, c), send_sems.at[j], recv_sems.at[j], sibling)
                  for j, (cx, cy) in enumerate(chips)]
        for cp in passed:
            cp.start()
        _staged_copies([(in_ref.at[pl.ds(r, n), :], out_ref.at[me, pl.ds(r, n), :]) for r, n in _row_chunks(rows)],
                       buf, in_sems, out_sems)
        for j, (cx, cy) in enumerate(chips):
            got = slab(2 * cx + cy, 1 - c)
            _remote(got, got, send_sems.at[j], recv_sems.at[j], sibling).wait_recv()
        for cp in passed:
            cp.wait_send()

    return pl.pallas_call(
        body, name=name, in_specs=[ANY, ANY], out_specs=ANY, out_shape=jax.ShapeDtypeStruct(land.shape, land.dtype),
        input_output_aliases={1: 0},
        scratch_shapes=[pltpu.SemaphoreType.DMA((3,)), pltpu.SemaphoreType.DMA((3,))] + _stage_scratch(shard.dtype, cols),
        compiler_params=pltpu.CompilerParams(vmem_limit_bytes=VMEM_LIMIT),
    )(shard, land)


def _gather_devices(block, name):
    rows, cols = block.shape

    def body(in_ref, out_ref, send_sems, recv_sems, local_sem):
        x, y, c, chips = _position()
        sibling = (x, y, 1 - c)

        def slot(px, py, pc):
            return out_ref.at[4 * px + 2 * py + pc]

        mine = pltpu.make_async_copy(in_ref, slot(x, y, c), local_sem)
        mine.start()
        first = [_remote(in_ref, slot(x, y, c), send_sems.at[0], recv_sems.at[0], sibling)]
        first += [_remote(in_ref, slot(x, y, c), send_sems.at[1 + j], recv_sems.at[1 + j], (cx, cy, c))
                  for j, (cx, cy) in enumerate(chips)]
        for cp in first:
            cp.start()
        passed = []
        for j, (cx, cy) in enumerate(chips):
            got = slot(cx, cy, c)
            _remote(got, got, send_sems.at[1 + j], recv_sems.at[1 + j], sibling).wait_recv()
            cp = _remote(got, got, send_sems.at[4 + j], recv_sems.at[4 + j], sibling)
            cp.start()
            passed.append(cp)
        got = slot(x, y, 1 - c)
        _remote(got, got, send_sems.at[0], recv_sems.at[0], sibling).wait_recv()
        for j, (cx, cy) in enumerate(chips):
            got = slot(cx, cy, 1 - c)
            _remote(got, got, send_sems.at[4 + j], recv_sems.at[4 + j], sibling).wait_recv()
        for cp in first + passed:
            cp.wait_send()
        mine.wait()

    return pl.pallas_call(
        body, name=name, in_specs=[ANY], out_specs=ANY,
        out_shape=jax.ShapeDtypeStruct((N_DEV, rows, cols), block.dtype),
        scratch_shapes=[pltpu.SemaphoreType.DMA((7,)), pltpu.SemaphoreType.DMA((7,)), pltpu.SemaphoreType.DMA],
    )(block)


def _pair_send(grads, name):
    n = len(grads)
    hs = [g.shape[2] for g in grads]
    offs = [sum(hs[:i]) for i in range(n)]
    cols = grads[0].shape[3]

    def body(*refs):
        g_refs = refs[:n]
        got_ref, send_sems, recv_sems = refs[n:]
        x, y, c, _ = _position()
        copies = [_remote(g_ref.at[:, 1 - c], got_ref.at[:, pl.ds(offs[i], hs[i]), :], send_sems.at[i], recv_sems.at[i],
                          (x, y, 1 - c)) for i, g_ref in enumerate(g_refs)]
        for cp in copies:
            cp.start()
        for cp in copies:
            cp.wait()

    return pl.pallas_call(
        body, name=name, in_specs=[ANY] * n, out_specs=ANY, out_shape=jax.ShapeDtypeStruct((N_CHIPS, sum(hs), cols), F32),
        scratch_shapes=[pltpu.SemaphoreType.DMA((n,)), pltpu.SemaphoreType.DMA((n,))],
    )(*grads)


def _pair_add(grads, got, name):
    n = len(grads)
    hs = [g.shape[2] for g in grads]
    offs = [sum(hs[:i]) for i in range(n)]
    cols = grads[0].shape[3]
    hmax = max(hs)
    units = [(i, k) for k in range(N_CHIPS) for i in range(n)]

    def body(*refs):
        g_refs = refs[:n]
        got_ref, out_ref, a_buf, b_buf, o_buf, a_sems, b_sems, o_sems = refs[n:]
        c = lax.axis_index("c")

        def loads(u):
            i, k = units[u]
            slot, rows = u % 2, pl.ds(0, hs[i])
            return (pltpu.make_async_copy(g_refs[i].at[k, c], a_buf.at[slot, rows, :], a_sems.at[slot]),
                    pltpu.make_async_copy(got_ref.at[k, pl.ds(offs[i], hs[i]), :], b_buf.at[slot, rows, :], b_sems.at[slot]))

        def store(u):
            i, k = units[u]
            return pltpu.make_async_copy(o_buf.at[u % 2, pl.ds(0, hs[i]), :], out_ref.at[k, pl.ds(offs[i], hs[i]), :],
                                         o_sems.at[u % 2])

        for cp in loads(0):
            cp.start()
        for u, (i, k) in enumerate(units):
            if u + 1 < len(units):
                for cp in loads(u + 1):
                    cp.start()
            for cp in loads(u):
                cp.wait()
            if u >= 2:
                store(u - 2).wait()
            rows = pl.ds(0, hs[i])
            o_buf[u % 2, rows, :] = (a_buf[u % 2, rows, :] + b_buf[u % 2, rows, :]).astype(BF16)
            store(u).start()
        store(len(units) - 2).wait()
        store(len(units) - 1).wait()

    return pl.pallas_call(
        body, name=name, in_specs=[ANY] * (n + 1), out_specs=ANY,
        out_shape=jax.ShapeDtypeStruct((N_CHIPS, sum(hs), cols), BF16),
        scratch_shapes=[pltpu.VMEM((2, hmax, cols), F32), pltpu.VMEM((2, hmax, cols), F32), pltpu.VMEM((2, hmax, cols), BF16),
                        pltpu.SemaphoreType.DMA((2,)), pltpu.SemaphoreType.DMA((2,)), pltpu.SemaphoreType.DMA((2,))],
        compiler_params=pltpu.CompilerParams(vmem_limit_bytes=VMEM_LIMIT),
    )(*grads, got)


def _chip_exchange(parts, name):
    _, rows, cols = parts.shape

    def body(in_ref, out_ref, send_sems, recv_sems):
        x, y, c, chips = _position()
        sent = [_remote(in_ref.at[2 * cx + cy], out_ref.at[j], send_sems.at[j], recv_sems.at[j], (cx, cy, c))
                for j, (cx, cy) in enumerate(chips)]
        for cp in sent:
            cp.start()
        for cp in sent:
            cp.wait()

    return pl.pallas_call(
        body, name=name, in_specs=[ANY], out_specs=ANY, out_shape=jax.ShapeDtypeStruct((3, rows, cols), parts.dtype),
        scratch_shapes=[pltpu.SemaphoreType.DMA((3,)), pltpu.SemaphoreType.DMA((3,))],
    )(parts)


def _exchange_copies(in_ref, land_ref, send_sems, recv_sems):
    x, y, c, chips = _position()
    return [_remote(in_ref.at[2 * cx + cy], land_ref.at[j], send_sems.at[j], recv_sems.at[j], (cx, cy, c))
            for j, (cx, cy) in enumerate(chips)]


def _exchange_start(parts, name):
    _, rows, cols = parts.shape

    def body(in_ref, land_ref, send_sems, recv_sems, in_thru, land_thru, token):
        for cp in _exchange_copies(in_ref, land_ref, send_sems, recv_sems):
            cp.start()
        token[...] = jnp.zeros_like(token)

    land = lax.empty((3, rows, cols), parts.dtype)
    return pl.pallas_call(
        body, name=name,
        out_shape=(pltpu.SemaphoreType.DMA((3,)), pltpu.SemaphoreType.DMA((3,)), pltpu.HBM(parts.shape, parts.dtype),
                   pltpu.HBM(land.shape, land.dtype), jax.ShapeDtypeStruct((8, LANES), F32)),
        in_specs=(HBM, HBM), out_specs=(SEM, SEM, HBM, HBM, pl.BlockSpec(memory_space=pltpu.VMEM)),
        input_output_aliases={0: 2, 1: 3},
        compiler_params=pltpu.CompilerParams(has_side_effects=SIDE_EFFECT),
    )(pltpu.with_memory_space_constraint(parts, pltpu.HBM), pltpu.with_memory_space_constraint(land, pltpu.HBM))


def _exchange_wait(send_sems, recv_sems, parts, land, after, name):
    def body(in_ref, land_ref, send_sems, recv_sems, after_ref, in_out, land_out):
        for cp in _exchange_copies(in_ref, land_ref, send_sems, recv_sems):
            cp.wait_send()
            cp.wait_recv()

    return pl.pallas_call(
        body, name=name, out_shape=(pltpu.HBM(parts.shape, parts.dtype), pltpu.HBM(land.shape, land.dtype)),
        in_specs=(HBM, HBM, SEM, SEM, ANY), out_specs=(HBM, HBM), input_output_aliases={0: 0, 1: 1},
        compiler_params=pltpu.CompilerParams(has_side_effects=SIDE_EFFECT),
    )(parts, land, send_sems, recv_sems, after)


def _chip_sum(parts, recv, chip, name):
    _, rows, cols = parts.shape
    tm = _tile(rows, 512, 16)

    def body(chip_ref, own_ref, recv_ref, o_ref):
        acc = own_ref[0].astype(F32)
        for j in range(3):
            acc = acc + recv_ref[j].astype(F32)
        o_ref[...] = acc

    return pl.pallas_call(
        body, name=name,
        grid_spec=pltpu.PrefetchScalarGridSpec(
            num_scalar_prefetch=1, grid=(rows // tm,),
            in_specs=[pl.BlockSpec((1, tm, cols), lambda i, chip_ref: (chip_ref[0], i, 0)),
                      pl.BlockSpec((3, tm, cols), lambda i, chip_ref: (0, i, 0))],
            out_specs=pl.BlockSpec((tm, cols), lambda i, chip_ref: (i, 0))),
        out_shape=jax.ShapeDtypeStruct((rows, cols), F32), compiler_params=_params("parallel"),
    )(chip, parts, recv)


def _join_unpack(mine, hs, groups, name):
    n = len(hs)
    offs = [sum(hs[:i]) for i in range(n)]
    cols = mine.shape[1]
    n_out = max(groups) + 1
    base = [2 * sum(h for h, g in zip(hs[:i], groups[:i]) if g == groups[i]) for i in range(n)]
    out_rows = [2 * sum(h for h, g in zip(hs, groups) if g == k) for k in range(n_out)]

    def body(in_ref, *refs):
        outs = refs[:n_out]
        send_sems, recv_sems, buf, in_sems, out_sems = refs[n_out:]
        x, y, c, _ = _position()
        sibling = (x, y, 1 - c)
        sent, local = [], []
        for i in range(n):
            src = in_ref.at[pl.ds(offs[i], hs[i]), :]
            here = outs[groups[i]].at[pl.ds(base[i] + c * hs[i], hs[i]), :]
            cp = _remote(src, here, send_sems.at[i], recv_sems.at[i], sibling)
            cp.start()
            sent.append(cp)
            local.append((src, here))
        _staged_copies(local, buf, in_sems, out_sems)
        for i, cp in enumerate(sent):
            there = outs[groups[i]].at[pl.ds(base[i] + (1 - c) * hs[i], hs[i]), :]
            _remote(there, there, send_sems.at[i], recv_sems.at[i], sibling).wait_recv()
            cp.wait_send()

    assert max(hs) <= STAGE_ROWS
    return pl.pallas_call(
        body, name=name, in_specs=[ANY], out_specs=[ANY] * n_out,
        out_shape=[jax.ShapeDtypeStruct((r, cols), F32) for r in out_rows],
        scratch_shapes=[pltpu.SemaphoreType.DMA((n,)), pltpu.SemaphoreType.DMA((n,))] + _stage_scratch(F32, cols),
        compiler_params=pltpu.CompilerParams(vmem_limit_bytes=VMEM_LIMIT),
    )(mine)


SMALL_ROWS = 16


def _small_rows(n):
    return -(-n // (SMALL_ROWS * LANES)) * SMALL_ROWS


def _pack_small(arrs):
    parts = []
    for a in arrs:
        flat = a.reshape(-1)
        rows = _small_rows(flat.shape[0])
        flat = jnp.pad(flat, (0, rows * LANES - flat.shape[0]))
        parts.append(flat.reshape(rows, LANES))
    return jnp.concatenate(parts, axis=0)


def _unpack_small(packed, shapes):
    out, r = [], 0
    for sh in shapes:
        n = math.prod(sh)
        cnt = _small_rows(n)
        out.append(packed[r:r + cnt].reshape(-1)[:n].reshape(sh))
        r += cnt
    return out


def _ffn_fwd(h, g_norm, w_gate_t, w_up_t, w_down, tag):
    n = _rms_fwd(h, g_norm, f"{tag}_norm")
    act, gate, up = _ffn_gate_up(n, w_gate_t, w_up_t, f"{tag}_gate_up")
    out = _matmul(act, w_down, add=h, name=f"{tag}_down")
    return out, (n, gate, up, act)


def _ffn_bwd(dh, dhb, h_in, saved, g_norm, w_gate_t, w_up_t, w_down, tag, after=None):
    n, gate, up, act = saved
    dgate, dup = _ffn_dact(dhb, w_down, gate, up, f"{tag}_dact", after)
    dw_down = _matmul(act, dhb, trans_a=True, name=f"{tag}_dwdown")
    dw_gate_t = _matmul(dgate, n, trans_a=True, name=f"{tag}_dwgate")
    dw_up_t = _matmul(dup, n, trans_a=True, name=f"{tag}_dwup")
    dh_in, dh_inb, dg = _dn_norm([(dgate, w_gate_t), (dup, w_up_t)], h_in, g_norm, dh, f"{tag}_dnorm")
    return dh_in, dh_inb, dg, dw_gate_t, dw_up_t, dw_down


def _local_step(x, tgt, w, big, late_weights, reduce_early):
    s = x.shape[0]
    tabs = _rope_tables(s)
    grads, gbig = {}, {}

    g_ev = w['ev_norm_g']
    n1 = _rms_fwd(x, g_ev, "ev_norm")
    proj0 = _matmul(n1, big['ev_w_in', 0], trans_b=True, name="ev_in")
    q0, k0, v0 = _qkv_prep_even(proj0, tabs, "ev_qkv")
    sinks = w['ev_sinks'].reshape(-1)
    o0, lse0, o0b = _attn_fwd(q0, k0, v0, sinks, max_dist=BLOCK - 1, name="ev_attn", emit_bf16=True)
    yconv, cout = _conv_fwd(proj0, w['ev_conv_w'][0], w['ev_conv_b'], w['ev_conv_ln_g'], w['ev_conv_ln_b'], "ev_conv")
    mix0 = (o0b[0], cout)
    h1 = _matmul(mix0, big['ev_w_out', 0], add=x, name="ev_out")
    big = {**big, **late_weights(h1)}

    g_f0 = w['ffn_norm_g'][0:1]
    h2, ffn0 = _ffn_fwd(h1, g_f0, big['ffn_w_gate', 0], big['ffn_w_up', 0], big['ffn_w_down', 0], "ffn0")

    g_od = w['od_norm_g']
    n3 = _rms_fwd(h2, g_od, "od_norm")
    proj1 = _matmul(n3, big['od_w_in', 0], trans_b=True, name="od_in")
    qkv = _qkv_prep_odd(proj1, tabs, "od_qkv")
    nb = len(DILATIONS)
    outs, lses = [], []
    for i, d in enumerate(DILATIONS):
        o_r, lse_r = _attn_fwd(qkv[i], qkv[nb + i], qkv[2 * nb + i], None, max_dist=BLOCK, name=f"od_attn{d}")
        outs.append(o_r)
        lses.append(lse_r)
    comb = _combine(outs, lses, "od_combine")
    c_bf16 = comb[0]
    c_fold = {1: comb[1]}
    lse_fold = {1: comb[2]}
    for i, d in enumerate(DILATIONS[1:]):
        c_fold[d], lse_fold[d] = comb[3 + 2 * i], comb[4 + 2 * i]
    w_sp = w['od_spatial_w'][0]
    sb_t = w['od_spatial_b'][0].T
    mixed, dout = _gate_fwd(proj1, w['od_sgu_ln_g'], w['od_sgu_ln_b'], w_sp, sb_t, "od_gate")
    mix1 = (c_bf16, dout)
    h3 = _matmul(mix1, big['od_w_out', 0], add=h2, name="od_out")

    g_f1 = w['ffn_norm_g'][1:2]
    h4, ffn1 = _ffn_fwd(h3, g_f1, big['ffn_w_gate', 1], big['ffn_w_up', 1], big['ffn_w_down', 1], "ffn1")

    dh4, dh4b, dg_final, loss_tile = _final_loss(h4, w['final_norm_g'].reshape(1, D_MODEL), tgt, "final")
    grads['final_norm_g'] = dg_final.reshape(D_MODEL)

    dh3, dh3b, dg_f1, gbig['ffn_w_gate', 1], gbig['ffn_w_up', 1], gbig['ffn_w_down', 1] = _ffn_bwd(
        dh4, dh4b, h3, ffn1, g_f1, big['ffn_w_gate', 1], big['ffn_w_up', 1], big['ffn_w_down', 1], "ffn1")

    dmix1 = _matmul(dh3b, big['od_w_out', 0], trans_b=True, name="od_dmix")
    gbig['od_w_out', 0] = _matmul_tn_pair(mix1[0], mix1[1], dh3b, "od_dwout")
    do_fold = dict(zip(DILATIONS[1:], _fold_dout(dmix1, "od_fold_dout")))
    do_fold[1] = dmix1[None]
    dqs, dks, dvs = [], [], []
    for i, d in enumerate(DILATIONS):
        dq_r, dk_r, dv_r = _attn_bwd(qkv[i], qkv[nb + i], qkv[2 * nb + i], do_fold[d], c_fold[d], lse_fold[d], None,
                                     max_dist=BLOCK, name=f"od_dattn{d}")
        dqs.append(dq_r)
        dks.append(dk_r)
        dvs.append(dv_r)
    dz, dg_sgu, db_sgu, dw_sp, dsb = _gate_bwd(dmix1, proj1, mixed, w['od_sgu_ln_g'], w['od_sgu_ln_b'], w_sp, "od_dgate")
    grads['od_sgu_ln_g'], grads['od_sgu_ln_b'] = dg_sgu, db_sgu
    grads['od_spatial_w'], grads['od_spatial_b'] = dw_sp[None], dsb[None]
    dproj1 = _qkv_post_odd(dqs, dks, dvs, dz, tabs, "od_dproj")
    gbig['od_w_in', 0] = _matmul(dproj1, n3, trans_a=True, name="od_dwin")
    dh2, dh2b, dg_od = _dn_norm([(dproj1, big['od_w_in', 0])], h2, g_od, dh3, "od_dnorm")
    grads['od_norm_g'] = dg_od
    token = reduce_early(0, gbig)

    dh1, dh1b, dg_f0, gbig['ffn_w_gate', 0], gbig['ffn_w_up', 0], gbig['ffn_w_down', 0] = _ffn_bwd(
        dh2, dh2b, h1, ffn0, g_f0, big['ffn_w_gate', 0], big['ffn_w_up', 0], big['ffn_w_down', 0], "ffn0", token)
    grads['ffn_norm_g'] = jnp.concatenate([dg_f0, dg_f1], axis=0)
    token = reduce_early(1, gbig)

    dmix0 = _matmul(dh1b, big['ev_w_out', 0], trans_b=True, name="ev_dmix", after=token)
    gbig['ev_w_out', 0] = _matmul_tn_pair(mix0[0], mix0[1], dh1b, "ev_dwout")
    dq0, dk0, dv0, dsink = _attn_bwd(q0, k0, v0, dmix0[None], o0, lse0, sinks, max_dist=BLOCK - 1, name="ev_dattn")
    grads['ev_sinks'] = dsink[:, 0, :].reshape(N_PAIRS, 2, HEAD_DIM)[:, :, 0].reshape(1, 8)
    dyc, dg_cln, db_cln, dcb = _conv_tail_bwd(dmix0, yconv, w['ev_conv_ln_g'], w['ev_conv_ln_b'], "ev_dconv_tail")
    grads['ev_conv_ln_g'], grads['ev_conv_ln_b'], grads['ev_conv_b'] = dg_cln, db_cln, dcb
    dglu, dconv_w = _conv_bwd(proj0, dyc, w['ev_conv_w'][0], "ev_dconv")
    grads['ev_conv_w'] = dconv_w[None]
    dproj0 = _qkv_post_even(dq0, dk0, dv0, dglu, tabs, "ev_dproj")
    gbig['ev_w_in', 0] = _matmul(dproj0, n1, trans_a=True, name="ev_dwin")
    dx, _, dg_ev = _dn_norm([(dproj0, big['ev_w_in', 0])], x, g_ev, dh1, "ev_dnorm")
    grads['ev_norm_g'] = dg_ev
    return loss_tile, dx, grads, gbig


def _shard_rows(w, layer, by_cols):
    return w[layer].T if by_cols else w[layer]


def kernel(x, ev_norm_g, ev_w_in, ev_sinks, ev_conv_w, ev_conv_b, ev_conv_ln_g, ev_conv_ln_b, ev_w_out, od_norm_g, od_w_in, od_sgu_ln_g, od_sgu_ln_b, od_spatial_w, od_spatial_b, od_w_out, ffn_norm_g, ffn_w_gate, ffn_w_up, ffn_w_down, final_norm_g, loss_target, m_ev_norm_g, m_ev_w_in, m_ev_sinks, m_ev_conv_w, m_ev_conv_b, m_ev_conv_ln_g, m_ev_conv_ln_b, m_ev_w_out, m_od_norm_g, m_od_w_in, m_od_sgu_ln_g, m_od_sgu_ln_b, m_od_spatial_w, m_od_spatial_b, m_od_w_out, m_ffn_norm_g, m_ffn_w_gate, m_ffn_w_up, m_ffn_w_down, m_final_norm_g, v_ev_norm_g, v_ev_w_in, v_ev_sinks, v_ev_conv_w, v_ev_conv_b, v_ev_conv_ln_g, v_ev_conv_ln_b, v_ev_w_out, v_od_norm_g, v_od_w_in, v_od_sgu_ln_g, v_od_sgu_ln_b, v_od_spatial_w, v_od_spatial_b, v_od_w_out, v_ffn_norm_g, v_ffn_w_gate, v_ffn_w_up, v_ffn_w_down, v_final_norm_g):
    given = dict(locals())
    wts = {n: given[n] for n in WEIGHTS}
    mom = {n: given["m_" + n] for n in WEIGHTS}
    var = {n: given["v_" + n] for n in WEIGHTS}
    chip = 2 * lax.axis_index("x") + lax.axis_index("y")

    shard_rows = [_shard_rows(wts[n], layer, by_cols).astype(BF16) for n, layer, by_cols in BIG]
    counts = [a.shape[0] for a in shard_rows]
    n_first = sum(n.startswith('ev_') for n, _, _ in BIG)

    def unpack(stacked, entries, cnts):
        out, r = {}, 0
        for (n, layer, _), cnt in zip(entries, cnts):
            out[n, layer] = stacked[:, r:r + cnt].reshape(N_CHIPS * cnt, D_MODEL)
            r += cnt
        return out

    first_w = _gather_chips(jnp.concatenate(shard_rows[:n_first], axis=0), "gather_weights_ev")
    big = unpack(first_w, BIG[:n_first], counts[:n_first])
    send_sems, recv_sems, late_shard, late_land, token = _gather_start(jnp.concatenate(shard_rows[n_first:], axis=0),
                                                                      first_w, "gather_weights_start")

    def late_weights(after):
        shard, land = _gather_wait(send_sems, recv_sems, late_shard, late_land, after, "gather_weights_wait")
        return unpack(_gather_finish(shard, land, "gather_weights_finish"), BIG[n_first:], counts[n_first:])

    full = {n: wts[n] for n in SMALL_REPL}
    full['ev_norm_g'] = full['ev_norm_g'] + token[0:1, 0:1]
    small_shards = [wts[n] for n in SMALL_SHARDED]
    small_shapes = [a.shape for a in small_shards]
    all_s = _gather_chips(_pack_small(small_shards), "gather_small_weights")
    per_chip = [_unpack_small(all_s[k], small_shapes) for k in range(N_CHIPS)]
    for i, n in enumerate(SMALL_SHARDED):
        full[n] = jnp.concatenate([per_chip[k][i] for k in range(N_CHIPS)], axis=-1)

    half_rows = {(n, layer): cnt // 2 for (n, layer, _), cnt in zip(BIG, counts)}
    in_flight = []

    def pair_sum(stage, gbig):
        split = [gbig[e].reshape(N_CHIPS, 2, half_rows[e], D_MODEL) for e in GRAD_STAGES[stage]]
        got = _pair_send(split, f"grad_pair_send{stage}")
        return _pair_add(split, got, f"grad_pair_add{stage}")

    def reduce_early(stage, gbig):
        *handles, token = _exchange_start(pair_sum(stage, gbig), f"grad_exchange_start{stage}")
        in_flight.append(handles)
        return token

    loss_tile, grad_x, grads, gbig = _local_step(x[0], loss_target[0], full, big, late_weights, reduce_early)
    loss = lax.psum(loss_tile[0, 0], ("x", "y", "c"))

    reduced = {}
    for stage, entries in enumerate(GRAD_STAGES):
        if stage < len(in_flight):
            chip_part, from_chips = _exchange_wait(*in_flight[stage], grad_x, f"grad_exchange_wait{stage}")
        else:
            chip_part = pair_sum(stage, gbig)
            from_chips = _chip_exchange(chip_part, f"grad_chip_exchange{stage}")
        my_half = _chip_sum(chip_part, from_chips, chip.reshape(1), f"grad_chip_sum{stage}")
        joined = _join_unpack(my_half, [half_rows[e] for e in entries], list(range(len(entries))), f"grad_join_halves{stage}")
        reduced.update(zip(entries, joined))

    small_names = SMALL_REPL + SMALL_SHARDED
    small_full_shapes = [grads[n].shape for n in small_names]
    spack = _pack_small([grads[n] for n in small_names])
    s_all = _gather_devices(spack, "grad_small_gather")
    s_sum = _unpack_small(_ordered_sum(s_all, "grad_small_sum"), small_full_shapes)
    g_all = dict(zip(small_names, s_sum))
    for n in SMALL_SHARDED:
        width = wts[n].shape[-1]
        g_all[n] = lax.dynamic_slice_in_dim(g_all[n], chip * width, width, axis=g_all[n].ndim - 1)

    delta, new_m, new_v = {}, {}, {}
    for n in BIG_NAMES:
        by_cols = [bc for nn, _, bc in BIG if nn == n][0]
        layers = wts[n].shape[0]

        def as_rows(a):
            return (jnp.swapaxes(a, 1, 2) if by_cols else a).reshape(-1, D_MODEL)

        def from_rows(a):
            a = a.reshape(layers, -1, D_MODEL)
            return jnp.swapaxes(a, 1, 2) if by_cols else a

        g_rows = [reduced[n, layer] for layer in range(layers)]
        g_rows = g_rows[0] if layers == 1 else jnp.concatenate(g_rows, axis=0)
        updated = _adamw(as_rows(wts[n]), g_rows, as_rows(mom[n]), as_rows(var[n]), f"adamw_{n}")
        g_all[n] = from_rows(g_rows)
        delta[n], new_m[n], new_v[n] = (from_rows(a) for a in updated)
    shapes = [wts[n].shape for n in small_names]
    d_s, m_s, v_s = _adamw(*[_pack_small([src[n] for n in small_names]) for src in (wts, g_all, mom, var)], "adamw_small")
    for dst, packed in ((delta, d_s), (new_m, m_s), (new_v, v_s)):
        dst.update(zip(small_names, _unpack_small(packed, shapes)))

    return (loss, grad_x[None], *[g_all[n] for n in WEIGHTS], *[delta[n] for n in WEIGHTS],
            *[new_m[n] for n in WEIGHTS], *[new_v[n] for n in WEIGHTS])
```

```python
import math

import jax
import jax.numpy as jnp
from jax import lax
from jax.experimental import pallas as pl
from jax.experimental.pallas import tpu as pltpu

F32 = jnp.float32
BF16 = jnp.bfloat16

D_MODEL = 1024
HEAD_DIM = 64
ROT_DIM = 16
ROPE_THETA = 500000.0
RMS_EPS = 1e-6
LN_EPS = 1e-5
BLOCK = 128
CONV_WIDTH = 31
CONV_HALO = 32
CONV_ROWS = 64
D_FF = 2816
N_GROUPS = 8
ATTN_W = 512
ATTN_SCALE = HEAD_DIM ** -0.5
NEG = -1e30
DILATIONS = (1, 4, 16)

ADAM_LR = 0.001
ADAM_B1 = 0.9
ADAM_B2 = 0.999
ADAM_EPS = 1e-08
ADAM_WD = 0.01
ADAM_STEP = 10

LANES = 128
N_PAIRS = ATTN_W // LANES
VMEM_LIMIT = 56 * 1024 * 1024
MESH = pl.DeviceIdType.MESH
N_CHIPS = 4
N_DEV = 8

WEIGHTS = ['ev_norm_g', 'ev_w_in', 'ev_sinks', 'ev_conv_w', 'ev_conv_b', 'ev_conv_ln_g', 'ev_conv_ln_b', 'ev_w_out',
           'od_norm_g', 'od_w_in', 'od_sgu_ln_g', 'od_sgu_ln_b', 'od_spatial_w', 'od_spatial_b', 'od_w_out',
           'ffn_norm_g', 'ffn_w_gate', 'ffn_w_up', 'ffn_w_down', 'final_norm_g']
BIG = [('ev_w_in', 0, True), ('ev_w_out', 0, False), ('od_w_in', 0, True), ('od_w_out', 0, False),
       ('ffn_w_gate', 0, True), ('ffn_w_gate', 1, True), ('ffn_w_up', 0, True), ('ffn_w_up', 1, True),
       ('ffn_w_down', 0, False), ('ffn_w_down', 1, False)]
BIG_NAMES = ['ev_w_in', 'ev_w_out', 'od_w_in', 'od_w_out', 'ffn_w_gate', 'ffn_w_up', 'ffn_w_down']
GRAD_STAGES = ([('od_w_in', 0), ('od_w_out', 0), ('ffn_w_gate', 1), ('ffn_w_up', 1), ('ffn_w_down', 1)],
               [('ffn_w_gate', 0), ('ffn_w_up', 0), ('ffn_w_down', 0)],
               [('ev_w_in', 0), ('ev_w_out', 0)])
SMALL_SHARDED = ['ev_conv_w', 'od_norm_g', 'od_sgu_ln_g', 'od_sgu_ln_b']
SMALL_REPL = ['ev_norm_g', 'ev_sinks', 'ev_conv_b', 'ev_conv_ln_g', 'ev_conv_ln_b', 'od_spatial_w', 'od_spatial_b',
              'ffn_norm_g', 'final_norm_g']


def _tile(n, cap, mult=LANES):
    best = None
    for t in range(mult, min(n, cap) + 1, mult):
        if n % t == 0:
            best = t
    assert best is not None, (n, cap)
    return best


def _params(*sem):
    return pltpu.CompilerParams(dimension_semantics=sem, vmem_limit_bytes=VMEM_LIMIT)


def _sigmoid(x):
    return 1.0 / (1.0 + jnp.exp(-x))


def _pair_block(p):
    return slice(p * LANES, (p + 1) * LANES)


def _matmul(a, b, *, name, trans_a=False, trans_b=False, add=None, out_dtype=F32, after=None):
    parts = a if isinstance(a, (tuple, list)) else (a,)
    if trans_a:
        k, m = parts[0].shape
    else:
        m = parts[0].shape[0]
        k = sum(p.shape[1] for p in parts)
    if trans_b:
        n, k2 = b.shape
    else:
        k2, n = b.shape
    assert k == k2 and b.dtype == BF16 and all(p.dtype == BF16 for p in parts)
    tm = _tile(m, D_FF // 2 if trans_a else 512)
    tn = _tile(n, D_FF // 2)
    tk = k if k <= D_FF else _tile(k, 2048)
    nk = k // tk
    na = len(parts)
    assert na == 1 or (nk == 1 and not trans_a)
    assert nk == 1 or out_dtype == F32
    dims = (((0 if trans_a else 1,), (1 if trans_b else 0,)), ((), ()))
    has_add = add is not None

    def body(*refs):
        a_refs, b_ref = refs[:na], refs[na]
        add_ref = refs[na + 1] if has_add else None
        o_ref = refs[na + 1 + has_add + (after is not None)]
        def product():
            a_val = a_refs[0][...] if na == 1 else jnp.concatenate([r[...] for r in a_refs], axis=1)
            return lax.dot_general(a_val, b_ref[...], dims, preferred_element_type=F32)

        if nk == 1:
            part = product()
            if has_add:
                part = part + add_ref[...]
            o_ref[...] = part.astype(o_ref.dtype)
            return
        kk = pl.program_id(2)

        @pl.when(kk == 0)
        def _():
            o_ref[...] = product() + add_ref[...] if has_add else product()

        @pl.when(kk > 0)
        def _():
            o_ref[...] = product() + o_ref[...]

    if trans_a:
        a_specs = [pl.BlockSpec((tk, tm), lambda i, j, kk: (kk, i))]
    elif na == 1:
        a_specs = [pl.BlockSpec((tm, tk), lambda i, j, kk: (i, kk))]
    else:
        a_specs = [pl.BlockSpec((tm, p.shape[1]), lambda i, j, kk: (i, 0)) for p in parts]
    b_spec = pl.BlockSpec((tn, tk), lambda i, j, kk: (j, kk)) if trans_b else pl.BlockSpec((tk, tn), lambda i, j, kk: (kk, j))
    o_spec = pl.BlockSpec((tm, tn), lambda i, j, kk: (i, j))
    in_specs = a_specs + [b_spec] + ([o_spec] if has_add else [])
    operands = list(parts) + [b] + ([add] if has_add else [])
    if after is not None:
        in_specs.append(_after_spec(after))
        operands.append(after)
    return pl.pallas_call(
        body, name=name, grid=(m // tm, n // tn, nk), in_specs=in_specs, out_specs=o_spec,
        out_shape=jax.ShapeDtypeStruct((m, n), out_dtype),
        compiler_params=_params("parallel", "parallel", "arbitrary"),
    )(*operands)


def _matmul_tn_pair(a1, a2, b, name):
    kdim, m1 = a1.shape
    m2 = a2.shape[1]
    n = b.shape[1]
    tn = _tile(n, 1024)
    tk = _tile(kdim, 2048)
    nk = kdim // tk
    dims = (((0,), (0,)), ((), ()))

    def body(a1_ref, a2_ref, b_ref, o_ref):
        kk = pl.program_id(1)
        def products():
            bv = b_ref[...]
            return (lax.dot_general(a1_ref[...], bv, dims, preferred_element_type=F32),
                    lax.dot_general(a2_ref[...], bv, dims, preferred_element_type=F32))

        @pl.when(kk == 0)
        def _():
            o_ref[0:m1, :], o_ref[m1:, :] = products()

        @pl.when(kk > 0)
        def _():
            top, bot = products()
            o_ref[0:m1, :] = top + o_ref[0:m1, :]
            o_ref[m1:, :] = bot + o_ref[m1:, :]

    return pl.pallas_call(
        body, name=name, grid=(n // tn, nk),
        in_specs=[pl.BlockSpec((tk, m1), lambda j, kk: (kk, 0)), pl.BlockSpec((tk, m2), lambda j, kk: (kk, 0)),
                  pl.BlockSpec((tk, tn), lambda j, kk: (kk, j))],
        out_specs=pl.BlockSpec((m1 + m2, tn), lambda j, kk: (0, j)),
        out_shape=jax.ShapeDtypeStruct((m1 + m2, n), F32),
        compiler_params=_params("parallel", "arbitrary"),
    )(a1, a2, b)


def _ffn_gate_up(n, w_gate_t, w_up_t, name):
    m, k = n.shape
    f = w_gate_t.shape[0]
    tm, tn = _tile(m, 1024), _tile(f, D_FF // 2)

    def body(n_ref, wg_ref, wu_ref, act_ref, gate_ref, up_ref):
        a = n_ref[...]
        for cols in _col_chunks(tn):
            gate = lax.dot_general(a, wg_ref[cols, :], NT, preferred_element_type=F32)
            up = lax.dot_general(a, wu_ref[cols, :], NT, preferred_element_type=F32)
            act_ref[:, cols] = (gate * _sigmoid(gate) * up).astype(BF16)
            gate_ref[:, cols] = gate.astype(BF16)
            up_ref[:, cols] = up.astype(BF16)

    wspec = pl.BlockSpec((tn, k), lambda j, i: (j, 0))
    ospec = pl.BlockSpec((tm, tn), lambda j, i: (i, j))
    return pl.pallas_call(
        body, name=name, grid=(f // tn, m // tm), in_specs=[pl.BlockSpec((tm, k), lambda j, i: (i, 0)), wspec, wspec],
        out_specs=[ospec] * 3, out_shape=[jax.ShapeDtypeStruct((m, f), BF16)] * 3,
        compiler_params=_params("parallel", "parallel"),
    )(n, w_gate_t, w_up_t)


def _col_chunks(n, width=384):
    return [slice(c, min(c + width, n)) for c in range(0, n, width)]


def _after_spec(after):
    return pl.BlockSpec(after.shape, lambda *_: (0,) * after.ndim)


def _ffn_dact(dhb, w_down, gate, up, name, after=None):
    m, k = dhb.shape
    f = w_down.shape[0]
    tm, tn = _tile(m, 1024), _tile(f, D_FF // 2)

    def body(d_ref, w_ref, g_ref, u_ref, *rest):
        dg_ref, du_ref = rest[-2:]
        d = d_ref[...]
        for cols in _col_chunks(tn):
            dact = lax.dot_general(d, w_ref[cols, :], NT, preferred_element_type=F32)
            g = g_ref[:, cols].astype(F32)
            sg = _sigmoid(g)
            dg_ref[:, cols] = (dact * u_ref[:, cols].astype(F32) * sg * (1.0 + g * (1.0 - sg))).astype(BF16)
            du_ref[:, cols] = (dact * g * sg).astype(BF16)

    ospec = pl.BlockSpec((tm, tn), lambda j, i: (i, j))
    extra = [] if after is None else [after]
    return pl.pallas_call(
        body, name=name, grid=(f // tn, m // tm),
        in_specs=[pl.BlockSpec((tm, k), lambda j, i: (i, 0)), pl.BlockSpec((tn, k), lambda j, i: (j, 0)), ospec, ospec]
        + [_after_spec(a) for a in extra],
        out_specs=[ospec] * 2, out_shape=[jax.ShapeDtypeStruct((m, f), BF16)] * 2,
        compiler_params=_params("parallel", "parallel"),
    )(dhb, w_down, gate, up, *extra)


def _dn_norm(pairs, h, g, dres, name):
    m = h.shape[0]
    tm = 512
    np_ = len(pairs)

    def body(*refs):
        a_refs, b_refs = refs[:np_], refs[np_:2 * np_]
        h_ref, dres_ref, g_ref, dh_ref, dhb_ref, dg_ref = refs[2 * np_:]

        @pl.when(_first_step())
        def _():
            dg_ref[...] = jnp.zeros_like(dg_ref)

        dy = jnp.dot(a_refs[0][...], b_refs[0][...], preferred_element_type=F32)
        for a_ref, b_ref in zip(a_refs[1:], b_refs[1:]):
            dy = jnp.dot(a_ref[...], b_ref[...], preferred_element_type=F32) + dy
        x = h_ref[...]
        r = lax.rsqrt(jnp.mean(x * x, axis=-1, keepdims=True) + RMS_EPS)
        xh = x * r
        dg_ref[...] += jnp.sum(dy * xh, axis=0, keepdims=True)
        dxh = dy * g_ref[...]
        tot = dres_ref[...] + r * (dxh - xh * jnp.mean(dxh * xh, axis=-1, keepdims=True))
        dh_ref[...] = tot
        dhb_ref[...] = tot.astype(BF16)

    row = lambda w: pl.BlockSpec((tm, w), lambda i: (i, 0))
    whole = lambda a: pl.BlockSpec(a.shape, lambda i: (0, 0))
    a_list, b_list = [a for a, _ in pairs], [b for _, b in pairs]
    return pl.pallas_call(
        body, name=name, grid=(m // tm,),
        in_specs=[row(a.shape[1]) for a in a_list] + [whole(b) for b in b_list] + [row(D_MODEL), row(D_MODEL), whole(g)],
        out_specs=[row(D_MODEL), row(D_MODEL), pl.BlockSpec((1, D_MODEL), lambda i: (0, 0))],
        out_shape=[jax.ShapeDtypeStruct((m, D_MODEL), F32), jax.ShapeDtypeStruct((m, D_MODEL), BF16),
                   jax.ShapeDtypeStruct((1, D_MODEL), F32)],
        compiler_params=_params("arbitrary"),
    )(*a_list, *b_list, h, dres, g)


def _rows(body, name, tm, tiled, consts, outs, accs=()):
    s = tiled[0].shape[0]
    assert s % tm == 0
    in_specs = [pl.BlockSpec((tm, a.shape[1]), lambda i: (i, 0)) for a in tiled]
    in_specs += [pl.BlockSpec(a.shape, lambda i, nd=a.ndim: (0,) * nd) for a in consts]
    out_shape = [jax.ShapeDtypeStruct((s, c), dt) for c, dt in outs]
    out_shape += [jax.ShapeDtypeStruct(sh, dt) for sh, dt in accs]
    out_specs = [pl.BlockSpec((tm, c), lambda i: (i, 0)) for c, _ in outs]
    out_specs += [pl.BlockSpec(sh, lambda i, nd=len(sh): (0,) * nd) for sh, _ in accs]
    return pl.pallas_call(
        body, name=name, grid=(s // tm,), in_specs=in_specs, out_specs=out_specs, out_shape=out_shape,
        compiler_params=_params("arbitrary"),
    )(*tiled, *consts)


def _first_step():
    return pl.program_id(0) == 0


def _rms_fwd(h, g, name):
    def body(h_ref, g_ref, n_ref):
        x = h_ref[...]
        r = lax.rsqrt(jnp.mean(x * x, axis=-1, keepdims=True) + RMS_EPS)
        n_ref[...] = (x * r * g_ref[...]).astype(BF16)

    return _rows(body, name, 512, [h], [g], [(D_MODEL, BF16)])[0]


def _final_loss(h, g, tgt, name):
    def body(h_ref, t_ref, g_ref, dh_ref, dhb_ref, dg_ref, loss_ref):
        @pl.when(_first_step())
        def _():
            dg_ref[...] = jnp.zeros_like(dg_ref)
            loss_ref[...] = jnp.zeros_like(loss_ref)

        x = h_ref[...]
        r = lax.rsqrt(jnp.mean(x * x, axis=-1, keepdims=True) + RMS_EPS)
        xh = x * r
        gg = g_ref[...]
        e = xh * gg - t_ref[...]
        loss_ref[...] += (0.5 / D_MODEL) * jnp.sum(jnp.sum(e * e, axis=-1, keepdims=True), axis=0, keepdims=True)
        dy = e * (1.0 / D_MODEL)
        dg_ref[...] += jnp.sum(dy * xh, axis=0, keepdims=True)
        dxh = dy * gg
        dx = r * (dxh - xh * jnp.mean(dxh * xh, axis=-1, keepdims=True))
        dh_ref[...] = dx
        dhb_ref[...] = dx.astype(BF16)

    return _rows(body, name, 512, [h, tgt], [g], [(D_MODEL, F32), (D_MODEL, BF16)],
                 [((1, D_MODEL), F32), ((1, LANES), F32)])


def _rope_tables(s):
    half = ROT_DIM // 2
    inv_freq = ROPE_THETA ** (-jnp.arange(half, dtype=F32) * (2.0 / ROT_DIM))
    ang = jnp.arange(s, dtype=F32)[:, None] * inv_freq[None, :]
    cos, sin = jnp.cos(ang), jnp.sin(ang)
    rest = HEAD_DIM - ROT_DIM
    ones = jnp.ones((s, rest), F32)
    zeros = jnp.zeros((s, rest), F32)
    zh = jnp.zeros((s, half), F32)
    c_t = jnp.concatenate([cos, cos, ones], axis=1)
    a_t = jnp.concatenate([-sin, zh, zeros], axis=1)
    b_t = jnp.concatenate([zh, sin, zeros], axis=1)
    return tuple(jnp.tile(t, (1, LANES // HEAD_DIM)) for t in (c_t, a_t, b_t))


def _rot(x, c, a, b):
    w = x.shape[1]
    half = ROT_DIM // 2
    return x * c + pltpu.roll(x, w - half, 1) * a + pltpu.roll(x, half, 1) * b


def _wide(t, w):
    return t if w == LANES else jnp.tile(t, (1, w // LANES))


def _low_lanes(rows):
    return lax.broadcasted_iota(jnp.int32, (rows, LANES), 1) < HEAD_DIM


def _fold_store(x, sc_ref, out_refs):
    tm = x.shape[0]
    if any(d > 1 for d in out_refs):
        for p in range(N_PAIRS):
            sc_ref[p] = x[:, _pair_block(p)]
    for d, o_ref in out_refs.items():
        if d == 1:
            o_ref[0] = x.astype(o_ref.dtype)
            continue
        for r in range(d):
            for p in range(N_PAIRS):
                o_ref[r, :, _pair_block(p)] = sc_ref[p, pl.ds(r, tm // d, stride=d), :].astype(o_ref.dtype)


def _unfold_load(x_ref, sc_ref, d, add=False):
    n = x_ref.shape[1]
    for r in range(d):
        for p in range(N_PAIRS):
            rows = pl.ds(r, n, stride=d) if d > 1 else slice(None)
            val = x_ref[r, :, _pair_block(p)].astype(F32)
            if add:
                val = val + sc_ref[p, rows, :]
            sc_ref[p, rows, :] = val


def _folded_spec(d, tm, w=ATTN_W):
    return pl.BlockSpec((d, tm // d, w), lambda i: (0, i, 0))


def _folded_shape(s, d, dtype, w=ATTN_W):
    return jax.ShapeDtypeStruct((d, s // d, w), dtype)


def _qkv_prep_even(proj, tabs, name):
    s = proj.shape[0]
    tm = 512

    def body(p_ref, c_ref, a_ref, b_ref, q_ref, k_ref, v_ref):
        c, a, b = c_ref[...], a_ref[...], b_ref[...]
        q_ref[0] = _rot(p_ref[:, 0:ATTN_W], _wide(c, ATTN_W), _wide(a, ATTN_W), _wide(b, ATTN_W)).astype(BF16)
        lo = _low_lanes(tm)
        for src, o_ref in ((_rot(p_ref[:, 512:640], c, a, b), k_ref), (p_ref[:, 640:768], v_ref)):
            swapped = pltpu.roll(src, HEAD_DIM, 1)
            o_ref[0, :, 0:LANES] = jnp.where(lo, src, swapped).astype(BF16)
            o_ref[0, :, LANES:] = jnp.where(lo, swapped, src).astype(BF16)

    row = lambda w: pl.BlockSpec((tm, w), lambda i: (i, 0))
    return pl.pallas_call(
        body, name=name, grid=(s // tm,), in_specs=[row(proj.shape[1]), row(LANES), row(LANES), row(LANES)],
        out_specs=[_folded_spec(1, tm), _folded_spec(1, tm, 2 * LANES), _folded_spec(1, tm, 2 * LANES)],
        out_shape=[_folded_shape(s, 1, BF16), _folded_shape(s, 1, BF16, 2 * LANES), _folded_shape(s, 1, BF16, 2 * LANES)],
        compiler_params=_params("parallel"),
    )(proj, *tabs)


def _qkv_post_even(dq, dk, dv, dglu, tabs, name):
    s = dglu.shape[0]
    tm = 512

    def body(dq_ref, dk_ref, dv_ref, dr_ref, c_ref, a_ref, b_ref, o_ref):
        c, a, b = c_ref[...], -a_ref[...], -b_ref[...]
        o_ref[:, 0:ATTN_W] = _rot(dq_ref[0], _wide(c, ATTN_W), _wide(a, ATTN_W), _wide(b, ATTN_W)).astype(BF16)
        lo = _low_lanes(tm)
        merged = []
        for ref in (dk_ref, dv_ref):
            first, second = ref[0, :, 0:LANES], ref[0, :, LANES:]
            merged.append(jnp.where(lo, first + pltpu.roll(first, HEAD_DIM, 1), second + pltpu.roll(second, HEAD_DIM, 1)))
        o_ref[:, 512:640] = _rot(merged[0], c, a, b).astype(BF16)
        o_ref[:, 640:768] = merged[1].astype(BF16)
        o_ref[:, 768:] = dr_ref[...]

    row = lambda w: pl.BlockSpec((tm, w), lambda i: (i, 0))
    return pl.pallas_call(
        body, name=name, grid=(s // tm,),
        in_specs=[_folded_spec(1, tm), _folded_spec(1, tm, 2 * LANES), _folded_spec(1, tm, 2 * LANES),
                  row(dglu.shape[1]), row(LANES), row(LANES), row(LANES)],
        out_specs=row(EVEN_IN), out_shape=jax.ShapeDtypeStruct((s, EVEN_IN), BF16),
        compiler_params=_params("parallel"),
    )(dq, dk, dv, dglu, *tabs)


def _qkv_prep_odd(proj, tabs, name):
    s = proj.shape[0]
    tm = 512

    def body(p_ref, c_ref, a_ref, b_ref, *rest):
        outs, sc_ref = rest[:-1], rest[-1]
        c, a, b = (_wide(t[...], ATTN_W) for t in (c_ref, a_ref, b_ref))
        for t in range(3):
            x = p_ref[:, t * ATTN_W:(t + 1) * ATTN_W]
            if t < 2:
                x = _rot(x, c, a, b)
            _fold_store(x, sc_ref, {d: outs[t * len(DILATIONS) + i] for i, d in enumerate(DILATIONS)})

    row = lambda w: pl.BlockSpec((tm, w), lambda i: (i, 0))
    return pl.pallas_call(
        body, name=name, grid=(s // tm,), in_specs=[row(proj.shape[1]), row(LANES), row(LANES), row(LANES)],
        out_specs=[_folded_spec(d, tm) for _ in range(3) for d in DILATIONS],
        out_shape=[_folded_shape(s, d, BF16) for _ in range(3) for d in DILATIONS],
        scratch_shapes=[pltpu.VMEM((N_PAIRS, tm, LANES), F32)],
        compiler_params=_params("parallel"),
    )(proj, *tabs)


def _qkv_post_odd(dqs, dks, dvs, dz, tabs, name):
    s = dz.shape[0]
    tm = 256
    nb = len(DILATIONS)

    def body(*refs):
        groups = (refs[:nb], refs[nb:2 * nb], refs[2 * nb:3 * nb])
        dz_ref, c_ref, a_ref, b_ref, o_ref, sc_ref = refs[3 * nb:]
        c, a, b = _wide(c_ref[...], ATTN_W), _wide(-a_ref[...], ATTN_W), _wide(-b_ref[...], ATTN_W)
        for t, group in enumerate(groups):
            for i, d in enumerate(DILATIONS):
                _unfold_load(group[i], sc_ref, d, add=i > 0)
            x = jnp.concatenate([sc_ref[p] for p in range(N_PAIRS)], axis=1)
            if t < 2:
                x = _rot(x, c, a, b)
            o_ref[:, t * ATTN_W:(t + 1) * ATTN_W] = x.astype(BF16)
        o_ref[:, 3 * ATTN_W:] = dz_ref[...]

    row = lambda w: pl.BlockSpec((tm, w), lambda i: (i, 0))
    return pl.pallas_call(
        body, name=name, grid=(s // tm,),
        in_specs=[_folded_spec(d, tm) for _ in range(3) for d in DILATIONS] + [row(dz.shape[1]), row(LANES), row(LANES), row(LANES)],
        out_specs=row(ODD_IN), out_shape=jax.ShapeDtypeStruct((s, ODD_IN), BF16),
        scratch_shapes=[pltpu.VMEM((N_PAIRS, tm, LANES), F32)],
        compiler_params=_params("parallel"),
    )(*dqs, *dks, *dvs, dz, *tabs)


def _fold_dout(dmix, name):
    s = dmix.shape[0]
    tm = 512
    ds = [d for d in DILATIONS if d > 1]

    def body(d_ref, *rest):
        outs, sc_ref = rest[:-1], rest[-1]
        _fold_store(d_ref[...], sc_ref, dict(zip(ds, outs)))

    return pl.pallas_call(
        body, name=name, grid=(s // tm,), in_specs=[pl.BlockSpec((tm, ATTN_W), lambda i: (i, 0))],
        out_specs=[_folded_spec(d, tm) for d in ds], out_shape=[_folded_shape(s, d, BF16) for d in ds],
        scratch_shapes=[pltpu.VMEM((N_PAIRS, tm, LANES), F32)],
        compiler_params=_params("parallel"),
    )(dmix)


def _window(j, i, tq):
    r0 = j * tq + i * BLOCK
    start = pl.multiple_of(jnp.maximum(r0 - BLOCK, 0), BLOCK)
    return pl.ds(start, 2 * BLOCK), r0 - start


def _band_valid(offset, max_dist):
    shape = (2 * BLOCK, 2 * BLOCK)
    dist = (lax.bitwise_and(lax.broadcasted_iota(jnp.int32, shape, 0), BLOCK - 1)
            - lax.broadcasted_iota(jnp.int32, shape, 1) + offset)
    return jnp.abs(2 * dist - max_dist) <= max_dist


def _stack_heads(lo, x):
    zero = jnp.zeros_like(x)
    return jnp.concatenate([jnp.where(lo, x, zero), jnp.where(lo, zero, x)], axis=0)


def _unstack_heads(lo, x):
    return jnp.where(lo, x[:BLOCK], x[BLOCK:])


NT = (((1,), (1,)), ((), ()))
TN = (((0,), (0,)), ((), ()))


def _attn_fwd(q, k, v, sinks, *, max_dist, name, emit_bf16=False):
    d, sp, wq = q.shape
    nq, nk = wq // LANES, k.shape[2] // LANES
    kdiv = nq // nk
    tq = min(sp, 1024)
    nsub = tq // BLOCK
    has_sink = sinks is not None

    def body(*refs):
        refs = list(refs)
        sink_ref = refs.pop(0) if has_sink else None
        q_ref, k_ref, v_ref, o_ref, lse_ref = refs[:5]
        pair = pl.program_id(1)
        j = pl.program_id(2)
        lo = _low_lanes(BLOCK)
        if has_sink:
            first_head = lax.broadcasted_iota(jnp.int32, (2 * BLOCK, 1), 0) < BLOCK
            sk = jnp.where(first_head, sink_ref[2 * pair], sink_ref[2 * pair + 1])
        for i in range(nsub):
            win, offset = _window(j, i, tq)
            rows = slice(i * BLOCK, (i + 1) * BLOCK)
            kw = k_ref[0, win, :]
            vw = v_ref[0, win, :]
            s = lax.dot_general(_stack_heads(lo, q_ref[0, rows, :]), kw, NT, preferred_element_type=F32) * ATTN_SCALE
            s = jnp.where(_band_valid(offset, max_dist), s, NEG)
            m = jnp.max(s, axis=-1, keepdims=True)
            if has_sink:
                m = jnp.maximum(m, sk)
            p = jnp.exp(s - m)
            l = jnp.sum(p, axis=-1, keepdims=True)
            if has_sink:
                l = l + jnp.exp(sk - m)
            o2 = _unstack_heads(lo, jnp.dot(p.astype(BF16), vw, preferred_element_type=F32) / l)
            o_ref[0, rows, :] = o2
            lse_ref[0, rows, :] = _unstack_heads(lo, m + jnp.log(l))
            if emit_bf16:
                refs[5][0, rows, :] = o2.astype(BF16)

    qspec = pl.BlockSpec((1, tq, LANES), lambda r, p, j: (r, j, p))
    kspec = pl.BlockSpec((1, sp, LANES), lambda r, p, j: (r, 0, p // kdiv))
    in_specs = [qspec, kspec, kspec]
    operands = [q, k, v]
    if has_sink:
        in_specs = [pl.BlockSpec(memory_space=pltpu.SMEM)] + in_specs
        operands = [sinks] + operands
    out_shape = [jax.ShapeDtypeStruct(q.shape, F32), jax.ShapeDtypeStruct(q.shape, F32)]
    if emit_bf16:
        out_shape.append(jax.ShapeDtypeStruct(q.shape, BF16))
    return pl.pallas_call(
        body, name=name, grid=(d, nq, sp // tq), in_specs=in_specs, out_specs=[qspec] * len(out_shape),
        out_shape=out_shape, compiler_params=_params("parallel", "parallel", "arbitrary"),
    )(*operands)


def _attn_bwd(q, k, v, do, oo, lse, sinks, *, max_dist, name):
    d, sp, wq = q.shape
    wk = k.shape[2]
    nq, nk = wq // LANES, wk // LANES
    kdiv = nq // nk
    tq = min(sp, 1024)
    nsub = tq // BLOCK
    has_sink = sinks is not None

    def body(*refs):
        refs = list(refs)
        sink_ref = refs.pop(0) if has_sink else None
        q_ref, k_ref, v_ref, do_ref, oo_ref, lse_ref, dq_ref, dk_ref, dv_ref = refs[:9]
        pk, g, j = pl.program_id(1), pl.program_id(2), pl.program_id(3)

        @pl.when((g == 0) & (j == 0))
        def _():
            dk_ref[...] = jnp.zeros_like(dk_ref)
            dv_ref[...] = jnp.zeros_like(dv_ref)

        lo = _low_lanes(BLOCK)
        if has_sink:
            first_head = lax.broadcasted_iota(jnp.int32, (2 * BLOCK, 1), 0) < BLOCK
            pair = pk * kdiv + g
            sk = jnp.where(first_head, sink_ref[2 * pair], sink_ref[2 * pair + 1])
            sink_acc = jnp.zeros((2 * BLOCK, LANES), F32)
        for i in range(nsub):
            win, offset = _window(j, i, tq)
            rows = slice(i * BLOCK, (i + 1) * BLOCK)
            kw = k_ref[0, win, :]
            vw = v_ref[0, win, :]
            do2 = do_ref[0, rows, :].astype(F32)
            qs = _stack_heads(lo, q_ref[0, rows, :])
            dos = _stack_heads(lo, do2.astype(BF16))
            prod = do2 * oo_ref[0, rows, :]
            delta = jnp.sum(_stack_heads(lo, prod), axis=-1, keepdims=True)
            lse2 = lse_ref[0, rows, :]
            lse_swapped = pltpu.roll(lse2, HEAD_DIM, 1)
            lse_st = jnp.concatenate([jnp.where(lo, lse2, lse_swapped), jnp.where(lo, lse_swapped, lse2)], axis=0)
            s = lax.dot_general(qs, kw, NT, preferred_element_type=F32) * ATTN_SCALE
            s = jnp.where(_band_valid(offset, max_dist), s, NEG)
            p = jnp.exp(s - jnp.tile(lse_st, (1, 2)))
            dv_ref[0, win, :] = lax.dot_general(p.astype(BF16), dos, TN, preferred_element_type=F32) + dv_ref[0, win, :]
            dp = lax.dot_general(dos, vw, NT, preferred_element_type=F32)
            ds = (p * (dp - delta) * ATTN_SCALE).astype(BF16)
            dq_ref[0, rows, :] = _unstack_heads(lo, jnp.dot(ds, kw, preferred_element_type=F32))
            dk_ref[0, win, :] = lax.dot_general(ds, qs, TN, preferred_element_type=F32) + dk_ref[0, win, :]
            if has_sink:
                sink_acc = sink_acc - jnp.exp(sk - lse_st) * delta
        if has_sink:
            dsink_ref = refs[9]

            @pl.when(j == 0)
            def _():
                dsink_ref[...] = jnp.zeros_like(dsink_ref)

            dsink_ref[0] += jnp.where(lo[0:1], jnp.sum(sink_acc[:BLOCK], axis=0, keepdims=True),
                                      jnp.sum(sink_acc[BLOCK:], axis=0, keepdims=True))

    def qmap(r, pk, g, j):
        return (r, j, pk * kdiv + g)

    def kmap(r, pk, g, j):
        return (r, 0, pk)

    qspec = pl.BlockSpec((1, tq, LANES), qmap)
    kspec = pl.BlockSpec((1, sp, LANES), kmap)
    in_specs = [qspec, kspec, kspec, qspec, qspec, qspec]
    operands = [q, k, v, do, oo, lse]
    out_specs = [qspec, kspec, kspec]
    out_shape = [jax.ShapeDtypeStruct((d, sp, wq), F32), jax.ShapeDtypeStruct((d, sp, wk), F32),
                 jax.ShapeDtypeStruct((d, sp, wk), F32)]
    if has_sink:
        in_specs = [pl.BlockSpec(memory_space=pltpu.SMEM)] + in_specs
        operands = [sinks] + operands
        out_specs.append(pl.BlockSpec((1, 1, LANES), lambda r, pk, g, j: (pk * kdiv + g, 0, 0)))
        out_shape.append(jax.ShapeDtypeStruct((nq, 1, LANES), F32))
    return pl.pallas_call(
        body, name=name, grid=(d, nk, kdiv, sp // tq), in_specs=in_specs, out_specs=out_specs, out_shape=out_shape,
        compiler_params=_params("parallel", "parallel", "arbitrary", "arbitrary"),
    )(*operands)


def _combine(outs, lses, name):
    s = outs[0].shape[1]
    tm = 512
    nb = len(DILATIONS)
    ds = [d for d in DILATIONS if d > 1]

    def body(*refs):
        o_refs, l_refs = refs[:nb], refs[nb:2 * nb]
        cb_ref, c_ref, lse_ref = refs[2 * nb:2 * nb + 3]
        folded = refs[2 * nb + 3:2 * nb + 3 + 2 * len(ds)]
        scratch = refs[2 * nb + 3 + 2 * len(ds):]
        so = {1: None}
        sl = {1: None}
        for i, d in enumerate(ds):
            so[d], sl[d] = scratch[2 * i], scratch[2 * i + 1]
            _unfold_load(o_refs[1 + i], so[d], d)
            _unfold_load(l_refs[1 + i], sl[d], d)
        for p in range(N_PAIRS):
            pb = _pair_block(p)
            ls = [l_refs[0][0, :, pb]] + [sl[d][p] for d in ds]
            os_ = [o_refs[0][0, :, pb]] + [so[d][p] for d in ds]
            m = ls[0]
            for t in ls[1:]:
                m = jnp.maximum(m, t)
            ws = [jnp.exp(t - m) for t in ls]
            tot = ws[0]
            for t in ws[1:]:
                tot = tot + t
            acc = ws[0] * os_[0]
            for w, o in zip(ws[1:], os_[1:]):
                acc = acc + w * o
            cmix = acc / tot
            lse = m + jnp.log(tot)
            cb_ref[:, pb] = cmix.astype(BF16)
            c_ref[0, :, pb] = cmix
            lse_ref[0, :, pb] = lse
            so[ds[0]][p] = cmix
            sl[ds[0]][p] = lse
        for i, d in enumerate(ds):
            for r in range(d):
                for p in range(N_PAIRS):
                    rows = pl.ds(r, tm // d, stride=d)
                    folded[2 * i][r, :, _pair_block(p)] = so[ds[0]][p, rows, :]
                    folded[2 * i + 1][r, :, _pair_block(p)] = sl[ds[0]][p, rows, :]

    in_specs = [_folded_spec(d, tm) for _ in range(2) for d in DILATIONS]
    out_specs = [pl.BlockSpec((tm, ATTN_W), lambda i: (i, 0)), _folded_spec(1, tm), _folded_spec(1, tm)]
    out_shape = [jax.ShapeDtypeStruct((s, ATTN_W), BF16), _folded_shape(s, 1, F32), _folded_shape(s, 1, F32)]
    for d in ds:
        out_specs += [_folded_spec(d, tm)] * 2
        out_shape += [_folded_shape(s, d, F32)] * 2
    return pl.pallas_call(
        body, name=name, grid=(s // tm,), in_specs=in_specs, out_specs=out_specs, out_shape=out_shape,
        scratch_shapes=[pltpu.VMEM((N_PAIRS, tm, LANES), F32)] * (2 * len(ds)),
        compiler_params=_params("parallel"),
    )(*outs, *lses)


GLU_A = slice(768, 1280)
GLU_B = slice(1280, 1792)
EVEN_IN = 1792
ODD_IN = 2560
CONV_CH = 512


def _conv_fwd(proj, w, b, ln_g, ln_b, name):
    s = proj.shape[0]
    tm = 512
    nh = tm // CONV_HALO
    lead = CONV_HALO - (CONV_WIDTH - 1)

    def body(p_ref, ph_ref, w_ref, b_ref, g_ref, bb_ref, y_ref, o_ref, xf_ref):
        xf_ref[CONV_HALO:, :] = p_ref[:, GLU_A] * _sigmoid(p_ref[:, GLU_B])
        hist = ph_ref[:, GLU_A] * _sigmoid(ph_ref[:, GLU_B])
        xf_ref[0:CONV_HALO, :] = jnp.where(pl.program_id(0) > 0, hist, 0.0)
        for c0 in range(0, tm, CONV_ROWS):
            acc = jnp.zeros((CONV_ROWS, CONV_CH), F32) + b_ref[...]
            for j in range(CONV_WIDTH):
                acc = acc + xf_ref[pl.ds(lead + j + c0, CONV_ROWS), :] * w_ref[j:j + 1, :]
            y_ref[c0:c0 + CONV_ROWS, :] = acc
            mu = jnp.mean(acc, axis=-1, keepdims=True)
            xc = acc - mu
            var = jnp.mean(xc * xc, axis=-1, keepdims=True)
            zz = xc * lax.rsqrt(var + LN_EPS) * g_ref[...] + bb_ref[...]
            o_ref[c0:c0 + CONV_ROWS, :] = (zz * _sigmoid(zz)).astype(BF16)

    def const(a):
        return pl.BlockSpec(a.shape, lambda i: (0, 0))

    return pl.pallas_call(
        body, name=name, grid=(s // tm,),
        in_specs=[pl.BlockSpec((tm, EVEN_IN), lambda i: (i, 0)),
                  pl.BlockSpec((CONV_HALO, EVEN_IN), lambda i: (jnp.maximum(i * nh - 1, 0), 0)),
                  const(w), const(b), const(ln_g), const(ln_b)],
        out_specs=[pl.BlockSpec((tm, CONV_CH), lambda i: (i, 0)), pl.BlockSpec((tm, CONV_CH), lambda i: (i, 0))],
        out_shape=[jax.ShapeDtypeStruct((s, CONV_CH), F32), jax.ShapeDtypeStruct((s, CONV_CH), BF16)],
        scratch_shapes=[pltpu.VMEM((tm + CONV_HALO, CONV_CH), F32)],
        compiler_params=_params("arbitrary"),
    )(proj, proj, w, b, ln_g, ln_b)


def _conv_tail_bwd(dmix, yconv, ln_g, ln_b, name):
    def body(d_ref, y_ref, g_ref, b_ref, dy_ref, dg_ref, db_ref, dcb_ref):
        @pl.when(_first_step())
        def _():
            dg_ref[...] = jnp.zeros_like(dg_ref)
            db_ref[...] = jnp.zeros_like(db_ref)
            dcb_ref[...] = jnp.zeros_like(dcb_ref)

        y = y_ref[...]
        g = g_ref[...]
        mu = jnp.mean(y, axis=-1, keepdims=True)
        xc = y - mu
        rstd = lax.rsqrt(jnp.mean(xc * xc, axis=-1, keepdims=True) + LN_EPS)
        xh = xc * rstd
        zz = xh * g + b_ref[...]
        sg = _sigmoid(zz)
        dzz = d_ref[:, CONV_CH:] * sg * (1.0 + zz * (1.0 - sg))
        dg_ref[...] += jnp.sum(dzz * xh, axis=0, keepdims=True)
        db_ref[...] += jnp.sum(dzz, axis=0, keepdims=True)
        dxh = dzz * g
        dy = rstd * (dxh - jnp.mean(dxh, axis=-1, keepdims=True) - xh * jnp.mean(dxh * xh, axis=-1, keepdims=True))
        dcb_ref[...] += jnp.sum(dy, axis=0, keepdims=True)
        dy_ref[...] = dy

    vec = ((1, CONV_CH), F32)
    return _rows(body, name, 512, [dmix, yconv], [ln_g, ln_b], [(CONV_CH, F32)], [vec, vec, vec])


def _conv_bwd(proj, dy, w, name):
    s = proj.shape[0]
    tm = 512
    nh = tm // CONV_HALO
    nsteps = s // tm
    lead = CONV_HALO - (CONV_WIDTH - 1)

    def body(p_ref, ph_ref, dy_ref, dyn_ref, w_ref, dglu_ref, dw_ref, xf_ref, dyf_ref):
        i = pl.program_id(0)

        @pl.when(i == 0)
        def _():
            dw_ref[...] = jnp.zeros_like(dw_ref)

        ga = p_ref[:, GLU_A]
        sgb = _sigmoid(p_ref[:, GLU_B])
        xf_ref[CONV_HALO:, :] = ga * sgb
        hist = ph_ref[:, GLU_A] * _sigmoid(ph_ref[:, GLU_B])
        xf_ref[0:CONV_HALO, :] = jnp.where(i > 0, hist, 0.0)
        dyt = dy_ref[...]
        dyf_ref[0:tm, :] = dyt
        dyf_ref[tm:, :] = jnp.where(i < nsteps - 1, dyn_ref[...], 0.0)
        for c0 in range(0, tm, CONV_ROWS):
            rows = slice(c0, c0 + CONV_ROWS)
            acc = jnp.zeros((CONV_ROWS, CONV_CH), F32)
            for j in range(CONV_WIDTH):
                acc = acc + dyf_ref[pl.ds(CONV_WIDTH - 1 - j + c0, CONV_ROWS), :] * w_ref[j:j + 1, :]
            a_c, s_c = ga[rows, :], sgb[rows, :]
            dglu_ref[rows, 0:CONV_CH] = (acc * s_c).astype(BF16)
            dglu_ref[rows, CONV_CH:] = (acc * a_c * s_c * (1.0 - s_c)).astype(BF16)
        for j in range(CONV_WIDTH):
            part = jnp.zeros((8, CONV_CH), F32)
            for c0 in range(0, tm, CONV_ROWS):
                prod = dy_ref[c0:c0 + CONV_ROWS, :] * xf_ref[pl.ds(lead + j + c0, CONV_ROWS), :]
                part = part + jnp.sum(prod.reshape(CONV_ROWS // 8, 8, CONV_CH), axis=0)
            dw_ref[j:j + 1, :] += jnp.sum(part, axis=0, keepdims=True)

    return pl.pallas_call(
        body, name=name, grid=(nsteps,),
        in_specs=[pl.BlockSpec((tm, EVEN_IN), lambda i: (i, 0)),
                  pl.BlockSpec((CONV_HALO, EVEN_IN), lambda i: (jnp.maximum(i * nh - 1, 0), 0)),
                  pl.BlockSpec((tm, CONV_CH), lambda i: (i, 0)),
                  pl.BlockSpec((CONV_HALO, CONV_CH), lambda i: (jnp.minimum((i + 1) * nh, s // CONV_HALO - 1), 0)),
                  pl.BlockSpec(w.shape, lambda i: (0, 0))],
        out_specs=[pl.BlockSpec((tm, 2 * CONV_CH), lambda i: (i, 0)), pl.BlockSpec(w.shape, lambda i: (0, 0))],
        out_shape=[jax.ShapeDtypeStruct((s, 2 * CONV_CH), BF16), jax.ShapeDtypeStruct(w.shape, F32)],
        scratch_shapes=[pltpu.VMEM((tm + CONV_HALO, CONV_CH), F32), pltpu.VMEM((tm + CONV_HALO, CONV_CH), F32)],
        compiler_params=_params("arbitrary"),
    )(proj, proj, dy, dy, w)


GATE_Z = slice(1536, 2560)
D_CH = 512
GELU_C = math.sqrt(2.0 / math.pi)
GELU_K = 0.044715


def _gelu_parts(z):
    t = jnp.tanh(GELU_C * (z + GELU_K * z * z * z))
    return 0.5 * z * (1.0 + t), t


def _lane_group(rows):
    return lax.broadcasted_iota(jnp.int32, (rows, D_CH), 1) // HEAD_DIM


def _tril_mask():
    return lax.broadcasted_iota(jnp.int32, (BLOCK, BLOCK), 0) >= lax.broadcasted_iota(jnp.int32, (BLOCK, BLOCK), 1)


def _layer_norm_parts(x):
    mu = jnp.mean(x, axis=-1, keepdims=True)
    xc = x - mu
    rstd = lax.rsqrt(jnp.mean(xc * xc, axis=-1, keepdims=True) + LN_EPS)
    return xc * rstd, rstd


def _gate_fwd(proj, ln_g, ln_b, w_sp, sb_t, name):
    tm = 512

    def body(p_ref, g_ref, b_ref, w_ref, sb_ref, mixed_ref, out_ref):
        zz, _ = _gelu_parts(p_ref[:, GATE_Z])
        u = zz[:, :D_CH]
        xh, _ = _layer_norm_parts(zz[:, D_CH:])
        gn = (xh * g_ref[...] + b_ref[...]).astype(BF16)
        grp = _lane_group(BLOCK)
        tri = _tril_mask()
        ws = [jnp.where(tri, w_ref[gi], 0.0).astype(BF16) for gi in range(N_GROUPS)]
        bias = jnp.zeros((BLOCK, D_CH), F32)
        for gi in range(N_GROUPS):
            bias = jnp.where(grp == gi, sb_ref[:, gi:gi + 1], bias)
        for ch in range(tm // BLOCK):
            rows = slice(ch * BLOCK, (ch + 1) * BLOCK)
            gc = gn[rows, :]
            mixed = bias
            for gi in range(N_GROUPS):
                r = jnp.dot(ws[gi], gc, preferred_element_type=F32)
                mixed = jnp.where(grp == gi, r + bias, mixed)
            mixed_ref[rows, :] = mixed
            out_ref[rows, :] = (u[rows, :] * mixed).astype(BF16)

    return _rows(body, name, tm, [proj], [ln_g, ln_b, w_sp, sb_t], [(D_CH, F32), (D_CH, BF16)])


def _gate_bwd(dmix, proj, mixed, ln_g, ln_b, w_sp, name):
    tm = 512

    def body(d_ref, p_ref, m_ref, g_ref, b_ref, w_ref, dz_ref, dg_ref, db_ref, dw_ref, dsb_ref, dgn_ref):
        @pl.when(_first_step())
        def _():
            dg_ref[...] = jnp.zeros_like(dg_ref)
            db_ref[...] = jnp.zeros_like(db_ref)
            dw_ref[...] = jnp.zeros_like(dw_ref)
            dsb_ref[...] = jnp.zeros_like(dsb_ref)

        z = p_ref[:, GATE_Z]
        zz, t = _gelu_parts(z)
        u = zz[:, :D_CH]
        xh, rstd = _layer_norm_parts(zz[:, D_CH:])
        g = g_ref[...]
        gn = (xh * g + b_ref[...]).astype(BF16)
        dd = d_ref[:, D_CH:]
        du = dd * m_ref[...]
        dm = dd * u
        grp = _lane_group(BLOCK)
        tri = _tril_mask()
        ws = [jnp.where(tri, w_ref[gi], 0.0).astype(BF16) for gi in range(N_GROUPS)]
        gsel = (lax.broadcasted_iota(jnp.int32, (N_GROUPS, D_CH), 1) // HEAD_DIM
                == lax.broadcasted_iota(jnp.int32, (N_GROUPS, D_CH), 0)).astype(F32)
        for ch in range(tm // BLOCK):
            rows = slice(ch * BLOCK, (ch + 1) * BLOCK)
            dmc = dm[rows, :]
            dmb = dmc.astype(BF16)
            gc = gn[rows, :]
            dgn = jnp.zeros((BLOCK, D_CH), F32)
            for gi in range(N_GROUPS):
                r = lax.dot_general(ws[gi], dmb, TN, preferred_element_type=F32)
                dgn = jnp.where(grp == gi, r, dgn)
                dmg = jnp.where(grp == gi, dmb, jnp.zeros_like(dmb))
                dwg = lax.dot_general(dmg, gc, NT, preferred_element_type=F32)
                dw_ref[gi] += jnp.where(tri, dwg, 0.0)
            dsb_ref[...] += lax.dot_general(gsel, dmc, NT, preferred_element_type=F32, precision=lax.Precision.HIGHEST)
            dgn_ref[rows, :] = dgn
        dgn = dgn_ref[...]
        db_ref[...] += jnp.sum(dgn, axis=0, keepdims=True)
        dg_ref[...] += jnp.sum(dgn * xh, axis=0, keepdims=True)
        dxh = dgn * g
        dgp = rstd * (dxh - jnp.mean(dxh, axis=-1, keepdims=True) - xh * jnp.mean(dxh * xh, axis=-1, keepdims=True))
        dgelu = 0.5 * (1.0 + t) + 0.5 * z * (1.0 - t * t) * GELU_C * (1.0 + 3.0 * GELU_K * z * z)
        dz_ref[:, 0:D_CH] = (du * dgelu[:, :D_CH]).astype(BF16)
        dz_ref[:, D_CH:] = (dgp * dgelu[:, D_CH:]).astype(BF16)

    s = proj.shape[0]
    tiled = [dmix, proj, mixed]
    consts = [ln_g, ln_b, w_sp]
    in_specs = [pl.BlockSpec((tm, a.shape[1]), lambda i: (i, 0)) for a in tiled]
    in_specs += [pl.BlockSpec(a.shape, lambda i, nd=a.ndim: (0,) * nd) for a in consts]
    vec = (1, D_CH)
    acc_shapes = [vec, vec, w_sp.shape, (N_GROUPS, BLOCK)]
    return pl.pallas_call(
        body, name=name, grid=(s // tm,), in_specs=in_specs,
        out_specs=[pl.BlockSpec((tm, 2 * D_CH), lambda i: (i, 0))]
        + [pl.BlockSpec(sh, lambda i, nd=len(sh): (0,) * nd) for sh in acc_shapes],
        out_shape=[jax.ShapeDtypeStruct((s, 2 * D_CH), BF16)] + [jax.ShapeDtypeStruct(sh, F32) for sh in acc_shapes],
        scratch_shapes=[pltpu.VMEM((tm, D_CH), F32)],
        compiler_params=_params("arbitrary"),
    )(*tiled, *consts)


def _adam_update(w, g, m, v):
    nm = ADAM_B1 * m + (1.0 - ADAM_B1) * g
    nv = ADAM_B2 * v + (1.0 - ADAM_B2) * (g * g)
    m_hat = nm / (1.0 - ADAM_B1 ** ADAM_STEP)
    v_hat = nv / (1.0 - ADAM_B2 ** ADAM_STEP)
    return -ADAM_LR * (m_hat / (jnp.sqrt(v_hat) + ADAM_EPS) + ADAM_WD * w), nm, nv


def _adamw(w, g, m, v, name):
    rows, cols = w.shape
    tm = _tile(rows, 512, 8)

    def body(w_ref, g_ref, m_ref, v_ref, d_ref, nm_ref, nv_ref):
        d_ref[...], nm_ref[...], nv_ref[...] = _adam_update(w_ref[...], g_ref[...], m_ref[...], v_ref[...])

    return _rows(body, name, tm, [w, g, m, v], [], [(cols, F32)] * 3)


def _ordered_sum(parts, name):
    n, rows, cols = parts.shape
    tm = _tile(rows, 512, 16 if parts.dtype == BF16 else 8)

    def body(p_ref, o_ref):
        acc = p_ref[0].astype(F32)
        for k in range(1, n):
            acc = acc + p_ref[k].astype(F32)
        o_ref[...] = acc

    return pl.pallas_call(body, name=name, grid=(rows // tm,),
                          in_specs=[pl.BlockSpec((n, tm, cols), lambda i: (0, i, 0))],
                          out_specs=pl.BlockSpec((tm, cols), lambda i: (i, 0)),
                          out_shape=jax.ShapeDtypeStruct((rows, cols), F32), compiler_params=_params("parallel"))(parts)


ANY = pl.BlockSpec(memory_space=pl.ANY)


def _position():
    x, y, c = lax.axis_index("x"), lax.axis_index("y"), lax.axis_index("c")
    other_chips = [(1 - x, y), (x, 1 - y), (1 - x, 1 - y)]
    return x, y, c, other_chips


def _remote(src, dst, send_sem, recv_sem, to):
    return pltpu.make_async_remote_copy(src_ref=src, dst_ref=dst, send_sem=send_sem, recv_sem=recv_sem,
                                        device_id=to, device_id_type=MESH)


STAGE_ROWS = 736


def _staged_copies(copies, buf, in_sems, out_sems):
    n = len(copies)

    def into(u):
        src = copies[u][0]
        return pltpu.make_async_copy(src, buf.at[u % 2, pl.ds(0, src.shape[0]), :], in_sems.at[u % 2])

    def out_of(u):
        dst = copies[u][1]
        return pltpu.make_async_copy(buf.at[u % 2, pl.ds(0, dst.shape[0]), :], dst, out_sems.at[u % 2])

    into(0).start()
    for u in range(n):
        into(u).wait()
        out_of(u).start()
        if u + 1 < n:
            if u >= 1:
                out_of(u - 1).wait()
            into(u + 1).start()
    if n >= 2:
        out_of(n - 2).wait()
    out_of(n - 1).wait()


def _stage_scratch(dtype, cols):
    return [pltpu.VMEM((2, STAGE_ROWS, cols), dtype), pltpu.SemaphoreType.DMA((2,)), pltpu.SemaphoreType.DMA((2,))]


def _row_chunks(rows):
    return [(r, min(STAGE_ROWS, rows - r)) for r in range(0, rows, STAGE_ROWS)]


def _gather_chips(shard, name):
    rows, cols = shard.shape
    half = rows // 2

    def body(in_ref, out_ref, send_sems, recv_sems, buf, in_sems, out_sems):
        x, y, c, chips = _position()
        me = 2 * x + y
        sibling = (x, y, 1 - c)

        def slab(chip, h):
            return out_ref.at[chip, pl.ds(h * half, half), :]

        first = [_remote(in_ref.at[pl.ds(c * half, half), :], slab(me, c), send_sems.at[j], recv_sems.at[j], (cx, cy, c))
                 for j, (cx, cy) in enumerate(chips)]
        for cp in first:
            cp.start()
        _staged_copies([(in_ref.at[pl.ds(r, n), :], out_ref.at[me, pl.ds(r, n), :]) for r, n in _row_chunks(rows)],
                       buf, in_sems, out_sems)
        passed = []
        for j, (cx, cy) in enumerate(chips):
            got = slab(2 * cx + cy, c)
            _remote(got, got, send_sems.at[j], recv_sems.at[j], sibling).wait_recv()
            cp = _remote(got, got, send_sems.at[3 + j], recv_sems.at[3 + j], sibling)
            cp.start()
            passed.append(cp)
        for j, (cx, cy) in enumerate(chips):
            got = slab(2 * cx + cy, 1 - c)
            _remote(got, got, send_sems.at[3 + j], recv_sems.at[3 + j], sibling).wait_recv()
        for cp in first + passed:
            cp.wait_send()

    return pl.pallas_call(
        body, name=name, in_specs=[ANY], out_specs=ANY,
        out_shape=jax.ShapeDtypeStruct((N_CHIPS, rows, cols), shard.dtype),
        scratch_shapes=[pltpu.SemaphoreType.DMA((6,)), pltpu.SemaphoreType.DMA((6,))] + _stage_scratch(shard.dtype, cols),
        compiler_params=pltpu.CompilerParams(vmem_limit_bytes=VMEM_LIMIT),
    )(shard)


HBM = pl.BlockSpec(memory_space=pltpu.HBM)
SEM = pl.BlockSpec(memory_space=pltpu.SEMAPHORE)
SIDE_EFFECT = pltpu.SideEffectType.DATAFLOW_SIDE_EFFECTING


def _ici_copies(in_ref, land_ref, send_sems, recv_sems, half):
    x, y, c, chips = _position()
    mine = pl.ds(c * half, half)
    sends = [_remote(in_ref.at[mine, :], land_ref.at[2 * x + y, mine, :], send_sems.at[j], recv_sems.at[j], (cx, cy, c))
             for j, (cx, cy) in enumerate(chips)]
    arrivals = [_remote(in_ref.at[mine, :], land_ref.at[2 * cx + cy, mine, :], send_sems.at[j], recv_sems.at[j], (cx, cy, c))
                for j, (cx, cy) in enumerate(chips)]
    return sends, arrivals


def _gather_start(shard, after, name):
    rows, cols = shard.shape

    def body(in_ref, land_ref, after_ref, send_sems, recv_sems, in_thru, land_thru, token):
        sends, _ = _ici_copies(in_ref, land_ref, send_sems, recv_sems, rows // 2)
        for cp in sends:
            cp.start()
        token[...] = jnp.zeros_like(token)

    land = lax.empty((N_CHIPS, rows, cols), shard.dtype)
    return pl.pallas_call(
        body, name=name,
        out_shape=(pltpu.SemaphoreType.DMA((3,)), pltpu.SemaphoreType.DMA((3,)), pltpu.HBM(shard.shape, shard.dtype),
                   pltpu.HBM(land.shape, land.dtype), jax.ShapeDtypeStruct((8, LANES), F32)),
        in_specs=(HBM, HBM, ANY), out_specs=(SEM, SEM, HBM, HBM, pl.BlockSpec(memory_space=pltpu.VMEM)),
        input_output_aliases={0: 2, 1: 3},
        compiler_params=pltpu.CompilerParams(has_side_effects=SIDE_EFFECT),
    )(pltpu.with_memory_space_constraint(shard, pltpu.HBM), pltpu.with_memory_space_constraint(land, pltpu.HBM), after)


def _gather_wait(send_sems, recv_sems, shard, land, after, name):
    rows = shard.shape[0]

    def body(in_ref, land_ref, send_sems, recv_sems, after_ref, in_out, land_out):
        sends, arrivals = _ici_copies(in_ref, land_ref, send_sems, recv_sems, rows // 2)
        for cp in sends:
            cp.wait_send()
        for cp in arrivals:
            cp.wait_recv()

    return pl.pallas_call(
        body, name=name, out_shape=(pltpu.HBM(shard.shape, shard.dtype), pltpu.HBM(land.shape, land.dtype)),
        in_specs=(HBM, HBM, SEM, SEM, ANY), out_specs=(HBM, HBM), input_output_aliases={0: 0, 1: 1},
        compiler_params=pltpu.CompilerParams(has_side_effects=SIDE_EFFECT),
    )(shard, land, send_sems, recv_sems, after)


def _gather_finish(shard, land, name):
    rows, cols = shard.shape
    half = rows // 2

    def body(in_ref, land_ref, out_ref, send_sems, recv_sems, buf, in_sems, out_sems):
        x, y, c, chips = _position()
        me = 2 * x + y
        sibling = (x, y, 1 - c)

        def slab(chip, h):
            return out_ref.at[chip, pl.ds(h * half, half), :]

        passed = [_remote(slab(2 * cx + cy, c), slab(2 * cx + cy, c), send_sems.at[j], recv_sems.at[j], sibling)
                  for j, (cx, cy) in enumerate(chips)]
        for cp in passed:
            cp.start()
        _staged_copies([(in_ref.at[pl.ds(r, n), :], out_ref.at[me, pl.ds(r, n), :]) for r, n in _row_chunks(rows)],
                       buf, in_sems, out_sems)
        for j, (cx, cy) in enumerate(chips):
            got = slab(2 * cx + cy, 1 - c)
            _remote(got, got, send_sems.at[j], recv_sems.at[j], sibling).wait_recv()
        for cp in passed:
            cp.wait_send()

    return pl.pallas_call(
        body, name=name, in_specs=[ANY, ANY], out_specs=ANY, out_shape=jax.ShapeDtypeStruct(land.shape, land.dtype),
        input_output_aliases={1: 0},
        scratch_shapes=[pltpu.SemaphoreType.DMA((3,)), pltpu.SemaphoreType.DMA((3,))] + _stage_scratch(shard.dtype, cols),
        compiler_params=pltpu.CompilerParams(vmem_limit_bytes=VMEM_LIMIT),
    )(shard, land)


def _gather_devices(block, name):
    rows, cols = block.shape

    def body(in_ref, out_ref, send_sems, recv_sems, local_sem):
        x, y, c, chips = _position()
        sibling = (x, y, 1 - c)

        def slot(px, py, pc):
            return out_ref.at[4 * px + 2 * py + pc]

        mine = pltpu.make_async_copy(in_ref, slot(x, y, c), local_sem)
        mine.start()
        first = [_remote(in_ref, slot(x, y, c), send_sems.at[0], recv_sems.at[0], sibling)]
        first += [_remote(in_ref, slot(x, y, c), send_sems.at[1 + j], recv_sems.at[1 + j], (cx, cy, c))
                  for j, (cx, cy) in enumerate(chips)]
        for cp in first:
            cp.start()
        passed = []
        for j, (cx, cy) in enumerate(chips):
            got = slot(cx, cy, c)
            _remote(got, got, send_sems.at[1 + j], recv_sems.at[1 + j], sibling).wait_recv()
            cp = _remote(got, got, send_sems.at[4 + j], recv_sems.at[4 + j], sibling)
            cp.start()
            passed.append(cp)
        got = slot(x, y, 1 - c)
        _remote(got, got, send_sems.at[0], recv_sems.at[0], sibling).wait_recv()
        for j, (cx, cy) in enumerate(chips):
            got = slot(cx, cy, 1 - c)
            _remote(got, got, send_sems.at[4 + j], recv_sems.at[4 + j], sibling).wait_recv()
        for cp in first + passed:
            cp.wait_send()
        mine.wait()

    return pl.pallas_call(
        body, name=name, in_specs=[ANY], out_specs=ANY,
        out_shape=jax.ShapeDtypeStruct((N_DEV, rows, cols), block.dtype),
        scratch_shapes=[pltpu.SemaphoreType.DMA((7,)), pltpu.SemaphoreType.DMA((7,)), pltpu.SemaphoreType.DMA],
    )(block)


def _pair_send(grads, name):
    n = len(grads)
    hs = [g.shape[2] for g in grads]
    offs = [sum(hs[:i]) for i in range(n)]
    cols = grads[0].shape[3]

    def body(*refs):
        g_refs = refs[:n]
        got_ref, send_sems, recv_sems = refs[n:]
        x, y, c, _ = _position()
        copies = [_remote(g_ref.at[:, 1 - c], got_ref.at[:, pl.ds(offs[i], hs[i]), :], send_sems.at[i], recv_sems.at[i],
                          (x, y, 1 - c)) for i, g_ref in enumerate(g_refs)]
        for cp in copies:
            cp.start()
        for cp in copies:
            cp.wait()

    return pl.pallas_call(
        body, name=name, in_specs=[ANY] * n, out_specs=ANY, out_shape=jax.ShapeDtypeStruct((N_CHIPS, sum(hs), cols), F32),
        scratch_shapes=[pltpu.SemaphoreType.DMA((n,)), pltpu.SemaphoreType.DMA((n,))],
    )(*grads)


def _pair_add(grads, got, name):
    n = len(grads)
    hs = [g.shape[2] for g in grads]
    offs = [sum(hs[:i]) for i in range(n)]
    cols = grads[0].shape[3]
    hmax = max(hs)
    units = [(i, k) for k in range(N_CHIPS) for i in range(n)]

    def body(*refs):
        g_refs = refs[:n]
        got_ref, out_ref, a_buf, b_buf, o_buf, a_sems, b_sems, o_sems = refs[n:]
        c = lax.axis_index("c")

        def loads(u):
            i, k = units[u]
            slot, rows = u % 2, pl.ds(0, hs[i])
            return (pltpu.make_async_copy(g_refs[i].at[k, c], a_buf.at[slot, rows, :], a_sems.at[slot]),
                    pltpu.make_async_copy(got_ref.at[k, pl.ds(offs[i], hs[i]), :], b_buf.at[slot, rows, :], b_sems.at[slot]))

        def store(u):
            i, k = units[u]
            return pltpu.make_async_copy(o_buf.at[u % 2, pl.ds(0, hs[i]), :], out_ref.at[k, pl.ds(offs[i], hs[i]), :],
                                         o_sems.at[u % 2])

        for cp in loads(0):
            cp.start()
        for u, (i, k) in enumerate(units):
            if u + 1 < len(units):
                for cp in loads(u + 1):
                    cp.start()
            for cp in loads(u):
                cp.wait()
            if u >= 2:
                store(u - 2).wait()
            rows = pl.ds(0, hs[i])
            o_buf[u % 2, rows, :] = (a_buf[u % 2, rows, :] + b_buf[u % 2, rows, :]).astype(BF16)
            store(u).start()
        store(len(units) - 2).wait()
        store(len(units) - 1).wait()

    return pl.pallas_call(
        body, name=name, in_specs=[ANY] * (n + 1), out_specs=ANY,
        out_shape=jax.ShapeDtypeStruct((N_CHIPS, sum(hs), cols), BF16),
        scratch_shapes=[pltpu.VMEM((2, hmax, cols), F32), pltpu.VMEM((2, hmax, cols), F32), pltpu.VMEM((2, hmax, cols), BF16),
                        pltpu.SemaphoreType.DMA((2,)), pltpu.SemaphoreType.DMA((2,)), pltpu.SemaphoreType.DMA((2,))],
        compiler_params=pltpu.CompilerParams(vmem_limit_bytes=VMEM_LIMIT),
    )(*grads, got)


def _chip_exchange(parts, name):
    _, rows, cols = parts.shape

    def body(in_ref, out_ref, send_sems, recv_sems):
        x, y, c, chips = _position()
        sent = [_remote(in_ref.at[2 * cx + cy], out_ref.at[j], send_sems.at[j], recv_sems.at[j], (cx, cy, c))
                for j, (cx, cy) in enumerate(chips)]
        for cp in sent:
            cp.start()
        for cp in sent:
            cp.wait()

    return pl.pallas_call(
        body, name=name, in_specs=[ANY], out_specs=ANY, out_shape=jax.ShapeDtypeStruct((3, rows, cols), parts.dtype),
        scratch_shapes=[pltpu.SemaphoreType.DMA((3,)), pltpu.SemaphoreType.DMA((3,))],
    )(parts)


def _exchange_copies(in_ref, land_ref, send_sems, recv_sems):
    x, y, c, chips = _position()
    return [_remote(in_ref.at[2 * cx + cy], land_ref.at[j], send_sems.at[j], recv_sems.at[j], (cx, cy, c))
            for j, (cx, cy) in enumerate(chips)]


def _exchange_start(parts, name):
    _, rows, cols = parts.shape

    def body(in_ref, land_ref, send_sems, recv_sems, in_thru, land_thru, token):
        for cp in _exchange_copies(in_ref, land_ref, send_sems, recv_sems):
            cp.start()
        token[...] = jnp.zeros_like(token)

    land = lax.empty((3, rows, cols), parts.dtype)
    return pl.pallas_call(
        body, name=name,
        out_shape=(pltpu.SemaphoreType.DMA((3,)), pltpu.SemaphoreType.DMA((3,)), pltpu.HBM(parts.shape, parts.dtype),
                   pltpu.HBM(land.shape, land.dtype), jax.ShapeDtypeStruct((8, LANES), F32)),
        in_specs=(HBM, HBM), out_specs=(SEM, SEM, HBM, HBM, pl.BlockSpec(memory_space=pltpu.VMEM)),
        input_output_aliases={0: 2, 1: 3},
        compiler_params=pltpu.CompilerParams(has_side_effects=SIDE_EFFECT),
    )(pltpu.with_memory_space_constraint(parts, pltpu.HBM), pltpu.with_memory_space_constraint(land, pltpu.HBM))


def _exchange_wait(send_sems, recv_sems, parts, land, after, name):
    def body(in_ref, land_ref, send_sems, recv_sems, after_ref, in_out, land_out):
        for cp in _exchange_copies(in_ref, land_ref, send_sems, recv_sems):
            cp.wait_send()
            cp.wait_recv()

    return pl.pallas_call(
        body, name=name, out_shape=(pltpu.HBM(parts.shape, parts.dtype), pltpu.HBM(land.shape, land.dtype)),
        in_specs=(HBM, HBM, SEM, SEM, ANY), out_specs=(HBM, HBM), input_output_aliases={0: 0, 1: 1},
        compiler_params=pltpu.CompilerParams(has_side_effects=SIDE_EFFECT),
    )(parts, land, send_sems, recv_sems, after)


def _chip_sum(parts, recv, chip, name):
    _, rows, cols = parts.shape
    tm = _tile(rows, 512, 16)

    def body(chip_ref, own_ref, recv_ref, o_ref):
        acc = own_ref[0].astype(F32)
        for j in range(3):
            acc = acc + recv_ref[j].astype(F32)
        o_ref[...] = acc

    return pl.pallas_call(
        body, name=name,
        grid_spec=pltpu.PrefetchScalarGridSpec(
            num_scalar_prefetch=1, grid=(rows // tm,),
            in_specs=[pl.BlockSpec((1, tm, cols), lambda i, chip_ref: (chip_ref[0], i, 0)),
                      pl.BlockSpec((3, tm, cols), lambda i, chip_ref: (0, i, 0))],
            out_specs=pl.BlockSpec((tm, cols), lambda i, chip_ref: (i, 0))),
        out_shape=jax.ShapeDtypeStruct((rows, cols), F32), compiler_params=_params("parallel"),
    )(chip, parts, recv)


def _join_unpack(mine, hs, groups, name):
    n = len(hs)
    offs = [sum(hs[:i]) for i in range(n)]
    cols = mine.shape[1]
    n_out = max(groups) + 1
    base = [2 * sum(h for h, g in zip(hs[:i], groups[:i]) if g == groups[i]) for i in range(n)]
    out_rows = [2 * sum(h for h, g in zip(hs, groups) if g == k) for k in range(n_out)]

    def body(in_ref, *refs):
        outs = refs[:n_out]
        send_sems, recv_sems, buf, in_sems, out_sems = refs[n_out:]
        x, y, c, _ = _position()
        sibling = (x, y, 1 - c)
        sent, local = [], []
        for i in range(n):
            src = in_ref.at[pl.ds(offs[i], hs[i]), :]
            here = outs[groups[i]].at[pl.ds(base[i] + c * hs[i], hs[i]), :]
            cp = _remote(src, here, send_sems.at[i], recv_sems.at[i], sibling)
            cp.start()
            sent.append(cp)
            local.append((src, here))
        _staged_copies(local, buf, in_sems, out_sems)
        for i, cp in enumerate(sent):
            there = outs[groups[i]].at[pl.ds(base[i] + (1 - c) * hs[i], hs[i]), :]
            _remote(there, there, send_sems.at[i], recv_sems.at[i], sibling).wait_recv()
            cp.wait_send()

    assert max(hs) <= STAGE_ROWS
    return pl.pallas_call(
        body, name=name, in_specs=[ANY], out_specs=[ANY] * n_out,
        out_shape=[jax.ShapeDtypeStruct((r, cols), F32) for r in out_rows],
        scratch_shapes=[pltpu.SemaphoreType.DMA((n,)), pltpu.SemaphoreType.DMA((n,))] + _stage_scratch(F32, cols),
        compiler_params=pltpu.CompilerParams(vmem_limit_bytes=VMEM_LIMIT),
    )(mine)


SMALL_ROWS = 16


def _small_rows(n):
    return -(-n // (SMALL_ROWS * LANES)) * SMALL_ROWS


def _pack_small(arrs):
    parts = []
    for a in arrs:
        flat = a.reshape(-1)
        rows = _small_rows(flat.shape[0])
        flat = jnp.pad(flat, (0, rows * LANES - flat.shape[0]))
        parts.append(flat.reshape(rows, LANES))
    return jnp.concatenate(parts, axis=0)


def _unpack_small(packed, shapes):
    out, r = [], 0
    for sh in shapes:
        n = math.prod(sh)
        cnt = _small_rows(n)
        out.append(packed[r:r + cnt].reshape(-1)[:n].reshape(sh))
        r += cnt
    return out


def _ffn_fwd(h, g_norm, w_gate_t, w_up_t, w_down, tag):
    n = _rms_fwd(h, g_norm, f"{tag}_norm")
    act, gate, up = _ffn_gate_up(n, w_gate_t, w_up_t, f"{tag}_gate_up")
    out = _matmul(act, w_down, add=h, name=f"{tag}_down")
    return out, (n, gate, up, act)


def _ffn_bwd(dh, dhb, h_in, saved, g_norm, w_gate_t, w_up_t, w_down, tag, after=None):
    n, gate, up, act = saved
    dgate, dup = _ffn_dact(dhb, w_down, gate, up, f"{tag}_dact", after)
    dw_down = _matmul(act, dhb, trans_a=True, name=f"{tag}_dwdown")
    dw_gate_t = _matmul(dgate, n, trans_a=True, name=f"{tag}_dwgate")
    dw_up_t = _matmul(dup, n, trans_a=True, name=f"{tag}_dwup")
    dh_in, dh_inb, dg = _dn_norm([(dgate, w_gate_t), (dup, w_up_t)], h_in, g_norm, dh, f"{tag}_dnorm")
    return dh_in, dh_inb, dg, dw_gate_t, dw_up_t, dw_down


def _local_step(x, tgt, w, big, late_weights, reduce_early):
    s = x.shape[0]
    tabs = _rope_tables(s)
    grads, gbig = {}, {}

    g_ev = w['ev_norm_g']
    n1 = _rms_fwd(x, g_ev, "ev_norm")
    proj0 = _matmul(n1, big['ev_w_in', 0], trans_b=True, name="ev_in")
    q0, k0, v0 = _qkv_prep_even(proj0, tabs, "ev_qkv")
    sinks = w['ev_sinks'].reshape(-1)
    o0, lse0, o0b = _attn_fwd(q0, k0, v0, sinks, max_dist=BLOCK - 1, name="ev_attn", emit_bf16=True)
    yconv, cout = _conv_fwd(proj0, w['ev_conv_w'][0], w['ev_conv_b'], w['ev_conv_ln_g'], w['ev_conv_ln_b'], "ev_conv")
    mix0 = (o0b[0], cout)
    h1 = _matmul(mix0, big['ev_w_out', 0], add=x, name="ev_out")
    big = {**big, **late_weights(h1)}

    g_f0 = w['ffn_norm_g'][0:1]
    h2, ffn0 = _ffn_fwd(h1, g_f0, big['ffn_w_gate', 0], big['ffn_w_up', 0], big['ffn_w_down', 0], "ffn0")

    g_od = w['od_norm_g']
    n3 = _rms_fwd(h2, g_od, "od_norm")
    proj1 = _matmul(n3, big['od_w_in', 0], trans_b=True, name="od_in")
    qkv = _qkv_prep_odd(proj1, tabs, "od_qkv")
    nb = len(DILATIONS)
    outs, lses = [], []
    for i, d in enumerate(DILATIONS):
        o_r, lse_r = _attn_fwd(qkv[i], qkv[nb + i], qkv[2 * nb + i], None, max_dist=BLOCK, name=f"od_attn{d}")
        outs.append(o_r)
        lses.append(lse_r)
    comb = _combine(outs, lses, "od_combine")
    c_bf16 = comb[0]
    c_fold = {1: comb[1]}
    lse_fold = {1: comb[2]}
    for i, d in enumerate(DILATIONS[1:]):
        c_fold[d], lse_fold[d] = comb[3 + 2 * i], comb[4 + 2 * i]
    w_sp = w['od_spatial_w'][0]
    sb_t = w['od_spatial_b'][0].T
    mixed, dout = _gate_fwd(proj1, w['od_sgu_ln_g'], w['od_sgu_ln_b'], w_sp, sb_t, "od_gate")
    mix1 = (c_bf16, dout)
    h3 = _matmul(mix1, big['od_w_out', 0], add=h2, name="od_out")

    g_f1 = w['ffn_norm_g'][1:2]
    h4, ffn1 = _ffn_fwd(h3, g_f1, big['ffn_w_gate', 1], big['ffn_w_up', 1], big['ffn_w_down', 1], "ffn1")

    dh4, dh4b, dg_final, loss_tile = _final_loss(h4, w['final_norm_g'].reshape(1, D_MODEL), tgt, "final")
    grads['final_norm_g'] = dg_final.reshape(D_MODEL)

    dh3, dh3b, dg_f1, gbig['ffn_w_gate', 1], gbig['ffn_w_up', 1], gbig['ffn_w_down', 1] = _ffn_bwd(
        dh4, dh4b, h3, ffn1, g_f1, big['ffn_w_gate', 1], big['ffn_w_up', 1], big['ffn_w_down', 1], "ffn1")

    dmix1 = _matmul(dh3b, big['od_w_out', 0], trans_b=True, name="od_dmix")
    gbig['od_w_out', 0] = _matmul_tn_pair(mix1[0], mix1[1], dh3b, "od_dwout")
    do_fold = dict(zip(DILATIONS[1:], _fold_dout(dmix1, "od_fold_dout")))
    do_fold[1] = dmix1[None]
    dqs, dks, dvs = [], [], []
    for i, d in enumerate(DILATIONS):
        dq_r, dk_r, dv_r = _attn_bwd(qkv[i], qkv[nb + i], qkv[2 * nb + i], do_fold[d], c_fold[d], lse_fold[d], None,
                                     max_dist=BLOCK, name=f"od_dattn{d}")
        dqs.append(dq_r)
        dks.append(dk_r)
        dvs.append(dv_r)
    dz, dg_sgu, db_sgu, dw_sp, dsb = _gate_bwd(dmix1, proj1, mixed, w['od_sgu_ln_g'], w['od_sgu_ln_b'], w_sp, "od_dgate")
    grads['od_sgu_ln_g'], grads['od_sgu_ln_b'] = dg_sgu, db_sgu
    grads['od_spatial_w'], grads['od_spatial_b'] = dw_sp[None], dsb[None]
    dproj1 = _qkv_post_odd(dqs, dks, dvs, dz, tabs, "od_dproj")
    gbig['od_w_in', 0] = _matmul(dproj1, n3, trans_a=True, name="od_dwin")
    dh2, dh2b, dg_od = _dn_norm([(dproj1, big['od_w_in', 0])], h2, g_od, dh3, "od_dnorm")
    grads['od_norm_g'] = dg_od
    token = reduce_early(0, gbig)

    dh1, dh1b, dg_f0, gbig['ffn_w_gate', 0], gbig['ffn_w_up', 0], gbig['ffn_w_down', 0] = _ffn_bwd(
        dh2, dh2b, h1, ffn0, g_f0, big['ffn_w_gate', 0], big['ffn_w_up', 0], big['ffn_w_down', 0], "ffn0", token)
    grads['ffn_norm_g'] = jnp.concatenate([dg_f0, dg_f1], axis=0)
    token = reduce_early(1, gbig)

    dmix0 = _matmul(dh1b, big['ev_w_out', 0], trans_b=True, name="ev_dmix", after=token)
    gbig['ev_w_out', 0] = _matmul_tn_pair(mix0[0], mix0[1], dh1b, "ev_dwout")
    dq0, dk0, dv0, dsink = _attn_bwd(q0, k0, v0, dmix0[None], o0, lse0, sinks, max_dist=BLOCK - 1, name="ev_dattn")
    grads['ev_sinks'] = dsink[:, 0, :].reshape(N_PAIRS, 2, HEAD_DIM)[:, :, 0].reshape(1, 8)
    dyc, dg_cln, db_cln, dcb = _conv_tail_bwd(dmix0, yconv, w['ev_conv_ln_g'], w['ev_conv_ln_b'], "ev_dconv_tail")
    grads['ev_conv_ln_g'], grads['ev_conv_ln_b'], grads['ev_conv_b'] = dg_cln, db_cln, dcb
    dglu, dconv_w = _conv_bwd(proj0, dyc, w['ev_conv_w'][0], "ev_dconv")
    grads['ev_conv_w'] = dconv_w[None]
    dproj0 = _qkv_post_even(dq0, dk0, dv0, dglu, tabs, "ev_dproj")
    gbig['ev_w_in', 0] = _matmul(dproj0, n1, trans_a=True, name="ev_dwin")
    dx, _, dg_ev = _dn_norm([(dproj0, big['ev_w_in', 0])], x, g_ev, dh1, "ev_dnorm")
    grads['ev_norm_g'] = dg_ev
    return loss_tile, dx, grads, gbig


def _shard_rows(w, layer, by_cols):
    return w[layer].T if by_cols else w[layer]


def kernel(x, ev_norm_g, ev_w_in, ev_sinks, ev_conv_w, ev_conv_b, ev_conv_ln_g, ev_conv_ln_b, ev_w_out, od_norm_g, od_w_in, od_sgu_ln_g, od_sgu_ln_b, od_spatial_w, od_spatial_b, od_w_out, ffn_norm_g, ffn_w_gate, ffn_w_up, ffn_w_down, final_norm_g, loss_target, m_ev_norm_g, m_ev_w_in, m_ev_sinks, m_ev_conv_w, m_ev_conv_b, m_ev_conv_ln_g, m_ev_conv_ln_b, m_ev_w_out, m_od_norm_g, m_od_w_in, m_od_sgu_ln_g, m_od_sgu_ln_b, m_od_spatial_w, m_od_spatial_b, m_od_w_out, m_ffn_norm_g, m_ffn_w_gate, m_ffn_w_up, m_ffn_w_down, m_final_norm_g, v_ev_norm_g, v_ev_w_in, v_ev_sinks, v_ev_conv_w, v_ev_conv_b, v_ev_conv_ln_g, v_ev_conv_ln_b, v_ev_w_out, v_od_norm_g, v_od_w_in, v_od_sgu_ln_g, v_od_sgu_ln_b, v_od_spatial_w, v_od_spatial_b, v_od_w_out, v_ffn_norm_g, v_ffn_w_gate, v_ffn_w_up, v_ffn_w_down, v_final_norm_g):
    given = dict(locals())
    wts = {n: given[n] for n in WEIGHTS}
    mom = {n: given["m_" + n] for n in WEIGHTS}
    var = {n: given["v_" + n] for n in WEIGHTS}
    chip = 2 * lax.axis_index("x") + lax.axis_index("y")

    shard_rows = [_shard_rows(wts[n], layer, by_cols).astype(BF16) for n, layer, by_cols in BIG]
    counts = [a.shape[0] for a in shard_rows]
    n_first = sum(n.startswith('ev_') for n, _, _ in BIG)

    def unpack(stacked, entries, cnts):
        out, r = {}, 0
        for (n, layer, _), cnt in zip(entries, cnts):
            out[n, layer] = stacked[:, r:r + cnt].reshape(N_CHIPS * cnt, D_MODEL)
            r += cnt
        return out

    first_w = _gather_chips(jnp.concatenate(shard_rows[:n_first], axis=0), "gather_weights_ev")
    big = unpack(first_w, BIG[:n_first], counts[:n_first])
    send_sems, recv_sems, late_shard, late_land, token = _gather_start(jnp.concatenate(shard_rows[n_first:], axis=0),
                                                                      first_w, "gather_weights_start")

    def late_weights(after):
        shard, land = _gather_wait(send_sems, recv_sems, late_shard, late_land, after, "gather_weights_wait")
        return unpack(_gather_finish(shard, land, "gather_weights_finish"), BIG[n_first:], counts[n_first:])

    full = {n: wts[n] for n in SMALL_REPL}
    full['ev_norm_g'] = full['ev_norm_g'] + token[0:1, 0:1]
    small_shards = [wts[n] for n in SMALL_SHARDED]
    small_shapes = [a.shape for a in small_shards]
    all_s = _gather_chips(_pack_small(small_shards), "gather_small_weights")
    per_chip = [_unpack_small(all_s[k], small_shapes) for k in range(N_CHIPS)]
    for i, n in enumerate(SMALL_SHARDED):
        full[n] = jnp.concatenate([per_chip[k][i] for k in range(N_CHIPS)], axis=-1)

    half_rows = {(n, layer): cnt // 2 for (n, layer, _), cnt in zip(BIG, counts)}
    in_flight = []

    def pair_sum(stage, gbig):
        split = [gbig[e].reshape(N_CHIPS, 2, half_rows[e], D_MODEL) for e in GRAD_STAGES[stage]]
        got = _pair_send(split, f"grad_pair_send{stage}")
        return _pair_add(split, got, f"grad_pair_add{stage}")

    def reduce_early(stage, gbig):
        *handles, token = _exchange_start(pair_sum(stage, gbig), f"grad_exchange_start{stage}")
        in_flight.append(handles)
        return token

    loss_tile, grad_x, grads, gbig = _local_step(x[0], loss_target[0], full, big, late_weights, reduce_early)
    loss = lax.psum(loss_tile[0, 0], ("x", "y", "c"))

    reduced = {}
    for stage, entries in enumerate(GRAD_STAGES):
        if stage < len(in_flight):
            chip_part, from_chips = _exchange_wait(*in_flight[stage], grad_x, f"grad_exchange_wait{stage}")
        else:
            chip_part = pair_sum(stage, gbig)
            from_chips = _chip_exchange(chip_part, f"grad_chip_exchange{stage}")
        my_half = _chip_sum(chip_part, from_chips, chip.reshape(1), f"grad_chip_sum{stage}")
        joined = _join_unpack(my_half, [half_rows[e] for e in entries], list(range(len(entries))), f"grad_join_halves{stage}")
        reduced.update(zip(entries, joined))

    small_names = SMALL_REPL + SMALL_SHARDED
    small_full_shapes = [grads[n].shape for n in small_names]
    spack = _pack_small([grads[n] for n in small_names])
    s_all = _gather_devices(spack, "grad_small_gather")
    s_sum = _unpack_small(_ordered_sum(s_all, "grad_small_sum"), small_full_shapes)
    g_all = dict(zip(small_names, s_sum))
    for n in SMALL_SHARDED:
        width = wts[n].shape[-1]
        g_all[n] = lax.dynamic_slice_in_dim(g_all[n], chip * width, width, axis=g_all[n].ndim - 1)

    delta, new_m, new_v = {}, {}, {}
    for n in BIG_NAMES:
        by_cols = [bc for nn, _, bc in BIG if nn == n][0]
        layers = wts[n].shape[0]

        def as_rows(a):
            return (jnp.swapaxes(a, 1, 2) if by_cols else a).reshape(-1, D_MODEL)

        def from_rows(a):
            a = a.reshape(layers, -1, D_MODEL)
            return jnp.swapaxes(a, 1, 2) if by_cols else a

        g_rows = [reduced[n, layer] for layer in range(layers)]
        g_rows = g_rows[0] if layers == 1 else jnp.concatenate(g_rows, axis=0)
        updated = _adamw(as_rows(wts[n]), g_rows, as_rows(mom[n]), as_rows(var[n]), f"adamw_{n}")
        g_all[n] = from_rows(g_rows)
        delta[n], new_m[n], new_v[n] = (from_rows(a) for a in updated)
    shapes = [wts[n].shape for n in small_names]
    d_s, m_s, v_s = _adamw(*[_pack_small([src[n] for n in small_names]) for src in (wts, g_all, mom, var)], "adamw_small")
    for dst, packed in ((delta, d_s), (new_m, m_s), (new_v, v_s)):
        dst.update(zip(small_names, _unpack_small(packed, shapes)))

    return (loss, grad_x[None], *[g_all[n] for n in WEIGHTS], *[delta[n] for n in WEIGHTS],
            *[new_m[n] for n in WEIGHTS], *[new_v[n] for n in WEIGHTS])
```

```python
import math

import jax
import jax.numpy as jnp
from jax import lax
from jax.experimental import pallas as pl
from jax.experimental.pallas import tpu as pltpu

F32 = jnp.float32
BF16 = jnp.bfloat16

D_MODEL = 1024
HEAD_DIM = 64
ROT_DIM = 16
ROPE_THETA = 500000.0
RMS_EPS = 1e-6
LN_EPS = 1e-5
BLOCK = 128
CONV_WIDTH = 31
CONV_HALO = 32
CONV_ROWS = 64
D_FF = 2816
N_GROUPS = 8
ATTN_W = 512
ATTN_SCALE = HEAD_DIM ** -0.5
NEG = -1e30
DILATIONS = (1, 4, 16)

ADAM_LR = 0.001
ADAM_B1 = 0.9
ADAM_B2 = 0.999
ADAM_EPS = 1e-08
ADAM_WD = 0.01
ADAM_STEP = 10

LANES = 128
N_PAIRS = ATTN_W // LANES
VMEM_LIMIT = 56 * 1024 * 1024
MESH = pl.DeviceIdType.MESH
N_CHIPS = 4
N_DEV = 8

WEIGHTS = ['ev_norm_g', 'ev_w_in', 'ev_sinks', 'ev_conv_w', 'ev_conv_b', 'ev_conv_ln_g', 'ev_conv_ln_b', 'ev_w_out',
           'od_norm_g', 'od_w_in', 'od_sgu_ln_g', 'od_sgu_ln_b', 'od_spatial_w', 'od_spatial_b', 'od_w_out',
           'ffn_norm_g', 'ffn_w_gate', 'ffn_w_up', 'ffn_w_down', 'final_norm_g']
BIG = [('ev_w_in', 0, True), ('ev_w_out', 0, False), ('od_w_in', 0, True), ('od_w_out', 0, False),
       ('ffn_w_gate', 0, True), ('ffn_w_gate', 1, True), ('ffn_w_up', 0, True), ('ffn_w_up', 1, True),
       ('ffn_w_down', 0, False), ('ffn_w_down', 1, False)]
BIG_NAMES = ['ev_w_in', 'ev_w_out', 'od_w_in', 'od_w_out', 'ffn_w_gate', 'ffn_w_up', 'ffn_w_down']
GRAD_STAGES = ([('od_w_in', 0), ('od_w_out', 0), ('ffn_w_gate', 1), ('ffn_w_up', 1), ('ffn_w_down', 1)],
               [('ffn_w_gate', 0), ('ffn_w_up', 0), ('ffn_w_down', 0)],
               [('ev_w_in', 0), ('ev_w_out', 0)])
SMALL_SHARDED = ['ev_conv_w', 'od_norm_g', 'od_sgu_ln_g', 'od_sgu_ln_b']
SMALL_REPL = ['ev_norm_g', 'ev_sinks', 'ev_conv_b', 'ev_conv_ln_g', 'ev_conv_ln_b', 'od_spatial_w', 'od_spatial_b',
              'ffn_norm_g', 'final_norm_g']


def _tile(n, cap, mult=LANES):
    best = None
    for t in range(mult, min(n, cap) + 1, mult):
        if n % t == 0:
            best = t
    assert best is not None, (n, cap)
    return best


def _params(*sem):
    return pltpu.CompilerParams(dimension_semantics=sem, vmem_limit_bytes=VMEM_LIMIT)


def _sigmoid(x):
    return 1.0 / (1.0 + jnp.exp(-x))


def _pair_block(p):
    return slice(p * LANES, (p + 1) * LANES)


def _matmul(a, b, *, name, trans_a=False, trans_b=False, add=None, out_dtype=F32, after=None):
    parts = a if isinstance(a, (tuple, list)) else (a,)
    if trans_a:
        k, m = parts[0].shape
    else:
        m = parts[0].shape[0]
        k = sum(p.shape[1] for p in parts)
    if trans_b:
        n, k2 = b.shape
    else:
        k2, n = b.shape
    assert k == k2 and b.dtype == BF16 and all(p.dtype == BF16 for p in parts)
    tm = _tile(m, D_FF // 2 if trans_a else 512)
    tn = _tile(n, D_FF // 2)
    tk = k if k <= D_FF else _tile(k, 2048)
    nk = k // tk
    na = len(parts)
    assert na == 1 or (nk == 1 and not trans_a)
    assert nk == 1 or out_dtype == F32
    dims = (((0 if trans_a else 1,), (1 if trans_b else 0,)), ((), ()))
    has_add = add is not None

    def body(*refs):
        a_refs, b_ref = refs[:na], refs[na]
        add_ref = refs[na + 1] if has_add else None
        o_ref = refs[na + 1 + has_add + (after is not None)]
        def product():
            a_val = a_refs[0][...] if na == 1 else jnp.concatenate([r[...] for r in a_refs], axis=1)
            return lax.dot_general(a_val, b_ref[...], dims, preferred_element_type=F32)

        if nk == 1:
            part = product()
            if has_add:
                part = part + add_ref[...]
            o_ref[...] = part.astype(o_ref.dtype)
            return
        kk = pl.program_id(2)

        @pl.when(kk == 0)
        def _():
            o_ref[...] = product() + add_ref[...] if has_add else product()

        @pl.when(kk > 0)
        def _():
            o_ref[...] = product() + o_ref[...]

    if trans_a:
        a_specs = [pl.BlockSpec((tk, tm), lambda i, j, kk: (kk, i))]
    elif na == 1:
        a_specs = [pl.BlockSpec((tm, tk), lambda i, j, kk: (i, kk))]
    else:
        a_specs = [pl.BlockSpec((tm, p.shape[1]), lambda i, j, kk: (i, 0)) for p in parts]
    b_spec = pl.BlockSpec((tn, tk), lambda i, j, kk: (j, kk)) if trans_b else pl.BlockSpec((tk, tn), lambda i, j, kk: (kk, j))
    o_spec = pl.BlockSpec((tm, tn), lambda i, j, kk: (i, j))
    in_specs = a_specs + [b_spec] + ([o_spec] if has_add else [])
    operands = list(parts) + [b] + ([add] if has_add else [])
    if after is not None:
        in_specs.append(_after_spec(after))
        operands.append(after)
    return pl.pallas_call(
        body, name=name, grid=(m // tm, n // tn, nk), in_specs=in_specs, out_specs=o_spec,
        out_shape=jax.ShapeDtypeStruct((m, n), out_dtype),
        compiler_params=_params("parallel", "parallel", "arbitrary"),
    )(*operands)


def _matmul_tn_pair(a1, a2, b, name):
    kdim, m1 = a1.shape
    m2 = a2.shape[1]
    n = b.shape[1]
    tn = _tile(n, 1024)
    tk = _tile(kdim, 2048)
    nk = kdim // tk
    dims = (((0,), (0,)), ((), ()))

    def body(a1_ref, a2_ref, b_ref, o_ref):
        kk = pl.program_id(1)
        def products():
            bv = b_ref[...]
            return (lax.dot_general(a1_ref[...], bv, dims, preferred_element_type=F32),
                    lax.dot_general(a2_ref[...], bv, dims, preferred_element_type=F32))

        @pl.when(kk == 0)
        def _():
            o_ref[0:m1, :], o_ref[m1:, :] = products()

        @pl.when(kk > 0)
        def _():
            top, bot = products()
            o_ref[0:m1, :] = top + o_ref[0:m1, :]
            o_ref[m1:, :] = bot + o_ref[m1:, :]

    return pl.pallas_call(
        body, name=name, grid=(n // tn, nk),
        in_specs=[pl.BlockSpec((tk, m1), lambda j, kk: (kk, 0)), pl.BlockSpec((tk, m2), lambda j, kk: (kk, 0)),
                  pl.BlockSpec((tk, tn), lambda j, kk: (kk, j))],
        out_specs=pl.BlockSpec((m1 + m2, tn), lambda j, kk: (0, j)),
        out_shape=jax.ShapeDtypeStruct((m1 + m2, n), F32),
        compiler_params=_params("parallel", "arbitrary"),
    )(a1, a2, b)


def _ffn_gate_up(n, w_gate_t, w_up_t, name):
    m, k = n.shape
    f = w_gate_t.shape[0]
    tm, tn = _tile(m, 1024), _tile(f, D_FF // 2)

    def body(n_ref, wg_ref, wu_ref, act_ref, gate_ref, up_ref):
        a = n_ref[...]
        for cols in _col_chunks(tn):
            gate = lax.dot_general(a, wg_ref[cols, :], NT, preferred_element_type=F32)
            up = lax.dot_general(a, wu_ref[cols, :], NT, preferred_element_type=F32)
            act_ref[:, cols] = (gate * _sigmoid(gate) * up).astype(BF16)
            gate_ref[:, cols] = gate.astype(BF16)
            up_ref[:, cols] = up.astype(BF16)

    wspec = pl.BlockSpec((tn, k), lambda j, i: (j, 0))
    ospec = pl.BlockSpec((tm, tn), lambda j, i: (i, j))
    return pl.pallas_call(
        body, name=name, grid=(f // tn, m // tm), in_specs=[pl.BlockSpec((tm, k), lambda j, i: (i, 0)), wspec, wspec],
        out_specs=[ospec] * 3, out_shape=[jax.ShapeDtypeStruct((m, f), BF16)] * 3,
        compiler_params=_params("parallel", "parallel"),
    )(n, w_gate_t, w_up_t)


def _col_chunks(n, width=384):
    return [slice(c, min(c + width, n)) for c in range(0, n, width)]


def _after_spec(after):
    return pl.BlockSpec(after.shape, lambda *_: (0,) * after.ndim)


def _ffn_dact(dhb, w_down, gate, up, name, after=None):
    m, k = dhb.shape
    f = w_down.shape[0]
    tm, tn = _tile(m, 1024), _tile(f, D_FF // 2)

    def body(d_ref, w_ref, g_ref, u_ref, *rest):
        dg_ref, du_ref = rest[-2:]
        d = d_ref[...]
        for cols in _col_chunks(tn):
            dact = lax.dot_general(d, w_ref[cols, :], NT, preferred_element_type=F32)
            g = g_ref[:, cols].astype(F32)
            sg = _sigmoid(g)
            dg_ref[:, cols] = (dact * u_ref[:, cols].astype(F32) * sg * (1.0 + g * (1.0 - sg))).astype(BF16)
            du_ref[:, cols] = (dact * g * sg).astype(BF16)

    ospec = pl.BlockSpec((tm, tn), lambda j, i: (i, j))
    extra = [] if after is None else [after]
    return pl.pallas_call(
        body, name=name, grid=(f // tn, m // tm),
        in_specs=[pl.BlockSpec((tm, k), lambda j, i: (i, 0)), pl.BlockSpec((tn, k), lambda j, i: (j, 0)), ospec, ospec]
        + [_after_spec(a) for a in extra],
        out_specs=[ospec] * 2, out_shape=[jax.ShapeDtypeStruct((m, f), BF16)] * 2,
        compiler_params=_params("parallel", "parallel"),
    )(dhb, w_down, gate, up, *extra)


def _dn_norm(pairs, h, g, dres, name):
    m = h.shape[0]
    tm = 512
    np_ = len(pairs)

    def body(*refs):
        a_refs, b_refs = refs[:np_], refs[np_:2 * np_]
        h_ref, dres_ref, g_ref, dh_ref, dhb_ref, dg_ref = refs[2 * np_:]

        @pl.when(_first_step())
        def _():
            dg_ref[...] = jnp.zeros_like(dg_ref)

        dy = jnp.dot(a_refs[0][...], b_refs[0][...], preferred_element_type=F32)
        for a_ref, b_ref in zip(a_refs[1:], b_refs[1:]):
            dy = jnp.dot(a_ref[...], b_ref[...], preferred_element_type=F32) + dy
        x = h_ref[...]
        r = lax.rsqrt(jnp.mean(x * x, axis=-1, keepdims=True) + RMS_EPS)
        xh = x * r
        dg_ref[...] += jnp.sum(dy * xh, axis=0, keepdims=True)
        dxh = dy * g_ref[...]
        tot = dres_ref[...] + r * (dxh - xh * jnp.mean(dxh * xh, axis=-1, keepdims=True))
        dh_ref[...] = tot
        dhb_ref[...] = tot.astype(BF16)

    row = lambda w: pl.BlockSpec((tm, w), lambda i: (i, 0))
    whole = lambda a: pl.BlockSpec(a.shape, lambda i: (0, 0))
    a_list, b_list = [a for a, _ in pairs], [b for _, b in pairs]
    return pl.pallas_call(
        body, name=name, grid=(m // tm,),
        in_specs=[row(a.shape[1]) for a in a_list] + [whole(b) for b in b_list] + [row(D_MODEL), row(D_MODEL), whole(g)],
        out_specs=[row(D_MODEL), row(D_MODEL), pl.BlockSpec((1, D_MODEL), lambda i: (0, 0))],
        out_shape=[jax.ShapeDtypeStruct((m, D_MODEL), F32), jax.ShapeDtypeStruct((m, D_MODEL), BF16),
                   jax.ShapeDtypeStruct((1, D_MODEL), F32)],
        compiler_params=_params("arbitrary"),
    )(*a_list, *b_list, h, dres, g)


def _rows(body, name, tm, tiled, consts, outs, accs=()):
    s = tiled[0].shape[0]
    assert s % tm == 0
    in_specs = [pl.BlockSpec((tm, a.shape[1]), lambda i: (i, 0)) for a in tiled]
    in_specs += [pl.BlockSpec(a.shape, lambda i, nd=a.ndim: (0,) * nd) for a in consts]
    out_shape = [jax.ShapeDtypeStruct((s, c), dt) for c, dt in outs]
    out_shape += [jax.ShapeDtypeStruct(sh, dt) for sh, dt in accs]
    out_specs = [pl.BlockSpec((tm, c), lambda i: (i, 0)) for c, _ in outs]
    out_specs += [pl.BlockSpec(sh, lambda i, nd=len(sh): (0,) * nd) for sh, _ in accs]
    return pl.pallas_call(
        body, name=name, grid=(s // tm,), in_specs=in_specs, out_specs=out_specs, out_shape=out_shape,
        compiler_params=_params("arbitrary"),
    )(*tiled, *consts)


def _first_step():
    return pl.program_id(0) == 0


def _rms_fwd(h, g, name):
    def body(h_ref, g_ref, n_ref):
        x = h_ref[...]
        r = lax.rsqrt(jnp.mean(x * x, axis=-1, keepdims=True) + RMS_EPS)
        n_ref[...] = (x * r * g_ref[...]).astype(BF16)

    return _rows(body, name, 512, [h], [g], [(D_MODEL, BF16)])[0]


def _final_loss(h, g, tgt, name):
    def body(h_ref, t_ref, g_ref, dh_ref, dhb_ref, dg_ref, loss_ref):
        @pl.when(_first_step())
        def _():
            dg_ref[...] = jnp.zeros_like(dg_ref)
            loss_ref[...] = jnp.zeros_like(loss_ref)

        x = h_ref[...]
        r = lax.rsqrt(jnp.mean(x * x, axis=-1, keepdims=True) + RMS_EPS)
        xh = x * r
        gg = g_ref[...]
        e = xh * gg - t_ref[...]
        loss_ref[...] += (0.5 / D_MODEL) * jnp.sum(jnp.sum(e * e, axis=-1, keepdims=True), axis=0, keepdims=True)
        dy = e * (1.0 / D_MODEL)
        dg_ref[...] += jnp.sum(dy * xh, axis=0, keepdims=True)
        dxh = dy * gg
        dx = r * (dxh - xh * jnp.mean(dxh * xh, axis=-1, keepdims=True))
        dh_ref[...] = dx
        dhb_ref[...] = dx.astype(BF16)

    return _rows(body, name, 512, [h, tgt], [g], [(D_MODEL, F32), (D_MODEL, BF16)],
                 [((1, D_MODEL), F32), ((1, LANES), F32)])


def _rope_tables(s):
    half = ROT_DIM // 2
    inv_freq = ROPE_THETA ** (-jnp.arange(half, dtype=F32) * (2.0 / ROT_DIM))
    ang = jnp.arange(s, dtype=F32)[:, None] * inv_freq[None, :]
    cos, sin = jnp.cos(ang), jnp.sin(ang)
    rest = HEAD_DIM - ROT_DIM
    ones = jnp.ones((s, rest), F32)
    zeros = jnp.zeros((s, rest), F32)
    zh = jnp.zeros((s, half), F32)
    c_t = jnp.concatenate([cos, cos, ones], axis=1)
    a_t = jnp.concatenate([-sin, zh, zeros], axis=1)
    b_t = jnp.concatenate([zh, sin, zeros], axis=1)
    return tuple(jnp.tile(t, (1, LANES // HEAD_DIM)) for t in (c_t, a_t, b_t))


def _rot(x, c, a, b):
    w = x.shape[1]
    half = ROT_DIM // 2
    return x * c + pltpu.roll(x, w - half, 1) * a + pltpu.roll(x, half, 1) * b


def _wide(t, w):
    return t if w == LANES else jnp.tile(t, (1, w // LANES))


def _low_lanes(rows):
    return lax.broadcasted_iota(jnp.int32, (rows, LANES), 1) < HEAD_DIM


def _fold_store(x, sc_ref, out_refs):
    tm = x.shape[0]
    if any(d > 1 for d in out_refs):
        for p in range(N_PAIRS):
            sc_ref[p] = x[:, _pair_block(p)]
    for d, o_ref in out_refs.items():
        if d == 1:
            o_ref[0] = x.astype(o_ref.dtype)
            continue
        for r in range(d):
            for p in range(N_PAIRS):
                o_ref[r, :, _pair_block(p)] = sc_ref[p, pl.ds(r, tm // d, stride=d), :].astype(o_ref.dtype)


def _unfold_load(x_ref, sc_ref, d, add=False):
    n = x_ref.shape[1]
    for r in range(d):
        for p in range(N_PAIRS):
            rows = pl.ds(r, n, stride=d) if d > 1 else slice(None)
            val = x_ref[r, :, _pair_block(p)].astype(F32)
            if add:
                val = val + sc_ref[p, rows, :]
            sc_ref[p, rows, :] = val


def _folded_spec(d, tm, w=ATTN_W):
    return pl.BlockSpec((d, tm // d, w), lambda i: (0, i, 0))


def _folded_shape(s, d, dtype, w=ATTN_W):
    return jax.ShapeDtypeStruct((d, s // d, w), dtype)


def _qkv_prep_even(proj, tabs, name):
    s = proj.shape[0]
    tm = 512

    def body(p_ref, c_ref, a_ref, b_ref, q_ref, k_ref, v_ref):
        c, a, b = c_ref[...], a_ref[...], b_ref[...]
        q_ref[0] = _rot(p_ref[:, 0:ATTN_W].astype(F32), _wide(c, ATTN_W), _wide(a, ATTN_W), _wide(b, ATTN_W)).astype(BF16)
        lo = _low_lanes(tm)
        for src, o_ref in ((_rot(p_ref[:, 512:640].astype(F32), c, a, b), k_ref), (p_ref[:, 640:768].astype(F32), v_ref)):
            swapped = pltpu.roll(src, HEAD_DIM, 1)
            o_ref[0, :, 0:LANES] = jnp.where(lo, src, swapped).astype(BF16)
            o_ref[0, :, LANES:] = jnp.where(lo, swapped, src).astype(BF16)

    row = lambda w: pl.BlockSpec((tm, w), lambda i: (i, 0))
    return pl.pallas_call(
        body, name=name, grid=(s // tm,), in_specs=[row(proj.shape[1]), row(LANES), row(LANES), row(LANES)],
        out_specs=[_folded_spec(1, tm), _folded_spec(1, tm, 2 * LANES), _folded_spec(1, tm, 2 * LANES)],
        out_shape=[_folded_shape(s, 1, BF16), _folded_shape(s, 1, BF16, 2 * LANES), _folded_shape(s, 1, BF16, 2 * LANES)],
        compiler_params=_params("parallel"),
    )(proj, *tabs)


def _qkv_post_even(dq, dk, dv, dglu, tabs, name):
    s = dglu.shape[0]
    tm = 512

    def body(dq_ref, dk_ref, dv_ref, dr_ref, c_ref, a_ref, b_ref, o_ref):
        c, a, b = c_ref[...], -a_ref[...], -b_ref[...]
        o_ref[:, 0:ATTN_W] = _rot(dq_ref[0], _wide(c, ATTN_W), _wide(a, ATTN_W), _wide(b, ATTN_W)).astype(BF16)
        lo = _low_lanes(tm)
        merged = []
        for ref in (dk_ref, dv_ref):
            first, second = ref[0, :, 0:LANES], ref[0, :, LANES:]
            merged.append(jnp.where(lo, first + pltpu.roll(first, HEAD_DIM, 1), second + pltpu.roll(second, HEAD_DIM, 1)))
        o_ref[:, 512:640] = _rot(merged[0], c, a, b).astype(BF16)
        o_ref[:, 640:768] = merged[1].astype(BF16)
        o_ref[:, 768:] = dr_ref[...]

    row = lambda w: pl.BlockSpec((tm, w), lambda i: (i, 0))
    return pl.pallas_call(
        body, name=name, grid=(s // tm,),
        in_specs=[_folded_spec(1, tm), _folded_spec(1, tm, 2 * LANES), _folded_spec(1, tm, 2 * LANES),
                  row(dglu.shape[1]), row(LANES), row(LANES), row(LANES)],
        out_specs=row(EVEN_IN), out_shape=jax.ShapeDtypeStruct((s, EVEN_IN), BF16),
        compiler_params=_params("parallel"),
    )(dq, dk, dv, dglu, *tabs)


def _qkv_prep_odd(proj, tabs, name):
    s = proj.shape[0]
    tm = 512

    def body(p_ref, c_ref, a_ref, b_ref, *rest):
        outs, sc_ref = rest[:-1], rest[-1]
        c, a, b = (_wide(t[...], ATTN_W) for t in (c_ref, a_ref, b_ref))
        for t in range(3):
            x = p_ref[:, t * ATTN_W:(t + 1) * ATTN_W].astype(F32)
            if t < 2:
                x = _rot(x, c, a, b)
            _fold_store(x, sc_ref, {d: outs[t * len(DILATIONS) + i] for i, d in enumerate(DILATIONS)})

    row = lambda w: pl.BlockSpec((tm, w), lambda i: (i, 0))
    return pl.pallas_call(
        body, name=name, grid=(s // tm,), in_specs=[row(proj.shape[1]), row(LANES), row(LANES), row(LANES)],
        out_specs=[_folded_spec(d, tm) for _ in range(3) for d in DILATIONS],
        out_shape=[_folded_shape(s, d, BF16) for _ in range(3) for d in DILATIONS],
        scratch_shapes=[pltpu.VMEM((N_PAIRS, tm, LANES), F32)],
        compiler_params=_params("parallel"),
    )(proj, *tabs)


def _qkv_post_odd(dqs, dks, dvs, dz, tabs, name):
    s = dz.shape[0]
    tm = 256
    nb = len(DILATIONS)

    def body(*refs):
        groups = (refs[:nb], refs[nb:2 * nb], refs[2 * nb:3 * nb])
        dz_ref, c_ref, a_ref, b_ref, o_ref, sc_ref = refs[3 * nb:]
        c, a, b = _wide(c_ref[...], ATTN_W), _wide(-a_ref[...], ATTN_W), _wide(-b_ref[...], ATTN_W)
        for t, group in enumerate(groups):
            for i, d in enumerate(DILATIONS):
                _unfold_load(group[i], sc_ref, d, add=i > 0)
            x = jnp.concatenate([sc_ref[p] for p in range(N_PAIRS)], axis=1)
            if t < 2:
                x = _rot(x, c, a, b)
            o_ref[:, t * ATTN_W:(t + 1) * ATTN_W] = x.astype(BF16)
        o_ref[:, 3 * ATTN_W:] = dz_ref[...]

    row = lambda w: pl.BlockSpec((tm, w), lambda i: (i, 0))
    return pl.pallas_call(
        body, name=name, grid=(s // tm,),
        in_specs=[_folded_spec(d, tm) for _ in range(3) for d in DILATIONS] + [row(dz.shape[1]), row(LANES), row(LANES), row(LANES)],
        out_specs=row(ODD_IN), out_shape=jax.ShapeDtypeStruct((s, ODD_IN), BF16),
        scratch_shapes=[pltpu.VMEM((N_PAIRS, tm, LANES), F32)],
        compiler_params=_params("parallel"),
    )(*dqs, *dks, *dvs, dz, *tabs)


def _fold_dout(dmix, name):
    s = dmix.shape[0]
    tm = 512
    ds = [d for d in DILATIONS if d > 1]

    def body(d_ref, *rest):
        outs, sc_ref = rest[:-1], rest[-1]
        _fold_store(d_ref[...], sc_ref, dict(zip(ds, outs)))

    return pl.pallas_call(
        body, name=name, grid=(s // tm,), in_specs=[pl.BlockSpec((tm, ATTN_W), lambda i: (i, 0))],
        out_specs=[_folded_spec(d, tm) for d in ds], out_shape=[_folded_shape(s, d, BF16) for d in ds],
        scratch_shapes=[pltpu.VMEM((N_PAIRS, tm, LANES), F32)],
        compiler_params=_params("parallel"),
    )(dmix)


def _window(j, i, tq):
    r0 = j * tq + i * BLOCK
    start = pl.multiple_of(jnp.maximum(r0 - BLOCK, 0), BLOCK)
    return pl.ds(start, 2 * BLOCK), r0 - start


def _band_valid(offset, max_dist):
    shape = (2 * BLOCK, 2 * BLOCK)
    dist = (lax.bitwise_and(lax.broadcasted_iota(jnp.int32, shape, 0), BLOCK - 1)
            - lax.broadcasted_iota(jnp.int32, shape, 1) + offset)
    return jnp.abs(2 * dist - max_dist) <= max_dist


def _stack_heads(lo, x):
    zero = jnp.zeros_like(x)
    return jnp.concatenate([jnp.where(lo, x, zero), jnp.where(lo, zero, x)], axis=0)


def _unstack_heads(lo, x):
    return jnp.where(lo, x[:BLOCK], x[BLOCK:])


NT = (((1,), (1,)), ((), ()))
TN = (((0,), (0,)), ((), ()))


def _attn_fwd(q, k, v, sinks, *, max_dist, name, emit_bf16=False):
    d, sp, wq = q.shape
    nq, nk = wq // LANES, k.shape[2] // LANES
    kdiv = nq // nk
    tq = min(sp, 1024)
    nsub = tq // BLOCK
    has_sink = sinks is not None

    def body(*refs):
        refs = list(refs)
        sink_ref = refs.pop(0) if has_sink else None
        q_ref, k_ref, v_ref, o_ref, lse_ref = refs[:5]
        pair = pl.program_id(1)
        j = pl.program_id(2)
        lo = _low_lanes(BLOCK)
        if has_sink:
            first_head = lax.broadcasted_iota(jnp.int32, (2 * BLOCK, 1), 0) < BLOCK
            sk = jnp.where(first_head, sink_ref[2 * pair], sink_ref[2 * pair + 1])
        for i in range(nsub):
            win, offset = _window(j, i, tq)
            rows = slice(i * BLOCK, (i + 1) * BLOCK)
            kw = k_ref[0, win, :]
            vw = v_ref[0, win, :]
            s = lax.dot_general(_stack_heads(lo, q_ref[0, rows, :]), kw, NT, preferred_element_type=F32) * ATTN_SCALE
            s = jnp.where(_band_valid(offset, max_dist), s, NEG)
            m = jnp.max(s, axis=-1, keepdims=True)
            if has_sink:
                m = jnp.maximum(m, sk)
            p = jnp.exp(s - m)
            l = jnp.sum(p, axis=-1, keepdims=True)
            if has_sink:
                l = l + jnp.exp(sk - m)
            o2 = _unstack_heads(lo, jnp.dot(p.astype(BF16), vw, preferred_element_type=F32) / l)
            o_ref[0, rows, :] = o2
            lse_ref[0, rows, :] = _unstack_heads(lo, m + jnp.log(l))
            if emit_bf16:
                refs[5][0, rows, :] = o2.astype(BF16)

    qspec = pl.BlockSpec((1, tq, LANES), lambda r, p, j: (r, j, p))
    kspec = pl.BlockSpec((1, sp, LANES), lambda r, p, j: (r, 0, p // kdiv))
    in_specs = [qspec, kspec, kspec]
    operands = [q, k, v]
    if has_sink:
        in_specs = [pl.BlockSpec(memory_space=pltpu.SMEM)] + in_specs
        operands = [sinks] + operands
    out_shape = [jax.ShapeDtypeStruct(q.shape, F32), jax.ShapeDtypeStruct(q.shape, F32)]
    if emit_bf16:
        out_shape.append(jax.ShapeDtypeStruct(q.shape, BF16))
    return pl.pallas_call(
        body, name=name, grid=(d, nq, sp // tq), in_specs=in_specs, out_specs=[qspec] * len(out_shape),
        out_shape=out_shape, compiler_params=_params("parallel", "parallel", "arbitrary"),
    )(*operands)


def _attn_bwd(q, k, v, do, oo, lse, sinks, *, max_dist, name):
    d, sp, wq = q.shape
    wk = k.shape[2]
    nq, nk = wq // LANES, wk // LANES
    kdiv = nq // nk
    tq = min(sp, 1024)
    nsub = tq // BLOCK
    has_sink = sinks is not None

    def body(*refs):
        refs = list(refs)
        sink_ref = refs.pop(0) if has_sink else None
        q_ref, k_ref, v_ref, do_ref, oo_ref, lse_ref, dq_ref, dk_ref, dv_ref = refs[:9]
        pk, g, j = pl.program_id(1), pl.program_id(2), pl.program_id(3)

        @pl.when((g == 0) & (j == 0))
        def _():
            dk_ref[...] = jnp.zeros_like(dk_ref)
            dv_ref[...] = jnp.zeros_like(dv_ref)

        lo = _low_lanes(BLOCK)
        if has_sink:
            first_head = lax.broadcasted_iota(jnp.int32, (2 * BLOCK, 1), 0) < BLOCK
            pair = pk * kdiv + g
            sk = jnp.where(first_head, sink_ref[2 * pair], sink_ref[2 * pair + 1])
            sink_acc = jnp.zeros((2 * BLOCK, LANES), F32)
        for i in range(nsub):
            win, offset = _window(j, i, tq)
            rows = slice(i * BLOCK, (i + 1) * BLOCK)
            kw = k_ref[0, win, :]
            vw = v_ref[0, win, :]
            do2 = do_ref[0, rows, :].astype(F32)
            qs = _stack_heads(lo, q_ref[0, rows, :])
            dos = _stack_heads(lo, do2.astype(BF16))
            prod = do2 * oo_ref[0, rows, :]
            delta = jnp.sum(_stack_heads(lo, prod), axis=-1, keepdims=True)
            lse2 = lse_ref[0, rows, :]
            lse_swapped = pltpu.roll(lse2, HEAD_DIM, 1)
            lse_st = jnp.concatenate([jnp.where(lo, lse2, lse_swapped), jnp.where(lo, lse_swapped, lse2)], axis=0)
            s = lax.dot_general(qs, kw, NT, preferred_element_type=F32) * ATTN_SCALE
            s = jnp.where(_band_valid(offset, max_dist), s, NEG)
            p = jnp.exp(s - jnp.tile(lse_st, (1, 2)))
            dv_ref[0, win, :] = lax.dot_general(p.astype(BF16), dos, TN, preferred_element_type=F32) + dv_ref[0, win, :]
            dp = lax.dot_general(dos, vw, NT, preferred_element_type=F32)
            ds = (p * (dp - delta) * ATTN_SCALE).astype(BF16)
            dq_ref[0, rows, :] = _unstack_heads(lo, jnp.dot(ds, kw, preferred_element_type=F32))
            dk_ref[0, win, :] = lax.dot_general(ds, qs, TN, preferred_element_type=F32) + dk_ref[0, win, :]
            if has_sink:
                sink_acc = sink_acc - jnp.exp(sk - lse_st) * delta
        if has_sink:
            dsink_ref = refs[9]

            @pl.when(j == 0)
            def _():
                dsink_ref[...] = jnp.zeros_like(dsink_ref)

            dsink_ref[0] += jnp.where(lo[0:1], jnp.sum(sink_acc[:BLOCK], axis=0, keepdims=True),
                                      jnp.sum(sink_acc[BLOCK:], axis=0, keepdims=True))

    def qmap(r, pk, g, j):
        return (r, j, pk * kdiv + g)

    def kmap(r, pk, g, j):
        return (r, 0, pk)

    qspec = pl.BlockSpec((1, tq, LANES), qmap)
    kspec = pl.BlockSpec((1, sp, LANES), kmap)
    in_specs = [qspec, kspec, kspec, qspec, qspec, qspec]
    operands = [q, k, v, do, oo, lse]
    out_specs = [qspec, kspec, kspec]
    out_shape = [jax.ShapeDtypeStruct((d, sp, wq), F32), jax.ShapeDtypeStruct((d, sp, wk), F32),
                 jax.ShapeDtypeStruct((d, sp, wk), F32)]
    if has_sink:
        in_specs = [pl.BlockSpec(memory_space=pltpu.SMEM)] + in_specs
        operands = [sinks] + operands
        out_specs.append(pl.BlockSpec((1, 1, LANES), lambda r, pk, g, j: (pk * kdiv + g, 0, 0)))
        out_shape.append(jax.ShapeDtypeStruct((nq, 1, LANES), F32))
    return pl.pallas_call(
        body, name=name, grid=(d, nk, kdiv, sp // tq), in_specs=in_specs, out_specs=out_specs, out_shape=out_shape,
        compiler_params=_params("parallel", "parallel", "arbitrary", "arbitrary"),
    )(*operands)


def _combine(outs, lses, name):
    s = outs[0].shape[1]
    tm = 512
    nb = len(DILATIONS)
    ds = [d for d in DILATIONS if d > 1]

    def body(*refs):
        o_refs, l_refs = refs[:nb], refs[nb:2 * nb]
        cb_ref, c_ref, lse_ref = refs[2 * nb:2 * nb + 3]
        folded = refs[2 * nb + 3:2 * nb + 3 + 2 * len(ds)]
        scratch = refs[2 * nb + 3 + 2 * len(ds):]
        so = {1: None}
        sl = {1: None}
        for i, d in enumerate(ds):
            so[d], sl[d] = scratch[2 * i], scratch[2 * i + 1]
            _unfold_load(o_refs[1 + i], so[d], d)
            _unfold_load(l_refs[1 + i], sl[d], d)
        for p in range(N_PAIRS):
            pb = _pair_block(p)
            ls = [l_refs[0][0, :, pb]] + [sl[d][p] for d in ds]
            os_ = [o_refs[0][0, :, pb]] + [so[d][p] for d in ds]
            m = ls[0]
            for t in ls[1:]:
                m = jnp.maximum(m, t)
            ws = [jnp.exp(t - m) for t in ls]
            tot = ws[0]
            for t in ws[1:]:
                tot = tot + t
            acc = ws[0] * os_[0]
            for w, o in zip(ws[1:], os_[1:]):
                acc = acc + w * o
            cmix = acc / tot
            lse = m + jnp.log(tot)
            cb_ref[:, pb] = cmix.astype(BF16)
            c_ref[0, :, pb] = cmix
            lse_ref[0, :, pb] = lse
            so[ds[0]][p] = cmix
            sl[ds[0]][p] = lse
        for i, d in enumerate(ds):
            for r in range(d):
                for p in range(N_PAIRS):
                    rows = pl.ds(r, tm // d, stride=d)
                    folded[2 * i][r, :, _pair_block(p)] = so[ds[0]][p, rows, :]
                    folded[2 * i + 1][r, :, _pair_block(p)] = sl[ds[0]][p, rows, :]

    in_specs = [_folded_spec(d, tm) for _ in range(2) for d in DILATIONS]
    out_specs = [pl.BlockSpec((tm, ATTN_W), lambda i: (i, 0)), _folded_spec(1, tm), _folded_spec(1, tm)]
    out_shape = [jax.ShapeDtypeStruct((s, ATTN_W), BF16), _folded_shape(s, 1, F32), _folded_shape(s, 1, F32)]
    for d in ds:
        out_specs += [_folded_spec(d, tm)] * 2
        out_shape += [_folded_shape(s, d, F32)] * 2
    return pl.pallas_call(
        body, name=name, grid=(s // tm,), in_specs=in_specs, out_specs=out_specs, out_shape=out_shape,
        scratch_shapes=[pltpu.VMEM((N_PAIRS, tm, LANES), F32)] * (2 * len(ds)),
        compiler_params=_params("parallel"),
    )(*outs, *lses)


GLU_A = slice(768, 1280)
GLU_B = slice(1280, 1792)
EVEN_IN = 1792
ODD_IN = 2560
CONV_CH = 512


def _glu(p_ref):
    return p_ref[:, GLU_A].astype(F32) * _sigmoid(p_ref[:, GLU_B].astype(F32))


def _conv_fwd(proj, w, b, ln_g, ln_b, name):
    s = proj.shape[0]
    tm = 512
    nh = tm // CONV_HALO
    lead = CONV_HALO - (CONV_WIDTH - 1)

    def body(p_ref, ph_ref, w_ref, b_ref, g_ref, bb_ref, y_ref, o_ref, xf_ref):
        xf_ref[CONV_HALO:, :] = _glu(p_ref)
        xf_ref[0:CONV_HALO, :] = jnp.where(pl.program_id(0) > 0, _glu(ph_ref), 0.0)
        for c0 in range(0, tm, CONV_ROWS):
            acc = jnp.zeros((CONV_ROWS, CONV_CH), F32) + b_ref[...]
            for j in range(CONV_WIDTH):
                acc = acc + xf_ref[pl.ds(lead + j + c0, CONV_ROWS), :] * w_ref[j:j + 1, :]
            y_ref[c0:c0 + CONV_ROWS, :] = acc
            mu = jnp.mean(acc, axis=-1, keepdims=True)
            xc = acc - mu
            var = jnp.mean(xc * xc, axis=-1, keepdims=True)
            zz = xc * lax.rsqrt(var + LN_EPS) * g_ref[...] + bb_ref[...]
            o_ref[c0:c0 + CONV_ROWS, :] = (zz * _sigmoid(zz)).astype(BF16)

    def const(a):
        return pl.BlockSpec(a.shape, lambda i: (0, 0))

    return pl.pallas_call(
        body, name=name, grid=(s // tm,),
        in_specs=[pl.BlockSpec((tm, EVEN_IN), lambda i: (i, 0)),
                  pl.BlockSpec((CONV_HALO, EVEN_IN), lambda i: (jnp.maximum(i * nh - 1, 0), 0)),
                  const(w), const(b), const(ln_g), const(ln_b)],
        out_specs=[pl.BlockSpec((tm, CONV_CH), lambda i: (i, 0)), pl.BlockSpec((tm, CONV_CH), lambda i: (i, 0))],
        out_shape=[jax.ShapeDtypeStruct((s, CONV_CH), F32), jax.ShapeDtypeStruct((s, CONV_CH), BF16)],
        scratch_shapes=[pltpu.VMEM((tm + CONV_HALO, CONV_CH), F32)],
        compiler_params=_params("arbitrary"),
    )(proj, proj, w, b, ln_g, ln_b)


def _conv_tail_bwd(dmix, yconv, ln_g, ln_b, name):
    def body(d_ref, y_ref, g_ref, b_ref, dy_ref, dg_ref, db_ref, dcb_ref):
        @pl.when(_first_step())
        def _():
            dg_ref[...] = jnp.zeros_like(dg_ref)
            db_ref[...] = jnp.zeros_like(db_ref)
            dcb_ref[...] = jnp.zeros_like(dcb_ref)

        y = y_ref[...]
        g = g_ref[...]
        mu = jnp.mean(y, axis=-1, keepdims=True)
        xc = y - mu
        rstd = lax.rsqrt(jnp.mean(xc * xc, axis=-1, keepdims=True) + LN_EPS)
        xh = xc * rstd
        zz = xh * g + b_ref[...]
        sg = _sigmoid(zz)
        dzz = d_ref[:, CONV_CH:] * sg * (1.0 + zz * (1.0 - sg))
        dg_ref[...] += jnp.sum(dzz * xh, axis=0, keepdims=True)
        db_ref[...] += jnp.sum(dzz, axis=0, keepdims=True)
        dxh = dzz * g
        dy = rstd * (dxh - jnp.mean(dxh, axis=-1, keepdims=True) - xh * jnp.mean(dxh * xh, axis=-1, keepdims=True))
        dcb_ref[...] += jnp.sum(dy, axis=0, keepdims=True)
        dy_ref[...] = dy

    vec = ((1, CONV_CH), F32)
    return _rows(body, name, 512, [dmix, yconv], [ln_g, ln_b], [(CONV_CH, F32)], [vec, vec, vec])


def _conv_bwd(proj, dy, w, name):
    s = proj.shape[0]
    tm = 512
    nh = tm // CONV_HALO
    nsteps = s // tm
    lead = CONV_HALO - (CONV_WIDTH - 1)

    def body(p_ref, ph_ref, dy_ref, dyn_ref, w_ref, dglu_ref, dw_ref, xf_ref, dyf_ref):
        i = pl.program_id(0)

        @pl.when(i == 0)
        def _():
            dw_ref[...] = jnp.zeros_like(dw_ref)

        ga = p_ref[:, GLU_A].astype(F32)
        sgb = _sigmoid(p_ref[:, GLU_B].astype(F32))
        xf_ref[CONV_HALO:, :] = ga * sgb
        xf_ref[0:CONV_HALO, :] = jnp.where(i > 0, _glu(ph_ref), 0.0)
        dyt = dy_ref[...]
        dyf_ref[0:tm, :] = dyt
        dyf_ref[tm:, :] = jnp.where(i < nsteps - 1, dyn_ref[...], 0.0)
        for c0 in range(0, tm, CONV_ROWS):
            rows = slice(c0, c0 + CONV_ROWS)
            acc = jnp.zeros((CONV_ROWS, CONV_CH), F32)
            for j in range(CONV_WIDTH):
                acc = acc + dyf_ref[pl.ds(CONV_WIDTH - 1 - j + c0, CONV_ROWS), :] * w_ref[j:j + 1, :]
            a_c, s_c = ga[rows, :], sgb[rows, :]
            dglu_ref[rows, 0:CONV_CH] = (acc * s_c).astype(BF16)
            dglu_ref[rows, CONV_CH:] = (acc * a_c * s_c * (1.0 - s_c)).astype(BF16)
        for j in range(CONV_WIDTH):
            part = jnp.zeros((8, CONV_CH), F32)
            for c0 in range(0, tm, CONV_ROWS):
                prod = dy_ref[c0:c0 + CONV_ROWS, :] * xf_ref[pl.ds(lead + j + c0, CONV_ROWS), :]
                part = part + jnp.sum(prod.reshape(CONV_ROWS // 8, 8, CONV_CH), axis=0)
            dw_ref[j:j + 1, :] += jnp.sum(part, axis=0, keepdims=True)

    return pl.pallas_call(
        body, name=name, grid=(nsteps,),
        in_specs=[pl.BlockSpec((tm, EVEN_IN), lambda i: (i, 0)),
                  pl.BlockSpec((CONV_HALO, EVEN_IN), lambda i: (jnp.maximum(i * nh - 1, 0), 0)),
                  pl.BlockSpec((tm, CONV_CH), lambda i: (i, 0)),
                  pl.BlockSpec((CONV_HALO, CONV_CH), lambda i: (jnp.minimum((i + 1) * nh, s // CONV_HALO - 1), 0)),
                  pl.BlockSpec(w.shape, lambda i: (0, 0))],
        out_specs=[pl.BlockSpec((tm, 2 * CONV_CH), lambda i: (i, 0)), pl.BlockSpec(w.shape, lambda i: (0, 0))],
        out_shape=[jax.ShapeDtypeStruct((s, 2 * CONV_CH), BF16), jax.ShapeDtypeStruct(w.shape, F32)],
        scratch_shapes=[pltpu.VMEM((tm + CONV_HALO, CONV_CH), F32), pltpu.VMEM((tm + CONV_HALO, CONV_CH), F32)],
        compiler_params=_params("arbitrary"),
    )(proj, proj, dy, dy, w)


GATE_Z = slice(1536, 2560)
D_CH = 512
GELU_C = math.sqrt(2.0 / math.pi)
GELU_K = 0.044715


def _gelu_parts(z):
    t = jnp.tanh(GELU_C * (z + GELU_K * z * z * z))
    return 0.5 * z * (1.0 + t), t


def _lane_group(rows):
    return lax.broadcasted_iota(jnp.int32, (rows, D_CH), 1) // HEAD_DIM


def _tril_mask():
    return lax.broadcasted_iota(jnp.int32, (BLOCK, BLOCK), 0) >= lax.broadcasted_iota(jnp.int32, (BLOCK, BLOCK), 1)


def _layer_norm_parts(x):
    mu = jnp.mean(x, axis=-1, keepdims=True)
    xc = x - mu
    rstd = lax.rsqrt(jnp.mean(xc * xc, axis=-1, keepdims=True) + LN_EPS)
    return xc * rstd, rstd


def _gate_fwd(proj, ln_g, ln_b, w_sp, sb_t, name):
    tm = 512

    def body(p_ref, g_ref, b_ref, w_ref, sb_ref, mixed_ref, out_ref):
        zz, _ = _gelu_parts(p_ref[:, GATE_Z].astype(F32))
        u = zz[:, :D_CH]
        xh, _ = _layer_norm_parts(zz[:, D_CH:])
        gn = (xh * g_ref[...] + b_ref[...]).astype(BF16)
        grp = _lane_group(BLOCK)
        tri = _tril_mask()
        ws = [jnp.where(tri, w_ref[gi], 0.0).astype(BF16) for gi in range(N_GROUPS)]
        bias = jnp.zeros((BLOCK, D_CH), F32)
        for gi in range(N_GROUPS):
            bias = jnp.where(grp == gi, sb_ref[:, gi:gi + 1], bias)
        for ch in range(tm // BLOCK):
            rows = slice(ch * BLOCK, (ch + 1) * BLOCK)
            gc = gn[rows, :]
            mixed = bias
            for gi in range(N_GROUPS):
                r = jnp.dot(ws[gi], gc, preferred_element_type=F32)
                mixed = jnp.where(grp == gi, r + bias, mixed)
            mixed_ref[rows, :] = mixed
            out_ref[rows, :] = (u[rows, :] * mixed).astype(BF16)

    return _rows(body, name, tm, [proj], [ln_g, ln_b, w_sp, sb_t], [(D_CH, F32), (D_CH, BF16)])


def _gate_bwd(dmix, proj, mixed, ln_g, ln_b, w_sp, name):
    tm = 512

    def body(d_ref, p_ref, m_ref, g_ref, b_ref, w_ref, dz_ref, dg_ref, db_ref, dw_ref, dsb_ref, dgn_ref):
        @pl.when(_first_step())
        def _():
            dg_ref[...] = jnp.zeros_like(dg_ref)
            db_ref[...] = jnp.zeros_like(db_ref)
            dw_ref[...] = jnp.zeros_like(dw_ref)
            dsb_ref[...] = jnp.zeros_like(dsb_ref)

        z = p_ref[:, GATE_Z].astype(F32)
        zz, t = _gelu_parts(z)
        u = zz[:, :D_CH]
        xh, rstd = _layer_norm_parts(zz[:, D_CH:])
        g = g_ref[...]
        gn = (xh * g + b_ref[...]).astype(BF16)
        dd = d_ref[:, D_CH:]
        du = dd * m_ref[...]
        dm = dd * u
        grp = _lane_group(BLOCK)
        tri = _tril_mask()
        ws = [jnp.where(tri, w_ref[gi], 0.0).astype(BF16) for gi in range(N_GROUPS)]
        gsel = (lax.broadcasted_iota(jnp.int32, (N_GROUPS, D_CH), 1) // HEAD_DIM
                == lax.broadcasted_iota(jnp.int32, (N_GROUPS, D_CH), 0)).astype(F32)
        for ch in range(tm // BLOCK):
            rows = slice(ch * BLOCK, (ch + 1) * BLOCK)
            dmc = dm[rows, :]
            dmb = dmc.astype(BF16)
            gc = gn[rows, :]
            dgn = jnp.zeros((BLOCK, D_CH), F32)
            for gi in range(N_GROUPS):
                r = lax.dot_general(ws[gi], dmb, TN, preferred_element_type=F32)
                dgn = jnp.where(grp == gi, r, dgn)
                dmg = jnp.where(grp == gi, dmb, jnp.zeros_like(dmb))
                dwg = lax.dot_general(dmg, gc, NT, preferred_element_type=F32)
                dw_ref[gi] += jnp.where(tri, dwg, 0.0)
            dsb_ref[...] += lax.dot_general(gsel, dmc, NT, preferred_element_type=F32, precision=lax.Precision.HIGHEST)
            dgn_ref[rows, :] = dgn
        dgn = dgn_ref[...]
        db_ref[...] += jnp.sum(dgn, axis=0, keepdims=True)
        dg_ref[...] += jnp.sum(dgn * xh, axis=0, keepdims=True)
        dxh = dgn * g
        dgp = rstd * (dxh - jnp.mean(dxh, axis=-1, keepdims=True) - xh * jnp.mean(dxh * xh, axis=-1, keepdims=True))
        dgelu = 0.5 * (1.0 + t) + 0.5 * z * (1.0 - t * t) * GELU_C * (1.0 + 3.0 * GELU_K * z * z)
        dz_ref[:, 0:D_CH] = (du * dgelu[:, :D_CH]).astype(BF16)
        dz_ref[:, D_CH:] = (dgp * dgelu[:, D_CH:]).astype(BF16)

    s = proj.shape[0]
    tiled = [dmix, proj, mixed]
    consts = [ln_g, ln_b, w_sp]
    in_specs = [pl.BlockSpec((tm, a.shape[1]), lambda i: (i, 0)) for a in tiled]
    in_specs += [pl.BlockSpec(a.shape, lambda i, nd=a.ndim: (0,) * nd) for a in consts]
    vec = (1, D_CH)
    acc_shapes = [vec, vec, w_sp.shape, (N_GROUPS, BLOCK)]
    return pl.pallas_call(
        body, name=name, grid=(s // tm,), in_specs=in_specs,
        out_specs=[pl.BlockSpec((tm, 2 * D_CH), lambda i: (i, 0))]
        + [pl.BlockSpec(sh, lambda i, nd=len(sh): (0,) * nd) for sh in acc_shapes],
        out_shape=[jax.ShapeDtypeStruct((s, 2 * D_CH), BF16)] + [jax.ShapeDtypeStruct(sh, F32) for sh in acc_shapes],
        scratch_shapes=[pltpu.VMEM((tm, D_CH), F32)],
        compiler_params=_params("arbitrary"),
    )(*tiled, *consts)


def _adam_update(w, g, m, v):
    nm = ADAM_B1 * m + (1.0 - ADAM_B1) * g
    nv = ADAM_B2 * v + (1.0 - ADAM_B2) * (g * g)
    m_hat = nm / (1.0 - ADAM_B1 ** ADAM_STEP)
    v_hat = nv / (1.0 - ADAM_B2 ** ADAM_STEP)
    return -ADAM_LR * (m_hat / (jnp.sqrt(v_hat) + ADAM_EPS) + ADAM_WD * w), nm, nv


def _adamw(w, g, m, v, name):
    rows, cols = w.shape
    tm = _tile(rows, 512, 8)

    def body(w_ref, g_ref, m_ref, v_ref, d_ref, nm_ref, nv_ref):
        d_ref[...], nm_ref[...], nv_ref[...] = _adam_update(w_ref[...], g_ref[...], m_ref[...], v_ref[...])

    return _rows(body, name, tm, [w, g, m, v], [], [(cols, F32)] * 3)


def _ordered_sum(parts, name):
    n, rows, cols = parts.shape
    tm = _tile(rows, 512, 16 if parts.dtype == BF16 else 8)

    def body(p_ref, o_ref):
        acc = p_ref[0].astype(F32)
        for k in range(1, n):
            acc = acc + p_ref[k].astype(F32)
        o_ref[...] = acc

    return pl.pallas_call(body, name=name, grid=(rows // tm,),
                          in_specs=[pl.BlockSpec((n, tm, cols), lambda i: (0, i, 0))],
                          out_specs=pl.BlockSpec((tm, cols), lambda i: (i, 0)),
                          out_shape=jax.ShapeDtypeStruct((rows, cols), F32), compiler_params=_params("parallel"))(parts)


ANY = pl.BlockSpec(memory_space=pl.ANY)


def _position():
    x, y, c = lax.axis_index("x"), lax.axis_index("y"), lax.axis_index("c")
    other_chips = [(1 - x, y), (x, 1 - y), (1 - x, 1 - y)]
    return x, y, c, other_chips


def _remote(src, dst, send_sem, recv_sem, to):
    return pltpu.make_async_remote_copy(src_ref=src, dst_ref=dst, send_sem=send_sem, recv_sem=recv_sem,
                                        device_id=to, device_id_type=MESH)


STAGE_ROWS = 736


def _staged_copies(copies, buf, in_sems, out_sems):
    n = len(copies)

    def into(u):
        src = copies[u][0]
        return pltpu.make_async_copy(src, buf.at[u % 2, pl.ds(0, src.shape[0]), :], in_sems.at[u % 2])

    def out_of(u):
        dst = copies[u][1]
        return pltpu.make_async_copy(buf.at[u % 2, pl.ds(0, dst.shape[0]), :], dst, out_sems.at[u % 2])

    into(0).start()
    for u in range(n):
        into(u).wait()
        out_of(u).start()
        if u + 1 < n:
            if u >= 1:
                out_of(u - 1).wait()
            into(u + 1).start()
    if n >= 2:
        out_of(n - 2).wait()
    out_of(n - 1).wait()


def _stage_scratch(dtype, cols):
    return [pltpu.VMEM((2, STAGE_ROWS, cols), dtype), pltpu.SemaphoreType.DMA((2,)), pltpu.SemaphoreType.DMA((2,))]


def _row_chunks(rows):
    return [(r, min(STAGE_ROWS, rows - r)) for r in range(0, rows, STAGE_ROWS)]


def _gather_chips(shard, name):
    rows, cols = shard.shape
    half = rows // 2

    def body(in_ref, out_ref, send_sems, recv_sems, buf, in_sems, out_sems):
        x, y, c, chips = _position()
        me = 2 * x + y
        sibling = (x, y, 1 - c)

        def slab(chip, h):
            return out_ref.at[chip, pl.ds(h * half, half), :]

        first = [_remote(in_ref.at[pl.ds(c * half, half), :], slab(me, c), send_sems.at[j], recv_sems.at[j], (cx, cy, c))
                 for j, (cx, cy) in enumerate(chips)]
        for cp in first:
            cp.start()
        _staged_copies([(in_ref.at[pl.ds(r, n), :], out_ref.at[me, pl.ds(r, n), :]) for r, n in _row_chunks(rows)],
                       buf, in_sems, out_sems)
        passed = []
        for j, (cx, cy) in enumerate(chips):
            got = slab(2 * cx + cy, c)
            _remote(got, got, send_sems.at[j], recv_sems.at[j], sibling).wait_recv()
            cp = _remote(got, got, send_sems.at[3 + j], recv_sems.at[3 + j], sibling)
            cp.start()
            passed.append(cp)
        for j, (cx, cy) in enumerate(chips):
            got = slab(2 * cx + cy, 1 - c)
            _remote(got, got, send_sems.at[3 + j], recv_sems.at[3 + j], sibling).wait_recv()
        for cp in first + passed:
            cp.wait_send()

    return pl.pallas_call(
        body, name=name, in_specs=[ANY], out_specs=ANY,
        out_shape=jax.ShapeDtypeStruct((N_CHIPS, rows, cols), shard.dtype),
        scratch_shapes=[pltpu.SemaphoreType.DMA((6,)), pltpu.SemaphoreType.DMA((6,))] + _stage_scratch(shard.dtype, cols),
        compiler_params=pltpu.CompilerParams(vmem_limit_bytes=VMEM_LIMIT),
    )(shard)


HBM = pl.BlockSpec(memory_space=pltpu.HBM)
SEM = pl.BlockSpec(memory_space=pltpu.SEMAPHORE)
SIDE_EFFECT = pltpu.SideEffectType.DATAFLOW_SIDE_EFFECTING


def _ici_copies(in_ref, land_ref, send_sems, recv_sems, half):
    x, y, c, chips = _position()
    mine = pl.ds(c * half, half)
    sends = [_remote(in_ref.at[mine, :], land_ref.at[2 * x + y, mine, :], send_sems.at[j], recv_sems.at[j], (cx, cy, c))
             for j, (cx, cy) in enumerate(chips)]
    arrivals = [_remote(in_ref.at[mine, :], land_ref.at[2 * cx + cy, mine, :], send_sems.at[j], recv_sems.at[j], (cx, cy, c))
                for j, (cx, cy) in enumerate(chips)]
    return sends, arrivals


def _gather_start(shard, after, name):
    rows, cols = shard.shape

    def body(in_ref, land_ref, after_ref, send_sems, recv_sems, in_thru, land_thru, token):
        sends, _ = _ici_copies(in_ref, land_ref, send_sems, recv_sems, rows // 2)
        for cp in sends:
            cp.start()
        token[...] = jnp.zeros_like(token)

    land = lax.empty((N_CHIPS, rows, cols), shard.dtype)
    return pl.pallas_call(
        body, name=name,
        out_shape=(pltpu.SemaphoreType.DMA((3,)), pltpu.SemaphoreType.DMA((3,)), pltpu.HBM(shard.shape, shard.dtype),
                   pltpu.HBM(land.shape, land.dtype), jax.ShapeDtypeStruct((8, LANES), F32)),
        in_specs=(HBM, HBM, ANY), out_specs=(SEM, SEM, HBM, HBM, pl.BlockSpec(memory_space=pltpu.VMEM)),
        input_output_aliases={0: 2, 1: 3},
        compiler_params=pltpu.CompilerParams(has_side_effects=SIDE_EFFECT),
    )(pltpu.with_memory_space_constraint(shard, pltpu.HBM), pltpu.with_memory_space_constraint(land, pltpu.HBM), after)


def _gather_wait(send_sems, recv_sems, shard, land, after, name):
    rows = shard.shape[0]

    def body(in_ref, land_ref, send_sems, recv_sems, after_ref, in_out, land_out):
        sends, arrivals = _ici_copies(in_ref, land_ref, send_sems, recv_sems, rows // 2)
        for cp in sends:
            cp.wait_send()
        for cp in arrivals:
            cp.wait_recv()

    return pl.pallas_call(
        body, name=name, out_shape=(pltpu.HBM(shard.shape, shard.dtype), pltpu.HBM(land.shape, land.dtype)),
        in_specs=(HBM, HBM, SEM, SEM, ANY), out_specs=(HBM, HBM), input_output_aliases={0: 0, 1: 1},
        compiler_params=pltpu.CompilerParams(has_side_effects=SIDE_EFFECT),
    )(shard, land, send_sems, recv_sems, after)


def _gather_finish(shard, land, name):
    rows, cols = shard.shape
    half = rows // 2

    def body(in_ref, land_ref, out_ref, send_sems, recv_sems, buf, in_sems, out_sems):
        x, y, c, chips = _position()
        me = 2 * x + y
        sibling = (x, y, 1 - c)

        def slab(chip, h):
            return out_ref.at[chip, pl.ds(h * half, half), :]

        passed = [_remote(slab(2 * cx + cy, c), slab(2 * cx + cy, c), send_sems.at[j], recv_sems.at[j], sibling)
                  for j, (cx, cy) in enumerate(chips)]
        for cp in passed:
            cp.start()
        _staged_copies([(in_ref.at[pl.ds(r, n), :], out_ref.at[me, pl.ds(r, n), :]) for r, n in _row_chunks(rows)],
                       buf, in_sems, out_sems)
        for j, (cx, cy) in enumerate(chips):
            got = slab(2 * cx + cy, 1 - c)
            _remote(got, got, send_sems.at[j], recv_sems.at[j], sibling).wait_recv()
        for cp in passed:
            cp.wait_send()

    return pl.pallas_call(
        body, name=name, in_specs=[ANY, ANY], out_specs=ANY, out_shape=jax.ShapeDtypeStruct(land.shape, land.dtype),
        input_output_aliases={1: 0},
        scratch_shapes=[pltpu.SemaphoreType.DMA((3,)), pltpu.SemaphoreType.DMA((3,))] + _stage_scratch(shard.dtype, cols),
        compiler_params=pltpu.CompilerParams(vmem_limit_bytes=VMEM_LIMIT),
    )(shard, land)


def _gather_devices(block, name):
    rows, cols = block.shape

    def body(in_ref, out_ref, send_sems, recv_sems, local_sem):
        x, y, c, chips = _position()
        sibling = (x, y, 1 - c)

        def slot(px, py, pc):
            return out_ref.at[4 * px + 2 * py + pc]

        mine = pltpu.make_async_copy(in_ref, slot(x, y, c), local_sem)
        mine.start()
        first = [_remote(in_ref, slot(x, y, c), send_sems.at[0], recv_sems.at[0], sibling)]
        first += [_remote(in_ref, slot(x, y, c), send_sems.at[1 + j], recv_sems.at[1 + j], (cx, cy, c))
                  for j, (cx, cy) in enumerate(chips)]
        for cp in first:
            cp.start()
        passed = []
        for j, (cx, cy) in enumerate(chips):
            got = slot(cx, cy, c)
            _remote(got, got, send_sems.at[1 + j], recv_sems.at[1 + j], sibling).wait_recv()
            cp = _remote(got, got, send_sems.at[4 + j], recv_sems.at[4 + j], sibling)
            cp.start()
            passed.append(cp)
        got = slot(x, y, 1 - c)
        _remote(got, got, send_sems.at[0], recv_sems.at[0], sibling).wait_recv()
        for j, (cx, cy) in enumerate(chips):
            got = slot(cx, cy, 1 - c)
            _remote(got, got, send_sems.at[4 + j], recv_sems.at[4 + j], sibling).wait_recv()
        for cp in first + passed:
            cp.wait_send()
        mine.wait()

    return pl.pallas_call(
        body, name=name, in_specs=[ANY], out_specs=ANY,
        out_shape=jax.ShapeDtypeStruct((N_DEV, rows, cols), block.dtype),
        scratch_shapes=[pltpu.SemaphoreType.DMA((7,)), pltpu.SemaphoreType.DMA((7,)), pltpu.SemaphoreType.DMA],
    )(block)


def _pair_send(grads, name):
    n = len(grads)
    hs = [g.shape[2] for g in grads]
    offs = [sum(hs[:i]) for i in range(n)]
    cols = grads[0].shape[3]

    def body(*refs):
        g_refs = refs[:n]
        got_ref, send_sems, recv_sems = refs[n:]
        x, y, c, _ = _position()
        copies = [_remote(g_ref.at[:, 1 - c], got_ref.at[:, pl.ds(offs[i], hs[i]), :], send_sems.at[i], recv_sems.at[i],
                          (x, y, 1 - c)) for i, g_ref in enumerate(g_refs)]
        for cp in copies:
            cp.start()
        for cp in copies:
            cp.wait()

    return pl.pallas_call(
        body, name=name, in_specs=[ANY] * n, out_specs=ANY, out_shape=jax.ShapeDtypeStruct((N_CHIPS, sum(hs), cols), F32),
        scratch_shapes=[pltpu.SemaphoreType.DMA((n,)), pltpu.SemaphoreType.DMA((n,))],
    )(*grads)


def _pair_add(grads, got, name):
    n = len(grads)
    hs = [g.shape[2] for g in grads]
    offs = [sum(hs[:i]) for i in range(n)]
    cols = grads[0].shape[3]
    hmax = max(hs)
    units = [(i, k) for k in range(N_CHIPS) for i in range(n)]

    def body(*refs):
        g_refs = refs[:n]
        got_ref, out_ref, a_buf, b_buf, o_buf, a_sems, b_sems, o_sems = refs[n:]
        c = lax.axis_index("c")

        def loads(u):
            i, k = units[u]
            slot, rows = u % 2, pl.ds(0, hs[i])
            return (pltpu.make_async_copy(g_refs[i].at[k, c], a_buf.at[slot, rows, :], a_sems.at[slot]),
                    pltpu.make_async_copy(got_ref.at[k, pl.ds(offs[i], hs[i]), :], b_buf.at[slot, rows, :], b_sems.at[slot]))

        def store(u):
            i, k = units[u]
            return pltpu.make_async_copy(o_buf.at[u % 2, pl.ds(0, hs[i]), :], out_ref.at[k, pl.ds(offs[i], hs[i]), :],
                                         o_sems.at[u % 2])

        for cp in loads(0):
            cp.start()
        for u, (i, k) in enumerate(units):
            if u + 1 < len(units):
                for cp in loads(u + 1):
                    cp.start()
            for cp in loads(u):
                cp.wait()
            if u >= 2:
                store(u - 2).wait()
            rows = pl.ds(0, hs[i])
            o_buf[u % 2, rows, :] = (a_buf[u % 2, rows, :] + b_buf[u % 2, rows, :]).astype(BF16)
            store(u).start()
        store(len(units) - 2).wait()
        store(len(units) - 1).wait()

    return pl.pallas_call(
        body, name=name, in_specs=[ANY] * (n + 1), out_specs=ANY,
        out_shape=jax.ShapeDtypeStruct((N_CHIPS, sum(hs), cols), BF16),
        scratch_shapes=[pltpu.VMEM((2, hmax, cols), F32), pltpu.VMEM((2, hmax, cols), F32), pltpu.VMEM((2, hmax, cols), BF16),
                        pltpu.SemaphoreType.DMA((2,)), pltpu.SemaphoreType.DMA((2,)), pltpu.SemaphoreType.DMA((2,))],
        compiler_params=pltpu.CompilerParams(vmem_limit_bytes=VMEM_LIMIT),
    )(*grads, got)


def _chip_exchange(parts, name):
    _, rows, cols = parts.shape

    def body(in_ref, out_ref, send_sems, recv_sems):
        x, y, c, chips = _position()
        sent = [_remote(in_ref.at[2 * cx + cy], out_ref.at[j], send_sems.at[j], recv_sems.at[j], (cx, cy, c))
                for j, (cx, cy) in enumerate(chips)]
        for cp in sent:
            cp.start()
        for cp in sent:
            cp.wait()

    return pl.pallas_call(
        body, name=name, in_specs=[ANY], out_specs=ANY, out_shape=jax.ShapeDtypeStruct((3, rows, cols), parts.dtype),
        scratch_shapes=[pltpu.SemaphoreType.DMA((3,)), pltpu.SemaphoreType.DMA((3,))],
    )(parts)


def _exchange_copies(in_ref, land_ref, send_sems, recv_sems):
    x, y, c, chips = _position()
    return [_remote(in_ref.at[2 * cx + cy], land_ref.at[j], send_sems.at[j], recv_sems.at[j], (cx, cy, c))
            for j, (cx, cy) in enumerate(chips)]


def _exchange_start(parts, name):
    _, rows, cols = parts.shape

    def body(in_ref, land_ref, send_sems, recv_sems, in_thru, land_thru, token):
        for cp in _exchange_copies(in_ref, land_ref, send_sems, recv_sems):
            cp.start()
        token[...] = jnp.zeros_like(token)

    land = lax.empty((3, rows, cols), parts.dtype)
    return pl.pallas_call(
        body, name=name,
        out_shape=(pltpu.SemaphoreType.DMA((3,)), pltpu.SemaphoreType.DMA((3,)), pltpu.HBM(parts.shape, parts.dtype),
                   pltpu.HBM(land.shape, land.dtype), jax.ShapeDtypeStruct((8, LANES), F32)),
        in_specs=(HBM, HBM), out_specs=(SEM, SEM, HBM, HBM, pl.BlockSpec(memory_space=pltpu.VMEM)),
        input_output_aliases={0: 2, 1: 3},
        compiler_params=pltpu.CompilerParams(has_side_effects=SIDE_EFFECT),
    )(pltpu.with_memory_space_constraint(parts, pltpu.HBM), pltpu.with_memory_space_constraint(land, pltpu.HBM))


def _exchange_wait(send_sems, recv_sems, parts, land, after, name):
    def body(in_ref, land_ref, send_sems, recv_sems, after_ref, in_out, land_out):
        for cp in _exchange_copies(in_ref, land_ref, send_sems, recv_sems):
            cp.wait_send()
            cp.wait_recv()

    return pl.pallas_call(
        body, name=name, out_shape=(pltpu.HBM(parts.shape, parts.dtype), pltpu.HBM(land.shape, land.dtype)),
        in_specs=(HBM, HBM, SEM, SEM, ANY), out_specs=(HBM, HBM), input_output_aliases={0: 0, 1: 1},
        compiler_params=pltpu.CompilerParams(has_side_effects=SIDE_EFFECT),
    )(parts, land, send_sems, recv_sems, after)


def _chip_sum(parts, recv, chip, name):
    _, rows, cols = parts.shape
    tm = _tile(rows, 512, 16)

    def body(chip_ref, own_ref, recv_ref, o_ref):
        acc = own_ref[0].astype(F32)
        for j in range(3):
            acc = acc + recv_ref[j].astype(F32)
        o_ref[...] = acc

    return pl.pallas_call(
        body, name=name,
        grid_spec=pltpu.PrefetchScalarGridSpec(
            num_scalar_prefetch=1, grid=(rows // tm,),
            in_specs=[pl.BlockSpec((1, tm, cols), lambda i, chip_ref: (chip_ref[0], i, 0)),
                      pl.BlockSpec((3, tm, cols), lambda i, chip_ref: (0, i, 0))],
            out_specs=pl.BlockSpec((tm, cols), lambda i, chip_ref: (i, 0))),
        out_shape=jax.ShapeDtypeStruct((rows, cols), F32), compiler_params=_params("parallel"),
    )(chip, parts, recv)


def _join_unpack(mine, hs, groups, name):
    n = len(hs)
    offs = [sum(hs[:i]) for i in range(n)]
    cols = mine.shape[1]
    n_out = max(groups) + 1
    base = [2 * sum(h for h, g in zip(hs[:i], groups[:i]) if g == groups[i]) for i in range(n)]
    out_rows = [2 * sum(h for h, g in zip(hs, groups) if g == k) for k in range(n_out)]

    def body(in_ref, *refs):
        outs = refs[:n_out]
        send_sems, recv_sems, buf, in_sems, out_sems = refs[n_out:]
        x, y, c, _ = _position()
        sibling = (x, y, 1 - c)
        sent, local = [], []
        for i in range(n):
            src = in_ref.at[pl.ds(offs[i], hs[i]), :]
            here = outs[groups[i]].at[pl.ds(base[i] + c * hs[i], hs[i]), :]
            cp = _remote(src, here, send_sems.at[i], recv_sems.at[i], sibling)
            cp.start()
            sent.append(cp)
            local.append((src, here))
        _staged_copies(local, buf, in_sems, out_sems)
        for i, cp in enumerate(sent):
            there = outs[groups[i]].at[pl.ds(base[i] + (1 - c) * hs[i], hs[i]), :]
            _remote(there, there, send_sems.at[i], recv_sems.at[i], sibling).wait_recv()
            cp.wait_send()

    assert max(hs) <= STAGE_ROWS
    return pl.pallas_call(
        body, name=name, in_specs=[ANY], out_specs=[ANY] * n_out,
        out_shape=[jax.ShapeDtypeStruct((r, cols), F32) for r in out_rows],
        scratch_shapes=[pltpu.SemaphoreType.DMA((n,)), pltpu.SemaphoreType.DMA((n,))] + _stage_scratch(F32, cols),
        compiler_params=pltpu.CompilerParams(vmem_limit_bytes=VMEM_LIMIT),
    )(mine)


SMALL_ROWS = 16
SMALL_PACK_ROWS = 256


def _small_rows(n):
    return -(-n // (SMALL_ROWS * LANES)) * SMALL_ROWS


def _pack_small(arrs):
    parts = []
    for a in arrs:
        flat = a.reshape(-1)
        rows = _small_rows(flat.shape[0])
        flat = jnp.pad(flat, (0, rows * LANES - flat.shape[0]))
        parts.append(flat.reshape(rows, LANES))
    total = sum(p.shape[0] for p in parts)
    parts.append(jnp.zeros((-total % SMALL_PACK_ROWS, LANES), F32))
    return jnp.concatenate(parts, axis=0)


def _unpack_small(packed, shapes):
    out, r = [], 0
    for sh in shapes:
        n = math.prod(sh)
        cnt = _small_rows(n)
        out.append(packed[r:r + cnt].reshape(-1)[:n].reshape(sh))
        r += cnt
    return out


def _ffn_fwd(h, g_norm, w_gate_t, w_up_t, w_down, tag):
    n = _rms_fwd(h, g_norm, f"{tag}_norm")
    act, gate, up = _ffn_gate_up(n, w_gate_t, w_up_t, f"{tag}_gate_up")
    out = _matmul(act, w_down, add=h, name=f"{tag}_down")
    return out, (n, gate, up, act)


def _ffn_bwd(dh, dhb, h_in, saved, g_norm, w_gate_t, w_up_t, w_down, tag, after=None):
    n, gate, up, act = saved
    dgate, dup = _ffn_dact(dhb, w_down, gate, up, f"{tag}_dact", after)
    dw_down = _matmul(act, dhb, trans_a=True, name=f"{tag}_dwdown")
    dw_gate_t = _matmul(dgate, n, trans_a=True, name=f"{tag}_dwgate")
    dw_up_t = _matmul(dup, n, trans_a=True, name=f"{tag}_dwup")
    dh_in, dh_inb, dg = _dn_norm([(dgate, w_gate_t), (dup, w_up_t)], h_in, g_norm, dh, f"{tag}_dnorm")
    return dh_in, dh_inb, dg, dw_gate_t, dw_up_t, dw_down


def _local_step(x, tgt, w, big, late_weights, reduce_early):
    s = x.shape[0]
    tabs = _rope_tables(s)
    grads, gbig = {}, {}

    g_ev = w['ev_norm_g']
    n1 = _rms_fwd(x, g_ev, "ev_norm")
    proj0 = _matmul(n1, big['ev_w_in', 0], trans_b=True, name="ev_in", out_dtype=BF16)
    q0, k0, v0 = _qkv_prep_even(proj0, tabs, "ev_qkv")
    sinks = w['ev_sinks'].reshape(-1)
    o0, lse0, o0b = _attn_fwd(q0, k0, v0, sinks, max_dist=BLOCK - 1, name="ev_attn", emit_bf16=True)
    yconv, cout = _conv_fwd(proj0, w['ev_conv_w'][0], w['ev_conv_b'], w['ev_conv_ln_g'], w['ev_conv_ln_b'], "ev_conv")
    mix0 = (o0b[0], cout)
    h1 = _matmul(mix0, big['ev_w_out', 0], add=x, name="ev_out")
    big = {**big, **late_weights(h1)}

    g_f0 = w['ffn_norm_g'][0:1]
    h2, ffn0 = _ffn_fwd(h1, g_f0, big['ffn_w_gate', 0], big['ffn_w_up', 0], big['ffn_w_down', 0], "ffn0")

    g_od = w['od_norm_g']
    n3 = _rms_fwd(h2, g_od, "od_norm")
    proj1 = _matmul(n3, big['od_w_in', 0], trans_b=True, name="od_in", out_dtype=BF16)
    qkv = _qkv_prep_odd(proj1, tabs, "od_qkv")
    nb = len(DILATIONS)
    outs, lses = [], []
    for i, d in enumerate(DILATIONS):
        o_r, lse_r = _attn_fwd(qkv[i], qkv[nb + i], qkv[2 * nb + i], None, max_dist=BLOCK, name=f"od_attn{d}")
        outs.append(o_r)
        lses.append(lse_r)
    comb = _combine(outs, lses, "od_combine")
    c_bf16 = comb[0]
    c_fold = {1: comb[1]}
    lse_fold = {1: comb[2]}
    for i, d in enumerate(DILATIONS[1:]):
        c_fold[d], lse_fold[d] = comb[3 + 2 * i], comb[4 + 2 * i]
    w_sp = w['od_spatial_w'][0]
    sb_t = w['od_spatial_b'][0].T
    mixed, dout = _gate_fwd(proj1, w['od_sgu_ln_g'], w['od_sgu_ln_b'], w_sp, sb_t, "od_gate")
    mix1 = (c_bf16, dout)
    h3 = _matmul(mix1, big['od_w_out', 0], add=h2, name="od_out")

    g_f1 = w['ffn_norm_g'][1:2]
    h4, ffn1 = _ffn_fwd(h3, g_f1, big['ffn_w_gate', 1], big['ffn_w_up', 1], big['ffn_w_down', 1], "ffn1")

    dh4, dh4b, dg_final, loss_tile = _final_loss(h4, w['final_norm_g'].reshape(1, D_MODEL), tgt, "final")
    grads['final_norm_g'] = dg_final.reshape(D_MODEL)

    dh3, dh3b, dg_f1, gbig['ffn_w_gate', 1], gbig['ffn_w_up', 1], gbig['ffn_w_down', 1] = _ffn_bwd(
        dh4, dh4b, h3, ffn1, g_f1, big['ffn_w_gate', 1], big['ffn_w_up', 1], big['ffn_w_down', 1], "ffn1")

    dmix1 = _matmul(dh3b, big['od_w_out', 0], trans_b=True, name="od_dmix")
    gbig['od_w_out', 0] = _matmul_tn_pair(mix1[0], mix1[1], dh3b, "od_dwout")
    do_fold = dict(zip(DILATIONS[1:], _fold_dout(dmix1, "od_fold_dout")))
    do_fold[1] = dmix1[None]
    dqs, dks, dvs = [], [], []
    for i, d in enumerate(DILATIONS):
        dq_r, dk_r, dv_r = _attn_bwd(qkv[i], qkv[nb + i], qkv[2 * nb + i], do_fold[d], c_fold[d], lse_fold[d], None,
                                     max_dist=BLOCK, name=f"od_dattn{d}")
        dqs.append(dq_r)
        dks.append(dk_r)
        dvs.append(dv_r)
    dz, dg_sgu, db_sgu, dw_sp, dsb = _gate_bwd(dmix1, proj1, mixed, w['od_sgu_ln_g'], w['od_sgu_ln_b'], w_sp, "od_dgate")
    grads['od_sgu_ln_g'], grads['od_sgu_ln_b'] = dg_sgu, db_sgu
    grads['od_spatial_w'], grads['od_spatial_b'] = dw_sp[None], dsb[None]
    dproj1 = _qkv_post_odd(dqs, dks, dvs, dz, tabs, "od_dproj")
    gbig['od_w_in', 0] = _matmul(dproj1, n3, trans_a=True, name="od_dwin")
    dh2, dh2b, dg_od = _dn_norm([(dproj1, big['od_w_in', 0])], h2, g_od, dh3, "od_dnorm")
    grads['od_norm_g'] = dg_od
    token = reduce_early(0, gbig)

    dh1, dh1b, dg_f0, gbig['ffn_w_gate', 0], gbig['ffn_w_up', 0], gbig['ffn_w_down', 0] = _ffn_bwd(
        dh2, dh2b, h1, ffn0, g_f0, big['ffn_w_gate', 0], big['ffn_w_up', 0], big['ffn_w_down', 0], "ffn0", token)
    grads['ffn_norm_g'] = jnp.concatenate([dg_f0, dg_f1], axis=0)
    token = reduce_early(1, gbig)

    dmix0 = _matmul(dh1b, big['ev_w_out', 0], trans_b=True, name="ev_dmix", after=token)
    gbig['ev_w_out', 0] = _matmul_tn_pair(mix0[0], mix0[1], dh1b, "ev_dwout")
    dq0, dk0, dv0, dsink = _attn_bwd(q0, k0, v0, dmix0[None], o0, lse0, sinks, max_dist=BLOCK - 1, name="ev_dattn")
    grads['ev_sinks'] = dsink[:, 0, :].reshape(N_PAIRS, 2, HEAD_DIM)[:, :, 0].reshape(1, 8)
    dyc, dg_cln, db_cln, dcb = _conv_tail_bwd(dmix0, yconv, w['ev_conv_ln_g'], w['ev_conv_ln_b'], "ev_dconv_tail")
    grads['ev_conv_ln_g'], grads['ev_conv_ln_b'], grads['ev_conv_b'] = dg_cln, db_cln, dcb
    dglu, dconv_w = _conv_bwd(proj0, dyc, w['ev_conv_w'][0], "ev_dconv")
    grads['ev_conv_w'] = dconv_w[None]
    dproj0 = _qkv_post_even(dq0, dk0, dv0, dglu, tabs, "ev_dproj")
    gbig['ev_w_in', 0] = _matmul(dproj0, n1, trans_a=True, name="ev_dwin")
    dx, _, dg_ev = _dn_norm([(dproj0, big['ev_w_in', 0])], x, g_ev, dh1, "ev_dnorm")
    grads['ev_norm_g'] = dg_ev
    return loss_tile, dx, grads, gbig


def _shard_rows(w, layer, by_cols):
    return w[layer].T if by_cols else w[layer]


def kernel(x, ev_norm_g, ev_w_in, ev_sinks, ev_conv_w, ev_conv_b, ev_conv_ln_g, ev_conv_ln_b, ev_w_out, od_norm_g, od_w_in, od_sgu_ln_g, od_sgu_ln_b, od_spatial_w, od_spatial_b, od_w_out, ffn_norm_g, ffn_w_gate, ffn_w_up, ffn_w_down, final_norm_g, loss_target, m_ev_norm_g, m_ev_w_in, m_ev_sinks, m_ev_conv_w, m_ev_conv_b, m_ev_conv_ln_g, m_ev_conv_ln_b, m_ev_w_out, m_od_norm_g, m_od_w_in, m_od_sgu_ln_g, m_od_sgu_ln_b, m_od_spatial_w, m_od_spatial_b, m_od_w_out, m_ffn_norm_g, m_ffn_w_gate, m_ffn_w_up, m_ffn_w_down, m_final_norm_g, v_ev_norm_g, v_ev_w_in, v_ev_sinks, v_ev_conv_w, v_ev_conv_b, v_ev_conv_ln_g, v_ev_conv_ln_b, v_ev_w_out, v_od_norm_g, v_od_w_in, v_od_sgu_ln_g, v_od_sgu_ln_b, v_od_spatial_w, v_od_spatial_b, v_od_w_out, v_ffn_norm_g, v_ffn_w_gate, v_ffn_w_up, v_ffn_w_down, v_final_norm_g):
    given = dict(locals())
    wts = {n: given[n] for n in WEIGHTS}
    mom = {n: given["m_" + n] for n in WEIGHTS}
    var = {n: given["v_" + n] for n in WEIGHTS}
    chip = 2 * lax.axis_index("x") + lax.axis_index("y")

    shard_rows = [_shard_rows(wts[n], layer, by_cols).astype(BF16) for n, layer, by_cols in BIG]
    counts = [a.shape[0] for a in shard_rows]
    n_first = sum(n.startswith('ev_') for n, _, _ in BIG)

    def unpack(stacked, entries, cnts):
        out, r = {}, 0
        for (n, layer, _), cnt in zip(entries, cnts):
            out[n, layer] = stacked[:, r:r + cnt].reshape(N_CHIPS * cnt, D_MODEL)
            r += cnt
        return out

    first_w = _gather_chips(jnp.concatenate(shard_rows[:n_first], axis=0), "gather_weights_ev")
    big = unpack(first_w, BIG[:n_first], counts[:n_first])
    send_sems, recv_sems, late_shard, late_land, token = _gather_start(jnp.concatenate(shard_rows[n_first:], axis=0),
                                                                      first_w, "gather_weights_start")

    def late_weights(after):
        shard, land = _gather_wait(send_sems, recv_sems, late_shard, late_land, after, "gather_weights_wait")
        return unpack(_gather_finish(shard, land, "gather_weights_finish"), BIG[n_first:], counts[n_first:])

    full = {n: wts[n] for n in SMALL_REPL}
    full['ev_norm_g'] = full['ev_norm_g'] + token[0:1, 0:1]
    small_shards = [wts[n] for n in SMALL_SHARDED]
    small_shapes = [a.shape for a in small_shards]
    all_s = _gather_chips(_pack_small(small_shards), "gather_small_weights")
    per_chip = [_unpack_small(all_s[k], small_shapes) for k in range(N_CHIPS)]
    for i, n in enumerate(SMALL_SHARDED):
        full[n] = jnp.concatenate([per_chip[k][i] for k in range(N_CHIPS)], axis=-1)

    half_rows = {(n, layer): cnt // 2 for (n, layer, _), cnt in zip(BIG, counts)}
    in_flight = []

    def pair_sum(stage, gbig):
        split = [gbig[e].reshape(N_CHIPS, 2, half_rows[e], D_MODEL) for e in GRAD_STAGES[stage]]
        got = _pair_send(split, f"grad_pair_send{stage}")
        return _pair_add(split, got, f"grad_pair_add{stage}")

    def reduce_early(stage, gbig):
        *handles, token = _exchange_start(pair_sum(stage, gbig), f"grad_exchange_start{stage}")
        in_flight.append(handles)
        return token

    loss_tile, grad_x, grads, gbig = _local_step(x[0], loss_target[0], full, big, late_weights, reduce_early)
    loss = lax.psum(loss_tile[0, 0], ("x", "y", "c"))

    reduced = {}
    for stage, entries in enumerate(GRAD_STAGES):
        if stage < len(in_flight):
            chip_part, from_chips = _exchange_wait(*in_flight[stage], grad_x, f"grad_exchange_wait{stage}")
        else:
            chip_part = pair_sum(stage, gbig)
            from_chips = _chip_exchange(chip_part, f"grad_chip_exchange{stage}")
        my_half = _chip_sum(chip_part, from_chips, chip.reshape(1), f"grad_chip_sum{stage}")
        joined = _join_unpack(my_half, [half_rows[e] for e in entries], list(range(len(entries))), f"grad_join_halves{stage}")
        reduced.update(zip(entries, joined))

    small_names = SMALL_REPL + SMALL_SHARDED
    small_full_shapes = [grads[n].shape for n in small_names]
    spack = _pack_small([grads[n] for n in small_names])
    s_all = _gather_devices(spack, "grad_small_gather")
    s_sum = _unpack_small(_ordered_sum(s_all, "grad_small_sum"), small_full_shapes)
    g_all = dict(zip(small_names, s_sum))
    for n in SMALL_SHARDED:
        width = wts[n].shape[-1]
        g_all[n] = lax.dynamic_slice_in_dim(g_all[n], chip * width, width, axis=g_all[n].ndim - 1)

    delta, new_m, new_v = {}, {}, {}
    for n in BIG_NAMES:
        by_cols = [bc for nn, _, bc in BIG if nn == n][0]
        layers = wts[n].shape[0]

        def as_rows(a):
            return (jnp.swapaxes(a, 1, 2) if by_cols else a).reshape(-1, D_MODEL)

        def from_rows(a):
            a = a.reshape(layers, -1, D_MODEL)
            return jnp.swapaxes(a, 1, 2) if by_cols else a

        g_rows = [reduced[n, layer] for layer in range(layers)]
        g_rows = g_rows[0] if layers == 1 else jnp.concatenate(g_rows, axis=0)
        updated = _adamw(as_rows(wts[n]), g_rows, as_rows(mom[n]), as_rows(var[n]), f"adamw_{n}")
        g_all[n] = from_rows(g_rows)
        delta[n], new_m[n], new_v[n] = (from_rows(a) for a in updated)
    shapes = [wts[n].shape for n in small_names]
    d_s, m_s, v_s = _adamw(*[_pack_small([src[n] for n in small_names]) for src in (wts, g_all, mom, var)], "adamw_small")
    for dst, packed in ((delta, d_s), (new_m, m_s), (new_v, v_s)):
        dst.update(zip(small_names, _unpack_small(packed, shapes)))

    return (loss, grad_x[None], *[g_all[n] for n in WEIGHTS], *[delta[n] for n in WEIGHTS],
            *[new_m[n] for n in WEIGHTS], *[new_v[n] for n in WEIGHTS])
```

```python
import math

import jax
import jax.numpy as jnp
from jax import lax
from jax.experimental import pallas as pl
from jax.experimental.pallas import tpu as pltpu

F32 = jnp.float32
BF16 = jnp.bfloat16

D_MODEL = 1024
HEAD_DIM = 64
ROT_DIM = 16
ROPE_THETA = 500000.0
RMS_EPS = 1e-6
LN_EPS = 1e-5
BLOCK = 128
CONV_WIDTH = 31
CONV_HALO = 32
CONV_ROWS = 64
D_FF = 2816
N_GROUPS = 8
ATTN_W = 512
ATTN_SCALE = HEAD_DIM ** -0.5
NEG = -1e30
DILATIONS = (1, 4, 16)

ADAM_LR = 0.001
ADAM_B1 = 0.9
ADAM_B2 = 0.999
ADAM_EPS = 1e-08
ADAM_WD = 0.01
ADAM_STEP = 10

LANES = 128
N_PAIRS = ATTN_W // LANES
VMEM_LIMIT = 56 * 1024 * 1024
MESH = pl.DeviceIdType.MESH
N_CHIPS = 4
N_DEV = 8

WEIGHTS = ['ev_norm_g', 'ev_w_in', 'ev_sinks', 'ev_conv_w', 'ev_conv_b', 'ev_conv_ln_g', 'ev_conv_ln_b', 'ev_w_out',
           'od_norm_g', 'od_w_in', 'od_sgu_ln_g', 'od_sgu_ln_b', 'od_spatial_w', 'od_spatial_b', 'od_w_out',
           'ffn_norm_g', 'ffn_w_gate', 'ffn_w_up', 'ffn_w_down', 'final_norm_g']
BIG = [('ev_w_in', 0, True), ('ev_w_out', 0, False), ('od_w_in', 0, True), ('od_w_out', 0, False),
       ('ffn_w_gate', 0, True), ('ffn_w_gate', 1, True), ('ffn_w_up', 0, True), ('ffn_w_up', 1, True),
       ('ffn_w_down', 0, False), ('ffn_w_down', 1, False)]
BIG_NAMES = ['ev_w_in', 'ev_w_out', 'od_w_in', 'od_w_out', 'ffn_w_gate', 'ffn_w_up', 'ffn_w_down']
GRAD_STAGES = ([('od_w_in', 0), ('od_w_out', 0), ('ffn_w_gate', 1), ('ffn_w_up', 1), ('ffn_w_down', 1)],
               [('ffn_w_gate', 0), ('ffn_w_up', 0), ('ffn_w_down', 0)],
               [('ev_w_in', 0), ('ev_w_out', 0)])
SMALL_SHARDED = ['ev_conv_w', 'od_norm_g', 'od_sgu_ln_g', 'od_sgu_ln_b']
SMALL_REPL = ['ev_norm_g', 'ev_sinks', 'ev_conv_b', 'ev_conv_ln_g', 'ev_conv_ln_b', 'od_spatial_w', 'od_spatial_b',
              'ffn_norm_g', 'final_norm_g']


def _tile(n, cap, mult=LANES):
    best = None
    for t in range(mult, min(n, cap) + 1, mult):
        if n % t == 0:
            best = t
    assert best is not None, (n, cap)
    return best


def _params(*sem):
    return pltpu.CompilerParams(dimension_semantics=sem, vmem_limit_bytes=VMEM_LIMIT)


def _sigmoid(x):
    return 1.0 / (1.0 + jnp.exp(-x))


def _pair_block(p):
    return slice(p * LANES, (p + 1) * LANES)


def _matmul(a, b, *, name, trans_a=False, trans_b=False, add=None, out_dtype=F32, after=None):
    parts = a if isinstance(a, (tuple, list)) else (a,)
    if trans_a:
        k, m = parts[0].shape
    else:
        m = parts[0].shape[0]
        k = sum(p.shape[1] for p in parts)
    if trans_b:
        n, k2 = b.shape
    else:
        k2, n = b.shape
    assert k == k2 and b.dtype == BF16 and all(p.dtype == BF16 for p in parts)
    tm = _tile(m, D_FF // 2 if trans_a else 512)
    tn = _tile(n, D_FF // 2)
    tk = k if k <= D_FF else _tile(k, 2048)
    nk = k // tk
    na = len(parts)
    assert na == 1 or (nk == 1 and not trans_a)
    assert nk == 1 or out_dtype == F32
    dims = (((0 if trans_a else 1,), (1 if trans_b else 0,)), ((), ()))
    has_add = add is not None

    def body(*refs):
        a_refs, b_ref = refs[:na], refs[na]
        add_ref = refs[na + 1] if has_add else None
        o_ref = refs[na + 1 + has_add + (after is not None)]
        def product():
            a_val = a_refs[0][...] if na == 1 else jnp.concatenate([r[...] for r in a_refs], axis=1)
            return lax.dot_general(a_val, b_ref[...], dims, preferred_element_type=F32)

        if nk == 1:
            part = product()
            if has_add:
                part = part + add_ref[...]
            o_ref[...] = part.astype(o_ref.dtype)
            return
        kk = pl.program_id(2)

        @pl.when(kk == 0)
        def _():
            o_ref[...] = product() + add_ref[...] if has_add else product()

        @pl.when(kk > 0)
        def _():
            o_ref[...] = product() + o_ref[...]

    if trans_a:
        a_specs = [pl.BlockSpec((tk, tm), lambda i, j, kk: (kk, i))]
    elif na == 1:
        a_specs = [pl.BlockSpec((tm, tk), lambda i, j, kk: (i, kk))]
    else:
        a_specs = [pl.BlockSpec((tm, p.shape[1]), lambda i, j, kk: (i, 0)) for p in parts]
    b_spec = pl.BlockSpec((tn, tk), lambda i, j, kk: (j, kk)) if trans_b else pl.BlockSpec((tk, tn), lambda i, j, kk: (kk, j))
    o_spec = pl.BlockSpec((tm, tn), lambda i, j, kk: (i, j))
    in_specs = a_specs + [b_spec] + ([o_spec] if has_add else [])
    operands = list(parts) + [b] + ([add] if has_add else [])
    if after is not None:
        in_specs.append(_after_spec(after))
        operands.append(after)
    return pl.pallas_call(
        body, name=name, grid=(m // tm, n // tn, nk), in_specs=in_specs, out_specs=o_spec,
        out_shape=jax.ShapeDtypeStruct((m, n), out_dtype),
        compiler_params=_params("parallel", "parallel", "arbitrary"),
    )(*operands)


def _matmul_rows(a, b, add, epilogue, consts, tiled, outs, accs, name):
    parts = a if isinstance(a, (tuple, list)) else (a,)
    m = parts[0].shape[0]
    tm = 512
    na, nc, nt, no = len(parts), len(consts), len(tiled), len(outs)

    def body(*refs):
        a_refs, b_ref, add_ref = refs[:na], refs[na], refs[na + 1]
        const_refs = refs[na + 2:na + 2 + nc]
        tiled_refs = refs[na + 2 + nc:na + 2 + nc + nt]
        out_refs = refs[na + 2 + nc + nt:]
        a_val = a_refs[0][...] if na == 1 else jnp.concatenate([r[...] for r in a_refs], axis=1)
        h = jnp.dot(a_val, b_ref[...], preferred_element_type=F32) + add_ref[...]
        results = epilogue(h, [r[...] for r in const_refs], [r[...] for r in tiled_refs])
        for o_ref, val in zip(out_refs[:no], results[:no]):
            o_ref[...] = val.astype(o_ref.dtype)
        if accs:
            @pl.when(_first_step())
            def _():
                for o_ref in out_refs[no:]:
                    o_ref[...] = jnp.zeros_like(o_ref)

            for o_ref, val in zip(out_refs[no:], results[no:]):
                o_ref[...] += val

    row = lambda w: pl.BlockSpec((tm, w), lambda i: (i, 0))
    whole = lambda shape: pl.BlockSpec(shape, lambda i: (0,) * len(shape))
    return pl.pallas_call(
        body, name=name, grid=(m // tm,),
        in_specs=[row(p.shape[1]) for p in parts] + [whole(b.shape), row(D_MODEL)] + [whole(c.shape) for c in consts]
        + [row(t.shape[1]) for t in tiled],
        out_specs=[row(c) for c, _ in outs] + [whole(sh) for sh, _ in accs],
        out_shape=[jax.ShapeDtypeStruct((m, c), dt) for c, dt in outs] + [jax.ShapeDtypeStruct(sh, dt) for sh, dt in accs],
        compiler_params=_params("arbitrary"),
    )(*parts, b, add, *consts, *tiled)


def _matmul_norm(a, b, add, g, name):
    def epilogue(h, consts, tiled):
        r = lax.rsqrt(jnp.mean(h * h, axis=-1, keepdims=True) + RMS_EPS)
        return [h, h * r * consts[0]]

    return _matmul_rows(a, b, add, epilogue, [g], [], [(D_MODEL, F32), (D_MODEL, BF16)], [], name)


def _matmul_final(a, b, add, g, tgt, name):
    def epilogue(h, consts, tiled):
        gg = consts[0]
        r = lax.rsqrt(jnp.mean(h * h, axis=-1, keepdims=True) + RMS_EPS)
        xh = h * r
        e = xh * gg - tiled[0]
        loss = (0.5 / D_MODEL) * jnp.sum(jnp.sum(e * e, axis=-1, keepdims=True), axis=0, keepdims=True)
        dy = e * (1.0 / D_MODEL)
        dxh = dy * gg
        dx = r * (dxh - xh * jnp.mean(dxh * xh, axis=-1, keepdims=True))
        return [dx, dx, jnp.sum(dy * xh, axis=0, keepdims=True), jnp.broadcast_to(loss, (1, LANES))]

    return _matmul_rows(a, b, add, epilogue, [g], [tgt], [(D_MODEL, F32), (D_MODEL, BF16)],
                        [((1, D_MODEL), F32), ((1, LANES), F32)], name)


def _matmul_tn_pair(a1, a2, b, name):
    kdim, m1 = a1.shape
    m2 = a2.shape[1]
    n = b.shape[1]
    tn = _tile(n, 1024)
    tk = _tile(kdim, 2048)
    nk = kdim // tk
    dims = (((0,), (0,)), ((), ()))

    def body(a1_ref, a2_ref, b_ref, o_ref):
        kk = pl.program_id(1)
        def products():
            bv = b_ref[...]
            return (lax.dot_general(a1_ref[...], bv, dims, preferred_element_type=F32),
                    lax.dot_general(a2_ref[...], bv, dims, preferred_element_type=F32))

        @pl.when(kk == 0)
        def _():
            o_ref[0:m1, :], o_ref[m1:, :] = products()

        @pl.when(kk > 0)
        def _():
            top, bot = products()
            o_ref[0:m1, :] = top + o_ref[0:m1, :]
            o_ref[m1:, :] = bot + o_ref[m1:, :]

    return pl.pallas_call(
        body, name=name, grid=(n // tn, nk),
        in_specs=[pl.BlockSpec((tk, m1), lambda j, kk: (kk, 0)), pl.BlockSpec((tk, m2), lambda j, kk: (kk, 0)),
                  pl.BlockSpec((tk, tn), lambda j, kk: (kk, j))],
        out_specs=pl.BlockSpec((m1 + m2, tn), lambda j, kk: (0, j)),
        out_shape=jax.ShapeDtypeStruct((m1 + m2, n), F32),
        compiler_params=_params("parallel", "arbitrary"),
    )(a1, a2, b)


def _ffn_gate_up(n, w_gate_t, w_up_t, name):
    m, k = n.shape
    f = w_gate_t.shape[0]
    tm, tn = _tile(m, 1024), _tile(f, D_FF // 2)

    def body(n_ref, wg_ref, wu_ref, act_ref, gate_ref, up_ref):
        a = n_ref[...]
        for cols in _col_chunks(tn):
            gate = lax.dot_general(a, wg_ref[cols, :], NT, preferred_element_type=F32)
            up = lax.dot_general(a, wu_ref[cols, :], NT, preferred_element_type=F32)
            act_ref[:, cols] = (gate * _sigmoid(gate) * up).astype(BF16)
            gate_ref[:, cols] = gate.astype(BF16)
            up_ref[:, cols] = up.astype(BF16)

    wspec = pl.BlockSpec((tn, k), lambda j, i: (j, 0))
    ospec = pl.BlockSpec((tm, tn), lambda j, i: (i, j))
    return pl.pallas_call(
        body, name=name, grid=(f // tn, m // tm), in_specs=[pl.BlockSpec((tm, k), lambda j, i: (i, 0)), wspec, wspec],
        out_specs=[ospec] * 3, out_shape=[jax.ShapeDtypeStruct((m, f), BF16)] * 3,
        compiler_params=_params("parallel", "parallel"),
    )(n, w_gate_t, w_up_t)


def _col_chunks(n, width=384):
    return [slice(c, min(c + width, n)) for c in range(0, n, width)]


def _after_spec(after):
    return pl.BlockSpec(after.shape, lambda *_: (0,) * after.ndim)


def _ffn_dact(dhb, w_down, gate, up, name, after=None):
    m, k = dhb.shape
    f = w_down.shape[0]
    tm, tn = _tile(m, 1024), _tile(f, D_FF // 2)

    def body(d_ref, w_ref, g_ref, u_ref, *rest):
        dg_ref, du_ref = rest[-2:]
        d = d_ref[...]
        for cols in _col_chunks(tn):
            dact = lax.dot_general(d, w_ref[cols, :], NT, preferred_element_type=F32)
            g = g_ref[:, cols].astype(F32)
            sg = _sigmoid(g)
            dg_ref[:, cols] = (dact * u_ref[:, cols].astype(F32) * sg * (1.0 + g * (1.0 - sg))).astype(BF16)
            du_ref[:, cols] = (dact * g * sg).astype(BF16)

    ospec = pl.BlockSpec((tm, tn), lambda j, i: (i, j))
    extra = [] if after is None else [after]
    return pl.pallas_call(
        body, name=name, grid=(f // tn, m // tm),
        in_specs=[pl.BlockSpec((tm, k), lambda j, i: (i, 0)), pl.BlockSpec((tn, k), lambda j, i: (j, 0)), ospec, ospec]
        + [_after_spec(a) for a in extra],
        out_specs=[ospec] * 2, out_shape=[jax.ShapeDtypeStruct((m, f), BF16)] * 2,
        compiler_params=_params("parallel", "parallel"),
    )(dhb, w_down, gate, up, *extra)


def _dn_norm(pairs, h, g, dres, name):
    m = h.shape[0]
    tm = 512
    np_ = len(pairs)

    def body(*refs):
        a_refs, b_refs = refs[:np_], refs[np_:2 * np_]
        h_ref, dres_ref, g_ref, dh_ref, dhb_ref, dg_ref = refs[2 * np_:]

        @pl.when(_first_step())
        def _():
            dg_ref[...] = jnp.zeros_like(dg_ref)

        dy = jnp.dot(a_refs[0][...], b_refs[0][...], preferred_element_type=F32)
        for a_ref, b_ref in zip(a_refs[1:], b_refs[1:]):
            dy = jnp.dot(a_ref[...], b_ref[...], preferred_element_type=F32) + dy
        x = h_ref[...]
        r = lax.rsqrt(jnp.mean(x * x, axis=-1, keepdims=True) + RMS_EPS)
        xh = x * r
        dg_ref[...] += jnp.sum(dy * xh, axis=0, keepdims=True)
        dxh = dy * g_ref[...]
        tot = dres_ref[...] + r * (dxh - xh * jnp.mean(dxh * xh, axis=-1, keepdims=True))
        dh_ref[...] = tot
        dhb_ref[...] = tot.astype(BF16)

    row = lambda w: pl.BlockSpec((tm, w), lambda i: (i, 0))
    whole = lambda a: pl.BlockSpec(a.shape, lambda i: (0, 0))
    a_list, b_list = [a for a, _ in pairs], [b for _, b in pairs]
    return pl.pallas_call(
        body, name=name, grid=(m // tm,),
        in_specs=[row(a.shape[1]) for a in a_list] + [whole(b) for b in b_list] + [row(D_MODEL), row(D_MODEL), whole(g)],
        out_specs=[row(D_MODEL), row(D_MODEL), pl.BlockSpec((1, D_MODEL), lambda i: (0, 0))],
        out_shape=[jax.ShapeDtypeStruct((m, D_MODEL), F32), jax.ShapeDtypeStruct((m, D_MODEL), BF16),
                   jax.ShapeDtypeStruct((1, D_MODEL), F32)],
        compiler_params=_params("arbitrary"),
    )(*a_list, *b_list, h, dres, g)


def _rows(body, name, tm, tiled, consts, outs, accs=()):
    s = tiled[0].shape[0]
    assert s % tm == 0
    in_specs = [pl.BlockSpec((tm, a.shape[1]), lambda i: (i, 0)) for a in tiled]
    in_specs += [pl.BlockSpec(a.shape, lambda i, nd=a.ndim: (0,) * nd) for a in consts]
    out_shape = [jax.ShapeDtypeStruct((s, c), dt) for c, dt in outs]
    out_shape += [jax.ShapeDtypeStruct(sh, dt) for sh, dt in accs]
    out_specs = [pl.BlockSpec((tm, c), lambda i: (i, 0)) for c, _ in outs]
    out_specs += [pl.BlockSpec(sh, lambda i, nd=len(sh): (0,) * nd) for sh, _ in accs]
    return pl.pallas_call(
        body, name=name, grid=(s // tm,), in_specs=in_specs, out_specs=out_specs, out_shape=out_shape,
        compiler_params=_params("arbitrary"),
    )(*tiled, *consts)


def _first_step():
    return pl.program_id(0) == 0


def _rms_fwd(h, g, name):
    def body(h_ref, g_ref, n_ref):
        x = h_ref[...]
        r = lax.rsqrt(jnp.mean(x * x, axis=-1, keepdims=True) + RMS_EPS)
        n_ref[...] = (x * r * g_ref[...]).astype(BF16)

    return _rows(body, name, 512, [h], [g], [(D_MODEL, BF16)])[0]


def _rope_tables(s):
    half = ROT_DIM // 2
    inv_freq = ROPE_THETA ** (-jnp.arange(half, dtype=F32) * (2.0 / ROT_DIM))
    ang = jnp.arange(s, dtype=F32)[:, None] * inv_freq[None, :]
    cos, sin = jnp.cos(ang), jnp.sin(ang)
    rest = HEAD_DIM - ROT_DIM
    ones = jnp.ones((s, rest), F32)
    zeros = jnp.zeros((s, rest), F32)
    zh = jnp.zeros((s, half), F32)
    c_t = jnp.concatenate([cos, cos, ones], axis=1)
    a_t = jnp.concatenate([-sin, zh, zeros], axis=1)
    b_t = jnp.concatenate([zh, sin, zeros], axis=1)
    return tuple(jnp.tile(t, (1, LANES // HEAD_DIM)) for t in (c_t, a_t, b_t))


def _rot(x, c, a, b):
    w = x.shape[1]
    half = ROT_DIM // 2
    return x * c + pltpu.roll(x, w - half, 1) * a + pltpu.roll(x, half, 1) * b


def _wide(t, w):
    return t if w == LANES else jnp.tile(t, (1, w // LANES))


def _low_lanes(rows):
    return lax.broadcasted_iota(jnp.int32, (rows, LANES), 1) < HEAD_DIM


def _fold_store(x, sc_ref, out_refs):
    tm = x.shape[0]
    if any(d > 1 for d in out_refs):
        for p in range(N_PAIRS):
            sc_ref[p] = x[:, _pair_block(p)]
    for d, o_ref in out_refs.items():
        if d == 1:
            o_ref[0] = x.astype(o_ref.dtype)
            continue
        for r in range(d):
            for p in range(N_PAIRS):
                o_ref[r, :, _pair_block(p)] = sc_ref[p, pl.ds(r, tm // d, stride=d), :].astype(o_ref.dtype)


def _unfold_load(x_ref, sc_ref, d, add=False):
    n = x_ref.shape[1]
    for r in range(d):
        for p in range(N_PAIRS):
            rows = pl.ds(r, n, stride=d) if d > 1 else slice(None)
            val = x_ref[r, :, _pair_block(p)].astype(F32)
            if add:
                val = val + sc_ref[p, rows, :]
            sc_ref[p, rows, :] = val


def _folded_spec(d, tm, w=ATTN_W):
    return pl.BlockSpec((d, tm // d, w), lambda i: (0, i, 0))


def _folded_shape(s, d, dtype, w=ATTN_W):
    return jax.ShapeDtypeStruct((d, s // d, w), dtype)


def _qkv_prep_even(proj, tabs, name):
    s = proj.shape[0]
    tm = 512

    def body(p_ref, c_ref, a_ref, b_ref, q_ref, k_ref, v_ref):
        c, a, b = c_ref[...], a_ref[...], b_ref[...]
        q_ref[0] = _rot(p_ref[:, 0:ATTN_W].astype(F32), _wide(c, ATTN_W), _wide(a, ATTN_W), _wide(b, ATTN_W)).astype(BF16)
        lo = _low_lanes(tm)
        for src, o_ref in ((_rot(p_ref[:, 512:640].astype(F32), c, a, b), k_ref), (p_ref[:, 640:768].astype(F32), v_ref)):
            swapped = pltpu.roll(src, HEAD_DIM, 1)
            o_ref[0, :, 0:LANES] = jnp.where(lo, src, swapped).astype(BF16)
            o_ref[0, :, LANES:] = jnp.where(lo, swapped, src).astype(BF16)

    row = lambda w: pl.BlockSpec((tm, w), lambda i: (i, 0))
    return pl.pallas_call(
        body, name=name, grid=(s // tm,), in_specs=[row(proj.shape[1]), row(LANES), row(LANES), row(LANES)],
        out_specs=[_folded_spec(1, tm), _folded_spec(1, tm, 2 * LANES), _folded_spec(1, tm, 2 * LANES)],
        out_shape=[_folded_shape(s, 1, BF16), _folded_shape(s, 1, BF16, 2 * LANES), _folded_shape(s, 1, BF16, 2 * LANES)],
        compiler_params=_params("parallel"),
    )(proj, *tabs)


def _qkv_post_even(dq, dk, dv, dglu, tabs, name):
    s = dglu.shape[0]
    tm = 512

    def body(dq_ref, dk_ref, dv_ref, dr_ref, c_ref, a_ref, b_ref, o_ref):
        c, a, b = c_ref[...], -a_ref[...], -b_ref[...]
        o_ref[:, 0:ATTN_W] = _rot(dq_ref[0], _wide(c, ATTN_W), _wide(a, ATTN_W), _wide(b, ATTN_W)).astype(BF16)
        lo = _low_lanes(tm)
        merged = []
        for ref in (dk_ref, dv_ref):
            first, second = ref[0, :, 0:LANES], ref[0, :, LANES:]
            merged.append(jnp.where(lo, first + pltpu.roll(first, HEAD_DIM, 1), second + pltpu.roll(second, HEAD_DIM, 1)))
        o_ref[:, 512:640] = _rot(merged[0], c, a, b).astype(BF16)
        o_ref[:, 640:768] = merged[1].astype(BF16)
        o_ref[:, 768:] = dr_ref[...]

    row = lambda w: pl.BlockSpec((tm, w), lambda i: (i, 0))
    return pl.pallas_call(
        body, name=name, grid=(s // tm,),
        in_specs=[_folded_spec(1, tm), _folded_spec(1, tm, 2 * LANES), _folded_spec(1, tm, 2 * LANES),
                  row(dglu.shape[1]), row(LANES), row(LANES), row(LANES)],
        out_specs=row(EVEN_IN), out_shape=jax.ShapeDtypeStruct((s, EVEN_IN), BF16),
        compiler_params=_params("parallel"),
    )(dq, dk, dv, dglu, *tabs)


def _qkv_prep_odd(proj, tabs, name):
    s = proj.shape[0]
    tm = 512

    def body(p_ref, c_ref, a_ref, b_ref, *rest):
        outs, sc_ref = rest[:-1], rest[-1]
        c, a, b = (_wide(t[...], ATTN_W) for t in (c_ref, a_ref, b_ref))
        for t in range(3):
            x = p_ref[:, t * ATTN_W:(t + 1) * ATTN_W].astype(F32)
            if t < 2:
                x = _rot(x, c, a, b)
            _fold_store(x, sc_ref, {d: outs[t * len(DILATIONS) + i] for i, d in enumerate(DILATIONS)})

    row = lambda w: pl.BlockSpec((tm, w), lambda i: (i, 0))
    return pl.pallas_call(
        body, name=name, grid=(s // tm,), in_specs=[row(proj.shape[1]), row(LANES), row(LANES), row(LANES)],
        out_specs=[_folded_spec(d, tm) for _ in range(3) for d in DILATIONS],
        out_shape=[_folded_shape(s, d, BF16) for _ in range(3) for d in DILATIONS],
        scratch_shapes=[pltpu.VMEM((N_PAIRS, tm, LANES), F32)],
        compiler_params=_params("parallel"),
    )(proj, *tabs)


def _qkv_post_odd(dqs, dks, dvs, dz, tabs, name):
    s = dz.shape[0]
    tm = 256
    nb = len(DILATIONS)

    def body(*refs):
        groups = (refs[:nb], refs[nb:2 * nb], refs[2 * nb:3 * nb])
        dz_ref, c_ref, a_ref, b_ref, o_ref, sc_ref = refs[3 * nb:]
        c, a, b = _wide(c_ref[...], ATTN_W), _wide(-a_ref[...], ATTN_W), _wide(-b_ref[...], ATTN_W)
        for t, group in enumerate(groups):
            for i, d in enumerate(DILATIONS):
                _unfold_load(group[i], sc_ref, d, add=i > 0)
            x = jnp.concatenate([sc_ref[p] for p in range(N_PAIRS)], axis=1)
            if t < 2:
                x = _rot(x, c, a, b)
            o_ref[:, t * ATTN_W:(t + 1) * ATTN_W] = x.astype(BF16)
        o_ref[:, 3 * ATTN_W:] = dz_ref[...]

    row = lambda w: pl.BlockSpec((tm, w), lambda i: (i, 0))
    return pl.pallas_call(
        body, name=name, grid=(s // tm,),
        in_specs=[_folded_spec(d, tm) for _ in range(3) for d in DILATIONS] + [row(dz.shape[1]), row(LANES), row(LANES), row(LANES)],
        out_specs=row(ODD_IN), out_shape=jax.ShapeDtypeStruct((s, ODD_IN), BF16),
        scratch_shapes=[pltpu.VMEM((N_PAIRS, tm, LANES), F32)],
        compiler_params=_params("parallel"),
    )(*dqs, *dks, *dvs, dz, *tabs)


def _fold_dout(dmix, name):
    s = dmix.shape[0]
    tm = 512
    ds = [d for d in DILATIONS if d > 1]

    def body(d_ref, *rest):
        outs, sc_ref = rest[:-1], rest[-1]
        _fold_store(d_ref[...], sc_ref, dict(zip(ds, outs)))

    return pl.pallas_call(
        body, name=name, grid=(s // tm,), in_specs=[pl.BlockSpec((tm, ATTN_W), lambda i: (i, 0))],
        out_specs=[_folded_spec(d, tm) for d in ds], out_shape=[_folded_shape(s, d, BF16) for d in ds],
        scratch_shapes=[pltpu.VMEM((N_PAIRS, tm, LANES), F32)],
        compiler_params=_params("parallel"),
    )(dmix)


def _window(j, i, tq):
    r0 = j * tq + i * BLOCK
    start = pl.multiple_of(jnp.maximum(r0 - BLOCK, 0), BLOCK)
    return pl.ds(start, 2 * BLOCK), r0 - start


def _band_valid(offset, max_dist):
    shape = (2 * BLOCK, 2 * BLOCK)
    dist = (lax.bitwise_and(lax.broadcasted_iota(jnp.int32, shape, 0), BLOCK - 1)
            - lax.broadcasted_iota(jnp.int32, shape, 1) + offset)
    return jnp.abs(2 * dist - max_dist) <= max_dist


def _stack_heads(lo, x):
    zero = jnp.zeros_like(x)
    return jnp.concatenate([jnp.where(lo, x, zero), jnp.where(lo, zero, x)], axis=0)


def _unstack_heads(lo, x):
    return jnp.where(lo, x[:BLOCK], x[BLOCK:])


NT = (((1,), (1,)), ((), ()))
TN = (((0,), (0,)), ((), ()))


def _attn_fwd(q, k, v, sinks, *, max_dist, name, emit_bf16=False):
    d, sp, wq = q.shape
    nq, nk = wq // LANES, k.shape[2] // LANES
    kdiv = nq // nk
    tq = min(sp, 1024)
    nsub = tq // BLOCK
    has_sink = sinks is not None

    def body(*refs):
        refs = list(refs)
        sink_ref = refs.pop(0) if has_sink else None
        q_ref, k_ref, v_ref, o_ref, lse_ref = refs[:5]
        pair = pl.program_id(1)
        j = pl.program_id(2)
        lo = _low_lanes(BLOCK)
        if has_sink:
            first_head = lax.broadcasted_iota(jnp.int32, (2 * BLOCK, 1), 0) < BLOCK
            sk = jnp.where(first_head, sink_ref[2 * pair], sink_ref[2 * pair + 1])
        for i in range(nsub):
            win, offset = _window(j, i, tq)
            rows = slice(i * BLOCK, (i + 1) * BLOCK)
            kw = k_ref[0, win, :]
            vw = v_ref[0, win, :]
            s = lax.dot_general(_stack_heads(lo, q_ref[0, rows, :]), kw, NT, preferred_element_type=F32) * ATTN_SCALE
            s = jnp.where(_band_valid(offset, max_dist), s, NEG)
            m = jnp.max(s, axis=-1, keepdims=True)
            if has_sink:
                m = jnp.maximum(m, sk)
            p = jnp.exp(s - m)
            l = jnp.sum(p, axis=-1, keepdims=True)
            if has_sink:
                l = l + jnp.exp(sk - m)
            o2 = _unstack_heads(lo, jnp.dot(p.astype(BF16), vw, preferred_element_type=F32) / l)
            o_ref[0, rows, :] = o2
            lse_ref[0, rows, :] = _unstack_heads(lo, m + jnp.log(l))
            if emit_bf16:
                refs[5][0, rows, :] = o2.astype(BF16)

    qspec = pl.BlockSpec((1, tq, LANES), lambda r, p, j: (r, j, p))
    kspec = pl.BlockSpec((1, sp, LANES), lambda r, p, j: (r, 0, p // kdiv))
    in_specs = [qspec, kspec, kspec]
    operands = [q, k, v]
    if has_sink:
        in_specs = [pl.BlockSpec(memory_space=pltpu.SMEM)] + in_specs
        operands = [sinks] + operands
    out_shape = [jax.ShapeDtypeStruct(q.shape, F32), jax.ShapeDtypeStruct(q.shape, F32)]
    if emit_bf16:
        out_shape.append(jax.ShapeDtypeStruct(q.shape, BF16))
    return pl.pallas_call(
        body, name=name, grid=(d, nq, sp // tq), in_specs=in_specs, out_specs=[qspec] * len(out_shape),
        out_shape=out_shape, compiler_params=_params("parallel", "parallel", "arbitrary"),
    )(*operands)


def _attn_bwd(q, k, v, do, oo, lse, sinks, *, max_dist, name):
    d, sp, wq = q.shape
    wk = k.shape[2]
    nq, nk = wq // LANES, wk // LANES
    kdiv = nq // nk
    tq = min(sp, 1024)
    nsub = tq // BLOCK
    has_sink = sinks is not None

    def body(*refs):
        refs = list(refs)
        sink_ref = refs.pop(0) if has_sink else None
        q_ref, k_ref, v_ref, do_ref, oo_ref, lse_ref, dq_ref, dk_ref, dv_ref = refs[:9]
        pk, g, j = pl.program_id(1), pl.program_id(2), pl.program_id(3)

        @pl.when((g == 0) & (j == 0))
        def _():
            dk_ref[...] = jnp.zeros_like(dk_ref)
            dv_ref[...] = jnp.zeros_like(dv_ref)

        lo = _low_lanes(BLOCK)
        if has_sink:
            first_head = lax.broadcasted_iota(jnp.int32, (2 * BLOCK, 1), 0) < BLOCK
            pair = pk * kdiv + g
            sk = jnp.where(first_head, sink_ref[2 * pair], sink_ref[2 * pair + 1])
            sink_acc = jnp.zeros((2 * BLOCK, LANES), F32)
        for i in range(nsub):
            win, offset = _window(j, i, tq)
            rows = slice(i * BLOCK, (i + 1) * BLOCK)
            kw = k_ref[0, win, :]
            vw = v_ref[0, win, :]
            do2 = do_ref[0, rows, :].astype(F32)
            qs = _stack_heads(lo, q_ref[0, rows, :])
            dos = _stack_heads(lo, do2.astype(BF16))
            prod = do2 * oo_ref[0, rows, :]
            delta = jnp.sum(_stack_heads(lo, prod), axis=-1, keepdims=True)
            lse2 = lse_ref[0, rows, :]
            lse_swapped = pltpu.roll(lse2, HEAD_DIM, 1)
            lse_st = jnp.concatenate([jnp.where(lo, lse2, lse_swapped), jnp.where(lo, lse_swapped, lse2)], axis=0)
            s = lax.dot_general(qs, kw, NT, preferred_element_type=F32) * ATTN_SCALE
            s = jnp.where(_band_valid(offset, max_dist), s, NEG)
            p = jnp.exp(s - jnp.tile(lse_st, (1, 2)))
            dv_ref[0, win, :] = lax.dot_general(p.astype(BF16), dos, TN, preferred_element_type=F32) + dv_ref[0, win, :]
            dp = lax.dot_general(dos, vw, NT, preferred_element_type=F32)
            ds = (p * (dp - delta) * ATTN_SCALE).astype(BF16)
            dq_ref[0, rows, :] = _unstack_heads(lo, jnp.dot(ds, kw, preferred_element_type=F32))
            dk_ref[0, win, :] = lax.dot_general(ds, qs, TN, preferred_element_type=F32) + dk_ref[0, win, :]
            if has_sink:
                sink_acc = sink_acc - jnp.exp(sk - lse_st) * delta
        if has_sink:
            dsink_ref = refs[9]

            @pl.when(j == 0)
            def _():
                dsink_ref[...] = jnp.zeros_like(dsink_ref)

            dsink_ref[0] += jnp.where(lo[0:1], jnp.sum(sink_acc[:BLOCK], axis=0, keepdims=True),
                                      jnp.sum(sink_acc[BLOCK:], axis=0, keepdims=True))

    def qmap(r, pk, g, j):
        return (r, j, pk * kdiv + g)

    def kmap(r, pk, g, j):
        return (r, 0, pk)

    qspec = pl.BlockSpec((1, tq, LANES), qmap)
    kspec = pl.BlockSpec((1, sp, LANES), kmap)
    in_specs = [qspec, kspec, kspec, qspec, qspec, qspec]
    operands = [q, k, v, do, oo, lse]
    out_specs = [qspec, kspec, kspec]
    out_shape = [jax.ShapeDtypeStruct((d, sp, wq), F32), jax.ShapeDtypeStruct((d, sp, wk), F32),
                 jax.ShapeDtypeStruct((d, sp, wk), F32)]
    if has_sink:
        in_specs = [pl.BlockSpec(memory_space=pltpu.SMEM)] + in_specs
        operands = [sinks] + operands
        out_specs.append(pl.BlockSpec((1, 1, LANES), lambda r, pk, g, j: (pk * kdiv + g, 0, 0)))
        out_shape.append(jax.ShapeDtypeStruct((nq, 1, LANES), F32))
    return pl.pallas_call(
        body, name=name, grid=(d, nk, kdiv, sp // tq), in_specs=in_specs, out_specs=out_specs, out_shape=out_shape,
        compiler_params=_params("parallel", "parallel", "arbitrary", "arbitrary"),
    )(*operands)


def _combine(outs, lses, name):
    s = outs[0].shape[1]
    tm = 512
    nb = len(DILATIONS)
    ds = [d for d in DILATIONS if d > 1]

    def body(*refs):
        o_refs, l_refs = refs[:nb], refs[nb:2 * nb]
        cb_ref, c_ref, lse_ref = refs[2 * nb:2 * nb + 3]
        folded = refs[2 * nb + 3:2 * nb + 3 + 2 * len(ds)]
        scratch = refs[2 * nb + 3 + 2 * len(ds):]
        so = {1: None}
        sl = {1: None}
        for i, d in enumerate(ds):
            so[d], sl[d] = scratch[2 * i], scratch[2 * i + 1]
            _unfold_load(o_refs[1 + i], so[d], d)
            _unfold_load(l_refs[1 + i], sl[d], d)
        for p in range(N_PAIRS):
            pb = _pair_block(p)
            ls = [l_refs[0][0, :, pb]] + [sl[d][p] for d in ds]
            os_ = [o_refs[0][0, :, pb]] + [so[d][p] for d in ds]
            m = ls[0]
            for t in ls[1:]:
                m = jnp.maximum(m, t)
            ws = [jnp.exp(t - m) for t in ls]
            tot = ws[0]
            for t in ws[1:]:
                tot = tot + t
            acc = ws[0] * os_[0]
            for w, o in zip(ws[1:], os_[1:]):
                acc = acc + w * o
            cmix = acc / tot
            lse = m + jnp.log(tot)
            cb_ref[:, pb] = cmix.astype(BF16)
            c_ref[0, :, pb] = cmix
            lse_ref[0, :, pb] = lse
            so[ds[0]][p] = cmix
            sl[ds[0]][p] = lse
        for i, d in enumerate(ds):
            for r in range(d):
                for p in range(N_PAIRS):
                    rows = pl.ds(r, tm // d, stride=d)
                    folded[2 * i][r, :, _pair_block(p)] = so[ds[0]][p, rows, :]
                    folded[2 * i + 1][r, :, _pair_block(p)] = sl[ds[0]][p, rows, :]

    in_specs = [_folded_spec(d, tm) for _ in range(2) for d in DILATIONS]
    out_specs = [pl.BlockSpec((tm, ATTN_W), lambda i: (i, 0)), _folded_spec(1, tm), _folded_spec(1, tm)]
    out_shape = [jax.ShapeDtypeStruct((s, ATTN_W), BF16), _folded_shape(s, 1, F32), _folded_shape(s, 1, F32)]
    for d in ds:
        out_specs += [_folded_spec(d, tm)] * 2
        out_shape += [_folded_shape(s, d, F32)] * 2
    return pl.pallas_call(
        body, name=name, grid=(s // tm,), in_specs=in_specs, out_specs=out_specs, out_shape=out_shape,
        scratch_shapes=[pltpu.VMEM((N_PAIRS, tm, LANES), F32)] * (2 * len(ds)),
        compiler_params=_params("parallel"),
    )(*outs, *lses)


GLU_A = slice(768, 1280)
GLU_B = slice(1280, 1792)
EVEN_IN = 1792
ODD_IN = 2560
CONV_CH = 512


def _shifted_copies(xs_ref):
    rows = xs_ref.shape[1] - 8
    for b in range(1, 8):
        xs_ref[b, 0:rows, :] = xs_ref[0, pl.ds(b, rows), :]


def _shifted_rows(xs_ref, start):
    return xs_ref[start % 8, pl.ds(start - start % 8, CONV_ROWS), :]


def _glu(p_ref):
    return p_ref[:, GLU_A].astype(F32) * _sigmoid(p_ref[:, GLU_B].astype(F32))


def _conv_fwd(proj, w, b, ln_g, ln_b, name):
    s = proj.shape[0]
    tm = 512
    nh = tm // CONV_HALO
    lead = CONV_HALO - (CONV_WIDTH - 1)

    def body(p_ref, ph_ref, w_ref, b_ref, g_ref, bb_ref, y_ref, o_ref, xs_ref):
        xs_ref[0, CONV_HALO:, :] = _glu(p_ref)
        xs_ref[0, 0:CONV_HALO, :] = jnp.where(pl.program_id(0) > 0, _glu(ph_ref), 0.0)
        _shifted_copies(xs_ref)
        for c0 in range(0, tm, CONV_ROWS):
            acc = jnp.zeros((CONV_ROWS, CONV_CH), F32) + b_ref[...]
            for j in range(CONV_WIDTH):
                acc = acc + _shifted_rows(xs_ref, lead + j + c0) * w_ref[j:j + 1, :]
            y_ref[c0:c0 + CONV_ROWS, :] = acc
            mu = jnp.mean(acc, axis=-1, keepdims=True)
            xc = acc - mu
            var = jnp.mean(xc * xc, axis=-1, keepdims=True)
            zz = xc * lax.rsqrt(var + LN_EPS) * g_ref[...] + bb_ref[...]
            o_ref[c0:c0 + CONV_ROWS, :] = (zz * _sigmoid(zz)).astype(BF16)

    def const(a):
        return pl.BlockSpec(a.shape, lambda i: (0, 0))

    return pl.pallas_call(
        body, name=name, grid=(s // tm,),
        in_specs=[pl.BlockSpec((tm, EVEN_IN), lambda i: (i, 0)),
                  pl.BlockSpec((CONV_HALO, EVEN_IN), lambda i: (jnp.maximum(i * nh - 1, 0), 0)),
                  const(w), const(b), const(ln_g), const(ln_b)],
        out_specs=[pl.BlockSpec((tm, CONV_CH), lambda i: (i, 0)), pl.BlockSpec((tm, CONV_CH), lambda i: (i, 0))],
        out_shape=[jax.ShapeDtypeStruct((s, CONV_CH), F32), jax.ShapeDtypeStruct((s, CONV_CH), BF16)],
        scratch_shapes=[pltpu.VMEM((8, tm + CONV_HALO, CONV_CH), F32)],
        compiler_params=_params("arbitrary"),
    )(proj, proj, w, b, ln_g, ln_b)


def _conv_tail_bwd(dmix, yconv, ln_g, ln_b, name):
    def body(d_ref, y_ref, g_ref, b_ref, dy_ref, dg_ref, db_ref, dcb_ref):
        @pl.when(_first_step())
        def _():
            dg_ref[...] = jnp.zeros_like(dg_ref)
            db_ref[...] = jnp.zeros_like(db_ref)
            dcb_ref[...] = jnp.zeros_like(dcb_ref)

        y = y_ref[...]
        g = g_ref[...]
        mu = jnp.mean(y, axis=-1, keepdims=True)
        xc = y - mu
        rstd = lax.rsqrt(jnp.mean(xc * xc, axis=-1, keepdims=True) + LN_EPS)
        xh = xc * rstd
        zz = xh * g + b_ref[...]
        sg = _sigmoid(zz)
        dzz = d_ref[:, CONV_CH:] * sg * (1.0 + zz * (1.0 - sg))
        dg_ref[...] += jnp.sum(dzz * xh, axis=0, keepdims=True)
        db_ref[...] += jnp.sum(dzz, axis=0, keepdims=True)
        dxh = dzz * g
        dy = rstd * (dxh - jnp.mean(dxh, axis=-1, keepdims=True) - xh * jnp.mean(dxh * xh, axis=-1, keepdims=True))
        dcb_ref[...] += jnp.sum(dy, axis=0, keepdims=True)
        dy_ref[...] = dy

    vec = ((1, CONV_CH), F32)
    return _rows(body, name, 512, [dmix, yconv], [ln_g, ln_b], [(CONV_CH, F32)], [vec, vec, vec])


def _conv_bwd(proj, dy, w, name):
    s = proj.shape[0]
    tm = 512
    nh = tm // CONV_HALO
    nsteps = s // tm
    lead = CONV_HALO - (CONV_WIDTH - 1)

    def body(p_ref, ph_ref, dy_ref, dyn_ref, w_ref, dglu_ref, dw_ref, xf_ref, dyf_ref):
        i = pl.program_id(0)

        @pl.when(i == 0)
        def _():
            dw_ref[...] = jnp.zeros_like(dw_ref)

        ga = p_ref[:, GLU_A].astype(F32)
        sgb = _sigmoid(p_ref[:, GLU_B].astype(F32))
        xf_ref[0, CONV_HALO:, :] = ga * sgb
        xf_ref[0, 0:CONV_HALO, :] = jnp.where(i > 0, _glu(ph_ref), 0.0)
        _shifted_copies(xf_ref)
        dyf_ref[0, 0:tm, :] = dy_ref[...]
        dyf_ref[0, tm:, :] = jnp.where(i < nsteps - 1, dyn_ref[...], 0.0)
        _shifted_copies(dyf_ref)
        for c0 in range(0, tm, CONV_ROWS):
            rows = slice(c0, c0 + CONV_ROWS)
            acc = jnp.zeros((CONV_ROWS, CONV_CH), F32)
            for j in range(CONV_WIDTH):
                acc = acc + _shifted_rows(dyf_ref, CONV_WIDTH - 1 - j + c0) * w_ref[j:j + 1, :]
            a_c, s_c = ga[rows, :], sgb[rows, :]
            dglu_ref[rows, 0:CONV_CH] = (acc * s_c).astype(BF16)
            dglu_ref[rows, CONV_CH:] = (acc * a_c * s_c * (1.0 - s_c)).astype(BF16)
        for j in range(CONV_WIDTH):
            part = jnp.zeros((8, CONV_CH), F32)
            for c0 in range(0, tm, CONV_ROWS):
                prod = dy_ref[c0:c0 + CONV_ROWS, :] * _shifted_rows(xf_ref, lead + j + c0)
                part = part + jnp.sum(prod.reshape(CONV_ROWS // 8, 8, CONV_CH), axis=0)
            dw_ref[j:j + 1, :] += jnp.sum(part, axis=0, keepdims=True)

    return pl.pallas_call(
        body, name=name, grid=(nsteps,),
        in_specs=[pl.BlockSpec((tm, EVEN_IN), lambda i: (i, 0)),
                  pl.BlockSpec((CONV_HALO, EVEN_IN), lambda i: (jnp.maximum(i * nh - 1, 0), 0)),
                  pl.BlockSpec((tm, CONV_CH), lambda i: (i, 0)),
                  pl.BlockSpec((CONV_HALO, CONV_CH), lambda i: (jnp.minimum((i + 1) * nh, s // CONV_HALO - 1), 0)),
                  pl.BlockSpec(w.shape, lambda i: (0, 0))],
        out_specs=[pl.BlockSpec((tm, 2 * CONV_CH), lambda i: (i, 0)), pl.BlockSpec(w.shape, lambda i: (0, 0))],
        out_shape=[jax.ShapeDtypeStruct((s, 2 * CONV_CH), BF16), jax.ShapeDtypeStruct(w.shape, F32)],
        scratch_shapes=[pltpu.VMEM((8, tm + CONV_HALO, CONV_CH), F32), pltpu.VMEM((8, tm + CONV_HALO, CONV_CH), F32)],
        compiler_params=_params("arbitrary"),
    )(proj, proj, dy, dy, w)


GATE_Z = slice(1536, 2560)
D_CH = 512
GELU_C = math.sqrt(2.0 / math.pi)
GELU_K = 0.044715


def _gelu_parts(z):
    t = jnp.tanh(GELU_C * (z + GELU_K * z * z * z))
    return 0.5 * z * (1.0 + t), t


def _lane_group(rows):
    return lax.broadcasted_iota(jnp.int32, (rows, D_CH), 1) // HEAD_DIM


def _tril_mask():
    return lax.broadcasted_iota(jnp.int32, (BLOCK, BLOCK), 0) >= lax.broadcasted_iota(jnp.int32, (BLOCK, BLOCK), 1)


def _layer_norm_parts(x):
    mu = jnp.mean(x, axis=-1, keepdims=True)
    xc = x - mu
    rstd = lax.rsqrt(jnp.mean(xc * xc, axis=-1, keepdims=True) + LN_EPS)
    return xc * rstd, rstd


def _gate_fwd(proj, ln_g, ln_b, w_sp, sb_t, name):
    tm = 512

    def body(p_ref, g_ref, b_ref, w_ref, sb_ref, mixed_ref, out_ref):
        zz, _ = _gelu_parts(p_ref[:, GATE_Z].astype(F32))
        u = zz[:, :D_CH]
        xh, _ = _layer_norm_parts(zz[:, D_CH:])
        gn = (xh * g_ref[...] + b_ref[...]).astype(BF16)
        grp = _lane_group(BLOCK)
        tri = _tril_mask()
        ws = [jnp.where(tri, w_ref[gi], 0.0).astype(BF16) for gi in range(N_GROUPS)]
        bias = jnp.zeros((BLOCK, D_CH), F32)
        for gi in range(N_GROUPS):
            bias = jnp.where(grp == gi, sb_ref[:, gi:gi + 1], bias)
        for ch in range(tm // BLOCK):
            rows = slice(ch * BLOCK, (ch + 1) * BLOCK)
            gc = gn[rows, :]
            mixed = bias
            for gi in range(N_GROUPS):
                r = jnp.dot(ws[gi], gc, preferred_element_type=F32)
                mixed = jnp.where(grp == gi, r + bias, mixed)
            mixed_ref[rows, :] = mixed
            out_ref[rows, :] = (u[rows, :] * mixed).astype(BF16)

    return _rows(body, name, tm, [proj], [ln_g, ln_b, w_sp, sb_t], [(D_CH, F32), (D_CH, BF16)])


def _gate_bwd(dmix, proj, mixed, ln_g, ln_b, w_sp, name):
    tm = 512

    def body(d_ref, p_ref, m_ref, g_ref, b_ref, w_ref, dz_ref, dg_ref, db_ref, dw_ref, dsb_ref, dgn_ref):
        @pl.when(_first_step())
        def _():
            dg_ref[...] = jnp.zeros_like(dg_ref)
            db_ref[...] = jnp.zeros_like(db_ref)
            dw_ref[...] = jnp.zeros_like(dw_ref)
            dsb_ref[...] = jnp.zeros_like(dsb_ref)

        z = p_ref[:, GATE_Z].astype(F32)
        zz, t = _gelu_parts(z)
        u = zz[:, :D_CH]
        xh, rstd = _layer_norm_parts(zz[:, D_CH:])
        g = g_ref[...]
        gn = (xh * g + b_ref[...]).astype(BF16)
        dd = d_ref[:, D_CH:]
        du = dd * m_ref[...]
        dm = dd * u
        grp = _lane_group(BLOCK)
        tri = _tril_mask()
        ws = [jnp.where(tri, w_ref[gi], 0.0).astype(BF16) for gi in range(N_GROUPS)]
        gsel = (lax.broadcasted_iota(jnp.int32, (N_GROUPS, D_CH), 1) // HEAD_DIM
                == lax.broadcasted_iota(jnp.int32, (N_GROUPS, D_CH), 0)).astype(F32)
        for ch in range(tm // BLOCK):
            rows = slice(ch * BLOCK, (ch + 1) * BLOCK)
            dmc = dm[rows, :]
            dmb = dmc.astype(BF16)
            gc = gn[rows, :]
            dgn = jnp.zeros((BLOCK, D_CH), F32)
            for gi in range(N_GROUPS):
                r = lax.dot_general(ws[gi], dmb, TN, preferred_element_type=F32)
                dgn = jnp.where(grp == gi, r, dgn)
                dmg = jnp.where(grp == gi, dmb, jnp.zeros_like(dmb))
                dwg = lax.dot_general(dmg, gc, NT, preferred_element_type=F32)
                dw_ref[gi] += jnp.where(tri, dwg, 0.0)
            dsb_ref[...] += lax.dot_general(gsel, dmc, NT, preferred_element_type=F32, precision=lax.Precision.HIGHEST)
            dgn_ref[rows, :] = dgn
        dgn = dgn_ref[...]
        db_ref[...] += jnp.sum(dgn, axis=0, keepdims=True)
        dg_ref[...] += jnp.sum(dgn * xh, axis=0, keepdims=True)
        dxh = dgn * g
        dgp = rstd * (dxh - jnp.mean(dxh, axis=-1, keepdims=True) - xh * jnp.mean(dxh * xh, axis=-1, keepdims=True))
        dgelu = 0.5 * (1.0 + t) + 0.5 * z * (1.0 - t * t) * GELU_C * (1.0 + 3.0 * GELU_K * z * z)
        dz_ref[:, 0:D_CH] = (du * dgelu[:, :D_CH]).astype(BF16)
        dz_ref[:, D_CH:] = (dgp * dgelu[:, D_CH:]).astype(BF16)

    s = proj.shape[0]
    tiled = [dmix, proj, mixed]
    consts = [ln_g, ln_b, w_sp]
    in_specs = [pl.BlockSpec((tm, a.shape[1]), lambda i: (i, 0)) for a in tiled]
    in_specs += [pl.BlockSpec(a.shape, lambda i, nd=a.ndim: (0,) * nd) for a in consts]
    vec = (1, D_CH)
    acc_shapes = [vec, vec, w_sp.shape, (N_GROUPS, BLOCK)]
    return pl.pallas_call(
        body, name=name, grid=(s // tm,), in_specs=in_specs,
        out_specs=[pl.BlockSpec((tm, 2 * D_CH), lambda i: (i, 0))]
        + [pl.BlockSpec(sh, lambda i, nd=len(sh): (0,) * nd) for sh in acc_shapes],
        out_shape=[jax.ShapeDtypeStruct((s, 2 * D_CH), BF16)] + [jax.ShapeDtypeStruct(sh, F32) for sh in acc_shapes],
        scratch_shapes=[pltpu.VMEM((tm, D_CH), F32)],
        compiler_params=_params("arbitrary"),
    )(*tiled, *consts)


def _adam_update(w, g, m, v):
    nm = ADAM_B1 * m + (1.0 - ADAM_B1) * g
    nv = ADAM_B2 * v + (1.0 - ADAM_B2) * (g * g)
    m_hat = nm / (1.0 - ADAM_B1 ** ADAM_STEP)
    v_hat = nv / (1.0 - ADAM_B2 ** ADAM_STEP)
    return -ADAM_LR * (m_hat / (jnp.sqrt(v_hat) + ADAM_EPS) + ADAM_WD * w), nm, nv


def _adamw(w, g, m, v, name):
    rows, cols = w.shape
    tm = _tile(rows, 512, 8)

    def body(w_ref, g_ref, m_ref, v_ref, d_ref, nm_ref, nv_ref):
        d_ref[...], nm_ref[...], nv_ref[...] = _adam_update(w_ref[...], g_ref[...], m_ref[...], v_ref[...])

    return _rows(body, name, tm, [w, g, m, v], [], [(cols, F32)] * 3)


def _ordered_sum(parts, name):
    n, rows, cols = parts.shape
    tm = _tile(rows, 512, 16 if parts.dtype == BF16 else 8)

    def body(p_ref, o_ref):
        acc = p_ref[0].astype(F32)
        for k in range(1, n):
            acc = acc + p_ref[k].astype(F32)
        o_ref[...] = acc

    return pl.pallas_call(body, name=name, grid=(rows // tm,),
                          in_specs=[pl.BlockSpec((n, tm, cols), lambda i: (0, i, 0))],
                          out_specs=pl.BlockSpec((tm, cols), lambda i: (i, 0)),
                          out_shape=jax.ShapeDtypeStruct((rows, cols), F32), compiler_params=_params("parallel"))(parts)


ANY = pl.BlockSpec(memory_space=pl.ANY)


def _position():
    x, y, c = lax.axis_index("x"), lax.axis_index("y"), lax.axis_index("c")
    other_chips = [(1 - x, y), (x, 1 - y), (1 - x, 1 - y)]
    return x, y, c, other_chips


def _remote(src, dst, send_sem, recv_sem, to):
    return pltpu.make_async_remote_copy(src_ref=src, dst_ref=dst, send_sem=send_sem, recv_sem=recv_sem,
                                        device_id=to, device_id_type=MESH)


STAGE_ROWS = 736


def _staged_copies(copies, buf, in_sems, out_sems):
    n = len(copies)

    def into(u):
        src = copies[u][0]
        return pltpu.make_async_copy(src, buf.at[u % 2, pl.ds(0, src.shape[0]), :], in_sems.at[u % 2])

    def out_of(u):
        dst = copies[u][1]
        return pltpu.make_async_copy(buf.at[u % 2, pl.ds(0, dst.shape[0]), :], dst, out_sems.at[u % 2])

    into(0).start()
    for u in range(n):
        into(u).wait()
        out_of(u).start()
        if u + 1 < n:
            if u >= 1:
                out_of(u - 1).wait()
            into(u + 1).start()
    if n >= 2:
        out_of(n - 2).wait()
    out_of(n - 1).wait()


def _stage_scratch(dtype, cols):
    return [pltpu.VMEM((2, STAGE_ROWS, cols), dtype), pltpu.SemaphoreType.DMA((2,)), pltpu.SemaphoreType.DMA((2,))]


def _row_chunks(rows):
    return [(r, min(STAGE_ROWS, rows - r)) for r in range(0, rows, STAGE_ROWS)]


def _gather_chips(shard, name):
    rows, cols = shard.shape
    half = rows // 2

    def body(in_ref, out_ref, send_sems, recv_sems, buf, in_sems, out_sems):
        x, y, c, chips = _position()
        me = 2 * x + y
        sibling = (x, y, 1 - c)

        def slab(chip, h):
            return out_ref.at[chip, pl.ds(h * half, half), :]

        first = [_remote(in_ref.at[pl.ds(c * half, half), :], slab(me, c), send_sems.at[j], recv_sems.at[j], (cx, cy, c))
                 for j, (cx, cy) in enumerate(chips)]
        for cp in first:
            cp.start()
        _staged_copies([(in_ref.at[pl.ds(r, n), :], out_ref.at[me, pl.ds(r, n), :]) for r, n in _row_chunks(rows)],
                       buf, in_sems, out_sems)
        passed = []
        for j, (cx, cy) in enumerate(chips):
            got = slab(2 * cx + cy, c)
            _remote(got, got, send_sems.at[j], recv_sems.at[j], sibling).wait_recv()
            cp = _remote(got, got, send_sems.at[3 + j], recv_sems.at[3 + j], sibling)
            cp.start()
            passed.append(cp)
        for j, (cx, cy) in enumerate(chips):
            got = slab(2 * cx + cy, 1 - c)
            _remote(got, got, send_sems.at[3 + j], recv_sems.at[3 + j], sibling).wait_recv()
        for cp in first + passed:
            cp.wait_send()

    return pl.pallas_call(
        body, name=name, in_specs=[ANY], out_specs=ANY,
        out_shape=jax.ShapeDtypeStruct((N_CHIPS, rows, cols), shard.dtype),
        scratch_shapes=[pltpu.SemaphoreType.DMA((6,)), pltpu.SemaphoreType.DMA((6,))] + _stage_scratch(shard.dtype, cols),
        compiler_params=pltpu.CompilerParams(vmem_limit_bytes=VMEM_LIMIT),
    )(shard)


HBM = pl.BlockSpec(memory_space=pltpu.HBM)
SEM = pl.BlockSpec(memory_space=pltpu.SEMAPHORE)
SIDE_EFFECT = pltpu.SideEffectType.DATAFLOW_SIDE_EFFECTING


def _ici_copies(in_ref, land_ref, send_sems, recv_sems, half):
    x, y, c, chips = _position()
    mine = pl.ds(c * half, half)
    sends = [_remote(in_ref.at[mine, :], land_ref.at[2 * x + y, mine, :], send_sems.at[j], recv_sems.at[j], (cx, cy, c))
             for j, (cx, cy) in enumerate(chips)]
    arrivals = [_remote(in_ref.at[mine, :], land_ref.at[2 * cx + cy, mine, :], send_sems.at[j], recv_sems.at[j], (cx, cy, c))
                for j, (cx, cy) in enumerate(chips)]
    return sends, arrivals


def _gather_start(shard, after, name):
    rows, cols = shard.shape

    def body(in_ref, land_ref, after_ref, send_sems, recv_sems, in_thru, land_thru, token):
        sends, _ = _ici_copies(in_ref, land_ref, send_sems, recv_sems, rows // 2)
        for cp in sends:
            cp.start()
        token[...] = jnp.zeros_like(token)

    land = lax.empty((N_CHIPS, rows, cols), shard.dtype)
    return pl.pallas_call(
        body, name=name,
        out_shape=(pltpu.SemaphoreType.DMA((3,)), pltpu.SemaphoreType.DMA((3,)), pltpu.HBM(shard.shape, shard.dtype),
                   pltpu.HBM(land.shape, land.dtype), jax.ShapeDtypeStruct((8, LANES), F32)),
        in_specs=(HBM, HBM, ANY), out_specs=(SEM, SEM, HBM, HBM, pl.BlockSpec(memory_space=pltpu.VMEM)),
        input_output_aliases={0: 2, 1: 3},
        compiler_params=pltpu.CompilerParams(has_side_effects=SIDE_EFFECT),
    )(pltpu.with_memory_space_constraint(shard, pltpu.HBM), pltpu.with_memory_space_constraint(land, pltpu.HBM), after)


def _gather_wait(send_sems, recv_sems, shard, land, after, name):
    rows = shard.shape[0]

    def body(in_ref, land_ref, send_sems, recv_sems, after_ref, in_out, land_out):
        sends, arrivals = _ici_copies(in_ref, land_ref, send_sems, recv_sems, rows // 2)
        for cp in sends:
            cp.wait_send()
        for cp in arrivals:
            cp.wait_recv()

    return pl.pallas_call(
        body, name=name, out_shape=(pltpu.HBM(shard.shape, shard.dtype), pltpu.HBM(land.shape, land.dtype)),
        in_specs=(HBM, HBM, SEM, SEM, ANY), out_specs=(HBM, HBM), input_output_aliases={0: 0, 1: 1},
        compiler_params=pltpu.CompilerParams(has_side_effects=SIDE_EFFECT),
    )(shard, land, send_sems, recv_sems, after)


def _gather_finish(shard, land, name):
    rows, cols = shard.shape
    half = rows // 2

    def body(in_ref, land_ref, out_ref, send_sems, recv_sems, buf, in_sems, out_sems):
        x, y, c, chips = _position()
        me = 2 * x + y
        sibling = (x, y, 1 - c)

        def slab(chip, h):
            return out_ref.at[chip, pl.ds(h * half, half), :]

        passed = [_remote(slab(2 * cx + cy, c), slab(2 * cx + cy, c), send_sems.at[j], recv_sems.at[j], sibling)
                  for j, (cx, cy) in enumerate(chips)]
        for cp in passed:
            cp.start()
        _staged_copies([(in_ref.at[pl.ds(r, n), :], out_ref.at[me, pl.ds(r, n), :]) for r, n in _row_chunks(rows)],
                       buf, in_sems, out_sems)
        for j, (cx, cy) in enumerate(chips):
            got = slab(2 * cx + cy, 1 - c)
            _remote(got, got, send_sems.at[j], recv_sems.at[j], sibling).wait_recv()
        for cp in passed:
            cp.wait_send()

    return pl.pallas_call(
        body, name=name, in_specs=[ANY, ANY], out_specs=ANY, out_shape=jax.ShapeDtypeStruct(land.shape, land.dtype),
        input_output_aliases={1: 0},
        scratch_shapes=[pltpu.SemaphoreType.DMA((3,)), pltpu.SemaphoreType.DMA((3,))] + _stage_scratch(shard.dtype, cols),
        compiler_params=pltpu.CompilerParams(vmem_limit_bytes=VMEM_LIMIT),
    )(shard, land)


def _gather_devices(block, name):
    rows, cols = block.shape

    def body(in_ref, out_ref, send_sems, recv_sems, local_sem):
        x, y, c, chips = _position()
        sibling = (x, y, 1 - c)

        def slot(px, py, pc):
            return out_ref.at[4 * px + 2 * py + pc]

        mine = pltpu.make_async_copy(in_ref, slot(x, y, c), local_sem)
        mine.start()
        first = [_remote(in_ref, slot(x, y, c), send_sems.at[0], recv_sems.at[0], sibling)]
        first += [_remote(in_ref, slot(x, y, c), send_sems.at[1 + j], recv_sems.at[1 + j], (cx, cy, c))
                  for j, (cx, cy) in enumerate(chips)]
        for cp in first:
            cp.start()
        passed = []
        for j, (cx, cy) in enumerate(chips):
            got = slot(cx, cy, c)
            _remote(got, got, send_sems.at[1 + j], recv_sems.at[1 + j], sibling).wait_recv()
            cp = _remote(got, got, send_sems.at[4 + j], recv_sems.at[4 + j], sibling)
            cp.start()
            passed.append(cp)
        got = slot(x, y, 1 - c)
        _remote(got, got, send_sems.at[0], recv_sems.at[0], sibling).wait_recv()
        for j, (cx, cy) in enumerate(chips):
            got = slot(cx, cy, 1 - c)
            _remote(got, got, send_sems.at[4 + j], recv_sems.at[4 + j], sibling).wait_recv()
        for cp in first + passed:
            cp.wait_send()
        mine.wait()

    return pl.pallas_call(
        body, name=name, in_specs=[ANY], out_specs=ANY,
        out_shape=jax.ShapeDtypeStruct((N_DEV, rows, cols), block.dtype),
        scratch_shapes=[pltpu.SemaphoreType.DMA((7,)), pltpu.SemaphoreType.DMA((7,)), pltpu.SemaphoreType.DMA],
    )(block)


def _pair_send(grads, name):
    n = len(grads)
    hs = [g.shape[2] for g in grads]
    offs = [sum(hs[:i]) for i in range(n)]
    cols = grads[0].shape[3]

    def body(*refs):
        g_refs = refs[:n]
        got_ref, send_sems, recv_sems = refs[n:]
        x, y, c, _ = _position()
        copies = [_remote(g_ref.at[:, 1 - c], got_ref.at[:, pl.ds(offs[i], hs[i]), :], send_sems.at[i], recv_sems.at[i],
                          (x, y, 1 - c)) for i, g_ref in enumerate(g_refs)]
        for cp in copies:
            cp.start()
        for cp in copies:
            cp.wait()

    return pl.pallas_call(
        body, name=name, in_specs=[ANY] * n, out_specs=ANY, out_shape=jax.ShapeDtypeStruct((N_CHIPS, sum(hs), cols), F32),
        scratch_shapes=[pltpu.SemaphoreType.DMA((n,)), pltpu.SemaphoreType.DMA((n,))],
    )(*grads)


def _pair_add(grads, got, name):
    n = len(grads)
    hs = [g.shape[2] for g in grads]
    offs = [sum(hs[:i]) for i in range(n)]
    cols = grads[0].shape[3]
    hmax = max(hs)
    units = [(i, k) for k in range(N_CHIPS) for i in range(n)]

    def body(*refs):
        g_refs = refs[:n]
        got_ref, out_ref, a_buf, b_buf, o_buf, a_sems, b_sems, o_sems = refs[n:]
        c = lax.axis_index("c")

        def loads(u):
            i, k = units[u]
            slot, rows = u % 2, pl.ds(0, hs[i])
            return (pltpu.make_async_copy(g_refs[i].at[k, c], a_buf.at[slot, rows, :], a_sems.at[slot]),
                    pltpu.make_async_copy(got_ref.at[k, pl.ds(offs[i], hs[i]), :], b_buf.at[slot, rows, :], b_sems.at[slot]))

        def store(u):
            i, k = units[u]
            return pltpu.make_async_copy(o_buf.at[u % 2, pl.ds(0, hs[i]), :], out_ref.at[k, pl.ds(offs[i], hs[i]), :],
                                         o_sems.at[u % 2])

        for cp in loads(0):
            cp.start()
        for u, (i, k) in enumerate(units):
            if u + 1 < len(units):
                for cp in loads(u + 1):
                    cp.start()
            for cp in loads(u):
                cp.wait()
            if u >= 2:
                store(u - 2).wait()
            rows = pl.ds(0, hs[i])
            o_buf[u % 2, rows, :] = (a_buf[u % 2, rows, :] + b_buf[u % 2, rows, :]).astype(BF16)
            store(u).start()
        store(len(units) - 2).wait()
        store(len(units) - 1).wait()

    return pl.pallas_call(
        body, name=name, in_specs=[ANY] * (n + 1), out_specs=ANY,
        out_shape=jax.ShapeDtypeStruct((N_CHIPS, sum(hs), cols), BF16),
        scratch_shapes=[pltpu.VMEM((2, hmax, cols), F32), pltpu.VMEM((2, hmax, cols), F32), pltpu.VMEM((2, hmax, cols), BF16),
                        pltpu.SemaphoreType.DMA((2,)), pltpu.SemaphoreType.DMA((2,)), pltpu.SemaphoreType.DMA((2,))],
        compiler_params=pltpu.CompilerParams(vmem_limit_bytes=VMEM_LIMIT),
    )(*grads, got)


def _chip_exchange(parts, name):
    _, rows, cols = parts.shape

    def body(in_ref, out_ref, send_sems, recv_sems):
        x, y, c, chips = _position()
        sent = [_remote(in_ref.at[2 * cx + cy], out_ref.at[j], send_sems.at[j], recv_sems.at[j], (cx, cy, c))
                for j, (cx, cy) in enumerate(chips)]
        for cp in sent:
            cp.start()
        for cp in sent:
            cp.wait()

    return pl.pallas_call(
        body, name=name, in_specs=[ANY], out_specs=ANY, out_shape=jax.ShapeDtypeStruct((3, rows, cols), parts.dtype),
        scratch_shapes=[pltpu.SemaphoreType.DMA((3,)), pltpu.SemaphoreType.DMA((3,))],
    )(parts)


def _exchange_copies(in_ref, land_ref, send_sems, recv_sems):
    x, y, c, chips = _position()
    return [_remote(in_ref.at[2 * cx + cy], land_ref.at[j], send_sems.at[j], recv_sems.at[j], (cx, cy, c))
            for j, (cx, cy) in enumerate(chips)]


def _exchange_start(parts, name):
    _, rows, cols = parts.shape

    def body(in_ref, land_ref, send_sems, recv_sems, in_thru, land_thru, token):
        for cp in _exchange_copies(in_ref, land_ref, send_sems, recv_sems):
            cp.start()
        token[...] = jnp.zeros_like(token)

    land = lax.empty((3, rows, cols), parts.dtype)
    return pl.pallas_call(
        body, name=name,
        out_shape=(pltpu.SemaphoreType.DMA((3,)), pltpu.SemaphoreType.DMA((3,)), pltpu.HBM(parts.shape, parts.dtype),
                   pltpu.HBM(land.shape, land.dtype), jax.ShapeDtypeStruct((8, LANES), F32)),
        in_specs=(HBM, HBM), out_specs=(SEM, SEM, HBM, HBM, pl.BlockSpec(memory_space=pltpu.VMEM)),
        input_output_aliases={0: 2, 1: 3},
        compiler_params=pltpu.CompilerParams(has_side_effects=SIDE_EFFECT),
    )(pltpu.with_memory_space_constraint(parts, pltpu.HBM), pltpu.with_memory_space_constraint(land, pltpu.HBM))


def _exchange_wait(send_sems, recv_sems, parts, land, after, name):
    def body(in_ref, land_ref, send_sems, recv_sems, after_ref, in_out, land_out):
        for cp in _exchange_copies(in_ref, land_ref, send_sems, recv_sems):
            cp.wait_send()
            cp.wait_recv()

    return pl.pallas_call(
        body, name=name, out_shape=(pltpu.HBM(parts.shape, parts.dtype), pltpu.HBM(land.shape, land.dtype)),
        in_specs=(HBM, HBM, SEM, SEM, ANY), out_specs=(HBM, HBM), input_output_aliases={0: 0, 1: 1},
        compiler_params=pltpu.CompilerParams(has_side_effects=SIDE_EFFECT),
    )(parts, land, send_sems, recv_sems, after)


def _chip_sum(parts, recv, chip, name):
    _, rows, cols = parts.shape
    tm = _tile(rows, 512, 16)

    def body(chip_ref, own_ref, recv_ref, o_ref):
        acc = own_ref[0].astype(F32)
        for j in range(3):
            acc = acc + recv_ref[j].astype(F32)
        o_ref[...] = acc

    return pl.pallas_call(
        body, name=name,
        grid_spec=pltpu.PrefetchScalarGridSpec(
            num_scalar_prefetch=1, grid=(rows // tm,),
            in_specs=[pl.BlockSpec((1, tm, cols), lambda i, chip_ref: (chip_ref[0], i, 0)),
                      pl.BlockSpec((3, tm, cols), lambda i, chip_ref: (0, i, 0))],
            out_specs=pl.BlockSpec((tm, cols), lambda i, chip_ref: (i, 0))),
        out_shape=jax.ShapeDtypeStruct((rows, cols), F32), compiler_params=_params("parallel"),
    )(chip, parts, recv)


def _join_unpack(mine, hs, groups, name):
    n = len(hs)
    offs = [sum(hs[:i]) for i in range(n)]
    cols = mine.shape[1]
    n_out = max(groups) + 1
    base = [2 * sum(h for h, g in zip(hs[:i], groups[:i]) if g == groups[i]) for i in range(n)]
    out_rows = [2 * sum(h for h, g in zip(hs, groups) if g == k) for k in range(n_out)]

    def body(in_ref, *refs):
        outs = refs[:n_out]
        send_sems, recv_sems, buf, in_sems, out_sems = refs[n_out:]
        x, y, c, _ = _position()
        sibling = (x, y, 1 - c)
        sent, local = [], []
        for i in range(n):
            src = in_ref.at[pl.ds(offs[i], hs[i]), :]
            here = outs[groups[i]].at[pl.ds(base[i] + c * hs[i], hs[i]), :]
            cp = _remote(src, here, send_sems.at[i], recv_sems.at[i], sibling)
            cp.start()
            sent.append(cp)
            local.append((src, here))
        _staged_copies(local, buf, in_sems, out_sems)
        for i, cp in enumerate(sent):
            there = outs[groups[i]].at[pl.ds(base[i] + (1 - c) * hs[i], hs[i]), :]
            _remote(there, there, send_sems.at[i], recv_sems.at[i], sibling).wait_recv()
            cp.wait_send()

    assert max(hs) <= STAGE_ROWS
    return pl.pallas_call(
        body, name=name, in_specs=[ANY], out_specs=[ANY] * n_out,
        out_shape=[jax.ShapeDtypeStruct((r, cols), F32) for r in out_rows],
        scratch_shapes=[pltpu.SemaphoreType.DMA((n,)), pltpu.SemaphoreType.DMA((n,))] + _stage_scratch(F32, cols),
        compiler_params=pltpu.CompilerParams(vmem_limit_bytes=VMEM_LIMIT),
    )(mine)


SMALL_ROWS = 16
SMALL_PACK_ROWS = 256


def _small_rows(n):
    return -(-n // (SMALL_ROWS * LANES)) * SMALL_ROWS


def _pack_small(arrs):
    parts = []
    for a in arrs:
        flat = a.reshape(-1)
        rows = _small_rows(flat.shape[0])
        flat = jnp.pad(flat, (0, rows * LANES - flat.shape[0]))
        parts.append(flat.reshape(rows, LANES))
    total = sum(p.shape[0] for p in parts)
    parts.append(jnp.zeros((-total % SMALL_PACK_ROWS, LANES), F32))
    return jnp.concatenate(parts, axis=0)


def _unpack_small(packed, shapes):
    out, r = [], 0
    for sh in shapes:
        n = math.prod(sh)
        cnt = _small_rows(n)
        out.append(packed[r:r + cnt].reshape(-1)[:n].reshape(sh))
        r += cnt
    return out


def _ffn_bwd(dh, dhb, h_in, saved, g_norm, w_gate_t, w_up_t, w_down, tag, after=None):
    n, gate, up, act = saved
    dgate, dup = _ffn_dact(dhb, w_down, gate, up, f"{tag}_dact", after)
    dw_down = _matmul(act, dhb, trans_a=True, name=f"{tag}_dwdown")
    dw_gate_t = _matmul(dgate, n, trans_a=True, name=f"{tag}_dwgate")
    dw_up_t = _matmul(dup, n, trans_a=True, name=f"{tag}_dwup")
    dh_in, dh_inb, dg = _dn_norm([(dgate, w_gate_t), (dup, w_up_t)], h_in, g_norm, dh, f"{tag}_dnorm")
    return dh_in, dh_inb, dg, dw_gate_t, dw_up_t, dw_down


def _local_step(x, tgt, w, big, late_weights, reduce_early):
    s = x.shape[0]
    tabs = _rope_tables(s)
    grads, gbig = {}, {}

    g_ev = w['ev_norm_g']
    n1 = _rms_fwd(x, g_ev, "ev_norm")
    proj0 = _matmul(n1, big['ev_w_in', 0], trans_b=True, name="ev_in", out_dtype=BF16)
    q0, k0, v0 = _qkv_prep_even(proj0, tabs, "ev_qkv")
    sinks = w['ev_sinks'].reshape(-1)
    o0, lse0, o0b = _attn_fwd(q0, k0, v0, sinks, max_dist=BLOCK - 1, name="ev_attn", emit_bf16=True)
    yconv, cout = _conv_fwd(proj0, w['ev_conv_w'][0], w['ev_conv_b'], w['ev_conv_ln_g'], w['ev_conv_ln_b'], "ev_conv")
    mix0 = (o0b[0], cout)
    g_f0 = w['ffn_norm_g'][0:1]
    h1, n2 = _matmul_norm(mix0, big['ev_w_out', 0], x, g_f0, "ev_out")
    big = {**big, **late_weights(h1)}

    g_od = w['od_norm_g']
    act0, gate0, up0 = _ffn_gate_up(n2, big['ffn_w_gate', 0], big['ffn_w_up', 0], "ffn0_gate_up")
    h2, n3 = _matmul_norm(act0, big['ffn_w_down', 0], h1, g_od, "ffn0_down")
    ffn0 = (n2, gate0, up0, act0)

    proj1 = _matmul(n3, big['od_w_in', 0], trans_b=True, name="od_in", out_dtype=BF16)
    qkv = _qkv_prep_odd(proj1, tabs, "od_qkv")
    nb = len(DILATIONS)
    outs, lses = [], []
    for i, d in enumerate(DILATIONS):
        o_r, lse_r = _attn_fwd(qkv[i], qkv[nb + i], qkv[2 * nb + i], None, max_dist=BLOCK, name=f"od_attn{d}")
        outs.append(o_r)
        lses.append(lse_r)
    comb = _combine(outs, lses, "od_combine")
    c_bf16 = comb[0]
    c_fold = {1: comb[1]}
    lse_fold = {1: comb[2]}
    for i, d in enumerate(DILATIONS[1:]):
        c_fold[d], lse_fold[d] = comb[3 + 2 * i], comb[4 + 2 * i]
    w_sp = w['od_spatial_w'][0]
    sb_t = w['od_spatial_b'][0].T
    mixed, dout = _gate_fwd(proj1, w['od_sgu_ln_g'], w['od_sgu_ln_b'], w_sp, sb_t, "od_gate")
    mix1 = (c_bf16, dout)
    g_f1 = w['ffn_norm_g'][1:2]
    h3, n4 = _matmul_norm(mix1, big['od_w_out', 0], h2, g_f1, "od_out")
    act1, gate1, up1 = _ffn_gate_up(n4, big['ffn_w_gate', 1], big['ffn_w_up', 1], "ffn1_gate_up")
    ffn1 = (n4, gate1, up1, act1)

    dh4, dh4b, dg_final, loss_tile = _matmul_final(act1, big['ffn_w_down', 1], h3, w['final_norm_g'].reshape(1, D_MODEL),
                                                   tgt, "ffn1_down_loss")
    grads['final_norm_g'] = dg_final.reshape(D_MODEL)

    dh3, dh3b, dg_f1, gbig['ffn_w_gate', 1], gbig['ffn_w_up', 1], gbig['ffn_w_down', 1] = _ffn_bwd(
        dh4, dh4b, h3, ffn1, g_f1, big['ffn_w_gate', 1], big['ffn_w_up', 1], big['ffn_w_down', 1], "ffn1")

    dmix1 = _matmul(dh3b, big['od_w_out', 0], trans_b=True, name="od_dmix")
    gbig['od_w_out', 0] = _matmul_tn_pair(mix1[0], mix1[1], dh3b, "od_dwout")
    do_fold = dict(zip(DILATIONS[1:], _fold_dout(dmix1, "od_fold_dout")))
    do_fold[1] = dmix1[None]
    dqs, dks, dvs = [], [], []
    for i, d in enumerate(DILATIONS):
        dq_r, dk_r, dv_r = _attn_bwd(qkv[i], qkv[nb + i], qkv[2 * nb + i], do_fold[d], c_fold[d], lse_fold[d], None,
                                     max_dist=BLOCK, name=f"od_dattn{d}")
        dqs.append(dq_r)
        dks.append(dk_r)
        dvs.append(dv_r)
    dz, dg_sgu, db_sgu, dw_sp, dsb = _gate_bwd(dmix1, proj1, mixed, w['od_sgu_ln_g'], w['od_sgu_ln_b'], w_sp, "od_dgate")
    grads['od_sgu_ln_g'], grads['od_sgu_ln_b'] = dg_sgu, db_sgu
    grads['od_spatial_w'], grads['od_spatial_b'] = dw_sp[None], dsb[None]
    dproj1 = _qkv_post_odd(dqs, dks, dvs, dz, tabs, "od_dproj")
    gbig['od_w_in', 0] = _matmul(dproj1, n3, trans_a=True, name="od_dwin")
    dh2, dh2b, dg_od = _dn_norm([(dproj1, big['od_w_in', 0])], h2, g_od, dh3, "od_dnorm")
    grads['od_norm_g'] = dg_od
    token = reduce_early(0, gbig)

    dh1, dh1b, dg_f0, gbig['ffn_w_gate', 0], gbig['ffn_w_up', 0], gbig['ffn_w_down', 0] = _ffn_bwd(
        dh2, dh2b, h1, ffn0, g_f0, big['ffn_w_gate', 0], big['ffn_w_up', 0], big['ffn_w_down', 0], "ffn0", token)
    grads['ffn_norm_g'] = jnp.concatenate([dg_f0, dg_f1], axis=0)
    token = reduce_early(1, gbig)

    dmix0 = _matmul(dh1b, big['ev_w_out', 0], trans_b=True, name="ev_dmix", after=token)
    gbig['ev_w_out', 0] = _matmul_tn_pair(mix0[0], mix0[1], dh1b, "ev_dwout")
    dq0, dk0, dv0, dsink = _attn_bwd(q0, k0, v0, dmix0[None], o0, lse0, sinks, max_dist=BLOCK - 1, name="ev_dattn")
    grads['ev_sinks'] = dsink[:, 0, :].reshape(N_PAIRS, 2, HEAD_DIM)[:, :, 0].reshape(1, 8)
    dyc, dg_cln, db_cln, dcb = _conv_tail_bwd(dmix0, yconv, w['ev_conv_ln_g'], w['ev_conv_ln_b'], "ev_dconv_tail")
    grads['ev_conv_ln_g'], grads['ev_conv_ln_b'], grads['ev_conv_b'] = dg_cln, db_cln, dcb
    dglu, dconv_w = _conv_bwd(proj0, dyc, w['ev_conv_w'][0], "ev_dconv")
    grads['ev_conv_w'] = dconv_w[None]
    dproj0 = _qkv_post_even(dq0, dk0, dv0, dglu, tabs, "ev_dproj")
    gbig['ev_w_in', 0] = _matmul(dproj0, n1, trans_a=True, name="ev_dwin")
    dx, _, dg_ev = _dn_norm([(dproj0, big['ev_w_in', 0])], x, g_ev, dh1, "ev_dnorm")
    grads['ev_norm_g'] = dg_ev
    return loss_tile, dx, grads, gbig


def _shard_rows(w, layer, by_cols):
    return w[layer].T if by_cols else w[layer]


def kernel(x, ev_norm_g, ev_w_in, ev_sinks, ev_conv_w, ev_conv_b, ev_conv_ln_g, ev_conv_ln_b, ev_w_out, od_norm_g, od_w_in, od_sgu_ln_g, od_sgu_ln_b, od_spatial_w, od_spatial_b, od_w_out, ffn_norm_g, ffn_w_gate, ffn_w_up, ffn_w_down, final_norm_g, loss_target, m_ev_norm_g, m_ev_w_in, m_ev_sinks, m_ev_conv_w, m_ev_conv_b, m_ev_conv_ln_g, m_ev_conv_ln_b, m_ev_w_out, m_od_norm_g, m_od_w_in, m_od_sgu_ln_g, m_od_sgu_ln_b, m_od_spatial_w, m_od_spatial_b, m_od_w_out, m_ffn_norm_g, m_ffn_w_gate, m_ffn_w_up, m_ffn_w_down, m_final_norm_g, v_ev_norm_g, v_ev_w_in, v_ev_sinks, v_ev_conv_w, v_ev_conv_b, v_ev_conv_ln_g, v_ev_conv_ln_b, v_ev_w_out, v_od_norm_g, v_od_w_in, v_od_sgu_ln_g, v_od_sgu_ln_b, v_od_spatial_w, v_od_spatial_b, v_od_w_out, v_ffn_norm_g, v_ffn_w_gate, v_ffn_w_up, v_ffn_w_down, v_final_norm_g):
    given = dict(locals())
    wts = {n: given[n] for n in WEIGHTS}
    mom = {n: given["m_" + n] for n in WEIGHTS}
    var = {n: given["v_" + n] for n in WEIGHTS}
    chip = 2 * lax.axis_index("x") + lax.axis_index("y")

    shard_rows = [_shard_rows(wts[n], layer, by_cols).astype(BF16) for n, layer, by_cols in BIG]
    counts = [a.shape[0] for a in shard_rows]
    n_first = sum(n.startswith('ev_') for n, _, _ in BIG)

    def unpack(stacked, entries, cnts):
        out, r = {}, 0
        for (n, layer, _), cnt in zip(entries, cnts):
            out[n, layer] = stacked[:, r:r + cnt].reshape(N_CHIPS * cnt, D_MODEL)
            r += cnt
        return out

    first_w = _gather_chips(jnp.concatenate(shard_rows[:n_first], axis=0), "gather_weights_ev")
    big = unpack(first_w, BIG[:n_first], counts[:n_first])
    send_sems, recv_sems, late_shard, late_land, token = _gather_start(jnp.concatenate(shard_rows[n_first:], axis=0),
                                                                      first_w, "gather_weights_start")

    def late_weights(after):
        shard, land = _gather_wait(send_sems, recv_sems, late_shard, late_land, after, "gather_weights_wait")
        return unpack(_gather_finish(shard, land, "gather_weights_finish"), BIG[n_first:], counts[n_first:])

    full = {n: wts[n] for n in SMALL_REPL}
    full['ev_norm_g'] = full['ev_norm_g'] + token[0:1, 0:1]
    small_shards = [wts[n] for n in SMALL_SHARDED]
    small_shapes = [a.shape for a in small_shards]
    all_s = _gather_chips(_pack_small(small_shards), "gather_small_weights")
    per_chip = [_unpack_small(all_s[k], small_shapes) for k in range(N_CHIPS)]
    for i, n in enumerate(SMALL_SHARDED):
        full[n] = jnp.concatenate([per_chip[k][i] for k in range(N_CHIPS)], axis=-1)

    half_rows = {(n, layer): cnt // 2 for (n, layer, _), cnt in zip(BIG, counts)}
    in_flight = []

    def pair_sum(stage, gbig):
        split = [gbig[e].reshape(N_CHIPS, 2, half_rows[e], D_MODEL) for e in GRAD_STAGES[stage]]
        got = _pair_send(split, f"grad_pair_send{stage}")
        return _pair_add(split, got, f"grad_pair_add{stage}")

    def reduce_early(stage, gbig):
        *handles, token = _exchange_start(pair_sum(stage, gbig), f"grad_exchange_start{stage}")
        in_flight.append(handles)
        return token

    loss_tile, grad_x, grads, gbig = _local_step(x[0], loss_target[0], full, big, late_weights, reduce_early)
    loss = lax.psum(loss_tile[0, 0], ("x", "y", "c"))

    reduced = {}
    for stage, entries in enumerate(GRAD_STAGES):
        if stage < len(in_flight):
            chip_part, from_chips = _exchange_wait(*in_flight[stage], grad_x, f"grad_exchange_wait{stage}")
        else:
            chip_part = pair_sum(stage, gbig)
            from_chips = _chip_exchange(chip_part, f"grad_chip_exchange{stage}")
        my_half = _chip_sum(chip_part, from_chips, chip.reshape(1), f"grad_chip_sum{stage}")
        joined = _join_unpack(my_half, [half_rows[e] for e in entries], list(range(len(entries))), f"grad_join_halves{stage}")
        reduced.update(zip(entries, joined))

    small_names = SMALL_REPL + SMALL_SHARDED
    small_full_shapes = [grads[n].shape for n in small_names]
    spack = _pack_small([grads[n] for n in small_names])
    s_all = _gather_devices(spack, "grad_small_gather")
    s_sum = _unpack_small(_ordered_sum(s_all, "grad_small_sum"), small_full_shapes)
    g_all = dict(zip(small_names, s_sum))
    for n in SMALL_SHARDED:
        width = wts[n].shape[-1]
        g_all[n] = lax.dynamic_slice_in_dim(g_all[n], chip * width, width, axis=g_all[n].ndim - 1)

    delta, new_m, new_v = {}, {}, {}
    for n in BIG_NAMES:
        by_cols = [bc for nn, _, bc in BIG if nn == n][0]
        layers = wts[n].shape[0]

        def as_rows(a):
            return (jnp.swapaxes(a, 1, 2) if by_cols else a).reshape(-1, D_MODEL)

        def from_rows(a):
            a = a.reshape(layers, -1, D_MODEL)
            return jnp.swapaxes(a, 1, 2) if by_cols else a

        g_rows = [reduced[n, layer] for layer in range(layers)]
        g_rows = g_rows[0] if layers == 1 else jnp.concatenate(g_rows, axis=0)
        updated = _adamw(as_rows(wts[n]), g_rows, as_rows(mom[n]), as_rows(var[n]), f"adamw_{n}")
        g_all[n] = from_rows(g_rows)
        delta[n], new_m[n], new_v[n] = (from_rows(a) for a in updated)
    shapes = [wts[n].shape for n in small_names]
    d_s, m_s, v_s = _adamw(*[_pack_small([src[n] for n in small_names]) for src in (wts, g_all, mom, var)], "adamw_small")
    for dst, packed in ((delta, d_s), (new_m, m_s), (new_v, v_s)):
        dst.update(zip(small_names, _unpack_small(packed, shapes)))

    return (loss, grad_x[None], *[g_all[n] for n in WEIGHTS], *[delta[n] for n in WEIGHTS],
            *[new_m[n] for n in WEIGHTS], *[new_v[n] for n in WEIGHTS])
```

```python
import math

import jax
import jax.numpy as jnp
from jax import lax
from jax.experimental import pallas as pl
from jax.experimental.pallas import tpu as pltpu

F32 = jnp.float32
BF16 = jnp.bfloat16

D_MODEL = 1024
HEAD_DIM = 64
ROT_DIM = 16
ROPE_THETA = 500000.0
RMS_EPS = 1e-6
LN_EPS = 1e-5
BLOCK = 128
CONV_WIDTH = 31
CONV_HALO = 32
CONV_ROWS = 64
D_FF = 2816
N_GROUPS = 8
ATTN_W = 512
ATTN_SCALE = HEAD_DIM ** -0.5
NEG = -1e30
DILATIONS = (1, 4, 16)

ADAM_LR = 0.001
ADAM_B1 = 0.9
ADAM_B2 = 0.999
ADAM_EPS = 1e-08
ADAM_WD = 0.01
ADAM_STEP = 10

LANES = 128
N_PAIRS = ATTN_W // LANES
VMEM_LIMIT = 56 * 1024 * 1024
MESH = pl.DeviceIdType.MESH
N_CHIPS = 4
N_DEV = 8

WEIGHTS = ['ev_norm_g', 'ev_w_in', 'ev_sinks', 'ev_conv_w', 'ev_conv_b', 'ev_conv_ln_g', 'ev_conv_ln_b', 'ev_w_out',
           'od_norm_g', 'od_w_in', 'od_sgu_ln_g', 'od_sgu_ln_b', 'od_spatial_w', 'od_spatial_b', 'od_w_out',
           'ffn_norm_g', 'ffn_w_gate', 'ffn_w_up', 'ffn_w_down', 'final_norm_g']
BIG = [('ev_w_in', 0, True), ('ev_w_out', 0, False), ('od_w_in', 0, True), ('od_w_out', 0, False),
       ('ffn_w_gate', 0, True), ('ffn_w_gate', 1, True), ('ffn_w_up', 0, True), ('ffn_w_up', 1, True),
       ('ffn_w_down', 0, False), ('ffn_w_down', 1, False)]
BIG_NAMES = ['ev_w_in', 'ev_w_out', 'od_w_in', 'od_w_out', 'ffn_w_gate', 'ffn_w_up', 'ffn_w_down']
GRAD_STAGES = ([('od_w_in', 0), ('od_w_out', 0), ('ffn_w_gate', 1), ('ffn_w_up', 1), ('ffn_w_down', 1)],
               [('ffn_w_gate', 0), ('ffn_w_up', 0), ('ffn_w_down', 0)],
               [('ev_w_in', 0), ('ev_w_out', 0)])
SMALL_SHARDED = ['ev_conv_w', 'od_norm_g', 'od_sgu_ln_g', 'od_sgu_ln_b']
SMALL_REPL = ['ev_norm_g', 'ev_sinks', 'ev_conv_b', 'ev_conv_ln_g', 'ev_conv_ln_b', 'od_spatial_w', 'od_spatial_b',
              'ffn_norm_g', 'final_norm_g']


def _tile(n, cap, mult=LANES):
    best = None
    for t in range(mult, min(n, cap) + 1, mult):
        if n % t == 0:
            best = t
    assert best is not None, (n, cap)
    return best


def _params(*sem):
    return pltpu.CompilerParams(dimension_semantics=sem, vmem_limit_bytes=VMEM_LIMIT)


def _sigmoid(x):
    return 1.0 / (1.0 + jnp.exp(-x))


def _pair_block(p):
    return slice(p * LANES, (p + 1) * LANES)


def _matmul(a, b, *, name, trans_a=False, trans_b=False, add=None, out_dtype=F32, after=None, rows_inner=False):
    parts = a if isinstance(a, (tuple, list)) else (a,)
    if trans_a:
        k, m = parts[0].shape
    else:
        m = parts[0].shape[0]
        k = sum(p.shape[1] for p in parts)
    if trans_b:
        n, k2 = b.shape
    else:
        k2, n = b.shape
    assert k == k2 and b.dtype == BF16 and all(p.dtype == BF16 for p in parts)
    tm = _tile(m, D_FF // 2 if trans_a else 512)
    tn = _tile(n, D_FF // 2)
    tk = k if k <= D_FF else _tile(k, 2048)
    nk = k // tk
    na = len(parts)
    assert na == 1 or (nk == 1 and not trans_a)
    assert nk == 1 or out_dtype == F32
    dims = (((0 if trans_a else 1,), (1 if trans_b else 0,)), ((), ()))
    has_add = add is not None

    def body(*refs):
        a_refs, b_ref = refs[:na], refs[na]
        add_ref = refs[na + 1] if has_add else None
        o_ref = refs[na + 1 + has_add + (after is not None)]
        def product():
            a_val = a_refs[0][...] if na == 1 else jnp.concatenate([r[...] for r in a_refs], axis=1)
            return lax.dot_general(a_val, b_ref[...], dims, preferred_element_type=F32)

        if nk == 1:
            part = product()
            if has_add:
                part = part + add_ref[...]
            o_ref[...] = part.astype(o_ref.dtype)
            return
        kk = pl.program_id(2)

        @pl.when(kk == 0)
        def _():
            o_ref[...] = product() + add_ref[...] if has_add else product()

        @pl.when(kk > 0)
        def _():
            o_ref[...] = product() + o_ref[...]

    def at(f):
        return (lambda j, i, kk: f(i, j, kk)) if rows_inner else f

    if trans_a:
        a_specs = [pl.BlockSpec((tk, tm), at(lambda i, j, kk: (kk, i)))]
    elif na == 1:
        a_specs = [pl.BlockSpec((tm, tk), at(lambda i, j, kk: (i, kk)))]
    else:
        a_specs = [pl.BlockSpec((tm, p.shape[1]), at(lambda i, j, kk: (i, 0))) for p in parts]
    b_spec = (pl.BlockSpec((tn, tk), at(lambda i, j, kk: (j, kk))) if trans_b
              else pl.BlockSpec((tk, tn), at(lambda i, j, kk: (kk, j))))
    o_spec = pl.BlockSpec((tm, tn), at(lambda i, j, kk: (i, j)))
    in_specs = a_specs + [b_spec] + ([o_spec] if has_add else [])
    operands = list(parts) + [b] + ([add] if has_add else [])
    if after is not None:
        in_specs.append(_after_spec(after))
        operands.append(after)
    grid = (n // tn, m // tm, nk) if rows_inner else (m // tm, n // tn, nk)
    return pl.pallas_call(
        body, name=name, grid=grid, in_specs=in_specs, out_specs=o_spec,
        out_shape=jax.ShapeDtypeStruct((m, n), out_dtype),
        compiler_params=_params("parallel", "parallel", "arbitrary"),
    )(*operands)


def _matmul_rows(a, b, add, epilogue, consts, tiled, outs, accs, name):
    parts = a if isinstance(a, (tuple, list)) else (a,)
    m = parts[0].shape[0]
    tm = 512
    na, nc, nt, no = len(parts), len(consts), len(tiled), len(outs)

    def body(*refs):
        a_refs, b_ref, add_ref = refs[:na], refs[na], refs[na + 1]
        const_refs = refs[na + 2:na + 2 + nc]
        tiled_refs = refs[na + 2 + nc:na + 2 + nc + nt]
        out_refs = refs[na + 2 + nc + nt:]
        a_val = a_refs[0][...] if na == 1 else jnp.concatenate([r[...] for r in a_refs], axis=1)
        h = jnp.dot(a_val, b_ref[...], preferred_element_type=F32) + add_ref[...]
        results = epilogue(h, [r[...] for r in const_refs], [r[...] for r in tiled_refs])
        for o_ref, val in zip(out_refs[:no], results[:no]):
            o_ref[...] = val.astype(o_ref.dtype)
        if accs:
            @pl.when(_first_step())
            def _():
                for o_ref in out_refs[no:]:
                    o_ref[...] = jnp.zeros_like(o_ref)

            for o_ref, val in zip(out_refs[no:], results[no:]):
                o_ref[...] += val

    row = lambda w: pl.BlockSpec((tm, w), lambda i: (i, 0))
    whole = lambda shape: pl.BlockSpec(shape, lambda i: (0,) * len(shape))
    return pl.pallas_call(
        body, name=name, grid=(m // tm,),
        in_specs=[row(p.shape[1]) for p in parts] + [whole(b.shape), row(D_MODEL)] + [whole(c.shape) for c in consts]
        + [row(t.shape[1]) for t in tiled],
        out_specs=[row(c) for c, _ in outs] + [whole(sh) for sh, _ in accs],
        out_shape=[jax.ShapeDtypeStruct((m, c), dt) for c, dt in outs] + [jax.ShapeDtypeStruct(sh, dt) for sh, dt in accs],
        compiler_params=_params("arbitrary"),
    )(*parts, b, add, *consts, *tiled)


def _matmul_norm(a, b, add, g, name):
    def epilogue(h, consts, tiled):
        r = lax.rsqrt(jnp.mean(h * h, axis=-1, keepdims=True) + RMS_EPS)
        return [h, h * r * consts[0]]

    return _matmul_rows(a, b, add, epilogue, [g], [], [(D_MODEL, F32), (D_MODEL, BF16)], [], name)


def _matmul_final(a, b, add, g, tgt, name):
    def epilogue(h, consts, tiled):
        gg = consts[0]
        r = lax.rsqrt(jnp.mean(h * h, axis=-1, keepdims=True) + RMS_EPS)
        xh = h * r
        e = xh * gg - tiled[0]
        loss = (0.5 / D_MODEL) * jnp.sum(jnp.sum(e * e, axis=-1, keepdims=True), axis=0, keepdims=True)
        dy = e * (1.0 / D_MODEL)
        dxh = dy * gg
        dx = r * (dxh - xh * jnp.mean(dxh * xh, axis=-1, keepdims=True))
        return [dx, dx, jnp.sum(dy * xh, axis=0, keepdims=True), jnp.broadcast_to(loss, (1, LANES))]

    return _matmul_rows(a, b, add, epilogue, [g], [tgt], [(D_MODEL, F32), (D_MODEL, BF16)],
                        [((1, D_MODEL), F32), ((1, LANES), F32)], name)


def _matmul_tn_pair(a1, a2, b, name):
    kdim, m1 = a1.shape
    m2 = a2.shape[1]
    n = b.shape[1]
    tn = _tile(n, 1024)
    tk = _tile(kdim, 2048)
    nk = kdim // tk
    dims = (((0,), (0,)), ((), ()))

    def body(a1_ref, a2_ref, b_ref, o_ref):
        kk = pl.program_id(1)
        def products():
            bv = b_ref[...]
            return (lax.dot_general(a1_ref[...], bv, dims, preferred_element_type=F32),
                    lax.dot_general(a2_ref[...], bv, dims, preferred_element_type=F32))

        @pl.when(kk == 0)
        def _():
            o_ref[0:m1, :], o_ref[m1:, :] = products()

        @pl.when(kk > 0)
        def _():
            top, bot = products()
            o_ref[0:m1, :] = top + o_ref[0:m1, :]
            o_ref[m1:, :] = bot + o_ref[m1:, :]

    return pl.pallas_call(
        body, name=name, grid=(n // tn, nk),
        in_specs=[pl.BlockSpec((tk, m1), lambda j, kk: (kk, 0)), pl.BlockSpec((tk, m2), lambda j, kk: (kk, 0)),
                  pl.BlockSpec((tk, tn), lambda j, kk: (kk, j))],
        out_specs=pl.BlockSpec((m1 + m2, tn), lambda j, kk: (0, j)),
        out_shape=jax.ShapeDtypeStruct((m1 + m2, n), F32),
        compiler_params=_params("parallel", "arbitrary"),
    )(a1, a2, b)


def _ffn_gate_up(n, w_gate_t, w_up_t, name):
    m, k = n.shape
    f = w_gate_t.shape[0]
    tm, tn = _tile(m, 1024), _tile(f, D_FF // 2)

    def body(n_ref, wg_ref, wu_ref, act_ref, gate_ref, up_ref):
        a = n_ref[...]

        def products(cols):
            return (lax.dot_general(a, wg_ref[cols, :], NT, preferred_element_type=F32),
                    lax.dot_general(a, wu_ref[cols, :], NT, preferred_element_type=F32))

        chunks = _col_chunks(tn)
        ahead = products(chunks[0])
        for idx, cols in enumerate(chunks):
            gate, up = ahead
            if idx + 1 < len(chunks):
                ahead = products(chunks[idx + 1])
            act_ref[:, cols] = (gate * _sigmoid(gate) * up).astype(BF16)
            gate_ref[:, cols] = gate.astype(BF16)
            up_ref[:, cols] = up.astype(BF16)

    wspec = pl.BlockSpec((tn, k), lambda j, i: (j, 0))
    ospec = pl.BlockSpec((tm, tn), lambda j, i: (i, j))
    return pl.pallas_call(
        body, name=name, grid=(f // tn, m // tm), in_specs=[pl.BlockSpec((tm, k), lambda j, i: (i, 0)), wspec, wspec],
        out_specs=[ospec] * 3, out_shape=[jax.ShapeDtypeStruct((m, f), BF16)] * 3,
        compiler_params=_params("parallel", "parallel"),
    )(n, w_gate_t, w_up_t)


def _col_chunks(n, width=384):
    return [slice(c, min(c + width, n)) for c in range(0, n, width)]


def _after_spec(after):
    return pl.BlockSpec(after.shape, lambda *_: (0,) * after.ndim)


def _ffn_dact(dhb, w_down, gate, up, name, after=None):
    m, k = dhb.shape
    f = w_down.shape[0]
    tm, tn = _tile(m, 1024), _tile(f, D_FF // 2)

    def body(d_ref, w_ref, g_ref, u_ref, *rest):
        dg_ref, du_ref = rest[-2:]
        d = d_ref[...]

        def product(cols):
            return lax.dot_general(d, w_ref[cols, :], NT, preferred_element_type=F32)

        chunks = _col_chunks(tn)
        ahead = product(chunks[0])
        for idx, cols in enumerate(chunks):
            dact = ahead
            if idx + 1 < len(chunks):
                ahead = product(chunks[idx + 1])
            g = g_ref[:, cols].astype(F32)
            sg = _sigmoid(g)
            dg_ref[:, cols] = (dact * u_ref[:, cols].astype(F32) * sg * (1.0 + g * (1.0 - sg))).astype(BF16)
            du_ref[:, cols] = (dact * g * sg).astype(BF16)

    ospec = pl.BlockSpec((tm, tn), lambda j, i: (i, j))
    extra = [] if after is None else [after]
    return pl.pallas_call(
        body, name=name, grid=(f // tn, m // tm),
        in_specs=[pl.BlockSpec((tm, k), lambda j, i: (i, 0)), pl.BlockSpec((tn, k), lambda j, i: (j, 0)), ospec, ospec]
        + [_after_spec(a) for a in extra],
        out_specs=[ospec] * 2, out_shape=[jax.ShapeDtypeStruct((m, f), BF16)] * 2,
        compiler_params=_params("parallel", "parallel"),
    )(dhb, w_down, gate, up, *extra)


def _dn_norm(pairs, h, g, dres, name):
    m = h.shape[0]
    tm = 512
    np_ = len(pairs)

    def body(*refs):
        a_refs, b_refs = refs[:np_], refs[np_:2 * np_]
        h_ref, dres_ref, g_ref, dh_ref, dhb_ref, dg_ref = refs[2 * np_:]

        @pl.when(_first_step())
        def _():
            dg_ref[...] = jnp.zeros_like(dg_ref)

        dy = jnp.dot(a_refs[0][...], b_refs[0][...], preferred_element_type=F32)
        for a_ref, b_ref in zip(a_refs[1:], b_refs[1:]):
            dy = jnp.dot(a_ref[...], b_ref[...], preferred_element_type=F32) + dy
        x = h_ref[...]
        r = lax.rsqrt(jnp.mean(x * x, axis=-1, keepdims=True) + RMS_EPS)
        xh = x * r
        dg_ref[...] += jnp.sum(dy * xh, axis=0, keepdims=True)
        dxh = dy * g_ref[...]
        tot = dres_ref[...] + r * (dxh - xh * jnp.mean(dxh * xh, axis=-1, keepdims=True))
        dh_ref[...] = tot
        dhb_ref[...] = tot.astype(BF16)

    row = lambda w: pl.BlockSpec((tm, w), lambda i: (i, 0))
    whole = lambda a: pl.BlockSpec(a.shape, lambda i: (0, 0))
    a_list, b_list = [a for a, _ in pairs], [b for _, b in pairs]
    return pl.pallas_call(
        body, name=name, grid=(m // tm,),
        in_specs=[row(a.shape[1]) for a in a_list] + [whole(b) for b in b_list] + [row(D_MODEL), row(D_MODEL), whole(g)],
        out_specs=[row(D_MODEL), row(D_MODEL), pl.BlockSpec((1, D_MODEL), lambda i: (0, 0))],
        out_shape=[jax.ShapeDtypeStruct((m, D_MODEL), F32), jax.ShapeDtypeStruct((m, D_MODEL), BF16),
                   jax.ShapeDtypeStruct((1, D_MODEL), F32)],
        compiler_params=_params("arbitrary"),
    )(*a_list, *b_list, h, dres, g)


def _rows(body, name, tm, tiled, consts, outs, accs=()):
    s = tiled[0].shape[0]
    assert s % tm == 0
    in_specs = [pl.BlockSpec((tm, a.shape[1]), lambda i: (i, 0)) for a in tiled]
    in_specs += [pl.BlockSpec(a.shape, lambda i, nd=a.ndim: (0,) * nd) for a in consts]
    out_shape = [jax.ShapeDtypeStruct((s, c), dt) for c, dt in outs]
    out_shape += [jax.ShapeDtypeStruct(sh, dt) for sh, dt in accs]
    out_specs = [pl.BlockSpec((tm, c), lambda i: (i, 0)) for c, _ in outs]
    out_specs += [pl.BlockSpec(sh, lambda i, nd=len(sh): (0,) * nd) for sh, _ in accs]
    return pl.pallas_call(
        body, name=name, grid=(s // tm,), in_specs=in_specs, out_specs=out_specs, out_shape=out_shape,
        compiler_params=_params("arbitrary"),
    )(*tiled, *consts)


def _first_step():
    return pl.program_id(0) == 0


def _rms_fwd(h, g, name):
    def body(h_ref, g_ref, n_ref):
        x = h_ref[...]
        r = lax.rsqrt(jnp.mean(x * x, axis=-1, keepdims=True) + RMS_EPS)
        n_ref[...] = (x * r * g_ref[...]).astype(BF16)

    return _rows(body, name, 512, [h], [g], [(D_MODEL, BF16)])[0]


def _rope_tables(s):
    half = ROT_DIM // 2
    inv_freq = ROPE_THETA ** (-jnp.arange(half, dtype=F32) * (2.0 / ROT_DIM))
    ang = jnp.arange(s, dtype=F32)[:, None] * inv_freq[None, :]
    cos, sin = jnp.cos(ang), jnp.sin(ang)
    rest = HEAD_DIM - ROT_DIM
    ones = jnp.ones((s, rest), F32)
    zeros = jnp.zeros((s, rest), F32)
    zh = jnp.zeros((s, half), F32)
    c_t = jnp.concatenate([cos, cos, ones], axis=1)
    a_t = jnp.concatenate([-sin, zh, zeros], axis=1)
    b_t = jnp.concatenate([zh, sin, zeros], axis=1)
    return tuple(jnp.tile(t, (1, LANES // HEAD_DIM)) for t in (c_t, a_t, b_t))


def _rot(x, c, a, b):
    w = x.shape[1]
    half = ROT_DIM // 2
    return x * c + pltpu.roll(x, w - half, 1) * a + pltpu.roll(x, half, 1) * b


def _wide(t, w):
    return t if w == LANES else jnp.tile(t, (1, w // LANES))


def _low_lanes(rows):
    return lax.broadcasted_iota(jnp.int32, (rows, LANES), 1) < HEAD_DIM


def _fold_store(x, sc_ref, out_refs):
    tm = x.shape[0]
    if any(d > 1 for d in out_refs):
        for p in range(N_PAIRS):
            sc_ref[p] = x[:, _pair_block(p)]
    for d, o_ref in out_refs.items():
        if d == 1:
            o_ref[0] = x.astype(o_ref.dtype)
            continue
        for r in range(d):
            for p in range(N_PAIRS):
                o_ref[r, :, _pair_block(p)] = sc_ref[p, pl.ds(r, tm // d, stride=d), :].astype(o_ref.dtype)


def _unfold_load(x_ref, sc_ref, d, add=False):
    n = x_ref.shape[1]
    for r in range(d):
        for p in range(N_PAIRS):
            rows = pl.ds(r, n, stride=d) if d > 1 else slice(None)
            val = x_ref[r, :, _pair_block(p)].astype(F32)
            if add:
                val = val + sc_ref[p, rows, :]
            sc_ref[p, rows, :] = val


def _folded_spec(d, tm, w=ATTN_W):
    return pl.BlockSpec((d, tm // d, w), lambda i: (0, i, 0))


def _folded_shape(s, d, dtype, w=ATTN_W):
    return jax.ShapeDtypeStruct((d, s // d, w), dtype)


def _qkv_prep_even(proj, tabs, name):
    s = proj.shape[0]
    tm = 512

    def body(p_ref, c_ref, a_ref, b_ref, q_ref, k_ref, v_ref):
        c, a, b = c_ref[...], a_ref[...], b_ref[...]
        q_ref[0] = _rot(p_ref[:, 0:ATTN_W].astype(F32), _wide(c, ATTN_W), _wide(a, ATTN_W), _wide(b, ATTN_W)).astype(BF16)
        lo = _low_lanes(tm)
        for src, o_ref in ((_rot(p_ref[:, 512:640].astype(F32), c, a, b), k_ref), (p_ref[:, 640:768].astype(F32), v_ref)):
            swapped = pltpu.roll(src, HEAD_DIM, 1)
            o_ref[0, :, 0:LANES] = jnp.where(lo, src, swapped).astype(BF16)
            o_ref[0, :, LANES:] = jnp.where(lo, swapped, src).astype(BF16)

    row = lambda w: pl.BlockSpec((tm, w), lambda i: (i, 0))
    return pl.pallas_call(
        body, name=name, grid=(s // tm,), in_specs=[row(proj.shape[1]), row(LANES), row(LANES), row(LANES)],
        out_specs=[_folded_spec(1, tm), _folded_spec(1, tm, 2 * LANES), _folded_spec(1, tm, 2 * LANES)],
        out_shape=[_folded_shape(s, 1, BF16), _folded_shape(s, 1, BF16, 2 * LANES), _folded_shape(s, 1, BF16, 2 * LANES)],
        compiler_params=_params("parallel"),
    )(proj, *tabs)


def _qkv_post_even(dq, dk, dv, dglu, tabs, name):
    s = dglu.shape[0]
    tm = 512

    def body(dq_ref, dk_ref, dv_ref, dr_ref, c_ref, a_ref, b_ref, o_ref):
        c, a, b = c_ref[...], -a_ref[...], -b_ref[...]
        o_ref[:, 0:ATTN_W] = _rot(dq_ref[0], _wide(c, ATTN_W), _wide(a, ATTN_W), _wide(b, ATTN_W)).astype(BF16)
        lo = _low_lanes(tm)
        merged = []
        for ref in (dk_ref, dv_ref):
            first, second = ref[0, :, 0:LANES], ref[0, :, LANES:]
            merged.append(jnp.where(lo, first + pltpu.roll(first, HEAD_DIM, 1), second + pltpu.roll(second, HEAD_DIM, 1)))
        o_ref[:, 512:640] = _rot(merged[0], c, a, b).astype(BF16)
        o_ref[:, 640:768] = merged[1].astype(BF16)
        o_ref[:, 768:] = dr_ref[...]

    row = lambda w: pl.BlockSpec((tm, w), lambda i: (i, 0))
    return pl.pallas_call(
        body, name=name, grid=(s // tm,),
        in_specs=[_folded_spec(1, tm), _folded_spec(1, tm, 2 * LANES), _folded_spec(1, tm, 2 * LANES),
                  row(dglu.shape[1]), row(LANES), row(LANES), row(LANES)],
        out_specs=row(EVEN_IN), out_shape=jax.ShapeDtypeStruct((s, EVEN_IN), BF16),
        compiler_params=_params("parallel"),
    )(dq, dk, dv, dglu, *tabs)


def _qkv_prep_odd(proj, tabs, name):
    s = proj.shape[0]
    tm = 512

    def body(p_ref, c_ref, a_ref, b_ref, *rest):
        outs, sc_ref = rest[:-1], rest[-1]
        c, a, b = (_wide(t[...], ATTN_W) for t in (c_ref, a_ref, b_ref))
        for t in range(3):
            x = p_ref[:, t * ATTN_W:(t + 1) * ATTN_W].astype(F32)
            if t < 2:
                x = _rot(x, c, a, b)
            _fold_store(x, sc_ref, {d: outs[t * len(DILATIONS) + i] for i, d in enumerate(DILATIONS)})

    row = lambda w: pl.BlockSpec((tm, w), lambda i: (i, 0))
    return pl.pallas_call(
        body, name=name, grid=(s // tm,), in_specs=[row(proj.shape[1]), row(LANES), row(LANES), row(LANES)],
        out_specs=[_folded_spec(d, tm) for _ in range(3) for d in DILATIONS],
        out_shape=[_folded_shape(s, d, BF16) for _ in range(3) for d in DILATIONS],
        scratch_shapes=[pltpu.VMEM((N_PAIRS, tm, LANES), F32)],
        compiler_params=_params("parallel"),
    )(proj, *tabs)


def _qkv_post_odd(dqs, dks, dvs, dz, tabs, name):
    s = dz.shape[0]
    tm = 256
    nb = len(DILATIONS)

    def body(*refs):
        groups = (refs[:nb], refs[nb:2 * nb], refs[2 * nb:3 * nb])
        dz_ref, c_ref, a_ref, b_ref, o_ref, sc_ref = refs[3 * nb:]
        c, a, b = _wide(c_ref[...], ATTN_W), _wide(-a_ref[...], ATTN_W), _wide(-b_ref[...], ATTN_W)
        for t, group in enumerate(groups):
            for i, d in enumerate(DILATIONS):
                _unfold_load(group[i], sc_ref, d, add=i > 0)
            x = jnp.concatenate([sc_ref[p] for p in range(N_PAIRS)], axis=1)
            if t < 2:
                x = _rot(x, c, a, b)
            o_ref[:, t * ATTN_W:(t + 1) * ATTN_W] = x.astype(BF16)
        o_ref[:, 3 * ATTN_W:] = dz_ref[...]

    row = lambda w: pl.BlockSpec((tm, w), lambda i: (i, 0))
    return pl.pallas_call(
        body, name=name, grid=(s // tm,),
        in_specs=[_folded_spec(d, tm) for _ in range(3) for d in DILATIONS] + [row(dz.shape[1]), row(LANES), row(LANES), row(LANES)],
        out_specs=row(ODD_IN), out_shape=jax.ShapeDtypeStruct((s, ODD_IN), BF16),
        scratch_shapes=[pltpu.VMEM((N_PAIRS, tm, LANES), F32)],
        compiler_params=_params("parallel"),
    )(*dqs, *dks, *dvs, dz, *tabs)


def _fold_dout(dmix, name):
    s = dmix.shape[0]
    tm = 512
    ds = [d for d in DILATIONS if d > 1]

    def body(d_ref, *rest):
        outs, sc_ref = rest[:-1], rest[-1]
        _fold_store(d_ref[...], sc_ref, dict(zip(ds, outs)))

    return pl.pallas_call(
        body, name=name, grid=(s // tm,), in_specs=[pl.BlockSpec((tm, ATTN_W), lambda i: (i, 0))],
        out_specs=[_folded_spec(d, tm) for d in ds], out_shape=[_folded_shape(s, d, BF16) for d in ds],
        scratch_shapes=[pltpu.VMEM((N_PAIRS, tm, LANES), F32)],
        compiler_params=_params("parallel"),
    )(dmix)


def _window(j, i, tq):
    r0 = j * tq + i * BLOCK
    start = pl.multiple_of(jnp.maximum(r0 - BLOCK, 0), BLOCK)
    return pl.ds(start, 2 * BLOCK), r0 - start


def _band_valid(offset, max_dist):
    shape = (2 * BLOCK, 2 * BLOCK)
    dist = (lax.bitwise_and(lax.broadcasted_iota(jnp.int32, shape, 0), BLOCK - 1)
            - lax.broadcasted_iota(jnp.int32, shape, 1) + offset)
    return jnp.abs(2 * dist - max_dist) <= max_dist


def _stack_heads(lo, x):
    zero = jnp.zeros_like(x)
    return jnp.concatenate([jnp.where(lo, x, zero), jnp.where(lo, zero, x)], axis=0)


def _unstack_heads(lo, x):
    return jnp.where(lo, x[:BLOCK], x[BLOCK:])


NT = (((1,), (1,)), ((), ()))
TN = (((0,), (0,)), ((), ()))


def _attn_fwd(q, k, v, sinks, *, max_dist, name, emit_bf16=False):
    d, sp, wq = q.shape
    nq, nk = wq // LANES, k.shape[2] // LANES
    kdiv = nq // nk
    tq = min(sp, 1024)
    nsub = tq // BLOCK
    has_sink = sinks is not None

    def body(*refs):
        refs = list(refs)
        sink_ref = refs.pop(0) if has_sink else None
        q_ref, k_ref, v_ref, o_ref, lse_ref = refs[:5]
        pair = pl.program_id(1)
        j = pl.program_id(2)
        lo = _low_lanes(BLOCK)
        if has_sink:
            first_head = lax.broadcasted_iota(jnp.int32, (2 * BLOCK, 1), 0) < BLOCK
            sk = jnp.where(first_head, sink_ref[2 * pair], sink_ref[2 * pair + 1])
        for i in range(nsub):
            win, offset = _window(j, i, tq)
            rows = slice(i * BLOCK, (i + 1) * BLOCK)
            kw = k_ref[0, win, :]
            vw = v_ref[0, win, :]
            s = lax.dot_general(_stack_heads(lo, q_ref[0, rows, :]), kw, NT, preferred_element_type=F32) * ATTN_SCALE
            s = jnp.where(_band_valid(offset, max_dist), s, NEG)
            m = jnp.max(s, axis=-1, keepdims=True)
            if has_sink:
                m = jnp.maximum(m, sk)
            p = jnp.exp(s - m)
            l = jnp.sum(p, axis=-1, keepdims=True)
            if has_sink:
                l = l + jnp.exp(sk - m)
            o2 = _unstack_heads(lo, jnp.dot(p.astype(BF16), vw, preferred_element_type=F32) / l)
            o_ref[0, rows, :] = o2
            lse_ref[0, rows, :] = _unstack_heads(lo, m + jnp.log(l))
            if emit_bf16:
                refs[5][0, rows, :] = o2.astype(BF16)

    qspec = pl.BlockSpec((1, tq, LANES), lambda r, p, j: (r, j, p))
    kspec = pl.BlockSpec((1, sp, LANES), lambda r, p, j: (r, 0, p // kdiv))
    in_specs = [qspec, kspec, kspec]
    operands = [q, k, v]
    if has_sink:
        in_specs = [pl.BlockSpec(memory_space=pltpu.SMEM)] + in_specs
        operands = [sinks] + operands
    out_shape = [jax.ShapeDtypeStruct(q.shape, F32), jax.ShapeDtypeStruct(q.shape, F32)]
    if emit_bf16:
        out_shape.append(jax.ShapeDtypeStruct(q.shape, BF16))
    return pl.pallas_call(
        body, name=name, grid=(d, nq, sp // tq), in_specs=in_specs, out_specs=[qspec] * len(out_shape),
        out_shape=out_shape, compiler_params=_params("parallel", "parallel", "arbitrary"),
    )(*operands)


def _attn_bwd(q, k, v, do, oo, lse, sinks, *, max_dist, name):
    d, sp, wq = q.shape
    wk = k.shape[2]
    nq, nk = wq // LANES, wk // LANES
    kdiv = nq // nk
    tq = min(sp, 1024)
    nsub = tq // BLOCK
    has_sink = sinks is not None

    def body(*refs):
        refs = list(refs)
        sink_ref = refs.pop(0) if has_sink else None
        q_ref, k_ref, v_ref, do_ref, oo_ref, lse_ref, dq_ref, dk_ref, dv_ref = refs[:9]
        pk, g, j = pl.program_id(1), pl.program_id(2), pl.program_id(3)

        @pl.when((g == 0) & (j == 0))
        def _():
            dk_ref[...] = jnp.zeros_like(dk_ref)
            dv_ref[...] = jnp.zeros_like(dv_ref)

        lo = _low_lanes(BLOCK)
        if has_sink:
            first_head = lax.broadcasted_iota(jnp.int32, (2 * BLOCK, 1), 0) < BLOCK
            pair = pk * kdiv + g
            sk = jnp.where(first_head, sink_ref[2 * pair], sink_ref[2 * pair + 1])
            sink_acc = jnp.zeros((2 * BLOCK, LANES), F32)
        for i in range(nsub):
            win, offset = _window(j, i, tq)
            rows = slice(i * BLOCK, (i + 1) * BLOCK)
            kw = k_ref[0, win, :]
            vw = v_ref[0, win, :]
            do2 = do_ref[0, rows, :].astype(F32)
            qs = _stack_heads(lo, q_ref[0, rows, :])
            dos = _stack_heads(lo, do2.astype(BF16))
            prod = do2 * oo_ref[0, rows, :]
            delta = jnp.sum(_stack_heads(lo, prod), axis=-1, keepdims=True)
            lse2 = lse_ref[0, rows, :]
            lse_swapped = pltpu.roll(lse2, HEAD_DIM, 1)
            lse_st = jnp.concatenate([jnp.where(lo, lse2, lse_swapped), jnp.where(lo, lse_swapped, lse2)], axis=0)
            s = lax.dot_general(qs, kw, NT, preferred_element_type=F32) * ATTN_SCALE
            s = jnp.where(_band_valid(offset, max_dist), s, NEG)
            p = jnp.exp(s - jnp.tile(lse_st, (1, 2)))
            dv_ref[0, win, :] = lax.dot_general(p.astype(BF16), dos, TN, preferred_element_type=F32) + dv_ref[0, win, :]
            dp = lax.dot_general(dos, vw, NT, preferred_element_type=F32)
            ds = (p * (dp - delta) * ATTN_SCALE).astype(BF16)
            dq_ref[0, rows, :] = _unstack_heads(lo, jnp.dot(ds, kw, preferred_element_type=F32))
            dk_ref[0, win, :] = lax.dot_general(ds, qs, TN, preferred_element_type=F32) + dk_ref[0, win, :]
            if has_sink:
                sink_acc = sink_acc - jnp.exp(sk - lse_st) * delta
        if has_sink:
            dsink_ref = refs[9]

            @pl.when(j == 0)
            def _():
                dsink_ref[...] = jnp.zeros_like(dsink_ref)

            dsink_ref[0] += jnp.where(lo[0:1], jnp.sum(sink_acc[:BLOCK], axis=0, keepdims=True),
                                      jnp.sum(sink_acc[BLOCK:], axis=0, keepdims=True))

    def qmap(r, pk, g, j):
        return (r, j, pk * kdiv + g)

    def kmap(r, pk, g, j):
        return (r, 0, pk)

    qspec = pl.BlockSpec((1, tq, LANES), qmap)
    kspec = pl.BlockSpec((1, sp, LANES), kmap)
    in_specs = [qspec, kspec, kspec, qspec, qspec, qspec]
    operands = [q, k, v, do, oo, lse]
    out_specs = [qspec, kspec, kspec]
    out_shape = [jax.ShapeDtypeStruct((d, sp, wq), F32), jax.ShapeDtypeStruct((d, sp, wk), F32),
                 jax.ShapeDtypeStruct((d, sp, wk), F32)]
    if has_sink:
        in_specs = [pl.BlockSpec(memory_space=pltpu.SMEM)] + in_specs
        operands = [sinks] + operands
        out_specs.append(pl.BlockSpec((1, 1, LANES), lambda r, pk, g, j: (pk * kdiv + g, 0, 0)))
        out_shape.append(jax.ShapeDtypeStruct((nq, 1, LANES), F32))
    return pl.pallas_call(
        body, name=name, grid=(d, nk, kdiv, sp // tq), in_specs=in_specs, out_specs=out_specs, out_shape=out_shape,
        compiler_params=_params("parallel", "parallel", "arbitrary", "arbitrary"),
    )(*operands)


def _combine(outs, lses, name):
    s = outs[0].shape[1]
    tm = 512
    nb = len(DILATIONS)
    ds = [d for d in DILATIONS if d > 1]

    def body(*refs):
        o_refs, l_refs = refs[:nb], refs[nb:2 * nb]
        cb_ref, c_ref, lse_ref = refs[2 * nb:2 * nb + 3]
        folded = refs[2 * nb + 3:2 * nb + 3 + 2 * len(ds)]
        scratch = refs[2 * nb + 3 + 2 * len(ds):]
        so = {1: None}
        sl = {1: None}
        for i, d in enumerate(ds):
            so[d], sl[d] = scratch[2 * i], scratch[2 * i + 1]
            _unfold_load(o_refs[1 + i], so[d], d)
            _unfold_load(l_refs[1 + i], sl[d], d)
        for p in range(N_PAIRS):
            pb = _pair_block(p)
            ls = [l_refs[0][0, :, pb]] + [sl[d][p] for d in ds]
            os_ = [o_refs[0][0, :, pb]] + [so[d][p] for d in ds]
            m = ls[0]
            for t in ls[1:]:
                m = jnp.maximum(m, t)
            ws = [jnp.exp(t - m) for t in ls]
            tot = ws[0]
            for t in ws[1:]:
                tot = tot + t
            acc = ws[0] * os_[0]
            for w, o in zip(ws[1:], os_[1:]):
                acc = acc + w * o
            cmix = acc / tot
            lse = m + jnp.log(tot)
            cb_ref[:, pb] = cmix.astype(BF16)
            c_ref[0, :, pb] = cmix
            lse_ref[0, :, pb] = lse
            so[ds[0]][p] = cmix
            sl[ds[0]][p] = lse
        for i, d in enumerate(ds):
            for r in range(d):
                for p in range(N_PAIRS):
                    rows = pl.ds(r, tm // d, stride=d)
                    folded[2 * i][r, :, _pair_block(p)] = so[ds[0]][p, rows, :]
                    folded[2 * i + 1][r, :, _pair_block(p)] = sl[ds[0]][p, rows, :]

    in_specs = [_folded_spec(d, tm) for _ in range(2) for d in DILATIONS]
    out_specs = [pl.BlockSpec((tm, ATTN_W), lambda i: (i, 0)), _folded_spec(1, tm), _folded_spec(1, tm)]
    out_shape = [jax.ShapeDtypeStruct((s, ATTN_W), BF16), _folded_shape(s, 1, F32), _folded_shape(s, 1, F32)]
    for d in ds:
        out_specs += [_folded_spec(d, tm)] * 2
        out_shape += [_folded_shape(s, d, F32)] * 2
    return pl.pallas_call(
        body, name=name, grid=(s // tm,), in_specs=in_specs, out_specs=out_specs, out_shape=out_shape,
        scratch_shapes=[pltpu.VMEM((N_PAIRS, tm, LANES), F32)] * (2 * len(ds)),
        compiler_params=_params("parallel"),
    )(*outs, *lses)


GLU_A = slice(768, 1280)
GLU_B = slice(1280, 1792)
EVEN_IN = 1792
ODD_IN = 2560
CONV_CH = 512


def _shifted_copies(xs_ref):
    rows = xs_ref.shape[1] - 8
    for b in range(1, 8):
        xs_ref[b, 0:rows, :] = xs_ref[0, pl.ds(b, rows), :]


def _shifted_rows(xs_ref, start):
    return xs_ref[start % 8, pl.ds(start - start % 8, CONV_ROWS), :]


def _glu(p_ref):
    return p_ref[:, GLU_A].astype(F32) * _sigmoid(p_ref[:, GLU_B].astype(F32))


def _conv_fwd(proj, w, b, ln_g, ln_b, name):
    s = proj.shape[0]
    tm = 512
    nh = tm // CONV_HALO
    lead = CONV_HALO - (CONV_WIDTH - 1)

    def body(p_ref, ph_ref, w_ref, b_ref, g_ref, bb_ref, y_ref, o_ref, xs_ref):
        xs_ref[0, CONV_HALO:, :] = _glu(p_ref)
        xs_ref[0, 0:CONV_HALO, :] = jnp.where(pl.program_id(0) > 0, _glu(ph_ref), 0.0)
        _shifted_copies(xs_ref)
        for c0 in range(0, tm, CONV_ROWS):
            acc = jnp.zeros((CONV_ROWS, CONV_CH), F32) + b_ref[...]
            for j in range(CONV_WIDTH):
                acc = acc + _shifted_rows(xs_ref, lead + j + c0) * w_ref[j:j + 1, :]
            y_ref[c0:c0 + CONV_ROWS, :] = acc
            mu = jnp.mean(acc, axis=-1, keepdims=True)
            xc = acc - mu
            var = jnp.mean(xc * xc, axis=-1, keepdims=True)
            zz = xc * lax.rsqrt(var + LN_EPS) * g_ref[...] + bb_ref[...]
            o_ref[c0:c0 + CONV_ROWS, :] = (zz * _sigmoid(zz)).astype(BF16)

    def const(a):
        return pl.BlockSpec(a.shape, lambda i: (0, 0))

    return pl.pallas_call(
        body, name=name, grid=(s // tm,),
        in_specs=[pl.BlockSpec((tm, EVEN_IN), lambda i: (i, 0)),
                  pl.BlockSpec((CONV_HALO, EVEN_IN), lambda i: (jnp.maximum(i * nh - 1, 0), 0)),
                  const(w), const(b), const(ln_g), const(ln_b)],
        out_specs=[pl.BlockSpec((tm, CONV_CH), lambda i: (i, 0)), pl.BlockSpec((tm, CONV_CH), lambda i: (i, 0))],
        out_shape=[jax.ShapeDtypeStruct((s, CONV_CH), F32), jax.ShapeDtypeStruct((s, CONV_CH), BF16)],
        scratch_shapes=[pltpu.VMEM((8, tm + CONV_HALO, CONV_CH), F32)],
        compiler_params=_params("arbitrary"),
    )(proj, proj, w, b, ln_g, ln_b)


def _conv_tail_bwd(dmix, yconv, ln_g, ln_b, name):
    def body(d_ref, y_ref, g_ref, b_ref, dy_ref, dg_ref, db_ref, dcb_ref):
        @pl.when(_first_step())
        def _():
            dg_ref[...] = jnp.zeros_like(dg_ref)
            db_ref[...] = jnp.zeros_like(db_ref)
            dcb_ref[...] = jnp.zeros_like(dcb_ref)

        y = y_ref[...]
        g = g_ref[...]
        mu = jnp.mean(y, axis=-1, keepdims=True)
        xc = y - mu
        rstd = lax.rsqrt(jnp.mean(xc * xc, axis=-1, keepdims=True) + LN_EPS)
        xh = xc * rstd
        zz = xh * g + b_ref[...]
        sg = _sigmoid(zz)
        dzz = d_ref[:, CONV_CH:] * sg * (1.0 + zz * (1.0 - sg))
        dg_ref[...] += jnp.sum(dzz * xh, axis=0, keepdims=True)
        db_ref[...] += jnp.sum(dzz, axis=0, keepdims=True)
        dxh = dzz * g
        dy = rstd * (dxh - jnp.mean(dxh, axis=-1, keepdims=True) - xh * jnp.mean(dxh * xh, axis=-1, keepdims=True))
        dcb_ref[...] += jnp.sum(dy, axis=0, keepdims=True)
        dy_ref[...] = dy

    vec = ((1, CONV_CH), F32)
    return _rows(body, name, 512, [dmix, yconv], [ln_g, ln_b], [(CONV_CH, F32)], [vec, vec, vec])


def _conv_bwd(proj, dy, w, name):
    s = proj.shape[0]
    tm = 512
    nh = tm // CONV_HALO
    nsteps = s // tm
    lead = CONV_HALO - (CONV_WIDTH - 1)

    def body(p_ref, ph_ref, dy_ref, dyn_ref, w_ref, dglu_ref, dw_ref, xf_ref, dyf_ref):
        i = pl.program_id(0)

        @pl.when(i == 0)
        def _():
            dw_ref[...] = jnp.zeros_like(dw_ref)

        ga = p_ref[:, GLU_A].astype(F32)
        sgb = _sigmoid(p_ref[:, GLU_B].astype(F32))
        xf_ref[0, CONV_HALO:, :] = ga * sgb
        xf_ref[0, 0:CONV_HALO, :] = jnp.where(i > 0, _glu(ph_ref), 0.0)
        _shifted_copies(xf_ref)
        dyf_ref[0, 0:tm, :] = dy_ref[...]
        dyf_ref[0, tm:, :] = jnp.where(i < nsteps - 1, dyn_ref[...], 0.0)
        _shifted_copies(dyf_ref)
        for c0 in range(0, tm, CONV_ROWS):
            rows = slice(c0, c0 + CONV_ROWS)
            acc = jnp.zeros((CONV_ROWS, CONV_CH), F32)
            for j in range(CONV_WIDTH):
                acc = acc + _shifted_rows(dyf_ref, CONV_WIDTH - 1 - j + c0) * w_ref[j:j + 1, :]
            a_c, s_c = ga[rows, :], sgb[rows, :]
            dglu_ref[rows, 0:CONV_CH] = (acc * s_c).astype(BF16)
            dglu_ref[rows, CONV_CH:] = (acc * a_c * s_c * (1.0 - s_c)).astype(BF16)
        for j in range(CONV_WIDTH):
            part = jnp.zeros((8, CONV_CH), F32)
            for c0 in range(0, tm, CONV_ROWS):
                prod = dy_ref[c0:c0 + CONV_ROWS, :] * _shifted_rows(xf_ref, lead + j + c0)
                part = part + jnp.sum(prod.reshape(CONV_ROWS // 8, 8, CONV_CH), axis=0)
            dw_ref[j:j + 1, :] += jnp.sum(part, axis=0, keepdims=True)

    return pl.pallas_call(
        body, name=name, grid=(nsteps,),
        in_specs=[pl.BlockSpec((tm, EVEN_IN), lambda i: (i, 0)),
                  pl.BlockSpec((CONV_HALO, EVEN_IN), lambda i: (jnp.maximum(i * nh - 1, 0), 0)),
                  pl.BlockSpec((tm, CONV_CH), lambda i: (i, 0)),
                  pl.BlockSpec((CONV_HALO, CONV_CH), lambda i: (jnp.minimum((i + 1) * nh, s // CONV_HALO - 1), 0)),
                  pl.BlockSpec(w.shape, lambda i: (0, 0))],
        out_specs=[pl.BlockSpec((tm, 2 * CONV_CH), lambda i: (i, 0)), pl.BlockSpec(w.shape, lambda i: (0, 0))],
        out_shape=[jax.ShapeDtypeStruct((s, 2 * CONV_CH), BF16), jax.ShapeDtypeStruct(w.shape, F32)],
        scratch_shapes=[pltpu.VMEM((8, tm + CONV_HALO, CONV_CH), F32), pltpu.VMEM((8, tm + CONV_HALO, CONV_CH), F32)],
        compiler_params=_params("arbitrary"),
    )(proj, proj, dy, dy, w)


GATE_Z = slice(1536, 2560)
D_CH = 512
GELU_C = math.sqrt(2.0 / math.pi)
GELU_K = 0.044715


def _gelu_parts(z):
    t = jnp.tanh(GELU_C * (z + GELU_K * z * z * z))
    return 0.5 * z * (1.0 + t), t


def _lane_group(rows):
    return lax.broadcasted_iota(jnp.int32, (rows, D_CH), 1) // HEAD_DIM


def _tril_mask():
    return lax.broadcasted_iota(jnp.int32, (BLOCK, BLOCK), 0) >= lax.broadcasted_iota(jnp.int32, (BLOCK, BLOCK), 1)


def _layer_norm_parts(x):
    mu = jnp.mean(x, axis=-1, keepdims=True)
    xc = x - mu
    rstd = lax.rsqrt(jnp.mean(xc * xc, axis=-1, keepdims=True) + LN_EPS)
    return xc * rstd, rstd


def _gate_fwd(proj, ln_g, ln_b, w_sp, sb_t, name):
    tm = 512

    def body(p_ref, g_ref, b_ref, w_ref, sb_ref, mixed_ref, out_ref):
        zz, _ = _gelu_parts(p_ref[:, GATE_Z].astype(F32))
        u = zz[:, :D_CH]
        xh, _ = _layer_norm_parts(zz[:, D_CH:])
        gn = (xh * g_ref[...] + b_ref[...]).astype(BF16)
        grp = _lane_group(BLOCK)
        tri = _tril_mask()
        ws = [jnp.where(tri, w_ref[gi], 0.0).astype(BF16) for gi in range(N_GROUPS)]
        bias = jnp.zeros((BLOCK, D_CH), F32)
        for gi in range(N_GROUPS):
            bias = jnp.where(grp == gi, sb_ref[:, gi:gi + 1], bias)
        for ch in range(tm // BLOCK):
            rows = slice(ch * BLOCK, (ch + 1) * BLOCK)
            gc = gn[rows, :]
            mixed = bias
            for gi in range(N_GROUPS):
                r = jnp.dot(ws[gi], gc, preferred_element_type=F32)
                mixed = jnp.where(grp == gi, r + bias, mixed)
            mixed_ref[rows, :] = mixed
            out_ref[rows, :] = (u[rows, :] * mixed).astype(BF16)

    return _rows(body, name, tm, [proj], [ln_g, ln_b, w_sp, sb_t], [(D_CH, F32), (D_CH, BF16)])


def _gate_bwd(dmix, proj, mixed, ln_g, ln_b, w_sp, name):
    tm = 512

    def body(d_ref, p_ref, m_ref, g_ref, b_ref, w_ref, dz_ref, dg_ref, db_ref, dw_ref, dsb_ref, dgn_ref):
        @pl.when(_first_step())
        def _():
            dg_ref[...] = jnp.zeros_like(dg_ref)
            db_ref[...] = jnp.zeros_like(db_ref)
            dw_ref[...] = jnp.zeros_like(dw_ref)
            dsb_ref[...] = jnp.zeros_like(dsb_ref)

        z = p_ref[:, GATE_Z].astype(F32)
        zz, t = _gelu_parts(z)
        u = zz[:, :D_CH]
        xh, rstd = _layer_norm_parts(zz[:, D_CH:])
        g = g_ref[...]
        gn = (xh * g + b_ref[...]).astype(BF16)
        dd = d_ref[:, D_CH:]
        du = dd * m_ref[...]
        dm = dd * u
        grp = _lane_group(BLOCK)
        tri = _tril_mask()
        ws = [jnp.where(tri, w_ref[gi], 0.0).astype(BF16) for gi in range(N_GROUPS)]
        gsel = (lax.broadcasted_iota(jnp.int32, (N_GROUPS, D_CH), 1) // HEAD_DIM
                == lax.broadcasted_iota(jnp.int32, (N_GROUPS, D_CH), 0)).astype(F32)
        for ch in range(tm // BLOCK):
            rows = slice(ch * BLOCK, (ch + 1) * BLOCK)
            dmc = dm[rows, :]
            dmb = dmc.astype(BF16)
            gc = gn[rows, :]
            dgn = jnp.zeros((BLOCK, D_CH), F32)
            for gi in range(N_GROUPS):
                r = lax.dot_general(ws[gi], dmb, TN, preferred_element_type=F32)
                dgn = jnp.where(grp == gi, r, dgn)
                dmg = jnp.where(grp == gi, dmb, jnp.zeros_like(dmb))
                dwg = lax.dot_general(dmg, gc, NT, preferred_element_type=F32)
                dw_ref[gi] += jnp.where(tri, dwg, 0.0)
            dsb_ref[...] += lax.dot_general(gsel, dmc, NT, preferred_element_type=F32, precision=lax.Precision.HIGHEST)
            dgn_ref[rows, :] = dgn
        dgn = dgn_ref[...]
        db_ref[...] += jnp.sum(dgn, axis=0, keepdims=True)
        dg_ref[...] += jnp.sum(dgn * xh, axis=0, keepdims=True)
        dxh = dgn * g
        dgp = rstd * (dxh - jnp.mean(dxh, axis=-1, keepdims=True) - xh * jnp.mean(dxh * xh, axis=-1, keepdims=True))
        dgelu = 0.5 * (1.0 + t) + 0.5 * z * (1.0 - t * t) * GELU_C * (1.0 + 3.0 * GELU_K * z * z)
        dz_ref[:, 0:D_CH] = (du * dgelu[:, :D_CH]).astype(BF16)
        dz_ref[:, D_CH:] = (dgp * dgelu[:, D_CH:]).astype(BF16)

    s = proj.shape[0]
    tiled = [dmix, proj, mixed]
    consts = [ln_g, ln_b, w_sp]
    in_specs = [pl.BlockSpec((tm, a.shape[1]), lambda i: (i, 0)) for a in tiled]
    in_specs += [pl.BlockSpec(a.shape, lambda i, nd=a.ndim: (0,) * nd) for a in consts]
    vec = (1, D_CH)
    acc_shapes = [vec, vec, w_sp.shape, (N_GROUPS, BLOCK)]
    return pl.pallas_call(
        body, name=name, grid=(s // tm,), in_specs=in_specs,
        out_specs=[pl.BlockSpec((tm, 2 * D_CH), lambda i: (i, 0))]
        + [pl.BlockSpec(sh, lambda i, nd=len(sh): (0,) * nd) for sh in acc_shapes],
        out_shape=[jax.ShapeDtypeStruct((s, 2 * D_CH), BF16)] + [jax.ShapeDtypeStruct(sh, F32) for sh in acc_shapes],
        scratch_shapes=[pltpu.VMEM((tm, D_CH), F32)],
        compiler_params=_params("arbitrary"),
    )(*tiled, *consts)


def _adam_update(w, g, m, v):
    nm = ADAM_B1 * m + (1.0 - ADAM_B1) * g
    nv = ADAM_B2 * v + (1.0 - ADAM_B2) * (g * g)
    m_hat = nm / (1.0 - ADAM_B1 ** ADAM_STEP)
    v_hat = nv / (1.0 - ADAM_B2 ** ADAM_STEP)
    return -ADAM_LR * (m_hat / (jnp.sqrt(v_hat) + ADAM_EPS) + ADAM_WD * w), nm, nv


def _adamw(w, g, m, v, name):
    rows, cols = w.shape
    tm = _tile(rows, 512, 8)

    def body(w_ref, g_ref, m_ref, v_ref, d_ref, nm_ref, nv_ref):
        d_ref[...], nm_ref[...], nv_ref[...] = _adam_update(w_ref[...], g_ref[...], m_ref[...], v_ref[...])

    return _rows(body, name, tm, [w, g, m, v], [], [(cols, F32)] * 3)


def _ordered_sum(parts, name):
    n, rows, cols = parts.shape
    tm = _tile(rows, 512, 16 if parts.dtype == BF16 else 8)

    def body(p_ref, o_ref):
        acc = p_ref[0].astype(F32)
        for k in range(1, n):
            acc = acc + p_ref[k].astype(F32)
        o_ref[...] = acc

    return pl.pallas_call(body, name=name, grid=(rows // tm,),
                          in_specs=[pl.BlockSpec((n, tm, cols), lambda i: (0, i, 0))],
                          out_specs=pl.BlockSpec((tm, cols), lambda i: (i, 0)),
                          out_shape=jax.ShapeDtypeStruct((rows, cols), F32), compiler_params=_params("parallel"))(parts)


ANY = pl.BlockSpec(memory_space=pl.ANY)


def _position():
    x, y, c = lax.axis_index("x"), lax.axis_index("y"), lax.axis_index("c")
    other_chips = [(1 - x, y), (x, 1 - y), (1 - x, 1 - y)]
    return x, y, c, other_chips


def _remote(src, dst, send_sem, recv_sem, to):
    return pltpu.make_async_remote_copy(src_ref=src, dst_ref=dst, send_sem=send_sem, recv_sem=recv_sem,
                                        device_id=to, device_id_type=MESH)


STAGE_ROWS = 736


def _staged_copies(copies, buf, in_sems, out_sems):
    n = len(copies)

    def into(u):
        src = copies[u][0]
        return pltpu.make_async_copy(src, buf.at[u % 2, pl.ds(0, src.shape[0]), :], in_sems.at[u % 2])

    def out_of(u):
        dst = copies[u][1]
        return pltpu.make_async_copy(buf.at[u % 2, pl.ds(0, dst.shape[0]), :], dst, out_sems.at[u % 2])

    into(0).start()
    for u in range(n):
        into(u).wait()
        out_of(u).start()
        if u + 1 < n:
            if u >= 1:
                out_of(u - 1).wait()
            into(u + 1).start()
    if n >= 2:
        out_of(n - 2).wait()
    out_of(n - 1).wait()


def _stage_scratch(dtype, cols):
    return [pltpu.VMEM((2, STAGE_ROWS, cols), dtype), pltpu.SemaphoreType.DMA((2,)), pltpu.SemaphoreType.DMA((2,))]


def _row_chunks(rows):
    return [(r, min(STAGE_ROWS, rows - r)) for r in range(0, rows, STAGE_ROWS)]


def _gather_chips(shard, name):
    rows, cols = shard.shape
    half = rows // 2

    def body(in_ref, out_ref, send_sems, recv_sems, buf, in_sems, out_sems):
        x, y, c, chips = _position()
        me = 2 * x + y
        sibling = (x, y, 1 - c)

        def slab(chip, h):
            return out_ref.at[chip, pl.ds(h * half, half), :]

        first = [_remote(in_ref.at[pl.ds(c * half, half), :], slab(me, c), send_sems.at[j], recv_sems.at[j], (cx, cy, c))
                 for j, (cx, cy) in enumerate(chips)]
        for cp in first:
            cp.start()
        _staged_copies([(in_ref.at[pl.ds(r, n), :], out_ref.at[me, pl.ds(r, n), :]) for r, n in _row_chunks(rows)],
                       buf, in_sems, out_sems)
        passed = []
        for j, (cx, cy) in enumerate(chips):
            got = slab(2 * cx + cy, c)
            _remote(got, got, send_sems.at[j], recv_sems.at[j], sibling).wait_recv()
            cp = _remote(got, got, send_sems.at[3 + j], recv_sems.at[3 + j], sibling)
            cp.start()
            passed.append(cp)
        for j, (cx, cy) in enumerate(chips):
            got = slab(2 * cx + cy, 1 - c)
            _remote(got, got, send_sems.at[3 + j], recv_sems.at[3 + j], sibling).wait_recv()
        for cp in first + passed:
            cp.wait_send()

    return pl.pallas_call(
        body, name=name, in_specs=[ANY], out_specs=ANY,
        out_shape=jax.ShapeDtypeStruct((N_CHIPS, rows, cols), shard.dtype),
        scratch_shapes=[pltpu.SemaphoreType.DMA((6,)), pltpu.SemaphoreType.DMA((6,))] + _stage_scratch(shard.dtype, cols),
        compiler_params=pltpu.CompilerParams(vmem_limit_bytes=VMEM_LIMIT),
    )(shard)


HBM = pl.BlockSpec(memory_space=pltpu.HBM)
SEM = pl.BlockSpec(memory_space=pltpu.SEMAPHORE)
SIDE_EFFECT = pltpu.SideEffectType.DATAFLOW_SIDE_EFFECTING


def _ici_copies(in_ref, land_ref, send_sems, recv_sems, half):
    x, y, c, chips = _position()
    mine = pl.ds(c * half, half)
    sends = [_remote(in_ref.at[mine, :], land_ref.at[2 * x + y, mine, :], send_sems.at[j], recv_sems.at[j], (cx, cy, c))
             for j, (cx, cy) in enumerate(chips)]
    arrivals = [_remote(in_ref.at[mine, :], land_ref.at[2 * cx + cy, mine, :], send_sems.at[j], recv_sems.at[j], (cx, cy, c))
                for j, (cx, cy) in enumerate(chips)]
    return sends, arrivals


def _gather_start(shard, after, name):
    rows, cols = shard.shape

    def body(in_ref, land_ref, after_ref, send_sems, recv_sems, in_thru, land_thru, token):
        sends, _ = _ici_copies(in_ref, land_ref, send_sems, recv_sems, rows // 2)
        for cp in sends:
            cp.start()
        token[...] = jnp.zeros_like(token)

    land = lax.empty((N_CHIPS, rows, cols), shard.dtype)
    return pl.pallas_call(
        body, name=name,
        out_shape=(pltpu.SemaphoreType.DMA((3,)), pltpu.SemaphoreType.DMA((3,)), pltpu.HBM(shard.shape, shard.dtype),
                   pltpu.HBM(land.shape, land.dtype), jax.ShapeDtypeStruct((8, LANES), F32)),
        in_specs=(HBM, HBM, ANY), out_specs=(SEM, SEM, HBM, HBM, pl.BlockSpec(memory_space=pltpu.VMEM)),
        input_output_aliases={0: 2, 1: 3},
        compiler_params=pltpu.CompilerParams(has_side_effects=SIDE_EFFECT),
    )(pltpu.with_memory_space_constraint(shard, pltpu.HBM), pltpu.with_memory_space_constraint(land, pltpu.HBM), after)


def _gather_wait(send_sems, recv_sems, shard, land, after, name):
    rows = shard.shape[0]

    def body(in_ref, land_ref, send_sems, recv_sems, after_ref, in_out, land_out):
        sends, arrivals = _ici_copies(in_ref, land_ref, send_sems, recv_sems, rows // 2)
        for cp in sends:
            cp.wait_send()
        for cp in arrivals:
            cp.wait_recv()

    return pl.pallas_call(
        body, name=name, out_shape=(pltpu.HBM(shard.shape, shard.dtype), pltpu.HBM(land.shape, land.dtype)),
        in_specs=(HBM, HBM, SEM, SEM, ANY), out_specs=(HBM, HBM), input_output_aliases={0: 0, 1: 1},
        compiler_params=pltpu.CompilerParams(has_side_effects=SIDE_EFFECT),
    )(shard, land, send_sems, recv_sems, after)


def _gather_finish(shard, land, name):
    rows, cols = shard.shape
    half = rows // 2

    def body(in_ref, land_ref, out_ref, send_sems, recv_sems, buf, in_sems, out_sems):
        x, y, c, chips = _position()
        me = 2 * x + y
        sibling = (x, y, 1 - c)

        def slab(chip, h):
            return out_ref.at[chip, pl.ds(h * half, half), :]

        passed = [_remote(slab(2 * cx + cy, c), slab(2 * cx + cy, c), send_sems.at[j], recv_sems.at[j], sibling)
                  for j, (cx, cy) in enumerate(chips)]
        for cp in passed:
            cp.start()
        _staged_copies([(in_ref.at[pl.ds(r, n), :], out_ref.at[me, pl.ds(r, n), :]) for r, n in _row_chunks(rows)],
                       buf, in_sems, out_sems)
        for j, (cx, cy) in enumerate(chips):
            got = slab(2 * cx + cy, 1 - c)
            _remote(got, got, send_sems.at[j], recv_sems.at[j], sibling).wait_recv()
        for cp in passed:
            cp.wait_send()

    return pl.pallas_call(
        body, name=name, in_specs=[ANY, ANY], out_specs=ANY, out_shape=jax.ShapeDtypeStruct(land.shape, land.dtype),
        input_output_aliases={1: 0},
        scratch_shapes=[pltpu.SemaphoreType.DMA((3,)), pltpu.SemaphoreType.DMA((3,))] + _stage_scratch(shard.dtype, cols),
        compiler_params=pltpu.CompilerParams(vmem_limit_bytes=VMEM_LIMIT),
    )(shard, land)


def _gather_devices(block, name):
    rows, cols = block.shape

    def body(in_ref, out_ref, send_sems, recv_sems, local_sem):
        x, y, c, chips = _position()
        sibling = (x, y, 1 - c)

        def slot(px, py, pc):
            return out_ref.at[4 * px + 2 * py + pc]

        mine = pltpu.make_async_copy(in_ref, slot(x, y, c), local_sem)
        mine.start()
        first = [_remote(in_ref, slot(x, y, c), send_sems.at[0], recv_sems.at[0], sibling)]
        first += [_remote(in_ref, slot(x, y, c), send_sems.at[1 + j], recv_sems.at[1 + j], (cx, cy, c))
                  for j, (cx, cy) in enumerate(chips)]
        for cp in first:
            cp.start()
        passed = []
        for j, (cx, cy) in enumerate(chips):
            got = slot(cx, cy, c)
            _remote(got, got, send_sems.at[1 + j], recv_sems.at[1 + j], sibling).wait_recv()
            cp = _remote(got, got, send_sems.at[4 + j], recv_sems.at[4 + j], sibling)
            cp.start()
            passed.append(cp)
        got = slot(x, y, 1 - c)
        _remote(got, got, send_sems.at[0], recv_sems.at[0], sibling).wait_recv()
        for j, (cx, cy) in enumerate(chips):
            got = slot(cx, cy, 1 - c)
            _remote(got, got, send_sems.at[4 + j], recv_sems.at[4 + j], sibling).wait_recv()
        for cp in first + passed:
            cp.wait_send()
        mine.wait()

    return pl.pallas_call(
        body, name=name, in_specs=[ANY], out_specs=ANY,
        out_shape=jax.ShapeDtypeStruct((N_DEV, rows, cols), block.dtype),
        scratch_shapes=[pltpu.SemaphoreType.DMA((7,)), pltpu.SemaphoreType.DMA((7,)), pltpu.SemaphoreType.DMA],
    )(block)


def _pair_send(grads, name):
    n = len(grads)
    hs = [g.shape[2] for g in grads]
    offs = [sum(hs[:i]) for i in range(n)]
    cols = grads[0].shape[3]

    def body(*refs):
        g_refs = refs[:n]
        got_ref, send_sems, recv_sems = refs[n:]
        x, y, c, _ = _position()
        copies = [_remote(g_ref.at[:, 1 - c], got_ref.at[:, pl.ds(offs[i], hs[i]), :], send_sems.at[i], recv_sems.at[i],
                          (x, y, 1 - c)) for i, g_ref in enumerate(g_refs)]
        for cp in copies:
            cp.start()
        for cp in copies:
            cp.wait()

    return pl.pallas_call(
        body, name=name, in_specs=[ANY] * n, out_specs=ANY, out_shape=jax.ShapeDtypeStruct((N_CHIPS, sum(hs), cols), F32),
        scratch_shapes=[pltpu.SemaphoreType.DMA((n,)), pltpu.SemaphoreType.DMA((n,))],
    )(*grads)


def _pair_copies(g_refs, land_ref, send_sems, recv_sems):
    x, y, c, _ = _position()
    hs = [g.shape[2] for g in g_refs]
    offs = [sum(hs[:i]) for i in range(len(hs))]
    return [_remote(g_ref.at[:, 1 - c], land_ref.at[:, pl.ds(offs[i], hs[i]), :], send_sems.at[i], recv_sems.at[i],
                    (x, y, 1 - c)) for i, g_ref in enumerate(g_refs)]


def _pair_send_start(grads, name):
    n = len(grads)
    land = lax.empty((N_CHIPS, sum(g.shape[2] for g in grads), grads[0].shape[3]), F32)

    def body(*refs):
        for cp in _pair_copies(refs[:n], refs[n], refs[n + 1], refs[n + 2]):
            cp.start()
        refs[-1][...] = jnp.zeros_like(refs[-1])

    buffers = [*grads, land]
    return pl.pallas_call(
        body, name=name,
        out_shape=(pltpu.SemaphoreType.DMA((n,)), pltpu.SemaphoreType.DMA((n,)),
                   *[pltpu.HBM(b.shape, b.dtype) for b in buffers], jax.ShapeDtypeStruct((8, LANES), F32)),
        in_specs=(HBM,) * (n + 1), out_specs=(SEM, SEM, *(HBM,) * (n + 1), pl.BlockSpec(memory_space=pltpu.VMEM)),
        input_output_aliases={i: 2 + i for i in range(n + 1)},
        compiler_params=pltpu.CompilerParams(has_side_effects=SIDE_EFFECT),
    )(*[pltpu.with_memory_space_constraint(b, pltpu.HBM) for b in buffers])


def _pair_send_wait(send_sems, recv_sems, buffers, after, name):
    n = len(buffers) - 1

    def body(*refs):
        for cp in _pair_copies(refs[:n], refs[n], refs[n + 1], refs[n + 2]):
            cp.wait_send()
            cp.wait_recv()

    return pl.pallas_call(
        body, name=name, out_shape=tuple(pltpu.HBM(b.shape, b.dtype) for b in buffers),
        in_specs=(*(HBM,) * (n + 1), SEM, SEM, ANY), out_specs=(HBM,) * (n + 1),
        input_output_aliases={i: i for i in range(n + 1)},
        compiler_params=pltpu.CompilerParams(has_side_effects=SIDE_EFFECT),
    )(*buffers, send_sems, recv_sems, after)


def _pair_add(grads, got, name):
    n = len(grads)
    hs = [g.shape[2] for g in grads]
    offs = [sum(hs[:i]) for i in range(n)]
    cols = grads[0].shape[3]
    hmax = max(hs)
    units = [(i, k) for k in range(N_CHIPS) for i in range(n)]

    def body(*refs):
        g_refs = refs[:n]
        got_ref, out_ref, a_buf, b_buf, o_buf, a_sems, b_sems, o_sems = refs[n:]
        c = lax.axis_index("c")

        def loads(u):
            i, k = units[u]
            slot, rows = u % 2, pl.ds(0, hs[i])
            return (pltpu.make_async_copy(g_refs[i].at[k, c], a_buf.at[slot, rows, :], a_sems.at[slot]),
                    pltpu.make_async_copy(got_ref.at[k, pl.ds(offs[i], hs[i]), :], b_buf.at[slot, rows, :], b_sems.at[slot]))

        def store(u):
            i, k = units[u]
            return pltpu.make_async_copy(o_buf.at[u % 2, pl.ds(0, hs[i]), :], out_ref.at[k, pl.ds(offs[i], hs[i]), :],
                                         o_sems.at[u % 2])

        for cp in loads(0):
            cp.start()
        for u, (i, k) in enumerate(units):
            if u + 1 < len(units):
                for cp in loads(u + 1):
                    cp.start()
            for cp in loads(u):
                cp.wait()
            if u >= 2:
                store(u - 2).wait()
            rows = pl.ds(0, hs[i])
            o_buf[u % 2, rows, :] = (a_buf[u % 2, rows, :] + b_buf[u % 2, rows, :]).astype(BF16)
            store(u).start()
        store(len(units) - 2).wait()
        store(len(units) - 1).wait()

    return pl.pallas_call(
        body, name=name, in_specs=[ANY] * (n + 1), out_specs=ANY,
        out_shape=jax.ShapeDtypeStruct((N_CHIPS, sum(hs), cols), BF16),
        scratch_shapes=[pltpu.VMEM((2, hmax, cols), F32), pltpu.VMEM((2, hmax, cols), F32), pltpu.VMEM((2, hmax, cols), BF16),
                        pltpu.SemaphoreType.DMA((2,)), pltpu.SemaphoreType.DMA((2,)), pltpu.SemaphoreType.DMA((2,))],
        compiler_params=pltpu.CompilerParams(vmem_limit_bytes=VMEM_LIMIT),
    )(*grads, got)


def _chip_exchange(parts, name):
    _, rows, cols = parts.shape

    def body(in_ref, out_ref, send_sems, recv_sems):
        x, y, c, chips = _position()
        sent = [_remote(in_ref.at[2 * cx + cy], out_ref.at[j], send_sems.at[j], recv_sems.at[j], (cx, cy, c))
                for j, (cx, cy) in enumerate(chips)]
        for cp in sent:
            cp.start()
        for cp in sent:
            cp.wait()

    return pl.pallas_call(
        body, name=name, in_specs=[ANY], out_specs=ANY, out_shape=jax.ShapeDtypeStruct((3, rows, cols), parts.dtype),
        scratch_shapes=[pltpu.SemaphoreType.DMA((3,)), pltpu.SemaphoreType.DMA((3,))],
    )(parts)


def _exchange_copies(in_ref, land_ref, send_sems, recv_sems):
    x, y, c, chips = _position()
    return [_remote(in_ref.at[2 * cx + cy], land_ref.at[j], send_sems.at[j], recv_sems.at[j], (cx, cy, c))
            for j, (cx, cy) in enumerate(chips)]


def _exchange_start(parts, name):
    _, rows, cols = parts.shape

    def body(in_ref, land_ref, send_sems, recv_sems, in_thru, land_thru, token):
        for cp in _exchange_copies(in_ref, land_ref, send_sems, recv_sems):
            cp.start()
        token[...] = jnp.zeros_like(token)

    land = lax.empty((3, rows, cols), parts.dtype)
    return pl.pallas_call(
        body, name=name,
        out_shape=(pltpu.SemaphoreType.DMA((3,)), pltpu.SemaphoreType.DMA((3,)), pltpu.HBM(parts.shape, parts.dtype),
                   pltpu.HBM(land.shape, land.dtype), jax.ShapeDtypeStruct((8, LANES), F32)),
        in_specs=(HBM, HBM), out_specs=(SEM, SEM, HBM, HBM, pl.BlockSpec(memory_space=pltpu.VMEM)),
        input_output_aliases={0: 2, 1: 3},
        compiler_params=pltpu.CompilerParams(has_side_effects=SIDE_EFFECT),
    )(pltpu.with_memory_space_constraint(parts, pltpu.HBM), pltpu.with_memory_space_constraint(land, pltpu.HBM))


def _exchange_wait(send_sems, recv_sems, parts, land, after, name):
    def body(in_ref, land_ref, send_sems, recv_sems, after_ref, in_out, land_out):
        for cp in _exchange_copies(in_ref, land_ref, send_sems, recv_sems):
            cp.wait_send()
            cp.wait_recv()

    return pl.pallas_call(
        body, name=name, out_shape=(pltpu.HBM(parts.shape, parts.dtype), pltpu.HBM(land.shape, land.dtype)),
        in_specs=(HBM, HBM, SEM, SEM, ANY), out_specs=(HBM, HBM), input_output_aliases={0: 0, 1: 1},
        compiler_params=pltpu.CompilerParams(has_side_effects=SIDE_EFFECT),
    )(parts, land, send_sems, recv_sems, after)


def _chip_sum(parts, recv, chip, name):
    _, rows, cols = parts.shape
    tm = _tile(rows, 512, 16)

    def body(chip_ref, own_ref, recv_ref, o_ref):
        acc = own_ref[0].astype(F32)
        for j in range(3):
            acc = acc + recv_ref[j].astype(F32)
        o_ref[...] = acc

    return pl.pallas_call(
        body, name=name,
        grid_spec=pltpu.PrefetchScalarGridSpec(
            num_scalar_prefetch=1, grid=(rows // tm,),
            in_specs=[pl.BlockSpec((1, tm, cols), lambda i, chip_ref: (chip_ref[0], i, 0)),
                      pl.BlockSpec((3, tm, cols), lambda i, chip_ref: (0, i, 0))],
            out_specs=pl.BlockSpec((tm, cols), lambda i, chip_ref: (i, 0))),
        out_shape=jax.ShapeDtypeStruct((rows, cols), F32), compiler_params=_params("parallel"),
    )(chip, parts, recv)


def _join_unpack(mine, hs, groups, name):
    n = len(hs)
    offs = [sum(hs[:i]) for i in range(n)]
    cols = mine.shape[1]
    n_out = max(groups) + 1
    base = [2 * sum(h for h, g in zip(hs[:i], groups[:i]) if g == groups[i]) for i in range(n)]
    out_rows = [2 * sum(h for h, g in zip(hs, groups) if g == k) for k in range(n_out)]

    def body(in_ref, *refs):
        outs = refs[:n_out]
        send_sems, recv_sems, buf, in_sems, out_sems = refs[n_out:]
        x, y, c, _ = _position()
        sibling = (x, y, 1 - c)
        sent, local = [], []
        for i in range(n):
            src = in_ref.at[pl.ds(offs[i], hs[i]), :]
            here = outs[groups[i]].at[pl.ds(base[i] + c * hs[i], hs[i]), :]
            cp = _remote(src, here, send_sems.at[i], recv_sems.at[i], sibling)
            cp.start()
            sent.append(cp)
            local.append((src, here))
        _staged_copies(local, buf, in_sems, out_sems)
        for i, cp in enumerate(sent):
            there = outs[groups[i]].at[pl.ds(base[i] + (1 - c) * hs[i], hs[i]), :]
            _remote(there, there, send_sems.at[i], recv_sems.at[i], sibling).wait_recv()
            cp.wait_send()

    assert max(hs) <= STAGE_ROWS
    return pl.pallas_call(
        body, name=name, in_specs=[ANY], out_specs=[ANY] * n_out,
        out_shape=[jax.ShapeDtypeStruct((r, cols), F32) for r in out_rows],
        scratch_shapes=[pltpu.SemaphoreType.DMA((n,)), pltpu.SemaphoreType.DMA((n,))] + _stage_scratch(F32, cols),
        compiler_params=pltpu.CompilerParams(vmem_limit_bytes=VMEM_LIMIT),
    )(mine)


SMALL_ROWS = 16
SMALL_PACK_ROWS = 256


def _small_rows(n):
    return -(-n // (SMALL_ROWS * LANES)) * SMALL_ROWS


def _pack_small(arrs):
    parts = []
    for a in arrs:
        flat = a.reshape(-1)
        rows = _small_rows(flat.shape[0])
        flat = jnp.pad(flat, (0, rows * LANES - flat.shape[0]))
        parts.append(flat.reshape(rows, LANES))
    total = sum(p.shape[0] for p in parts)
    parts.append(jnp.zeros((-total % SMALL_PACK_ROWS, LANES), F32))
    return jnp.concatenate(parts, axis=0)


def _unpack_small(packed, shapes):
    out, r = [], 0
    for sh in shapes:
        n = math.prod(sh)
        cnt = _small_rows(n)
        out.append(packed[r:r + cnt].reshape(-1)[:n].reshape(sh))
        r += cnt
    return out


def _ffn_bwd(dh, dhb, h_in, saved, g_norm, w_gate_t, w_up_t, w_down, tag, after=None):
    n, gate, up, act = saved
    dgate, dup = _ffn_dact(dhb, w_down, gate, up, f"{tag}_dact", after)
    dw_down = _matmul(act, dhb, trans_a=True, name=f"{tag}_dwdown")
    dw_gate_t = _matmul(dgate, n, trans_a=True, name=f"{tag}_dwgate")
    dw_up_t = _matmul(dup, n, trans_a=True, name=f"{tag}_dwup")
    dh_in, dh_inb, dg = _dn_norm([(dgate, w_gate_t), (dup, w_up_t)], h_in, g_norm, dh, f"{tag}_dnorm")
    return dh_in, dh_inb, dg, dw_gate_t, dw_up_t, dw_down


def _local_step(x, tgt, w, big, late_weights, reduce_send, reduce_exchange):
    s = x.shape[0]
    tabs = _rope_tables(s)
    grads, gbig = {}, {}

    g_ev = w['ev_norm_g']
    n1 = _rms_fwd(x, g_ev, "ev_norm")
    proj0 = _matmul(n1, big['ev_w_in', 0], trans_b=True, name="ev_in", out_dtype=BF16, rows_inner=True)
    q0, k0, v0 = _qkv_prep_even(proj0, tabs, "ev_qkv")
    sinks = w['ev_sinks'].reshape(-1)
    o0, lse0, o0b = _attn_fwd(q0, k0, v0, sinks, max_dist=BLOCK - 1, name="ev_attn", emit_bf16=True)
    yconv, cout = _conv_fwd(proj0, w['ev_conv_w'][0], w['ev_conv_b'], w['ev_conv_ln_g'], w['ev_conv_ln_b'], "ev_conv")
    mix0 = (o0b[0], cout)
    g_f0 = w['ffn_norm_g'][0:1]
    h1, n2 = _matmul_norm(mix0, big['ev_w_out', 0], x, g_f0, "ev_out")
    big = {**big, **late_weights(h1)}

    g_od = w['od_norm_g']
    act0, gate0, up0 = _ffn_gate_up(n2, big['ffn_w_gate', 0], big['ffn_w_up', 0], "ffn0_gate_up")
    h2, n3 = _matmul_norm(act0, big['ffn_w_down', 0], h1, g_od, "ffn0_down")
    ffn0 = (n2, gate0, up0, act0)

    proj1 = _matmul(n3, big['od_w_in', 0], trans_b=True, name="od_in", out_dtype=BF16, rows_inner=True)
    qkv = _qkv_prep_odd(proj1, tabs, "od_qkv")
    nb = len(DILATIONS)
    outs, lses = [], []
    for i, d in enumerate(DILATIONS):
        o_r, lse_r = _attn_fwd(qkv[i], qkv[nb + i], qkv[2 * nb + i], None, max_dist=BLOCK, name=f"od_attn{d}")
        outs.append(o_r)
        lses.append(lse_r)
    comb = _combine(outs, lses, "od_combine")
    c_bf16 = comb[0]
    c_fold = {1: comb[1]}
    lse_fold = {1: comb[2]}
    for i, d in enumerate(DILATIONS[1:]):
        c_fold[d], lse_fold[d] = comb[3 + 2 * i], comb[4 + 2 * i]
    w_sp = w['od_spatial_w'][0]
    sb_t = w['od_spatial_b'][0].T
    mixed, dout = _gate_fwd(proj1, w['od_sgu_ln_g'], w['od_sgu_ln_b'], w_sp, sb_t, "od_gate")
    mix1 = (c_bf16, dout)
    g_f1 = w['ffn_norm_g'][1:2]
    h3, n4 = _matmul_norm(mix1, big['od_w_out', 0], h2, g_f1, "od_out")
    act1, gate1, up1 = _ffn_gate_up(n4, big['ffn_w_gate', 1], big['ffn_w_up', 1], "ffn1_gate_up")
    ffn1 = (n4, gate1, up1, act1)

    dh4, dh4b, dg_final, loss_tile = _matmul_final(act1, big['ffn_w_down', 1], h3, w['final_norm_g'].reshape(1, D_MODEL),
                                                   tgt, "ffn1_down_loss")
    grads['final_norm_g'] = dg_final.reshape(D_MODEL)

    dh3, dh3b, dg_f1, gbig['ffn_w_gate', 1], gbig['ffn_w_up', 1], gbig['ffn_w_down', 1] = _ffn_bwd(
        dh4, dh4b, h3, ffn1, g_f1, big['ffn_w_gate', 1], big['ffn_w_up', 1], big['ffn_w_down', 1], "ffn1")

    dmix1 = _matmul(dh3b, big['od_w_out', 0], trans_b=True, name="od_dmix")
    gbig['od_w_out', 0] = _matmul_tn_pair(mix1[0], mix1[1], dh3b, "od_dwout")
    do_fold = dict(zip(DILATIONS[1:], _fold_dout(dmix1, "od_fold_dout")))
    do_fold[1] = dmix1[None]
    dqs, dks, dvs = [], [], []
    for i, d in enumerate(DILATIONS):
        dq_r, dk_r, dv_r = _attn_bwd(qkv[i], qkv[nb + i], qkv[2 * nb + i], do_fold[d], c_fold[d], lse_fold[d], None,
                                     max_dist=BLOCK, name=f"od_dattn{d}")
        dqs.append(dq_r)
        dks.append(dk_r)
        dvs.append(dv_r)
    dz, dg_sgu, db_sgu, dw_sp, dsb = _gate_bwd(dmix1, proj1, mixed, w['od_sgu_ln_g'], w['od_sgu_ln_b'], w_sp, "od_dgate")
    grads['od_sgu_ln_g'], grads['od_sgu_ln_b'] = dg_sgu, db_sgu
    grads['od_spatial_w'], grads['od_spatial_b'] = dw_sp[None], dsb[None]
    dproj1 = _qkv_post_odd(dqs, dks, dvs, dz, tabs, "od_dproj")
    gbig['od_w_in', 0] = _matmul(dproj1, n3, trans_a=True, name="od_dwin")
    dh2, dh2b, dg_od = _dn_norm([(dproj1, big['od_w_in', 0])], h2, g_od, dh3, "od_dnorm")
    grads['od_norm_g'] = dg_od
    token = reduce_send(0, gbig)

    dh1, dh1b, dg_f0, gbig['ffn_w_gate', 0], gbig['ffn_w_up', 0], gbig['ffn_w_down', 0] = _ffn_bwd(
        dh2, dh2b, h1, ffn0, g_f0, big['ffn_w_gate', 0], big['ffn_w_up', 0], big['ffn_w_down', 0], "ffn0", token)
    grads['ffn_norm_g'] = jnp.concatenate([dg_f0, dg_f1], axis=0)
    token = reduce_exchange(0, dh1) + reduce_send(1, gbig)

    dmix0 = _matmul(dh1b, big['ev_w_out', 0], trans_b=True, name="ev_dmix", after=token)
    gbig['ev_w_out', 0] = _matmul_tn_pair(mix0[0], mix0[1], dh1b, "ev_dwout")
    dq0, dk0, dv0, dsink = _attn_bwd(q0, k0, v0, dmix0[None], o0, lse0, sinks, max_dist=BLOCK - 1, name="ev_dattn")
    grads['ev_sinks'] = dsink[:, 0, :].reshape(N_PAIRS, 2, HEAD_DIM)[:, :, 0].reshape(1, 8)
    token = reduce_exchange(1, dq0)
    dyc, dg_cln, db_cln, dcb = _conv_tail_bwd(dmix0, yconv, w['ev_conv_ln_g'] + token[0:1, 0:1], w['ev_conv_ln_b'],
                                              "ev_dconv_tail")
    grads['ev_conv_ln_g'], grads['ev_conv_ln_b'], grads['ev_conv_b'] = dg_cln, db_cln, dcb
    dglu, dconv_w = _conv_bwd(proj0, dyc, w['ev_conv_w'][0], "ev_dconv")
    grads['ev_conv_w'] = dconv_w[None]
    dproj0 = _qkv_post_even(dq0, dk0, dv0, dglu, tabs, "ev_dproj")
    gbig['ev_w_in', 0] = _matmul(dproj0, n1, trans_a=True, name="ev_dwin")
    dx, _, dg_ev = _dn_norm([(dproj0, big['ev_w_in', 0])], x, g_ev, dh1, "ev_dnorm")
    grads['ev_norm_g'] = dg_ev
    return loss_tile, dx, grads, gbig


def _shard_rows(w, layer, by_cols):
    return w[layer].T if by_cols else w[layer]


def kernel(x, ev_norm_g, ev_w_in, ev_sinks, ev_conv_w, ev_conv_b, ev_conv_ln_g, ev_conv_ln_b, ev_w_out, od_norm_g, od_w_in, od_sgu_ln_g, od_sgu_ln_b, od_spatial_w, od_spatial_b, od_w_out, ffn_norm_g, ffn_w_gate, ffn_w_up, ffn_w_down, final_norm_g, loss_target, m_ev_norm_g, m_ev_w_in, m_ev_sinks, m_ev_conv_w, m_ev_conv_b, m_ev_conv_ln_g, m_ev_conv_ln_b, m_ev_w_out, m_od_norm_g, m_od_w_in, m_od_sgu_ln_g, m_od_sgu_ln_b, m_od_spatial_w, m_od_spatial_b, m_od_w_out, m_ffn_norm_g, m_ffn_w_gate, m_ffn_w_up, m_ffn_w_down, m_final_norm_g, v_ev_norm_g, v_ev_w_in, v_ev_sinks, v_ev_conv_w, v_ev_conv_b, v_ev_conv_ln_g, v_ev_conv_ln_b, v_ev_w_out, v_od_norm_g, v_od_w_in, v_od_sgu_ln_g, v_od_sgu_ln_b, v_od_spatial_w, v_od_spatial_b, v_od_w_out, v_ffn_norm_g, v_ffn_w_gate, v_ffn_w_up, v_ffn_w_down, v_final_norm_g):
    given = dict(locals())
    wts = {n: given[n] for n in WEIGHTS}
    mom = {n: given["m_" + n] for n in WEIGHTS}
    var = {n: given["v_" + n] for n in WEIGHTS}
    chip = 2 * lax.axis_index("x") + lax.axis_index("y")

    shard_rows = [_shard_rows(wts[n], layer, by_cols).astype(BF16) for n, layer, by_cols in BIG]
    counts = [a.shape[0] for a in shard_rows]
    n_first = sum(n.startswith('ev_') for n, _, _ in BIG)

    def unpack(stacked, entries, cnts):
        out, r = {}, 0
        for (n, layer, _), cnt in zip(entries, cnts):
            out[n, layer] = stacked[:, r:r + cnt].reshape(N_CHIPS * cnt, D_MODEL)
            r += cnt
        return out

    first_w = _gather_chips(jnp.concatenate(shard_rows[:n_first], axis=0), "gather_weights_ev")
    big = unpack(first_w, BIG[:n_first], counts[:n_first])
    send_sems, recv_sems, late_shard, late_land, token = _gather_start(jnp.concatenate(shard_rows[n_first:], axis=0),
                                                                      first_w, "gather_weights_start")

    def late_weights(after):
        shard, land = _gather_wait(send_sems, recv_sems, late_shard, late_land, after, "gather_weights_wait")
        return unpack(_gather_finish(shard, land, "gather_weights_finish"), BIG[n_first:], counts[n_first:])

    full = {n: wts[n] for n in SMALL_REPL}
    full['ev_norm_g'] = full['ev_norm_g'] + token[0:1, 0:1]
    small_shards = [wts[n] for n in SMALL_SHARDED]
    small_shapes = [a.shape for a in small_shards]
    all_s = _gather_chips(_pack_small(small_shards), "gather_small_weights")
    per_chip = [_unpack_small(all_s[k], small_shapes) for k in range(N_CHIPS)]
    for i, n in enumerate(SMALL_SHARDED):
        full[n] = jnp.concatenate([per_chip[k][i] for k in range(N_CHIPS)], axis=-1)

    half_rows = {(n, layer): cnt // 2 for (n, layer, _), cnt in zip(BIG, counts)}
    in_flight = []

    sending = {}

    def halves(stage, gbig):
        return [gbig[e].reshape(N_CHIPS, 2, half_rows[e], D_MODEL) for e in GRAD_STAGES[stage]]

    def reduce_send(stage, gbig):
        send_sems, recv_sems, *buffers, token = _pair_send_start(halves(stage, gbig), f"grad_pair_start{stage}")
        sending[stage] = (send_sems, recv_sems, buffers)
        return token

    def reduce_exchange(stage, after):
        send_sems, recv_sems, buffers = sending.pop(stage)
        *split, got = _pair_send_wait(send_sems, recv_sems, buffers, after, f"grad_pair_wait{stage}")
        chip_part = _pair_add(split, got, f"grad_pair_add{stage}")
        *handles, token = _exchange_start(chip_part, f"grad_exchange_start{stage}")
        in_flight.append(handles)
        return token

    loss_tile, grad_x, grads, gbig = _local_step(x[0], loss_target[0], full, big, late_weights, reduce_send, reduce_exchange)
    loss = lax.psum(loss_tile[0, 0], ("x", "y", "c"))

    reduced = {}
    for stage, entries in enumerate(GRAD_STAGES):
        if stage < len(in_flight):
            chip_part, from_chips = _exchange_wait(*in_flight[stage], grad_x, f"grad_exchange_wait{stage}")
        else:
            split = halves(stage, gbig)
            chip_part = _pair_add(split, _pair_send(split, f"grad_pair_send{stage}"), f"grad_pair_add{stage}")
            from_chips = _chip_exchange(chip_part, f"grad_chip_exchange{stage}")
        my_half = _chip_sum(chip_part, from_chips, chip.reshape(1), f"grad_chip_sum{stage}")
        joined = _join_unpack(my_half, [half_rows[e] for e in entries], list(range(len(entries))), f"grad_join_halves{stage}")
        reduced.update(zip(entries, joined))

    small_names = SMALL_REPL + SMALL_SHARDED
    small_full_shapes = [grads[n].shape for n in small_names]
    spack = _pack_small([grads[n] for n in small_names])
    s_all = _gather_devices(spack, "grad_small_gather")
    s_sum = _unpack_small(_ordered_sum(s_all, "grad_small_sum"), small_full_shapes)
    g_all = dict(zip(small_names, s_sum))
    for n in SMALL_SHARDED:
        width = wts[n].shape[-1]
        g_all[n] = lax.dynamic_slice_in_dim(g_all[n], chip * width, width, axis=g_all[n].ndim - 1)

    delta, new_m, new_v = {}, {}, {}
    for n in BIG_NAMES:
        by_cols = [bc for nn, _, bc in BIG if nn == n][0]
        layers = wts[n].shape[0]

        def as_rows(a):
            return (jnp.swapaxes(a, 1, 2) if by_cols else a).reshape(-1, D_MODEL)

        def from_rows(a):
            a = a.reshape(layers, -1, D_MODEL)
            return jnp.swapaxes(a, 1, 2) if by_cols else a

        g_rows = [reduced[n, layer] for layer in range(layers)]
        g_rows = g_rows[0] if layers == 1 else jnp.concatenate(g_rows, axis=0)
        updated = _adamw(as_rows(wts[n]), g_rows, as_rows(mom[n]), as_rows(var[n]), f"adamw_{n}")
        g_all[n] = from_rows(g_rows)
        delta[n], new_m[n], new_v[n] = (from_rows(a) for a in updated)
    shapes = [wts[n].shape for n in small_names]
    d_s, m_s, v_s = _adamw(*[_pack_small([src[n] for n in small_names]) for src in (wts, g_all, mom, var)], "adamw_small")
    for dst, packed in ((delta, d_s), (new_m, m_s), (new_v, v_s)):
        dst.update(zip(small_names, _unpack_small(packed, shapes)))

    return (loss, grad_x[None], *[g_all[n] for n in WEIGHTS], *[delta[n] for n in WEIGHTS],
            *[new_m[n] for n in WEIGHTS], *[new_v[n] for n in WEIGHTS])
```

```python
import math

import jax
import jax.numpy as jnp
from jax import lax
from jax.experimental import pallas as pl
from jax.experimental.pallas import tpu as pltpu

F32 = jnp.float32
BF16 = jnp.bfloat16

D_MODEL = 1024
HEAD_DIM = 64
ROT_DIM = 16
ROPE_THETA = 500000.0
RMS_EPS = 1e-6
LN_EPS = 1e-5
BLOCK = 128
CONV_WIDTH = 31
CONV_HALO = 32
CONV_ROWS = 64
D_FF = 2816
N_GROUPS = 8
ATTN_W = 512
ATTN_SCALE = HEAD_DIM ** -0.5
NEG = -1e30
DILATIONS = (1, 4, 16)

ADAM_LR = 0.001
ADAM_B1 = 0.9
ADAM_B2 = 0.999
ADAM_EPS = 1e-08
ADAM_WD = 0.01
ADAM_STEP = 10

LANES = 128
N_PAIRS = ATTN_W // LANES
VMEM_LIMIT = 56 * 1024 * 1024
MESH = pl.DeviceIdType.MESH
N_CHIPS = 4
N_DEV = 8

WEIGHTS = ['ev_norm_g', 'ev_w_in', 'ev_sinks', 'ev_conv_w', 'ev_conv_b', 'ev_conv_ln_g', 'ev_conv_ln_b', 'ev_w_out',
           'od_norm_g', 'od_w_in', 'od_sgu_ln_g', 'od_sgu_ln_b', 'od_spatial_w', 'od_spatial_b', 'od_w_out',
           'ffn_norm_g', 'ffn_w_gate', 'ffn_w_up', 'ffn_w_down', 'final_norm_g']
BIG = [('ev_w_in', 0, True), ('ev_w_out', 0, False), ('od_w_in', 0, True), ('od_w_out', 0, False),
       ('ffn_w_gate', 0, True), ('ffn_w_gate', 1, True), ('ffn_w_up', 0, True), ('ffn_w_up', 1, True),
       ('ffn_w_down', 0, False), ('ffn_w_down', 1, False)]
BIG_NAMES = ['ev_w_in', 'ev_w_out', 'od_w_in', 'od_w_out', 'ffn_w_gate', 'ffn_w_up', 'ffn_w_down']
GRAD_STAGES = ([('od_w_in', 0), ('od_w_out', 0), ('ffn_w_gate', 1), ('ffn_w_up', 1), ('ffn_w_down', 1)],
               [('ffn_w_gate', 0), ('ffn_w_up', 0), ('ffn_w_down', 0)],
               [('ev_w_in', 0), ('ev_w_out', 0)])
SMALL_SHARDED = ['ev_conv_w', 'od_norm_g', 'od_sgu_ln_g', 'od_sgu_ln_b']
SMALL_REPL = ['ev_norm_g', 'ev_sinks', 'ev_conv_b', 'ev_conv_ln_g', 'ev_conv_ln_b', 'od_spatial_w', 'od_spatial_b',
              'ffn_norm_g', 'final_norm_g']


def _tile(n, cap, mult=LANES):
    best = None
    for t in range(mult, min(n, cap) + 1, mult):
        if n % t == 0:
            best = t
    assert best is not None, (n, cap)
    return best


def _params(*sem):
    return pltpu.CompilerParams(dimension_semantics=sem, vmem_limit_bytes=VMEM_LIMIT)


def _sigmoid(x):
    return 1.0 / (1.0 + jnp.exp(-x))


def _pair_block(p):
    return slice(p * LANES, (p + 1) * LANES)


def _matmul(a, b, *, name, trans_a=False, trans_b=False, add=None, out_dtype=F32, after=None, rows_inner=False):
    parts = a if isinstance(a, (tuple, list)) else (a,)
    if trans_a:
        k, m = parts[0].shape
    else:
        m = parts[0].shape[0]
        k = sum(p.shape[1] for p in parts)
    if trans_b:
        n, k2 = b.shape
    else:
        k2, n = b.shape
    assert k == k2 and b.dtype == BF16 and all(p.dtype == BF16 for p in parts)
    tm = _tile(m, D_FF // 2 if trans_a else 512)
    tn = _tile(n, D_FF // 2)
    tk = k if k <= D_FF else _tile(k, 2048)
    nk = k // tk
    na = len(parts)
    assert na == 1 or (nk == 1 and not trans_a)
    assert nk == 1 or out_dtype == F32
    dims = (((0 if trans_a else 1,), (1 if trans_b else 0,)), ((), ()))
    has_add = add is not None

    def body(*refs):
        a_refs, b_ref = refs[:na], refs[na]
        add_ref = refs[na + 1] if has_add else None
        o_ref = refs[na + 1 + has_add + (after is not None)]
        def product():
            a_val = a_refs[0][...] if na == 1 else jnp.concatenate([r[...] for r in a_refs], axis=1)
            return lax.dot_general(a_val, b_ref[...], dims, preferred_element_type=F32)

        if nk == 1:
            part = product()
            if has_add:
                part = part + add_ref[...]
            o_ref[...] = part.astype(o_ref.dtype)
            return
        kk = pl.program_id(2)

        @pl.when(kk == 0)
        def _():
            o_ref[...] = product() + add_ref[...] if has_add else product()

        @pl.when(kk > 0)
        def _():
            o_ref[...] = product() + o_ref[...]

    def at(f):
        return (lambda j, i, kk: f(i, j, kk)) if rows_inner else f

    if trans_a:
        a_specs = [pl.BlockSpec((tk, tm), at(lambda i, j, kk: (kk, i)))]
    elif na == 1:
        a_specs = [pl.BlockSpec((tm, tk), at(lambda i, j, kk: (i, kk)))]
    else:
        a_specs = [pl.BlockSpec((tm, p.shape[1]), at(lambda i, j, kk: (i, 0))) for p in parts]
    b_spec = (pl.BlockSpec((tn, tk), at(lambda i, j, kk: (j, kk))) if trans_b
              else pl.BlockSpec((tk, tn), at(lambda i, j, kk: (kk, j))))
    o_spec = pl.BlockSpec((tm, tn), at(lambda i, j, kk: (i, j)))
    in_specs = a_specs + [b_spec] + ([o_spec] if has_add else [])
    operands = list(parts) + [b] + ([add] if has_add else [])
    if after is not None:
        in_specs.append(_after_spec(after))
        operands.append(after)
    grid = (n // tn, m // tm, nk) if rows_inner else (m // tm, n // tn, nk)
    return pl.pallas_call(
        body, name=name, grid=grid, in_specs=in_specs, out_specs=o_spec,
        out_shape=jax.ShapeDtypeStruct((m, n), out_dtype),
        compiler_params=_params("parallel", "parallel", "arbitrary"),
    )(*operands)


def _matmul_rows(a, b, add, epilogue, consts, tiled, outs, accs, name):
    parts = a if isinstance(a, (tuple, list)) else (a,)
    m = parts[0].shape[0]
    tm = 512
    na, nc, nt, no = len(parts), len(consts), len(tiled), len(outs)

    def body(*refs):
        a_refs, b_ref, add_ref = refs[:na], refs[na], refs[na + 1]
        const_refs = refs[na + 2:na + 2 + nc]
        tiled_refs = refs[na + 2 + nc:na + 2 + nc + nt]
        out_refs = refs[na + 2 + nc + nt:]
        a_val = a_refs[0][...] if na == 1 else jnp.concatenate([r[...] for r in a_refs], axis=1)
        h = jnp.dot(a_val, b_ref[...], preferred_element_type=F32) + add_ref[...]
        results = epilogue(h, [r[...] for r in const_refs], [r[...] for r in tiled_refs])
        for o_ref, val in zip(out_refs[:no], results[:no]):
            o_ref[...] = val.astype(o_ref.dtype)
        if accs:
            @pl.when(_first_step())
            def _():
                for o_ref in out_refs[no:]:
                    o_ref[...] = jnp.zeros_like(o_ref)

            for o_ref, val in zip(out_refs[no:], results[no:]):
                o_ref[...] += val

    row = lambda w: pl.BlockSpec((tm, w), lambda i: (i, 0))
    whole = lambda shape: pl.BlockSpec(shape, lambda i: (0,) * len(shape))
    return pl.pallas_call(
        body, name=name, grid=(m // tm,),
        in_specs=[row(p.shape[1]) for p in parts] + [whole(b.shape), row(D_MODEL)] + [whole(c.shape) for c in consts]
        + [row(t.shape[1]) for t in tiled],
        out_specs=[row(c) for c, _ in outs] + [whole(sh) for sh, _ in accs],
        out_shape=[jax.ShapeDtypeStruct((m, c), dt) for c, dt in outs] + [jax.ShapeDtypeStruct(sh, dt) for sh, dt in accs],
        compiler_params=_params("arbitrary"),
    )(*parts, b, add, *consts, *tiled)


def _matmul_norm(a, b, add, g, name):
    def epilogue(h, consts, tiled):
        r = lax.rsqrt(jnp.mean(h * h, axis=-1, keepdims=True) + RMS_EPS)
        return [h, h * r * consts[0]]

    return _matmul_rows(a, b, add, epilogue, [g], [], [(D_MODEL, F32), (D_MODEL, BF16)], [], name)


def _matmul_final(a, b, add, g, tgt, name):
    def epilogue(h, consts, tiled):
        gg = consts[0]
        r = lax.rsqrt(jnp.mean(h * h, axis=-1, keepdims=True) + RMS_EPS)
        xh = h * r
        e = xh * gg - tiled[0]
        loss = (0.5 / D_MODEL) * jnp.sum(jnp.sum(e * e, axis=-1, keepdims=True), axis=0, keepdims=True)
        dy = e * (1.0 / D_MODEL)
        dxh = dy * gg
        dx = r * (dxh - xh * jnp.mean(dxh * xh, axis=-1, keepdims=True))
        return [dx, dx, jnp.sum(dy * xh, axis=0, keepdims=True), jnp.broadcast_to(loss, (1, LANES))]

    return _matmul_rows(a, b, add, epilogue, [g], [tgt], [(D_MODEL, F32), (D_MODEL, BF16)],
                        [((1, D_MODEL), F32), ((1, LANES), F32)], name)


def _matmul_tn_pair(a1, a2, b, name):
    kdim, m1 = a1.shape
    m2 = a2.shape[1]
    n = b.shape[1]
    tn = _tile(n, 1024)
    tk = _tile(kdim, 2048)
    nk = kdim // tk
    dims = (((0,), (0,)), ((), ()))

    def body(a1_ref, a2_ref, b_ref, o_ref):
        kk = pl.program_id(1)
        def products():
            bv = b_ref[...]
            return (lax.dot_general(a1_ref[...], bv, dims, preferred_element_type=F32),
                    lax.dot_general(a2_ref[...], bv, dims, preferred_element_type=F32))

        @pl.when(kk == 0)
        def _():
            o_ref[0:m1, :], o_ref[m1:, :] = products()

        @pl.when(kk > 0)
        def _():
            top, bot = products()
            o_ref[0:m1, :] = top + o_ref[0:m1, :]
            o_ref[m1:, :] = bot + o_ref[m1:, :]

    return pl.pallas_call(
        body, name=name, grid=(n // tn, nk),
        in_specs=[pl.BlockSpec((tk, m1), lambda j, kk: (kk, 0)), pl.BlockSpec((tk, m2), lambda j, kk: (kk, 0)),
                  pl.BlockSpec((tk, tn), lambda j, kk: (kk, j))],
        out_specs=pl.BlockSpec((m1 + m2, tn), lambda j, kk: (0, j)),
        out_shape=jax.ShapeDtypeStruct((m1 + m2, n), F32),
        compiler_params=_params("parallel", "arbitrary"),
    )(a1, a2, b)


def _ffn_gate_up(n, w_gate_t, w_up_t, name):
    m, k = n.shape
    f = w_gate_t.shape[0]
    tm, tn = _tile(m, 1024), _tile(f, D_FF // 2)

    def body(n_ref, wg_ref, wu_ref, act_ref, gate_ref, up_ref):
        a = n_ref[...]

        def products(cols):
            return (lax.dot_general(a, wg_ref[cols, :], NT, preferred_element_type=F32),
                    lax.dot_general(a, wu_ref[cols, :], NT, preferred_element_type=F32))

        chunks = _col_chunks(tn)
        ahead = products(chunks[0])
        for idx, cols in enumerate(chunks):
            gate, up = ahead
            if idx + 1 < len(chunks):
                ahead = products(chunks[idx + 1])
            act_ref[:, cols] = (gate * _sigmoid(gate) * up).astype(BF16)
            gate_ref[:, cols] = gate.astype(BF16)
            up_ref[:, cols] = up.astype(BF16)

    wspec = pl.BlockSpec((tn, k), lambda j, i: (j, 0))
    ospec = pl.BlockSpec((tm, tn), lambda j, i: (i, j))
    return pl.pallas_call(
        body, name=name, grid=(f // tn, m // tm), in_specs=[pl.BlockSpec((tm, k), lambda j, i: (i, 0)), wspec, wspec],
        out_specs=[ospec] * 3, out_shape=[jax.ShapeDtypeStruct((m, f), BF16)] * 3,
        compiler_params=_params("parallel", "parallel"),
    )(n, w_gate_t, w_up_t)


def _col_chunks(n, width=384):
    return [slice(c, min(c + width, n)) for c in range(0, n, width)]


def _after_spec(after):
    return pl.BlockSpec(after.shape, lambda *_: (0,) * after.ndim)


def _ffn_dact(dhb, w_down, gate, up, name, after=None):
    m, k = dhb.shape
    f = w_down.shape[0]
    tm, tn = _tile(m, 1024), _tile(f, D_FF // 2)

    def body(d_ref, w_ref, g_ref, u_ref, *rest):
        dg_ref, du_ref = rest[-2:]
        d = d_ref[...]

        def product(cols):
            return lax.dot_general(d, w_ref[cols, :], NT, preferred_element_type=F32)

        chunks = _col_chunks(tn)
        ahead = product(chunks[0])
        for idx, cols in enumerate(chunks):
            dact = ahead
            if idx + 1 < len(chunks):
                ahead = product(chunks[idx + 1])
            g = g_ref[:, cols].astype(F32)
            sg = _sigmoid(g)
            dg_ref[:, cols] = (dact * u_ref[:, cols].astype(F32) * sg * (1.0 + g * (1.0 - sg))).astype(BF16)
            du_ref[:, cols] = (dact * g * sg).astype(BF16)

    ospec = pl.BlockSpec((tm, tn), lambda j, i: (i, j))
    extra = [] if after is None else [after]
    return pl.pallas_call(
        body, name=name, grid=(f // tn, m // tm),
        in_specs=[pl.BlockSpec((tm, k), lambda j, i: (i, 0)), pl.BlockSpec((tn, k), lambda j, i: (j, 0)), ospec, ospec]
        + [_after_spec(a) for a in extra],
        out_specs=[ospec] * 2, out_shape=[jax.ShapeDtypeStruct((m, f), BF16)] * 2,
        compiler_params=_params("parallel", "parallel"),
    )(dhb, w_down, gate, up, *extra)


def _dn_norm(pairs, h, g, dres, name):
    m = h.shape[0]
    tm = 512
    np_ = len(pairs)

    def body(*refs):
        a_refs, b_refs = refs[:np_], refs[np_:2 * np_]
        h_ref, dres_ref, g_ref, dh_ref, dhb_ref, dg_ref = refs[2 * np_:]

        @pl.when(_first_step())
        def _():
            dg_ref[...] = jnp.zeros_like(dg_ref)

        dy = jnp.dot(a_refs[0][...], b_refs[0][...], preferred_element_type=F32)
        for a_ref, b_ref in zip(a_refs[1:], b_refs[1:]):
            dy = jnp.dot(a_ref[...], b_ref[...], preferred_element_type=F32) + dy
        x = h_ref[...]
        r = lax.rsqrt(jnp.mean(x * x, axis=-1, keepdims=True) + RMS_EPS)
        xh = x * r
        dg_ref[...] += jnp.sum(dy * xh, axis=0, keepdims=True)
        dxh = dy * g_ref[...]
        tot = dres_ref[...] + r * (dxh - xh * jnp.mean(dxh * xh, axis=-1, keepdims=True))
        dh_ref[...] = tot
        dhb_ref[...] = tot.astype(BF16)

    row = lambda w: pl.BlockSpec((tm, w), lambda i: (i, 0))
    whole = lambda a: pl.BlockSpec(a.shape, lambda i: (0, 0))
    a_list, b_list = [a for a, _ in pairs], [b for _, b in pairs]
    return pl.pallas_call(
        body, name=name, grid=(m // tm,),
        in_specs=[row(a.shape[1]) for a in a_list] + [whole(b) for b in b_list] + [row(D_MODEL), row(D_MODEL), whole(g)],
        out_specs=[row(D_MODEL), row(D_MODEL), pl.BlockSpec((1, D_MODEL), lambda i: (0, 0))],
        out_shape=[jax.ShapeDtypeStruct((m, D_MODEL), F32), jax.ShapeDtypeStruct((m, D_MODEL), BF16),
                   jax.ShapeDtypeStruct((1, D_MODEL), F32)],
        compiler_params=_params("arbitrary"),
    )(*a_list, *b_list, h, dres, g)


def _rows(body, name, tm, tiled, consts, outs, accs=()):
    s = tiled[0].shape[0]
    assert s % tm == 0
    in_specs = [pl.BlockSpec((tm, a.shape[1]), lambda i: (i, 0)) for a in tiled]
    in_specs += [pl.BlockSpec(a.shape, lambda i, nd=a.ndim: (0,) * nd) for a in consts]
    out_shape = [jax.ShapeDtypeStruct((s, c), dt) for c, dt in outs]
    out_shape += [jax.ShapeDtypeStruct(sh, dt) for sh, dt in accs]
    out_specs = [pl.BlockSpec((tm, c), lambda i: (i, 0)) for c, _ in outs]
    out_specs += [pl.BlockSpec(sh, lambda i, nd=len(sh): (0,) * nd) for sh, _ in accs]
    return pl.pallas_call(
        body, name=name, grid=(s // tm,), in_specs=in_specs, out_specs=out_specs, out_shape=out_shape,
        compiler_params=_params("arbitrary"),
    )(*tiled, *consts)


def _first_step():
    return pl.program_id(0) == 0


def _rms_fwd(h, g, name):
    def body(h_ref, g_ref, n_ref):
        x = h_ref[...]
        r = lax.rsqrt(jnp.mean(x * x, axis=-1, keepdims=True) + RMS_EPS)
        n_ref[...] = (x * r * g_ref[...]).astype(BF16)

    return _rows(body, name, 512, [h], [g], [(D_MODEL, BF16)])[0]


def _rope_tables(s):
    half = ROT_DIM // 2
    inv_freq = ROPE_THETA ** (-jnp.arange(half, dtype=F32) * (2.0 / ROT_DIM))
    ang = jnp.arange(s, dtype=F32)[:, None] * inv_freq[None, :]
    cos, sin = jnp.cos(ang), jnp.sin(ang)
    rest = HEAD_DIM - ROT_DIM
    ones = jnp.ones((s, rest), F32)
    zeros = jnp.zeros((s, rest), F32)
    zh = jnp.zeros((s, half), F32)
    c_t = jnp.concatenate([cos, cos, ones], axis=1)
    a_t = jnp.concatenate([-sin, zh, zeros], axis=1)
    b_t = jnp.concatenate([zh, sin, zeros], axis=1)
    return tuple(jnp.tile(t, (1, LANES // HEAD_DIM)) for t in (c_t, a_t, b_t))


def _rot(x, c, a, b):
    w = x.shape[1]
    half = ROT_DIM // 2
    return x * c + pltpu.roll(x, w - half, 1) * a + pltpu.roll(x, half, 1) * b


def _wide(t, w):
    return t if w == LANES else jnp.tile(t, (1, w // LANES))


def _low_lanes(rows):
    return lax.broadcasted_iota(jnp.int32, (rows, LANES), 1) < HEAD_DIM


def _fold_store(x, sc_ref, out_refs):
    tm = x.shape[0]
    if any(d > 1 for d in out_refs):
        for p in range(N_PAIRS):
            sc_ref[p] = x[:, _pair_block(p)]
    for d, o_ref in out_refs.items():
        if d == 1:
            o_ref[0] = x.astype(o_ref.dtype)
            continue
        for r in range(d):
            for p in range(N_PAIRS):
                o_ref[r, :, _pair_block(p)] = sc_ref[p, pl.ds(r, tm // d, stride=d), :].astype(o_ref.dtype)


def _unfold_load(x_ref, sc_ref, d, add=False):
    n = x_ref.shape[1]
    for r in range(d):
        for p in range(N_PAIRS):
            rows = pl.ds(r, n, stride=d) if d > 1 else slice(None)
            val = x_ref[r, :, _pair_block(p)].astype(F32)
            if add:
                val = val + sc_ref[p, rows, :]
            sc_ref[p, rows, :] = val


def _folded_spec(d, tm, w=ATTN_W):
    return pl.BlockSpec((d, tm // d, w), lambda i: (0, i, 0))


def _folded_shape(s, d, dtype, w=ATTN_W):
    return jax.ShapeDtypeStruct((d, s // d, w), dtype)


def _qkv_prep_even(proj, tabs, name):
    s = proj.shape[0]
    tm = 512

    def body(p_ref, c_ref, a_ref, b_ref, q_ref, k_ref, v_ref):
        c, a, b = c_ref[...], a_ref[...], b_ref[...]
        q_ref[0] = _rot(p_ref[:, 0:ATTN_W].astype(F32), _wide(c, ATTN_W), _wide(a, ATTN_W), _wide(b, ATTN_W)).astype(BF16)
        lo = _low_lanes(tm)
        for src, o_ref in ((_rot(p_ref[:, 512:640].astype(F32), c, a, b), k_ref), (p_ref[:, 640:768].astype(F32), v_ref)):
            swapped = pltpu.roll(src, HEAD_DIM, 1)
            o_ref[0, :, 0:LANES] = jnp.where(lo, src, swapped).astype(BF16)
            o_ref[0, :, LANES:] = jnp.where(lo, swapped, src).astype(BF16)

    row = lambda w: pl.BlockSpec((tm, w), lambda i: (i, 0))
    return pl.pallas_call(
        body, name=name, grid=(s // tm,), in_specs=[row(proj.shape[1]), row(LANES), row(LANES), row(LANES)],
        out_specs=[_folded_spec(1, tm), _folded_spec(1, tm, 2 * LANES), _folded_spec(1, tm, 2 * LANES)],
        out_shape=[_folded_shape(s, 1, BF16), _folded_shape(s, 1, BF16, 2 * LANES), _folded_shape(s, 1, BF16, 2 * LANES)],
        compiler_params=_params("parallel"),
    )(proj, *tabs)


def _qkv_post_even(dq, dk, dv, dglu, tabs, name):
    s = dglu.shape[0]
    tm = 512

    def body(dq_ref, dk_ref, dv_ref, dr_ref, c_ref, a_ref, b_ref, o_ref):
        c, a, b = c_ref[...], -a_ref[...], -b_ref[...]
        o_ref[:, 0:ATTN_W] = _rot(dq_ref[0].astype(F32), _wide(c, ATTN_W), _wide(a, ATTN_W), _wide(b, ATTN_W)).astype(BF16)
        lo = _low_lanes(tm)
        merged = []
        for ref in (dk_ref, dv_ref):
            first, second = ref[0, :, 0:LANES].astype(F32), ref[0, :, LANES:].astype(F32)
            merged.append(jnp.where(lo, first + pltpu.roll(first, HEAD_DIM, 1), second + pltpu.roll(second, HEAD_DIM, 1)))
        o_ref[:, 512:640] = _rot(merged[0], c, a, b).astype(BF16)
        o_ref[:, 640:768] = merged[1].astype(BF16)
        o_ref[:, 768:] = dr_ref[...]

    row = lambda w: pl.BlockSpec((tm, w), lambda i: (i, 0))
    return pl.pallas_call(
        body, name=name, grid=(s // tm,),
        in_specs=[_folded_spec(1, tm), _folded_spec(1, tm, 2 * LANES), _folded_spec(1, tm, 2 * LANES),
                  row(dglu.shape[1]), row(LANES), row(LANES), row(LANES)],
        out_specs=row(EVEN_IN), out_shape=jax.ShapeDtypeStruct((s, EVEN_IN), BF16),
        compiler_params=_params("parallel"),
    )(dq, dk, dv, dglu, *tabs)


def _qkv_prep_odd(proj, tabs, name):
    s = proj.shape[0]
    tm = 512

    def body(p_ref, c_ref, a_ref, b_ref, *rest):
        outs, sc_ref = rest[:-1], rest[-1]
        c, a, b = (_wide(t[...], ATTN_W) for t in (c_ref, a_ref, b_ref))
        for t in range(3):
            x = p_ref[:, t * ATTN_W:(t + 1) * ATTN_W].astype(F32)
            if t < 2:
                x = _rot(x, c, a, b)
            _fold_store(x, sc_ref, {d: outs[t * len(DILATIONS) + i] for i, d in enumerate(DILATIONS)})

    row = lambda w: pl.BlockSpec((tm, w), lambda i: (i, 0))
    return pl.pallas_call(
        body, name=name, grid=(s // tm,), in_specs=[row(proj.shape[1]), row(LANES), row(LANES), row(LANES)],
        out_specs=[_folded_spec(d, tm) for _ in range(3) for d in DILATIONS],
        out_shape=[_folded_shape(s, d, BF16) for _ in range(3) for d in DILATIONS],
        scratch_shapes=[pltpu.VMEM((N_PAIRS, tm, LANES), F32)],
        compiler_params=_params("parallel"),
    )(proj, *tabs)


def _qkv_post_odd(dqs, dks, dvs, dz, tabs, name):
    s = dz.shape[0]
    tm = 256
    nb = len(DILATIONS)

    def body(*refs):
        groups = (refs[:nb], refs[nb:2 * nb], refs[2 * nb:3 * nb])
        dz_ref, c_ref, a_ref, b_ref, o_ref, sc_ref = refs[3 * nb:]
        c, a, b = _wide(c_ref[...], ATTN_W), _wide(-a_ref[...], ATTN_W), _wide(-b_ref[...], ATTN_W)
        for t, group in enumerate(groups):
            for i, d in enumerate(DILATIONS):
                _unfold_load(group[i], sc_ref, d, add=i > 0)
            x = jnp.concatenate([sc_ref[p] for p in range(N_PAIRS)], axis=1)
            if t < 2:
                x = _rot(x, c, a, b)
            o_ref[:, t * ATTN_W:(t + 1) * ATTN_W] = x.astype(BF16)
        o_ref[:, 3 * ATTN_W:] = dz_ref[...]

    row = lambda w: pl.BlockSpec((tm, w), lambda i: (i, 0))
    return pl.pallas_call(
        body, name=name, grid=(s // tm,),
        in_specs=[_folded_spec(d, tm) for _ in range(3) for d in DILATIONS] + [row(dz.shape[1]), row(LANES), row(LANES), row(LANES)],
        out_specs=row(ODD_IN), out_shape=jax.ShapeDtypeStruct((s, ODD_IN), BF16),
        scratch_shapes=[pltpu.VMEM((N_PAIRS, tm, LANES), F32)],
        compiler_params=_params("parallel"),
    )(*dqs, *dks, *dvs, dz, *tabs)


def _fold_dout(dmix, name):
    s = dmix.shape[0]
    tm = 512
    ds = [d for d in DILATIONS if d > 1]

    def body(d_ref, *rest):
        outs, sc_ref = rest[:-1], rest[-1]
        _fold_store(d_ref[...], sc_ref, dict(zip(ds, outs)))

    return pl.pallas_call(
        body, name=name, grid=(s // tm,), in_specs=[pl.BlockSpec((tm, ATTN_W), lambda i: (i, 0))],
        out_specs=[_folded_spec(d, tm) for d in ds], out_shape=[_folded_shape(s, d, BF16) for d in ds],
        scratch_shapes=[pltpu.VMEM((N_PAIRS, tm, LANES), F32)],
        compiler_params=_params("parallel"),
    )(dmix)


def _window(j, i, tq):
    r0 = j * tq + i * BLOCK
    start = pl.multiple_of(jnp.maximum(r0 - BLOCK, 0), BLOCK)
    return pl.ds(start, 2 * BLOCK), r0 - start


def _band_valid(offset, max_dist):
    shape = (2 * BLOCK, 2 * BLOCK)
    dist = (lax.bitwise_and(lax.broadcasted_iota(jnp.int32, shape, 0), BLOCK - 1)
            - lax.broadcasted_iota(jnp.int32, shape, 1) + offset)
    return jnp.abs(2 * dist - max_dist) <= max_dist


def _stack_heads(lo, x):
    zero = jnp.zeros_like(x)
    return jnp.concatenate([jnp.where(lo, x, zero), jnp.where(lo, zero, x)], axis=0)


def _unstack_heads(lo, x):
    return jnp.where(lo, x[:BLOCK], x[BLOCK:])


NT = (((1,), (1,)), ((), ()))
TN = (((0,), (0,)), ((), ()))


def _attn_fwd(q, k, v, sinks, *, max_dist, name, emit_bf16=False, o_dtype=F32):
    d, sp, wq = q.shape
    nq, nk = wq // LANES, k.shape[2] // LANES
    kdiv = nq // nk
    tq = min(sp, 1024)
    nsub = tq // BLOCK
    has_sink = sinks is not None

    def body(*refs):
        refs = list(refs)
        sink_ref = refs.pop(0) if has_sink else None
        q_ref, k_ref, v_ref, o_ref, lse_ref = refs[:5]
        pair = pl.program_id(1)
        j = pl.program_id(2)
        lo = _low_lanes(BLOCK)
        if has_sink:
            first_head = lax.broadcasted_iota(jnp.int32, (2 * BLOCK, 1), 0) < BLOCK
            sk = jnp.where(first_head, sink_ref[2 * pair], sink_ref[2 * pair + 1])
        for i in range(nsub):
            win, offset = _window(j, i, tq)
            rows = slice(i * BLOCK, (i + 1) * BLOCK)
            kw = k_ref[0, win, :]
            vw = v_ref[0, win, :]
            s = lax.dot_general(_stack_heads(lo, q_ref[0, rows, :]), kw, NT, preferred_element_type=F32) * ATTN_SCALE
            s = jnp.where(_band_valid(offset, max_dist), s, NEG)
            m = jnp.max(s, axis=-1, keepdims=True)
            if has_sink:
                m = jnp.maximum(m, sk)
            p = jnp.exp(s - m)
            l = jnp.sum(p, axis=-1, keepdims=True)
            if has_sink:
                l = l + jnp.exp(sk - m)
            o2 = _unstack_heads(lo, jnp.dot(p.astype(BF16), vw, preferred_element_type=F32) / l)
            o_ref[0, rows, :] = o2.astype(o_ref.dtype)
            lse_ref[0, rows, :] = _unstack_heads(lo, m + jnp.log(l))
            if emit_bf16:
                refs[5][0, rows, :] = o2.astype(BF16)

    qspec = pl.BlockSpec((1, tq, LANES), lambda r, p, j: (r, j, p))
    kspec = pl.BlockSpec((1, sp, LANES), lambda r, p, j: (r, 0, p // kdiv))
    in_specs = [qspec, kspec, kspec]
    operands = [q, k, v]
    if has_sink:
        in_specs = [pl.BlockSpec(memory_space=pltpu.SMEM)] + in_specs
        operands = [sinks] + operands
    out_shape = [jax.ShapeDtypeStruct(q.shape, o_dtype), jax.ShapeDtypeStruct(q.shape, F32)]
    if emit_bf16:
        out_shape.append(jax.ShapeDtypeStruct(q.shape, BF16))
    return pl.pallas_call(
        body, name=name, grid=(d, nq, sp // tq), in_specs=in_specs, out_specs=[qspec] * len(out_shape),
        out_shape=out_shape, compiler_params=_params("parallel", "parallel", "arbitrary"),
    )(*operands)


def _attn_bwd(q, k, v, do, oo, lse, sinks, *, max_dist, name):
    d, sp, wq = q.shape
    wk = k.shape[2]
    nq, nk = wq // LANES, wk // LANES
    kdiv = nq // nk
    tq = min(sp, 1024)
    nsub = tq // BLOCK
    has_sink = sinks is not None

    def body(*refs):
        refs = list(refs)
        sink_ref = refs.pop(0) if has_sink else None
        q_ref, k_ref, v_ref, do_ref, oo_ref, lse_ref, dq_ref, dk_out, dv_out = refs[:9]
        dk_ref, dv_ref = refs[-2:]
        pk, g, j = pl.program_id(1), pl.program_id(2), pl.program_id(3)

        @pl.when((g == 0) & (j == 0))
        def _():
            dk_ref[...] = jnp.zeros_like(dk_ref)
            dv_ref[...] = jnp.zeros_like(dv_ref)

        lo = _low_lanes(BLOCK)
        if has_sink:
            first_head = lax.broadcasted_iota(jnp.int32, (2 * BLOCK, 1), 0) < BLOCK
            pair = pk * kdiv + g
            sk = jnp.where(first_head, sink_ref[2 * pair], sink_ref[2 * pair + 1])
            sink_acc = jnp.zeros((2 * BLOCK, LANES), F32)
        for i in range(nsub):
            win, offset = _window(j, i, tq)
            rows = slice(i * BLOCK, (i + 1) * BLOCK)
            kw = k_ref[0, win, :]
            vw = v_ref[0, win, :]
            do2 = do_ref[0, rows, :].astype(F32)
            qs = _stack_heads(lo, q_ref[0, rows, :])
            dos = _stack_heads(lo, do2.astype(BF16))
            prod = do2 * oo_ref[0, rows, :]
            delta = jnp.sum(_stack_heads(lo, prod), axis=-1, keepdims=True)
            lse2 = lse_ref[0, rows, :]
            lse_swapped = pltpu.roll(lse2, HEAD_DIM, 1)
            lse_st = jnp.concatenate([jnp.where(lo, lse2, lse_swapped), jnp.where(lo, lse_swapped, lse2)], axis=0)
            s = lax.dot_general(qs, kw, NT, preferred_element_type=F32) * ATTN_SCALE
            s = jnp.where(_band_valid(offset, max_dist), s, NEG)
            p = jnp.exp(s - jnp.tile(lse_st, (1, 2)))
            dv_ref[win, :] = lax.dot_general(p.astype(BF16), dos, TN, preferred_element_type=F32) + dv_ref[win, :]
            dp = lax.dot_general(dos, vw, NT, preferred_element_type=F32)
            ds = (p * (dp - delta) * ATTN_SCALE).astype(BF16)
            dq_ref[0, rows, :] = _unstack_heads(lo, jnp.dot(ds, kw, preferred_element_type=F32)).astype(BF16)
            dk_ref[win, :] = lax.dot_general(ds, qs, TN, preferred_element_type=F32) + dk_ref[win, :]
            if has_sink:
                sink_acc = sink_acc - jnp.exp(sk - lse_st) * delta

        @pl.when((g == kdiv - 1) & (j == sp // tq - 1))
        def _():
            dk_out[0] = dk_ref[...].astype(BF16)
            dv_out[0] = dv_ref[...].astype(BF16)

        if has_sink:
            dsink_ref = refs[9]

            @pl.when(j == 0)
            def _():
                dsink_ref[...] = jnp.zeros_like(dsink_ref)

            dsink_ref[0] += jnp.where(lo[0:1], jnp.sum(sink_acc[:BLOCK], axis=0, keepdims=True),
                                      jnp.sum(sink_acc[BLOCK:], axis=0, keepdims=True))

    def qmap(r, pk, g, j):
        return (r, j, pk * kdiv + g)

    def kmap(r, pk, g, j):
        return (r, 0, pk)

    qspec = pl.BlockSpec((1, tq, LANES), qmap)
    kspec = pl.BlockSpec((1, sp, LANES), kmap)
    in_specs = [qspec, kspec, kspec, qspec, qspec, qspec]
    operands = [q, k, v, do, oo, lse]
    out_specs = [qspec, kspec, kspec]
    out_shape = [jax.ShapeDtypeStruct((d, sp, wq), BF16), jax.ShapeDtypeStruct((d, sp, wk), BF16),
                 jax.ShapeDtypeStruct((d, sp, wk), BF16)]
    if has_sink:
        in_specs = [pl.BlockSpec(memory_space=pltpu.SMEM)] + in_specs
        operands = [sinks] + operands
        out_specs.append(pl.BlockSpec((1, 1, LANES), lambda r, pk, g, j: (pk * kdiv + g, 0, 0)))
        out_shape.append(jax.ShapeDtypeStruct((nq, 1, LANES), F32))
    nsteps = sp // tq
    return pl.pallas_call(
        body, name=name, grid=(d, nk, kdiv, nsteps), in_specs=in_specs, out_specs=out_specs, out_shape=out_shape,
        scratch_shapes=[pltpu.VMEM((sp, LANES), F32), pltpu.VMEM((sp, LANES), F32)],
        compiler_params=_params("parallel", "parallel", "arbitrary", "arbitrary"),
    )(*operands)


def _combine(outs, lses, name):
    s = outs[0].shape[1]
    tm = 512
    nb = len(DILATIONS)
    ds = [d for d in DILATIONS if d > 1]

    def body(*refs):
        o_refs, l_refs = refs[:nb], refs[nb:2 * nb]
        cb_ref, c_ref, lse_ref = refs[2 * nb:2 * nb + 3]
        folded = refs[2 * nb + 3:2 * nb + 3 + 2 * len(ds)]
        scratch = refs[2 * nb + 3 + 2 * len(ds):]
        so = {1: None}
        sl = {1: None}
        for i, d in enumerate(ds):
            so[d], sl[d] = scratch[2 * i], scratch[2 * i + 1]
            _unfold_load(o_refs[1 + i], so[d], d)
            _unfold_load(l_refs[1 + i], sl[d], d)
        for p in range(N_PAIRS):
            pb = _pair_block(p)
            ls = [l_refs[0][0, :, pb]] + [sl[d][p] for d in ds]
            os_ = [o_refs[0][0, :, pb].astype(F32)] + [so[d][p] for d in ds]
            m = ls[0]
            for t in ls[1:]:
                m = jnp.maximum(m, t)
            ws = [jnp.exp(t - m) for t in ls]
            tot = ws[0]
            for t in ws[1:]:
                tot = tot + t
            acc = ws[0] * os_[0]
            for w, o in zip(ws[1:], os_[1:]):
                acc = acc + w * o
            cmix = acc / tot
            lse = m + jnp.log(tot)
            cb_ref[:, pb] = cmix.astype(BF16)
            c_ref[0, :, pb] = cmix
            lse_ref[0, :, pb] = lse
            so[ds[0]][p] = cmix
            sl[ds[0]][p] = lse
        for i, d in enumerate(ds):
            for r in range(d):
                for p in range(N_PAIRS):
                    rows = pl.ds(r, tm // d, stride=d)
                    folded[2 * i][r, :, _pair_block(p)] = so[ds[0]][p, rows, :]
                    folded[2 * i + 1][r, :, _pair_block(p)] = sl[ds[0]][p, rows, :]

    in_specs = [_folded_spec(d, tm) for _ in range(2) for d in DILATIONS]
    out_specs = [pl.BlockSpec((tm, ATTN_W), lambda i: (i, 0)), _folded_spec(1, tm), _folded_spec(1, tm)]
    out_shape = [jax.ShapeDtypeStruct((s, ATTN_W), BF16), _folded_shape(s, 1, F32), _folded_shape(s, 1, F32)]
    for d in ds:
        out_specs += [_folded_spec(d, tm)] * 2
        out_shape += [_folded_shape(s, d, F32)] * 2
    return pl.pallas_call(
        body, name=name, grid=(s // tm,), in_specs=in_specs, out_specs=out_specs, out_shape=out_shape,
        scratch_shapes=[pltpu.VMEM((N_PAIRS, tm, LANES), F32)] * (2 * len(ds)),
        compiler_params=_params("parallel"),
    )(*outs, *lses)


GLU_A = slice(768, 1280)
GLU_B = slice(1280, 1792)
EVEN_IN = 1792
ODD_IN = 2560
CONV_CH = 512


def _shifted_copies(xs_ref):
    rows = xs_ref.shape[1] - 8
    for b in range(1, 8):
        xs_ref[b, 0:rows, :] = xs_ref[0, pl.ds(b, rows), :]


def _shifted_rows(xs_ref, start):
    return xs_ref[start % 8, pl.ds(start - start % 8, CONV_ROWS), :]


def _glu(p_ref):
    return p_ref[:, GLU_A].astype(F32) * _sigmoid(p_ref[:, GLU_B].astype(F32))


def _conv_fwd(proj, w, b, ln_g, ln_b, name):
    s = proj.shape[0]
    tm = 512
    nh = tm // CONV_HALO
    lead = CONV_HALO - (CONV_WIDTH - 1)

    def body(p_ref, ph_ref, w_ref, b_ref, g_ref, bb_ref, y_ref, o_ref, xs_ref):
        xs_ref[0, CONV_HALO:, :] = _glu(p_ref)
        xs_ref[0, 0:CONV_HALO, :] = jnp.where(pl.program_id(0) > 0, _glu(ph_ref), 0.0)
        _shifted_copies(xs_ref)
        for c0 in range(0, tm, CONV_ROWS):
            acc = jnp.zeros((CONV_ROWS, CONV_CH), F32) + b_ref[...]
            for j in range(CONV_WIDTH):
                acc = acc + _shifted_rows(xs_ref, lead + j + c0) * w_ref[j:j + 1, :]
            y_ref[c0:c0 + CONV_ROWS, :] = acc
            mu = jnp.mean(acc, axis=-1, keepdims=True)
            xc = acc - mu
            var = jnp.mean(xc * xc, axis=-1, keepdims=True)
            zz = xc * lax.rsqrt(var + LN_EPS) * g_ref[...] + bb_ref[...]
            o_ref[c0:c0 + CONV_ROWS, :] = (zz * _sigmoid(zz)).astype(BF16)

    def const(a):
        return pl.BlockSpec(a.shape, lambda i: (0, 0))

    return pl.pallas_call(
        body, name=name, grid=(s // tm,),
        in_specs=[pl.BlockSpec((tm, EVEN_IN), lambda i: (i, 0)),
                  pl.BlockSpec((CONV_HALO, EVEN_IN), lambda i: (jnp.maximum(i * nh - 1, 0), 0)),
                  const(w), const(b), const(ln_g), const(ln_b)],
        out_specs=[pl.BlockSpec((tm, CONV_CH), lambda i: (i, 0)), pl.BlockSpec((tm, CONV_CH), lambda i: (i, 0))],
        out_shape=[jax.ShapeDtypeStruct((s, CONV_CH), F32), jax.ShapeDtypeStruct((s, CONV_CH), BF16)],
        scratch_shapes=[pltpu.VMEM((8, tm + CONV_HALO, CONV_CH), F32)],
        compiler_params=_params("arbitrary"),
    )(proj, proj, w, b, ln_g, ln_b)


def _conv_tail_bwd(dmix, yconv, ln_g, ln_b, name):
    def body(d_ref, y_ref, g_ref, b_ref, dy_ref, dg_ref, db_ref, dcb_ref):
        @pl.when(_first_step())
        def _():
            dg_ref[...] = jnp.zeros_like(dg_ref)
            db_ref[...] = jnp.zeros_like(db_ref)
            dcb_ref[...] = jnp.zeros_like(dcb_ref)

        y = y_ref[...]
        g = g_ref[...]
        mu = jnp.mean(y, axis=-1, keepdims=True)
        xc = y - mu
        rstd = lax.rsqrt(jnp.mean(xc * xc, axis=-1, keepdims=True) + LN_EPS)
        xh = xc * rstd
        zz = xh * g + b_ref[...]
        sg = _sigmoid(zz)
        dzz = d_ref[:, CONV_CH:] * sg * (1.0 + zz * (1.0 - sg))
        dg_ref[...] += jnp.sum(dzz * xh, axis=0, keepdims=True)
        db_ref[...] += jnp.sum(dzz, axis=0, keepdims=True)
        dxh = dzz * g
        dy = rstd * (dxh - jnp.mean(dxh, axis=-1, keepdims=True) - xh * jnp.mean(dxh * xh, axis=-1, keepdims=True))
        dcb_ref[...] += jnp.sum(dy, axis=0, keepdims=True)
        dy_ref[...] = dy

    vec = ((1, CONV_CH), F32)
    return _rows(body, name, 512, [dmix, yconv], [ln_g, ln_b], [(CONV_CH, F32)], [vec, vec, vec])


def _conv_bwd(proj, dy, w, name):
    s = proj.shape[0]
    tm = 512
    nh = tm // CONV_HALO
    nsteps = s // tm
    lead = CONV_HALO - (CONV_WIDTH - 1)

    def body(p_ref, ph_ref, dy_ref, dyn_ref, w_ref, dglu_ref, dw_ref, xf_ref, dyf_ref):
        i = pl.program_id(0)

        @pl.when(i == 0)
        def _():
            dw_ref[...] = jnp.zeros_like(dw_ref)

        ga = p_ref[:, GLU_A].astype(F32)
        sgb = _sigmoid(p_ref[:, GLU_B].astype(F32))
        xf_ref[0, CONV_HALO:, :] = ga * sgb
        xf_ref[0, 0:CONV_HALO, :] = jnp.where(i > 0, _glu(ph_ref), 0.0)
        _shifted_copies(xf_ref)
        dyf_ref[0, 0:tm, :] = dy_ref[...]
        dyf_ref[0, tm:, :] = jnp.where(i < nsteps - 1, dyn_ref[...], 0.0)
        _shifted_copies(dyf_ref)
        for c0 in range(0, tm, CONV_ROWS):
            rows = slice(c0, c0 + CONV_ROWS)
            acc = jnp.zeros((CONV_ROWS, CONV_CH), F32)
            for j in range(CONV_WIDTH):
                acc = acc + _shifted_rows(dyf_ref, CONV_WIDTH - 1 - j + c0) * w_ref[j:j + 1, :]
            a_c, s_c = ga[rows, :], sgb[rows, :]
            dglu_ref[rows, 0:CONV_CH] = (acc * s_c).astype(BF16)
            dglu_ref[rows, CONV_CH:] = (acc * a_c * s_c * (1.0 - s_c)).astype(BF16)
        for j in range(CONV_WIDTH):
            part = jnp.zeros((8, CONV_CH), F32)
            for c0 in range(0, tm, CONV_ROWS):
                prod = dy_ref[c0:c0 + CONV_ROWS, :] * _shifted_rows(xf_ref, lead + j + c0)
                part = part + jnp.sum(prod.reshape(CONV_ROWS // 8, 8, CONV_CH), axis=0)
            dw_ref[j:j + 1, :] += jnp.sum(part, axis=0, keepdims=True)

    return pl.pallas_call(
        body, name=name, grid=(nsteps,),
        in_specs=[pl.BlockSpec((tm, EVEN_IN), lambda i: (i, 0)),
                  pl.BlockSpec((CONV_HALO, EVEN_IN), lambda i: (jnp.maximum(i * nh - 1, 0), 0)),
                  pl.BlockSpec((tm, CONV_CH), lambda i: (i, 0)),
                  pl.BlockSpec((CONV_HALO, CONV_CH), lambda i: (jnp.minimum((i + 1) * nh, s // CONV_HALO - 1), 0)),
                  pl.BlockSpec(w.shape, lambda i: (0, 0))],
        out_specs=[pl.BlockSpec((tm, 2 * CONV_CH), lambda i: (i, 0)), pl.BlockSpec(w.shape, lambda i: (0, 0))],
        out_shape=[jax.ShapeDtypeStruct((s, 2 * CONV_CH), BF16), jax.ShapeDtypeStruct(w.shape, F32)],
        scratch_shapes=[pltpu.VMEM((8, tm + CONV_HALO, CONV_CH), F32), pltpu.VMEM((8, tm + CONV_HALO, CONV_CH), F32)],
        compiler_params=_params("arbitrary"),
    )(proj, proj, dy, dy, w)


GATE_Z = slice(1536, 2560)
D_CH = 512
GELU_C = math.sqrt(2.0 / math.pi)
GELU_K = 0.044715


def _gelu_parts(z):
    t = jnp.tanh(GELU_C * (z + GELU_K * z * z * z))
    return 0.5 * z * (1.0 + t), t


def _lane_group(rows):
    return lax.broadcasted_iota(jnp.int32, (rows, D_CH), 1) // HEAD_DIM


def _tril_mask():
    return lax.broadcasted_iota(jnp.int32, (BLOCK, BLOCK), 0) >= lax.broadcasted_iota(jnp.int32, (BLOCK, BLOCK), 1)


def _layer_norm_parts(x):
    mu = jnp.mean(x, axis=-1, keepdims=True)
    xc = x - mu
    rstd = lax.rsqrt(jnp.mean(xc * xc, axis=-1, keepdims=True) + LN_EPS)
    return xc * rstd, rstd


def _gate_fwd(proj, ln_g, ln_b, w_sp, sb_t, name):
    tm = 512

    def body(p_ref, g_ref, b_ref, w_ref, sb_ref, mixed_ref, out_ref):
        zz, _ = _gelu_parts(p_ref[:, GATE_Z].astype(F32))
        u = zz[:, :D_CH]
        xh, _ = _layer_norm_parts(zz[:, D_CH:])
        gn = (xh * g_ref[...] + b_ref[...]).astype(BF16)
        grp = _lane_group(BLOCK)
        tri = _tril_mask()
        ws = [jnp.where(tri, w_ref[gi], 0.0).astype(BF16) for gi in range(N_GROUPS)]
        bias = jnp.zeros((BLOCK, D_CH), F32)
        for gi in range(N_GROUPS):
            bias = jnp.where(grp == gi, sb_ref[:, gi:gi + 1], bias)
        for ch in range(tm // BLOCK):
            rows = slice(ch * BLOCK, (ch + 1) * BLOCK)
            gc = gn[rows, :]
            mixed = bias
            for gi in range(N_GROUPS):
                r = jnp.dot(ws[gi], gc, preferred_element_type=F32)
                mixed = jnp.where(grp == gi, r + bias, mixed)
            mixed_ref[rows, :] = mixed
            out_ref[rows, :] = (u[rows, :] * mixed).astype(BF16)

    return _rows(body, name, tm, [proj], [ln_g, ln_b, w_sp, sb_t], [(D_CH, F32), (D_CH, BF16)])


def _gate_bwd(dmix, proj, mixed, ln_g, ln_b, w_sp, name):
    tm = 512

    def body(d_ref, p_ref, m_ref, g_ref, b_ref, w_ref, dz_ref, dg_ref, db_ref, dw_ref, dsb_ref, dgn_ref):
        @pl.when(_first_step())
        def _():
            dg_ref[...] = jnp.zeros_like(dg_ref)
            db_ref[...] = jnp.zeros_like(db_ref)
            dw_ref[...] = jnp.zeros_like(dw_ref)
            dsb_ref[...] = jnp.zeros_like(dsb_ref)

        z = p_ref[:, GATE_Z].astype(F32)
        zz, t = _gelu_parts(z)
        u = zz[:, :D_CH]
        xh, rstd = _layer_norm_parts(zz[:, D_CH:])
        g = g_ref[...]
        gn = (xh * g + b_ref[...]).astype(BF16)
        dd = d_ref[:, D_CH:]
        du = dd * m_ref[...]
        dm = dd * u
        grp = _lane_group(BLOCK)
        tri = _tril_mask()
        ws = [jnp.where(tri, w_ref[gi], 0.0).astype(BF16) for gi in range(N_GROUPS)]
        gsel = (lax.broadcasted_iota(jnp.int32, (N_GROUPS, D_CH), 1) // HEAD_DIM
                == lax.broadcasted_iota(jnp.int32, (N_GROUPS, D_CH), 0)).astype(F32)
        for ch in range(tm // BLOCK):
            rows = slice(ch * BLOCK, (ch + 1) * BLOCK)
            dmc = dm[rows, :]
            dmb = dmc.astype(BF16)
            gc = gn[rows, :]
            dgn = jnp.zeros((BLOCK, D_CH), F32)
            for gi in range(N_GROUPS):
                r = lax.dot_general(ws[gi], dmb, TN, preferred_element_type=F32)
                dgn = jnp.where(grp == gi, r, dgn)
                dmg = jnp.where(grp == gi, dmb, jnp.zeros_like(dmb))
                dwg = lax.dot_general(dmg, gc, NT, preferred_element_type=F32)
                dw_ref[gi] += jnp.where(tri, dwg, 0.0)
            dsb_ref[...] += lax.dot_general(gsel, dmc, NT, preferred_element_type=F32, precision=lax.Precision.HIGHEST)
            dgn_ref[rows, :] = dgn
        dgn = dgn_ref[...]
        db_ref[...] += jnp.sum(dgn, axis=0, keepdims=True)
        dg_ref[...] += jnp.sum(dgn * xh, axis=0, keepdims=True)
        dxh = dgn * g
        dgp = rstd * (dxh - jnp.mean(dxh, axis=-1, keepdims=True) - xh * jnp.mean(dxh * xh, axis=-1, keepdims=True))
        dgelu = 0.5 * (1.0 + t) + 0.5 * z * (1.0 - t * t) * GELU_C * (1.0 + 3.0 * GELU_K * z * z)
        dz_ref[:, 0:D_CH] = (du * dgelu[:, :D_CH]).astype(BF16)
        dz_ref[:, D_CH:] = (dgp * dgelu[:, D_CH:]).astype(BF16)

    s = proj.shape[0]
    tiled = [dmix, proj, mixed]
    consts = [ln_g, ln_b, w_sp]
    in_specs = [pl.BlockSpec((tm, a.shape[1]), lambda i: (i, 0)) for a in tiled]
    in_specs += [pl.BlockSpec(a.shape, lambda i, nd=a.ndim: (0,) * nd) for a in consts]
    vec = (1, D_CH)
    acc_shapes = [vec, vec, w_sp.shape, (N_GROUPS, BLOCK)]
    return pl.pallas_call(
        body, name=name, grid=(s // tm,), in_specs=in_specs,
        out_specs=[pl.BlockSpec((tm, 2 * D_CH), lambda i: (i, 0))]
        + [pl.BlockSpec(sh, lambda i, nd=len(sh): (0,) * nd) for sh in acc_shapes],
        out_shape=[jax.ShapeDtypeStruct((s, 2 * D_CH), BF16)] + [jax.ShapeDtypeStruct(sh, F32) for sh in acc_shapes],
        scratch_shapes=[pltpu.VMEM((tm, D_CH), F32)],
        compiler_params=_params("arbitrary"),
    )(*tiled, *consts)


def _adam_update(w, g, m, v):
    nm = ADAM_B1 * m + (1.0 - ADAM_B1) * g
    nv = ADAM_B2 * v + (1.0 - ADAM_B2) * (g * g)
    m_hat = nm / (1.0 - ADAM_B1 ** ADAM_STEP)
    v_hat = nv / (1.0 - ADAM_B2 ** ADAM_STEP)
    return -ADAM_LR * (m_hat / (jnp.sqrt(v_hat) + ADAM_EPS) + ADAM_WD * w), nm, nv


def _adamw(w, g, m, v, name):
    rows, cols = w.shape
    tm = _tile(rows, 512, 8)

    def body(w_ref, g_ref, m_ref, v_ref, d_ref, nm_ref, nv_ref):
        d_ref[...], nm_ref[...], nv_ref[...] = _adam_update(w_ref[...], g_ref[...], m_ref[...], v_ref[...])

    return _rows(body, name, tm, [w, g, m, v], [], [(cols, F32)] * 3)


def _ordered_sum(parts, name):
    n, rows, cols = parts.shape
    tm = _tile(rows, 512, 16 if parts.dtype == BF16 else 8)

    def body(p_ref, o_ref):
        acc = p_ref[0].astype(F32)
        for k in range(1, n):
            acc = acc + p_ref[k].astype(F32)
        o_ref[...] = acc

    return pl.pallas_call(body, name=name, grid=(rows // tm,),
                          in_specs=[pl.BlockSpec((n, tm, cols), lambda i: (0, i, 0))],
                          out_specs=pl.BlockSpec((tm, cols), lambda i: (i, 0)),
                          out_shape=jax.ShapeDtypeStruct((rows, cols), F32), compiler_params=_params("parallel"))(parts)


ANY = pl.BlockSpec(memory_space=pl.ANY)


def _position():
    x, y, c = lax.axis_index("x"), lax.axis_index("y"), lax.axis_index("c")
    other_chips = [(1 - x, y), (x, 1 - y), (1 - x, 1 - y)]
    return x, y, c, other_chips


def _remote(src, dst, send_sem, recv_sem, to):
    return pltpu.make_async_remote_copy(src_ref=src, dst_ref=dst, send_sem=send_sem, recv_sem=recv_sem,
                                        device_id=to, device_id_type=MESH)


STAGE_ROWS = 736


def _staged_copies(copies, buf, in_sems, out_sems):
    n = len(copies)

    def into(u):
        src = copies[u][0]
        return pltpu.make_async_copy(src, buf.at[u % 2, pl.ds(0, src.shape[0]), :], in_sems.at[u % 2])

    def out_of(u):
        dst = copies[u][1]
        return pltpu.make_async_copy(buf.at[u % 2, pl.ds(0, dst.shape[0]), :], dst, out_sems.at[u % 2])

    into(0).start()
    for u in range(n):
        into(u).wait()
        out_of(u).start()
        if u + 1 < n:
            if u >= 1:
                out_of(u - 1).wait()
            into(u + 1).start()
    if n >= 2:
        out_of(n - 2).wait()
    out_of(n - 1).wait()


def _stage_scratch(dtype, cols):
    return [pltpu.VMEM((2, STAGE_ROWS, cols), dtype), pltpu.SemaphoreType.DMA((2,)), pltpu.SemaphoreType.DMA((2,))]


def _row_chunks(rows):
    return [(r, min(STAGE_ROWS, rows - r)) for r in range(0, rows, STAGE_ROWS)]


def _gather_chips(shard, name):
    rows, cols = shard.shape
    half = rows // 2

    def body(in_ref, out_ref, send_sems, recv_sems, buf, in_sems, out_sems):
        x, y, c, chips = _position()
        me = 2 * x + y
        sibling = (x, y, 1 - c)

        def slab(chip, h):
            return out_ref.at[chip, pl.ds(h * half, half), :]

        first = [_remote(in_ref.at[pl.ds(c * half, half), :], slab(me, c), send_sems.at[j], recv_sems.at[j], (cx, cy, c))
                 for j, (cx, cy) in enumerate(chips)]
        for cp in first:
            cp.start()
        _staged_copies([(in_ref.at[pl.ds(r, n), :], out_ref.at[me, pl.ds(r, n), :]) for r, n in _row_chunks(rows)],
                       buf, in_sems, out_sems)
        passed = []
        for j, (cx, cy) in enumerate(chips):
            got = slab(2 * cx + cy, c)
            _remote(got, got, send_sems.at[j], recv_sems.at[j], sibling).wait_recv()
            cp = _remote(got, got, send_sems.at[3 + j], recv_sems.at[3 + j], sibling)
            cp.start()
            passed.append(cp)
        for j, (cx, cy) in enumerate(chips):
            got = slab(2 * cx + cy, 1 - c)
            _remote(got, got, send_sems.at[3 + j], recv_sems.at[3 + j], sibling).wait_recv()
        for cp in first + passed:
            cp.wait_send()

    return pl.pallas_call(
        body, name=name, in_specs=[ANY], out_specs=ANY,
        out_shape=jax.ShapeDtypeStruct((N_CHIPS, rows, cols), shard.dtype),
        scratch_shapes=[pltpu.SemaphoreType.DMA((6,)), pltpu.SemaphoreType.DMA((6,))] + _stage_scratch(shard.dtype, cols),
        compiler_params=pltpu.CompilerParams(vmem_limit_bytes=VMEM_LIMIT),
    )(shard)


HBM = pl.BlockSpec(memory_space=pltpu.HBM)
SEM = pl.BlockSpec(memory_space=pltpu.SEMAPHORE)
SIDE_EFFECT = pltpu.SideEffectType.DATAFLOW_SIDE_EFFECTING


def _ici_copies(in_ref, land_ref, send_sems, recv_sems, half):
    x, y, c, chips = _position()
    mine = pl.ds(c * half, half)
    sends = [_remote(in_ref.at[mine, :], land_ref.at[2 * x + y, mine, :], send_sems.at[j], recv_sems.at[j], (cx, cy, c))
             for j, (cx, cy) in enumerate(chips)]
    arrivals = [_remote(in_ref.at[mine, :], land_ref.at[2 * cx + cy, mine, :], send_sems.at[j], recv_sems.at[j], (cx, cy, c))
                for j, (cx, cy) in enumerate(chips)]
    return sends, arrivals


def _gather_start(shard, after, name):
    rows, cols = shard.shape

    def body(in_ref, land_ref, after_ref, send_sems, recv_sems, in_thru, land_thru, token):
        sends, _ = _ici_copies(in_ref, land_ref, send_sems, recv_sems, rows // 2)
        for cp in sends:
            cp.start()
        token[...] = jnp.zeros_like(token)

    land = lax.empty((N_CHIPS, rows, cols), shard.dtype)
    return pl.pallas_call(
        body, name=name,
        out_shape=(pltpu.SemaphoreType.DMA((3,)), pltpu.SemaphoreType.DMA((3,)), pltpu.HBM(shard.shape, shard.dtype),
                   pltpu.HBM(land.shape, land.dtype), jax.ShapeDtypeStruct((8, LANES), F32)),
        in_specs=(HBM, HBM, ANY), out_specs=(SEM, SEM, HBM, HBM, pl.BlockSpec(memory_space=pltpu.VMEM)),
        input_output_aliases={0: 2, 1: 3},
        compiler_params=pltpu.CompilerParams(has_side_effects=SIDE_EFFECT),
    )(pltpu.with_memory_space_constraint(shard, pltpu.HBM), pltpu.with_memory_space_constraint(land, pltpu.HBM), after)


def _gather_wait(send_sems, recv_sems, shard, land, after, name):
    rows = shard.shape[0]

    def body(in_ref, land_ref, send_sems, recv_sems, after_ref, in_out, land_out):
        sends, arrivals = _ici_copies(in_ref, land_ref, send_sems, recv_sems, rows // 2)
        for cp in sends:
            cp.wait_send()
        for cp in arrivals:
            cp.wait_recv()

    return pl.pallas_call(
        body, name=name, out_shape=(pltpu.HBM(shard.shape, shard.dtype), pltpu.HBM(land.shape, land.dtype)),
        in_specs=(HBM, HBM, SEM, SEM, ANY), out_specs=(HBM, HBM), input_output_aliases={0: 0, 1: 1},
        compiler_params=pltpu.CompilerParams(has_side_effects=SIDE_EFFECT),
    )(shard, land, send_sems, recv_sems, after)


def _gather_finish(shard, land, name):
    rows, cols = shard.shape
    half = rows // 2

    def body(in_ref, land_ref, out_ref, send_sems, recv_sems, buf, in_sems, out_sems):
        x, y, c, chips = _position()
        me = 2 * x + y
        sibling = (x, y, 1 - c)

        def slab(chip, h):
            return out_ref.at[chip, pl.ds(h * half, half), :]

        passed = [_remote(slab(2 * cx + cy, c), slab(2 * cx + cy, c), send_sems.at[j], recv_sems.at[j], sibling)
                  for j, (cx, cy) in enumerate(chips)]
        for cp in passed:
            cp.start()
        _staged_copies([(in_ref.at[pl.ds(r, n), :], out_ref.at[me, pl.ds(r, n), :]) for r, n in _row_chunks(rows)],
                       buf, in_sems, out_sems)
        for j, (cx, cy) in enumerate(chips):
            got = slab(2 * cx + cy, 1 - c)
            _remote(got, got, send_sems.at[j], recv_sems.at[j], sibling).wait_recv()
        for cp in passed:
            cp.wait_send()

    return pl.pallas_call(
        body, name=name, in_specs=[ANY, ANY], out_specs=ANY, out_shape=jax.ShapeDtypeStruct(land.shape, land.dtype),
        input_output_aliases={1: 0},
        scratch_shapes=[pltpu.SemaphoreType.DMA((3,)), pltpu.SemaphoreType.DMA((3,))] + _stage_scratch(shard.dtype, cols),
        compiler_params=pltpu.CompilerParams(vmem_limit_bytes=VMEM_LIMIT),
    )(shard, land)


def _gather_devices(block, name):
    rows, cols = block.shape

    def body(in_ref, out_ref, send_sems, recv_sems, local_sem):
        x, y, c, chips = _position()
        sibling = (x, y, 1 - c)

        def slot(px, py, pc):
            return out_ref.at[4 * px + 2 * py + pc]

        mine = pltpu.make_async_copy(in_ref, slot(x, y, c), local_sem)
        mine.start()
        first = [_remote(in_ref, slot(x, y, c), send_sems.at[0], recv_sems.at[0], sibling)]
        first += [_remote(in_ref, slot(x, y, c), send_sems.at[1 + j], recv_sems.at[1 + j], (cx, cy, c))
                  for j, (cx, cy) in enumerate(chips)]
        for cp in first:
            cp.start()
        passed = []
        for j, (cx, cy) in enumerate(chips):
            got = slot(cx, cy, c)
            _remote(got, got, send_sems.at[1 + j], recv_sems.at[1 + j], sibling).wait_recv()
            cp = _remote(got, got, send_sems.at[4 + j], recv_sems.at[4 + j], sibling)
            cp.start()
            passed.append(cp)
        got = slot(x, y, 1 - c)
        _remote(got, got, send_sems.at[0], recv_sems.at[0], sibling).wait_recv()
        for j, (cx, cy) in enumerate(chips):
            got = slot(cx, cy, 1 - c)
            _remote(got, got, send_sems.at[4 + j], recv_sems.at[4 + j], sibling).wait_recv()
        for cp in first + passed:
            cp.wait_send()
        mine.wait()

    return pl.pallas_call(
        body, name=name, in_specs=[ANY], out_specs=ANY,
        out_shape=jax.ShapeDtypeStruct((N_DEV, rows, cols), block.dtype),
        scratch_shapes=[pltpu.SemaphoreType.DMA((7,)), pltpu.SemaphoreType.DMA((7,)), pltpu.SemaphoreType.DMA],
    )(block)


def _pair_send(grads, name):
    n = len(grads)
    hs = [g.shape[2] for g in grads]
    offs = [sum(hs[:i]) for i in range(n)]
    cols = grads[0].shape[3]

    def body(*refs):
        g_refs = refs[:n]
        got_ref, send_sems, recv_sems = refs[n:]
        x, y, c, _ = _position()
        copies = [_remote(g_ref.at[:, 1 - c], got_ref.at[:, pl.ds(offs[i], hs[i]), :], send_sems.at[i], recv_sems.at[i],
                          (x, y, 1 - c)) for i, g_ref in enumerate(g_refs)]
        for cp in copies:
            cp.start()
        for cp in copies:
            cp.wait()

    return pl.pallas_call(
        body, name=name, in_specs=[ANY] * n, out_specs=ANY, out_shape=jax.ShapeDtypeStruct((N_CHIPS, sum(hs), cols), F32),
        scratch_shapes=[pltpu.SemaphoreType.DMA((n,)), pltpu.SemaphoreType.DMA((n,))],
    )(*grads)


def _pair_copies(g_refs, land_ref, send_sems, recv_sems):
    x, y, c, _ = _position()
    hs = [g.shape[2] for g in g_refs]
    offs = [sum(hs[:i]) for i in range(len(hs))]
    return [_remote(g_ref.at[:, 1 - c], land_ref.at[:, pl.ds(offs[i], hs[i]), :], send_sems.at[i], recv_sems.at[i],
                    (x, y, 1 - c)) for i, g_ref in enumerate(g_refs)]


def _pair_send_start(grads, name):
    n = len(grads)
    land = lax.empty((N_CHIPS, sum(g.shape[2] for g in grads), grads[0].shape[3]), F32)

    def body(*refs):
        for cp in _pair_copies(refs[:n], refs[n], refs[n + 1], refs[n + 2]):
            cp.start()
        refs[-1][...] = jnp.zeros_like(refs[-1])

    buffers = [*grads, land]
    return pl.pallas_call(
        body, name=name,
        out_shape=(pltpu.SemaphoreType.DMA((n,)), pltpu.SemaphoreType.DMA((n,)),
                   *[pltpu.HBM(b.shape, b.dtype) for b in buffers], jax.ShapeDtypeStruct((8, LANES), F32)),
        in_specs=(HBM,) * (n + 1), out_specs=(SEM, SEM, *(HBM,) * (n + 1), pl.BlockSpec(memory_space=pltpu.VMEM)),
        input_output_aliases={i: 2 + i for i in range(n + 1)},
        compiler_params=pltpu.CompilerParams(has_side_effects=SIDE_EFFECT),
    )(*[pltpu.with_memory_space_constraint(b, pltpu.HBM) for b in buffers])


def _pair_send_wait(send_sems, recv_sems, buffers, after, name):
    n = len(buffers) - 1

    def body(*refs):
        for cp in _pair_copies(refs[:n], refs[n], refs[n + 1], refs[n + 2]):
            cp.wait_send()
            cp.wait_recv()

    return pl.pallas_call(
        body, name=name, out_shape=tuple(pltpu.HBM(b.shape, b.dtype) for b in buffers),
        in_specs=(*(HBM,) * (n + 1), SEM, SEM, ANY), out_specs=(HBM,) * (n + 1),
        input_output_aliases={i: i for i in range(n + 1)},
        compiler_params=pltpu.CompilerParams(has_side_effects=SIDE_EFFECT),
    )(*buffers, send_sems, recv_sems, after)


def _pair_add(grads, got, name):
    n = len(grads)
    hs = [g.shape[2] for g in grads]
    offs = [sum(hs[:i]) for i in range(n)]
    cols = grads[0].shape[3]
    hmax = max(hs)
    units = [(i, k) for k in range(N_CHIPS) for i in range(n)]

    def body(*refs):
        g_refs = refs[:n]
        got_ref, out_ref, a_buf, b_buf, o_buf, a_sems, b_sems, o_sems = refs[n:]
        c = lax.axis_index("c")

        def loads(u):
            i, k = units[u]
            slot, rows = u % 2, pl.ds(0, hs[i])
            return (pltpu.make_async_copy(g_refs[i].at[k, c], a_buf.at[slot, rows, :], a_sems.at[slot]),
                    pltpu.make_async_copy(got_ref.at[k, pl.ds(offs[i], hs[i]), :], b_buf.at[slot, rows, :], b_sems.at[slot]))

        def store(u):
            i, k = units[u]
            return pltpu.make_async_copy(o_buf.at[u % 2, pl.ds(0, hs[i]), :], out_ref.at[k, pl.ds(offs[i], hs[i]), :],
                                         o_sems.at[u % 2])

        for cp in loads(0):
            cp.start()
        for u, (i, k) in enumerate(units):
            if u + 1 < len(units):
                for cp in loads(u + 1):
                    cp.start()
            for cp in loads(u):
                cp.wait()
            if u >= 2:
                store(u - 2).wait()
            rows = pl.ds(0, hs[i])
            o_buf[u % 2, rows, :] = (a_buf[u % 2, rows, :] + b_buf[u % 2, rows, :]).astype(BF16)
            store(u).start()
        store(len(units) - 2).wait()
        store(len(units) - 1).wait()

    return pl.pallas_call(
        body, name=name, in_specs=[ANY] * (n + 1), out_specs=ANY,
        out_shape=jax.ShapeDtypeStruct((N_CHIPS, sum(hs), cols), BF16),
        scratch_shapes=[pltpu.VMEM((2, hmax, cols), F32), pltpu.VMEM((2, hmax, cols), F32), pltpu.VMEM((2, hmax, cols), BF16),
                        pltpu.SemaphoreType.DMA((2,)), pltpu.SemaphoreType.DMA((2,)), pltpu.SemaphoreType.DMA((2,))],
        compiler_params=pltpu.CompilerParams(vmem_limit_bytes=VMEM_LIMIT),
    )(*grads, got)


def _chip_exchange(parts, name):
    _, rows, cols = parts.shape

    def body(in_ref, out_ref, send_sems, recv_sems):
        x, y, c, chips = _position()
        sent = [_remote(in_ref.at[2 * cx + cy], out_ref.at[j], send_sems.at[j], recv_sems.at[j], (cx, cy, c))
                for j, (cx, cy) in enumerate(chips)]
        for cp in sent:
            cp.start()
        for cp in sent:
            cp.wait()

    return pl.pallas_call(
        body, name=name, in_specs=[ANY], out_specs=ANY, out_shape=jax.ShapeDtypeStruct((3, rows, cols), parts.dtype),
        scratch_shapes=[pltpu.SemaphoreType.DMA((3,)), pltpu.SemaphoreType.DMA((3,))],
    )(parts)


def _exchange_copies(in_ref, land_ref, send_sems, recv_sems):
    x, y, c, chips = _position()
    return [_remote(in_ref.at[2 * cx + cy], land_ref.at[j], send_sems.at[j], recv_sems.at[j], (cx, cy, c))
            for j, (cx, cy) in enumerate(chips)]


def _exchange_start(parts, name):
    _, rows, cols = parts.shape

    def body(in_ref, land_ref, send_sems, recv_sems, in_thru, land_thru, token):
        for cp in _exchange_copies(in_ref, land_ref, send_sems, recv_sems):
            cp.start()
        token[...] = jnp.zeros_like(token)

    land = lax.empty((3, rows, cols), parts.dtype)
    return pl.pallas_call(
        body, name=name,
        out_shape=(pltpu.SemaphoreType.DMA((3,)), pltpu.SemaphoreType.DMA((3,)), pltpu.HBM(parts.shape, parts.dtype),
                   pltpu.HBM(land.shape, land.dtype), jax.ShapeDtypeStruct((8, LANES), F32)),
        in_specs=(HBM, HBM), out_specs=(SEM, SEM, HBM, HBM, pl.BlockSpec(memory_space=pltpu.VMEM)),
        input_output_aliases={0: 2, 1: 3},
        compiler_params=pltpu.CompilerParams(has_side_effects=SIDE_EFFECT),
    )(pltpu.with_memory_space_constraint(parts, pltpu.HBM), pltpu.with_memory_space_constraint(land, pltpu.HBM))


def _exchange_wait(send_sems, recv_sems, parts, land, after, name):
    def body(in_ref, land_ref, send_sems, recv_sems, after_ref, in_out, land_out):
        for cp in _exchange_copies(in_ref, land_ref, send_sems, recv_sems):
            cp.wait_send()
            cp.wait_recv()

    return pl.pallas_call(
        body, name=name, out_shape=(pltpu.HBM(parts.shape, parts.dtype), pltpu.HBM(land.shape, land.dtype)),
        in_specs=(HBM, HBM, SEM, SEM, ANY), out_specs=(HBM, HBM), input_output_aliases={0: 0, 1: 1},
        compiler_params=pltpu.CompilerParams(has_side_effects=SIDE_EFFECT),
    )(parts, land, send_sems, recv_sems, after)


def _chip_sum(parts, recv, chip, name):
    _, rows, cols = parts.shape
    tm = _tile(rows, 512, 16)

    def body(chip_ref, own_ref, recv_ref, o_ref):
        acc = own_ref[0].astype(F32)
        for j in range(3):
            acc = acc + recv_ref[j].astype(F32)
        o_ref[...] = acc

    return pl.pallas_call(
        body, name=name,
        grid_spec=pltpu.PrefetchScalarGridSpec(
            num_scalar_prefetch=1, grid=(rows // tm,),
            in_specs=[pl.BlockSpec((1, tm, cols), lambda i, chip_ref: (chip_ref[0], i, 0)),
                      pl.BlockSpec((3, tm, cols), lambda i, chip_ref: (0, i, 0))],
            out_specs=pl.BlockSpec((tm, cols), lambda i, chip_ref: (i, 0))),
        out_shape=jax.ShapeDtypeStruct((rows, cols), F32), compiler_params=_params("parallel"),
    )(chip, parts, recv)


def _join_unpack(mine, hs, groups, name):
    n = len(hs)
    offs = [sum(hs[:i]) for i in range(n)]
    cols = mine.shape[1]
    n_out = max(groups) + 1
    base = [2 * sum(h for h, g in zip(hs[:i], groups[:i]) if g == groups[i]) for i in range(n)]
    out_rows = [2 * sum(h for h, g in zip(hs, groups) if g == k) for k in range(n_out)]

    def body(in_ref, *refs):
        outs = refs[:n_out]
        send_sems, recv_sems, buf, in_sems, out_sems = refs[n_out:]
        x, y, c, _ = _position()
        sibling = (x, y, 1 - c)
        sent, local = [], []
        for i in range(n):
            src = in_ref.at[pl.ds(offs[i], hs[i]), :]
            here = outs[groups[i]].at[pl.ds(base[i] + c * hs[i], hs[i]), :]
            cp = _remote(src, here, send_sems.at[i], recv_sems.at[i], sibling)
            cp.start()
            sent.append(cp)
            local.append((src, here))
        _staged_copies(local, buf, in_sems, out_sems)
        for i, cp in enumerate(sent):
            there = outs[groups[i]].at[pl.ds(base[i] + (1 - c) * hs[i], hs[i]), :]
            _remote(there, there, send_sems.at[i], recv_sems.at[i], sibling).wait_recv()
            cp.wait_send()

    assert max(hs) <= STAGE_ROWS
    return pl.pallas_call(
        body, name=name, in_specs=[ANY], out_specs=[ANY] * n_out,
        out_shape=[jax.ShapeDtypeStruct((r, cols), F32) for r in out_rows],
        scratch_shapes=[pltpu.SemaphoreType.DMA((n,)), pltpu.SemaphoreType.DMA((n,))] + _stage_scratch(F32, cols),
        compiler_params=pltpu.CompilerParams(vmem_limit_bytes=VMEM_LIMIT),
    )(mine)


SMALL_ROWS = 16
SMALL_PACK_ROWS = 256


def _small_rows(n):
    return -(-n // (SMALL_ROWS * LANES)) * SMALL_ROWS


def _pack_small(arrs):
    parts = []
    for a in arrs:
        flat = a.reshape(-1)
        rows = _small_rows(flat.shape[0])
        flat = jnp.pad(flat, (0, rows * LANES - flat.shape[0]))
        parts.append(flat.reshape(rows, LANES))
    total = sum(p.shape[0] for p in parts)
    parts.append(jnp.zeros((-total % SMALL_PACK_ROWS, LANES), F32))
    return jnp.concatenate(parts, axis=0)


def _unpack_small(packed, shapes):
    out, r = [], 0
    for sh in shapes:
        n = math.prod(sh)
        cnt = _small_rows(n)
        out.append(packed[r:r + cnt].reshape(-1)[:n].reshape(sh))
        r += cnt
    return out


def _ffn_bwd(dh, dhb, h_in, saved, g_norm, w_gate_t, w_up_t, w_down, tag, after=None):
    n, gate, up, act = saved
    dgate, dup = _ffn_dact(dhb, w_down, gate, up, f"{tag}_dact", after)
    dw_down = _matmul(act, dhb, trans_a=True, name=f"{tag}_dwdown")
    dw_gate_t = _matmul(dgate, n, trans_a=True, name=f"{tag}_dwgate")
    dw_up_t = _matmul(dup, n, trans_a=True, name=f"{tag}_dwup")
    dh_in, dh_inb, dg = _dn_norm([(dgate, w_gate_t), (dup, w_up_t)], h_in, g_norm, dh, f"{tag}_dnorm")
    return dh_in, dh_inb, dg, dw_gate_t, dw_up_t, dw_down


def _local_step(x, tgt, w, big, late_weights, reduce_send, reduce_exchange):
    s = x.shape[0]
    tabs = _rope_tables(s)
    grads, gbig = {}, {}

    g_ev = w['ev_norm_g']
    n1 = _rms_fwd(x, g_ev, "ev_norm")
    proj0 = _matmul(n1, big['ev_w_in', 0], trans_b=True, name="ev_in", out_dtype=BF16, rows_inner=True)
    q0, k0, v0 = _qkv_prep_even(proj0, tabs, "ev_qkv")
    sinks = w['ev_sinks'].reshape(-1)
    o0, lse0, o0b = _attn_fwd(q0, k0, v0, sinks, max_dist=BLOCK - 1, name="ev_attn", emit_bf16=True)
    yconv, cout = _conv_fwd(proj0, w['ev_conv_w'][0], w['ev_conv_b'], w['ev_conv_ln_g'], w['ev_conv_ln_b'], "ev_conv")
    mix0 = (o0b[0], cout)
    g_f0 = w['ffn_norm_g'][0:1]
    h1, n2 = _matmul_norm(mix0, big['ev_w_out', 0], x, g_f0, "ev_out")
    big = {**big, **late_weights(h1)}

    g_od = w['od_norm_g']
    act0, gate0, up0 = _ffn_gate_up(n2, big['ffn_w_gate', 0], big['ffn_w_up', 0], "ffn0_gate_up")
    h2, n3 = _matmul_norm(act0, big['ffn_w_down', 0], h1, g_od, "ffn0_down")
    ffn0 = (n2, gate0, up0, act0)

    proj1 = _matmul(n3, big['od_w_in', 0], trans_b=True, name="od_in", out_dtype=BF16, rows_inner=True)
    qkv = _qkv_prep_odd(proj1, tabs, "od_qkv")
    nb = len(DILATIONS)
    outs, lses = [], []
    for i, d in enumerate(DILATIONS):
        o_r, lse_r = _attn_fwd(qkv[i], qkv[nb + i], qkv[2 * nb + i], None, max_dist=BLOCK, name=f"od_attn{d}", o_dtype=BF16)
        outs.append(o_r)
        lses.append(lse_r)
    comb = _combine(outs, lses, "od_combine")
    c_bf16 = comb[0]
    c_fold = {1: comb[1]}
    lse_fold = {1: comb[2]}
    for i, d in enumerate(DILATIONS[1:]):
        c_fold[d], lse_fold[d] = comb[3 + 2 * i], comb[4 + 2 * i]
    w_sp = w['od_spatial_w'][0]
    sb_t = w['od_spatial_b'][0].T
    mixed, dout = _gate_fwd(proj1, w['od_sgu_ln_g'], w['od_sgu_ln_b'], w_sp, sb_t, "od_gate")
    mix1 = (c_bf16, dout)
    g_f1 = w['ffn_norm_g'][1:2]
    h3, n4 = _matmul_norm(mix1, big['od_w_out', 0], h2, g_f1, "od_out")
    act1, gate1, up1 = _ffn_gate_up(n4, big['ffn_w_gate', 1], big['ffn_w_up', 1], "ffn1_gate_up")
    ffn1 = (n4, gate1, up1, act1)

    dh4, dh4b, dg_final, loss_tile = _matmul_final(act1, big['ffn_w_down', 1], h3, w['final_norm_g'].reshape(1, D_MODEL),
                                                   tgt, "ffn1_down_loss")
    grads['final_norm_g'] = dg_final.reshape(D_MODEL)

    dh3, dh3b, dg_f1, gbig['ffn_w_gate', 1], gbig['ffn_w_up', 1], gbig['ffn_w_down', 1] = _ffn_bwd(
        dh4, dh4b, h3, ffn1, g_f1, big['ffn_w_gate', 1], big['ffn_w_up', 1], big['ffn_w_down', 1], "ffn1")

    dmix1 = _matmul(dh3b, big['od_w_out', 0], trans_b=True, name="od_dmix")
    gbig['od_w_out', 0] = _matmul_tn_pair(mix1[0], mix1[1], dh3b, "od_dwout")
    do_fold = dict(zip(DILATIONS[1:], _fold_dout(dmix1, "od_fold_dout")))
    do_fold[1] = dmix1[None]
    dqs, dks, dvs = [], [], []
    for i, d in enumerate(DILATIONS):
        dq_r, dk_r, dv_r = _attn_bwd(qkv[i], qkv[nb + i], qkv[2 * nb + i], do_fold[d], c_fold[d], lse_fold[d], None,
                                     max_dist=BLOCK, name=f"od_dattn{d}")
        dqs.append(dq_r)
        dks.append(dk_r)
        dvs.append(dv_r)
    dz, dg_sgu, db_sgu, dw_sp, dsb = _gate_bwd(dmix1, proj1, mixed, w['od_sgu_ln_g'], w['od_sgu_ln_b'], w_sp, "od_dgate")
    grads['od_sgu_ln_g'], grads['od_sgu_ln_b'] = dg_sgu, db_sgu
    grads['od_spatial_w'], grads['od_spatial_b'] = dw_sp[None], dsb[None]
    dproj1 = _qkv_post_odd(dqs, dks, dvs, dz, tabs, "od_dproj")
    gbig['od_w_in', 0] = _matmul(dproj1, n3, trans_a=True, name="od_dwin")
    dh2, dh2b, dg_od = _dn_norm([(dproj1, big['od_w_in', 0])], h2, g_od, dh3, "od_dnorm")
    grads['od_norm_g'] = dg_od
    token = reduce_send(0, gbig)

    dh1, dh1b, dg_f0, gbig['ffn_w_gate', 0], gbig['ffn_w_up', 0], gbig['ffn_w_down', 0] = _ffn_bwd(
        dh2, dh2b, h1, ffn0, g_f0, big['ffn_w_gate', 0], big['ffn_w_up', 0], big['ffn_w_down', 0], "ffn0", token)
    grads['ffn_norm_g'] = jnp.concatenate([dg_f0, dg_f1], axis=0)
    token = reduce_exchange(0, dh1) + reduce_send(1, gbig)

    dmix0 = _matmul(dh1b, big['ev_w_out', 0], trans_b=True, name="ev_dmix", after=token)
    gbig['ev_w_out', 0] = _matmul_tn_pair(mix0[0], mix0[1], dh1b, "ev_dwout")
    dq0, dk0, dv0, dsink = _attn_bwd(q0, k0, v0, dmix0[None], o0, lse0, sinks, max_dist=BLOCK - 1, name="ev_dattn")
    grads['ev_sinks'] = dsink[:, 0, :].reshape(N_PAIRS, 2, HEAD_DIM)[:, :, 0].reshape(1, 8)
    token = reduce_exchange(1, dq0)
    dyc, dg_cln, db_cln, dcb = _conv_tail_bwd(dmix0, yconv, w['ev_conv_ln_g'] + token[0:1, 0:1], w['ev_conv_ln_b'],
                                              "ev_dconv_tail")
    grads['ev_conv_ln_g'], grads['ev_conv_ln_b'], grads['ev_conv_b'] = dg_cln, db_cln, dcb
    dglu, dconv_w = _conv_bwd(proj0, dyc, w['ev_conv_w'][0], "ev_dconv")
    grads['ev_conv_w'] = dconv_w[None]
    dproj0 = _qkv_post_even(dq0, dk0, dv0, dglu, tabs, "ev_dproj")
    gbig['ev_w_in', 0] = _matmul(dproj0, n1, trans_a=True, name="ev_dwin")
    dx, _, dg_ev = _dn_norm([(dproj0, big['ev_w_in', 0])], x, g_ev, dh1, "ev_dnorm")
    grads['ev_norm_g'] = dg_ev
    return loss_tile, dx, grads, gbig


def _shard_rows(w, layer, by_cols):
    return w[layer].T if by_cols else w[layer]


def kernel(x, ev_norm_g, ev_w_in, ev_sinks, ev_conv_w, ev_conv_b, ev_conv_ln_g, ev_conv_ln_b, ev_w_out, od_norm_g, od_w_in, od_sgu_ln_g, od_sgu_ln_b, od_spatial_w, od_spatial_b, od_w_out, ffn_norm_g, ffn_w_gate, ffn_w_up, ffn_w_down, final_norm_g, loss_target, m_ev_norm_g, m_ev_w_in, m_ev_sinks, m_ev_conv_w, m_ev_conv_b, m_ev_conv_ln_g, m_ev_conv_ln_b, m_ev_w_out, m_od_norm_g, m_od_w_in, m_od_sgu_ln_g, m_od_sgu_ln_b, m_od_spatial_w, m_od_spatial_b, m_od_w_out, m_ffn_norm_g, m_ffn_w_gate, m_ffn_w_up, m_ffn_w_down, m_final_norm_g, v_ev_norm_g, v_ev_w_in, v_ev_sinks, v_ev_conv_w, v_ev_conv_b, v_ev_conv_ln_g, v_ev_conv_ln_b, v_ev_w_out, v_od_norm_g, v_od_w_in, v_od_sgu_ln_g, v_od_sgu_ln_b, v_od_spatial_w, v_od_spatial_b, v_od_w_out, v_ffn_norm_g, v_ffn_w_gate, v_ffn_w_up, v_ffn_w_down, v_final_norm_g):
    given = dict(locals())
    wts = {n: given[n] for n in WEIGHTS}
    mom = {n: given["m_" + n] for n in WEIGHTS}
    var = {n: given["v_" + n] for n in WEIGHTS}
    chip = 2 * lax.axis_index("x") + lax.axis_index("y")

    shard_rows = [_shard_rows(wts[n], layer, by_cols).astype(BF16) for n, layer, by_cols in BIG]
    counts = [a.shape[0] for a in shard_rows]
    n_first = sum(n.startswith('ev_') for n, _, _ in BIG)

    def unpack(stacked, entries, cnts):
        out, r = {}, 0
        for (n, layer, _), cnt in zip(entries, cnts):
            out[n, layer] = stacked[:, r:r + cnt].reshape(N_CHIPS * cnt, D_MODEL)
            r += cnt
        return out

    first_w = _gather_chips(jnp.concatenate(shard_rows[:n_first], axis=0), "gather_weights_ev")
    big = unpack(first_w, BIG[:n_first], counts[:n_first])
    send_sems, recv_sems, late_shard, late_land, token = _gather_start(jnp.concatenate(shard_rows[n_first:], axis=0),
                                                                      first_w, "gather_weights_start")

    def late_weights(after):
        shard, land = _gather_wait(send_sems, recv_sems, late_shard, late_land, after, "gather_weights_wait")
        return unpack(_gather_finish(shard, land, "gather_weights_finish"), BIG[n_first:], counts[n_first:])

    full = {n: wts[n] for n in SMALL_REPL}
    full['ev_norm_g'] = full['ev_norm_g'] + token[0:1, 0:1]
    small_shards = [wts[n] for n in SMALL_SHARDED]
    small_shapes = [a.shape for a in small_shards]
    all_s = _gather_chips(_pack_small(small_shards), "gather_small_weights")
    per_chip = [_unpack_small(all_s[k], small_shapes) for k in range(N_CHIPS)]
    for i, n in enumerate(SMALL_SHARDED):
        full[n] = jnp.concatenate([per_chip[k][i] for k in range(N_CHIPS)], axis=-1)

    half_rows = {(n, layer): cnt // 2 for (n, layer, _), cnt in zip(BIG, counts)}
    in_flight = []

    sending = {}

    def halves(stage, gbig):
        return [gbig[e].reshape(N_CHIPS, 2, half_rows[e], D_MODEL) for e in GRAD_STAGES[stage]]

    def reduce_send(stage, gbig):
        send_sems, recv_sems, *buffers, token = _pair_send_start(halves(stage, gbig), f"grad_pair_start{stage}")
        sending[stage] = (send_sems, recv_sems, buffers)
        return token

    def reduce_exchange(stage, after):
        send_sems, recv_sems, buffers = sending.pop(stage)
        *split, got = _pair_send_wait(send_sems, recv_sems, buffers, after, f"grad_pair_wait{stage}")
        chip_part = _pair_add(split, got, f"grad_pair_add{stage}")
        *handles, token = _exchange_start(chip_part, f"grad_exchange_start{stage}")
        in_flight.append(handles)
        return token

    loss_tile, grad_x, grads, gbig = _local_step(x[0], loss_target[0], full, big, late_weights, reduce_send, reduce_exchange)
    loss = lax.psum(loss_tile[0, 0], ("x", "y", "c"))

    reduced = {}
    for stage, entries in enumerate(GRAD_STAGES):
        if stage < len(in_flight):
            chip_part, from_chips = _exchange_wait(*in_flight[stage], grad_x, f"grad_exchange_wait{stage}")
        else:
            split = halves(stage, gbig)
            chip_part = _pair_add(split, _pair_send(split, f"grad_pair_send{stage}"), f"grad_pair_add{stage}")
            from_chips = _chip_exchange(chip_part, f"grad_chip_exchange{stage}")
        my_half = _chip_sum(chip_part, from_chips, chip.reshape(1), f"grad_chip_sum{stage}")
        joined = _join_unpack(my_half, [half_rows[e] for e in entries], list(range(len(entries))), f"grad_join_halves{stage}")
        reduced.update(zip(entries, joined))

    small_names = SMALL_REPL + SMALL_SHARDED
    small_full_shapes = [grads[n].shape for n in small_names]
    spack = _pack_small([grads[n] for n in small_names])
    s_all = _gather_devices(spack, "grad_small_gather")
    s_sum = _unpack_small(_ordered_sum(s_all, "grad_small_sum"), small_full_shapes)
    g_all = dict(zip(small_names, s_sum))
    for n in SMALL_SHARDED:
        width = wts[n].shape[-1]
        g_all[n] = lax.dynamic_slice_in_dim(g_all[n], chip * width, width, axis=g_all[n].ndim - 1)

    delta, new_m, new_v = {}, {}, {}
    for n in BIG_NAMES:
        by_cols = [bc for nn, _, bc in BIG if nn == n][0]
        layers = wts[n].shape[0]

        def as_rows(a):
            return (jnp.swapaxes(a, 1, 2) if by_cols else a).reshape(-1, D_MODEL)

        def from_rows(a):
            a = a.reshape(layers, -1, D_MODEL)
            return jnp.swapaxes(a, 1, 2) if by_cols else a

        g_rows = [reduced[n, layer] for layer in range(layers)]
        g_rows = g_rows[0] if layers == 1 else jnp.concatenate(g_rows, axis=0)
        updated = _adamw(as_rows(wts[n]), g_rows, as_rows(mom[n]), as_rows(var[n]), f"adamw_{n}")
        g_all[n] = from_rows(g_rows)
        delta[n], new_m[n], new_v[n] = (from_rows(a) for a in updated)
    shapes = [wts[n].shape for n in small_names]
    d_s, m_s, v_s = _adamw(*[_pack_small([src[n] for n in small_names]) for src in (wts, g_all, mom, var)], "adamw_small")
    for dst, packed in ((delta, d_s), (new_m, m_s), (new_v, v_s)):
        dst.update(zip(small_names, _unpack_small(packed, shapes)))

    return (loss, grad_x[None], *[g_all[n] for n in WEIGHTS], *[delta[n] for n in WEIGHTS],
            *[new_m[n] for n in WEIGHTS], *[new_v[n] for n in WEIGHTS])
```

```python
import math

import jax
import jax.numpy as jnp
from jax import lax
from jax.experimental import pallas as pl
from jax.experimental.pallas import tpu as pltpu

F32 = jnp.float32
BF16 = jnp.bfloat16

D_MODEL = 1024
HEAD_DIM = 64
ROT_DIM = 16
ROPE_THETA = 500000.0
RMS_EPS = 1e-6
LN_EPS = 1e-5
BLOCK = 128
CONV_WIDTH = 31
CONV_HALO = 32
CONV_ROWS = 64
D_FF = 2816
N_GROUPS = 8
ATTN_W = 512
ATTN_SCALE = HEAD_DIM ** -0.5
NEG = -1e30
DILATIONS = (1, 4, 16)

ADAM_LR = 0.001
ADAM_B1 = 0.9
ADAM_B2 = 0.999
ADAM_EPS = 1e-08
ADAM_WD = 0.01
ADAM_STEP = 10

LANES = 128
N_PAIRS = ATTN_W // LANES
VMEM_LIMIT = 56 * 1024 * 1024
MESH = pl.DeviceIdType.MESH
N_CHIPS = 4
N_DEV = 8

WEIGHTS = ['ev_norm_g', 'ev_w_in', 'ev_sinks', 'ev_conv_w', 'ev_conv_b', 'ev_conv_ln_g', 'ev_conv_ln_b', 'ev_w_out',
           'od_norm_g', 'od_w_in', 'od_sgu_ln_g', 'od_sgu_ln_b', 'od_spatial_w', 'od_spatial_b', 'od_w_out',
           'ffn_norm_g', 'ffn_w_gate', 'ffn_w_up', 'ffn_w_down', 'final_norm_g']
BIG = [('ev_w_in', 0, True), ('ev_w_out', 0, False), ('od_w_in', 0, True), ('od_w_out', 0, False),
       ('ffn_w_gate', 0, True), ('ffn_w_gate', 1, True), ('ffn_w_up', 0, True), ('ffn_w_up', 1, True),
       ('ffn_w_down', 0, False), ('ffn_w_down', 1, False)]
BIG_NAMES = ['ev_w_in', 'ev_w_out', 'od_w_in', 'od_w_out', 'ffn_w_gate', 'ffn_w_up', 'ffn_w_down']
GRAD_STAGES = ([('od_w_in', 0), ('od_w_out', 0), ('ffn_w_gate', 1), ('ffn_w_up', 1), ('ffn_w_down', 1)],
               [('ffn_w_gate', 0), ('ffn_w_up', 0), ('ffn_w_down', 0)],
               [('ev_w_in', 0), ('ev_w_out', 0)])
SMALL_SHARDED = ['ev_conv_w', 'od_norm_g', 'od_sgu_ln_g', 'od_sgu_ln_b']
SMALL_REPL = ['ev_norm_g', 'ev_sinks', 'ev_conv_b', 'ev_conv_ln_g', 'ev_conv_ln_b', 'od_spatial_w', 'od_spatial_b',
              'ffn_norm_g', 'final_norm_g']


def _tile(n, cap, mult=LANES):
    best = None
    for t in range(mult, min(n, cap) + 1, mult):
        if n % t == 0:
            best = t
    assert best is not None, (n, cap)
    return best


def _params(*sem):
    return pltpu.CompilerParams(dimension_semantics=sem, vmem_limit_bytes=VMEM_LIMIT)


def _sigmoid(x):
    return 1.0 / (1.0 + jnp.exp(-x))


def _pair_block(p):
    return slice(p * LANES, (p + 1) * LANES)


def _matmul(a, b, *, name, trans_a=False, trans_b=False, add=None, out_dtype=F32, after=None, rows_inner=False):
    parts = a if isinstance(a, (tuple, list)) else (a,)
    if trans_a:
        k, m = parts[0].shape
    else:
        m = parts[0].shape[0]
        k = sum(p.shape[1] for p in parts)
    if trans_b:
        n, k2 = b.shape
    else:
        k2, n = b.shape
    assert k == k2 and b.dtype == BF16 and all(p.dtype == BF16 for p in parts)
    tm = _tile(m, D_FF // 2 if trans_a else 512)
    tn = _tile(n, D_FF // 2)
    tk = k if k <= D_FF else _tile(k, 2048)
    nk = k // tk
    na = len(parts)
    assert na == 1 or (nk == 1 and not trans_a)
    assert nk == 1 or out_dtype == F32
    dims = (((0 if trans_a else 1,), (1 if trans_b else 0,)), ((), ()))
    has_add = add is not None

    def body(*refs):
        a_refs, b_ref = refs[:na], refs[na]
        add_ref = refs[na + 1] if has_add else None
        o_ref = refs[na + 1 + has_add + (after is not None)]
        def product():
            a_val = a_refs[0][...] if na == 1 else jnp.concatenate([r[...] for r in a_refs], axis=1)
            return lax.dot_general(a_val, b_ref[...], dims, preferred_element_type=F32)

        if nk == 1:
            part = product()
            if has_add:
                part = part + add_ref[...]
            o_ref[...] = part.astype(o_ref.dtype)
            return
        kk = pl.program_id(2)

        @pl.when(kk == 0)
        def _():
            o_ref[...] = product() + add_ref[...] if has_add else product()

        @pl.when(kk > 0)
        def _():
            o_ref[...] = product() + o_ref[...]

    def at(f):
        return (lambda j, i, kk: f(i, j, kk)) if rows_inner else f

    if trans_a:
        a_specs = [pl.BlockSpec((tk, tm), at(lambda i, j, kk: (kk, i)))]
    elif na == 1:
        a_specs = [pl.BlockSpec((tm, tk), at(lambda i, j, kk: (i, kk)))]
    else:
        a_specs = [pl.BlockSpec((tm, p.shape[1]), at(lambda i, j, kk: (i, 0))) for p in parts]
    b_spec = (pl.BlockSpec((tn, tk), at(lambda i, j, kk: (j, kk))) if trans_b
              else pl.BlockSpec((tk, tn), at(lambda i, j, kk: (kk, j))))
    o_spec = pl.BlockSpec((tm, tn), at(lambda i, j, kk: (i, j)))
    in_specs = a_specs + [b_spec] + ([o_spec] if has_add else [])
    operands = list(parts) + [b] + ([add] if has_add else [])
    if after is not None:
        in_specs.append(_after_spec(after))
        operands.append(after)
    grid = (n // tn, m // tm, nk) if rows_inner else (m // tm, n // tn, nk)
    return pl.pallas_call(
        body, name=name, grid=grid, in_specs=in_specs, out_specs=o_spec,
        out_shape=jax.ShapeDtypeStruct((m, n), out_dtype),
        compiler_params=_params("parallel", "parallel", "arbitrary"),
    )(*operands)


def _matmul_rows(a, b, add, epilogue, consts, tiled, outs, accs, name):
    parts = a if isinstance(a, (tuple, list)) else (a,)
    m = parts[0].shape[0]
    tm = 512
    na, nc, nt, no = len(parts), len(consts), len(tiled), len(outs)

    def body(*refs):
        a_refs, b_ref, add_ref = refs[:na], refs[na], refs[na + 1]
        const_refs = refs[na + 2:na + 2 + nc]
        tiled_refs = refs[na + 2 + nc:na + 2 + nc + nt]
        out_refs = refs[na + 2 + nc + nt:]
        a_val = a_refs[0][...] if na == 1 else jnp.concatenate([r[...] for r in a_refs], axis=1)
        h = jnp.dot(a_val, b_ref[...], preferred_element_type=F32) + add_ref[...]
        results = epilogue(h, [r[...] for r in const_refs], [r[...] for r in tiled_refs])
        for o_ref, val in zip(out_refs[:no], results[:no]):
            o_ref[...] = val.astype(o_ref.dtype)
        if accs:
            @pl.when(_first_step())
            def _():
                for o_ref in out_refs[no:]:
                    o_ref[...] = jnp.zeros_like(o_ref)

            for o_ref, val in zip(out_refs[no:], results[no:]):
                o_ref[...] += val

    row = lambda w: pl.BlockSpec((tm, w), lambda i: (i, 0))
    whole = lambda shape: pl.BlockSpec(shape, lambda i: (0,) * len(shape))
    return pl.pallas_call(
        body, name=name, grid=(m // tm,),
        in_specs=[row(p.shape[1]) for p in parts] + [whole(b.shape), row(D_MODEL)] + [whole(c.shape) for c in consts]
        + [row(t.shape[1]) for t in tiled],
        out_specs=[row(c) for c, _ in outs] + [whole(sh) for sh, _ in accs],
        out_shape=[jax.ShapeDtypeStruct((m, c), dt) for c, dt in outs] + [jax.ShapeDtypeStruct(sh, dt) for sh, dt in accs],
        compiler_params=_params("arbitrary"),
    )(*parts, b, add, *consts, *tiled)


def _matmul_norm(a, b, add, g, name):
    def epilogue(h, consts, tiled):
        r = lax.rsqrt(jnp.mean(h * h, axis=-1, keepdims=True) + RMS_EPS)
        return [h, h * r * consts[0]]

    return _matmul_rows(a, b, add, epilogue, [g], [], [(D_MODEL, F32), (D_MODEL, BF16)], [], name)


def _matmul_final(a, b, add, g, tgt, name):
    def epilogue(h, consts, tiled):
        gg = consts[0]
        r = lax.rsqrt(jnp.mean(h * h, axis=-1, keepdims=True) + RMS_EPS)
        xh = h * r
        e = xh * gg - tiled[0]
        loss = (0.5 / D_MODEL) * jnp.sum(jnp.sum(e * e, axis=-1, keepdims=True), axis=0, keepdims=True)
        dy = e * (1.0 / D_MODEL)
        dxh = dy * gg
        dx = r * (dxh - xh * jnp.mean(dxh * xh, axis=-1, keepdims=True))
        return [dx, dx, jnp.sum(dy * xh, axis=0, keepdims=True), jnp.broadcast_to(loss, (1, LANES))]

    return _matmul_rows(a, b, add, epilogue, [g], [tgt], [(D_MODEL, F32), (D_MODEL, BF16)],
                        [((1, D_MODEL), F32), ((1, LANES), F32)], name)


def _matmul_tn_pair(a1, a2, b, name):
    kdim, m1 = a1.shape
    m2 = a2.shape[1]
    n = b.shape[1]
    tn = _tile(n, 1024)
    tk = _tile(kdim, 2048)
    nk = kdim // tk
    dims = (((0,), (0,)), ((), ()))

    def body(a1_ref, a2_ref, b_ref, o_ref):
        kk = pl.program_id(1)
        def products():
            bv = b_ref[...]
            return (lax.dot_general(a1_ref[...], bv, dims, preferred_element_type=F32),
                    lax.dot_general(a2_ref[...], bv, dims, preferred_element_type=F32))

        @pl.when(kk == 0)
        def _():
            o_ref[0:m1, :], o_ref[m1:, :] = products()

        @pl.when(kk > 0)
        def _():
            top, bot = products()
            o_ref[0:m1, :] = top + o_ref[0:m1, :]
            o_ref[m1:, :] = bot + o_ref[m1:, :]

    return pl.pallas_call(
        body, name=name, grid=(n // tn, nk),
        in_specs=[pl.BlockSpec((tk, m1), lambda j, kk: (kk, 0)), pl.BlockSpec((tk, m2), lambda j, kk: (kk, 0)),
                  pl.BlockSpec((tk, tn), lambda j, kk: (kk, j))],
        out_specs=pl.BlockSpec((m1 + m2, tn), lambda j, kk: (0, j)),
        out_shape=jax.ShapeDtypeStruct((m1 + m2, n), F32),
        compiler_params=_params("parallel", "arbitrary"),
    )(a1, a2, b)


def _ffn_gate_up(n, w_gate_t, w_up_t, name):
    m, k = n.shape
    f = w_gate_t.shape[0]
    tm, tn = _tile(m, 1024), _tile(f, D_FF // 2)

    def body(n_ref, wg_ref, wu_ref, act_ref, gate_ref, up_ref):
        a = n_ref[...]

        def products(cols):
            return (lax.dot_general(a, wg_ref[cols, :], NT, preferred_element_type=F32),
                    lax.dot_general(a, wu_ref[cols, :], NT, preferred_element_type=F32))

        chunks = _col_chunks(tn)
        ahead = products(chunks[0])
        for idx, cols in enumerate(chunks):
            gate, up = ahead
            if idx + 1 < len(chunks):
                ahead = products(chunks[idx + 1])
            act_ref[:, cols] = (gate * _sigmoid(gate) * up).astype(BF16)
            gate_ref[:, cols] = gate.astype(BF16)
            up_ref[:, cols] = up.astype(BF16)

    wspec = pl.BlockSpec((tn, k), lambda j, i: (j, 0))
    ospec = pl.BlockSpec((tm, tn), lambda j, i: (i, j))
    return pl.pallas_call(
        body, name=name, grid=(f // tn, m // tm), in_specs=[pl.BlockSpec((tm, k), lambda j, i: (i, 0)), wspec, wspec],
        out_specs=[ospec] * 3, out_shape=[jax.ShapeDtypeStruct((m, f), BF16)] * 3,
        compiler_params=_params("parallel", "parallel"),
    )(n, w_gate_t, w_up_t)


def _col_chunks(n, width=384):
    return [slice(c, min(c + width, n)) for c in range(0, n, width)]


def _after_spec(after):
    return pl.BlockSpec(after.shape, lambda *_: (0,) * after.ndim)


def _ffn_dact(dhb, w_down, gate, up, name, after=None):
    m, k = dhb.shape
    f = w_down.shape[0]
    tm, tn = _tile(m, 1024), _tile(f, D_FF // 2)

    def body(d_ref, w_ref, g_ref, u_ref, *rest):
        dg_ref, du_ref = rest[-2:]
        d = d_ref[...]

        def product(cols):
            return lax.dot_general(d, w_ref[cols, :], NT, preferred_element_type=F32)

        chunks = _col_chunks(tn)
        ahead = product(chunks[0])
        for idx, cols in enumerate(chunks):
            dact = ahead
            if idx + 1 < len(chunks):
                ahead = product(chunks[idx + 1])
            g = g_ref[:, cols].astype(F32)
            sg = _sigmoid(g)
            dg_ref[:, cols] = (dact * u_ref[:, cols].astype(F32) * sg * (1.0 + g * (1.0 - sg))).astype(BF16)
            du_ref[:, cols] = (dact * g * sg).astype(BF16)

    ospec = pl.BlockSpec((tm, tn), lambda j, i: (i, j))
    extra = [] if after is None else [after]
    return pl.pallas_call(
        body, name=name, grid=(f // tn, m // tm),
        in_specs=[pl.BlockSpec((tm, k), lambda j, i: (i, 0)), pl.BlockSpec((tn, k), lambda j, i: (j, 0)), ospec, ospec]
        + [_after_spec(a) for a in extra],
        out_specs=[ospec] * 2, out_shape=[jax.ShapeDtypeStruct((m, f), BF16)] * 2,
        compiler_params=_params("parallel", "parallel"),
    )(dhb, w_down, gate, up, *extra)


def _dn_norm(pairs, h, g, dres, name):
    m = h.shape[0]
    tm = 512
    np_ = len(pairs)

    def body(*refs):
        a_refs, b_refs = refs[:np_], refs[np_:2 * np_]
        h_ref, dres_ref, g_ref, dh_ref, dhb_ref, dg_ref = refs[2 * np_:]

        @pl.when(_first_step())
        def _():
            dg_ref[...] = jnp.zeros_like(dg_ref)

        dy = jnp.dot(a_refs[0][...], b_refs[0][...], preferred_element_type=F32)
        for a_ref, b_ref in zip(a_refs[1:], b_refs[1:]):
            dy = jnp.dot(a_ref[...], b_ref[...], preferred_element_type=F32) + dy
        x = h_ref[...]
        r = lax.rsqrt(jnp.mean(x * x, axis=-1, keepdims=True) + RMS_EPS)
        xh = x * r
        dg_ref[...] += jnp.sum(dy * xh, axis=0, keepdims=True)
        dxh = dy * g_ref[...]
        tot = dres_ref[...] + r * (dxh - xh * jnp.mean(dxh * xh, axis=-1, keepdims=True))
        dh_ref[...] = tot
        dhb_ref[...] = tot.astype(BF16)

    row = lambda w: pl.BlockSpec((tm, w), lambda i: (i, 0))
    whole = lambda a: pl.BlockSpec(a.shape, lambda i: (0, 0))
    a_list, b_list = [a for a, _ in pairs], [b for _, b in pairs]
    return pl.pallas_call(
        body, name=name, grid=(m // tm,),
        in_specs=[row(a.shape[1]) for a in a_list] + [whole(b) for b in b_list] + [row(D_MODEL), row(D_MODEL), whole(g)],
        out_specs=[row(D_MODEL), row(D_MODEL), pl.BlockSpec((1, D_MODEL), lambda i: (0, 0))],
        out_shape=[jax.ShapeDtypeStruct((m, D_MODEL), F32), jax.ShapeDtypeStruct((m, D_MODEL), BF16),
                   jax.ShapeDtypeStruct((1, D_MODEL), F32)],
        compiler_params=_params("arbitrary"),
    )(*a_list, *b_list, h, dres, g)


def _rows(body, name, tm, tiled, consts, outs, accs=()):
    s = tiled[0].shape[0]
    assert s % tm == 0
    in_specs = [pl.BlockSpec((tm, a.shape[1]), lambda i: (i, 0)) for a in tiled]
    in_specs += [pl.BlockSpec(a.shape, lambda i, nd=a.ndim: (0,) * nd) for a in consts]
    out_shape = [jax.ShapeDtypeStruct((s, c), dt) for c, dt in outs]
    out_shape += [jax.ShapeDtypeStruct(sh, dt) for sh, dt in accs]
    out_specs = [pl.BlockSpec((tm, c), lambda i: (i, 0)) for c, _ in outs]
    out_specs += [pl.BlockSpec(sh, lambda i, nd=len(sh): (0,) * nd) for sh, _ in accs]
    return pl.pallas_call(
        body, name=name, grid=(s // tm,), in_specs=in_specs, out_specs=out_specs, out_shape=out_shape,
        compiler_params=_params("arbitrary"),
    )(*tiled, *consts)


def _first_step():
    return pl.program_id(0) == 0


def _rms_fwd(h, g, name):
    def body(h_ref, g_ref, n_ref):
        x = h_ref[...]
        r = lax.rsqrt(jnp.mean(x * x, axis=-1, keepdims=True) + RMS_EPS)
        n_ref[...] = (x * r * g_ref[...]).astype(BF16)

    return _rows(body, name, 512, [h], [g], [(D_MODEL, BF16)])[0]


def _rope_tables(s):
    half = ROT_DIM // 2
    inv_freq = ROPE_THETA ** (-jnp.arange(half, dtype=F32) * (2.0 / ROT_DIM))
    ang = jnp.arange(s, dtype=F32)[:, None] * inv_freq[None, :]
    cos, sin = jnp.cos(ang), jnp.sin(ang)
    rest = HEAD_DIM - ROT_DIM
    ones = jnp.ones((s, rest), F32)
    zeros = jnp.zeros((s, rest), F32)
    zh = jnp.zeros((s, half), F32)
    c_t = jnp.concatenate([cos, cos, ones], axis=1)
    a_t = jnp.concatenate([-sin, zh, zeros], axis=1)
    b_t = jnp.concatenate([zh, sin, zeros], axis=1)
    return tuple(jnp.tile(t, (1, LANES // HEAD_DIM)) for t in (c_t, a_t, b_t))


def _rot(x, c, a, b):
    w = x.shape[1]
    half = ROT_DIM // 2
    return x * c + pltpu.roll(x, w - half, 1) * a + pltpu.roll(x, half, 1) * b


def _wide(t, w):
    return t if w == LANES else jnp.tile(t, (1, w // LANES))


def _low_lanes(rows):
    return lax.broadcasted_iota(jnp.int32, (rows, LANES), 1) < HEAD_DIM


def _fold_store(x, sc_ref, out_refs):
    tm = x.shape[0]
    if any(d > 1 for d in out_refs):
        for p in range(N_PAIRS):
            sc_ref[p] = x[:, _pair_block(p)]
    for d, o_ref in out_refs.items():
        if d == 1:
            o_ref[0] = x.astype(o_ref.dtype)
            continue
        for r in range(d):
            for p in range(N_PAIRS):
                o_ref[r, :, _pair_block(p)] = sc_ref[p, pl.ds(r, tm // d, stride=d), :].astype(o_ref.dtype)


def _unfold_load(x_ref, sc_ref, d, add=False):
    n = x_ref.shape[1]
    for r in range(d):
        for p in range(N_PAIRS):
            rows = pl.ds(r, n, stride=d) if d > 1 else slice(None)
            val = x_ref[r, :, _pair_block(p)].astype(F32)
            if add:
                val = val + sc_ref[p, rows, :]
            sc_ref[p, rows, :] = val


def _folded_spec(d, tm, w=ATTN_W):
    return pl.BlockSpec((d, tm // d, w), lambda i: (0, i, 0))


def _folded_shape(s, d, dtype, w=ATTN_W):
    return jax.ShapeDtypeStruct((d, s // d, w), dtype)


def _qkv_prep_even(proj, tabs, name):
    s = proj.shape[0]
    tm = 512

    def body(p_ref, c_ref, a_ref, b_ref, q_ref, k_ref, v_ref):
        c, a, b = c_ref[...], a_ref[...], b_ref[...]
        q_ref[0] = _rot(p_ref[:, 0:ATTN_W].astype(F32), _wide(c, ATTN_W), _wide(a, ATTN_W), _wide(b, ATTN_W)).astype(BF16)
        lo = _low_lanes(tm)
        for src, o_ref in ((_rot(p_ref[:, 512:640].astype(F32), c, a, b), k_ref), (p_ref[:, 640:768].astype(F32), v_ref)):
            swapped = pltpu.roll(src, HEAD_DIM, 1)
            o_ref[0, :, 0:LANES] = jnp.where(lo, src, swapped).astype(BF16)
            o_ref[0, :, LANES:] = jnp.where(lo, swapped, src).astype(BF16)

    row = lambda w: pl.BlockSpec((tm, w), lambda i: (i, 0))
    return pl.pallas_call(
        body, name=name, grid=(s // tm,), in_specs=[row(proj.shape[1]), row(LANES), row(LANES), row(LANES)],
        out_specs=[_folded_spec(1, tm), _folded_spec(1, tm, 2 * LANES), _folded_spec(1, tm, 2 * LANES)],
        out_shape=[_folded_shape(s, 1, BF16), _folded_shape(s, 1, BF16, 2 * LANES), _folded_shape(s, 1, BF16, 2 * LANES)],
        compiler_params=_params("parallel"),
    )(proj, *tabs)


def _qkv_post_even(dq, dk, dv, dglu, tabs, name):
    s = dglu.shape[0]
    tm = 512

    def body(dq_ref, dk_ref, dv_ref, dr_ref, c_ref, a_ref, b_ref, o_ref):
        c, a, b = c_ref[...], -a_ref[...], -b_ref[...]
        o_ref[:, 0:ATTN_W] = _rot(dq_ref[0].astype(F32), _wide(c, ATTN_W), _wide(a, ATTN_W), _wide(b, ATTN_W)).astype(BF16)
        lo = _low_lanes(tm)
        merged = []
        for ref in (dk_ref, dv_ref):
            first, second = ref[0, :, 0:LANES].astype(F32), ref[0, :, LANES:].astype(F32)
            merged.append(jnp.where(lo, first + pltpu.roll(first, HEAD_DIM, 1), second + pltpu.roll(second, HEAD_DIM, 1)))
        o_ref[:, 512:640] = _rot(merged[0], c, a, b).astype(BF16)
        o_ref[:, 640:768] = merged[1].astype(BF16)
        o_ref[:, 768:] = dr_ref[...]

    row = lambda w: pl.BlockSpec((tm, w), lambda i: (i, 0))
    return pl.pallas_call(
        body, name=name, grid=(s // tm,),
        in_specs=[_folded_spec(1, tm), _folded_spec(1, tm, 2 * LANES), _folded_spec(1, tm, 2 * LANES),
                  row(dglu.shape[1]), row(LANES), row(LANES), row(LANES)],
        out_specs=row(EVEN_IN), out_shape=jax.ShapeDtypeStruct((s, EVEN_IN), BF16),
        compiler_params=_params("parallel"),
    )(dq, dk, dv, dglu, *tabs)


def _qkv_prep_odd(proj, tabs, name):
    s = proj.shape[0]
    tm = 1024

    def body(p_ref, c_ref, a_ref, b_ref, *rest):
        outs, sc_ref = rest[:-1], rest[-1]
        c, a, b = (_wide(t[...], ATTN_W) for t in (c_ref, a_ref, b_ref))
        for t in range(3):
            x = p_ref[:, t * ATTN_W:(t + 1) * ATTN_W].astype(F32)
            if t < 2:
                x = _rot(x, c, a, b)
            _fold_store(x, sc_ref, {d: outs[t * len(DILATIONS) + i] for i, d in enumerate(DILATIONS)})

    row = lambda w: pl.BlockSpec((tm, w), lambda i: (i, 0))
    return pl.pallas_call(
        body, name=name, grid=(s // tm,), in_specs=[row(proj.shape[1]), row(LANES), row(LANES), row(LANES)],
        out_specs=[_folded_spec(d, tm) for _ in range(3) for d in DILATIONS],
        out_shape=[_folded_shape(s, d, BF16) for _ in range(3) for d in DILATIONS],
        scratch_shapes=[pltpu.VMEM((N_PAIRS, tm, LANES), F32)],
        compiler_params=_params("parallel"),
    )(proj, *tabs)


def _qkv_post_odd(dqs, dks, dvs, dz, tabs, name):
    s = dz.shape[0]
    tm = 512
    nb = len(DILATIONS)

    def body(*refs):
        groups = (refs[:nb], refs[nb:2 * nb], refs[2 * nb:3 * nb])
        dz_ref, c_ref, a_ref, b_ref, o_ref, sc_ref = refs[3 * nb:]
        c, a, b = _wide(c_ref[...], ATTN_W), _wide(-a_ref[...], ATTN_W), _wide(-b_ref[...], ATTN_W)
        for t, group in enumerate(groups):
            for i, d in enumerate(DILATIONS):
                _unfold_load(group[i], sc_ref, d, add=i > 0)
            x = jnp.concatenate([sc_ref[p] for p in range(N_PAIRS)], axis=1)
            if t < 2:
                x = _rot(x, c, a, b)
            o_ref[:, t * ATTN_W:(t + 1) * ATTN_W] = x.astype(BF16)
        o_ref[:, 3 * ATTN_W:] = dz_ref[...]

    row = lambda w: pl.BlockSpec((tm, w), lambda i: (i, 0))
    return pl.pallas_call(
        body, name=name, grid=(s // tm,),
        in_specs=[_folded_spec(d, tm) for _ in range(3) for d in DILATIONS] + [row(dz.shape[1]), row(LANES), row(LANES), row(LANES)],
        out_specs=row(ODD_IN), out_shape=jax.ShapeDtypeStruct((s, ODD_IN), BF16),
        scratch_shapes=[pltpu.VMEM((N_PAIRS, tm, LANES), F32)],
        compiler_params=_params("parallel"),
    )(*dqs, *dks, *dvs, dz, *tabs)


def _fold_dout(dmix, name):
    s = dmix.shape[0]
    tm = 512
    ds = [d for d in DILATIONS if d > 1]

    def body(d_ref, *rest):
        outs, sc_ref = rest[:-1], rest[-1]
        _fold_store(d_ref[...], sc_ref, dict(zip(ds, outs)))

    return pl.pallas_call(
        body, name=name, grid=(s // tm,), in_specs=[pl.BlockSpec((tm, ATTN_W), lambda i: (i, 0))],
        out_specs=[_folded_spec(d, tm) for d in ds], out_shape=[_folded_shape(s, d, BF16) for d in ds],
        scratch_shapes=[pltpu.VMEM((N_PAIRS, tm, LANES), F32)],
        compiler_params=_params("parallel"),
    )(dmix)


def _window(j, i, tq):
    r0 = j * tq + i * BLOCK
    if i > 0:
        return pl.ds(pl.multiple_of(r0 - BLOCK, BLOCK), 2 * BLOCK), BLOCK
    start = pl.multiple_of(jnp.maximum(r0 - BLOCK, 0), BLOCK)
    return pl.ds(start, 2 * BLOCK), r0 - start


def _band_valid(offset, max_dist):
    shape = (2 * BLOCK, 2 * BLOCK)
    dist = (lax.bitwise_and(lax.broadcasted_iota(jnp.int32, shape, 0), BLOCK - 1)
            - lax.broadcasted_iota(jnp.int32, shape, 1) + offset)
    return jnp.abs(2 * dist - max_dist) <= max_dist


def _stack_heads(lo, x):
    zero = jnp.zeros_like(x)
    return jnp.concatenate([jnp.where(lo, x, zero), jnp.where(lo, zero, x)], axis=0)


def _unstack_heads(lo, x):
    return jnp.where(lo, x[:BLOCK], x[BLOCK:])


NT = (((1,), (1,)), ((), ()))
TN = (((0,), (0,)), ((), ()))


def _attn_fwd(q, k, v, sinks, *, max_dist, name, emit_bf16=False, o_dtype=F32):
    d, sp, wq = q.shape
    nq, nk = wq // LANES, k.shape[2] // LANES
    kdiv = nq // nk
    tq = min(sp, 1024)
    nsub = tq // BLOCK
    has_sink = sinks is not None

    def body(*refs):
        refs = list(refs)
        sink_ref = refs.pop(0) if has_sink else None
        q_ref, k_ref, v_ref, o_ref, lse_ref = refs[:5]
        pair = pl.program_id(1)
        j = pl.program_id(2)
        lo = _low_lanes(BLOCK)
        if has_sink:
            first_head = lax.broadcasted_iota(jnp.int32, (2 * BLOCK, 1), 0) < BLOCK
            sk = jnp.where(first_head, sink_ref[2 * pair], sink_ref[2 * pair + 1])
        for i in range(nsub):
            win, offset = _window(j, i, tq)
            rows = slice(i * BLOCK, (i + 1) * BLOCK)
            kw = k_ref[0, win, :]
            vw = v_ref[0, win, :]
            s = lax.dot_general(_stack_heads(lo, q_ref[0, rows, :]), kw, NT, preferred_element_type=F32) * ATTN_SCALE
            s = jnp.where(_band_valid(offset, max_dist), s, NEG)
            m = jnp.max(s, axis=-1, keepdims=True)
            if has_sink:
                m = jnp.maximum(m, sk)
            p = jnp.exp(s - m)
            l = jnp.sum(p, axis=-1, keepdims=True)
            if has_sink:
                l = l + jnp.exp(sk - m)
            o2 = _unstack_heads(lo, jnp.dot(p.astype(BF16), vw, preferred_element_type=F32) / l)
            o_ref[0, rows, :] = o2.astype(o_ref.dtype)
            lse_ref[0, rows, :] = _unstack_heads(lo, m + jnp.log(l))
            if emit_bf16:
                refs[5][0, rows, :] = o2.astype(BF16)

    qspec = pl.BlockSpec((1, tq, LANES), lambda r, p, j: (r, j, p))
    kspec = pl.BlockSpec((1, sp, LANES), lambda r, p, j: (r, 0, p // kdiv))
    in_specs = [qspec, kspec, kspec]
    operands = [q, k, v]
    if has_sink:
        in_specs = [pl.BlockSpec(memory_space=pltpu.SMEM)] + in_specs
        operands = [sinks] + operands
    out_shape = [jax.ShapeDtypeStruct(q.shape, o_dtype), jax.ShapeDtypeStruct(q.shape, F32)]
    if emit_bf16:
        out_shape.append(jax.ShapeDtypeStruct(q.shape, BF16))
    return pl.pallas_call(
        body, name=name, grid=(d, nq, sp // tq), in_specs=in_specs, out_specs=[qspec] * len(out_shape),
        out_shape=out_shape, compiler_params=_params("parallel", "parallel", "arbitrary"),
    )(*operands)


def _attn_bwd(q, k, v, do, oo, lse, sinks, *, max_dist, name):
    d, sp, wq = q.shape
    wk = k.shape[2]
    nq, nk = wq // LANES, wk // LANES
    kdiv = nq // nk
    tq = min(sp, 1024)
    nsub = tq // BLOCK
    has_sink = sinks is not None

    def body(*refs):
        refs = list(refs)
        sink_ref = refs.pop(0) if has_sink else None
        q_ref, k_ref, v_ref, do_ref, oo_ref, lse_ref, dq_ref, dk_out, dv_out = refs[:9]
        dk_ref, dv_ref = refs[-2:]
        pk, g, j = pl.program_id(1), pl.program_id(2), pl.program_id(3)

        @pl.when((g == 0) & (j == 0))
        def _():
            dk_ref[...] = jnp.zeros_like(dk_ref)
            dv_ref[...] = jnp.zeros_like(dv_ref)

        lo = _low_lanes(BLOCK)
        if has_sink:
            first_head = lax.broadcasted_iota(jnp.int32, (2 * BLOCK, 1), 0) < BLOCK
            pair = pk * kdiv + g
            sk = jnp.where(first_head, sink_ref[2 * pair], sink_ref[2 * pair + 1])
            sink_acc = jnp.zeros((2 * BLOCK, LANES), F32)
        for i in range(nsub):
            win, offset = _window(j, i, tq)
            rows = slice(i * BLOCK, (i + 1) * BLOCK)
            kw = k_ref[0, win, :]
            vw = v_ref[0, win, :]
            do2 = do_ref[0, rows, :].astype(F32)
            qs = _stack_heads(lo, q_ref[0, rows, :])
            dos = _stack_heads(lo, do2.astype(BF16))
            prod = do2 * oo_ref[0, rows, :]
            delta = jnp.sum(_stack_heads(lo, prod), axis=-1, keepdims=True)
            lse2 = lse_ref[0, rows, :]
            lse_swapped = pltpu.roll(lse2, HEAD_DIM, 1)
            lse_st = jnp.concatenate([jnp.where(lo, lse2, lse_swapped), jnp.where(lo, lse_swapped, lse2)], axis=0)
            s = lax.dot_general(qs, kw, NT, preferred_element_type=F32) * ATTN_SCALE
            s = jnp.where(_band_valid(offset, max_dist), s, NEG)
            p = jnp.exp(s - jnp.tile(lse_st, (1, 2)))
            dv_ref[win, :] = lax.dot_general(p.astype(BF16), dos, TN, preferred_element_type=F32) + dv_ref[win, :]
            dp = lax.dot_general(dos, vw, NT, preferred_element_type=F32)
            ds = (p * (dp - delta) * ATTN_SCALE).astype(BF16)
            dq_ref[0, rows, :] = _unstack_heads(lo, jnp.dot(ds, kw, preferred_element_type=F32)).astype(BF16)
            dk_ref[win, :] = lax.dot_general(ds, qs, TN, preferred_element_type=F32) + dk_ref[win, :]
            if has_sink:
                sink_acc = sink_acc - jnp.exp(sk - lse_st) * delta

        @pl.when((g == kdiv - 1) & (j == sp // tq - 1))
        def _():
            dk_out[0] = dk_ref[...].astype(BF16)
            dv_out[0] = dv_ref[...].astype(BF16)

        if has_sink:
            dsink_ref = refs[9]

            @pl.when(j == 0)
            def _():
                dsink_ref[...] = jnp.zeros_like(dsink_ref)

            dsink_ref[0] += jnp.where(lo[0:1], jnp.sum(sink_acc[:BLOCK], axis=0, keepdims=True),
                                      jnp.sum(sink_acc[BLOCK:], axis=0, keepdims=True))

    def qmap(r, pk, g, j):
        return (r, j, pk * kdiv + g)

    def kmap(r, pk, g, j):
        return (r, 0, pk)

    qspec = pl.BlockSpec((1, tq, LANES), qmap)
    kspec = pl.BlockSpec((1, sp, LANES), kmap)
    in_specs = [qspec, kspec, kspec, qspec, qspec, qspec]
    operands = [q, k, v, do, oo, lse]
    out_specs = [qspec, kspec, kspec]
    out_shape = [jax.ShapeDtypeStruct((d, sp, wq), BF16), jax.ShapeDtypeStruct((d, sp, wk), BF16),
                 jax.ShapeDtypeStruct((d, sp, wk), BF16)]
    if has_sink:
        in_specs = [pl.BlockSpec(memory_space=pltpu.SMEM)] + in_specs
        operands = [sinks] + operands
        out_specs.append(pl.BlockSpec((1, 1, LANES), lambda r, pk, g, j: (pk * kdiv + g, 0, 0)))
        out_shape.append(jax.ShapeDtypeStruct((nq, 1, LANES), F32))
    nsteps = sp // tq
    return pl.pallas_call(
        body, name=name, grid=(d, nk, kdiv, nsteps), in_specs=in_specs, out_specs=out_specs, out_shape=out_shape,
        scratch_shapes=[pltpu.VMEM((sp, LANES), F32), pltpu.VMEM((sp, LANES), F32)],
        compiler_params=_params("parallel", "parallel", "arbitrary", "arbitrary"),
    )(*operands)


def _combine(outs, lses, name):
    s = outs[0].shape[1]
    tm = 512
    nb = len(DILATIONS)
    ds = [d for d in DILATIONS if d > 1]

    def body(*refs):
        o_refs, l_refs = refs[:nb], refs[nb:2 * nb]
        cb_ref, c_ref, lse_ref = refs[2 * nb:2 * nb + 3]
        folded = refs[2 * nb + 3:2 * nb + 3 + 2 * len(ds)]
        scratch = refs[2 * nb + 3 + 2 * len(ds):]
        so = {1: None}
        sl = {1: None}
        for i, d in enumerate(ds):
            so[d], sl[d] = scratch[2 * i], scratch[2 * i + 1]
            _unfold_load(o_refs[1 + i], so[d], d)
            _unfold_load(l_refs[1 + i], sl[d], d)
        for p in range(N_PAIRS):
            pb = _pair_block(p)
            ls = [l_refs[0][0, :, pb]] + [sl[d][p] for d in ds]
            os_ = [o_refs[0][0, :, pb].astype(F32)] + [so[d][p] for d in ds]
            m = ls[0]
            for t in ls[1:]:
                m = jnp.maximum(m, t)
            ws = [jnp.exp(t - m) for t in ls]
            tot = ws[0]
            for t in ws[1:]:
                tot = tot + t
            acc = ws[0] * os_[0]
            for w, o in zip(ws[1:], os_[1:]):
                acc = acc + w * o
            cmix = acc / tot
            lse = m + jnp.log(tot)
            cb_ref[:, pb] = cmix.astype(BF16)
            c_ref[0, :, pb] = cmix
            lse_ref[0, :, pb] = lse
            so[ds[0]][p] = cmix
            sl[ds[0]][p] = lse
        for i, d in enumerate(ds):
            for r in range(d):
                for p in range(N_PAIRS):
                    rows = pl.ds(r, tm // d, stride=d)
                    folded[2 * i][r, :, _pair_block(p)] = so[ds[0]][p, rows, :]
                    folded[2 * i + 1][r, :, _pair_block(p)] = sl[ds[0]][p, rows, :]

    in_specs = [_folded_spec(d, tm) for _ in range(2) for d in DILATIONS]
    out_specs = [pl.BlockSpec((tm, ATTN_W), lambda i: (i, 0)), _folded_spec(1, tm), _folded_spec(1, tm)]
    out_shape = [jax.ShapeDtypeStruct((s, ATTN_W), BF16), _folded_shape(s, 1, F32), _folded_shape(s, 1, F32)]
    for d in ds:
        out_specs += [_folded_spec(d, tm)] * 2
        out_shape += [_folded_shape(s, d, F32)] * 2
    return pl.pallas_call(
        body, name=name, grid=(s // tm,), in_specs=in_specs, out_specs=out_specs, out_shape=out_shape,
        scratch_shapes=[pltpu.VMEM((N_PAIRS, tm, LANES), F32)] * (2 * len(ds)),
        compiler_params=_params("parallel"),
    )(*outs, *lses)


GLU_A = slice(768, 1280)
GLU_B = slice(1280, 1792)
EVEN_IN = 1792
ODD_IN = 2560
CONV_CH = 512


def _shifted_copies(xs_ref):
    rows = xs_ref.shape[1] - 8
    for b in range(1, 8):
        xs_ref[b, 0:rows, :] = xs_ref[0, pl.ds(b, rows), :]


def _shifted_rows(xs_ref, start):
    return xs_ref[start % 8, pl.ds(start - start % 8, CONV_ROWS), :]


def _glu(p_ref):
    return p_ref[:, GLU_A].astype(F32) * _sigmoid(p_ref[:, GLU_B].astype(F32))


def _conv_fwd(proj, w, b, ln_g, ln_b, name):
    s = proj.shape[0]
    tm = 512
    nh = tm // CONV_HALO
    lead = CONV_HALO - (CONV_WIDTH - 1)

    def body(p_ref, ph_ref, w_ref, b_ref, g_ref, bb_ref, y_ref, o_ref, xs_ref):
        xs_ref[0, CONV_HALO:, :] = _glu(p_ref)
        xs_ref[0, 0:CONV_HALO, :] = jnp.where(pl.program_id(0) > 0, _glu(ph_ref), 0.0)
        _shifted_copies(xs_ref)
        for c0 in range(0, tm, CONV_ROWS):
            acc = jnp.zeros((CONV_ROWS, CONV_CH), F32) + b_ref[...]
            for j in range(CONV_WIDTH):
                acc = acc + _shifted_rows(xs_ref, lead + j + c0) * w_ref[j:j + 1, :]
            y_ref[c0:c0 + CONV_ROWS, :] = acc
            mu = jnp.mean(acc, axis=-1, keepdims=True)
            xc = acc - mu
            var = jnp.mean(xc * xc, axis=-1, keepdims=True)
            zz = xc * lax.rsqrt(var + LN_EPS) * g_ref[...] + bb_ref[...]
            o_ref[c0:c0 + CONV_ROWS, :] = (zz * _sigmoid(zz)).astype(BF16)

    def const(a):
        return pl.BlockSpec(a.shape, lambda i: (0, 0))

    return pl.pallas_call(
        body, name=name, grid=(s // tm,),
        in_specs=[pl.BlockSpec((tm, EVEN_IN), lambda i: (i, 0)),
                  pl.BlockSpec((CONV_HALO, EVEN_IN), lambda i: (jnp.maximum(i * nh - 1, 0), 0)),
                  const(w), const(b), const(ln_g), const(ln_b)],
        out_specs=[pl.BlockSpec((tm, CONV_CH), lambda i: (i, 0)), pl.BlockSpec((tm, CONV_CH), lambda i: (i, 0))],
        out_shape=[jax.ShapeDtypeStruct((s, CONV_CH), F32), jax.ShapeDtypeStruct((s, CONV_CH), BF16)],
        scratch_shapes=[pltpu.VMEM((8, tm + CONV_HALO, CONV_CH), F32)],
        compiler_params=_params("arbitrary"),
    )(proj, proj, w, b, ln_g, ln_b)


def _conv_tail_bwd(dmix, yconv, ln_g, ln_b, name):
    def body(d_ref, y_ref, g_ref, b_ref, dy_ref, dg_ref, db_ref, dcb_ref):
        @pl.when(_first_step())
        def _():
            dg_ref[...] = jnp.zeros_like(dg_ref)
            db_ref[...] = jnp.zeros_like(db_ref)
            dcb_ref[...] = jnp.zeros_like(dcb_ref)

        y = y_ref[...]
        g = g_ref[...]
        mu = jnp.mean(y, axis=-1, keepdims=True)
        xc = y - mu
        rstd = lax.rsqrt(jnp.mean(xc * xc, axis=-1, keepdims=True) + LN_EPS)
        xh = xc * rstd
        zz = xh * g + b_ref[...]
        sg = _sigmoid(zz)
        dzz = d_ref[:, CONV_CH:] * sg * (1.0 + zz * (1.0 - sg))
        dg_ref[...] += jnp.sum(dzz * xh, axis=0, keepdims=True)
        db_ref[...] += jnp.sum(dzz, axis=0, keepdims=True)
        dxh = dzz * g
        dy = rstd * (dxh - jnp.mean(dxh, axis=-1, keepdims=True) - xh * jnp.mean(dxh * xh, axis=-1, keepdims=True))
        dcb_ref[...] += jnp.sum(dy, axis=0, keepdims=True)
        dy_ref[...] = dy

    vec = ((1, CONV_CH), F32)
    return _rows(body, name, 512, [dmix, yconv], [ln_g, ln_b], [(CONV_CH, F32)], [vec, vec, vec])


def _conv_bwd(proj, dy, w, name):
    s = proj.shape[0]
    tm = 512
    nh = tm // CONV_HALO
    nsteps = s // tm
    lead = CONV_HALO - (CONV_WIDTH - 1)

    def body(p_ref, ph_ref, dy_ref, dyn_ref, w_ref, dglu_ref, dw_ref, xf_ref, dyf_ref):
        i = pl.program_id(0)

        @pl.when(i == 0)
        def _():
            dw_ref[...] = jnp.zeros_like(dw_ref)

        ga = p_ref[:, GLU_A].astype(F32)
        sgb = _sigmoid(p_ref[:, GLU_B].astype(F32))
        xf_ref[0, CONV_HALO:, :] = ga * sgb
        xf_ref[0, 0:CONV_HALO, :] = jnp.where(i > 0, _glu(ph_ref), 0.0)
        _shifted_copies(xf_ref)
        dyf_ref[0, 0:tm, :] = dy_ref[...]
        dyf_ref[0, tm:, :] = jnp.where(i < nsteps - 1, dyn_ref[...], 0.0)
        _shifted_copies(dyf_ref)
        for c0 in range(0, tm, CONV_ROWS):
            rows = slice(c0, c0 + CONV_ROWS)
            acc = jnp.zeros((CONV_ROWS, CONV_CH), F32)
            for j in range(CONV_WIDTH):
                acc = acc + _shifted_rows(dyf_ref, CONV_WIDTH - 1 - j + c0) * w_ref[j:j + 1, :]
            a_c, s_c = ga[rows, :], sgb[rows, :]
            dglu_ref[rows, 0:CONV_CH] = (acc * s_c).astype(BF16)
            dglu_ref[rows, CONV_CH:] = (acc * a_c * s_c * (1.0 - s_c)).astype(BF16)
        for j in range(CONV_WIDTH):
            part = jnp.zeros((8, CONV_CH), F32)
            for c0 in range(0, tm, CONV_ROWS):
                prod = dy_ref[c0:c0 + CONV_ROWS, :] * _shifted_rows(xf_ref, lead + j + c0)
                part = part + jnp.sum(prod.reshape(CONV_ROWS // 8, 8, CONV_CH), axis=0)
            dw_ref[j:j + 1, :] += jnp.sum(part, axis=0, keepdims=True)

    return pl.pallas_call(
        body, name=name, grid=(nsteps,),
        in_specs=[pl.BlockSpec((tm, EVEN_IN), lambda i: (i, 0)),
                  pl.BlockSpec((CONV_HALO, EVEN_IN), lambda i: (jnp.maximum(i * nh - 1, 0), 0)),
                  pl.BlockSpec((tm, CONV_CH), lambda i: (i, 0)),
                  pl.BlockSpec((CONV_HALO, CONV_CH), lambda i: (jnp.minimum((i + 1) * nh, s // CONV_HALO - 1), 0)),
                  pl.BlockSpec(w.shape, lambda i: (0, 0))],
        out_specs=[pl.BlockSpec((tm, 2 * CONV_CH), lambda i: (i, 0)), pl.BlockSpec(w.shape, lambda i: (0, 0))],
        out_shape=[jax.ShapeDtypeStruct((s, 2 * CONV_CH), BF16), jax.ShapeDtypeStruct(w.shape, F32)],
        scratch_shapes=[pltpu.VMEM((8, tm + CONV_HALO, CONV_CH), F32), pltpu.VMEM((8, tm + CONV_HALO, CONV_CH), F32)],
        compiler_params=_params("arbitrary"),
    )(proj, proj, dy, dy, w)


GATE_Z = slice(1536, 2560)
D_CH = 512
GELU_C = math.sqrt(2.0 / math.pi)
GELU_K = 0.044715


def _gelu_parts(z):
    t = jnp.tanh(GELU_C * (z + GELU_K * z * z * z))
    return 0.5 * z * (1.0 + t), t


def _lane_group(rows):
    return lax.broadcasted_iota(jnp.int32, (rows, D_CH), 1) // HEAD_DIM


def _tril_mask():
    return lax.broadcasted_iota(jnp.int32, (BLOCK, BLOCK), 0) >= lax.broadcasted_iota(jnp.int32, (BLOCK, BLOCK), 1)


def _layer_norm_parts(x):
    mu = jnp.mean(x, axis=-1, keepdims=True)
    xc = x - mu
    rstd = lax.rsqrt(jnp.mean(xc * xc, axis=-1, keepdims=True) + LN_EPS)
    return xc * rstd, rstd


def _gate_fwd(proj, ln_g, ln_b, w_sp, sb_t, name):
    tm = 512

    def body(p_ref, g_ref, b_ref, w_ref, sb_ref, mixed_ref, out_ref):
        zz, _ = _gelu_parts(p_ref[:, GATE_Z].astype(F32))
        u = zz[:, :D_CH]
        xh, _ = _layer_norm_parts(zz[:, D_CH:])
        gn = (xh * g_ref[...] + b_ref[...]).astype(BF16)
        grp = _lane_group(BLOCK)
        tri = _tril_mask()
        ws = [jnp.where(tri, w_ref[gi], 0.0).astype(BF16) for gi in range(N_GROUPS)]
        bias = jnp.zeros((BLOCK, D_CH), F32)
        for gi in range(N_GROUPS):
            bias = jnp.where(grp == gi, sb_ref[:, gi:gi + 1], bias)
        for ch in range(tm // BLOCK):
            rows = slice(ch * BLOCK, (ch + 1) * BLOCK)
            gc = gn[rows, :]
            mixed = bias
            for gi in range(N_GROUPS):
                r = jnp.dot(ws[gi], gc, preferred_element_type=F32)
                mixed = jnp.where(grp == gi, r + bias, mixed)
            mixed_ref[rows, :] = mixed
            out_ref[rows, :] = (u[rows, :] * mixed).astype(BF16)

    return _rows(body, name, tm, [proj], [ln_g, ln_b, w_sp, sb_t], [(D_CH, F32), (D_CH, BF16)])


def _gate_bwd(dmix, proj, mixed, ln_g, ln_b, w_sp, name):
    tm = 512

    def body(d_ref, p_ref, m_ref, g_ref, b_ref, w_ref, dz_ref, dg_ref, db_ref, dw_ref, dsb_ref, dgn_ref):
        @pl.when(_first_step())
        def _():
            dg_ref[...] = jnp.zeros_like(dg_ref)
            db_ref[...] = jnp.zeros_like(db_ref)
            dw_ref[...] = jnp.zeros_like(dw_ref)
            dsb_ref[...] = jnp.zeros_like(dsb_ref)

        z = p_ref[:, GATE_Z].astype(F32)
        zz, t = _gelu_parts(z)
        u = zz[:, :D_CH]
        xh, rstd = _layer_norm_parts(zz[:, D_CH:])
        g = g_ref[...]
        gn = (xh * g + b_ref[...]).astype(BF16)
        dd = d_ref[:, D_CH:]
        du = dd * m_ref[...]
        dm = dd * u
        grp = _lane_group(BLOCK)
        tri = _tril_mask()
        ws = [jnp.where(tri, w_ref[gi], 0.0).astype(BF16) for gi in range(N_GROUPS)]
        gsel = (lax.broadcasted_iota(jnp.int32, (N_GROUPS, D_CH), 1) // HEAD_DIM
                == lax.broadcasted_iota(jnp.int32, (N_GROUPS, D_CH), 0)).astype(F32)
        for ch in range(tm // BLOCK):
            rows = slice(ch * BLOCK, (ch + 1) * BLOCK)
            dmc = dm[rows, :]
            dmb = dmc.astype(BF16)
            gc = gn[rows, :]
            dgn = jnp.zeros((BLOCK, D_CH), F32)
            for gi in range(N_GROUPS):
                r = lax.dot_general(ws[gi], dmb, TN, preferred_element_type=F32)
                dgn = jnp.where(grp == gi, r, dgn)
                dmg = jnp.where(grp == gi, dmb, jnp.zeros_like(dmb))
                dwg = lax.dot_general(dmg, gc, NT, preferred_element_type=F32)
                dw_ref[gi] += jnp.where(tri, dwg, 0.0)
            dsb_ref[...] += lax.dot_general(gsel, dmc, NT, preferred_element_type=F32, precision=lax.Precision.HIGHEST)
            dgn_ref[rows, :] = dgn
        dgn = dgn_ref[...]
        db_ref[...] += jnp.sum(dgn, axis=0, keepdims=True)
        dg_ref[...] += jnp.sum(dgn * xh, axis=0, keepdims=True)
        dxh = dgn * g
        dgp = rstd * (dxh - jnp.mean(dxh, axis=-1, keepdims=True) - xh * jnp.mean(dxh * xh, axis=-1, keepdims=True))
        dgelu = 0.5 * (1.0 + t) + 0.5 * z * (1.0 - t * t) * GELU_C * (1.0 + 3.0 * GELU_K * z * z)
        dz_ref[:, 0:D_CH] = (du * dgelu[:, :D_CH]).astype(BF16)
        dz_ref[:, D_CH:] = (dgp * dgelu[:, D_CH:]).astype(BF16)

    s = proj.shape[0]
    tiled = [dmix, proj, mixed]
    consts = [ln_g, ln_b, w_sp]
    in_specs = [pl.BlockSpec((tm, a.shape[1]), lambda i: (i, 0)) for a in tiled]
    in_specs += [pl.BlockSpec(a.shape, lambda i, nd=a.ndim: (0,) * nd) for a in consts]
    vec = (1, D_CH)
    acc_shapes = [vec, vec, w_sp.shape, (N_GROUPS, BLOCK)]
    return pl.pallas_call(
        body, name=name, grid=(s // tm,), in_specs=in_specs,
        out_specs=[pl.BlockSpec((tm, 2 * D_CH), lambda i: (i, 0))]
        + [pl.BlockSpec(sh, lambda i, nd=len(sh): (0,) * nd) for sh in acc_shapes],
        out_shape=[jax.ShapeDtypeStruct((s, 2 * D_CH), BF16)] + [jax.ShapeDtypeStruct(sh, F32) for sh in acc_shapes],
        scratch_shapes=[pltpu.VMEM((tm, D_CH), F32)],
        compiler_params=_params("arbitrary"),
    )(*tiled, *consts)


def _adam_update(w, g, m, v):
    nm = ADAM_B1 * m + (1.0 - ADAM_B1) * g
    nv = ADAM_B2 * v + (1.0 - ADAM_B2) * (g * g)
    m_hat = nm / (1.0 - ADAM_B1 ** ADAM_STEP)
    v_hat = nv / (1.0 - ADAM_B2 ** ADAM_STEP)
    return -ADAM_LR * (m_hat / (jnp.sqrt(v_hat) + ADAM_EPS) + ADAM_WD * w), nm, nv


def _adamw(w, g, m, v, name):
    rows, cols = w.shape
    tm = _tile(rows, 512, 8)

    def body(w_ref, g_ref, m_ref, v_ref, d_ref, nm_ref, nv_ref):
        d_ref[...], nm_ref[...], nv_ref[...] = _adam_update(w_ref[...], g_ref[...], m_ref[...], v_ref[...])

    return _rows(body, name, tm, [w, g, m, v], [], [(cols, F32)] * 3)


def _ordered_sum(parts, name):
    n, rows, cols = parts.shape
    tm = _tile(rows, 512, 16 if parts.dtype == BF16 else 8)

    def body(p_ref, o_ref):
        acc = p_ref[0].astype(F32)
        for k in range(1, n):
            acc = acc + p_ref[k].astype(F32)
        o_ref[...] = acc

    return pl.pallas_call(body, name=name, grid=(rows // tm,),
                          in_specs=[pl.BlockSpec((n, tm, cols), lambda i: (0, i, 0))],
                          out_specs=pl.BlockSpec((tm, cols), lambda i: (i, 0)),
                          out_shape=jax.ShapeDtypeStruct((rows, cols), F32), compiler_params=_params("parallel"))(parts)


ANY = pl.BlockSpec(memory_space=pl.ANY)


def _position():
    x, y, c = lax.axis_index("x"), lax.axis_index("y"), lax.axis_index("c")
    other_chips = [(1 - x, y), (x, 1 - y), (1 - x, 1 - y)]
    return x, y, c, other_chips


def _remote(src, dst, send_sem, recv_sem, to):
    return pltpu.make_async_remote_copy(src_ref=src, dst_ref=dst, send_sem=send_sem, recv_sem=recv_sem,
                                        device_id=to, device_id_type=MESH)


STAGE_ROWS = 736


def _staged_copies(copies, buf, in_sems, out_sems):
    n = len(copies)

    def into(u):
        src = copies[u][0]
        return pltpu.make_async_copy(src, buf.at[u % 2, pl.ds(0, src.shape[0]), :], in_sems.at[u % 2])

    def out_of(u):
        dst = copies[u][1]
        return pltpu.make_async_copy(buf.at[u % 2, pl.ds(0, dst.shape[0]), :], dst, out_sems.at[u % 2])

    into(0).start()
    for u in range(n):
        into(u).wait()
        out_of(u).start()
        if u + 1 < n:
            if u >= 1:
                out_of(u - 1).wait()
            into(u + 1).start()
    if n >= 2:
        out_of(n - 2).wait()
    out_of(n - 1).wait()


def _stage_scratch(dtype, cols):
    return [pltpu.VMEM((2, STAGE_ROWS, cols), dtype), pltpu.SemaphoreType.DMA((2,)), pltpu.SemaphoreType.DMA((2,))]


def _row_chunks(rows):
    return [(r, min(STAGE_ROWS, rows - r)) for r in range(0, rows, STAGE_ROWS)]


def _gather_chips(shard, name):
    rows, cols = shard.shape
    half = rows // 2

    def body(in_ref, out_ref, send_sems, recv_sems, buf, in_sems, out_sems):
        x, y, c, chips = _position()
        me = 2 * x + y
        sibling = (x, y, 1 - c)

        def slab(chip, h):
            return out_ref.at[chip, pl.ds(h * half, half), :]

        first = [_remote(in_ref.at[pl.ds(c * half, half), :], slab(me, c), send_sems.at[j], recv_sems.at[j], (cx, cy, c))
                 for j, (cx, cy) in enumerate(chips)]
        for cp in first:
            cp.start()
        _staged_copies([(in_ref.at[pl.ds(r, n), :], out_ref.at[me, pl.ds(r, n), :]) for r, n in _row_chunks(rows)],
                       buf, in_sems, out_sems)
        passed = []
        for j, (cx, cy) in enumerate(chips):
            got = slab(2 * cx + cy, c)
            _remote(got, got, send_sems.at[j], recv_sems.at[j], sibling).wait_recv()
            cp = _remote(got, got, send_sems.at[3 + j], recv_sems.at[3 + j], sibling)
            cp.start()
            passed.append(cp)
        for j, (cx, cy) in enumerate(chips):
            got = slab(2 * cx + cy, 1 - c)
            _remote(got, got, send_sems.at[3 + j], recv_sems.at[3 + j], sibling).wait_recv()
        for cp in first + passed:
            cp.wait_send()

    return pl.pallas_call(
        body, name=name, in_specs=[ANY], out_specs=ANY,
        out_shape=jax.ShapeDtypeStruct((N_CHIPS, rows, cols), shard.dtype),
        scratch_shapes=[pltpu.SemaphoreType.DMA((6,)), pltpu.SemaphoreType.DMA((6,))] + _stage_scratch(shard.dtype, cols),
        compiler_params=pltpu.CompilerParams(vmem_limit_bytes=VMEM_LIMIT),
    )(shard)


HBM = pl.BlockSpec(memory_space=pltpu.HBM)
SEM = pl.BlockSpec(memory_space=pltpu.SEMAPHORE)
SIDE_EFFECT = pltpu.SideEffectType.DATAFLOW_SIDE_EFFECTING


def _ici_copies(in_ref, land_ref, send_sems, recv_sems, half):
    x, y, c, chips = _position()
    mine = pl.ds(c * half, half)
    sends = [_remote(in_ref.at[mine, :], land_ref.at[2 * x + y, mine, :], send_sems.at[j], recv_sems.at[j], (cx, cy, c))
             for j, (cx, cy) in enumerate(chips)]
    arrivals = [_remote(in_ref.at[mine, :], land_ref.at[2 * cx + cy, mine, :], send_sems.at[j], recv_sems.at[j], (cx, cy, c))
                for j, (cx, cy) in enumerate(chips)]
    return sends, arrivals


def _gather_start(shard, after, name):
    rows, cols = shard.shape

    def body(in_ref, land_ref, after_ref, send_sems, recv_sems, in_thru, land_thru, token):
        sends, _ = _ici_copies(in_ref, land_ref, send_sems, recv_sems, rows // 2)
        for cp in sends:
            cp.start()
        token[...] = jnp.zeros_like(token)

    land = lax.empty((N_CHIPS, rows, cols), shard.dtype)
    return pl.pallas_call(
        body, name=name,
        out_shape=(pltpu.SemaphoreType.DMA((3,)), pltpu.SemaphoreType.DMA((3,)), pltpu.HBM(shard.shape, shard.dtype),
                   pltpu.HBM(land.shape, land.dtype), jax.ShapeDtypeStruct((8, LANES), F32)),
        in_specs=(HBM, HBM, ANY), out_specs=(SEM, SEM, HBM, HBM, pl.BlockSpec(memory_space=pltpu.VMEM)),
        input_output_aliases={0: 2, 1: 3},
        compiler_params=pltpu.CompilerParams(has_side_effects=SIDE_EFFECT),
    )(pltpu.with_memory_space_constraint(shard, pltpu.HBM), pltpu.with_memory_space_constraint(land, pltpu.HBM), after)


def _gather_wait(send_sems, recv_sems, shard, land, after, name):
    rows = shard.shape[0]

    def body(in_ref, land_ref, send_sems, recv_sems, after_ref, in_out, land_out):
        sends, arrivals = _ici_copies(in_ref, land_ref, send_sems, recv_sems, rows // 2)
        for cp in sends:
            cp.wait_send()
        for cp in arrivals:
            cp.wait_recv()

    return pl.pallas_call(
        body, name=name, out_shape=(pltpu.HBM(shard.shape, shard.dtype), pltpu.HBM(land.shape, land.dtype)),
        in_specs=(HBM, HBM, SEM, SEM, ANY), out_specs=(HBM, HBM), input_output_aliases={0: 0, 1: 1},
        compiler_params=pltpu.CompilerParams(has_side_effects=SIDE_EFFECT),
    )(shard, land, send_sems, recv_sems, after)


def _gather_finish(shard, land, name):
    rows, cols = shard.shape
    half = rows // 2

    def body(in_ref, land_ref, out_ref, send_sems, recv_sems, buf, in_sems, out_sems):
        x, y, c, chips = _position()
        me = 2 * x + y
        sibling = (x, y, 1 - c)

        def slab(chip, h):
            return out_ref.at[chip, pl.ds(h * half, half), :]

        passed = [_remote(slab(2 * cx + cy, c), slab(2 * cx + cy, c), send_sems.at[j], recv_sems.at[j], sibling)
                  for j, (cx, cy) in enumerate(chips)]
        for cp in passed:
            cp.start()
        _staged_copies([(in_ref.at[pl.ds(r, n), :], out_ref.at[me, pl.ds(r, n), :]) for r, n in _row_chunks(rows)],
                       buf, in_sems, out_sems)
        for j, (cx, cy) in enumerate(chips):
            got = slab(2 * cx + cy, 1 - c)
            _remote(got, got, send_sems.at[j], recv_sems.at[j], sibling).wait_recv()
        for cp in passed:
            cp.wait_send()

    return pl.pallas_call(
        body, name=name, in_specs=[ANY, ANY], out_specs=ANY, out_shape=jax.ShapeDtypeStruct(land.shape, land.dtype),
        input_output_aliases={1: 0},
        scratch_shapes=[pltpu.SemaphoreType.DMA((3,)), pltpu.SemaphoreType.DMA((3,))] + _stage_scratch(shard.dtype, cols),
        compiler_params=pltpu.CompilerParams(vmem_limit_bytes=VMEM_LIMIT),
    )(shard, land)


def _gather_devices(block, name):
    rows, cols = block.shape

    def body(in_ref, out_ref, send_sems, recv_sems, local_sem):
        x, y, c, chips = _position()
        sibling = (x, y, 1 - c)

        def slot(px, py, pc):
            return out_ref.at[4 * px + 2 * py + pc]

        mine = pltpu.make_async_copy(in_ref, slot(x, y, c), local_sem)
        mine.start()
        first = [_remote(in_ref, slot(x, y, c), send_sems.at[0], recv_sems.at[0], sibling)]
        first += [_remote(in_ref, slot(x, y, c), send_sems.at[1 + j], recv_sems.at[1 + j], (cx, cy, c))
                  for j, (cx, cy) in enumerate(chips)]
        for cp in first:
            cp.start()
        passed = []
        for j, (cx, cy) in enumerate(chips):
            got = slot(cx, cy, c)
            _remote(got, got, send_sems.at[1 + j], recv_sems.at[1 + j], sibling).wait_recv()
            cp = _remote(got, got, send_sems.at[4 + j], recv_sems.at[4 + j], sibling)
            cp.start()
            passed.append(cp)
        got = slot(x, y, 1 - c)
        _remote(got, got, send_sems.at[0], recv_sems.at[0], sibling).wait_recv()
        for j, (cx, cy) in enumerate(chips):
            got = slot(cx, cy, 1 - c)
            _remote(got, got, send_sems.at[4 + j], recv_sems.at[4 + j], sibling).wait_recv()
        for cp in first + passed:
            cp.wait_send()
        mine.wait()

    return pl.pallas_call(
        body, name=name, in_specs=[ANY], out_specs=ANY,
        out_shape=jax.ShapeDtypeStruct((N_DEV, rows, cols), block.dtype),
        scratch_shapes=[pltpu.SemaphoreType.DMA((7,)), pltpu.SemaphoreType.DMA((7,)), pltpu.SemaphoreType.DMA],
    )(block)


def _pair_send(grads, name):
    n = len(grads)
    hs = [g.shape[2] for g in grads]
    offs = [sum(hs[:i]) for i in range(n)]
    cols = grads[0].shape[3]

    def body(*refs):
        g_refs = refs[:n]
        got_ref, send_sems, recv_sems = refs[n:]
        x, y, c, _ = _position()
        copies = [_remote(g_ref.at[:, 1 - c], got_ref.at[:, pl.ds(offs[i], hs[i]), :], send_sems.at[i], recv_sems.at[i],
                          (x, y, 1 - c)) for i, g_ref in enumerate(g_refs)]
        for cp in copies:
            cp.start()
        for cp in copies:
            cp.wait()

    return pl.pallas_call(
        body, name=name, in_specs=[ANY] * n, out_specs=ANY, out_shape=jax.ShapeDtypeStruct((N_CHIPS, sum(hs), cols), F32),
        scratch_shapes=[pltpu.SemaphoreType.DMA((n,)), pltpu.SemaphoreType.DMA((n,))],
    )(*grads)


def _pair_copies(g_refs, land_ref, send_sems, recv_sems):
    x, y, c, _ = _position()
    hs = [g.shape[2] for g in g_refs]
    offs = [sum(hs[:i]) for i in range(len(hs))]
    return [_remote(g_ref.at[:, 1 - c], land_ref.at[:, pl.ds(offs[i], hs[i]), :], send_sems.at[i], recv_sems.at[i],
                    (x, y, 1 - c)) for i, g_ref in enumerate(g_refs)]


def _pair_send_start(grads, name):
    n = len(grads)
    land = lax.empty((N_CHIPS, sum(g.shape[2] for g in grads), grads[0].shape[3]), F32)

    def body(*refs):
        for cp in _pair_copies(refs[:n], refs[n], refs[n + 1], refs[n + 2]):
            cp.start()
        refs[-1][...] = jnp.zeros_like(refs[-1])

    buffers = [*grads, land]
    return pl.pallas_call(
        body, name=name,
        out_shape=(pltpu.SemaphoreType.DMA((n,)), pltpu.SemaphoreType.DMA((n,)),
                   *[pltpu.HBM(b.shape, b.dtype) for b in buffers], jax.ShapeDtypeStruct((8, LANES), F32)),
        in_specs=(HBM,) * (n + 1), out_specs=(SEM, SEM, *(HBM,) * (n + 1), pl.BlockSpec(memory_space=pltpu.VMEM)),
        input_output_aliases={i: 2 + i for i in range(n + 1)},
        compiler_params=pltpu.CompilerParams(has_side_effects=SIDE_EFFECT),
    )(*[pltpu.with_memory_space_constraint(b, pltpu.HBM) for b in buffers])


def _pair_send_wait(send_sems, recv_sems, buffers, after, name):
    n = len(buffers) - 1

    def body(*refs):
        for cp in _pair_copies(refs[:n], refs[n], refs[n + 1], refs[n + 2]):
            cp.wait_send()
            cp.wait_recv()

    return pl.pallas_call(
        body, name=name, out_shape=tuple(pltpu.HBM(b.shape, b.dtype) for b in buffers),
        in_specs=(*(HBM,) * (n + 1), SEM, SEM, ANY), out_specs=(HBM,) * (n + 1),
        input_output_aliases={i: i for i in range(n + 1)},
        compiler_params=pltpu.CompilerParams(has_side_effects=SIDE_EFFECT),
    )(*buffers, send_sems, recv_sems, after)


def _pair_add(grads, got, name):
    n = len(grads)
    hs = [g.shape[2] for g in grads]
    offs = [sum(hs[:i]) for i in range(n)]
    cols = grads[0].shape[3]
    hmax = max(hs)
    units = [(i, k) for k in range(N_CHIPS) for i in range(n)]

    def body(*refs):
        g_refs = refs[:n]
        got_ref, out_ref, a_buf, b_buf, o_buf, a_sems, b_sems, o_sems = refs[n:]
        c = lax.axis_index("c")

        def loads(u):
            i, k = units[u]
            slot, rows = u % 2, pl.ds(0, hs[i])
            return (pltpu.make_async_copy(g_refs[i].at[k, c], a_buf.at[slot, rows, :], a_sems.at[slot]),
                    pltpu.make_async_copy(got_ref.at[k, pl.ds(offs[i], hs[i]), :], b_buf.at[slot, rows, :], b_sems.at[slot]))

        def store(u):
            i, k = units[u]
            return pltpu.make_async_copy(o_buf.at[u % 2, pl.ds(0, hs[i]), :], out_ref.at[k, pl.ds(offs[i], hs[i]), :],
                                         o_sems.at[u % 2])

        for cp in loads(0):
            cp.start()
        for u, (i, k) in enumerate(units):
            if u + 1 < len(units):
                for cp in loads(u + 1):
                    cp.start()
            for cp in loads(u):
                cp.wait()
            if u >= 2:
                store(u - 2).wait()
            rows = pl.ds(0, hs[i])
            o_buf[u % 2, rows, :] = (a_buf[u % 2, rows, :] + b_buf[u % 2, rows, :]).astype(BF16)
            store(u).start()
        store(len(units) - 2).wait()
        store(len(units) - 1).wait()

    return pl.pallas_call(
        body, name=name, in_specs=[ANY] * (n + 1), out_specs=ANY,
        out_shape=jax.ShapeDtypeStruct((N_CHIPS, sum(hs), cols), BF16),
        scratch_shapes=[pltpu.VMEM((2, hmax, cols), F32), pltpu.VMEM((2, hmax, cols), F32), pltpu.VMEM((2, hmax, cols), BF16),
                        pltpu.SemaphoreType.DMA((2,)), pltpu.SemaphoreType.DMA((2,)), pltpu.SemaphoreType.DMA((2,))],
        compiler_params=pltpu.CompilerParams(vmem_limit_bytes=VMEM_LIMIT),
    )(*grads, got)


def _chip_exchange(parts, name):
    _, rows, cols = parts.shape

    def body(in_ref, out_ref, send_sems, recv_sems):
        x, y, c, chips = _position()
        sent = [_remote(in_ref.at[2 * cx + cy], out_ref.at[j], send_sems.at[j], recv_sems.at[j], (cx, cy, c))
                for j, (cx, cy) in enumerate(chips)]
        for cp in sent:
            cp.start()
        for cp in sent:
            cp.wait()

    return pl.pallas_call(
        body, name=name, in_specs=[ANY], out_specs=ANY, out_shape=jax.ShapeDtypeStruct((3, rows, cols), parts.dtype),
        scratch_shapes=[pltpu.SemaphoreType.DMA((3,)), pltpu.SemaphoreType.DMA((3,))],
    )(parts)


def _exchange_copies(in_ref, land_ref, send_sems, recv_sems):
    x, y, c, chips = _position()
    return [_remote(in_ref.at[2 * cx + cy], land_ref.at[j], send_sems.at[j], recv_sems.at[j], (cx, cy, c))
            for j, (cx, cy) in enumerate(chips)]


def _exchange_start(parts, name):
    _, rows, cols = parts.shape

    def body(in_ref, land_ref, send_sems, recv_sems, in_thru, land_thru, token):
        for cp in _exchange_copies(in_ref, land_ref, send_sems, recv_sems):
            cp.start()
        token[...] = jnp.zeros_like(token)

    land = lax.empty((3, rows, cols), parts.dtype)
    return pl.pallas_call(
        body, name=name,
        out_shape=(pltpu.SemaphoreType.DMA((3,)), pltpu.SemaphoreType.DMA((3,)), pltpu.HBM(parts.shape, parts.dtype),
                   pltpu.HBM(land.shape, land.dtype), jax.ShapeDtypeStruct((8, LANES), F32)),
        in_specs=(HBM, HBM), out_specs=(SEM, SEM, HBM, HBM, pl.BlockSpec(memory_space=pltpu.VMEM)),
        input_output_aliases={0: 2, 1: 3},
        compiler_params=pltpu.CompilerParams(has_side_effects=SIDE_EFFECT),
    )(pltpu.with_memory_space_constraint(parts, pltpu.HBM), pltpu.with_memory_space_constraint(land, pltpu.HBM))


def _exchange_wait(send_sems, recv_sems, parts, land, after, name):
    def body(in_ref, land_ref, send_sems, recv_sems, after_ref, in_out, land_out):
        for cp in _exchange_copies(in_ref, land_ref, send_sems, recv_sems):
            cp.wait_send()
            cp.wait_recv()

    return pl.pallas_call(
        body, name=name, out_shape=(pltpu.HBM(parts.shape, parts.dtype), pltpu.HBM(land.shape, land.dtype)),
        in_specs=(HBM, HBM, SEM, SEM, ANY), out_specs=(HBM, HBM), input_output_aliases={0: 0, 1: 1},
        compiler_params=pltpu.CompilerParams(has_side_effects=SIDE_EFFECT),
    )(parts, land, send_sems, recv_sems, after)


def _chip_sum(parts, recv, chip, name):
    _, rows, cols = parts.shape
    tm = _tile(rows, 512, 16)

    def body(chip_ref, own_ref, recv_ref, o_ref):
        acc = own_ref[0].astype(F32)
        for j in range(3):
            acc = acc + recv_ref[j].astype(F32)
        o_ref[...] = acc

    return pl.pallas_call(
        body, name=name,
        grid_spec=pltpu.PrefetchScalarGridSpec(
            num_scalar_prefetch=1, grid=(rows // tm,),
            in_specs=[pl.BlockSpec((1, tm, cols), lambda i, chip_ref: (chip_ref[0], i, 0)),
                      pl.BlockSpec((3, tm, cols), lambda i, chip_ref: (0, i, 0))],
            out_specs=pl.BlockSpec((tm, cols), lambda i, chip_ref: (i, 0))),
        out_shape=jax.ShapeDtypeStruct((rows, cols), F32), compiler_params=_params("parallel"),
    )(chip, parts, recv)


def _join_unpack(mine, hs, groups, name):
    n = len(hs)
    offs = [sum(hs[:i]) for i in range(n)]
    cols = mine.shape[1]
    n_out = max(groups) + 1
    base = [2 * sum(h for h, g in zip(hs[:i], groups[:i]) if g == groups[i]) for i in range(n)]
    out_rows = [2 * sum(h for h, g in zip(hs, groups) if g == k) for k in range(n_out)]

    def body(in_ref, *refs):
        outs = refs[:n_out]
        send_sems, recv_sems, buf, in_sems, out_sems = refs[n_out:]
        x, y, c, _ = _position()
        sibling = (x, y, 1 - c)
        sent, local = [], []
        for i in range(n):
            src = in_ref.at[pl.ds(offs[i], hs[i]), :]
            here = outs[groups[i]].at[pl.ds(base[i] + c * hs[i], hs[i]), :]
            cp = _remote(src, here, send_sems.at[i], recv_sems.at[i], sibling)
            cp.start()
            sent.append(cp)
            local.append((src, here))
        _staged_copies(local, buf, in_sems, out_sems)
        for i, cp in enumerate(sent):
            there = outs[groups[i]].at[pl.ds(base[i] + (1 - c) * hs[i], hs[i]), :]
            _remote(there, there, send_sems.at[i], recv_sems.at[i], sibling).wait_recv()
            cp.wait_send()

    assert max(hs) <= STAGE_ROWS
    return pl.pallas_call(
        body, name=name, in_specs=[ANY], out_specs=[ANY] * n_out,
        out_shape=[jax.ShapeDtypeStruct((r, cols), F32) for r in out_rows],
        scratch_shapes=[pltpu.SemaphoreType.DMA((n,)), pltpu.SemaphoreType.DMA((n,))] + _stage_scratch(F32, cols),
        compiler_params=pltpu.CompilerParams(vmem_limit_bytes=VMEM_LIMIT),
    )(mine)


SMALL_ROWS = 16
SMALL_PACK_ROWS = 256


def _small_rows(n):
    return -(-n // (SMALL_ROWS * LANES)) * SMALL_ROWS


def _pack_small(arrs):
    parts = []
    for a in arrs:
        flat = a.reshape(-1)
        rows = _small_rows(flat.shape[0])
        flat = jnp.pad(flat, (0, rows * LANES - flat.shape[0]))
        parts.append(flat.reshape(rows, LANES))
    total = sum(p.shape[0] for p in parts)
    parts.append(jnp.zeros((-total % SMALL_PACK_ROWS, LANES), F32))
    return jnp.concatenate(parts, axis=0)


def _unpack_small(packed, shapes):
    out, r = [], 0
    for sh in shapes:
        n = math.prod(sh)
        cnt = _small_rows(n)
        out.append(packed[r:r + cnt].reshape(-1)[:n].reshape(sh))
        r += cnt
    return out


def _ffn_bwd(dh, dhb, h_in, saved, g_norm, w_gate_t, w_up_t, w_down, tag, after=None):
    n, gate, up, act = saved
    dgate, dup = _ffn_dact(dhb, w_down, gate, up, f"{tag}_dact", after)
    dw_down = _matmul(act, dhb, trans_a=True, name=f"{tag}_dwdown")
    dw_gate_t = _matmul(dgate, n, trans_a=True, name=f"{tag}_dwgate")
    dw_up_t = _matmul(dup, n, trans_a=True, name=f"{tag}_dwup")
    dh_in, dh_inb, dg = _dn_norm([(dgate, w_gate_t), (dup, w_up_t)], h_in, g_norm, dh, f"{tag}_dnorm")
    return dh_in, dh_inb, dg, dw_gate_t, dw_up_t, dw_down


def _local_step(x, tgt, w, big, late_weights, reduce_send, reduce_exchange):
    s = x.shape[0]
    tabs = _rope_tables(s)
    grads, gbig = {}, {}

    g_ev = w['ev_norm_g']
    n1 = _rms_fwd(x, g_ev, "ev_norm")
    proj0 = _matmul(n1, big['ev_w_in', 0], trans_b=True, name="ev_in", out_dtype=BF16, rows_inner=True)
    q0, k0, v0 = _qkv_prep_even(proj0, tabs, "ev_qkv")
    sinks = w['ev_sinks'].reshape(-1)
    o0, lse0, o0b = _attn_fwd(q0, k0, v0, sinks, max_dist=BLOCK - 1, name="ev_attn", emit_bf16=True)
    yconv, cout = _conv_fwd(proj0, w['ev_conv_w'][0], w['ev_conv_b'], w['ev_conv_ln_g'], w['ev_conv_ln_b'], "ev_conv")
    mix0 = (o0b[0], cout)
    g_f0 = w['ffn_norm_g'][0:1]
    h1, n2 = _matmul_norm(mix0, big['ev_w_out', 0], x, g_f0, "ev_out")
    big = {**big, **late_weights(h1)}

    g_od = w['od_norm_g']
    act0, gate0, up0 = _ffn_gate_up(n2, big['ffn_w_gate', 0], big['ffn_w_up', 0], "ffn0_gate_up")
    h2, n3 = _matmul_norm(act0, big['ffn_w_down', 0], h1, g_od, "ffn0_down")
    ffn0 = (n2, gate0, up0, act0)

    proj1 = _matmul(n3, big['od_w_in', 0], trans_b=True, name="od_in", out_dtype=BF16, rows_inner=True)
    qkv = _qkv_prep_odd(proj1, tabs, "od_qkv")
    nb = len(DILATIONS)
    outs, lses = [], []
    for i, d in enumerate(DILATIONS):
        o_r, lse_r = _attn_fwd(qkv[i], qkv[nb + i], qkv[2 * nb + i], None, max_dist=BLOCK, name=f"od_attn{d}", o_dtype=BF16)
        outs.append(o_r)
        lses.append(lse_r)
    comb = _combine(outs, lses, "od_combine")
    c_bf16 = comb[0]
    c_fold = {1: comb[1]}
    lse_fold = {1: comb[2]}
    for i, d in enumerate(DILATIONS[1:]):
        c_fold[d], lse_fold[d] = comb[3 + 2 * i], comb[4 + 2 * i]
    w_sp = w['od_spatial_w'][0]
    sb_t = w['od_spatial_b'][0].T
    mixed, dout = _gate_fwd(proj1, w['od_sgu_ln_g'], w['od_sgu_ln_b'], w_sp, sb_t, "od_gate")
    mix1 = (c_bf16, dout)
    g_f1 = w['ffn_norm_g'][1:2]
    h3, n4 = _matmul_norm(mix1, big['od_w_out', 0], h2, g_f1, "od_out")
    act1, gate1, up1 = _ffn_gate_up(n4, big['ffn_w_gate', 1], big['ffn_w_up', 1], "ffn1_gate_up")
    ffn1 = (n4, gate1, up1, act1)

    dh4, dh4b, dg_final, loss_tile = _matmul_final(act1, big['ffn_w_down', 1], h3, w['final_norm_g'].reshape(1, D_MODEL),
                                                   tgt, "ffn1_down_loss")
    grads['final_norm_g'] = dg_final.reshape(D_MODEL)

    dh3, dh3b, dg_f1, gbig['ffn_w_gate', 1], gbig['ffn_w_up', 1], gbig['ffn_w_down', 1] = _ffn_bwd(
        dh4, dh4b, h3, ffn1, g_f1, big['ffn_w_gate', 1], big['ffn_w_up', 1], big['ffn_w_down', 1], "ffn1")

    dmix1 = _matmul(dh3b, big['od_w_out', 0], trans_b=True, name="od_dmix")
    gbig['od_w_out', 0] = _matmul_tn_pair(mix1[0], mix1[1], dh3b, "od_dwout")
    do_fold = dict(zip(DILATIONS[1:], _fold_dout(dmix1, "od_fold_dout")))
    do_fold[1] = dmix1[None]
    dqs, dks, dvs = [], [], []
    for i, d in enumerate(DILATIONS):
        dq_r, dk_r, dv_r = _attn_bwd(qkv[i], qkv[nb + i], qkv[2 * nb + i], do_fold[d], c_fold[d], lse_fold[d], None,
                                     max_dist=BLOCK, name=f"od_dattn{d}")
        dqs.append(dq_r)
        dks.append(dk_r)
        dvs.append(dv_r)
    dz, dg_sgu, db_sgu, dw_sp, dsb = _gate_bwd(dmix1, proj1, mixed, w['od_sgu_ln_g'], w['od_sgu_ln_b'], w_sp, "od_dgate")
    grads['od_sgu_ln_g'], grads['od_sgu_ln_b'] = dg_sgu, db_sgu
    grads['od_spatial_w'], grads['od_spatial_b'] = dw_sp[None], dsb[None]
    dproj1 = _qkv_post_odd(dqs, dks, dvs, dz, tabs, "od_dproj")
    gbig['od_w_in', 0] = _matmul(dproj1, n3, trans_a=True, name="od_dwin")
    dh2, dh2b, dg_od = _dn_norm([(dproj1, big['od_w_in', 0])], h2, g_od, dh3, "od_dnorm")
    grads['od_norm_g'] = dg_od
    token = reduce_send(0, gbig)

    dh1, dh1b, dg_f0, gbig['ffn_w_gate', 0], gbig['ffn_w_up', 0], gbig['ffn_w_down', 0] = _ffn_bwd(
        dh2, dh2b, h1, ffn0, g_f0, big['ffn_w_gate', 0], big['ffn_w_up', 0], big['ffn_w_down', 0], "ffn0", token)
    grads['ffn_norm_g'] = jnp.concatenate([dg_f0, dg_f1], axis=0)
    token = reduce_exchange(0, dh1) + reduce_send(1, gbig)

    dmix0 = _matmul(dh1b, big['ev_w_out', 0], trans_b=True, name="ev_dmix", after=token)
    gbig['ev_w_out', 0] = _matmul_tn_pair(mix0[0], mix0[1], dh1b, "ev_dwout")
    dq0, dk0, dv0, dsink = _attn_bwd(q0, k0, v0, dmix0[None], o0, lse0, sinks, max_dist=BLOCK - 1, name="ev_dattn")
    grads['ev_sinks'] = dsink[:, 0, :].reshape(N_PAIRS, 2, HEAD_DIM)[:, :, 0].reshape(1, 8)
    token = reduce_exchange(1, dq0)
    dyc, dg_cln, db_cln, dcb = _conv_tail_bwd(dmix0, yconv, w['ev_conv_ln_g'] + token[0:1, 0:1], w['ev_conv_ln_b'],
                                              "ev_dconv_tail")
    grads['ev_conv_ln_g'], grads['ev_conv_ln_b'], grads['ev_conv_b'] = dg_cln, db_cln, dcb
    dglu, dconv_w = _conv_bwd(proj0, dyc, w['ev_conv_w'][0], "ev_dconv")
    grads['ev_conv_w'] = dconv_w[None]
    dproj0 = _qkv_post_even(dq0, dk0, dv0, dglu, tabs, "ev_dproj")
    gbig['ev_w_in', 0] = _matmul(dproj0, n1, trans_a=True, name="ev_dwin")
    dx, _, dg_ev = _dn_norm([(dproj0, big['ev_w_in', 0])], x, g_ev, dh1, "ev_dnorm")
    grads['ev_norm_g'] = dg_ev
    return loss_tile, dx, grads, gbig


def _shard_rows(w, layer, by_cols):
    return w[layer].T if by_cols else w[layer]


def kernel(x, ev_norm_g, ev_w_in, ev_sinks, ev_conv_w, ev_conv_b, ev_conv_ln_g, ev_conv_ln_b, ev_w_out, od_norm_g, od_w_in, od_sgu_ln_g, od_sgu_ln_b, od_spatial_w, od_spatial_b, od_w_out, ffn_norm_g, ffn_w_gate, ffn_w_up, ffn_w_down, final_norm_g, loss_target, m_ev_norm_g, m_ev_w_in, m_ev_sinks, m_ev_conv_w, m_ev_conv_b, m_ev_conv_ln_g, m_ev_conv_ln_b, m_ev_w_out, m_od_norm_g, m_od_w_in, m_od_sgu_ln_g, m_od_sgu_ln_b, m_od_spatial_w, m_od_spatial_b, m_od_w_out, m_ffn_norm_g, m_ffn_w_gate, m_ffn_w_up, m_ffn_w_down, m_final_norm_g, v_ev_norm_g, v_ev_w_in, v_ev_sinks, v_ev_conv_w, v_ev_conv_b, v_ev_conv_ln_g, v_ev_conv_ln_b, v_ev_w_out, v_od_norm_g, v_od_w_in, v_od_sgu_ln_g, v_od_sgu_ln_b, v_od_spatial_w, v_od_spatial_b, v_od_w_out, v_ffn_norm_g, v_ffn_w_gate, v_ffn_w_up, v_ffn_w_down, v_final_norm_g):
    given = dict(locals())
    wts = {n: given[n] for n in WEIGHTS}
    mom = {n: given["m_" + n] for n in WEIGHTS}
    var = {n: given["v_" + n] for n in WEIGHTS}
    chip = 2 * lax.axis_index("x") + lax.axis_index("y")

    shard_rows = [_shard_rows(wts[n], layer, by_cols).astype(BF16) for n, layer, by_cols in BIG]
    counts = [a.shape[0] for a in shard_rows]
    n_first = sum(n.startswith('ev_') for n, _, _ in BIG)

    def unpack(stacked, entries, cnts):
        out, r = {}, 0
        for (n, layer, _), cnt in zip(entries, cnts):
            out[n, layer] = stacked[:, r:r + cnt].reshape(N_CHIPS * cnt, D_MODEL)
            r += cnt
        return out

    first_w = _gather_chips(jnp.concatenate(shard_rows[:n_first], axis=0), "gather_weights_ev")
    big = unpack(first_w, BIG[:n_first], counts[:n_first])
    send_sems, recv_sems, late_shard, late_land, token = _gather_start(jnp.concatenate(shard_rows[n_first:], axis=0),
                                                                      first_w, "gather_weights_start")

    def late_weights(after):
        shard, land = _gather_wait(send_sems, recv_sems, late_shard, late_land, after, "gather_weights_wait")
        return unpack(_gather_finish(shard, land, "gather_weights_finish"), BIG[n_first:], counts[n_first:])

    full = {n: wts[n] for n in SMALL_REPL}
    full['ev_norm_g'] = full['ev_norm_g'] + token[0:1, 0:1]
    small_shards = [wts[n] for n in SMALL_SHARDED]
    small_shapes = [a.shape for a in small_shards]
    all_s = _gather_chips(_pack_small(small_shards), "gather_small_weights")
    per_chip = [_unpack_small(all_s[k], small_shapes) for k in range(N_CHIPS)]
    for i, n in enumerate(SMALL_SHARDED):
        full[n] = jnp.concatenate([per_chip[k][i] for k in range(N_CHIPS)], axis=-1)

    half_rows = {(n, layer): cnt // 2 for (n, layer, _), cnt in zip(BIG, counts)}
    in_flight = []

    sending = {}

    def halves(stage, gbig):
        return [gbig[e].reshape(N_CHIPS, 2, half_rows[e], D_MODEL) for e in GRAD_STAGES[stage]]

    def reduce_send(stage, gbig):
        send_sems, recv_sems, *buffers, token = _pair_send_start(halves(stage, gbig), f"grad_pair_start{stage}")
        sending[stage] = (send_sems, recv_sems, buffers)
        return token

    def reduce_exchange(stage, after):
        send_sems, recv_sems, buffers = sending.pop(stage)
        *split, got = _pair_send_wait(send_sems, recv_sems, buffers, after, f"grad_pair_wait{stage}")
        chip_part = _pair_add(split, got, f"grad_pair_add{stage}")
        *handles, token = _exchange_start(chip_part, f"grad_exchange_start{stage}")
        in_flight.append(handles)
        return token

    loss_tile, grad_x, grads, gbig = _local_step(x[0], loss_target[0], full, big, late_weights, reduce_send, reduce_exchange)
    loss = lax.psum(loss_tile[0, 0], ("x", "y", "c"))

    reduced = {}
    for stage, entries in enumerate(GRAD_STAGES):
        if stage < len(in_flight):
            chip_part, from_chips = _exchange_wait(*in_flight[stage], grad_x, f"grad_exchange_wait{stage}")
        else:
            split = halves(stage, gbig)
            chip_part = _pair_add(split, _pair_send(split, f"grad_pair_send{stage}"), f"grad_pair_add{stage}")
            from_chips = _chip_exchange(chip_part, f"grad_chip_exchange{stage}")
        my_half = _chip_sum(chip_part, from_chips, chip.reshape(1), f"grad_chip_sum{stage}")
        joined = _join_unpack(my_half, [half_rows[e] for e in entries], list(range(len(entries))), f"grad_join_halves{stage}")
        reduced.update(zip(entries, joined))

    small_names = SMALL_REPL + SMALL_SHARDED
    small_full_shapes = [grads[n].shape for n in small_names]
    spack = _pack_small([grads[n] for n in small_names])
    s_all = _gather_devices(spack, "grad_small_gather")
    s_sum = _unpack_small(_ordered_sum(s_all, "grad_small_sum"), small_full_shapes)
    g_all = dict(zip(small_names, s_sum))
    for n in SMALL_SHARDED:
        width = wts[n].shape[-1]
        g_all[n] = lax.dynamic_slice_in_dim(g_all[n], chip * width, width, axis=g_all[n].ndim - 1)

    delta, new_m, new_v = {}, {}, {}
    for n in BIG_NAMES:
        by_cols = [bc for nn, _, bc in BIG if nn == n][0]
        layers = wts[n].shape[0]

        def as_rows(a):
            return (jnp.swapaxes(a, 1, 2) if by_cols else a).reshape(-1, D_MODEL)

        def from_rows(a):
            a = a.reshape(layers, -1, D_MODEL)
            return jnp.swapaxes(a, 1, 2) if by_cols else a

        g_rows = [reduced[n, layer] for layer in range(layers)]
        g_rows = g_rows[0] if layers == 1 else jnp.concatenate(g_rows, axis=0)
        updated = _adamw(as_rows(wts[n]), g_rows, as_rows(mom[n]), as_rows(var[n]), f"adamw_{n}")
        g_all[n] = from_rows(g_rows)
        delta[n], new_m[n], new_v[n] = (from_rows(a) for a in updated)
    shapes = [wts[n].shape for n in small_names]
    d_s, m_s, v_s = _adamw(*[_pack_small([src[n] for n in small_names]) for src in (wts, g_all, mom, var)], "adamw_small")
    for dst, packed in ((delta, d_s), (new_m, m_s), (new_v, v_s)):
        dst.update(zip(small_names, _unpack_small(packed, shapes)))

    return (loss, grad_x[None], *[g_all[n] for n in WEIGHTS], *[delta[n] for n in WEIGHTS],
            *[new_m[n] for n in WEIGHTS], *[new_v[n] for n in WEIGHTS])
```

```python
import math

import jax
import jax.numpy as jnp
from jax import lax
from jax.experimental import pallas as pl
from jax.experimental.pallas import tpu as pltpu

F32 = jnp.float32
BF16 = jnp.bfloat16

D_MODEL = 1024
HEAD_DIM = 64
ROT_DIM = 16
ROPE_THETA = 500000.0
RMS_EPS = 1e-6
LN_EPS = 1e-5
BLOCK = 128
CONV_WIDTH = 31
CONV_HALO = 32
CONV_ROWS = 64
D_FF = 2816
N_GROUPS = 8
ATTN_W = 512
ATTN_SCALE = HEAD_DIM ** -0.5
NEG = -1e30
DILATIONS = (1, 4, 16)

ADAM_LR = 0.001
ADAM_B1 = 0.9
ADAM_B2 = 0.999
ADAM_EPS = 1e-08
ADAM_WD = 0.01
ADAM_STEP = 10

LANES = 128
N_PAIRS = ATTN_W // LANES
VMEM_LIMIT = 56 * 1024 * 1024
MESH = pl.DeviceIdType.MESH
N_CHIPS = 4
N_DEV = 8

WEIGHTS = ['ev_norm_g', 'ev_w_in', 'ev_sinks', 'ev_conv_w', 'ev_conv_b', 'ev_conv_ln_g', 'ev_conv_ln_b', 'ev_w_out',
           'od_norm_g', 'od_w_in', 'od_sgu_ln_g', 'od_sgu_ln_b', 'od_spatial_w', 'od_spatial_b', 'od_w_out',
           'ffn_norm_g', 'ffn_w_gate', 'ffn_w_up', 'ffn_w_down', 'final_norm_g']
BIG = [('ev_w_in', 0, True), ('ev_w_out', 0, False), ('od_w_in', 0, True), ('od_w_out', 0, False),
       ('ffn_w_gate', 0, True), ('ffn_w_gate', 1, True), ('ffn_w_up', 0, True), ('ffn_w_up', 1, True),
       ('ffn_w_down', 0, False), ('ffn_w_down', 1, False)]
BIG_NAMES = ['ev_w_in', 'ev_w_out', 'od_w_in', 'od_w_out', 'ffn_w_gate', 'ffn_w_up', 'ffn_w_down']
GRAD_STAGES = ([('od_w_in', 0), ('od_w_out', 0), ('ffn_w_gate', 1), ('ffn_w_up', 1), ('ffn_w_down', 1)],
               [('ffn_w_gate', 0), ('ffn_w_up', 0), ('ffn_w_down', 0)],
               [('ev_w_in', 0), ('ev_w_out', 0)])
SMALL_SHARDED = ['ev_conv_w', 'od_norm_g', 'od_sgu_ln_g', 'od_sgu_ln_b']
SMALL_REPL = ['ev_norm_g', 'ev_sinks', 'ev_conv_b', 'ev_conv_ln_g', 'ev_conv_ln_b', 'od_spatial_w', 'od_spatial_b',
              'ffn_norm_g', 'final_norm_g']


def _tile(n, cap, mult=LANES):
    best = None
    for t in range(mult, min(n, cap) + 1, mult):
        if n % t == 0:
            best = t
    assert best is not None, (n, cap)
    return best


def _params(*sem):
    return pltpu.CompilerParams(dimension_semantics=sem, vmem_limit_bytes=VMEM_LIMIT)


def _sigmoid(x):
    return 1.0 / (1.0 + jnp.exp(-x))


def _pair_block(p):
    return slice(p * LANES, (p + 1) * LANES)


def _matmul(a, b, *, name, trans_a=False, trans_b=False, add=None, out_dtype=F32, after=None, rows_inner=False):
    parts = a if isinstance(a, (tuple, list)) else (a,)
    if trans_a:
        k, m = parts[0].shape
    else:
        m = parts[0].shape[0]
        k = sum(p.shape[1] for p in parts)
    if trans_b:
        n, k2 = b.shape
    else:
        k2, n = b.shape
    assert k == k2 and b.dtype == BF16 and all(p.dtype == BF16 for p in parts)
    tm = _tile(m, D_FF // 2 if trans_a else 512)
    tn = _tile(n, D_FF // 2)
    tk = k if k <= D_FF else _tile(k, 2048)
    nk = k // tk
    na = len(parts)
    assert na == 1 or (nk == 1 and not trans_a)
    assert nk == 1 or out_dtype == F32
    dims = (((0 if trans_a else 1,), (1 if trans_b else 0,)), ((), ()))
    has_add = add is not None

    def body(*refs):
        a_refs, b_ref = refs[:na], refs[na]
        add_ref = refs[na + 1] if has_add else None
        o_ref = refs[na + 1 + has_add + (after is not None)]
        def product():
            a_val = a_refs[0][...] if na == 1 else jnp.concatenate([r[...] for r in a_refs], axis=1)
            return lax.dot_general(a_val, b_ref[...], dims, preferred_element_type=F32)

        if nk == 1:
            part = product()
            if has_add:
                part = part + add_ref[...]
            o_ref[...] = part.astype(o_ref.dtype)
            return
        kk = pl.program_id(2)

        @pl.when(kk == 0)
        def _():
            o_ref[...] = product() + add_ref[...] if has_add else product()

        @pl.when(kk > 0)
        def _():
            o_ref[...] = product() + o_ref[...]

    def at(f):
        return (lambda j, i, kk: f(i, j, kk)) if rows_inner else f

    if trans_a:
        a_specs = [pl.BlockSpec((tk, tm), at(lambda i, j, kk: (kk, i)))]
    elif na == 1:
        a_specs = [pl.BlockSpec((tm, tk), at(lambda i, j, kk: (i, kk)))]
    else:
        a_specs = [pl.BlockSpec((tm, p.shape[1]), at(lambda i, j, kk: (i, 0))) for p in parts]
    b_spec = (pl.BlockSpec((tn, tk), at(lambda i, j, kk: (j, kk))) if trans_b
              else pl.BlockSpec((tk, tn), at(lambda i, j, kk: (kk, j))))
    o_spec = pl.BlockSpec((tm, tn), at(lambda i, j, kk: (i, j)))
    in_specs = a_specs + [b_spec] + ([o_spec] if has_add else [])
    operands = list(parts) + [b] + ([add] if has_add else [])
    if after is not None:
        in_specs.append(_after_spec(after))
        operands.append(after)
    grid = (n // tn, m // tm, nk) if rows_inner else (m // tm, n // tn, nk)
    return pl.pallas_call(
        body, name=name, grid=grid, in_specs=in_specs, out_specs=o_spec,
        out_shape=jax.ShapeDtypeStruct((m, n), out_dtype),
        compiler_params=_params("parallel", "parallel", "arbitrary"),
    )(*operands)


def _matmul_rows(a, b, add, epilogue, consts, tiled, outs, accs, name):
    parts = a if isinstance(a, (tuple, list)) else (a,)
    m = parts[0].shape[0]
    tm = 512
    na, nc, nt, no = len(parts), len(consts), len(tiled), len(outs)

    def body(*refs):
        a_refs, b_ref, add_ref = refs[:na], refs[na], refs[na + 1]
        const_refs = refs[na + 2:na + 2 + nc]
        tiled_refs = refs[na + 2 + nc:na + 2 + nc + nt]
        out_refs = refs[na + 2 + nc + nt:]
        a_val = a_refs[0][...] if na == 1 else jnp.concatenate([r[...] for r in a_refs], axis=1)
        h = jnp.dot(a_val, b_ref[...], preferred_element_type=F32) + add_ref[...]
        results = epilogue(h, [r[...] for r in const_refs], [r[...] for r in tiled_refs])
        for o_ref, val in zip(out_refs[:no], results[:no]):
            o_ref[...] = val.astype(o_ref.dtype)
        if accs:
            @pl.when(_first_step())
            def _():
                for o_ref in out_refs[no:]:
                    o_ref[...] = jnp.zeros_like(o_ref)

            for o_ref, val in zip(out_refs[no:], results[no:]):
                o_ref[...] += val

    row = lambda w: pl.BlockSpec((tm, w), lambda i: (i, 0))
    whole = lambda shape: pl.BlockSpec(shape, lambda i: (0,) * len(shape))
    return pl.pallas_call(
        body, name=name, grid=(m // tm,),
        in_specs=[row(p.shape[1]) for p in parts] + [whole(b.shape), row(D_MODEL)] + [whole(c.shape) for c in consts]
        + [row(t.shape[1]) for t in tiled],
        out_specs=[row(c) for c, _ in outs] + [whole(sh) for sh, _ in accs],
        out_shape=[jax.ShapeDtypeStruct((m, c), dt) for c, dt in outs] + [jax.ShapeDtypeStruct(sh, dt) for sh, dt in accs],
        compiler_params=_params("arbitrary"),
    )(*parts, b, add, *consts, *tiled)


def _matmul_norm(a, b, add, g, name):
    def epilogue(h, consts, tiled):
        r = lax.rsqrt(jnp.mean(h * h, axis=-1, keepdims=True) + RMS_EPS)
        return [h, h * r * consts[0]]

    return _matmul_rows(a, b, add, epilogue, [g], [], [(D_MODEL, F32), (D_MODEL, BF16)], [], name)


def _matmul_final(a, b, add, g, tgt, name):
    def epilogue(h, consts, tiled):
        gg = consts[0]
        r = lax.rsqrt(jnp.mean(h * h, axis=-1, keepdims=True) + RMS_EPS)
        xh = h * r
        e = xh * gg - tiled[0]
        loss = (0.5 / D_MODEL) * jnp.sum(jnp.sum(e * e, axis=-1, keepdims=True), axis=0, keepdims=True)
        dy = e * (1.0 / D_MODEL)
        dxh = dy * gg
        dx = r * (dxh - xh * jnp.mean(dxh * xh, axis=-1, keepdims=True))
        return [dx, dx, jnp.sum(dy * xh, axis=0, keepdims=True), jnp.broadcast_to(loss, (1, LANES))]

    return _matmul_rows(a, b, add, epilogue, [g], [tgt], [(D_MODEL, F32), (D_MODEL, BF16)],
                        [((1, D_MODEL), F32), ((1, LANES), F32)], name)


def _matmul_tn_pair(a1, a2, b, name):
    kdim, m1 = a1.shape
    m2 = a2.shape[1]
    n = b.shape[1]
    tn = _tile(n, 1024)
    tk = _tile(kdim, 2048)
    nk = kdim // tk
    dims = (((0,), (0,)), ((), ()))

    def body(a1_ref, a2_ref, b_ref, o_ref):
        kk = pl.program_id(1)
        def products():
            bv = b_ref[...]
            return (lax.dot_general(a1_ref[...], bv, dims, preferred_element_type=F32),
                    lax.dot_general(a2_ref[...], bv, dims, preferred_element_type=F32))

        @pl.when(kk == 0)
        def _():
            o_ref[0:m1, :], o_ref[m1:, :] = products()

        @pl.when(kk > 0)
        def _():
            top, bot = products()
            o_ref[0:m1, :] = top + o_ref[0:m1, :]
            o_ref[m1:, :] = bot + o_ref[m1:, :]

    return pl.pallas_call(
        body, name=name, grid=(n // tn, nk),
        in_specs=[pl.BlockSpec((tk, m1), lambda j, kk: (kk, 0)), pl.BlockSpec((tk, m2), lambda j, kk: (kk, 0)),
                  pl.BlockSpec((tk, tn), lambda j, kk: (kk, j))],
        out_specs=pl.BlockSpec((m1 + m2, tn), lambda j, kk: (0, j)),
        out_shape=jax.ShapeDtypeStruct((m1 + m2, n), F32),
        compiler_params=_params("parallel", "arbitrary"),
    )(a1, a2, b)


def _ffn_gate_up(n, w_gate_t, w_up_t, name):
    m, k = n.shape
    f = w_gate_t.shape[0]
    tm, tn = _tile(m, 1024), _tile(f, D_FF // 2)

    def body(n_ref, wg_ref, wu_ref, act_ref, gate_ref, up_ref):
        a = n_ref[...]

        def products(cols):
            return (lax.dot_general(a, wg_ref[cols, :], NT, preferred_element_type=F32),
                    lax.dot_general(a, wu_ref[cols, :], NT, preferred_element_type=F32))

        chunks = _col_chunks(tn)
        ahead = products(chunks[0])
        for idx, cols in enumerate(chunks):
            gate, up = ahead
            if idx + 1 < len(chunks):
                ahead = products(chunks[idx + 1])
            act_ref[:, cols] = (gate * _sigmoid(gate) * up).astype(BF16)
            gate_ref[:, cols] = gate.astype(BF16)
            up_ref[:, cols] = up.astype(BF16)

    wspec = pl.BlockSpec((tn, k), lambda j, i: (j, 0))
    ospec = pl.BlockSpec((tm, tn), lambda j, i: (i, j))
    return pl.pallas_call(
        body, name=name, grid=(f // tn, m // tm), in_specs=[pl.BlockSpec((tm, k), lambda j, i: (i, 0)), wspec, wspec],
        out_specs=[ospec] * 3, out_shape=[jax.ShapeDtypeStruct((m, f), BF16)] * 3,
        compiler_params=_params("parallel", "parallel"),
    )(n, w_gate_t, w_up_t)


def _col_chunks(n, width=384):
    return [slice(c, min(c + width, n)) for c in range(0, n, width)]


def _after_spec(after):
    return pl.BlockSpec(after.shape, lambda *_: (0,) * after.ndim)


def _ffn_dact(dhb, w_down, gate, up, name, after=None):
    m, k = dhb.shape
    f = w_down.shape[0]
    tm, tn = _tile(m, 1024), _tile(f, D_FF // 2)

    def body(d_ref, w_ref, g_ref, u_ref, *rest):
        dg_ref, du_ref = rest[-2:]
        d = d_ref[...]

        def product(cols):
            return lax.dot_general(d, w_ref[cols, :], NT, preferred_element_type=F32)

        chunks = _col_chunks(tn)
        ahead = product(chunks[0])
        for idx, cols in enumerate(chunks):
            dact = ahead
            if idx + 1 < len(chunks):
                ahead = product(chunks[idx + 1])
            g = g_ref[:, cols].astype(F32)
            sg = _sigmoid(g)
            dg_ref[:, cols] = (dact * u_ref[:, cols].astype(F32) * sg * (1.0 + g * (1.0 - sg))).astype(BF16)
            du_ref[:, cols] = (dact * g * sg).astype(BF16)

    ospec = pl.BlockSpec((tm, tn), lambda j, i: (i, j))
    extra = [] if after is None else [after]
    return pl.pallas_call(
        body, name=name, grid=(f // tn, m // tm),
        in_specs=[pl.BlockSpec((tm, k), lambda j, i: (i, 0)), pl.BlockSpec((tn, k), lambda j, i: (j, 0)), ospec, ospec]
        + [_after_spec(a) for a in extra],
        out_specs=[ospec] * 2, out_shape=[jax.ShapeDtypeStruct((m, f), BF16)] * 2,
        compiler_params=_params("parallel", "parallel"),
    )(dhb, w_down, gate, up, *extra)


def _dn_norm(pairs, h, g, dres, name):
    m = h.shape[0]
    tm = 512
    np_ = len(pairs)

    def body(*refs):
        a_refs, b_refs = refs[:np_], refs[np_:2 * np_]
        h_ref, dres_ref, g_ref, dh_ref, dhb_ref, dg_ref = refs[2 * np_:]

        @pl.when(_first_step())
        def _():
            dg_ref[...] = jnp.zeros_like(dg_ref)

        dy = jnp.dot(a_refs[0][...], b_refs[0][...], preferred_element_type=F32)
        for a_ref, b_ref in zip(a_refs[1:], b_refs[1:]):
            dy = jnp.dot(a_ref[...], b_ref[...], preferred_element_type=F32) + dy
        x = h_ref[...]
        r = lax.rsqrt(jnp.mean(x * x, axis=-1, keepdims=True) + RMS_EPS)
        xh = x * r
        dg_ref[...] += jnp.sum(dy * xh, axis=0, keepdims=True)
        dxh = dy * g_ref[...]
        tot = dres_ref[...] + r * (dxh - xh * jnp.mean(dxh * xh, axis=-1, keepdims=True))
        dh_ref[...] = tot
        dhb_ref[...] = tot.astype(BF16)

    row = lambda w: pl.BlockSpec((tm, w), lambda i: (i, 0))
    whole = lambda a: pl.BlockSpec(a.shape, lambda i: (0, 0))
    a_list, b_list = [a for a, _ in pairs], [b for _, b in pairs]
    return pl.pallas_call(
        body, name=name, grid=(m // tm,),
        in_specs=[row(a.shape[1]) for a in a_list] + [whole(b) for b in b_list] + [row(D_MODEL), row(D_MODEL), whole(g)],
        out_specs=[row(D_MODEL), row(D_MODEL), pl.BlockSpec((1, D_MODEL), lambda i: (0, 0))],
        out_shape=[jax.ShapeDtypeStruct((m, D_MODEL), F32), jax.ShapeDtypeStruct((m, D_MODEL), BF16),
                   jax.ShapeDtypeStruct((1, D_MODEL), F32)],
        compiler_params=_params("arbitrary"),
    )(*a_list, *b_list, h, dres, g)


def _rows(body, name, tm, tiled, consts, outs, accs=()):
    s = tiled[0].shape[0]
    assert s % tm == 0
    in_specs = [pl.BlockSpec((tm, a.shape[1]), lambda i: (i, 0)) for a in tiled]
    in_specs += [pl.BlockSpec(a.shape, lambda i, nd=a.ndim: (0,) * nd) for a in consts]
    out_shape = [jax.ShapeDtypeStruct((s, c), dt) for c, dt in outs]
    out_shape += [jax.ShapeDtypeStruct(sh, dt) for sh, dt in accs]
    out_specs = [pl.BlockSpec((tm, c), lambda i: (i, 0)) for c, _ in outs]
    out_specs += [pl.BlockSpec(sh, lambda i, nd=len(sh): (0,) * nd) for sh, _ in accs]
    return pl.pallas_call(
        body, name=name, grid=(s // tm,), in_specs=in_specs, out_specs=out_specs, out_shape=out_shape,
        compiler_params=_params("arbitrary"),
    )(*tiled, *consts)


def _first_step():
    return pl.program_id(0) == 0


def _rms_fwd(h, g, name):
    def body(h_ref, g_ref, n_ref):
        x = h_ref[...]
        r = lax.rsqrt(jnp.mean(x * x, axis=-1, keepdims=True) + RMS_EPS)
        n_ref[...] = (x * r * g_ref[...]).astype(BF16)

    return _rows(body, name, 512, [h], [g], [(D_MODEL, BF16)])[0]


def _rope_tables(s):
    half = ROT_DIM // 2
    inv_freq = ROPE_THETA ** (-jnp.arange(half, dtype=F32) * (2.0 / ROT_DIM))
    ang = jnp.arange(s, dtype=F32)[:, None] * inv_freq[None, :]
    cos, sin = jnp.cos(ang), jnp.sin(ang)
    rest = HEAD_DIM - ROT_DIM
    ones = jnp.ones((s, rest), F32)
    zeros = jnp.zeros((s, rest), F32)
    zh = jnp.zeros((s, half), F32)
    c_t = jnp.concatenate([cos, cos, ones], axis=1)
    a_t = jnp.concatenate([-sin, zh, zeros], axis=1)
    b_t = jnp.concatenate([zh, sin, zeros], axis=1)
    return tuple(jnp.tile(t, (1, LANES // HEAD_DIM)) for t in (c_t, a_t, b_t))


def _rot(x, c, a, b):
    w = x.shape[1]
    half = ROT_DIM // 2
    return x * c + pltpu.roll(x, w - half, 1) * a + pltpu.roll(x, half, 1) * b


def _wide(t, w):
    return t if w == LANES else jnp.tile(t, (1, w // LANES))


def _low_lanes(rows):
    return lax.broadcasted_iota(jnp.int32, (rows, LANES), 1) < HEAD_DIM


def _fold_store(x, sc_ref, out_refs):
    tm = x.shape[0]
    if any(d > 1 for d in out_refs):
        for p in range(N_PAIRS):
            sc_ref[p] = x[:, _pair_block(p)]
    for d, o_ref in out_refs.items():
        if d == 1:
            o_ref[0] = x.astype(o_ref.dtype)
            continue
        for r in range(d):
            for p in range(N_PAIRS):
                o_ref[r, :, _pair_block(p)] = sc_ref[p, pl.ds(r, tm // d, stride=d), :].astype(o_ref.dtype)


def _unfold_load(x_ref, sc_ref, d, add=False):
    n = x_ref.shape[1]
    for r in range(d):
        for p in range(N_PAIRS):
            rows = pl.ds(r, n, stride=d) if d > 1 else slice(None)
            val = x_ref[r, :, _pair_block(p)].astype(F32)
            if add:
                val = val + sc_ref[p, rows, :]
            sc_ref[p, rows, :] = val


def _folded_spec(d, tm, w=ATTN_W):
    return pl.BlockSpec((d, tm // d, w), lambda i: (0, i, 0))


def _folded_shape(s, d, dtype, w=ATTN_W):
    return jax.ShapeDtypeStruct((d, s // d, w), dtype)


def _qkv_prep_even(proj, tabs, name):
    s = proj.shape[0]
    tm = 512

    def body(p_ref, c_ref, a_ref, b_ref, q_ref, k_ref, v_ref):
        c, a, b = c_ref[...], a_ref[...], b_ref[...]
        q_ref[0] = _rot(p_ref[:, 0:ATTN_W].astype(F32), _wide(c, ATTN_W), _wide(a, ATTN_W), _wide(b, ATTN_W)).astype(BF16)
        lo = _low_lanes(tm)
        for src, o_ref in ((_rot(p_ref[:, 512:640].astype(F32), c, a, b), k_ref), (p_ref[:, 640:768].astype(F32), v_ref)):
            swapped = pltpu.roll(src, HEAD_DIM, 1)
            o_ref[0, :, 0:LANES] = jnp.where(lo, src, swapped).astype(BF16)
            o_ref[0, :, LANES:] = jnp.where(lo, swapped, src).astype(BF16)

    row = lambda w: pl.BlockSpec((tm, w), lambda i: (i, 0))
    return pl.pallas_call(
        body, name=name, grid=(s // tm,), in_specs=[row(proj.shape[1]), row(LANES), row(LANES), row(LANES)],
        out_specs=[_folded_spec(1, tm), _folded_spec(1, tm, 2 * LANES), _folded_spec(1, tm, 2 * LANES)],
        out_shape=[_folded_shape(s, 1, BF16), _folded_shape(s, 1, BF16, 2 * LANES), _folded_shape(s, 1, BF16, 2 * LANES)],
        compiler_params=_params("parallel"),
    )(proj, *tabs)


def _qkv_post_even(dq, dk, dv, dglu, tabs, name):
    s = dglu.shape[0]
    tm = 512

    def body(dq_ref, dk_ref, dv_ref, dr_ref, c_ref, a_ref, b_ref, o_ref):
        c, a, b = c_ref[...], -a_ref[...], -b_ref[...]
        o_ref[:, 0:ATTN_W] = _rot(dq_ref[0].astype(F32), _wide(c, ATTN_W), _wide(a, ATTN_W), _wide(b, ATTN_W)).astype(BF16)
        lo = _low_lanes(tm)
        merged = []
        for ref in (dk_ref, dv_ref):
            first, second = ref[0, :, 0:LANES].astype(F32), ref[0, :, LANES:].astype(F32)
            merged.append(jnp.where(lo, first + pltpu.roll(first, HEAD_DIM, 1), second + pltpu.roll(second, HEAD_DIM, 1)))
        o_ref[:, 512:640] = _rot(merged[0], c, a, b).astype(BF16)
        o_ref[:, 640:768] = merged[1].astype(BF16)
        o_ref[:, 768:] = dr_ref[...]

    row = lambda w: pl.BlockSpec((tm, w), lambda i: (i, 0))
    return pl.pallas_call(
        body, name=name, grid=(s // tm,),
        in_specs=[_folded_spec(1, tm), _folded_spec(1, tm, 2 * LANES), _folded_spec(1, tm, 2 * LANES),
                  row(dglu.shape[1]), row(LANES), row(LANES), row(LANES)],
        out_specs=row(EVEN_IN), out_shape=jax.ShapeDtypeStruct((s, EVEN_IN), BF16),
        compiler_params=_params("parallel"),
    )(dq, dk, dv, dglu, *tabs)


def _qkv_prep_odd(proj, tabs, name):
    s = proj.shape[0]
    tm = 1024

    def body(p_ref, c_ref, a_ref, b_ref, *rest):
        outs, sc_ref = rest[:-1], rest[-1]
        c, a, b = (_wide(t[...], ATTN_W) for t in (c_ref, a_ref, b_ref))
        for t in range(3):
            x = p_ref[:, t * ATTN_W:(t + 1) * ATTN_W].astype(F32)
            if t < 2:
                x = _rot(x, c, a, b)
            _fold_store(x, sc_ref, {d: outs[t * len(DILATIONS) + i] for i, d in enumerate(DILATIONS)})

    row = lambda w: pl.BlockSpec((tm, w), lambda i: (i, 0))
    return pl.pallas_call(
        body, name=name, grid=(s // tm,), in_specs=[row(proj.shape[1]), row(LANES), row(LANES), row(LANES)],
        out_specs=[_folded_spec(d, tm) for _ in range(3) for d in DILATIONS],
        out_shape=[_folded_shape(s, d, BF16) for _ in range(3) for d in DILATIONS],
        scratch_shapes=[pltpu.VMEM((N_PAIRS, tm, LANES), F32)],
        compiler_params=_params("parallel"),
    )(proj, *tabs)


def _qkv_post_odd(dqs, dks, dvs, dz, tabs, name):
    s = dz.shape[0]
    tm = 512
    nb = len(DILATIONS)

    def body(*refs):
        groups = (refs[:nb], refs[nb:2 * nb], refs[2 * nb:3 * nb])
        dz_ref, c_ref, a_ref, b_ref, o_ref, sc_ref = refs[3 * nb:]
        c, a, b = _wide(c_ref[...], ATTN_W), _wide(-a_ref[...], ATTN_W), _wide(-b_ref[...], ATTN_W)
        for t, group in enumerate(groups):
            for i, d in enumerate(DILATIONS):
                _unfold_load(group[i], sc_ref, d, add=i > 0)
            x = jnp.concatenate([sc_ref[p] for p in range(N_PAIRS)], axis=1)
            if t < 2:
                x = _rot(x, c, a, b)
            o_ref[:, t * ATTN_W:(t + 1) * ATTN_W] = x.astype(BF16)
        o_ref[:, 3 * ATTN_W:] = dz_ref[...]

    row = lambda w: pl.BlockSpec((tm, w), lambda i: (i, 0))
    return pl.pallas_call(
        body, name=name, grid=(s // tm,),
        in_specs=[_folded_spec(d, tm) for _ in range(3) for d in DILATIONS] + [row(dz.shape[1]), row(LANES), row(LANES), row(LANES)],
        out_specs=row(ODD_IN), out_shape=jax.ShapeDtypeStruct((s, ODD_IN), BF16),
        scratch_shapes=[pltpu.VMEM((N_PAIRS, tm, LANES), F32)],
        compiler_params=_params("parallel"),
    )(*dqs, *dks, *dvs, dz, *tabs)


def _fold_dout(dmix, name):
    s = dmix.shape[0]
    tm = 512
    ds = [d for d in DILATIONS if d > 1]

    def body(d_ref, *rest):
        outs, sc_ref = rest[:-1], rest[-1]
        _fold_store(d_ref[...], sc_ref, dict(zip(ds, outs)))

    return pl.pallas_call(
        body, name=name, grid=(s // tm,), in_specs=[pl.BlockSpec((tm, ATTN_W), lambda i: (i, 0))],
        out_specs=[_folded_spec(d, tm) for d in ds], out_shape=[_folded_shape(s, d, BF16) for d in ds],
        scratch_shapes=[pltpu.VMEM((N_PAIRS, tm, LANES), F32)],
        compiler_params=_params("parallel"),
    )(dmix)


def _window(j, i, tq):
    r0 = j * tq + i * BLOCK
    if i > 0:
        return pl.ds(pl.multiple_of(r0 - BLOCK, BLOCK), 2 * BLOCK), BLOCK
    start = pl.multiple_of(jnp.maximum(r0 - BLOCK, 0), BLOCK)
    return pl.ds(start, 2 * BLOCK), r0 - start


def _band_valid(offset, max_dist):
    shape = (2 * BLOCK, 2 * BLOCK)
    dist = (lax.bitwise_and(lax.broadcasted_iota(jnp.int32, shape, 0), BLOCK - 1)
            - lax.broadcasted_iota(jnp.int32, shape, 1) + offset)
    return jnp.abs(2 * dist - max_dist) <= max_dist


def _stack_heads(lo, x):
    zero = jnp.zeros_like(x)
    return jnp.concatenate([jnp.where(lo, x, zero), jnp.where(lo, zero, x)], axis=0)


def _unstack_heads(lo, x):
    return jnp.where(lo, x[:BLOCK], x[BLOCK:])


NT = (((1,), (1,)), ((), ()))
TN = (((0,), (0,)), ((), ()))


def _attn_fwd(q, k, v, sinks, *, max_dist, name, emit_bf16=False, o_dtype=F32):
    d, sp, wq = q.shape
    nq, nk = wq // LANES, k.shape[2] // LANES
    kdiv = nq // nk
    tq = min(sp, 1024)
    nsub = tq // BLOCK
    has_sink = sinks is not None

    def body(*refs):
        refs = list(refs)
        sink_ref = refs.pop(0) if has_sink else None
        q_ref, k_ref, v_ref, o_ref, lse_ref = refs[:5]
        pair = pl.program_id(1)
        j = pl.program_id(2)
        lo = _low_lanes(BLOCK)
        if has_sink:
            first_head = lax.broadcasted_iota(jnp.int32, (2 * BLOCK, 1), 0) < BLOCK
            sk = jnp.where(first_head, sink_ref[2 * pair], sink_ref[2 * pair + 1])
        for i in range(nsub):
            win, offset = _window(j, i, tq)
            rows = slice(i * BLOCK, (i + 1) * BLOCK)
            kw = k_ref[0, win, :]
            vw = v_ref[0, win, :]
            s = lax.dot_general(_stack_heads(lo, q_ref[0, rows, :]), kw, NT, preferred_element_type=F32) * ATTN_SCALE
            s = jnp.where(_band_valid(offset, max_dist), s, NEG)
            m = jnp.max(s, axis=-1, keepdims=True)
            if has_sink:
                m = jnp.maximum(m, sk)
            p = jnp.exp(s - m)
            l = jnp.sum(p, axis=-1, keepdims=True)
            if has_sink:
                l = l + jnp.exp(sk - m)
            o2 = _unstack_heads(lo, jnp.dot(p.astype(BF16), vw, preferred_element_type=F32) / l)
            o_ref[0, rows, :] = o2.astype(o_ref.dtype)
            lse_ref[0, rows, :] = _unstack_heads(lo, m + jnp.log(l))
            if emit_bf16:
                refs[5][0, rows, :] = o2.astype(BF16)

    qspec = pl.BlockSpec((1, tq, LANES), lambda r, p, j: (r, j, p))
    kspec = pl.BlockSpec((1, sp, LANES), lambda r, p, j: (r, 0, p // kdiv))
    in_specs = [qspec, kspec, kspec]
    operands = [q, k, v]
    if has_sink:
        in_specs = [pl.BlockSpec(memory_space=pltpu.SMEM)] + in_specs
        operands = [sinks] + operands
    out_shape = [jax.ShapeDtypeStruct(q.shape, o_dtype), jax.ShapeDtypeStruct(q.shape, F32)]
    if emit_bf16:
        out_shape.append(jax.ShapeDtypeStruct(q.shape, BF16))
    return pl.pallas_call(
        body, name=name, grid=(d, nq, sp // tq), in_specs=in_specs, out_specs=[qspec] * len(out_shape),
        out_shape=out_shape, compiler_params=_params("parallel", "parallel", "arbitrary"),
    )(*operands)


def _attn_bwd(q, k, v, do, oo, lse, sinks, *, max_dist, name):
    d, sp, wq = q.shape
    wk = k.shape[2]
    nq, nk = wq // LANES, wk // LANES
    kdiv = nq // nk
    tq = min(sp, 1024)
    nsub = tq // BLOCK
    has_sink = sinks is not None

    def body(*refs):
        refs = list(refs)
        sink_ref = refs.pop(0) if has_sink else None
        q_ref, k_ref, v_ref, do_ref, oo_ref, lse_ref, dq_ref, dk_out, dv_out = refs[:9]
        dk_ref, dv_ref = refs[-2:]
        pk, g, j = pl.program_id(1), pl.program_id(2), pl.program_id(3)

        @pl.when((g == 0) & (j == 0))
        def _():
            dk_ref[...] = jnp.zeros_like(dk_ref)
            dv_ref[...] = jnp.zeros_like(dv_ref)

        lo = _low_lanes(BLOCK)
        if has_sink:
            first_head = lax.broadcasted_iota(jnp.int32, (2 * BLOCK, 1), 0) < BLOCK
            pair = pk * kdiv + g
            sk = jnp.where(first_head, sink_ref[2 * pair], sink_ref[2 * pair + 1])
            sink_acc = jnp.zeros((2 * BLOCK, LANES), F32)
        for i in range(nsub):
            win, offset = _window(j, i, tq)
            rows = slice(i * BLOCK, (i + 1) * BLOCK)
            kw = k_ref[0, win, :]
            vw = v_ref[0, win, :]
            do2 = do_ref[0, rows, :].astype(F32)
            qs = _stack_heads(lo, q_ref[0, rows, :])
            dos = _stack_heads(lo, do2.astype(BF16))
            prod = do2 * oo_ref[0, rows, :]
            delta = jnp.sum(_stack_heads(lo, prod), axis=-1, keepdims=True)
            lse2 = lse_ref[0, rows, :]
            lse_swapped = pltpu.roll(lse2, HEAD_DIM, 1)
            lse_st = jnp.concatenate([jnp.where(lo, lse2, lse_swapped), jnp.where(lo, lse_swapped, lse2)], axis=0)
            s = lax.dot_general(qs, kw, NT, preferred_element_type=F32) * ATTN_SCALE
            s = jnp.where(_band_valid(offset, max_dist), s, NEG)
            p = jnp.exp(s - jnp.tile(lse_st, (1, 2)))
            dv_ref[win, :] = lax.dot_general(p.astype(BF16), dos, TN, preferred_element_type=F32) + dv_ref[win, :]
            dp = lax.dot_general(dos, vw, NT, preferred_element_type=F32)
            ds = (p * (dp - delta) * ATTN_SCALE).astype(BF16)
            dq_ref[0, rows, :] = _unstack_heads(lo, jnp.dot(ds, kw, preferred_element_type=F32)).astype(BF16)
            dk_ref[win, :] = lax.dot_general(ds, qs, TN, preferred_element_type=F32) + dk_ref[win, :]
            if has_sink:
                sink_acc = sink_acc - jnp.exp(sk - lse_st) * delta

        @pl.when((g == kdiv - 1) & (j == sp // tq - 1))
        def _():
            dk_out[0] = dk_ref[...].astype(BF16)
            dv_out[0] = dv_ref[...].astype(BF16)

        if has_sink:
            dsink_ref = refs[9]

            @pl.when(j == 0)
            def _():
                dsink_ref[...] = jnp.zeros_like(dsink_ref)

            dsink_ref[0] += jnp.where(lo[0:1], jnp.sum(sink_acc[:BLOCK], axis=0, keepdims=True),
                                      jnp.sum(sink_acc[BLOCK:], axis=0, keepdims=True))

    def qmap(r, pk, g, j):
        return (r, j, pk * kdiv + g)

    def kmap(r, pk, g, j):
        return (r, 0, pk)

    qspec = pl.BlockSpec((1, tq, LANES), qmap)
    kspec = pl.BlockSpec((1, sp, LANES), kmap)
    in_specs = [qspec, kspec, kspec, qspec, qspec, qspec]
    operands = [q, k, v, do, oo, lse]
    out_specs = [qspec, kspec, kspec]
    out_shape = [jax.ShapeDtypeStruct((d, sp, wq), BF16), jax.ShapeDtypeStruct((d, sp, wk), BF16),
                 jax.ShapeDtypeStruct((d, sp, wk), BF16)]
    if has_sink:
        in_specs = [pl.BlockSpec(memory_space=pltpu.SMEM)] + in_specs
        operands = [sinks] + operands
        out_specs.append(pl.BlockSpec((1, 1, LANES), lambda r, pk, g, j: (pk * kdiv + g, 0, 0)))
        out_shape.append(jax.ShapeDtypeStruct((nq, 1, LANES), F32))
    nsteps = sp // tq
    return pl.pallas_call(
        body, name=name, grid=(d, nk, kdiv, nsteps), in_specs=in_specs, out_specs=out_specs, out_shape=out_shape,
        scratch_shapes=[pltpu.VMEM((sp, LANES), F32), pltpu.VMEM((sp, LANES), F32)],
        compiler_params=_params("parallel", "parallel", "arbitrary", "arbitrary"),
    )(*operands)


def _combine(outs, lses, name):
    s = outs[0].shape[1]
    tm = 512
    nb = len(DILATIONS)
    ds = [d for d in DILATIONS if d > 1]

    def body(*refs):
        o_refs, l_refs = refs[:nb], refs[nb:2 * nb]
        cb_ref, c_ref, lse_ref = refs[2 * nb:2 * nb + 3]
        folded = refs[2 * nb + 3:2 * nb + 3 + 2 * len(ds)]
        scratch = refs[2 * nb + 3 + 2 * len(ds):]
        so = {1: None}
        sl = {1: None}
        for i, d in enumerate(ds):
            so[d], sl[d] = scratch[2 * i], scratch[2 * i + 1]
            _unfold_load(o_refs[1 + i], so[d], d)
            _unfold_load(l_refs[1 + i], sl[d], d)
        for p in range(N_PAIRS):
            pb = _pair_block(p)
            ls = [l_refs[0][0, :, pb]] + [sl[d][p] for d in ds]
            os_ = [o_refs[0][0, :, pb].astype(F32)] + [so[d][p] for d in ds]
            m = ls[0]
            for t in ls[1:]:
                m = jnp.maximum(m, t)
            ws = [jnp.exp(t - m) for t in ls]
            tot = ws[0]
            for t in ws[1:]:
                tot = tot + t
            acc = ws[0] * os_[0]
            for w, o in zip(ws[1:], os_[1:]):
                acc = acc + w * o
            cmix = acc / tot
            lse = m + jnp.log(tot)
            cb_ref[:, pb] = cmix.astype(BF16)
            c_ref[0, :, pb] = cmix
            lse_ref[0, :, pb] = lse
            so[ds[0]][p] = cmix
            sl[ds[0]][p] = lse
        for i, d in enumerate(ds):
            for r in range(d):
                for p in range(N_PAIRS):
                    rows = pl.ds(r, tm // d, stride=d)
                    folded[2 * i][r, :, _pair_block(p)] = so[ds[0]][p, rows, :]
                    folded[2 * i + 1][r, :, _pair_block(p)] = sl[ds[0]][p, rows, :]

    in_specs = [_folded_spec(d, tm) for _ in range(2) for d in DILATIONS]
    out_specs = [pl.BlockSpec((tm, ATTN_W), lambda i: (i, 0)), _folded_spec(1, tm), _folded_spec(1, tm)]
    out_shape = [jax.ShapeDtypeStruct((s, ATTN_W), BF16), _folded_shape(s, 1, F32), _folded_shape(s, 1, F32)]
    for d in ds:
        out_specs += [_folded_spec(d, tm)] * 2
        out_shape += [_folded_shape(s, d, F32)] * 2
    return pl.pallas_call(
        body, name=name, grid=(s // tm,), in_specs=in_specs, out_specs=out_specs, out_shape=out_shape,
        scratch_shapes=[pltpu.VMEM((N_PAIRS, tm, LANES), F32)] * (2 * len(ds)),
        compiler_params=_params("parallel"),
    )(*outs, *lses)


GLU_A = slice(768, 1280)
GLU_B = slice(1280, 1792)
EVEN_IN = 1792
ODD_IN = 2560
CONV_CH = 512


def _shifted_copies(xs_ref):
    rows = xs_ref.shape[1] - 8
    for b in range(1, 8):
        xs_ref[b, 0:rows, :] = xs_ref[0, pl.ds(b, rows), :]


def _shifted_rows(xs_ref, start):
    return xs_ref[start % 8, pl.ds(start - start % 8, CONV_ROWS), :]


def _glu(p_ref):
    return p_ref[:, GLU_A].astype(F32) * _sigmoid(p_ref[:, GLU_B].astype(F32))


def _conv_fwd(proj, w, b, ln_g, ln_b, name):
    s = proj.shape[0]
    tm = 512
    nh = tm // CONV_HALO
    lead = CONV_HALO - (CONV_WIDTH - 1)

    def body(p_ref, ph_ref, w_ref, b_ref, g_ref, bb_ref, y_ref, o_ref, xs_ref):
        xs_ref[0, CONV_HALO:, :] = _glu(p_ref)
        xs_ref[0, 0:CONV_HALO, :] = jnp.where(pl.program_id(0) > 0, _glu(ph_ref), 0.0)
        _shifted_copies(xs_ref)
        for c0 in range(0, tm, CONV_ROWS):
            acc = jnp.zeros((CONV_ROWS, CONV_CH), F32) + b_ref[...]
            for j in range(CONV_WIDTH):
                acc = acc + _shifted_rows(xs_ref, lead + j + c0) * w_ref[j:j + 1, :]
            y_ref[c0:c0 + CONV_ROWS, :] = acc
            mu = jnp.mean(acc, axis=-1, keepdims=True)
            xc = acc - mu
            var = jnp.mean(xc * xc, axis=-1, keepdims=True)
            zz = xc * lax.rsqrt(var + LN_EPS) * g_ref[...] + bb_ref[...]
            o_ref[c0:c0 + CONV_ROWS, :] = (zz * _sigmoid(zz)).astype(BF16)

    def const(a):
        return pl.BlockSpec(a.shape, lambda i: (0, 0))

    return pl.pallas_call(
        body, name=name, grid=(s // tm,),
        in_specs=[pl.BlockSpec((tm, EVEN_IN), lambda i: (i, 0)),
                  pl.BlockSpec((CONV_HALO, EVEN_IN), lambda i: (jnp.maximum(i * nh - 1, 0), 0)),
                  const(w), const(b), const(ln_g), const(ln_b)],
        out_specs=[pl.BlockSpec((tm, CONV_CH), lambda i: (i, 0)), pl.BlockSpec((tm, CONV_CH), lambda i: (i, 0))],
        out_shape=[jax.ShapeDtypeStruct((s, CONV_CH), F32), jax.ShapeDtypeStruct((s, CONV_CH), BF16)],
        scratch_shapes=[pltpu.VMEM((8, tm + CONV_HALO, CONV_CH), F32)],
        compiler_params=_params("arbitrary"),
    )(proj, proj, w, b, ln_g, ln_b)


def _conv_tail_bwd(dmix, yconv, ln_g, ln_b, name):
    def body(d_ref, y_ref, g_ref, b_ref, dy_ref, dg_ref, db_ref, dcb_ref):
        @pl.when(_first_step())
        def _():
            dg_ref[...] = jnp.zeros_like(dg_ref)
            db_ref[...] = jnp.zeros_like(db_ref)
            dcb_ref[...] = jnp.zeros_like(dcb_ref)

        y = y_ref[...]
        g = g_ref[...]
        mu = jnp.mean(y, axis=-1, keepdims=True)
        xc = y - mu
        rstd = lax.rsqrt(jnp.mean(xc * xc, axis=-1, keepdims=True) + LN_EPS)
        xh = xc * rstd
        zz = xh * g + b_ref[...]
        sg = _sigmoid(zz)
        dzz = d_ref[:, CONV_CH:] * sg * (1.0 + zz * (1.0 - sg))
        dg_ref[...] += jnp.sum(dzz * xh, axis=0, keepdims=True)
        db_ref[...] += jnp.sum(dzz, axis=0, keepdims=True)
        dxh = dzz * g
        dy = rstd * (dxh - jnp.mean(dxh, axis=-1, keepdims=True) - xh * jnp.mean(dxh * xh, axis=-1, keepdims=True))
        dcb_ref[...] += jnp.sum(dy, axis=0, keepdims=True)
        dy_ref[...] = dy

    vec = ((1, CONV_CH), F32)
    return _rows(body, name, 512, [dmix, yconv], [ln_g, ln_b], [(CONV_CH, F32)], [vec, vec, vec])


def _conv_bwd(proj, dy, w, name):
    s = proj.shape[0]
    tm = 512
    nh = tm // CONV_HALO
    nsteps = s // tm
    lead = CONV_HALO - (CONV_WIDTH - 1)

    def body(p_ref, ph_ref, dy_ref, dyn_ref, w_ref, dglu_ref, dw_ref, xf_ref, dyf_ref):
        i = pl.program_id(0)

        @pl.when(i == 0)
        def _():
            dw_ref[...] = jnp.zeros_like(dw_ref)

        ga = p_ref[:, GLU_A].astype(F32)
        sgb = _sigmoid(p_ref[:, GLU_B].astype(F32))
        xf_ref[0, CONV_HALO:, :] = ga * sgb
        xf_ref[0, 0:CONV_HALO, :] = jnp.where(i > 0, _glu(ph_ref), 0.0)
        _shifted_copies(xf_ref)
        dyf_ref[0, 0:tm, :] = dy_ref[...]
        dyf_ref[0, tm:, :] = jnp.where(i < nsteps - 1, dyn_ref[...], 0.0)
        _shifted_copies(dyf_ref)
        for c0 in range(0, tm, CONV_ROWS):
            rows = slice(c0, c0 + CONV_ROWS)
            acc = jnp.zeros((CONV_ROWS, CONV_CH), F32)
            for j in range(CONV_WIDTH):
                acc = acc + _shifted_rows(dyf_ref, CONV_WIDTH - 1 - j + c0) * w_ref[j:j + 1, :]
            a_c, s_c = ga[rows, :], sgb[rows, :]
            dglu_ref[rows, 0:CONV_CH] = (acc * s_c).astype(BF16)
            dglu_ref[rows, CONV_CH:] = (acc * a_c * s_c * (1.0 - s_c)).astype(BF16)
        for j in range(CONV_WIDTH):
            part = jnp.zeros((8, CONV_CH), F32)
            for c0 in range(0, tm, CONV_ROWS):
                prod = dy_ref[c0:c0 + CONV_ROWS, :] * _shifted_rows(xf_ref, lead + j + c0)
                part = part + jnp.sum(prod.reshape(CONV_ROWS // 8, 8, CONV_CH), axis=0)
            dw_ref[j:j + 1, :] += jnp.sum(part, axis=0, keepdims=True)

    return pl.pallas_call(
        body, name=name, grid=(nsteps,),
        in_specs=[pl.BlockSpec((tm, EVEN_IN), lambda i: (i, 0)),
                  pl.BlockSpec((CONV_HALO, EVEN_IN), lambda i: (jnp.maximum(i * nh - 1, 0), 0)),
                  pl.BlockSpec((tm, CONV_CH), lambda i: (i, 0)),
                  pl.BlockSpec((CONV_HALO, CONV_CH), lambda i: (jnp.minimum((i + 1) * nh, s // CONV_HALO - 1), 0)),
                  pl.BlockSpec(w.shape, lambda i: (0, 0))],
        out_specs=[pl.BlockSpec((tm, 2 * CONV_CH), lambda i: (i, 0)), pl.BlockSpec(w.shape, lambda i: (0, 0))],
        out_shape=[jax.ShapeDtypeStruct((s, 2 * CONV_CH), BF16), jax.ShapeDtypeStruct(w.shape, F32)],
        scratch_shapes=[pltpu.VMEM((8, tm + CONV_HALO, CONV_CH), F32), pltpu.VMEM((8, tm + CONV_HALO, CONV_CH), F32)],
        compiler_params=_params("arbitrary"),
    )(proj, proj, dy, dy, w)


GATE_Z = slice(1536, 2560)
D_CH = 512
GELU_C = math.sqrt(2.0 / math.pi)
GELU_K = 0.044715


def _gelu_parts(z):
    t = jnp.tanh(GELU_C * (z + GELU_K * z * z * z))
    return 0.5 * z * (1.0 + t), t


def _lane_group(rows):
    return lax.broadcasted_iota(jnp.int32, (rows, D_CH), 1) // HEAD_DIM


def _tril_mask():
    return lax.broadcasted_iota(jnp.int32, (BLOCK, BLOCK), 0) >= lax.broadcasted_iota(jnp.int32, (BLOCK, BLOCK), 1)


def _layer_norm_parts(x):
    mu = jnp.mean(x, axis=-1, keepdims=True)
    xc = x - mu
    rstd = lax.rsqrt(jnp.mean(xc * xc, axis=-1, keepdims=True) + LN_EPS)
    return xc * rstd, rstd


def _gate_fwd(proj, ln_g, ln_b, w_sp, sb_t, name):
    tm = 512

    def body(p_ref, g_ref, b_ref, w_ref, sb_ref, mixed_ref, out_ref):
        zz, _ = _gelu_parts(p_ref[:, GATE_Z].astype(F32))
        u = zz[:, :D_CH]
        xh, _ = _layer_norm_parts(zz[:, D_CH:])
        gn = (xh * g_ref[...] + b_ref[...]).astype(BF16)
        grp = _lane_group(BLOCK)
        tri = _tril_mask()
        ws = [jnp.where(tri, w_ref[gi], 0.0).astype(BF16) for gi in range(N_GROUPS)]
        bias = jnp.zeros((BLOCK, D_CH), F32)
        for gi in range(N_GROUPS):
            bias = jnp.where(grp == gi, sb_ref[:, gi:gi + 1], bias)
        for ch in range(tm // BLOCK):
            rows = slice(ch * BLOCK, (ch + 1) * BLOCK)
            gc = gn[rows, :]
            mixed = bias
            for gi in range(N_GROUPS):
                r = jnp.dot(ws[gi], gc, preferred_element_type=F32)
                mixed = jnp.where(grp == gi, r + bias, mixed)
            mixed_ref[rows, :] = mixed
            out_ref[rows, :] = (u[rows, :] * mixed).astype(BF16)

    return _rows(body, name, tm, [proj], [ln_g, ln_b, w_sp, sb_t], [(D_CH, F32), (D_CH, BF16)])


def _gate_bwd(dmix, proj, mixed, ln_g, ln_b, w_sp, name):
    tm = 512

    def body(d_ref, p_ref, m_ref, g_ref, b_ref, w_ref, dz_ref, dg_ref, db_ref, dw_ref, dsb_ref, dgn_ref):
        @pl.when(_first_step())
        def _():
            dg_ref[...] = jnp.zeros_like(dg_ref)
            db_ref[...] = jnp.zeros_like(db_ref)
            dw_ref[...] = jnp.zeros_like(dw_ref)
            dsb_ref[...] = jnp.zeros_like(dsb_ref)

        z = p_ref[:, GATE_Z].astype(F32)
        zz, t = _gelu_parts(z)
        u = zz[:, :D_CH]
        xh, rstd = _layer_norm_parts(zz[:, D_CH:])
        g = g_ref[...]
        gn = (xh * g + b_ref[...]).astype(BF16)
        dd = d_ref[:, D_CH:]
        du = dd * m_ref[...]
        dm = dd * u
        grp = _lane_group(BLOCK)
        tri = _tril_mask()
        ws = [jnp.where(tri, w_ref[gi], 0.0).astype(BF16) for gi in range(N_GROUPS)]
        gsel = (lax.broadcasted_iota(jnp.int32, (N_GROUPS, D_CH), 1) // HEAD_DIM
                == lax.broadcasted_iota(jnp.int32, (N_GROUPS, D_CH), 0)).astype(F32)
        for ch in range(tm // BLOCK):
            rows = slice(ch * BLOCK, (ch + 1) * BLOCK)
            dmc = dm[rows, :]
            dmb = dmc.astype(BF16)
            gc = gn[rows, :]
            dgn = jnp.zeros((BLOCK, D_CH), F32)
            for gi in range(N_GROUPS):
                r = lax.dot_general(ws[gi], dmb, TN, preferred_element_type=F32)
                dgn = jnp.where(grp == gi, r, dgn)
                dmg = jnp.where(grp == gi, dmb, jnp.zeros_like(dmb))
                dwg = lax.dot_general(dmg, gc, NT, preferred_element_type=F32)
                dw_ref[gi] += jnp.where(tri, dwg, 0.0)
            dsb_ref[...] += lax.dot_general(gsel, dmc, NT, preferred_element_type=F32, precision=lax.Precision.HIGHEST)
            dgn_ref[rows, :] = dgn
        dgn = dgn_ref[...]
        db_ref[...] += jnp.sum(dgn, axis=0, keepdims=True)
        dg_ref[...] += jnp.sum(dgn * xh, axis=0, keepdims=True)
        dxh = dgn * g
        dgp = rstd * (dxh - jnp.mean(dxh, axis=-1, keepdims=True) - xh * jnp.mean(dxh * xh, axis=-1, keepdims=True))
        dgelu = 0.5 * (1.0 + t) + 0.5 * z * (1.0 - t * t) * GELU_C * (1.0 + 3.0 * GELU_K * z * z)
        dz_ref[:, 0:D_CH] = (du * dgelu[:, :D_CH]).astype(BF16)
        dz_ref[:, D_CH:] = (dgp * dgelu[:, D_CH:]).astype(BF16)

    s = proj.shape[0]
    tiled = [dmix, proj, mixed]
    consts = [ln_g, ln_b, w_sp]
    in_specs = [pl.BlockSpec((tm, a.shape[1]), lambda i: (i, 0)) for a in tiled]
    in_specs += [pl.BlockSpec(a.shape, lambda i, nd=a.ndim: (0,) * nd) for a in consts]
    vec = (1, D_CH)
    acc_shapes = [vec, vec, w_sp.shape, (N_GROUPS, BLOCK)]
    return pl.pallas_call(
        body, name=name, grid=(s // tm,), in_specs=in_specs,
        out_specs=[pl.BlockSpec((tm, 2 * D_CH), lambda i: (i, 0))]
        + [pl.BlockSpec(sh, lambda i, nd=len(sh): (0,) * nd) for sh in acc_shapes],
        out_shape=[jax.ShapeDtypeStruct((s, 2 * D_CH), BF16)] + [jax.ShapeDtypeStruct(sh, F32) for sh in acc_shapes],
        scratch_shapes=[pltpu.VMEM((tm, D_CH), F32)],
        compiler_params=_params("arbitrary"),
    )(*tiled, *consts)


def _adam_update(w, g, m, v):
    nm = ADAM_B1 * m + (1.0 - ADAM_B1) * g
    nv = ADAM_B2 * v + (1.0 - ADAM_B2) * (g * g)
    m_hat = nm / (1.0 - ADAM_B1 ** ADAM_STEP)
    v_hat = nv / (1.0 - ADAM_B2 ** ADAM_STEP)
    return -ADAM_LR * (m_hat / (jnp.sqrt(v_hat) + ADAM_EPS) + ADAM_WD * w), nm, nv


def _adamw(w, g, m, v, name):
    rows, cols = w.shape
    tm = _tile(rows, 512, 8)

    def body(w_ref, g_ref, m_ref, v_ref, d_ref, nm_ref, nv_ref):
        d_ref[...], nm_ref[...], nv_ref[...] = _adam_update(w_ref[...], g_ref[...], m_ref[...], v_ref[...])

    return _rows(body, name, tm, [w, g, m, v], [], [(cols, F32)] * 3)


def _ordered_sum(parts, name):
    n, rows, cols = parts.shape
    tm = _tile(rows, 512, 16 if parts.dtype == BF16 else 8)

    def body(p_ref, o_ref):
        acc = p_ref[0].astype(F32)
        for k in range(1, n):
            acc = acc + p_ref[k].astype(F32)
        o_ref[...] = acc

    return pl.pallas_call(body, name=name, grid=(rows // tm,),
                          in_specs=[pl.BlockSpec((n, tm, cols), lambda i: (0, i, 0))],
                          out_specs=pl.BlockSpec((tm, cols), lambda i: (i, 0)),
                          out_shape=jax.ShapeDtypeStruct((rows, cols), F32), compiler_params=_params("parallel"))(parts)


ANY = pl.BlockSpec(memory_space=pl.ANY)


def _position():
    x, y, c = lax.axis_index("x"), lax.axis_index("y"), lax.axis_index("c")
    other_chips = [(1 - x, y), (x, 1 - y), (1 - x, 1 - y)]
    return x, y, c, other_chips


def _remote(src, dst, send_sem, recv_sem, to):
    return pltpu.make_async_remote_copy(src_ref=src, dst_ref=dst, send_sem=send_sem, recv_sem=recv_sem,
                                        device_id=to, device_id_type=MESH)


STAGE_ROWS = 736


def _staged_copies(copies, buf, in_sems, out_sems):
    n = len(copies)

    def into(u):
        src = copies[u][0]
        return pltpu.make_async_copy(src, buf.at[u % 2, pl.ds(0, src.shape[0]), :], in_sems.at[u % 2])

    def out_of(u):
        dst = copies[u][1]
        return pltpu.make_async_copy(buf.at[u % 2, pl.ds(0, dst.shape[0]), :], dst, out_sems.at[u % 2])

    into(0).start()
    for u in range(n):
        into(u).wait()
        out_of(u).start()
        if u + 1 < n:
            if u >= 1:
                out_of(u - 1).wait()
            into(u + 1).start()
    if n >= 2:
        out_of(n - 2).wait()
    out_of(n - 1).wait()


def _stage_scratch(dtype, cols):
    return [pltpu.VMEM((2, STAGE_ROWS, cols), dtype), pltpu.SemaphoreType.DMA((2,)), pltpu.SemaphoreType.DMA((2,))]


def _row_chunks(rows):
    return [(r, min(STAGE_ROWS, rows - r)) for r in range(0, rows, STAGE_ROWS)]


def _gather_chips(shard, name):
    rows, cols = shard.shape
    half = rows // 2

    def body(in_ref, out_ref, send_sems, recv_sems, buf, in_sems, out_sems):
        x, y, c, chips = _position()
        me = 2 * x + y
        sibling = (x, y, 1 - c)

        def slab(chip, h):
            return out_ref.at[chip, pl.ds(h * half, half), :]

        first = [_remote(in_ref.at[pl.ds(c * half, half), :], slab(me, c), send_sems.at[j], recv_sems.at[j], (cx, cy, c))
                 for j, (cx, cy) in enumerate(chips)]
        for cp in first:
            cp.start()
        _staged_copies([(in_ref.at[pl.ds(r, n), :], out_ref.at[me, pl.ds(r, n), :]) for r, n in _row_chunks(rows)],
                       buf, in_sems, out_sems)
        passed = []
        for j, (cx, cy) in enumerate(chips):
            got = slab(2 * cx + cy, c)
            _remote(got, got, send_sems.at[j], recv_sems.at[j], sibling).wait_recv()
            cp = _remote(got, got, send_sems.at[3 + j], recv_sems.at[3 + j], sibling)
            cp.start()
            passed.append(cp)
        for j, (cx, cy) in enumerate(chips):
            got = slab(2 * cx + cy, 1 - c)
            _remote(got, got, send_sems.at[3 + j], recv_sems.at[3 + j], sibling).wait_recv()
        for cp in first + passed:
            cp.wait_send()

    return pl.pallas_call(
        body, name=name, in_specs=[ANY], out_specs=ANY,
        out_shape=jax.ShapeDtypeStruct((N_CHIPS, rows, cols), shard.dtype),
        scratch_shapes=[pltpu.SemaphoreType.DMA((6,)), pltpu.SemaphoreType.DMA((6,))] + _stage_scratch(shard.dtype, cols),
        compiler_params=pltpu.CompilerParams(vmem_limit_bytes=VMEM_LIMIT),
    )(shard)


HBM = pl.BlockSpec(memory_space=pltpu.HBM)
SEM = pl.BlockSpec(memory_space=pltpu.SEMAPHORE)
SIDE_EFFECT = pltpu.SideEffectType.DATAFLOW_SIDE_EFFECTING


def _ici_copies(in_ref, land_ref, send_sems, recv_sems, half):
    x, y, c, chips = _position()
    mine = pl.ds(c * half, half)
    sends = [_remote(in_ref.at[mine, :], land_ref.at[2 * x + y, mine, :], send_sems.at[j], recv_sems.at[j], (cx, cy, c))
             for j, (cx, cy) in enumerate(chips)]
    arrivals = [_remote(in_ref.at[mine, :], land_ref.at[2 * cx + cy, mine, :], send_sems.at[j], recv_sems.at[j], (cx, cy, c))
                for j, (cx, cy) in enumerate(chips)]
    return sends, arrivals


def _gather_start(shard, after, name):
    rows, cols = shard.shape

    def body(in_ref, land_ref, after_ref, send_sems, recv_sems, in_thru, land_thru, token):
        sends, _ = _ici_copies(in_ref, land_ref, send_sems, recv_sems, rows // 2)
        for cp in sends:
            cp.start()
        token[...] = jnp.zeros_like(token)

    land = lax.empty((N_CHIPS, rows, cols), shard.dtype)
    return pl.pallas_call(
        body, name=name,
        out_shape=(pltpu.SemaphoreType.DMA((3,)), pltpu.SemaphoreType.DMA((3,)), pltpu.HBM(shard.shape, shard.dtype),
                   pltpu.HBM(land.shape, land.dtype), jax.ShapeDtypeStruct((8, LANES), F32)),
        in_specs=(HBM, HBM, ANY), out_specs=(SEM, SEM, HBM, HBM, pl.BlockSpec(memory_space=pltpu.VMEM)),
        input_output_aliases={0: 2, 1: 3},
        compiler_params=pltpu.CompilerParams(has_side_effects=SIDE_EFFECT),
    )(pltpu.with_memory_space_constraint(shard, pltpu.HBM), pltpu.with_memory_space_constraint(land, pltpu.HBM), after)


def _gather_wait(send_sems, recv_sems, shard, land, after, name):
    rows = shard.shape[0]

    def body(in_ref, land_ref, send_sems, recv_sems, after_ref, in_out, land_out):
        sends, arrivals = _ici_copies(in_ref, land_ref, send_sems, recv_sems, rows // 2)
        for cp in sends:
            cp.wait_send()
        for cp in arrivals:
            cp.wait_recv()

    return pl.pallas_call(
        body, name=name, out_shape=(pltpu.HBM(shard.shape, shard.dtype), pltpu.HBM(land.shape, land.dtype)),
        in_specs=(HBM, HBM, SEM, SEM, ANY), out_specs=(HBM, HBM), input_output_aliases={0: 0, 1: 1},
        compiler_params=pltpu.CompilerParams(has_side_effects=SIDE_EFFECT),
    )(shard, land, send_sems, recv_sems, after)


def _gather_finish(shard, land, name):
    rows, cols = shard.shape
    half = rows // 2

    def body(in_ref, land_ref, out_ref, send_sems, recv_sems, buf, in_sems, out_sems):
        x, y, c, chips = _position()
        me = 2 * x + y
        sibling = (x, y, 1 - c)

        def slab(chip, h):
            return out_ref.at[chip, pl.ds(h * half, half), :]

        passed = [_remote(slab(2 * cx + cy, c), slab(2 * cx + cy, c), send_sems.at[j], recv_sems.at[j], sibling)
                  for j, (cx, cy) in enumerate(chips)]
        for cp in passed:
            cp.start()
        _staged_copies([(in_ref.at[pl.ds(r, n), :], out_ref.at[me, pl.ds(r, n), :]) for r, n in _row_chunks(rows)],
                       buf, in_sems, out_sems)
        for j, (cx, cy) in enumerate(chips):
            got = slab(2 * cx + cy, 1 - c)
            _remote(got, got, send_sems.at[j], recv_sems.at[j], sibling).wait_recv()
        for cp in passed:
            cp.wait_send()

    return pl.pallas_call(
        body, name=name, in_specs=[ANY, ANY], out_specs=ANY, out_shape=jax.ShapeDtypeStruct(land.shape, land.dtype),
        input_output_aliases={1: 0},
        scratch_shapes=[pltpu.SemaphoreType.DMA((3,)), pltpu.SemaphoreType.DMA((3,))] + _stage_scratch(shard.dtype, cols),
        compiler_params=pltpu.CompilerParams(vmem_limit_bytes=VMEM_LIMIT),
    )(shard, land)


def _pair_send(grads, name):
    n = len(grads)
    hs = [g.shape[2] for g in grads]
    offs = [sum(hs[:i]) for i in range(n)]
    cols = grads[0].shape[3]

    def body(*refs):
        g_refs = refs[:n]
        got_ref, send_sems, recv_sems = refs[n:]
        x, y, c, _ = _position()
        copies = [_remote(g_ref.at[:, 1 - c], got_ref.at[:, pl.ds(offs[i], hs[i]), :], send_sems.at[i], recv_sems.at[i],
                          (x, y, 1 - c)) for i, g_ref in enumerate(g_refs)]
        for cp in copies:
            cp.start()
        for cp in copies:
            cp.wait()

    return pl.pallas_call(
        body, name=name, in_specs=[ANY] * n, out_specs=ANY, out_shape=jax.ShapeDtypeStruct((N_CHIPS, sum(hs), cols), F32),
        scratch_shapes=[pltpu.SemaphoreType.DMA((n,)), pltpu.SemaphoreType.DMA((n,))],
    )(*grads)


def _device_copies(in_ref, land_ref, send_sems, recv_sems):
    x, y, c, _ = _position()
    flips = [(fx, fy, fc) for fx in range(2) for fy in range(2) for fc in range(2) if fx + fy + fc]
    copies = []
    for k, (fx, fy, fc) in enumerate(flips):
        px, py, pc = (1 - x if fx else x), (1 - y if fy else y), (1 - c if fc else c)
        copies.append((_remote(in_ref, land_ref.at[4 * x + 2 * y + c], send_sems.at[k], recv_sems.at[k], (px, py, pc)),
                       _remote(in_ref, land_ref.at[4 * px + 2 * py + pc], send_sems.at[k], recv_sems.at[k], (px, py, pc))))
    return copies


def _devices_start(block, name):
    land = lax.empty((N_DEV,) + block.shape, block.dtype)

    def body(in_ref, land_ref, send_sems, recv_sems, in_thru, land_thru):
        for send, _ in _device_copies(in_ref, land_ref, send_sems, recv_sems):
            send.start()

    return pl.pallas_call(
        body, name=name,
        out_shape=(pltpu.SemaphoreType.DMA((N_DEV - 1,)), pltpu.SemaphoreType.DMA((N_DEV - 1,)),
                   pltpu.HBM(block.shape, block.dtype), pltpu.HBM(land.shape, land.dtype)),
        in_specs=(HBM, HBM), out_specs=(SEM, SEM, HBM, HBM), input_output_aliases={0: 2, 1: 3},
        compiler_params=pltpu.CompilerParams(has_side_effects=SIDE_EFFECT),
    )(pltpu.with_memory_space_constraint(block, pltpu.HBM), pltpu.with_memory_space_constraint(land, pltpu.HBM))


def _devices_wait(send_sems, recv_sems, block, land, after, name):
    def body(in_ref, land_ref, send_sems, recv_sems, after_ref, in_out, land_out):
        for send, arrival in _device_copies(in_ref, land_ref, send_sems, recv_sems):
            send.wait_send()
            arrival.wait_recv()

    return pl.pallas_call(
        body, name=name, out_shape=(pltpu.HBM(block.shape, block.dtype), pltpu.HBM(land.shape, land.dtype)),
        in_specs=(HBM, HBM, SEM, SEM, ANY), out_specs=(HBM, HBM), input_output_aliases={0: 0, 1: 1},
        compiler_params=pltpu.CompilerParams(has_side_effects=SIDE_EFFECT),
    )(block, land, send_sems, recv_sems, after)


def _pair_copies(g_refs, land_ref, send_sems, recv_sems):
    x, y, c, _ = _position()
    hs = [g.shape[2] for g in g_refs]
    offs = [sum(hs[:i]) for i in range(len(hs))]
    return [_remote(g_ref.at[:, 1 - c], land_ref.at[:, pl.ds(offs[i], hs[i]), :], send_sems.at[i], recv_sems.at[i],
                    (x, y, 1 - c)) for i, g_ref in enumerate(g_refs)]


def _pair_send_start(grads, name):
    n = len(grads)
    land = lax.empty((N_CHIPS, sum(g.shape[2] for g in grads), grads[0].shape[3]), F32)

    def body(*refs):
        for cp in _pair_copies(refs[:n], refs[n], refs[n + 1], refs[n + 2]):
            cp.start()
        refs[-1][...] = jnp.zeros_like(refs[-1])

    buffers = [*grads, land]
    return pl.pallas_call(
        body, name=name,
        out_shape=(pltpu.SemaphoreType.DMA((n,)), pltpu.SemaphoreType.DMA((n,)),
                   *[pltpu.HBM(b.shape, b.dtype) for b in buffers], jax.ShapeDtypeStruct((8, LANES), F32)),
        in_specs=(HBM,) * (n + 1), out_specs=(SEM, SEM, *(HBM,) * (n + 1), pl.BlockSpec(memory_space=pltpu.VMEM)),
        input_output_aliases={i: 2 + i for i in range(n + 1)},
        compiler_params=pltpu.CompilerParams(has_side_effects=SIDE_EFFECT),
    )(*[pltpu.with_memory_space_constraint(b, pltpu.HBM) for b in buffers])


def _pair_send_wait(send_sems, recv_sems, buffers, after, name):
    n = len(buffers) - 1

    def body(*refs):
        for cp in _pair_copies(refs[:n], refs[n], refs[n + 1], refs[n + 2]):
            cp.wait_send()
            cp.wait_recv()

    return pl.pallas_call(
        body, name=name, out_shape=tuple(pltpu.HBM(b.shape, b.dtype) for b in buffers),
        in_specs=(*(HBM,) * (n + 1), SEM, SEM, ANY), out_specs=(HBM,) * (n + 1),
        input_output_aliases={i: i for i in range(n + 1)},
        compiler_params=pltpu.CompilerParams(has_side_effects=SIDE_EFFECT),
    )(*buffers, send_sems, recv_sems, after)


def _pair_add(grads, got, name):
    n = len(grads)
    hs = [g.shape[2] for g in grads]
    offs = [sum(hs[:i]) for i in range(n)]
    cols = grads[0].shape[3]
    hmax = max(hs)
    units = [(i, k) for k in range(N_CHIPS) for i in range(n)]

    def body(*refs):
        g_refs = refs[:n]
        got_ref, out_ref, a_buf, b_buf, o_buf, a_sems, b_sems, o_sems = refs[n:]
        c = lax.axis_index("c")

        def loads(u):
            i, k = units[u]
            slot, rows = u % 2, pl.ds(0, hs[i])
            return (pltpu.make_async_copy(g_refs[i].at[k, c], a_buf.at[slot, rows, :], a_sems.at[slot]),
                    pltpu.make_async_copy(got_ref.at[k, pl.ds(offs[i], hs[i]), :], b_buf.at[slot, rows, :], b_sems.at[slot]))

        def store(u):
            i, k = units[u]
            return pltpu.make_async_copy(o_buf.at[u % 2, pl.ds(0, hs[i]), :], out_ref.at[k, pl.ds(offs[i], hs[i]), :],
                                         o_sems.at[u % 2])

        for cp in loads(0):
            cp.start()
        for u, (i, k) in enumerate(units):
            if u + 1 < len(units):
                for cp in loads(u + 1):
                    cp.start()
            for cp in loads(u):
                cp.wait()
            if u >= 2:
                store(u - 2).wait()
            rows = pl.ds(0, hs[i])
            o_buf[u % 2, rows, :] = (a_buf[u % 2, rows, :] + b_buf[u % 2, rows, :]).astype(BF16)
            store(u).start()
        store(len(units) - 2).wait()
        store(len(units) - 1).wait()

    return pl.pallas_call(
        body, name=name, in_specs=[ANY] * (n + 1), out_specs=ANY,
        out_shape=jax.ShapeDtypeStruct((N_CHIPS, sum(hs), cols), BF16),
        scratch_shapes=[pltpu.VMEM((2, hmax, cols), F32), pltpu.VMEM((2, hmax, cols), F32), pltpu.VMEM((2, hmax, cols), BF16),
                        pltpu.SemaphoreType.DMA((2,)), pltpu.SemaphoreType.DMA((2,)), pltpu.SemaphoreType.DMA((2,))],
        compiler_params=pltpu.CompilerParams(vmem_limit_bytes=VMEM_LIMIT),
    )(*grads, got)


def _exchange_copies(in_ref, land_ref, send_sems, recv_sems):
    x, y, c, chips = _position()
    return [_remote(in_ref.at[2 * cx + cy], land_ref.at[j], send_sems.at[j], recv_sems.at[j], (cx, cy, c))
            for j, (cx, cy) in enumerate(chips)]


def _exchange_start(parts, name):
    _, rows, cols = parts.shape

    def body(in_ref, land_ref, send_sems, recv_sems, in_thru, land_thru, token):
        for cp in _exchange_copies(in_ref, land_ref, send_sems, recv_sems):
            cp.start()
        token[...] = jnp.zeros_like(token)

    land = lax.empty((3, rows, cols), parts.dtype)
    return pl.pallas_call(
        body, name=name,
        out_shape=(pltpu.SemaphoreType.DMA((3,)), pltpu.SemaphoreType.DMA((3,)), pltpu.HBM(parts.shape, parts.dtype),
                   pltpu.HBM(land.shape, land.dtype), jax.ShapeDtypeStruct((8, LANES), F32)),
        in_specs=(HBM, HBM), out_specs=(SEM, SEM, HBM, HBM, pl.BlockSpec(memory_space=pltpu.VMEM)),
        input_output_aliases={0: 2, 1: 3},
        compiler_params=pltpu.CompilerParams(has_side_effects=SIDE_EFFECT),
    )(pltpu.with_memory_space_constraint(parts, pltpu.HBM), pltpu.with_memory_space_constraint(land, pltpu.HBM))


def _exchange_wait(send_sems, recv_sems, parts, land, after, name):
    def body(in_ref, land_ref, send_sems, recv_sems, after_ref, in_out, land_out):
        for cp in _exchange_copies(in_ref, land_ref, send_sems, recv_sems):
            cp.wait_send()
            cp.wait_recv()

    return pl.pallas_call(
        body, name=name, out_shape=(pltpu.HBM(parts.shape, parts.dtype), pltpu.HBM(land.shape, land.dtype)),
        in_specs=(HBM, HBM, SEM, SEM, ANY), out_specs=(HBM, HBM), input_output_aliases={0: 0, 1: 1},
        compiler_params=pltpu.CompilerParams(has_side_effects=SIDE_EFFECT),
    )(parts, land, send_sems, recv_sems, after)


def _chip_sum(parts, recv, chip, name):
    _, rows, cols = parts.shape
    tm = _tile(rows, 512, 16)

    def body(chip_ref, own_ref, recv_ref, o_ref):
        acc = own_ref[0].astype(F32)
        for j in range(3):
            acc = acc + recv_ref[j].astype(F32)
        o_ref[...] = acc

    return pl.pallas_call(
        body, name=name,
        grid_spec=pltpu.PrefetchScalarGridSpec(
            num_scalar_prefetch=1, grid=(rows // tm,),
            in_specs=[pl.BlockSpec((1, tm, cols), lambda i, chip_ref: (chip_ref[0], i, 0)),
                      pl.BlockSpec((3, tm, cols), lambda i, chip_ref: (0, i, 0))],
            out_specs=pl.BlockSpec((tm, cols), lambda i, chip_ref: (i, 0))),
        out_shape=jax.ShapeDtypeStruct((rows, cols), F32), compiler_params=_params("parallel"),
    )(chip, parts, recv)


def _join_unpack(mine, hs, groups, name):
    n = len(hs)
    offs = [sum(hs[:i]) for i in range(n)]
    cols = mine.shape[1]
    n_out = max(groups) + 1
    base = [2 * sum(h for h, g in zip(hs[:i], groups[:i]) if g == groups[i]) for i in range(n)]
    out_rows = [2 * sum(h for h, g in zip(hs, groups) if g == k) for k in range(n_out)]

    def body(in_ref, *refs):
        outs = refs[:n_out]
        send_sems, recv_sems, buf, in_sems, out_sems = refs[n_out:]
        x, y, c, _ = _position()
        sibling = (x, y, 1 - c)
        sent, local = [], []
        for i in range(n):
            src = in_ref.at[pl.ds(offs[i], hs[i]), :]
            here = outs[groups[i]].at[pl.ds(base[i] + c * hs[i], hs[i]), :]
            cp = _remote(src, here, send_sems.at[i], recv_sems.at[i], sibling)
            cp.start()
            sent.append(cp)
            local.append((src, here))
        _staged_copies(local, buf, in_sems, out_sems)
        for i, cp in enumerate(sent):
            there = outs[groups[i]].at[pl.ds(base[i] + (1 - c) * hs[i], hs[i]), :]
            _remote(there, there, send_sems.at[i], recv_sems.at[i], sibling).wait_recv()
            cp.wait_send()

    assert max(hs) <= STAGE_ROWS
    return pl.pallas_call(
        body, name=name, in_specs=[ANY], out_specs=[ANY] * n_out,
        out_shape=[jax.ShapeDtypeStruct((r, cols), F32) for r in out_rows],
        scratch_shapes=[pltpu.SemaphoreType.DMA((n,)), pltpu.SemaphoreType.DMA((n,))] + _stage_scratch(F32, cols),
        compiler_params=pltpu.CompilerParams(vmem_limit_bytes=VMEM_LIMIT),
    )(mine)


SMALL_ROWS = 16
SMALL_PACK_ROWS = 256


def _small_rows(n):
    return -(-n // (SMALL_ROWS * LANES)) * SMALL_ROWS


def _pack_small(arrs):
    parts = []
    for a in arrs:
        flat = a.reshape(-1)
        rows = _small_rows(flat.shape[0])
        flat = jnp.pad(flat, (0, rows * LANES - flat.shape[0]))
        parts.append(flat.reshape(rows, LANES))
    total = sum(p.shape[0] for p in parts)
    parts.append(jnp.zeros((-total % SMALL_PACK_ROWS, LANES), F32))
    return jnp.concatenate(parts, axis=0)


def _unpack_small(packed, shapes):
    out, r = [], 0
    for sh in shapes:
        n = math.prod(sh)
        cnt = _small_rows(n)
        out.append(packed[r:r + cnt].reshape(-1)[:n].reshape(sh))
        r += cnt
    return out


def _ffn_bwd(dh, dhb, h_in, saved, g_norm, w_gate_t, w_up_t, w_down, tag, after=None):
    n, gate, up, act = saved
    dgate, dup = _ffn_dact(dhb, w_down, gate, up, f"{tag}_dact", after)
    dw_down = _matmul(act, dhb, trans_a=True, name=f"{tag}_dwdown")
    dw_gate_t = _matmul(dgate, n, trans_a=True, name=f"{tag}_dwgate")
    dw_up_t = _matmul(dup, n, trans_a=True, name=f"{tag}_dwup")
    dh_in, dh_inb, dg = _dn_norm([(dgate, w_gate_t), (dup, w_up_t)], h_in, g_norm, dh, f"{tag}_dnorm")
    return dh_in, dh_inb, dg, dw_gate_t, dw_up_t, dw_down


def _local_step(x, tgt, w, big, late_weights, reduce_send, reduce_exchange):
    s = x.shape[0]
    tabs = _rope_tables(s)
    grads, gbig = {}, {}

    g_ev = w['ev_norm_g']
    n1 = _rms_fwd(x, g_ev, "ev_norm")
    proj0 = _matmul(n1, big['ev_w_in', 0], trans_b=True, name="ev_in", out_dtype=BF16, rows_inner=True)
    q0, k0, v0 = _qkv_prep_even(proj0, tabs, "ev_qkv")
    sinks = w['ev_sinks'].reshape(-1)
    o0, lse0, o0b = _attn_fwd(q0, k0, v0, sinks, max_dist=BLOCK - 1, name="ev_attn", emit_bf16=True)
    yconv, cout = _conv_fwd(proj0, w['ev_conv_w'][0], w['ev_conv_b'], w['ev_conv_ln_g'], w['ev_conv_ln_b'], "ev_conv")
    mix0 = (o0b[0], cout)
    g_f0 = w['ffn_norm_g'][0:1]
    h1, n2 = _matmul_norm(mix0, big['ev_w_out', 0], x, g_f0, "ev_out")
    big = {**big, **late_weights(h1)}

    g_od = w['od_norm_g']
    act0, gate0, up0 = _ffn_gate_up(n2, big['ffn_w_gate', 0], big['ffn_w_up', 0], "ffn0_gate_up")
    h2, n3 = _matmul_norm(act0, big['ffn_w_down', 0], h1, g_od, "ffn0_down")
    ffn0 = (n2, gate0, up0, act0)

    proj1 = _matmul(n3, big['od_w_in', 0], trans_b=True, name="od_in", out_dtype=BF16, rows_inner=True)
    qkv = _qkv_prep_odd(proj1, tabs, "od_qkv")
    nb = len(DILATIONS)
    outs, lses = [], []
    for i, d in enumerate(DILATIONS):
        o_r, lse_r = _attn_fwd(qkv[i], qkv[nb + i], qkv[2 * nb + i], None, max_dist=BLOCK, name=f"od_attn{d}", o_dtype=BF16)
        outs.append(o_r)
        lses.append(lse_r)
    comb = _combine(outs, lses, "od_combine")
    c_bf16 = comb[0]
    c_fold = {1: comb[1]}
    lse_fold = {1: comb[2]}
    for i, d in enumerate(DILATIONS[1:]):
        c_fold[d], lse_fold[d] = comb[3 + 2 * i], comb[4 + 2 * i]
    w_sp = w['od_spatial_w'][0]
    sb_t = w['od_spatial_b'][0].T
    mixed, dout = _gate_fwd(proj1, w['od_sgu_ln_g'], w['od_sgu_ln_b'], w_sp, sb_t, "od_gate")
    mix1 = (c_bf16, dout)
    g_f1 = w['ffn_norm_g'][1:2]
    h3, n4 = _matmul_norm(mix1, big['od_w_out', 0], h2, g_f1, "od_out")
    act1, gate1, up1 = _ffn_gate_up(n4, big['ffn_w_gate', 1], big['ffn_w_up', 1], "ffn1_gate_up")
    ffn1 = (n4, gate1, up1, act1)

    dh4, dh4b, dg_final, loss_tile = _matmul_final(act1, big['ffn_w_down', 1], h3, w['final_norm_g'].reshape(1, D_MODEL),
                                                   tgt, "ffn1_down_loss")
    grads['final_norm_g'] = dg_final.reshape(D_MODEL)

    dh3, dh3b, dg_f1, gbig['ffn_w_gate', 1], gbig['ffn_w_up', 1], gbig['ffn_w_down', 1] = _ffn_bwd(
        dh4, dh4b, h3, ffn1, g_f1, big['ffn_w_gate', 1], big['ffn_w_up', 1], big['ffn_w_down', 1], "ffn1")

    dmix1 = _matmul(dh3b, big['od_w_out', 0], trans_b=True, name="od_dmix")
    gbig['od_w_out', 0] = _matmul_tn_pair(mix1[0], mix1[1], dh3b, "od_dwout")
    do_fold = dict(zip(DILATIONS[1:], _fold_dout(dmix1, "od_fold_dout")))
    do_fold[1] = dmix1[None]
    dqs, dks, dvs = [], [], []
    for i, d in enumerate(DILATIONS):
        dq_r, dk_r, dv_r = _attn_bwd(qkv[i], qkv[nb + i], qkv[2 * nb + i], do_fold[d], c_fold[d], lse_fold[d], None,
                                     max_dist=BLOCK, name=f"od_dattn{d}")
        dqs.append(dq_r)
        dks.append(dk_r)
        dvs.append(dv_r)
    dz, dg_sgu, db_sgu, dw_sp, dsb = _gate_bwd(dmix1, proj1, mixed, w['od_sgu_ln_g'], w['od_sgu_ln_b'], w_sp, "od_dgate")
    grads['od_sgu_ln_g'], grads['od_sgu_ln_b'] = dg_sgu, db_sgu
    grads['od_spatial_w'], grads['od_spatial_b'] = dw_sp[None], dsb[None]
    dproj1 = _qkv_post_odd(dqs, dks, dvs, dz, tabs, "od_dproj")
    gbig['od_w_in', 0] = _matmul(dproj1, n3, trans_a=True, name="od_dwin")
    dh2, dh2b, dg_od = _dn_norm([(dproj1, big['od_w_in', 0])], h2, g_od, dh3, "od_dnorm")
    grads['od_norm_g'] = dg_od
    token = reduce_send(0, gbig)

    dh1, dh1b, dg_f0, gbig['ffn_w_gate', 0], gbig['ffn_w_up', 0], gbig['ffn_w_down', 0] = _ffn_bwd(
        dh2, dh2b, h1, ffn0, g_f0, big['ffn_w_gate', 0], big['ffn_w_up', 0], big['ffn_w_down', 0], "ffn0", token)
    grads['ffn_norm_g'] = jnp.concatenate([dg_f0, dg_f1], axis=0)
    token = reduce_exchange(0, dh1) + reduce_send(1, gbig)

    dmix0 = _matmul(dh1b, big['ev_w_out', 0], trans_b=True, name="ev_dmix", after=token)
    gbig['ev_w_out', 0] = _matmul_tn_pair(mix0[0], mix0[1], dh1b, "ev_dwout")
    dq0, dk0, dv0, dsink = _attn_bwd(q0, k0, v0, dmix0[None], o0, lse0, sinks, max_dist=BLOCK - 1, name="ev_dattn")
    grads['ev_sinks'] = dsink[:, 0, :].reshape(N_PAIRS, 2, HEAD_DIM)[:, :, 0].reshape(1, 8)
    token = reduce_exchange(1, dq0)
    dyc, dg_cln, db_cln, dcb = _conv_tail_bwd(dmix0, yconv, w['ev_conv_ln_g'] + token[0:1, 0:1], w['ev_conv_ln_b'],
                                              "ev_dconv_tail")
    grads['ev_conv_ln_g'], grads['ev_conv_ln_b'], grads['ev_conv_b'] = dg_cln, db_cln, dcb
    dglu, dconv_w = _conv_bwd(proj0, dyc, w['ev_conv_w'][0], "ev_dconv")
    grads['ev_conv_w'] = dconv_w[None]
    dproj0 = _qkv_post_even(dq0, dk0, dv0, dglu, tabs, "ev_dproj")
    gbig['ev_w_in', 0] = _matmul(dproj0, n1, trans_a=True, name="ev_dwin")
    dx, _, dg_ev = _dn_norm([(dproj0, big['ev_w_in', 0])], x, g_ev, dh1, "ev_dnorm")
    grads['ev_norm_g'] = dg_ev
    return loss_tile, dx, grads, gbig


def _shard_rows(w, layer, by_cols):
    return w[layer].T if by_cols else w[layer]


def kernel(x, ev_norm_g, ev_w_in, ev_sinks, ev_conv_w, ev_conv_b, ev_conv_ln_g, ev_conv_ln_b, ev_w_out, od_norm_g, od_w_in, od_sgu_ln_g, od_sgu_ln_b, od_spatial_w, od_spatial_b, od_w_out, ffn_norm_g, ffn_w_gate, ffn_w_up, ffn_w_down, final_norm_g, loss_target, m_ev_norm_g, m_ev_w_in, m_ev_sinks, m_ev_conv_w, m_ev_conv_b, m_ev_conv_ln_g, m_ev_conv_ln_b, m_ev_w_out, m_od_norm_g, m_od_w_in, m_od_sgu_ln_g, m_od_sgu_ln_b, m_od_spatial_w, m_od_spatial_b, m_od_w_out, m_ffn_norm_g, m_ffn_w_gate, m_ffn_w_up, m_ffn_w_down, m_final_norm_g, v_ev_norm_g, v_ev_w_in, v_ev_sinks, v_ev_conv_w, v_ev_conv_b, v_ev_conv_ln_g, v_ev_conv_ln_b, v_ev_w_out, v_od_norm_g, v_od_w_in, v_od_sgu_ln_g, v_od_sgu_ln_b, v_od_spatial_w, v_od_spatial_b, v_od_w_out, v_ffn_norm_g, v_ffn_w_gate, v_ffn_w_up, v_ffn_w_down, v_final_norm_g):
    given = dict(locals())
    wts = {n: given[n] for n in WEIGHTS}
    mom = {n: given["m_" + n] for n in WEIGHTS}
    var = {n: given["v_" + n] for n in WEIGHTS}
    chip = 2 * lax.axis_index("x") + lax.axis_index("y")

    shard_rows = [_shard_rows(wts[n], layer, by_cols).astype(BF16) for n, layer, by_cols in BIG]
    counts = [a.shape[0] for a in shard_rows]
    n_first = sum(n.startswith('ev_') for n, _, _ in BIG)

    def unpack(stacked, entries, cnts):
        out, r = {}, 0
        for (n, layer, _), cnt in zip(entries, cnts):
            out[n, layer] = stacked[:, r:r + cnt].reshape(N_CHIPS * cnt, D_MODEL)
            r += cnt
        return out

    first_w = _gather_chips(jnp.concatenate(shard_rows[:n_first], axis=0), "gather_weights_ev")
    big = unpack(first_w, BIG[:n_first], counts[:n_first])
    send_sems, recv_sems, late_shard, late_land, token = _gather_start(jnp.concatenate(shard_rows[n_first:], axis=0),
                                                                      first_w, "gather_weights_start")

    def late_weights(after):
        shard, land = _gather_wait(send_sems, recv_sems, late_shard, late_land, after, "gather_weights_wait")
        return unpack(_gather_finish(shard, land, "gather_weights_finish"), BIG[n_first:], counts[n_first:])

    full = {n: wts[n] for n in SMALL_REPL}
    full['ev_norm_g'] = full['ev_norm_g'] + token[0:1, 0:1]
    small_shards = [wts[n] for n in SMALL_SHARDED]
    small_shapes = [a.shape for a in small_shards]
    all_s = _gather_chips(_pack_small(small_shards), "gather_small_weights")
    per_chip = [_unpack_small(all_s[k], small_shapes) for k in range(N_CHIPS)]
    for i, n in enumerate(SMALL_SHARDED):
        full[n] = jnp.concatenate([per_chip[k][i] for k in range(N_CHIPS)], axis=-1)

    half_rows = {(n, layer): cnt // 2 for (n, layer, _), cnt in zip(BIG, counts)}
    in_flight = []

    sending = {}

    def halves(stage, gbig):
        return [gbig[e].reshape(N_CHIPS, 2, half_rows[e], D_MODEL) for e in GRAD_STAGES[stage]]

    def reduce_send(stage, gbig):
        send_sems, recv_sems, *buffers, token = _pair_send_start(halves(stage, gbig), f"grad_pair_start{stage}")
        sending[stage] = (send_sems, recv_sems, buffers)
        return token

    def reduce_exchange(stage, after):
        send_sems, recv_sems, buffers = sending.pop(stage)
        *split, got = _pair_send_wait(send_sems, recv_sems, buffers, after, f"grad_pair_wait{stage}")
        chip_part = _pair_add(split, got, f"grad_pair_add{stage}")
        *handles, token = _exchange_start(chip_part, f"grad_exchange_start{stage}")
        in_flight.append(handles)
        return token

    loss_tile, grad_x, grads, gbig = _local_step(x[0], loss_target[0], full, big, late_weights, reduce_send, reduce_exchange)
    loss = lax.psum(loss_tile[0, 0], ("x", "y", "c"))

    last = len(GRAD_STAGES) - 1
    split = halves(last, gbig)
    chip_part = _pair_add(split, _pair_send(split, f"grad_pair_send{last}"), f"grad_pair_add{last}")
    *handles, _ = _exchange_start(chip_part, f"grad_exchange_start{last}")
    in_flight.append(handles)
    small_names = SMALL_REPL + SMALL_SHARDED
    small_full_shapes = [grads[n].shape for n in small_names]
    small_flight = _devices_start(_pack_small([grads[n] for n in small_names]), "grad_small_start")

    reduced, after = {}, grad_x
    for stage, entries in enumerate(GRAD_STAGES):
        chip_part, from_chips = _exchange_wait(*in_flight[stage], after, f"grad_exchange_wait{stage}")
        my_half = _chip_sum(chip_part, from_chips, chip.reshape(1), f"grad_chip_sum{stage}")
        joined = _join_unpack(my_half, [half_rows[e] for e in entries], list(range(len(entries))), f"grad_join_halves{stage}")
        reduced.update(zip(entries, joined))
        after = joined[0]

    spack, s_land = _devices_wait(*small_flight, after, "grad_small_wait")
    s_all = lax.dynamic_update_slice(s_land, spack[None], (2 * chip + lax.axis_index("c"), 0, 0))
    s_sum = _unpack_small(_ordered_sum(s_all, "grad_small_sum"), small_full_shapes)
    g_all = dict(zip(small_names, s_sum))
    for n in SMALL_SHARDED:
        width = wts[n].shape[-1]
        g_all[n] = lax.dynamic_slice_in_dim(g_all[n], chip * width, width, axis=g_all[n].ndim - 1)

    delta, new_m, new_v = {}, {}, {}
    for n in BIG_NAMES:
        by_cols = [bc for nn, _, bc in BIG if nn == n][0]
        layers = wts[n].shape[0]

        def as_rows(a):
            return (jnp.swapaxes(a, 1, 2) if by_cols else a).reshape(-1, D_MODEL)

        def from_rows(a):
            a = a.reshape(layers, -1, D_MODEL)
            return jnp.swapaxes(a, 1, 2) if by_cols else a

        g_rows = [reduced[n, layer] for layer in range(layers)]
        g_rows = g_rows[0] if layers == 1 else jnp.concatenate(g_rows, axis=0)
        updated = _adamw(as_rows(wts[n]), g_rows, as_rows(mom[n]), as_rows(var[n]), f"adamw_{n}")
        g_all[n] = from_rows(g_rows)
        delta[n], new_m[n], new_v[n] = (from_rows(a) for a in updated)
    shapes = [wts[n].shape for n in small_names]
    d_s, m_s, v_s = _adamw(*[_pack_small([src[n] for n in small_names]) for src in (wts, g_all, mom, var)], "adamw_small")
    for dst, packed in ((delta, d_s), (new_m, m_s), (new_v, v_s)):
        dst.update(zip(small_names, _unpack_small(packed, shapes)))

    return (loss, grad_x[None], *[g_all[n] for n in WEIGHTS], *[delta[n] for n in WEIGHTS],
            *[new_m[n] for n in WEIGHTS], *[new_v[n] for n in WEIGHTS])
```

```python
import math

import jax
import jax.numpy as jnp
from jax import lax
from jax.experimental import pallas as pl
from jax.experimental.pallas import tpu as pltpu

F32 = jnp.float32
BF16 = jnp.bfloat16

D_MODEL = 1024
HEAD_DIM = 64
ROT_DIM = 16
ROPE_THETA = 500000.0
RMS_EPS = 1e-6
LN_EPS = 1e-5
BLOCK = 128
CONV_WIDTH = 31
CONV_HALO = 32
CONV_ROWS = 64
D_FF = 2816
N_GROUPS = 8
ATTN_W = 512
ATTN_SCALE = HEAD_DIM ** -0.5
NEG = -1e30
DILATIONS = (1, 4, 16)

ADAM_LR = 0.001
ADAM_B1 = 0.9
ADAM_B2 = 0.999
ADAM_EPS = 1e-08
ADAM_WD = 0.01
ADAM_STEP = 10

LANES = 128
N_PAIRS = ATTN_W // LANES
VMEM_LIMIT = 56 * 1024 * 1024
MESH = pl.DeviceIdType.MESH
N_CHIPS = 4
N_DEV = 8

WEIGHTS = ['ev_norm_g', 'ev_w_in', 'ev_sinks', 'ev_conv_w', 'ev_conv_b', 'ev_conv_ln_g', 'ev_conv_ln_b', 'ev_w_out',
           'od_norm_g', 'od_w_in', 'od_sgu_ln_g', 'od_sgu_ln_b', 'od_spatial_w', 'od_spatial_b', 'od_w_out',
           'ffn_norm_g', 'ffn_w_gate', 'ffn_w_up', 'ffn_w_down', 'final_norm_g']
BIG = [('ev_w_in', 0, True), ('ev_w_out', 0, False), ('od_w_in', 0, True), ('od_w_out', 0, False),
       ('ffn_w_gate', 0, True), ('ffn_w_gate', 1, True), ('ffn_w_up', 0, True), ('ffn_w_up', 1, True),
       ('ffn_w_down', 0, False), ('ffn_w_down', 1, False)]
BIG_NAMES = ['ev_w_in', 'ev_w_out', 'od_w_in', 'od_w_out', 'ffn_w_gate', 'ffn_w_up', 'ffn_w_down']
GRAD_STAGES = ([('od_w_in', 0), ('od_w_out', 0), ('ffn_w_gate', 1), ('ffn_w_up', 1), ('ffn_w_down', 1)],
               [('ffn_w_gate', 0), ('ffn_w_up', 0), ('ffn_w_down', 0)],
               [('ev_w_in', 0), ('ev_w_out', 0)])
SMALL_SHARDED = ['ev_conv_w', 'od_norm_g', 'od_sgu_ln_g', 'od_sgu_ln_b']
SMALL_REPL = ['ev_norm_g', 'ev_sinks', 'ev_conv_b', 'ev_conv_ln_g', 'ev_conv_ln_b', 'od_spatial_w', 'od_spatial_b',
              'ffn_norm_g', 'final_norm_g']


def _tile(n, cap, mult=LANES):
    best = None
    for t in range(mult, min(n, cap) + 1, mult):
        if n % t == 0:
            best = t
    assert best is not None, (n, cap)
    return best


def _params(*sem):
    return pltpu.CompilerParams(dimension_semantics=sem, vmem_limit_bytes=VMEM_LIMIT)


def _sigmoid(x):
    return 1.0 / (1.0 + jnp.exp(-x))


def _pair_block(p):
    return slice(p * LANES, (p + 1) * LANES)


def _matmul(a, b, *, name, trans_a=False, trans_b=False, add=None, out_dtype=F32, after=None, rows_inner=False):
    parts = a if isinstance(a, (tuple, list)) else (a,)
    if trans_a:
        k, m = parts[0].shape
    else:
        m = parts[0].shape[0]
        k = sum(p.shape[1] for p in parts)
    if trans_b:
        n, k2 = b.shape
    else:
        k2, n = b.shape
    assert k == k2 and b.dtype == BF16 and all(p.dtype == BF16 for p in parts)
    tm = _tile(m, D_FF // 2 if trans_a else 512)
    tn = _tile(n, D_FF // 2)
    tk = k if k <= D_FF else _tile(k, 2048)
    nk = k // tk
    na = len(parts)
    assert na == 1 or (nk == 1 and not trans_a)
    assert nk == 1 or out_dtype == F32
    dims = (((0 if trans_a else 1,), (1 if trans_b else 0,)), ((), ()))
    has_add = add is not None

    def body(*refs):
        a_refs, b_ref = refs[:na], refs[na]
        add_ref = refs[na + 1] if has_add else None
        o_ref = refs[na + 1 + has_add + (after is not None)]
        def product():
            a_val = a_refs[0][...] if na == 1 else jnp.concatenate([r[...] for r in a_refs], axis=1)
            return lax.dot_general(a_val, b_ref[...], dims, preferred_element_type=F32)

        if nk == 1:
            part = product()
            if has_add:
                part = part + add_ref[...]
            o_ref[...] = part.astype(o_ref.dtype)
            return
        kk = pl.program_id(2)

        @pl.when(kk == 0)
        def _():
            o_ref[...] = product() + add_ref[...] if has_add else product()

        @pl.when(kk > 0)
        def _():
            o_ref[...] = product() + o_ref[...]

    def at(f):
        return (lambda j, i, kk: f(i, j, kk)) if rows_inner else f

    if trans_a:
        a_specs = [pl.BlockSpec((tk, tm), at(lambda i, j, kk: (kk, i)))]
    elif na == 1:
        a_specs = [pl.BlockSpec((tm, tk), at(lambda i, j, kk: (i, kk)))]
    else:
        a_specs = [pl.BlockSpec((tm, p.shape[1]), at(lambda i, j, kk: (i, 0))) for p in parts]
    b_spec = (pl.BlockSpec((tn, tk), at(lambda i, j, kk: (j, kk))) if trans_b
              else pl.BlockSpec((tk, tn), at(lambda i, j, kk: (kk, j))))
    o_spec = pl.BlockSpec((tm, tn), at(lambda i, j, kk: (i, j)))
    in_specs = a_specs + [b_spec] + ([o_spec] if has_add else [])
    operands = list(parts) + [b] + ([add] if has_add else [])
    if after is not None:
        in_specs.append(_after_spec(after))
        operands.append(after)
    grid = (n // tn, m // tm, nk) if rows_inner else (m // tm, n // tn, nk)
    return pl.pallas_call(
        body, name=name, grid=grid, in_specs=in_specs, out_specs=o_spec,
        out_shape=jax.ShapeDtypeStruct((m, n), out_dtype),
        compiler_params=_params("parallel", "parallel", "arbitrary"),
    )(*operands)


def _matmul_rows(a, b, add, epilogue, consts, tiled, outs, accs, name):
    parts = a if isinstance(a, (tuple, list)) else (a,)
    m = parts[0].shape[0]
    tm = 512
    na, nc, nt, no = len(parts), len(consts), len(tiled), len(outs)

    def body(*refs):
        a_refs, b_ref, add_ref = refs[:na], refs[na], refs[na + 1]
        const_refs = refs[na + 2:na + 2 + nc]
        tiled_refs = refs[na + 2 + nc:na + 2 + nc + nt]
        out_refs = refs[na + 2 + nc + nt:]
        a_val = a_refs[0][...] if na == 1 else jnp.concatenate([r[...] for r in a_refs], axis=1)
        h = jnp.dot(a_val, b_ref[...], preferred_element_type=F32) + add_ref[...]
        results = epilogue(h, [r[...] for r in const_refs], [r[...] for r in tiled_refs])
        for o_ref, val in zip(out_refs[:no], results[:no]):
            o_ref[...] = val.astype(o_ref.dtype)
        if accs:
            @pl.when(_first_step())
            def _():
                for o_ref in out_refs[no:]:
                    o_ref[...] = jnp.zeros_like(o_ref)

            for o_ref, val in zip(out_refs[no:], results[no:]):
                o_ref[...] += val

    row = lambda w: pl.BlockSpec((tm, w), lambda i: (i, 0))
    whole = lambda shape: pl.BlockSpec(shape, lambda i: (0,) * len(shape))
    return pl.pallas_call(
        body, name=name, grid=(m // tm,),
        in_specs=[row(p.shape[1]) for p in parts] + [whole(b.shape), row(D_MODEL)] + [whole(c.shape) for c in consts]
        + [row(t.shape[1]) for t in tiled],
        out_specs=[row(c) for c, _ in outs] + [whole(sh) for sh, _ in accs],
        out_shape=[jax.ShapeDtypeStruct((m, c), dt) for c, dt in outs] + [jax.ShapeDtypeStruct(sh, dt) for sh, dt in accs],
        compiler_params=_params("arbitrary"),
    )(*parts, b, add, *consts, *tiled)


def _matmul_norm(a, b, add, g, name):
    def epilogue(h, consts, tiled):
        r = lax.rsqrt(jnp.mean(h * h, axis=-1, keepdims=True) + RMS_EPS)
        return [h, h * r * consts[0]]

    return _matmul_rows(a, b, add, epilogue, [g], [], [(D_MODEL, F32), (D_MODEL, BF16)], [], name)


def _matmul_final(a, b, add, g, tgt, name):
    def epilogue(h, consts, tiled):
        gg = consts[0]
        r = lax.rsqrt(jnp.mean(h * h, axis=-1, keepdims=True) + RMS_EPS)
        xh = h * r
        e = xh * gg - tiled[0]
        loss = (0.5 / D_MODEL) * jnp.sum(jnp.sum(e * e, axis=-1, keepdims=True), axis=0, keepdims=True)
        dy = e * (1.0 / D_MODEL)
        dxh = dy * gg
        dx = r * (dxh - xh * jnp.mean(dxh * xh, axis=-1, keepdims=True))
        return [dx, dx, jnp.sum(dy * xh, axis=0, keepdims=True), jnp.broadcast_to(loss, (1, LANES))]

    return _matmul_rows(a, b, add, epilogue, [g], [tgt], [(D_MODEL, F32), (D_MODEL, BF16)],
                        [((1, D_MODEL), F32), ((1, LANES), F32)], name)


def _matmul_tn_pair(a1, a2, b, name):
    kdim, m1 = a1.shape
    m2 = a2.shape[1]
    n = b.shape[1]
    tn = _tile(n, 1024)
    tk = _tile(kdim, 2048)
    nk = kdim // tk
    dims = (((0,), (0,)), ((), ()))

    def body(a1_ref, a2_ref, b_ref, o_ref):
        kk = pl.program_id(1)
        def products():
            bv = b_ref[...]
            return (lax.dot_general(a1_ref[...], bv, dims, preferred_element_type=F32),
                    lax.dot_general(a2_ref[...], bv, dims, preferred_element_type=F32))

        @pl.when(kk == 0)
        def _():
            o_ref[0:m1, :], o_ref[m1:, :] = products()

        @pl.when(kk > 0)
        def _():
            top, bot = products()
            o_ref[0:m1, :] = top + o_ref[0:m1, :]
            o_ref[m1:, :] = bot + o_ref[m1:, :]

    return pl.pallas_call(
        body, name=name, grid=(n // tn, nk),
        in_specs=[pl.BlockSpec((tk, m1), lambda j, kk: (kk, 0)), pl.BlockSpec((tk, m2), lambda j, kk: (kk, 0)),
                  pl.BlockSpec((tk, tn), lambda j, kk: (kk, j))],
        out_specs=pl.BlockSpec((m1 + m2, tn), lambda j, kk: (0, j)),
        out_shape=jax.ShapeDtypeStruct((m1 + m2, n), F32),
        compiler_params=_params("parallel", "arbitrary"),
    )(a1, a2, b)


def _ffn_gate_up(n, w_gate_t, w_up_t, name):
    m, k = n.shape
    f = w_gate_t.shape[0]
    tm, tn = _tile(m, 1024), _tile(f, D_FF // 2)

    def body(n_ref, wg_ref, wu_ref, act_ref, gate_ref, up_ref):
        a = n_ref[...]

        def products(cols):
            return (lax.dot_general(a, wg_ref[cols, :], NT, preferred_element_type=F32),
                    lax.dot_general(a, wu_ref[cols, :], NT, preferred_element_type=F32))

        chunks = _col_chunks(tn)
        ahead = products(chunks[0])
        for idx, cols in enumerate(chunks):
            gate, up = ahead
            if idx + 1 < len(chunks):
                ahead = products(chunks[idx + 1])
            act_ref[:, cols] = (gate * _sigmoid(gate) * up).astype(BF16)
            gate_ref[:, cols] = gate.astype(BF16)
            up_ref[:, cols] = up.astype(BF16)

    wspec = pl.BlockSpec((tn, k), lambda j, i: (j, 0))
    ospec = pl.BlockSpec((tm, tn), lambda j, i: (i, j))
    return pl.pallas_call(
        body, name=name, grid=(f // tn, m // tm), in_specs=[pl.BlockSpec((tm, k), lambda j, i: (i, 0)), wspec, wspec],
        out_specs=[ospec] * 3, out_shape=[jax.ShapeDtypeStruct((m, f), BF16)] * 3,
        compiler_params=_params("parallel", "parallel"),
    )(n, w_gate_t, w_up_t)


def _col_chunks(n, width=384):
    return [slice(c, min(c + width, n)) for c in range(0, n, width)]


def _after_spec(after):
    return pl.BlockSpec(after.shape, lambda *_: (0,) * after.ndim)


def _ffn_dact(dhb, w_down, gate, up, name, after=None):
    m, k = dhb.shape
    f = w_down.shape[0]
    tm, tn = _tile(m, 1024), _tile(f, D_FF // 2)

    def body(d_ref, w_ref, g_ref, u_ref, *rest):
        dg_ref, du_ref = rest[-2:]
        d = d_ref[...]

        def product(cols):
            return lax.dot_general(d, w_ref[cols, :], NT, preferred_element_type=F32)

        chunks = _col_chunks(tn)
        ahead = product(chunks[0])
        for idx, cols in enumerate(chunks):
            dact = ahead
            if idx + 1 < len(chunks):
                ahead = product(chunks[idx + 1])
            g = g_ref[:, cols].astype(F32)
            sg = _sigmoid(g)
            dg_ref[:, cols] = (dact * u_ref[:, cols].astype(F32) * sg * (1.0 + g * (1.0 - sg))).astype(BF16)
            du_ref[:, cols] = (dact * g * sg).astype(BF16)

    ospec = pl.BlockSpec((tm, tn), lambda j, i: (i, j))
    extra = [] if after is None else [after]
    return pl.pallas_call(
        body, name=name, grid=(f // tn, m // tm),
        in_specs=[pl.BlockSpec((tm, k), lambda j, i: (i, 0)), pl.BlockSpec((tn, k), lambda j, i: (j, 0)), ospec, ospec]
        + [_after_spec(a) for a in extra],
        out_specs=[ospec] * 2, out_shape=[jax.ShapeDtypeStruct((m, f), BF16)] * 2,
        compiler_params=_params("parallel", "parallel"),
    )(dhb, w_down, gate, up, *extra)


def _dn_norm(pairs, h, g, dres, name):
    m = h.shape[0]
    tm = 512
    np_ = len(pairs)

    def body(*refs):
        a_refs, b_refs = refs[:np_], refs[np_:2 * np_]
        h_ref, dres_ref, g_ref, dh_ref, dhb_ref, dg_ref = refs[2 * np_:]

        @pl.when(_first_step())
        def _():
            dg_ref[...] = jnp.zeros_like(dg_ref)

        dy = jnp.dot(a_refs[0][...], b_refs[0][...], preferred_element_type=F32)
        for a_ref, b_ref in zip(a_refs[1:], b_refs[1:]):
            dy = jnp.dot(a_ref[...], b_ref[...], preferred_element_type=F32) + dy
        x = h_ref[...]
        r = lax.rsqrt(jnp.mean(x * x, axis=-1, keepdims=True) + RMS_EPS)
        xh = x * r
        dg_ref[...] += jnp.sum(dy * xh, axis=0, keepdims=True)
        dxh = dy * g_ref[...]
        tot = dres_ref[...] + r * (dxh - xh * jnp.mean(dxh * xh, axis=-1, keepdims=True))
        dh_ref[...] = tot
        dhb_ref[...] = tot.astype(BF16)

    row = lambda w: pl.BlockSpec((tm, w), lambda i: (i, 0))
    whole = lambda a: pl.BlockSpec(a.shape, lambda i: (0, 0))
    a_list, b_list = [a for a, _ in pairs], [b for _, b in pairs]
    return pl.pallas_call(
        body, name=name, grid=(m // tm,),
        in_specs=[row(a.shape[1]) for a in a_list] + [whole(b) for b in b_list] + [row(D_MODEL), row(D_MODEL), whole(g)],
        out_specs=[row(D_MODEL), row(D_MODEL), pl.BlockSpec((1, D_MODEL), lambda i: (0, 0))],
        out_shape=[jax.ShapeDtypeStruct((m, D_MODEL), F32), jax.ShapeDtypeStruct((m, D_MODEL), BF16),
                   jax.ShapeDtypeStruct((1, D_MODEL), F32)],
        compiler_params=_params("arbitrary"),
    )(*a_list, *b_list, h, dres, g)


def _rows(body, name, tm, tiled, consts, outs, accs=()):
    s = tiled[0].shape[0]
    assert s % tm == 0
    in_specs = [pl.BlockSpec((tm, a.shape[1]), lambda i: (i, 0)) for a in tiled]
    in_specs += [pl.BlockSpec(a.shape, lambda i, nd=a.ndim: (0,) * nd) for a in consts]
    out_shape = [jax.ShapeDtypeStruct((s, c), dt) for c, dt in outs]
    out_shape += [jax.ShapeDtypeStruct(sh, dt) for sh, dt in accs]
    out_specs = [pl.BlockSpec((tm, c), lambda i: (i, 0)) for c, _ in outs]
    out_specs += [pl.BlockSpec(sh, lambda i, nd=len(sh): (0,) * nd) for sh, _ in accs]
    return pl.pallas_call(
        body, name=name, grid=(s // tm,), in_specs=in_specs, out_specs=out_specs, out_shape=out_shape,
        compiler_params=_params("arbitrary"),
    )(*tiled, *consts)


def _first_step():
    return pl.program_id(0) == 0


def _rms_fwd(h, g, name):
    def body(h_ref, g_ref, n_ref):
        x = h_ref[...]
        r = lax.rsqrt(jnp.mean(x * x, axis=-1, keepdims=True) + RMS_EPS)
        n_ref[...] = (x * r * g_ref[...]).astype(BF16)

    return _rows(body, name, 512, [h], [g], [(D_MODEL, BF16)])[0]


def _rope_tables(s):
    half = ROT_DIM // 2
    inv_freq = ROPE_THETA ** (-jnp.arange(half, dtype=F32) * (2.0 / ROT_DIM))
    ang = jnp.arange(s, dtype=F32)[:, None] * inv_freq[None, :]
    cos, sin = jnp.cos(ang), jnp.sin(ang)
    rest = HEAD_DIM - ROT_DIM
    ones = jnp.ones((s, rest), F32)
    zeros = jnp.zeros((s, rest), F32)
    zh = jnp.zeros((s, half), F32)
    c_t = jnp.concatenate([cos, cos, ones], axis=1)
    a_t = jnp.concatenate([-sin, zh, zeros], axis=1)
    b_t = jnp.concatenate([zh, sin, zeros], axis=1)
    return tuple(jnp.tile(t, (1, LANES // HEAD_DIM)) for t in (c_t, a_t, b_t))


def _rot(x, c, a, b):
    w = x.shape[1]
    half = ROT_DIM // 2
    return x * c + pltpu.roll(x, w - half, 1) * a + pltpu.roll(x, half, 1) * b


def _wide(t, w):
    return t if w == LANES else jnp.tile(t, (1, w // LANES))


def _low_lanes(rows):
    return lax.broadcasted_iota(jnp.int32, (rows, LANES), 1) < HEAD_DIM


def _fold_store(x, sc_ref, out_refs):
    tm = x.shape[0]
    if any(d > 1 for d in out_refs):
        for p in range(N_PAIRS):
            sc_ref[p] = x[:, _pair_block(p)]
    for d, o_ref in out_refs.items():
        if d == 1:
            o_ref[0] = x.astype(o_ref.dtype)
            continue
        for r in range(d):
            for p in range(N_PAIRS):
                o_ref[r, :, _pair_block(p)] = sc_ref[p, pl.ds(r, tm // d, stride=d), :].astype(o_ref.dtype)


def _unfold_load(x_ref, sc_ref, d, add=False):
    n = x_ref.shape[1]
    for r in range(d):
        for p in range(N_PAIRS):
            rows = pl.ds(r, n, stride=d) if d > 1 else slice(None)
            val = x_ref[r, :, _pair_block(p)].astype(F32)
            if add:
                val = val + sc_ref[p, rows, :]
            sc_ref[p, rows, :] = val


def _folded_spec(d, tm, w=ATTN_W):
    return pl.BlockSpec((d, tm // d, w), lambda i: (0, i, 0))


def _folded_shape(s, d, dtype, w=ATTN_W):
    return jax.ShapeDtypeStruct((d, s // d, w), dtype)


def _qkv_prep_even(proj, tabs, name):
    s = proj.shape[0]
    tm = 512

    def body(p_ref, c_ref, a_ref, b_ref, q_ref, k_ref, v_ref):
        c, a, b = c_ref[...], a_ref[...], b_ref[...]
        q_ref[0] = _rot(p_ref[:, 0:ATTN_W].astype(F32), _wide(c, ATTN_W), _wide(a, ATTN_W), _wide(b, ATTN_W)).astype(BF16)
        lo = _low_lanes(tm)
        for src, o_ref in ((_rot(p_ref[:, 512:640].astype(F32), c, a, b), k_ref), (p_ref[:, 640:768].astype(F32), v_ref)):
            swapped = pltpu.roll(src, HEAD_DIM, 1)
            o_ref[0, :, 0:LANES] = jnp.where(lo, src, swapped).astype(BF16)
            o_ref[0, :, LANES:] = jnp.where(lo, swapped, src).astype(BF16)

    row = lambda w: pl.BlockSpec((tm, w), lambda i: (i, 0))
    return pl.pallas_call(
        body, name=name, grid=(s // tm,), in_specs=[row(proj.shape[1]), row(LANES), row(LANES), row(LANES)],
        out_specs=[_folded_spec(1, tm), _folded_spec(1, tm, 2 * LANES), _folded_spec(1, tm, 2 * LANES)],
        out_shape=[_folded_shape(s, 1, BF16), _folded_shape(s, 1, BF16, 2 * LANES), _folded_shape(s, 1, BF16, 2 * LANES)],
        compiler_params=_params("parallel"),
    )(proj, *tabs)


def _qkv_post_even(dq, dk, dv, dglu, tabs, name):
    s = dglu.shape[0]
    tm = 512

    def body(dq_ref, dk_ref, dv_ref, dr_ref, c_ref, a_ref, b_ref, o_ref):
        c, a, b = c_ref[...], -a_ref[...], -b_ref[...]
        o_ref[:, 0:ATTN_W] = _rot(dq_ref[0].astype(F32), _wide(c, ATTN_W), _wide(a, ATTN_W), _wide(b, ATTN_W)).astype(BF16)
        lo = _low_lanes(tm)
        merged = []
        for ref in (dk_ref, dv_ref):
            first, second = ref[0, :, 0:LANES].astype(F32), ref[0, :, LANES:].astype(F32)
            merged.append(jnp.where(lo, first + pltpu.roll(first, HEAD_DIM, 1), second + pltpu.roll(second, HEAD_DIM, 1)))
        o_ref[:, 512:640] = _rot(merged[0], c, a, b).astype(BF16)
        o_ref[:, 640:768] = merged[1].astype(BF16)
        o_ref[:, 768:] = dr_ref[...]

    row = lambda w: pl.BlockSpec((tm, w), lambda i: (i, 0))
    return pl.pallas_call(
        body, name=name, grid=(s // tm,),
        in_specs=[_folded_spec(1, tm), _folded_spec(1, tm, 2 * LANES), _folded_spec(1, tm, 2 * LANES),
                  row(dglu.shape[1]), row(LANES), row(LANES), row(LANES)],
        out_specs=row(EVEN_IN), out_shape=jax.ShapeDtypeStruct((s, EVEN_IN), BF16),
        compiler_params=_params("parallel"),
    )(dq, dk, dv, dglu, *tabs)


def _qkv_prep_odd(proj, tabs, name):
    s = proj.shape[0]
    tm = 1024

    def body(p_ref, c_ref, a_ref, b_ref, *rest):
        outs, sc_ref = rest[:-1], rest[-1]
        c, a, b = (_wide(t[...], ATTN_W) for t in (c_ref, a_ref, b_ref))
        for t in range(3):
            x = p_ref[:, t * ATTN_W:(t + 1) * ATTN_W].astype(F32)
            if t < 2:
                x = _rot(x, c, a, b)
            _fold_store(x, sc_ref, {d: outs[t * len(DILATIONS) + i] for i, d in enumerate(DILATIONS)})

    row = lambda w: pl.BlockSpec((tm, w), lambda i: (i, 0))
    return pl.pallas_call(
        body, name=name, grid=(s // tm,), in_specs=[row(proj.shape[1]), row(LANES), row(LANES), row(LANES)],
        out_specs=[_folded_spec(d, tm) for _ in range(3) for d in DILATIONS],
        out_shape=[_folded_shape(s, d, BF16) for _ in range(3) for d in DILATIONS],
        scratch_shapes=[pltpu.VMEM((N_PAIRS, tm, LANES), F32)],
        compiler_params=_params("parallel"),
    )(proj, *tabs)


def _qkv_post_odd(dqs, dks, dvs, dz, tabs, name):
    s = dz.shape[0]
    tm = 512
    nb = len(DILATIONS)

    def body(*refs):
        groups = (refs[:nb], refs[nb:2 * nb], refs[2 * nb:3 * nb])
        dz_ref, c_ref, a_ref, b_ref, o_ref, sc_ref = refs[3 * nb:]
        c, a, b = _wide(c_ref[...], ATTN_W), _wide(-a_ref[...], ATTN_W), _wide(-b_ref[...], ATTN_W)
        for t, group in enumerate(groups):
            for i, d in enumerate(DILATIONS):
                _unfold_load(group[i], sc_ref, d, add=i > 0)
            x = jnp.concatenate([sc_ref[p] for p in range(N_PAIRS)], axis=1)
            if t < 2:
                x = _rot(x, c, a, b)
            o_ref[:, t * ATTN_W:(t + 1) * ATTN_W] = x.astype(BF16)
        o_ref[:, 3 * ATTN_W:] = dz_ref[...]

    row = lambda w: pl.BlockSpec((tm, w), lambda i: (i, 0))
    return pl.pallas_call(
        body, name=name, grid=(s // tm,),
        in_specs=[_folded_spec(d, tm) for _ in range(3) for d in DILATIONS] + [row(dz.shape[1]), row(LANES), row(LANES), row(LANES)],
        out_specs=row(ODD_IN), out_shape=jax.ShapeDtypeStruct((s, ODD_IN), BF16),
        scratch_shapes=[pltpu.VMEM((N_PAIRS, tm, LANES), F32)],
        compiler_params=_params("parallel"),
    )(*dqs, *dks, *dvs, dz, *tabs)


def _fold_dout(dmix, name):
    s = dmix.shape[0]
    tm = 512
    ds = [d for d in DILATIONS if d > 1]

    def body(d_ref, *rest):
        outs, sc_ref = rest[:-1], rest[-1]
        _fold_store(d_ref[...], sc_ref, dict(zip(ds, outs)))

    return pl.pallas_call(
        body, name=name, grid=(s // tm,), in_specs=[pl.BlockSpec((tm, ATTN_W), lambda i: (i, 0))],
        out_specs=[_folded_spec(d, tm) for d in ds], out_shape=[_folded_shape(s, d, BF16) for d in ds],
        scratch_shapes=[pltpu.VMEM((N_PAIRS, tm, LANES), F32)],
        compiler_params=_params("parallel"),
    )(dmix)


def _window(j, i, tq):
    r0 = j * tq + i * BLOCK
    if i > 0:
        return pl.ds(pl.multiple_of(r0 - BLOCK, BLOCK), 2 * BLOCK), BLOCK
    start = pl.multiple_of(jnp.maximum(r0 - BLOCK, 0), BLOCK)
    return pl.ds(start, 2 * BLOCK), r0 - start


def _band_valid(offset, max_dist):
    shape = (2 * BLOCK, 2 * BLOCK)
    dist = (lax.bitwise_and(lax.broadcasted_iota(jnp.int32, shape, 0), BLOCK - 1)
            - lax.broadcasted_iota(jnp.int32, shape, 1) + offset)
    return jnp.abs(2 * dist - max_dist) <= max_dist


def _stack_heads(lo, x):
    zero = jnp.zeros_like(x)
    return jnp.concatenate([jnp.where(lo, x, zero), jnp.where(lo, zero, x)], axis=0)


def _unstack_heads(lo, x):
    return jnp.where(lo, x[:BLOCK], x[BLOCK:])


NT = (((1,), (1,)), ((), ()))
TN = (((0,), (0,)), ((), ()))


def _attn_fwd(q, k, v, sinks, *, max_dist, name, emit_bf16=False, o_dtype=F32):
    d, sp, wq = q.shape
    nq, nk = wq // LANES, k.shape[2] // LANES
    kdiv = nq // nk
    tq = min(sp, 1024)
    nsub = tq // BLOCK
    has_sink = sinks is not None

    def body(*refs):
        refs = list(refs)
        sink_ref = refs.pop(0) if has_sink else None
        q_ref, k_ref, v_ref, o_ref, lse_ref = refs[:5]
        pair = pl.program_id(1)
        j = pl.program_id(2)
        lo = _low_lanes(BLOCK)
        if has_sink:
            first_head = lax.broadcasted_iota(jnp.int32, (2 * BLOCK, 1), 0) < BLOCK
            sk = jnp.where(first_head, sink_ref[2 * pair], sink_ref[2 * pair + 1])
        for i in range(nsub):
            win, offset = _window(j, i, tq)
            rows = slice(i * BLOCK, (i + 1) * BLOCK)
            kw = k_ref[0, win, :]
            vw = v_ref[0, win, :]
            s = lax.dot_general(_stack_heads(lo, q_ref[0, rows, :]), kw, NT, preferred_element_type=F32) * ATTN_SCALE
            s = jnp.where(_band_valid(offset, max_dist), s, NEG)
            m = jnp.max(s, axis=-1, keepdims=True)
            if has_sink:
                m = jnp.maximum(m, sk)
            p = jnp.exp(s - m)
            l = jnp.sum(p, axis=-1, keepdims=True)
            if has_sink:
                l = l + jnp.exp(sk - m)
            o2 = _unstack_heads(lo, jnp.dot(p.astype(BF16), vw, preferred_element_type=F32) / l)
            o_ref[0, rows, :] = o2.astype(o_ref.dtype)
            lse_ref[0, rows, :] = _unstack_heads(lo, m + jnp.log(l))
            if emit_bf16:
                refs[5][0, rows, :] = o2.astype(BF16)

    qspec = pl.BlockSpec((1, tq, LANES), lambda r, p, j: (r, j, p))
    kspec = pl.BlockSpec((1, sp, LANES), lambda r, p, j: (r, 0, p // kdiv))
    in_specs = [qspec, kspec, kspec]
    operands = [q, k, v]
    if has_sink:
        in_specs = [pl.BlockSpec(memory_space=pltpu.SMEM)] + in_specs
        operands = [sinks] + operands
    out_shape = [jax.ShapeDtypeStruct(q.shape, o_dtype), jax.ShapeDtypeStruct(q.shape, F32)]
    if emit_bf16:
        out_shape.append(jax.ShapeDtypeStruct(q.shape, BF16))
    return pl.pallas_call(
        body, name=name, grid=(d, nq, sp // tq), in_specs=in_specs, out_specs=[qspec] * len(out_shape),
        out_shape=out_shape, compiler_params=_params("parallel", "parallel", "arbitrary"),
    )(*operands)


def _attn_bwd(q, k, v, do, oo, lse, sinks, *, max_dist, name):
    d, sp, wq = q.shape
    wk = k.shape[2]
    nq, nk = wq // LANES, wk // LANES
    kdiv = nq // nk
    tq = min(sp, 1024)
    nsub = tq // BLOCK
    has_sink = sinks is not None

    def body(*refs):
        refs = list(refs)
        sink_ref = refs.pop(0) if has_sink else None
        q_ref, k_ref, v_ref, do_ref, oo_ref, lse_ref, dq_ref, dk_out, dv_out = refs[:9]
        dk_ref, dv_ref = refs[-2:]
        pk, g, j = pl.program_id(1), pl.program_id(2), pl.program_id(3)

        @pl.when((g == 0) & (j == 0))
        def _():
            dk_ref[...] = jnp.zeros_like(dk_ref)
            dv_ref[...] = jnp.zeros_like(dv_ref)

        lo = _low_lanes(BLOCK)
        if has_sink:
            first_head = lax.broadcasted_iota(jnp.int32, (2 * BLOCK, 1), 0) < BLOCK
            pair = pk * kdiv + g
            sk = jnp.where(first_head, sink_ref[2 * pair], sink_ref[2 * pair + 1])
            sink_acc = jnp.zeros((2 * BLOCK, LANES), F32)
        for i in range(nsub):
            win, offset = _window(j, i, tq)
            rows = slice(i * BLOCK, (i + 1) * BLOCK)
            kw = k_ref[0, win, :]
            vw = v_ref[0, win, :]
            do2 = do_ref[0, rows, :].astype(F32)
            qs = _stack_heads(lo, q_ref[0, rows, :])
            dos = _stack_heads(lo, do2.astype(BF16))
            prod = do2 * oo_ref[0, rows, :]
            delta = jnp.sum(_stack_heads(lo, prod), axis=-1, keepdims=True)
            lse2 = lse_ref[0, rows, :]
            lse_swapped = pltpu.roll(lse2, HEAD_DIM, 1)
            lse_st = jnp.concatenate([jnp.where(lo, lse2, lse_swapped), jnp.where(lo, lse_swapped, lse2)], axis=0)
            s = lax.dot_general(qs, kw, NT, preferred_element_type=F32) * ATTN_SCALE
            s = jnp.where(_band_valid(offset, max_dist), s, NEG)
            p = jnp.exp(s - jnp.tile(lse_st, (1, 2)))
            dv_ref[win, :] = lax.dot_general(p.astype(BF16), dos, TN, preferred_element_type=F32) + dv_ref[win, :]
            dp = lax.dot_general(dos, vw, NT, preferred_element_type=F32)
            ds = (p * (dp - delta) * ATTN_SCALE).astype(BF16)
            dq_ref[0, rows, :] = _unstack_heads(lo, jnp.dot(ds, kw, preferred_element_type=F32)).astype(BF16)
            dk_ref[win, :] = lax.dot_general(ds, qs, TN, preferred_element_type=F32) + dk_ref[win, :]
            if has_sink:
                sink_acc = sink_acc - jnp.exp(sk - lse_st) * delta

        @pl.when((g == kdiv - 1) & (j == sp // tq - 1))
        def _():
            dk_out[0] = dk_ref[...].astype(BF16)
            dv_out[0] = dv_ref[...].astype(BF16)

        if has_sink:
            dsink_ref = refs[9]

            @pl.when(j == 0)
            def _():
                dsink_ref[...] = jnp.zeros_like(dsink_ref)

            dsink_ref[0] += jnp.where(lo[0:1], jnp.sum(sink_acc[:BLOCK], axis=0, keepdims=True),
                                      jnp.sum(sink_acc[BLOCK:], axis=0, keepdims=True))

    def qmap(r, pk, g, j):
        return (r, j, pk * kdiv + g)

    def kmap(r, pk, g, j):
        return (r, 0, pk)

    qspec = pl.BlockSpec((1, tq, LANES), qmap)
    kspec = pl.BlockSpec((1, sp, LANES), kmap)
    in_specs = [qspec, kspec, kspec, qspec, qspec, qspec]
    operands = [q, k, v, do, oo, lse]
    out_specs = [qspec, kspec, kspec]
    out_shape = [jax.ShapeDtypeStruct((d, sp, wq), BF16), jax.ShapeDtypeStruct((d, sp, wk), BF16),
                 jax.ShapeDtypeStruct((d, sp, wk), BF16)]
    if has_sink:
        in_specs = [pl.BlockSpec(memory_space=pltpu.SMEM)] + in_specs
        operands = [sinks] + operands
        out_specs.append(pl.BlockSpec((1, 1, LANES), lambda r, pk, g, j: (pk * kdiv + g, 0, 0)))
        out_shape.append(jax.ShapeDtypeStruct((nq, 1, LANES), F32))
    nsteps = sp // tq
    return pl.pallas_call(
        body, name=name, grid=(d, nk, kdiv, nsteps), in_specs=in_specs, out_specs=out_specs, out_shape=out_shape,
        scratch_shapes=[pltpu.VMEM((sp, LANES), F32), pltpu.VMEM((sp, LANES), F32)],
        compiler_params=_params("parallel", "parallel", "arbitrary", "arbitrary"),
    )(*operands)


def _combine(outs, lses, name):
    s = outs[0].shape[1]
    tm = 512
    nb = len(DILATIONS)
    ds = [d for d in DILATIONS if d > 1]

    def body(*refs):
        o_refs, l_refs = refs[:nb], refs[nb:2 * nb]
        cb_ref, c_ref, lse_ref = refs[2 * nb:2 * nb + 3]
        folded = refs[2 * nb + 3:2 * nb + 3 + 2 * len(ds)]
        scratch = refs[2 * nb + 3 + 2 * len(ds):]
        so = {1: None}
        sl = {1: None}
        for i, d in enumerate(ds):
            so[d], sl[d] = scratch[2 * i], scratch[2 * i + 1]
            _unfold_load(o_refs[1 + i], so[d], d)
            _unfold_load(l_refs[1 + i], sl[d], d)
        for p in range(N_PAIRS):
            pb = _pair_block(p)
            ls = [l_refs[0][0, :, pb]] + [sl[d][p] for d in ds]
            os_ = [o_refs[0][0, :, pb].astype(F32)] + [so[d][p] for d in ds]
            m = ls[0]
            for t in ls[1:]:
                m = jnp.maximum(m, t)
            ws = [jnp.exp(t - m) for t in ls]
            tot = ws[0]
            for t in ws[1:]:
                tot = tot + t
            acc = ws[0] * os_[0]
            for w, o in zip(ws[1:], os_[1:]):
                acc = acc + w * o
            cmix = acc / tot
            lse = m + jnp.log(tot)
            cb_ref[:, pb] = cmix.astype(BF16)
            c_ref[0, :, pb] = cmix
            lse_ref[0, :, pb] = lse
            so[ds[0]][p] = cmix
            sl[ds[0]][p] = lse
        for i, d in enumerate(ds):
            for r in range(d):
                for p in range(N_PAIRS):
                    rows = pl.ds(r, tm // d, stride=d)
                    folded[2 * i][r, :, _pair_block(p)] = so[ds[0]][p, rows, :]
                    folded[2 * i + 1][r, :, _pair_block(p)] = sl[ds[0]][p, rows, :]

    in_specs = [_folded_spec(d, tm) for _ in range(2) for d in DILATIONS]
    out_specs = [pl.BlockSpec((tm, ATTN_W), lambda i: (i, 0)), _folded_spec(1, tm), _folded_spec(1, tm)]
    out_shape = [jax.ShapeDtypeStruct((s, ATTN_W), BF16), _folded_shape(s, 1, F32), _folded_shape(s, 1, F32)]
    for d in ds:
        out_specs += [_folded_spec(d, tm)] * 2
        out_shape += [_folded_shape(s, d, F32)] * 2
    return pl.pallas_call(
        body, name=name, grid=(s // tm,), in_specs=in_specs, out_specs=out_specs, out_shape=out_shape,
        scratch_shapes=[pltpu.VMEM((N_PAIRS, tm, LANES), F32)] * (2 * len(ds)),
        compiler_params=_params("parallel"),
    )(*outs, *lses)


GLU_A = slice(768, 1280)
GLU_B = slice(1280, 1792)
EVEN_IN = 1792
ODD_IN = 2560
CONV_CH = 512


def _shifted_copies(xs_ref):
    rows = xs_ref.shape[1] - 8
    for b in range(1, 8):
        xs_ref[b, 0:rows, :] = xs_ref[0, pl.ds(b, rows), :]


def _shifted_rows(xs_ref, start):
    return xs_ref[start % 8, pl.ds(start - start % 8, CONV_ROWS), :]


def _glu(p_ref):
    return p_ref[:, GLU_A].astype(F32) * _sigmoid(p_ref[:, GLU_B].astype(F32))


def _conv_fwd(proj, w, b, ln_g, ln_b, name):
    s = proj.shape[0]
    tm = 512
    nh = tm // CONV_HALO
    lead = CONV_HALO - (CONV_WIDTH - 1)

    def body(p_ref, ph_ref, w_ref, b_ref, g_ref, bb_ref, y_ref, o_ref, xs_ref):
        xs_ref[0, CONV_HALO:, :] = _glu(p_ref)
        xs_ref[0, 0:CONV_HALO, :] = jnp.where(pl.program_id(0) > 0, _glu(ph_ref), 0.0)
        _shifted_copies(xs_ref)
        for c0 in range(0, tm, CONV_ROWS):
            acc = jnp.zeros((CONV_ROWS, CONV_CH), F32) + b_ref[...]
            for j in range(CONV_WIDTH):
                acc = acc + _shifted_rows(xs_ref, lead + j + c0) * w_ref[j:j + 1, :]
            y_ref[c0:c0 + CONV_ROWS, :] = acc
            mu = jnp.mean(acc, axis=-1, keepdims=True)
            xc = acc - mu
            var = jnp.mean(xc * xc, axis=-1, keepdims=True)
            zz = xc * lax.rsqrt(var + LN_EPS) * g_ref[...] + bb_ref[...]
            o_ref[c0:c0 + CONV_ROWS, :] = (zz * _sigmoid(zz)).astype(BF16)

    def const(a):
        return pl.BlockSpec(a.shape, lambda i: (0, 0))

    return pl.pallas_call(
        body, name=name, grid=(s // tm,),
        in_specs=[pl.BlockSpec((tm, EVEN_IN), lambda i: (i, 0)),
                  pl.BlockSpec((CONV_HALO, EVEN_IN), lambda i: (jnp.maximum(i * nh - 1, 0), 0)),
                  const(w), const(b), const(ln_g), const(ln_b)],
        out_specs=[pl.BlockSpec((tm, CONV_CH), lambda i: (i, 0)), pl.BlockSpec((tm, CONV_CH), lambda i: (i, 0))],
        out_shape=[jax.ShapeDtypeStruct((s, CONV_CH), F32), jax.ShapeDtypeStruct((s, CONV_CH), BF16)],
        scratch_shapes=[pltpu.VMEM((8, tm + CONV_HALO, CONV_CH), F32)],
        compiler_params=_params("arbitrary"),
    )(proj, proj, w, b, ln_g, ln_b)


def _conv_tail_bwd(dmix, yconv, ln_g, ln_b, name):
    def body(d_ref, y_ref, g_ref, b_ref, dy_ref, dg_ref, db_ref, dcb_ref):
        @pl.when(_first_step())
        def _():
            dg_ref[...] = jnp.zeros_like(dg_ref)
            db_ref[...] = jnp.zeros_like(db_ref)
            dcb_ref[...] = jnp.zeros_like(dcb_ref)

        y = y_ref[...]
        g = g_ref[...]
        mu = jnp.mean(y, axis=-1, keepdims=True)
        xc = y - mu
        rstd = lax.rsqrt(jnp.mean(xc * xc, axis=-1, keepdims=True) + LN_EPS)
        xh = xc * rstd
        zz = xh * g + b_ref[...]
        sg = _sigmoid(zz)
        dzz = d_ref[:, CONV_CH:] * sg * (1.0 + zz * (1.0 - sg))
        dg_ref[...] += jnp.sum(dzz * xh, axis=0, keepdims=True)
        db_ref[...] += jnp.sum(dzz, axis=0, keepdims=True)
        dxh = dzz * g
        dy = rstd * (dxh - jnp.mean(dxh, axis=-1, keepdims=True) - xh * jnp.mean(dxh * xh, axis=-1, keepdims=True))
        dcb_ref[...] += jnp.sum(dy, axis=0, keepdims=True)
        dy_ref[...] = dy

    vec = ((1, CONV_CH), F32)
    return _rows(body, name, 512, [dmix, yconv], [ln_g, ln_b], [(CONV_CH, F32)], [vec, vec, vec])


def _conv_bwd(proj, dy, w, name):
    s = proj.shape[0]
    tm = 512
    nh = tm // CONV_HALO
    nsteps = s // tm
    lead = CONV_HALO - (CONV_WIDTH - 1)

    def body(p_ref, ph_ref, dy_ref, dyn_ref, w_ref, dglu_ref, dw_ref, xf_ref, dyf_ref, part_ref):
        i = pl.program_id(0)

        @pl.when(i == 0)
        def _():
            dw_ref[...] = jnp.zeros_like(dw_ref)

        ga = p_ref[:, GLU_A].astype(F32)
        sgb = _sigmoid(p_ref[:, GLU_B].astype(F32))
        xf_ref[0, CONV_HALO:, :] = ga * sgb
        xf_ref[0, 0:CONV_HALO, :] = jnp.where(i > 0, _glu(ph_ref), 0.0)
        _shifted_copies(xf_ref)
        dyf_ref[0, 0:tm, :] = dy_ref[...]
        dyf_ref[0, tm:, :] = jnp.where(i < nsteps - 1, dyn_ref[...], 0.0)
        _shifted_copies(dyf_ref)
        for c0 in range(0, tm, CONV_ROWS):
            rows = slice(c0, c0 + CONV_ROWS)
            acc = jnp.zeros((CONV_ROWS, CONV_CH), F32)
            for j in range(CONV_WIDTH):
                acc = acc + _shifted_rows(dyf_ref, CONV_WIDTH - 1 - j + c0) * w_ref[j:j + 1, :]
            a_c, s_c = ga[rows, :], sgb[rows, :]
            dglu_ref[rows, 0:CONV_CH] = (acc * s_c).astype(BF16)
            dglu_ref[rows, CONV_CH:] = (acc * a_c * s_c * (1.0 - s_c)).astype(BF16)
        for c0 in range(0, tm, CONV_ROWS):
            dy_c = dy_ref[c0:c0 + CONV_ROWS, :]
            for j in range(CONV_WIDTH):
                prod = dy_c * _shifted_rows(xf_ref, lead + j + c0)
                part = jnp.sum(prod.reshape(CONV_ROWS // 8, 8, CONV_CH), axis=0)
                part_ref[j] = part if c0 == 0 else part + part_ref[j]
        for j in range(CONV_WIDTH):
            dw_ref[j:j + 1, :] += jnp.sum(part_ref[j], axis=0, keepdims=True)

    return pl.pallas_call(
        body, name=name, grid=(nsteps,),
        in_specs=[pl.BlockSpec((tm, EVEN_IN), lambda i: (i, 0)),
                  pl.BlockSpec((CONV_HALO, EVEN_IN), lambda i: (jnp.maximum(i * nh - 1, 0), 0)),
                  pl.BlockSpec((tm, CONV_CH), lambda i: (i, 0)),
                  pl.BlockSpec((CONV_HALO, CONV_CH), lambda i: (jnp.minimum((i + 1) * nh, s // CONV_HALO - 1), 0)),
                  pl.BlockSpec(w.shape, lambda i: (0, 0))],
        out_specs=[pl.BlockSpec((tm, 2 * CONV_CH), lambda i: (i, 0)), pl.BlockSpec(w.shape, lambda i: (0, 0))],
        out_shape=[jax.ShapeDtypeStruct((s, 2 * CONV_CH), BF16), jax.ShapeDtypeStruct(w.shape, F32)],
        scratch_shapes=[pltpu.VMEM((8, tm + CONV_HALO, CONV_CH), F32), pltpu.VMEM((8, tm + CONV_HALO, CONV_CH), F32),
                        pltpu.VMEM((CONV_WIDTH, 8, CONV_CH), F32)],
        compiler_params=_params("arbitrary"),
    )(proj, proj, dy, dy, w)


GATE_Z = slice(1536, 2560)
D_CH = 512
GELU_C = math.sqrt(2.0 / math.pi)
GELU_K = 0.044715


def _gelu_parts(z):
    t = jnp.tanh(GELU_C * (z + GELU_K * z * z * z))
    return 0.5 * z * (1.0 + t), t


def _lane_group(rows):
    return lax.broadcasted_iota(jnp.int32, (rows, D_CH), 1) // HEAD_DIM


def _tril_mask():
    return lax.broadcasted_iota(jnp.int32, (BLOCK, BLOCK), 0) >= lax.broadcasted_iota(jnp.int32, (BLOCK, BLOCK), 1)


def _layer_norm_parts(x):
    mu = jnp.mean(x, axis=-1, keepdims=True)
    xc = x - mu
    rstd = lax.rsqrt(jnp.mean(xc * xc, axis=-1, keepdims=True) + LN_EPS)
    return xc * rstd, rstd


def _gate_fwd(proj, ln_g, ln_b, w_sp, sb_t, name):
    tm = 512

    def body(p_ref, g_ref, b_ref, w_ref, sb_ref, mixed_ref, out_ref):
        zz, _ = _gelu_parts(p_ref[:, GATE_Z].astype(F32))
        u = zz[:, :D_CH]
        xh, _ = _layer_norm_parts(zz[:, D_CH:])
        gn = (xh * g_ref[...] + b_ref[...]).astype(BF16)
        grp = _lane_group(BLOCK)
        tri = _tril_mask()
        ws = [jnp.where(tri, w_ref[gi], 0.0).astype(BF16) for gi in range(N_GROUPS)]
        bias = jnp.zeros((BLOCK, D_CH), F32)
        for gi in range(N_GROUPS):
            bias = jnp.where(grp == gi, sb_ref[:, gi:gi + 1], bias)
        for ch in range(tm // BLOCK):
            rows = slice(ch * BLOCK, (ch + 1) * BLOCK)
            gc = gn[rows, :]
            mixed = bias
            for gi in range(N_GROUPS):
                r = jnp.dot(ws[gi], gc, preferred_element_type=F32)
                mixed = jnp.where(grp == gi, r + bias, mixed)
            mixed_ref[rows, :] = mixed
            out_ref[rows, :] = (u[rows, :] * mixed).astype(BF16)

    return _rows(body, name, tm, [proj], [ln_g, ln_b, w_sp, sb_t], [(D_CH, F32), (D_CH, BF16)])


def _gate_bwd(dmix, proj, mixed, ln_g, ln_b, w_sp, name):
    tm = 512

    def body(d_ref, p_ref, m_ref, g_ref, b_ref, w_ref, dz_ref, dg_ref, db_ref, dw_ref, dsb_ref, dgn_ref):
        @pl.when(_first_step())
        def _():
            dg_ref[...] = jnp.zeros_like(dg_ref)
            db_ref[...] = jnp.zeros_like(db_ref)
            dw_ref[...] = jnp.zeros_like(dw_ref)
            dsb_ref[...] = jnp.zeros_like(dsb_ref)

        z = p_ref[:, GATE_Z].astype(F32)
        zz, t = _gelu_parts(z)
        u = zz[:, :D_CH]
        xh, rstd = _layer_norm_parts(zz[:, D_CH:])
        g = g_ref[...]
        gn = (xh * g + b_ref[...]).astype(BF16)
        dd = d_ref[:, D_CH:]
        du = dd * m_ref[...]
        dm = dd * u
        grp = _lane_group(BLOCK)
        tri = _tril_mask()
        ws = [jnp.where(tri, w_ref[gi], 0.0).astype(BF16) for gi in range(N_GROUPS)]
        gsel = (lax.broadcasted_iota(jnp.int32, (N_GROUPS, D_CH), 1) // HEAD_DIM
                == lax.broadcasted_iota(jnp.int32, (N_GROUPS, D_CH), 0)).astype(F32)
        for ch in range(tm // BLOCK):
            rows = slice(ch * BLOCK, (ch + 1) * BLOCK)
            dmc = dm[rows, :]
            dmb = dmc.astype(BF16)
            gc = gn[rows, :]
            dgn = jnp.zeros((BLOCK, D_CH), F32)
            for gi in range(N_GROUPS):
                r = lax.dot_general(ws[gi], dmb, TN, preferred_element_type=F32)
                dgn = jnp.where(grp == gi, r, dgn)
                dmg = jnp.where(grp == gi, dmb, jnp.zeros_like(dmb))
                dwg = lax.dot_general(dmg, gc, NT, preferred_element_type=F32)
                dw_ref[gi] += jnp.where(tri, dwg, 0.0)
            dsb_ref[...] += lax.dot_general(gsel, dmc, NT, preferred_element_type=F32, precision=lax.Precision.HIGHEST)
            dgn_ref[rows, :] = dgn
        dgn = dgn_ref[...]
        db_ref[...] += jnp.sum(dgn, axis=0, keepdims=True)
        dg_ref[...] += jnp.sum(dgn * xh, axis=0, keepdims=True)
        dxh = dgn * g
        dgp = rstd * (dxh - jnp.mean(dxh, axis=-1, keepdims=True) - xh * jnp.mean(dxh * xh, axis=-1, keepdims=True))
        dgelu = 0.5 * (1.0 + t) + 0.5 * z * (1.0 - t * t) * GELU_C * (1.0 + 3.0 * GELU_K * z * z)
        dz_ref[:, 0:D_CH] = (du * dgelu[:, :D_CH]).astype(BF16)
        dz_ref[:, D_CH:] = (dgp * dgelu[:, D_CH:]).astype(BF16)

    s = proj.shape[0]
    tiled = [dmix, proj, mixed]
    consts = [ln_g, ln_b, w_sp]
    in_specs = [pl.BlockSpec((tm, a.shape[1]), lambda i: (i, 0)) for a in tiled]
    in_specs += [pl.BlockSpec(a.shape, lambda i, nd=a.ndim: (0,) * nd) for a in consts]
    vec = (1, D_CH)
    acc_shapes = [vec, vec, w_sp.shape, (N_GROUPS, BLOCK)]
    return pl.pallas_call(
        body, name=name, grid=(s // tm,), in_specs=in_specs,
        out_specs=[pl.BlockSpec((tm, 2 * D_CH), lambda i: (i, 0))]
        + [pl.BlockSpec(sh, lambda i, nd=len(sh): (0,) * nd) for sh in acc_shapes],
        out_shape=[jax.ShapeDtypeStruct((s, 2 * D_CH), BF16)] + [jax.ShapeDtypeStruct(sh, F32) for sh in acc_shapes],
        scratch_shapes=[pltpu.VMEM((tm, D_CH), F32)],
        compiler_params=_params("arbitrary"),
    )(*tiled, *consts)


def _adam_update(w, g, m, v):
    nm = ADAM_B1 * m + (1.0 - ADAM_B1) * g
    nv = ADAM_B2 * v + (1.0 - ADAM_B2) * (g * g)
    m_hat = nm / (1.0 - ADAM_B1 ** ADAM_STEP)
    v_hat = nv / (1.0 - ADAM_B2 ** ADAM_STEP)
    return -ADAM_LR * (m_hat / (jnp.sqrt(v_hat) + ADAM_EPS) + ADAM_WD * w), nm, nv


def _adamw(w, g, m, v, name):
    rows, cols = w.shape
    tm = _tile(rows, 512, 8)

    def body(w_ref, g_ref, m_ref, v_ref, d_ref, nm_ref, nv_ref):
        d_ref[...], nm_ref[...], nv_ref[...] = _adam_update(w_ref[...], g_ref[...], m_ref[...], v_ref[...])

    return _rows(body, name, tm, [w, g, m, v], [], [(cols, F32)] * 3)


def _ordered_sum(parts, name):
    n, rows, cols = parts.shape
    tm = _tile(rows, 512, 16 if parts.dtype == BF16 else 8)

    def body(p_ref, o_ref):
        acc = p_ref[0].astype(F32)
        for k in range(1, n):
            acc = acc + p_ref[k].astype(F32)
        o_ref[...] = acc

    return pl.pallas_call(body, name=name, grid=(rows // tm,),
                          in_specs=[pl.BlockSpec((n, tm, cols), lambda i: (0, i, 0))],
                          out_specs=pl.BlockSpec((tm, cols), lambda i: (i, 0)),
                          out_shape=jax.ShapeDtypeStruct((rows, cols), F32), compiler_params=_params("parallel"))(parts)


ANY = pl.BlockSpec(memory_space=pl.ANY)


def _position():
    x, y, c = lax.axis_index("x"), lax.axis_index("y"), lax.axis_index("c")
    other_chips = [(1 - x, y), (x, 1 - y), (1 - x, 1 - y)]
    return x, y, c, other_chips


def _remote(src, dst, send_sem, recv_sem, to):
    return pltpu.make_async_remote_copy(src_ref=src, dst_ref=dst, send_sem=send_sem, recv_sem=recv_sem,
                                        device_id=to, device_id_type=MESH)


STAGE_ROWS = 736


def _staged_copies(copies, buf, in_sems, out_sems):
    n = len(copies)

    def into(u):
        src = copies[u][0]
        return pltpu.make_async_copy(src, buf.at[u % 2, pl.ds(0, src.shape[0]), :], in_sems.at[u % 2])

    def out_of(u):
        dst = copies[u][1]
        return pltpu.make_async_copy(buf.at[u % 2, pl.ds(0, dst.shape[0]), :], dst, out_sems.at[u % 2])

    into(0).start()
    for u in range(n):
        into(u).wait()
        out_of(u).start()
        if u + 1 < n:
            if u >= 1:
                out_of(u - 1).wait()
            into(u + 1).start()
    if n >= 2:
        out_of(n - 2).wait()
    out_of(n - 1).wait()


def _stage_scratch(dtype, cols):
    return [pltpu.VMEM((2, STAGE_ROWS, cols), dtype), pltpu.SemaphoreType.DMA((2,)), pltpu.SemaphoreType.DMA((2,))]


def _row_chunks(rows):
    return [(r, min(STAGE_ROWS, rows - r)) for r in range(0, rows, STAGE_ROWS)]


def _gather_chips(shard, name):
    rows, cols = shard.shape
    half = rows // 2

    def body(in_ref, out_ref, send_sems, recv_sems, buf, in_sems, out_sems):
        x, y, c, chips = _position()
        me = 2 * x + y
        sibling = (x, y, 1 - c)

        def slab(chip, h):
            return out_ref.at[chip, pl.ds(h * half, half), :]

        first = [_remote(in_ref.at[pl.ds(c * half, half), :], slab(me, c), send_sems.at[j], recv_sems.at[j], (cx, cy, c))
                 for j, (cx, cy) in enumerate(chips)]
        for cp in first:
            cp.start()
        _staged_copies([(in_ref.at[pl.ds(r, n), :], out_ref.at[me, pl.ds(r, n), :]) for r, n in _row_chunks(rows)],
                       buf, in_sems, out_sems)
        passed = []
        for j, (cx, cy) in enumerate(chips):
            got = slab(2 * cx + cy, c)
            _remote(got, got, send_sems.at[j], recv_sems.at[j], sibling).wait_recv()
            cp = _remote(got, got, send_sems.at[3 + j], recv_sems.at[3 + j], sibling)
            cp.start()
            passed.append(cp)
        for j, (cx, cy) in enumerate(chips):
            got = slab(2 * cx + cy, 1 - c)
            _remote(got, got, send_sems.at[3 + j], recv_sems.at[3 + j], sibling).wait_recv()
        for cp in first + passed:
            cp.wait_send()

    return pl.pallas_call(
        body, name=name, in_specs=[ANY], out_specs=ANY,
        out_shape=jax.ShapeDtypeStruct((N_CHIPS, rows, cols), shard.dtype),
        scratch_shapes=[pltpu.SemaphoreType.DMA((6,)), pltpu.SemaphoreType.DMA((6,))] + _stage_scratch(shard.dtype, cols),
        compiler_params=pltpu.CompilerParams(vmem_limit_bytes=VMEM_LIMIT),
    )(shard)


HBM = pl.BlockSpec(memory_space=pltpu.HBM)
SEM = pl.BlockSpec(memory_space=pltpu.SEMAPHORE)
SIDE_EFFECT = pltpu.SideEffectType.DATAFLOW_SIDE_EFFECTING


def _ici_copies(in_ref, land_ref, send_sems, recv_sems, half):
    x, y, c, chips = _position()
    mine = pl.ds(c * half, half)
    sends = [_remote(in_ref.at[mine, :], land_ref.at[2 * x + y, mine, :], send_sems.at[j], recv_sems.at[j], (cx, cy, c))
             for j, (cx, cy) in enumerate(chips)]
    arrivals = [_remote(in_ref.at[mine, :], land_ref.at[2 * cx + cy, mine, :], send_sems.at[j], recv_sems.at[j], (cx, cy, c))
                for j, (cx, cy) in enumerate(chips)]
    return sends, arrivals


def _gather_start(shard, after, name):
    rows, cols = shard.shape

    def body(in_ref, land_ref, after_ref, send_sems, recv_sems, in_thru, land_thru, token):
        sends, _ = _ici_copies(in_ref, land_ref, send_sems, recv_sems, rows // 2)
        for cp in sends:
            cp.start()
        token[...] = jnp.zeros_like(token)

    land = lax.empty((N_CHIPS, rows, cols), shard.dtype)
    return pl.pallas_call(
        body, name=name,
        out_shape=(pltpu.SemaphoreType.DMA((3,)), pltpu.SemaphoreType.DMA((3,)), pltpu.HBM(shard.shape, shard.dtype),
                   pltpu.HBM(land.shape, land.dtype), jax.ShapeDtypeStruct((8, LANES), F32)),
        in_specs=(HBM, HBM, ANY), out_specs=(SEM, SEM, HBM, HBM, pl.BlockSpec(memory_space=pltpu.VMEM)),
        input_output_aliases={0: 2, 1: 3},
        compiler_params=pltpu.CompilerParams(has_side_effects=SIDE_EFFECT),
    )(pltpu.with_memory_space_constraint(shard, pltpu.HBM), pltpu.with_memory_space_constraint(land, pltpu.HBM), after)


def _gather_wait(send_sems, recv_sems, shard, land, after, name):
    rows = shard.shape[0]

    def body(in_ref, land_ref, send_sems, recv_sems, after_ref, in_out, land_out):
        sends, arrivals = _ici_copies(in_ref, land_ref, send_sems, recv_sems, rows // 2)
        for cp in sends:
            cp.wait_send()
        for cp in arrivals:
            cp.wait_recv()

    return pl.pallas_call(
        body, name=name, out_shape=(pltpu.HBM(shard.shape, shard.dtype), pltpu.HBM(land.shape, land.dtype)),
        in_specs=(HBM, HBM, SEM, SEM, ANY), out_specs=(HBM, HBM), input_output_aliases={0: 0, 1: 1},
        compiler_params=pltpu.CompilerParams(has_side_effects=SIDE_EFFECT),
    )(shard, land, send_sems, recv_sems, after)


def _gather_finish(shard, land, name):
    rows, cols = shard.shape
    half = rows // 2

    def body(in_ref, land_ref, out_ref, send_sems, recv_sems, buf, in_sems, out_sems):
        x, y, c, chips = _position()
        me = 2 * x + y
        sibling = (x, y, 1 - c)

        def slab(chip, h):
            return out_ref.at[chip, pl.ds(h * half, half), :]

        passed = [_remote(slab(2 * cx + cy, c), slab(2 * cx + cy, c), send_sems.at[j], recv_sems.at[j], sibling)
                  for j, (cx, cy) in enumerate(chips)]
        for cp in passed:
            cp.start()
        _staged_copies([(in_ref.at[pl.ds(r, n), :], out_ref.at[me, pl.ds(r, n), :]) for r, n in _row_chunks(rows)],
                       buf, in_sems, out_sems)
        for j, (cx, cy) in enumerate(chips):
            got = slab(2 * cx + cy, 1 - c)
            _remote(got, got, send_sems.at[j], recv_sems.at[j], sibling).wait_recv()
        for cp in passed:
            cp.wait_send()

    return pl.pallas_call(
        body, name=name, in_specs=[ANY, ANY], out_specs=ANY, out_shape=jax.ShapeDtypeStruct(land.shape, land.dtype),
        input_output_aliases={1: 0},
        scratch_shapes=[pltpu.SemaphoreType.DMA((3,)), pltpu.SemaphoreType.DMA((3,))] + _stage_scratch(shard.dtype, cols),
        compiler_params=pltpu.CompilerParams(vmem_limit_bytes=VMEM_LIMIT),
    )(shard, land)


def _gather_devices(block, name):
    rows, cols = block.shape

    def body(in_ref, out_ref, send_sems, recv_sems, local_sem):
        x, y, c, chips = _position()
        sibling = (x, y, 1 - c)

        def slot(px, py, pc):
            return out_ref.at[4 * px + 2 * py + pc]

        mine = pltpu.make_async_copy(in_ref, slot(x, y, c), local_sem)
        mine.start()
        first = [_remote(in_ref, slot(x, y, c), send_sems.at[0], recv_sems.at[0], sibling)]
        first += [_remote(in_ref, slot(x, y, c), send_sems.at[1 + j], recv_sems.at[1 + j], (cx, cy, c))
                  for j, (cx, cy) in enumerate(chips)]
        for cp in first:
            cp.start()
        passed = []
        for j, (cx, cy) in enumerate(chips):
            got = slot(cx, cy, c)
            _remote(got, got, send_sems.at[1 + j], recv_sems.at[1 + j], sibling).wait_recv()
            cp = _remote(got, got, send_sems.at[4 + j], recv_sems.at[4 + j], sibling)
            cp.start()
            passed.append(cp)
        got = slot(x, y, 1 - c)
        _remote(got, got, send_sems.at[0], recv_sems.at[0], sibling).wait_recv()
        for j, (cx, cy) in enumerate(chips):
            got = slot(cx, cy, 1 - c)
            _remote(got, got, send_sems.at[4 + j], recv_sems.at[4 + j], sibling).wait_recv()
        for cp in first + passed:
            cp.wait_send()
        mine.wait()

    return pl.pallas_call(
        body, name=name, in_specs=[ANY], out_specs=ANY,
        out_shape=jax.ShapeDtypeStruct((N_DEV, rows, cols), block.dtype),
        scratch_shapes=[pltpu.SemaphoreType.DMA((7,)), pltpu.SemaphoreType.DMA((7,)), pltpu.SemaphoreType.DMA],
    )(block)


def _pair_send(grads, name):
    n = len(grads)
    hs = [g.shape[2] for g in grads]
    offs = [sum(hs[:i]) for i in range(n)]
    cols = grads[0].shape[3]

    def body(*refs):
        g_refs = refs[:n]
        got_ref, send_sems, recv_sems = refs[n:]
        x, y, c, _ = _position()
        copies = [_remote(g_ref.at[:, 1 - c], got_ref.at[:, pl.ds(offs[i], hs[i]), :], send_sems.at[i], recv_sems.at[i],
                          (x, y, 1 - c)) for i, g_ref in enumerate(g_refs)]
        for cp in copies:
            cp.start()
        for cp in copies:
            cp.wait()

    return pl.pallas_call(
        body, name=name, in_specs=[ANY] * n, out_specs=ANY, out_shape=jax.ShapeDtypeStruct((N_CHIPS, sum(hs), cols), F32),
        scratch_shapes=[pltpu.SemaphoreType.DMA((n,)), pltpu.SemaphoreType.DMA((n,))],
    )(*grads)


def _pair_copies(g_refs, land_ref, send_sems, recv_sems):
    x, y, c, _ = _position()
    hs = [g.shape[2] for g in g_refs]
    offs = [sum(hs[:i]) for i in range(len(hs))]
    return [_remote(g_ref.at[:, 1 - c], land_ref.at[:, pl.ds(offs[i], hs[i]), :], send_sems.at[i], recv_sems.at[i],
                    (x, y, 1 - c)) for i, g_ref in enumerate(g_refs)]


def _pair_send_start(grads, name):
    n = len(grads)
    land = lax.empty((N_CHIPS, sum(g.shape[2] for g in grads), grads[0].shape[3]), F32)

    def body(*refs):
        for cp in _pair_copies(refs[:n], refs[n], refs[n + 1], refs[n + 2]):
            cp.start()
        refs[-1][...] = jnp.zeros_like(refs[-1])

    buffers = [*grads, land]
    return pl.pallas_call(
        body, name=name,
        out_shape=(pltpu.SemaphoreType.DMA((n,)), pltpu.SemaphoreType.DMA((n,)),
                   *[pltpu.HBM(b.shape, b.dtype) for b in buffers], jax.ShapeDtypeStruct((8, LANES), F32)),
        in_specs=(HBM,) * (n + 1), out_specs=(SEM, SEM, *(HBM,) * (n + 1), pl.BlockSpec(memory_space=pltpu.VMEM)),
        input_output_aliases={i: 2 + i for i in range(n + 1)},
        compiler_params=pltpu.CompilerParams(has_side_effects=SIDE_EFFECT),
    )(*[pltpu.with_memory_space_constraint(b, pltpu.HBM) for b in buffers])


def _pair_send_wait(send_sems, recv_sems, buffers, after, name):
    n = len(buffers) - 1

    def body(*refs):
        for cp in _pair_copies(refs[:n], refs[n], refs[n + 1], refs[n + 2]):
            cp.wait_send()
            cp.wait_recv()

    return pl.pallas_call(
        body, name=name, out_shape=tuple(pltpu.HBM(b.shape, b.dtype) for b in buffers),
        in_specs=(*(HBM,) * (n + 1), SEM, SEM, ANY), out_specs=(HBM,) * (n + 1),
        input_output_aliases={i: i for i in range(n + 1)},
        compiler_params=pltpu.CompilerParams(has_side_effects=SIDE_EFFECT),
    )(*buffers, send_sems, recv_sems, after)


def _pair_add(grads, got, name):
    n = len(grads)
    hs = [g.shape[2] for g in grads]
    offs = [sum(hs[:i]) for i in range(n)]
    cols = grads[0].shape[3]
    hmax = max(hs)
    units = [(i, k) for k in range(N_CHIPS) for i in range(n)]

    def body(*refs):
        g_refs = refs[:n]
        got_ref, out_ref, a_buf, b_buf, o_buf, a_sems, b_sems, o_sems = refs[n:]
        c = lax.axis_index("c")

        def loads(u):
            i, k = units[u]
            slot, rows = u % 2, pl.ds(0, hs[i])
            return (pltpu.make_async_copy(g_refs[i].at[k, c], a_buf.at[slot, rows, :], a_sems.at[slot]),
                    pltpu.make_async_copy(got_ref.at[k, pl.ds(offs[i], hs[i]), :], b_buf.at[slot, rows, :], b_sems.at[slot]))

        def store(u):
            i, k = units[u]
            return pltpu.make_async_copy(o_buf.at[u % 2, pl.ds(0, hs[i]), :], out_ref.at[k, pl.ds(offs[i], hs[i]), :],
                                         o_sems.at[u % 2])

        for cp in loads(0):
            cp.start()
        for u, (i, k) in enumerate(units):
            if u + 1 < len(units):
                for cp in loads(u + 1):
                    cp.start()
            for cp in loads(u):
                cp.wait()
            if u >= 2:
                store(u - 2).wait()
            rows = pl.ds(0, hs[i])
            o_buf[u % 2, rows, :] = (a_buf[u % 2, rows, :] + b_buf[u % 2, rows, :]).astype(BF16)
            store(u).start()
        store(len(units) - 2).wait()
        store(len(units) - 1).wait()

    return pl.pallas_call(
        body, name=name, in_specs=[ANY] * (n + 1), out_specs=ANY,
        out_shape=jax.ShapeDtypeStruct((N_CHIPS, sum(hs), cols), BF16),
        scratch_shapes=[pltpu.VMEM((2, hmax, cols), F32), pltpu.VMEM((2, hmax, cols), F32), pltpu.VMEM((2, hmax, cols), BF16),
                        pltpu.SemaphoreType.DMA((2,)), pltpu.SemaphoreType.DMA((2,)), pltpu.SemaphoreType.DMA((2,))],
        compiler_params=pltpu.CompilerParams(vmem_limit_bytes=VMEM_LIMIT),
    )(*grads, got)


def _chip_exchange(parts, name):
    _, rows, cols = parts.shape

    def body(in_ref, out_ref, send_sems, recv_sems):
        x, y, c, chips = _position()
        sent = [_remote(in_ref.at[2 * cx + cy], out_ref.at[j], send_sems.at[j], recv_sems.at[j], (cx, cy, c))
                for j, (cx, cy) in enumerate(chips)]
        for cp in sent:
            cp.start()
        for cp in sent:
            cp.wait()

    return pl.pallas_call(
        body, name=name, in_specs=[ANY], out_specs=ANY, out_shape=jax.ShapeDtypeStruct((3, rows, cols), parts.dtype),
        scratch_shapes=[pltpu.SemaphoreType.DMA((3,)), pltpu.SemaphoreType.DMA((3,))],
    )(parts)


def _exchange_copies(in_ref, land_ref, send_sems, recv_sems):
    x, y, c, chips = _position()
    return [_remote(in_ref.at[2 * cx + cy], land_ref.at[j], send_sems.at[j], recv_sems.at[j], (cx, cy, c))
            for j, (cx, cy) in enumerate(chips)]


def _exchange_start(parts, name):
    _, rows, cols = parts.shape

    def body(in_ref, land_ref, send_sems, recv_sems, in_thru, land_thru, token):
        for cp in _exchange_copies(in_ref, land_ref, send_sems, recv_sems):
            cp.start()
        token[...] = jnp.zeros_like(token)

    land = lax.empty((3, rows, cols), parts.dtype)
    return pl.pallas_call(
        body, name=name,
        out_shape=(pltpu.SemaphoreType.DMA((3,)), pltpu.SemaphoreType.DMA((3,)), pltpu.HBM(parts.shape, parts.dtype),
                   pltpu.HBM(land.shape, land.dtype), jax.ShapeDtypeStruct((8, LANES), F32)),
        in_specs=(HBM, HBM), out_specs=(SEM, SEM, HBM, HBM, pl.BlockSpec(memory_space=pltpu.VMEM)),
        input_output_aliases={0: 2, 1: 3},
        compiler_params=pltpu.CompilerParams(has_side_effects=SIDE_EFFECT),
    )(pltpu.with_memory_space_constraint(parts, pltpu.HBM), pltpu.with_memory_space_constraint(land, pltpu.HBM))


def _exchange_wait(send_sems, recv_sems, parts, land, after, name):
    def body(in_ref, land_ref, send_sems, recv_sems, after_ref, in_out, land_out):
        for cp in _exchange_copies(in_ref, land_ref, send_sems, recv_sems):
            cp.wait_send()
            cp.wait_recv()

    return pl.pallas_call(
        body, name=name, out_shape=(pltpu.HBM(parts.shape, parts.dtype), pltpu.HBM(land.shape, land.dtype)),
        in_specs=(HBM, HBM, SEM, SEM, ANY), out_specs=(HBM, HBM), input_output_aliases={0: 0, 1: 1},
        compiler_params=pltpu.CompilerParams(has_side_effects=SIDE_EFFECT),
    )(parts, land, send_sems, recv_sems, after)


def _chip_sum(parts, recv, chip, name):
    _, rows, cols = parts.shape
    tm = _tile(rows, 512, 16)

    def body(chip_ref, own_ref, recv_ref, o_ref):
        acc = own_ref[0].astype(F32)
        for j in range(3):
            acc = acc + recv_ref[j].astype(F32)
        o_ref[...] = acc

    return pl.pallas_call(
        body, name=name,
        grid_spec=pltpu.PrefetchScalarGridSpec(
            num_scalar_prefetch=1, grid=(rows // tm,),
            in_specs=[pl.BlockSpec((1, tm, cols), lambda i, chip_ref: (chip_ref[0], i, 0)),
                      pl.BlockSpec((3, tm, cols), lambda i, chip_ref: (0, i, 0))],
            out_specs=pl.BlockSpec((tm, cols), lambda i, chip_ref: (i, 0))),
        out_shape=jax.ShapeDtypeStruct((rows, cols), F32), compiler_params=_params("parallel"),
    )(chip, parts, recv)


def _join_unpack(mine, hs, groups, name):
    n = len(hs)
    offs = [sum(hs[:i]) for i in range(n)]
    cols = mine.shape[1]
    n_out = max(groups) + 1
    base = [2 * sum(h for h, g in zip(hs[:i], groups[:i]) if g == groups[i]) for i in range(n)]
    out_rows = [2 * sum(h for h, g in zip(hs, groups) if g == k) for k in range(n_out)]

    def body(in_ref, *refs):
        outs = refs[:n_out]
        send_sems, recv_sems, buf, in_sems, out_sems = refs[n_out:]
        x, y, c, _ = _position()
        sibling = (x, y, 1 - c)
        sent, local = [], []
        for i in range(n):
            src = in_ref.at[pl.ds(offs[i], hs[i]), :]
            here = outs[groups[i]].at[pl.ds(base[i] + c * hs[i], hs[i]), :]
            cp = _remote(src, here, send_sems.at[i], recv_sems.at[i], sibling)
            cp.start()
            sent.append(cp)
            local.append((src, here))
        _staged_copies(local, buf, in_sems, out_sems)
        for i, cp in enumerate(sent):
            there = outs[groups[i]].at[pl.ds(base[i] + (1 - c) * hs[i], hs[i]), :]
            _remote(there, there, send_sems.at[i], recv_sems.at[i], sibling).wait_recv()
            cp.wait_send()

    assert max(hs) <= STAGE_ROWS
    return pl.pallas_call(
        body, name=name, in_specs=[ANY], out_specs=[ANY] * n_out,
        out_shape=[jax.ShapeDtypeStruct((r, cols), F32) for r in out_rows],
        scratch_shapes=[pltpu.SemaphoreType.DMA((n,)), pltpu.SemaphoreType.DMA((n,))] + _stage_scratch(F32, cols),
        compiler_params=pltpu.CompilerParams(vmem_limit_bytes=VMEM_LIMIT),
    )(mine)


SMALL_ROWS = 16
SMALL_PACK_ROWS = 256


def _small_rows(n):
    return -(-n // (SMALL_ROWS * LANES)) * SMALL_ROWS


def _pack_small(arrs):
    parts = []
    for a in arrs:
        flat = a.reshape(-1)
        rows = _small_rows(flat.shape[0])
        flat = jnp.pad(flat, (0, rows * LANES - flat.shape[0]))
        parts.append(flat.reshape(rows, LANES))
    total = sum(p.shape[0] for p in parts)
    parts.append(jnp.zeros((-total % SMALL_PACK_ROWS, LANES), F32))
    return jnp.concatenate(parts, axis=0)


def _unpack_small(packed, shapes):
    out, r = [], 0
    for sh in shapes:
        n = math.prod(sh)
        cnt = _small_rows(n)
        out.append(packed[r:r + cnt].reshape(-1)[:n].reshape(sh))
        r += cnt
    return out


def _ffn_bwd(dh, dhb, h_in, saved, g_norm, w_gate_t, w_up_t, w_down, tag, after=None):
    n, gate, up, act = saved
    dgate, dup = _ffn_dact(dhb, w_down, gate, up, f"{tag}_dact", after)
    dw_down = _matmul(act, dhb, trans_a=True, name=f"{tag}_dwdown")
    dw_gate_t = _matmul(dgate, n, trans_a=True, name=f"{tag}_dwgate")
    dw_up_t = _matmul(dup, n, trans_a=True, name=f"{tag}_dwup")
    dh_in, dh_inb, dg = _dn_norm([(dgate, w_gate_t), (dup, w_up_t)], h_in, g_norm, dh, f"{tag}_dnorm")
    return dh_in, dh_inb, dg, dw_gate_t, dw_up_t, dw_down


def _local_step(x, tgt, w, big, late_weights, reduce_send, reduce_exchange):
    s = x.shape[0]
    tabs = _rope_tables(s)
    grads, gbig = {}, {}

    g_ev = w['ev_norm_g']
    n1 = _rms_fwd(x, g_ev, "ev_norm")
    proj0 = _matmul(n1, big['ev_w_in', 0], trans_b=True, name="ev_in", out_dtype=BF16, rows_inner=True)
    q0, k0, v0 = _qkv_prep_even(proj0, tabs, "ev_qkv")
    sinks = w['ev_sinks'].reshape(-1)
    o0, lse0, o0b = _attn_fwd(q0, k0, v0, sinks, max_dist=BLOCK - 1, name="ev_attn", emit_bf16=True)
    yconv, cout = _conv_fwd(proj0, w['ev_conv_w'][0], w['ev_conv_b'], w['ev_conv_ln_g'], w['ev_conv_ln_b'], "ev_conv")
    mix0 = (o0b[0], cout)
    g_f0 = w['ffn_norm_g'][0:1]
    h1, n2 = _matmul_norm(mix0, big['ev_w_out', 0], x, g_f0, "ev_out")
    big = {**big, **late_weights(h1)}

    g_od = w['od_norm_g']
    act0, gate0, up0 = _ffn_gate_up(n2, big['ffn_w_gate', 0], big['ffn_w_up', 0], "ffn0_gate_up")
    h2, n3 = _matmul_norm(act0, big['ffn_w_down', 0], h1, g_od, "ffn0_down")
    ffn0 = (n2, gate0, up0, act0)

    proj1 = _matmul(n3, big['od_w_in', 0], trans_b=True, name="od_in", out_dtype=BF16, rows_inner=True)
    qkv = _qkv_prep_odd(proj1, tabs, "od_qkv")
    nb = len(DILATIONS)
    outs, lses = [], []
    for i, d in enumerate(DILATIONS):
        o_r, lse_r = _attn_fwd(qkv[i], qkv[nb + i], qkv[2 * nb + i], None, max_dist=BLOCK, name=f"od_attn{d}", o_dtype=BF16)
        outs.append(o_r)
        lses.append(lse_r)
    comb = _combine(outs, lses, "od_combine")
    c_bf16 = comb[0]
    c_fold = {1: comb[1]}
    lse_fold = {1: comb[2]}
    for i, d in enumerate(DILATIONS[1:]):
        c_fold[d], lse_fold[d] = comb[3 + 2 * i], comb[4 + 2 * i]
    w_sp = w['od_spatial_w'][0]
    sb_t = w['od_spatial_b'][0].T
    mixed, dout = _gate_fwd(proj1, w['od_sgu_ln_g'], w['od_sgu_ln_b'], w_sp, sb_t, "od_gate")
    mix1 = (c_bf16, dout)
    g_f1 = w['ffn_norm_g'][1:2]
    h3, n4 = _matmul_norm(mix1, big['od_w_out', 0], h2, g_f1, "od_out")
    act1, gate1, up1 = _ffn_gate_up(n4, big['ffn_w_gate', 1], big['ffn_w_up', 1], "ffn1_gate_up")
    ffn1 = (n4, gate1, up1, act1)

    dh4, dh4b, dg_final, loss_tile = _matmul_final(act1, big['ffn_w_down', 1], h3, w['final_norm_g'].reshape(1, D_MODEL),
                                                   tgt, "ffn1_down_loss")
    grads['final_norm_g'] = dg_final.reshape(D_MODEL)

    dh3, dh3b, dg_f1, gbig['ffn_w_gate', 1], gbig['ffn_w_up', 1], gbig['ffn_w_down', 1] = _ffn_bwd(
        dh4, dh4b, h3, ffn1, g_f1, big['ffn_w_gate', 1], big['ffn_w_up', 1], big['ffn_w_down', 1], "ffn1")

    dmix1 = _matmul(dh3b, big['od_w_out', 0], trans_b=True, name="od_dmix")
    gbig['od_w_out', 0] = _matmul_tn_pair(mix1[0], mix1[1], dh3b, "od_dwout")
    do_fold = dict(zip(DILATIONS[1:], _fold_dout(dmix1, "od_fold_dout")))
    do_fold[1] = dmix1[None]
    dqs, dks, dvs = [], [], []
    for i, d in enumerate(DILATIONS):
        dq_r, dk_r, dv_r = _attn_bwd(qkv[i], qkv[nb + i], qkv[2 * nb + i], do_fold[d], c_fold[d], lse_fold[d], None,
                                     max_dist=BLOCK, name=f"od_dattn{d}")
        dqs.append(dq_r)
        dks.append(dk_r)
        dvs.append(dv_r)
    dz, dg_sgu, db_sgu, dw_sp, dsb = _gate_bwd(dmix1, proj1, mixed, w['od_sgu_ln_g'], w['od_sgu_ln_b'], w_sp, "od_dgate")
    grads['od_sgu_ln_g'], grads['od_sgu_ln_b'] = dg_sgu, db_sgu
    grads['od_spatial_w'], grads['od_spatial_b'] = dw_sp[None], dsb[None]
    dproj1 = _qkv_post_odd(dqs, dks, dvs, dz, tabs, "od_dproj")
    gbig['od_w_in', 0] = _matmul(dproj1, n3, trans_a=True, name="od_dwin")
    dh2, dh2b, dg_od = _dn_norm([(dproj1, big['od_w_in', 0])], h2, g_od, dh3, "od_dnorm")
    grads['od_norm_g'] = dg_od
    token = reduce_send(0, gbig)

    dh1, dh1b, dg_f0, gbig['ffn_w_gate', 0], gbig['ffn_w_up', 0], gbig['ffn_w_down', 0] = _ffn_bwd(
        dh2, dh2b, h1, ffn0, g_f0, big['ffn_w_gate', 0], big['ffn_w_up', 0], big['ffn_w_down', 0], "ffn0", token)
    grads['ffn_norm_g'] = jnp.concatenate([dg_f0, dg_f1], axis=0)
    token = reduce_exchange(0, dh1) + reduce_send(1, gbig)

    dmix0 = _matmul(dh1b, big['ev_w_out', 0], trans_b=True, name="ev_dmix", after=token)
    gbig['ev_w_out', 0] = _matmul_tn_pair(mix0[0], mix0[1], dh1b, "ev_dwout")
    dq0, dk0, dv0, dsink = _attn_bwd(q0, k0, v0, dmix0[None], o0, lse0, sinks, max_dist=BLOCK - 1, name="ev_dattn")
    grads['ev_sinks'] = dsink[:, 0, :].reshape(N_PAIRS, 2, HEAD_DIM)[:, :, 0].reshape(1, 8)
    token = reduce_exchange(1, dq0)
    dyc, dg_cln, db_cln, dcb = _conv_tail_bwd(dmix0, yconv, w['ev_conv_ln_g'] + token[0:1, 0:1], w['ev_conv_ln_b'],
                                              "ev_dconv_tail")
    grads['ev_conv_ln_g'], grads['ev_conv_ln_b'], grads['ev_conv_b'] = dg_cln, db_cln, dcb
    dglu, dconv_w = _conv_bwd(proj0, dyc, w['ev_conv_w'][0], "ev_dconv")
    grads['ev_conv_w'] = dconv_w[None]
    dproj0 = _qkv_post_even(dq0, dk0, dv0, dglu, tabs, "ev_dproj")
    gbig['ev_w_in', 0] = _matmul(dproj0, n1, trans_a=True, name="ev_dwin")
    dx, _, dg_ev = _dn_norm([(dproj0, big['ev_w_in', 0])], x, g_ev, dh1, "ev_dnorm")
    grads['ev_norm_g'] = dg_ev
    return loss_tile, dx, grads, gbig


def _shard_rows(w, layer, by_cols):
    return w[layer].T if by_cols else w[layer]


def kernel(x, ev_norm_g, ev_w_in, ev_sinks, ev_conv_w, ev_conv_b, ev_conv_ln_g, ev_conv_ln_b, ev_w_out, od_norm_g, od_w_in, od_sgu_ln_g, od_sgu_ln_b, od_spatial_w, od_spatial_b, od_w_out, ffn_norm_g, ffn_w_gate, ffn_w_up, ffn_w_down, final_norm_g, loss_target, m_ev_norm_g, m_ev_w_in, m_ev_sinks, m_ev_conv_w, m_ev_conv_b, m_ev_conv_ln_g, m_ev_conv_ln_b, m_ev_w_out, m_od_norm_g, m_od_w_in, m_od_sgu_ln_g, m_od_sgu_ln_b, m_od_spatial_w, m_od_spatial_b, m_od_w_out, m_ffn_norm_g, m_ffn_w_gate, m_ffn_w_up, m_ffn_w_down, m_final_norm_g, v_ev_norm_g, v_ev_w_in, v_ev_sinks, v_ev_conv_w, v_ev_conv_b, v_ev_conv_ln_g, v_ev_conv_ln_b, v_ev_w_out, v_od_norm_g, v_od_w_in, v_od_sgu_ln_g, v_od_sgu_ln_b, v_od_spatial_w, v_od_spatial_b, v_od_w_out, v_ffn_norm_g, v_ffn_w_gate, v_ffn_w_up, v_ffn_w_down, v_final_norm_g):
    given = dict(locals())
    wts = {n: given[n] for n in WEIGHTS}
    mom = {n: given["m_" + n] for n in WEIGHTS}
    var = {n: given["v_" + n] for n in WEIGHTS}
    chip = 2 * lax.axis_index("x") + lax.axis_index("y")

    shard_rows = [_shard_rows(wts[n], layer, by_cols).astype(BF16) for n, layer, by_cols in BIG]
    counts = [a.shape[0] for a in shard_rows]
    n_first = sum(n.startswith('ev_') for n, _, _ in BIG)

    def unpack(stacked, entries, cnts):
        out, r = {}, 0
        for (n, layer, _), cnt in zip(entries, cnts):
            out[n, layer] = stacked[:, r:r + cnt].reshape(N_CHIPS * cnt, D_MODEL)
            r += cnt
        return out

    first_w = _gather_chips(jnp.concatenate(shard_rows[:n_first], axis=0), "gather_weights_ev")
    big = unpack(first_w, BIG[:n_first], counts[:n_first])
    send_sems, recv_sems, late_shard, late_land, token = _gather_start(jnp.concatenate(shard_rows[n_first:], axis=0),
                                                                      first_w, "gather_weights_start")

    def late_weights(after):
        shard, land = _gather_wait(send_sems, recv_sems, late_shard, late_land, after, "gather_weights_wait")
        return unpack(_gather_finish(shard, land, "gather_weights_finish"), BIG[n_first:], counts[n_first:])

    full = {n: wts[n] for n in SMALL_REPL}
    full['ev_norm_g'] = full['ev_norm_g'] + token[0:1, 0:1]
    small_shards = [wts[n] for n in SMALL_SHARDED]
    small_shapes = [a.shape for a in small_shards]
    all_s = _gather_chips(_pack_small(small_shards), "gather_small_weights")
    per_chip = [_unpack_small(all_s[k], small_shapes) for k in range(N_CHIPS)]
    for i, n in enumerate(SMALL_SHARDED):
        full[n] = jnp.concatenate([per_chip[k][i] for k in range(N_CHIPS)], axis=-1)

    half_rows = {(n, layer): cnt // 2 for (n, layer, _), cnt in zip(BIG, counts)}
    in_flight = []

    sending = {}

    def halves(stage, gbig):
        return [gbig[e].reshape(N_CHIPS, 2, half_rows[e], D_MODEL) for e in GRAD_STAGES[stage]]

    def reduce_send(stage, gbig):
        send_sems, recv_sems, *buffers, token = _pair_send_start(halves(stage, gbig), f"grad_pair_start{stage}")
        sending[stage] = (send_sems, recv_sems, buffers)
        return token

    def reduce_exchange(stage, after):
        send_sems, recv_sems, buffers = sending.pop(stage)
        *split, got = _pair_send_wait(send_sems, recv_sems, buffers, after, f"grad_pair_wait{stage}")
        chip_part = _pair_add(split, got, f"grad_pair_add{stage}")
        *handles, token = _exchange_start(chip_part, f"grad_exchange_start{stage}")
        in_flight.append(handles)
        return token

    loss_tile, grad_x, grads, gbig = _local_step(x[0], loss_target[0], full, big, late_weights, reduce_send, reduce_exchange)
    loss = lax.psum(loss_tile[0, 0], ("x", "y", "c"))

    reduced = {}
    for stage, entries in enumerate(GRAD_STAGES):
        if stage < len(in_flight):
            chip_part, from_chips = _exchange_wait(*in_flight[stage], grad_x, f"grad_exchange_wait{stage}")
        else:
            split = halves(stage, gbig)
            chip_part = _pair_add(split, _pair_send(split, f"grad_pair_send{stage}"), f"grad_pair_add{stage}")
            from_chips = _chip_exchange(chip_part, f"grad_chip_exchange{stage}")
        my_half = _chip_sum(chip_part, from_chips, chip.reshape(1), f"grad_chip_sum{stage}")
        joined = _join_unpack(my_half, [half_rows[e] for e in entries], list(range(len(entries))), f"grad_join_halves{stage}")
        reduced.update(zip(entries, joined))

    small_names = SMALL_REPL + SMALL_SHARDED
    small_full_shapes = [grads[n].shape for n in small_names]
    spack = _pack_small([grads[n] for n in small_names])
    s_all = _gather_devices(spack, "grad_small_gather")
    s_sum = _unpack_small(_ordered_sum(s_all, "grad_small_sum"), small_full_shapes)
    g_all = dict(zip(small_names, s_sum))
    for n in SMALL_SHARDED:
        width = wts[n].shape[-1]
        g_all[n] = lax.dynamic_slice_in_dim(g_all[n], chip * width, width, axis=g_all[n].ndim - 1)

    delta, new_m, new_v = {}, {}, {}
    for n in BIG_NAMES:
        by_cols = [bc for nn, _, bc in BIG if nn == n][0]
        layers = wts[n].shape[0]

        def as_rows(a):
            return (jnp.swapaxes(a, 1, 2) if by_cols else a).reshape(-1, D_MODEL)

        def from_rows(a):
            a = a.reshape(layers, -1, D_MODEL)
            return jnp.swapaxes(a, 1, 2) if by_cols else a

        g_rows = [reduced[n, layer] for layer in range(layers)]
        g_rows = g_rows[0] if layers == 1 else jnp.concatenate(g_rows, axis=0)
        updated = _adamw(as_rows(wts[n]), g_rows, as_rows(mom[n]), as_rows(var[n]), f"adamw_{n}")
        g_all[n] = from_rows(g_rows)
        delta[n], new_m[n], new_v[n] = (from_rows(a) for a in updated)
    shapes = [wts[n].shape for n in small_names]
    d_s, m_s, v_s = _adamw(*[_pack_small([src[n] for n in small_names]) for src in (wts, g_all, mom, var)], "adamw_small")
    for dst, packed in ((delta, d_s), (new_m, m_s), (new_v, v_s)):
        dst.update(zip(small_names, _unpack_small(packed, shapes)))

    return (loss, grad_x[None], *[g_all[n] for n in WEIGHTS], *[delta[n] for n in WEIGHTS],
            *[new_m[n] for n in WEIGHTS], *[new_v[n] for n in WEIGHTS])
```

```python
import math

import jax
import jax.numpy as jnp
from jax import lax
from jax.experimental import pallas as pl
from jax.experimental.pallas import tpu as pltpu

F32 = jnp.float32
BF16 = jnp.bfloat16

D_MODEL = 1024
HEAD_DIM = 64
ROT_DIM = 16
ROPE_THETA = 500000.0
RMS_EPS = 1e-6
LN_EPS = 1e-5
BLOCK = 128
CONV_WIDTH = 31
CONV_HALO = 32
CONV_ROWS = 64
D_FF = 2816
N_GROUPS = 8
ATTN_W = 512
ATTN_SCALE = HEAD_DIM ** -0.5
NEG = -1e30
DILATIONS = (1, 4, 16)

ADAM_LR = 0.001
ADAM_B1 = 0.9
ADAM_B2 = 0.999
ADAM_EPS = 1e-08
ADAM_WD = 0.01
ADAM_STEP = 10

LANES = 128
N_PAIRS = ATTN_W // LANES
VMEM_LIMIT = 56 * 1024 * 1024
MESH = pl.DeviceIdType.MESH
N_CHIPS = 4
N_DEV = 8

WEIGHTS = ['ev_norm_g', 'ev_w_in', 'ev_sinks', 'ev_conv_w', 'ev_conv_b', 'ev_conv_ln_g', 'ev_conv_ln_b', 'ev_w_out',
           'od_norm_g', 'od_w_in', 'od_sgu_ln_g', 'od_sgu_ln_b', 'od_spatial_w', 'od_spatial_b', 'od_w_out',
           'ffn_norm_g', 'ffn_w_gate', 'ffn_w_up', 'ffn_w_down', 'final_norm_g']
BIG = [('ev_w_in', 0, True), ('ev_w_out', 0, False), ('od_w_in', 0, True), ('od_w_out', 0, False),
       ('ffn_w_gate', 0, True), ('ffn_w_gate', 1, True), ('ffn_w_up', 0, True), ('ffn_w_up', 1, True),
       ('ffn_w_down', 0, False), ('ffn_w_down', 1, False)]
BIG_NAMES = ['ev_w_in', 'ev_w_out', 'od_w_in', 'od_w_out', 'ffn_w_gate', 'ffn_w_up', 'ffn_w_down']
GRAD_STAGES = ([('od_w_in', 0), ('od_w_out', 0), ('ffn_w_gate', 1), ('ffn_w_up', 1), ('ffn_w_down', 1)],
               [('ffn_w_gate', 0), ('ffn_w_up', 0), ('ffn_w_down', 0)],
               [('ev_w_in', 0), ('ev_w_out', 0)])
SMALL_SHARDED = ['ev_conv_w', 'od_norm_g', 'od_sgu_ln_g', 'od_sgu_ln_b']
SMALL_REPL = ['ev_norm_g', 'ev_sinks', 'ev_conv_b', 'ev_conv_ln_g', 'ev_conv_ln_b', 'od_spatial_w', 'od_spatial_b',
              'ffn_norm_g', 'final_norm_g']


def _tile(n, cap, mult=LANES):
    best = None
    for t in range(mult, min(n, cap) + 1, mult):
        if n % t == 0:
            best = t
    assert best is not None, (n, cap)
    return best


def _params(*sem):
    return pltpu.CompilerParams(dimension_semantics=sem, vmem_limit_bytes=VMEM_LIMIT)


def _sigmoid(x):
    return 1.0 / (1.0 + jnp.exp(-x))


def _pair_block(p):
    return slice(p * LANES, (p + 1) * LANES)


def _matmul(a, b, *, name, trans_a=False, trans_b=False, add=None, out_dtype=F32, after=None, rows_inner=False):
    parts = a if isinstance(a, (tuple, list)) else (a,)
    if trans_a:
        k, m = parts[0].shape
    else:
        m = parts[0].shape[0]
        k = sum(p.shape[1] for p in parts)
    if trans_b:
        n, k2 = b.shape
    else:
        k2, n = b.shape
    assert k == k2 and b.dtype == BF16 and all(p.dtype == BF16 for p in parts)
    tm = _tile(m, D_FF // 2 if trans_a else 512)
    tn = _tile(n, D_FF // 2)
    tk = k if k <= D_FF else _tile(k, 2048)
    nk = k // tk
    na = len(parts)
    assert na == 1 or (nk == 1 and not trans_a)
    assert nk == 1 or out_dtype == F32
    dims = (((0 if trans_a else 1,), (1 if trans_b else 0,)), ((), ()))
    has_add = add is not None

    def body(*refs):
        a_refs, b_ref = refs[:na], refs[na]
        add_ref = refs[na + 1] if has_add else None
        o_ref = refs[na + 1 + has_add + (after is not None)]
        def product():
            a_val = a_refs[0][...] if na == 1 else jnp.concatenate([r[...] for r in a_refs], axis=1)
            return lax.dot_general(a_val, b_ref[...], dims, preferred_element_type=F32)

        if nk == 1:
            part = product()
            if has_add:
                part = part + add_ref[...]
            o_ref[...] = part.astype(o_ref.dtype)
            return
        kk = pl.program_id(2)

        @pl.when(kk == 0)
        def _():
            o_ref[...] = product() + add_ref[...] if has_add else product()

        @pl.when(kk > 0)
        def _():
            o_ref[...] = product() + o_ref[...]

    def at(f):
        return (lambda j, i, kk: f(i, j, kk)) if rows_inner else f

    if trans_a:
        a_specs = [pl.BlockSpec((tk, tm), at(lambda i, j, kk: (kk, i)))]
    elif na == 1:
        a_specs = [pl.BlockSpec((tm, tk), at(lambda i, j, kk: (i, kk)))]
    else:
        a_specs = [pl.BlockSpec((tm, p.shape[1]), at(lambda i, j, kk: (i, 0))) for p in parts]
    b_spec = (pl.BlockSpec((tn, tk), at(lambda i, j, kk: (j, kk))) if trans_b
              else pl.BlockSpec((tk, tn), at(lambda i, j, kk: (kk, j))))
    o_spec = pl.BlockSpec((tm, tn), at(lambda i, j, kk: (i, j)))
    in_specs = a_specs + [b_spec] + ([o_spec] if has_add else [])
    operands = list(parts) + [b] + ([add] if has_add else [])
    if after is not None:
        in_specs.append(_after_spec(after))
        operands.append(after)
    grid = (n // tn, m // tm, nk) if rows_inner else (m // tm, n // tn, nk)
    return pl.pallas_call(
        body, name=name, grid=grid, in_specs=in_specs, out_specs=o_spec,
        out_shape=jax.ShapeDtypeStruct((m, n), out_dtype),
        compiler_params=_params("parallel", "parallel", "arbitrary"),
    )(*operands)


def _matmul_rows(a, b, add, epilogue, consts, tiled, outs, accs, name):
    parts = a if isinstance(a, (tuple, list)) else (a,)
    m = parts[0].shape[0]
    tm = 512
    na, nc, nt, no = len(parts), len(consts), len(tiled), len(outs)

    def body(*refs):
        a_refs, b_ref, add_ref = refs[:na], refs[na], refs[na + 1]
        const_refs = refs[na + 2:na + 2 + nc]
        tiled_refs = refs[na + 2 + nc:na + 2 + nc + nt]
        out_refs = refs[na + 2 + nc + nt:]
        a_val = a_refs[0][...] if na == 1 else jnp.concatenate([r[...] for r in a_refs], axis=1)
        h = jnp.dot(a_val, b_ref[...], preferred_element_type=F32) + add_ref[...]
        results = epilogue(h, [r[...] for r in const_refs], [r[...] for r in tiled_refs])
        for o_ref, val in zip(out_refs[:no], results[:no]):
            o_ref[...] = val.astype(o_ref.dtype)
        if accs:
            @pl.when(_first_step())
            def _():
                for o_ref in out_refs[no:]:
                    o_ref[...] = jnp.zeros_like(o_ref)

            for o_ref, val in zip(out_refs[no:], results[no:]):
                o_ref[...] += val

    row = lambda w: pl.BlockSpec((tm, w), lambda i: (i, 0))
    whole = lambda shape: pl.BlockSpec(shape, lambda i: (0,) * len(shape))
    return pl.pallas_call(
        body, name=name, grid=(m // tm,),
        in_specs=[row(p.shape[1]) for p in parts] + [whole(b.shape), row(D_MODEL)] + [whole(c.shape) for c in consts]
        + [row(t.shape[1]) for t in tiled],
        out_specs=[row(c) for c, _ in outs] + [whole(sh) for sh, _ in accs],
        out_shape=[jax.ShapeDtypeStruct((m, c), dt) for c, dt in outs] + [jax.ShapeDtypeStruct(sh, dt) for sh, dt in accs],
        compiler_params=_params("arbitrary"),
    )(*parts, b, add, *consts, *tiled)


def _matmul_norm(a, b, add, g, name):
    def epilogue(h, consts, tiled):
        r = lax.rsqrt(jnp.mean(h * h, axis=-1, keepdims=True) + RMS_EPS)
        return [h, h * r * consts[0]]

    return _matmul_rows(a, b, add, epilogue, [g], [], [(D_MODEL, F32), (D_MODEL, BF16)], [], name)


def _matmul_final(a, b, add, g, tgt, name):
    def epilogue(h, consts, tiled):
        gg = consts[0]
        r = lax.rsqrt(jnp.mean(h * h, axis=-1, keepdims=True) + RMS_EPS)
        xh = h * r
        e = xh * gg - tiled[0]
        loss = (0.5 / D_MODEL) * jnp.sum(jnp.sum(e * e, axis=-1, keepdims=True), axis=0, keepdims=True)
        dy = e * (1.0 / D_MODEL)
        dxh = dy * gg
        dx = r * (dxh - xh * jnp.mean(dxh * xh, axis=-1, keepdims=True))
        return [dx, dx, jnp.sum(dy * xh, axis=0, keepdims=True), jnp.broadcast_to(loss, (1, LANES))]

    return _matmul_rows(a, b, add, epilogue, [g], [tgt], [(D_MODEL, F32), (D_MODEL, BF16)],
                        [((1, D_MODEL), F32), ((1, LANES), F32)], name)


def _matmul_tn_pair(a1, a2, b, name):
    kdim, m1 = a1.shape
    m2 = a2.shape[1]
    n = b.shape[1]
    tn = _tile(n, 1024)
    tk = _tile(kdim, 2048)
    nk = kdim // tk
    dims = (((0,), (0,)), ((), ()))

    def body(a1_ref, a2_ref, b_ref, o_ref):
        kk = pl.program_id(1)
        def products():
            bv = b_ref[...]
            return (lax.dot_general(a1_ref[...], bv, dims, preferred_element_type=F32),
                    lax.dot_general(a2_ref[...], bv, dims, preferred_element_type=F32))

        @pl.when(kk == 0)
        def _():
            o_ref[0:m1, :], o_ref[m1:, :] = products()

        @pl.when(kk > 0)
        def _():
            top, bot = products()
            o_ref[0:m1, :] = top + o_ref[0:m1, :]
            o_ref[m1:, :] = bot + o_ref[m1:, :]

    return pl.pallas_call(
        body, name=name, grid=(n // tn, nk),
        in_specs=[pl.BlockSpec((tk, m1), lambda j, kk: (kk, 0)), pl.BlockSpec((tk, m2), lambda j, kk: (kk, 0)),
                  pl.BlockSpec((tk, tn), lambda j, kk: (kk, j))],
        out_specs=pl.BlockSpec((m1 + m2, tn), lambda j, kk: (0, j)),
        out_shape=jax.ShapeDtypeStruct((m1 + m2, n), F32),
        compiler_params=_params("parallel", "arbitrary"),
    )(a1, a2, b)


def _ffn_gate_up(n, w_gate_t, w_up_t, name):
    m, k = n.shape
    f = w_gate_t.shape[0]
    tm, tn = _tile(m, 1024), _tile(f, D_FF // 2)

    def body(n_ref, wg_ref, wu_ref, act_ref, gate_ref, up_ref):
        a = n_ref[...]

        def products(cols):
            return (lax.dot_general(a, wg_ref[cols, :], NT, preferred_element_type=F32),
                    lax.dot_general(a, wu_ref[cols, :], NT, preferred_element_type=F32))

        chunks = _col_chunks(tn)
        ahead = products(chunks[0])
        for idx, cols in enumerate(chunks):
            gate, up = ahead
            if idx + 1 < len(chunks):
                ahead = products(chunks[idx + 1])
            act_ref[:, cols] = (gate * _sigmoid(gate) * up).astype(BF16)
            gate_ref[:, cols] = gate.astype(BF16)
            up_ref[:, cols] = up.astype(BF16)

    wspec = pl.BlockSpec((tn, k), lambda j, i: (j, 0))
    ospec = pl.BlockSpec((tm, tn), lambda j, i: (i, j))
    return pl.pallas_call(
        body, name=name, grid=(f // tn, m // tm), in_specs=[pl.BlockSpec((tm, k), lambda j, i: (i, 0)), wspec, wspec],
        out_specs=[ospec] * 3, out_shape=[jax.ShapeDtypeStruct((m, f), BF16)] * 3,
        compiler_params=_params("parallel", "parallel"),
    )(n, w_gate_t, w_up_t)


def _col_chunks(n, width=384):
    return [slice(c, min(c + width, n)) for c in range(0, n, width)]


def _after_spec(after):
    return pl.BlockSpec(after.shape, lambda *_: (0,) * after.ndim)


def _ffn_dact(dhb, w_down, gate, up, name, after=None):
    m, k = dhb.shape
    f = w_down.shape[0]
    tm, tn = _tile(m, 1024), _tile(f, D_FF // 2)

    def body(d_ref, w_ref, g_ref, u_ref, *rest):
        dg_ref, du_ref = rest[-2:]
        d = d_ref[...]

        def product(cols):
            return lax.dot_general(d, w_ref[cols, :], NT, preferred_element_type=F32)

        chunks = _col_chunks(tn)
        ahead = product(chunks[0])
        for idx, cols in enumerate(chunks):
            dact = ahead
            if idx + 1 < len(chunks):
                ahead = product(chunks[idx + 1])
            g = g_ref[:, cols].astype(F32)
            sg = _sigmoid(g)
            dg_ref[:, cols] = (dact * u_ref[:, cols].astype(F32) * sg * (1.0 + g * (1.0 - sg))).astype(BF16)
            du_ref[:, cols] = (dact * g * sg).astype(BF16)

    ospec = pl.BlockSpec((tm, tn), lambda j, i: (i, j))
    extra = [] if after is None else [after]
    return pl.pallas_call(
        body, name=name, grid=(f // tn, m // tm),
        in_specs=[pl.BlockSpec((tm, k), lambda j, i: (i, 0)), pl.BlockSpec((tn, k), lambda j, i: (j, 0)), ospec, ospec]
        + [_after_spec(a) for a in extra],
        out_specs=[ospec] * 2, out_shape=[jax.ShapeDtypeStruct((m, f), BF16)] * 2,
        compiler_params=_params("parallel", "parallel"),
    )(dhb, w_down, gate, up, *extra)


def _dn_norm(pairs, h, g, dres, name):
    m = h.shape[0]
    tm = 512
    np_ = len(pairs)

    def body(*refs):
        a_refs, b_refs = refs[:np_], refs[np_:2 * np_]
        h_ref, dres_ref, g_ref, dh_ref, dhb_ref, dg_ref = refs[2 * np_:]

        @pl.when(_first_step())
        def _():
            dg_ref[...] = jnp.zeros_like(dg_ref)

        dy = jnp.dot(a_refs[0][...], b_refs[0][...], preferred_element_type=F32)
        for a_ref, b_ref in zip(a_refs[1:], b_refs[1:]):
            dy = jnp.dot(a_ref[...], b_ref[...], preferred_element_type=F32) + dy
        x = h_ref[...]
        r = lax.rsqrt(jnp.mean(x * x, axis=-1, keepdims=True) + RMS_EPS)
        xh = x * r
        dg_ref[...] += jnp.sum(dy * xh, axis=0, keepdims=True)
        dxh = dy * g_ref[...]
        tot = dres_ref[...] + r * (dxh - xh * jnp.mean(dxh * xh, axis=-1, keepdims=True))
        dh_ref[...] = tot
        dhb_ref[...] = tot.astype(BF16)

    row = lambda w: pl.BlockSpec((tm, w), lambda i: (i, 0))
    whole = lambda a: pl.BlockSpec(a.shape, lambda i: (0, 0))
    a_list, b_list = [a for a, _ in pairs], [b for _, b in pairs]
    return pl.pallas_call(
        body, name=name, grid=(m // tm,),
        in_specs=[row(a.shape[1]) for a in a_list] + [whole(b) for b in b_list] + [row(D_MODEL), row(D_MODEL), whole(g)],
        out_specs=[row(D_MODEL), row(D_MODEL), pl.BlockSpec((1, D_MODEL), lambda i: (0, 0))],
        out_shape=[jax.ShapeDtypeStruct((m, D_MODEL), F32), jax.ShapeDtypeStruct((m, D_MODEL), BF16),
                   jax.ShapeDtypeStruct((1, D_MODEL), F32)],
        compiler_params=_params("arbitrary"),
    )(*a_list, *b_list, h, dres, g)


def _rows(body, name, tm, tiled, consts, outs, accs=()):
    s = tiled[0].shape[0]
    assert s % tm == 0
    in_specs = [pl.BlockSpec((tm, a.shape[1]), lambda i: (i, 0)) for a in tiled]
    in_specs += [pl.BlockSpec(a.shape, lambda i, nd=a.ndim: (0,) * nd) for a in consts]
    out_shape = [jax.ShapeDtypeStruct((s, c), dt) for c, dt in outs]
    out_shape += [jax.ShapeDtypeStruct(sh, dt) for sh, dt in accs]
    out_specs = [pl.BlockSpec((tm, c), lambda i: (i, 0)) for c, _ in outs]
    out_specs += [pl.BlockSpec(sh, lambda i, nd=len(sh): (0,) * nd) for sh, _ in accs]
    return pl.pallas_call(
        body, name=name, grid=(s // tm,), in_specs=in_specs, out_specs=out_specs, out_shape=out_shape,
        compiler_params=_params("arbitrary"),
    )(*tiled, *consts)


def _first_step():
    return pl.program_id(0) == 0


def _rms_fwd(h, g, name):
    def body(h_ref, g_ref, n_ref):
        x = h_ref[...]
        r = lax.rsqrt(jnp.mean(x * x, axis=-1, keepdims=True) + RMS_EPS)
        n_ref[...] = (x * r * g_ref[...]).astype(BF16)

    return _rows(body, name, 512, [h], [g], [(D_MODEL, BF16)])[0]


def _rope_tables(s):
    half = ROT_DIM // 2
    inv_freq = ROPE_THETA ** (-jnp.arange(half, dtype=F32) * (2.0 / ROT_DIM))
    ang = jnp.arange(s, dtype=F32)[:, None] * inv_freq[None, :]
    cos, sin = jnp.cos(ang), jnp.sin(ang)
    rest = HEAD_DIM - ROT_DIM
    ones = jnp.ones((s, rest), F32)
    zeros = jnp.zeros((s, rest), F32)
    zh = jnp.zeros((s, half), F32)
    c_t = jnp.concatenate([cos, cos, ones], axis=1)
    a_t = jnp.concatenate([-sin, zh, zeros], axis=1)
    b_t = jnp.concatenate([zh, sin, zeros], axis=1)
    return tuple(jnp.tile(t, (1, LANES // HEAD_DIM)) for t in (c_t, a_t, b_t))


def _rot(x, c, a, b):
    w = x.shape[1]
    half = ROT_DIM // 2
    return x * c + pltpu.roll(x, w - half, 1) * a + pltpu.roll(x, half, 1) * b


def _wide(t, w):
    return t if w == LANES else jnp.tile(t, (1, w // LANES))


def _low_lanes(rows):
    return lax.broadcasted_iota(jnp.int32, (rows, LANES), 1) < HEAD_DIM


def _fold_store(x, sc_ref, out_refs):
    tm = x.shape[0]
    if any(d > 1 for d in out_refs):
        for p in range(N_PAIRS):
            sc_ref[p] = x[:, _pair_block(p)]
    for d, o_ref in out_refs.items():
        if d == 1:
            o_ref[0] = x.astype(o_ref.dtype)
            continue
        for r in range(d):
            for p in range(N_PAIRS):
                o_ref[r, :, _pair_block(p)] = sc_ref[p, pl.ds(r, tm // d, stride=d), :].astype(o_ref.dtype)


def _unfold_load(x_ref, sc_ref, d, add=False):
    n = x_ref.shape[1]
    for r in range(d):
        for p in range(N_PAIRS):
            rows = pl.ds(r, n, stride=d) if d > 1 else slice(None)
            val = x_ref[r, :, _pair_block(p)].astype(F32)
            if add:
                val = val + sc_ref[p, rows, :]
            sc_ref[p, rows, :] = val


def _folded_spec(d, tm, w=ATTN_W):
    return pl.BlockSpec((d, tm // d, w), lambda i: (0, i, 0))


def _folded_shape(s, d, dtype, w=ATTN_W):
    return jax.ShapeDtypeStruct((d, s // d, w), dtype)


def _qkv_prep_even(proj, tabs, name):
    s = proj.shape[0]
    tm = 512

    def body(p_ref, c_ref, a_ref, b_ref, q_ref, k_ref, v_ref):
        c, a, b = c_ref[...], a_ref[...], b_ref[...]
        q_ref[0] = _rot(p_ref[:, 0:ATTN_W].astype(F32), _wide(c, ATTN_W), _wide(a, ATTN_W), _wide(b, ATTN_W)).astype(BF16)
        lo = _low_lanes(tm)
        for src, o_ref in ((_rot(p_ref[:, 512:640].astype(F32), c, a, b), k_ref), (p_ref[:, 640:768].astype(F32), v_ref)):
            swapped = pltpu.roll(src, HEAD_DIM, 1)
            o_ref[0, :, 0:LANES] = jnp.where(lo, src, swapped).astype(BF16)
            o_ref[0, :, LANES:] = jnp.where(lo, swapped, src).astype(BF16)

    row = lambda w: pl.BlockSpec((tm, w), lambda i: (i, 0))
    return pl.pallas_call(
        body, name=name, grid=(s // tm,), in_specs=[row(proj.shape[1]), row(LANES), row(LANES), row(LANES)],
        out_specs=[_folded_spec(1, tm), _folded_spec(1, tm, 2 * LANES), _folded_spec(1, tm, 2 * LANES)],
        out_shape=[_folded_shape(s, 1, BF16), _folded_shape(s, 1, BF16, 2 * LANES), _folded_shape(s, 1, BF16, 2 * LANES)],
        compiler_params=_params("parallel"),
    )(proj, *tabs)


def _qkv_post_even(dq, dk, dv, dglu, tabs, name):
    s = dglu.shape[0]
    tm = 512

    def body(dq_ref, dk_ref, dv_ref, dr_ref, c_ref, a_ref, b_ref, o_ref):
        c, a, b = c_ref[...], -a_ref[...], -b_ref[...]
        o_ref[:, 0:ATTN_W] = _rot(dq_ref[0].astype(F32), _wide(c, ATTN_W), _wide(a, ATTN_W), _wide(b, ATTN_W)).astype(BF16)
        lo = _low_lanes(tm)
        merged = []
        for ref in (dk_ref, dv_ref):
            first, second = ref[0, :, 0:LANES].astype(F32), ref[0, :, LANES:].astype(F32)
            merged.append(jnp.where(lo, first + pltpu.roll(first, HEAD_DIM, 1), second + pltpu.roll(second, HEAD_DIM, 1)))
        o_ref[:, 512:640] = _rot(merged[0], c, a, b).astype(BF16)
        o_ref[:, 640:768] = merged[1].astype(BF16)
        o_ref[:, 768:] = dr_ref[...]

    row = lambda w: pl.BlockSpec((tm, w), lambda i: (i, 0))
    return pl.pallas_call(
        body, name=name, grid=(s // tm,),
        in_specs=[_folded_spec(1, tm), _folded_spec(1, tm, 2 * LANES), _folded_spec(1, tm, 2 * LANES),
                  row(dglu.shape[1]), row(LANES), row(LANES), row(LANES)],
        out_specs=row(EVEN_IN), out_shape=jax.ShapeDtypeStruct((s, EVEN_IN), BF16),
        compiler_params=_params("parallel"),
    )(dq, dk, dv, dglu, *tabs)


def _qkv_prep_odd(proj, tabs, name):
    s = proj.shape[0]
    tm = 1024

    def body(p_ref, c_ref, a_ref, b_ref, *rest):
        outs, sc_ref = rest[:-1], rest[-1]
        c, a, b = (_wide(t[...], ATTN_W) for t in (c_ref, a_ref, b_ref))
        for t in range(3):
            x = p_ref[:, t * ATTN_W:(t + 1) * ATTN_W].astype(F32)
            if t < 2:
                x = _rot(x, c, a, b)
            _fold_store(x, sc_ref, {d: outs[t * len(DILATIONS) + i] for i, d in enumerate(DILATIONS)})

    row = lambda w: pl.BlockSpec((tm, w), lambda i: (i, 0))
    return pl.pallas_call(
        body, name=name, grid=(s // tm,), in_specs=[row(proj.shape[1]), row(LANES), row(LANES), row(LANES)],
        out_specs=[_folded_spec(d, tm) for _ in range(3) for d in DILATIONS],
        out_shape=[_folded_shape(s, d, BF16) for _ in range(3) for d in DILATIONS],
        scratch_shapes=[pltpu.VMEM((N_PAIRS, tm, LANES), F32)],
        compiler_params=_params("parallel"),
    )(proj, *tabs)


def _qkv_post_odd(dqs, dks, dvs, dz, tabs, name):
    s = dz.shape[0]
    tm = 512
    nb = len(DILATIONS)

    def body(*refs):
        groups = (refs[:nb], refs[nb:2 * nb], refs[2 * nb:3 * nb])
        dz_ref, c_ref, a_ref, b_ref, o_ref, sc_ref = refs[3 * nb:]
        c, a, b = _wide(c_ref[...], ATTN_W), _wide(-a_ref[...], ATTN_W), _wide(-b_ref[...], ATTN_W)
        for t, group in enumerate(groups):
            for i, d in enumerate(DILATIONS):
                _unfold_load(group[i], sc_ref, d, add=i > 0)
            x = jnp.concatenate([sc_ref[p] for p in range(N_PAIRS)], axis=1)
            if t < 2:
                x = _rot(x, c, a, b)
            o_ref[:, t * ATTN_W:(t + 1) * ATTN_W] = x.astype(BF16)
        o_ref[:, 3 * ATTN_W:] = dz_ref[...]

    row = lambda w: pl.BlockSpec((tm, w), lambda i: (i, 0))
    return pl.pallas_call(
        body, name=name, grid=(s // tm,),
        in_specs=[_folded_spec(d, tm) for _ in range(3) for d in DILATIONS] + [row(dz.shape[1]), row(LANES), row(LANES), row(LANES)],
        out_specs=row(ODD_IN), out_shape=jax.ShapeDtypeStruct((s, ODD_IN), BF16),
        scratch_shapes=[pltpu.VMEM((N_PAIRS, tm, LANES), F32)],
        compiler_params=_params("parallel"),
    )(*dqs, *dks, *dvs, dz, *tabs)


def _fold_dout(dmix, name):
    s = dmix.shape[0]
    tm = 512
    ds = [d for d in DILATIONS if d > 1]

    def body(d_ref, *rest):
        outs, sc_ref = rest[:-1], rest[-1]
        _fold_store(d_ref[...], sc_ref, dict(zip(ds, outs)))

    return pl.pallas_call(
        body, name=name, grid=(s // tm,), in_specs=[pl.BlockSpec((tm, ATTN_W), lambda i: (i, 0))],
        out_specs=[_folded_spec(d, tm) for d in ds], out_shape=[_folded_shape(s, d, BF16) for d in ds],
        scratch_shapes=[pltpu.VMEM((N_PAIRS, tm, LANES), F32)],
        compiler_params=_params("parallel"),
    )(dmix)


def _window(j, i, tq):
    r0 = j * tq + i * BLOCK
    if i > 0:
        return pl.ds(pl.multiple_of(r0 - BLOCK, BLOCK), 2 * BLOCK), BLOCK
    start = pl.multiple_of(jnp.maximum(r0 - BLOCK, 0), BLOCK)
    return pl.ds(start, 2 * BLOCK), r0 - start


def _band_valid(offset, max_dist):
    shape = (2 * BLOCK, 2 * BLOCK)
    dist = (lax.bitwise_and(lax.broadcasted_iota(jnp.int32, shape, 0), BLOCK - 1)
            - lax.broadcasted_iota(jnp.int32, shape, 1) + offset)
    return jnp.abs(2 * dist - max_dist) <= max_dist


def _stack_heads(lo, x):
    zero = jnp.zeros_like(x)
    return jnp.concatenate([jnp.where(lo, x, zero), jnp.where(lo, zero, x)], axis=0)


def _unstack_heads(lo, x):
    return jnp.where(lo, x[:BLOCK], x[BLOCK:])


NT = (((1,), (1,)), ((), ()))
TN = (((0,), (0,)), ((), ()))


def _attn_fwd(q, k, v, sinks, *, max_dist, name, emit_bf16=False, o_dtype=F32):
    d, sp, wq = q.shape
    nq, nk = wq // LANES, k.shape[2] // LANES
    kdiv = nq // nk
    tq = min(sp, 1024)
    nsub = tq // BLOCK
    has_sink = sinks is not None

    def body(*refs):
        refs = list(refs)
        sink_ref = refs.pop(0) if has_sink else None
        q_ref, k_ref, v_ref, o_ref, lse_ref = refs[:5]
        pair = pl.program_id(1)
        j = pl.program_id(2)
        lo = _low_lanes(BLOCK)
        if has_sink:
            first_head = lax.broadcasted_iota(jnp.int32, (2 * BLOCK, 1), 0) < BLOCK
            sk = jnp.where(first_head, sink_ref[2 * pair], sink_ref[2 * pair + 1])
        for i in range(nsub):
            win, offset = _window(j, i, tq)
            rows = slice(i * BLOCK, (i + 1) * BLOCK)
            kw = k_ref[0, win, :]
            vw = v_ref[0, win, :]
            s = lax.dot_general(_stack_heads(lo, q_ref[0, rows, :]), kw, NT, preferred_element_type=F32) * ATTN_SCALE
            s = jnp.where(_band_valid(offset, max_dist), s, NEG)
            m = jnp.max(s, axis=-1, keepdims=True)
            if has_sink:
                m = jnp.maximum(m, sk)
            p = jnp.exp(s - m)
            l = jnp.sum(p, axis=-1, keepdims=True)
            if has_sink:
                l = l + jnp.exp(sk - m)
            o2 = _unstack_heads(lo, jnp.dot(p.astype(BF16), vw, preferred_element_type=F32) / l)
            o_ref[0, rows, :] = o2.astype(o_ref.dtype)
            lse_ref[0, rows, :] = _unstack_heads(lo, m + jnp.log(l))
            if emit_bf16:
                refs[5][0, rows, :] = o2.astype(BF16)

    qspec = pl.BlockSpec((1, tq, LANES), lambda r, p, j: (r, j, p))
    kspec = pl.BlockSpec((1, sp, LANES), lambda r, p, j: (r, 0, p // kdiv))
    in_specs = [qspec, kspec, kspec]
    operands = [q, k, v]
    if has_sink:
        in_specs = [pl.BlockSpec(memory_space=pltpu.SMEM)] + in_specs
        operands = [sinks] + operands
    out_shape = [jax.ShapeDtypeStruct(q.shape, o_dtype), jax.ShapeDtypeStruct(q.shape, F32)]
    if emit_bf16:
        out_shape.append(jax.ShapeDtypeStruct(q.shape, BF16))
    return pl.pallas_call(
        body, name=name, grid=(d, nq, sp // tq), in_specs=in_specs, out_specs=[qspec] * len(out_shape),
        out_shape=out_shape, compiler_params=_params("parallel", "parallel", "arbitrary"),
    )(*operands)


def _attn_bwd(q, k, v, do, oo, lse, sinks, *, max_dist, name):
    d, sp, wq = q.shape
    wk = k.shape[2]
    nq, nk = wq // LANES, wk // LANES
    kdiv = nq // nk
    tq = min(sp, 1024)
    nsub = tq // BLOCK
    has_sink = sinks is not None

    def body(*refs):
        refs = list(refs)
        sink_ref = refs.pop(0) if has_sink else None
        q_ref, k_ref, v_ref, do_ref, oo_ref, lse_ref, dq_ref, dk_out, dv_out = refs[:9]
        dk_ref, dv_ref = refs[-2:]
        pk, g, j = pl.program_id(1), pl.program_id(2), pl.program_id(3)

        @pl.when((g == 0) & (j == 0))
        def _():
            dk_ref[...] = jnp.zeros_like(dk_ref)
            dv_ref[...] = jnp.zeros_like(dv_ref)

        lo = _low_lanes(BLOCK)
        if has_sink:
            first_head = lax.broadcasted_iota(jnp.int32, (2 * BLOCK, 1), 0) < BLOCK
            pair = pk * kdiv + g
            sk = jnp.where(first_head, sink_ref[2 * pair], sink_ref[2 * pair + 1])
            sink_acc = jnp.zeros((2 * BLOCK, LANES), F32)
        for i in range(nsub):
            win, offset = _window(j, i, tq)
            rows = slice(i * BLOCK, (i + 1) * BLOCK)
            kw = k_ref[0, win, :]
            vw = v_ref[0, win, :]
            do2 = do_ref[0, rows, :].astype(F32)
            qs = _stack_heads(lo, q_ref[0, rows, :])
            dos = _stack_heads(lo, do2.astype(BF16))
            prod = do2 * oo_ref[0, rows, :]
            delta = jnp.sum(_stack_heads(lo, prod), axis=-1, keepdims=True)
            lse2 = lse_ref[0, rows, :]
            lse_swapped = pltpu.roll(lse2, HEAD_DIM, 1)
            lse_st = jnp.concatenate([jnp.where(lo, lse2, lse_swapped), jnp.where(lo, lse_swapped, lse2)], axis=0)
            s = lax.dot_general(qs, kw, NT, preferred_element_type=F32) * ATTN_SCALE
            s = jnp.where(_band_valid(offset, max_dist), s, NEG)
            p = jnp.exp(s - jnp.tile(lse_st, (1, 2)))
            dv_ref[win, :] = lax.dot_general(p.astype(BF16), dos, TN, preferred_element_type=F32) + dv_ref[win, :]
            dp = lax.dot_general(dos, vw, NT, preferred_element_type=F32)
            ds = (p * (dp - delta) * ATTN_SCALE).astype(BF16)
            dq_ref[0, rows, :] = _unstack_heads(lo, jnp.dot(ds, kw, preferred_element_type=F32)).astype(BF16)
            dk_ref[win, :] = lax.dot_general(ds, qs, TN, preferred_element_type=F32) + dk_ref[win, :]
            if has_sink:
                sink_acc = sink_acc - jnp.exp(sk - lse_st) * delta

        @pl.when((g == kdiv - 1) & (j == sp // tq - 1))
        def _():
            dk_out[0] = dk_ref[...].astype(BF16)
            dv_out[0] = dv_ref[...].astype(BF16)

        if has_sink:
            dsink_ref = refs[9]

            @pl.when(j == 0)
            def _():
                dsink_ref[...] = jnp.zeros_like(dsink_ref)

            dsink_ref[0] += jnp.where(lo[0:1], jnp.sum(sink_acc[:BLOCK], axis=0, keepdims=True),
                                      jnp.sum(sink_acc[BLOCK:], axis=0, keepdims=True))

    def qmap(r, pk, g, j):
        return (r, j, pk * kdiv + g)

    def kmap(r, pk, g, j):
        return (r, 0, pk)

    qspec = pl.BlockSpec((1, tq, LANES), qmap)
    kspec = pl.BlockSpec((1, sp, LANES), kmap)
    in_specs = [qspec, kspec, kspec, qspec, qspec, qspec]
    operands = [q, k, v, do, oo, lse]
    out_specs = [qspec, kspec, kspec]
    out_shape = [jax.ShapeDtypeStruct((d, sp, wq), BF16), jax.ShapeDtypeStruct((d, sp, wk), BF16),
                 jax.ShapeDtypeStruct((d, sp, wk), BF16)]
    if has_sink:
        in_specs = [pl.BlockSpec(memory_space=pltpu.SMEM)] + in_specs
        operands = [sinks] + operands
        out_specs.append(pl.BlockSpec((1, 1, LANES), lambda r, pk, g, j: (pk * kdiv + g, 0, 0)))
        out_shape.append(jax.ShapeDtypeStruct((nq, 1, LANES), F32))
    nsteps = sp // tq
    return pl.pallas_call(
        body, name=name, grid=(d, nk, kdiv, nsteps), in_specs=in_specs, out_specs=out_specs, out_shape=out_shape,
        scratch_shapes=[pltpu.VMEM((sp, LANES), F32), pltpu.VMEM((sp, LANES), F32)],
        compiler_params=_params("parallel", "parallel", "arbitrary", "arbitrary"),
    )(*operands)


def _combine(outs, lses, name):
    s = outs[0].shape[1]
    tm = 512
    nb = len(DILATIONS)
    ds = [d for d in DILATIONS if d > 1]

    def body(*refs):
        o_refs, l_refs = refs[:nb], refs[nb:2 * nb]
        cb_ref, c_ref, lse_ref = refs[2 * nb:2 * nb + 3]
        folded = refs[2 * nb + 3:2 * nb + 3 + 2 * len(ds)]
        scratch = refs[2 * nb + 3 + 2 * len(ds):]
        so = {1: None}
        sl = {1: None}
        for i, d in enumerate(ds):
            so[d], sl[d] = scratch[2 * i], scratch[2 * i + 1]
            _unfold_load(o_refs[1 + i], so[d], d)
            _unfold_load(l_refs[1 + i], sl[d], d)
        for p in range(N_PAIRS):
            pb = _pair_block(p)
            ls = [l_refs[0][0, :, pb]] + [sl[d][p] for d in ds]
            os_ = [o_refs[0][0, :, pb].astype(F32)] + [so[d][p] for d in ds]
            m = ls[0]
            for t in ls[1:]:
                m = jnp.maximum(m, t)
            ws = [jnp.exp(t - m) for t in ls]
            tot = ws[0]
            for t in ws[1:]:
                tot = tot + t
            acc = ws[0] * os_[0]
            for w, o in zip(ws[1:], os_[1:]):
                acc = acc + w * o
            cmix = acc / tot
            lse = m + jnp.log(tot)
            cb_ref[:, pb] = cmix.astype(BF16)
            c_ref[0, :, pb] = cmix
            lse_ref[0, :, pb] = lse
            so[ds[0]][p] = cmix
            sl[ds[0]][p] = lse
        for i, d in enumerate(ds):
            for r in range(d):
                for p in range(N_PAIRS):
                    rows = pl.ds(r, tm // d, stride=d)
                    folded[2 * i][r, :, _pair_block(p)] = so[ds[0]][p, rows, :]
                    folded[2 * i + 1][r, :, _pair_block(p)] = sl[ds[0]][p, rows, :]

    in_specs = [_folded_spec(d, tm) for _ in range(2) for d in DILATIONS]
    out_specs = [pl.BlockSpec((tm, ATTN_W), lambda i: (i, 0)), _folded_spec(1, tm), _folded_spec(1, tm)]
    out_shape = [jax.ShapeDtypeStruct((s, ATTN_W), BF16), _folded_shape(s, 1, F32), _folded_shape(s, 1, F32)]
    for d in ds:
        out_specs += [_folded_spec(d, tm)] * 2
        out_shape += [_folded_shape(s, d, F32)] * 2
    return pl.pallas_call(
        body, name=name, grid=(s // tm,), in_specs=in_specs, out_specs=out_specs, out_shape=out_shape,
        scratch_shapes=[pltpu.VMEM((N_PAIRS, tm, LANES), F32)] * (2 * len(ds)),
        compiler_params=_params("parallel"),
    )(*outs, *lses)


GLU_A = slice(768, 1280)
GLU_B = slice(1280, 1792)
EVEN_IN = 1792
ODD_IN = 2560
CONV_CH = 512


def _shifted_copies(xs_ref):
    rows = xs_ref.shape[1] - 8
    for b in range(1, 8):
        xs_ref[b, 0:rows, :] = xs_ref[0, pl.ds(b, rows), :]


def _shifted_rows(xs_ref, start):
    return xs_ref[start % 8, pl.ds(start - start % 8, CONV_ROWS), :]


def _glu(p_ref):
    return p_ref[:, GLU_A].astype(F32) * _sigmoid(p_ref[:, GLU_B].astype(F32))


def _conv_fwd(proj, w, b, ln_g, ln_b, name):
    s = proj.shape[0]
    tm = 512
    nh = tm // CONV_HALO
    lead = CONV_HALO - (CONV_WIDTH - 1)

    def body(p_ref, ph_ref, w_ref, b_ref, g_ref, bb_ref, y_ref, o_ref, xs_ref):
        xs_ref[0, CONV_HALO:, :] = _glu(p_ref)
        xs_ref[0, 0:CONV_HALO, :] = jnp.where(pl.program_id(0) > 0, _glu(ph_ref), 0.0)
        _shifted_copies(xs_ref)
        for c0 in range(0, tm, CONV_ROWS):
            acc = jnp.zeros((CONV_ROWS, CONV_CH), F32) + b_ref[...]
            for j in range(CONV_WIDTH):
                acc = acc + _shifted_rows(xs_ref, lead + j + c0) * w_ref[j:j + 1, :]
            y_ref[c0:c0 + CONV_ROWS, :] = acc
            mu = jnp.mean(acc, axis=-1, keepdims=True)
            xc = acc - mu
            var = jnp.mean(xc * xc, axis=-1, keepdims=True)
            zz = xc * lax.rsqrt(var + LN_EPS) * g_ref[...] + bb_ref[...]
            o_ref[c0:c0 + CONV_ROWS, :] = (zz * _sigmoid(zz)).astype(BF16)

    def const(a):
        return pl.BlockSpec(a.shape, lambda i: (0, 0))

    return pl.pallas_call(
        body, name=name, grid=(s // tm,),
        in_specs=[pl.BlockSpec((tm, EVEN_IN), lambda i: (i, 0)),
                  pl.BlockSpec((CONV_HALO, EVEN_IN), lambda i: (jnp.maximum(i * nh - 1, 0), 0)),
                  const(w), const(b), const(ln_g), const(ln_b)],
        out_specs=[pl.BlockSpec((tm, CONV_CH), lambda i: (i, 0)), pl.BlockSpec((tm, CONV_CH), lambda i: (i, 0))],
        out_shape=[jax.ShapeDtypeStruct((s, CONV_CH), F32), jax.ShapeDtypeStruct((s, CONV_CH), BF16)],
        scratch_shapes=[pltpu.VMEM((8, tm + CONV_HALO, CONV_CH), F32)],
        compiler_params=_params("arbitrary"),
    )(proj, proj, w, b, ln_g, ln_b)


def _conv_tail_bwd(dmix, yconv, ln_g, ln_b, name):
    def body(d_ref, y_ref, g_ref, b_ref, dy_ref, dg_ref, db_ref, dcb_ref):
        @pl.when(_first_step())
        def _():
            dg_ref[...] = jnp.zeros_like(dg_ref)
            db_ref[...] = jnp.zeros_like(db_ref)
            dcb_ref[...] = jnp.zeros_like(dcb_ref)

        y = y_ref[...]
        g = g_ref[...]
        mu = jnp.mean(y, axis=-1, keepdims=True)
        xc = y - mu
        rstd = lax.rsqrt(jnp.mean(xc * xc, axis=-1, keepdims=True) + LN_EPS)
        xh = xc * rstd
        zz = xh * g + b_ref[...]
        sg = _sigmoid(zz)
        dzz = d_ref[:, CONV_CH:] * sg * (1.0 + zz * (1.0 - sg))
        dg_ref[...] += jnp.sum(dzz * xh, axis=0, keepdims=True)
        db_ref[...] += jnp.sum(dzz, axis=0, keepdims=True)
        dxh = dzz * g
        dy = rstd * (dxh - jnp.mean(dxh, axis=-1, keepdims=True) - xh * jnp.mean(dxh * xh, axis=-1, keepdims=True))
        dcb_ref[...] += jnp.sum(dy, axis=0, keepdims=True)
        dy_ref[...] = dy

    vec = ((1, CONV_CH), F32)
    return _rows(body, name, 512, [dmix, yconv], [ln_g, ln_b], [(CONV_CH, F32)], [vec, vec, vec])


def _conv_bwd(proj, dy, w, name):
    s = proj.shape[0]
    tm = 512
    nh = tm // CONV_HALO
    nsteps = s // tm
    lead = CONV_HALO - (CONV_WIDTH - 1)

    def body(p_ref, ph_ref, dy_ref, dyn_ref, w_ref, dglu_ref, dw_ref, xf_ref, dyf_ref, part_ref):
        i = pl.program_id(0)

        @pl.when(i == 0)
        def _():
            dw_ref[...] = jnp.zeros_like(dw_ref)

        ga = p_ref[:, GLU_A].astype(F32)
        sgb = _sigmoid(p_ref[:, GLU_B].astype(F32))
        xf_ref[0, CONV_HALO:, :] = ga * sgb
        xf_ref[0, 0:CONV_HALO, :] = jnp.where(i > 0, _glu(ph_ref), 0.0)
        _shifted_copies(xf_ref)
        dyf_ref[0, 0:tm, :] = dy_ref[...]
        dyf_ref[0, tm:, :] = jnp.where(i < nsteps - 1, dyn_ref[...], 0.0)
        _shifted_copies(dyf_ref)
        for c0 in range(0, tm, CONV_ROWS):
            rows = slice(c0, c0 + CONV_ROWS)
            acc = jnp.zeros((CONV_ROWS, CONV_CH), F32)
            for j in range(CONV_WIDTH):
                acc = acc + _shifted_rows(dyf_ref, CONV_WIDTH - 1 - j + c0) * w_ref[j:j + 1, :]
            a_c, s_c = ga[rows, :], sgb[rows, :]
            dglu_ref[rows, 0:CONV_CH] = (acc * s_c).astype(BF16)
            dglu_ref[rows, CONV_CH:] = (acc * a_c * s_c * (1.0 - s_c)).astype(BF16)
        for c0 in range(0, tm, CONV_ROWS):
            dy_c = dy_ref[c0:c0 + CONV_ROWS, :]
            for j in range(CONV_WIDTH):
                prod = dy_c * _shifted_rows(xf_ref, lead + j + c0)
                part = jnp.sum(prod.reshape(CONV_ROWS // 8, 8, CONV_CH), axis=0)
                part_ref[j] = part if c0 == 0 else part + part_ref[j]
        for j in range(CONV_WIDTH):
            dw_ref[j:j + 1, :] += jnp.sum(part_ref[j], axis=0, keepdims=True)

    return pl.pallas_call(
        body, name=name, grid=(nsteps,),
        in_specs=[pl.BlockSpec((tm, EVEN_IN), lambda i: (i, 0)),
                  pl.BlockSpec((CONV_HALO, EVEN_IN), lambda i: (jnp.maximum(i * nh - 1, 0), 0)),
                  pl.BlockSpec((tm, CONV_CH), lambda i: (i, 0)),
                  pl.BlockSpec((CONV_HALO, CONV_CH), lambda i: (jnp.minimum((i + 1) * nh, s // CONV_HALO - 1), 0)),
                  pl.BlockSpec(w.shape, lambda i: (0, 0))],
        out_specs=[pl.BlockSpec((tm, 2 * CONV_CH), lambda i: (i, 0)), pl.BlockSpec(w.shape, lambda i: (0, 0))],
        out_shape=[jax.ShapeDtypeStruct((s, 2 * CONV_CH), BF16), jax.ShapeDtypeStruct(w.shape, F32)],
        scratch_shapes=[pltpu.VMEM((8, tm + CONV_HALO, CONV_CH), F32), pltpu.VMEM((8, tm + CONV_HALO, CONV_CH), F32),
                        pltpu.VMEM((CONV_WIDTH, 8, CONV_CH), F32)],
        compiler_params=_params("arbitrary"),
    )(proj, proj, dy, dy, w)


GATE_Z = slice(1536, 2560)
D_CH = 512
GELU_C = math.sqrt(2.0 / math.pi)
GELU_K = 0.044715


def _gelu_parts(z):
    t = jnp.tanh(GELU_C * (z + GELU_K * z * z * z))
    return 0.5 * z * (1.0 + t), t


def _lane_group(rows):
    return lax.broadcasted_iota(jnp.int32, (rows, D_CH), 1) // HEAD_DIM


def _tril_mask():
    return lax.broadcasted_iota(jnp.int32, (BLOCK, BLOCK), 0) >= lax.broadcasted_iota(jnp.int32, (BLOCK, BLOCK), 1)


def _layer_norm_parts(x):
    mu = jnp.mean(x, axis=-1, keepdims=True)
    xc = x - mu
    rstd = lax.rsqrt(jnp.mean(xc * xc, axis=-1, keepdims=True) + LN_EPS)
    return xc * rstd, rstd


def _gate_fwd(proj, ln_g, ln_b, w_sp, sb_t, name):
    tm = 512

    def body(p_ref, g_ref, b_ref, w_ref, sb_ref, mixed_ref, out_ref):
        zz, _ = _gelu_parts(p_ref[:, GATE_Z].astype(F32))
        u = zz[:, :D_CH]
        xh, _ = _layer_norm_parts(zz[:, D_CH:])
        gn = (xh * g_ref[...] + b_ref[...]).astype(BF16)
        grp = _lane_group(BLOCK)
        tri = _tril_mask()
        ws = [jnp.where(tri, w_ref[gi], 0.0).astype(BF16) for gi in range(N_GROUPS)]
        bias = jnp.zeros((BLOCK, D_CH), F32)
        for gi in range(N_GROUPS):
            bias = jnp.where(grp == gi, sb_ref[:, gi:gi + 1], bias)
        for ch in range(tm // BLOCK):
            rows = slice(ch * BLOCK, (ch + 1) * BLOCK)
            gc = gn[rows, :]
            mixed = bias
            for gi in range(N_GROUPS):
                r = jnp.dot(ws[gi], gc, preferred_element_type=F32)
                mixed = jnp.where(grp == gi, r + bias, mixed)
            mixed_ref[rows, :] = mixed
            out_ref[rows, :] = (u[rows, :] * mixed).astype(BF16)

    return _rows(body, name, tm, [proj], [ln_g, ln_b, w_sp, sb_t], [(D_CH, F32), (D_CH, BF16)])


def _gate_bwd(dmix, proj, mixed, ln_g, ln_b, w_sp, name):
    tm = 512

    def body(d_ref, p_ref, m_ref, g_ref, b_ref, w_ref, dz_ref, dg_ref, db_ref, dw_ref, dsb_ref, dgn_ref):
        @pl.when(_first_step())
        def _():
            dg_ref[...] = jnp.zeros_like(dg_ref)
            db_ref[...] = jnp.zeros_like(db_ref)
            dw_ref[...] = jnp.zeros_like(dw_ref)
            dsb_ref[...] = jnp.zeros_like(dsb_ref)

        z = p_ref[:, GATE_Z].astype(F32)
        zz, t = _gelu_parts(z)
        u = zz[:, :D_CH]
        xh, rstd = _layer_norm_parts(zz[:, D_CH:])
        g = g_ref[...]
        gn = (xh * g + b_ref[...]).astype(BF16)
        dd = d_ref[:, D_CH:]
        du = dd * m_ref[...]
        dm = dd * u
        grp = _lane_group(BLOCK)
        tri = _tril_mask()
        ws = [jnp.where(tri, w_ref[gi], 0.0).astype(BF16) for gi in range(N_GROUPS)]
        gsel = (lax.broadcasted_iota(jnp.int32, (N_GROUPS, D_CH), 1) // HEAD_DIM
                == lax.broadcasted_iota(jnp.int32, (N_GROUPS, D_CH), 0)).astype(F32)
        for ch in range(tm // BLOCK):
            rows = slice(ch * BLOCK, (ch + 1) * BLOCK)
            dmc = dm[rows, :]
            dmb = dmc.astype(BF16)
            gc = gn[rows, :]
            dgn = jnp.zeros((BLOCK, D_CH), F32)
            for gi in range(N_GROUPS):
                r = lax.dot_general(ws[gi], dmb, TN, preferred_element_type=F32)
                dgn = jnp.where(grp == gi, r, dgn)
                dmg = jnp.where(grp == gi, dmb, jnp.zeros_like(dmb))
                dwg = lax.dot_general(dmg, gc, NT, preferred_element_type=F32)
                dw_ref[gi] += jnp.where(tri, dwg, 0.0)
            dsb_ref[...] += lax.dot_general(gsel, dmc, NT, preferred_element_type=F32, precision=lax.Precision.HIGHEST)
            dgn_ref[rows, :] = dgn
        dgn = dgn_ref[...]
        db_ref[...] += jnp.sum(dgn, axis=0, keepdims=True)
        dg_ref[...] += jnp.sum(dgn * xh, axis=0, keepdims=True)
        dxh = dgn * g
        dgp = rstd * (dxh - jnp.mean(dxh, axis=-1, keepdims=True) - xh * jnp.mean(dxh * xh, axis=-1, keepdims=True))
        dgelu = 0.5 * (1.0 + t) + 0.5 * z * (1.0 - t * t) * GELU_C * (1.0 + 3.0 * GELU_K * z * z)
        dz_ref[:, 0:D_CH] = (du * dgelu[:, :D_CH]).astype(BF16)
        dz_ref[:, D_CH:] = (dgp * dgelu[:, D_CH:]).astype(BF16)

    s = proj.shape[0]
    tiled = [dmix, proj, mixed]
    consts = [ln_g, ln_b, w_sp]
    in_specs = [pl.BlockSpec((tm, a.shape[1]), lambda i: (i, 0)) for a in tiled]
    in_specs += [pl.BlockSpec(a.shape, lambda i, nd=a.ndim: (0,) * nd) for a in consts]
    vec = (1, D_CH)
    acc_shapes = [vec, vec, w_sp.shape, (N_GROUPS, BLOCK)]
    return pl.pallas_call(
        body, name=name, grid=(s // tm,), in_specs=in_specs,
        out_specs=[pl.BlockSpec((tm, 2 * D_CH), lambda i: (i, 0))]
        + [pl.BlockSpec(sh, lambda i, nd=len(sh): (0,) * nd) for sh in acc_shapes],
        out_shape=[jax.ShapeDtypeStruct((s, 2 * D_CH), BF16)] + [jax.ShapeDtypeStruct(sh, F32) for sh in acc_shapes],
        scratch_shapes=[pltpu.VMEM((tm, D_CH), F32)],
        compiler_params=_params("arbitrary"),
    )(*tiled, *consts)


def _adam_update(w, g, m, v):
    nm = ADAM_B1 * m + (1.0 - ADAM_B1) * g
    nv = ADAM_B2 * v + (1.0 - ADAM_B2) * (g * g)
    m_hat = nm / (1.0 - ADAM_B1 ** ADAM_STEP)
    v_hat = nv / (1.0 - ADAM_B2 ** ADAM_STEP)
    return -ADAM_LR * (m_hat / (jnp.sqrt(v_hat) + ADAM_EPS) + ADAM_WD * w), nm, nv


def _adamw(w, g, m, v, name):
    rows, cols = w.shape
    tm = _tile(rows, 512, 8) if rows % 8 == 0 else rows

    def body(w_ref, g_ref, m_ref, v_ref, d_ref, nm_ref, nv_ref):
        d_ref[...], nm_ref[...], nv_ref[...] = _adam_update(w_ref[...], g_ref[...], m_ref[...], v_ref[...])

    return _rows(body, name, tm, [w, g, m, v], [], [(cols, F32)] * 3)


def _ordered_sum(parts, name):
    n, rows, cols = parts.shape
    tm = _tile(rows, 512, 16 if parts.dtype == BF16 else 8)

    def body(p_ref, o_ref):
        acc = p_ref[0].astype(F32)
        for k in range(1, n):
            acc = acc + p_ref[k].astype(F32)
        o_ref[...] = acc

    return pl.pallas_call(body, name=name, grid=(rows // tm,),
                          in_specs=[pl.BlockSpec((n, tm, cols), lambda i: (0, i, 0))],
                          out_specs=pl.BlockSpec((tm, cols), lambda i: (i, 0)),
                          out_shape=jax.ShapeDtypeStruct((rows, cols), F32), compiler_params=_params("parallel"))(parts)


ANY = pl.BlockSpec(memory_space=pl.ANY)


def _position():
    x, y, c = lax.axis_index("x"), lax.axis_index("y"), lax.axis_index("c")
    other_chips = [(1 - x, y), (x, 1 - y), (1 - x, 1 - y)]
    return x, y, c, other_chips


def _remote(src, dst, send_sem, recv_sem, to):
    return pltpu.make_async_remote_copy(src_ref=src, dst_ref=dst, send_sem=send_sem, recv_sem=recv_sem,
                                        device_id=to, device_id_type=MESH)


STAGE_ROWS = 736


def _staged_copies(copies, buf, in_sems, out_sems):
    n = len(copies)

    def into(u):
        src = copies[u][0]
        return pltpu.make_async_copy(src, buf.at[u % 2, pl.ds(0, src.shape[0]), :], in_sems.at[u % 2])

    def out_of(u):
        dst = copies[u][1]
        return pltpu.make_async_copy(buf.at[u % 2, pl.ds(0, dst.shape[0]), :], dst, out_sems.at[u % 2])

    into(0).start()
    for u in range(n):
        into(u).wait()
        out_of(u).start()
        if u + 1 < n:
            if u >= 1:
                out_of(u - 1).wait()
            into(u + 1).start()
    if n >= 2:
        out_of(n - 2).wait()
    out_of(n - 1).wait()


def _stage_scratch(dtype, cols):
    return [pltpu.VMEM((2, STAGE_ROWS, cols), dtype), pltpu.SemaphoreType.DMA((2,)), pltpu.SemaphoreType.DMA((2,))]


def _row_chunks(rows):
    return [(r, min(STAGE_ROWS, rows - r)) for r in range(0, rows, STAGE_ROWS)]


def _gather_chips(shard, name):
    rows, cols = shard.shape
    half = rows // 2

    def body(in_ref, out_ref, send_sems, recv_sems, buf, in_sems, out_sems):
        x, y, c, chips = _position()
        me = 2 * x + y
        sibling = (x, y, 1 - c)

        def slab(chip, h):
            return out_ref.at[chip, pl.ds(h * half, half), :]

        first = [_remote(in_ref.at[pl.ds(c * half, half), :], slab(me, c), send_sems.at[j], recv_sems.at[j], (cx, cy, c))
                 for j, (cx, cy) in enumerate(chips)]
        for cp in first:
            cp.start()
        _staged_copies([(in_ref.at[pl.ds(r, n), :], out_ref.at[me, pl.ds(r, n), :]) for r, n in _row_chunks(rows)],
                       buf, in_sems, out_sems)
        passed = []
        for j, (cx, cy) in enumerate(chips):
            got = slab(2 * cx + cy, c)
            _remote(got, got, send_sems.at[j], recv_sems.at[j], sibling).wait_recv()
            cp = _remote(got, got, send_sems.at[3 + j], recv_sems.at[3 + j], sibling)
            cp.start()
            passed.append(cp)
        for j, (cx, cy) in enumerate(chips):
            got = slab(2 * cx + cy, 1 - c)
            _remote(got, got, send_sems.at[3 + j], recv_sems.at[3 + j], sibling).wait_recv()
        for cp in first + passed:
            cp.wait_send()

    return pl.pallas_call(
        body, name=name, in_specs=[ANY], out_specs=ANY,
        out_shape=jax.ShapeDtypeStruct((N_CHIPS, rows, cols), shard.dtype),
        scratch_shapes=[pltpu.SemaphoreType.DMA((6,)), pltpu.SemaphoreType.DMA((6,))] + _stage_scratch(shard.dtype, cols),
        compiler_params=pltpu.CompilerParams(vmem_limit_bytes=VMEM_LIMIT),
    )(shard)


HBM = pl.BlockSpec(memory_space=pltpu.HBM)
SEM = pl.BlockSpec(memory_space=pltpu.SEMAPHORE)
SIDE_EFFECT = pltpu.SideEffectType.DATAFLOW_SIDE_EFFECTING


def _ici_copies(in_ref, land_ref, send_sems, recv_sems, half):
    x, y, c, chips = _position()
    mine = pl.ds(c * half, half)
    sends = [_remote(in_ref.at[mine, :], land_ref.at[2 * x + y, mine, :], send_sems.at[j], recv_sems.at[j], (cx, cy, c))
             for j, (cx, cy) in enumerate(chips)]
    arrivals = [_remote(in_ref.at[mine, :], land_ref.at[2 * cx + cy, mine, :], send_sems.at[j], recv_sems.at[j], (cx, cy, c))
                for j, (cx, cy) in enumerate(chips)]
    return sends, arrivals


def _gather_start(shard, after, name):
    rows, cols = shard.shape

    def body(in_ref, land_ref, after_ref, send_sems, recv_sems, in_thru, land_thru, token):
        sends, _ = _ici_copies(in_ref, land_ref, send_sems, recv_sems, rows // 2)
        for cp in sends:
            cp.start()
        token[...] = jnp.zeros_like(token)

    land = lax.empty((N_CHIPS, rows, cols), shard.dtype)
    return pl.pallas_call(
        body, name=name,
        out_shape=(pltpu.SemaphoreType.DMA((3,)), pltpu.SemaphoreType.DMA((3,)), pltpu.HBM(shard.shape, shard.dtype),
                   pltpu.HBM(land.shape, land.dtype), jax.ShapeDtypeStruct((8, LANES), F32)),
        in_specs=(HBM, HBM, ANY), out_specs=(SEM, SEM, HBM, HBM, pl.BlockSpec(memory_space=pltpu.VMEM)),
        input_output_aliases={0: 2, 1: 3},
        compiler_params=pltpu.CompilerParams(has_side_effects=SIDE_EFFECT),
    )(pltpu.with_memory_space_constraint(shard, pltpu.HBM), pltpu.with_memory_space_constraint(land, pltpu.HBM), after)


def _gather_wait(send_sems, recv_sems, shard, land, after, name):
    rows = shard.shape[0]

    def body(in_ref, land_ref, send_sems, recv_sems, after_ref, in_out, land_out):
        sends, arrivals = _ici_copies(in_ref, land_ref, send_sems, recv_sems, rows // 2)
        for cp in sends:
            cp.wait_send()
        for cp in arrivals:
            cp.wait_recv()

    return pl.pallas_call(
        body, name=name, out_shape=(pltpu.HBM(shard.shape, shard.dtype), pltpu.HBM(land.shape, land.dtype)),
        in_specs=(HBM, HBM, SEM, SEM, ANY), out_specs=(HBM, HBM), input_output_aliases={0: 0, 1: 1},
        compiler_params=pltpu.CompilerParams(has_side_effects=SIDE_EFFECT),
    )(shard, land, send_sems, recv_sems, after)


def _gather_finish(shard, land, name):
    rows, cols = shard.shape
    half = rows // 2

    def body(in_ref, land_ref, out_ref, send_sems, recv_sems, buf, in_sems, out_sems):
        x, y, c, chips = _position()
        me = 2 * x + y
        sibling = (x, y, 1 - c)

        def slab(chip, h):
            return out_ref.at[chip, pl.ds(h * half, half), :]

        passed = [_remote(slab(2 * cx + cy, c), slab(2 * cx + cy, c), send_sems.at[j], recv_sems.at[j], sibling)
                  for j, (cx, cy) in enumerate(chips)]
        for cp in passed:
            cp.start()
        _staged_copies([(in_ref.at[pl.ds(r, n), :], out_ref.at[me, pl.ds(r, n), :]) for r, n in _row_chunks(rows)],
                       buf, in_sems, out_sems)
        for j, (cx, cy) in enumerate(chips):
            got = slab(2 * cx + cy, 1 - c)
            _remote(got, got, send_sems.at[j], recv_sems.at[j], sibling).wait_recv()
        for cp in passed:
            cp.wait_send()

    return pl.pallas_call(
        body, name=name, in_specs=[ANY, ANY], out_specs=ANY, out_shape=jax.ShapeDtypeStruct(land.shape, land.dtype),
        input_output_aliases={1: 0},
        scratch_shapes=[pltpu.SemaphoreType.DMA((3,)), pltpu.SemaphoreType.DMA((3,))] + _stage_scratch(shard.dtype, cols),
        compiler_params=pltpu.CompilerParams(vmem_limit_bytes=VMEM_LIMIT),
    )(shard, land)


def _gather_devices(block, name):
    rows, cols = block.shape

    def body(in_ref, out_ref, send_sems, recv_sems, local_sem):
        x, y, c, chips = _position()
        sibling = (x, y, 1 - c)

        def slot(px, py, pc):
            return out_ref.at[4 * px + 2 * py + pc]

        mine = pltpu.make_async_copy(in_ref, slot(x, y, c), local_sem)
        mine.start()
        first = [_remote(in_ref, slot(x, y, c), send_sems.at[0], recv_sems.at[0], sibling)]
        first += [_remote(in_ref, slot(x, y, c), send_sems.at[1 + j], recv_sems.at[1 + j], (cx, cy, c))
                  for j, (cx, cy) in enumerate(chips)]
        for cp in first:
            cp.start()
        passed = []
        for j, (cx, cy) in enumerate(chips):
            got = slot(cx, cy, c)
            _remote(got, got, send_sems.at[1 + j], recv_sems.at[1 + j], sibling).wait_recv()
            cp = _remote(got, got, send_sems.at[4 + j], recv_sems.at[4 + j], sibling)
            cp.start()
            passed.append(cp)
        got = slot(x, y, 1 - c)
        _remote(got, got, send_sems.at[0], recv_sems.at[0], sibling).wait_recv()
        for j, (cx, cy) in enumerate(chips):
            got = slot(cx, cy, 1 - c)
            _remote(got, got, send_sems.at[4 + j], recv_sems.at[4 + j], sibling).wait_recv()
        for cp in first + passed:
            cp.wait_send()
        mine.wait()

    return pl.pallas_call(
        body, name=name, in_specs=[ANY], out_specs=ANY,
        out_shape=jax.ShapeDtypeStruct((N_DEV, rows, cols), block.dtype),
        scratch_shapes=[pltpu.SemaphoreType.DMA((7,)), pltpu.SemaphoreType.DMA((7,)), pltpu.SemaphoreType.DMA],
    )(block)


def _pair_send(grads, name):
    n = len(grads)
    hs = [g.shape[2] for g in grads]
    offs = [sum(hs[:i]) for i in range(n)]
    cols = grads[0].shape[3]

    def body(*refs):
        g_refs = refs[:n]
        got_ref, send_sems, recv_sems = refs[n:]
        x, y, c, _ = _position()
        copies = [_remote(g_ref.at[:, 1 - c], got_ref.at[:, pl.ds(offs[i], hs[i]), :], send_sems.at[i], recv_sems.at[i],
                          (x, y, 1 - c)) for i, g_ref in enumerate(g_refs)]
        for cp in copies:
            cp.start()
        for cp in copies:
            cp.wait()

    return pl.pallas_call(
        body, name=name, in_specs=[ANY] * n, out_specs=ANY, out_shape=jax.ShapeDtypeStruct((N_CHIPS, sum(hs), cols), F32),
        scratch_shapes=[pltpu.SemaphoreType.DMA((n,)), pltpu.SemaphoreType.DMA((n,))],
    )(*grads)


def _pair_copies(g_refs, land_ref, send_sems, recv_sems):
    x, y, c, _ = _position()
    hs = [g.shape[2] for g in g_refs]
    offs = [sum(hs[:i]) for i in range(len(hs))]
    return [_remote(g_ref.at[:, 1 - c], land_ref.at[:, pl.ds(offs[i], hs[i]), :], send_sems.at[i], recv_sems.at[i],
                    (x, y, 1 - c)) for i, g_ref in enumerate(g_refs)]


def _pair_send_start(grads, name):
    n = len(grads)
    land = lax.empty((N_CHIPS, sum(g.shape[2] for g in grads), grads[0].shape[3]), F32)

    def body(*refs):
        for cp in _pair_copies(refs[:n], refs[n], refs[n + 1], refs[n + 2]):
            cp.start()
        refs[-1][...] = jnp.zeros_like(refs[-1])

    buffers = [*grads, land]
    return pl.pallas_call(
        body, name=name,
        out_shape=(pltpu.SemaphoreType.DMA((n,)), pltpu.SemaphoreType.DMA((n,)),
                   *[pltpu.HBM(b.shape, b.dtype) for b in buffers], jax.ShapeDtypeStruct((8, LANES), F32)),
        in_specs=(HBM,) * (n + 1), out_specs=(SEM, SEM, *(HBM,) * (n + 1), pl.BlockSpec(memory_space=pltpu.VMEM)),
        input_output_aliases={i: 2 + i for i in range(n + 1)},
        compiler_params=pltpu.CompilerParams(has_side_effects=SIDE_EFFECT),
    )(*[pltpu.with_memory_space_constraint(b, pltpu.HBM) for b in buffers])


def _pair_send_wait(send_sems, recv_sems, buffers, after, name):
    n = len(buffers) - 1

    def body(*refs):
        for cp in _pair_copies(refs[:n], refs[n], refs[n + 1], refs[n + 2]):
            cp.wait_send()
            cp.wait_recv()

    return pl.pallas_call(
        body, name=name, out_shape=tuple(pltpu.HBM(b.shape, b.dtype) for b in buffers),
        in_specs=(*(HBM,) * (n + 1), SEM, SEM, ANY), out_specs=(HBM,) * (n + 1),
        input_output_aliases={i: i for i in range(n + 1)},
        compiler_params=pltpu.CompilerParams(has_side_effects=SIDE_EFFECT),
    )(*buffers, send_sems, recv_sems, after)


def _pair_add(grads, got, name):
    n = len(grads)
    hs = [g.shape[2] for g in grads]
    offs = [sum(hs[:i]) for i in range(n)]
    cols = grads[0].shape[3]
    hmax = max(hs)
    units = [(i, k) for k in range(N_CHIPS) for i in range(n)]

    def body(*refs):
        g_refs = refs[:n]
        got_ref, out_ref, a_buf, b_buf, o_buf, a_sems, b_sems, o_sems = refs[n:]
        c = lax.axis_index("c")

        def loads(u):
            i, k = units[u]
            slot, rows = u % 2, pl.ds(0, hs[i])
            return (pltpu.make_async_copy(g_refs[i].at[k, c], a_buf.at[slot, rows, :], a_sems.at[slot]),
                    pltpu.make_async_copy(got_ref.at[k, pl.ds(offs[i], hs[i]), :], b_buf.at[slot, rows, :], b_sems.at[slot]))

        def store(u):
            i, k = units[u]
            return pltpu.make_async_copy(o_buf.at[u % 2, pl.ds(0, hs[i]), :], out_ref.at[k, pl.ds(offs[i], hs[i]), :],
                                         o_sems.at[u % 2])

        for cp in loads(0):
            cp.start()
        for u, (i, k) in enumerate(units):
            if u + 1 < len(units):
                for cp in loads(u + 1):
                    cp.start()
            for cp in loads(u):
                cp.wait()
            if u >= 2:
                store(u - 2).wait()
            rows = pl.ds(0, hs[i])
            o_buf[u % 2, rows, :] = (a_buf[u % 2, rows, :] + b_buf[u % 2, rows, :]).astype(BF16)
            store(u).start()
        store(len(units) - 2).wait()
        store(len(units) - 1).wait()

    return pl.pallas_call(
        body, name=name, in_specs=[ANY] * (n + 1), out_specs=ANY,
        out_shape=jax.ShapeDtypeStruct((N_CHIPS, sum(hs), cols), BF16),
        scratch_shapes=[pltpu.VMEM((2, hmax, cols), F32), pltpu.VMEM((2, hmax, cols), F32), pltpu.VMEM((2, hmax, cols), BF16),
                        pltpu.SemaphoreType.DMA((2,)), pltpu.SemaphoreType.DMA((2,)), pltpu.SemaphoreType.DMA((2,))],
        compiler_params=pltpu.CompilerParams(vmem_limit_bytes=VMEM_LIMIT),
    )(*grads, got)


def _chip_exchange(parts, name):
    _, rows, cols = parts.shape

    def body(in_ref, out_ref, send_sems, recv_sems):
        x, y, c, chips = _position()
        sent = [_remote(in_ref.at[2 * cx + cy], out_ref.at[j], send_sems.at[j], recv_sems.at[j], (cx, cy, c))
                for j, (cx, cy) in enumerate(chips)]
        for cp in sent:
            cp.start()
        for cp in sent:
            cp.wait()

    return pl.pallas_call(
        body, name=name, in_specs=[ANY], out_specs=ANY, out_shape=jax.ShapeDtypeStruct((3, rows, cols), parts.dtype),
        scratch_shapes=[pltpu.SemaphoreType.DMA((3,)), pltpu.SemaphoreType.DMA((3,))],
    )(parts)


def _exchange_copies(in_ref, land_ref, send_sems, recv_sems):
    x, y, c, chips = _position()
    return [_remote(in_ref.at[2 * cx + cy], land_ref.at[j], send_sems.at[j], recv_sems.at[j], (cx, cy, c))
            for j, (cx, cy) in enumerate(chips)]


def _exchange_start(parts, name):
    _, rows, cols = parts.shape

    def body(in_ref, land_ref, send_sems, recv_sems, in_thru, land_thru, token):
        for cp in _exchange_copies(in_ref, land_ref, send_sems, recv_sems):
            cp.start()
        token[...] = jnp.zeros_like(token)

    land = lax.empty((3, rows, cols), parts.dtype)
    return pl.pallas_call(
        body, name=name,
        out_shape=(pltpu.SemaphoreType.DMA((3,)), pltpu.SemaphoreType.DMA((3,)), pltpu.HBM(parts.shape, parts.dtype),
                   pltpu.HBM(land.shape, land.dtype), jax.ShapeDtypeStruct((8, LANES), F32)),
        in_specs=(HBM, HBM), out_specs=(SEM, SEM, HBM, HBM, pl.BlockSpec(memory_space=pltpu.VMEM)),
        input_output_aliases={0: 2, 1: 3},
        compiler_params=pltpu.CompilerParams(has_side_effects=SIDE_EFFECT),
    )(pltpu.with_memory_space_constraint(parts, pltpu.HBM), pltpu.with_memory_space_constraint(land, pltpu.HBM))


def _exchange_wait(send_sems, recv_sems, parts, land, after, name):
    def body(in_ref, land_ref, send_sems, recv_sems, after_ref, in_out, land_out):
        for cp in _exchange_copies(in_ref, land_ref, send_sems, recv_sems):
            cp.wait_send()
            cp.wait_recv()

    return pl.pallas_call(
        body, name=name, out_shape=(pltpu.HBM(parts.shape, parts.dtype), pltpu.HBM(land.shape, land.dtype)),
        in_specs=(HBM, HBM, SEM, SEM, ANY), out_specs=(HBM, HBM), input_output_aliases={0: 0, 1: 1},
        compiler_params=pltpu.CompilerParams(has_side_effects=SIDE_EFFECT),
    )(parts, land, send_sems, recv_sems, after)


def _chip_sum(parts, recv, chip, name):
    _, rows, cols = parts.shape
    tm = _tile(rows, 512, 16)

    def body(chip_ref, own_ref, recv_ref, o_ref):
        acc = own_ref[0].astype(F32)
        for j in range(3):
            acc = acc + recv_ref[j].astype(F32)
        o_ref[...] = acc

    return pl.pallas_call(
        body, name=name,
        grid_spec=pltpu.PrefetchScalarGridSpec(
            num_scalar_prefetch=1, grid=(rows // tm,),
            in_specs=[pl.BlockSpec((1, tm, cols), lambda i, chip_ref: (chip_ref[0], i, 0)),
                      pl.BlockSpec((3, tm, cols), lambda i, chip_ref: (0, i, 0))],
            out_specs=pl.BlockSpec((tm, cols), lambda i, chip_ref: (i, 0))),
        out_shape=jax.ShapeDtypeStruct((rows, cols), F32), compiler_params=_params("parallel"),
    )(chip, parts, recv)


def _join_unpack(mine, hs, groups, name):
    n = len(hs)
    offs = [sum(hs[:i]) for i in range(n)]
    cols = mine.shape[1]
    n_out = max(groups) + 1
    base = [2 * sum(h for h, g in zip(hs[:i], groups[:i]) if g == groups[i]) for i in range(n)]
    out_rows = [2 * sum(h for h, g in zip(hs, groups) if g == k) for k in range(n_out)]

    def body(in_ref, *refs):
        outs = refs[:n_out]
        send_sems, recv_sems, buf, in_sems, out_sems = refs[n_out:]
        x, y, c, _ = _position()
        sibling = (x, y, 1 - c)
        sent, local = [], []
        for i in range(n):
            src = in_ref.at[pl.ds(offs[i], hs[i]), :]
            here = outs[groups[i]].at[pl.ds(base[i] + c * hs[i], hs[i]), :]
            cp = _remote(src, here, send_sems.at[i], recv_sems.at[i], sibling)
            cp.start()
            sent.append(cp)
            local.append((src, here))
        _staged_copies(local, buf, in_sems, out_sems)
        for i, cp in enumerate(sent):
            there = outs[groups[i]].at[pl.ds(base[i] + (1 - c) * hs[i], hs[i]), :]
            _remote(there, there, send_sems.at[i], recv_sems.at[i], sibling).wait_recv()
            cp.wait_send()

    assert max(hs) <= STAGE_ROWS
    return pl.pallas_call(
        body, name=name, in_specs=[ANY], out_specs=[ANY] * n_out,
        out_shape=[jax.ShapeDtypeStruct((r, cols), F32) for r in out_rows],
        scratch_shapes=[pltpu.SemaphoreType.DMA((n,)), pltpu.SemaphoreType.DMA((n,))] + _stage_scratch(F32, cols),
        compiler_params=pltpu.CompilerParams(vmem_limit_bytes=VMEM_LIMIT),
    )(mine)


SMALL_ROWS = 16
SMALL_PACK_ROWS = 256


def _small_rows(n):
    return -(-n // (SMALL_ROWS * LANES)) * SMALL_ROWS


def _pack_small(arrs):
    parts = []
    for a in arrs:
        flat = a.reshape(-1)
        rows = _small_rows(flat.shape[0])
        flat = jnp.pad(flat, (0, rows * LANES - flat.shape[0]))
        parts.append(flat.reshape(rows, LANES))
    total = sum(p.shape[0] for p in parts)
    parts.append(jnp.zeros((-total % SMALL_PACK_ROWS, LANES), F32))
    return jnp.concatenate(parts, axis=0)


def _unpack_small(packed, shapes):
    out, r = [], 0
    for sh in shapes:
        n = math.prod(sh)
        cnt = _small_rows(n)
        out.append(packed[r:r + cnt].reshape(-1)[:n].reshape(sh))
        r += cnt
    return out


def _ffn_bwd(dh, dhb, h_in, saved, g_norm, w_gate_t, w_up_t, w_down, tag, after=None):
    n, gate, up, act = saved
    dgate, dup = _ffn_dact(dhb, w_down, gate, up, f"{tag}_dact", after)
    dw_down = _matmul(act, dhb, trans_a=True, name=f"{tag}_dwdown")
    dw_gate_t = _matmul(dgate, n, trans_a=True, name=f"{tag}_dwgate")
    dw_up_t = _matmul(dup, n, trans_a=True, name=f"{tag}_dwup")
    dh_in, dh_inb, dg = _dn_norm([(dgate, w_gate_t), (dup, w_up_t)], h_in, g_norm, dh, f"{tag}_dnorm")
    return dh_in, dh_inb, dg, dw_gate_t, dw_up_t, dw_down


def _local_step(x, tgt, w, big, late_weights, reduce_send, reduce_exchange):
    s = x.shape[0]
    tabs = _rope_tables(s)
    grads, gbig = {}, {}

    g_ev = w['ev_norm_g']
    n1 = _rms_fwd(x, g_ev, "ev_norm")
    proj0 = _matmul(n1, big['ev_w_in', 0], trans_b=True, name="ev_in", out_dtype=BF16, rows_inner=True)
    q0, k0, v0 = _qkv_prep_even(proj0, tabs, "ev_qkv")
    sinks = w['ev_sinks'].reshape(-1)
    o0, lse0, o0b = _attn_fwd(q0, k0, v0, sinks, max_dist=BLOCK - 1, name="ev_attn", emit_bf16=True)
    yconv, cout = _conv_fwd(proj0, w['ev_conv_w'][0], w['ev_conv_b'], w['ev_conv_ln_g'], w['ev_conv_ln_b'], "ev_conv")
    mix0 = (o0b[0], cout)
    g_f0 = w['ffn_norm_g'][0:1]
    h1, n2 = _matmul_norm(mix0, big['ev_w_out', 0], x, g_f0, "ev_out")
    big = {**big, **late_weights(h1)}

    g_od = w['od_norm_g']
    act0, gate0, up0 = _ffn_gate_up(n2, big['ffn_w_gate', 0], big['ffn_w_up', 0], "ffn0_gate_up")
    h2, n3 = _matmul_norm(act0, big['ffn_w_down', 0], h1, g_od, "ffn0_down")
    ffn0 = (n2, gate0, up0, act0)

    proj1 = _matmul(n3, big['od_w_in', 0], trans_b=True, name="od_in", out_dtype=BF16, rows_inner=True)
    qkv = _qkv_prep_odd(proj1, tabs, "od_qkv")
    nb = len(DILATIONS)
    outs, lses = [], []
    for i, d in enumerate(DILATIONS):
        o_r, lse_r = _attn_fwd(qkv[i], qkv[nb + i], qkv[2 * nb + i], None, max_dist=BLOCK, name=f"od_attn{d}", o_dtype=BF16)
        outs.append(o_r)
        lses.append(lse_r)
    comb = _combine(outs, lses, "od_combine")
    c_bf16 = comb[0]
    c_fold = {1: comb[1]}
    lse_fold = {1: comb[2]}
    for i, d in enumerate(DILATIONS[1:]):
        c_fold[d], lse_fold[d] = comb[3 + 2 * i], comb[4 + 2 * i]
    w_sp = w['od_spatial_w'][0]
    sb_t = w['od_spatial_b'][0].T
    mixed, dout = _gate_fwd(proj1, w['od_sgu_ln_g'], w['od_sgu_ln_b'], w_sp, sb_t, "od_gate")
    mix1 = (c_bf16, dout)
    g_f1 = w['ffn_norm_g'][1:2]
    h3, n4 = _matmul_norm(mix1, big['od_w_out', 0], h2, g_f1, "od_out")
    act1, gate1, up1 = _ffn_gate_up(n4, big['ffn_w_gate', 1], big['ffn_w_up', 1], "ffn1_gate_up")
    ffn1 = (n4, gate1, up1, act1)

    dh4, dh4b, dg_final, loss_tile = _matmul_final(act1, big['ffn_w_down', 1], h3, w['final_norm_g'].reshape(1, D_MODEL),
                                                   tgt, "ffn1_down_loss")
    grads['final_norm_g'] = dg_final.reshape(D_MODEL)

    dh3, dh3b, dg_f1, gbig['ffn_w_gate', 1], gbig['ffn_w_up', 1], gbig['ffn_w_down', 1] = _ffn_bwd(
        dh4, dh4b, h3, ffn1, g_f1, big['ffn_w_gate', 1], big['ffn_w_up', 1], big['ffn_w_down', 1], "ffn1")

    dmix1 = _matmul(dh3b, big['od_w_out', 0], trans_b=True, name="od_dmix")
    gbig['od_w_out', 0] = _matmul_tn_pair(mix1[0], mix1[1], dh3b, "od_dwout")
    do_fold = dict(zip(DILATIONS[1:], _fold_dout(dmix1, "od_fold_dout")))
    do_fold[1] = dmix1[None]
    dqs, dks, dvs = [], [], []
    for i, d in enumerate(DILATIONS):
        dq_r, dk_r, dv_r = _attn_bwd(qkv[i], qkv[nb + i], qkv[2 * nb + i], do_fold[d], c_fold[d], lse_fold[d], None,
                                     max_dist=BLOCK, name=f"od_dattn{d}")
        dqs.append(dq_r)
        dks.append(dk_r)
        dvs.append(dv_r)
    dz, dg_sgu, db_sgu, dw_sp, dsb = _gate_bwd(dmix1, proj1, mixed, w['od_sgu_ln_g'], w['od_sgu_ln_b'], w_sp, "od_dgate")
    grads['od_sgu_ln_g'], grads['od_sgu_ln_b'] = dg_sgu, db_sgu
    grads['od_spatial_w'], grads['od_spatial_b'] = dw_sp[None], dsb[None]
    dproj1 = _qkv_post_odd(dqs, dks, dvs, dz, tabs, "od_dproj")
    gbig['od_w_in', 0] = _matmul(dproj1, n3, trans_a=True, name="od_dwin")
    dh2, dh2b, dg_od = _dn_norm([(dproj1, big['od_w_in', 0])], h2, g_od, dh3, "od_dnorm")
    grads['od_norm_g'] = dg_od
    token = reduce_send(0, gbig)

    dh1, dh1b, dg_f0, gbig['ffn_w_gate', 0], gbig['ffn_w_up', 0], gbig['ffn_w_down', 0] = _ffn_bwd(
        dh2, dh2b, h1, ffn0, g_f0, big['ffn_w_gate', 0], big['ffn_w_up', 0], big['ffn_w_down', 0], "ffn0", token)
    grads['ffn_norm_g'] = jnp.concatenate([dg_f0, dg_f1], axis=0)
    token = reduce_exchange(0, dh1) + reduce_send(1, gbig)

    dmix0 = _matmul(dh1b, big['ev_w_out', 0], trans_b=True, name="ev_dmix", after=token)
    gbig['ev_w_out', 0] = _matmul_tn_pair(mix0[0], mix0[1], dh1b, "ev_dwout")
    dq0, dk0, dv0, dsink = _attn_bwd(q0, k0, v0, dmix0[None], o0, lse0, sinks, max_dist=BLOCK - 1, name="ev_dattn")
    grads['ev_sinks'] = dsink[:, 0, :].reshape(N_PAIRS, 2, HEAD_DIM)[:, :, 0].reshape(1, 8)
    token = reduce_exchange(1, dq0)
    dyc, dg_cln, db_cln, dcb = _conv_tail_bwd(dmix0, yconv, w['ev_conv_ln_g'] + token[0:1, 0:1], w['ev_conv_ln_b'],
                                              "ev_dconv_tail")
    grads['ev_conv_ln_g'], grads['ev_conv_ln_b'], grads['ev_conv_b'] = dg_cln, db_cln, dcb
    dglu, dconv_w = _conv_bwd(proj0, dyc, w['ev_conv_w'][0], "ev_dconv")
    grads['ev_conv_w'] = dconv_w[None]
    dproj0 = _qkv_post_even(dq0, dk0, dv0, dglu, tabs, "ev_dproj")
    gbig['ev_w_in', 0] = _matmul(dproj0, n1, trans_a=True, name="ev_dwin")
    dx, _, dg_ev = _dn_norm([(dproj0, big['ev_w_in', 0])], x, g_ev, dh1, "ev_dnorm")
    grads['ev_norm_g'] = dg_ev
    return loss_tile, dx, grads, gbig


def _shard_rows(w, layer, by_cols):
    return w[layer].T if by_cols else w[layer]


def kernel(x, ev_norm_g, ev_w_in, ev_sinks, ev_conv_w, ev_conv_b, ev_conv_ln_g, ev_conv_ln_b, ev_w_out, od_norm_g, od_w_in, od_sgu_ln_g, od_sgu_ln_b, od_spatial_w, od_spatial_b, od_w_out, ffn_norm_g, ffn_w_gate, ffn_w_up, ffn_w_down, final_norm_g, loss_target, m_ev_norm_g, m_ev_w_in, m_ev_sinks, m_ev_conv_w, m_ev_conv_b, m_ev_conv_ln_g, m_ev_conv_ln_b, m_ev_w_out, m_od_norm_g, m_od_w_in, m_od_sgu_ln_g, m_od_sgu_ln_b, m_od_spatial_w, m_od_spatial_b, m_od_w_out, m_ffn_norm_g, m_ffn_w_gate, m_ffn_w_up, m_ffn_w_down, m_final_norm_g, v_ev_norm_g, v_ev_w_in, v_ev_sinks, v_ev_conv_w, v_ev_conv_b, v_ev_conv_ln_g, v_ev_conv_ln_b, v_ev_w_out, v_od_norm_g, v_od_w_in, v_od_sgu_ln_g, v_od_sgu_ln_b, v_od_spatial_w, v_od_spatial_b, v_od_w_out, v_ffn_norm_g, v_ffn_w_gate, v_ffn_w_up, v_ffn_w_down, v_final_norm_g):
    given = dict(locals())
    wts = {n: given[n] for n in WEIGHTS}
    mom = {n: given["m_" + n] for n in WEIGHTS}
    var = {n: given["v_" + n] for n in WEIGHTS}
    chip = 2 * lax.axis_index("x") + lax.axis_index("y")

    shard_rows = [_shard_rows(wts[n], layer, by_cols).astype(BF16) for n, layer, by_cols in BIG]
    counts = [a.shape[0] for a in shard_rows]
    n_first = sum(n.startswith('ev_') for n, _, _ in BIG)

    def unpack(stacked, entries, cnts):
        out, r = {}, 0
        for (n, layer, _), cnt in zip(entries, cnts):
            out[n, layer] = stacked[:, r:r + cnt].reshape(N_CHIPS * cnt, D_MODEL)
            r += cnt
        return out

    first_w = _gather_chips(jnp.concatenate(shard_rows[:n_first], axis=0), "gather_weights_ev")
    big = unpack(first_w, BIG[:n_first], counts[:n_first])
    send_sems, recv_sems, late_shard, late_land, token = _gather_start(jnp.concatenate(shard_rows[n_first:], axis=0),
                                                                      first_w, "gather_weights_start")

    def late_weights(after):
        shard, land = _gather_wait(send_sems, recv_sems, late_shard, late_land, after, "gather_weights_wait")
        return unpack(_gather_finish(shard, land, "gather_weights_finish"), BIG[n_first:], counts[n_first:])

    full = {n: wts[n] for n in SMALL_REPL}
    full['ev_norm_g'] = full['ev_norm_g'] + token[0:1, 0:1]
    small_shards = [wts[n] for n in SMALL_SHARDED]
    small_shapes = [a.shape for a in small_shards]
    all_s = _gather_chips(_pack_small(small_shards), "gather_small_weights")
    per_chip = [_unpack_small(all_s[k], small_shapes) for k in range(N_CHIPS)]
    for i, n in enumerate(SMALL_SHARDED):
        full[n] = jnp.concatenate([per_chip[k][i] for k in range(N_CHIPS)], axis=-1)

    half_rows = {(n, layer): cnt // 2 for (n, layer, _), cnt in zip(BIG, counts)}
    in_flight = []

    sending = {}

    def halves(stage, gbig):
        return [gbig[e].reshape(N_CHIPS, 2, half_rows[e], D_MODEL) for e in GRAD_STAGES[stage]]

    def reduce_send(stage, gbig):
        send_sems, recv_sems, *buffers, token = _pair_send_start(halves(stage, gbig), f"grad_pair_start{stage}")
        sending[stage] = (send_sems, recv_sems, buffers)
        return token

    def reduce_exchange(stage, after):
        send_sems, recv_sems, buffers = sending.pop(stage)
        *split, got = _pair_send_wait(send_sems, recv_sems, buffers, after, f"grad_pair_wait{stage}")
        chip_part = _pair_add(split, got, f"grad_pair_add{stage}")
        *handles, token = _exchange_start(chip_part, f"grad_exchange_start{stage}")
        in_flight.append(handles)
        return token

    loss_tile, grad_x, grads, gbig = _local_step(x[0], loss_target[0], full, big, late_weights, reduce_send, reduce_exchange)
    loss = lax.psum(loss_tile[0, 0], ("x", "y", "c"))

    reduced = {}
    for stage, entries in enumerate(GRAD_STAGES):
        if stage < len(in_flight):
            chip_part, from_chips = _exchange_wait(*in_flight[stage], grad_x, f"grad_exchange_wait{stage}")
        else:
            split = halves(stage, gbig)
            chip_part = _pair_add(split, _pair_send(split, f"grad_pair_send{stage}"), f"grad_pair_add{stage}")
            from_chips = _chip_exchange(chip_part, f"grad_chip_exchange{stage}")
        my_half = _chip_sum(chip_part, from_chips, chip.reshape(1), f"grad_chip_sum{stage}")
        joined = _join_unpack(my_half, [half_rows[e] for e in entries], list(range(len(entries))), f"grad_join_halves{stage}")
        reduced.update(zip(entries, joined))

    small_names = SMALL_REPL + SMALL_SHARDED
    small_full_shapes = [grads[n].shape for n in small_names]
    spack = _pack_small([grads[n] for n in small_names])
    s_all = _gather_devices(spack, "grad_small_gather")
    s_sum = _unpack_small(_ordered_sum(s_all, "grad_small_sum"), small_full_shapes)
    g_all = dict(zip(small_names, s_sum))
    for n in SMALL_SHARDED:
        width = wts[n].shape[-1]
        g_all[n] = lax.dynamic_slice_in_dim(g_all[n], chip * width, width, axis=g_all[n].ndim - 1)

    delta, new_m, new_v = {}, {}, {}
    for n in BIG_NAMES:
        by_cols = [bc for nn, _, bc in BIG if nn == n][0]
        layers = wts[n].shape[0]

        def as_rows(a):
            return (jnp.swapaxes(a, 1, 2) if by_cols else a).reshape(-1, D_MODEL)

        def from_rows(a):
            a = a.reshape(layers, -1, D_MODEL)
            return jnp.swapaxes(a, 1, 2) if by_cols else a

        g_rows = [reduced[n, layer] for layer in range(layers)]
        g_rows = g_rows[0] if layers == 1 else jnp.concatenate(g_rows, axis=0)
        updated = _adamw(as_rows(wts[n]), g_rows, as_rows(mom[n]), as_rows(var[n]), f"adamw_{n}")
        g_all[n] = from_rows(g_rows)
        delta[n], new_m[n], new_v[n] = (from_rows(a) for a in updated)
    for n in small_names:
        shape = wts[n].shape
        as_2d = (lambda a: a.reshape(-1, shape[-1]))
        updated = _adamw(as_2d(wts[n]), as_2d(g_all[n]), as_2d(mom[n]), as_2d(var[n]), f"adamw_{n}")
        delta[n], new_m[n], new_v[n] = (a.reshape(shape) for a in updated)

    return (loss, grad_x[None], *[g_all[n] for n in WEIGHTS], *[delta[n] for n in WEIGHTS],
            *[new_m[n] for n in WEIGHTS], *[new_v[n] for n in WEIGHTS])
```

```python
import math

import jax
import jax.numpy as jnp
from jax import lax
from jax.experimental import pallas as pl
from jax.experimental.pallas import tpu as pltpu

F32 = jnp.float32
BF16 = jnp.bfloat16

D_MODEL = 1024
HEAD_DIM = 64
ROT_DIM = 16
ROPE_THETA = 500000.0
RMS_EPS = 1e-6
LN_EPS = 1e-5
BLOCK = 128
CONV_WIDTH = 31
CONV_HALO = 32
CONV_ROWS = 64
D_FF = 2816
N_GROUPS = 8
ATTN_W = 512
ATTN_SCALE = HEAD_DIM ** -0.5
NEG = -1e30
DILATIONS = (1, 4, 16)

ADAM_LR = 0.001
ADAM_B1 = 0.9
ADAM_B2 = 0.999
ADAM_EPS = 1e-08
ADAM_WD = 0.01
ADAM_STEP = 10

LANES = 128
N_PAIRS = ATTN_W // LANES
VMEM_LIMIT = 56 * 1024 * 1024
MESH = pl.DeviceIdType.MESH
N_CHIPS = 4
N_DEV = 8

WEIGHTS = ['ev_norm_g', 'ev_w_in', 'ev_sinks', 'ev_conv_w', 'ev_conv_b', 'ev_conv_ln_g', 'ev_conv_ln_b', 'ev_w_out',
           'od_norm_g', 'od_w_in', 'od_sgu_ln_g', 'od_sgu_ln_b', 'od_spatial_w', 'od_spatial_b', 'od_w_out',
           'ffn_norm_g', 'ffn_w_gate', 'ffn_w_up', 'ffn_w_down', 'final_norm_g']
BIG = [('ev_w_in', 0, True), ('ev_w_out', 0, False), ('od_w_in', 0, True), ('od_w_out', 0, False),
       ('ffn_w_gate', 0, True), ('ffn_w_gate', 1, True), ('ffn_w_up', 0, True), ('ffn_w_up', 1, True),
       ('ffn_w_down', 0, False), ('ffn_w_down', 1, False)]
BIG_NAMES = ['ev_w_in', 'ev_w_out', 'od_w_in', 'od_w_out', 'ffn_w_gate', 'ffn_w_up', 'ffn_w_down']
GRAD_STAGES = ([('od_w_in', 0), ('od_w_out', 0), ('ffn_w_gate', 1), ('ffn_w_up', 1), ('ffn_w_down', 1)],
               [('ffn_w_gate', 0), ('ffn_w_up', 0), ('ffn_w_down', 0)],
               [('ev_w_in', 0), ('ev_w_out', 0)])
SMALL_SHARDED = ['ev_conv_w', 'od_norm_g', 'od_sgu_ln_g', 'od_sgu_ln_b']
SMALL_REPL = ['ev_norm_g', 'ev_sinks', 'ev_conv_b', 'ev_conv_ln_g', 'ev_conv_ln_b', 'od_spatial_w', 'od_spatial_b',
              'ffn_norm_g', 'final_norm_g']


def _tile(n, cap, mult=LANES):
    best = None
    for t in range(mult, min(n, cap) + 1, mult):
        if n % t == 0:
            best = t
    assert best is not None, (n, cap)
    return best


def _params(*sem):
    return pltpu.CompilerParams(dimension_semantics=sem, vmem_limit_bytes=VMEM_LIMIT)


def _sigmoid(x):
    return 1.0 / (1.0 + jnp.exp(-x))


def _pair_block(p):
    return slice(p * LANES, (p + 1) * LANES)


def _matmul(a, b, *, name, trans_a=False, trans_b=False, add=None, out_dtype=F32, after=None, rows_inner=False):
    parts = a if isinstance(a, (tuple, list)) else (a,)
    if trans_a:
        k, m = parts[0].shape
    else:
        m = parts[0].shape[0]
        k = sum(p.shape[1] for p in parts)
    if trans_b:
        n, k2 = b.shape
    else:
        k2, n = b.shape
    assert k == k2 and b.dtype == BF16 and all(p.dtype == BF16 for p in parts)
    tm = _tile(m, D_FF // 2 if trans_a else 512)
    tn = _tile(n, D_FF // 2)
    tk = k if k <= D_FF else _tile(k, 2048)
    nk = k // tk
    na = len(parts)
    assert na == 1 or (nk == 1 and not trans_a)
    assert nk == 1 or out_dtype == F32
    dims = (((0 if trans_a else 1,), (1 if trans_b else 0,)), ((), ()))
    has_add = add is not None

    def body(*refs):
        a_refs, b_ref = refs[:na], refs[na]
        add_ref = refs[na + 1] if has_add else None
        o_ref = refs[na + 1 + has_add + (after is not None)]
        def product():
            a_val = a_refs[0][...] if na == 1 else jnp.concatenate([r[...] for r in a_refs], axis=1)
            return lax.dot_general(a_val, b_ref[...], dims, preferred_element_type=F32)

        if nk == 1:
            part = product()
            if has_add:
                part = part + add_ref[...]
            o_ref[...] = part.astype(o_ref.dtype)
            return
        kk = pl.program_id(2)

        @pl.when(kk == 0)
        def _():
            o_ref[...] = product() + add_ref[...] if has_add else product()

        @pl.when(kk > 0)
        def _():
            o_ref[...] = product() + o_ref[...]

    def at(f):
        return (lambda j, i, kk: f(i, j, kk)) if rows_inner else f

    if trans_a:
        a_specs = [pl.BlockSpec((tk, tm), at(lambda i, j, kk: (kk, i)))]
    elif na == 1:
        a_specs = [pl.BlockSpec((tm, tk), at(lambda i, j, kk: (i, kk)))]
    else:
        a_specs = [pl.BlockSpec((tm, p.shape[1]), at(lambda i, j, kk: (i, 0))) for p in parts]
    b_spec = (pl.BlockSpec((tn, tk), at(lambda i, j, kk: (j, kk))) if trans_b
              else pl.BlockSpec((tk, tn), at(lambda i, j, kk: (kk, j))))
    o_spec = pl.BlockSpec((tm, tn), at(lambda i, j, kk: (i, j)))
    in_specs = a_specs + [b_spec] + ([o_spec] if has_add else [])
    operands = list(parts) + [b] + ([add] if has_add else [])
    if after is not None:
        in_specs.append(_after_spec(after))
        operands.append(after)
    grid = (n // tn, m // tm, nk) if rows_inner else (m // tm, n // tn, nk)
    return pl.pallas_call(
        body, name=name, grid=grid, in_specs=in_specs, out_specs=o_spec,
        out_shape=jax.ShapeDtypeStruct((m, n), out_dtype),
        compiler_params=_params("parallel", "parallel", "arbitrary"),
    )(*operands)


def _matmul_rows(a, b, add, epilogue, consts, tiled, outs, accs, name):
    parts = a if isinstance(a, (tuple, list)) else (a,)
    m = parts[0].shape[0]
    tm = 512
    na, nc, nt, no = len(parts), len(consts), len(tiled), len(outs)

    def body(*refs):
        a_refs, b_ref, add_ref = refs[:na], refs[na], refs[na + 1]
        const_refs = refs[na + 2:na + 2 + nc]
        tiled_refs = refs[na + 2 + nc:na + 2 + nc + nt]
        out_refs = refs[na + 2 + nc + nt:]
        a_val = a_refs[0][...] if na == 1 else jnp.concatenate([r[...] for r in a_refs], axis=1)
        h = jnp.dot(a_val, b_ref[...], preferred_element_type=F32) + add_ref[...]
        results = epilogue(h, [r[...] for r in const_refs], [r[...] for r in tiled_refs])
        for o_ref, val in zip(out_refs[:no], results[:no]):
            o_ref[...] = val.astype(o_ref.dtype)
        if accs:
            @pl.when(_first_step())
            def _():
                for o_ref in out_refs[no:]:
                    o_ref[...] = jnp.zeros_like(o_ref)

            for o_ref, val in zip(out_refs[no:], results[no:]):
                o_ref[...] += val

    row = lambda w: pl.BlockSpec((tm, w), lambda i: (i, 0))
    whole = lambda shape: pl.BlockSpec(shape, lambda i: (0,) * len(shape))
    return pl.pallas_call(
        body, name=name, grid=(m // tm,),
        in_specs=[row(p.shape[1]) for p in parts] + [whole(b.shape), row(D_MODEL)] + [whole(c.shape) for c in consts]
        + [row(t.shape[1]) for t in tiled],
        out_specs=[row(c) for c, _ in outs] + [whole(sh) for sh, _ in accs],
        out_shape=[jax.ShapeDtypeStruct((m, c), dt) for c, dt in outs] + [jax.ShapeDtypeStruct(sh, dt) for sh, dt in accs],
        compiler_params=_params("arbitrary"),
    )(*parts, b, add, *consts, *tiled)


def _matmul_norm(a, b, add, g, name):
    def epilogue(h, consts, tiled):
        r = lax.rsqrt(jnp.mean(h * h, axis=-1, keepdims=True) + RMS_EPS)
        return [h, h * r * consts[0]]

    return _matmul_rows(a, b, add, epilogue, [g], [], [(D_MODEL, F32), (D_MODEL, BF16)], [], name)


def _matmul_final(a, b, add, g, tgt, name):
    def epilogue(h, consts, tiled):
        gg = consts[0]
        r = lax.rsqrt(jnp.mean(h * h, axis=-1, keepdims=True) + RMS_EPS)
        xh = h * r
        e = xh * gg - tiled[0]
        loss = (0.5 / D_MODEL) * jnp.sum(jnp.sum(e * e, axis=-1, keepdims=True), axis=0, keepdims=True)
        dy = e * (1.0 / D_MODEL)
        dxh = dy * gg
        dx = r * (dxh - xh * jnp.mean(dxh * xh, axis=-1, keepdims=True))
        return [dx, dx, jnp.sum(dy * xh, axis=0, keepdims=True), jnp.broadcast_to(loss, (1, LANES))]

    return _matmul_rows(a, b, add, epilogue, [g], [tgt], [(D_MODEL, F32), (D_MODEL, BF16)],
                        [((1, D_MODEL), F32), ((1, LANES), F32)], name)


def _matmul_tn_pair(a1, a2, b, name):
    kdim, m1 = a1.shape
    m2 = a2.shape[1]
    n = b.shape[1]
    tn = _tile(n, 1024)
    tk = _tile(kdim, 2048)
    nk = kdim // tk
    dims = (((0,), (0,)), ((), ()))

    def body(a1_ref, a2_ref, b_ref, o_ref):
        kk = pl.program_id(1)
        def products():
            bv = b_ref[...]
            return (lax.dot_general(a1_ref[...], bv, dims, preferred_element_type=F32),
                    lax.dot_general(a2_ref[...], bv, dims, preferred_element_type=F32))

        @pl.when(kk == 0)
        def _():
            o_ref[0:m1, :], o_ref[m1:, :] = products()

        @pl.when(kk > 0)
        def _():
            top, bot = products()
            o_ref[0:m1, :] = top + o_ref[0:m1, :]
            o_ref[m1:, :] = bot + o_ref[m1:, :]

    return pl.pallas_call(
        body, name=name, grid=(n // tn, nk),
        in_specs=[pl.BlockSpec((tk, m1), lambda j, kk: (kk, 0)), pl.BlockSpec((tk, m2), lambda j, kk: (kk, 0)),
                  pl.BlockSpec((tk, tn), lambda j, kk: (kk, j))],
        out_specs=pl.BlockSpec((m1 + m2, tn), lambda j, kk: (0, j)),
        out_shape=jax.ShapeDtypeStruct((m1 + m2, n), F32),
        compiler_params=_params("parallel", "arbitrary"),
    )(a1, a2, b)


def _ffn_gate_up(n, w_gate_t, w_up_t, name):
    m, k = n.shape
    f = w_gate_t.shape[0]
    tm, tn = _tile(m, 1024), _tile(f, D_FF // 2)

    def body(n_ref, wg_ref, wu_ref, act_ref, gate_ref, up_ref):
        a = n_ref[...]

        def products(cols):
            return (lax.dot_general(a, wg_ref[cols, :], NT, preferred_element_type=F32),
                    lax.dot_general(a, wu_ref[cols, :], NT, preferred_element_type=F32))

        chunks = _col_chunks(tn)
        ahead = products(chunks[0])
        for idx, cols in enumerate(chunks):
            gate, up = ahead
            if idx + 1 < len(chunks):
                ahead = products(chunks[idx + 1])
            act_ref[:, cols] = (gate * _sigmoid(gate) * up).astype(BF16)
            gate_ref[:, cols] = gate.astype(BF16)
            up_ref[:, cols] = up.astype(BF16)

    wspec = pl.BlockSpec((tn, k), lambda j, i: (j, 0))
    ospec = pl.BlockSpec((tm, tn), lambda j, i: (i, j))
    return pl.pallas_call(
        body, name=name, grid=(f // tn, m // tm), in_specs=[pl.BlockSpec((tm, k), lambda j, i: (i, 0)), wspec, wspec],
        out_specs=[ospec] * 3, out_shape=[jax.ShapeDtypeStruct((m, f), BF16)] * 3,
        compiler_params=_params("parallel", "parallel"),
    )(n, w_gate_t, w_up_t)


def _col_chunks(n, width=384):
    return [slice(c, min(c + width, n)) for c in range(0, n, width)]


def _after_spec(after):
    return pl.BlockSpec(after.shape, lambda *_: (0,) * after.ndim)


def _ffn_dact(dhb, w_down, gate, up, name, after=None):
    m, k = dhb.shape
    f = w_down.shape[0]
    tm, tn = _tile(m, 1024), _tile(f, D_FF // 2)

    def body(d_ref, w_ref, g_ref, u_ref, *rest):
        dg_ref, du_ref = rest[-2:]
        d = d_ref[...]

        def product(cols):
            return lax.dot_general(d, w_ref[cols, :], NT, preferred_element_type=F32)

        chunks = _col_chunks(tn)
        ahead = product(chunks[0])
        for idx, cols in enumerate(chunks):
            dact = ahead
            if idx + 1 < len(chunks):
                ahead = product(chunks[idx + 1])
            g = g_ref[:, cols].astype(F32)
            sg = _sigmoid(g)
            dg_ref[:, cols] = (dact * u_ref[:, cols].astype(F32) * sg * (1.0 + g * (1.0 - sg))).astype(BF16)
            du_ref[:, cols] = (dact * g * sg).astype(BF16)

    ospec = pl.BlockSpec((tm, tn), lambda j, i: (i, j))
    extra = [] if after is None else [after]
    return pl.pallas_call(
        body, name=name, grid=(f // tn, m // tm),
        in_specs=[pl.BlockSpec((tm, k), lambda j, i: (i, 0)), pl.BlockSpec((tn, k), lambda j, i: (j, 0)), ospec, ospec]
        + [_after_spec(a) for a in extra],
        out_specs=[ospec] * 2, out_shape=[jax.ShapeDtypeStruct((m, f), BF16)] * 2,
        compiler_params=_params("parallel", "parallel"),
    )(dhb, w_down, gate, up, *extra)


def _dn_norm(pairs, h, g, dres, name):
    m = h.shape[0]
    tm = 512
    np_ = len(pairs)

    def body(*refs):
        a_refs, b_refs = refs[:np_], refs[np_:2 * np_]
        h_ref, dres_ref, g_ref, dh_ref, dhb_ref, dg_ref = refs[2 * np_:]

        @pl.when(_first_step())
        def _():
            dg_ref[...] = jnp.zeros_like(dg_ref)

        dy = jnp.dot(a_refs[0][...], b_refs[0][...], preferred_element_type=F32)
        for a_ref, b_ref in zip(a_refs[1:], b_refs[1:]):
            dy = jnp.dot(a_ref[...], b_ref[...], preferred_element_type=F32) + dy
        x = h_ref[...]
        r = lax.rsqrt(jnp.mean(x * x, axis=-1, keepdims=True) + RMS_EPS)
        xh = x * r
        dg_ref[...] += jnp.sum(dy * xh, axis=0, keepdims=True)
        dxh = dy * g_ref[...]
        tot = dres_ref[...] + r * (dxh - xh * jnp.mean(dxh * xh, axis=-1, keepdims=True))
        dh_ref[...] = tot
        dhb_ref[...] = tot.astype(BF16)

    row = lambda w: pl.BlockSpec((tm, w), lambda i: (i, 0))
    whole = lambda a: pl.BlockSpec(a.shape, lambda i: (0, 0))
    a_list, b_list = [a for a, _ in pairs], [b for _, b in pairs]
    return pl.pallas_call(
        body, name=name, grid=(m // tm,),
        in_specs=[row(a.shape[1]) for a in a_list] + [whole(b) for b in b_list] + [row(D_MODEL), row(D_MODEL), whole(g)],
        out_specs=[row(D_MODEL), row(D_MODEL), pl.BlockSpec((1, D_MODEL), lambda i: (0, 0))],
        out_shape=[jax.ShapeDtypeStruct((m, D_MODEL), F32), jax.ShapeDtypeStruct((m, D_MODEL), BF16),
                   jax.ShapeDtypeStruct((1, D_MODEL), F32)],
        compiler_params=_params("arbitrary"),
    )(*a_list, *b_list, h, dres, g)


def _rows(body, name, tm, tiled, consts, outs, accs=()):
    s = tiled[0].shape[0]
    assert s % tm == 0
    in_specs = [pl.BlockSpec((tm, a.shape[1]), lambda i: (i, 0)) for a in tiled]
    in_specs += [pl.BlockSpec(a.shape, lambda i, nd=a.ndim: (0,) * nd) for a in consts]
    out_shape = [jax.ShapeDtypeStruct((s, c), dt) for c, dt in outs]
    out_shape += [jax.ShapeDtypeStruct(sh, dt) for sh, dt in accs]
    out_specs = [pl.BlockSpec((tm, c), lambda i: (i, 0)) for c, _ in outs]
    out_specs += [pl.BlockSpec(sh, lambda i, nd=len(sh): (0,) * nd) for sh, _ in accs]
    return pl.pallas_call(
        body, name=name, grid=(s // tm,), in_specs=in_specs, out_specs=out_specs, out_shape=out_shape,
        compiler_params=_params("arbitrary"),
    )(*tiled, *consts)


def _first_step():
    return pl.program_id(0) == 0


def _rms_fwd(h, g, name):
    def body(h_ref, g_ref, n_ref):
        x = h_ref[...]
        r = lax.rsqrt(jnp.mean(x * x, axis=-1, keepdims=True) + RMS_EPS)
        n_ref[...] = (x * r * g_ref[...]).astype(BF16)

    return _rows(body, name, 512, [h], [g], [(D_MODEL, BF16)])[0]


def _rope_tables(s):
    half = ROT_DIM // 2
    inv_freq = ROPE_THETA ** (-jnp.arange(half, dtype=F32) * (2.0 / ROT_DIM))
    ang = jnp.arange(s, dtype=F32)[:, None] * inv_freq[None, :]
    cos, sin = jnp.cos(ang), jnp.sin(ang)
    rest = HEAD_DIM - ROT_DIM
    ones = jnp.ones((s, rest), F32)
    zeros = jnp.zeros((s, rest), F32)
    zh = jnp.zeros((s, half), F32)
    c_t = jnp.concatenate([cos, cos, ones], axis=1)
    a_t = jnp.concatenate([-sin, zh, zeros], axis=1)
    b_t = jnp.concatenate([zh, sin, zeros], axis=1)
    return tuple(jnp.tile(t, (1, LANES // HEAD_DIM)) for t in (c_t, a_t, b_t))


def _rot(x, c, a, b):
    w = x.shape[1]
    half = ROT_DIM // 2
    return x * c + pltpu.roll(x, w - half, 1) * a + pltpu.roll(x, half, 1) * b


def _wide(t, w):
    return t if w == LANES else jnp.tile(t, (1, w // LANES))


def _low_lanes(rows):
    return lax.broadcasted_iota(jnp.int32, (rows, LANES), 1) < HEAD_DIM


def _fold_store(x, sc_ref, out_refs):
    tm = x.shape[0]
    if any(d > 1 for d in out_refs):
        for p in range(N_PAIRS):
            sc_ref[p] = x[:, _pair_block(p)]
    for d, o_ref in out_refs.items():
        if d == 1:
            o_ref[0] = x.astype(o_ref.dtype)
            continue
        for r in range(d):
            for p in range(N_PAIRS):
                o_ref[r, :, _pair_block(p)] = sc_ref[p, pl.ds(r, tm // d, stride=d), :].astype(o_ref.dtype)


def _unfold_load(x_ref, sc_ref, d, add=False):
    n = x_ref.shape[1]
    for r in range(d):
        for p in range(N_PAIRS):
            rows = pl.ds(r, n, stride=d) if d > 1 else slice(None)
            val = x_ref[r, :, _pair_block(p)].astype(F32)
            if add:
                val = val + sc_ref[p, rows, :]
            sc_ref[p, rows, :] = val


def _folded_spec(d, tm, w=ATTN_W):
    return pl.BlockSpec((d, tm // d, w), lambda i: (0, i, 0))


def _folded_shape(s, d, dtype, w=ATTN_W):
    return jax.ShapeDtypeStruct((d, s // d, w), dtype)


def _qkv_prep_even(proj, tabs, name):
    s = proj.shape[0]
    tm = 512

    def body(p_ref, c_ref, a_ref, b_ref, q_ref, k_ref, v_ref):
        c, a, b = c_ref[...], a_ref[...], b_ref[...]
        q_ref[0] = _rot(p_ref[:, 0:ATTN_W].astype(F32), _wide(c, ATTN_W), _wide(a, ATTN_W), _wide(b, ATTN_W)).astype(BF16)
        lo = _low_lanes(tm)
        for src, o_ref in ((_rot(p_ref[:, 512:640].astype(F32), c, a, b), k_ref), (p_ref[:, 640:768].astype(F32), v_ref)):
            swapped = pltpu.roll(src, HEAD_DIM, 1)
            o_ref[0, :, 0:LANES] = jnp.where(lo, src, swapped).astype(BF16)
            o_ref[0, :, LANES:] = jnp.where(lo, swapped, src).astype(BF16)

    row = lambda w: pl.BlockSpec((tm, w), lambda i: (i, 0))
    return pl.pallas_call(
        body, name=name, grid=(s // tm,), in_specs=[row(proj.shape[1]), row(LANES), row(LANES), row(LANES)],
        out_specs=[_folded_spec(1, tm), _folded_spec(1, tm, 2 * LANES), _folded_spec(1, tm, 2 * LANES)],
        out_shape=[_folded_shape(s, 1, BF16), _folded_shape(s, 1, BF16, 2 * LANES), _folded_shape(s, 1, BF16, 2 * LANES)],
        compiler_params=_params("parallel"),
    )(proj, *tabs)


def _qkv_post_even(dq, dk, dv, dglu, tabs, name):
    s = dglu.shape[0]
    tm = 512

    def body(dq_ref, dk_ref, dv_ref, dr_ref, c_ref, a_ref, b_ref, o_ref):
        c, a, b = c_ref[...], -a_ref[...], -b_ref[...]
        o_ref[:, 0:ATTN_W] = _rot(dq_ref[0].astype(F32), _wide(c, ATTN_W), _wide(a, ATTN_W), _wide(b, ATTN_W)).astype(BF16)
        lo = _low_lanes(tm)
        merged = []
        for ref in (dk_ref, dv_ref):
            first, second = ref[0, :, 0:LANES].astype(F32), ref[0, :, LANES:].astype(F32)
            merged.append(jnp.where(lo, first + pltpu.roll(first, HEAD_DIM, 1), second + pltpu.roll(second, HEAD_DIM, 1)))
        o_ref[:, 512:640] = _rot(merged[0], c, a, b).astype(BF16)
        o_ref[:, 640:768] = merged[1].astype(BF16)
        o_ref[:, 768:] = dr_ref[...]

    row = lambda w: pl.BlockSpec((tm, w), lambda i: (i, 0))
    return pl.pallas_call(
        body, name=name, grid=(s // tm,),
        in_specs=[_folded_spec(1, tm), _folded_spec(1, tm, 2 * LANES), _folded_spec(1, tm, 2 * LANES),
                  row(dglu.shape[1]), row(LANES), row(LANES), row(LANES)],
        out_specs=row(EVEN_IN), out_shape=jax.ShapeDtypeStruct((s, EVEN_IN), BF16),
        compiler_params=_params("parallel"),
    )(dq, dk, dv, dglu, *tabs)


def _qkv_prep_odd(proj, tabs, name):
    s = proj.shape[0]
    tm = 1024

    def body(p_ref, c_ref, a_ref, b_ref, *rest):
        outs, sc_ref = rest[:-1], rest[-1]
        c, a, b = (_wide(t[...], ATTN_W) for t in (c_ref, a_ref, b_ref))
        for t in range(3):
            x = p_ref[:, t * ATTN_W:(t + 1) * ATTN_W].astype(F32)
            if t < 2:
                x = _rot(x, c, a, b)
            _fold_store(x, sc_ref, {d: outs[t * len(DILATIONS) + i] for i, d in enumerate(DILATIONS)})

    row = lambda w: pl.BlockSpec((tm, w), lambda i: (i, 0))
    return pl.pallas_call(
        body, name=name, grid=(s // tm,), in_specs=[row(proj.shape[1]), row(LANES), row(LANES), row(LANES)],
        out_specs=[_folded_spec(d, tm) for _ in range(3) for d in DILATIONS],
        out_shape=[_folded_shape(s, d, BF16) for _ in range(3) for d in DILATIONS],
        scratch_shapes=[pltpu.VMEM((N_PAIRS, tm, LANES), F32)],
        compiler_params=_params("parallel"),
    )(proj, *tabs)


def _qkv_post_odd(dqs, dks, dvs, dz, tabs, name):
    s = dz.shape[0]
    tm = 512
    nb = len(DILATIONS)

    def body(*refs):
        groups = (refs[:nb], refs[nb:2 * nb], refs[2 * nb:3 * nb])
        dz_ref, c_ref, a_ref, b_ref, o_ref, sc_ref = refs[3 * nb:]
        c, a, b = _wide(c_ref[...], ATTN_W), _wide(-a_ref[...], ATTN_W), _wide(-b_ref[...], ATTN_W)
        for t, group in enumerate(groups):
            for i, d in enumerate(DILATIONS):
                _unfold_load(group[i], sc_ref, d, add=i > 0)
            x = jnp.concatenate([sc_ref[p] for p in range(N_PAIRS)], axis=1)
            if t < 2:
                x = _rot(x, c, a, b)
            o_ref[:, t * ATTN_W:(t + 1) * ATTN_W] = x.astype(BF16)
        o_ref[:, 3 * ATTN_W:] = dz_ref[...]

    row = lambda w: pl.BlockSpec((tm, w), lambda i: (i, 0))
    return pl.pallas_call(
        body, name=name, grid=(s // tm,),
        in_specs=[_folded_spec(d, tm) for _ in range(3) for d in DILATIONS] + [row(dz.shape[1]), row(LANES), row(LANES), row(LANES)],
        out_specs=row(ODD_IN), out_shape=jax.ShapeDtypeStruct((s, ODD_IN), BF16),
        scratch_shapes=[pltpu.VMEM((N_PAIRS, tm, LANES), F32)],
        compiler_params=_params("parallel"),
    )(*dqs, *dks, *dvs, dz, *tabs)


def _fold_dout(dmix, name):
    s = dmix.shape[0]
    tm = 512
    ds = [d for d in DILATIONS if d > 1]

    def body(d_ref, *rest):
        outs, sc_ref = rest[:-1], rest[-1]
        _fold_store(d_ref[...], sc_ref, dict(zip(ds, outs)))

    return pl.pallas_call(
        body, name=name, grid=(s // tm,), in_specs=[pl.BlockSpec((tm, ATTN_W), lambda i: (i, 0))],
        out_specs=[_folded_spec(d, tm) for d in ds], out_shape=[_folded_shape(s, d, BF16) for d in ds],
        scratch_shapes=[pltpu.VMEM((N_PAIRS, tm, LANES), F32)],
        compiler_params=_params("parallel"),
    )(dmix)


def _window(j, i, tq):
    r0 = j * tq + i * BLOCK
    if i > 0:
        return pl.ds(pl.multiple_of(r0 - BLOCK, BLOCK), 2 * BLOCK), BLOCK
    start = pl.multiple_of(jnp.maximum(r0 - BLOCK, 0), BLOCK)
    return pl.ds(start, 2 * BLOCK), r0 - start


def _band_valid(offset, max_dist):
    shape = (2 * BLOCK, 2 * BLOCK)
    dist = (lax.bitwise_and(lax.broadcasted_iota(jnp.int32, shape, 0), BLOCK - 1)
            - lax.broadcasted_iota(jnp.int32, shape, 1) + offset)
    return jnp.abs(2 * dist - max_dist) <= max_dist


def _stack_heads(lo, x):
    zero = jnp.zeros_like(x)
    return jnp.concatenate([jnp.where(lo, x, zero), jnp.where(lo, zero, x)], axis=0)


def _unstack_heads(lo, x):
    return jnp.where(lo, x[:BLOCK], x[BLOCK:])


NT = (((1,), (1,)), ((), ()))
TN = (((0,), (0,)), ((), ()))


def _attn_fwd(q, k, v, sinks, *, max_dist, name, emit_bf16=False, o_dtype=F32):
    d, sp, wq = q.shape
    nq, nk = wq // LANES, k.shape[2] // LANES
    kdiv = nq // nk
    tq = min(sp, 1024)
    nsub = tq // BLOCK
    has_sink = sinks is not None

    def body(*refs):
        refs = list(refs)
        sink_ref = refs.pop(0) if has_sink else None
        q_ref, k_ref, v_ref, o_ref, lse_ref = refs[:5]
        pair = pl.program_id(1)
        j = pl.program_id(2)
        lo = _low_lanes(BLOCK)
        if has_sink:
            first_head = lax.broadcasted_iota(jnp.int32, (2 * BLOCK, 1), 0) < BLOCK
            sk = jnp.where(first_head, sink_ref[2 * pair], sink_ref[2 * pair + 1])
        for i in range(nsub):
            win, offset = _window(j, i, tq)
            rows = slice(i * BLOCK, (i + 1) * BLOCK)
            kw = k_ref[0, win, :]
            vw = v_ref[0, win, :]
            s = lax.dot_general(_stack_heads(lo, q_ref[0, rows, :]), kw, NT, preferred_element_type=F32) * ATTN_SCALE
            s = jnp.where(_band_valid(offset, max_dist), s, NEG)
            m = jnp.max(s, axis=-1, keepdims=True)
            if has_sink:
                m = jnp.maximum(m, sk)
            p = jnp.exp(s - m)
            l = jnp.sum(p, axis=-1, keepdims=True)
            if has_sink:
                l = l + jnp.exp(sk - m)
            o2 = _unstack_heads(lo, jnp.dot(p.astype(BF16), vw, preferred_element_type=F32) / l)
            o_ref[0, rows, :] = o2.astype(o_ref.dtype)
            lse_ref[0, rows, :] = _unstack_heads(lo, m + jnp.log(l))
            if emit_bf16:
                refs[5][0, rows, :] = o2.astype(BF16)

    qspec = pl.BlockSpec((1, tq, LANES), lambda r, p, j: (r, j, p))
    kspec = pl.BlockSpec((1, sp, LANES), lambda r, p, j: (r, 0, p // kdiv))
    in_specs = [qspec, kspec, kspec]
    operands = [q, k, v]
    if has_sink:
        in_specs = [pl.BlockSpec(memory_space=pltpu.SMEM)] + in_specs
        operands = [sinks] + operands
    out_shape = [jax.ShapeDtypeStruct(q.shape, o_dtype), jax.ShapeDtypeStruct(q.shape, F32)]
    if emit_bf16:
        out_shape.append(jax.ShapeDtypeStruct(q.shape, BF16))
    return pl.pallas_call(
        body, name=name, grid=(d, nq, sp // tq), in_specs=in_specs, out_specs=[qspec] * len(out_shape),
        out_shape=out_shape, compiler_params=_params("parallel", "parallel", "arbitrary"),
    )(*operands)


def _attn_bwd(q, k, v, do, oo, lse, sinks, *, max_dist, name):
    d, sp, wq = q.shape
    wk = k.shape[2]
    nq, nk = wq // LANES, wk // LANES
    kdiv = nq // nk
    tq = min(sp, 1024)
    nsub = tq // BLOCK
    has_sink = sinks is not None

    def body(*refs):
        refs = list(refs)
        sink_ref = refs.pop(0) if has_sink else None
        q_ref, k_ref, v_ref, do_ref, oo_ref, lse_ref, dq_ref, dk_out, dv_out = refs[:9]
        dk_ref, dv_ref = refs[-2:]
        pk, g, j = pl.program_id(1), pl.program_id(2), pl.program_id(3)

        @pl.when((g == 0) & (j == 0))
        def _():
            dk_ref[...] = jnp.zeros_like(dk_ref)
            dv_ref[...] = jnp.zeros_like(dv_ref)

        lo = _low_lanes(BLOCK)
        if has_sink:
            first_head = lax.broadcasted_iota(jnp.int32, (2 * BLOCK, 1), 0) < BLOCK
            pair = pk * kdiv + g
            sk = jnp.where(first_head, sink_ref[2 * pair], sink_ref[2 * pair + 1])
            sink_acc = jnp.zeros((2 * BLOCK, LANES), F32)
        for i in range(nsub):
            win, offset = _window(j, i, tq)
            rows = slice(i * BLOCK, (i + 1) * BLOCK)
            kw = k_ref[0, win, :]
            vw = v_ref[0, win, :]
            do2 = do_ref[0, rows, :].astype(F32)
            qs = _stack_heads(lo, q_ref[0, rows, :])
            dos = _stack_heads(lo, do2.astype(BF16))
            prod = do2 * oo_ref[0, rows, :]
            delta = jnp.sum(_stack_heads(lo, prod), axis=-1, keepdims=True)
            lse2 = lse_ref[0, rows, :]
            lse_swapped = pltpu.roll(lse2, HEAD_DIM, 1)
            lse_st = jnp.concatenate([jnp.where(lo, lse2, lse_swapped), jnp.where(lo, lse_swapped, lse2)], axis=0)
            s = lax.dot_general(qs, kw, NT, preferred_element_type=F32) * ATTN_SCALE
            s = jnp.where(_band_valid(offset, max_dist), s, NEG)
            p = jnp.exp(s - jnp.tile(lse_st, (1, 2)))
            dv_ref[win, :] = lax.dot_general(p.astype(BF16), dos, TN, preferred_element_type=F32) + dv_ref[win, :]
            dp = lax.dot_general(dos, vw, NT, preferred_element_type=F32)
            ds = (p * (dp - delta) * ATTN_SCALE).astype(BF16)
            dq_ref[0, rows, :] = _unstack_heads(lo, jnp.dot(ds, kw, preferred_element_type=F32)).astype(BF16)
            dk_ref[win, :] = lax.dot_general(ds, qs, TN, preferred_element_type=F32) + dk_ref[win, :]
            if has_sink:
                sink_acc = sink_acc - jnp.exp(sk - lse_st) * delta

        @pl.when((g == kdiv - 1) & (j == sp // tq - 1))
        def _():
            dk_out[0] = dk_ref[...].astype(BF16)
            dv_out[0] = dv_ref[...].astype(BF16)

        if has_sink:
            dsink_ref = refs[9]

            @pl.when(j == 0)
            def _():
                dsink_ref[...] = jnp.zeros_like(dsink_ref)

            dsink_ref[0] += jnp.where(lo[0:1], jnp.sum(sink_acc[:BLOCK], axis=0, keepdims=True),
                                      jnp.sum(sink_acc[BLOCK:], axis=0, keepdims=True))

    def qmap(r, pk, g, j):
        return (r, j, pk * kdiv + g)

    def kmap(r, pk, g, j):
        return (r, 0, pk)

    qspec = pl.BlockSpec((1, tq, LANES), qmap)
    kspec = pl.BlockSpec((1, sp, LANES), kmap)
    in_specs = [qspec, kspec, kspec, qspec, qspec, qspec]
    operands = [q, k, v, do, oo, lse]
    out_specs = [qspec, kspec, kspec]
    out_shape = [jax.ShapeDtypeStruct((d, sp, wq), BF16), jax.ShapeDtypeStruct((d, sp, wk), BF16),
                 jax.ShapeDtypeStruct((d, sp, wk), BF16)]
    if has_sink:
        in_specs = [pl.BlockSpec(memory_space=pltpu.SMEM)] + in_specs
        operands = [sinks] + operands
        out_specs.append(pl.BlockSpec((1, 1, LANES), lambda r, pk, g, j: (pk * kdiv + g, 0, 0)))
        out_shape.append(jax.ShapeDtypeStruct((nq, 1, LANES), F32))
    nsteps = sp // tq
    return pl.pallas_call(
        body, name=name, grid=(d, nk, kdiv, nsteps), in_specs=in_specs, out_specs=out_specs, out_shape=out_shape,
        scratch_shapes=[pltpu.VMEM((sp, LANES), F32), pltpu.VMEM((sp, LANES), F32)],
        compiler_params=_params("parallel", "parallel", "arbitrary", "arbitrary"),
    )(*operands)


def _combine(outs, lses, name):
    s = outs[0].shape[1]
    tm = 512
    nb = len(DILATIONS)
    ds = [d for d in DILATIONS if d > 1]

    def body(*refs):
        o_refs, l_refs = refs[:nb], refs[nb:2 * nb]
        cb_ref, c_ref, lse_ref = refs[2 * nb:2 * nb + 3]
        folded = refs[2 * nb + 3:2 * nb + 3 + 2 * len(ds)]
        scratch = refs[2 * nb + 3 + 2 * len(ds):]
        so = {1: None}
        sl = {1: None}
        for i, d in enumerate(ds):
            so[d], sl[d] = scratch[2 * i], scratch[2 * i + 1]
            _unfold_load(o_refs[1 + i], so[d], d)
            _unfold_load(l_refs[1 + i], sl[d], d)
        for p in range(N_PAIRS):
            pb = _pair_block(p)
            ls = [l_refs[0][0, :, pb]] + [sl[d][p] for d in ds]
            os_ = [o_refs[0][0, :, pb].astype(F32)] + [so[d][p] for d in ds]
            m = ls[0]
            for t in ls[1:]:
                m = jnp.maximum(m, t)
            ws = [jnp.exp(t - m) for t in ls]
            tot = ws[0]
            for t in ws[1:]:
                tot = tot + t
            acc = ws[0] * os_[0]
            for w, o in zip(ws[1:], os_[1:]):
                acc = acc + w * o
            cmix = acc / tot
            lse = m + jnp.log(tot)
            cb_ref[:, pb] = cmix.astype(BF16)
            c_ref[0, :, pb] = cmix
            lse_ref[0, :, pb] = lse
            so[ds[0]][p] = cmix
            sl[ds[0]][p] = lse
        for i, d in enumerate(ds):
            for r in range(d):
                for p in range(N_PAIRS):
                    rows = pl.ds(r, tm // d, stride=d)
                    folded[2 * i][r, :, _pair_block(p)] = so[ds[0]][p, rows, :]
                    folded[2 * i + 1][r, :, _pair_block(p)] = sl[ds[0]][p, rows, :]

    in_specs = [_folded_spec(d, tm) for _ in range(2) for d in DILATIONS]
    out_specs = [pl.BlockSpec((tm, ATTN_W), lambda i: (i, 0)), _folded_spec(1, tm), _folded_spec(1, tm)]
    out_shape = [jax.ShapeDtypeStruct((s, ATTN_W), BF16), _folded_shape(s, 1, F32), _folded_shape(s, 1, F32)]
    for d in ds:
        out_specs += [_folded_spec(d, tm)] * 2
        out_shape += [_folded_shape(s, d, F32)] * 2
    return pl.pallas_call(
        body, name=name, grid=(s // tm,), in_specs=in_specs, out_specs=out_specs, out_shape=out_shape,
        scratch_shapes=[pltpu.VMEM((N_PAIRS, tm, LANES), F32)] * (2 * len(ds)),
        compiler_params=_params("parallel"),
    )(*outs, *lses)


GLU_A = slice(768, 1280)
GLU_B = slice(1280, 1792)
EVEN_IN = 1792
ODD_IN = 2560
CONV_CH = 512


def _shifted_copies(xs_ref):
    rows = xs_ref.shape[1] - 8
    for b in range(1, 8):
        xs_ref[b, 0:rows, :] = xs_ref[0, pl.ds(b, rows), :]


def _shifted_rows(xs_ref, start):
    return xs_ref[start % 8, pl.ds(start - start % 8, CONV_ROWS), :]


def _glu(p_ref):
    return p_ref[:, GLU_A].astype(F32) * _sigmoid(p_ref[:, GLU_B].astype(F32))


def _conv_fwd(proj, w, b, ln_g, ln_b, name):
    s = proj.shape[0]
    tm = 512
    nh = tm // CONV_HALO
    lead = CONV_HALO - (CONV_WIDTH - 1)

    def body(p_ref, ph_ref, w_ref, b_ref, g_ref, bb_ref, y_ref, o_ref, xs_ref):
        xs_ref[0, CONV_HALO:, :] = _glu(p_ref)
        xs_ref[0, 0:CONV_HALO, :] = jnp.where(pl.program_id(0) > 0, _glu(ph_ref), 0.0)
        _shifted_copies(xs_ref)
        for c0 in range(0, tm, CONV_ROWS):
            acc = jnp.zeros((CONV_ROWS, CONV_CH), F32) + b_ref[...]
            for j in range(CONV_WIDTH):
                acc = acc + _shifted_rows(xs_ref, lead + j + c0) * w_ref[j:j + 1, :]
            y_ref[c0:c0 + CONV_ROWS, :] = acc
            mu = jnp.mean(acc, axis=-1, keepdims=True)
            xc = acc - mu
            var = jnp.mean(xc * xc, axis=-1, keepdims=True)
            zz = xc * lax.rsqrt(var + LN_EPS) * g_ref[...] + bb_ref[...]
            o_ref[c0:c0 + CONV_ROWS, :] = (zz * _sigmoid(zz)).astype(BF16)

    def const(a):
        return pl.BlockSpec(a.shape, lambda i: (0, 0))

    return pl.pallas_call(
        body, name=name, grid=(s // tm,),
        in_specs=[pl.BlockSpec((tm, EVEN_IN), lambda i: (i, 0)),
                  pl.BlockSpec((CONV_HALO, EVEN_IN), lambda i: (jnp.maximum(i * nh - 1, 0), 0)),
                  const(w), const(b), const(ln_g), const(ln_b)],
        out_specs=[pl.BlockSpec((tm, CONV_CH), lambda i: (i, 0)), pl.BlockSpec((tm, CONV_CH), lambda i: (i, 0))],
        out_shape=[jax.ShapeDtypeStruct((s, CONV_CH), F32), jax.ShapeDtypeStruct((s, CONV_CH), BF16)],
        scratch_shapes=[pltpu.VMEM((8, tm + CONV_HALO, CONV_CH), F32)],
        compiler_params=_params("arbitrary"),
    )(proj, proj, w, b, ln_g, ln_b)


def _conv_tail_bwd(dmix, yconv, ln_g, ln_b, name):
    def body(d_ref, y_ref, g_ref, b_ref, dy_ref, dg_ref, db_ref, dcb_ref):
        @pl.when(_first_step())
        def _():
            dg_ref[...] = jnp.zeros_like(dg_ref)
            db_ref[...] = jnp.zeros_like(db_ref)
            dcb_ref[...] = jnp.zeros_like(dcb_ref)

        y = y_ref[...]
        g = g_ref[...]
        mu = jnp.mean(y, axis=-1, keepdims=True)
        xc = y - mu
        rstd = lax.rsqrt(jnp.mean(xc * xc, axis=-1, keepdims=True) + LN_EPS)
        xh = xc * rstd
        zz = xh * g + b_ref[...]
        sg = _sigmoid(zz)
        dzz = d_ref[:, CONV_CH:] * sg * (1.0 + zz * (1.0 - sg))
        dg_ref[...] += jnp.sum(dzz * xh, axis=0, keepdims=True)
        db_ref[...] += jnp.sum(dzz, axis=0, keepdims=True)
        dxh = dzz * g
        dy = rstd * (dxh - jnp.mean(dxh, axis=-1, keepdims=True) - xh * jnp.mean(dxh * xh, axis=-1, keepdims=True))
        dcb_ref[...] += jnp.sum(dy, axis=0, keepdims=True)
        dy_ref[...] = dy

    vec = ((1, CONV_CH), F32)
    return _rows(body, name, 512, [dmix, yconv], [ln_g, ln_b], [(CONV_CH, F32)], [vec, vec, vec])


def _conv_bwd(proj, dy, w, name):
    s = proj.shape[0]
    tm = 512
    nh = tm // CONV_HALO
    nsteps = s // tm
    lead = CONV_HALO - (CONV_WIDTH - 1)

    def body(p_ref, ph_ref, dy_ref, dyn_ref, w_ref, dglu_ref, dw_ref, xf_ref, dyf_ref, part_ref):
        i = pl.program_id(0)

        @pl.when(i == 0)
        def _():
            dw_ref[...] = jnp.zeros_like(dw_ref)

        ga = p_ref[:, GLU_A].astype(F32)
        sgb = _sigmoid(p_ref[:, GLU_B].astype(F32))
        xf_ref[0, CONV_HALO:, :] = ga * sgb
        xf_ref[0, 0:CONV_HALO, :] = jnp.where(i > 0, _glu(ph_ref), 0.0)
        _shifted_copies(xf_ref)
        dyf_ref[0, 0:tm, :] = dy_ref[...]
        dyf_ref[0, tm:, :] = jnp.where(i < nsteps - 1, dyn_ref[...], 0.0)
        _shifted_copies(dyf_ref)
        for c0 in range(0, tm, CONV_ROWS):
            rows = slice(c0, c0 + CONV_ROWS)
            acc = jnp.zeros((CONV_ROWS, CONV_CH), F32)
            for j in range(CONV_WIDTH):
                acc = acc + _shifted_rows(dyf_ref, CONV_WIDTH - 1 - j + c0) * w_ref[j:j + 1, :]
            a_c, s_c = ga[rows, :], sgb[rows, :]
            dglu_ref[rows, 0:CONV_CH] = (acc * s_c).astype(BF16)
            dglu_ref[rows, CONV_CH:] = (acc * a_c * s_c * (1.0 - s_c)).astype(BF16)
        for c0 in range(0, tm, CONV_ROWS):
            dy_c = dy_ref[c0:c0 + CONV_ROWS, :]
            for j in range(CONV_WIDTH):
                prod = dy_c * _shifted_rows(xf_ref, lead + j + c0)
                part = jnp.sum(prod.reshape(CONV_ROWS // 8, 8, CONV_CH), axis=0)
                part_ref[j] = part if c0 == 0 else part + part_ref[j]
        for j in range(CONV_WIDTH):
            dw_ref[j:j + 1, :] += jnp.sum(part_ref[j], axis=0, keepdims=True)

    return pl.pallas_call(
        body, name=name, grid=(nsteps,),
        in_specs=[pl.BlockSpec((tm, EVEN_IN), lambda i: (i, 0)),
                  pl.BlockSpec((CONV_HALO, EVEN_IN), lambda i: (jnp.maximum(i * nh - 1, 0), 0)),
                  pl.BlockSpec((tm, CONV_CH), lambda i: (i, 0)),
                  pl.BlockSpec((CONV_HALO, CONV_CH), lambda i: (jnp.minimum((i + 1) * nh, s // CONV_HALO - 1), 0)),
                  pl.BlockSpec(w.shape, lambda i: (0, 0))],
        out_specs=[pl.BlockSpec((tm, 2 * CONV_CH), lambda i: (i, 0)), pl.BlockSpec(w.shape, lambda i: (0, 0))],
        out_shape=[jax.ShapeDtypeStruct((s, 2 * CONV_CH), BF16), jax.ShapeDtypeStruct(w.shape, F32)],
        scratch_shapes=[pltpu.VMEM((8, tm + CONV_HALO, CONV_CH), F32), pltpu.VMEM((8, tm + CONV_HALO, CONV_CH), F32),
                        pltpu.VMEM((CONV_WIDTH, 8, CONV_CH), F32)],
        compiler_params=_params("arbitrary"),
    )(proj, proj, dy, dy, w)


GATE_Z = slice(1536, 2560)
D_CH = 512
GELU_C = math.sqrt(2.0 / math.pi)
GELU_K = 0.044715


def _gelu_parts(z):
    t = jnp.tanh(GELU_C * (z + GELU_K * z * z * z))
    return 0.5 * z * (1.0 + t), t


def _lane_group(rows):
    return lax.broadcasted_iota(jnp.int32, (rows, D_CH), 1) // HEAD_DIM


def _tril_mask():
    return lax.broadcasted_iota(jnp.int32, (BLOCK, BLOCK), 0) >= lax.broadcasted_iota(jnp.int32, (BLOCK, BLOCK), 1)


def _layer_norm_parts(x):
    mu = jnp.mean(x, axis=-1, keepdims=True)
    xc = x - mu
    rstd = lax.rsqrt(jnp.mean(xc * xc, axis=-1, keepdims=True) + LN_EPS)
    return xc * rstd, rstd


def _gate_fwd(proj, ln_g, ln_b, w_sp, sb_t, name):
    tm = 512

    def body(p_ref, g_ref, b_ref, w_ref, sb_ref, mixed_ref, out_ref):
        zz, _ = _gelu_parts(p_ref[:, GATE_Z].astype(F32))
        u = zz[:, :D_CH]
        xh, _ = _layer_norm_parts(zz[:, D_CH:])
        gn = (xh * g_ref[...] + b_ref[...]).astype(BF16)
        grp = _lane_group(BLOCK)
        tri = _tril_mask()
        ws = [jnp.where(tri, w_ref[gi], 0.0).astype(BF16) for gi in range(N_GROUPS)]
        bias = jnp.zeros((BLOCK, D_CH), F32)
        for gi in range(N_GROUPS):
            bias = jnp.where(grp == gi, sb_ref[:, gi:gi + 1], bias)
        for ch in range(tm // BLOCK):
            rows = slice(ch * BLOCK, (ch + 1) * BLOCK)
            gc = gn[rows, :]
            mixed = bias
            for gi in range(N_GROUPS):
                r = jnp.dot(ws[gi], gc, preferred_element_type=F32)
                mixed = jnp.where(grp == gi, r + bias, mixed)
            mixed_ref[rows, :] = mixed
            out_ref[rows, :] = (u[rows, :] * mixed).astype(BF16)

    return _rows(body, name, tm, [proj], [ln_g, ln_b, w_sp, sb_t], [(D_CH, F32), (D_CH, BF16)])


def _gate_bwd(dmix, proj, mixed, ln_g, ln_b, w_sp, name):
    tm = 512

    def body(d_ref, p_ref, m_ref, g_ref, b_ref, w_ref, dz_ref, dg_ref, db_ref, dw_ref, dsb_ref, dgn_ref):
        @pl.when(_first_step())
        def _():
            dg_ref[...] = jnp.zeros_like(dg_ref)
            db_ref[...] = jnp.zeros_like(db_ref)
            dw_ref[...] = jnp.zeros_like(dw_ref)
            dsb_ref[...] = jnp.zeros_like(dsb_ref)

        z = p_ref[:, GATE_Z].astype(F32)
        zz, t = _gelu_parts(z)
        u = zz[:, :D_CH]
        xh, rstd = _layer_norm_parts(zz[:, D_CH:])
        g = g_ref[...]
        gn = (xh * g + b_ref[...]).astype(BF16)
        dd = d_ref[:, D_CH:]
        du = dd * m_ref[...]
        dm = dd * u
        grp = _lane_group(BLOCK)
        tri = _tril_mask()
        ws = [jnp.where(tri, w_ref[gi], 0.0).astype(BF16) for gi in range(N_GROUPS)]
        gsel = (lax.broadcasted_iota(jnp.int32, (N_GROUPS, D_CH), 1) // HEAD_DIM
                == lax.broadcasted_iota(jnp.int32, (N_GROUPS, D_CH), 0)).astype(F32)
        for ch in range(tm // BLOCK):
            rows = slice(ch * BLOCK, (ch + 1) * BLOCK)
            dmc = dm[rows, :]
            dmb = dmc.astype(BF16)
            gc = gn[rows, :]
            dgn = jnp.zeros((BLOCK, D_CH), F32)
            for gi in range(N_GROUPS):
                r = lax.dot_general(ws[gi], dmb, TN, preferred_element_type=F32)
                dgn = jnp.where(grp == gi, r, dgn)
                dmg = jnp.where(grp == gi, dmb, jnp.zeros_like(dmb))
                dwg = lax.dot_general(dmg, gc, NT, preferred_element_type=F32)
                dw_ref[gi] += jnp.where(tri, dwg, 0.0)
            dsb_ref[...] += lax.dot_general(gsel, dmc, NT, preferred_element_type=F32, precision=lax.Precision.HIGHEST)
            dgn_ref[rows, :] = dgn
        dgn = dgn_ref[...]
        db_ref[...] += jnp.sum(dgn, axis=0, keepdims=True)
        dg_ref[...] += jnp.sum(dgn * xh, axis=0, keepdims=True)
        dxh = dgn * g
        dgp = rstd * (dxh - jnp.mean(dxh, axis=-1, keepdims=True) - xh * jnp.mean(dxh * xh, axis=-1, keepdims=True))
        dgelu = 0.5 * (1.0 + t) + 0.5 * z * (1.0 - t * t) * GELU_C * (1.0 + 3.0 * GELU_K * z * z)
        dz_ref[:, 0:D_CH] = (du * dgelu[:, :D_CH]).astype(BF16)
        dz_ref[:, D_CH:] = (dgp * dgelu[:, D_CH:]).astype(BF16)

    s = proj.shape[0]
    tiled = [dmix, proj, mixed]
    consts = [ln_g, ln_b, w_sp]
    in_specs = [pl.BlockSpec((tm, a.shape[1]), lambda i: (i, 0)) for a in tiled]
    in_specs += [pl.BlockSpec(a.shape, lambda i, nd=a.ndim: (0,) * nd) for a in consts]
    vec = (1, D_CH)
    acc_shapes = [vec, vec, w_sp.shape, (N_GROUPS, BLOCK)]
    return pl.pallas_call(
        body, name=name, grid=(s // tm,), in_specs=in_specs,
        out_specs=[pl.BlockSpec((tm, 2 * D_CH), lambda i: (i, 0))]
        + [pl.BlockSpec(sh, lambda i, nd=len(sh): (0,) * nd) for sh in acc_shapes],
        out_shape=[jax.ShapeDtypeStruct((s, 2 * D_CH), BF16)] + [jax.ShapeDtypeStruct(sh, F32) for sh in acc_shapes],
        scratch_shapes=[pltpu.VMEM((tm, D_CH), F32)],
        compiler_params=_params("arbitrary"),
    )(*tiled, *consts)


def _adam_update(w, g, m, v):
    nm = ADAM_B1 * m + (1.0 - ADAM_B1) * g
    nv = ADAM_B2 * v + (1.0 - ADAM_B2) * (g * g)
    m_hat = nm / (1.0 - ADAM_B1 ** ADAM_STEP)
    v_hat = nv / (1.0 - ADAM_B2 ** ADAM_STEP)
    return -ADAM_LR * (m_hat / (jnp.sqrt(v_hat) + ADAM_EPS) + ADAM_WD * w), nm, nv


def _adamw(w, g, m, v, name):
    rows, cols = w.shape
    tm = _tile(rows, 512, 8) if rows % 8 == 0 else rows

    def body(w_ref, g_ref, m_ref, v_ref, d_ref, nm_ref, nv_ref):
        d_ref[...], nm_ref[...], nv_ref[...] = _adam_update(w_ref[...], g_ref[...], m_ref[...], v_ref[...])

    return _rows(body, name, tm, [w, g, m, v], [], [(cols, F32)] * 3)


def _ordered_sum(parts, name):
    n, rows, cols = parts.shape
    tm = _tile(rows, 512, 16 if parts.dtype == BF16 else 8)

    def body(p_ref, o_ref):
        acc = p_ref[0].astype(F32)
        for k in range(1, n):
            acc = acc + p_ref[k].astype(F32)
        o_ref[...] = acc

    return pl.pallas_call(body, name=name, grid=(rows // tm,),
                          in_specs=[pl.BlockSpec((n, tm, cols), lambda i: (0, i, 0))],
                          out_specs=pl.BlockSpec((tm, cols), lambda i: (i, 0)),
                          out_shape=jax.ShapeDtypeStruct((rows, cols), F32), compiler_params=_params("parallel"))(parts)


ANY = pl.BlockSpec(memory_space=pl.ANY)


def _position():
    x, y, c = lax.axis_index("x"), lax.axis_index("y"), lax.axis_index("c")
    other_chips = [(1 - x, y), (x, 1 - y), (1 - x, 1 - y)]
    return x, y, c, other_chips


def _remote(src, dst, send_sem, recv_sem, to):
    return pltpu.make_async_remote_copy(src_ref=src, dst_ref=dst, send_sem=send_sem, recv_sem=recv_sem,
                                        device_id=to, device_id_type=MESH)


STAGE_ROWS = 736


def _staged_copies(copies, buf, in_sems, out_sems):
    n = len(copies)

    def into(u):
        src = copies[u][0]
        return pltpu.make_async_copy(src, buf.at[u % 2, pl.ds(0, src.shape[0]), :], in_sems.at[u % 2])

    def out_of(u):
        dst = copies[u][1]
        return pltpu.make_async_copy(buf.at[u % 2, pl.ds(0, dst.shape[0]), :], dst, out_sems.at[u % 2])

    into(0).start()
    for u in range(n):
        into(u).wait()
        out_of(u).start()
        if u + 1 < n:
            if u >= 1:
                out_of(u - 1).wait()
            into(u + 1).start()
    if n >= 2:
        out_of(n - 2).wait()
    out_of(n - 1).wait()


def _stage_scratch(dtype, cols):
    return [pltpu.VMEM((2, STAGE_ROWS, cols), dtype), pltpu.SemaphoreType.DMA((2,)), pltpu.SemaphoreType.DMA((2,))]


def _row_chunks(rows):
    return [(r, min(STAGE_ROWS, rows - r)) for r in range(0, rows, STAGE_ROWS)]


def _gather_chips(shard, name):
    rows, cols = shard.shape
    half = rows // 2

    def body(in_ref, out_ref, send_sems, recv_sems, buf, in_sems, out_sems):
        x, y, c, chips = _position()
        me = 2 * x + y
        sibling = (x, y, 1 - c)

        def slab(chip, h):
            return out_ref.at[chip, pl.ds(h * half, half), :]

        first = [_remote(in_ref.at[pl.ds(c * half, half), :], slab(me, c), send_sems.at[j], recv_sems.at[j], (cx, cy, c))
                 for j, (cx, cy) in enumerate(chips)]
        for cp in first:
            cp.start()
        _staged_copies([(in_ref.at[pl.ds(r, n), :], out_ref.at[me, pl.ds(r, n), :]) for r, n in _row_chunks(rows)],
                       buf, in_sems, out_sems)
        passed = []
        for j, (cx, cy) in enumerate(chips):
            got = slab(2 * cx + cy, c)
            _remote(got, got, send_sems.at[j], recv_sems.at[j], sibling).wait_recv()
            cp = _remote(got, got, send_sems.at[3 + j], recv_sems.at[3 + j], sibling)
            cp.start()
            passed.append(cp)
        for j, (cx, cy) in enumerate(chips):
            got = slab(2 * cx + cy, 1 - c)
            _remote(got, got, send_sems.at[3 + j], recv_sems.at[3 + j], sibling).wait_recv()
        for cp in first + passed:
            cp.wait_send()

    return pl.pallas_call(
        body, name=name, in_specs=[ANY], out_specs=ANY,
        out_shape=jax.ShapeDtypeStruct((N_CHIPS, rows, cols), shard.dtype),
        scratch_shapes=[pltpu.SemaphoreType.DMA((6,)), pltpu.SemaphoreType.DMA((6,))] + _stage_scratch(shard.dtype, cols),
        compiler_params=pltpu.CompilerParams(vmem_limit_bytes=VMEM_LIMIT),
    )(shard)


HBM = pl.BlockSpec(memory_space=pltpu.HBM)
SEM = pl.BlockSpec(memory_space=pltpu.SEMAPHORE)
SIDE_EFFECT = pltpu.SideEffectType.DATAFLOW_SIDE_EFFECTING


def _ici_copies(in_ref, land_ref, send_sems, recv_sems, half):
    x, y, c, chips = _position()
    mine = pl.ds(c * half, half)
    sends = [_remote(in_ref.at[mine, :], land_ref.at[2 * x + y, mine, :], send_sems.at[j], recv_sems.at[j], (cx, cy, c))
             for j, (cx, cy) in enumerate(chips)]
    arrivals = [_remote(in_ref.at[mine, :], land_ref.at[2 * cx + cy, mine, :], send_sems.at[j], recv_sems.at[j], (cx, cy, c))
                for j, (cx, cy) in enumerate(chips)]
    return sends, arrivals


def _gather_start(shard, after, name):
    rows, cols = shard.shape

    def body(in_ref, land_ref, after_ref, send_sems, recv_sems, in_thru, land_thru, token):
        sends, _ = _ici_copies(in_ref, land_ref, send_sems, recv_sems, rows // 2)
        for cp in sends:
            cp.start()
        token[...] = jnp.zeros_like(token)

    land = lax.empty((N_CHIPS, rows, cols), shard.dtype)
    return pl.pallas_call(
        body, name=name,
        out_shape=(pltpu.SemaphoreType.DMA((3,)), pltpu.SemaphoreType.DMA((3,)), pltpu.HBM(shard.shape, shard.dtype),
                   pltpu.HBM(land.shape, land.dtype), jax.ShapeDtypeStruct((8, LANES), F32)),
        in_specs=(HBM, HBM, ANY), out_specs=(SEM, SEM, HBM, HBM, pl.BlockSpec(memory_space=pltpu.VMEM)),
        input_output_aliases={0: 2, 1: 3},
        compiler_params=pltpu.CompilerParams(has_side_effects=SIDE_EFFECT),
    )(pltpu.with_memory_space_constraint(shard, pltpu.HBM), pltpu.with_memory_space_constraint(land, pltpu.HBM), after)


def _gather_wait(send_sems, recv_sems, shard, land, after, name):
    rows = shard.shape[0]

    def body(in_ref, land_ref, send_sems, recv_sems, after_ref, in_out, land_out):
        sends, arrivals = _ici_copies(in_ref, land_ref, send_sems, recv_sems, rows // 2)
        for cp in sends:
            cp.wait_send()
        for cp in arrivals:
            cp.wait_recv()

    return pl.pallas_call(
        body, name=name, out_shape=(pltpu.HBM(shard.shape, shard.dtype), pltpu.HBM(land.shape, land.dtype)),
        in_specs=(HBM, HBM, SEM, SEM, ANY), out_specs=(HBM, HBM), input_output_aliases={0: 0, 1: 1},
        compiler_params=pltpu.CompilerParams(has_side_effects=SIDE_EFFECT),
    )(shard, land, send_sems, recv_sems, after)


def _gather_finish(shard, land, name):
    rows, cols = shard.shape
    half = rows // 2

    def body(in_ref, land_ref, out_ref, send_sems, recv_sems, buf, in_sems, out_sems):
        x, y, c, chips = _position()
        me = 2 * x + y
        sibling = (x, y, 1 - c)

        def slab(chip, h):
            return out_ref.at[chip, pl.ds(h * half, half), :]

        passed = [_remote(slab(2 * cx + cy, c), slab(2 * cx + cy, c), send_sems.at[j], recv_sems.at[j], sibling)
                  for j, (cx, cy) in enumerate(chips)]
        for cp in passed:
            cp.start()
        _staged_copies([(in_ref.at[pl.ds(r, n), :], out_ref.at[me, pl.ds(r, n), :]) for r, n in _row_chunks(rows)],
                       buf, in_sems, out_sems)
        for j, (cx, cy) in enumerate(chips):
            got = slab(2 * cx + cy, 1 - c)
            _remote(got, got, send_sems.at[j], recv_sems.at[j], sibling).wait_recv()
        for cp in passed:
            cp.wait_send()

    return pl.pallas_call(
        body, name=name, in_specs=[ANY, ANY], out_specs=ANY, out_shape=jax.ShapeDtypeStruct(land.shape, land.dtype),
        input_output_aliases={1: 0},
        scratch_shapes=[pltpu.SemaphoreType.DMA((3,)), pltpu.SemaphoreType.DMA((3,))] + _stage_scratch(shard.dtype, cols),
        compiler_params=pltpu.CompilerParams(vmem_limit_bytes=VMEM_LIMIT),
    )(shard, land)


def _gather_devices(block, name):
    rows, cols = block.shape

    def body(in_ref, out_ref, send_sems, recv_sems, local_sem):
        x, y, c, chips = _position()
        sibling = (x, y, 1 - c)

        def slot(px, py, pc):
            return out_ref.at[4 * px + 2 * py + pc]

        mine = pltpu.make_async_copy(in_ref, slot(x, y, c), local_sem)
        mine.start()
        first = [_remote(in_ref, slot(x, y, c), send_sems.at[0], recv_sems.at[0], sibling)]
        first += [_remote(in_ref, slot(x, y, c), send_sems.at[1 + j], recv_sems.at[1 + j], (cx, cy, c))
                  for j, (cx, cy) in enumerate(chips)]
        for cp in first:
            cp.start()
        passed = []
        for j, (cx, cy) in enumerate(chips):
            got = slot(cx, cy, c)
            _remote(got, got, send_sems.at[1 + j], recv_sems.at[1 + j], sibling).wait_recv()
            cp = _remote(got, got, send_sems.at[4 + j], recv_sems.at[4 + j], sibling)
            cp.start()
            passed.append(cp)
        got = slot(x, y, 1 - c)
        _remote(got, got, send_sems.at[0], recv_sems.at[0], sibling).wait_recv()
        for j, (cx, cy) in enumerate(chips):
            got = slot(cx, cy, 1 - c)
            _remote(got, got, send_sems.at[4 + j], recv_sems.at[4 + j], sibling).wait_recv()
        for cp in first + passed:
            cp.wait_send()
        mine.wait()

    return pl.pallas_call(
        body, name=name, in_specs=[ANY], out_specs=ANY,
        out_shape=jax.ShapeDtypeStruct((N_DEV, rows, cols), block.dtype),
        scratch_shapes=[pltpu.SemaphoreType.DMA((7,)), pltpu.SemaphoreType.DMA((7,)), pltpu.SemaphoreType.DMA],
    )(block)


def _pair_send(grads, name):
    n = len(grads)
    hs = [g.shape[2] for g in grads]
    offs = [sum(hs[:i]) for i in range(n)]
    cols = grads[0].shape[3]

    def body(*refs):
        g_refs = refs[:n]
        got_ref, send_sems, recv_sems = refs[n:]
        x, y, c, _ = _position()
        copies = [_remote(g_ref.at[:, 1 - c], got_ref.at[:, pl.ds(offs[i], hs[i]), :], send_sems.at[i], recv_sems.at[i],
                          (x, y, 1 - c)) for i, g_ref in enumerate(g_refs)]
        for cp in copies:
            cp.start()
        for cp in copies:
            cp.wait()

    return pl.pallas_call(
        body, name=name, in_specs=[ANY] * n, out_specs=ANY, out_shape=jax.ShapeDtypeStruct((N_CHIPS, sum(hs), cols), F32),
        scratch_shapes=[pltpu.SemaphoreType.DMA((n,)), pltpu.SemaphoreType.DMA((n,))],
    )(*grads)


def _pair_copies(g_refs, land_ref, send_sems, recv_sems):
    x, y, c, _ = _position()
    hs = [g.shape[2] for g in g_refs]
    offs = [sum(hs[:i]) for i in range(len(hs))]
    return [_remote(g_ref.at[:, 1 - c], land_ref.at[:, pl.ds(offs[i], hs[i]), :], send_sems.at[i], recv_sems.at[i],
                    (x, y, 1 - c)) for i, g_ref in enumerate(g_refs)]


def _pair_send_start(grads, name):
    n = len(grads)
    land = lax.empty((N_CHIPS, sum(g.shape[2] for g in grads), grads[0].shape[3]), F32)

    def body(*refs):
        for cp in _pair_copies(refs[:n], refs[n], refs[n + 1], refs[n + 2]):
            cp.start()
        refs[-1][...] = jnp.zeros_like(refs[-1])

    buffers = [*grads, land]
    return pl.pallas_call(
        body, name=name,
        out_shape=(pltpu.SemaphoreType.DMA((n,)), pltpu.SemaphoreType.DMA((n,)),
                   *[pltpu.HBM(b.shape, b.dtype) for b in buffers], jax.ShapeDtypeStruct((8, LANES), F32)),
        in_specs=(HBM,) * (n + 1), out_specs=(SEM, SEM, *(HBM,) * (n + 1), pl.BlockSpec(memory_space=pltpu.VMEM)),
        input_output_aliases={i: 2 + i for i in range(n + 1)},
        compiler_params=pltpu.CompilerParams(has_side_effects=SIDE_EFFECT),
    )(*[pltpu.with_memory_space_constraint(b, pltpu.HBM) for b in buffers])


def _pair_send_wait(send_sems, recv_sems, buffers, after, name):
    n = len(buffers) - 1

    def body(*refs):
        for cp in _pair_copies(refs[:n], refs[n], refs[n + 1], refs[n + 2]):
            cp.wait_send()
            cp.wait_recv()

    return pl.pallas_call(
        body, name=name, out_shape=tuple(pltpu.HBM(b.shape, b.dtype) for b in buffers),
        in_specs=(*(HBM,) * (n + 1), SEM, SEM, ANY), out_specs=(HBM,) * (n + 1),
        input_output_aliases={i: i for i in range(n + 1)},
        compiler_params=pltpu.CompilerParams(has_side_effects=SIDE_EFFECT),
    )(*buffers, send_sems, recv_sems, after)


def _pair_add(grads, got, name):
    n = len(grads)
    hs = [g.shape[2] for g in grads]
    offs = [sum(hs[:i]) for i in range(n)]
    cols = grads[0].shape[3]
    hmax = max(hs)
    units = [(i, k) for k in range(N_CHIPS) for i in range(n)]

    def body(*refs):
        g_refs = refs[:n]
        got_ref, out_ref, a_buf, b_buf, o_buf, a_sems, b_sems, o_sems = refs[n:]
        c = lax.axis_index("c")

        def loads(u):
            i, k = units[u]
            slot, rows = u % 2, pl.ds(0, hs[i])
            return (pltpu.make_async_copy(g_refs[i].at[k, c], a_buf.at[slot, rows, :], a_sems.at[slot]),
                    pltpu.make_async_copy(got_ref.at[k, pl.ds(offs[i], hs[i]), :], b_buf.at[slot, rows, :], b_sems.at[slot]))

        def store(u):
            i, k = units[u]
            return pltpu.make_async_copy(o_buf.at[u % 2, pl.ds(0, hs[i]), :], out_ref.at[k, pl.ds(offs[i], hs[i]), :],
                                         o_sems.at[u % 2])

        for cp in loads(0):
            cp.start()
        for u, (i, k) in enumerate(units):
            if u + 1 < len(units):
                for cp in loads(u + 1):
                    cp.start()
            for cp in loads(u):
                cp.wait()
            if u >= 2:
                store(u - 2).wait()
            rows = pl.ds(0, hs[i])
            o_buf[u % 2, rows, :] = (a_buf[u % 2, rows, :] + b_buf[u % 2, rows, :]).astype(BF16)
            store(u).start()
        store(len(units) - 2).wait()
        store(len(units) - 1).wait()

    return pl.pallas_call(
        body, name=name, in_specs=[ANY] * (n + 1), out_specs=ANY,
        out_shape=jax.ShapeDtypeStruct((N_CHIPS, sum(hs), cols), BF16),
        scratch_shapes=[pltpu.VMEM((2, hmax, cols), F32), pltpu.VMEM((2, hmax, cols), F32), pltpu.VMEM((2, hmax, cols), BF16),
                        pltpu.SemaphoreType.DMA((2,)), pltpu.SemaphoreType.DMA((2,)), pltpu.SemaphoreType.DMA((2,))],
        compiler_params=pltpu.CompilerParams(vmem_limit_bytes=VMEM_LIMIT),
    )(*grads, got)


def _chip_exchange(parts, name):
    _, rows, cols = parts.shape

    def body(in_ref, out_ref, send_sems, recv_sems):
        x, y, c, chips = _position()
        sent = [_remote(in_ref.at[2 * cx + cy], out_ref.at[j], send_sems.at[j], recv_sems.at[j], (cx, cy, c))
                for j, (cx, cy) in enumerate(chips)]
        for cp in sent:
            cp.start()
        for cp in sent:
            cp.wait()

    return pl.pallas_call(
        body, name=name, in_specs=[ANY], out_specs=ANY, out_shape=jax.ShapeDtypeStruct((3, rows, cols), parts.dtype),
        scratch_shapes=[pltpu.SemaphoreType.DMA((3,)), pltpu.SemaphoreType.DMA((3,))],
    )(parts)


def _exchange_copies(in_ref, land_ref, send_sems, recv_sems):
    x, y, c, chips = _position()
    return [_remote(in_ref.at[2 * cx + cy], land_ref.at[j], send_sems.at[j], recv_sems.at[j], (cx, cy, c))
            for j, (cx, cy) in enumerate(chips)]


def _exchange_start(parts, name):
    _, rows, cols = parts.shape

    def body(in_ref, land_ref, send_sems, recv_sems, in_thru, land_thru, token):
        for cp in _exchange_copies(in_ref, land_ref, send_sems, recv_sems):
            cp.start()
        token[...] = jnp.zeros_like(token)

    land = lax.empty((3, rows, cols), parts.dtype)
    return pl.pallas_call(
        body, name=name,
        out_shape=(pltpu.SemaphoreType.DMA((3,)), pltpu.SemaphoreType.DMA((3,)), pltpu.HBM(parts.shape, parts.dtype),
                   pltpu.HBM(land.shape, land.dtype), jax.ShapeDtypeStruct((8, LANES), F32)),
        in_specs=(HBM, HBM), out_specs=(SEM, SEM, HBM, HBM, pl.BlockSpec(memory_space=pltpu.VMEM)),
        input_output_aliases={0: 2, 1: 3},
        compiler_params=pltpu.CompilerParams(has_side_effects=SIDE_EFFECT),
    )(pltpu.with_memory_space_constraint(parts, pltpu.HBM), pltpu.with_memory_space_constraint(land, pltpu.HBM))


def _exchange_wait(send_sems, recv_sems, parts, land, after, name):
    def body(in_ref, land_ref, send_sems, recv_sems, after_ref, in_out, land_out):
        for cp in _exchange_copies(in_ref, land_ref, send_sems, recv_sems):
            cp.wait_send()
            cp.wait_recv()

    return pl.pallas_call(
        body, name=name, out_shape=(pltpu.HBM(parts.shape, parts.dtype), pltpu.HBM(land.shape, land.dtype)),
        in_specs=(HBM, HBM, SEM, SEM, ANY), out_specs=(HBM, HBM), input_output_aliases={0: 0, 1: 1},
        compiler_params=pltpu.CompilerParams(has_side_effects=SIDE_EFFECT),
    )(parts, land, send_sems, recv_sems, after)


def _chip_sum(parts, recv, chip, name):
    _, rows, cols = parts.shape
    tm = _tile(rows, 512, 16)

    def body(chip_ref, own_ref, recv_ref, o_ref):
        acc = own_ref[0].astype(F32)
        for j in range(3):
            acc = acc + recv_ref[j].astype(F32)
        o_ref[...] = acc

    return pl.pallas_call(
        body, name=name,
        grid_spec=pltpu.PrefetchScalarGridSpec(
            num_scalar_prefetch=1, grid=(rows // tm,),
            in_specs=[pl.BlockSpec((1, tm, cols), lambda i, chip_ref: (chip_ref[0], i, 0)),
                      pl.BlockSpec((3, tm, cols), lambda i, chip_ref: (0, i, 0))],
            out_specs=pl.BlockSpec((tm, cols), lambda i, chip_ref: (i, 0))),
        out_shape=jax.ShapeDtypeStruct((rows, cols), F32), compiler_params=_params("parallel"),
    )(chip, parts, recv)


def _join_unpack(mine, hs, groups, name):
    n = len(hs)
    offs = [sum(hs[:i]) for i in range(n)]
    cols = mine.shape[1]
    n_out = max(groups) + 1
    base = [2 * sum(h for h, g in zip(hs[:i], groups[:i]) if g == groups[i]) for i in range(n)]
    out_rows = [2 * sum(h for h, g in zip(hs, groups) if g == k) for k in range(n_out)]

    def body(in_ref, *refs):
        outs = refs[:n_out]
        send_sems, recv_sems, buf, in_sems, out_sems = refs[n_out:]
        x, y, c, _ = _position()
        sibling = (x, y, 1 - c)
        sent, local = [], []
        for i in range(n):
            src = in_ref.at[pl.ds(offs[i], hs[i]), :]
            here = outs[groups[i]].at[pl.ds(base[i] + c * hs[i], hs[i]), :]
            cp = _remote(src, here, send_sems.at[i], recv_sems.at[i], sibling)
            cp.start()
            sent.append(cp)
            local.append((src, here))
        _staged_copies(local, buf, in_sems, out_sems)
        for i, cp in enumerate(sent):
            there = outs[groups[i]].at[pl.ds(base[i] + (1 - c) * hs[i], hs[i]), :]
            _remote(there, there, send_sems.at[i], recv_sems.at[i], sibling).wait_recv()
            cp.wait_send()

    assert max(hs) <= STAGE_ROWS
    return pl.pallas_call(
        body, name=name, in_specs=[ANY], out_specs=[ANY] * n_out,
        out_shape=[jax.ShapeDtypeStruct((r, cols), F32) for r in out_rows],
        scratch_shapes=[pltpu.SemaphoreType.DMA((n,)), pltpu.SemaphoreType.DMA((n,))] + _stage_scratch(F32, cols),
        compiler_params=pltpu.CompilerParams(vmem_limit_bytes=VMEM_LIMIT),
    )(mine)


SMALL_ROWS = 16
SMALL_PACK_ROWS = 256


def _small_rows(n):
    return -(-n // (SMALL_ROWS * LANES)) * SMALL_ROWS


def _pack_small(arrs):
    parts = []
    for a in arrs:
        flat = a.reshape(-1)
        rows = _small_rows(flat.shape[0])
        flat = jnp.pad(flat, (0, rows * LANES - flat.shape[0]))
        parts.append(flat.reshape(rows, LANES))
    total = sum(p.shape[0] for p in parts)
    parts.append(jnp.zeros((-total % SMALL_PACK_ROWS, LANES), F32))
    return jnp.concatenate(parts, axis=0)


def _unpack_small(packed, shapes):
    out, r = [], 0
    for sh in shapes:
        n = math.prod(sh)
        cnt = _small_rows(n)
        out.append(packed[r:r + cnt].reshape(-1)[:n].reshape(sh))
        r += cnt
    return out


def _ffn_bwd(dh, dhb, h_in, saved, g_norm, w_gate_t, w_up_t, w_down, tag, after=None):
    n, gate, up, act = saved
    dgate, dup = _ffn_dact(dhb, w_down, gate, up, f"{tag}_dact", after)
    dw_down = _matmul(act, dhb, trans_a=True, name=f"{tag}_dwdown")
    dw_gate_t = _matmul(dgate, n, trans_a=True, name=f"{tag}_dwgate")
    dw_up_t = _matmul(dup, n, trans_a=True, name=f"{tag}_dwup")
    dh_in, dh_inb, dg = _dn_norm([(dgate, w_gate_t), (dup, w_up_t)], h_in, g_norm, dh, f"{tag}_dnorm")
    return dh_in, dh_inb, dg, dw_gate_t, dw_up_t, dw_down


def _local_step(x, tgt, w, big, late_weights, reduce_send, reduce_exchange):
    s = x.shape[0]
    tabs = _rope_tables(s)
    grads, gbig = {}, {}

    g_ev = w['ev_norm_g']
    n1 = _rms_fwd(x, g_ev, "ev_norm")
    proj0 = _matmul(n1, big['ev_w_in', 0], trans_b=True, name="ev_in", out_dtype=BF16, rows_inner=True)
    q0, k0, v0 = _qkv_prep_even(proj0, tabs, "ev_qkv")
    sinks = w['ev_sinks'].reshape(-1)
    o0, lse0, o0b = _attn_fwd(q0, k0, v0, sinks, max_dist=BLOCK - 1, name="ev_attn", emit_bf16=True)
    yconv, cout = _conv_fwd(proj0, w['ev_conv_w'][0], w['ev_conv_b'], w['ev_conv_ln_g'], w['ev_conv_ln_b'], "ev_conv")
    mix0 = (o0b[0], cout)
    g_f0 = w['ffn_norm_g'][0:1]
    h1, n2 = _matmul_norm(mix0, big['ev_w_out', 0], x, g_f0, "ev_out")
    big = {**big, **late_weights(h1)}

    g_od = w['od_norm_g']
    act0, gate0, up0 = _ffn_gate_up(n2, big['ffn_w_gate', 0], big['ffn_w_up', 0], "ffn0_gate_up")
    h2, n3 = _matmul_norm(act0, big['ffn_w_down', 0], h1, g_od, "ffn0_down")
    ffn0 = (n2, gate0, up0, act0)

    proj1 = _matmul(n3, big['od_w_in', 0], trans_b=True, name="od_in", out_dtype=BF16, rows_inner=True)
    qkv = _qkv_prep_odd(proj1, tabs, "od_qkv")
    nb = len(DILATIONS)
    outs, lses = [], []
    for i, d in enumerate(DILATIONS):
        o_r, lse_r = _attn_fwd(qkv[i], qkv[nb + i], qkv[2 * nb + i], None, max_dist=BLOCK, name=f"od_attn{d}", o_dtype=BF16)
        outs.append(o_r)
        lses.append(lse_r)
    comb = _combine(outs, lses, "od_combine")
    c_bf16 = comb[0]
    c_fold = {1: comb[1]}
    lse_fold = {1: comb[2]}
    for i, d in enumerate(DILATIONS[1:]):
        c_fold[d], lse_fold[d] = comb[3 + 2 * i], comb[4 + 2 * i]
    w_sp = w['od_spatial_w'][0]
    sb_t = w['od_spatial_b'][0].T
    mixed, dout = _gate_fwd(proj1, w['od_sgu_ln_g'], w['od_sgu_ln_b'], w_sp, sb_t, "od_gate")
    mix1 = (c_bf16, dout)
    g_f1 = w['ffn_norm_g'][1:2]
    h3, n4 = _matmul_norm(mix1, big['od_w_out', 0], h2, g_f1, "od_out")
    act1, gate1, up1 = _ffn_gate_up(n4, big['ffn_w_gate', 1], big['ffn_w_up', 1], "ffn1_gate_up")
    ffn1 = (n4, gate1, up1, act1)

    dh4, dh4b, dg_final, loss_tile = _matmul_final(act1, big['ffn_w_down', 1], h3, w['final_norm_g'].reshape(1, D_MODEL),
                                                   tgt, "ffn1_down_loss")
    grads['final_norm_g'] = dg_final.reshape(D_MODEL)

    dh3, dh3b, dg_f1, gbig['ffn_w_gate', 1], gbig['ffn_w_up', 1], gbig['ffn_w_down', 1] = _ffn_bwd(
        dh4, dh4b, h3, ffn1, g_f1, big['ffn_w_gate', 1], big['ffn_w_up', 1], big['ffn_w_down', 1], "ffn1")

    dmix1 = _matmul(dh3b, big['od_w_out', 0], trans_b=True, name="od_dmix")
    gbig['od_w_out', 0] = _matmul_tn_pair(mix1[0], mix1[1], dh3b, "od_dwout")
    do_fold = dict(zip(DILATIONS[1:], _fold_dout(dmix1, "od_fold_dout")))
    do_fold[1] = dmix1[None]
    dqs, dks, dvs = [], [], []
    for i, d in enumerate(DILATIONS):
        dq_r, dk_r, dv_r = _attn_bwd(qkv[i], qkv[nb + i], qkv[2 * nb + i], do_fold[d], c_fold[d], lse_fold[d], None,
                                     max_dist=BLOCK, name=f"od_dattn{d}")
        dqs.append(dq_r)
        dks.append(dk_r)
        dvs.append(dv_r)
    dz, dg_sgu, db_sgu, dw_sp, dsb = _gate_bwd(dmix1, proj1, mixed, w['od_sgu_ln_g'], w['od_sgu_ln_b'], w_sp, "od_dgate")
    grads['od_sgu_ln_g'], grads['od_sgu_ln_b'] = dg_sgu, db_sgu
    grads['od_spatial_w'], grads['od_spatial_b'] = dw_sp[None], dsb[None]
    dproj1 = _qkv_post_odd(dqs, dks, dvs, dz, tabs, "od_dproj")
    gbig['od_w_in', 0] = _matmul(dproj1, n3, trans_a=True, name="od_dwin")
    dh2, dh2b, dg_od = _dn_norm([(dproj1, big['od_w_in', 0])], h2, g_od, dh3, "od_dnorm")
    grads['od_norm_g'] = dg_od
    token = reduce_send(0, gbig)

    dh1, dh1b, dg_f0, gbig['ffn_w_gate', 0], gbig['ffn_w_up', 0], gbig['ffn_w_down', 0] = _ffn_bwd(
        dh2, dh2b, h1, ffn0, g_f0, big['ffn_w_gate', 0], big['ffn_w_up', 0], big['ffn_w_down', 0], "ffn0", token)
    grads['ffn_norm_g'] = jnp.concatenate([dg_f0, dg_f1], axis=0)
    token = reduce_exchange(0, dh1) + reduce_send(1, gbig)

    dmix0 = _matmul(dh1b, big['ev_w_out', 0], trans_b=True, name="ev_dmix", after=token)
    gbig['ev_w_out', 0] = _matmul_tn_pair(mix0[0], mix0[1], dh1b, "ev_dwout")
    dq0, dk0, dv0, dsink = _attn_bwd(q0, k0, v0, dmix0[None], o0, lse0, sinks, max_dist=BLOCK - 1, name="ev_dattn")
    grads['ev_sinks'] = dsink[:, 0, :].reshape(N_PAIRS, 2, HEAD_DIM)[:, :, 0].reshape(1, 8)
    token = reduce_exchange(1, dq0)
    dyc, dg_cln, db_cln, dcb = _conv_tail_bwd(dmix0, yconv, w['ev_conv_ln_g'] + token[0:1, 0:1], w['ev_conv_ln_b'],
                                              "ev_dconv_tail")
    grads['ev_conv_ln_g'], grads['ev_conv_ln_b'], grads['ev_conv_b'] = dg_cln, db_cln, dcb
    dglu, dconv_w = _conv_bwd(proj0, dyc, w['ev_conv_w'][0], "ev_dconv")
    grads['ev_conv_w'] = dconv_w[None]
    dproj0 = _qkv_post_even(dq0, dk0, dv0, dglu, tabs, "ev_dproj")
    gbig['ev_w_in', 0] = _matmul(dproj0, n1, trans_a=True, name="ev_dwin")
    dx, _, dg_ev = _dn_norm([(dproj0, big['ev_w_in', 0])], x, g_ev, dh1, "ev_dnorm")
    grads['ev_norm_g'] = dg_ev
    return loss_tile, dx, grads, gbig


def _shard_rows(w, layer, by_cols):
    return w[layer].T if by_cols else w[layer]


def kernel(x, ev_norm_g, ev_w_in, ev_sinks, ev_conv_w, ev_conv_b, ev_conv_ln_g, ev_conv_ln_b, ev_w_out, od_norm_g, od_w_in, od_sgu_ln_g, od_sgu_ln_b, od_spatial_w, od_spatial_b, od_w_out, ffn_norm_g, ffn_w_gate, ffn_w_up, ffn_w_down, final_norm_g, loss_target, m_ev_norm_g, m_ev_w_in, m_ev_sinks, m_ev_conv_w, m_ev_conv_b, m_ev_conv_ln_g, m_ev_conv_ln_b, m_ev_w_out, m_od_norm_g, m_od_w_in, m_od_sgu_ln_g, m_od_sgu_ln_b, m_od_spatial_w, m_od_spatial_b, m_od_w_out, m_ffn_norm_g, m_ffn_w_gate, m_ffn_w_up, m_ffn_w_down, m_final_norm_g, v_ev_norm_g, v_ev_w_in, v_ev_sinks, v_ev_conv_w, v_ev_conv_b, v_ev_conv_ln_g, v_ev_conv_ln_b, v_ev_w_out, v_od_norm_g, v_od_w_in, v_od_sgu_ln_g, v_od_sgu_ln_b, v_od_spatial_w, v_od_spatial_b, v_od_w_out, v_ffn_norm_g, v_ffn_w_gate, v_ffn_w_up, v_ffn_w_down, v_final_norm_g):
    given = dict(locals())
    wts = {n: given[n] for n in WEIGHTS}
    mom = {n: given["m_" + n] for n in WEIGHTS}
    var = {n: given["v_" + n] for n in WEIGHTS}
    chip = 2 * lax.axis_index("x") + lax.axis_index("y")

    shard_rows = [_shard_rows(wts[n], layer, by_cols).astype(BF16) for n, layer, by_cols in BIG]
    counts = [a.shape[0] for a in shard_rows]
    n_first = sum(n.startswith('ev_') for n, _, _ in BIG)

    def unpack(stacked, entries, cnts):
        out, r = {}, 0
        for (n, layer, _), cnt in zip(entries, cnts):
            out[n, layer] = stacked[:, r:r + cnt].reshape(N_CHIPS * cnt, D_MODEL)
            r += cnt
        return out

    first_w = _gather_chips(jnp.concatenate(shard_rows[:n_first], axis=0), "gather_weights_ev")
    big = unpack(first_w, BIG[:n_first], counts[:n_first])
    send_sems, recv_sems, late_shard, late_land, token = _gather_start(jnp.concatenate(shard_rows[n_first:], axis=0),
                                                                      first_w, "gather_weights_start")

    def late_weights(after):
        shard, land = _gather_wait(send_sems, recv_sems, late_shard, late_land, after, "gather_weights_wait")
        return unpack(_gather_finish(shard, land, "gather_weights_finish"), BIG[n_first:], counts[n_first:])

    full = {n: wts[n] for n in SMALL_REPL}
    full['ev_norm_g'] = full['ev_norm_g'] + token[0:1, 0:1]
    small_shards = [wts[n] for n in SMALL_SHARDED]
    small_shapes = [a.shape for a in small_shards]
    all_s = _gather_chips(_pack_small(small_shards), "gather_small_weights")
    r = 0
    for n, sh in zip(SMALL_SHARDED, small_shapes):
        size = math.prod(sh)
        stack = all_s[:, r:r + _small_rows(size)].reshape(N_CHIPS, -1)[:, :size].reshape((N_CHIPS,) + sh)
        full[n] = jnp.moveaxis(stack, 0, -2).reshape(sh[:-1] + (N_CHIPS * sh[-1],))
        r += _small_rows(size)

    half_rows = {(n, layer): cnt // 2 for (n, layer, _), cnt in zip(BIG, counts)}
    in_flight = []

    sending = {}

    def halves(stage, gbig):
        return [gbig[e].reshape(N_CHIPS, 2, half_rows[e], D_MODEL) for e in GRAD_STAGES[stage]]

    def reduce_send(stage, gbig):
        send_sems, recv_sems, *buffers, token = _pair_send_start(halves(stage, gbig), f"grad_pair_start{stage}")
        sending[stage] = (send_sems, recv_sems, buffers)
        return token

    def reduce_exchange(stage, after):
        send_sems, recv_sems, buffers = sending.pop(stage)
        *split, got = _pair_send_wait(send_sems, recv_sems, buffers, after, f"grad_pair_wait{stage}")
        chip_part = _pair_add(split, got, f"grad_pair_add{stage}")
        *handles, token = _exchange_start(chip_part, f"grad_exchange_start{stage}")
        in_flight.append(handles)
        return token

    loss_tile, grad_x, grads, gbig = _local_step(x[0], loss_target[0], full, big, late_weights, reduce_send, reduce_exchange)
    loss = lax.psum(loss_tile[0, 0], ("x", "y", "c"))

    reduced = {}
    for stage, entries in enumerate(GRAD_STAGES):
        if stage < len(in_flight):
            chip_part, from_chips = _exchange_wait(*in_flight[stage], grad_x, f"grad_exchange_wait{stage}")
        else:
            split = halves(stage, gbig)
            chip_part = _pair_add(split, _pair_send(split, f"grad_pair_send{stage}"), f"grad_pair_add{stage}")
            from_chips = _chip_exchange(chip_part, f"grad_chip_exchange{stage}")
        my_half = _chip_sum(chip_part, from_chips, chip.reshape(1), f"grad_chip_sum{stage}")
        joined = _join_unpack(my_half, [half_rows[e] for e in entries], list(range(len(entries))), f"grad_join_halves{stage}")
        reduced.update(zip(entries, joined))

    small_names = SMALL_REPL + SMALL_SHARDED
    small_full_shapes = [grads[n].shape for n in small_names]
    spack = _pack_small([grads[n] for n in small_names])
    s_all = _gather_devices(spack, "grad_small_gather")
    s_sum = _unpack_small(_ordered_sum(s_all, "grad_small_sum"), small_full_shapes)
    g_all = dict(zip(small_names, s_sum))
    for n in SMALL_SHARDED:
        width = wts[n].shape[-1]
        g_all[n] = lax.dynamic_slice_in_dim(g_all[n], chip * width, width, axis=g_all[n].ndim - 1)

    delta, new_m, new_v = {}, {}, {}
    for n in BIG_NAMES:
        by_cols = [bc for nn, _, bc in BIG if nn == n][0]
        layers = wts[n].shape[0]

        def as_rows(a):
            return (jnp.swapaxes(a, 1, 2) if by_cols else a).reshape(-1, D_MODEL)

        def from_rows(a):
            a = a.reshape(layers, -1, D_MODEL)
            return jnp.swapaxes(a, 1, 2) if by_cols else a

        g_rows = [reduced[n, layer] for layer in range(layers)]
        g_rows = g_rows[0] if layers == 1 else jnp.concatenate(g_rows, axis=0)
        updated = _adamw(as_rows(wts[n]), g_rows, as_rows(mom[n]), as_rows(var[n]), f"adamw_{n}")
        g_all[n] = from_rows(g_rows)
        delta[n], new_m[n], new_v[n] = (from_rows(a) for a in updated)
    for n in small_names:
        shape = wts[n].shape
        as_2d = (lambda a: a.reshape(-1, shape[-1]))
        updated = _adamw(as_2d(wts[n]), as_2d(g_all[n]), as_2d(mom[n]), as_2d(var[n]), f"adamw_{n}")
        delta[n], new_m[n], new_v[n] = (a.reshape(shape) for a in updated)

    return (loss, grad_x[None], *[g_all[n] for n in WEIGHTS], *[delta[n] for n in WEIGHTS],
            *[new_m[n] for n in WEIGHTS], *[new_v[n] for n in WEIGHTS])
```

```python
import math

import jax
import jax.numpy as jnp
from jax import lax
from jax.experimental import pallas as pl
from jax.experimental.pallas import tpu as pltpu

F32 = jnp.float32
BF16 = jnp.bfloat16

D_MODEL = 1024
HEAD_DIM = 64
ROT_DIM = 16
ROPE_THETA = 500000.0
RMS_EPS = 1e-6
LN_EPS = 1e-5
BLOCK = 128
CONV_WIDTH = 31
CONV_HALO = 32
CONV_ROWS = 64
D_FF = 2816
N_GROUPS = 8
ATTN_W = 512
ATTN_SCALE = HEAD_DIM ** -0.5
NEG = -1e30
DILATIONS = (1, 4, 16)

ADAM_LR = 0.001
ADAM_B1 = 0.9
ADAM_B2 = 0.999
ADAM_EPS = 1e-08
ADAM_WD = 0.01
ADAM_STEP = 10

LANES = 128
N_PAIRS = ATTN_W // LANES
VMEM_LIMIT = 56 * 1024 * 1024
MESH = pl.DeviceIdType.MESH
N_CHIPS = 4
N_DEV = 8

WEIGHTS = ['ev_norm_g', 'ev_w_in', 'ev_sinks', 'ev_conv_w', 'ev_conv_b', 'ev_conv_ln_g', 'ev_conv_ln_b', 'ev_w_out',
           'od_norm_g', 'od_w_in', 'od_sgu_ln_g', 'od_sgu_ln_b', 'od_spatial_w', 'od_spatial_b', 'od_w_out',
           'ffn_norm_g', 'ffn_w_gate', 'ffn_w_up', 'ffn_w_down', 'final_norm_g']
BIG = [('ev_w_in', 0, True), ('ev_w_out', 0, False), ('od_w_in', 0, True), ('od_w_out', 0, False),
       ('ffn_w_gate', 0, True), ('ffn_w_gate', 1, True), ('ffn_w_up', 0, True), ('ffn_w_up', 1, True),
       ('ffn_w_down', 0, False), ('ffn_w_down', 1, False)]
BIG_NAMES = ['ev_w_in', 'ev_w_out', 'od_w_in', 'od_w_out', 'ffn_w_gate', 'ffn_w_up', 'ffn_w_down']
GRAD_STAGES = ([('od_w_in', 0), ('od_w_out', 0), ('ffn_w_gate', 1), ('ffn_w_up', 1), ('ffn_w_down', 1)],
               [('ffn_w_gate', 0), ('ffn_w_up', 0), ('ffn_w_down', 0)],
               [('ev_w_in', 0), ('ev_w_out', 0)])
SMALL_SHARDED = ['ev_conv_w', 'od_norm_g', 'od_sgu_ln_g', 'od_sgu_ln_b']
SMALL_REPL = ['ev_norm_g', 'ev_sinks', 'ev_conv_b', 'ev_conv_ln_g', 'ev_conv_ln_b', 'od_spatial_w', 'od_spatial_b',
              'ffn_norm_g', 'final_norm_g']


def _tile(n, cap, mult=LANES):
    best = None
    for t in range(mult, min(n, cap) + 1, mult):
        if n % t == 0:
            best = t
    assert best is not None, (n, cap)
    return best


def _params(*sem):
    return pltpu.CompilerParams(dimension_semantics=sem, vmem_limit_bytes=VMEM_LIMIT)


def _sigmoid(x):
    return 1.0 / (1.0 + jnp.exp(-x))


def _pair_block(p):
    return slice(p * LANES, (p + 1) * LANES)


def _matmul(a, b, *, name, trans_a=False, trans_b=False, add=None, out_dtype=F32, after=None, rows_inner=False):
    parts = a if isinstance(a, (tuple, list)) else (a,)
    if trans_a:
        k, m = parts[0].shape
    else:
        m = parts[0].shape[0]
        k = sum(p.shape[1] for p in parts)
    if trans_b:
        n, k2 = b.shape
    else:
        k2, n = b.shape
    assert k == k2 and b.dtype == BF16 and all(p.dtype == BF16 for p in parts)
    tm = _tile(m, D_FF // 2 if trans_a else 512)
    tn = _tile(n, D_FF // 2)
    tk = k if k <= D_FF else _tile(k, 2048)
    nk = k // tk
    na = len(parts)
    assert na == 1 or (nk == 1 and not trans_a)
    assert nk == 1 or out_dtype == F32
    dims = (((0 if trans_a else 1,), (1 if trans_b else 0,)), ((), ()))
    has_add = add is not None

    def body(*refs):
        a_refs, b_ref = refs[:na], refs[na]
        add_ref = refs[na + 1] if has_add else None
        o_ref = refs[na + 1 + has_add + (after is not None)]
        def product():
            a_val = a_refs[0][...] if na == 1 else jnp.concatenate([r[...] for r in a_refs], axis=1)
            return lax.dot_general(a_val, b_ref[...], dims, preferred_element_type=F32)

        if nk == 1:
            part = product()
            if has_add:
                part = part + add_ref[...]
            o_ref[...] = part.astype(o_ref.dtype)
            return
        kk = pl.program_id(2)

        @pl.when(kk == 0)
        def _():
            o_ref[...] = product() + add_ref[...] if has_add else product()

        @pl.when(kk > 0)
        def _():
            o_ref[...] = product() + o_ref[...]

    def at(f):
        return (lambda j, i, kk: f(i, j, kk)) if rows_inner else f

    if trans_a:
        a_specs = [pl.BlockSpec((tk, tm), at(lambda i, j, kk: (kk, i)))]
    elif na == 1:
        a_specs = [pl.BlockSpec((tm, tk), at(lambda i, j, kk: (i, kk)))]
    else:
        a_specs = [pl.BlockSpec((tm, p.shape[1]), at(lambda i, j, kk: (i, 0))) for p in parts]
    b_spec = (pl.BlockSpec((tn, tk), at(lambda i, j, kk: (j, kk))) if trans_b
              else pl.BlockSpec((tk, tn), at(lambda i, j, kk: (kk, j))))
    o_spec = pl.BlockSpec((tm, tn), at(lambda i, j, kk: (i, j)))
    in_specs = a_specs + [b_spec] + ([o_spec] if has_add else [])
    operands = list(parts) + [b] + ([add] if has_add else [])
    if after is not None:
        in_specs.append(_after_spec(after))
        operands.append(after)
    grid = (n // tn, m // tm, nk) if rows_inner else (m // tm, n // tn, nk)
    return pl.pallas_call(
        body, name=name, grid=grid, in_specs=in_specs, out_specs=o_spec,
        out_shape=jax.ShapeDtypeStruct((m, n), out_dtype),
        compiler_params=_params("parallel", "parallel", "arbitrary"),
    )(*operands)


def _matmul_rows(a, b, add, epilogue, consts, tiled, outs, accs, name):
    parts = a if isinstance(a, (tuple, list)) else (a,)
    m = parts[0].shape[0]
    tm = 512
    na, nc, nt, no = len(parts), len(consts), len(tiled), len(outs)

    def body(*refs):
        a_refs, b_ref, add_ref = refs[:na], refs[na], refs[na + 1]
        const_refs = refs[na + 2:na + 2 + nc]
        tiled_refs = refs[na + 2 + nc:na + 2 + nc + nt]
        out_refs = refs[na + 2 + nc + nt:]
        a_val = a_refs[0][...] if na == 1 else jnp.concatenate([r[...] for r in a_refs], axis=1)
        h = jnp.dot(a_val, b_ref[...], preferred_element_type=F32) + add_ref[...]
        results = epilogue(h, [r[...] for r in const_refs], [r[...] for r in tiled_refs])
        for o_ref, val in zip(out_refs[:no], results[:no]):
            o_ref[...] = val.astype(o_ref.dtype)
        if accs:
            @pl.when(_first_step())
            def _():
                for o_ref in out_refs[no:]:
                    o_ref[...] = jnp.zeros_like(o_ref)

            for o_ref, val in zip(out_refs[no:], results[no:]):
                o_ref[...] += val

    row = lambda w: pl.BlockSpec((tm, w), lambda i: (i, 0))
    whole = lambda shape: pl.BlockSpec(shape, lambda i: (0,) * len(shape))
    return pl.pallas_call(
        body, name=name, grid=(m // tm,),
        in_specs=[row(p.shape[1]) for p in parts] + [whole(b.shape), row(D_MODEL)] + [whole(c.shape) for c in consts]
        + [row(t.shape[1]) for t in tiled],
        out_specs=[row(c) for c, _ in outs] + [whole(sh) for sh, _ in accs],
        out_shape=[jax.ShapeDtypeStruct((m, c), dt) for c, dt in outs] + [jax.ShapeDtypeStruct(sh, dt) for sh, dt in accs],
        compiler_params=_params("arbitrary"),
    )(*parts, b, add, *consts, *tiled)


def _matmul_norm(a, b, add, g, name):
    def epilogue(h, consts, tiled):
        r = lax.rsqrt(jnp.mean(h * h, axis=-1, keepdims=True) + RMS_EPS)
        return [h, h * r * consts[0]]

    return _matmul_rows(a, b, add, epilogue, [g], [], [(D_MODEL, F32), (D_MODEL, BF16)], [], name)


def _matmul_final(a, b, add, g, tgt, name):
    def epilogue(h, consts, tiled):
        gg = consts[0]
        r = lax.rsqrt(jnp.mean(h * h, axis=-1, keepdims=True) + RMS_EPS)
        xh = h * r
        e = xh * gg - tiled[0]
        loss = (0.5 / D_MODEL) * jnp.sum(jnp.sum(e * e, axis=-1, keepdims=True), axis=0, keepdims=True)
        dy = e * (1.0 / D_MODEL)
        dxh = dy * gg
        dx = r * (dxh - xh * jnp.mean(dxh * xh, axis=-1, keepdims=True))
        return [dx, dx, jnp.sum(dy * xh, axis=0, keepdims=True), jnp.broadcast_to(loss, (1, LANES))]

    return _matmul_rows(a, b, add, epilogue, [g], [tgt], [(D_MODEL, F32), (D_MODEL, BF16)],
                        [((1, D_MODEL), F32), ((1, LANES), F32)], name)


def _matmul_tn_pair(a1, a2, b, name):
    kdim, m1 = a1.shape
    m2 = a2.shape[1]
    n = b.shape[1]
    tn = _tile(n, 1024)
    tk = _tile(kdim, 2048)
    nk = kdim // tk
    dims = (((0,), (0,)), ((), ()))

    def body(a1_ref, a2_ref, b_ref, o_ref):
        kk = pl.program_id(1)
        def products():
            bv = b_ref[...]
            return (lax.dot_general(a1_ref[...], bv, dims, preferred_element_type=F32),
                    lax.dot_general(a2_ref[...], bv, dims, preferred_element_type=F32))

        @pl.when(kk == 0)
        def _():
            o_ref[0:m1, :], o_ref[m1:, :] = products()

        @pl.when(kk > 0)
        def _():
            top, bot = products()
            o_ref[0:m1, :] = top + o_ref[0:m1, :]
            o_ref[m1:, :] = bot + o_ref[m1:, :]

    return pl.pallas_call(
        body, name=name, grid=(n // tn, nk),
        in_specs=[pl.BlockSpec((tk, m1), lambda j, kk: (kk, 0)), pl.BlockSpec((tk, m2), lambda j, kk: (kk, 0)),
                  pl.BlockSpec((tk, tn), lambda j, kk: (kk, j))],
        out_specs=pl.BlockSpec((m1 + m2, tn), lambda j, kk: (0, j)),
        out_shape=jax.ShapeDtypeStruct((m1 + m2, n), F32),
        compiler_params=_params("parallel", "arbitrary"),
    )(a1, a2, b)


def _ffn_gate_up(n, w_gate_t, w_up_t, name):
    m, k = n.shape
    f = w_gate_t.shape[0]
    tm, tn = _tile(m, 1024), _tile(f, D_FF // 2)

    def body(n_ref, wg_ref, wu_ref, act_ref, gate_ref, up_ref):
        a = n_ref[...]

        def products(cols):
            return (lax.dot_general(a, wg_ref[cols, :], NT, preferred_element_type=F32),
                    lax.dot_general(a, wu_ref[cols, :], NT, preferred_element_type=F32))

        chunks = _col_chunks(tn)
        ahead = products(chunks[0])
        for idx, cols in enumerate(chunks):
            gate, up = ahead
            if idx + 1 < len(chunks):
                ahead = products(chunks[idx + 1])
            act_ref[:, cols] = (gate * _sigmoid(gate) * up).astype(BF16)
            gate_ref[:, cols] = gate.astype(BF16)
            up_ref[:, cols] = up.astype(BF16)

    wspec = pl.BlockSpec((tn, k), lambda j, i: (j, 0))
    ospec = pl.BlockSpec((tm, tn), lambda j, i: (i, j))
    return pl.pallas_call(
        body, name=name, grid=(f // tn, m // tm), in_specs=[pl.BlockSpec((tm, k), lambda j, i: (i, 0)), wspec, wspec],
        out_specs=[ospec] * 3, out_shape=[jax.ShapeDtypeStruct((m, f), BF16)] * 3,
        compiler_params=_params("parallel", "parallel"),
    )(n, w_gate_t, w_up_t)


def _col_chunks(n, width=384):
    return [slice(c, min(c + width, n)) for c in range(0, n, width)]


def _after_spec(after):
    return pl.BlockSpec(after.shape, lambda *_: (0,) * after.ndim)


def _ffn_dact(dhb, w_down, gate, up, name, after=None):
    m, k = dhb.shape
    f = w_down.shape[0]
    tm, tn = _tile(m, 1024), _tile(f, D_FF // 2)

    def body(d_ref, w_ref, g_ref, u_ref, *rest):
        dg_ref, du_ref = rest[-2:]
        d = d_ref[...]

        def product(cols):
            return lax.dot_general(d, w_ref[cols, :], NT, preferred_element_type=F32)

        chunks = _col_chunks(tn)
        ahead = product(chunks[0])
        for idx, cols in enumerate(chunks):
            dact = ahead
            if idx + 1 < len(chunks):
                ahead = product(chunks[idx + 1])
            g = g_ref[:, cols].astype(F32)
            sg = _sigmoid(g)
            dg_ref[:, cols] = (dact * u_ref[:, cols].astype(F32) * sg * (1.0 + g * (1.0 - sg))).astype(BF16)
            du_ref[:, cols] = (dact * g * sg).astype(BF16)

    ospec = pl.BlockSpec((tm, tn), lambda j, i: (i, j))
    extra = [] if after is None else [after]
    return pl.pallas_call(
        body, name=name, grid=(f // tn, m // tm),
        in_specs=[pl.BlockSpec((tm, k), lambda j, i: (i, 0)), pl.BlockSpec((tn, k), lambda j, i: (j, 0)), ospec, ospec]
        + [_after_spec(a) for a in extra],
        out_specs=[ospec] * 2, out_shape=[jax.ShapeDtypeStruct((m, f), BF16)] * 2,
        compiler_params=_params("parallel", "parallel"),
    )(dhb, w_down, gate, up, *extra)


def _dn_norm(pairs, h, g, dres, name):
    m = h.shape[0]
    tm = 512
    np_ = len(pairs)

    def body(*refs):
        a_refs, b_refs = refs[:np_], refs[np_:2 * np_]
        h_ref, dres_ref, g_ref, dh_ref, dhb_ref, dg_ref = refs[2 * np_:]

        @pl.when(_first_step())
        def _():
            dg_ref[...] = jnp.zeros_like(dg_ref)

        dy = jnp.dot(a_refs[0][...], b_refs[0][...], preferred_element_type=F32)
        for a_ref, b_ref in zip(a_refs[1:], b_refs[1:]):
            dy = jnp.dot(a_ref[...], b_ref[...], preferred_element_type=F32) + dy
        x = h_ref[...]
        r = lax.rsqrt(jnp.mean(x * x, axis=-1, keepdims=True) + RMS_EPS)
        xh = x * r
        dg_ref[...] += jnp.sum(dy * xh, axis=0, keepdims=True)
        dxh = dy * g_ref[...]
        tot = dres_ref[...] + r * (dxh - xh * jnp.mean(dxh * xh, axis=-1, keepdims=True))
        dh_ref[...] = tot
        dhb_ref[...] = tot.astype(BF16)

    row = lambda w: pl.BlockSpec((tm, w), lambda i: (i, 0))
    whole = lambda a: pl.BlockSpec(a.shape, lambda i: (0, 0))
    a_list, b_list = [a for a, _ in pairs], [b for _, b in pairs]
    return pl.pallas_call(
        body, name=name, grid=(m // tm,),
        in_specs=[row(a.shape[1]) for a in a_list] + [whole(b) for b in b_list] + [row(D_MODEL), row(D_MODEL), whole(g)],
        out_specs=[row(D_MODEL), row(D_MODEL), pl.BlockSpec((1, D_MODEL), lambda i: (0, 0))],
        out_shape=[jax.ShapeDtypeStruct((m, D_MODEL), F32), jax.ShapeDtypeStruct((m, D_MODEL), BF16),
                   jax.ShapeDtypeStruct((1, D_MODEL), F32)],
        compiler_params=_params("arbitrary"),
    )(*a_list, *b_list, h, dres, g)


def _rows(body, name, tm, tiled, consts, outs, accs=()):
    s = tiled[0].shape[0]
    assert s % tm == 0
    in_specs = [pl.BlockSpec((tm, a.shape[1]), lambda i: (i, 0)) for a in tiled]
    in_specs += [pl.BlockSpec(a.shape, lambda i, nd=a.ndim: (0,) * nd) for a in consts]
    out_shape = [jax.ShapeDtypeStruct((s, c), dt) for c, dt in outs]
    out_shape += [jax.ShapeDtypeStruct(sh, dt) for sh, dt in accs]
    out_specs = [pl.BlockSpec((tm, c), lambda i: (i, 0)) for c, _ in outs]
    out_specs += [pl.BlockSpec(sh, lambda i, nd=len(sh): (0,) * nd) for sh, _ in accs]
    return pl.pallas_call(
        body, name=name, grid=(s // tm,), in_specs=in_specs, out_specs=out_specs, out_shape=out_shape,
        compiler_params=_params("arbitrary"),
    )(*tiled, *consts)


def _first_step():
    return pl.program_id(0) == 0


def _rms_fwd(h, g, name):
    def body(h_ref, g_ref, n_ref):
        x = h_ref[...]
        r = lax.rsqrt(jnp.mean(x * x, axis=-1, keepdims=True) + RMS_EPS)
        n_ref[...] = (x * r * g_ref[...]).astype(BF16)

    return _rows(body, name, 512, [h], [g], [(D_MODEL, BF16)])[0]


def _rope_tables(s):
    half = ROT_DIM // 2
    inv_freq = ROPE_THETA ** (-jnp.arange(half, dtype=F32) * (2.0 / ROT_DIM))
    ang = jnp.arange(s, dtype=F32)[:, None] * inv_freq[None, :]
    cos, sin = jnp.cos(ang), jnp.sin(ang)
    rest = HEAD_DIM - ROT_DIM
    ones = jnp.ones((s, rest), F32)
    zeros = jnp.zeros((s, rest), F32)
    zh = jnp.zeros((s, half), F32)
    c_t = jnp.concatenate([cos, cos, ones], axis=1)
    a_t = jnp.concatenate([-sin, zh, zeros], axis=1)
    b_t = jnp.concatenate([zh, sin, zeros], axis=1)
    return tuple(jnp.tile(t, (1, LANES // HEAD_DIM)) for t in (c_t, a_t, b_t))


def _rot(x, c, a, b):
    w = x.shape[1]
    half = ROT_DIM // 2
    return x * c + pltpu.roll(x, w - half, 1) * a + pltpu.roll(x, half, 1) * b


def _wide(t, w):
    return t if w == LANES else jnp.tile(t, (1, w // LANES))


def _low_lanes(rows):
    return lax.broadcasted_iota(jnp.int32, (rows, LANES), 1) < HEAD_DIM


def _fold_store(x, sc_ref, out_refs):
    tm = x.shape[0]
    if any(d > 1 for d in out_refs):
        for p in range(N_PAIRS):
            sc_ref[p] = x[:, _pair_block(p)]
    for d, o_ref in out_refs.items():
        if d == 1:
            o_ref[0] = x.astype(o_ref.dtype)
            continue
        for r in range(d):
            for p in range(N_PAIRS):
                o_ref[r, :, _pair_block(p)] = sc_ref[p, pl.ds(r, tm // d, stride=d), :].astype(o_ref.dtype)


def _unfold_load(x_ref, sc_ref, d, add=False):
    n = x_ref.shape[1]
    for r in range(d):
        for p in range(N_PAIRS):
            rows = pl.ds(r, n, stride=d) if d > 1 else slice(None)
            val = x_ref[r, :, _pair_block(p)].astype(F32)
            if add:
                val = val + sc_ref[p, rows, :]
            sc_ref[p, rows, :] = val


def _folded_spec(d, tm, w=ATTN_W):
    return pl.BlockSpec((d, tm // d, w), lambda i: (0, i, 0))


def _folded_shape(s, d, dtype, w=ATTN_W):
    return jax.ShapeDtypeStruct((d, s // d, w), dtype)


def _qkv_prep_even(proj, tabs, name):
    s = proj.shape[0]
    tm = 512

    def body(p_ref, c_ref, a_ref, b_ref, q_ref, k_ref, v_ref):
        c, a, b = c_ref[...], a_ref[...], b_ref[...]
        q_ref[0] = _rot(p_ref[:, 0:ATTN_W].astype(F32), _wide(c, ATTN_W), _wide(a, ATTN_W), _wide(b, ATTN_W)).astype(BF16)
        lo = _low_lanes(tm)
        for src, o_ref in ((_rot(p_ref[:, 512:640].astype(F32), c, a, b), k_ref), (p_ref[:, 640:768].astype(F32), v_ref)):
            swapped = pltpu.roll(src, HEAD_DIM, 1)
            o_ref[0, :, 0:LANES] = jnp.where(lo, src, swapped).astype(BF16)
            o_ref[0, :, LANES:] = jnp.where(lo, swapped, src).astype(BF16)

    row = lambda w: pl.BlockSpec((tm, w), lambda i: (i, 0))
    return pl.pallas_call(
        body, name=name, grid=(s // tm,), in_specs=[row(proj.shape[1]), row(LANES), row(LANES), row(LANES)],
        out_specs=[_folded_spec(1, tm), _folded_spec(1, tm, 2 * LANES), _folded_spec(1, tm, 2 * LANES)],
        out_shape=[_folded_shape(s, 1, BF16), _folded_shape(s, 1, BF16, 2 * LANES), _folded_shape(s, 1, BF16, 2 * LANES)],
        compiler_params=_params("parallel"),
    )(proj, *tabs)


def _qkv_post_even(dq, dk, dv, dglu, tabs, name):
    s = dglu.shape[0]
    tm = 512

    def body(dq_ref, dk_ref, dv_ref, dr_ref, c_ref, a_ref, b_ref, o_ref):
        c, a, b = c_ref[...], -a_ref[...], -b_ref[...]
        o_ref[:, 0:ATTN_W] = _rot(dq_ref[0].astype(F32), _wide(c, ATTN_W), _wide(a, ATTN_W), _wide(b, ATTN_W)).astype(BF16)
        lo = _low_lanes(tm)
        merged = []
        for ref in (dk_ref, dv_ref):
            first, second = ref[0, :, 0:LANES].astype(F32), ref[0, :, LANES:].astype(F32)
            merged.append(jnp.where(lo, first + pltpu.roll(first, HEAD_DIM, 1), second + pltpu.roll(second, HEAD_DIM, 1)))
        o_ref[:, 512:640] = _rot(merged[0], c, a, b).astype(BF16)
        o_ref[:, 640:768] = merged[1].astype(BF16)
        o_ref[:, 768:] = dr_ref[...]

    row = lambda w: pl.BlockSpec((tm, w), lambda i: (i, 0))
    return pl.pallas_call(
        body, name=name, grid=(s // tm,),
        in_specs=[_folded_spec(1, tm), _folded_spec(1, tm, 2 * LANES), _folded_spec(1, tm, 2 * LANES),
                  row(dglu.shape[1]), row(LANES), row(LANES), row(LANES)],
        out_specs=row(EVEN_IN), out_shape=jax.ShapeDtypeStruct((s, EVEN_IN), BF16),
        compiler_params=_params("parallel"),
    )(dq, dk, dv, dglu, *tabs)


def _qkv_prep_odd(proj, tabs, name):
    s = proj.shape[0]
    tm = 1024

    def body(p_ref, c_ref, a_ref, b_ref, *rest):
        outs, sc_ref = rest[:-1], rest[-1]
        c, a, b = (_wide(t[...], ATTN_W) for t in (c_ref, a_ref, b_ref))
        for t in range(3):
            x = p_ref[:, t * ATTN_W:(t + 1) * ATTN_W].astype(F32)
            if t < 2:
                x = _rot(x, c, a, b)
            _fold_store(x, sc_ref, {d: outs[t * len(DILATIONS) + i] for i, d in enumerate(DILATIONS)})

    row = lambda w: pl.BlockSpec((tm, w), lambda i: (i, 0))
    return pl.pallas_call(
        body, name=name, grid=(s // tm,), in_specs=[row(proj.shape[1]), row(LANES), row(LANES), row(LANES)],
        out_specs=[_folded_spec(d, tm) for _ in range(3) for d in DILATIONS],
        out_shape=[_folded_shape(s, d, BF16) for _ in range(3) for d in DILATIONS],
        scratch_shapes=[pltpu.VMEM((N_PAIRS, tm, LANES), F32)],
        compiler_params=_params("parallel"),
    )(proj, *tabs)


def _qkv_post_odd(dqs, dks, dvs, dz, tabs, name):
    s = dz.shape[0]
    tm = 512
    nb = len(DILATIONS)

    def body(*refs):
        groups = (refs[:nb], refs[nb:2 * nb], refs[2 * nb:3 * nb])
        dz_ref, c_ref, a_ref, b_ref, o_ref, sc_ref = refs[3 * nb:]
        c, a, b = _wide(c_ref[...], ATTN_W), _wide(-a_ref[...], ATTN_W), _wide(-b_ref[...], ATTN_W)
        for t, group in enumerate(groups):
            for i, d in enumerate(DILATIONS):
                _unfold_load(group[i], sc_ref, d, add=i > 0)
            x = jnp.concatenate([sc_ref[p] for p in range(N_PAIRS)], axis=1)
            if t < 2:
                x = _rot(x, c, a, b)
            o_ref[:, t * ATTN_W:(t + 1) * ATTN_W] = x.astype(BF16)
        o_ref[:, 3 * ATTN_W:] = dz_ref[...]

    row = lambda w: pl.BlockSpec((tm, w), lambda i: (i, 0))
    return pl.pallas_call(
        body, name=name, grid=(s // tm,),
        in_specs=[_folded_spec(d, tm) for _ in range(3) for d in DILATIONS] + [row(dz.shape[1]), row(LANES), row(LANES), row(LANES)],
        out_specs=row(ODD_IN), out_shape=jax.ShapeDtypeStruct((s, ODD_IN), BF16),
        scratch_shapes=[pltpu.VMEM((N_PAIRS, tm, LANES), F32)],
        compiler_params=_params("parallel"),
    )(*dqs, *dks, *dvs, dz, *tabs)


def _fold_dout(dmix, name):
    s = dmix.shape[0]
    tm = 512
    ds = [d for d in DILATIONS if d > 1]

    def body(d_ref, *rest):
        outs, sc_ref = rest[:-1], rest[-1]
        _fold_store(d_ref[...], sc_ref, dict(zip(ds, outs)))

    return pl.pallas_call(
        body, name=name, grid=(s // tm,), in_specs=[pl.BlockSpec((tm, ATTN_W), lambda i: (i, 0))],
        out_specs=[_folded_spec(d, tm) for d in ds], out_shape=[_folded_shape(s, d, BF16) for d in ds],
        scratch_shapes=[pltpu.VMEM((N_PAIRS, tm, LANES), F32)],
        compiler_params=_params("parallel"),
    )(dmix)


def _window(j, i, tq):
    r0 = j * tq + i * BLOCK
    if i > 0:
        return pl.ds(pl.multiple_of(r0 - BLOCK, BLOCK), 2 * BLOCK), BLOCK
    start = pl.multiple_of(jnp.maximum(r0 - BLOCK, 0), BLOCK)
    return pl.ds(start, 2 * BLOCK), r0 - start


def _band_valid(offset, max_dist):
    shape = (2 * BLOCK, 2 * BLOCK)
    dist = (lax.bitwise_and(lax.broadcasted_iota(jnp.int32, shape, 0), BLOCK - 1)
            - lax.broadcasted_iota(jnp.int32, shape, 1) + offset)
    return jnp.abs(2 * dist - max_dist) <= max_dist


def _stack_heads(lo, x):
    zero = jnp.zeros_like(x)
    return jnp.concatenate([jnp.where(lo, x, zero), jnp.where(lo, zero, x)], axis=0)


def _unstack_heads(lo, x):
    return jnp.where(lo, x[:BLOCK], x[BLOCK:])


NT = (((1,), (1,)), ((), ()))
TN = (((0,), (0,)), ((), ()))


def _attn_fwd(q, k, v, sinks, *, max_dist, name, emit_bf16=False, o_dtype=F32):
    d, sp, wq = q.shape
    nq, nk = wq // LANES, k.shape[2] // LANES
    kdiv = nq // nk
    tq = min(sp, 1024)
    nsub = tq // BLOCK
    has_sink = sinks is not None

    def body(*refs):
        refs = list(refs)
        sink_ref = refs.pop(0) if has_sink else None
        q_ref, k_ref, v_ref, o_ref, lse_ref = refs[:5]
        pair = pl.program_id(1)
        j = pl.program_id(2)
        lo = _low_lanes(BLOCK)
        if has_sink:
            first_head = lax.broadcasted_iota(jnp.int32, (2 * BLOCK, 1), 0) < BLOCK
            sk = jnp.where(first_head, sink_ref[2 * pair], sink_ref[2 * pair + 1])
        for i in range(nsub):
            win, offset = _window(j, i, tq)
            rows = slice(i * BLOCK, (i + 1) * BLOCK)
            kw = k_ref[0, win, :]
            vw = v_ref[0, win, :]
            s = lax.dot_general(_stack_heads(lo, q_ref[0, rows, :]), kw, NT, preferred_element_type=F32) * ATTN_SCALE
            s = jnp.where(_band_valid(offset, max_dist), s, NEG)
            m = jnp.max(s, axis=-1, keepdims=True)
            if has_sink:
                m = jnp.maximum(m, sk)
            p = jnp.exp(s - m)
            l = jnp.sum(p, axis=-1, keepdims=True)
            if has_sink:
                l = l + jnp.exp(sk - m)
            o2 = _unstack_heads(lo, jnp.dot(p.astype(BF16), vw, preferred_element_type=F32) / l)
            o_ref[0, rows, :] = o2.astype(o_ref.dtype)
            lse_ref[0, rows, :] = _unstack_heads(lo, m + jnp.log(l))
            if emit_bf16:
                refs[5][0, rows, :] = o2.astype(BF16)

    qspec = pl.BlockSpec((1, tq, LANES), lambda r, p, j: (r, j, p))
    kspec = pl.BlockSpec((1, sp, LANES), lambda r, p, j: (r, 0, p // kdiv))
    in_specs = [qspec, kspec, kspec]
    operands = [q, k, v]
    if has_sink:
        in_specs = [pl.BlockSpec(memory_space=pltpu.SMEM)] + in_specs
        operands = [sinks] + operands
    out_shape = [jax.ShapeDtypeStruct(q.shape, o_dtype), jax.ShapeDtypeStruct(q.shape, F32)]
    if emit_bf16:
        out_shape.append(jax.ShapeDtypeStruct(q.shape, BF16))
    return pl.pallas_call(
        body, name=name, grid=(d, nq, sp // tq), in_specs=in_specs, out_specs=[qspec] * len(out_shape),
        out_shape=out_shape, compiler_params=_params("parallel", "parallel", "arbitrary"),
    )(*operands)


def _attn_bwd(q, k, v, do, oo, lse, sinks, *, max_dist, name):
    d, sp, wq = q.shape
    wk = k.shape[2]
    nq, nk = wq // LANES, wk // LANES
    kdiv = nq // nk
    tq = min(sp, 1024)
    nsub = tq // BLOCK
    has_sink = sinks is not None

    def body(*refs):
        refs = list(refs)
        sink_ref = refs.pop(0) if has_sink else None
        q_ref, k_ref, v_ref, do_ref, oo_ref, lse_ref, dq_ref, dk_out, dv_out = refs[:9]
        dk_ref, dv_ref = refs[-2:]
        pk, g, j = pl.program_id(1), pl.program_id(2), pl.program_id(3)

        @pl.when((g == 0) & (j == 0))
        def _():
            dk_ref[...] = jnp.zeros_like(dk_ref)
            dv_ref[...] = jnp.zeros_like(dv_ref)

        lo = _low_lanes(BLOCK)
        if has_sink:
            first_head = lax.broadcasted_iota(jnp.int32, (2 * BLOCK, 1), 0) < BLOCK
            pair = pk * kdiv + g
            sk = jnp.where(first_head, sink_ref[2 * pair], sink_ref[2 * pair + 1])
            sink_acc = jnp.zeros((2 * BLOCK, LANES), F32)
        for i in range(nsub):
            win, offset = _window(j, i, tq)
            rows = slice(i * BLOCK, (i + 1) * BLOCK)
            kw = k_ref[0, win, :]
            vw = v_ref[0, win, :]
            do2 = do_ref[0, rows, :].astype(F32)
            qs = _stack_heads(lo, q_ref[0, rows, :])
            dos = _stack_heads(lo, do2.astype(BF16))
            prod = do2 * oo_ref[0, rows, :]
            delta = jnp.sum(_stack_heads(lo, prod), axis=-1, keepdims=True)
            lse2 = lse_ref[0, rows, :]
            lse_swapped = pltpu.roll(lse2, HEAD_DIM, 1)
            lse_st = jnp.concatenate([jnp.where(lo, lse2, lse_swapped), jnp.where(lo, lse_swapped, lse2)], axis=0)
            s = lax.dot_general(qs, kw, NT, preferred_element_type=F32) * ATTN_SCALE
            s = jnp.where(_band_valid(offset, max_dist), s, NEG)
            p = jnp.exp(s - jnp.tile(lse_st, (1, 2)))
            dv_ref[win, :] = lax.dot_general(p.astype(BF16), dos, TN, preferred_element_type=F32) + dv_ref[win, :]
            dp = lax.dot_general(dos, vw, NT, preferred_element_type=F32)
            ds = (p * (dp - delta) * ATTN_SCALE).astype(BF16)
            dq_ref[0, rows, :] = _unstack_heads(lo, jnp.dot(ds, kw, preferred_element_type=F32)).astype(BF16)
            dk_ref[win, :] = lax.dot_general(ds, qs, TN, preferred_element_type=F32) + dk_ref[win, :]
            if has_sink:
                sink_acc = sink_acc - jnp.exp(sk - lse_st) * delta

        @pl.when((g == kdiv - 1) & (j == sp // tq - 1))
        def _():
            dk_out[0] = dk_ref[...].astype(BF16)
            dv_out[0] = dv_ref[...].astype(BF16)

        if has_sink:
            dsink_ref = refs[9]

            @pl.when(j == 0)
            def _():
                dsink_ref[...] = jnp.zeros_like(dsink_ref)

            dsink_ref[0] += jnp.where(lo[0:1], jnp.sum(sink_acc[:BLOCK], axis=0, keepdims=True),
                                      jnp.sum(sink_acc[BLOCK:], axis=0, keepdims=True))

    def qmap(r, pk, g, j):
        return (r, j, pk * kdiv + g)

    def kmap(r, pk, g, j):
        return (r, 0, pk)

    qspec = pl.BlockSpec((1, tq, LANES), qmap)
    kspec = pl.BlockSpec((1, sp, LANES), kmap)
    in_specs = [qspec, kspec, kspec, qspec, qspec, qspec]
    operands = [q, k, v, do, oo, lse]
    out_specs = [qspec, kspec, kspec]
    out_shape = [jax.ShapeDtypeStruct((d, sp, wq), BF16), jax.ShapeDtypeStruct((d, sp, wk), BF16),
                 jax.ShapeDtypeStruct((d, sp, wk), BF16)]
    if has_sink:
        in_specs = [pl.BlockSpec(memory_space=pltpu.SMEM)] + in_specs
        operands = [sinks] + operands
        out_specs.append(pl.BlockSpec((1, 1, LANES), lambda r, pk, g, j: (pk * kdiv + g, 0, 0)))
        out_shape.append(jax.ShapeDtypeStruct((nq, 1, LANES), F32))
    nsteps = sp // tq
    return pl.pallas_call(
        body, name=name, grid=(d, nk, kdiv, nsteps), in_specs=in_specs, out_specs=out_specs, out_shape=out_shape,
        scratch_shapes=[pltpu.VMEM((sp, LANES), F32), pltpu.VMEM((sp, LANES), F32)],
        compiler_params=_params("parallel", "parallel", "arbitrary", "arbitrary"),
    )(*operands)


def _combine(outs, lses, name):
    s = outs[0].shape[1]
    tm = 512
    nb = len(DILATIONS)
    ds = [d for d in DILATIONS if d > 1]

    def body(*refs):
        o_refs, l_refs = refs[:nb], refs[nb:2 * nb]
        cb_ref, c_ref, lse_ref = refs[2 * nb:2 * nb + 3]
        folded = refs[2 * nb + 3:2 * nb + 3 + 2 * len(ds)]
        scratch = refs[2 * nb + 3 + 2 * len(ds):]
        so = {1: None}
        sl = {1: None}
        for i, d in enumerate(ds):
            so[d], sl[d] = scratch[2 * i], scratch[2 * i + 1]
            _unfold_load(o_refs[1 + i], so[d], d)
            _unfold_load(l_refs[1 + i], sl[d], d)
        for p in range(N_PAIRS):
            pb = _pair_block(p)
            ls = [l_refs[0][0, :, pb]] + [sl[d][p] for d in ds]
            os_ = [o_refs[0][0, :, pb].astype(F32)] + [so[d][p] for d in ds]
            m = ls[0]
            for t in ls[1:]:
                m = jnp.maximum(m, t)
            ws = [jnp.exp(t - m) for t in ls]
            tot = ws[0]
            for t in ws[1:]:
                tot = tot + t
            acc = ws[0] * os_[0]
            for w, o in zip(ws[1:], os_[1:]):
                acc = acc + w * o
            cmix = acc / tot
            lse = m + jnp.log(tot)
            cb_ref[:, pb] = cmix.astype(BF16)
            c_ref[0, :, pb] = cmix
            lse_ref[0, :, pb] = lse
            so[ds[0]][p] = cmix
            sl[ds[0]][p] = lse
        for i, d in enumerate(ds):
            for r in range(d):
                for p in range(N_PAIRS):
                    rows = pl.ds(r, tm // d, stride=d)
                    folded[2 * i][r, :, _pair_block(p)] = so[ds[0]][p, rows, :]
                    folded[2 * i + 1][r, :, _pair_block(p)] = sl[ds[0]][p, rows, :]

    in_specs = [_folded_spec(d, tm) for _ in range(2) for d in DILATIONS]
    out_specs = [pl.BlockSpec((tm, ATTN_W), lambda i: (i, 0)), _folded_spec(1, tm), _folded_spec(1, tm)]
    out_shape = [jax.ShapeDtypeStruct((s, ATTN_W), BF16), _folded_shape(s, 1, F32), _folded_shape(s, 1, F32)]
    for d in ds:
        out_specs += [_folded_spec(d, tm)] * 2
        out_shape += [_folded_shape(s, d, F32)] * 2
    return pl.pallas_call(
        body, name=name, grid=(s // tm,), in_specs=in_specs, out_specs=out_specs, out_shape=out_shape,
        scratch_shapes=[pltpu.VMEM((N_PAIRS, tm, LANES), F32)] * (2 * len(ds)),
        compiler_params=_params("parallel"),
    )(*outs, *lses)


GLU_A = slice(768, 1280)
GLU_B = slice(1280, 1792)
EVEN_IN = 1792
ODD_IN = 2560
CONV_CH = 512


def _shifted_copies(xs_ref):
    rows = xs_ref.shape[1] - 8
    for b in range(1, 8):
        xs_ref[b, 0:rows, :] = xs_ref[0, pl.ds(b, rows), :]


def _shifted_rows(xs_ref, start):
    return xs_ref[start % 8, pl.ds(start - start % 8, CONV_ROWS), :]


def _glu(p_ref):
    return p_ref[:, GLU_A].astype(F32) * _sigmoid(p_ref[:, GLU_B].astype(F32))


def _conv_fwd(proj, w, b, ln_g, ln_b, name):
    s = proj.shape[0]
    tm = 512
    nh = tm // CONV_HALO
    lead = CONV_HALO - (CONV_WIDTH - 1)

    def body(p_ref, ph_ref, w_ref, b_ref, g_ref, bb_ref, y_ref, o_ref, xs_ref):
        xs_ref[0, CONV_HALO:, :] = _glu(p_ref)
        xs_ref[0, 0:CONV_HALO, :] = jnp.where(pl.program_id(0) > 0, _glu(ph_ref), 0.0)
        _shifted_copies(xs_ref)
        for c0 in range(0, tm, CONV_ROWS):
            acc = jnp.zeros((CONV_ROWS, CONV_CH), F32) + b_ref[...]
            for j in range(CONV_WIDTH):
                acc = acc + _shifted_rows(xs_ref, lead + j + c0) * w_ref[j:j + 1, :]
            y_ref[c0:c0 + CONV_ROWS, :] = acc
            mu = jnp.mean(acc, axis=-1, keepdims=True)
            xc = acc - mu
            var = jnp.mean(xc * xc, axis=-1, keepdims=True)
            zz = xc * lax.rsqrt(var + LN_EPS) * g_ref[...] + bb_ref[...]
            o_ref[c0:c0 + CONV_ROWS, :] = (zz * _sigmoid(zz)).astype(BF16)

    def const(a):
        return pl.BlockSpec(a.shape, lambda i: (0, 0))

    return pl.pallas_call(
        body, name=name, grid=(s // tm,),
        in_specs=[pl.BlockSpec((tm, EVEN_IN), lambda i: (i, 0)),
                  pl.BlockSpec((CONV_HALO, EVEN_IN), lambda i: (jnp.maximum(i * nh - 1, 0), 0)),
                  const(w), const(b), const(ln_g), const(ln_b)],
        out_specs=[pl.BlockSpec((tm, CONV_CH), lambda i: (i, 0)), pl.BlockSpec((tm, CONV_CH), lambda i: (i, 0))],
        out_shape=[jax.ShapeDtypeStruct((s, CONV_CH), F32), jax.ShapeDtypeStruct((s, CONV_CH), BF16)],
        scratch_shapes=[pltpu.VMEM((8, tm + CONV_HALO, CONV_CH), F32)],
        compiler_params=_params("arbitrary"),
    )(proj, proj, w, b, ln_g, ln_b)


def _conv_tail_bwd(dmix, yconv, ln_g, ln_b, name):
    def body(d_ref, y_ref, g_ref, b_ref, dy_ref, dg_ref, db_ref, dcb_ref):
        @pl.when(_first_step())
        def _():
            dg_ref[...] = jnp.zeros_like(dg_ref)
            db_ref[...] = jnp.zeros_like(db_ref)
            dcb_ref[...] = jnp.zeros_like(dcb_ref)

        y = y_ref[...]
        g = g_ref[...]
        mu = jnp.mean(y, axis=-1, keepdims=True)
        xc = y - mu
        rstd = lax.rsqrt(jnp.mean(xc * xc, axis=-1, keepdims=True) + LN_EPS)
        xh = xc * rstd
        zz = xh * g + b_ref[...]
        sg = _sigmoid(zz)
        dzz = d_ref[:, CONV_CH:] * sg * (1.0 + zz * (1.0 - sg))
        dg_ref[...] += jnp.sum(dzz * xh, axis=0, keepdims=True)
        db_ref[...] += jnp.sum(dzz, axis=0, keepdims=True)
        dxh = dzz * g
        dy = rstd * (dxh - jnp.mean(dxh, axis=-1, keepdims=True) - xh * jnp.mean(dxh * xh, axis=-1, keepdims=True))
        dcb_ref[...] += jnp.sum(dy, axis=0, keepdims=True)
        dy_ref[...] = dy

    vec = ((1, CONV_CH), F32)
    return _rows(body, name, 512, [dmix, yconv], [ln_g, ln_b], [(CONV_CH, F32)], [vec, vec, vec])


def _conv_bwd(proj, dy, w, name):
    s = proj.shape[0]
    tm = 512
    nh = tm // CONV_HALO
    nsteps = s // tm
    lead = CONV_HALO - (CONV_WIDTH - 1)

    def body(p_ref, ph_ref, dy_ref, dyn_ref, w_ref, dglu_ref, dw_ref, xf_ref, dyf_ref, part_ref):
        i = pl.program_id(0)

        @pl.when(i == 0)
        def _():
            dw_ref[...] = jnp.zeros_like(dw_ref)

        ga = p_ref[:, GLU_A].astype(F32)
        sgb = _sigmoid(p_ref[:, GLU_B].astype(F32))
        xf_ref[0, CONV_HALO:, :] = ga * sgb
        xf_ref[0, 0:CONV_HALO, :] = jnp.where(i > 0, _glu(ph_ref), 0.0)
        _shifted_copies(xf_ref)
        dyf_ref[0, 0:tm, :] = dy_ref[...]
        dyf_ref[0, tm:, :] = jnp.where(i < nsteps - 1, dyn_ref[...], 0.0)
        _shifted_copies(dyf_ref)
        for c0 in range(0, tm, CONV_ROWS):
            rows = slice(c0, c0 + CONV_ROWS)
            acc = jnp.zeros((CONV_ROWS, CONV_CH), F32)
            for j in range(CONV_WIDTH):
                acc = acc + _shifted_rows(dyf_ref, CONV_WIDTH - 1 - j + c0) * w_ref[j:j + 1, :]
            a_c, s_c = ga[rows, :], sgb[rows, :]
            dglu_ref[rows, 0:CONV_CH] = (acc * s_c).astype(BF16)
            dglu_ref[rows, CONV_CH:] = (acc * a_c * s_c * (1.0 - s_c)).astype(BF16)
        for c0 in range(0, tm, CONV_ROWS):
            dy_c = dy_ref[c0:c0 + CONV_ROWS, :]
            for j in range(CONV_WIDTH):
                prod = dy_c * _shifted_rows(xf_ref, lead + j + c0)
                part = jnp.sum(prod.reshape(CONV_ROWS // 8, 8, CONV_CH), axis=0)
                part_ref[j] = part if c0 == 0 else part + part_ref[j]
        for j in range(CONV_WIDTH):
            dw_ref[j:j + 1, :] += jnp.sum(part_ref[j], axis=0, keepdims=True)

    return pl.pallas_call(
        body, name=name, grid=(nsteps,),
        in_specs=[pl.BlockSpec((tm, EVEN_IN), lambda i: (i, 0)),
                  pl.BlockSpec((CONV_HALO, EVEN_IN), lambda i: (jnp.maximum(i * nh - 1, 0), 0)),
                  pl.BlockSpec((tm, CONV_CH), lambda i: (i, 0)),
                  pl.BlockSpec((CONV_HALO, CONV_CH), lambda i: (jnp.minimum((i + 1) * nh, s // CONV_HALO - 1), 0)),
                  pl.BlockSpec(w.shape, lambda i: (0, 0))],
        out_specs=[pl.BlockSpec((tm, 2 * CONV_CH), lambda i: (i, 0)), pl.BlockSpec(w.shape, lambda i: (0, 0))],
        out_shape=[jax.ShapeDtypeStruct((s, 2 * CONV_CH), BF16), jax.ShapeDtypeStruct(w.shape, F32)],
        scratch_shapes=[pltpu.VMEM((8, tm + CONV_HALO, CONV_CH), F32), pltpu.VMEM((8, tm + CONV_HALO, CONV_CH), F32),
                        pltpu.VMEM((CONV_WIDTH, 8, CONV_CH), F32)],
        compiler_params=_params("arbitrary"),
    )(proj, proj, dy, dy, w)


GATE_Z = slice(1536, 2560)
D_CH = 512
GELU_C = math.sqrt(2.0 / math.pi)
GELU_K = 0.044715


def _gelu_parts(z):
    t = jnp.tanh(GELU_C * (z + GELU_K * z * z * z))
    return 0.5 * z * (1.0 + t), t


def _lane_group(rows):
    return lax.broadcasted_iota(jnp.int32, (rows, D_CH), 1) // HEAD_DIM


def _tril_mask():
    return lax.broadcasted_iota(jnp.int32, (BLOCK, BLOCK), 0) >= lax.broadcasted_iota(jnp.int32, (BLOCK, BLOCK), 1)


def _layer_norm_parts(x):
    mu = jnp.mean(x, axis=-1, keepdims=True)
    xc = x - mu
    rstd = lax.rsqrt(jnp.mean(xc * xc, axis=-1, keepdims=True) + LN_EPS)
    return xc * rstd, rstd


def _gate_fwd(proj, ln_g, ln_b, w_sp, sb_t, name):
    tm = 512

    def body(p_ref, g_ref, b_ref, w_ref, sb_ref, mixed_ref, out_ref):
        zz, _ = _gelu_parts(p_ref[:, GATE_Z].astype(F32))
        u = zz[:, :D_CH]
        xh, _ = _layer_norm_parts(zz[:, D_CH:])
        gn = (xh * g_ref[...] + b_ref[...]).astype(BF16)
        grp = _lane_group(BLOCK)
        tri = _tril_mask()
        ws = [jnp.where(tri, w_ref[gi], 0.0).astype(BF16) for gi in range(N_GROUPS)]
        bias = jnp.zeros((BLOCK, D_CH), F32)
        for gi in range(N_GROUPS):
            bias = jnp.where(grp == gi, sb_ref[:, gi:gi + 1], bias)
        for ch in range(tm // BLOCK):
            rows = slice(ch * BLOCK, (ch + 1) * BLOCK)
            gc = gn[rows, :]
            mixed = bias
            for gi in range(N_GROUPS):
                r = jnp.dot(ws[gi], gc, preferred_element_type=F32)
                mixed = jnp.where(grp == gi, r + bias, mixed)
            mixed_ref[rows, :] = mixed
            out_ref[rows, :] = (u[rows, :] * mixed).astype(BF16)

    return _rows(body, name, tm, [proj], [ln_g, ln_b, w_sp, sb_t], [(D_CH, F32), (D_CH, BF16)])


def _gate_bwd(dmix, proj, mixed, ln_g, ln_b, w_sp, name):
    tm = 512

    def body(d_ref, p_ref, m_ref, g_ref, b_ref, w_ref, dz_ref, dg_ref, db_ref, dw_ref, dsb_ref, dgn_ref):
        @pl.when(_first_step())
        def _():
            dg_ref[...] = jnp.zeros_like(dg_ref)
            db_ref[...] = jnp.zeros_like(db_ref)
            dw_ref[...] = jnp.zeros_like(dw_ref)
            dsb_ref[...] = jnp.zeros_like(dsb_ref)

        z = p_ref[:, GATE_Z].astype(F32)
        zz, t = _gelu_parts(z)
        u = zz[:, :D_CH]
        xh, rstd = _layer_norm_parts(zz[:, D_CH:])
        g = g_ref[...]
        gn = (xh * g + b_ref[...]).astype(BF16)
        dd = d_ref[:, D_CH:]
        du = dd * m_ref[...]
        dm = dd * u
        grp = _lane_group(BLOCK)
        tri = _tril_mask()
        ws = [jnp.where(tri, w_ref[gi], 0.0).astype(BF16) for gi in range(N_GROUPS)]
        gsel = (lax.broadcasted_iota(jnp.int32, (N_GROUPS, D_CH), 1) // HEAD_DIM
                == lax.broadcasted_iota(jnp.int32, (N_GROUPS, D_CH), 0)).astype(F32)
        for ch in range(tm // BLOCK):
            rows = slice(ch * BLOCK, (ch + 1) * BLOCK)
            dmc = dm[rows, :]
            dmb = dmc.astype(BF16)
            gc = gn[rows, :]
            dgn = jnp.zeros((BLOCK, D_CH), F32)
            for gi in range(N_GROUPS):
                r = lax.dot_general(ws[gi], dmb, TN, preferred_element_type=F32)
                dgn = jnp.where(grp == gi, r, dgn)
                dmg = jnp.where(grp == gi, dmb, jnp.zeros_like(dmb))
                dwg = lax.dot_general(dmg, gc, NT, preferred_element_type=F32)
                dw_ref[gi] += jnp.where(tri, dwg, 0.0)
            dsb_ref[...] += lax.dot_general(gsel, dmc, NT, preferred_element_type=F32, precision=lax.Precision.HIGHEST)
            dgn_ref[rows, :] = dgn
        dgn = dgn_ref[...]
        db_ref[...] += jnp.sum(dgn, axis=0, keepdims=True)
        dg_ref[...] += jnp.sum(dgn * xh, axis=0, keepdims=True)
        dxh = dgn * g
        dgp = rstd * (dxh - jnp.mean(dxh, axis=-1, keepdims=True) - xh * jnp.mean(dxh * xh, axis=-1, keepdims=True))
        dgelu = 0.5 * (1.0 + t) + 0.5 * z * (1.0 - t * t) * GELU_C * (1.0 + 3.0 * GELU_K * z * z)
        dz_ref[:, 0:D_CH] = (du * dgelu[:, :D_CH]).astype(BF16)
        dz_ref[:, D_CH:] = (dgp * dgelu[:, D_CH:]).astype(BF16)

    s = proj.shape[0]
    tiled = [dmix, proj, mixed]
    consts = [ln_g, ln_b, w_sp]
    in_specs = [pl.BlockSpec((tm, a.shape[1]), lambda i: (i, 0)) for a in tiled]
    in_specs += [pl.BlockSpec(a.shape, lambda i, nd=a.ndim: (0,) * nd) for a in consts]
    vec = (1, D_CH)
    acc_shapes = [vec, vec, w_sp.shape, (N_GROUPS, BLOCK)]
    return pl.pallas_call(
        body, name=name, grid=(s // tm,), in_specs=in_specs,
        out_specs=[pl.BlockSpec((tm, 2 * D_CH), lambda i: (i, 0))]
        + [pl.BlockSpec(sh, lambda i, nd=len(sh): (0,) * nd) for sh in acc_shapes],
        out_shape=[jax.ShapeDtypeStruct((s, 2 * D_CH), BF16)] + [jax.ShapeDtypeStruct(sh, F32) for sh in acc_shapes],
        scratch_shapes=[pltpu.VMEM((tm, D_CH), F32)],
        compiler_params=_params("arbitrary"),
    )(*tiled, *consts)


def _adam_update(w, g, m, v):
    nm = ADAM_B1 * m + (1.0 - ADAM_B1) * g
    nv = ADAM_B2 * v + (1.0 - ADAM_B2) * (g * g)
    m_hat = nm / (1.0 - ADAM_B1 ** ADAM_STEP)
    v_hat = nv / (1.0 - ADAM_B2 ** ADAM_STEP)
    return -ADAM_LR * (m_hat / (jnp.sqrt(v_hat) + ADAM_EPS) + ADAM_WD * w), nm, nv


def _adamw(w, g, m, v, name):
    rows, cols = w.shape
    tm = _tile(rows, 512, 8) if rows % 8 == 0 else rows

    def body(w_ref, g_ref, m_ref, v_ref, d_ref, nm_ref, nv_ref):
        d_ref[...], nm_ref[...], nv_ref[...] = _adam_update(w_ref[...], g_ref[...], m_ref[...], v_ref[...])

    return _rows(body, name, tm, [w, g, m, v], [], [(cols, F32)] * 3)


def _ordered_sum(parts, name):
    n, rows, cols = parts.shape
    tm = _tile(rows, 512, 16 if parts.dtype == BF16 else 8)

    def body(p_ref, o_ref):
        acc = p_ref[0].astype(F32)
        for k in range(1, n):
            acc = acc + p_ref[k].astype(F32)
        o_ref[...] = acc

    return pl.pallas_call(body, name=name, grid=(rows // tm,),
                          in_specs=[pl.BlockSpec((n, tm, cols), lambda i: (0, i, 0))],
                          out_specs=pl.BlockSpec((tm, cols), lambda i: (i, 0)),
                          out_shape=jax.ShapeDtypeStruct((rows, cols), F32), compiler_params=_params("parallel"))(parts)


ANY = pl.BlockSpec(memory_space=pl.ANY)


def _position():
    x, y, c = lax.axis_index("x"), lax.axis_index("y"), lax.axis_index("c")
    other_chips = [(1 - x, y), (x, 1 - y), (1 - x, 1 - y)]
    return x, y, c, other_chips


def _remote(src, dst, send_sem, recv_sem, to):
    return pltpu.make_async_remote_copy(src_ref=src, dst_ref=dst, send_sem=send_sem, recv_sem=recv_sem,
                                        device_id=to, device_id_type=MESH)


STAGE_ROWS = 736


def _staged_copies(copies, buf, in_sems, out_sems):
    n = len(copies)

    def into(u):
        src = copies[u][0]
        return pltpu.make_async_copy(src, buf.at[u % 2, pl.ds(0, src.shape[0]), :], in_sems.at[u % 2])

    def out_of(u):
        dst = copies[u][1]
        return pltpu.make_async_copy(buf.at[u % 2, pl.ds(0, dst.shape[0]), :], dst, out_sems.at[u % 2])

    into(0).start()
    for u in range(n):
        into(u).wait()
        out_of(u).start()
        if u + 1 < n:
            if u >= 1:
                out_of(u - 1).wait()
            into(u + 1).start()
    if n >= 2:
        out_of(n - 2).wait()
    out_of(n - 1).wait()


def _stage_scratch(dtype, cols):
    return [pltpu.VMEM((2, STAGE_ROWS, cols), dtype), pltpu.SemaphoreType.DMA((2,)), pltpu.SemaphoreType.DMA((2,))]


def _row_chunks(rows):
    return [(r, min(STAGE_ROWS, rows - r)) for r in range(0, rows, STAGE_ROWS)]


def _gather_chips(shard, name):
    rows, cols = shard.shape
    half = rows // 2

    def body(in_ref, out_ref, send_sems, recv_sems, buf, in_sems, out_sems):
        x, y, c, chips = _position()
        me = 2 * x + y
        sibling = (x, y, 1 - c)

        def slab(chip, h):
            return out_ref.at[chip, pl.ds(h * half, half), :]

        first = [_remote(in_ref.at[pl.ds(c * half, half), :], slab(me, c), send_sems.at[j], recv_sems.at[j], (cx, cy, c))
                 for j, (cx, cy) in enumerate(chips)]
        for cp in first:
            cp.start()
        _staged_copies([(in_ref.at[pl.ds(r, n), :], out_ref.at[me, pl.ds(r, n), :]) for r, n in _row_chunks(rows)],
                       buf, in_sems, out_sems)
        passed = []
        for j, (cx, cy) in enumerate(chips):
            got = slab(2 * cx + cy, c)
            _remote(got, got, send_sems.at[j], recv_sems.at[j], sibling).wait_recv()
            cp = _remote(got, got, send_sems.at[3 + j], recv_sems.at[3 + j], sibling)
            cp.start()
            passed.append(cp)
        for j, (cx, cy) in enumerate(chips):
            got = slab(2 * cx + cy, 1 - c)
            _remote(got, got, send_sems.at[3 + j], recv_sems.at[3 + j], sibling).wait_recv()
        for cp in first + passed:
            cp.wait_send()

    return pl.pallas_call(
        body, name=name, in_specs=[ANY], out_specs=ANY,
        out_shape=jax.ShapeDtypeStruct((N_CHIPS, rows, cols), shard.dtype),
        scratch_shapes=[pltpu.SemaphoreType.DMA((6,)), pltpu.SemaphoreType.DMA((6,))] + _stage_scratch(shard.dtype, cols),
        compiler_params=pltpu.CompilerParams(vmem_limit_bytes=VMEM_LIMIT),
    )(shard)


HBM = pl.BlockSpec(memory_space=pltpu.HBM)
SEM = pl.BlockSpec(memory_space=pltpu.SEMAPHORE)
SIDE_EFFECT = pltpu.SideEffectType.DATAFLOW_SIDE_EFFECTING


def _ici_copies(in_ref, land_ref, send_sems, recv_sems, half):
    x, y, c, chips = _position()
    mine = pl.ds(c * half, half)
    sends = [_remote(in_ref.at[mine, :], land_ref.at[2 * x + y, mine, :], send_sems.at[j], recv_sems.at[j], (cx, cy, c))
             for j, (cx, cy) in enumerate(chips)]
    arrivals = [_remote(in_ref.at[mine, :], land_ref.at[2 * cx + cy, mine, :], send_sems.at[j], recv_sems.at[j], (cx, cy, c))
                for j, (cx, cy) in enumerate(chips)]
    return sends, arrivals


def _gather_start(shard, after, name):
    rows, cols = shard.shape

    def body(in_ref, land_ref, after_ref, send_sems, recv_sems, in_thru, land_thru, token):
        sends, _ = _ici_copies(in_ref, land_ref, send_sems, recv_sems, rows // 2)
        for cp in sends:
            cp.start()
        token[...] = jnp.zeros_like(token)

    land = lax.empty((N_CHIPS, rows, cols), shard.dtype)
    return pl.pallas_call(
        body, name=name,
        out_shape=(pltpu.SemaphoreType.DMA((3,)), pltpu.SemaphoreType.DMA((3,)), pltpu.HBM(shard.shape, shard.dtype),
                   pltpu.HBM(land.shape, land.dtype), jax.ShapeDtypeStruct((8, LANES), F32)),
        in_specs=(HBM, HBM, ANY), out_specs=(SEM, SEM, HBM, HBM, pl.BlockSpec(memory_space=pltpu.VMEM)),
        input_output_aliases={0: 2, 1: 3},
        compiler_params=pltpu.CompilerParams(has_side_effects=SIDE_EFFECT),
    )(pltpu.with_memory_space_constraint(shard, pltpu.HBM), pltpu.with_memory_space_constraint(land, pltpu.HBM), after)


def _gather_wait(send_sems, recv_sems, shard, land, after, name):
    rows = shard.shape[0]

    def body(in_ref, land_ref, send_sems, recv_sems, after_ref, in_out, land_out):
        sends, arrivals = _ici_copies(in_ref, land_ref, send_sems, recv_sems, rows // 2)
        for cp in sends:
            cp.wait_send()
        for cp in arrivals:
            cp.wait_recv()

    return pl.pallas_call(
        body, name=name, out_shape=(pltpu.HBM(shard.shape, shard.dtype), pltpu.HBM(land.shape, land.dtype)),
        in_specs=(HBM, HBM, SEM, SEM, ANY), out_specs=(HBM, HBM), input_output_aliases={0: 0, 1: 1},
        compiler_params=pltpu.CompilerParams(has_side_effects=SIDE_EFFECT),
    )(shard, land, send_sems, recv_sems, after)


def _gather_finish(shard, land, name):
    rows, cols = shard.shape
    half = rows // 2

    def body(in_ref, land_ref, out_ref, send_sems, recv_sems, buf, in_sems, out_sems):
        x, y, c, chips = _position()
        me = 2 * x + y
        sibling = (x, y, 1 - c)

        def slab(chip, h):
            return out_ref.at[chip, pl.ds(h * half, half), :]

        passed = [_remote(slab(2 * cx + cy, c), slab(2 * cx + cy, c), send_sems.at[j], recv_sems.at[j], sibling)
                  for j, (cx, cy) in enumerate(chips)]
        for cp in passed:
            cp.start()
        _staged_copies([(in_ref.at[pl.ds(r, n), :], out_ref.at[me, pl.ds(r, n), :]) for r, n in _row_chunks(rows)],
                       buf, in_sems, out_sems)
        for j, (cx, cy) in enumerate(chips):
            got = slab(2 * cx + cy, 1 - c)
            _remote(got, got, send_sems.at[j], recv_sems.at[j], sibling).wait_recv()
        for cp in passed:
            cp.wait_send()

    return pl.pallas_call(
        body, name=name, in_specs=[ANY, ANY], out_specs=ANY, out_shape=jax.ShapeDtypeStruct(land.shape, land.dtype),
        input_output_aliases={1: 0},
        scratch_shapes=[pltpu.SemaphoreType.DMA((3,)), pltpu.SemaphoreType.DMA((3,))] + _stage_scratch(shard.dtype, cols),
        compiler_params=pltpu.CompilerParams(vmem_limit_bytes=VMEM_LIMIT),
    )(shard, land)


def _gather_devices(block, name):
    rows, cols = block.shape

    def body(in_ref, out_ref, send_sems, recv_sems, local_sem):
        x, y, c, chips = _position()
        sibling = (x, y, 1 - c)

        def slot(px, py, pc):
            return out_ref.at[4 * px + 2 * py + pc]

        mine = pltpu.make_async_copy(in_ref, slot(x, y, c), local_sem)
        mine.start()
        first = [_remote(in_ref, slot(x, y, c), send_sems.at[0], recv_sems.at[0], sibling)]
        first += [_remote(in_ref, slot(x, y, c), send_sems.at[1 + j], recv_sems.at[1 + j], (cx, cy, c))
                  for j, (cx, cy) in enumerate(chips)]
        for cp in first:
            cp.start()
        passed = []
        for j, (cx, cy) in enumerate(chips):
            got = slot(cx, cy, c)
            _remote(got, got, send_sems.at[1 + j], recv_sems.at[1 + j], sibling).wait_recv()
            cp = _remote(got, got, send_sems.at[4 + j], recv_sems.at[4 + j], sibling)
            cp.start()
            passed.append(cp)
        got = slot(x, y, 1 - c)
        _remote(got, got, send_sems.at[0], recv_sems.at[0], sibling).wait_recv()
        for j, (cx, cy) in enumerate(chips):
            got = slot(cx, cy, 1 - c)
            _remote(got, got, send_sems.at[4 + j], recv_sems.at[4 + j], sibling).wait_recv()
        for cp in first + passed:
            cp.wait_send()
        mine.wait()

    return pl.pallas_call(
        body, name=name, in_specs=[ANY], out_specs=ANY,
        out_shape=jax.ShapeDtypeStruct((N_DEV, rows, cols), block.dtype),
        scratch_shapes=[pltpu.SemaphoreType.DMA((7,)), pltpu.SemaphoreType.DMA((7,)), pltpu.SemaphoreType.DMA],
    )(block)


def _pair_send(grads, name):
    n = len(grads)
    hs = [g.shape[2] for g in grads]
    offs = [sum(hs[:i]) for i in range(n)]
    cols = grads[0].shape[3]

    def body(*refs):
        g_refs = refs[:n]
        got_ref, send_sems, recv_sems = refs[n:]
        x, y, c, _ = _position()
        copies = [_remote(g_ref.at[:, 1 - c], got_ref.at[:, pl.ds(offs[i], hs[i]), :], send_sems.at[i], recv_sems.at[i],
                          (x, y, 1 - c)) for i, g_ref in enumerate(g_refs)]
        for cp in copies:
            cp.start()
        for cp in copies:
            cp.wait()

    return pl.pallas_call(
        body, name=name, in_specs=[ANY] * n, out_specs=ANY, out_shape=jax.ShapeDtypeStruct((N_CHIPS, sum(hs), cols), F32),
        scratch_shapes=[pltpu.SemaphoreType.DMA((n,)), pltpu.SemaphoreType.DMA((n,))],
    )(*grads)


def _pair_copies(g_refs, land_ref, send_sems, recv_sems):
    x, y, c, _ = _position()
    hs = [g.shape[2] for g in g_refs]
    offs = [sum(hs[:i]) for i in range(len(hs))]
    return [_remote(g_ref.at[:, 1 - c], land_ref.at[:, pl.ds(offs[i], hs[i]), :], send_sems.at[i], recv_sems.at[i],
                    (x, y, 1 - c)) for i, g_ref in enumerate(g_refs)]


def _pair_send_start(grads, name):
    n = len(grads)
    land = lax.empty((N_CHIPS, sum(g.shape[2] for g in grads), grads[0].shape[3]), F32)

    def body(*refs):
        for cp in _pair_copies(refs[:n], refs[n], refs[n + 1], refs[n + 2]):
            cp.start()
        refs[-1][...] = jnp.zeros_like(refs[-1])

    buffers = [*grads, land]
    return pl.pallas_call(
        body, name=name,
        out_shape=(pltpu.SemaphoreType.DMA((n,)), pltpu.SemaphoreType.DMA((n,)),
                   *[pltpu.HBM(b.shape, b.dtype) for b in buffers], jax.ShapeDtypeStruct((8, LANES), F32)),
        in_specs=(HBM,) * (n + 1), out_specs=(SEM, SEM, *(HBM,) * (n + 1), pl.BlockSpec(memory_space=pltpu.VMEM)),
        input_output_aliases={i: 2 + i for i in range(n + 1)},
        compiler_params=pltpu.CompilerParams(has_side_effects=SIDE_EFFECT),
    )(*[pltpu.with_memory_space_constraint(b, pltpu.HBM) for b in buffers])


def _pair_send_wait(send_sems, recv_sems, buffers, after, name):
    n = len(buffers) - 1

    def body(*refs):
        for cp in _pair_copies(refs[:n], refs[n], refs[n + 1], refs[n + 2]):
            cp.wait_send()
            cp.wait_recv()

    return pl.pallas_call(
        body, name=name, out_shape=tuple(pltpu.HBM(b.shape, b.dtype) for b in buffers),
        in_specs=(*(HBM,) * (n + 1), SEM, SEM, ANY), out_specs=(HBM,) * (n + 1),
        input_output_aliases={i: i for i in range(n + 1)},
        compiler_params=pltpu.CompilerParams(has_side_effects=SIDE_EFFECT),
    )(*buffers, send_sems, recv_sems, after)


def _pair_add(grads, got, name):
    n = len(grads)
    hs = [g.shape[2] for g in grads]
    offs = [sum(hs[:i]) for i in range(n)]
    cols = grads[0].shape[3]
    hmax = max(hs)
    units = [(i, k) for k in range(N_CHIPS) for i in range(n)]

    def body(*refs):
        g_refs = refs[:n]
        got_ref, out_ref, a_buf, b_buf, o_buf, a_sems, b_sems, o_sems = refs[n:]
        c = lax.axis_index("c")

        def loads(u):
            i, k = units[u]
            slot, rows = u % 2, pl.ds(0, hs[i])
            return (pltpu.make_async_copy(g_refs[i].at[k, c], a_buf.at[slot, rows, :], a_sems.at[slot]),
                    pltpu.make_async_copy(got_ref.at[k, pl.ds(offs[i], hs[i]), :], b_buf.at[slot, rows, :], b_sems.at[slot]))

        def store(u):
            i, k = units[u]
            return pltpu.make_async_copy(o_buf.at[u % 2, pl.ds(0, hs[i]), :], out_ref.at[k, pl.ds(offs[i], hs[i]), :],
                                         o_sems.at[u % 2])

        for cp in loads(0):
            cp.start()
        for u, (i, k) in enumerate(units):
            if u + 1 < len(units):
                for cp in loads(u + 1):
                    cp.start()
            for cp in loads(u):
                cp.wait()
            if u >= 2:
                store(u - 2).wait()
            rows = pl.ds(0, hs[i])
            o_buf[u % 2, rows, :] = (a_buf[u % 2, rows, :] + b_buf[u % 2, rows, :]).astype(BF16)
            store(u).start()
        store(len(units) - 2).wait()
        store(len(units) - 1).wait()

    return pl.pallas_call(
        body, name=name, in_specs=[ANY] * (n + 1), out_specs=ANY,
        out_shape=jax.ShapeDtypeStruct((N_CHIPS, sum(hs), cols), BF16),
        scratch_shapes=[pltpu.VMEM((2, hmax, cols), F32), pltpu.VMEM((2, hmax, cols), F32), pltpu.VMEM((2, hmax, cols), BF16),
                        pltpu.SemaphoreType.DMA((2,)), pltpu.SemaphoreType.DMA((2,)), pltpu.SemaphoreType.DMA((2,))],
        compiler_params=pltpu.CompilerParams(vmem_limit_bytes=VMEM_LIMIT),
    )(*grads, got)


def _exchange_copies(in_ref, land_ref, send_sems, recv_sems):
    x, y, c, chips = _position()
    return [_remote(in_ref.at[2 * cx + cy], land_ref.at[j], send_sems.at[j], recv_sems.at[j], (cx, cy, c))
            for j, (cx, cy) in enumerate(chips)]


def _exchange_start(parts, name):
    _, rows, cols = parts.shape

    def body(in_ref, land_ref, send_sems, recv_sems, in_thru, land_thru, token):
        for cp in _exchange_copies(in_ref, land_ref, send_sems, recv_sems):
            cp.start()
        token[...] = jnp.zeros_like(token)

    land = lax.empty((3, rows, cols), parts.dtype)
    return pl.pallas_call(
        body, name=name,
        out_shape=(pltpu.SemaphoreType.DMA((3,)), pltpu.SemaphoreType.DMA((3,)), pltpu.HBM(parts.shape, parts.dtype),
                   pltpu.HBM(land.shape, land.dtype), jax.ShapeDtypeStruct((8, LANES), F32)),
        in_specs=(HBM, HBM), out_specs=(SEM, SEM, HBM, HBM, pl.BlockSpec(memory_space=pltpu.VMEM)),
        input_output_aliases={0: 2, 1: 3},
        compiler_params=pltpu.CompilerParams(has_side_effects=SIDE_EFFECT),
    )(pltpu.with_memory_space_constraint(parts, pltpu.HBM), pltpu.with_memory_space_constraint(land, pltpu.HBM))


def _exchange_wait(send_sems, recv_sems, parts, land, after, name):
    def body(in_ref, land_ref, send_sems, recv_sems, after_ref, in_out, land_out):
        for cp in _exchange_copies(in_ref, land_ref, send_sems, recv_sems):
            cp.wait_send()
            cp.wait_recv()

    return pl.pallas_call(
        body, name=name, out_shape=(pltpu.HBM(parts.shape, parts.dtype), pltpu.HBM(land.shape, land.dtype)),
        in_specs=(HBM, HBM, SEM, SEM, ANY), out_specs=(HBM, HBM), input_output_aliases={0: 0, 1: 1},
        compiler_params=pltpu.CompilerParams(has_side_effects=SIDE_EFFECT),
    )(parts, land, send_sems, recv_sems, after)


def _chip_sum(parts, recv, chip, name):
    _, rows, cols = parts.shape
    tm = _tile(rows, 512, 16)

    def body(chip_ref, own_ref, recv_ref, o_ref):
        acc = own_ref[0].astype(F32)
        for j in range(3):
            acc = acc + recv_ref[j].astype(F32)
        o_ref[...] = acc

    return pl.pallas_call(
        body, name=name,
        grid_spec=pltpu.PrefetchScalarGridSpec(
            num_scalar_prefetch=1, grid=(rows // tm,),
            in_specs=[pl.BlockSpec((1, tm, cols), lambda i, chip_ref: (chip_ref[0], i, 0)),
                      pl.BlockSpec((3, tm, cols), lambda i, chip_ref: (0, i, 0))],
            out_specs=pl.BlockSpec((tm, cols), lambda i, chip_ref: (i, 0))),
        out_shape=jax.ShapeDtypeStruct((rows, cols), F32), compiler_params=_params("parallel"),
    )(chip, parts, recv)


def _join_unpack(mine, hs, groups, name):
    n = len(hs)
    offs = [sum(hs[:i]) for i in range(n)]
    cols = mine.shape[1]
    n_out = max(groups) + 1
    base = [2 * sum(h for h, g in zip(hs[:i], groups[:i]) if g == groups[i]) for i in range(n)]
    out_rows = [2 * sum(h for h, g in zip(hs, groups) if g == k) for k in range(n_out)]

    def body(in_ref, *refs):
        outs = refs[:n_out]
        send_sems, recv_sems, buf, in_sems, out_sems = refs[n_out:]
        x, y, c, _ = _position()
        sibling = (x, y, 1 - c)
        sent, local = [], []
        for i in range(n):
            src = in_ref.at[pl.ds(offs[i], hs[i]), :]
            here = outs[groups[i]].at[pl.ds(base[i] + c * hs[i], hs[i]), :]
            cp = _remote(src, here, send_sems.at[i], recv_sems.at[i], sibling)
            cp.start()
            sent.append(cp)
            local.append((src, here))
        _staged_copies(local, buf, in_sems, out_sems)
        for i, cp in enumerate(sent):
            there = outs[groups[i]].at[pl.ds(base[i] + (1 - c) * hs[i], hs[i]), :]
            _remote(there, there, send_sems.at[i], recv_sems.at[i], sibling).wait_recv()
            cp.wait_send()

    assert max(hs) <= STAGE_ROWS
    return pl.pallas_call(
        body, name=name, in_specs=[ANY], out_specs=[ANY] * n_out,
        out_shape=[jax.ShapeDtypeStruct((r, cols), F32) for r in out_rows],
        scratch_shapes=[pltpu.SemaphoreType.DMA((n,)), pltpu.SemaphoreType.DMA((n,))] + _stage_scratch(F32, cols),
        compiler_params=pltpu.CompilerParams(vmem_limit_bytes=VMEM_LIMIT),
    )(mine)


SMALL_ROWS = 16
SMALL_PACK_ROWS = 256


def _small_rows(n):
    return -(-n // (SMALL_ROWS * LANES)) * SMALL_ROWS


def _pack_small(arrs):
    parts = []
    for a in arrs:
        flat = a.reshape(-1)
        rows = _small_rows(flat.shape[0])
        flat = jnp.pad(flat, (0, rows * LANES - flat.shape[0]))
        parts.append(flat.reshape(rows, LANES))
    total = sum(p.shape[0] for p in parts)
    parts.append(jnp.zeros((-total % SMALL_PACK_ROWS, LANES), F32))
    return jnp.concatenate(parts, axis=0)


def _unpack_small(packed, shapes):
    out, r = [], 0
    for sh in shapes:
        n = math.prod(sh)
        cnt = _small_rows(n)
        out.append(packed[r:r + cnt].reshape(-1)[:n].reshape(sh))
        r += cnt
    return out


def _ffn_bwd(dh, dhb, h_in, saved, g_norm, w_gate_t, w_up_t, w_down, tag, after=None):
    n, gate, up, act = saved
    dgate, dup = _ffn_dact(dhb, w_down, gate, up, f"{tag}_dact", after)
    dw_down = _matmul(act, dhb, trans_a=True, name=f"{tag}_dwdown")
    dw_gate_t = _matmul(dgate, n, trans_a=True, name=f"{tag}_dwgate")
    dw_up_t = _matmul(dup, n, trans_a=True, name=f"{tag}_dwup")
    dh_in, dh_inb, dg = _dn_norm([(dgate, w_gate_t), (dup, w_up_t)], h_in, g_norm, dh, f"{tag}_dnorm")
    return dh_in, dh_inb, dg, dw_gate_t, dw_up_t, dw_down


def _local_step(x, tgt, w, big, late_weights, reduce_send, reduce_exchange):
    s = x.shape[0]
    tabs = _rope_tables(s)
    grads, gbig = {}, {}

    g_ev = w['ev_norm_g']
    n1 = _rms_fwd(x, g_ev, "ev_norm")
    proj0 = _matmul(n1, big['ev_w_in', 0], trans_b=True, name="ev_in", out_dtype=BF16, rows_inner=True)
    q0, k0, v0 = _qkv_prep_even(proj0, tabs, "ev_qkv")
    sinks = w['ev_sinks'].reshape(-1)
    o0, lse0, o0b = _attn_fwd(q0, k0, v0, sinks, max_dist=BLOCK - 1, name="ev_attn", emit_bf16=True)
    yconv, cout = _conv_fwd(proj0, w['ev_conv_w'][0], w['ev_conv_b'], w['ev_conv_ln_g'], w['ev_conv_ln_b'], "ev_conv")
    mix0 = (o0b[0], cout)
    g_f0 = w['ffn_norm_g'][0:1]
    h1, n2 = _matmul_norm(mix0, big['ev_w_out', 0], x, g_f0, "ev_out")
    big = {**big, **late_weights(h1)}

    g_od = w['od_norm_g']
    act0, gate0, up0 = _ffn_gate_up(n2, big['ffn_w_gate', 0], big['ffn_w_up', 0], "ffn0_gate_up")
    h2, n3 = _matmul_norm(act0, big['ffn_w_down', 0], h1, g_od, "ffn0_down")
    ffn0 = (n2, gate0, up0, act0)

    proj1 = _matmul(n3, big['od_w_in', 0], trans_b=True, name="od_in", out_dtype=BF16, rows_inner=True)
    qkv = _qkv_prep_odd(proj1, tabs, "od_qkv")
    nb = len(DILATIONS)
    outs, lses = [], []
    for i, d in enumerate(DILATIONS):
        o_r, lse_r = _attn_fwd(qkv[i], qkv[nb + i], qkv[2 * nb + i], None, max_dist=BLOCK, name=f"od_attn{d}", o_dtype=BF16)
        outs.append(o_r)
        lses.append(lse_r)
    comb = _combine(outs, lses, "od_combine")
    c_bf16 = comb[0]
    c_fold = {1: comb[1]}
    lse_fold = {1: comb[2]}
    for i, d in enumerate(DILATIONS[1:]):
        c_fold[d], lse_fold[d] = comb[3 + 2 * i], comb[4 + 2 * i]
    w_sp = w['od_spatial_w'][0]
    sb_t = w['od_spatial_b'][0].T
    mixed, dout = _gate_fwd(proj1, w['od_sgu_ln_g'], w['od_sgu_ln_b'], w_sp, sb_t, "od_gate")
    mix1 = (c_bf16, dout)
    g_f1 = w['ffn_norm_g'][1:2]
    h3, n4 = _matmul_norm(mix1, big['od_w_out', 0], h2, g_f1, "od_out")
    act1, gate1, up1 = _ffn_gate_up(n4, big['ffn_w_gate', 1], big['ffn_w_up', 1], "ffn1_gate_up")
    ffn1 = (n4, gate1, up1, act1)

    dh4, dh4b, dg_final, loss_tile = _matmul_final(act1, big['ffn_w_down', 1], h3, w['final_norm_g'].reshape(1, D_MODEL),
                                                   tgt, "ffn1_down_loss")
    grads['final_norm_g'] = dg_final.reshape(D_MODEL)

    dh3, dh3b, dg_f1, gbig['ffn_w_gate', 1], gbig['ffn_w_up', 1], gbig['ffn_w_down', 1] = _ffn_bwd(
        dh4, dh4b, h3, ffn1, g_f1, big['ffn_w_gate', 1], big['ffn_w_up', 1], big['ffn_w_down', 1], "ffn1")

    dmix1 = _matmul(dh3b, big['od_w_out', 0], trans_b=True, name="od_dmix")
    gbig['od_w_out', 0] = _matmul_tn_pair(mix1[0], mix1[1], dh3b, "od_dwout")
    do_fold = dict(zip(DILATIONS[1:], _fold_dout(dmix1, "od_fold_dout")))
    do_fold[1] = dmix1[None]
    dqs, dks, dvs = [], [], []
    for i, d in enumerate(DILATIONS):
        dq_r, dk_r, dv_r = _attn_bwd(qkv[i], qkv[nb + i], qkv[2 * nb + i], do_fold[d], c_fold[d], lse_fold[d], None,
                                     max_dist=BLOCK, name=f"od_dattn{d}")
        dqs.append(dq_r)
        dks.append(dk_r)
        dvs.append(dv_r)
    dz, dg_sgu, db_sgu, dw_sp, dsb = _gate_bwd(dmix1, proj1, mixed, w['od_sgu_ln_g'], w['od_sgu_ln_b'], w_sp, "od_dgate")
    grads['od_sgu_ln_g'], grads['od_sgu_ln_b'] = dg_sgu, db_sgu
    grads['od_spatial_w'], grads['od_spatial_b'] = dw_sp[None], dsb[None]
    dproj1 = _qkv_post_odd(dqs, dks, dvs, dz, tabs, "od_dproj")
    gbig['od_w_in', 0] = _matmul(dproj1, n3, trans_a=True, name="od_dwin")
    dh2, dh2b, dg_od = _dn_norm([(dproj1, big['od_w_in', 0])], h2, g_od, dh3, "od_dnorm")
    grads['od_norm_g'] = dg_od
    token = reduce_send(0, gbig)

    dh1, dh1b, dg_f0, gbig['ffn_w_gate', 0], gbig['ffn_w_up', 0], gbig['ffn_w_down', 0] = _ffn_bwd(
        dh2, dh2b, h1, ffn0, g_f0, big['ffn_w_gate', 0], big['ffn_w_up', 0], big['ffn_w_down', 0], "ffn0", token)
    grads['ffn_norm_g'] = jnp.concatenate([dg_f0, dg_f1], axis=0)
    token = reduce_exchange(0, dh1) + reduce_send(1, gbig)

    dmix0 = _matmul(dh1b, big['ev_w_out', 0], trans_b=True, name="ev_dmix", after=token)
    gbig['ev_w_out', 0] = _matmul_tn_pair(mix0[0], mix0[1], dh1b, "ev_dwout")
    dq0, dk0, dv0, dsink = _attn_bwd(q0, k0, v0, dmix0[None], o0, lse0, sinks, max_dist=BLOCK - 1, name="ev_dattn")
    grads['ev_sinks'] = dsink[:, 0, :].reshape(N_PAIRS, 2, HEAD_DIM)[:, :, 0].reshape(1, 8)
    token = reduce_exchange(1, dq0)
    dyc, dg_cln, db_cln, dcb = _conv_tail_bwd(dmix0, yconv, w['ev_conv_ln_g'] + token[0:1, 0:1], w['ev_conv_ln_b'],
                                              "ev_dconv_tail")
    grads['ev_conv_ln_g'], grads['ev_conv_ln_b'], grads['ev_conv_b'] = dg_cln, db_cln, dcb
    dglu, dconv_w = _conv_bwd(proj0, dyc, w['ev_conv_w'][0], "ev_dconv")
    grads['ev_conv_w'] = dconv_w[None]
    dproj0 = _qkv_post_even(dq0, dk0, dv0, dglu, tabs, "ev_dproj")
    gbig['ev_w_in', 0] = _matmul(dproj0, n1, trans_a=True, name="ev_dwin")
    dx, _, dg_ev = _dn_norm([(dproj0, big['ev_w_in', 0])], x, g_ev, dh1, "ev_dnorm")
    grads['ev_norm_g'] = dg_ev
    return loss_tile, dx, grads, gbig


def _shard_rows(w, layer, by_cols):
    return w[layer].T if by_cols else w[layer]


def kernel(x, ev_norm_g, ev_w_in, ev_sinks, ev_conv_w, ev_conv_b, ev_conv_ln_g, ev_conv_ln_b, ev_w_out, od_norm_g, od_w_in, od_sgu_ln_g, od_sgu_ln_b, od_spatial_w, od_spatial_b, od_w_out, ffn_norm_g, ffn_w_gate, ffn_w_up, ffn_w_down, final_norm_g, loss_target, m_ev_norm_g, m_ev_w_in, m_ev_sinks, m_ev_conv_w, m_ev_conv_b, m_ev_conv_ln_g, m_ev_conv_ln_b, m_ev_w_out, m_od_norm_g, m_od_w_in, m_od_sgu_ln_g, m_od_sgu_ln_b, m_od_spatial_w, m_od_spatial_b, m_od_w_out, m_ffn_norm_g, m_ffn_w_gate, m_ffn_w_up, m_ffn_w_down, m_final_norm_g, v_ev_norm_g, v_ev_w_in, v_ev_sinks, v_ev_conv_w, v_ev_conv_b, v_ev_conv_ln_g, v_ev_conv_ln_b, v_ev_w_out, v_od_norm_g, v_od_w_in, v_od_sgu_ln_g, v_od_sgu_ln_b, v_od_spatial_w, v_od_spatial_b, v_od_w_out, v_ffn_norm_g, v_ffn_w_gate, v_ffn_w_up, v_ffn_w_down, v_final_norm_g):
    given = dict(locals())
    wts = {n: given[n] for n in WEIGHTS}
    mom = {n: given["m_" + n] for n in WEIGHTS}
    var = {n: given["v_" + n] for n in WEIGHTS}
    chip = 2 * lax.axis_index("x") + lax.axis_index("y")

    shard_rows = [_shard_rows(wts[n], layer, by_cols).astype(BF16) for n, layer, by_cols in BIG]
    counts = [a.shape[0] for a in shard_rows]
    n_first = sum(n.startswith('ev_') for n, _, _ in BIG)

    def unpack(stacked, entries, cnts):
        out, r = {}, 0
        for (n, layer, _), cnt in zip(entries, cnts):
            out[n, layer] = stacked[:, r:r + cnt].reshape(N_CHIPS * cnt, D_MODEL)
            r += cnt
        return out

    first_w = _gather_chips(jnp.concatenate(shard_rows[:n_first], axis=0), "gather_weights_ev")
    big = unpack(first_w, BIG[:n_first], counts[:n_first])
    send_sems, recv_sems, late_shard, late_land, token = _gather_start(jnp.concatenate(shard_rows[n_first:], axis=0),
                                                                      first_w, "gather_weights_start")

    def late_weights(after):
        shard, land = _gather_wait(send_sems, recv_sems, late_shard, late_land, after, "gather_weights_wait")
        return unpack(_gather_finish(shard, land, "gather_weights_finish"), BIG[n_first:], counts[n_first:])

    full = {n: wts[n] for n in SMALL_REPL}
    full['ev_norm_g'] = full['ev_norm_g'] + token[0:1, 0:1]
    small_shards = [wts[n] for n in SMALL_SHARDED]
    small_shapes = [a.shape for a in small_shards]
    all_s = _gather_chips(_pack_small(small_shards), "gather_small_weights")
    per_chip = [_unpack_small(all_s[k], small_shapes) for k in range(N_CHIPS)]
    for i, n in enumerate(SMALL_SHARDED):
        full[n] = jnp.concatenate([per_chip[k][i] for k in range(N_CHIPS)], axis=-1)

    half_rows = {(n, layer): cnt // 2 for (n, layer, _), cnt in zip(BIG, counts)}
    in_flight = []

    sending = {}

    def halves(stage, gbig):
        return [gbig[e].reshape(N_CHIPS, 2, half_rows[e], D_MODEL) for e in GRAD_STAGES[stage]]

    def reduce_send(stage, gbig):
        send_sems, recv_sems, *buffers, token = _pair_send_start(halves(stage, gbig), f"grad_pair_start{stage}")
        sending[stage] = (send_sems, recv_sems, buffers)
        return token

    def reduce_exchange(stage, after):
        send_sems, recv_sems, buffers = sending.pop(stage)
        *split, got = _pair_send_wait(send_sems, recv_sems, buffers, after, f"grad_pair_wait{stage}")
        chip_part = _pair_add(split, got, f"grad_pair_add{stage}")
        *handles, token = _exchange_start(chip_part, f"grad_exchange_start{stage}")
        in_flight.append(handles)
        return token

    loss_tile, grad_x, grads, gbig = _local_step(x[0], loss_target[0], full, big, late_weights, reduce_send, reduce_exchange)
    loss = lax.psum(loss_tile[0, 0], ("x", "y", "c"))

    last = len(GRAD_STAGES) - 1
    split = halves(last, gbig)
    chip_part = _pair_add(split, _pair_send(split, f"grad_pair_send{last}"), f"grad_pair_add{last}")
    *handles, _ = _exchange_start(chip_part, f"grad_exchange_start{last}")
    in_flight.append(handles)

    reduced, after = {}, grad_x
    for stage, entries in enumerate(GRAD_STAGES):
        chip_part, from_chips = _exchange_wait(*in_flight[stage], after, f"grad_exchange_wait{stage}")
        my_half = _chip_sum(chip_part, from_chips, chip.reshape(1), f"grad_chip_sum{stage}")
        joined = _join_unpack(my_half, [half_rows[e] for e in entries], list(range(len(entries))), f"grad_join_halves{stage}")
        reduced.update(zip(entries, joined))
        after = joined[0]

    small_names = SMALL_REPL + SMALL_SHARDED
    small_full_shapes = [grads[n].shape for n in small_names]
    spack = _pack_small([grads[n] for n in small_names])
    s_all = _gather_devices(spack, "grad_small_gather")
    s_sum = _unpack_small(_ordered_sum(s_all, "grad_small_sum"), small_full_shapes)
    g_all = dict(zip(small_names, s_sum))
    for n in SMALL_SHARDED:
        width = wts[n].shape[-1]
        g_all[n] = lax.dynamic_slice_in_dim(g_all[n], chip * width, width, axis=g_all[n].ndim - 1)

    delta, new_m, new_v = {}, {}, {}
    for n in BIG_NAMES:
        by_cols = [bc for nn, _, bc in BIG if nn == n][0]
        layers = wts[n].shape[0]

        def as_rows(a):
            return (jnp.swapaxes(a, 1, 2) if by_cols else a).reshape(-1, D_MODEL)

        def from_rows(a):
            a = a.reshape(layers, -1, D_MODEL)
            return jnp.swapaxes(a, 1, 2) if by_cols else a

        g_rows = [reduced[n, layer] for layer in range(layers)]
        g_rows = g_rows[0] if layers == 1 else jnp.concatenate(g_rows, axis=0)
        updated = _adamw(as_rows(wts[n]), g_rows, as_rows(mom[n]), as_rows(var[n]), f"adamw_{n}")
        g_all[n] = from_rows(g_rows)
        delta[n], new_m[n], new_v[n] = (from_rows(a) for a in updated)
    for n in small_names:
        shape = wts[n].shape
        as_2d = (lambda a: a.reshape(-1, shape[-1]))
        updated = _adamw(as_2d(wts[n]), as_2d(g_all[n]), as_2d(mom[n]), as_2d(var[n]), f"adamw_{n}")
        delta[n], new_m[n], new_v[n] = (a.reshape(shape) for a in updated)

    return (loss, grad_x[None], *[g_all[n] for n in WEIGHTS], *[delta[n] for n in WEIGHTS],
            *[new_m[n] for n in WEIGHTS], *[new_v[n] for n in WEIGHTS])
```

```python
import math

import jax
import jax.numpy as jnp
from jax import lax
from jax.experimental import pallas as pl
from jax.experimental.pallas import tpu as pltpu

F32 = jnp.float32
BF16 = jnp.bfloat16

D_MODEL = 1024
HEAD_DIM = 64
ROT_DIM = 16
ROPE_THETA = 500000.0
RMS_EPS = 1e-6
LN_EPS = 1e-5
BLOCK = 128
CONV_WIDTH = 31
CONV_HALO = 32
CONV_ROWS = 64
D_FF = 2816
N_GROUPS = 8
ATTN_W = 512
ATTN_SCALE = HEAD_DIM ** -0.5
NEG = -1e30
DILATIONS = (1, 4, 16)

ADAM_LR = 0.001
ADAM_B1 = 0.9
ADAM_B2 = 0.999
ADAM_EPS = 1e-08
ADAM_WD = 0.01
ADAM_STEP = 10

LANES = 128
N_PAIRS = ATTN_W // LANES
VMEM_LIMIT = 56 * 1024 * 1024
MESH = pl.DeviceIdType.MESH
N_CHIPS = 4
N_DEV = 8

WEIGHTS = ['ev_norm_g', 'ev_w_in', 'ev_sinks', 'ev_conv_w', 'ev_conv_b', 'ev_conv_ln_g', 'ev_conv_ln_b', 'ev_w_out',
           'od_norm_g', 'od_w_in', 'od_sgu_ln_g', 'od_sgu_ln_b', 'od_spatial_w', 'od_spatial_b', 'od_w_out',
           'ffn_norm_g', 'ffn_w_gate', 'ffn_w_up', 'ffn_w_down', 'final_norm_g']
BIG = [('ev_w_in', 0, True), ('ev_w_out', 0, False), ('od_w_in', 0, True), ('od_w_out', 0, False),
       ('ffn_w_gate', 0, True), ('ffn_w_gate', 1, True), ('ffn_w_up', 0, True), ('ffn_w_up', 1, True),
       ('ffn_w_down', 0, False), ('ffn_w_down', 1, False)]
BIG_NAMES = ['ev_w_in', 'ev_w_out', 'od_w_in', 'od_w_out', 'ffn_w_gate', 'ffn_w_up', 'ffn_w_down']
GRAD_STAGES = ([('od_w_in', 0), ('od_w_out', 0), ('ffn_w_gate', 1), ('ffn_w_up', 1), ('ffn_w_down', 1)],
               [('ffn_w_gate', 0), ('ffn_w_up', 0), ('ffn_w_down', 0)],
               [('ev_w_in', 0), ('ev_w_out', 0)])
SMALL_SHARDED = ['ev_conv_w', 'od_norm_g', 'od_sgu_ln_g', 'od_sgu_ln_b']
SMALL_REPL = ['ev_norm_g', 'ev_sinks', 'ev_conv_b', 'ev_conv_ln_g', 'ev_conv_ln_b', 'od_spatial_w', 'od_spatial_b',
              'ffn_norm_g', 'final_norm_g']


def _tile(n, cap, mult=LANES):
    best = None
    for t in range(mult, min(n, cap) + 1, mult):
        if n % t == 0:
            best = t
    assert best is not None, (n, cap)
    return best


def _params(*sem):
    return pltpu.CompilerParams(dimension_semantics=sem, vmem_limit_bytes=VMEM_LIMIT)


def _sigmoid(x):
    return 1.0 / (1.0 + jnp.exp(-x))


def _pair_block(p):
    return slice(p * LANES, (p + 1) * LANES)


def _matmul(a, b, *, name, trans_a=False, trans_b=False, add=None, out_dtype=F32, after=None, rows_inner=False):
    parts = a if isinstance(a, (tuple, list)) else (a,)
    if trans_a:
        k, m = parts[0].shape
    else:
        m = parts[0].shape[0]
        k = sum(p.shape[1] for p in parts)
    if trans_b:
        n, k2 = b.shape
    else:
        k2, n = b.shape
    assert k == k2 and b.dtype == BF16 and all(p.dtype == BF16 for p in parts)
    tm = _tile(m, D_FF // 2 if trans_a else 512)
    tn = _tile(n, D_FF // 2)
    tk = k if k <= D_FF else _tile(k, 2048)
    nk = k // tk
    na = len(parts)
    assert na == 1 or (nk == 1 and not trans_a)
    assert nk == 1 or out_dtype == F32
    dims = (((0 if trans_a else 1,), (1 if trans_b else 0,)), ((), ()))
    has_add = add is not None

    def body(*refs):
        a_refs, b_ref = refs[:na], refs[na]
        add_ref = refs[na + 1] if has_add else None
        o_ref = refs[na + 1 + has_add + (after is not None)]
        def product():
            a_val = a_refs[0][...] if na == 1 else jnp.concatenate([r[...] for r in a_refs], axis=1)
            return lax.dot_general(a_val, b_ref[...], dims, preferred_element_type=F32)

        if nk == 1:
            part = product()
            if has_add:
                part = part + add_ref[...]
            o_ref[...] = part.astype(o_ref.dtype)
            return
        kk = pl.program_id(2)

        @pl.when(kk == 0)
        def _():
            o_ref[...] = product() + add_ref[...] if has_add else product()

        @pl.when(kk > 0)
        def _():
            o_ref[...] = product() + o_ref[...]

    def at(f):
        return (lambda j, i, kk: f(i, j, kk)) if rows_inner else f

    if trans_a:
        a_specs = [pl.BlockSpec((tk, tm), at(lambda i, j, kk: (kk, i)))]
    elif na == 1:
        a_specs = [pl.BlockSpec((tm, tk), at(lambda i, j, kk: (i, kk)))]
    else:
        a_specs = [pl.BlockSpec((tm, p.shape[1]), at(lambda i, j, kk: (i, 0))) for p in parts]
    b_spec = (pl.BlockSpec((tn, tk), at(lambda i, j, kk: (j, kk))) if trans_b
              else pl.BlockSpec((tk, tn), at(lambda i, j, kk: (kk, j))))
    o_spec = pl.BlockSpec((tm, tn), at(lambda i, j, kk: (i, j)))
    in_specs = a_specs + [b_spec] + ([o_spec] if has_add else [])
    operands = list(parts) + [b] + ([add] if has_add else [])
    if after is not None:
        in_specs.append(_after_spec(after))
        operands.append(after)
    grid = (n // tn, m // tm, nk) if rows_inner else (m // tm, n // tn, nk)
    return pl.pallas_call(
        body, name=name, grid=grid, in_specs=in_specs, out_specs=o_spec,
        out_shape=jax.ShapeDtypeStruct((m, n), out_dtype),
        compiler_params=_params("parallel", "parallel", "arbitrary"),
    )(*operands)


def _matmul_rows(a, b, add, epilogue, consts, tiled, outs, accs, name):
    parts = a if isinstance(a, (tuple, list)) else (a,)
    m = parts[0].shape[0]
    tm = 512
    na, nc, nt, no = len(parts), len(consts), len(tiled), len(outs)

    def body(*refs):
        a_refs, b_ref, add_ref = refs[:na], refs[na], refs[na + 1]
        const_refs = refs[na + 2:na + 2 + nc]
        tiled_refs = refs[na + 2 + nc:na + 2 + nc + nt]
        out_refs = refs[na + 2 + nc + nt:]
        a_val = a_refs[0][...] if na == 1 else jnp.concatenate([r[...] for r in a_refs], axis=1)
        h = jnp.dot(a_val, b_ref[...], preferred_element_type=F32) + add_ref[...]
        results = epilogue(h, [r[...] for r in const_refs], [r[...] for r in tiled_refs])
        for o_ref, val in zip(out_refs[:no], results[:no]):
            o_ref[...] = val.astype(o_ref.dtype)
        if accs:
            @pl.when(_first_step())
            def _():
                for o_ref in out_refs[no:]:
                    o_ref[...] = jnp.zeros_like(o_ref)

            for o_ref, val in zip(out_refs[no:], results[no:]):
                o_ref[...] += val

    row = lambda w: pl.BlockSpec((tm, w), lambda i: (i, 0))
    whole = lambda shape: pl.BlockSpec(shape, lambda i: (0,) * len(shape))
    return pl.pallas_call(
        body, name=name, grid=(m // tm,),
        in_specs=[row(p.shape[1]) for p in parts] + [whole(b.shape), row(D_MODEL)] + [whole(c.shape) for c in consts]
        + [row(t.shape[1]) for t in tiled],
        out_specs=[row(c) for c, _ in outs] + [whole(sh) for sh, _ in accs],
        out_shape=[jax.ShapeDtypeStruct((m, c), dt) for c, dt in outs] + [jax.ShapeDtypeStruct(sh, dt) for sh, dt in accs],
        compiler_params=_params("arbitrary"),
    )(*parts, b, add, *consts, *tiled)


def _matmul_norm(a, b, add, g, name):
    def epilogue(h, consts, tiled):
        r = lax.rsqrt(jnp.mean(h * h, axis=-1, keepdims=True) + RMS_EPS)
        return [h, h * r * consts[0]]

    return _matmul_rows(a, b, add, epilogue, [g], [], [(D_MODEL, F32), (D_MODEL, BF16)], [], name)


def _matmul_final(a, b, add, g, tgt, name):
    def epilogue(h, consts, tiled):
        gg = consts[0]
        r = lax.rsqrt(jnp.mean(h * h, axis=-1, keepdims=True) + RMS_EPS)
        xh = h * r
        e = xh * gg - tiled[0]
        loss = (0.5 / D_MODEL) * jnp.sum(jnp.sum(e * e, axis=-1, keepdims=True), axis=0, keepdims=True)
        dy = e * (1.0 / D_MODEL)
        dxh = dy * gg
        dx = r * (dxh - xh * jnp.mean(dxh * xh, axis=-1, keepdims=True))
        return [dx, dx, jnp.sum(dy * xh, axis=0, keepdims=True), jnp.broadcast_to(loss, (1, LANES))]

    return _matmul_rows(a, b, add, epilogue, [g], [tgt], [(D_MODEL, F32), (D_MODEL, BF16)],
                        [((1, D_MODEL), F32), ((1, LANES), F32)], name)


def _matmul_tn_pair(a1, a2, b, name):
    kdim, m1 = a1.shape
    m2 = a2.shape[1]
    n = b.shape[1]
    tn = _tile(n, 1024)
    tk = _tile(kdim, 2048)
    nk = kdim // tk
    dims = (((0,), (0,)), ((), ()))

    def body(a1_ref, a2_ref, b_ref, o_ref):
        kk = pl.program_id(1)
        def products():
            bv = b_ref[...]
            return (lax.dot_general(a1_ref[...], bv, dims, preferred_element_type=F32),
                    lax.dot_general(a2_ref[...], bv, dims, preferred_element_type=F32))

        @pl.when(kk == 0)
        def _():
            o_ref[0:m1, :], o_ref[m1:, :] = products()

        @pl.when(kk > 0)
        def _():
            top, bot = products()
            o_ref[0:m1, :] = top + o_ref[0:m1, :]
            o_ref[m1:, :] = bot + o_ref[m1:, :]

    return pl.pallas_call(
        body, name=name, grid=(n // tn, nk),
        in_specs=[pl.BlockSpec((tk, m1), lambda j, kk: (kk, 0)), pl.BlockSpec((tk, m2), lambda j, kk: (kk, 0)),
                  pl.BlockSpec((tk, tn), lambda j, kk: (kk, j))],
        out_specs=pl.BlockSpec((m1 + m2, tn), lambda j, kk: (0, j)),
        out_shape=jax.ShapeDtypeStruct((m1 + m2, n), F32),
        compiler_params=_params("parallel", "arbitrary"),
    )(a1, a2, b)


def _ffn_gate_up(n, w_gate_t, w_up_t, name):
    m, k = n.shape
    f = w_gate_t.shape[0]
    tm, tn = _tile(m, 1024), _tile(f, D_FF // 2)

    def body(n_ref, wg_ref, wu_ref, act_ref, gate_ref, up_ref):
        a = n_ref[...]

        def products(cols):
            return (lax.dot_general(a, wg_ref[cols, :], NT, preferred_element_type=F32),
                    lax.dot_general(a, wu_ref[cols, :], NT, preferred_element_type=F32))

        chunks = _col_chunks(tn)
        ahead = products(chunks[0])
        for idx, cols in enumerate(chunks):
            gate, up = ahead
            if idx + 1 < len(chunks):
                ahead = products(chunks[idx + 1])
            act_ref[:, cols] = (gate * _sigmoid(gate) * up).astype(BF16)
            gate_ref[:, cols] = gate.astype(BF16)
            up_ref[:, cols] = up.astype(BF16)

    wspec = pl.BlockSpec((tn, k), lambda j, i: (j, 0))
    ospec = pl.BlockSpec((tm, tn), lambda j, i: (i, j))
    return pl.pallas_call(
        body, name=name, grid=(f // tn, m // tm), in_specs=[pl.BlockSpec((tm, k), lambda j, i: (i, 0)), wspec, wspec],
        out_specs=[ospec] * 3, out_shape=[jax.ShapeDtypeStruct((m, f), BF16)] * 3,
        compiler_params=_params("parallel", "parallel"),
    )(n, w_gate_t, w_up_t)


def _col_chunks(n, width=384):
    return [slice(c, min(c + width, n)) for c in range(0, n, width)]


def _after_spec(after):
    return pl.BlockSpec(after.shape, lambda *_: (0,) * after.ndim)


def _ffn_dact(dhb, w_down, gate, up, name, after=None):
    m, k = dhb.shape
    f = w_down.shape[0]
    tm, tn = _tile(m, 1024), _tile(f, D_FF // 2)

    def body(d_ref, w_ref, g_ref, u_ref, *rest):
        dg_ref, du_ref = rest[-2:]
        d = d_ref[...]

        def product(cols):
            return lax.dot_general(d, w_ref[cols, :], NT, preferred_element_type=F32)

        chunks = _col_chunks(tn)
        ahead = product(chunks[0])
        for idx, cols in enumerate(chunks):
            dact = ahead
            if idx + 1 < len(chunks):
                ahead = product(chunks[idx + 1])
            g = g_ref[:, cols].astype(F32)
            sg = _sigmoid(g)
            dg_ref[:, cols] = (dact * u_ref[:, cols].astype(F32) * sg * (1.0 + g * (1.0 - sg))).astype(BF16)
            du_ref[:, cols] = (dact * g * sg).astype(BF16)

    ospec = pl.BlockSpec((tm, tn), lambda j, i: (i, j))
    extra = [] if after is None else [after]
    return pl.pallas_call(
        body, name=name, grid=(f // tn, m // tm),
        in_specs=[pl.BlockSpec((tm, k), lambda j, i: (i, 0)), pl.BlockSpec((tn, k), lambda j, i: (j, 0)), ospec, ospec]
        + [_after_spec(a) for a in extra],
        out_specs=[ospec] * 2, out_shape=[jax.ShapeDtypeStruct((m, f), BF16)] * 2,
        compiler_params=_params("parallel", "parallel"),
    )(dhb, w_down, gate, up, *extra)


def _dn_norm(pairs, h, g, dres, name):
    m = h.shape[0]
    tm = 512
    np_ = len(pairs)

    def body(*refs):
        a_refs, b_refs = refs[:np_], refs[np_:2 * np_]
        h_ref, dres_ref, g_ref, dh_ref, dhb_ref, dg_ref = refs[2 * np_:]

        @pl.when(_first_step())
        def _():
            dg_ref[...] = jnp.zeros_like(dg_ref)

        dy = jnp.dot(a_refs[0][...], b_refs[0][...], preferred_element_type=F32)
        for a_ref, b_ref in zip(a_refs[1:], b_refs[1:]):
            dy = jnp.dot(a_ref[...], b_ref[...], preferred_element_type=F32) + dy
        x = h_ref[...]
        r = lax.rsqrt(jnp.mean(x * x, axis=-1, keepdims=True) + RMS_EPS)
        xh = x * r
        dg_ref[...] += jnp.sum(dy * xh, axis=0, keepdims=True)
        dxh = dy * g_ref[...]
        tot = dres_ref[...] + r * (dxh - xh * jnp.mean(dxh * xh, axis=-1, keepdims=True))
        dh_ref[...] = tot
        dhb_ref[...] = tot.astype(BF16)

    row = lambda w: pl.BlockSpec((tm, w), lambda i: (i, 0))
    whole = lambda a: pl.BlockSpec(a.shape, lambda i: (0, 0))
    a_list, b_list = [a for a, _ in pairs], [b for _, b in pairs]
    return pl.pallas_call(
        body, name=name, grid=(m // tm,),
        in_specs=[row(a.shape[1]) for a in a_list] + [whole(b) for b in b_list] + [row(D_MODEL), row(D_MODEL), whole(g)],
        out_specs=[row(D_MODEL), row(D_MODEL), pl.BlockSpec((1, D_MODEL), lambda i: (0, 0))],
        out_shape=[jax.ShapeDtypeStruct((m, D_MODEL), F32), jax.ShapeDtypeStruct((m, D_MODEL), BF16),
                   jax.ShapeDtypeStruct((1, D_MODEL), F32)],
        compiler_params=_params("arbitrary"),
    )(*a_list, *b_list, h, dres, g)


def _rows(body, name, tm, tiled, consts, outs, accs=()):
    s = tiled[0].shape[0]
    assert s % tm == 0
    in_specs = [pl.BlockSpec((tm, a.shape[1]), lambda i: (i, 0)) for a in tiled]
    in_specs += [pl.BlockSpec(a.shape, lambda i, nd=a.ndim: (0,) * nd) for a in consts]
    out_shape = [jax.ShapeDtypeStruct((s, c), dt) for c, dt in outs]
    out_shape += [jax.ShapeDtypeStruct(sh, dt) for sh, dt in accs]
    out_specs = [pl.BlockSpec((tm, c), lambda i: (i, 0)) for c, _ in outs]
    out_specs += [pl.BlockSpec(sh, lambda i, nd=len(sh): (0,) * nd) for sh, _ in accs]
    return pl.pallas_call(
        body, name=name, grid=(s // tm,), in_specs=in_specs, out_specs=out_specs, out_shape=out_shape,
        compiler_params=_params("arbitrary"),
    )(*tiled, *consts)


def _first_step():
    return pl.program_id(0) == 0


def _rms_fwd(h, g, name):
    def body(h_ref, g_ref, n_ref):
        x = h_ref[...]
        r = lax.rsqrt(jnp.mean(x * x, axis=-1, keepdims=True) + RMS_EPS)
        n_ref[...] = (x * r * g_ref[...]).astype(BF16)

    return _rows(body, name, 512, [h], [g], [(D_MODEL, BF16)])[0]


def _rope_tables(s):
    half = ROT_DIM // 2
    inv_freq = ROPE_THETA ** (-jnp.arange(half, dtype=F32) * (2.0 / ROT_DIM))
    ang = jnp.arange(s, dtype=F32)[:, None] * inv_freq[None, :]
    cos, sin = jnp.cos(ang), jnp.sin(ang)
    rest = HEAD_DIM - ROT_DIM
    ones = jnp.ones((s, rest), F32)
    zeros = jnp.zeros((s, rest), F32)
    zh = jnp.zeros((s, half), F32)
    c_t = jnp.concatenate([cos, cos, ones], axis=1)
    a_t = jnp.concatenate([-sin, zh, zeros], axis=1)
    b_t = jnp.concatenate([zh, sin, zeros], axis=1)
    return tuple(jnp.tile(t, (1, LANES // HEAD_DIM)) for t in (c_t, a_t, b_t))


def _rot(x, c, a, b):
    w = x.shape[1]
    half = ROT_DIM // 2
    return x * c + pltpu.roll(x, w - half, 1) * a + pltpu.roll(x, half, 1) * b


def _wide(t, w):
    return t if w == LANES else jnp.tile(t, (1, w // LANES))


def _low_lanes(rows):
    return lax.broadcasted_iota(jnp.int32, (rows, LANES), 1) < HEAD_DIM


def _fold_store(x, sc_ref, out_refs):
    tm = x.shape[0]
    if any(d > 1 for d in out_refs):
        for p in range(N_PAIRS):
            sc_ref[p] = x[:, _pair_block(p)]
    for d, o_ref in out_refs.items():
        if d == 1:
            o_ref[0] = x.astype(o_ref.dtype)
            continue
        for r in range(d):
            for p in range(N_PAIRS):
                o_ref[r, :, _pair_block(p)] = sc_ref[p, pl.ds(r, tm // d, stride=d), :].astype(o_ref.dtype)


def _unfold_load(x_ref, sc_ref, d, add=False):
    n = x_ref.shape[1]
    for r in range(d):
        for p in range(N_PAIRS):
            rows = pl.ds(r, n, stride=d) if d > 1 else slice(None)
            val = x_ref[r, :, _pair_block(p)].astype(F32)
            if add:
                val = val + sc_ref[p, rows, :]
            sc_ref[p, rows, :] = val


def _folded_spec(d, tm, w=ATTN_W):
    return pl.BlockSpec((d, tm // d, w), lambda i: (0, i, 0))


def _folded_shape(s, d, dtype, w=ATTN_W):
    return jax.ShapeDtypeStruct((d, s // d, w), dtype)


def _qkv_prep_even(proj, tabs, name):
    s = proj.shape[0]
    tm = 512

    def body(p_ref, c_ref, a_ref, b_ref, q_ref, k_ref, v_ref):
        c, a, b = c_ref[...], a_ref[...], b_ref[...]
        q_ref[0] = _rot(p_ref[:, 0:ATTN_W].astype(F32), _wide(c, ATTN_W), _wide(a, ATTN_W), _wide(b, ATTN_W)).astype(BF16)
        lo = _low_lanes(tm)
        for src, o_ref in ((_rot(p_ref[:, 512:640].astype(F32), c, a, b), k_ref), (p_ref[:, 640:768].astype(F32), v_ref)):
            swapped = pltpu.roll(src, HEAD_DIM, 1)
            o_ref[0, :, 0:LANES] = jnp.where(lo, src, swapped).astype(BF16)
            o_ref[0, :, LANES:] = jnp.where(lo, swapped, src).astype(BF16)

    row = lambda w: pl.BlockSpec((tm, w), lambda i: (i, 0))
    return pl.pallas_call(
        body, name=name, grid=(s // tm,), in_specs=[row(proj.shape[1]), row(LANES), row(LANES), row(LANES)],
        out_specs=[_folded_spec(1, tm), _folded_spec(1, tm, 2 * LANES), _folded_spec(1, tm, 2 * LANES)],
        out_shape=[_folded_shape(s, 1, BF16), _folded_shape(s, 1, BF16, 2 * LANES), _folded_shape(s, 1, BF16, 2 * LANES)],
        compiler_params=_params("parallel"),
    )(proj, *tabs)


def _qkv_post_even(dq, dk, dv, dglu, tabs, name):
    s = dglu.shape[0]
    tm = 512

    def body(dq_ref, dk_ref, dv_ref, dr_ref, c_ref, a_ref, b_ref, o_ref):
        c, a, b = c_ref[...], -a_ref[...], -b_ref[...]
        o_ref[:, 0:ATTN_W] = _rot(dq_ref[0].astype(F32), _wide(c, ATTN_W), _wide(a, ATTN_W), _wide(b, ATTN_W)).astype(BF16)
        lo = _low_lanes(tm)
        merged = []
        for ref in (dk_ref, dv_ref):
            first, second = ref[0, :, 0:LANES].astype(F32), ref[0, :, LANES:].astype(F32)
            merged.append(jnp.where(lo, first + pltpu.roll(first, HEAD_DIM, 1), second + pltpu.roll(second, HEAD_DIM, 1)))
        o_ref[:, 512:640] = _rot(merged[0], c, a, b).astype(BF16)
        o_ref[:, 640:768] = merged[1].astype(BF16)
        o_ref[:, 768:] = dr_ref[...]

    row = lambda w: pl.BlockSpec((tm, w), lambda i: (i, 0))
    return pl.pallas_call(
        body, name=name, grid=(s // tm,),
        in_specs=[_folded_spec(1, tm), _folded_spec(1, tm, 2 * LANES), _folded_spec(1, tm, 2 * LANES),
                  row(dglu.shape[1]), row(LANES), row(LANES), row(LANES)],
        out_specs=row(EVEN_IN), out_shape=jax.ShapeDtypeStruct((s, EVEN_IN), BF16),
        compiler_params=_params("parallel"),
    )(dq, dk, dv, dglu, *tabs)


def _qkv_prep_odd(proj, tabs, name):
    s = proj.shape[0]
    tm = 1024

    def body(p_ref, c_ref, a_ref, b_ref, *rest):
        outs, sc_ref = rest[:-1], rest[-1]
        c, a, b = (_wide(t[...], ATTN_W) for t in (c_ref, a_ref, b_ref))
        for t in range(3):
            x = p_ref[:, t * ATTN_W:(t + 1) * ATTN_W].astype(F32)
            if t < 2:
                x = _rot(x, c, a, b)
            _fold_store(x, sc_ref, {d: outs[t * len(DILATIONS) + i] for i, d in enumerate(DILATIONS)})

    row = lambda w: pl.BlockSpec((tm, w), lambda i: (i, 0))
    return pl.pallas_call(
        body, name=name, grid=(s // tm,), in_specs=[row(proj.shape[1]), row(LANES), row(LANES), row(LANES)],
        out_specs=[_folded_spec(d, tm) for _ in range(3) for d in DILATIONS],
        out_shape=[_folded_shape(s, d, BF16) for _ in range(3) for d in DILATIONS],
        scratch_shapes=[pltpu.VMEM((N_PAIRS, tm, LANES), F32)],
        compiler_params=_params("parallel"),
    )(proj, *tabs)


def _qkv_post_odd(dqs, dks, dvs, dz, tabs, name):
    s = dz.shape[0]
    tm = 512
    nb = len(DILATIONS)

    def body(*refs):
        groups = (refs[:nb], refs[nb:2 * nb], refs[2 * nb:3 * nb])
        dz_ref, c_ref, a_ref, b_ref, o_ref, sc_ref = refs[3 * nb:]
        c, a, b = _wide(c_ref[...], ATTN_W), _wide(-a_ref[...], ATTN_W), _wide(-b_ref[...], ATTN_W)
        for t, group in enumerate(groups):
            for i, d in enumerate(DILATIONS):
                _unfold_load(group[i], sc_ref, d, add=i > 0)
            x = jnp.concatenate([sc_ref[p] for p in range(N_PAIRS)], axis=1)
            if t < 2:
                x = _rot(x, c, a, b)
            o_ref[:, t * ATTN_W:(t + 1) * ATTN_W] = x.astype(BF16)
        o_ref[:, 3 * ATTN_W:] = dz_ref[...]

    row = lambda w: pl.BlockSpec((tm, w), lambda i: (i, 0))
    return pl.pallas_call(
        body, name=name, grid=(s // tm,),
        in_specs=[_folded_spec(d, tm) for _ in range(3) for d in DILATIONS] + [row(dz.shape[1]), row(LANES), row(LANES), row(LANES)],
        out_specs=row(ODD_IN), out_shape=jax.ShapeDtypeStruct((s, ODD_IN), BF16),
        scratch_shapes=[pltpu.VMEM((N_PAIRS, tm, LANES), F32)],
        compiler_params=_params("parallel"),
    )(*dqs, *dks, *dvs, dz, *tabs)


def _fold_dout(dmix, name):
    s = dmix.shape[0]
    tm = 512
    ds = [d for d in DILATIONS if d > 1]

    def body(d_ref, *rest):
        outs, sc_ref = rest[:-1], rest[-1]
        _fold_store(d_ref[...], sc_ref, dict(zip(ds, outs)))

    return pl.pallas_call(
        body, name=name, grid=(s // tm,), in_specs=[pl.BlockSpec((tm, ATTN_W), lambda i: (i, 0))],
        out_specs=[_folded_spec(d, tm) for d in ds], out_shape=[_folded_shape(s, d, BF16) for d in ds],
        scratch_shapes=[pltpu.VMEM((N_PAIRS, tm, LANES), F32)],
        compiler_params=_params("parallel"),
    )(dmix)


def _window(j, i, tq):
    r0 = j * tq + i * BLOCK
    if i > 0:
        return pl.ds(pl.multiple_of(r0 - BLOCK, BLOCK), 2 * BLOCK), BLOCK
    start = pl.multiple_of(jnp.maximum(r0 - BLOCK, 0), BLOCK)
    return pl.ds(start, 2 * BLOCK), r0 - start


def _band_valid(offset, max_dist):
    shape = (2 * BLOCK, 2 * BLOCK)
    dist = (lax.bitwise_and(lax.broadcasted_iota(jnp.int32, shape, 0), BLOCK - 1)
            - lax.broadcasted_iota(jnp.int32, shape, 1) + offset)
    return jnp.abs(2 * dist - max_dist) <= max_dist


def _stack_heads(lo, x):
    zero = jnp.zeros_like(x)
    return jnp.concatenate([jnp.where(lo, x, zero), jnp.where(lo, zero, x)], axis=0)


def _unstack_heads(lo, x):
    return jnp.where(lo, x[:BLOCK], x[BLOCK:])


NT = (((1,), (1,)), ((), ()))
TN = (((0,), (0,)), ((), ()))


def _attn_fwd(q, k, v, sinks, *, max_dist, name, emit_bf16=False, o_dtype=F32):
    d, sp, wq = q.shape
    nq, nk = wq // LANES, k.shape[2] // LANES
    kdiv = nq // nk
    tq = min(sp, 1024)
    nsub = tq // BLOCK
    has_sink = sinks is not None

    def body(*refs):
        refs = list(refs)
        sink_ref = refs.pop(0) if has_sink else None
        q_ref, k_ref, v_ref, o_ref, lse_ref = refs[:5]
        pair = pl.program_id(1)
        j = pl.program_id(2)
        lo = _low_lanes(BLOCK)
        if has_sink:
            first_head = lax.broadcasted_iota(jnp.int32, (2 * BLOCK, 1), 0) < BLOCK
            sk = jnp.where(first_head, sink_ref[2 * pair], sink_ref[2 * pair + 1])
        for i in range(nsub):
            win, offset = _window(j, i, tq)
            rows = slice(i * BLOCK, (i + 1) * BLOCK)
            kw = k_ref[0, win, :]
            vw = v_ref[0, win, :]
            s = lax.dot_general(_stack_heads(lo, q_ref[0, rows, :]), kw, NT, preferred_element_type=F32) * ATTN_SCALE
            s = jnp.where(_band_valid(offset, max_dist), s, NEG)
            m = jnp.max(s, axis=-1, keepdims=True)
            if has_sink:
                m = jnp.maximum(m, sk)
            p = jnp.exp(s - m)
            l = jnp.sum(p, axis=-1, keepdims=True)
            if has_sink:
                l = l + jnp.exp(sk - m)
            o2 = _unstack_heads(lo, jnp.dot(p.astype(BF16), vw, preferred_element_type=F32) / l)
            o_ref[0, rows, :] = o2.astype(o_ref.dtype)
            lse_ref[0, rows, :] = _unstack_heads(lo, m + jnp.log(l))
            if emit_bf16:
                refs[5][0, rows, :] = o2.astype(BF16)

    qspec = pl.BlockSpec((1, tq, LANES), lambda r, p, j: (r, j, p))
    kspec = pl.BlockSpec((1, sp, LANES), lambda r, p, j: (r, 0, p // kdiv))
    in_specs = [qspec, kspec, kspec]
    operands = [q, k, v]
    if has_sink:
        in_specs = [pl.BlockSpec(memory_space=pltpu.SMEM)] + in_specs
        operands = [sinks] + operands
    out_shape = [jax.ShapeDtypeStruct(q.shape, o_dtype), jax.ShapeDtypeStruct(q.shape, F32)]
    if emit_bf16:
        out_shape.append(jax.ShapeDtypeStruct(q.shape, BF16))
    return pl.pallas_call(
        body, name=name, grid=(d, nq, sp // tq), in_specs=in_specs, out_specs=[qspec] * len(out_shape),
        out_shape=out_shape, compiler_params=_params("parallel", "parallel", "arbitrary"),
    )(*operands)


def _attn_bwd(q, k, v, do, oo, lse, sinks, *, max_dist, name):
    d, sp, wq = q.shape
    wk = k.shape[2]
    nq, nk = wq // LANES, wk // LANES
    kdiv = nq // nk
    tq = min(sp, 1024)
    nsub = tq // BLOCK
    has_sink = sinks is not None

    def body(*refs):
        refs = list(refs)
        sink_ref = refs.pop(0) if has_sink else None
        q_ref, k_ref, v_ref, do_ref, oo_ref, lse_ref, dq_ref, dk_out, dv_out = refs[:9]
        dk_ref, dv_ref = refs[-2:]
        pk, g, j = pl.program_id(1), pl.program_id(2), pl.program_id(3)

        @pl.when((g == 0) & (j == 0))
        def _():
            dk_ref[...] = jnp.zeros_like(dk_ref)
            dv_ref[...] = jnp.zeros_like(dv_ref)

        lo = _low_lanes(BLOCK)
        if has_sink:
            first_head = lax.broadcasted_iota(jnp.int32, (2 * BLOCK, 1), 0) < BLOCK
            pair = pk * kdiv + g
            sk = jnp.where(first_head, sink_ref[2 * pair], sink_ref[2 * pair + 1])
            sink_acc = jnp.zeros((2 * BLOCK, LANES), F32)
        for i in range(nsub):
            win, offset = _window(j, i, tq)
            rows = slice(i * BLOCK, (i + 1) * BLOCK)
            kw = k_ref[0, win, :]
            vw = v_ref[0, win, :]
            do2 = do_ref[0, rows, :].astype(F32)
            qs = _stack_heads(lo, q_ref[0, rows, :])
            dos = _stack_heads(lo, do2.astype(BF16))
            prod = do2 * oo_ref[0, rows, :]
            delta = jnp.sum(_stack_heads(lo, prod), axis=-1, keepdims=True)
            lse2 = lse_ref[0, rows, :]
            lse_swapped = pltpu.roll(lse2, HEAD_DIM, 1)
            lse_st = jnp.concatenate([jnp.where(lo, lse2, lse_swapped), jnp.where(lo, lse_swapped, lse2)], axis=0)
            s = lax.dot_general(qs, kw, NT, preferred_element_type=F32) * ATTN_SCALE
            s = jnp.where(_band_valid(offset, max_dist), s, NEG)
            p = jnp.exp(s - jnp.tile(lse_st, (1, 2)))
            dv_ref[win, :] = lax.dot_general(p.astype(BF16), dos, TN, preferred_element_type=F32) + dv_ref[win, :]
            dp = lax.dot_general(dos, vw, NT, preferred_element_type=F32)
            ds = (p * (dp - delta) * ATTN_SCALE).astype(BF16)
            dq_ref[0, rows, :] = _unstack_heads(lo, jnp.dot(ds, kw, preferred_element_type=F32)).astype(BF16)
            dk_ref[win, :] = lax.dot_general(ds, qs, TN, preferred_element_type=F32) + dk_ref[win, :]
            if has_sink:
                sink_acc = sink_acc - jnp.exp(sk - lse_st) * delta

        @pl.when((g == kdiv - 1) & (j == sp // tq - 1))
        def _():
            dk_out[0] = dk_ref[...].astype(BF16)
            dv_out[0] = dv_ref[...].astype(BF16)

        if has_sink:
            dsink_ref = refs[9]

            @pl.when(j == 0)
            def _():
                dsink_ref[...] = jnp.zeros_like(dsink_ref)

            dsink_ref[0] += jnp.where(lo[0:1], jnp.sum(sink_acc[:BLOCK], axis=0, keepdims=True),
                                      jnp.sum(sink_acc[BLOCK:], axis=0, keepdims=True))

    def qmap(r, pk, g, j):
        return (r, j, pk * kdiv + g)

    def kmap(r, pk, g, j):
        return (r, 0, pk)

    qspec = pl.BlockSpec((1, tq, LANES), qmap)
    kspec = pl.BlockSpec((1, sp, LANES), kmap)
    in_specs = [qspec, kspec, kspec, qspec, qspec, qspec]
    operands = [q, k, v, do, oo, lse]
    out_specs = [qspec, kspec, kspec]
    out_shape = [jax.ShapeDtypeStruct((d, sp, wq), BF16), jax.ShapeDtypeStruct((d, sp, wk), BF16),
                 jax.ShapeDtypeStruct((d, sp, wk), BF16)]
    if has_sink:
        in_specs = [pl.BlockSpec(memory_space=pltpu.SMEM)] + in_specs
        operands = [sinks] + operands
        out_specs.append(pl.BlockSpec((1, 1, LANES), lambda r, pk, g, j: (pk * kdiv + g, 0, 0)))
        out_shape.append(jax.ShapeDtypeStruct((nq, 1, LANES), F32))
    nsteps = sp // tq
    return pl.pallas_call(
        body, name=name, grid=(d, nk, kdiv, nsteps), in_specs=in_specs, out_specs=out_specs, out_shape=out_shape,
        scratch_shapes=[pltpu.VMEM((sp, LANES), F32), pltpu.VMEM((sp, LANES), F32)],
        compiler_params=_params("parallel", "parallel", "arbitrary", "arbitrary"),
    )(*operands)


def _combine(outs, lses, name):
    s = outs[0].shape[1]
    tm = 512
    nb = len(DILATIONS)
    ds = [d for d in DILATIONS if d > 1]

    def body(*refs):
        o_refs, l_refs = refs[:nb], refs[nb:2 * nb]
        cb_ref, c_ref, lse_ref = refs[2 * nb:2 * nb + 3]
        folded = refs[2 * nb + 3:2 * nb + 3 + 2 * len(ds)]
        scratch = refs[2 * nb + 3 + 2 * len(ds):]
        so = {1: None}
        sl = {1: None}
        for i, d in enumerate(ds):
            so[d], sl[d] = scratch[2 * i], scratch[2 * i + 1]
            _unfold_load(o_refs[1 + i], so[d], d)
            _unfold_load(l_refs[1 + i], sl[d], d)
        for p in range(N_PAIRS):
            pb = _pair_block(p)
            ls = [l_refs[0][0, :, pb]] + [sl[d][p] for d in ds]
            os_ = [o_refs[0][0, :, pb].astype(F32)] + [so[d][p] for d in ds]
            m = ls[0]
            for t in ls[1:]:
                m = jnp.maximum(m, t)
            ws = [jnp.exp(t - m) for t in ls]
            tot = ws[0]
            for t in ws[1:]:
                tot = tot + t
            acc = ws[0] * os_[0]
            for w, o in zip(ws[1:], os_[1:]):
                acc = acc + w * o
            cmix = acc / tot
            lse = m + jnp.log(tot)
            cb_ref[:, pb] = cmix.astype(BF16)
            c_ref[0, :, pb] = cmix
            lse_ref[0, :, pb] = lse
            so[ds[0]][p] = cmix
            sl[ds[0]][p] = lse
        for i, d in enumerate(ds):
            for r in range(d):
                for p in range(N_PAIRS):
                    rows = pl.ds(r, tm // d, stride=d)
                    folded[2 * i][r, :, _pair_block(p)] = so[ds[0]][p, rows, :]
                    folded[2 * i + 1][r, :, _pair_block(p)] = sl[ds[0]][p, rows, :]

    in_specs = [_folded_spec(d, tm) for _ in range(2) for d in DILATIONS]
    out_specs = [pl.BlockSpec((tm, ATTN_W), lambda i: (i, 0)), _folded_spec(1, tm), _folded_spec(1, tm)]
    out_shape = [jax.ShapeDtypeStruct((s, ATTN_W), BF16), _folded_shape(s, 1, F32), _folded_shape(s, 1, F32)]
    for d in ds:
        out_specs += [_folded_spec(d, tm)] * 2
        out_shape += [_folded_shape(s, d, F32)] * 2
    return pl.pallas_call(
        body, name=name, grid=(s // tm,), in_specs=in_specs, out_specs=out_specs, out_shape=out_shape,
        scratch_shapes=[pltpu.VMEM((N_PAIRS, tm, LANES), F32)] * (2 * len(ds)),
        compiler_params=_params("parallel"),
    )(*outs, *lses)


GLU_A = slice(768, 1280)
GLU_B = slice(1280, 1792)
EVEN_IN = 1792
ODD_IN = 2560
CONV_CH = 512


def _shifted_copies(xs_ref):
    rows = xs_ref.shape[1] - 8
    for b in range(1, 8):
        xs_ref[b, 0:rows, :] = xs_ref[0, pl.ds(b, rows), :]


def _shifted_rows(xs_ref, start):
    return xs_ref[start % 8, pl.ds(start - start % 8, CONV_ROWS), :]


def _glu(p_ref):
    return p_ref[:, GLU_A].astype(F32) * _sigmoid(p_ref[:, GLU_B].astype(F32))


def _conv_fwd(proj, w, b, ln_g, ln_b, name):
    s = proj.shape[0]
    tm = 512
    nh = tm // CONV_HALO
    lead = CONV_HALO - (CONV_WIDTH - 1)

    def body(p_ref, ph_ref, w_ref, b_ref, g_ref, bb_ref, y_ref, o_ref, xs_ref):
        xs_ref[0, CONV_HALO:, :] = _glu(p_ref)
        xs_ref[0, 0:CONV_HALO, :] = jnp.where(pl.program_id(0) > 0, _glu(ph_ref), 0.0)
        _shifted_copies(xs_ref)
        for c0 in range(0, tm, CONV_ROWS):
            acc = jnp.zeros((CONV_ROWS, CONV_CH), F32) + b_ref[...]
            for j in range(CONV_WIDTH):
                acc = acc + _shifted_rows(xs_ref, lead + j + c0) * w_ref[j:j + 1, :]
            y_ref[c0:c0 + CONV_ROWS, :] = acc
            mu = jnp.mean(acc, axis=-1, keepdims=True)
            xc = acc - mu
            var = jnp.mean(xc * xc, axis=-1, keepdims=True)
            zz = xc * lax.rsqrt(var + LN_EPS) * g_ref[...] + bb_ref[...]
            o_ref[c0:c0 + CONV_ROWS, :] = (zz * _sigmoid(zz)).astype(BF16)

    def const(a):
        return pl.BlockSpec(a.shape, lambda i: (0, 0))

    return pl.pallas_call(
        body, name=name, grid=(s // tm,),
        in_specs=[pl.BlockSpec((tm, EVEN_IN), lambda i: (i, 0)),
                  pl.BlockSpec((CONV_HALO, EVEN_IN), lambda i: (jnp.maximum(i * nh - 1, 0), 0)),
                  const(w), const(b), const(ln_g), const(ln_b)],
        out_specs=[pl.BlockSpec((tm, CONV_CH), lambda i: (i, 0)), pl.BlockSpec((tm, CONV_CH), lambda i: (i, 0))],
        out_shape=[jax.ShapeDtypeStruct((s, CONV_CH), F32), jax.ShapeDtypeStruct((s, CONV_CH), BF16)],
        scratch_shapes=[pltpu.VMEM((8, tm + CONV_HALO, CONV_CH), F32)],
        compiler_params=_params("arbitrary"),
    )(proj, proj, w, b, ln_g, ln_b)


def _conv_tail_bwd(dmix, yconv, ln_g, ln_b, name):
    def body(d_ref, y_ref, g_ref, b_ref, dy_ref, dg_ref, db_ref, dcb_ref):
        @pl.when(_first_step())
        def _():
            dg_ref[...] = jnp.zeros_like(dg_ref)
            db_ref[...] = jnp.zeros_like(db_ref)
            dcb_ref[...] = jnp.zeros_like(dcb_ref)

        y = y_ref[...]
        g = g_ref[...]
        mu = jnp.mean(y, axis=-1, keepdims=True)
        xc = y - mu
        rstd = lax.rsqrt(jnp.mean(xc * xc, axis=-1, keepdims=True) + LN_EPS)
        xh = xc * rstd
        zz = xh * g + b_ref[...]
        sg = _sigmoid(zz)
        dzz = d_ref[:, CONV_CH:] * sg * (1.0 + zz * (1.0 - sg))
        dg_ref[...] += jnp.sum(dzz * xh, axis=0, keepdims=True)
        db_ref[...] += jnp.sum(dzz, axis=0, keepdims=True)
        dxh = dzz * g
        dy = rstd * (dxh - jnp.mean(dxh, axis=-1, keepdims=True) - xh * jnp.mean(dxh * xh, axis=-1, keepdims=True))
        dcb_ref[...] += jnp.sum(dy, axis=0, keepdims=True)
        dy_ref[...] = dy

    vec = ((1, CONV_CH), F32)
    return _rows(body, name, 512, [dmix, yconv], [ln_g, ln_b], [(CONV_CH, F32)], [vec, vec, vec])


def _conv_bwd(proj, dy, w, name):
    s = proj.shape[0]
    tm = 512
    nh = tm // CONV_HALO
    nsteps = s // tm
    lead = CONV_HALO - (CONV_WIDTH - 1)

    def body(p_ref, ph_ref, dy_ref, dyn_ref, w_ref, dglu_ref, dw_ref, xf_ref, dyf_ref, part_ref):
        i = pl.program_id(0)

        @pl.when(i == 0)
        def _():
            dw_ref[...] = jnp.zeros_like(dw_ref)

        ga = p_ref[:, GLU_A].astype(F32)
        sgb = _sigmoid(p_ref[:, GLU_B].astype(F32))
        xf_ref[0, CONV_HALO:, :] = ga * sgb
        xf_ref[0, 0:CONV_HALO, :] = jnp.where(i > 0, _glu(ph_ref), 0.0)
        _shifted_copies(xf_ref)
        dyf_ref[0, 0:tm, :] = dy_ref[...]
        dyf_ref[0, tm:, :] = jnp.where(i < nsteps - 1, dyn_ref[...], 0.0)
        _shifted_copies(dyf_ref)
        for c0 in range(0, tm, CONV_ROWS):
            rows = slice(c0, c0 + CONV_ROWS)
            acc = jnp.zeros((CONV_ROWS, CONV_CH), F32)
            for j in range(CONV_WIDTH):
                acc = acc + _shifted_rows(dyf_ref, CONV_WIDTH - 1 - j + c0) * w_ref[j:j + 1, :]
            a_c, s_c = ga[rows, :], sgb[rows, :]
            dglu_ref[rows, 0:CONV_CH] = (acc * s_c).astype(BF16)
            dglu_ref[rows, CONV_CH:] = (acc * a_c * s_c * (1.0 - s_c)).astype(BF16)
        for c0 in range(0, tm, CONV_ROWS):
            dy_c = dy_ref[c0:c0 + CONV_ROWS, :]
            for j in range(CONV_WIDTH):
                prod = dy_c * _shifted_rows(xf_ref, lead + j + c0)
                part = jnp.sum(prod.reshape(CONV_ROWS // 8, 8, CONV_CH), axis=0)
                part_ref[j] = part if c0 == 0 else part + part_ref[j]
        for j in range(CONV_WIDTH):
            dw_ref[j:j + 1, :] += jnp.sum(part_ref[j], axis=0, keepdims=True)

    return pl.pallas_call(
        body, name=name, grid=(nsteps,),
        in_specs=[pl.BlockSpec((tm, EVEN_IN), lambda i: (i, 0)),
                  pl.BlockSpec((CONV_HALO, EVEN_IN), lambda i: (jnp.maximum(i * nh - 1, 0), 0)),
                  pl.BlockSpec((tm, CONV_CH), lambda i: (i, 0)),
                  pl.BlockSpec((CONV_HALO, CONV_CH), lambda i: (jnp.minimum((i + 1) * nh, s // CONV_HALO - 1), 0)),
                  pl.BlockSpec(w.shape, lambda i: (0, 0))],
        out_specs=[pl.BlockSpec((tm, 2 * CONV_CH), lambda i: (i, 0)), pl.BlockSpec(w.shape, lambda i: (0, 0))],
        out_shape=[jax.ShapeDtypeStruct((s, 2 * CONV_CH), BF16), jax.ShapeDtypeStruct(w.shape, F32)],
        scratch_shapes=[pltpu.VMEM((8, tm + CONV_HALO, CONV_CH), F32), pltpu.VMEM((8, tm + CONV_HALO, CONV_CH), F32),
                        pltpu.VMEM((CONV_WIDTH, 8, CONV_CH), F32)],
        compiler_params=_params("arbitrary"),
    )(proj, proj, dy, dy, w)


GATE_Z = slice(1536, 2560)
D_CH = 512
GELU_C = math.sqrt(2.0 / math.pi)
GELU_K = 0.044715


def _gelu_parts(z):
    t = jnp.tanh(GELU_C * (z + GELU_K * z * z * z))
    return 0.5 * z * (1.0 + t), t


def _lane_group(rows):
    return lax.broadcasted_iota(jnp.int32, (rows, D_CH), 1) // HEAD_DIM


def _tril_mask():
    return lax.broadcasted_iota(jnp.int32, (BLOCK, BLOCK), 0) >= lax.broadcasted_iota(jnp.int32, (BLOCK, BLOCK), 1)


def _layer_norm_parts(x):
    mu = jnp.mean(x, axis=-1, keepdims=True)
    xc = x - mu
    rstd = lax.rsqrt(jnp.mean(xc * xc, axis=-1, keepdims=True) + LN_EPS)
    return xc * rstd, rstd


def _gate_fwd(proj, ln_g, ln_b, w_sp, sb_t, name):
    tm = 512

    def body(p_ref, g_ref, b_ref, w_ref, sb_ref, mixed_ref, out_ref):
        zz, _ = _gelu_parts(p_ref[:, GATE_Z].astype(F32))
        u = zz[:, :D_CH]
        xh, _ = _layer_norm_parts(zz[:, D_CH:])
        gn = (xh * g_ref[...] + b_ref[...]).astype(BF16)
        grp = _lane_group(BLOCK)
        tri = _tril_mask()
        ws = [jnp.where(tri, w_ref[gi], 0.0).astype(BF16) for gi in range(N_GROUPS)]
        bias = jnp.zeros((BLOCK, D_CH), F32)
        for gi in range(N_GROUPS):
            bias = jnp.where(grp == gi, sb_ref[:, gi:gi + 1], bias)
        for ch in range(tm // BLOCK):
            rows = slice(ch * BLOCK, (ch + 1) * BLOCK)
            gc = gn[rows, :]
            mixed = bias
            for gi in range(N_GROUPS):
                r = jnp.dot(ws[gi], gc, preferred_element_type=F32)
                mixed = jnp.where(grp == gi, r + bias, mixed)
            mixed_ref[rows, :] = mixed
            out_ref[rows, :] = (u[rows, :] * mixed).astype(BF16)

    return _rows(body, name, tm, [proj], [ln_g, ln_b, w_sp, sb_t], [(D_CH, F32), (D_CH, BF16)])


def _gate_bwd(dmix, proj, mixed, ln_g, ln_b, w_sp, name):
    tm = 512

    def body(d_ref, p_ref, m_ref, g_ref, b_ref, w_ref, dz_ref, dg_ref, db_ref, dw_ref, dsb_ref, dgn_ref):
        @pl.when(_first_step())
        def _():
            dg_ref[...] = jnp.zeros_like(dg_ref)
            db_ref[...] = jnp.zeros_like(db_ref)
            dw_ref[...] = jnp.zeros_like(dw_ref)
            dsb_ref[...] = jnp.zeros_like(dsb_ref)

        z = p_ref[:, GATE_Z].astype(F32)
        zz, t = _gelu_parts(z)
        u = zz[:, :D_CH]
        xh, rstd = _layer_norm_parts(zz[:, D_CH:])
        g = g_ref[...]
        gn = (xh * g + b_ref[...]).astype(BF16)
        dd = d_ref[:, D_CH:]
        du = dd * m_ref[...]
        dm = dd * u
        grp = _lane_group(BLOCK)
        tri = _tril_mask()
        ws = [jnp.where(tri, w_ref[gi], 0.0).astype(BF16) for gi in range(N_GROUPS)]
        gsel = (lax.broadcasted_iota(jnp.int32, (N_GROUPS, D_CH), 1) // HEAD_DIM
                == lax.broadcasted_iota(jnp.int32, (N_GROUPS, D_CH), 0)).astype(F32)
        for ch in range(tm // BLOCK):
            rows = slice(ch * BLOCK, (ch + 1) * BLOCK)
            dmc = dm[rows, :]
            dmb = dmc.astype(BF16)
            gc = gn[rows, :]
            dgn = jnp.zeros((BLOCK, D_CH), F32)
            for gi in range(N_GROUPS):
                r = lax.dot_general(ws[gi], dmb, TN, preferred_element_type=F32)
                dgn = jnp.where(grp == gi, r, dgn)
                dmg = jnp.where(grp == gi, dmb, jnp.zeros_like(dmb))
                dwg = lax.dot_general(dmg, gc, NT, preferred_element_type=F32)
                dw_ref[gi] += jnp.where(tri, dwg, 0.0)
            dsb_ref[...] += lax.dot_general(gsel, dmc, NT, preferred_element_type=F32, precision=lax.Precision.HIGHEST)
            dgn_ref[rows, :] = dgn
        dgn = dgn_ref[...]
        db_ref[...] += jnp.sum(dgn, axis=0, keepdims=True)
        dg_ref[...] += jnp.sum(dgn * xh, axis=0, keepdims=True)
        dxh = dgn * g
        dgp = rstd * (dxh - jnp.mean(dxh, axis=-1, keepdims=True) - xh * jnp.mean(dxh * xh, axis=-1, keepdims=True))
        dgelu = 0.5 * (1.0 + t) + 0.5 * z * (1.0 - t * t) * GELU_C * (1.0 + 3.0 * GELU_K * z * z)
        dz_ref[:, 0:D_CH] = (du * dgelu[:, :D_CH]).astype(BF16)
        dz_ref[:, D_CH:] = (dgp * dgelu[:, D_CH:]).astype(BF16)

    s = proj.shape[0]
    tiled = [dmix, proj, mixed]
    consts = [ln_g, ln_b, w_sp]
    in_specs = [pl.BlockSpec((tm, a.shape[1]), lambda i: (i, 0)) for a in tiled]
    in_specs += [pl.BlockSpec(a.shape, lambda i, nd=a.ndim: (0,) * nd) for a in consts]
    vec = (1, D_CH)
    acc_shapes = [vec, vec, w_sp.shape, (N_GROUPS, BLOCK)]
    return pl.pallas_call(
        body, name=name, grid=(s // tm,), in_specs=in_specs,
        out_specs=[pl.BlockSpec((tm, 2 * D_CH), lambda i: (i, 0))]
        + [pl.BlockSpec(sh, lambda i, nd=len(sh): (0,) * nd) for sh in acc_shapes],
        out_shape=[jax.ShapeDtypeStruct((s, 2 * D_CH), BF16)] + [jax.ShapeDtypeStruct(sh, F32) for sh in acc_shapes],
        scratch_shapes=[pltpu.VMEM((tm, D_CH), F32)],
        compiler_params=_params("arbitrary"),
    )(*tiled, *consts)


def _adam_update(w, g, m, v):
    nm = ADAM_B1 * m + (1.0 - ADAM_B1) * g
    nv = ADAM_B2 * v + (1.0 - ADAM_B2) * (g * g)
    m_hat = nm / (1.0 - ADAM_B1 ** ADAM_STEP)
    v_hat = nv / (1.0 - ADAM_B2 ** ADAM_STEP)
    return -ADAM_LR * (m_hat / (jnp.sqrt(v_hat) + ADAM_EPS) + ADAM_WD * w), nm, nv


def _adamw(w, g, m, v, name):
    rows, cols = w.shape
    tm = _tile(rows, 512, 8) if rows % 8 == 0 else rows

    def body(w_ref, g_ref, m_ref, v_ref, d_ref, nm_ref, nv_ref):
        d_ref[...], nm_ref[...], nv_ref[...] = _adam_update(w_ref[...], g_ref[...], m_ref[...], v_ref[...])

    return _rows(body, name, tm, [w, g, m, v], [], [(cols, F32)] * 3)


def _ordered_sum(parts, name):
    n, rows, cols = parts.shape
    tm = _tile(rows, 512, 16 if parts.dtype == BF16 else 8)

    def body(p_ref, o_ref):
        acc = p_ref[0].astype(F32)
        for k in range(1, n):
            acc = acc + p_ref[k].astype(F32)
        o_ref[...] = acc

    return pl.pallas_call(body, name=name, grid=(rows // tm,),
                          in_specs=[pl.BlockSpec((n, tm, cols), lambda i: (0, i, 0))],
                          out_specs=pl.BlockSpec((tm, cols), lambda i: (i, 0)),
                          out_shape=jax.ShapeDtypeStruct((rows, cols), F32), compiler_params=_params("parallel"))(parts)


ANY = pl.BlockSpec(memory_space=pl.ANY)


def _position():
    x, y, c = lax.axis_index("x"), lax.axis_index("y"), lax.axis_index("c")
    other_chips = [(1 - x, y), (x, 1 - y), (1 - x, 1 - y)]
    return x, y, c, other_chips


def _remote(src, dst, send_sem, recv_sem, to):
    return pltpu.make_async_remote_copy(src_ref=src, dst_ref=dst, send_sem=send_sem, recv_sem=recv_sem,
                                        device_id=to, device_id_type=MESH)


STAGE_ROWS = 736


def _staged_copies(copies, buf, in_sems, out_sems):
    n = len(copies)

    def into(u):
        src = copies[u][0]
        return pltpu.make_async_copy(src, buf.at[u % 2, pl.ds(0, src.shape[0]), :], in_sems.at[u % 2])

    def out_of(u):
        dst = copies[u][1]
        return pltpu.make_async_copy(buf.at[u % 2, pl.ds(0, dst.shape[0]), :], dst, out_sems.at[u % 2])

    into(0).start()
    for u in range(n):
        into(u).wait()
        out_of(u).start()
        if u + 1 < n:
            if u >= 1:
                out_of(u - 1).wait()
            into(u + 1).start()
    if n >= 2:
        out_of(n - 2).wait()
    out_of(n - 1).wait()


def _stage_scratch(dtype, cols):
    return [pltpu.VMEM((2, STAGE_ROWS, cols), dtype), pltpu.SemaphoreType.DMA((2,)), pltpu.SemaphoreType.DMA((2,))]


def _row_chunks(rows):
    return [(r, min(STAGE_ROWS, rows - r)) for r in range(0, rows, STAGE_ROWS)]


def _gather_chips(shard, name):
    rows, cols = shard.shape
    half = rows // 2

    def body(in_ref, out_ref, send_sems, recv_sems, buf, in_sems, out_sems):
        x, y, c, chips = _position()
        me = 2 * x + y
        sibling = (x, y, 1 - c)

        def slab(chip, h):
            return out_ref.at[chip, pl.ds(h * half, half), :]

        first = [_remote(in_ref.at[pl.ds(c * half, half), :], slab(me, c), send_sems.at[j], recv_sems.at[j], (cx, cy, c))
                 for j, (cx, cy) in enumerate(chips)]
        for cp in first:
            cp.start()
        _staged_copies([(in_ref.at[pl.ds(r, n), :], out_ref.at[me, pl.ds(r, n), :]) for r, n in _row_chunks(rows)],
                       buf, in_sems, out_sems)
        passed = []
        for j, (cx, cy) in enumerate(chips):
            got = slab(2 * cx + cy, c)
            _remote(got, got, send_sems.at[j], recv_sems.at[j], sibling).wait_recv()
            cp = _remote(got, got, send_sems.at[3 + j], recv_sems.at[3 + j], sibling)
            cp.start()
            passed.append(cp)
        for j, (cx, cy) in enumerate(chips):
            got = slab(2 * cx + cy, 1 - c)
            _remote(got, got, send_sems.at[3 + j], recv_sems.at[3 + j], sibling).wait_recv()
        for cp in first + passed:
            cp.wait_send()

    return pl.pallas_call(
        body, name=name, in_specs=[ANY], out_specs=ANY,
        out_shape=jax.ShapeDtypeStruct((N_CHIPS, rows, cols), shard.dtype),
        scratch_shapes=[pltpu.SemaphoreType.DMA((6,)), pltpu.SemaphoreType.DMA((6,))] + _stage_scratch(shard.dtype, cols),
        compiler_params=pltpu.CompilerParams(vmem_limit_bytes=VMEM_LIMIT),
    )(shard)


HBM = pl.BlockSpec(memory_space=pltpu.HBM)
SEM = pl.BlockSpec(memory_space=pltpu.SEMAPHORE)
SIDE_EFFECT = pltpu.SideEffectType.DATAFLOW_SIDE_EFFECTING


def _ici_copies(in_ref, land_ref, send_sems, recv_sems, half):
    x, y, c, chips = _position()
    mine = pl.ds(c * half, half)
    sends = [_remote(in_ref.at[mine, :], land_ref.at[2 * x + y, mine, :], send_sems.at[j], recv_sems.at[j], (cx, cy, c))
             for j, (cx, cy) in enumerate(chips)]
    arrivals = [_remote(in_ref.at[mine, :], land_ref.at[2 * cx + cy, mine, :], send_sems.at[j], recv_sems.at[j], (cx, cy, c))
                for j, (cx, cy) in enumerate(chips)]
    return sends, arrivals


def _gather_start(shard, after, name):
    rows, cols = shard.shape

    def body(in_ref, land_ref, after_ref, send_sems, recv_sems, in_thru, land_thru, token):
        sends, _ = _ici_copies(in_ref, land_ref, send_sems, recv_sems, rows // 2)
        for cp in sends:
            cp.start()
        token[...] = jnp.zeros_like(token)

    land = lax.empty((N_CHIPS, rows, cols), shard.dtype)
    return pl.pallas_call(
        body, name=name,
        out_shape=(pltpu.SemaphoreType.DMA((3,)), pltpu.SemaphoreType.DMA((3,)), pltpu.HBM(shard.shape, shard.dtype),
                   pltpu.HBM(land.shape, land.dtype), jax.ShapeDtypeStruct((8, LANES), F32)),
        in_specs=(HBM, HBM, ANY), out_specs=(SEM, SEM, HBM, HBM, pl.BlockSpec(memory_space=pltpu.VMEM)),
        input_output_aliases={0: 2, 1: 3},
        compiler_params=pltpu.CompilerParams(has_side_effects=SIDE_EFFECT),
    )(pltpu.with_memory_space_constraint(shard, pltpu.HBM), pltpu.with_memory_space_constraint(land, pltpu.HBM), after)


def _gather_wait(send_sems, recv_sems, shard, land, after, name):
    rows = shard.shape[0]

    def body(in_ref, land_ref, send_sems, recv_sems, after_ref, in_out, land_out):
        sends, arrivals = _ici_copies(in_ref, land_ref, send_sems, recv_sems, rows // 2)
        for cp in sends:
            cp.wait_send()
        for cp in arrivals:
            cp.wait_recv()

    return pl.pallas_call(
        body, name=name, out_shape=(pltpu.HBM(shard.shape, shard.dtype), pltpu.HBM(land.shape, land.dtype)),
        in_specs=(HBM, HBM, SEM, SEM, ANY), out_specs=(HBM, HBM), input_output_aliases={0: 0, 1: 1},
        compiler_params=pltpu.CompilerParams(has_side_effects=SIDE_EFFECT),
    )(shard, land, send_sems, recv_sems, after)


def _gather_finish(shard, land, name):
    rows, cols = shard.shape
    half = rows // 2

    def body(in_ref, land_ref, out_ref, send_sems, recv_sems, buf, in_sems, out_sems):
        x, y, c, chips = _position()
        me = 2 * x + y
        sibling = (x, y, 1 - c)

        def slab(chip, h):
            return out_ref.at[chip, pl.ds(h * half, half), :]

        passed = [_remote(slab(2 * cx + cy, c), slab(2 * cx + cy, c), send_sems.at[j], recv_sems.at[j], sibling)
                  for j, (cx, cy) in enumerate(chips)]
        for cp in passed:
            cp.start()
        _staged_copies([(in_ref.at[pl.ds(r, n), :], out_ref.at[me, pl.ds(r, n), :]) for r, n in _row_chunks(rows)],
                       buf, in_sems, out_sems)
        for j, (cx, cy) in enumerate(chips):
            got = slab(2 * cx + cy, 1 - c)
            _remote(got, got, send_sems.at[j], recv_sems.at[j], sibling).wait_recv()
        for cp in passed:
            cp.wait_send()

    return pl.pallas_call(
        body, name=name, in_specs=[ANY, ANY], out_specs=ANY, out_shape=jax.ShapeDtypeStruct(land.shape, land.dtype),
        input_output_aliases={1: 0},
        scratch_shapes=[pltpu.SemaphoreType.DMA((3,)), pltpu.SemaphoreType.DMA((3,))] + _stage_scratch(shard.dtype, cols),
        compiler_params=pltpu.CompilerParams(vmem_limit_bytes=VMEM_LIMIT),
    )(shard, land)


def _pair_send(grads, name):
    n = len(grads)
    hs = [g.shape[2] for g in grads]
    offs = [sum(hs[:i]) for i in range(n)]
    cols = grads[0].shape[3]

    def body(*refs):
        g_refs = refs[:n]
        got_ref, send_sems, recv_sems = refs[n:]
        x, y, c, _ = _position()
        copies = [_remote(g_ref.at[:, 1 - c], got_ref.at[:, pl.ds(offs[i], hs[i]), :], send_sems.at[i], recv_sems.at[i],
                          (x, y, 1 - c)) for i, g_ref in enumerate(g_refs)]
        for cp in copies:
            cp.start()
        for cp in copies:
            cp.wait()

    return pl.pallas_call(
        body, name=name, in_specs=[ANY] * n, out_specs=ANY, out_shape=jax.ShapeDtypeStruct((N_CHIPS, sum(hs), cols), F32),
        scratch_shapes=[pltpu.SemaphoreType.DMA((n,)), pltpu.SemaphoreType.DMA((n,))],
    )(*grads)


def _device_copies(in_ref, land_ref, send_sems, recv_sems):
    x, y, c, _ = _position()
    flips = [(fx, fy, fc) for fx in range(2) for fy in range(2) for fc in range(2) if fx + fy + fc]
    copies = []
    for k, (fx, fy, fc) in enumerate(flips):
        px, py, pc = (1 - x if fx else x), (1 - y if fy else y), (1 - c if fc else c)
        copies.append((_remote(in_ref, land_ref.at[4 * x + 2 * y + c], send_sems.at[k], recv_sems.at[k], (px, py, pc)),
                       _remote(in_ref, land_ref.at[4 * px + 2 * py + pc], send_sems.at[k], recv_sems.at[k], (px, py, pc))))
    return copies


def _devices_start(block, after, name):
    land = lax.empty((N_DEV,) + block.shape, block.dtype)

    def body(in_ref, land_ref, after_ref, send_sems, recv_sems, in_thru, land_thru):
        for send, _ in _device_copies(in_ref, land_ref, send_sems, recv_sems):
            send.start()

    return pl.pallas_call(
        body, name=name,
        out_shape=(pltpu.SemaphoreType.DMA((N_DEV - 1,)), pltpu.SemaphoreType.DMA((N_DEV - 1,)),
                   pltpu.HBM(block.shape, block.dtype), pltpu.HBM(land.shape, land.dtype)),
        in_specs=(HBM, HBM, ANY), out_specs=(SEM, SEM, HBM, HBM), input_output_aliases={0: 2, 1: 3},
        compiler_params=pltpu.CompilerParams(has_side_effects=SIDE_EFFECT),
    )(pltpu.with_memory_space_constraint(block, pltpu.HBM), pltpu.with_memory_space_constraint(land, pltpu.HBM), after)


def _devices_wait(send_sems, recv_sems, block, land, after, name):
    def body(in_ref, land_ref, send_sems, recv_sems, after_ref, in_out, land_out):
        for send, arrival in _device_copies(in_ref, land_ref, send_sems, recv_sems):
            send.wait_send()
            arrival.wait_recv()

    return pl.pallas_call(
        body, name=name, out_shape=(pltpu.HBM(block.shape, block.dtype), pltpu.HBM(land.shape, land.dtype)),
        in_specs=(HBM, HBM, SEM, SEM, ANY), out_specs=(HBM, HBM), input_output_aliases={0: 0, 1: 1},
        compiler_params=pltpu.CompilerParams(has_side_effects=SIDE_EFFECT),
    )(block, land, send_sems, recv_sems, after)


def _pair_copies(g_refs, land_ref, send_sems, recv_sems):
    x, y, c, _ = _position()
    hs = [g.shape[2] for g in g_refs]
    offs = [sum(hs[:i]) for i in range(len(hs))]
    return [_remote(g_ref.at[:, 1 - c], land_ref.at[:, pl.ds(offs[i], hs[i]), :], send_sems.at[i], recv_sems.at[i],
                    (x, y, 1 - c)) for i, g_ref in enumerate(g_refs)]


def _pair_send_start(grads, name):
    n = len(grads)
    land = lax.empty((N_CHIPS, sum(g.shape[2] for g in grads), grads[0].shape[3]), F32)

    def body(*refs):
        for cp in _pair_copies(refs[:n], refs[n], refs[n + 1], refs[n + 2]):
            cp.start()
        refs[-1][...] = jnp.zeros_like(refs[-1])

    buffers = [*grads, land]
    return pl.pallas_call(
        body, name=name,
        out_shape=(pltpu.SemaphoreType.DMA((n,)), pltpu.SemaphoreType.DMA((n,)),
                   *[pltpu.HBM(b.shape, b.dtype) for b in buffers], jax.ShapeDtypeStruct((8, LANES), F32)),
        in_specs=(HBM,) * (n + 1), out_specs=(SEM, SEM, *(HBM,) * (n + 1), pl.BlockSpec(memory_space=pltpu.VMEM)),
        input_output_aliases={i: 2 + i for i in range(n + 1)},
        compiler_params=pltpu.CompilerParams(has_side_effects=SIDE_EFFECT),
    )(*[pltpu.with_memory_space_constraint(b, pltpu.HBM) for b in buffers])


def _pair_send_wait(send_sems, recv_sems, buffers, after, name):
    n = len(buffers) - 1

    def body(*refs):
        for cp in _pair_copies(refs[:n], refs[n], refs[n + 1], refs[n + 2]):
            cp.wait_send()
            cp.wait_recv()

    return pl.pallas_call(
        body, name=name, out_shape=tuple(pltpu.HBM(b.shape, b.dtype) for b in buffers),
        in_specs=(*(HBM,) * (n + 1), SEM, SEM, ANY), out_specs=(HBM,) * (n + 1),
        input_output_aliases={i: i for i in range(n + 1)},
        compiler_params=pltpu.CompilerParams(has_side_effects=SIDE_EFFECT),
    )(*buffers, send_sems, recv_sems, after)


def _pair_add(grads, got, name):
    n = len(grads)
    hs = [g.shape[2] for g in grads]
    offs = [sum(hs[:i]) for i in range(n)]
    cols = grads[0].shape[3]
    hmax = max(hs)
    units = [(i, k) for k in range(N_CHIPS) for i in range(n)]

    def body(*refs):
        g_refs = refs[:n]
        got_ref, out_ref, a_buf, b_buf, o_buf, a_sems, b_sems, o_sems = refs[n:]
        c = lax.axis_index("c")

        def loads(u):
            i, k = units[u]
            slot, rows = u % 2, pl.ds(0, hs[i])
            return (pltpu.make_async_copy(g_refs[i].at[k, c], a_buf.at[slot, rows, :], a_sems.at[slot]),
                    pltpu.make_async_copy(got_ref.at[k, pl.ds(offs[i], hs[i]), :], b_buf.at[slot, rows, :], b_sems.at[slot]))

        def store(u):
            i, k = units[u]
            return pltpu.make_async_copy(o_buf.at[u % 2, pl.ds(0, hs[i]), :], out_ref.at[k, pl.ds(offs[i], hs[i]), :],
                                         o_sems.at[u % 2])

        for cp in loads(0):
            cp.start()
        for u, (i, k) in enumerate(units):
            if u + 1 < len(units):
                for cp in loads(u + 1):
                    cp.start()
            for cp in loads(u):
                cp.wait()
            if u >= 2:
                store(u - 2).wait()
            rows = pl.ds(0, hs[i])
            o_buf[u % 2, rows, :] = (a_buf[u % 2, rows, :] + b_buf[u % 2, rows, :]).astype(BF16)
            store(u).start()
        store(len(units) - 2).wait()
        store(len(units) - 1).wait()

    return pl.pallas_call(
        body, name=name, in_specs=[ANY] * (n + 1), out_specs=ANY,
        out_shape=jax.ShapeDtypeStruct((N_CHIPS, sum(hs), cols), BF16),
        scratch_shapes=[pltpu.VMEM((2, hmax, cols), F32), pltpu.VMEM((2, hmax, cols), F32), pltpu.VMEM((2, hmax, cols), BF16),
                        pltpu.SemaphoreType.DMA((2,)), pltpu.SemaphoreType.DMA((2,)), pltpu.SemaphoreType.DMA((2,))],
        compiler_params=pltpu.CompilerParams(vmem_limit_bytes=VMEM_LIMIT),
    )(*grads, got)


def _exchange_copies(in_ref, land_ref, send_sems, recv_sems):
    x, y, c, chips = _position()
    return [_remote(in_ref.at[2 * cx + cy], land_ref.at[j], send_sems.at[j], recv_sems.at[j], (cx, cy, c))
            for j, (cx, cy) in enumerate(chips)]


def _exchange_start(parts, name):
    _, rows, cols = parts.shape

    def body(in_ref, land_ref, send_sems, recv_sems, in_thru, land_thru, token):
        for cp in _exchange_copies(in_ref, land_ref, send_sems, recv_sems):
            cp.start()
        token[...] = jnp.zeros_like(token)

    land = lax.empty((3, rows, cols), parts.dtype)
    return pl.pallas_call(
        body, name=name,
        out_shape=(pltpu.SemaphoreType.DMA((3,)), pltpu.SemaphoreType.DMA((3,)), pltpu.HBM(parts.shape, parts.dtype),
                   pltpu.HBM(land.shape, land.dtype), jax.ShapeDtypeStruct((8, LANES), F32)),
        in_specs=(HBM, HBM), out_specs=(SEM, SEM, HBM, HBM, pl.BlockSpec(memory_space=pltpu.VMEM)),
        input_output_aliases={0: 2, 1: 3},
        compiler_params=pltpu.CompilerParams(has_side_effects=SIDE_EFFECT),
    )(pltpu.with_memory_space_constraint(parts, pltpu.HBM), pltpu.with_memory_space_constraint(land, pltpu.HBM))


def _exchange_wait(send_sems, recv_sems, parts, land, after, name):
    def body(in_ref, land_ref, send_sems, recv_sems, after_ref, in_out, land_out):
        for cp in _exchange_copies(in_ref, land_ref, send_sems, recv_sems):
            cp.wait_send()
            cp.wait_recv()

    return pl.pallas_call(
        body, name=name, out_shape=(pltpu.HBM(parts.shape, parts.dtype), pltpu.HBM(land.shape, land.dtype)),
        in_specs=(HBM, HBM, SEM, SEM, ANY), out_specs=(HBM, HBM), input_output_aliases={0: 0, 1: 1},
        compiler_params=pltpu.CompilerParams(has_side_effects=SIDE_EFFECT),
    )(parts, land, send_sems, recv_sems, after)


def _chip_sum(parts, recv, chip, name):
    _, rows, cols = parts.shape
    tm = _tile(rows, 512, 16)

    def body(chip_ref, own_ref, recv_ref, o_ref):
        acc = own_ref[0].astype(F32)
        for j in range(3):
            acc = acc + recv_ref[j].astype(F32)
        o_ref[...] = acc

    return pl.pallas_call(
        body, name=name,
        grid_spec=pltpu.PrefetchScalarGridSpec(
            num_scalar_prefetch=1, grid=(rows // tm,),
            in_specs=[pl.BlockSpec((1, tm, cols), lambda i, chip_ref: (chip_ref[0], i, 0)),
                      pl.BlockSpec((3, tm, cols), lambda i, chip_ref: (0, i, 0))],
            out_specs=pl.BlockSpec((tm, cols), lambda i, chip_ref: (i, 0))),
        out_shape=jax.ShapeDtypeStruct((rows, cols), F32), compiler_params=_params("parallel"),
    )(chip, parts, recv)


def _join_unpack(mine, hs, groups, name):
    n = len(hs)
    offs = [sum(hs[:i]) for i in range(n)]
    cols = mine.shape[1]
    n_out = max(groups) + 1
    base = [2 * sum(h for h, g in zip(hs[:i], groups[:i]) if g == groups[i]) for i in range(n)]
    out_rows = [2 * sum(h for h, g in zip(hs, groups) if g == k) for k in range(n_out)]

    def body(in_ref, *refs):
        outs = refs[:n_out]
        send_sems, recv_sems, buf, in_sems, out_sems = refs[n_out:]
        x, y, c, _ = _position()
        sibling = (x, y, 1 - c)
        sent, local = [], []
        for i in range(n):
            src = in_ref.at[pl.ds(offs[i], hs[i]), :]
            here = outs[groups[i]].at[pl.ds(base[i] + c * hs[i], hs[i]), :]
            cp = _remote(src, here, send_sems.at[i], recv_sems.at[i], sibling)
            cp.start()
            sent.append(cp)
            local.append((src, here))
        _staged_copies(local, buf, in_sems, out_sems)
        for i, cp in enumerate(sent):
            there = outs[groups[i]].at[pl.ds(base[i] + (1 - c) * hs[i], hs[i]), :]
            _remote(there, there, send_sems.at[i], recv_sems.at[i], sibling).wait_recv()
            cp.wait_send()

    assert max(hs) <= STAGE_ROWS
    return pl.pallas_call(
        body, name=name, in_specs=[ANY], out_specs=[ANY] * n_out,
        out_shape=[jax.ShapeDtypeStruct((r, cols), F32) for r in out_rows],
        scratch_shapes=[pltpu.SemaphoreType.DMA((n,)), pltpu.SemaphoreType.DMA((n,))] + _stage_scratch(F32, cols),
        compiler_params=pltpu.CompilerParams(vmem_limit_bytes=VMEM_LIMIT),
    )(mine)


SMALL_ROWS = 16
SMALL_PACK_ROWS = 256


def _small_rows(n):
    return -(-n // (SMALL_ROWS * LANES)) * SMALL_ROWS


def _pack_small(arrs):
    parts = []
    for a in arrs:
        flat = a.reshape(-1)
        rows = _small_rows(flat.shape[0])
        flat = jnp.pad(flat, (0, rows * LANES - flat.shape[0]))
        parts.append(flat.reshape(rows, LANES))
    total = sum(p.shape[0] for p in parts)
    parts.append(jnp.zeros((-total % SMALL_PACK_ROWS, LANES), F32))
    return jnp.concatenate(parts, axis=0)


def _unpack_small(packed, shapes):
    out, r = [], 0
    for sh in shapes:
        n = math.prod(sh)
        cnt = _small_rows(n)
        out.append(packed[r:r + cnt].reshape(-1)[:n].reshape(sh))
        r += cnt
    return out


def _ffn_bwd(dh, dhb, h_in, saved, g_norm, w_gate_t, w_up_t, w_down, tag, after=None):
    n, gate, up, act = saved
    dgate, dup = _ffn_dact(dhb, w_down, gate, up, f"{tag}_dact", after)
    dw_down = _matmul(act, dhb, trans_a=True, name=f"{tag}_dwdown")
    dw_gate_t = _matmul(dgate, n, trans_a=True, name=f"{tag}_dwgate")
    dw_up_t = _matmul(dup, n, trans_a=True, name=f"{tag}_dwup")
    dh_in, dh_inb, dg = _dn_norm([(dgate, w_gate_t), (dup, w_up_t)], h_in, g_norm, dh, f"{tag}_dnorm")
    return dh_in, dh_inb, dg, dw_gate_t, dw_up_t, dw_down


def _local_step(x, tgt, w, big, late_weights, reduce_send, reduce_exchange):
    s = x.shape[0]
    tabs = _rope_tables(s)
    grads, gbig = {}, {}

    g_ev = w['ev_norm_g']
    n1 = _rms_fwd(x, g_ev, "ev_norm")
    proj0 = _matmul(n1, big['ev_w_in', 0], trans_b=True, name="ev_in", out_dtype=BF16, rows_inner=True)
    q0, k0, v0 = _qkv_prep_even(proj0, tabs, "ev_qkv")
    sinks = w['ev_sinks'].reshape(-1)
    o0, lse0, o0b = _attn_fwd(q0, k0, v0, sinks, max_dist=BLOCK - 1, name="ev_attn", emit_bf16=True)
    yconv, cout = _conv_fwd(proj0, w['ev_conv_w'][0], w['ev_conv_b'], w['ev_conv_ln_g'], w['ev_conv_ln_b'], "ev_conv")
    mix0 = (o0b[0], cout)
    g_f0 = w['ffn_norm_g'][0:1]
    h1, n2 = _matmul_norm(mix0, big['ev_w_out', 0], x, g_f0, "ev_out")
    big = {**big, **late_weights(h1)}

    g_od = w['od_norm_g']
    act0, gate0, up0 = _ffn_gate_up(n2, big['ffn_w_gate', 0], big['ffn_w_up', 0], "ffn0_gate_up")
    h2, n3 = _matmul_norm(act0, big['ffn_w_down', 0], h1, g_od, "ffn0_down")
    ffn0 = (n2, gate0, up0, act0)

    proj1 = _matmul(n3, big['od_w_in', 0], trans_b=True, name="od_in", out_dtype=BF16, rows_inner=True)
    qkv = _qkv_prep_odd(proj1, tabs, "od_qkv")
    nb = len(DILATIONS)
    outs, lses = [], []
    for i, d in enumerate(DILATIONS):
        o_r, lse_r = _attn_fwd(qkv[i], qkv[nb + i], qkv[2 * nb + i], None, max_dist=BLOCK, name=f"od_attn{d}", o_dtype=BF16)
        outs.append(o_r)
        lses.append(lse_r)
    comb = _combine(outs, lses, "od_combine")
    c_bf16 = comb[0]
    c_fold = {1: comb[1]}
    lse_fold = {1: comb[2]}
    for i, d in enumerate(DILATIONS[1:]):
        c_fold[d], lse_fold[d] = comb[3 + 2 * i], comb[4 + 2 * i]
    w_sp = w['od_spatial_w'][0]
    sb_t = w['od_spatial_b'][0].T
    mixed, dout = _gate_fwd(proj1, w['od_sgu_ln_g'], w['od_sgu_ln_b'], w_sp, sb_t, "od_gate")
    mix1 = (c_bf16, dout)
    g_f1 = w['ffn_norm_g'][1:2]
    h3, n4 = _matmul_norm(mix1, big['od_w_out', 0], h2, g_f1, "od_out")
    act1, gate1, up1 = _ffn_gate_up(n4, big['ffn_w_gate', 1], big['ffn_w_up', 1], "ffn1_gate_up")
    ffn1 = (n4, gate1, up1, act1)

    dh4, dh4b, dg_final, loss_tile = _matmul_final(act1, big['ffn_w_down', 1], h3, w['final_norm_g'].reshape(1, D_MODEL),
                                                   tgt, "ffn1_down_loss")
    grads['final_norm_g'] = dg_final.reshape(D_MODEL)

    dh3, dh3b, dg_f1, gbig['ffn_w_gate', 1], gbig['ffn_w_up', 1], gbig['ffn_w_down', 1] = _ffn_bwd(
        dh4, dh4b, h3, ffn1, g_f1, big['ffn_w_gate', 1], big['ffn_w_up', 1], big['ffn_w_down', 1], "ffn1")

    dmix1 = _matmul(dh3b, big['od_w_out', 0], trans_b=True, name="od_dmix")
    gbig['od_w_out', 0] = _matmul_tn_pair(mix1[0], mix1[1], dh3b, "od_dwout")
    do_fold = dict(zip(DILATIONS[1:], _fold_dout(dmix1, "od_fold_dout")))
    do_fold[1] = dmix1[None]
    dqs, dks, dvs = [], [], []
    for i, d in enumerate(DILATIONS):
        dq_r, dk_r, dv_r = _attn_bwd(qkv[i], qkv[nb + i], qkv[2 * nb + i], do_fold[d], c_fold[d], lse_fold[d], None,
                                     max_dist=BLOCK, name=f"od_dattn{d}")
        dqs.append(dq_r)
        dks.append(dk_r)
        dvs.append(dv_r)
    dz, dg_sgu, db_sgu, dw_sp, dsb = _gate_bwd(dmix1, proj1, mixed, w['od_sgu_ln_g'], w['od_sgu_ln_b'], w_sp, "od_dgate")
    grads['od_sgu_ln_g'], grads['od_sgu_ln_b'] = dg_sgu, db_sgu
    grads['od_spatial_w'], grads['od_spatial_b'] = dw_sp[None], dsb[None]
    dproj1 = _qkv_post_odd(dqs, dks, dvs, dz, tabs, "od_dproj")
    gbig['od_w_in', 0] = _matmul(dproj1, n3, trans_a=True, name="od_dwin")
    dh2, dh2b, dg_od = _dn_norm([(dproj1, big['od_w_in', 0])], h2, g_od, dh3, "od_dnorm")
    grads['od_norm_g'] = dg_od
    token = reduce_send(0, gbig)

    dh1, dh1b, dg_f0, gbig['ffn_w_gate', 0], gbig['ffn_w_up', 0], gbig['ffn_w_down', 0] = _ffn_bwd(
        dh2, dh2b, h1, ffn0, g_f0, big['ffn_w_gate', 0], big['ffn_w_up', 0], big['ffn_w_down', 0], "ffn0", token)
    grads['ffn_norm_g'] = jnp.concatenate([dg_f0, dg_f1], axis=0)
    token = reduce_exchange(0, dh1) + reduce_send(1, gbig)

    dmix0 = _matmul(dh1b, big['ev_w_out', 0], trans_b=True, name="ev_dmix", after=token)
    gbig['ev_w_out', 0] = _matmul_tn_pair(mix0[0], mix0[1], dh1b, "ev_dwout")
    dq0, dk0, dv0, dsink = _attn_bwd(q0, k0, v0, dmix0[None], o0, lse0, sinks, max_dist=BLOCK - 1, name="ev_dattn")
    grads['ev_sinks'] = dsink[:, 0, :].reshape(N_PAIRS, 2, HEAD_DIM)[:, :, 0].reshape(1, 8)
    token = reduce_exchange(1, dq0)
    dyc, dg_cln, db_cln, dcb = _conv_tail_bwd(dmix0, yconv, w['ev_conv_ln_g'] + token[0:1, 0:1], w['ev_conv_ln_b'],
                                              "ev_dconv_tail")
    grads['ev_conv_ln_g'], grads['ev_conv_ln_b'], grads['ev_conv_b'] = dg_cln, db_cln, dcb
    dglu, dconv_w = _conv_bwd(proj0, dyc, w['ev_conv_w'][0], "ev_dconv")
    grads['ev_conv_w'] = dconv_w[None]
    dproj0 = _qkv_post_even(dq0, dk0, dv0, dglu, tabs, "ev_dproj")
    gbig['ev_w_in', 0] = _matmul(dproj0, n1, trans_a=True, name="ev_dwin")
    dx, _, dg_ev = _dn_norm([(dproj0, big['ev_w_in', 0])], x, g_ev, dh1, "ev_dnorm")
    grads['ev_norm_g'] = dg_ev
    return loss_tile, dx, grads, gbig


def _shard_rows(w, layer, by_cols):
    return w[layer].T if by_cols else w[layer]


def kernel(x, ev_norm_g, ev_w_in, ev_sinks, ev_conv_w, ev_conv_b, ev_conv_ln_g, ev_conv_ln_b, ev_w_out, od_norm_g, od_w_in, od_sgu_ln_g, od_sgu_ln_b, od_spatial_w, od_spatial_b, od_w_out, ffn_norm_g, ffn_w_gate, ffn_w_up, ffn_w_down, final_norm_g, loss_target, m_ev_norm_g, m_ev_w_in, m_ev_sinks, m_ev_conv_w, m_ev_conv_b, m_ev_conv_ln_g, m_ev_conv_ln_b, m_ev_w_out, m_od_norm_g, m_od_w_in, m_od_sgu_ln_g, m_od_sgu_ln_b, m_od_spatial_w, m_od_spatial_b, m_od_w_out, m_ffn_norm_g, m_ffn_w_gate, m_ffn_w_up, m_ffn_w_down, m_final_norm_g, v_ev_norm_g, v_ev_w_in, v_ev_sinks, v_ev_conv_w, v_ev_conv_b, v_ev_conv_ln_g, v_ev_conv_ln_b, v_ev_w_out, v_od_norm_g, v_od_w_in, v_od_sgu_ln_g, v_od_sgu_ln_b, v_od_spatial_w, v_od_spatial_b, v_od_w_out, v_ffn_norm_g, v_ffn_w_gate, v_ffn_w_up, v_ffn_w_down, v_final_norm_g):
    given = dict(locals())
    wts = {n: given[n] for n in WEIGHTS}
    mom = {n: given["m_" + n] for n in WEIGHTS}
    var = {n: given["v_" + n] for n in WEIGHTS}
    chip = 2 * lax.axis_index("x") + lax.axis_index("y")

    shard_rows = [_shard_rows(wts[n], layer, by_cols).astype(BF16) for n, layer, by_cols in BIG]
    counts = [a.shape[0] for a in shard_rows]
    n_first = sum(n.startswith('ev_') for n, _, _ in BIG)

    def unpack(stacked, entries, cnts):
        out, r = {}, 0
        for (n, layer, _), cnt in zip(entries, cnts):
            out[n, layer] = stacked[:, r:r + cnt].reshape(N_CHIPS * cnt, D_MODEL)
            r += cnt
        return out

    first_w = _gather_chips(jnp.concatenate(shard_rows[:n_first], axis=0), "gather_weights_ev")
    big = unpack(first_w, BIG[:n_first], counts[:n_first])
    send_sems, recv_sems, late_shard, late_land, token = _gather_start(jnp.concatenate(shard_rows[n_first:], axis=0),
                                                                      first_w, "gather_weights_start")

    def late_weights(after):
        shard, land = _gather_wait(send_sems, recv_sems, late_shard, late_land, after, "gather_weights_wait")
        return unpack(_gather_finish(shard, land, "gather_weights_finish"), BIG[n_first:], counts[n_first:])

    full = {n: wts[n] for n in SMALL_REPL}
    full['ev_norm_g'] = full['ev_norm_g'] + token[0:1, 0:1]
    small_shards = [wts[n] for n in SMALL_SHARDED]
    small_shapes = [a.shape for a in small_shards]
    all_s = _gather_chips(_pack_small(small_shards), "gather_small_weights")
    per_chip = [_unpack_small(all_s[k], small_shapes) for k in range(N_CHIPS)]
    for i, n in enumerate(SMALL_SHARDED):
        full[n] = jnp.concatenate([per_chip[k][i] for k in range(N_CHIPS)], axis=-1)

    half_rows = {(n, layer): cnt // 2 for (n, layer, _), cnt in zip(BIG, counts)}
    in_flight = []

    sending = {}

    def halves(stage, gbig):
        return [gbig[e].reshape(N_CHIPS, 2, half_rows[e], D_MODEL) for e in GRAD_STAGES[stage]]

    def reduce_send(stage, gbig):
        send_sems, recv_sems, *buffers, token = _pair_send_start(halves(stage, gbig), f"grad_pair_start{stage}")
        sending[stage] = (send_sems, recv_sems, buffers)
        return token

    def reduce_exchange(stage, after):
        send_sems, recv_sems, buffers = sending.pop(stage)
        *split, got = _pair_send_wait(send_sems, recv_sems, buffers, after, f"grad_pair_wait{stage}")
        chip_part = _pair_add(split, got, f"grad_pair_add{stage}")
        *handles, token = _exchange_start(chip_part, f"grad_exchange_start{stage}")
        in_flight.append(handles)
        return token

    loss_tile, grad_x, grads, gbig = _local_step(x[0], loss_target[0], full, big, late_weights, reduce_send, reduce_exchange)
    loss = lax.psum(loss_tile[0, 0], ("x", "y", "c"))

    last = len(GRAD_STAGES) - 1
    split = halves(last, gbig)
    chip_part = _pair_add(split, _pair_send(split, f"grad_pair_send{last}"), f"grad_pair_add{last}")
    *handles, _ = _exchange_start(chip_part, f"grad_exchange_start{last}")
    in_flight.append(handles)

    reduced, after = {}, grad_x
    for stage, entries in enumerate(GRAD_STAGES):
        chip_part, from_chips = _exchange_wait(*in_flight[stage], after, f"grad_exchange_wait{stage}")
        my_half = _chip_sum(chip_part, from_chips, chip.reshape(1), f"grad_chip_sum{stage}")
        joined = _join_unpack(my_half, [half_rows[e] for e in entries], list(range(len(entries))), f"grad_join_halves{stage}")
        reduced.update(zip(entries, joined))
        after = joined[0]

    small_names = SMALL_REPL + SMALL_SHARDED
    small_full_shapes = [grads[n].shape for n in small_names]
    small_flight = _devices_start(_pack_small([grads[n] for n in small_names]), after, "grad_small_start")

    g_all, delta, new_m, new_v = {}, {}, {}, {}
    for n in BIG_NAMES:
        by_cols = [bc for nn, _, bc in BIG if nn == n][0]
        layers = wts[n].shape[0]

        def as_rows(a):
            return (jnp.swapaxes(a, 1, 2) if by_cols else a).reshape(-1, D_MODEL)

        def from_rows(a):
            a = a.reshape(layers, -1, D_MODEL)
            return jnp.swapaxes(a, 1, 2) if by_cols else a

        g_rows = [reduced[n, layer] for layer in range(layers)]
        g_rows = g_rows[0] if layers == 1 else jnp.concatenate(g_rows, axis=0)
        updated = _adamw(as_rows(wts[n]), g_rows, as_rows(mom[n]), as_rows(var[n]), f"adamw_{n}")
        g_all[n] = from_rows(g_rows)
        delta[n], new_m[n], new_v[n] = (from_rows(a) for a in updated)

    spack, s_land = _devices_wait(*small_flight, delta[BIG_NAMES[-1]], "grad_small_wait")
    s_all = lax.dynamic_update_slice(s_land, spack[None], (2 * chip + lax.axis_index("c"), 0, 0))
    g_all.update(zip(small_names, _unpack_small(_ordered_sum(s_all, "grad_small_sum"), small_full_shapes)))
    for n in SMALL_SHARDED:
        width = wts[n].shape[-1]
        g_all[n] = lax.dynamic_slice_in_dim(g_all[n], chip * width, width, axis=g_all[n].ndim - 1)
    for n in small_names:
        shape = wts[n].shape
        as_2d = (lambda a: a.reshape(-1, shape[-1]))
        updated = _adamw(as_2d(wts[n]), as_2d(g_all[n]), as_2d(mom[n]), as_2d(var[n]), f"adamw_{n}")
        delta[n], new_m[n], new_v[n] = (a.reshape(shape) for a in updated)

    return (loss, grad_x[None], *[g_all[n] for n in WEIGHTS], *[delta[n] for n in WEIGHTS],
            *[new_m[n] for n in WEIGHTS], *[new_v[n] for n in WEIGHTS])
```
